```python
import math
import jax, jax.numpy as jnp
from jax import lax
import numpy as np

D_MODEL = 1024
BATCH = 8
SEQ = 4096
DEPTH = 2

D_MIX = D_MODEL
ATTN_WIDTH = 3 * D_MIX // 8
LRU_WIDTH = 3 * D_MIX // 8
S5_WIDTH = D_MIX - ATTN_WIDTH - LRU_WIDTH

HEAD_DIM = 64
N_ATTN_HEADS = ATTN_WIDTH // HEAD_DIM
DILATED_PAIRS = ((128, 1), (512, 4), (2048, 16))
ATTN_BLOCK = 128
ROPE_THETA = 10000.0

LRU_HEAD = 64
N_LRU_HEADS = LRU_WIDTH // LRU_HEAD
LRU_CONV = 4
LRU_C = 8.0

S5_GROUP = 16
N_S5_GROUPS = S5_WIDTH // S5_GROUP
S5_STATE = 64

D_FF = 3 * D_MODEL
FFN_CONV = 3

DEEPNORM_ALPHA = (2 * DEPTH) ** 0.25
DEEPNORM_BETA = (8 * DEPTH) ** -0.25
LN_EPS = 1e-5
RMS_EPS = 1e-6

Q_OFF = 0
K_OFF = ATTN_WIDTH
V_OFF = 2 * ATTN_WIDTH
LRU_X_OFF = 3 * ATTN_WIDTH
LRU_G_OFF = LRU_X_OFF + LRU_WIDTH
S5_OFF = LRU_G_OFF + LRU_WIDTH
D_IN = S5_OFF + S5_WIDTH

kernel_name = 'hybrid_dilated_attn_rglru_s5_deepnorm'


def _layer_norm(x, g, b):
    xf = x.astype(jnp.float32)
    mu = jnp.mean(xf, axis=-1, keepdims=True)
    var = jnp.mean(jnp.square(xf - mu), axis=-1, keepdims=True)
    y = (xf - mu) * lax.rsqrt(var + LN_EPS) * g.astype(jnp.float32) + b.astype(jnp.float32)
    return y.astype(x.dtype)


def _rms_norm(x, g):
    xf = x.astype(jnp.float32)
    ms = jnp.mean(jnp.square(xf), axis=-1, keepdims=True)
    return xf * lax.rsqrt(ms + RMS_EPS) * g.astype(jnp.float32)


def _causal_dwconv(x, w, b):
    k = w.shape[0]
    y = lax.conv_general_dilated(
        x, w[:, None, :].astype(x.dtype), window_strides=(1,), padding=((k - 1, 0),),
        dimension_numbers=('NWC', 'WIO', 'NWC'), feature_group_count=x.shape[-1])
    return y + b.astype(x.dtype)


def _rope(x):
    s = x.shape[1]
    half = HEAD_DIM // 2
    pos = jnp.arange(s, dtype=jnp.float32)
    inv = ROPE_THETA ** (-jnp.arange(half, dtype=jnp.float32) * 2.0 / HEAD_DIM)
    ang = pos[:, None] * inv[None, :]
    cos = jnp.cos(ang)[None, :, None, :]
    sin = jnp.sin(ang)[None, :, None, :]
    xf = x.astype(jnp.float32)
    x1, x2 = xf[..., :half], xf[..., half:]
    return jnp.concatenate([x1 * cos - x2 * sin, x2 * cos + x1 * sin], axis=-1)


def _dilated_branch(q, k, v, window, dilation):
    b, s, nh, hd = q.shape
    m = s // dilation
    nback = window // dilation
    nb = -(-m // ATTN_BLOCK)
    mp = nb * ATTN_BLOCK

    def strided(t, left):
        t = t.reshape(b, m, dilation, nh, hd)
        return jnp.pad(t, ((0, 0), (left, mp - m), (0, 0), (0, 0), (0, 0)))

    def key_blocks(t):
        t = strided(t, ATTN_BLOCK).reshape(b, nb + 1, ATTN_BLOCK, dilation, nh, hd)
        return jnp.concatenate([t[:, :-1], t[:, 1:]], axis=2)

    qb = strided(q, 0).reshape(b, nb, ATTN_BLOCK, dilation, nh, hd)
    kb = key_blocks(k)
    vb = key_blocks(v)
    scores = jnp.einsum('bnqchd,bnkchd->bnchqk', qb, kb) * (hd ** -0.5)
    qi = jnp.arange(ATTN_BLOCK)[:, None]
    ki = jnp.arange(2 * ATTN_BLOCK)[None, :]
    dist = qi + ATTN_BLOCK - ki
    blk = jnp.arange(nb)[:, None, None]
    valid = (dist >= 0) & (dist <= nback) & ((blk - 1) * ATTN_BLOCK + ki >= 0)
    scores = jnp.where(valid[None, :, None, None], scores, -jnp.inf)
    lse = jax.nn.logsumexp(scores, axis=-1)
    probs = jnp.exp(scores - lse[..., None])
    out = jnp.einsum('bnchqk,bnkchd->bnqchd', probs, vb)
    out = out.reshape(b, mp, dilation, nh, hd)[:, :m].reshape(b, s, nh, hd)
    lse = jnp.transpose(lse, (0, 1, 4, 2, 3)).reshape(b, mp, dilation, nh)[:, :m].reshape(b, s, nh)
    return out, lse


def _dilated_attention(q, k, v):
    q = _rope(q)
    k = _rope(k)
    v = v.astype(jnp.float32)
    outs, lses = [], []
    for window, dilation in DILATED_PAIRS:
        o, l = _dilated_branch(q, k, v, window, dilation)
        outs.append(o)
        lses.append(l)
    wts = jax.nn.softmax(jnp.stack(lses, axis=0), axis=0)
    return jnp.sum(wts[..., None] * jnp.stack(outs, axis=0), axis=0)


def _linear_combine(e1, e2):
    a1, b1 = e1
    a2, b2 = e2
    return (a1 * a2, a2 * b1 + b2)


def _complex_combine(e1, e2):
    ar1, ai1, br1, bi1 = e1
    ar2, ai2, br2, bi2 = e2
    return (ar2 * ar1 - ai2 * ai1,
            ar2 * ai1 + ai2 * ar1,
            ar2 * br1 - ai2 * bi1 + br2,
            ar2 * bi1 + ai2 * br1 + bi2)


def _rg_lru_branch(xr, gate, conv_w, conv_b, wr, br, wi, bi, lam):
    b, s, _ = xr.shape
    f32 = jnp.float32
    xc = _causal_dwconv(xr, conv_w, conv_b).astype(f32)
    xh = xc.reshape(b, s, N_LRU_HEADS, LRU_HEAD)
    r = jax.nn.sigmoid(jnp.einsum('bshi,hij->bshj', xh, wr.astype(f32)).reshape(b, s, LRU_WIDTH) + br.astype(f32))
    i = jax.nn.sigmoid(jnp.einsum('bshi,hij->bshj', xh, wi.astype(f32)).reshape(b, s, LRU_WIDTH) + bi.astype(f32))
    log_a = -LRU_C * r * jax.nn.softplus(-lam.astype(f32))
    a = jnp.exp(log_a)
    u = jnp.sqrt(-jnp.expm1(2.0 * log_a)) * (i * xc)
    _, h = lax.associative_scan(_linear_combine, (a, u), axis=1)
    return h * jax.nn.gelu(gate.astype(f32))


def _s5_branch(u, a_re, a_im, b_re, b_im, c_re, c_im, d, log_step, w_glu, b_glu):
    f32 = jnp.float32
    bsz, s, _ = u.shape
    uf = u.astype(f32).reshape(bsz, s, N_S5_GROUPS, S5_GROUP)
    a_re, a_im = a_re.astype(f32), a_im.astype(f32)
    b_re, b_im = b_re.astype(f32), b_im.astype(f32)
    step = jnp.exp(log_step.astype(f32))[:, None]
    dt_re, dt_im = step * a_re, step * a_im
    mag = jnp.exp(dt_re)
    ab_re, ab_im = mag * jnp.cos(dt_im), mag * jnp.sin(dt_im)
    z_re, z_im = ab_re - 1.0, ab_im
    den = a_re * a_re + a_im * a_im
    f_re = (z_re * a_re + z_im * a_im) / den
    f_im = (z_im * a_re - z_re * a_im) / den
    bb_re = f_re[..., None] * b_re - f_im[..., None] * b_im
    bb_im = f_re[..., None] * b_im + f_im[..., None] * b_re
    bu_re = jnp.einsum('bsgc,gpc->bsgp', uf, bb_re)
    bu_im = jnp.einsum('bsgc,gpc->bsgp', uf, bb_im)
    shape = bu_re.shape
    elems = (jnp.broadcast_to(ab_re, shape), jnp.broadcast_to(ab_im, shape), bu_re, bu_im)
    _, _, x_re, x_im = lax.associative_scan(_complex_combine, elems, axis=1)
    y = (jnp.einsum('bsgp,gcp->bsgc', x_re, c_re.astype(f32))
         - jnp.einsum('bsgp,gcp->bsgc', x_im, c_im.astype(f32))
         + d.astype(f32) * uf)
    y = jax.nn.gelu(y.reshape(bsz, s, S5_WIDTH))
    return y * jax.nn.sigmoid(y @ w_glu.astype(f32) + b_glu.astype(f32))


def _hybrid_mixer(h, w_in, lru_conv_w, lru_conv_b, lru_wr, lru_br, lru_wi, lru_bi, lru_lambda,
                  s5_a_re, s5_a_im, s5_b_re, s5_b_im, s5_c_re, s5_c_im, s5_d, s5_log_step,
                  s5_w_glu, s5_b_glu, mix_norm_g, w_out):
    b, s, _ = h.shape
    proj = h @ w_in

    def heads(off):
        return proj[..., off:off + ATTN_WIDTH].reshape(b, s, N_ATTN_HEADS, HEAD_DIM)

    attn = _dilated_attention(heads(Q_OFF), heads(K_OFF), heads(V_OFF)).reshape(b, s, ATTN_WIDTH)
    lru = _rg_lru_branch(proj[..., LRU_X_OFF:LRU_X_OFF + LRU_WIDTH],
                         proj[..., LRU_G_OFF:LRU_G_OFF + LRU_WIDTH],
                         lru_conv_w, lru_conv_b, lru_wr, lru_br, lru_wi, lru_bi, lru_lambda)
    ssm = _s5_branch(proj[..., S5_OFF:S5_OFF + S5_WIDTH], s5_a_re, s5_a_im, s5_b_re, s5_b_im,
                     s5_c_re, s5_c_im, s5_d, s5_log_step, s5_w_glu, s5_b_glu)
    g_attn = mix_norm_g[:ATTN_WIDTH]
    g_lru = mix_norm_g[ATTN_WIDTH:ATTN_WIDTH + LRU_WIDTH]
    g_s5 = mix_norm_g[ATTN_WIDTH + LRU_WIDTH:]
    mixed = jnp.concatenate([_rms_norm(attn, g_attn), _rms_norm(lru, g_lru), _rms_norm(ssm, g_s5)],
                            axis=-1).astype(h.dtype)
    return mixed @ w_out


def _conv_ffn(h, w_up, conv_w, conv_b, w_down):
    up = _causal_dwconv(h @ w_up, conv_w, conv_b)
    gate, val = jnp.split(up, 2, axis=-1)
    return (jax.nn.gelu(gate) * val) @ w_down


def _fwd_setup_inputs(seed: int = 0) -> dict:
    key = jax.random.key(seed)
    ks = iter(jax.random.split(key, 32))
    f32 = jnp.float32
    L = DEPTH

    def nrm(shape, scale):
        return jax.random.normal(next(ks), shape, f32) * scale

    x = nrm((BATCH, SEQ, D_MODEL), 1.0)
    w_in = nrm((L, D_MODEL, D_IN), D_MODEL ** -0.5)
    lru_conv_w = nrm((L, LRU_CONV, LRU_WIDTH), LRU_CONV ** -0.5)
    lru_conv_b = nrm((L, LRU_WIDTH), 0.02)
    lru_wr = nrm((L, N_LRU_HEADS, LRU_HEAD, LRU_HEAD), LRU_HEAD ** -0.5)
    lru_br = nrm((L, LRU_WIDTH), 0.02)
    lru_wi = nrm((L, N_LRU_HEADS, LRU_HEAD, LRU_HEAD), LRU_HEAD ** -0.5)
    lru_bi = nrm((L, LRU_WIDTH), 0.02)
    a_c = jax.random.uniform(next(ks), (L, LRU_WIDTH), f32, 0.9, 0.999)
    a0 = a_c ** (1.0 / LRU_C)
    lru_lambda = jnp.log(a0) - jnp.log1p(-a0)
    s5_a_re = -0.5 + nrm((L, N_S5_GROUPS, S5_STATE), 0.01)
    s5_a_im = jnp.pi * jnp.arange(S5_STATE, dtype=f32) + nrm((L, N_S5_GROUPS, S5_STATE), 0.01)
    s5_b_re = nrm((L, N_S5_GROUPS, S5_STATE, S5_GROUP), (2 * S5_GROUP) ** -0.5)
    s5_b_im = nrm((L, N_S5_GROUPS, S5_STATE, S5_GROUP), (2 * S5_GROUP) ** -0.5)
    s5_c_re = nrm((L, N_S5_GROUPS, S5_GROUP, S5_STATE), (2 * S5_STATE) ** -0.5)
    s5_c_im = nrm((L, N_S5_GROUPS, S5_GROUP, S5_STATE), (2 * S5_STATE) ** -0.5)
    s5_d = nrm((L, N_S5_GROUPS, S5_GROUP), 1.0)
    s5_log_step = jax.random.uniform(next(ks), (L, N_S5_GROUPS), f32, math.log(1e-3), math.log(1e-1))
    s5_w_glu = nrm((L, S5_WIDTH, S5_WIDTH), S5_WIDTH ** -0.5)
    s5_b_glu = nrm((L, S5_WIDTH), 0.02)
    mix_norm_g = 1.0 + nrm((L, D_MIX), 0.02)
    w_out = nrm((L, D_MIX, D_MODEL), D_MIX ** -0.5 * DEEPNORM_BETA)
    ln1_g = 1.0 + nrm((L, D_MODEL), 0.02)
    ln1_b = nrm((L, D_MODEL), 0.02)
    w_up = nrm((L, D_MODEL, 2 * D_FF), D_MODEL ** -0.5)
    ffn_conv_w = nrm((L, FFN_CONV, 2 * D_FF), FFN_CONV ** -0.5)
    ffn_conv_b = nrm((L, 2 * D_FF), 0.02)
    w_down = nrm((L, D_FF, D_MODEL), D_FF ** -0.5 * DEEPNORM_BETA)
    ln2_g = 1.0 + nrm((L, D_MODEL), 0.02)
    ln2_b = nrm((L, D_MODEL), 0.02)
    return {'x': x, 'w_in': w_in, 'lru_conv_w': lru_conv_w, 'lru_conv_b': lru_conv_b,
            'lru_wr': lru_wr, 'lru_br': lru_br, 'lru_wi': lru_wi, 'lru_bi': lru_bi,
            'lru_lambda': lru_lambda, 's5_a_re': s5_a_re, 's5_a_im': s5_a_im,
            's5_b_re': s5_b_re, 's5_b_im': s5_b_im, 's5_c_re': s5_c_re, 's5_c_im': s5_c_im,
            's5_d': s5_d, 's5_log_step': s5_log_step, 's5_w_glu': s5_w_glu, 's5_b_glu': s5_b_glu,
            'mix_norm_g': mix_norm_g, 'w_out': w_out, 'ln1_g': ln1_g, 'ln1_b': ln1_b,
            'w_up': w_up, 'ffn_conv_w': ffn_conv_w, 'ffn_conv_b': ffn_conv_b, 'w_down': w_down,
            'ln2_g': ln2_g, 'ln2_b': ln2_b}


def _fwd_reference(x, w_in, lru_conv_w, lru_conv_b, lru_wr, lru_br, lru_wi, lru_bi, lru_lambda,
              s5_a_re, s5_a_im, s5_b_re, s5_b_im, s5_c_re, s5_c_im, s5_d, s5_log_step,
              s5_w_glu, s5_b_glu, mix_norm_g, w_out, ln1_g, ln1_b, w_up, ffn_conv_w, ffn_conv_b,
              w_down, ln2_g, ln2_b):
    h = x
    for l in range(DEPTH):
        mix = _hybrid_mixer(h, w_in[l], lru_conv_w[l], lru_conv_b[l], lru_wr[l], lru_br[l],
                            lru_wi[l], lru_bi[l], lru_lambda[l], s5_a_re[l], s5_a_im[l],
                            s5_b_re[l], s5_b_im[l], s5_c_re[l], s5_c_im[l], s5_d[l],
                            s5_log_step[l], s5_w_glu[l], s5_b_glu[l], mix_norm_g[l], w_out[l])
        h = _layer_norm(DEEPNORM_ALPHA * h + mix, ln1_g[l], ln1_b[l])
        ffn = _conv_ffn(h, w_up[l], ffn_conv_w[l], ffn_conv_b[l], w_down[l])
        h = _layer_norm(DEEPNORM_ALPHA * h + ffn, ln2_g[l], ln2_b[l])
    return h


import jax as _jax
import jax.numpy as _jnp

TWIN_FORMAT = 'train_step'
FWD_PARAMS = ['x', 'w_in', 'lru_conv_w', 'lru_conv_b', 'lru_wr', 'lru_br', 'lru_wi', 'lru_bi', 'lru_lambda', 's5_a_re', 's5_a_im', 's5_b_re', 's5_b_im', 's5_c_re', 's5_c_im', 's5_d', 's5_log_step', 's5_w_glu', 's5_b_glu', 'mix_norm_g', 'w_out', 'ln1_g', 'ln1_b', 'w_up', 'ffn_conv_w', 'ffn_conv_b', 'w_down', 'ln2_g', 'ln2_b']
TWIN_WEIGHTS = ['w_in', 'lru_conv_w', 'lru_conv_b', 'lru_wr', 'lru_br', 'lru_wi', 'lru_bi', 'lru_lambda', 's5_a_re', 's5_a_im', 's5_b_re', 's5_b_im', 's5_c_re', 's5_c_im', 's5_d', 's5_log_step', 's5_w_glu', 's5_b_glu', 'mix_norm_g', 'w_out', 'ln1_g', 'ln1_b', 'w_up', 'ffn_conv_w', 'ffn_conv_b', 'w_down', 'ln2_g', 'ln2_b']
TWIN_DIFF_INPUT = 'x'
TWIN_INPUTS = ['x', 'w_in', 'lru_conv_w', 'lru_conv_b', 'lru_wr', 'lru_br', 'lru_wi', 'lru_bi', 'lru_lambda', 's5_a_re', 's5_a_im', 's5_b_re', 's5_b_im', 's5_c_re', 's5_c_im', 's5_d', 's5_log_step', 's5_w_glu', 's5_b_glu', 'mix_norm_g', 'w_out', 'ln1_g', 'ln1_b', 'w_up', 'ffn_conv_w', 'ffn_conv_b', 'w_down', 'ln2_g', 'ln2_b', 'loss_target', 'm_w_in', 'm_lru_conv_w', 'm_lru_conv_b', 'm_lru_wr', 'm_lru_br', 'm_lru_wi', 'm_lru_bi', 'm_lru_lambda', 'm_s5_a_re', 'm_s5_a_im', 'm_s5_b_re', 'm_s5_b_im', 'm_s5_c_re', 'm_s5_c_im', 'm_s5_d', 'm_s5_log_step', 'm_s5_w_glu', 'm_s5_b_glu', 'm_mix_norm_g', 'm_w_out', 'm_ln1_g', 'm_ln1_b', 'm_w_up', 'm_ffn_conv_w', 'm_ffn_conv_b', 'm_w_down', 'm_ln2_g', 'm_ln2_b', 'v_w_in', 'v_lru_conv_w', 'v_lru_conv_b', 'v_lru_wr', 'v_lru_br', 'v_lru_wi', 'v_lru_bi', 'v_lru_lambda', 'v_s5_a_re', 'v_s5_a_im', 'v_s5_b_re', 'v_s5_b_im', 'v_s5_c_re', 'v_s5_c_im', 'v_s5_d', 'v_s5_log_step', 'v_s5_w_glu', 'v_s5_b_glu', 'v_mix_norm_g', 'v_w_out', 'v_ln1_g', 'v_ln1_b', 'v_w_up', 'v_ffn_conv_w', 'v_ffn_conv_b', 'v_w_down', 'v_ln2_g', 'v_ln2_b']
TWIN_OUTPUTS = ['loss', 'grad_x', 'grad_w_in', 'grad_lru_conv_w', 'grad_lru_conv_b', 'grad_lru_wr', 'grad_lru_br', 'grad_lru_wi', 'grad_lru_bi', 'grad_lru_lambda', 'grad_s5_a_re', 'grad_s5_a_im', 'grad_s5_b_re', 'grad_s5_b_im', 'grad_s5_c_re', 'grad_s5_c_im', 'grad_s5_d', 'grad_s5_log_step', 'grad_s5_w_glu', 'grad_s5_b_glu', 'grad_mix_norm_g', 'grad_w_out', 'grad_ln1_g', 'grad_ln1_b', 'grad_w_up', 'grad_ffn_conv_w', 'grad_ffn_conv_b', 'grad_w_down', 'grad_ln2_g', 'grad_ln2_b', 'delta_w_in', 'delta_lru_conv_w', 'delta_lru_conv_b', 'delta_lru_wr', 'delta_lru_br', 'delta_lru_wi', 'delta_lru_bi', 'delta_lru_lambda', 'delta_s5_a_re', 'delta_s5_a_im', 'delta_s5_b_re', 'delta_s5_b_im', 'delta_s5_c_re', 'delta_s5_c_im', 'delta_s5_d', 'delta_s5_log_step', 'delta_s5_w_glu', 'delta_s5_b_glu', 'delta_mix_norm_g', 'delta_w_out', 'delta_ln1_g', 'delta_ln1_b', 'delta_w_up', 'delta_ffn_conv_w', 'delta_ffn_conv_b', 'delta_w_down', 'delta_ln2_g', 'delta_ln2_b', 'new_m_w_in', 'new_m_lru_conv_w', 'new_m_lru_conv_b', 'new_m_lru_wr', 'new_m_lru_br', 'new_m_lru_wi', 'new_m_lru_bi', 'new_m_lru_lambda', 'new_m_s5_a_re', 'new_m_s5_a_im', 'new_m_s5_b_re', 'new_m_s5_b_im', 'new_m_s5_c_re', 'new_m_s5_c_im', 'new_m_s5_d', 'new_m_s5_log_step', 'new_m_s5_w_glu', 'new_m_s5_b_glu', 'new_m_mix_norm_g', 'new_m_w_out', 'new_m_ln1_g', 'new_m_ln1_b', 'new_m_w_up', 'new_m_ffn_conv_w', 'new_m_ffn_conv_b', 'new_m_w_down', 'new_m_ln2_g', 'new_m_ln2_b', 'new_v_w_in', 'new_v_lru_conv_w', 'new_v_lru_conv_b', 'new_v_lru_wr', 'new_v_lru_br', 'new_v_lru_wi', 'new_v_lru_bi', 'new_v_lru_lambda', 'new_v_s5_a_re', 'new_v_s5_a_im', 'new_v_s5_b_re', 'new_v_s5_b_im', 'new_v_s5_c_re', 'new_v_s5_c_im', 'new_v_s5_d', 'new_v_s5_log_step', 'new_v_s5_w_glu', 'new_v_s5_b_glu', 'new_v_mix_norm_g', 'new_v_w_out', 'new_v_ln1_g', 'new_v_ln1_b', 'new_v_w_up', 'new_v_ffn_conv_w', 'new_v_ffn_conv_b', 'new_v_w_down', 'new_v_ln2_g', 'new_v_ln2_b']
TWIN_LEAF_KINDS = {'loss': 'loss', 'grad_x': 'grad_x', 'grad_w_in': 'grad_w', 'grad_lru_conv_w': 'grad_w', 'grad_lru_conv_b': 'grad_w', 'grad_lru_wr': 'grad_w', 'grad_lru_br': 'grad_w', 'grad_lru_wi': 'grad_w', 'grad_lru_bi': 'grad_w', 'grad_lru_lambda': 'grad_w', 'grad_s5_a_re': 'grad_w', 'grad_s5_a_im': 'grad_w', 'grad_s5_b_re': 'grad_w', 'grad_s5_b_im': 'grad_w', 'grad_s5_c_re': 'grad_w', 'grad_s5_c_im': 'grad_w', 'grad_s5_d': 'grad_w', 'grad_s5_log_step': 'grad_w', 'grad_s5_w_glu': 'grad_w', 'grad_s5_b_glu': 'grad_w', 'grad_mix_norm_g': 'grad_w', 'grad_w_out': 'grad_w', 'grad_ln1_g': 'grad_w', 'grad_ln1_b': 'grad_w', 'grad_w_up': 'grad_w', 'grad_ffn_conv_w': 'grad_w', 'grad_ffn_conv_b': 'grad_w', 'grad_w_down': 'grad_w', 'grad_ln2_g': 'grad_w', 'grad_ln2_b': 'grad_w', 'delta_w_in': 'delta_w', 'delta_lru_conv_w': 'delta_w', 'delta_lru_conv_b': 'delta_w', 'delta_lru_wr': 'delta_w', 'delta_lru_br': 'delta_w', 'delta_lru_wi': 'delta_w', 'delta_lru_bi': 'delta_w', 'delta_lru_lambda': 'delta_w', 'delta_s5_a_re': 'delta_w', 'delta_s5_a_im': 'delta_w', 'delta_s5_b_re': 'delta_w', 'delta_s5_b_im': 'delta_w', 'delta_s5_c_re': 'delta_w', 'delta_s5_c_im': 'delta_w', 'delta_s5_d': 'delta_w', 'delta_s5_log_step': 'delta_w', 'delta_s5_w_glu': 'delta_w', 'delta_s5_b_glu': 'delta_w', 'delta_mix_norm_g': 'delta_w', 'delta_w_out': 'delta_w', 'delta_ln1_g': 'delta_w', 'delta_ln1_b': 'delta_w', 'delta_w_up': 'delta_w', 'delta_ffn_conv_w': 'delta_w', 'delta_ffn_conv_b': 'delta_w', 'delta_w_down': 'delta_w', 'delta_ln2_g': 'delta_w', 'delta_ln2_b': 'delta_w', 'new_m_w_in': 'new_m', 'new_m_lru_conv_w': 'new_m', 'new_m_lru_conv_b': 'new_m', 'new_m_lru_wr': 'new_m', 'new_m_lru_br': 'new_m', 'new_m_lru_wi': 'new_m', 'new_m_lru_bi': 'new_m', 'new_m_lru_lambda': 'new_m', 'new_m_s5_a_re': 'new_m', 'new_m_s5_a_im': 'new_m', 'new_m_s5_b_re': 'new_m', 'new_m_s5_b_im': 'new_m', 'new_m_s5_c_re': 'new_m', 'new_m_s5_c_im': 'new_m', 'new_m_s5_d': 'new_m', 'new_m_s5_log_step': 'new_m', 'new_m_s5_w_glu': 'new_m', 'new_m_s5_b_glu': 'new_m', 'new_m_mix_norm_g': 'new_m', 'new_m_w_out': 'new_m', 'new_m_ln1_g': 'new_m', 'new_m_ln1_b': 'new_m', 'new_m_w_up': 'new_m', 'new_m_ffn_conv_w': 'new_m', 'new_m_ffn_conv_b': 'new_m', 'new_m_w_down': 'new_m', 'new_m_ln2_g': 'new_m', 'new_m_ln2_b': 'new_m', 'new_v_w_in': 'new_v', 'new_v_lru_conv_w': 'new_v', 'new_v_lru_conv_b': 'new_v', 'new_v_lru_wr': 'new_v', 'new_v_lru_br': 'new_v', 'new_v_lru_wi': 'new_v', 'new_v_lru_bi': 'new_v', 'new_v_lru_lambda': 'new_v', 'new_v_s5_a_re': 'new_v', 'new_v_s5_a_im': 'new_v', 'new_v_s5_b_re': 'new_v', 'new_v_s5_b_im': 'new_v', 'new_v_s5_c_re': 'new_v', 'new_v_s5_c_im': 'new_v', 'new_v_s5_d': 'new_v', 'new_v_s5_log_step': 'new_v', 'new_v_s5_w_glu': 'new_v', 'new_v_s5_b_glu': 'new_v', 'new_v_mix_norm_g': 'new_v', 'new_v_w_out': 'new_v', 'new_v_ln1_g': 'new_v', 'new_v_ln1_b': 'new_v', 'new_v_w_up': 'new_v', 'new_v_ffn_conv_w': 'new_v', 'new_v_ffn_conv_b': 'new_v', 'new_v_w_down': 'new_v', 'new_v_ln2_g': 'new_v', 'new_v_ln2_b': 'new_v'}


def _forward(args):
    return _fwd_reference(*[args[k] for k in FWD_PARAMS])


def _output_shape():
    def fwd():
        inp = _fwd_setup_inputs(0)
        return _fwd_reference(*[inp[k] for k in FWD_PARAMS])
    out = _jax.eval_shape(fwd)
    return out.shape, out.dtype

N_MICROBATCH = 1
ADAM_LR = 0.001
ADAM_B1 = 0.9
ADAM_B2 = 0.999
ADAM_EPS = 1e-08
ADAM_WD = 0.01
ADAM_STEP = 10
PER_EXAMPLE_BATCH_AXIS = {'x': 0, 'loss_target': 0}
SHARED_INPUTS = []
_WEIGHT_DTYPES = {'w_in': _jnp.float32, 'lru_conv_w': _jnp.float32, 'lru_conv_b': _jnp.float32, 'lru_wr': _jnp.float32, 'lru_br': _jnp.float32, 'lru_wi': _jnp.float32, 'lru_bi': _jnp.float32, 'lru_lambda': _jnp.float32, 's5_a_re': _jnp.float32, 's5_a_im': _jnp.float32, 's5_b_re': _jnp.float32, 's5_b_im': _jnp.float32, 's5_c_re': _jnp.float32, 's5_c_im': _jnp.float32, 's5_d': _jnp.float32, 's5_log_step': _jnp.float32, 's5_w_glu': _jnp.float32, 's5_b_glu': _jnp.float32, 'mix_norm_g': _jnp.float32, 'w_out': _jnp.float32, 'ln1_g': _jnp.float32, 'ln1_b': _jnp.float32, 'w_up': _jnp.float32, 'ffn_conv_w': _jnp.float32, 'ffn_conv_b': _jnp.float32, 'w_down': _jnp.float32, 'ln2_g': _jnp.float32, 'ln2_b': _jnp.float32}
MOMENT_SCALE = {'w_in': 6.658840e-02, 'lru_conv_w': 8.013134e-02, 'lru_conv_b': 1.014331e+00, 'lru_wr': 2.903332e-02, 'lru_br': 2.107803e-02, 'lru_wi': 5.333775e-02, 'lru_bi': 2.762688e-02, 'lru_lambda': 3.815291e-02, 's5_a_re': 3.820653e-03, 's5_a_im': 3.417523e-03, 's5_b_re': 2.292410e-03, 's5_b_im': 2.314758e-03, 's5_c_re': 4.815804e-03, 's5_c_im': 4.631903e-03, 's5_d': 1.406875e-01, 's5_log_step': 3.598283e+00, 's5_w_glu': 2.330365e-02, 's5_b_glu': 5.104208e-02, 'mix_norm_g': 9.592126e-02, 'w_out': 2.013025e-01, 'ln1_g': 1.090142e+00, 'ln1_b': 8.541726e-01, 'w_up': 2.465089e-02, 'ffn_conv_w': 2.517966e-02, 'ffn_conv_b': 4.062188e-02, 'w_down': 8.467597e-02, 'ln2_g': 2.270641e+01, 'ln2_b': 3.724663e+00}


def _to_microbatches(a, axis):
    t = _jnp.moveaxis(a, axis, 0)
    t = t.reshape((N_MICROBATCH, t.shape[0] // N_MICROBATCH) + t.shape[1:])
    return _jnp.moveaxis(t, 1, axis + 1)


def setup_inputs(seed: int = 0) -> dict:
    inp = _fwd_setup_inputs(seed)
    key = _jax.random.fold_in(_jax.random.key(seed), 7919)
    shape, _ = _output_shape()
    out = dict(inp)
    out["loss_target"] = _jax.random.normal(_jax.random.fold_in(key, 0), shape, _jnp.float32)
    for i, name in enumerate(TWIN_WEIGHTS):
        w = inp[name].astype(_jnp.float32)
        if MOMENT_SCALE is None:
            s = _jnp.sqrt(_jnp.mean(_jnp.square(w)) + 1e-30)
        else:
            s = MOMENT_SCALE[name]
        km, kv = _jax.random.split(_jax.random.fold_in(key, i + 1))
        out[name] = w
        out["m_" + name] = s * _jax.random.normal(km, w.shape, _jnp.float32)
        out["v_" + name] = (s * s) * _jax.random.uniform(kv, w.shape, _jnp.float32, 0.5, 1.5)
    if N_MICROBATCH > 1:
        for name, axis in PER_EXAMPLE_BATCH_AXIS.items():
            out[name] = _to_microbatches(out[name], axis)
    return {'x': out['x'], 'w_in': out['w_in'], 'lru_conv_w': out['lru_conv_w'], 'lru_conv_b': out['lru_conv_b'], 'lru_wr': out['lru_wr'], 'lru_br': out['lru_br'], 'lru_wi': out['lru_wi'], 'lru_bi': out['lru_bi'], 'lru_lambda': out['lru_lambda'], 's5_a_re': out['s5_a_re'], 's5_a_im': out['s5_a_im'], 's5_b_re': out['s5_b_re'], 's5_b_im': out['s5_b_im'], 's5_c_re': out['s5_c_re'], 's5_c_im': out['s5_c_im'], 's5_d': out['s5_d'], 's5_log_step': out['s5_log_step'], 's5_w_glu': out['s5_w_glu'], 's5_b_glu': out['s5_b_glu'], 'mix_norm_g': out['mix_norm_g'], 'w_out': out['w_out'], 'ln1_g': out['ln1_g'], 'ln1_b': out['ln1_b'], 'w_up': out['w_up'], 'ffn_conv_w': out['ffn_conv_w'], 'ffn_conv_b': out['ffn_conv_b'], 'w_down': out['w_down'], 'ln2_g': out['ln2_g'], 'ln2_b': out['ln2_b'], 'loss_target': out['loss_target'], 'm_w_in': out['m_w_in'], 'm_lru_conv_w': out['m_lru_conv_w'], 'm_lru_conv_b': out['m_lru_conv_b'], 'm_lru_wr': out['m_lru_wr'], 'm_lru_br': out['m_lru_br'], 'm_lru_wi': out['m_lru_wi'], 'm_lru_bi': out['m_lru_bi'], 'm_lru_lambda': out['m_lru_lambda'], 'm_s5_a_re': out['m_s5_a_re'], 'm_s5_a_im': out['m_s5_a_im'], 'm_s5_b_re': out['m_s5_b_re'], 'm_s5_b_im': out['m_s5_b_im'], 'm_s5_c_re': out['m_s5_c_re'], 'm_s5_c_im': out['m_s5_c_im'], 'm_s5_d': out['m_s5_d'], 'm_s5_log_step': out['m_s5_log_step'], 'm_s5_w_glu': out['m_s5_w_glu'], 'm_s5_b_glu': out['m_s5_b_glu'], 'm_mix_norm_g': out['m_mix_norm_g'], 'm_w_out': out['m_w_out'], 'm_ln1_g': out['m_ln1_g'], 'm_ln1_b': out['m_ln1_b'], 'm_w_up': out['m_w_up'], 'm_ffn_conv_w': out['m_ffn_conv_w'], 'm_ffn_conv_b': out['m_ffn_conv_b'], 'm_w_down': out['m_w_down'], 'm_ln2_g': out['m_ln2_g'], 'm_ln2_b': out['m_ln2_b'], 'v_w_in': out['v_w_in'], 'v_lru_conv_w': out['v_lru_conv_w'], 'v_lru_conv_b': out['v_lru_conv_b'], 'v_lru_wr': out['v_lru_wr'], 'v_lru_br': out['v_lru_br'], 'v_lru_wi': out['v_lru_wi'], 'v_lru_bi': out['v_lru_bi'], 'v_lru_lambda': out['v_lru_lambda'], 'v_s5_a_re': out['v_s5_a_re'], 'v_s5_a_im': out['v_s5_a_im'], 'v_s5_b_re': out['v_s5_b_re'], 'v_s5_b_im': out['v_s5_b_im'], 'v_s5_c_re': out['v_s5_c_re'], 'v_s5_c_im': out['v_s5_c_im'], 'v_s5_d': out['v_s5_d'], 'v_s5_log_step': out['v_s5_log_step'], 'v_s5_w_glu': out['v_s5_w_glu'], 'v_s5_b_glu': out['v_s5_b_glu'], 'v_mix_norm_g': out['v_mix_norm_g'], 'v_w_out': out['v_w_out'], 'v_ln1_g': out['v_ln1_g'], 'v_ln1_b': out['v_ln1_b'], 'v_w_up': out['v_w_up'], 'v_ffn_conv_w': out['v_ffn_conv_w'], 'v_ffn_conv_b': out['v_ffn_conv_b'], 'v_w_down': out['v_w_down'], 'v_ln2_g': out['v_ln2_g'], 'v_ln2_b': out['v_ln2_b']}


def _loss(weights, diff, rest, loss_target):
    with _jax.named_scope("forward"):
        args = {**rest, TWIN_DIFF_INPUT: diff, **{k: w.astype(_WEIGHT_DTYPES[k]) for k, w in weights.items()}}
        y = _forward(args)
    with _jax.named_scope("loss_head"):
        err = _jnp.square(y.astype(_jnp.float32) - loss_target)
        return 0.5 * _jnp.sum(_jnp.mean(err, axis=-1)) if err.ndim else 0.5 * err


def _adamw(w, g, m, v):
    m = ADAM_B1 * m + (1.0 - ADAM_B1) * g
    v = ADAM_B2 * v + (1.0 - ADAM_B2) * _jnp.square(g)
    m_hat = m / (1.0 - ADAM_B1 ** ADAM_STEP)
    v_hat = v / (1.0 - ADAM_B2 ** ADAM_STEP)
    delta = -ADAM_LR * (m_hat / (_jnp.sqrt(v_hat) + ADAM_EPS) + ADAM_WD * w)
    return delta, m, v


def reference(x, w_in, lru_conv_w, lru_conv_b, lru_wr, lru_br, lru_wi, lru_bi, lru_lambda, s5_a_re, s5_a_im, s5_b_re, s5_b_im, s5_c_re, s5_c_im, s5_d, s5_log_step, s5_w_glu, s5_b_glu, mix_norm_g, w_out, ln1_g, ln1_b, w_up, ffn_conv_w, ffn_conv_b, w_down, ln2_g, ln2_b, loss_target, m_w_in, m_lru_conv_w, m_lru_conv_b, m_lru_wr, m_lru_br, m_lru_wi, m_lru_bi, m_lru_lambda, m_s5_a_re, m_s5_a_im, m_s5_b_re, m_s5_b_im, m_s5_c_re, m_s5_c_im, m_s5_d, m_s5_log_step, m_s5_w_glu, m_s5_b_glu, m_mix_norm_g, m_w_out, m_ln1_g, m_ln1_b, m_w_up, m_ffn_conv_w, m_ffn_conv_b, m_w_down, m_ln2_g, m_ln2_b, v_w_in, v_lru_conv_w, v_lru_conv_b, v_lru_wr, v_lru_br, v_lru_wi, v_lru_bi, v_lru_lambda, v_s5_a_re, v_s5_a_im, v_s5_b_re, v_s5_b_im, v_s5_c_re, v_s5_c_im, v_s5_d, v_s5_log_step, v_s5_w_glu, v_s5_b_glu, v_mix_norm_g, v_w_out, v_ln1_g, v_ln1_b, v_w_up, v_ffn_conv_w, v_ffn_conv_b, v_w_down, v_ln2_g, v_ln2_b):
    given = dict(x=x, w_in=w_in, lru_conv_w=lru_conv_w, lru_conv_b=lru_conv_b, lru_wr=lru_wr, lru_br=lru_br, lru_wi=lru_wi, lru_bi=lru_bi, lru_lambda=lru_lambda, s5_a_re=s5_a_re, s5_a_im=s5_a_im, s5_b_re=s5_b_re, s5_b_im=s5_b_im, s5_c_re=s5_c_re, s5_c_im=s5_c_im, s5_d=s5_d, s5_log_step=s5_log_step, s5_w_glu=s5_w_glu, s5_b_glu=s5_b_glu, mix_norm_g=mix_norm_g, w_out=w_out, ln1_g=ln1_g, ln1_b=ln1_b, w_up=w_up, ffn_conv_w=ffn_conv_w, ffn_conv_b=ffn_conv_b, w_down=w_down, ln2_g=ln2_g, ln2_b=ln2_b, loss_target=loss_target, m_w_in=m_w_in, m_lru_conv_w=m_lru_conv_w, m_lru_conv_b=m_lru_conv_b, m_lru_wr=m_lru_wr, m_lru_br=m_lru_br, m_lru_wi=m_lru_wi, m_lru_bi=m_lru_bi, m_lru_lambda=m_lru_lambda, m_s5_a_re=m_s5_a_re, m_s5_a_im=m_s5_a_im, m_s5_b_re=m_s5_b_re, m_s5_b_im=m_s5_b_im, m_s5_c_re=m_s5_c_re, m_s5_c_im=m_s5_c_im, m_s5_d=m_s5_d, m_s5_log_step=m_s5_log_step, m_s5_w_glu=m_s5_w_glu, m_s5_b_glu=m_s5_b_glu, m_mix_norm_g=m_mix_norm_g, m_w_out=m_w_out, m_ln1_g=m_ln1_g, m_ln1_b=m_ln1_b, m_w_up=m_w_up, m_ffn_conv_w=m_ffn_conv_w, m_ffn_conv_b=m_ffn_conv_b, m_w_down=m_w_down, m_ln2_g=m_ln2_g, m_ln2_b=m_ln2_b, v_w_in=v_w_in, v_lru_conv_w=v_lru_conv_w, v_lru_conv_b=v_lru_conv_b, v_lru_wr=v_lru_wr, v_lru_br=v_lru_br, v_lru_wi=v_lru_wi, v_lru_bi=v_lru_bi, v_lru_lambda=v_lru_lambda, v_s5_a_re=v_s5_a_re, v_s5_a_im=v_s5_a_im, v_s5_b_re=v_s5_b_re, v_s5_b_im=v_s5_b_im, v_s5_c_re=v_s5_c_re, v_s5_c_im=v_s5_c_im, v_s5_d=v_s5_d, v_s5_log_step=v_s5_log_step, v_s5_w_glu=v_s5_w_glu, v_s5_b_glu=v_s5_b_glu, v_mix_norm_g=v_mix_norm_g, v_w_out=v_w_out, v_ln1_g=v_ln1_g, v_ln1_b=v_ln1_b, v_w_up=v_w_up, v_ffn_conv_w=v_ffn_conv_w, v_ffn_conv_b=v_ffn_conv_b, v_w_down=v_w_down, v_ln2_g=v_ln2_g, v_ln2_b=v_ln2_b)
    weights = {n: given[n] for n in TWIN_WEIGHTS}
    shared = {n: given[n] for n in SHARED_INPUTS}
    per_example = {n: given[n] for n in ['x']}
    grad_fn = _jax.value_and_grad(_loss, argnums=(0, 1))

    def one_microbatch(ex, loss_target):
        ex = dict(ex)
        diff = ex.pop(TWIN_DIFF_INPUT)
        return grad_fn(weights, diff, {**shared, **ex}, loss_target)

    if N_MICROBATCH == 1:
        loss, (grad_w, grad_x) = one_microbatch(per_example, given["loss_target"])
    else:
        def body(carry, xs):
            loss_sum, grad_sum = carry
            l_k, (gw_k, gx_k) = one_microbatch(xs[0], xs[1])
            with _jax.named_scope("update"):
                return (loss_sum + l_k, _jax.tree.map(_jnp.add, grad_sum, gw_k)), gx_k

        init = (_jnp.zeros((), _jnp.float32), _jax.tree.map(_jnp.zeros_like, weights))
        (loss, grad_w), grad_x = _jax.lax.scan(body, init, (per_example, given["loss_target"]))
    with _jax.named_scope("update"):
        delta_w, new_m, new_v = {}, {}, {}
        for n in TWIN_WEIGHTS:
            delta_w[n], new_m[n], new_v[n] = _adamw(weights[n], grad_w[n], given["m_" + n], given["v_" + n])
    return (loss, grad_x, *[grad_w[n] for n in TWIN_WEIGHTS], *[delta_w[n] for n in TWIN_WEIGHTS],
            *[new_m[n] for n in TWIN_WEIGHTS], *[new_v[n] for n in TWIN_WEIGHTS])
```

```python
import functools
import math

import jax
import jax.numpy as jnp
from jax import lax
from jax.experimental import pallas as pl
from jax.experimental.pallas import tpu as pltpu

F32 = jnp.float32
BF16 = jnp.bfloat16

N_DEV = 8
DEPTH = 2
D_MODEL = 1024
ATTN_W = 384
LRU_W = 384
S5_W = 256
D_IN = 2176
D_FF = 3072
HEAD = 64
ATTN_BLK = 128
DILATIONS = (1, 4, 16)
S5_G = 16
S5_P = 64
S5_C = 16
S5_STATES = S5_G * S5_P
LRU_C = 8.0
LRU_CONV = 4
FFN_CONV = 3
ROPE_THETA = 10000.0
ALPHA = (2 * DEPTH) ** 0.25
LN_EPS = 1e-5
RMS_EPS = 1e-6
ADAM_LR, ADAM_B1, ADAM_B2, ADAM_EPS, ADAM_WD, ADAM_STEP = 0.001, 0.9, 0.999, 1e-8, 0.01, 10

LANE = 128
SCAN_T = 256
FFN_CB = 512
VMEM_LIMIT = 56 * 1024 * 1024

AXES = ("x", "y", "c")

WEIGHTS = ['w_in', 'lru_conv_w', 'lru_conv_b', 'lru_wr', 'lru_br', 'lru_wi', 'lru_bi', 'lru_lambda',
           's5_a_re', 's5_a_im', 's5_b_re', 's5_b_im', 's5_c_re', 's5_c_im', 's5_d', 's5_log_step',
           's5_w_glu', 's5_b_glu', 'mix_norm_g', 'w_out', 'ln1_g', 'ln1_b', 'w_up', 'ffn_conv_w',
           'ffn_conv_b', 'w_down', 'ln2_g', 'ln2_b']
SHARD_AXIS = {'w_in': 2, 'lru_conv_w': 2, 's5_w_glu': 1, 'w_out': 1, 'w_up': 2, 'ffn_conv_w': 2, 'w_down': 1}
SHARDED = [n for n in WEIGHTS if n in SHARD_AXIS]
REPLICATED = [n for n in WEIGHTS if n not in SHARD_AXIS]
F32_PAYLOAD = ('lru_conv_w', 'ffn_conv_w')


def _cparams(sem=None):
    return pltpu.CompilerParams(dimension_semantics=sem, vmem_limit_bytes=VMEM_LIMIT)


def _gelu(x):
    c = math.sqrt(2.0 / math.pi)
    t = jnp.tanh(c * (x + 0.044715 * (x * x * x)))
    return 0.5 * x * (1.0 + t)


def _gelu_grad(x):
    c = math.sqrt(2.0 / math.pi)
    x2 = x * x
    t = jnp.tanh(c * (x + 0.044715 * (x2 * x)))
    return 0.5 * (1.0 + t) + 0.5 * x * (1.0 - t * t) * (c * (1.0 + 3.0 * 0.044715 * x2))


def _sigmoid(x):
    return 1.0 / (1.0 + jnp.exp(-x))


def _log1p(x):
    u = 1.0 + x
    d = u - 1.0
    return jnp.where(d == 0.0, x, jnp.log(u) * (x / jnp.where(d == 0.0, 1.0, d)))


def _softplus(x):
    return jnp.maximum(x, 0.0) + _log1p(jnp.exp(-jnp.abs(x)))


def _expm1(x):
    return jnp.tanh(0.5 * x) * (jnp.exp(x) + 1.0)


def _dot(a, b):
    return jnp.dot(a.astype(BF16), b.astype(BF16), preferred_element_type=F32)


def _dot_nt(a, b):
    return lax.dot_general(a.astype(BF16), b.astype(BF16), (((1,), (1,)), ((), ())),
                           preferred_element_type=F32)


def _dot_tn(a, b):
    return lax.dot_general(a.astype(BF16), b.astype(BF16), (((0,), (0,)), ((), ())),
                           preferred_element_type=F32)


def _rows(shape):
    return lax.broadcasted_iota(jnp.int32, shape, 0)


def _shift_down(x, s, fill):
    r = pltpu.roll(x, s, axis=0)
    return jnp.where(_rows(x.shape) >= s, r, fill)


def _shift_up(x, s, fill):
    t = x.shape[0]
    r = pltpu.roll(x, t - s, axis=0)
    return jnp.where(_rows(x.shape) < t - s, r, fill)


def _shift_down_prev(x, s, prev8):
    if s == 0:
        return x
    t, l = x.shape
    r = pltpu.roll(x, s, axis=0)
    pr = pltpu.roll(prev8, s, axis=0)
    pad = jnp.concatenate([pr, jnp.zeros((t - 8, l), x.dtype)], axis=0)
    return jnp.where(_rows(x.shape) < s, pad, r)


def _shift_up_next(x, s, next8):
    if s == 0:
        return x
    t, l = x.shape
    r = pltpu.roll(x, t - s, axis=0)
    nx = pltpu.roll(next8, 8 - s, axis=0)
    pad = jnp.concatenate([jnp.zeros((t - 8, l), x.dtype), nx], axis=0)
    return jnp.where(_rows(x.shape) >= t - s, pad, r)


def _scan_fwd(a, x):
    t = x.shape[0]
    s = 1
    while s < t:
        x = x + a * _shift_down(x, s, 0.0)
        a = a * _shift_down(a, s, 1.0)
        s *= 2
    return a, x


def _scan_rev(a, x):
    t = x.shape[0]
    s = 1
    while s < t:
        x = x + a * _shift_up(x, s, 0.0)
        a = a * _shift_up(a, s, 1.0)
        s *= 2
    return a, x


def _cpowers(lr, li, n):
    out = [(lr, li)]
    for _ in range(n - 1):
        lr, li = lr * lr - li * li, 2.0 * lr * li
        out.append((lr, li))
    return out


def _cscan(xr, xi, pows, reverse):
    shift = _shift_up if reverse else _shift_down
    s = 1
    for pr, pi in pows:
        sr = shift(xr, s, 0.0)
        si = shift(xi, s, 0.0)
        xr, xi = xr + pr * sr - pi * si, xi + pr * si + pi * sr
        s *= 2
    return xr, xi


def _mm_nn(a, b, tm, tn, name, out_dtype=F32):
    m, k = a.shape
    n = b.shape[1]

    def body(a_ref, b_ref, o_ref):
        o_ref[...] = _dot(a_ref[...], b_ref[...]).astype(out_dtype)

    return pl.pallas_call(
        body, out_shape=jax.ShapeDtypeStruct((m, n), out_dtype), grid=(n // tn, m // tm),
        in_specs=[pl.BlockSpec((tm, k), lambda j, i: (i, 0)), pl.BlockSpec((k, tn), lambda j, i: (0, j))],
        out_specs=pl.BlockSpec((tm, tn), lambda j, i: (i, j)), name=name,
        compiler_params=_cparams(("parallel", "parallel")))(a, b)


def _mm_nt(a, w, tm, tn, name, add=None, add_scale=1.0):
    m, k = a.shape
    n = w.shape[0]

    def body(*refs):
        if add is None:
            a_ref, w_ref, o_ref = refs
            o_ref[...] = _dot_nt(a_ref[...], w_ref[...])
        else:
            a_ref, w_ref, c_ref, o_ref = refs
            o_ref[...] = _dot_nt(a_ref[...], w_ref[...]) + add_scale * c_ref[...]

    in_specs = [pl.BlockSpec((tm, k), lambda j, i: (i, 0)), pl.BlockSpec((tn, k), lambda j, i: (j, 0))]
    args = [a, w]
    if add is not None:
        in_specs.append(pl.BlockSpec((tm, tn), lambda j, i: (i, j)))
        args.append(add)
    return pl.pallas_call(
        body, out_shape=jax.ShapeDtypeStruct((m, n), F32), grid=(n // tn, m // tm),
        in_specs=in_specs, out_specs=pl.BlockSpec((tm, tn), lambda j, i: (i, j)), name=name,
        compiler_params=_cparams(("parallel", "parallel")))(*args)


def _mm_tn(a, b, tm, tn, ts, name):
    s, m = a.shape
    n = b.shape[1]

    def body(a_ref, b_ref, o_ref):
        @pl.when(pl.program_id(2) == 0)
        def _():
            o_ref[...] = jnp.zeros_like(o_ref)
        o_ref[...] += _dot_tn(a_ref[...], b_ref[...])

    return pl.pallas_call(
        body, out_shape=jax.ShapeDtypeStruct((m, n), F32), grid=(m // tm, n // tn, s // ts),
        in_specs=[pl.BlockSpec((ts, tm), lambda i, j, k: (k, i)), pl.BlockSpec((ts, tn), lambda i, j, k: (k, j))],
        out_specs=pl.BlockSpec((tm, tn), lambda i, j, k: (i, j)), name=name,
        compiler_params=_cparams(("parallel", "parallel", "arbitrary")))(a, b)


def _ln_fwd(a, b, g, bias, name):
    s, d = a.shape
    tm = 512

    def body(a_ref, b_ref, g_ref, bias_ref, r_ref, h_ref):
        r = ALPHA * a_ref[...] + b_ref[...]
        mu = jnp.mean(r, axis=-1, keepdims=True)
        xc = r - mu
        var = jnp.mean(xc * xc, axis=-1, keepdims=True)
        r_ref[...] = r
        h_ref[...] = xc * lax.rsqrt(var + LN_EPS) * g_ref[...] + bias_ref[...]

    row = pl.BlockSpec((tm, d), lambda i: (i, 0))
    vec = pl.BlockSpec((1, d), lambda i: (0, 0))
    return pl.pallas_call(
        body, out_shape=(jax.ShapeDtypeStruct((s, d), F32), jax.ShapeDtypeStruct((s, d), F32)),
        grid=(s // tm,), in_specs=[row, row, vec, vec], out_specs=(row, row), name=name,
        compiler_params=_cparams(("parallel",)))(a, b, g, bias)


def _ln_bwd(r, dh, g, name):
    s, d = r.shape
    tm = 512

    def body(r_ref, dh_ref, g_ref, dr_ref, dg_ref, db_ref):
        @pl.when(pl.program_id(0) == 0)
        def _():
            dg_ref[...] = jnp.zeros_like(dg_ref)
            db_ref[...] = jnp.zeros_like(db_ref)
        rr = r_ref[...]
        dh_ = dh_ref[...]
        mu = jnp.mean(rr, axis=-1, keepdims=True)
        xc = rr - mu
        var = jnp.mean(xc * xc, axis=-1, keepdims=True)
        rstd = lax.rsqrt(var + LN_EPS)
        xh = xc * rstd
        dxh = dh_ * g_ref[...]
        m1 = jnp.mean(dxh, axis=-1, keepdims=True)
        m2 = jnp.mean(dxh * xh, axis=-1, keepdims=True)
        dr_ref[...] = rstd * (dxh - m1 - xh * m2)
        dg_ref[...] += jnp.sum(dh_ * xh, axis=0, keepdims=True)
        db_ref[...] += jnp.sum(dh_, axis=0, keepdims=True)

    row = pl.BlockSpec((tm, d), lambda i: (i, 0))
    vec = pl.BlockSpec((1, d), lambda i: (0, 0))
    return pl.pallas_call(
        body, out_shape=(jax.ShapeDtypeStruct((s, d), F32), jax.ShapeDtypeStruct((1, d), F32),
                         jax.ShapeDtypeStruct((1, d), F32)),
        grid=(s // tm,), in_specs=[row, row, vec], out_specs=(row, vec, vec), name=name,
        compiler_params=_cparams(("arbitrary",)))(r, dh, g)


def _loss_head(y, target):
    s, d = y.shape
    tm = 512

    def body(y_ref, t_ref, dy_ref, l_ref):
        @pl.when(pl.program_id(0) == 0)
        def _():
            l_ref[...] = jnp.zeros_like(l_ref)
        e = y_ref[...] - t_ref[...]
        dy_ref[...] = e * (1.0 / d)
        part = 0.5 * jnp.sum(jnp.mean(e * e, axis=-1, keepdims=True), axis=0, keepdims=True)
        l_ref[...] += jnp.broadcast_to(part, l_ref.shape)

    row = pl.BlockSpec((tm, d), lambda i: (i, 0))
    return pl.pallas_call(
        body, out_shape=(jax.ShapeDtypeStruct((s, d), F32), jax.ShapeDtypeStruct((1, LANE), F32)),
        grid=(s // tm,), in_specs=[row, row], out_specs=(row, pl.BlockSpec((1, LANE), lambda i: (0, 0))),
        name="loss_head", compiler_params=_cparams(("arbitrary",)))(y, target)


def _rope_tables(s):
    half = HEAD // 2
    pos = jnp.arange(s, dtype=F32)
    inv = ROPE_THETA ** (-jnp.arange(half, dtype=F32) * 2.0 / HEAD)
    ang = pos[:, None] * inv[None, :]
    cos, sin = jnp.cos(ang), jnp.sin(ang)
    cos = jnp.concatenate([cos, cos, cos, cos], axis=1)
    sin = jnp.concatenate([-sin, sin, -sin, sin], axis=1)
    return cos, sin


def _rotate(x, cos, sin):
    lane = lax.broadcasted_iota(jnp.int32, x.shape, 1)
    partner = jnp.where((lane % HEAD) < HEAD // 2, pltpu.roll(x, LANE - HEAD // 2, axis=1),
                        pltpu.roll(x, HEAD // 2, axis=1))
    return x * cos + partner * sin


def _rope_fwd(proj, cos, sin, name):
    s = proj.shape[0]
    tm = 512
    w = 3 * ATTN_W

    def body(p_ref, c_ref, s_ref, o_ref):
        c, sn = c_ref[...], s_ref[...]
        for j in range(w // LANE):
            x = p_ref[:, j * LANE:(j + 1) * LANE]
            if j < 2 * ATTN_W // LANE:
                x = _rotate(x, c, sn)
            o_ref[:, j * LANE:(j + 1) * LANE] = x.astype(BF16)

    tab = pl.BlockSpec((tm, LANE), lambda i: (i, 0))
    return pl.pallas_call(
        body, out_shape=jax.ShapeDtypeStruct((s, w), BF16), grid=(s // tm,),
        in_specs=[pl.BlockSpec((tm, w), lambda i: (i, 0)), tab, tab],
        out_specs=pl.BlockSpec((tm, w), lambda i: (i, 0)), name=name,
        compiler_params=_cparams(("parallel",)))(proj, cos, sin)


def _dproj_assemble(dqkv_list, dxr, dgate, du, cos, sin, name):
    s = dxr.shape[0]
    tm = 512
    nq = 3 * ATTN_W // LANE

    def body(*refs):
        br = refs[:9]
        dxr_ref, dg_ref, du_ref, c_ref, s_ref, o_ref = refs[9:]
        c, sn = c_ref[...], -s_ref[...]
        for j in range(nq):
            part, jj = divmod(j, ATTN_W // LANE)
            x = br[part][:, jj * LANE:(jj + 1) * LANE] + br[3 + part][:, jj * LANE:(jj + 1) * LANE] \
                + br[6 + part][:, jj * LANE:(jj + 1) * LANE]
            if part < 2:
                x = _rotate(x, c, sn)
            o_ref[:, j * LANE:(j + 1) * LANE] = x.astype(BF16)
        o_ref[:, 3 * ATTN_W:3 * ATTN_W + LRU_W] = dxr_ref[...].astype(BF16)
        o_ref[:, 3 * ATTN_W + LRU_W:3 * ATTN_W + 2 * LRU_W] = dg_ref[...].astype(BF16)
        o_ref[:, 3 * ATTN_W + 2 * LRU_W:] = du_ref[...].astype(BF16)

    a_spec = pl.BlockSpec((tm, ATTN_W), lambda i: (i, 0))
    tab = pl.BlockSpec((tm, LANE), lambda i: (i, 0))
    ordered = [dqkv_list[b][p] for b in range(3) for p in range(3)]
    return pl.pallas_call(
        body, out_shape=jax.ShapeDtypeStruct((s, D_IN), BF16), grid=(s // tm,),
        in_specs=[a_spec] * 9 + [a_spec, a_spec, pl.BlockSpec((tm, S5_W), lambda i: (i, 0)), tab, tab],
        out_specs=pl.BlockSpec((tm, D_IN), lambda i: (i, 0)), name=name,
        compiler_params=_cparams(("parallel",)))(*ordered, dxr, dgate, du, cos, sin)


def _attn_specs(d):
    nq = 3 * ATTN_W // LANE
    na = ATTN_W // LANE
    q = pl.BlockSpec((ATTN_BLK, LANE), lambda c, h, b: (b, c * nq + h))
    ko = pl.BlockSpec((ATTN_BLK, LANE), lambda c, h, b: (b, c * nq + na + h))
    kp = pl.BlockSpec((ATTN_BLK, LANE), lambda c, h, b: (jnp.maximum(b - 1, 0), c * nq + na + h))
    vo = pl.BlockSpec((ATTN_BLK, LANE), lambda c, h, b: (b, c * nq + 2 * na + h))
    vp = pl.BlockSpec((ATTN_BLK, LANE), lambda c, h, b: (jnp.maximum(b - 1, 0), c * nq + 2 * na + h))
    o = pl.BlockSpec((ATTN_BLK, LANE), lambda c, h, b: (b, c * na + h))
    return q, ko, kp, vo, vp, o


def _band_mask(qb):
    qi = lax.broadcasted_iota(jnp.int32, (ATTN_BLK, 2 * ATTN_BLK), 0)
    ki = lax.broadcasted_iota(jnp.int32, (ATTN_BLK, 2 * ATTN_BLK), 1)
    dist = qi + ATTN_BLK - ki
    return (dist >= 0) & (dist <= ATTN_BLK) & ((ki >= ATTN_BLK) | (qb > 0))


def _attn_fwd(qkv, d, name):
    s = qkv.shape[0]
    m = s // d
    nb = m // ATTN_BLK
    qv = qkv.reshape(m, d * 3 * ATTN_W)

    def body(q_ref, ko_ref, kp_ref, vo_ref, vp_ref, o_ref, l_ref):
        valid = _band_mask(pl.program_id(2))
        for hh in range(LANE // HEAD):
            sl = slice(hh * HEAD, (hh + 1) * HEAD)
            q = q_ref[:, sl]
            k = jnp.concatenate([kp_ref[:, sl], ko_ref[:, sl]], axis=0)
            v = jnp.concatenate([vp_ref[:, sl], vo_ref[:, sl]], axis=0)
            sc = jnp.where(valid, _dot_nt(q, k) * (HEAD ** -0.5), -1e30)
            mx = jnp.max(sc, axis=-1, keepdims=True)
            p = jnp.exp(sc - mx)
            l = jnp.sum(p, axis=-1, keepdims=True)
            o_ref[:, sl] = _dot(p, v) / l
            l_ref[:, sl] = jnp.broadcast_to(mx + jnp.log(l), (ATTN_BLK, HEAD))

    q, ko, kp, vo, vp, o = _attn_specs(d)
    shp = jax.ShapeDtypeStruct((m, d * ATTN_W), F32)
    out, lse = pl.pallas_call(
        body, out_shape=(shp, shp), grid=(d, ATTN_W // LANE, nb),
        in_specs=[q, ko, kp, vo, vp], out_specs=(o, o), name=name,
        compiler_params=_cparams(("parallel", "parallel", "parallel")))(qv, qv, qv, qv, qv)
    return out.reshape(s, ATTN_W), lse.reshape(s, ATTN_W)


def _attn_bwd_dq(qkv, o_all, do_all, lse_all, d, name):
    s = qkv.shape[0]
    m = s // d
    nb = m // ATTN_BLK
    qv = qkv.reshape(m, d * 3 * ATTN_W)
    view = lambda t: t.reshape(m, d * ATTN_W)

    def body(q_ref, ko_ref, kp_ref, vo_ref, vp_ref, o_ref, do_ref, l_ref, dq_ref):
        valid = _band_mask(pl.program_id(2))
        for hh in range(LANE // HEAD):
            sl = slice(hh * HEAD, (hh + 1) * HEAD)
            q = q_ref[:, sl]
            k = jnp.concatenate([kp_ref[:, sl], ko_ref[:, sl]], axis=0)
            v = jnp.concatenate([vp_ref[:, sl], vo_ref[:, sl]], axis=0)
            sc = jnp.where(valid, _dot_nt(q, k) * (HEAD ** -0.5), -1e30)
            p = jnp.exp(sc - l_ref[:, sl][:, 0:1])
            do = do_ref[:, sl]
            delta = jnp.sum(do * o_ref[:, sl], axis=-1, keepdims=True)
            ds = p * (_dot_nt(do, v) - delta) * (HEAD ** -0.5)
            dq_ref[:, sl] = _dot(ds, k)

    q, ko, kp, vo, vp, o = _attn_specs(d)
    dq = pl.pallas_call(
        body, out_shape=jax.ShapeDtypeStruct((m, d * ATTN_W), F32), grid=(d, ATTN_W // LANE, nb),
        in_specs=[q, ko, kp, vo, vp, o, o, o], out_specs=o, name=name,
        compiler_params=_cparams(("parallel", "parallel", "parallel")))(
            qv, qv, qv, qv, qv, view(o_all), view(do_all), view(lse_all))
    return dq.reshape(s, ATTN_W)


def _attn_bwd_dkv(qkv, o_all, do_all, lse_all, d, name):
    s = qkv.shape[0]
    m = s // d
    nb = m // ATTN_BLK
    qv = qkv.reshape(m, d * 3 * ATTN_W)
    view = lambda t: t.reshape(m, d * ATTN_W)
    nq = 3 * ATTN_W // LANE
    na = ATTN_W // LANE

    def body(qo_ref, qn_ref, k_ref, v_ref, oo_ref, on_ref, doo_ref, don_ref, lo_ref, ln_ref, dk_ref, dv_ref):
        kb = pl.program_id(2)
        qi = lax.broadcasted_iota(jnp.int32, (ATTN_BLK, ATTN_BLK), 0)
        ki = lax.broadcasted_iota(jnp.int32, (ATTN_BLK, ATTN_BLK), 1)
        valid_own = qi >= ki
        valid_next = (qi <= ki) & (kb + 1 < nb)
        for hh in range(LANE // HEAD):
            sl = slice(hh * HEAD, (hh + 1) * HEAD)
            k = k_ref[:, sl]
            v = v_ref[:, sl]
            dk = jnp.zeros((ATTN_BLK, HEAD), F32)
            dv = jnp.zeros((ATTN_BLK, HEAD), F32)
            for q_ref, o_ref, do_ref, l_ref, valid in ((qo_ref, oo_ref, doo_ref, lo_ref, valid_own),
                                                       (qn_ref, on_ref, don_ref, ln_ref, valid_next)):
                q = q_ref[:, sl]
                sc = jnp.where(valid, _dot_nt(q, k) * (HEAD ** -0.5), -1e30)
                p = jnp.exp(sc - l_ref[:, sl][:, 0:1])
                do = do_ref[:, sl]
                delta = jnp.sum(do * o_ref[:, sl], axis=-1, keepdims=True)
                ds = p * (_dot_nt(do, v) - delta) * (HEAD ** -0.5)
                dv = dv + _dot(p.T, do)
                dk = dk + _dot(ds.T, q)
            dk_ref[:, sl] = dk
            dv_ref[:, sl] = dv

    nxt = lambda b: jnp.minimum(b + 1, nb - 1)
    blk = (ATTN_BLK, LANE)
    qo = pl.BlockSpec(blk, lambda c, h, b: (b, c * nq + h))
    qn = pl.BlockSpec(blk, lambda c, h, b: (nxt(b), c * nq + h))
    ks = pl.BlockSpec(blk, lambda c, h, b: (b, c * nq + na + h))
    vs = pl.BlockSpec(blk, lambda c, h, b: (b, c * nq + 2 * na + h))
    ao = pl.BlockSpec(blk, lambda c, h, b: (b, c * na + h))
    an = pl.BlockSpec(blk, lambda c, h, b: (nxt(b), c * na + h))
    shp = jax.ShapeDtypeStruct((m, d * ATTN_W), F32)
    ov, dov, lv = view(o_all), view(do_all), view(lse_all)
    dk, dv = pl.pallas_call(
        body, out_shape=(shp, shp), grid=(d, na, nb),
        in_specs=[qo, qn, ks, vs, ao, an, ao, an, ao, an], out_specs=(ao, ao), name=name,
        compiler_params=_cparams(("parallel", "parallel", "parallel")))(
            qv, qv, qv, qv, ov, ov, dov, dov, lv, lv)
    return dk.reshape(s, ATTN_W), dv.reshape(s, ATTN_W)


def _rms(x, g):
    ms = jnp.mean(x * x, axis=-1, keepdims=True)
    return x * lax.rsqrt(ms + RMS_EPS) * g


def _rms_bwd(x, g, dy):
    ms = jnp.mean(x * x, axis=-1, keepdims=True)
    r = lax.rsqrt(ms + RMS_EPS)
    dyg = dy * g
    dx = r * dyg - x * (r * r * r) * jnp.mean(x * dyg, axis=-1, keepdims=True)
    return dx, dy * x * r


def _mix_fwd(outs, lses, lru, s5, g, name):
    s = lru.shape[0]
    tm = 256

    def body(o1, o2, o3, l1, l2, l3, lru_ref, s5_ref, g_ref, mixed_ref, o_ref, lse_ref):
        a1, a2, a3 = l1[...], l2[...], l3[...]
        mx = jnp.maximum(jnp.maximum(a1, a2), a3)
        e1, e2, e3 = jnp.exp(a1 - mx), jnp.exp(a2 - mx), jnp.exp(a3 - mx)
        den = e1 + e2 + e3
        o = (e1 * o1[...] + e2 * o2[...] + e3 * o3[...]) / den
        o_ref[...] = o
        lse_ref[...] = mx + jnp.log(den)
        gg = g_ref[...]
        mixed_ref[:, :ATTN_W] = _rms(o, gg[:, :ATTN_W]).astype(BF16)
        mixed_ref[:, ATTN_W:ATTN_W + LRU_W] = _rms(lru_ref[...], gg[:, ATTN_W:ATTN_W + LRU_W]).astype(BF16)
        mixed_ref[:, ATTN_W + LRU_W:] = _rms(s5_ref[...], gg[:, ATTN_W + LRU_W:]).astype(BF16)

    a = pl.BlockSpec((tm, ATTN_W), lambda i: (i, 0))
    s5s = pl.BlockSpec((tm, S5_W), lambda i: (i, 0))
    full = pl.BlockSpec((tm, D_MODEL), lambda i: (i, 0))
    vec = pl.BlockSpec((1, D_MODEL), lambda i: (0, 0))
    return pl.pallas_call(
        body, out_shape=(jax.ShapeDtypeStruct((s, D_MODEL), BF16), jax.ShapeDtypeStruct((s, ATTN_W), F32),
                         jax.ShapeDtypeStruct((s, ATTN_W), F32)),
        grid=(s // tm,), in_specs=[a] * 6 + [a, s5s, vec], out_specs=(full, a, a), name=name,
        compiler_params=_cparams(("parallel",)))(*outs, *lses, lru, s5, g)


def _mix_bwd(dmixed, o, lru, s5, g, name):
    s = lru.shape[0]
    tm = 256

    def body(dm_ref, o_ref, lru_ref, s5_ref, g_ref, do_ref, dlru_ref, ds5_ref, dg_ref):
        @pl.when(pl.program_id(0) == 0)
        def _():
            dg_ref[...] = jnp.zeros_like(dg_ref)
        gg = g_ref[...]
        dm = dm_ref[...]
        dx, dgr = _rms_bwd(o_ref[...], gg[:, :ATTN_W], dm[:, :ATTN_W])
        do_ref[...] = dx
        dg_ref[:, :ATTN_W] += jnp.sum(dgr, axis=0, keepdims=True)
        dx, dgr = _rms_bwd(lru_ref[...], gg[:, ATTN_W:ATTN_W + LRU_W], dm[:, ATTN_W:ATTN_W + LRU_W])
        dlru_ref[...] = dx
        dg_ref[:, ATTN_W:ATTN_W + LRU_W] += jnp.sum(dgr, axis=0, keepdims=True)
        dx, dgr = _rms_bwd(s5_ref[...], gg[:, ATTN_W + LRU_W:], dm[:, ATTN_W + LRU_W:])
        ds5_ref[...] = dx
        dg_ref[:, ATTN_W + LRU_W:] += jnp.sum(dgr, axis=0, keepdims=True)

    a = pl.BlockSpec((tm, ATTN_W), lambda i: (i, 0))
    s5s = pl.BlockSpec((tm, S5_W), lambda i: (i, 0))
    full = pl.BlockSpec((tm, D_MODEL), lambda i: (i, 0))
    vec = pl.BlockSpec((1, D_MODEL), lambda i: (0, 0))
    return pl.pallas_call(
        body, out_shape=(jax.ShapeDtypeStruct((s, ATTN_W), F32), jax.ShapeDtypeStruct((s, LRU_W), F32),
                         jax.ShapeDtypeStruct((s, S5_W), F32), jax.ShapeDtypeStruct((1, D_MODEL), F32)),
        grid=(s // tm,), in_specs=[full, a, a, s5s, vec], out_specs=(a, a, s5s, vec), name=name,
        compiler_params=_cparams(("arbitrary",)))(dmixed, o, lru, s5, g)


def _lru_gate_math(xc, pre_r, pre_i, lam):
    r = _sigmoid(pre_r)
    i = _sigmoid(pre_i)
    log_a = -LRU_C * r * _softplus(-lam)
    a = jnp.exp(log_a)
    u = jnp.sqrt(-_expm1(2.0 * log_a)) * (i * xc)
    return a, u


def _lru_conv(x, prev8, cw, cb):
    y = cb + cw[LRU_CONV - 1:LRU_CONV, :] * x
    for k in range(LRU_CONV - 1):
        y = y + cw[k:k + 1, :] * _shift_down_prev(x, LRU_CONV - 1 - k, prev8)
    return y


def _lru_specs(s):
    xo = 3 * ATTN_W // LANE
    go = xo + LRU_W // LANE
    xr = pl.BlockSpec((s, LANE), lambda j: (0, xo + j))
    gt = pl.BlockSpec((s, LANE), lambda j: (0, go + j))
    cw = pl.BlockSpec((LRU_CONV, LANE), lambda j: (0, j))
    vec = pl.BlockSpec((1, LANE), lambda j: (0, j))
    wbd = pl.BlockSpec((LANE, LANE), lambda j: (j, j))
    col = pl.BlockSpec((s, LANE), lambda j: (0, j))
    return xr, gt, cw, vec, wbd, col


def _lru_fwd(proj, cw, cb, wr, br, wi, bi, lam, name):
    s = proj.shape[0]
    t = SCAN_T

    def body(xr_ref, gt_ref, cw_ref, cb_ref, wr_ref, br_ref, wi_ref, bi_ref, lam_ref, o_ref):
        cwv, cbv, lamv = cw_ref[...], cb_ref[...], lam_ref[...]
        wrv, wiv, brv, biv = wr_ref[...], wi_ref[...], br_ref[...], bi_ref[...]

        def chunk(c, carry):
            h_c, prev8 = carry
            rows = pl.ds(pl.multiple_of(c * t, t), t)
            x = xr_ref[rows, :]
            xc = _lru_conv(x, prev8, cwv, cbv)
            a, u = _lru_gate_math(xc, _dot(xc, wrv) + brv, _dot(xc, wiv) + biv, lamv)
            acum, hloc = _scan_fwd(a, u)
            h = hloc + acum * h_c
            o_ref[rows, :] = h * _gelu(gt_ref[rows, :])
            return h[t - 1:t, :], x[t - 8:t, :]

        lax.fori_loop(0, s // t, chunk, (jnp.zeros((1, LANE), F32), jnp.zeros((8, LANE), F32)))

    xr, gt, cws, vec, wbd, col = _lru_specs(s)
    return pl.pallas_call(
        body, out_shape=jax.ShapeDtypeStruct((s, LRU_W), F32), grid=(LRU_W // LANE,),
        in_specs=[xr, gt, cws, vec, wbd, vec, wbd, vec, vec], out_specs=col, name=name,
        compiler_params=_cparams(("parallel",)))(proj, proj, cw, cb, wr, br, wi, bi, lam)


def _lru_bwd(proj, dout, cw, cb, wr, br, wi, bi, lam, name):
    s = proj.shape[0]
    t = SCAN_T
    nc = s // t

    def body(xr_ref, gt_ref, do_ref, cw_ref, cb_ref, wr_ref, br_ref, wi_ref, bi_ref, lam_ref,
             dxr_ref, dgt_ref, dcw_ref, dcb_ref, dwr_ref, dbr_ref, dwi_ref, dbi_ref, dlam_ref,
             xc_s, a_s, h_s):
        cwv, cbv, lamv = cw_ref[...], cb_ref[...], lam_ref[...]
        wrv, wiv, brv, biv = wr_ref[...], wi_ref[...], br_ref[...], bi_ref[...]

        def fchunk(c, carry):
            h_c, prev8 = carry
            rows = pl.ds(pl.multiple_of(c * t, t), t)
            x = xr_ref[rows, :]
            xc = _lru_conv(x, prev8, cwv, cbv)
            a, u = _lru_gate_math(xc, _dot(xc, wrv) + brv, _dot(xc, wiv) + biv, lamv)
            acum, hloc = _scan_fwd(a, u)
            h = hloc + acum * h_c
            xc_s[rows, :] = xc
            a_s[rows, :] = a
            h_s[rows, :] = h
            return h[t - 1:t, :], x[t - 8:t, :]

        lax.fori_loop(0, nc, fchunk, (jnp.zeros((1, LANE), F32), jnp.zeros((8, LANE), F32)))

        z1 = jnp.zeros((1, LANE), F32)
        zw = jnp.zeros((LANE, LANE), F32)

        def bchunk(ci, carry):
            g_next, a_next, dxc_next8, dcw, dcb, dwr, dbr, dwi, dbi, dlam = carry
            c = nc - 1 - ci
            t0 = pl.multiple_of(c * t, t)
            rows = pl.ds(t0, t)
            before = pl.ds(pl.multiple_of(jnp.maximum(t0 - 8, 0), 8), 8)
            has_prev = (c > 0).astype(F32)
            x, gt, do = xr_ref[rows, :], gt_ref[rows, :], do_ref[rows, :]
            xc, a, h = xc_s[rows, :], a_s[rows, :], h_s[rows, :]
            prev8_x = xr_ref[before, :] * has_prev
            prev8_h = h_s[before, :] * has_prev
            dgt_ref[rows, :] = do * h * _gelu_grad(gt)
            dh = do * _gelu(gt)
            a_plus = _shift_up_next(a, 1, jnp.broadcast_to(a_next, (8, LANE)))
            acum, gloc = _scan_rev(a_plus, dh)
            g = gloc + acum * g_next
            da = g * _shift_down_prev(h, 1, prev8_h)
            pre_r = _dot(xc, wrv) + brv
            pre_i = _dot(xc, wiv) + biv
            _, vjp = jax.vjp(_lru_gate_math, xc, pre_r, pre_i, lamv)
            dxc, dpre_r, dpre_i, dlam_c = vjp((da, g))
            dxc = dxc + _dot_nt(dpre_r, wrv) + _dot_nt(dpre_i, wiv)
            dx = cwv[LRU_CONV - 1:LRU_CONV, :] * dxc
            dcw_rows = [None] * LRU_CONV
            dcw_rows[LRU_CONV - 1] = jnp.sum(dxc * x, axis=0, keepdims=True)
            for k in range(LRU_CONV - 1):
                sh = LRU_CONV - 1 - k
                dx = dx + cwv[k:k + 1, :] * _shift_up_next(dxc, sh, dxc_next8)
                dcw_rows[k] = jnp.sum(dxc * _shift_down_prev(x, sh, prev8_x), axis=0, keepdims=True)
            dxr_ref[rows, :] = dx
            return (g[0:1, :], a[0:1, :], dxc[0:8, :],
                    dcw + jnp.concatenate(dcw_rows, axis=0),
                    dcb + jnp.sum(dxc, axis=0, keepdims=True),
                    dwr + _dot_tn(xc, dpre_r), dbr + jnp.sum(dpre_r, axis=0, keepdims=True),
                    dwi + _dot_tn(xc, dpre_i), dbi + jnp.sum(dpre_i, axis=0, keepdims=True),
                    dlam + dlam_c)

        init = (z1, z1, jnp.zeros((8, LANE), F32), jnp.zeros((LRU_CONV, LANE), F32), z1, zw, z1, zw, z1, z1)
        res = lax.fori_loop(0, nc, bchunk, init)
        dcw_ref[...] = res[3]
        dcb_ref[...] = res[4]
        dwr_ref[...] = res[5]
        dbr_ref[...] = res[6]
        dwi_ref[...] = res[7]
        dbi_ref[...] = res[8]
        dlam_ref[...] = res[9]

    xr, gt, cws, vec, wbd, col = _lru_specs(s)
    vshape = jax.ShapeDtypeStruct((1, LRU_W), F32)
    wshape = jax.ShapeDtypeStruct((LRU_W, LRU_W), F32)
    return pl.pallas_call(
        body,
        out_shape=(jax.ShapeDtypeStruct((s, LRU_W), F32), jax.ShapeDtypeStruct((s, LRU_W), F32),
                   jax.ShapeDtypeStruct((LRU_CONV, LRU_W), F32), vshape, wshape, vshape, wshape, vshape, vshape),
        grid=(LRU_W // LANE,),
        in_specs=[xr, gt, col, cws, vec, wbd, vec, wbd, vec, vec],
        out_specs=(col, col, cws, vec, wbd, vec, wbd, vec, vec),
        scratch_shapes=[pltpu.VMEM((s, LANE), F32)] * 3, name=name,
        compiler_params=_cparams(("parallel",)))(proj, proj, dout, cw, cb, wr, br, wi, bi, lam)


def _s5_disc_math(a_re, a_im, log_step, bt_re, bt_im):
    step = jnp.exp(log_step)
    dt_re, dt_im = step * a_re, step * a_im
    mag = jnp.exp(dt_re)
    ab_re, ab_im = mag * jnp.cos(dt_im), mag * jnp.sin(dt_im)
    z_re, z_im = ab_re - 1.0, ab_im
    den = a_re * a_re + a_im * a_im
    f_re = (z_re * a_re + z_im * a_im) / den
    f_im = (z_im * a_re - z_re * a_im) / den
    bb_re = f_re * bt_re - f_im * bt_im
    bb_im = f_re * bt_im + f_im * bt_re
    return ab_re, ab_im, bb_re, bb_im


def _s5_disc_fwd(a_re, a_im, log_step, bt_re, bt_im, name):
    def body(ar, ai, ls, br, bi, o1, o2, o3, o4):
        r = _s5_disc_math(ar[...], ai[...], ls[...], br[...], bi[...])
        o1[...], o2[...], o3[...], o4[...] = r

    shp = jax.ShapeDtypeStruct(a_re.shape, F32)
    return pl.pallas_call(body, out_shape=(shp,) * 4, name=name)(a_re, a_im, log_step, bt_re, bt_im)


def _s5_disc_bwd(a_re, a_im, log_step, bt_re, bt_im, cts, name):
    def body(ar, ai, ls, br, bi, c1, c2, c3, c4, o1, o2, o3, o4, o5):
        _, vjp = jax.vjp(_s5_disc_math, ar[...], ai[...], ls[...], br[...], bi[...])
        r = vjp((c1[...], c2[...], c3[...], c4[...]))
        o1[...], o2[...], o3[...], o4[...], o5[...] = r

    shp = jax.ShapeDtypeStruct(a_re.shape, F32)
    return pl.pallas_call(body, out_shape=(shp,) * 5, name=name)(a_re, a_im, log_step, bt_re, bt_im, *cts)


def _s5_u_specs(s):
    uo = (3 * ATTN_W + 2 * LRU_W) // LANE
    return (pl.BlockSpec((s, LANE), lambda j: (0, uo)), pl.BlockSpec((s, LANE), lambda j: (0, uo + 1)))


def _s5_scan_fwd(proj, b_re, b_im, lam_re, lam_im, c_re, c_im, name):
    s = proj.shape[0]
    t = SCAN_T
    nlog = int(math.log2(t))

    def body(u0_ref, u1_ref, bre_ref, bim_ref, lre_ref, lim_ref, cre_ref, cim_ref, xre_ref, xim_ref, y_ref):
        @pl.when(pl.program_id(0) == 0)
        def _():
            y_ref[...] = jnp.zeros_like(y_ref)
        lr, li = lre_ref[...], lim_ref[...]
        pows = _cpowers(lr, li, nlog)
        first = _rows((t, LANE)) == 0
        tab_r, tab_i = _cscan(jnp.where(first, lr, 0.0), jnp.where(first, li, 0.0), pows, False)
        bre, bim, cre, cim = bre_ref[...], bim_ref[...], cre_ref[...], cim_ref[...]

        def chunk(c, carry):
            cr, ci = carry
            rows = pl.ds(pl.multiple_of(c * t, t), t)
            u = jnp.concatenate([u0_ref[rows, :], u1_ref[rows, :]], axis=1).astype(BF16)
            xr, xi = _cscan(_dot(u, bre), _dot(u, bim), pows, False)
            xr, xi = xr + tab_r * cr - tab_i * ci, xi + tab_r * ci + tab_i * cr
            xre_ref[rows, :] = xr
            xim_ref[rows, :] = xi
            y_ref[rows, :] += _dot(xr, cre) - _dot(xi, cim)
            return xr[t - 1:t, :], xi[t - 1:t, :]

        z = jnp.zeros((1, LANE), F32)
        lax.fori_loop(0, s // t, chunk, (z, z))

    u0, u1 = _s5_u_specs(s)
    bsp = pl.BlockSpec((S5_W, LANE), lambda j: (0, j))
    csp = pl.BlockSpec((LANE, S5_W), lambda j: (j, 0))
    vec = pl.BlockSpec((1, LANE), lambda j: (0, j))
    xsp = pl.BlockSpec((s, LANE), lambda j: (0, j))
    ysp = pl.BlockSpec((s, S5_W), lambda j: (0, 0))
    xshape = jax.ShapeDtypeStruct((s, S5_STATES), F32)
    return pl.pallas_call(
        body, out_shape=(xshape, xshape, jax.ShapeDtypeStruct((s, S5_W), F32)),
        grid=(S5_STATES // LANE,), in_specs=[u0, u1, bsp, bsp, vec, vec, csp, csp],
        out_specs=(xsp, xsp, ysp), name=name,
        compiler_params=_cparams(("arbitrary",)))(proj, proj, b_re, b_im, lam_re, lam_im, c_re, c_im)


def _s5_scan_bwd(proj, dy, du_init, x_re, x_im, b_re, b_im, lam_re, lam_im, c_re, c_im, name):
    s = proj.shape[0]
    t = SCAN_T
    nc = s // t
    nlog = int(math.log2(t))

    def body(u0_ref, u1_ref, dy_ref, dui_ref, xre_ref, xim_ref, bre_ref, bim_ref, lre_ref, lim_ref,
             cre_ref, cim_ref, du_ref, dlr_ref, dli_ref, dbr_ref, dbi_ref, dcr_ref, dci_ref):
        @pl.when(pl.program_id(0) == 0)
        def _():
            du_ref[...] = dui_ref[...]
        mr, mi = lre_ref[...], -lim_ref[...]
        pows = _cpowers(mr, mi, nlog)
        last = _rows((t, LANE)) == t - 1
        tab_r, tab_i = _cscan(jnp.where(last, mr, 0.0), jnp.where(last, mi, 0.0), pows, True)
        bre, bim, cre, cim = bre_ref[...], bim_ref[...], cre_ref[...], cim_ref[...]
        dbr_ref[...] = jnp.zeros_like(dbr_ref)
        dbi_ref[...] = jnp.zeros_like(dbi_ref)
        dcr_ref[...] = jnp.zeros_like(dcr_ref)
        dci_ref[...] = jnp.zeros_like(dci_ref)

        def chunk(ci_, carry):
            gnr, gni, dlr, dli = carry
            c = nc - 1 - ci_
            t0 = pl.multiple_of(c * t, t)
            rows = pl.ds(t0, t)
            before = pl.ds(pl.multiple_of(jnp.maximum(t0 - 8, 0), 8), 8)
            has_prev = (c > 0).astype(F32)
            dyc = dy_ref[rows, :].astype(BF16)
            u = jnp.concatenate([u0_ref[rows, :], u1_ref[rows, :]], axis=1).astype(BF16)
            gr, gi = _cscan(_dot_nt(dyc, cre), -_dot_nt(dyc, cim), pows, True)
            gr, gi = gr + tab_r * gnr - tab_i * gni, gi + tab_r * gni + tab_i * gnr
            xr, xi = xre_ref[rows, :], xim_ref[rows, :]
            xpr = _shift_down_prev(xr, 1, xre_ref[before, :] * has_prev)
            xpi = _shift_down_prev(xi, 1, xim_ref[before, :] * has_prev)
            dlr = dlr + jnp.sum(gr * xpr + gi * xpi, axis=0, keepdims=True)
            dli = dli + jnp.sum(gi * xpr - gr * xpi, axis=0, keepdims=True)
            du_ref[rows, :] += _dot_nt(gr, bre) + _dot_nt(gi, bim)
            dbr_ref[...] += _dot_tn(u, gr)
            dbi_ref[...] += _dot_tn(u, gi)
            dcr_ref[...] += _dot_tn(xr, dyc)
            dci_ref[...] -= _dot_tn(xi, dyc)
            return gr[0:1, :], gi[0:1, :], dlr, dli

        z = jnp.zeros((1, LANE), F32)
        res = lax.fori_loop(0, nc, chunk, (z, z, z, z))
        dlr_ref[...] = res[2]
        dli_ref[...] = res[3]

    u0, u1 = _s5_u_specs(s)
    bsp = pl.BlockSpec((S5_W, LANE), lambda j: (0, j))
    csp = pl.BlockSpec((LANE, S5_W), lambda j: (j, 0))
    vec = pl.BlockSpec((1, LANE), lambda j: (0, j))
    xsp = pl.BlockSpec((s, LANE), lambda j: (0, j))
    ysp = pl.BlockSpec((s, S5_W), lambda j: (0, 0))
    return pl.pallas_call(
        body,
        out_shape=(jax.ShapeDtypeStruct((s, S5_W), F32),
                   jax.ShapeDtypeStruct((1, S5_STATES), F32), jax.ShapeDtypeStruct((1, S5_STATES), F32),
                   jax.ShapeDtypeStruct((S5_W, S5_STATES), F32), jax.ShapeDtypeStruct((S5_W, S5_STATES), F32),
                   jax.ShapeDtypeStruct((S5_STATES, S5_W), F32), jax.ShapeDtypeStruct((S5_STATES, S5_W), F32)),
        grid=(S5_STATES // LANE,),
        in_specs=[u0, u1, ysp, ysp, xsp, xsp, bsp, bsp, vec, vec, csp, csp],
        out_specs=(ysp, vec, vec, bsp, bsp, csp, csp), name=name,
        compiler_params=_cparams(("arbitrary",)))(
            proj, proj, dy, du_init, x_re, x_im, b_re, b_im, lam_re, lam_im, c_re, c_im)


def _s5_out_fwd(proj, y_acc, dvec, w_glu, b_glu, name):
    s = proj.shape[0]
    tm = 512
    uo = (3 * ATTN_W + 2 * LRU_W) // LANE

    def body(u0_ref, u1_ref, y_ref, d_ref, w_ref, b_ref, o_ref, yp_ref):
        u = jnp.concatenate([u0_ref[...], u1_ref[...]], axis=1)
        y = y_ref[...] + d_ref[...] * u
        yp_ref[...] = y
        yg = _gelu(y)
        o_ref[...] = yg * _sigmoid(_dot(yg, w_ref[...]) + b_ref[...])

    u0 = pl.BlockSpec((tm, LANE), lambda i: (i, uo))
    u1 = pl.BlockSpec((tm, LANE), lambda i: (i, uo + 1))
    row = pl.BlockSpec((tm, S5_W), lambda i: (i, 0))
    vec = pl.BlockSpec((1, S5_W), lambda i: (0, 0))
    wsp = pl.BlockSpec((S5_W, S5_W), lambda i: (0, 0))
    shp = jax.ShapeDtypeStruct((s, S5_W), F32)
    return pl.pallas_call(
        body, out_shape=(shp, shp), grid=(s // tm,), in_specs=[u0, u1, row, vec, wsp, vec],
        out_specs=(row, row), name=name,
        compiler_params=_cparams(("parallel",)))(proj, proj, y_acc, dvec, w_glu, b_glu)


def _s5_out_bwd(proj, y_pre, dout, dvec, w_glu, b_glu, name):
    s = proj.shape[0]
    tm = 512
    uo = (3 * ATTN_W + 2 * LRU_W) // LANE

    def body(u0_ref, u1_ref, y_ref, do_ref, d_ref, w_ref, b_ref, dy_ref, dud_ref, dd_ref, dw_ref, db_ref):
        @pl.when(pl.program_id(0) == 0)
        def _():
            dd_ref[...] = jnp.zeros_like(dd_ref)
            dw_ref[...] = jnp.zeros_like(dw_ref)
            db_ref[...] = jnp.zeros_like(db_ref)
        u = jnp.concatenate([u0_ref[...], u1_ref[...]], axis=1)
        y = y_ref[...]
        do = do_ref[...]
        yg = _gelu(y)
        sg = _sigmoid(_dot(yg, w_ref[...]) + b_ref[...])
        dz = do * yg * sg * (1.0 - sg)
        dyg = do * sg + _dot_nt(dz, w_ref[...])
        dy = dyg * _gelu_grad(y)
        dy_ref[...] = dy
        dud_ref[...] = d_ref[...] * dy
        dd_ref[...] += jnp.sum(dy * u, axis=0, keepdims=True)
        dw_ref[...] += _dot_tn(yg, dz)
        db_ref[...] += jnp.sum(dz, axis=0, keepdims=True)

    u0 = pl.BlockSpec((tm, LANE), lambda i: (i, uo))
    u1 = pl.BlockSpec((tm, LANE), lambda i: (i, uo + 1))
    row = pl.BlockSpec((tm, S5_W), lambda i: (i, 0))
    vec = pl.BlockSpec((1, S5_W), lambda i: (0, 0))
    wsp = pl.BlockSpec((S5_W, S5_W), lambda i: (0, 0))
    shp = jax.ShapeDtypeStruct((s, S5_W), F32)
    vshape = jax.ShapeDtypeStruct((1, S5_W), F32)
    return pl.pallas_call(
        body, out_shape=(shp, shp, vshape, jax.ShapeDtypeStruct((S5_W, S5_W), F32), vshape),
        grid=(s // tm,), in_specs=[u0, u1, row, row, vec, wsp, vec],
        out_specs=(row, row, vec, wsp, vec), name=name,
        compiler_params=_cparams(("arbitrary",)))(proj, proj, y_pre, dout, dvec, w_glu, b_glu)


def _ffn_conv(x, prev8, cw, cb):
    y = cb + cw[FFN_CONV - 1:FFN_CONV, :] * x
    for k in range(FFN_CONV - 1):
        y = y + cw[k:k + 1, :] * _shift_down_prev(x, FFN_CONV - 1 - k, prev8)
    return y


def _ffn_act_fwd(up, cw, cb, name):
    s = up.shape[0]
    tm = 256
    tb = 2 * FFN_CB

    def body(x_ref, p_ref, cw_ref, cb_ref, o_ref):
        prev8 = p_ref[...] * (pl.program_id(1) > 0).astype(F32)
        y = _ffn_conv(x_ref[...], prev8, cw_ref[...], cb_ref[...])
        o_ref[...] = (_gelu(y[:, :FFN_CB]) * y[:, FFN_CB:]).astype(BF16)

    main = pl.BlockSpec((tm, tb), lambda j, i: (i, j))
    prev = pl.BlockSpec((8, tb), lambda j, i: (jnp.maximum(i * (tm // 8) - 1, 0), j))
    return pl.pallas_call(
        body, out_shape=jax.ShapeDtypeStruct((s, D_FF), BF16), grid=(D_FF // FFN_CB, s // tm),
        in_specs=[main, prev, pl.BlockSpec((FFN_CONV, tb), lambda j, i: (0, j)),
                  pl.BlockSpec((1, tb), lambda j, i: (0, j))],
        out_specs=pl.BlockSpec((tm, FFN_CB), lambda j, i: (i, j)), name=name,
        compiler_params=_cparams(("parallel", "parallel")))(up, up, cw, cb)


def _ffn_act_bwd(up, dact, cw, cb, name):
    s = up.shape[0]
    tm = 256
    tb = 2 * FFN_CB
    nr = s // tm

    def body(x_ref, p_ref, n_ref, da_ref, dan_ref, cw_ref, cb_ref, dup_ref, dcw_ref, dcb_ref):
        i = pl.program_id(1)

        @pl.when(i == 0)
        def _():
            dcw_ref[...] = jnp.zeros_like(dcw_ref)
            dcb_ref[...] = jnp.zeros_like(dcb_ref)
        has_next = (i < nr - 1).astype(F32)
        prev8 = p_ref[...] * (i > 0).astype(F32)
        cwv = cw_ref[...]
        x = x_ref[...]
        xe = jnp.concatenate([x, n_ref[...]], axis=0)
        dae = jnp.concatenate([da_ref[...], dan_ref[...] * has_next], axis=0)
        y = _ffn_conv(xe, prev8, cwv, cb_ref[...])
        gate, val = y[:, :FFN_CB], y[:, FFN_CB:]
        dy = jnp.concatenate([dae * val * _gelu_grad(gate), dae * _gelu(gate)], axis=1)
        dx = cwv[FFN_CONV - 1:FFN_CONV, :] * dy[:tm, :]
        dym = dy[:tm, :]
        dcw_rows = [None] * FFN_CONV
        dcw_rows[FFN_CONV - 1] = jnp.sum(dym * x, axis=0, keepdims=True)
        for k in range(FFN_CONV - 1):
            sh = FFN_CONV - 1 - k
            dx = dx + cwv[k:k + 1, :] * pltpu.roll(dy, tm + 8 - sh, axis=0)[:tm, :]
            dcw_rows[k] = jnp.sum(dym * _shift_down_prev(x, sh, prev8), axis=0, keepdims=True)
        dup_ref[...] = dx.astype(BF16)
        dcw_ref[...] += jnp.concatenate(dcw_rows, axis=0)
        dcb_ref[...] += jnp.sum(dym, axis=0, keepdims=True)

    main = pl.BlockSpec((tm, tb), lambda j, i: (i, j))
    prev = pl.BlockSpec((8, tb), lambda j, i: (jnp.maximum(i * (tm // 8) - 1, 0), j))
    nxt = pl.BlockSpec((8, tb), lambda j, i: (jnp.minimum((i + 1) * (tm // 8), s // 8 - 1), j))
    da = pl.BlockSpec((tm, FFN_CB), lambda j, i: (i, j))
    dan = pl.BlockSpec((8, FFN_CB), lambda j, i: (jnp.minimum((i + 1) * (tm // 8), s // 8 - 1), j))
    cws = pl.BlockSpec((FFN_CONV, tb), lambda j, i: (0, j))
    cbs = pl.BlockSpec((1, tb), lambda j, i: (0, j))
    return pl.pallas_call(
        body, out_shape=(jax.ShapeDtypeStruct((s, 2 * D_FF), BF16),
                         jax.ShapeDtypeStruct((FFN_CONV, 2 * D_FF), F32),
                         jax.ShapeDtypeStruct((1, 2 * D_FF), F32)),
        grid=(D_FF // FFN_CB, nr), in_specs=[main, prev, nxt, da, dan, cws, cbs],
        out_specs=(main, cws, cbs), name=name,
        compiler_params=_cparams(("parallel", "arbitrary")))(up, up, up, dact, dact, cw, cb)


def _adamw(w, g, m, v):
    r, c = w.shape
    tm = 8
    for cand in (512, 256, 128, 64, 32, 16):
        if r % cand == 0:
            tm = cand
            break

    def body(w_ref, g_ref, m_ref, v_ref, d_ref, mo_ref, vo_ref):
        gg = g_ref[...]
        mn = ADAM_B1 * m_ref[...] + (1.0 - ADAM_B1) * gg
        vn = ADAM_B2 * v_ref[...] + (1.0 - ADAM_B2) * (gg * gg)
        m_hat = mn / (1.0 - ADAM_B1 ** ADAM_STEP)
        v_hat = vn / (1.0 - ADAM_B2 ** ADAM_STEP)
        d_ref[...] = -ADAM_LR * (m_hat / (jnp.sqrt(v_hat) + ADAM_EPS) + ADAM_WD * w_ref[...])
        mo_ref[...] = mn
        vo_ref[...] = vn

    row = pl.BlockSpec((tm, c), lambda i: (i, 0))
    shp = jax.ShapeDtypeStruct((r, c), F32)
    return pl.pallas_call(
        body, out_shape=(shp, shp, shp), grid=(r // tm,), in_specs=[row] * 4, out_specs=(row,) * 3,
        name="adamw", compiler_params=_cparams(("parallel",)))(w, g, m, v)


def _sum_rows(buf, name):
    _, r, c = buf.shape
    tm = 8
    for cand in (256, 128, 64, 32, 16):
        if r % cand == 0:
            tm = cand
            break

    def body(b_ref, o_ref):
        acc = b_ref[0]
        for k in range(1, N_DEV):
            acc = acc + b_ref[k]
        o_ref[...] = acc

    return pl.pallas_call(
        body, out_shape=jax.ShapeDtypeStruct((r, c), F32), grid=(r // tm,),
        in_specs=[pl.BlockSpec((N_DEV, tm, c), lambda i: (0, i, 0))],
        out_specs=pl.BlockSpec((tm, c), lambda i: (i, 0)), name=name,
        compiler_params=_cparams(("parallel",)))(buf)


def _all_gather(shard, name):
    r, c_ = shard.shape

    def body(x_ref, out_ref, send_sems, recv_sems, local_sem):
        x, y, c = lax.axis_index("x"), lax.axis_index("y"), lax.axis_index("c")
        me, sibling = (x, y, c), (x, y, 1 - c)
        chips = [(1 - x, y), (x, 1 - y), (1 - x, 1 - y)]

        def rows(px, py, pc):
            return out_ref.at[4 * px + 2 * py + pc]

        def copy(k, block, to, src=None):
            return pltpu.make_async_remote_copy(
                src_ref=rows(*block) if src is None else src, dst_ref=rows(*block),
                send_sem=send_sems.at[k], recv_sem=recv_sems.at[k],
                device_id=to, device_id_type=pl.DeviceIdType.MESH)

        mine = pltpu.make_async_copy(x_ref, rows(*me), local_sem)
        mine.start()
        first = [copy(0, me, sibling, src=x_ref)]
        first += [copy(1 + j, me, (*chip, c), src=x_ref) for j, chip in enumerate(chips)]
        for cp in first:
            cp.start()
        passed = [copy(4 + j, (*chip, c), sibling) for j, chip in enumerate(chips)]
        for j, chip in enumerate(chips):
            copy(1 + j, (*chip, c), me).wait_recv()
            passed[j].start()
        copy(0, sibling, me).wait_recv()
        for j, chip in enumerate(chips):
            copy(4 + j, (*chip, 1 - c), me).wait_recv()
        for cp in first + passed:
            cp.wait_send()
        mine.wait()

    return pl.pallas_call(
        body, out_shape=jax.ShapeDtypeStruct((N_DEV, r, c_), shard.dtype),
        in_specs=[pl.BlockSpec(memory_space=pl.ANY)], out_specs=pl.BlockSpec(memory_space=pl.ANY),
        scratch_shapes=[pltpu.SemaphoreType.DMA((7,)), pltpu.SemaphoreType.DMA((7,)), pltpu.SemaphoreType.DMA],
        name=name)(shard)


def _all_to_all(buf, name):
    def body(b_ref, out_ref, send_sems, recv_sems, local_sem):
        x, y, c = lax.axis_index("x"), lax.axis_index("y"), lax.axis_index("c")
        me = 4 * x + 2 * y + c
        mine = pltpu.make_async_copy(b_ref.at[me], out_ref.at[me], local_sem)
        mine.start()
        copies = []
        for k in range(1, N_DEV):
            px = x ^ ((k >> 2) & 1)
            py = y ^ ((k >> 1) & 1)
            pc = c ^ (k & 1)
            cp = pltpu.make_async_remote_copy(
                src_ref=b_ref.at[4 * px + 2 * py + pc], dst_ref=out_ref.at[me],
                send_sem=send_sems.at[k - 1], recv_sem=recv_sems.at[k - 1],
                device_id=(px, py, pc), device_id_type=pl.DeviceIdType.MESH)
            cp.start()
            copies.append(cp)
        for cp in copies:
            cp.wait()
        mine.wait()

    return pl.pallas_call(
        body, out_shape=jax.ShapeDtypeStruct(buf.shape, buf.dtype),
        in_specs=[pl.BlockSpec(memory_space=pl.ANY)], out_specs=pl.BlockSpec(memory_space=pl.ANY),
        scratch_shapes=[pltpu.SemaphoreType.DMA((7,)), pltpu.SemaphoreType.DMA((7,)), pltpu.SemaphoreType.DMA],
        name=name)(buf)


def _block_diag(w):
    h, a, b = w.shape
    eye = jnp.eye(h, dtype=w.dtype)
    return (w[:, :, None, :] * eye[:, None, :, None]).reshape(h * a, h * b)


def _block_diag_extract(m, h):
    a, b = m.shape[0] // h, m.shape[1] // h
    idx = jnp.arange(h)
    return m.reshape(h, a, h, b)[idx, :, idx, :]


def _ffn_interleave(w):
    lead = w.shape[:-1]
    nb = D_FF // FFN_CB
    return jnp.swapaxes(w.reshape(*lead, 2, nb, FFN_CB), -3, -2).reshape(*lead, 2 * D_FF)


def _ffn_deinterleave(w):
    lead = w.shape[:-1]
    nb = D_FF // FFN_CB
    return jnp.swapaxes(w.reshape(*lead, nb, 2, FFN_CB), -3, -2).reshape(*lead, 2 * D_FF)


def _gather_full(gathered, local_shape, axis):
    t = jnp.moveaxis(gathered, 0, axis)
    shape = list(local_shape)
    shape[axis] *= N_DEV
    return t.reshape(shape)


def _scatter_blocks(full, axis):
    shape = list(full.shape)
    shape[axis:axis + 1] = [N_DEV, shape[axis] // N_DEV]
    return jnp.moveaxis(full.reshape(shape), axis, 0).reshape(N_DEV, -1)


def _pad_rows(flat, mult):
    n = flat.shape[-1]
    pad = (-n) % mult
    if pad:
        flat = jnp.concatenate([flat, jnp.zeros(flat.shape[:-1] + (pad,), flat.dtype)], axis=-1)
    return flat


def _layer_fwd(h_in, w, cos, sin, l):
    tag = "l%d_" % l
    proj = _mm_nn(h_in, w['w_in'], 512, D_IN, tag + "proj")
    qkv = _rope_fwd(proj, cos, sin, tag + "rope")
    outs, lses = [], []
    for d in DILATIONS:
        o, ls = _attn_fwd(qkv, d, tag + "attn_d%d" % d)
        outs.append(o)
        lses.append(ls)
    lru = _lru_fwd(proj, w['lru_conv_w'], w['lru_conv_b'], w['lru_wr'], w['lru_br'], w['lru_wi'],
                   w['lru_bi'], w['lru_lambda'], tag + "lru")
    x_re, x_im, y_acc = _s5_scan_fwd(proj, w['s5_bb_re'], w['s5_bb_im'], w['s5_lam_re'], w['s5_lam_im'],
                                     w['s5_cc_re'], w['s5_cc_im'], tag + "s5_scan")
    s5, y_pre = _s5_out_fwd(proj, y_acc, w['s5_d'], w['s5_w_glu'], w['s5_b_glu'], tag + "s5_out")
    mixed, attn_o, attn_lse = _mix_fwd(outs, lses, lru, s5, w['mix_norm_g'], tag + "mix")
    mixo = _mm_nn(mixed, w['w_out'], 512, D_MODEL, tag + "out_proj")
    r1, h1 = _ln_fwd(h_in, mixo, w['ln1_g'], w['ln1_b'], tag + "ln1")
    up = _mm_nn(h1, w['w_up'], 512, 1536, tag + "up_proj")
    act = _ffn_act_fwd(up, w['ffn_conv_w'], w['ffn_conv_b'], tag + "ffn_act")
    ffn = _mm_nn(act, w['w_down'], 512, D_MODEL, tag + "down_proj")
    r2, h2 = _ln_fwd(h1, ffn, w['ln2_g'], w['ln2_b'], tag + "ln2")
    saved = dict(h_in=h_in, proj=proj, qkv=qkv, lru=lru, x_re=x_re, x_im=x_im, y_pre=y_pre, s5=s5,
                 mixed=mixed, attn_o=attn_o, attn_lse=attn_lse, r1=r1, h1=h1, up=up, act=act, r2=r2)
    return h2, saved


def _layer_bwd(dh2, sv, w, cos, sin, l):
    tag = "l%d_" % l
    g = {}
    dr2, g['ln2_g'], g['ln2_b'] = _ln_bwd(sv['r2'], dh2, w['ln2_g'], tag + "ln2_bwd")
    g['w_down'] = _mm_tn(sv['act'], dr2, 1024, D_MODEL, 512, tag + "down_dw")
    dact = _mm_nt(dr2, w['w_down'], 512, D_FF, tag + "down_dx")
    dup, g['ffn_conv_w'], g['ffn_conv_b'] = _ffn_act_bwd(sv['up'], dact, w['ffn_conv_w'], w['ffn_conv_b'],
                                                        tag + "ffn_act_bwd")
    g['w_up'] = _mm_tn(sv['h1'], dup, 1024, 1536, 512, tag + "up_dw")
    dh1 = _mm_nt(dup, w['w_up'], 256, D_MODEL, tag + "up_dx", add=dr2, add_scale=ALPHA)
    dr1, g['ln1_g'], g['ln1_b'] = _ln_bwd(sv['r1'], dh1, w['ln1_g'], tag + "ln1_bwd")
    g['w_out'] = _mm_tn(sv['mixed'], dr1, 1024, D_MODEL, 512, tag + "out_dw")
    dmixed = _mm_nt(dr1, w['w_out'], 512, D_MODEL, tag + "out_dx")
    d_o, dlru, ds5, g['mix_norm_g'] = _mix_bwd(dmixed, sv['attn_o'], sv['lru'], sv['s5'], w['mix_norm_g'],
                                               tag + "mix_bwd")
    dy, dud, g['s5_d'], g['s5_w_glu'], g['s5_b_glu'] = _s5_out_bwd(
        sv['proj'], sv['y_pre'], ds5, w['s5_d'], w['s5_w_glu'], w['s5_b_glu'], tag + "s5_out_bwd")
    du, g['s5_lam_re'], g['s5_lam_im'], g['s5_bb_re'], g['s5_bb_im'], g['s5_cc_re'], g['s5_cc_im'] = \
        _s5_scan_bwd(sv['proj'], dy, dud, sv['x_re'], sv['x_im'], w['s5_bb_re'], w['s5_bb_im'],
                     w['s5_lam_re'], w['s5_lam_im'], w['s5_cc_re'], w['s5_cc_im'], tag + "s5_scan_bwd")
    (dxr, dgate, g['lru_conv_w'], g['lru_conv_b'], g['lru_wr'], g['lru_br'], g['lru_wi'], g['lru_bi'],
     g['lru_lambda']) = _lru_bwd(sv['proj'], dlru, w['lru_conv_w'], w['lru_conv_b'], w['lru_wr'],
                                 w['lru_br'], w['lru_wi'], w['lru_bi'], w['lru_lambda'], tag + "lru_bwd")
    dqkv = []
    for d in DILATIONS:
        dq = _attn_bwd_dq(sv['qkv'], sv['attn_o'], d_o, sv['attn_lse'], d, tag + "attn_dq_d%d" % d)
        dk, dv = _attn_bwd_dkv(sv['qkv'], sv['attn_o'], d_o, sv['attn_lse'], d, tag + "attn_dkv_d%d" % d)
        dqkv.append((dq, dk, dv))
    dproj = _dproj_assemble(dqkv, dxr, dgate, du, cos, sin, tag + "dproj")
    g['w_in'] = _mm_tn(sv['h_in'], dproj, 1024, D_IN, 512, tag + "in_dw")
    dh_in = _mm_nt(dproj, w['w_in'], 512, D_MODEL, tag + "in_dx", add=dr1, add_scale=ALPHA)
    return dh_in, g


def _s5_rep(a):
    return jnp.repeat(a, S5_C, axis=0)


def _prepare_layer(full, l):
    w = {}
    w['w_in'] = full['w_in'][l].astype(BF16)
    w['w_out'] = full['w_out'][l].astype(BF16)
    w['w_up'] = _ffn_interleave(full['w_up'][l]).astype(BF16)
    w['w_down'] = full['w_down'][l].astype(BF16)
    w['s5_w_glu'] = full['s5_w_glu'][l].astype(BF16)
    w['ffn_conv_w'] = _ffn_interleave(full['ffn_conv_w'][l])
    w['ffn_conv_b'] = _ffn_interleave(full['ffn_conv_b'][l])[None, :]
    w['lru_conv_w'] = full['lru_conv_w'][l]
    for n in ('lru_conv_b', 'lru_br', 'lru_bi', 'lru_lambda', 's5_b_glu', 'mix_norm_g',
              'ln1_g', 'ln1_b', 'ln2_g', 'ln2_b'):
        w[n] = full[n][l][None, :]
    w['lru_wr'] = _block_diag(full['lru_wr'][l]).astype(BF16)
    w['lru_wi'] = _block_diag(full['lru_wi'][l]).astype(BF16)
    w['s5_d'] = full['s5_d'][l].reshape(1, S5_W)
    disc_in = (_s5_rep(full['s5_a_re'][l]), _s5_rep(full['s5_a_im'][l]),
               _s5_rep(jnp.broadcast_to(full['s5_log_step'][l][:, None], (S5_G, S5_P))),
               jnp.swapaxes(full['s5_b_re'][l], 1, 2).reshape(S5_W, S5_P),
               jnp.swapaxes(full['s5_b_im'][l], 1, 2).reshape(S5_W, S5_P))
    ab_re, ab_im, bb_re, bb_im = _s5_disc_fwd(*disc_in, "l%d_s5_disc" % l)
    w['s5_disc_in'] = disc_in
    w['s5_lam_re'] = ab_re[::S5_C].reshape(1, S5_STATES)
    w['s5_lam_im'] = ab_im[::S5_C].reshape(1, S5_STATES)
    w['s5_bb_re'] = _block_diag(bb_re.reshape(S5_G, S5_C, S5_P)).astype(BF16)
    w['s5_bb_im'] = _block_diag(bb_im.reshape(S5_G, S5_C, S5_P)).astype(BF16)
    w['s5_cc_re'] = _block_diag(jnp.swapaxes(full['s5_c_re'][l], 1, 2)).astype(BF16)
    w['s5_cc_im'] = _block_diag(jnp.swapaxes(full['s5_c_im'][l], 1, 2)).astype(BF16)
    return w


def _finish_layer_grads(g, w, l):
    out = {}
    for n in ('w_in', 'w_out', 'w_down', 's5_w_glu', 'lru_conv_w'):
        out[n] = g[n]
    out['w_up'] = _ffn_deinterleave(g['w_up'])
    out['ffn_conv_w'] = _ffn_deinterleave(g['ffn_conv_w'])
    out['ffn_conv_b'] = _ffn_deinterleave(g['ffn_conv_b'])[0]
    for n in ('lru_conv_b', 'lru_br', 'lru_bi', 'lru_lambda', 's5_b_glu', 'mix_norm_g',
              'ln1_g', 'ln1_b', 'ln2_g', 'ln2_b'):
        out[n] = g[n][0]
    out['lru_wr'] = _block_diag_extract(g['lru_wr'], LRU_W // HEAD)
    out['lru_wi'] = _block_diag_extract(g['lru_wi'], LRU_W // HEAD)
    out['s5_d'] = g['s5_d'].reshape(S5_G, S5_C)
    out['s5_c_re'] = jnp.swapaxes(_block_diag_extract(g['s5_cc_re'], S5_G), 1, 2)
    out['s5_c_im'] = jnp.swapaxes(_block_diag_extract(g['s5_cc_im'], S5_G), 1, 2)
    rep = lambda v: _s5_rep(v.reshape(S5_G, S5_P)) * (1.0 / S5_C)
    cts = (rep(g['s5_lam_re']), rep(g['s5_lam_im']),
           _block_diag_extract(g['s5_bb_re'], S5_G).reshape(S5_W, S5_P),
           _block_diag_extract(g['s5_bb_im'], S5_G).reshape(S5_W, S5_P))
    da_re, da_im, dls, dbt_re, dbt_im = _s5_disc_bwd(*w['s5_disc_in'], cts, "l%d_s5_disc_bwd" % l)
    out['s5_a_re'] = da_re.reshape(S5_G, S5_C, S5_P).sum(axis=1)
    out['s5_a_im'] = da_im.reshape(S5_G, S5_C, S5_P).sum(axis=1)
    out['s5_log_step'] = dls.reshape(S5_G, S5_C * S5_P).sum(axis=1)
    out['s5_b_re'] = jnp.swapaxes(dbt_re.reshape(S5_G, S5_C, S5_P), 1, 2)
    out['s5_b_im'] = jnp.swapaxes(dbt_im.reshape(S5_G, S5_C, S5_P), 1, 2)
    return out


def _local_step(x, target, full):
    s = x.shape[0]
    cos, sin = _rope_tables(s)
    ws = [_prepare_layer(full, l) for l in range(DEPTH)]
    h = x
    saved = []
    for l in range(DEPTH):
        h, sv = _layer_fwd(h, ws[l], cos, sin, l)
        saved.append(sv)
    dh, loss_vec = _loss_head(h, target)
    grads = [None] * DEPTH
    for l in reversed(range(DEPTH)):
        dh, g = _layer_bwd(dh, saved[l], ws[l], cos, sin, l)
        grads[l] = _finish_layer_grads(g, ws[l], l)
    stacked = {n: jnp.stack([grads[l][n] for l in range(DEPTH)], axis=0) for n in WEIGHTS}
    return loss_vec[0, 0], dh, stacked


def _f32_as_bf16_pairs(a):
    return lax.bitcast_convert_type(a, BF16).reshape(-1)


def _bf16_pairs_as_f32(a, shape):
    return lax.bitcast_convert_type(a.reshape(-1, 2), F32).reshape(shape)


def _gather_weights(local):
    pieces, sizes = [], []
    for n in SHARDED:
        p = _f32_as_bf16_pairs(local[n]) if n in F32_PAYLOAD else local[n].astype(BF16).reshape(-1)
        pieces.append(p)
        sizes.append(p.shape[0])
    flat = _pad_rows(jnp.concatenate(pieces), 16 * 1024).reshape(-1, 1024)
    gathered = _all_gather(flat, "gather_weights").reshape(N_DEV, -1)
    full, off = {}, 0
    for n, sz in zip(SHARDED, sizes):
        part = gathered[:, off:off + sz]
        off += sz
        lshape = local[n].shape
        if n in F32_PAYLOAD:
            part = _bf16_pairs_as_f32(part, (N_DEV,) + lshape)
        else:
            part = part.reshape((N_DEV,) + lshape)
        full[n] = _gather_full(part, lshape, SHARD_AXIS[n])
    return full


def kernel(x, w_in, lru_conv_w, lru_conv_b, lru_wr, lru_br, lru_wi, lru_bi, lru_lambda, s5_a_re, s5_a_im, s5_b_re, s5_b_im, s5_c_re, s5_c_im, s5_d, s5_log_step, s5_w_glu, s5_b_glu, mix_norm_g, w_out, ln1_g, ln1_b, w_up, ffn_conv_w, ffn_conv_b, w_down, ln2_g, ln2_b, loss_target, m_w_in, m_lru_conv_w, m_lru_conv_b, m_lru_wr, m_lru_br, m_lru_wi, m_lru_bi, m_lru_lambda, m_s5_a_re, m_s5_a_im, m_s5_b_re, m_s5_b_im, m_s5_c_re, m_s5_c_im, m_s5_d, m_s5_log_step, m_s5_w_glu, m_s5_b_glu, m_mix_norm_g, m_w_out, m_ln1_g, m_ln1_b, m_w_up, m_ffn_conv_w, m_ffn_conv_b, m_w_down, m_ln2_g, m_ln2_b, v_w_in, v_lru_conv_w, v_lru_conv_b, v_lru_wr, v_lru_br, v_lru_wi, v_lru_bi, v_lru_lambda, v_s5_a_re, v_s5_a_im, v_s5_b_re, v_s5_b_im, v_s5_c_re, v_s5_c_im, v_s5_d, v_s5_log_step, v_s5_w_glu, v_s5_b_glu, v_mix_norm_g, v_w_out, v_ln1_g, v_ln1_b, v_w_up, v_ffn_conv_w, v_ffn_conv_b, v_w_down, v_ln2_g, v_ln2_b):
    args = locals()
    wl = {n: args[n] for n in WEIGHTS}
    ml = {n: args['m_' + n] for n in WEIGHTS}
    vl = {n: args['v_' + n] for n in WEIGHTS}
    me = 4 * lax.axis_index("x") + 2 * lax.axis_index("y") + lax.axis_index("c")

    full = _gather_weights(wl)
    for n in REPLICATED:
        full[n] = wl[n]

    loss_local, grad_x, partial = _local_step(x[0], loss_target[0], full)
    loss = lax.psum(loss_local, AXES)

    rep_sizes = [int(wl[n].size) for n in REPLICATED]
    rep_total = sum(rep_sizes)
    rep_per = -(-rep_total // (N_DEV * 1024)) * 1024

    def rep_flat(tree):
        flat = jnp.concatenate([tree[n].reshape(-1) for n in REPLICATED])
        return _pad_rows(flat, N_DEV * rep_per)

    sh_sizes = [int(wl[n].size) for n in SHARDED]
    rows = [_scatter_blocks(partial[n], SHARD_AXIS[n]) for n in SHARDED]
    rows.append(rep_flat(partial).reshape(N_DEV, rep_per))
    buf = _pad_rows(jnp.concatenate(rows, axis=1), 8 * 1024)
    n_own = buf.shape[1]
    buf = buf.reshape(N_DEV, n_own // 1024, 1024)
    landed = _all_to_all(buf, "scatter_grads")
    g_own = _sum_rows(landed, "sum_grads")

    def own_flat(tree):
        parts = [tree[n].reshape(-1) for n in SHARDED]
        parts.append(lax.dynamic_slice(rep_flat(tree), (me * rep_per,), (rep_per,)))
        return _pad_rows(jnp.concatenate(parts), 8 * 1024).reshape(n_own // 1024, 1024)

    delta, new_m, new_v = _adamw(own_flat(wl), g_own, own_flat(ml), own_flat(vl))

    results = {}
    sh_total = sum(sh_sizes)
    for kind, arr in (('grad', g_own), ('delta', delta), ('m', new_m), ('v', new_v)):
        flat = arr.reshape(-1)
        off = 0
        for n, sz in zip(SHARDED, sh_sizes):
            results[kind, n] = flat[off:off + sz].reshape(wl[n].shape)
            off += sz
    rep_own = jnp.stack([a.reshape(-1)[sh_total:sh_total + rep_per] for a in (g_own, delta, new_m, new_v)])
    rep_all = _all_gather(rep_own.reshape(4 * rep_per // 1024, 1024), "gather_replicated")
    rep_all = rep_all.reshape(N_DEV, 4, rep_per)
    for k, kind in enumerate(('grad', 'delta', 'm', 'v')):
        flat = rep_all[:, k, :].reshape(-1)
        off = 0
        for n, sz in zip(REPLICATED, rep_sizes):
            results[kind, n] = flat[off:off + sz].reshape(wl[n].shape)
            off += sz

    out = [loss, grad_x[None]]
    for kind in ('grad', 'delta', 'm', 'v'):
        out.extend(results[kind, n] for n in WEIGHTS)
    return tuple(out)
```

```python
import functools
import math

import jax
import jax.numpy as jnp
from jax import lax
from jax.experimental import pallas as pl
from jax.experimental.pallas import tpu as pltpu

F32 = jnp.float32
BF16 = jnp.bfloat16

N_DEV = 8
DEPTH = 2
D_MODEL = 1024
ATTN_W = 384
LRU_W = 384
S5_W = 256
D_IN = 2176
D_FF = 3072
HEAD = 64
ATTN_BLK = 128
ATTN_TILE = 1024
DILATIONS = (1, 4, 16)
S5_G = 16
S5_P = 64
S5_C = 16
S5_STATES = S5_G * S5_P
LRU_C = 8.0
LRU_CONV = 4
FFN_CONV = 3
ROPE_THETA = 10000.0
ALPHA = (2 * DEPTH) ** 0.25
LN_EPS = 1e-5
RMS_EPS = 1e-6
ADAM_LR, ADAM_B1, ADAM_B2, ADAM_EPS, ADAM_WD, ADAM_STEP = 0.001, 0.9, 0.999, 1e-8, 0.01, 10

LANE = 128
SCAN_T = 256
FFN_CB = 2 * D_FF // N_DEV
VMEM_LIMIT = 56 * 1024 * 1024

AXES = ("x", "y", "c")

WEIGHTS = ['w_in', 'lru_conv_w', 'lru_conv_b', 'lru_wr', 'lru_br', 'lru_wi', 'lru_bi', 'lru_lambda',
           's5_a_re', 's5_a_im', 's5_b_re', 's5_b_im', 's5_c_re', 's5_c_im', 's5_d', 's5_log_step',
           's5_w_glu', 's5_b_glu', 'mix_norm_g', 'w_out', 'ln1_g', 'ln1_b', 'w_up', 'ffn_conv_w',
           'ffn_conv_b', 'w_down', 'ln2_g', 'ln2_b']
SHARD_AXIS = {'w_in': 2, 'lru_conv_w': 2, 's5_w_glu': 1, 'w_out': 1, 'w_up': 2, 'ffn_conv_w': 2, 'w_down': 1}
BIG = ['w_in', 'w_out', 'w_up', 'w_down']
SMALL_SHARDED = ['lru_conv_w', 'ffn_conv_w', 's5_w_glu']
REPLICATED = [n for n in WEIGHTS if n not in SHARD_AXIS]


def _cparams(sem=None):
    return pltpu.CompilerParams(dimension_semantics=sem, vmem_limit_bytes=VMEM_LIMIT)


def _ffn_dev(jb):
    return jb // 2 + (N_DEV // 2) * (jb % 2)


def _gelu(x):
    c = math.sqrt(2.0 / math.pi)
    t = jnp.tanh(c * (x + 0.044715 * (x * x * x)))
    return 0.5 * x * (1.0 + t)


def _gelu_grad(x):
    c = math.sqrt(2.0 / math.pi)
    x2 = x * x
    t = jnp.tanh(c * (x + 0.044715 * (x2 * x)))
    return 0.5 * (1.0 + t) + 0.5 * x * (1.0 - t * t) * (c * (1.0 + 3.0 * 0.044715 * x2))


def _sigmoid(x):
    return 1.0 / (1.0 + jnp.exp(-x))


def _log1p(x):
    u = 1.0 + x
    d = u - 1.0
    return jnp.where(d == 0.0, x, jnp.log(u) * (x / jnp.where(d == 0.0, 1.0, d)))


def _softplus(x):
    return jnp.maximum(x, 0.0) + _log1p(jnp.exp(-jnp.abs(x)))


def _expm1(x):
    return jnp.tanh(0.5 * x) * (jnp.exp(x) + 1.0)


def _dot(a, b):
    return jnp.dot(a.astype(BF16), b.astype(BF16), preferred_element_type=F32)


def _dot_nt(a, b):
    return lax.dot_general(a.astype(BF16), b.astype(BF16), (((1,), (1,)), ((), ())),
                           preferred_element_type=F32)


def _dot_tn(a, b):
    return lax.dot_general(a.astype(BF16), b.astype(BF16), (((0,), (0,)), ((), ())),
                           preferred_element_type=F32)


def _rows(shape):
    return lax.broadcasted_iota(jnp.int32, shape, 0)


def _shift_down(x, s, fill):
    r = pltpu.roll(x, s, axis=0)
    return jnp.where(_rows(x.shape) >= s, r, fill)


def _shift_up(x, s, fill):
    t = x.shape[0]
    r = pltpu.roll(x, t - s, axis=0)
    return jnp.where(_rows(x.shape) < t - s, r, fill)


def _shift_down_prev(x, s, prev8):
    if s == 0:
        return x
    t, l = x.shape
    r = pltpu.roll(x, s, axis=0)
    pr = pltpu.roll(prev8, s, axis=0)
    pad = jnp.concatenate([pr, jnp.zeros((t - 8, l), x.dtype)], axis=0)
    return jnp.where(_rows(x.shape) < s, pad, r)


def _shift_up_next(x, s, next8):
    if s == 0:
        return x
    t, l = x.shape
    r = pltpu.roll(x, t - s, axis=0)
    nx = pltpu.roll(next8, 8 - s, axis=0)
    pad = jnp.concatenate([jnp.zeros((t - 8, l), x.dtype), nx], axis=0)
    return jnp.where(_rows(x.shape) >= t - s, pad, r)


def _scan_fwd(a, x):
    t = x.shape[0]
    s = 1
    while s < t:
        x = x + a * _shift_down(x, s, 0.0)
        a = a * _shift_down(a, s, 1.0)
        s *= 2
    return a, x


def _scan_rev(a, x):
    t = x.shape[0]
    s = 1
    while s < t:
        x = x + a * _shift_up(x, s, 0.0)
        a = a * _shift_up(a, s, 1.0)
        s *= 2
    return a, x


def _cpowers(lr, li, n):
    out = [(lr, li)]
    for _ in range(n - 1):
        lr, li = lr * lr - li * li, 2.0 * lr * li
        out.append((lr, li))
    return out


def _cscan(xr, xi, pows, reverse):
    shift = _shift_up if reverse else _shift_down
    s = 1
    for pr, pi in pows:
        sr = shift(xr, s, 0.0)
        si = shift(xi, s, 0.0)
        xr, xi = xr + pr * sr - pi * si, xi + pr * si + pi * sr
        s *= 2
    return xr, xi


def _mm_nn(a, b, tm, tn, name, out_dtype=F32):
    m, k = a.shape
    n = b.shape[1]

    def body(a_ref, b_ref, o_ref):
        o_ref[...] = _dot(a_ref[...], b_ref[...]).astype(out_dtype)

    return pl.pallas_call(
        body, out_shape=jax.ShapeDtypeStruct((m, n), out_dtype), grid=(n // tn, m // tm),
        in_specs=[pl.BlockSpec((tm, k), lambda j, i: (i, 0)), pl.BlockSpec((k, tn), lambda j, i: (0, j))],
        out_specs=pl.BlockSpec((tm, tn), lambda j, i: (i, j)), name=name,
        compiler_params=_cparams(("parallel", "parallel")))(a, b)


def _mm_nt(a, w, tm, tn, name, add=None, add_scale=1.0):
    m, k = a.shape
    n = w.shape[0]

    def body(*refs):
        if add is None:
            a_ref, w_ref, o_ref = refs
            o_ref[...] = _dot_nt(a_ref[...], w_ref[...])
        else:
            a_ref, w_ref, c_ref, o_ref = refs
            o_ref[...] = _dot_nt(a_ref[...], w_ref[...]) + add_scale * c_ref[...]

    in_specs = [pl.BlockSpec((tm, k), lambda j, i: (i, 0)), pl.BlockSpec((tn, k), lambda j, i: (j, 0))]
    args = [a, w]
    if add is not None:
        in_specs.append(pl.BlockSpec((tm, tn), lambda j, i: (i, j)))
        args.append(add)
    return pl.pallas_call(
        body, out_shape=jax.ShapeDtypeStruct((m, n), F32), grid=(n // tn, m // tm),
        in_specs=in_specs, out_specs=pl.BlockSpec((tm, tn), lambda j, i: (i, j)), name=name,
        compiler_params=_cparams(("parallel", "parallel")))(*args)


def _mm_tn(a, b, tm, tn, ts, name):
    s, m = a.shape
    n = b.shape[1]

    def body(a_ref, b_ref, o_ref):
        @pl.when(pl.program_id(2) == 0)
        def _():
            o_ref[...] = jnp.zeros_like(o_ref)
        o_ref[...] += _dot_tn(a_ref[...], b_ref[...])

    return pl.pallas_call(
        body, out_shape=jax.ShapeDtypeStruct((m, n), F32), grid=(m // tm, n // tn, s // ts),
        in_specs=[pl.BlockSpec((ts, tm), lambda i, j, k: (k, i)), pl.BlockSpec((ts, tn), lambda i, j, k: (k, j))],
        out_specs=pl.BlockSpec((tm, tn), lambda i, j, k: (i, j)), name=name,
        compiler_params=_cparams(("parallel", "parallel", "arbitrary")))(a, b)


def _mm_up(h, wg, tm, name):
    s, d = h.shape

    def body(a_ref, w_ref, o_ref):
        o_ref[...] = _dot(a_ref[...], w_ref[...])

    return pl.pallas_call(
        body, out_shape=jax.ShapeDtypeStruct((s, 2 * D_FF), F32), grid=(N_DEV, s // tm),
        in_specs=[pl.BlockSpec((tm, d), lambda j, i: (i, 0)),
                  pl.BlockSpec((None, d, FFN_CB), lambda j, i: (_ffn_dev(j), 0, 0))],
        out_specs=pl.BlockSpec((tm, FFN_CB), lambda j, i: (i, j)), name=name,
        compiler_params=_cparams(("parallel", "parallel")))(h, wg)


def _mm_up_dx(dup, wg, add, add_scale, tm, name):
    s = dup.shape[0]
    d = wg.shape[1]

    def body(a_ref, w_ref, c_ref, o_ref):
        @pl.when(pl.program_id(1) == 0)
        def _():
            o_ref[...] = add_scale * c_ref[...]
        o_ref[...] += _dot_nt(a_ref[...], w_ref[...])

    return pl.pallas_call(
        body, out_shape=jax.ShapeDtypeStruct((s, d), F32), grid=(s // tm, N_DEV),
        in_specs=[pl.BlockSpec((tm, FFN_CB), lambda i, j: (i, j)),
                  pl.BlockSpec((None, d, FFN_CB), lambda i, j: (_ffn_dev(j), 0, 0)),
                  pl.BlockSpec((tm, d), lambda i, j: (i, 0))],
        out_specs=pl.BlockSpec((tm, d), lambda i, j: (i, 0)), name=name,
        compiler_params=_cparams(("parallel", "arbitrary")))(dup, wg, add)


def _mm_up_dw(h, dup, ts, name):
    s, d = h.shape

    def body(a_ref, b_ref, o_ref):
        @pl.when(pl.program_id(1) == 0)
        def _():
            o_ref[...] = jnp.zeros_like(o_ref)
        o_ref[...] += _dot_tn(a_ref[...], b_ref[...])

    return pl.pallas_call(
        body, out_shape=jax.ShapeDtypeStruct((N_DEV, d, FFN_CB), F32), grid=(N_DEV, s // ts),
        in_specs=[pl.BlockSpec((ts, d), lambda j, k: (k, 0)), pl.BlockSpec((ts, FFN_CB), lambda j, k: (k, j))],
        out_specs=pl.BlockSpec((None, d, FFN_CB), lambda j, k: (_ffn_dev(j), 0, 0)), name=name,
        compiler_params=_cparams(("parallel", "arbitrary")))(h, dup)


def _ln_fwd(a, b, g, bias, name):
    s, d = a.shape
    tm = 512

    def body(a_ref, b_ref, g_ref, bias_ref, r_ref, h_ref):
        r = ALPHA * a_ref[...] + b_ref[...]
        mu = jnp.mean(r, axis=-1, keepdims=True)
        xc = r - mu
        var = jnp.mean(xc * xc, axis=-1, keepdims=True)
        r_ref[...] = r
        h_ref[...] = xc * lax.rsqrt(var + LN_EPS) * g_ref[...] + bias_ref[...]

    row = pl.BlockSpec((tm, d), lambda i: (i, 0))
    vec = pl.BlockSpec((1, d), lambda i: (0, 0))
    return pl.pallas_call(
        body, out_shape=(jax.ShapeDtypeStruct((s, d), F32), jax.ShapeDtypeStruct((s, d), F32)),
        grid=(s // tm,), in_specs=[row, row, vec, vec], out_specs=(row, row), name=name,
        compiler_params=_cparams(("parallel",)))(a, b, g, bias)


def _ln_bwd(r, dh, g, name):
    s, d = r.shape
    tm = 512

    def body(r_ref, dh_ref, g_ref, dr_ref, dg_ref, db_ref):
        @pl.when(pl.program_id(0) == 0)
        def _():
            dg_ref[...] = jnp.zeros_like(dg_ref)
            db_ref[...] = jnp.zeros_like(db_ref)
        rr = r_ref[...]
        dh_ = dh_ref[...]
        mu = jnp.mean(rr, axis=-1, keepdims=True)
        xc = rr - mu
        var = jnp.mean(xc * xc, axis=-1, keepdims=True)
        rstd = lax.rsqrt(var + LN_EPS)
        xh = xc * rstd
        dxh = dh_ * g_ref[...]
        m1 = jnp.mean(dxh, axis=-1, keepdims=True)
        m2 = jnp.mean(dxh * xh, axis=-1, keepdims=True)
        dr_ref[...] = rstd * (dxh - m1 - xh * m2)
        dg_ref[...] += jnp.sum(dh_ * xh, axis=0, keepdims=True)
        db_ref[...] += jnp.sum(dh_, axis=0, keepdims=True)

    row = pl.BlockSpec((tm, d), lambda i: (i, 0))
    vec = pl.BlockSpec((1, d), lambda i: (0, 0))
    return pl.pallas_call(
        body, out_shape=(jax.ShapeDtypeStruct((s, d), F32), jax.ShapeDtypeStruct((1, d), F32),
                         jax.ShapeDtypeStruct((1, d), F32)),
        grid=(s // tm,), in_specs=[row, row, vec], out_specs=(row, vec, vec), name=name,
        compiler_params=_cparams(("arbitrary",)))(r, dh, g)


def _loss_head(y, target):
    s, d = y.shape
    tm = 512

    def body(y_ref, t_ref, dy_ref, l_ref):
        @pl.when(pl.program_id(0) == 0)
        def _():
            l_ref[...] = jnp.zeros_like(l_ref)
        e = y_ref[...] - t_ref[...]
        dy_ref[...] = e * (1.0 / d)
        part = 0.5 * jnp.sum(jnp.mean(e * e, axis=-1, keepdims=True), axis=0, keepdims=True)
        l_ref[...] += jnp.broadcast_to(part, l_ref.shape)

    row = pl.BlockSpec((tm, d), lambda i: (i, 0))
    return pl.pallas_call(
        body, out_shape=(jax.ShapeDtypeStruct((s, d), F32), jax.ShapeDtypeStruct((1, LANE), F32)),
        grid=(s // tm,), in_specs=[row, row], out_specs=(row, pl.BlockSpec((1, LANE), lambda i: (0, 0))),
        name="loss_head", compiler_params=_cparams(("arbitrary",)))(y, target)


def _rope_tables(s):
    half = HEAD // 2
    pos = jnp.arange(s, dtype=F32)
    inv = ROPE_THETA ** (-jnp.arange(half, dtype=F32) * 2.0 / HEAD)
    ang = pos[:, None] * inv[None, :]
    cos, sin = jnp.cos(ang), jnp.sin(ang)
    cos = jnp.concatenate([cos, cos, cos, cos], axis=1)
    sin = jnp.concatenate([-sin, sin, -sin, sin], axis=1)
    return cos, sin


def _rotate(x, cos, sin):
    lane = lax.broadcasted_iota(jnp.int32, x.shape, 1)
    partner = jnp.where((lane % HEAD) < HEAD // 2, pltpu.roll(x, LANE - HEAD // 2, axis=1),
                        pltpu.roll(x, HEAD // 2, axis=1))
    return x * cos + partner * sin


def _rope_fwd(proj, cos, sin, name):
    s = proj.shape[0]
    tm = 512
    w = 3 * ATTN_W

    def body(p_ref, c_ref, s_ref, o_ref):
        c, sn = c_ref[...], s_ref[...]
        for j in range(w // LANE):
            x = p_ref[:, j * LANE:(j + 1) * LANE]
            if j < 2 * ATTN_W // LANE:
                x = _rotate(x, c, sn)
            o_ref[:, j * LANE:(j + 1) * LANE] = x.astype(BF16)

    tab = pl.BlockSpec((tm, LANE), lambda i: (i, 0))
    return pl.pallas_call(
        body, out_shape=jax.ShapeDtypeStruct((s, w), BF16), grid=(s // tm,),
        in_specs=[pl.BlockSpec((tm, w), lambda i: (i, 0)), tab, tab],
        out_specs=pl.BlockSpec((tm, w), lambda i: (i, 0)), name=name,
        compiler_params=_cparams(("parallel",)))(proj, cos, sin)


def _dproj_assemble(dqkv_list, dxr, dgate, du, cos, sin, name):
    s = dxr.shape[0]
    tm = 512
    nq = 3 * ATTN_W // LANE

    def body(*refs):
        br = refs[:9]
        dxr_ref, dg_ref, du_ref, c_ref, s_ref, o_ref = refs[9:]
        c, sn = c_ref[...], -s_ref[...]
        for j in range(nq):
            part, jj = divmod(j, ATTN_W // LANE)
            cols = slice(jj * LANE, (jj + 1) * LANE)
            x = br[part][:, cols] + br[3 + part][:, cols] + br[6 + part][:, cols]
            if part < 2:
                x = _rotate(x, c, sn)
            o_ref[:, j * LANE:(j + 1) * LANE] = x.astype(BF16)
        o_ref[:, 3 * ATTN_W:3 * ATTN_W + LRU_W] = dxr_ref[...].astype(BF16)
        o_ref[:, 3 * ATTN_W + LRU_W:3 * ATTN_W + 2 * LRU_W] = dg_ref[...].astype(BF16)
        o_ref[:, 3 * ATTN_W + 2 * LRU_W:] = du_ref[...].astype(BF16)

    a_spec = pl.BlockSpec((tm, ATTN_W), lambda i: (i, 0))
    tab = pl.BlockSpec((tm, LANE), lambda i: (i, 0))
    ordered = [dqkv_list[b][p] for b in range(3) for p in range(3)]
    return pl.pallas_call(
        body, out_shape=jax.ShapeDtypeStruct((s, D_IN), BF16), grid=(s // tm,),
        in_specs=[a_spec] * 9 + [a_spec, a_spec, pl.BlockSpec((tm, S5_W), lambda i: (i, 0)), tab, tab],
        out_specs=pl.BlockSpec((tm, D_IN), lambda i: (i, 0)), name=name,
        compiler_params=_cparams(("parallel",)))(*ordered, dxr, dgate, du, cos, sin)


def _attn_tiles(s, d):
    m = s // d
    tq = min(m, ATTN_TILE)
    return m, tq, tq // ATTN_BLK


def _band_mask(qb):
    qi = lax.broadcasted_iota(jnp.int32, (ATTN_BLK, 2 * ATTN_BLK), 0)
    ki = lax.broadcasted_iota(jnp.int32, (ATTN_BLK, 2 * ATTN_BLK), 1)
    dist = qi + ATTN_BLK - ki
    return (dist >= 0) & (dist <= ATTN_BLK) & ((ki >= ATTN_BLK) | (qb > 0))


def _head_cols(h):
    return (slice(h * HEAD, (h + 1) * HEAD), slice(ATTN_W + h * HEAD, ATTN_W + (h + 1) * HEAD),
            slice(2 * ATTN_W + h * HEAD, 2 * ATTN_W + (h + 1) * HEAD))


def _attn_fwd(qkv, d, name):
    s = qkv.shape[0]
    w3 = 3 * ATTN_W
    m, tq, n = _attn_tiles(s, d)
    qv = qkv.reshape(m, d * w3)
    scale = HEAD ** -0.5

    def body(x_ref, p_ref, o_ref, l_ref):
        b = pl.program_id(1)

        def block(i, first):
            r0 = 0 if first else pl.multiple_of(i * ATTN_BLK, ATTN_BLK)
            rows = pl.ds(r0, ATTN_BLK)
            valid = _band_mask(b * n + i)
            if not first:
                krows = pl.ds(pl.multiple_of(i * ATTN_BLK - ATTN_BLK, ATTN_BLK), 2 * ATTN_BLK)
            for h in range(ATTN_W // HEAD):
                qs, ks, vs = _head_cols(h)
                q = x_ref[rows, qs]
                if first:
                    k = jnp.concatenate([p_ref[:, ks], x_ref[0:ATTN_BLK, ks]], axis=0)
                    v = jnp.concatenate([p_ref[:, vs], x_ref[0:ATTN_BLK, vs]], axis=0)
                else:
                    k = x_ref[krows, ks]
                    v = x_ref[krows, vs]
                sc = jnp.where(valid, _dot_nt(q, k) * scale, -1e30)
                mx = jnp.max(sc, axis=-1, keepdims=True)
                p = jnp.exp(sc - mx)
                l = jnp.sum(p, axis=-1, keepdims=True)
                o_ref[rows, qs] = _dot(p, v) / l
                l_ref[rows, qs] = jnp.broadcast_to(mx + jnp.log(l), (ATTN_BLK, HEAD))

        block(0, True)
        if n > 1:
            def loop(i, carry):
                block(i, False)
                return carry
            lax.fori_loop(1, n, loop, 0)

    shp = jax.ShapeDtypeStruct((m, d * ATTN_W), F32)
    ospec = pl.BlockSpec((tq, ATTN_W), lambda c, b: (b, c))
    out, lse = pl.pallas_call(
        body, out_shape=(shp, shp), grid=(d, m // tq),
        in_specs=[pl.BlockSpec((tq, w3), lambda c, b: (b, c)),
                  pl.BlockSpec((ATTN_BLK, w3), lambda c, b: (jnp.maximum(b * n - 1, 0), c))],
        out_specs=(ospec, ospec), name=name,
        compiler_params=_cparams(("parallel", "parallel")))(qv, qv)
    return out.reshape(s, ATTN_W), lse.reshape(s, ATTN_W)


def _attn_bwd(qkv, o_all, do_all, lse_all, d, name):
    s = qkv.shape[0]
    w3 = 3 * ATTN_W
    m, tq, n = _attn_tiles(s, d)
    nb = m // ATTN_BLK
    qv = qkv.reshape(m, d * w3)
    view = lambda t: t.reshape(m, d * ATTN_W)
    scale = HEAD ** -0.5

    def body(x_ref, p_ref, nx_ref, o_ref, do_ref, l_ref, on_ref, don_ref, ln_ref, dq_ref, dk_ref, dv_ref):
        b = pl.program_id(1)
        dk_ref[...] = jnp.zeros_like(dk_ref)
        dv_ref[...] = jnp.zeros_like(dv_ref)

        def grads(q, k, v, o, do, lse, valid):
            sc = jnp.where(valid, _dot_nt(q, k) * scale, -1e30)
            p = jnp.exp(sc - lse)
            delta = jnp.sum(do * o, axis=-1, keepdims=True)
            return p, p * (_dot_nt(do, v) - delta) * scale

        def block(i, first):
            r0 = 0 if first else pl.multiple_of(i * ATTN_BLK, ATTN_BLK)
            rows = pl.ds(r0, ATTN_BLK)
            valid = _band_mask(b * n + i)
            if not first:
                krows = pl.ds(pl.multiple_of(i * ATTN_BLK - ATTN_BLK, ATTN_BLK), 2 * ATTN_BLK)
            for h in range(ATTN_W // HEAD):
                qs, ks, vs = _head_cols(h)
                q = x_ref[rows, qs]
                do = do_ref[rows, qs]
                if first:
                    k = jnp.concatenate([p_ref[:, ks], x_ref[0:ATTN_BLK, ks]], axis=0)
                    v = jnp.concatenate([p_ref[:, vs], x_ref[0:ATTN_BLK, vs]], axis=0)
                else:
                    k = x_ref[krows, ks]
                    v = x_ref[krows, vs]
                p, ds = grads(q, k, v, o_ref[rows, qs], do, l_ref[rows, qs][:, 0:1], valid)
                dq_ref[rows, qs] = _dot(ds, k)
                if first:
                    dk_ref[0:ATTN_BLK, qs] += _dot_tn(ds[:, ATTN_BLK:], q)
                    dv_ref[0:ATTN_BLK, qs] += _dot_tn(p[:, ATTN_BLK:], do)
                else:
                    dk_ref[krows, qs] += _dot_tn(ds, q)
                    dv_ref[krows, qs] += _dot_tn(p, do)

        block(0, True)
        if n > 1:
            def loop(i, carry):
                block(i, False)
                return carry
            lax.fori_loop(1, n, loop, 0)

        last = slice((n - 1) * ATTN_BLK, n * ATTN_BLK)
        qi = lax.broadcasted_iota(jnp.int32, (ATTN_BLK, ATTN_BLK), 0)
        ki = lax.broadcasted_iota(jnp.int32, (ATTN_BLK, ATTN_BLK), 1)
        valid_next = (qi <= ki) & ((b + 1) * n < nb)
        for h in range(ATTN_W // HEAD):
            qs, ks, vs = _head_cols(h)
            q = nx_ref[:, qs]
            do = don_ref[:, qs]
            p, ds = grads(q, x_ref[last, ks], x_ref[last, vs], on_ref[:, qs], do, ln_ref[:, qs][:, 0:1],
                          valid_next)
            dk_ref[last, qs] += _dot_tn(ds, q)
            dv_ref[last, qs] += _dot_tn(p, do)

    nxt = lambda b: jnp.minimum((b + 1) * n, nb - 1)
    xs = pl.BlockSpec((tq, w3), lambda c, b: (b, c))
    xp = pl.BlockSpec((ATTN_BLK, w3), lambda c, b: (jnp.maximum(b * n - 1, 0), c))
    xn = pl.BlockSpec((ATTN_BLK, w3), lambda c, b: (nxt(b), c))
    a = pl.BlockSpec((tq, ATTN_W), lambda c, b: (b, c))
    an = pl.BlockSpec((ATTN_BLK, ATTN_W), lambda c, b: (nxt(b), c))
    shp = jax.ShapeDtypeStruct((m, d * ATTN_W), F32)
    ov, dov, lv = view(o_all), view(do_all), view(lse_all)
    dq, dk, dv = pl.pallas_call(
        body, out_shape=(shp, shp, shp), grid=(d, m // tq),
        in_specs=[xs, xp, xn, a, a, a, an, an, an], out_specs=(a, a, a), name=name,
        compiler_params=_cparams(("parallel", "parallel")))(qv, qv, qv, ov, dov, lv, ov, dov, lv)
    return dq.reshape(s, ATTN_W), dk.reshape(s, ATTN_W), dv.reshape(s, ATTN_W)


def _rms(x, g):
    ms = jnp.mean(x * x, axis=-1, keepdims=True)
    return x * lax.rsqrt(ms + RMS_EPS) * g


def _rms_bwd(x, g, dy):
    ms = jnp.mean(x * x, axis=-1, keepdims=True)
    r = lax.rsqrt(ms + RMS_EPS)
    dyg = dy * g
    dx = r * dyg - x * (r * r * r) * jnp.mean(x * dyg, axis=-1, keepdims=True)
    return dx, dy * x * r


def _mix_fwd(outs, lses, lru, s5, g, name):
    s = lru.shape[0]
    tm = 256

    def body(o1, o2, o3, l1, l2, l3, lru_ref, s5_ref, g_ref, mixed_ref, o_ref, lse_ref):
        a1, a2, a3 = l1[...], l2[...], l3[...]
        mx = jnp.maximum(jnp.maximum(a1, a2), a3)
        e1, e2, e3 = jnp.exp(a1 - mx), jnp.exp(a2 - mx), jnp.exp(a3 - mx)
        den = e1 + e2 + e3
        o = (e1 * o1[...] + e2 * o2[...] + e3 * o3[...]) / den
        o_ref[...] = o
        lse_ref[...] = mx + jnp.log(den)
        gg = g_ref[...]
        mixed_ref[:, :ATTN_W] = _rms(o, gg[:, :ATTN_W]).astype(BF16)
        mixed_ref[:, ATTN_W:ATTN_W + LRU_W] = _rms(lru_ref[...], gg[:, ATTN_W:ATTN_W + LRU_W]).astype(BF16)
        mixed_ref[:, ATTN_W + LRU_W:] = _rms(s5_ref[...], gg[:, ATTN_W + LRU_W:]).astype(BF16)

    a = pl.BlockSpec((tm, ATTN_W), lambda i: (i, 0))
    s5s = pl.BlockSpec((tm, S5_W), lambda i: (i, 0))
    full = pl.BlockSpec((tm, D_MODEL), lambda i: (i, 0))
    vec = pl.BlockSpec((1, D_MODEL), lambda i: (0, 0))
    return pl.pallas_call(
        body, out_shape=(jax.ShapeDtypeStruct((s, D_MODEL), BF16), jax.ShapeDtypeStruct((s, ATTN_W), F32),
                         jax.ShapeDtypeStruct((s, ATTN_W), F32)),
        grid=(s // tm,), in_specs=[a] * 6 + [a, s5s, vec], out_specs=(full, a, a), name=name,
        compiler_params=_cparams(("parallel",)))(*outs, *lses, lru, s5, g)


def _mix_bwd(dmixed, o, lru, s5, g, name):
    s = lru.shape[0]
    tm = 256

    def body(dm_ref, o_ref, lru_ref, s5_ref, g_ref, do_ref, dlru_ref, ds5_ref, dg_ref):
        @pl.when(pl.program_id(0) == 0)
        def _():
            dg_ref[...] = jnp.zeros_like(dg_ref)
        gg = g_ref[...]
        dm = dm_ref[...]
        dx, dgr = _rms_bwd(o_ref[...], gg[:, :ATTN_W], dm[:, :ATTN_W])
        do_ref[...] = dx
        dg_ref[:, :ATTN_W] += jnp.sum(dgr, axis=0, keepdims=True)
        dx, dgr = _rms_bwd(lru_ref[...], gg[:, ATTN_W:ATTN_W + LRU_W], dm[:, ATTN_W:ATTN_W + LRU_W])
        dlru_ref[...] = dx
        dg_ref[:, ATTN_W:ATTN_W + LRU_W] += jnp.sum(dgr, axis=0, keepdims=True)
        dx, dgr = _rms_bwd(s5_ref[...], gg[:, ATTN_W + LRU_W:], dm[:, ATTN_W + LRU_W:])
        ds5_ref[...] = dx
        dg_ref[:, ATTN_W + LRU_W:] += jnp.sum(dgr, axis=0, keepdims=True)

    a = pl.BlockSpec((tm, ATTN_W), lambda i: (i, 0))
    s5s = pl.BlockSpec((tm, S5_W), lambda i: (i, 0))
    full = pl.BlockSpec((tm, D_MODEL), lambda i: (i, 0))
    vec = pl.BlockSpec((1, D_MODEL), lambda i: (0, 0))
    return pl.pallas_call(
        body, out_shape=(jax.ShapeDtypeStruct((s, ATTN_W), F32), jax.ShapeDtypeStruct((s, LRU_W), F32),
                         jax.ShapeDtypeStruct((s, S5_W), F32), jax.ShapeDtypeStruct((1, D_MODEL), F32)),
        grid=(s // tm,), in_specs=[full, a, a, s5s, vec], out_specs=(a, a, s5s, vec), name=name,
        compiler_params=_cparams(("arbitrary",)))(dmixed, o, lru, s5, g)


def _lru_gate_math(xc, pre_r, pre_i, lam):
    r = _sigmoid(pre_r)
    i = _sigmoid(pre_i)
    log_a = -LRU_C * r * _softplus(-lam)
    a = jnp.exp(log_a)
    u = jnp.sqrt(-_expm1(2.0 * log_a)) * (i * xc)
    return a, u


def _lru_conv(x, prev8, cw, cb):
    y = cb + cw[LRU_CONV - 1:LRU_CONV, :] * x
    for k in range(LRU_CONV - 1):
        y = y + cw[k:k + 1, :] * _shift_down_prev(x, LRU_CONV - 1 - k, prev8)
    return y


def _lru_specs(s):
    xo = 3 * ATTN_W // LANE
    go = xo + LRU_W // LANE
    xr = pl.BlockSpec((s, LANE), lambda j: (0, xo + j))
    gt = pl.BlockSpec((s, LANE), lambda j: (0, go + j))
    cw = pl.BlockSpec((LRU_CONV, LANE), lambda j: (0, j))
    vec = pl.BlockSpec((1, LANE), lambda j: (0, j))
    wbd = pl.BlockSpec((LANE, LANE), lambda j: (j, j))
    col = pl.BlockSpec((s, LANE), lambda j: (0, j))
    return xr, gt, cw, vec, wbd, col


def _lru_fwd(proj, cw, cb, wr, br, wi, bi, lam, name):
    s = proj.shape[0]
    t = SCAN_T

    def body(xr_ref, gt_ref, cw_ref, cb_ref, wr_ref, br_ref, wi_ref, bi_ref, lam_ref, o_ref):
        cwv, cbv, lamv = cw_ref[...], cb_ref[...], lam_ref[...]
        wrv, wiv, brv, biv = wr_ref[...], wi_ref[...], br_ref[...], bi_ref[...]

        def chunk(c, carry):
            h_c, prev8 = carry
            rows = pl.ds(pl.multiple_of(c * t, t), t)
            x = xr_ref[rows, :]
            xc = _lru_conv(x, prev8, cwv, cbv)
            a, u = _lru_gate_math(xc, _dot(xc, wrv) + brv, _dot(xc, wiv) + biv, lamv)
            acum, hloc = _scan_fwd(a, u)
            h = hloc + acum * h_c
            o_ref[rows, :] = h * _gelu(gt_ref[rows, :])
            return h[t - 1:t, :], x[t - 8:t, :]

        lax.fori_loop(0, s // t, chunk, (jnp.zeros((1, LANE), F32), jnp.zeros((8, LANE), F32)))

    xr, gt, cws, vec, wbd, col = _lru_specs(s)
    return pl.pallas_call(
        body, out_shape=jax.ShapeDtypeStruct((s, LRU_W), F32), grid=(LRU_W // LANE,),
        in_specs=[xr, gt, cws, vec, wbd, vec, wbd, vec, vec], out_specs=col, name=name,
        compiler_params=_cparams(("parallel",)))(proj, proj, cw, cb, wr, br, wi, bi, lam)


def _lru_bwd(proj, dout, cw, cb, wr, br, wi, bi, lam, name):
    s = proj.shape[0]
    t = SCAN_T
    nc = s // t

    def body(xr_ref, gt_ref, do_ref, cw_ref, cb_ref, wr_ref, br_ref, wi_ref, bi_ref, lam_ref,
             dxr_ref, dgt_ref, dcw_ref, dcb_ref, dwr_ref, dbr_ref, dwi_ref, dbi_ref, dlam_ref,
             xc_s, a_s, h_s):
        cwv, cbv, lamv = cw_ref[...], cb_ref[...], lam_ref[...]
        wrv, wiv, brv, biv = wr_ref[...], wi_ref[...], br_ref[...], bi_ref[...]

        def fchunk(c, carry):
            h_c, prev8 = carry
            rows = pl.ds(pl.multiple_of(c * t, t), t)
            x = xr_ref[rows, :]
            xc = _lru_conv(x, prev8, cwv, cbv)
            a, u = _lru_gate_math(xc, _dot(xc, wrv) + brv, _dot(xc, wiv) + biv, lamv)
            acum, hloc = _scan_fwd(a, u)
            h = hloc + acum * h_c
            xc_s[rows, :] = xc
            a_s[rows, :] = a
            h_s[rows, :] = h
            return h[t - 1:t, :], x[t - 8:t, :]

        lax.fori_loop(0, nc, fchunk, (jnp.zeros((1, LANE), F32), jnp.zeros((8, LANE), F32)))

        z1 = jnp.zeros((1, LANE), F32)
        zw = jnp.zeros((LANE, LANE), F32)

        def bchunk(ci, carry):
            g_next, a_next, dxc_next8, dcw, dcb, dwr, dbr, dwi, dbi, dlam = carry
            c = nc - 1 - ci
            t0 = pl.multiple_of(c * t, t)
            rows = pl.ds(t0, t)
            before = pl.ds(pl.multiple_of(jnp.maximum(t0 - 8, 0), 8), 8)
            has_prev = (c > 0).astype(F32)
            x, gt, do = xr_ref[rows, :], gt_ref[rows, :], do_ref[rows, :]
            xc, a, h = xc_s[rows, :], a_s[rows, :], h_s[rows, :]
            prev8_x = xr_ref[before, :] * has_prev
            prev8_h = h_s[before, :] * has_prev
            dgt_ref[rows, :] = do * h * _gelu_grad(gt)
            dh = do * _gelu(gt)
            a_plus = _shift_up_next(a, 1, jnp.broadcast_to(a_next, (8, LANE)))
            acum, gloc = _scan_rev(a_plus, dh)
            g = gloc + acum * g_next
            da = g * _shift_down_prev(h, 1, prev8_h)
            pre_r = _dot(xc, wrv) + brv
            pre_i = _dot(xc, wiv) + biv
            _, vjp = jax.vjp(_lru_gate_math, xc, pre_r, pre_i, lamv)
            dxc, dpre_r, dpre_i, dlam_c = vjp((da, g))
            dxc = dxc + _dot_nt(dpre_r, wrv) + _dot_nt(dpre_i, wiv)
            dx = cwv[LRU_CONV - 1:LRU_CONV, :] * dxc
            dcw_rows = [None] * LRU_CONV
            dcw_rows[LRU_CONV - 1] = jnp.sum(dxc * x, axis=0, keepdims=True)
            for k in range(LRU_CONV - 1):
                sh = LRU_CONV - 1 - k
                dx = dx + cwv[k:k + 1, :] * _shift_up_next(dxc, sh, dxc_next8)
                dcw_rows[k] = jnp.sum(dxc * _shift_down_prev(x, sh, prev8_x), axis=0, keepdims=True)
            dxr_ref[rows, :] = dx
            return (g[0:1, :], a[0:1, :], dxc[0:8, :],
                    dcw + jnp.concatenate(dcw_rows, axis=0),
                    dcb + jnp.sum(dxc, axis=0, keepdims=True),
                    dwr + _dot_tn(xc, dpre_r), dbr + jnp.sum(dpre_r, axis=0, keepdims=True),
                    dwi + _dot_tn(xc, dpre_i), dbi + jnp.sum(dpre_i, axis=0, keepdims=True),
                    dlam + dlam_c)

        init = (z1, z1, jnp.zeros((8, LANE), F32), jnp.zeros((LRU_CONV, LANE), F32), z1, zw, z1, zw, z1, z1)
        res = lax.fori_loop(0, nc, bchunk, init)
        dcw_ref[...] = res[3]
        dcb_ref[...] = res[4]
        dwr_ref[...] = res[5]
        dbr_ref[...] = res[6]
        dwi_ref[...] = res[7]
        dbi_ref[...] = res[8]
        dlam_ref[...] = res[9]

    xr, gt, cws, vec, wbd, col = _lru_specs(s)
    vshape = jax.ShapeDtypeStruct((1, LRU_W), F32)
    wshape = jax.ShapeDtypeStruct((LRU_W, LRU_W), F32)
    return pl.pallas_call(
        body,
        out_shape=(jax.ShapeDtypeStruct((s, LRU_W), F32), jax.ShapeDtypeStruct((s, LRU_W), F32),
                   jax.ShapeDtypeStruct((LRU_CONV, LRU_W), F32), vshape, wshape, vshape, wshape, vshape, vshape),
        grid=(LRU_W // LANE,),
        in_specs=[xr, gt, col, cws, vec, wbd, vec, wbd, vec, vec],
        out_specs=(col, col, cws, vec, wbd, vec, wbd, vec, vec),
        scratch_shapes=[pltpu.VMEM((s, LANE), F32)] * 3, name=name,
        compiler_params=_cparams(("parallel",)))(proj, proj, dout, cw, cb, wr, br, wi, bi, lam)


def _s5_disc_math(a_re, a_im, log_step, bt_re, bt_im):
    step = jnp.exp(log_step)
    dt_re, dt_im = step * a_re, step * a_im
    mag = jnp.exp(dt_re)
    ab_re, ab_im = mag * jnp.cos(dt_im), mag * jnp.sin(dt_im)
    z_re, z_im = ab_re - 1.0, ab_im
    den = a_re * a_re + a_im * a_im
    f_re = (z_re * a_re + z_im * a_im) / den
    f_im = (z_im * a_re - z_re * a_im) / den
    bb_re = f_re * bt_re - f_im * bt_im
    bb_im = f_re * bt_im + f_im * bt_re
    return ab_re, ab_im, bb_re, bb_im


def _s5_disc_fwd(a_re, a_im, log_step, bt_re, bt_im, name):
    def body(ar, ai, ls, br, bi, o1, o2, o3, o4):
        r = _s5_disc_math(ar[...], ai[...], ls[...], br[...], bi[...])
        o1[...], o2[...], o3[...], o4[...] = r

    shp = jax.ShapeDtypeStruct(a_re.shape, F32)
    return pl.pallas_call(body, out_shape=(shp,) * 4, name=name)(a_re, a_im, log_step, bt_re, bt_im)


def _s5_disc_bwd(a_re, a_im, log_step, bt_re, bt_im, cts, name):
    def body(ar, ai, ls, br, bi, c1, c2, c3, c4, o1, o2, o3, o4, o5):
        _, vjp = jax.vjp(_s5_disc_math, ar[...], ai[...], ls[...], br[...], bi[...])
        r = vjp((c1[...], c2[...], c3[...], c4[...]))
        o1[...], o2[...], o3[...], o4[...], o5[...] = r

    shp = jax.ShapeDtypeStruct(a_re.shape, F32)
    return pl.pallas_call(body, out_shape=(shp,) * 5, name=name)(a_re, a_im, log_step, bt_re, bt_im, *cts)


def _s5_u_specs(s):
    uo = (3 * ATTN_W + 2 * LRU_W) // LANE
    return (pl.BlockSpec((s, LANE), lambda j: (0, uo)), pl.BlockSpec((s, LANE), lambda j: (0, uo + 1)))


def _s5_scan_fwd(proj, b_re, b_im, lam_re, lam_im, c_re, c_im, name):
    s = proj.shape[0]
    t = SCAN_T
    nlog = int(math.log2(t))

    def body(u0_ref, u1_ref, bre_ref, bim_ref, lre_ref, lim_ref, cre_ref, cim_ref, xre_ref, xim_ref, y_ref):
        @pl.when(pl.program_id(0) == 0)
        def _():
            y_ref[...] = jnp.zeros_like(y_ref)
        lr, li = lre_ref[...], lim_ref[...]
        pows = _cpowers(lr, li, nlog)
        first = _rows((t, LANE)) == 0
        tab_r, tab_i = _cscan(jnp.where(first, lr, 0.0), jnp.where(first, li, 0.0), pows, False)
        bre, bim, cre, cim = bre_ref[...], bim_ref[...], cre_ref[...], cim_ref[...]

        def chunk(c, carry):
            cr, ci = carry
            rows = pl.ds(pl.multiple_of(c * t, t), t)
            u = jnp.concatenate([u0_ref[rows, :], u1_ref[rows, :]], axis=1).astype(BF16)
            xr, xi = _cscan(_dot(u, bre), _dot(u, bim), pows, False)
            xr, xi = xr + tab_r * cr - tab_i * ci, xi + tab_r * ci + tab_i * cr
            xre_ref[rows, :] = xr
            xim_ref[rows, :] = xi
            y_ref[rows, :] += _dot(xr, cre) - _dot(xi, cim)
            return xr[t - 1:t, :], xi[t - 1:t, :]

        z = jnp.zeros((1, LANE), F32)
        lax.fori_loop(0, s // t, chunk, (z, z))

    u0, u1 = _s5_u_specs(s)
    bsp = pl.BlockSpec((S5_W, LANE), lambda j: (0, j))
    csp = pl.BlockSpec((LANE, S5_W), lambda j: (j, 0))
    vec = pl.BlockSpec((1, LANE), lambda j: (0, j))
    xsp = pl.BlockSpec((s, LANE), lambda j: (0, j))
    ysp = pl.BlockSpec((s, S5_W), lambda j: (0, 0))
    xshape = jax.ShapeDtypeStruct((s, S5_STATES), F32)
    return pl.pallas_call(
        body, out_shape=(xshape, xshape, jax.ShapeDtypeStruct((s, S5_W), F32)),
        grid=(S5_STATES // LANE,), in_specs=[u0, u1, bsp, bsp, vec, vec, csp, csp],
        out_specs=(xsp, xsp, ysp), name=name,
        compiler_params=_cparams(("arbitrary",)))(proj, proj, b_re, b_im, lam_re, lam_im, c_re, c_im)


def _s5_scan_bwd(proj, dy, du_init, x_re, x_im, b_re, b_im, lam_re, lam_im, c_re, c_im, name):
    s = proj.shape[0]
    t = SCAN_T
    nc = s // t
    nlog = int(math.log2(t))

    def body(u0_ref, u1_ref, dy_ref, dui_ref, xre_ref, xim_ref, bre_ref, bim_ref, lre_ref, lim_ref,
             cre_ref, cim_ref, du_ref, dlr_ref, dli_ref, dbr_ref, dbi_ref, dcr_ref, dci_ref):
        @pl.when(pl.program_id(0) == 0)
        def _():
            du_ref[...] = dui_ref[...]
        mr, mi = lre_ref[...], -lim_ref[...]
        pows = _cpowers(mr, mi, nlog)
        last = _rows((t, LANE)) == t - 1
        tab_r, tab_i = _cscan(jnp.where(last, mr, 0.0), jnp.where(last, mi, 0.0), pows, True)
        bre, bim, cre, cim = bre_ref[...], bim_ref[...], cre_ref[...], cim_ref[...]
        dbr_ref[...] = jnp.zeros_like(dbr_ref)
        dbi_ref[...] = jnp.zeros_like(dbi_ref)
        dcr_ref[...] = jnp.zeros_like(dcr_ref)
        dci_ref[...] = jnp.zeros_like(dci_ref)

        def chunk(ci_, carry):
            gnr, gni, dlr, dli = carry
            c = nc - 1 - ci_
            t0 = pl.multiple_of(c * t, t)
            rows = pl.ds(t0, t)
            before = pl.ds(pl.multiple_of(jnp.maximum(t0 - 8, 0), 8), 8)
            has_prev = (c > 0).astype(F32)
            dyc = dy_ref[rows, :].astype(BF16)
            u = jnp.concatenate([u0_ref[rows, :], u1_ref[rows, :]], axis=1).astype(BF16)
            gr, gi = _cscan(_dot_nt(dyc, cre), -_dot_nt(dyc, cim), pows, True)
            gr, gi = gr + tab_r * gnr - tab_i * gni, gi + tab_r * gni + tab_i * gnr
            xr, xi = xre_ref[rows, :], xim_ref[rows, :]
            xpr = _shift_down_prev(xr, 1, xre_ref[before, :] * has_prev)
            xpi = _shift_down_prev(xi, 1, xim_ref[before, :] * has_prev)
            dlr = dlr + jnp.sum(gr * xpr + gi * xpi, axis=0, keepdims=True)
            dli = dli + jnp.sum(gi * xpr - gr * xpi, axis=0, keepdims=True)
            du_ref[rows, :] += _dot_nt(gr, bre) + _dot_nt(gi, bim)
            dbr_ref[...] += _dot_tn(u, gr)
            dbi_ref[...] += _dot_tn(u, gi)
            dcr_ref[...] += _dot_tn(xr, dyc)
            dci_ref[...] -= _dot_tn(xi, dyc)
            return gr[0:1, :], gi[0:1, :], dlr, dli

        z = jnp.zeros((1, LANE), F32)
        res = lax.fori_loop(0, nc, chunk, (z, z, z, z))
        dlr_ref[...] = res[2]
        dli_ref[...] = res[3]

    u0, u1 = _s5_u_specs(s)
    bsp = pl.BlockSpec((S5_W, LANE), lambda j: (0, j))
    csp = pl.BlockSpec((LANE, S5_W), lambda j: (j, 0))
    vec = pl.BlockSpec((1, LANE), lambda j: (0, j))
    xsp = pl.BlockSpec((s, LANE), lambda j: (0, j))
    ysp = pl.BlockSpec((s, S5_W), lambda j: (0, 0))
    return pl.pallas_call(
        body,
        out_shape=(jax.ShapeDtypeStruct((s, S5_W), F32),
                   jax.ShapeDtypeStruct((1, S5_STATES), F32), jax.ShapeDtypeStruct((1, S5_STATES), F32),
                   jax.ShapeDtypeStruct((S5_W, S5_STATES), F32), jax.ShapeDtypeStruct((S5_W, S5_STATES), F32),
                   jax.ShapeDtypeStruct((S5_STATES, S5_W), F32), jax.ShapeDtypeStruct((S5_STATES, S5_W), F32)),
        grid=(S5_STATES // LANE,),
        in_specs=[u0, u1, ysp, ysp, xsp, xsp, bsp, bsp, vec, vec, csp, csp],
        out_specs=(ysp, vec, vec, bsp, bsp, csp, csp), name=name,
        compiler_params=_cparams(("arbitrary",)))(
            proj, proj, dy, du_init, x_re, x_im, b_re, b_im, lam_re, lam_im, c_re, c_im)


def _s5_out_fwd(proj, y_acc, dvec, w_glu, b_glu, name):
    s = proj.shape[0]
    tm = 512
    uo = (3 * ATTN_W + 2 * LRU_W) // LANE

    def body(u0_ref, u1_ref, y_ref, d_ref, w_ref, b_ref, o_ref, yp_ref):
        u = jnp.concatenate([u0_ref[...], u1_ref[...]], axis=1)
        y = y_ref[...] + d_ref[...] * u
        yp_ref[...] = y
        yg = _gelu(y)
        o_ref[...] = yg * _sigmoid(_dot(yg, w_ref[...]) + b_ref[...])

    u0 = pl.BlockSpec((tm, LANE), lambda i: (i, uo))
    u1 = pl.BlockSpec((tm, LANE), lambda i: (i, uo + 1))
    row = pl.BlockSpec((tm, S5_W), lambda i: (i, 0))
    vec = pl.BlockSpec((1, S5_W), lambda i: (0, 0))
    wsp = pl.BlockSpec((S5_W, S5_W), lambda i: (0, 0))
    shp = jax.ShapeDtypeStruct((s, S5_W), F32)
    return pl.pallas_call(
        body, out_shape=(shp, shp), grid=(s // tm,), in_specs=[u0, u1, row, vec, wsp, vec],
        out_specs=(row, row), name=name,
        compiler_params=_cparams(("parallel",)))(proj, proj, y_acc, dvec, w_glu, b_glu)


def _s5_out_bwd(proj, y_pre, dout, dvec, w_glu, b_glu, name):
    s = proj.shape[0]
    tm = 512
    uo = (3 * ATTN_W + 2 * LRU_W) // LANE

    def body(u0_ref, u1_ref, y_ref, do_ref, d_ref, w_ref, b_ref, dy_ref, dud_ref, dd_ref, dw_ref, db_ref):
        @pl.when(pl.program_id(0) == 0)
        def _():
            dd_ref[...] = jnp.zeros_like(dd_ref)
            dw_ref[...] = jnp.zeros_like(dw_ref)
            db_ref[...] = jnp.zeros_like(db_ref)
        u = jnp.concatenate([u0_ref[...], u1_ref[...]], axis=1)
        y = y_ref[...]
        do = do_ref[...]
        yg = _gelu(y)
        sg = _sigmoid(_dot(yg, w_ref[...]) + b_ref[...])
        dz = do * yg * sg * (1.0 - sg)
        dyg = do * sg + _dot_nt(dz, w_ref[...])
        dy = dyg * _gelu_grad(y)
        dy_ref[...] = dy
        dud_ref[...] = d_ref[...] * dy
        dd_ref[...] += jnp.sum(dy * u, axis=0, keepdims=True)
        dw_ref[...] += _dot_tn(yg, dz)
        db_ref[...] += jnp.sum(dz, axis=0, keepdims=True)

    u0 = pl.BlockSpec((tm, LANE), lambda i: (i, uo))
    u1 = pl.BlockSpec((tm, LANE), lambda i: (i, uo + 1))
    row = pl.BlockSpec((tm, S5_W), lambda i: (i, 0))
    vec = pl.BlockSpec((1, S5_W), lambda i: (0, 0))
    wsp = pl.BlockSpec((S5_W, S5_W), lambda i: (0, 0))
    shp = jax.ShapeDtypeStruct((s, S5_W), F32)
    vshape = jax.ShapeDtypeStruct((1, S5_W), F32)
    return pl.pallas_call(
        body, out_shape=(shp, shp, vshape, jax.ShapeDtypeStruct((S5_W, S5_W), F32), vshape),
        grid=(s // tm,), in_specs=[u0, u1, row, row, vec, wsp, vec],
        out_specs=(row, row, vec, wsp, vec), name=name,
        compiler_params=_cparams(("arbitrary",)))(proj, proj, y_pre, dout, dvec, w_glu, b_glu)


def _ffn_conv(x, prev8, cw, cb):
    y = cb + cw[FFN_CONV - 1:FFN_CONV, :] * x
    for k in range(FFN_CONV - 1):
        y = y + cw[k:k + 1, :] * _shift_down_prev(x, FFN_CONV - 1 - k, prev8)
    return y


def _ffn_act_fwd(up, cw, cb, name):
    s = up.shape[0]
    tm = 256
    tb = 2 * FFN_CB

    def body(x_ref, p_ref, cw_ref, cb_ref, o_ref):
        prev8 = p_ref[...] * (pl.program_id(1) > 0).astype(F32)
        y = _ffn_conv(x_ref[...], prev8, cw_ref[...], cb_ref[...])
        o_ref[...] = (_gelu(y[:, :FFN_CB]) * y[:, FFN_CB:]).astype(BF16)

    main = pl.BlockSpec((tm, tb), lambda j, i: (i, j))
    prev = pl.BlockSpec((8, tb), lambda j, i: (jnp.maximum(i * (tm // 8) - 1, 0), j))
    return pl.pallas_call(
        body, out_shape=jax.ShapeDtypeStruct((s, D_FF), BF16), grid=(D_FF // FFN_CB, s // tm),
        in_specs=[main, prev, pl.BlockSpec((FFN_CONV, tb), lambda j, i: (0, j)),
                  pl.BlockSpec((1, tb), lambda j, i: (0, j))],
        out_specs=pl.BlockSpec((tm, FFN_CB), lambda j, i: (i, j)), name=name,
        compiler_params=_cparams(("parallel", "parallel")))(up, up, cw, cb)


def _ffn_act_bwd(up, dact, cw, cb, name):
    s = up.shape[0]
    tm = 256
    tb = 2 * FFN_CB
    nr = s // tm

    def body(x_ref, p_ref, n_ref, da_ref, dan_ref, cw_ref, cb_ref, dup_ref, dcw_ref, dcb_ref):
        i = pl.program_id(1)

        @pl.when(i == 0)
        def _():
            dcw_ref[...] = jnp.zeros_like(dcw_ref)
            dcb_ref[...] = jnp.zeros_like(dcb_ref)
        has_next = (i < nr - 1).astype(F32)
        prev8 = p_ref[...] * (i > 0).astype(F32)
        cwv = cw_ref[...]
        x = x_ref[...]
        xe = jnp.concatenate([x, n_ref[...]], axis=0)
        dae = jnp.concatenate([da_ref[...], dan_ref[...] * has_next], axis=0)
        y = _ffn_conv(xe, prev8, cwv, cb_ref[...])
        gate, val = y[:, :FFN_CB], y[:, FFN_CB:]
        dy = jnp.concatenate([dae * val * _gelu_grad(gate), dae * _gelu(gate)], axis=1)
        dym = dy[:tm, :]
        dx = cwv[FFN_CONV - 1:FFN_CONV, :] * dym
        dcw_rows = [None] * FFN_CONV
        dcw_rows[FFN_CONV - 1] = jnp.sum(dym * x, axis=0, keepdims=True)
        for k in range(FFN_CONV - 1):
            sh = FFN_CONV - 1 - k
            dx = dx + cwv[k:k + 1, :] * pltpu.roll(dy, tm + 8 - sh, axis=0)[:tm, :]
            dcw_rows[k] = jnp.sum(dym * _shift_down_prev(x, sh, prev8), axis=0, keepdims=True)
        dup_ref[...] = dx.astype(BF16)
        dcw_ref[...] += jnp.concatenate(dcw_rows, axis=0)
        dcb_ref[...] += jnp.sum(dym, axis=0, keepdims=True)

    main = pl.BlockSpec((tm, tb), lambda j, i: (i, j))
    prev = pl.BlockSpec((8, tb), lambda j, i: (jnp.maximum(i * (tm // 8) - 1, 0), j))
    nxt = pl.BlockSpec((8, tb), lambda j, i: (jnp.minimum((i + 1) * (tm // 8), s // 8 - 1), j))
    da = pl.BlockSpec((tm, FFN_CB), lambda j, i: (i, j))
    dan = pl.BlockSpec((8, FFN_CB), lambda j, i: (jnp.minimum((i + 1) * (tm // 8), s // 8 - 1), j))
    cws = pl.BlockSpec((FFN_CONV, tb), lambda j, i: (0, j))
    cbs = pl.BlockSpec((1, tb), lambda j, i: (0, j))
    return pl.pallas_call(
        body, out_shape=(jax.ShapeDtypeStruct((s, 2 * D_FF), BF16),
                         jax.ShapeDtypeStruct((FFN_CONV, 2 * D_FF), F32),
                         jax.ShapeDtypeStruct((1, 2 * D_FF), F32)),
        grid=(D_FF // FFN_CB, nr), in_specs=[main, prev, nxt, da, dan, cws, cbs],
        out_specs=(main, cws, cbs), name=name,
        compiler_params=_cparams(("parallel", "arbitrary")))(up, up, up, dact, dact, cw, cb)


def _adamw_sum(landed, w, m, v, layer, prev, name):
    _, r, c = landed.shape
    nl = w.shape[0]
    tm = 8
    for cand in (512, 256, 128, 64, 32, 16):
        if r % cand == 0 and N_DEV * cand * c * 4 <= 4 * 1024 * 1024:
            tm = cand
            break

    def body(*refs):
        ld_ref, w_ref, m_ref, v_ref = refs[:4]
        g_ref, d_ref, mo_ref, vo_ref = refs[-4:]
        gg = ld_ref[0]
        for k in range(1, N_DEV):
            gg = gg + ld_ref[k]
        mn = ADAM_B1 * m_ref[...] + (1.0 - ADAM_B1) * gg
        vn = ADAM_B2 * v_ref[...] + (1.0 - ADAM_B2) * (gg * gg)
        m_hat = mn / (1.0 - ADAM_B1 ** ADAM_STEP)
        v_hat = vn / (1.0 - ADAM_B2 ** ADAM_STEP)
        g_ref[...] = gg
        d_ref[...] = -ADAM_LR * (m_hat / (jnp.sqrt(v_hat) + ADAM_EPS) + ADAM_WD * w_ref[...])
        mo_ref[...] = mn
        vo_ref[...] = vn

    blk = pl.BlockSpec((None, tm, c), lambda i: (layer, i, 0))
    in_specs = [pl.BlockSpec((N_DEV, tm, c), lambda i: (0, i, 0)), blk, blk, blk]
    args = [landed, w, m, v]
    aliases = {}
    if prev is not None:
        in_specs += [pl.BlockSpec(memory_space=pl.ANY)] * 4
        args += list(prev)
        aliases = {4 + k: k for k in range(4)}
    shp = jax.ShapeDtypeStruct((nl, r, c), F32)
    return pl.pallas_call(
        body, out_shape=(shp,) * 4, grid=(r // tm,), in_specs=in_specs, out_specs=(blk,) * 4,
        input_output_aliases=aliases, name=name, compiler_params=_cparams(("parallel",)))(*args)


def _all_gather(shards, name):
    na = len(shards)

    def body(*refs):
        x_refs, out_refs = refs[:na], refs[na:2 * na]
        send_sems, recv_sems, local_sems = refs[2 * na:]
        x, y, c = lax.axis_index("x"), lax.axis_index("y"), lax.axis_index("c")
        me, sibling = (x, y, c), (x, y, 1 - c)
        chips = [(1 - x, y), (x, 1 - y), (1 - x, 1 - y)]

        def copy(a, k, block, to, src=None):
            dst = out_refs[a].at[4 * block[0] + 2 * block[1] + block[2]]
            return pltpu.make_async_remote_copy(
                src_ref=dst if src is None else src, dst_ref=dst,
                send_sem=send_sems.at[7 * a + k], recv_sem=recv_sems.at[7 * a + k],
                device_id=to, device_id_type=pl.DeviceIdType.MESH)

        mine, first, passed = [], [], []
        for a in range(na):
            cp = pltpu.make_async_copy(x_refs[a], out_refs[a].at[4 * x + 2 * y + c], local_sems.at[a])
            cp.start()
            mine.append(cp)
            cps = [copy(a, 0, me, sibling, src=x_refs[a])]
            cps += [copy(a, 1 + j, me, (*chip, c), src=x_refs[a]) for j, chip in enumerate(chips)]
            for cp in cps:
                cp.start()
            first += cps
        for j, chip in enumerate(chips):
            for a in range(na):
                copy(a, 1 + j, (*chip, c), me).wait_recv()
                cp = copy(a, 4 + j, (*chip, c), sibling)
                cp.start()
                passed.append(cp)
        for a in range(na):
            copy(a, 0, sibling, me).wait_recv()
            for j, chip in enumerate(chips):
                copy(a, 4 + j, (*chip, 1 - c), me).wait_recv()
        for cp in first + passed:
            cp.wait_send()
        for cp in mine:
            cp.wait()

    anyspec = pl.BlockSpec(memory_space=pl.ANY)
    return pl.pallas_call(
        body, out_shape=tuple(jax.ShapeDtypeStruct((N_DEV,) + t.shape, t.dtype) for t in shards),
        in_specs=[anyspec] * na, out_specs=tuple([anyspec] * na),
        scratch_shapes=[pltpu.SemaphoreType.DMA((7 * na,)), pltpu.SemaphoreType.DMA((7 * na,)),
                        pltpu.SemaphoreType.DMA((na,))],
        name=name)(*shards)


def _all_to_all(bufs, name):
    na = len(bufs)

    def body(*refs):
        b_refs, out_refs = refs[:na], refs[na:2 * na]
        send_sems, recv_sems, local_sems = refs[2 * na:]
        x, y, c = lax.axis_index("x"), lax.axis_index("y"), lax.axis_index("c")
        me = 4 * x + 2 * y + c
        copies = []
        for a in range(na):
            cp = pltpu.make_async_copy(b_refs[a].at[me], out_refs[a].at[me], local_sems.at[a])
            cp.start()
            copies.append(cp)
        for k in range(1, N_DEV):
            px = x ^ ((k >> 2) & 1)
            py = y ^ ((k >> 1) & 1)
            pc = c ^ (k & 1)
            for a in range(na):
                cp = pltpu.make_async_remote_copy(
                    src_ref=b_refs[a].at[4 * px + 2 * py + pc], dst_ref=out_refs[a].at[me],
                    send_sem=send_sems.at[7 * a + k - 1], recv_sem=recv_sems.at[7 * a + k - 1],
                    device_id=(px, py, pc), device_id_type=pl.DeviceIdType.MESH)
                cp.start()
                copies.append(cp)
        for cp in copies:
            cp.wait()

    anyspec = pl.BlockSpec(memory_space=pl.ANY)
    return pl.pallas_call(
        body, out_shape=tuple(jax.ShapeDtypeStruct(t.shape, t.dtype) for t in bufs),
        in_specs=[anyspec] * na, out_specs=tuple([anyspec] * na),
        scratch_shapes=[pltpu.SemaphoreType.DMA((7 * na,)), pltpu.SemaphoreType.DMA((7 * na,)),
                        pltpu.SemaphoreType.DMA((na,))],
        name=name)(*bufs)


def _block_diag(w):
    h, a, b = w.shape
    eye = jnp.eye(h, dtype=w.dtype)
    return (w[:, :, None, :] * eye[:, None, :, None]).reshape(h * a, h * b)


def _block_diag_extract(m, h):
    a, b = m.shape[0] // h, m.shape[1] // h
    return jnp.stack([m[i * a:(i + 1) * a, i * b:(i + 1) * b] for i in range(h)], axis=0)


def _ffn_interleave(w):
    lead = w.shape[:-1]
    nb = D_FF // FFN_CB
    return jnp.swapaxes(w.reshape(*lead, 2, nb, FFN_CB), -3, -2).reshape(*lead, 2 * D_FF)


def _ffn_deinterleave(w):
    lead = w.shape[:-1]
    nb = D_FF // FFN_CB
    return jnp.swapaxes(w.reshape(*lead, nb, 2, FFN_CB), -3, -2).reshape(*lead, 2 * D_FF)


def _gather_full(gathered, axis):
    shape = list(gathered.shape[1:])
    shape[axis] *= N_DEV
    return jnp.moveaxis(gathered, 0, axis).reshape(shape)


def _scatter_blocks(full, axis):
    shape = list(full.shape)
    shape[axis:axis + 1] = [N_DEV, shape[axis] // N_DEV]
    return jnp.moveaxis(full.reshape(shape), axis, 0)


def _pad_to(flat, mult):
    pad = (-flat.shape[-1]) % mult
    if pad:
        flat = jnp.concatenate([flat, jnp.zeros(flat.shape[:-1] + (pad,), flat.dtype)], axis=-1)
    return flat


def _layer_fwd(h_in, w, cos, sin, l):
    tag = "l%d_" % l
    proj = _mm_nn(h_in, w['w_in'], 512, D_IN, tag + "proj")
    qkv = _rope_fwd(proj, cos, sin, tag + "rope")
    outs, lses = [], []
    for d in DILATIONS:
        o, ls = _attn_fwd(qkv, d, tag + "attn_d%d" % d)
        outs.append(o)
        lses.append(ls)
    lru = _lru_fwd(proj, w['lru_conv_w'], w['lru_conv_b'], w['lru_wr'], w['lru_br'], w['lru_wi'],
                   w['lru_bi'], w['lru_lambda'], tag + "lru")
    x_re, x_im, y_acc = _s5_scan_fwd(proj, w['s5_bb_re'], w['s5_bb_im'], w['s5_lam_re'], w['s5_lam_im'],
                                     w['s5_cc_re'], w['s5_cc_im'], tag + "s5_scan")
    s5, y_pre = _s5_out_fwd(proj, y_acc, w['s5_d'], w['s5_w_glu'], w['s5_b_glu'], tag + "s5_out")
    mixed, attn_o, attn_lse = _mix_fwd(outs, lses, lru, s5, w['mix_norm_g'], tag + "mix")
    mixo = _mm_nn(mixed, w['w_out'], 512, D_MODEL, tag + "out_proj")
    r1, h1 = _ln_fwd(h_in, mixo, w['ln1_g'], w['ln1_b'], tag + "ln1")
    up = _mm_up(h1, w['w_up_g'], 512, tag + "up_proj")
    act = _ffn_act_fwd(up, w['ffn_conv_w'], w['ffn_conv_b'], tag + "ffn_act")
    ffn = _mm_nn(act, w['w_down'], 512, D_MODEL, tag + "down_proj")
    r2, h2 = _ln_fwd(h1, ffn, w['ln2_g'], w['ln2_b'], tag + "ln2")
    saved = dict(h_in=h_in, proj=proj, qkv=qkv, lru=lru, x_re=x_re, x_im=x_im, y_pre=y_pre, s5=s5,
                 mixed=mixed, attn_o=attn_o, attn_lse=attn_lse, r1=r1, h1=h1, up=up, act=act, r2=r2)
    return h2, saved


def _layer_bwd(dh2, sv, w, cos, sin, l):
    tag = "l%d_" % l
    g = {}
    dr2, g['ln2_g'], g['ln2_b'] = _ln_bwd(sv['r2'], dh2, w['ln2_g'], tag + "ln2_bwd")
    g['w_down'] = _mm_tn(sv['act'], dr2, 1024, D_MODEL, 512, tag + "down_dw")
    dact = _mm_nt(dr2, w['w_down'], 512, D_FF, tag + "down_dx")
    dup, g['ffn_conv_w'], g['ffn_conv_b'] = _ffn_act_bwd(sv['up'], dact, w['ffn_conv_w'], w['ffn_conv_b'],
                                                        tag + "ffn_act_bwd")
    g['w_up_g'] = _mm_up_dw(sv['h1'], dup, 512, tag + "up_dw")
    dh1 = _mm_up_dx(dup, w['w_up_g'], dr2, ALPHA, 1024, tag + "up_dx")
    dr1, g['ln1_g'], g['ln1_b'] = _ln_bwd(sv['r1'], dh1, w['ln1_g'], tag + "ln1_bwd")
    g['w_out'] = _mm_tn(sv['mixed'], dr1, 1024, D_MODEL, 512, tag + "out_dw")
    dmixed = _mm_nt(dr1, w['w_out'], 512, D_MODEL, tag + "out_dx")
    d_o, dlru, ds5, g['mix_norm_g'] = _mix_bwd(dmixed, sv['attn_o'], sv['lru'], sv['s5'], w['mix_norm_g'],
                                               tag + "mix_bwd")
    dy, dud, g['s5_d'], g['s5_w_glu'], g['s5_b_glu'] = _s5_out_bwd(
        sv['proj'], sv['y_pre'], ds5, w['s5_d'], w['s5_w_glu'], w['s5_b_glu'], tag + "s5_out_bwd")
    du, g['s5_lam_re'], g['s5_lam_im'], g['s5_bb_re'], g['s5_bb_im'], g['s5_cc_re'], g['s5_cc_im'] = \
        _s5_scan_bwd(sv['proj'], dy, dud, sv['x_re'], sv['x_im'], w['s5_bb_re'], w['s5_bb_im'],
                     w['s5_lam_re'], w['s5_lam_im'], w['s5_cc_re'], w['s5_cc_im'], tag + "s5_scan_bwd")
    (dxr, dgate, g['lru_conv_w'], g['lru_conv_b'], g['lru_wr'], g['lru_br'], g['lru_wi'], g['lru_bi'],
     g['lru_lambda']) = _lru_bwd(sv['proj'], dlru, w['lru_conv_w'], w['lru_conv_b'], w['lru_wr'],
                                 w['lru_br'], w['lru_wi'], w['lru_bi'], w['lru_lambda'], tag + "lru_bwd")
    dqkv = [_attn_bwd(sv['qkv'], sv['attn_o'], d_o, sv['attn_lse'], d, tag + "attn_bwd_d%d" % d)
            for d in DILATIONS]
    dproj = _dproj_assemble(dqkv, dxr, dgate, du, cos, sin, tag + "dproj")
    g['w_in'] = _mm_tn(sv['h_in'], dproj, 1024, D_IN, 512, tag + "in_dw")
    dh_in = _mm_nt(dproj, w['w_in'], 512, D_MODEL, tag + "in_dx", add=dr1, add_scale=ALPHA)
    return dh_in, g


def _s5_rep(a):
    return jnp.repeat(a, S5_C, axis=0)


def _prepare_layer(p, l):
    w = {}
    for n in ('w_in', 'w_out', 'w_down', 'w_up_g', 's5_w_glu'):
        w[n] = p[n].astype(BF16)
    w['ffn_conv_w'] = _ffn_interleave(p['ffn_conv_w'])
    w['ffn_conv_b'] = _ffn_interleave(p['ffn_conv_b'])[None, :]
    w['lru_conv_w'] = p['lru_conv_w']
    for n in ('lru_conv_b', 'lru_br', 'lru_bi', 'lru_lambda', 's5_b_glu', 'mix_norm_g',
              'ln1_g', 'ln1_b', 'ln2_g', 'ln2_b'):
        w[n] = p[n][None, :]
    w['lru_wr'] = _block_diag(p['lru_wr']).astype(BF16)
    w['lru_wi'] = _block_diag(p['lru_wi']).astype(BF16)
    w['s5_d'] = p['s5_d'].reshape(1, S5_W)
    disc_in = (_s5_rep(p['s5_a_re']), _s5_rep(p['s5_a_im']),
               _s5_rep(jnp.broadcast_to(p['s5_log_step'][:, None], (S5_G, S5_P))),
               jnp.swapaxes(p['s5_b_re'], 1, 2).reshape(S5_W, S5_P),
               jnp.swapaxes(p['s5_b_im'], 1, 2).reshape(S5_W, S5_P))
    ab_re, ab_im, bb_re, bb_im = _s5_disc_fwd(*disc_in, "l%d_s5_disc" % l)
    w['s5_disc_in'] = disc_in
    w['s5_lam_re'] = ab_re.reshape(S5_G, S5_C, S5_P)[:, 0, :].reshape(1, S5_STATES)
    w['s5_lam_im'] = ab_im.reshape(S5_G, S5_C, S5_P)[:, 0, :].reshape(1, S5_STATES)
    w['s5_bb_re'] = _block_diag(bb_re.reshape(S5_G, S5_C, S5_P)).astype(BF16)
    w['s5_bb_im'] = _block_diag(bb_im.reshape(S5_G, S5_C, S5_P)).astype(BF16)
    w['s5_cc_re'] = _block_diag(jnp.swapaxes(p['s5_c_re'], 1, 2)).astype(BF16)
    w['s5_cc_im'] = _block_diag(jnp.swapaxes(p['s5_c_im'], 1, 2)).astype(BF16)
    return w


def _finish_layer_grads(g, w, l):
    out = {}
    for n in ('w_in', 'w_out', 'w_down', 'w_up_g', 's5_w_glu', 'lru_conv_w'):
        out[n] = g[n]
    out['ffn_conv_w'] = _ffn_deinterleave(g['ffn_conv_w'])
    out['ffn_conv_b'] = _ffn_deinterleave(g['ffn_conv_b'])[0]
    for n in ('lru_conv_b', 'lru_br', 'lru_bi', 'lru_lambda', 's5_b_glu', 'mix_norm_g',
              'ln1_g', 'ln1_b', 'ln2_g', 'ln2_b'):
        out[n] = g[n][0]
    out['lru_wr'] = _block_diag_extract(g['lru_wr'], LRU_W // HEAD)
    out['lru_wi'] = _block_diag_extract(g['lru_wi'], LRU_W // HEAD)
    out['s5_d'] = g['s5_d'].reshape(S5_G, S5_C)
    out['s5_c_re'] = jnp.swapaxes(_block_diag_extract(g['s5_cc_re'], S5_G), 1, 2)
    out['s5_c_im'] = jnp.swapaxes(_block_diag_extract(g['s5_cc_im'], S5_G), 1, 2)
    rep = lambda v: _s5_rep(v.reshape(S5_G, S5_P)) * (1.0 / S5_C)
    cts = (rep(g['s5_lam_re']), rep(g['s5_lam_im']),
           _block_diag_extract(g['s5_bb_re'], S5_G).reshape(S5_W, S5_P),
           _block_diag_extract(g['s5_bb_im'], S5_G).reshape(S5_W, S5_P))
    da_re, da_im, dls, dbt_re, dbt_im = _s5_disc_bwd(*w['s5_disc_in'], cts, "l%d_s5_disc_bwd" % l)
    out['s5_a_re'] = da_re.reshape(S5_G, S5_C, S5_P).sum(axis=1)
    out['s5_a_im'] = da_im.reshape(S5_G, S5_C, S5_P).sum(axis=1)
    out['s5_log_step'] = dls.reshape(S5_G, S5_C * S5_P).sum(axis=1)
    out['s5_b_re'] = jnp.swapaxes(dbt_re.reshape(S5_G, S5_C, S5_P), 1, 2)
    out['s5_b_im'] = jnp.swapaxes(dbt_im.reshape(S5_G, S5_C, S5_P), 1, 2)
    return out


def _local_step(x, target, layers):
    s = x.shape[0]
    cos, sin = _rope_tables(s)
    ws = [_prepare_layer(layers[l], l) for l in range(DEPTH)]
    h = x
    saved = []
    for l in range(DEPTH):
        h, sv = _layer_fwd(h, ws[l], cos, sin, l)
        saved.append(sv)
    dh, loss_vec = _loss_head(h, target)
    grads = [None] * DEPTH
    for l in reversed(range(DEPTH)):
        dh, g = _layer_bwd(dh, saved[l], ws[l], cos, sin, l)
        grads[l] = _finish_layer_grads(g, ws[l], l)
    return loss_vec[0, 0], dh, grads


def kernel(x, w_in, lru_conv_w, lru_conv_b, lru_wr, lru_br, lru_wi, lru_bi, lru_lambda, s5_a_re, s5_a_im, s5_b_re, s5_b_im, s5_c_re, s5_c_im, s5_d, s5_log_step, s5_w_glu, s5_b_glu, mix_norm_g, w_out, ln1_g, ln1_b, w_up, ffn_conv_w, ffn_conv_b, w_down, ln2_g, ln2_b, loss_target, m_w_in, m_lru_conv_w, m_lru_conv_b, m_lru_wr, m_lru_br, m_lru_wi, m_lru_bi, m_lru_lambda, m_s5_a_re, m_s5_a_im, m_s5_b_re, m_s5_b_im, m_s5_c_re, m_s5_c_im, m_s5_d, m_s5_log_step, m_s5_w_glu, m_s5_b_glu, m_mix_norm_g, m_w_out, m_ln1_g, m_ln1_b, m_w_up, m_ffn_conv_w, m_ffn_conv_b, m_w_down, m_ln2_g, m_ln2_b, v_w_in, v_lru_conv_w, v_lru_conv_b, v_lru_wr, v_lru_br, v_lru_wi, v_lru_bi, v_lru_lambda, v_s5_a_re, v_s5_a_im, v_s5_b_re, v_s5_b_im, v_s5_c_re, v_s5_c_im, v_s5_d, v_s5_log_step, v_s5_w_glu, v_s5_b_glu, v_mix_norm_g, v_w_out, v_ln1_g, v_ln1_b, v_w_up, v_ffn_conv_w, v_ffn_conv_b, v_w_down, v_ln2_g, v_ln2_b):
    args = locals()
    wl = {n: args[n] for n in WEIGHTS}
    ml = {n: args['m_' + n] for n in WEIGHTS}
    vl = {n: args['v_' + n] for n in WEIGHTS}
    me = 4 * lax.axis_index("x") + 2 * lax.axis_index("y") + lax.axis_index("c")

    small_sizes = [int(wl[n].size) for n in SMALL_SHARDED]
    small_flat = _pad_to(jnp.concatenate([wl[n].reshape(-1) for n in SMALL_SHARDED]), 8 * 1024)
    small_all, = _all_gather([small_flat.reshape(-1, 1024)], "gather_small")
    small_all = small_all.reshape(N_DEV, -1)
    small_full, off = {}, 0
    for n, sz in zip(SMALL_SHARDED, small_sizes):
        small_full[n] = _gather_full(small_all[:, off:off + sz].reshape((N_DEV,) + wl[n].shape), SHARD_AXIS[n])
        off += sz
    layers = []
    for l in range(DEPTH):
        g_in, g_out, g_up, g_down = _all_gather([wl[n][l].astype(BF16) for n in BIG], "gather_l%d" % l)
        p = {n: wl[n][l] for n in REPLICATED}
        p.update({n: small_full[n][l] for n in SMALL_SHARDED})
        p['w_in'] = _gather_full(g_in, 1)
        p['w_out'] = g_out.reshape(D_MODEL, D_MODEL)
        p['w_down'] = g_down.reshape(D_FF, D_MODEL)
        p['w_up_g'] = g_up
        layers.append(p)

    loss_local, grad_x, grads = _local_step(x[0], loss_target[0], layers)
    loss = lax.psum(loss_local, AXES)

    results = {}
    big_prev = {n: None for n in BIG}
    for l in reversed(range(DEPTH)):
        g = grads[l]
        send = [_scatter_blocks(g['w_in'], 1), g['w_out'].reshape(N_DEV, D_MODEL // N_DEV, D_MODEL),
                g['w_up_g'], g['w_down'].reshape(N_DEV, D_FF // N_DEV, D_MODEL)]
        landed = _all_to_all(send, "scatter_l%d" % l)
        for n, ld in zip(BIG, landed):
            big_prev[n] = _adamw_sum(ld, wl[n], ml[n], vl[n], l, big_prev[n], "adamw_%s_l%d" % (n, l))
    for n in BIG:
        results['grad', n], results['delta', n], results['m', n], results['v', n] = big_prev[n]

    stacked = {n: jnp.stack([grads[l][n] for l in range(DEPTH)], axis=0) for n in SMALL_SHARDED + REPLICATED}
    rep_sizes = [int(wl[n].size) for n in REPLICATED]
    rep_per = -(-sum(rep_sizes) // (N_DEV * 1024)) * 1024

    def rep_flat(tree):
        return _pad_to(jnp.concatenate([tree[n].reshape(-1) for n in REPLICATED]), N_DEV * rep_per)

    rows = [_scatter_blocks(stacked[n], SHARD_AXIS[n]).reshape(N_DEV, -1) for n in SMALL_SHARDED]
    rows.append(rep_flat(stacked).reshape(N_DEV, rep_per))
    small_send = _pad_to(jnp.concatenate(rows, axis=1), 8 * 1024)
    n_own = small_send.shape[1]
    small_landed, = _all_to_all([small_send.reshape(N_DEV, n_own // 1024, 1024)], "scatter_small")

    def own_flat(tree):
        parts = [tree[n].reshape(-1) for n in SMALL_SHARDED]
        parts.append(lax.dynamic_slice(rep_flat(tree), (me * rep_per,), (rep_per,)))
        return _pad_to(jnp.concatenate(parts), 8 * 1024).reshape(1, n_own // 1024, 1024)

    small_res = _adamw_sum(small_landed, own_flat(wl), own_flat(ml), own_flat(vl), 0, None, "adamw_small")

    kinds = ('grad', 'delta', 'm', 'v')
    sh_total = sum(small_sizes)
    for kind, arr in zip(kinds, small_res):
        flat = arr.reshape(-1)
        off = 0
        for n, sz in zip(SMALL_SHARDED, small_sizes):
            results[kind, n] = flat[off:off + sz].reshape(wl[n].shape)
            off += sz
    rep_own = jnp.stack([a.reshape(-1)[sh_total:sh_total + rep_per] for a in small_res])
    rep_all, = _all_gather([rep_own.reshape(4 * rep_per // 1024, 1024)], "gather_replicated")
    rep_all = rep_all.reshape(N_DEV, 4, rep_per)
    for k, kind in enumerate(kinds):
        flat = rep_all[:, k, :].reshape(-1)
        off = 0
        for n, sz in zip(REPLICATED, rep_sizes):
            results[kind, n] = flat[off:off + sz].reshape(wl[n].shape)
            off += sz

    out = [loss, grad_x[None]]
    for kind in kinds:
        out.extend(results[kind, n] for n in WEIGHTS)
    return tuple(out)
```

```python
import functools
import math

import jax
import jax.numpy as jnp
from jax import lax
from jax.experimental import pallas as pl
from jax.experimental.pallas import tpu as pltpu

F32 = jnp.float32
BF16 = jnp.bfloat16

N_DEV = 8
DEPTH = 2
D_MODEL = 1024
ATTN_W = 384
LRU_W = 384
S5_W = 256
D_IN = 2176
D_FF = 3072
HEAD = 64
ATTN_BLK = 128
ATTN_TILE = 1024
DILATIONS = (1, 4, 16)
S5_G = 16
S5_P = 64
S5_C = 16
S5_STATES = S5_G * S5_P
LRU_C = 8.0
LRU_CONV = 4
FFN_CONV = 3
ROPE_THETA = 10000.0
ALPHA = (2 * DEPTH) ** 0.25
LN_EPS = 1e-5
RMS_EPS = 1e-6
ADAM_LR, ADAM_B1, ADAM_B2, ADAM_EPS, ADAM_WD, ADAM_STEP = 0.001, 0.9, 0.999, 1e-8, 0.01, 10

LANE = 128
SCAN_T = 256
FFN_CB = 2 * D_FF // N_DEV
VMEM_LIMIT = 56 * 1024 * 1024

AXES = ("x", "y", "c")

WEIGHTS = ['w_in', 'lru_conv_w', 'lru_conv_b', 'lru_wr', 'lru_br', 'lru_wi', 'lru_bi', 'lru_lambda',
           's5_a_re', 's5_a_im', 's5_b_re', 's5_b_im', 's5_c_re', 's5_c_im', 's5_d', 's5_log_step',
           's5_w_glu', 's5_b_glu', 'mix_norm_g', 'w_out', 'ln1_g', 'ln1_b', 'w_up', 'ffn_conv_w',
           'ffn_conv_b', 'w_down', 'ln2_g', 'ln2_b']
SHARD_AXIS = {'w_in': 2, 'lru_conv_w': 2, 's5_w_glu': 1, 'w_out': 1, 'w_up': 2, 'ffn_conv_w': 2, 'w_down': 1}
BIG = ['w_in', 'w_out', 'w_up', 'w_down']
SMALL_SHARDED = ['lru_conv_w', 'ffn_conv_w', 's5_w_glu']
REPLICATED = [n for n in WEIGHTS if n not in SHARD_AXIS]


def _cparams(sem=None):
    return pltpu.CompilerParams(dimension_semantics=sem, vmem_limit_bytes=VMEM_LIMIT)


def _ffn_dev(jb):
    return jb // 2 + (N_DEV // 2) * (jb % 2)


def _gelu(x):
    c = math.sqrt(2.0 / math.pi)
    t = jnp.tanh(c * (x + 0.044715 * (x * x * x)))
    return 0.5 * x * (1.0 + t)


def _gelu_grad(x):
    c = math.sqrt(2.0 / math.pi)
    x2 = x * x
    t = jnp.tanh(c * (x + 0.044715 * (x2 * x)))
    return 0.5 * (1.0 + t) + 0.5 * x * (1.0 - t * t) * (c * (1.0 + 3.0 * 0.044715 * x2))


def _sigmoid(x):
    return 1.0 / (1.0 + jnp.exp(-x))


def _log1p(x):
    u = 1.0 + x
    d = u - 1.0
    return jnp.where(d == 0.0, x, jnp.log(u) * (x / jnp.where(d == 0.0, 1.0, d)))


def _softplus(x):
    return jnp.maximum(x, 0.0) + _log1p(jnp.exp(-jnp.abs(x)))


def _expm1(x):
    return jnp.tanh(0.5 * x) * (jnp.exp(x) + 1.0)


def _dot(a, b):
    return jnp.dot(a.astype(BF16), b.astype(BF16), preferred_element_type=F32)


def _dot_nt(a, b):
    return lax.dot_general(a.astype(BF16), b.astype(BF16), (((1,), (1,)), ((), ())),
                           preferred_element_type=F32)


def _dot_tn(a, b):
    return lax.dot_general(a.astype(BF16), b.astype(BF16), (((0,), (0,)), ((), ())),
                           preferred_element_type=F32)


def _rows(shape):
    return lax.broadcasted_iota(jnp.int32, shape, 0)


def _shift_down(x, s, fill):
    r = pltpu.roll(x, s, axis=0)
    return jnp.where(_rows(x.shape) >= s, r, fill)


def _shift_up(x, s, fill):
    t = x.shape[0]
    r = pltpu.roll(x, t - s, axis=0)
    return jnp.where(_rows(x.shape) < t - s, r, fill)


def _shift_down_prev(x, s, prev8):
    if s == 0:
        return x
    t, l = x.shape
    r = pltpu.roll(x, s, axis=0)
    pr = pltpu.roll(prev8, s, axis=0)
    pad = jnp.concatenate([pr, jnp.zeros((t - 8, l), x.dtype)], axis=0)
    return jnp.where(_rows(x.shape) < s, pad, r)


def _shift_up_next(x, s, next8):
    if s == 0:
        return x
    t, l = x.shape
    r = pltpu.roll(x, t - s, axis=0)
    nx = pltpu.roll(next8, 8 - s, axis=0)
    pad = jnp.concatenate([jnp.zeros((t - 8, l), x.dtype), nx], axis=0)
    return jnp.where(_rows(x.shape) >= t - s, pad, r)


def _scan_fwd(a, x):
    t = x.shape[0]
    s = 1
    while s < t:
        x = x + a * _shift_down(x, s, 0.0)
        a = a * _shift_down(a, s, 1.0)
        s *= 2
    return a, x


def _scan_rev(a, x):
    t = x.shape[0]
    s = 1
    while s < t:
        x = x + a * _shift_up(x, s, 0.0)
        a = a * _shift_up(a, s, 1.0)
        s *= 2
    return a, x


def _cpowers(lr, li, n):
    out = [(lr, li)]
    for _ in range(n - 1):
        lr, li = lr * lr - li * li, 2.0 * lr * li
        out.append((lr, li))
    return out


def _cscan(xr, xi, pows, reverse):
    shift = _shift_up if reverse else _shift_down
    s = 1
    for pr, pi in pows:
        sr = shift(xr, s, 0.0)
        si = shift(xi, s, 0.0)
        xr, xi = xr + pr * sr - pi * si, xi + pr * si + pi * sr
        s *= 2
    return xr, xi


def _dep_args(dep):
    return ([], []) if dep is None else ([pl.BlockSpec(memory_space=pl.ANY)], [dep])


def _mm_nn(a, b, tm, tn, name, out_dtype=F32, dep=None):
    m, k = a.shape
    n = b.shape[1]

    def body(a_ref, b_ref, *rest):
        o_ref = rest[-1]
        o_ref[...] = _dot(a_ref[...], b_ref[...]).astype(out_dtype)

    dep_specs, dep_ops = _dep_args(dep)
    return pl.pallas_call(
        body, out_shape=jax.ShapeDtypeStruct((m, n), out_dtype), grid=(n // tn, m // tm),
        in_specs=[pl.BlockSpec((tm, k), lambda j, i: (i, 0)),
                  pl.BlockSpec((k, tn), lambda j, i: (0, j))] + dep_specs,
        out_specs=pl.BlockSpec((tm, tn), lambda j, i: (i, j)), name=name,
        compiler_params=_cparams(("parallel", "parallel")))(a, b, *dep_ops)


def _mm_nt(a, w, tm, tn, name, add=None, add_scale=1.0):
    m, k = a.shape
    n = w.shape[0]

    def body(*refs):
        if add is None:
            a_ref, w_ref, o_ref = refs
            o_ref[...] = _dot_nt(a_ref[...], w_ref[...])
        else:
            a_ref, w_ref, c_ref, o_ref = refs
            o_ref[...] = _dot_nt(a_ref[...], w_ref[...]) + add_scale * c_ref[...]

    in_specs = [pl.BlockSpec((tm, k), lambda j, i: (i, 0)), pl.BlockSpec((tn, k), lambda j, i: (j, 0))]
    args = [a, w]
    if add is not None:
        in_specs.append(pl.BlockSpec((tm, tn), lambda j, i: (i, j)))
        args.append(add)
    return pl.pallas_call(
        body, out_shape=jax.ShapeDtypeStruct((m, n), F32), grid=(n // tn, m // tm),
        in_specs=in_specs, out_specs=pl.BlockSpec((tm, tn), lambda j, i: (i, j)), name=name,
        compiler_params=_cparams(("parallel", "parallel")))(*args)


def _mm_tn(a, b, tm, tn, ts, name):
    s, m = a.shape
    n = b.shape[1]

    def body(a_ref, b_ref, o_ref):
        @pl.when(pl.program_id(2) == 0)
        def _():
            o_ref[...] = jnp.zeros_like(o_ref)
        o_ref[...] += _dot_tn(a_ref[...], b_ref[...])

    return pl.pallas_call(
        body, out_shape=jax.ShapeDtypeStruct((m, n), F32), grid=(m // tm, n // tn, s // ts),
        in_specs=[pl.BlockSpec((ts, tm), lambda i, j, k: (k, i)), pl.BlockSpec((ts, tn), lambda i, j, k: (k, j))],
        out_specs=pl.BlockSpec((tm, tn), lambda i, j, k: (i, j)), name=name,
        compiler_params=_cparams(("parallel", "parallel", "arbitrary")))(a, b)


def _mm_up(h, wg, tm, name):
    s, d = h.shape

    def body(a_ref, w_ref, o_ref):
        o_ref[...] = _dot(a_ref[...], w_ref[...])

    return pl.pallas_call(
        body, out_shape=jax.ShapeDtypeStruct((s, 2 * D_FF), F32), grid=(N_DEV, s // tm),
        in_specs=[pl.BlockSpec((tm, d), lambda j, i: (i, 0)),
                  pl.BlockSpec((None, d, FFN_CB), lambda j, i: (_ffn_dev(j), 0, 0))],
        out_specs=pl.BlockSpec((tm, FFN_CB), lambda j, i: (i, j)), name=name,
        compiler_params=_cparams(("parallel", "parallel")))(h, wg)


def _mm_up_dx(dup, wg, add, add_scale, tm, name):
    s = dup.shape[0]
    d = wg.shape[1]

    def body(a_ref, w_ref, c_ref, o_ref):
        @pl.when(pl.program_id(1) == 0)
        def _():
            o_ref[...] = add_scale * c_ref[...]
        o_ref[...] += _dot_nt(a_ref[...], w_ref[...])

    return pl.pallas_call(
        body, out_shape=jax.ShapeDtypeStruct((s, d), F32), grid=(s // tm, N_DEV),
        in_specs=[pl.BlockSpec((tm, FFN_CB), lambda i, j: (i, j)),
                  pl.BlockSpec((None, d, FFN_CB), lambda i, j: (_ffn_dev(j), 0, 0)),
                  pl.BlockSpec((tm, d), lambda i, j: (i, 0))],
        out_specs=pl.BlockSpec((tm, d), lambda i, j: (i, 0)), name=name,
        compiler_params=_cparams(("parallel", "arbitrary")))(dup, wg, add)


def _mm_up_dw(h, dup, ts, name):
    s, d = h.shape

    def body(a_ref, b_ref, o_ref):
        @pl.when(pl.program_id(1) == 0)
        def _():
            o_ref[...] = jnp.zeros_like(o_ref)
        o_ref[...] += _dot_tn(a_ref[...], b_ref[...])

    return pl.pallas_call(
        body, out_shape=jax.ShapeDtypeStruct((N_DEV, d, FFN_CB), F32), grid=(N_DEV, s // ts),
        in_specs=[pl.BlockSpec((ts, d), lambda j, k: (k, 0)), pl.BlockSpec((ts, FFN_CB), lambda j, k: (k, j))],
        out_specs=pl.BlockSpec((None, d, FFN_CB), lambda j, k: (_ffn_dev(j), 0, 0)), name=name,
        compiler_params=_cparams(("parallel", "arbitrary")))(h, dup)


def _ln_fwd(a, b, g, bias, name):
    s, d = a.shape
    tm = 512

    def body(a_ref, b_ref, g_ref, bias_ref, r_ref, h_ref):
        r = ALPHA * a_ref[...] + b_ref[...]
        mu = jnp.mean(r, axis=-1, keepdims=True)
        xc = r - mu
        var = jnp.mean(xc * xc, axis=-1, keepdims=True)
        r_ref[...] = r
        h_ref[...] = xc * lax.rsqrt(var + LN_EPS) * g_ref[...] + bias_ref[...]

    row = pl.BlockSpec((tm, d), lambda i: (i, 0))
    vec = pl.BlockSpec((1, d), lambda i: (0, 0))
    return pl.pallas_call(
        body, out_shape=(jax.ShapeDtypeStruct((s, d), F32), jax.ShapeDtypeStruct((s, d), F32)),
        grid=(s // tm,), in_specs=[row, row, vec, vec], out_specs=(row, row), name=name,
        compiler_params=_cparams(("parallel",)))(a, b, g, bias)


def _ln_bwd(r, dh, g, name, dep=None):
    s, d = r.shape
    tm = 512

    def body(r_ref, dh_ref, g_ref, *rest):
        dr_ref, dg_ref, db_ref = rest[-3:]

        @pl.when(pl.program_id(0) == 0)
        def _():
            dg_ref[...] = jnp.zeros_like(dg_ref)
            db_ref[...] = jnp.zeros_like(db_ref)
        rr = r_ref[...]
        dh_ = dh_ref[...]
        mu = jnp.mean(rr, axis=-1, keepdims=True)
        xc = rr - mu
        var = jnp.mean(xc * xc, axis=-1, keepdims=True)
        rstd = lax.rsqrt(var + LN_EPS)
        xh = xc * rstd
        dxh = dh_ * g_ref[...]
        m1 = jnp.mean(dxh, axis=-1, keepdims=True)
        m2 = jnp.mean(dxh * xh, axis=-1, keepdims=True)
        dr_ref[...] = rstd * (dxh - m1 - xh * m2)
        dg_ref[...] += jnp.sum(dh_ * xh, axis=0, keepdims=True)
        db_ref[...] += jnp.sum(dh_, axis=0, keepdims=True)

    row = pl.BlockSpec((tm, d), lambda i: (i, 0))
    vec = pl.BlockSpec((1, d), lambda i: (0, 0))
    dep_specs, dep_ops = _dep_args(dep)
    return pl.pallas_call(
        body, out_shape=(jax.ShapeDtypeStruct((s, d), F32), jax.ShapeDtypeStruct((1, d), F32),
                         jax.ShapeDtypeStruct((1, d), F32)),
        grid=(s // tm,), in_specs=[row, row, vec] + dep_specs, out_specs=(row, vec, vec), name=name,
        compiler_params=_cparams(("arbitrary",)))(r, dh, g, *dep_ops)


def _loss_head(y, target):
    s, d = y.shape
    tm = 512

    def body(y_ref, t_ref, dy_ref, l_ref):
        @pl.when(pl.program_id(0) == 0)
        def _():
            l_ref[...] = jnp.zeros_like(l_ref)
        e = y_ref[...] - t_ref[...]
        dy_ref[...] = e * (1.0 / d)
        part = 0.5 * jnp.sum(jnp.mean(e * e, axis=-1, keepdims=True), axis=0, keepdims=True)
        l_ref[...] += jnp.broadcast_to(part, l_ref.shape)

    row = pl.BlockSpec((tm, d), lambda i: (i, 0))
    return pl.pallas_call(
        body, out_shape=(jax.ShapeDtypeStruct((s, d), F32), jax.ShapeDtypeStruct((1, LANE), F32)),
        grid=(s // tm,), in_specs=[row, row], out_specs=(row, pl.BlockSpec((1, LANE), lambda i: (0, 0))),
        name="loss_head", compiler_params=_cparams(("arbitrary",)))(y, target)


def _rope_tables(s):
    half = HEAD // 2
    pos = jnp.arange(s, dtype=F32)
    inv = ROPE_THETA ** (-jnp.arange(half, dtype=F32) * 2.0 / HEAD)
    ang = pos[:, None] * inv[None, :]
    cos, sin = jnp.cos(ang), jnp.sin(ang)
    cos = jnp.concatenate([cos, cos, cos, cos], axis=1)
    sin = jnp.concatenate([-sin, sin, -sin, sin], axis=1)
    return cos, sin


def _rotate(x, cos, sin):
    lane = lax.broadcasted_iota(jnp.int32, x.shape, 1)
    partner = jnp.where((lane % HEAD) < HEAD // 2, pltpu.roll(x, LANE - HEAD // 2, axis=1),
                        pltpu.roll(x, HEAD // 2, axis=1))
    return x * cos + partner * sin


def _rope_fwd(proj, cos, sin, name):
    s = proj.shape[0]
    tm = 512
    w = 3 * ATTN_W

    def body(p_ref, c_ref, s_ref, o_ref):
        c, sn = c_ref[...], s_ref[...]
        for j in range(w // LANE):
            x = p_ref[:, j * LANE:(j + 1) * LANE]
            if j < 2 * ATTN_W // LANE:
                x = _rotate(x, c, sn)
            o_ref[:, j * LANE:(j + 1) * LANE] = x.astype(BF16)

    tab = pl.BlockSpec((tm, LANE), lambda i: (i, 0))
    return pl.pallas_call(
        body, out_shape=jax.ShapeDtypeStruct((s, w), BF16), grid=(s // tm,),
        in_specs=[pl.BlockSpec((tm, w), lambda i: (i, 0)), tab, tab],
        out_specs=pl.BlockSpec((tm, w), lambda i: (i, 0)), name=name,
        compiler_params=_cparams(("parallel",)))(proj, cos, sin)


def _dproj_assemble(dqkv_list, dxr, dgate, du, cos, sin, name):
    s = dxr.shape[0]
    tm = 512
    nq = 3 * ATTN_W // LANE

    def body(*refs):
        br = refs[:9]
        dxr_ref, dg_ref, du_ref, c_ref, s_ref, o_ref = refs[9:]
        c, sn = c_ref[...], -s_ref[...]
        for j in range(nq):
            part, jj = divmod(j, ATTN_W // LANE)
            cols = slice(jj * LANE, (jj + 1) * LANE)
            x = br[part][:, cols] + br[3 + part][:, cols] + br[6 + part][:, cols]
            if part < 2:
                x = _rotate(x, c, sn)
            o_ref[:, j * LANE:(j + 1) * LANE] = x.astype(BF16)
        o_ref[:, 3 * ATTN_W:3 * ATTN_W + LRU_W] = dxr_ref[...].astype(BF16)
        o_ref[:, 3 * ATTN_W + LRU_W:3 * ATTN_W + 2 * LRU_W] = dg_ref[...].astype(BF16)
        o_ref[:, 3 * ATTN_W + 2 * LRU_W:] = du_ref[...].astype(BF16)

    a_spec = pl.BlockSpec((tm, ATTN_W), lambda i: (i, 0))
    tab = pl.BlockSpec((tm, LANE), lambda i: (i, 0))
    ordered = [dqkv_list[b][p] for b in range(3) for p in range(3)]
    return pl.pallas_call(
        body, out_shape=jax.ShapeDtypeStruct((s, D_IN), BF16), grid=(s // tm,),
        in_specs=[a_spec] * 9 + [a_spec, a_spec, pl.BlockSpec((tm, S5_W), lambda i: (i, 0)), tab, tab],
        out_specs=pl.BlockSpec((tm, D_IN), lambda i: (i, 0)), name=name,
        compiler_params=_cparams(("parallel",)))(*ordered, dxr, dgate, du, cos, sin)


def _attn_tiles(s, d):
    m = s // d
    tq = min(m, ATTN_TILE)
    return m, tq, tq // ATTN_BLK


def _band_mask(qb):
    qi = lax.broadcasted_iota(jnp.int32, (ATTN_BLK, 2 * ATTN_BLK), 0)
    ki = lax.broadcasted_iota(jnp.int32, (ATTN_BLK, 2 * ATTN_BLK), 1)
    dist = qi + ATTN_BLK - ki
    return (dist >= 0) & (dist <= ATTN_BLK) & ((ki >= ATTN_BLK) | (qb > 0))


def _head_cols(h):
    return (slice(h * HEAD, (h + 1) * HEAD), slice(ATTN_W + h * HEAD, ATTN_W + (h + 1) * HEAD),
            slice(2 * ATTN_W + h * HEAD, 2 * ATTN_W + (h + 1) * HEAD))


def _attn_fwd(qkv, d, name):
    s = qkv.shape[0]
    w3 = 3 * ATTN_W
    m, tq, n = _attn_tiles(s, d)
    qv = qkv.reshape(m, d * w3)
    scale = HEAD ** -0.5

    def body(x_ref, p_ref, o_ref, l_ref):
        b = pl.program_id(1)

        def block(i, first):
            r0 = 0 if first else pl.multiple_of(i * ATTN_BLK, ATTN_BLK)
            rows = pl.ds(r0, ATTN_BLK)
            valid = _band_mask(b * n + i)
            if not first:
                krows = pl.ds(pl.multiple_of(i * ATTN_BLK - ATTN_BLK, ATTN_BLK), 2 * ATTN_BLK)
            for h in range(ATTN_W // HEAD):
                qs, ks, vs = _head_cols(h)
                q = x_ref[rows, qs]
                if first:
                    k = jnp.concatenate([p_ref[:, ks], x_ref[0:ATTN_BLK, ks]], axis=0)
                    v = jnp.concatenate([p_ref[:, vs], x_ref[0:ATTN_BLK, vs]], axis=0)
                else:
                    k = x_ref[krows, ks]
                    v = x_ref[krows, vs]
                sc = jnp.where(valid, _dot_nt(q, k) * scale, -1e30)
                mx = jnp.max(sc, axis=-1, keepdims=True)
                p = jnp.exp(sc - mx)
                l = jnp.sum(p, axis=-1, keepdims=True)
                o_ref[rows, qs] = _dot(p, v) / l
                l_ref[rows, qs] = jnp.broadcast_to(mx + jnp.log(l), (ATTN_BLK, HEAD))

        block(0, True)
        if n > 1:
            def loop(i, carry):
                block(i, False)
                return carry
            lax.fori_loop(1, n, loop, 0)

    shp = jax.ShapeDtypeStruct((m, d * ATTN_W), F32)
    ospec = pl.BlockSpec((tq, ATTN_W), lambda c, b: (b, c))
    out, lse = pl.pallas_call(
        body, out_shape=(shp, shp), grid=(d, m // tq),
        in_specs=[pl.BlockSpec((tq, w3), lambda c, b: (b, c)),
                  pl.BlockSpec((ATTN_BLK, w3), lambda c, b: (jnp.maximum(b * n - 1, 0), c))],
        out_specs=(ospec, ospec), name=name,
        compiler_params=_cparams(("parallel", "parallel")))(qv, qv)
    return out.reshape(s, ATTN_W), lse.reshape(s, ATTN_W)


def _attn_bwd(qkv, o_all, do_all, lse_all, d, name):
    s = qkv.shape[0]
    w3 = 3 * ATTN_W
    m, tq, n = _attn_tiles(s, d)
    nb = m // ATTN_BLK
    qv = qkv.reshape(m, d * w3)
    view = lambda t: t.reshape(m, d * ATTN_W)
    scale = HEAD ** -0.5

    def body(x_ref, p_ref, nx_ref, o_ref, do_ref, l_ref, on_ref, don_ref, ln_ref, dq_ref, dk_ref, dv_ref):
        b = pl.program_id(1)
        dk_ref[...] = jnp.zeros_like(dk_ref)
        dv_ref[...] = jnp.zeros_like(dv_ref)

        def grads(q, k, v, o, do, lse, valid):
            sc = jnp.where(valid, _dot_nt(q, k) * scale, -1e30)
            p = jnp.exp(sc - lse)
            delta = jnp.sum(do * o, axis=-1, keepdims=True)
            return p, p * (_dot_nt(do, v) - delta) * scale

        def block(i, first):
            r0 = 0 if first else pl.multiple_of(i * ATTN_BLK, ATTN_BLK)
            rows = pl.ds(r0, ATTN_BLK)
            valid = _band_mask(b * n + i)
            if not first:
                krows = pl.ds(pl.multiple_of(i * ATTN_BLK - ATTN_BLK, ATTN_BLK), 2 * ATTN_BLK)
            for h in range(ATTN_W // HEAD):
                qs, ks, vs = _head_cols(h)
                q = x_ref[rows, qs]
                do = do_ref[rows, qs]
                if first:
                    k = jnp.concatenate([p_ref[:, ks], x_ref[0:ATTN_BLK, ks]], axis=0)
                    v = jnp.concatenate([p_ref[:, vs], x_ref[0:ATTN_BLK, vs]], axis=0)
                else:
                    k = x_ref[krows, ks]
                    v = x_ref[krows, vs]
                p, ds = grads(q, k, v, o_ref[rows, qs], do, l_ref[rows, qs][:, 0:1], valid)
                dq_ref[rows, qs] = _dot(ds, k)
                if first:
                    dk_ref[0:ATTN_BLK, qs] += _dot_tn(ds[:, ATTN_BLK:], q)
                    dv_ref[0:ATTN_BLK, qs] += _dot_tn(p[:, ATTN_BLK:], do)
                else:
                    dk_ref[krows, qs] += _dot_tn(ds, q)
                    dv_ref[krows, qs] += _dot_tn(p, do)

        block(0, True)
        if n > 1:
            def loop(i, carry):
                block(i, False)
                return carry
            lax.fori_loop(1, n, loop, 0)

        last = slice((n - 1) * ATTN_BLK, n * ATTN_BLK)
        qi = lax.broadcasted_iota(jnp.int32, (ATTN_BLK, ATTN_BLK), 0)
        ki = lax.broadcasted_iota(jnp.int32, (ATTN_BLK, ATTN_BLK), 1)
        valid_next = (qi <= ki) & ((b + 1) * n < nb)
        for h in range(ATTN_W // HEAD):
            qs, ks, vs = _head_cols(h)
            q = nx_ref[:, qs]
            do = don_ref[:, qs]
            p, ds = grads(q, x_ref[last, ks], x_ref[last, vs], on_ref[:, qs], do, ln_ref[:, qs][:, 0:1],
                          valid_next)
            dk_ref[last, qs] += _dot_tn(ds, q)
            dv_ref[last, qs] += _dot_tn(p, do)

    nxt = lambda b: jnp.minimum((b + 1) * n, nb - 1)
    xs = pl.BlockSpec((tq, w3), lambda c, b: (b, c))
    xp = pl.BlockSpec((ATTN_BLK, w3), lambda c, b: (jnp.maximum(b * n - 1, 0), c))
    xn = pl.BlockSpec((ATTN_BLK, w3), lambda c, b: (nxt(b), c))
    a = pl.BlockSpec((tq, ATTN_W), lambda c, b: (b, c))
    an = pl.BlockSpec((ATTN_BLK, ATTN_W), lambda c, b: (nxt(b), c))
    shp = jax.ShapeDtypeStruct((m, d * ATTN_W), F32)
    ov, dov, lv = view(o_all), view(do_all), view(lse_all)
    dq, dk, dv = pl.pallas_call(
        body, out_shape=(shp, shp, shp), grid=(d, m // tq),
        in_specs=[xs, xp, xn, a, a, a, an, an, an], out_specs=(a, a, a), name=name,
        compiler_params=_cparams(("parallel", "parallel")))(qv, qv, qv, ov, dov, lv, ov, dov, lv)
    return dq.reshape(s, ATTN_W), dk.reshape(s, ATTN_W), dv.reshape(s, ATTN_W)


def _rms(x, g):
    ms = jnp.mean(x * x, axis=-1, keepdims=True)
    return x * lax.rsqrt(ms + RMS_EPS) * g


def _rms_bwd(x, g, dy):
    ms = jnp.mean(x * x, axis=-1, keepdims=True)
    r = lax.rsqrt(ms + RMS_EPS)
    dyg = dy * g
    dx = r * dyg - x * (r * r * r) * jnp.mean(x * dyg, axis=-1, keepdims=True)
    return dx, dy * x * r


def _mix_fwd(outs, lses, lru, s5, g, name):
    s = lru.shape[0]
    tm = 256

    def body(o1, o2, o3, l1, l2, l3, lru_ref, s5_ref, g_ref, mixed_ref, o_ref, lse_ref):
        a1, a2, a3 = l1[...], l2[...], l3[...]
        mx = jnp.maximum(jnp.maximum(a1, a2), a3)
        e1, e2, e3 = jnp.exp(a1 - mx), jnp.exp(a2 - mx), jnp.exp(a3 - mx)
        den = e1 + e2 + e3
        o = (e1 * o1[...] + e2 * o2[...] + e3 * o3[...]) / den
        o_ref[...] = o
        lse_ref[...] = mx + jnp.log(den)
        gg = g_ref[...]
        mixed_ref[:, :ATTN_W] = _rms(o, gg[:, :ATTN_W]).astype(BF16)
        mixed_ref[:, ATTN_W:ATTN_W + LRU_W] = _rms(lru_ref[...], gg[:, ATTN_W:ATTN_W + LRU_W]).astype(BF16)
        mixed_ref[:, ATTN_W + LRU_W:] = _rms(s5_ref[...], gg[:, ATTN_W + LRU_W:]).astype(BF16)

    a = pl.BlockSpec((tm, ATTN_W), lambda i: (i, 0))
    s5s = pl.BlockSpec((tm, S5_W), lambda i: (i, 0))
    full = pl.BlockSpec((tm, D_MODEL), lambda i: (i, 0))
    vec = pl.BlockSpec((1, D_MODEL), lambda i: (0, 0))
    return pl.pallas_call(
        body, out_shape=(jax.ShapeDtypeStruct((s, D_MODEL), BF16), jax.ShapeDtypeStruct((s, ATTN_W), F32),
                         jax.ShapeDtypeStruct((s, ATTN_W), F32)),
        grid=(s // tm,), in_specs=[a] * 6 + [a, s5s, vec], out_specs=(full, a, a), name=name,
        compiler_params=_cparams(("parallel",)))(*outs, *lses, lru, s5, g)


def _mix_bwd(dmixed, o, lru, s5, g, name):
    s = lru.shape[0]
    tm = 256

    def body(dm_ref, o_ref, lru_ref, s5_ref, g_ref, do_ref, dlru_ref, ds5_ref, dg_ref):
        @pl.when(pl.program_id(0) == 0)
        def _():
            dg_ref[...] = jnp.zeros_like(dg_ref)
        gg = g_ref[...]
        dm = dm_ref[...]
        dx, dgr = _rms_bwd(o_ref[...], gg[:, :ATTN_W], dm[:, :ATTN_W])
        do_ref[...] = dx
        dg_ref[:, :ATTN_W] += jnp.sum(dgr, axis=0, keepdims=True)
        dx, dgr = _rms_bwd(lru_ref[...], gg[:, ATTN_W:ATTN_W + LRU_W], dm[:, ATTN_W:ATTN_W + LRU_W])
        dlru_ref[...] = dx
        dg_ref[:, ATTN_W:ATTN_W + LRU_W] += jnp.sum(dgr, axis=0, keepdims=True)
        dx, dgr = _rms_bwd(s5_ref[...], gg[:, ATTN_W + LRU_W:], dm[:, ATTN_W + LRU_W:])
        ds5_ref[...] = dx
        dg_ref[:, ATTN_W + LRU_W:] += jnp.sum(dgr, axis=0, keepdims=True)

    a = pl.BlockSpec((tm, ATTN_W), lambda i: (i, 0))
    s5s = pl.BlockSpec((tm, S5_W), lambda i: (i, 0))
    full = pl.BlockSpec((tm, D_MODEL), lambda i: (i, 0))
    vec = pl.BlockSpec((1, D_MODEL), lambda i: (0, 0))
    return pl.pallas_call(
        body, out_shape=(jax.ShapeDtypeStruct((s, ATTN_W), F32), jax.ShapeDtypeStruct((s, LRU_W), F32),
                         jax.ShapeDtypeStruct((s, S5_W), F32), jax.ShapeDtypeStruct((1, D_MODEL), F32)),
        grid=(s // tm,), in_specs=[full, a, a, s5s, vec], out_specs=(a, a, s5s, vec), name=name,
        compiler_params=_cparams(("arbitrary",)))(dmixed, o, lru, s5, g)


def _lru_gate_math(xc, pre_r, pre_i, lam):
    r = _sigmoid(pre_r)
    i = _sigmoid(pre_i)
    log_a = -LRU_C * r * _softplus(-lam)
    a = jnp.exp(log_a)
    u = jnp.sqrt(-_expm1(2.0 * log_a)) * (i * xc)
    return a, u


def _lru_conv(x, prev8, cw, cb):
    y = cb + cw[LRU_CONV - 1:LRU_CONV, :] * x
    for k in range(LRU_CONV - 1):
        y = y + cw[k:k + 1, :] * _shift_down_prev(x, LRU_CONV - 1 - k, prev8)
    return y


def _lru_specs(s):
    xo = 3 * ATTN_W // LANE
    go = xo + LRU_W // LANE
    xr = pl.BlockSpec((s, LANE), lambda j: (0, xo + j))
    gt = pl.BlockSpec((s, LANE), lambda j: (0, go + j))
    cw = pl.BlockSpec((LRU_CONV, LANE), lambda j: (0, j))
    vec = pl.BlockSpec((1, LANE), lambda j: (0, j))
    wbd = pl.BlockSpec((LANE, LANE), lambda j: (j, j))
    col = pl.BlockSpec((s, LANE), lambda j: (0, j))
    return xr, gt, cw, vec, wbd, col


def _lru_fwd(proj, cw, cb, wr, br, wi, bi, lam, name):
    s = proj.shape[0]
    t = SCAN_T

    def body(xr_ref, gt_ref, cw_ref, cb_ref, wr_ref, br_ref, wi_ref, bi_ref, lam_ref, o_ref):
        cwv, cbv, lamv = cw_ref[...], cb_ref[...], lam_ref[...]
        wrv, wiv, brv, biv = wr_ref[...], wi_ref[...], br_ref[...], bi_ref[...]

        def chunk(c, carry):
            h_c, prev8 = carry
            rows = pl.ds(pl.multiple_of(c * t, t), t)
            x = xr_ref[rows, :]
            xc = _lru_conv(x, prev8, cwv, cbv)
            a, u = _lru_gate_math(xc, _dot(xc, wrv) + brv, _dot(xc, wiv) + biv, lamv)
            acum, hloc = _scan_fwd(a, u)
            h = hloc + acum * h_c
            o_ref[rows, :] = h * _gelu(gt_ref[rows, :])
            return h[t - 1:t, :], x[t - 8:t, :]

        lax.fori_loop(0, s // t, chunk, (jnp.zeros((1, LANE), F32), jnp.zeros((8, LANE), F32)))

    xr, gt, cws, vec, wbd, col = _lru_specs(s)
    return pl.pallas_call(
        body, out_shape=jax.ShapeDtypeStruct((s, LRU_W), F32), grid=(LRU_W // LANE,),
        in_specs=[xr, gt, cws, vec, wbd, vec, wbd, vec, vec], out_specs=col, name=name,
        compiler_params=_cparams(("parallel",)))(proj, proj, cw, cb, wr, br, wi, bi, lam)


def _lru_bwd(proj, dout, cw, cb, wr, br, wi, bi, lam, name):
    s = proj.shape[0]
    t = SCAN_T
    nc = s // t

    def body(xr_ref, gt_ref, do_ref, cw_ref, cb_ref, wr_ref, br_ref, wi_ref, bi_ref, lam_ref,
             dxr_ref, dgt_ref, dcw_ref, dcb_ref, dwr_ref, dbr_ref, dwi_ref, dbi_ref, dlam_ref,
             xc_s, a_s, h_s):
        cwv, cbv, lamv = cw_ref[...], cb_ref[...], lam_ref[...]
        wrv, wiv, brv, biv = wr_ref[...], wi_ref[...], br_ref[...], bi_ref[...]

        def fchunk(c, carry):
            h_c, prev8 = carry
            rows = pl.ds(pl.multiple_of(c * t, t), t)
            x = xr_ref[rows, :]
            xc = _lru_conv(x, prev8, cwv, cbv)
            a, u = _lru_gate_math(xc, _dot(xc, wrv) + brv, _dot(xc, wiv) + biv, lamv)
            acum, hloc = _scan_fwd(a, u)
            h = hloc + acum * h_c
            xc_s[rows, :] = xc
            a_s[rows, :] = a
            h_s[rows, :] = h
            return h[t - 1:t, :], x[t - 8:t, :]

        lax.fori_loop(0, nc, fchunk, (jnp.zeros((1, LANE), F32), jnp.zeros((8, LANE), F32)))

        z1 = jnp.zeros((1, LANE), F32)
        zw = jnp.zeros((LANE, LANE), F32)

        def bchunk(ci, carry):
            g_next, a_next, dxc_next8, dcw, dcb, dwr, dbr, dwi, dbi, dlam = carry
            c = nc - 1 - ci
            t0 = pl.multiple_of(c * t, t)
            rows = pl.ds(t0, t)
            before = pl.ds(pl.multiple_of(jnp.maximum(t0 - 8, 0), 8), 8)
            has_prev = (c > 0).astype(F32)
            x, gt, do = xr_ref[rows, :], gt_ref[rows, :], do_ref[rows, :]
            xc, a, h = xc_s[rows, :], a_s[rows, :], h_s[rows, :]
            prev8_x = xr_ref[before, :] * has_prev
            prev8_h = h_s[before, :] * has_prev
            dgt_ref[rows, :] = do * h * _gelu_grad(gt)
            dh = do * _gelu(gt)
            a_plus = _shift_up_next(a, 1, jnp.broadcast_to(a_next, (8, LANE)))
            acum, gloc = _scan_rev(a_plus, dh)
            g = gloc + acum * g_next
            da = g * _shift_down_prev(h, 1, prev8_h)
            pre_r = _dot(xc, wrv) + brv
            pre_i = _dot(xc, wiv) + biv
            _, vjp = jax.vjp(_lru_gate_math, xc, pre_r, pre_i, lamv)
            dxc, dpre_r, dpre_i, dlam_c = vjp((da, g))
            dxc = dxc + _dot_nt(dpre_r, wrv) + _dot_nt(dpre_i, wiv)
            dx = cwv[LRU_CONV - 1:LRU_CONV, :] * dxc
            dcw_rows = [None] * LRU_CONV
            dcw_rows[LRU_CONV - 1] = jnp.sum(dxc * x, axis=0, keepdims=True)
            for k in range(LRU_CONV - 1):
                sh = LRU_CONV - 1 - k
                dx = dx + cwv[k:k + 1, :] * _shift_up_next(dxc, sh, dxc_next8)
                dcw_rows[k] = jnp.sum(dxc * _shift_down_prev(x, sh, prev8_x), axis=0, keepdims=True)
            dxr_ref[rows, :] = dx
            return (g[0:1, :], a[0:1, :], dxc[0:8, :],
                    dcw + jnp.concatenate(dcw_rows, axis=0),
                    dcb + jnp.sum(dxc, axis=0, keepdims=True),
                    dwr + _dot_tn(xc, dpre_r), dbr + jnp.sum(dpre_r, axis=0, keepdims=True),
                    dwi + _dot_tn(xc, dpre_i), dbi + jnp.sum(dpre_i, axis=0, keepdims=True),
                    dlam + dlam_c)

        init = (z1, z1, jnp.zeros((8, LANE), F32), jnp.zeros((LRU_CONV, LANE), F32), z1, zw, z1, zw, z1, z1)
        res = lax.fori_loop(0, nc, bchunk, init)
        dcw_ref[...] = res[3]
        dcb_ref[...] = res[4]
        dwr_ref[...] = res[5]
        dbr_ref[...] = res[6]
        dwi_ref[...] = res[7]
        dbi_ref[...] = res[8]
        dlam_ref[...] = res[9]

    xr, gt, cws, vec, wbd, col = _lru_specs(s)
    vshape = jax.ShapeDtypeStruct((1, LRU_W), F32)
    wshape = jax.ShapeDtypeStruct((LRU_W, LRU_W), F32)
    return pl.pallas_call(
        body,
        out_shape=(jax.ShapeDtypeStruct((s, LRU_W), F32), jax.ShapeDtypeStruct((s, LRU_W), F32),
                   jax.ShapeDtypeStruct((LRU_CONV, LRU_W), F32), vshape, wshape, vshape, wshape, vshape, vshape),
        grid=(LRU_W // LANE,),
        in_specs=[xr, gt, col, cws, vec, wbd, vec, wbd, vec, vec],
        out_specs=(col, col, cws, vec, wbd, vec, wbd, vec, vec),
        scratch_shapes=[pltpu.VMEM((s, LANE), F32)] * 3, name=name,
        compiler_params=_cparams(("parallel",)))(proj, proj, dout, cw, cb, wr, br, wi, bi, lam)


def _s5_disc_math(a_re, a_im, log_step, bt_re, bt_im):
    step = jnp.exp(log_step)
    dt_re, dt_im = step * a_re, step * a_im
    mag = jnp.exp(dt_re)
    ab_re, ab_im = mag * jnp.cos(dt_im), mag * jnp.sin(dt_im)
    z_re, z_im = ab_re - 1.0, ab_im
    den = a_re * a_re + a_im * a_im
    f_re = (z_re * a_re + z_im * a_im) / den
    f_im = (z_im * a_re - z_re * a_im) / den
    bb_re = f_re * bt_re - f_im * bt_im
    bb_im = f_re * bt_im + f_im * bt_re
    return ab_re, ab_im, bb_re, bb_im


def _s5_disc_fwd(a_re, a_im, log_step, bt_re, bt_im, name):
    def body(ar, ai, ls, br, bi, o1, o2, o3, o4):
        r = _s5_disc_math(ar[...], ai[...], ls[...], br[...], bi[...])
        o1[...], o2[...], o3[...], o4[...] = r

    shp = jax.ShapeDtypeStruct(a_re.shape, F32)
    return pl.pallas_call(body, out_shape=(shp,) * 4, name=name)(a_re, a_im, log_step, bt_re, bt_im)


def _s5_disc_bwd(a_re, a_im, log_step, bt_re, bt_im, cts, name):
    def body(ar, ai, ls, br, bi, c1, c2, c3, c4, o1, o2, o3, o4, o5):
        _, vjp = jax.vjp(_s5_disc_math, ar[...], ai[...], ls[...], br[...], bi[...])
        r = vjp((c1[...], c2[...], c3[...], c4[...]))
        o1[...], o2[...], o3[...], o4[...], o5[...] = r

    shp = jax.ShapeDtypeStruct(a_re.shape, F32)
    return pl.pallas_call(body, out_shape=(shp,) * 5, name=name)(a_re, a_im, log_step, bt_re, bt_im, *cts)


def _s5_u_specs(s):
    uo = (3 * ATTN_W + 2 * LRU_W) // LANE
    return (pl.BlockSpec((s, LANE), lambda j: (0, uo)), pl.BlockSpec((s, LANE), lambda j: (0, uo + 1)))


def _s5_scan_fwd(proj, b_re, b_im, lam_re, lam_im, c_re, c_im, name):
    s = proj.shape[0]
    t = SCAN_T
    nlog = int(math.log2(t))

    def body(u0_ref, u1_ref, bre_ref, bim_ref, lre_ref, lim_ref, cre_ref, cim_ref, xre_ref, xim_ref, y_ref):
        @pl.when(pl.program_id(0) == 0)
        def _():
            y_ref[...] = jnp.zeros_like(y_ref)
        lr, li = lre_ref[...], lim_ref[...]
        pows = _cpowers(lr, li, nlog)
        first = _rows((t, LANE)) == 0
        tab_r, tab_i = _cscan(jnp.where(first, lr, 0.0), jnp.where(first, li, 0.0), pows, False)
        bre, bim, cre, cim = bre_ref[...], bim_ref[...], cre_ref[...], cim_ref[...]

        def chunk(c, carry):
            cr, ci = carry
            rows = pl.ds(pl.multiple_of(c * t, t), t)
            u = jnp.concatenate([u0_ref[rows, :], u1_ref[rows, :]], axis=1).astype(BF16)
            xr, xi = _cscan(_dot(u, bre), _dot(u, bim), pows, False)
            xr, xi = xr + tab_r * cr - tab_i * ci, xi + tab_r * ci + tab_i * cr
            xre_ref[rows, :] = xr
            xim_ref[rows, :] = xi
            y_ref[rows, :] += _dot(xr, cre) - _dot(xi, cim)
            return xr[t - 1:t, :], xi[t - 1:t, :]

        z = jnp.zeros((1, LANE), F32)
        lax.fori_loop(0, s // t, chunk, (z, z))

    u0, u1 = _s5_u_specs(s)
    bsp = pl.BlockSpec((S5_W, LANE), lambda j: (0, j))
    csp = pl.BlockSpec((LANE, S5_W), lambda j: (j, 0))
    vec = pl.BlockSpec((1, LANE), lambda j: (0, j))
    xsp = pl.BlockSpec((s, LANE), lambda j: (0, j))
    ysp = pl.BlockSpec((s, S5_W), lambda j: (0, 0))
    xshape = jax.ShapeDtypeStruct((s, S5_STATES), F32)
    return pl.pallas_call(
        body, out_shape=(xshape, xshape, jax.ShapeDtypeStruct((s, S5_W), F32)),
        grid=(S5_STATES // LANE,), in_specs=[u0, u1, bsp, bsp, vec, vec, csp, csp],
        out_specs=(xsp, xsp, ysp), name=name,
        compiler_params=_cparams(("arbitrary",)))(proj, proj, b_re, b_im, lam_re, lam_im, c_re, c_im)


def _s5_scan_bwd(proj, dy, du_init, x_re, x_im, b_re, b_im, lam_re, lam_im, c_re, c_im, name):
    s = proj.shape[0]
    t = SCAN_T
    nc = s // t
    nlog = int(math.log2(t))

    def body(u0_ref, u1_ref, dy_ref, dui_ref, xre_ref, xim_ref, bre_ref, bim_ref, lre_ref, lim_ref,
             cre_ref, cim_ref, du_ref, dlr_ref, dli_ref, dbr_ref, dbi_ref, dcr_ref, dci_ref):
        @pl.when(pl.program_id(0) == 0)
        def _():
            du_ref[...] = dui_ref[...]
        mr, mi = lre_ref[...], -lim_ref[...]
        pows = _cpowers(mr, mi, nlog)
        last = _rows((t, LANE)) == t - 1
        tab_r, tab_i = _cscan(jnp.where(last, mr, 0.0), jnp.where(last, mi, 0.0), pows, True)
        bre, bim, cre, cim = bre_ref[...], bim_ref[...], cre_ref[...], cim_ref[...]
        dbr_ref[...] = jnp.zeros_like(dbr_ref)
        dbi_ref[...] = jnp.zeros_like(dbi_ref)
        dcr_ref[...] = jnp.zeros_like(dcr_ref)
        dci_ref[...] = jnp.zeros_like(dci_ref)

        def chunk(ci_, carry):
            gnr, gni, dlr, dli = carry
            c = nc - 1 - ci_
            t0 = pl.multiple_of(c * t, t)
            rows = pl.ds(t0, t)
            before = pl.ds(pl.multiple_of(jnp.maximum(t0 - 8, 0), 8), 8)
            has_prev = (c > 0).astype(F32)
            dyc = dy_ref[rows, :].astype(BF16)
            u = jnp.concatenate([u0_ref[rows, :], u1_ref[rows, :]], axis=1).astype(BF16)
            gr, gi = _cscan(_dot_nt(dyc, cre), -_dot_nt(dyc, cim), pows, True)
            gr, gi = gr + tab_r * gnr - tab_i * gni, gi + tab_r * gni + tab_i * gnr
            xr, xi = xre_ref[rows, :], xim_ref[rows, :]
            xpr = _shift_down_prev(xr, 1, xre_ref[before, :] * has_prev)
            xpi = _shift_down_prev(xi, 1, xim_ref[before, :] * has_prev)
            dlr = dlr + jnp.sum(gr * xpr + gi * xpi, axis=0, keepdims=True)
            dli = dli + jnp.sum(gi * xpr - gr * xpi, axis=0, keepdims=True)
            du_ref[rows, :] += _dot_nt(gr, bre) + _dot_nt(gi, bim)
            dbr_ref[...] += _dot_tn(u, gr)
            dbi_ref[...] += _dot_tn(u, gi)
            dcr_ref[...] += _dot_tn(xr, dyc)
            dci_ref[...] -= _dot_tn(xi, dyc)
            return gr[0:1, :], gi[0:1, :], dlr, dli

        z = jnp.zeros((1, LANE), F32)
        res = lax.fori_loop(0, nc, chunk, (z, z, z, z))
        dlr_ref[...] = res[2]
        dli_ref[...] = res[3]

    u0, u1 = _s5_u_specs(s)
    bsp = pl.BlockSpec((S5_W, LANE), lambda j: (0, j))
    csp = pl.BlockSpec((LANE, S5_W), lambda j: (j, 0))
    vec = pl.BlockSpec((1, LANE), lambda j: (0, j))
    xsp = pl.BlockSpec((s, LANE), lambda j: (0, j))
    ysp = pl.BlockSpec((s, S5_W), lambda j: (0, 0))
    return pl.pallas_call(
        body,
        out_shape=(jax.ShapeDtypeStruct((s, S5_W), F32),
                   jax.ShapeDtypeStruct((1, S5_STATES), F32), jax.ShapeDtypeStruct((1, S5_STATES), F32),
                   jax.ShapeDtypeStruct((S5_W, S5_STATES), F32), jax.ShapeDtypeStruct((S5_W, S5_STATES), F32),
                   jax.ShapeDtypeStruct((S5_STATES, S5_W), F32), jax.ShapeDtypeStruct((S5_STATES, S5_W), F32)),
        grid=(S5_STATES // LANE,),
        in_specs=[u0, u1, ysp, ysp, xsp, xsp, bsp, bsp, vec, vec, csp, csp],
        out_specs=(ysp, vec, vec, bsp, bsp, csp, csp), name=name,
        compiler_params=_cparams(("arbitrary",)))(
            proj, proj, dy, du_init, x_re, x_im, b_re, b_im, lam_re, lam_im, c_re, c_im)


def _s5_out_fwd(proj, y_acc, dvec, w_glu, b_glu, name):
    s = proj.shape[0]
    tm = 512
    uo = (3 * ATTN_W + 2 * LRU_W) // LANE

    def body(u0_ref, u1_ref, y_ref, d_ref, w_ref, b_ref, o_ref, yp_ref):
        u = jnp.concatenate([u0_ref[...], u1_ref[...]], axis=1)
        y = y_ref[...] + d_ref[...] * u
        yp_ref[...] = y
        yg = _gelu(y)
        o_ref[...] = yg * _sigmoid(_dot(yg, w_ref[...]) + b_ref[...])

    u0 = pl.BlockSpec((tm, LANE), lambda i: (i, uo))
    u1 = pl.BlockSpec((tm, LANE), lambda i: (i, uo + 1))
    row = pl.BlockSpec((tm, S5_W), lambda i: (i, 0))
    vec = pl.BlockSpec((1, S5_W), lambda i: (0, 0))
    wsp = pl.BlockSpec((S5_W, S5_W), lambda i: (0, 0))
    shp = jax.ShapeDtypeStruct((s, S5_W), F32)
    return pl.pallas_call(
        body, out_shape=(shp, shp), grid=(s // tm,), in_specs=[u0, u1, row, vec, wsp, vec],
        out_specs=(row, row), name=name,
        compiler_params=_cparams(("parallel",)))(proj, proj, y_acc, dvec, w_glu, b_glu)


def _s5_out_bwd(proj, y_pre, dout, dvec, w_glu, b_glu, name):
    s = proj.shape[0]
    tm = 512
    uo = (3 * ATTN_W + 2 * LRU_W) // LANE

    def body(u0_ref, u1_ref, y_ref, do_ref, d_ref, w_ref, b_ref, dy_ref, dud_ref, dd_ref, dw_ref, db_ref):
        @pl.when(pl.program_id(0) == 0)
        def _():
            dd_ref[...] = jnp.zeros_like(dd_ref)
            dw_ref[...] = jnp.zeros_like(dw_ref)
            db_ref[...] = jnp.zeros_like(db_ref)
        u = jnp.concatenate([u0_ref[...], u1_ref[...]], axis=1)
        y = y_ref[...]
        do = do_ref[...]
        yg = _gelu(y)
        sg = _sigmoid(_dot(yg, w_ref[...]) + b_ref[...])
        dz = do * yg * sg * (1.0 - sg)
        dyg = do * sg + _dot_nt(dz, w_ref[...])
        dy = dyg * _gelu_grad(y)
        dy_ref[...] = dy
        dud_ref[...] = d_ref[...] * dy
        dd_ref[...] += jnp.sum(dy * u, axis=0, keepdims=True)
        dw_ref[...] += _dot_tn(yg, dz)
        db_ref[...] += jnp.sum(dz, axis=0, keepdims=True)

    u0 = pl.BlockSpec((tm, LANE), lambda i: (i, uo))
    u1 = pl.BlockSpec((tm, LANE), lambda i: (i, uo + 1))
    row = pl.BlockSpec((tm, S5_W), lambda i: (i, 0))
    vec = pl.BlockSpec((1, S5_W), lambda i: (0, 0))
    wsp = pl.BlockSpec((S5_W, S5_W), lambda i: (0, 0))
    shp = jax.ShapeDtypeStruct((s, S5_W), F32)
    vshape = jax.ShapeDtypeStruct((1, S5_W), F32)
    return pl.pallas_call(
        body, out_shape=(shp, shp, vshape, jax.ShapeDtypeStruct((S5_W, S5_W), F32), vshape),
        grid=(s // tm,), in_specs=[u0, u1, row, row, vec, wsp, vec],
        out_specs=(row, row, vec, wsp, vec), name=name,
        compiler_params=_cparams(("arbitrary",)))(proj, proj, y_pre, dout, dvec, w_glu, b_glu)


def _ffn_conv(x, prev8, cw, cb):
    y = cb + cw[FFN_CONV - 1:FFN_CONV, :] * x
    for k in range(FFN_CONV - 1):
        y = y + cw[k:k + 1, :] * _shift_down_prev(x, FFN_CONV - 1 - k, prev8)
    return y


def _ffn_act_fwd(up, cw, cb, name):
    s = up.shape[0]
    tm = 256
    tb = 2 * FFN_CB

    def body(x_ref, p_ref, cw_ref, cb_ref, o_ref):
        prev8 = p_ref[...] * (pl.program_id(1) > 0).astype(F32)
        y = _ffn_conv(x_ref[...], prev8, cw_ref[...], cb_ref[...])
        o_ref[...] = (_gelu(y[:, :FFN_CB]) * y[:, FFN_CB:]).astype(BF16)

    main = pl.BlockSpec((tm, tb), lambda j, i: (i, j))
    prev = pl.BlockSpec((8, tb), lambda j, i: (jnp.maximum(i * (tm // 8) - 1, 0), j))
    return pl.pallas_call(
        body, out_shape=jax.ShapeDtypeStruct((s, D_FF), BF16), grid=(D_FF // FFN_CB, s // tm),
        in_specs=[main, prev, pl.BlockSpec((FFN_CONV, tb), lambda j, i: (0, j)),
                  pl.BlockSpec((1, tb), lambda j, i: (0, j))],
        out_specs=pl.BlockSpec((tm, FFN_CB), lambda j, i: (i, j)), name=name,
        compiler_params=_cparams(("parallel", "parallel")))(up, up, cw, cb)


def _ffn_act_bwd(up, dact, cw, cb, name):
    s = up.shape[0]
    tm = 256
    tb = 2 * FFN_CB
    nr = s // tm

    def body(x_ref, p_ref, n_ref, da_ref, dan_ref, cw_ref, cb_ref, dup_ref, dcw_ref, dcb_ref):
        i = pl.program_id(1)

        @pl.when(i == 0)
        def _():
            dcw_ref[...] = jnp.zeros_like(dcw_ref)
            dcb_ref[...] = jnp.zeros_like(dcb_ref)
        has_next = (i < nr - 1).astype(F32)
        prev8 = p_ref[...] * (i > 0).astype(F32)
        cwv = cw_ref[...]
        x = x_ref[...]
        xe = jnp.concatenate([x, n_ref[...]], axis=0)
        dae = jnp.concatenate([da_ref[...], dan_ref[...] * has_next], axis=0)
        y = _ffn_conv(xe, prev8, cwv, cb_ref[...])
        gate, val = y[:, :FFN_CB], y[:, FFN_CB:]
        dy = jnp.concatenate([dae * val * _gelu_grad(gate), dae * _gelu(gate)], axis=1)
        dym = dy[:tm, :]
        dx = cwv[FFN_CONV - 1:FFN_CONV, :] * dym
        dcw_rows = [None] * FFN_CONV
        dcw_rows[FFN_CONV - 1] = jnp.sum(dym * x, axis=0, keepdims=True)
        for k in range(FFN_CONV - 1):
            sh = FFN_CONV - 1 - k
            dx = dx + cwv[k:k + 1, :] * pltpu.roll(dy, tm + 8 - sh, axis=0)[:tm, :]
            dcw_rows[k] = jnp.sum(dym * _shift_down_prev(x, sh, prev8), axis=0, keepdims=True)
        dup_ref[...] = dx.astype(BF16)
        dcw_ref[...] += jnp.concatenate(dcw_rows, axis=0)
        dcb_ref[...] += jnp.sum(dym, axis=0, keepdims=True)

    main = pl.BlockSpec((tm, tb), lambda j, i: (i, j))
    prev = pl.BlockSpec((8, tb), lambda j, i: (jnp.maximum(i * (tm // 8) - 1, 0), j))
    nxt = pl.BlockSpec((8, tb), lambda j, i: (jnp.minimum((i + 1) * (tm // 8), s // 8 - 1), j))
    da = pl.BlockSpec((tm, FFN_CB), lambda j, i: (i, j))
    dan = pl.BlockSpec((8, FFN_CB), lambda j, i: (jnp.minimum((i + 1) * (tm // 8), s // 8 - 1), j))
    cws = pl.BlockSpec((FFN_CONV, tb), lambda j, i: (0, j))
    cbs = pl.BlockSpec((1, tb), lambda j, i: (0, j))
    return pl.pallas_call(
        body, out_shape=(jax.ShapeDtypeStruct((s, 2 * D_FF), BF16),
                         jax.ShapeDtypeStruct((FFN_CONV, 2 * D_FF), F32),
                         jax.ShapeDtypeStruct((1, 2 * D_FF), F32)),
        grid=(D_FF // FFN_CB, nr), in_specs=[main, prev, nxt, da, dan, cws, cbs],
        out_specs=(main, cws, cbs), name=name,
        compiler_params=_cparams(("parallel", "arbitrary")))(up, up, up, dact, dact, cw, cb)


def _adamw_sum(landed, w, m, v, layer, prev, name):
    _, r, c = landed.shape
    nl = w.shape[0]
    tm = 8
    for cand in (512, 256, 128, 64, 32, 16):
        if r % cand == 0 and N_DEV * cand * c * 4 <= 4 * 1024 * 1024:
            tm = cand
            break

    def body(*refs):
        ld_ref, w_ref, m_ref, v_ref = refs[:4]
        g_ref, d_ref, mo_ref, vo_ref = refs[-4:]
        gg = ld_ref[0]
        for k in range(1, N_DEV):
            gg = gg + ld_ref[k]
        mn = ADAM_B1 * m_ref[...] + (1.0 - ADAM_B1) * gg
        vn = ADAM_B2 * v_ref[...] + (1.0 - ADAM_B2) * (gg * gg)
        m_hat = mn / (1.0 - ADAM_B1 ** ADAM_STEP)
        v_hat = vn / (1.0 - ADAM_B2 ** ADAM_STEP)
        g_ref[...] = gg
        d_ref[...] = -ADAM_LR * (m_hat / (jnp.sqrt(v_hat) + ADAM_EPS) + ADAM_WD * w_ref[...])
        mo_ref[...] = mn
        vo_ref[...] = vn

    blk = pl.BlockSpec((None, tm, c), lambda i: (layer, i, 0))
    in_specs = [pl.BlockSpec((N_DEV, tm, c), lambda i: (0, i, 0)), blk, blk, blk]
    args = [landed, w, m, v]
    aliases = {}
    if prev is not None:
        in_specs += [pl.BlockSpec(memory_space=pl.ANY)] * 4
        args += list(prev)
        aliases = {4 + k: k for k in range(4)}
    shp = jax.ShapeDtypeStruct((nl, r, c), F32)
    return pl.pallas_call(
        body, out_shape=(shp,) * 4, grid=(r // tm,), in_specs=in_specs, out_specs=(blk,) * 4,
        input_output_aliases=aliases, name=name, compiler_params=_cparams(("parallel",)))(*args)


def _all_gather(shards, name):
    na = len(shards)

    def body(*refs):
        x_refs, out_refs = refs[:na], refs[na:2 * na]
        send_sems, recv_sems, local_sems = refs[2 * na:]
        x, y, c = lax.axis_index("x"), lax.axis_index("y"), lax.axis_index("c")
        me, sibling = (x, y, c), (x, y, 1 - c)
        chips = [(1 - x, y), (x, 1 - y), (1 - x, 1 - y)]

        def copy(a, k, block, to, src=None):
            dst = out_refs[a].at[4 * block[0] + 2 * block[1] + block[2]]
            return pltpu.make_async_remote_copy(
                src_ref=dst if src is None else src, dst_ref=dst,
                send_sem=send_sems.at[7 * a + k], recv_sem=recv_sems.at[7 * a + k],
                device_id=to, device_id_type=pl.DeviceIdType.MESH)

        mine, first, passed = [], [], []
        for a in range(na):
            cp = pltpu.make_async_copy(x_refs[a], out_refs[a].at[4 * x + 2 * y + c], local_sems.at[a])
            cp.start()
            mine.append(cp)
            cps = [copy(a, 0, me, sibling, src=x_refs[a])]
            cps += [copy(a, 1 + j, me, (*chip, c), src=x_refs[a]) for j, chip in enumerate(chips)]
            for cp in cps:
                cp.start()
            first += cps
        for j, chip in enumerate(chips):
            for a in range(na):
                copy(a, 1 + j, (*chip, c), me).wait_recv()
                cp = copy(a, 4 + j, (*chip, c), sibling)
                cp.start()
                passed.append(cp)
        for a in range(na):
            copy(a, 0, sibling, me).wait_recv()
            for j, chip in enumerate(chips):
                copy(a, 4 + j, (*chip, 1 - c), me).wait_recv()
        for cp in first + passed:
            cp.wait_send()
        for cp in mine:
            cp.wait()

    anyspec = pl.BlockSpec(memory_space=pl.ANY)
    return pl.pallas_call(
        body, out_shape=tuple(jax.ShapeDtypeStruct((N_DEV,) + t.shape, t.dtype) for t in shards),
        in_specs=[anyspec] * na, out_specs=tuple([anyspec] * na),
        scratch_shapes=[pltpu.SemaphoreType.DMA((7 * na,)), pltpu.SemaphoreType.DMA((7 * na,)),
                        pltpu.SemaphoreType.DMA((na,))],
        name=name)(*shards)


def _all_to_all(bufs, name):
    na = len(bufs)

    def body(*refs):
        b_refs, out_refs = refs[:na], refs[na:2 * na]
        send_sems, recv_sems, local_sems = refs[2 * na:]
        x, y, c = lax.axis_index("x"), lax.axis_index("y"), lax.axis_index("c")
        me = 4 * x + 2 * y + c
        copies = []
        for a in range(na):
            cp = pltpu.make_async_copy(b_refs[a].at[me], out_refs[a].at[me], local_sems.at[a])
            cp.start()
            copies.append(cp)
        for k in range(1, N_DEV):
            px = x ^ ((k >> 2) & 1)
            py = y ^ ((k >> 1) & 1)
            pc = c ^ (k & 1)
            for a in range(na):
                cp = pltpu.make_async_remote_copy(
                    src_ref=b_refs[a].at[4 * px + 2 * py + pc], dst_ref=out_refs[a].at[me],
                    send_sem=send_sems.at[7 * a + k - 1], recv_sem=recv_sems.at[7 * a + k - 1],
                    device_id=(px, py, pc), device_id_type=pl.DeviceIdType.MESH)
                cp.start()
                copies.append(cp)
        for cp in copies:
            cp.wait()

    anyspec = pl.BlockSpec(memory_space=pl.ANY)
    return pl.pallas_call(
        body, out_shape=tuple(jax.ShapeDtypeStruct(t.shape, t.dtype) for t in bufs),
        in_specs=[anyspec] * na, out_specs=tuple([anyspec] * na),
        scratch_shapes=[pltpu.SemaphoreType.DMA((7 * na,)), pltpu.SemaphoreType.DMA((7 * na,)),
                        pltpu.SemaphoreType.DMA((na,))],
        name=name)(*bufs)


_HBM = pl.BlockSpec(memory_space=pltpu.HBM)
_SEM = pl.BlockSpec(memory_space=pltpu.SEMAPHORE)
_EFFECT = pltpu.SideEffectType.DATAFLOW_SIDE_EFFECTING


def _exchange_copies(src_refs, land_refs, send_sems, recv_sems, local_sems, gather):
    x, y, c = lax.axis_index("x"), lax.axis_index("y"), lax.axis_index("c")
    me = 4 * x + 2 * y + c
    local, remote = [], []
    for a, (src, land) in enumerate(zip(src_refs, land_refs)):
        local.append(pltpu.make_async_copy(src if gather else src.at[me], land.at[me], local_sems.at[a]))
    for k in range(1, N_DEV):
        px = x ^ ((k >> 2) & 1)
        py = y ^ ((k >> 1) & 1)
        pc = c ^ (k & 1)
        for a, (src, land) in enumerate(zip(src_refs, land_refs)):
            remote.append(pltpu.make_async_remote_copy(
                src_ref=src if gather else src.at[4 * px + 2 * py + pc], dst_ref=land.at[me],
                send_sem=send_sems.at[7 * a + k - 1], recv_sem=recv_sems.at[7 * a + k - 1],
                device_id=(px, py, pc), device_id_type=pl.DeviceIdType.MESH))
    return local, remote


def _exchange_start(srcs, gather, name, dep=None):
    na = len(srcs)
    lands = [lax.empty(((N_DEV,) + t.shape) if gather else t.shape, t.dtype) for t in srcs]

    def body(*refs):
        src_refs, land_refs = refs[:na], refs[na:2 * na]
        nin = 2 * na + (0 if dep is None else 1)
        send_sems, recv_sems, local_sems = refs[nin:nin + 3]
        token = refs[-1]
        local, remote = _exchange_copies(src_refs, land_refs, send_sems, recv_sems, local_sems, gather)
        for cp in local + remote:
            cp.start()
        token[...] = jnp.zeros_like(token)

    dep_specs, dep_ops = _dep_args(dep)
    hbm = lambda t: pltpu.HBM(t.shape, t.dtype)
    out = pl.pallas_call(
        body, name=name,
        out_shape=(pltpu.SemaphoreType.DMA((7 * na,)), pltpu.SemaphoreType.DMA((7 * na,)),
                   pltpu.SemaphoreType.DMA((na,)), *[hbm(t) for t in srcs], *[hbm(t) for t in lands],
                   jax.ShapeDtypeStruct((8, LANE), F32)),
        in_specs=[_HBM] * (2 * na) + dep_specs,
        out_specs=(_SEM, _SEM, _SEM, *[_HBM] * (2 * na), pl.BlockSpec(memory_space=pltpu.VMEM)),
        input_output_aliases={i: 3 + i for i in range(2 * na)},
        compiler_params=pltpu.CompilerParams(has_side_effects=_EFFECT),
    )(*[pltpu.with_memory_space_constraint(t, pltpu.HBM) for t in srcs + lands], *dep_ops)
    return (out[:3], out[3:3 + na], out[3 + na:3 + 2 * na]), out[-1]


def _exchange_wait(handle, gather, after, name):
    sems, srcs, lands = handle
    na = len(srcs)

    def body(*refs):
        src_refs, land_refs = refs[:na], refs[na:2 * na]
        send_sems, recv_sems, local_sems = refs[2 * na:2 * na + 3]
        local, remote = _exchange_copies(src_refs, land_refs, send_sems, recv_sems, local_sems, gather)
        for cp in remote:
            cp.wait_send()
            cp.wait_recv()
        for cp in local:
            cp.wait()

    hbm = lambda t: pltpu.HBM(t.shape, t.dtype)
    out = pl.pallas_call(
        body, name=name, out_shape=(*[hbm(t) for t in srcs], *[hbm(t) for t in lands]),
        in_specs=[_HBM] * (2 * na) + [_SEM] * 3 + [pl.BlockSpec(memory_space=pl.ANY)],
        out_specs=tuple([_HBM] * (2 * na)), input_output_aliases={i: i for i in range(2 * na)},
        compiler_params=pltpu.CompilerParams(has_side_effects=_EFFECT),
    )(*srcs, *lands, *sems, after)
    return out[na:]


def _block_diag(w):
    h, a, b = w.shape
    eye = jnp.eye(h, dtype=w.dtype)
    return (w[:, :, None, :] * eye[:, None, :, None]).reshape(h * a, h * b)


def _block_diag_extract(m, h):
    a, b = m.shape[0] // h, m.shape[1] // h
    return jnp.stack([m[i * a:(i + 1) * a, i * b:(i + 1) * b] for i in range(h)], axis=0)


def _ffn_interleave(w):
    lead = w.shape[:-1]
    nb = D_FF // FFN_CB
    return jnp.swapaxes(w.reshape(*lead, 2, nb, FFN_CB), -3, -2).reshape(*lead, 2 * D_FF)


def _ffn_deinterleave(w):
    lead = w.shape[:-1]
    nb = D_FF // FFN_CB
    return jnp.swapaxes(w.reshape(*lead, nb, 2, FFN_CB), -3, -2).reshape(*lead, 2 * D_FF)


def _gather_full(gathered, axis):
    shape = list(gathered.shape[1:])
    shape[axis] *= N_DEV
    return jnp.moveaxis(gathered, 0, axis).reshape(shape)


def _scatter_blocks(full, axis):
    shape = list(full.shape)
    shape[axis:axis + 1] = [N_DEV, shape[axis] // N_DEV]
    return jnp.moveaxis(full.reshape(shape), axis, 0)


def _pad_to(flat, mult):
    pad = (-flat.shape[-1]) % mult
    if pad:
        flat = jnp.concatenate([flat, jnp.zeros(flat.shape[:-1] + (pad,), flat.dtype)], axis=-1)
    return flat


def _layer_fwd(h_in, w, cos, sin, l, dep=None):
    tag = "l%d_" % l
    proj = _mm_nn(h_in, w['w_in'], 512, D_IN, tag + "proj", dep=dep)
    qkv = _rope_fwd(proj, cos, sin, tag + "rope")
    outs, lses = [], []
    for d in DILATIONS:
        o, ls = _attn_fwd(qkv, d, tag + "attn_d%d" % d)
        outs.append(o)
        lses.append(ls)
    lru = _lru_fwd(proj, w['lru_conv_w'], w['lru_conv_b'], w['lru_wr'], w['lru_br'], w['lru_wi'],
                   w['lru_bi'], w['lru_lambda'], tag + "lru")
    x_re, x_im, y_acc = _s5_scan_fwd(proj, w['s5_bb_re'], w['s5_bb_im'], w['s5_lam_re'], w['s5_lam_im'],
                                     w['s5_cc_re'], w['s5_cc_im'], tag + "s5_scan")
    s5, y_pre = _s5_out_fwd(proj, y_acc, w['s5_d'], w['s5_w_glu'], w['s5_b_glu'], tag + "s5_out")
    mixed, attn_o, attn_lse = _mix_fwd(outs, lses, lru, s5, w['mix_norm_g'], tag + "mix")
    mixo = _mm_nn(mixed, w['w_out'], 512, D_MODEL, tag + "out_proj")
    r1, h1 = _ln_fwd(h_in, mixo, w['ln1_g'], w['ln1_b'], tag + "ln1")
    up = _mm_up(h1, w['w_up_g'], 512, tag + "up_proj")
    act = _ffn_act_fwd(up, w['ffn_conv_w'], w['ffn_conv_b'], tag + "ffn_act")
    ffn = _mm_nn(act, w['w_down'], 512, D_MODEL, tag + "down_proj")
    r2, h2 = _ln_fwd(h1, ffn, w['ln2_g'], w['ln2_b'], tag + "ln2")
    saved = dict(h_in=h_in, proj=proj, qkv=qkv, lru=lru, x_re=x_re, x_im=x_im, y_pre=y_pre, s5=s5,
                 mixed=mixed, attn_o=attn_o, attn_lse=attn_lse, r1=r1, h1=h1, up=up, act=act, r2=r2)
    return h2, saved


def _layer_bwd_ffn(dh2, sv, w, l, dep=None):
    tag = "l%d_" % l
    g = {}
    dr2, g['ln2_g'], g['ln2_b'] = _ln_bwd(sv['r2'], dh2, w['ln2_g'], tag + "ln2_bwd", dep=dep)
    g['w_down'] = _mm_tn(sv['act'], dr2, 1024, D_MODEL, 512, tag + "down_dw")
    dact = _mm_nt(dr2, w['w_down'], 512, D_FF, tag + "down_dx")
    dup, g['ffn_conv_w'], g['ffn_conv_b'] = _ffn_act_bwd(sv['up'], dact, w['ffn_conv_w'], w['ffn_conv_b'],
                                                        tag + "ffn_act_bwd")
    g['w_up_g'] = _mm_up_dw(sv['h1'], dup, 512, tag + "up_dw")
    dh1 = _mm_up_dx(dup, w['w_up_g'], dr2, ALPHA, 1024, tag + "up_dx")
    return dh1, g


def _layer_bwd_mix(dh1, sv, w, cos, sin, l, dep=None):
    tag = "l%d_" % l
    g = {}
    dr1, g['ln1_g'], g['ln1_b'] = _ln_bwd(sv['r1'], dh1, w['ln1_g'], tag + "ln1_bwd", dep=dep)
    g['w_out'] = _mm_tn(sv['mixed'], dr1, 1024, D_MODEL, 512, tag + "out_dw")
    dmixed = _mm_nt(dr1, w['w_out'], 512, D_MODEL, tag + "out_dx")
    d_o, dlru, ds5, g['mix_norm_g'] = _mix_bwd(dmixed, sv['attn_o'], sv['lru'], sv['s5'], w['mix_norm_g'],
                                               tag + "mix_bwd")
    dy, dud, g['s5_d'], g['s5_w_glu'], g['s5_b_glu'] = _s5_out_bwd(
        sv['proj'], sv['y_pre'], ds5, w['s5_d'], w['s5_w_glu'], w['s5_b_glu'], tag + "s5_out_bwd")
    du, g['s5_lam_re'], g['s5_lam_im'], g['s5_bb_re'], g['s5_bb_im'], g['s5_cc_re'], g['s5_cc_im'] = \
        _s5_scan_bwd(sv['proj'], dy, dud, sv['x_re'], sv['x_im'], w['s5_bb_re'], w['s5_bb_im'],
                     w['s5_lam_re'], w['s5_lam_im'], w['s5_cc_re'], w['s5_cc_im'], tag + "s5_scan_bwd")
    (dxr, dgate, g['lru_conv_w'], g['lru_conv_b'], g['lru_wr'], g['lru_br'], g['lru_wi'], g['lru_bi'],
     g['lru_lambda']) = _lru_bwd(sv['proj'], dlru, w['lru_conv_w'], w['lru_conv_b'], w['lru_wr'],
                                 w['lru_br'], w['lru_wi'], w['lru_bi'], w['lru_lambda'], tag + "lru_bwd")
    dqkv = [_attn_bwd(sv['qkv'], sv['attn_o'], d_o, sv['attn_lse'], d, tag + "attn_bwd_d%d" % d)
            for d in DILATIONS]
    dproj = _dproj_assemble(dqkv, dxr, dgate, du, cos, sin, tag + "dproj")
    g['w_in'] = _mm_tn(sv['h_in'], dproj, 1024, D_IN, 512, tag + "in_dw")
    dh_in = _mm_nt(dproj, w['w_in'], 512, D_MODEL, tag + "in_dx", add=dr1, add_scale=ALPHA)
    return dh_in, g


def _s5_rep(a):
    return jnp.repeat(a, S5_C, axis=0)


def _prepare_layer(p, l):
    w = {}
    for n in ('w_in', 'w_out', 'w_down', 'w_up_g', 's5_w_glu'):
        w[n] = p[n].astype(BF16)
    w['ffn_conv_w'] = _ffn_interleave(p['ffn_conv_w'])
    w['ffn_conv_b'] = _ffn_interleave(p['ffn_conv_b'])[None, :]
    w['lru_conv_w'] = p['lru_conv_w']
    for n in ('lru_conv_b', 'lru_br', 'lru_bi', 'lru_lambda', 's5_b_glu', 'mix_norm_g',
              'ln1_g', 'ln1_b', 'ln2_g', 'ln2_b'):
        w[n] = p[n][None, :]
    w['lru_wr'] = _block_diag(p['lru_wr']).astype(BF16)
    w['lru_wi'] = _block_diag(p['lru_wi']).astype(BF16)
    w['s5_d'] = p['s5_d'].reshape(1, S5_W)
    disc_in = (_s5_rep(p['s5_a_re']), _s5_rep(p['s5_a_im']),
               _s5_rep(jnp.broadcast_to(p['s5_log_step'][:, None], (S5_G, S5_P))),
               jnp.swapaxes(p['s5_b_re'], 1, 2).reshape(S5_W, S5_P),
               jnp.swapaxes(p['s5_b_im'], 1, 2).reshape(S5_W, S5_P))
    ab_re, ab_im, bb_re, bb_im = _s5_disc_fwd(*disc_in, "l%d_s5_disc" % l)
    w['s5_disc_in'] = disc_in
    w['s5_lam_re'] = ab_re.reshape(S5_G, S5_C, S5_P)[:, 0, :].reshape(1, S5_STATES)
    w['s5_lam_im'] = ab_im.reshape(S5_G, S5_C, S5_P)[:, 0, :].reshape(1, S5_STATES)
    w['s5_bb_re'] = _block_diag(bb_re.reshape(S5_G, S5_C, S5_P)).astype(BF16)
    w['s5_bb_im'] = _block_diag(bb_im.reshape(S5_G, S5_C, S5_P)).astype(BF16)
    w['s5_cc_re'] = _block_diag(jnp.swapaxes(p['s5_c_re'], 1, 2)).astype(BF16)
    w['s5_cc_im'] = _block_diag(jnp.swapaxes(p['s5_c_im'], 1, 2)).astype(BF16)
    return w


def _finish_layer_grads(g, w, l):
    out = {}
    for n in ('w_in', 'w_out', 'w_down', 'w_up_g', 's5_w_glu', 'lru_conv_w'):
        out[n] = g[n]
    out['ffn_conv_w'] = _ffn_deinterleave(g['ffn_conv_w'])
    out['ffn_conv_b'] = _ffn_deinterleave(g['ffn_conv_b'])[0]
    for n in ('lru_conv_b', 'lru_br', 'lru_bi', 'lru_lambda', 's5_b_glu', 'mix_norm_g',
              'ln1_g', 'ln1_b', 'ln2_g', 'ln2_b'):
        out[n] = g[n][0]
    out['lru_wr'] = _block_diag_extract(g['lru_wr'], LRU_W // HEAD)
    out['lru_wi'] = _block_diag_extract(g['lru_wi'], LRU_W // HEAD)
    out['s5_d'] = g['s5_d'].reshape(S5_G, S5_C)
    out['s5_c_re'] = jnp.swapaxes(_block_diag_extract(g['s5_cc_re'], S5_G), 1, 2)
    out['s5_c_im'] = jnp.swapaxes(_block_diag_extract(g['s5_cc_im'], S5_G), 1, 2)
    rep = lambda v: _s5_rep(v.reshape(S5_G, S5_P)) * (1.0 / S5_C)
    cts = (rep(g['s5_lam_re']), rep(g['s5_lam_im']),
           _block_diag_extract(g['s5_bb_re'], S5_G).reshape(S5_W, S5_P),
           _block_diag_extract(g['s5_bb_im'], S5_G).reshape(S5_W, S5_P))
    da_re, da_im, dls, dbt_re, dbt_im = _s5_disc_bwd(*w['s5_disc_in'], cts, "l%d_s5_disc_bwd" % l)
    out['s5_a_re'] = da_re.reshape(S5_G, S5_C, S5_P).sum(axis=1)
    out['s5_a_im'] = da_im.reshape(S5_G, S5_C, S5_P).sum(axis=1)
    out['s5_log_step'] = dls.reshape(S5_G, S5_C * S5_P).sum(axis=1)
    out['s5_b_re'] = jnp.swapaxes(dbt_re.reshape(S5_G, S5_C, S5_P), 1, 2)
    out['s5_b_im'] = jnp.swapaxes(dbt_im.reshape(S5_G, S5_C, S5_P), 1, 2)
    return out


def _run_step(x, target, get_layer, after_ffn_grads, after_mix_grads):
    cos, sin = _rope_tables(x.shape[0])
    h = x
    ws, saved = [], []
    for l in range(DEPTH):
        p, dep = get_layer(l, h)
        ws.append(_prepare_layer(p, l))
        h, sv = _layer_fwd(h, ws[l], cos, sin, l, dep)
        saved.append(sv)
    dh, loss_vec = _loss_head(h, target)
    dep = None
    for l in reversed(range(DEPTH)):
        dh1, g = _layer_bwd_ffn(dh, saved[l], ws[l], l, dep)
        dep = after_ffn_grads(l, g)
        dh, g_mix = _layer_bwd_mix(dh1, saved[l], ws[l], cos, sin, l, dep)
        g.update(g_mix)
        dep = after_mix_grads(l, _finish_layer_grads(g, ws[l], l))
    return loss_vec[0, 0], dh


def _local_step(x, target, layers):
    grads = [None] * DEPTH

    def keep(l, g):
        grads[l] = g

    loss, dx = _run_step(x, target, lambda l, h: (layers[l], None), lambda l, g: None, keep)
    return loss, dx, grads


def kernel(x, w_in, lru_conv_w, lru_conv_b, lru_wr, lru_br, lru_wi, lru_bi, lru_lambda, s5_a_re, s5_a_im, s5_b_re, s5_b_im, s5_c_re, s5_c_im, s5_d, s5_log_step, s5_w_glu, s5_b_glu, mix_norm_g, w_out, ln1_g, ln1_b, w_up, ffn_conv_w, ffn_conv_b, w_down, ln2_g, ln2_b, loss_target, m_w_in, m_lru_conv_w, m_lru_conv_b, m_lru_wr, m_lru_br, m_lru_wi, m_lru_bi, m_lru_lambda, m_s5_a_re, m_s5_a_im, m_s5_b_re, m_s5_b_im, m_s5_c_re, m_s5_c_im, m_s5_d, m_s5_log_step, m_s5_w_glu, m_s5_b_glu, m_mix_norm_g, m_w_out, m_ln1_g, m_ln1_b, m_w_up, m_ffn_conv_w, m_ffn_conv_b, m_w_down, m_ln2_g, m_ln2_b, v_w_in, v_lru_conv_w, v_lru_conv_b, v_lru_wr, v_lru_br, v_lru_wi, v_lru_bi, v_lru_lambda, v_s5_a_re, v_s5_a_im, v_s5_b_re, v_s5_b_im, v_s5_c_re, v_s5_c_im, v_s5_d, v_s5_log_step, v_s5_w_glu, v_s5_b_glu, v_mix_norm_g, v_w_out, v_ln1_g, v_ln1_b, v_w_up, v_ffn_conv_w, v_ffn_conv_b, v_w_down, v_ln2_g, v_ln2_b):
    args = locals()
    wl = {n: args[n] for n in WEIGHTS}
    ml = {n: args['m_' + n] for n in WEIGHTS}
    vl = {n: args['v_' + n] for n in WEIGHTS}
    me = 4 * lax.axis_index("x") + 2 * lax.axis_index("y") + lax.axis_index("c")

    small_sizes = [int(wl[n].size) for n in SMALL_SHARDED]
    small_flat = _pad_to(jnp.concatenate([wl[n].reshape(-1) for n in SMALL_SHARDED]), 8 * 1024)
    small_all, = _all_gather([small_flat.reshape(-1, 1024)], "gather_small")
    small_all = small_all.reshape(N_DEV, -1)
    small_full, off = {}, 0
    for n, sz in zip(SMALL_SHARDED, small_sizes):
        small_full[n] = _gather_full(small_all[:, off:off + sz].reshape((N_DEV,) + wl[n].shape), SHARD_AXIS[n])
        off += sz
    def layer_params(l, gathered):
        g_in, g_out, g_up, g_down = gathered
        p = {n: wl[n][l] for n in REPLICATED}
        p.update({n: small_full[n][l] for n in SMALL_SHARDED})
        p['w_in'] = _gather_full(g_in, 1)
        p['w_out'] = g_out.reshape(D_MODEL, D_MODEL)
        p['w_down'] = g_down.reshape(D_FF, D_MODEL)
        p['w_up_g'] = g_up
        return p

    gathered0 = _all_gather([wl[n][0].astype(BF16) for n in BIG], "gather_l0")
    gather1, gather1_token = _exchange_start([wl[n][1].astype(BF16) for n in BIG], True, "gather_l1_start",
                                             dep=gathered0[0])

    def get_layer(l, h):
        if l == 0:
            return layer_params(0, gathered0), gather1_token
        return layer_params(1, _exchange_wait(gather1, True, h, "gather_l1_wait")), None

    scatters, grads = {}, [None] * DEPTH

    def after_ffn_grads(l, g):
        send = [g['w_up_g'], g['w_down'].reshape(N_DEV, D_FF // N_DEV, D_MODEL)]
        scatters[l, 'ffn'], token = _exchange_start(send, False, "scatter_ffn_l%d_start" % l)
        return token

    def after_mix_grads(l, g):
        grads[l] = g
        send = [_scatter_blocks(g['w_in'], 1), g['w_out'].reshape(N_DEV, D_MODEL // N_DEV, D_MODEL)]
        scatters[l, 'mix'], token = _exchange_start(send, False, "scatter_mix_l%d_start" % l)
        return token

    loss_local, grad_x = _run_step(x[0], loss_target[0], get_layer, after_ffn_grads, after_mix_grads)
    loss = lax.psum(loss_local, AXES)

    results = {}
    big_prev = {n: None for n in BIG}
    for l in reversed(range(DEPTH)):
        for part, names in (('ffn', ('w_up', 'w_down')), ('mix', ('w_in', 'w_out'))):
            landed = _exchange_wait(scatters[l, part], False, grad_x, "scatter_%s_l%d_wait" % (part, l))
            for n, ld in zip(names, landed):
                big_prev[n] = _adamw_sum(ld, wl[n], ml[n], vl[n], l, big_prev[n], "adamw_%s_l%d" % (n, l))
    for n in BIG:
        results['grad', n], results['delta', n], results['m', n], results['v', n] = big_prev[n]

    stacked = {n: jnp.stack([grads[l][n] for l in range(DEPTH)], axis=0) for n in SMALL_SHARDED + REPLICATED}
    rep_sizes = [int(wl[n].size) for n in REPLICATED]
    rep_per = -(-sum(rep_sizes) // (N_DEV * 1024)) * 1024

    def rep_flat(tree):
        return _pad_to(jnp.concatenate([tree[n].reshape(-1) for n in REPLICATED]), N_DEV * rep_per)

    rows = [_scatter_blocks(stacked[n], SHARD_AXIS[n]).reshape(N_DEV, -1) for n in SMALL_SHARDED]
    rows.append(rep_flat(stacked).reshape(N_DEV, rep_per))
    small_send = _pad_to(jnp.concatenate(rows, axis=1), 8 * 1024)
    n_own = small_send.shape[1]
    small_landed, = _all_to_all([small_send.reshape(N_DEV, n_own // 1024, 1024)], "scatter_small")

    def own_flat(tree):
        parts = [tree[n].reshape(-1) for n in SMALL_SHARDED]
        parts.append(lax.dynamic_slice(rep_flat(tree), (me * rep_per,), (rep_per,)))
        return _pad_to(jnp.concatenate(parts), 8 * 1024).reshape(1, n_own // 1024, 1024)

    small_res = _adamw_sum(small_landed, own_flat(wl), own_flat(ml), own_flat(vl), 0, None, "adamw_small")

    kinds = ('grad', 'delta', 'm', 'v')
    sh_total = sum(small_sizes)
    for kind, arr in zip(kinds, small_res):
        flat = arr.reshape(-1)
        off = 0
        for n, sz in zip(SMALL_SHARDED, small_sizes):
            results[kind, n] = flat[off:off + sz].reshape(wl[n].shape)
            off += sz
    rep_own = jnp.stack([a.reshape(-1)[sh_total:sh_total + rep_per] for a in small_res])
    rep_all, = _all_gather([rep_own.reshape(4 * rep_per // 1024, 1024)], "gather_replicated")
    rep_all = rep_all.reshape(N_DEV, 4, rep_per)
    for k, kind in enumerate(kinds):
        flat = rep_all[:, k, :].reshape(-1)
        off = 0
        for n, sz in zip(REPLICATED, rep_sizes):
            results[kind, n] = flat[off:off + sz].reshape(wl[n].shape)
            off += sz

    out = [loss, grad_x[None]]
    for kind in kinds:
        out.extend(results[kind, n] for n in WEIGHTS)
    return tuple(out)
```

```python
import functools
import math

import jax
import jax.numpy as jnp
from jax import lax
from jax.experimental import pallas as pl
from jax.experimental.pallas import tpu as pltpu

F32 = jnp.float32
BF16 = jnp.bfloat16

N_DEV = 8
DEPTH = 2
D_MODEL = 1024
ATTN_W = 384
LRU_W = 384
S5_W = 256
D_IN = 2176
D_FF = 3072
HEAD = 64
ATTN_BLK = 128
ATTN_TILE = 1024
DILATIONS = (1, 4, 16)
S5_G = 16
S5_P = 64
S5_C = 16
S5_STATES = S5_G * S5_P
LRU_C = 8.0
LRU_CONV = 4
FFN_CONV = 3
ROPE_THETA = 10000.0
ALPHA = (2 * DEPTH) ** 0.25
LN_EPS = 1e-5
RMS_EPS = 1e-6
ADAM_LR, ADAM_B1, ADAM_B2, ADAM_EPS, ADAM_WD, ADAM_STEP = 0.001, 0.9, 0.999, 1e-8, 0.01, 10

LANE = 128
SCAN_T = 256
FFN_CB = 2 * D_FF // N_DEV
VMEM_LIMIT = 56 * 1024 * 1024

AXES = ("x", "y", "c")

WEIGHTS = ['w_in', 'lru_conv_w', 'lru_conv_b', 'lru_wr', 'lru_br', 'lru_wi', 'lru_bi', 'lru_lambda',
           's5_a_re', 's5_a_im', 's5_b_re', 's5_b_im', 's5_c_re', 's5_c_im', 's5_d', 's5_log_step',
           's5_w_glu', 's5_b_glu', 'mix_norm_g', 'w_out', 'ln1_g', 'ln1_b', 'w_up', 'ffn_conv_w',
           'ffn_conv_b', 'w_down', 'ln2_g', 'ln2_b']
SHARD_AXIS = {'w_in': 2, 'lru_conv_w': 2, 's5_w_glu': 1, 'w_out': 1, 'w_up': 2, 'ffn_conv_w': 2, 'w_down': 1}
BIG = ['w_in', 'w_out', 'w_up', 'w_down']
SMALL_SHARDED = ['lru_conv_w', 'ffn_conv_w', 's5_w_glu']
REPLICATED = [n for n in WEIGHTS if n not in SHARD_AXIS]


def _cparams(sem=None):
    return pltpu.CompilerParams(dimension_semantics=sem, vmem_limit_bytes=VMEM_LIMIT)


def _ffn_dev(jb):
    return jb // 2 + (N_DEV // 2) * (jb % 2)


def _gelu(x):
    c = math.sqrt(2.0 / math.pi)
    t = jnp.tanh(c * (x + 0.044715 * (x * x * x)))
    return 0.5 * x * (1.0 + t)


def _gelu_grad(x):
    c = math.sqrt(2.0 / math.pi)
    x2 = x * x
    t = jnp.tanh(c * (x + 0.044715 * (x2 * x)))
    return 0.5 * (1.0 + t) + 0.5 * x * (1.0 - t * t) * (c * (1.0 + 3.0 * 0.044715 * x2))


def _sigmoid(x):
    return 1.0 / (1.0 + jnp.exp(-x))


def _log1p(x):
    u = 1.0 + x
    d = u - 1.0
    return jnp.where(d == 0.0, x, jnp.log(u) * (x / jnp.where(d == 0.0, 1.0, d)))


def _softplus(x):
    return jnp.maximum(x, 0.0) + _log1p(jnp.exp(-jnp.abs(x)))


def _expm1(x):
    return jnp.tanh(0.5 * x) * (jnp.exp(x) + 1.0)


def _dot(a, b):
    return jnp.dot(a.astype(BF16), b.astype(BF16), preferred_element_type=F32)


def _dot_nt(a, b):
    return lax.dot_general(a.astype(BF16), b.astype(BF16), (((1,), (1,)), ((), ())),
                           preferred_element_type=F32)


def _dot_tn(a, b):
    return lax.dot_general(a.astype(BF16), b.astype(BF16), (((0,), (0,)), ((), ())),
                           preferred_element_type=F32)


def _rows(shape):
    return lax.broadcasted_iota(jnp.int32, shape, 0)


def _shift_down(x, s, fill):
    r = pltpu.roll(x, s, axis=0)
    return jnp.where(_rows(x.shape) >= s, r, fill)


def _shift_up(x, s, fill):
    t = x.shape[0]
    r = pltpu.roll(x, t - s, axis=0)
    return jnp.where(_rows(x.shape) < t - s, r, fill)


def _shift_down_prev(x, s, prev8):
    if s == 0:
        return x
    t, l = x.shape
    r = pltpu.roll(x, s, axis=0)
    pr = pltpu.roll(prev8, s, axis=0)
    pad = jnp.concatenate([pr, jnp.zeros((t - 8, l), x.dtype)], axis=0)
    return jnp.where(_rows(x.shape) < s, pad, r)


def _shift_up_next(x, s, next8):
    if s == 0:
        return x
    t, l = x.shape
    r = pltpu.roll(x, t - s, axis=0)
    nx = pltpu.roll(next8, 8 - s, axis=0)
    pad = jnp.concatenate([jnp.zeros((t - 8, l), x.dtype), nx], axis=0)
    return jnp.where(_rows(x.shape) >= t - s, pad, r)


def _scan_fwd(a, x):
    t = x.shape[0]
    s = 1
    while s < t:
        x = x + a * _shift_down(x, s, 0.0)
        a = a * _shift_down(a, s, 1.0)
        s *= 2
    return a, x


def _scan_rev(a, x):
    t = x.shape[0]
    s = 1
    while s < t:
        x = x + a * _shift_up(x, s, 0.0)
        a = a * _shift_up(a, s, 1.0)
        s *= 2
    return a, x


def _cpowers(lr, li, n):
    out = [(lr, li)]
    for _ in range(n - 1):
        lr, li = lr * lr - li * li, 2.0 * lr * li
        out.append((lr, li))
    return out


def _cscan(xr, xi, pows, reverse):
    shift = _shift_up if reverse else _shift_down
    s = 1
    for pr, pi in pows:
        sr = shift(xr, s, 0.0)
        si = shift(xi, s, 0.0)
        xr, xi = xr + pr * sr - pi * si, xi + pr * si + pi * sr
        s *= 2
    return xr, xi


def _dep_args(dep):
    return ([], []) if dep is None else ([pl.BlockSpec(memory_space=pl.ANY)], [dep])


def _mm_nn(a, b, tm, tn, name, out_dtype=F32, dep=None):
    m, k = a.shape
    n = b.shape[1]

    def body(a_ref, b_ref, *rest):
        o_ref = rest[-1]
        o_ref[...] = _dot(a_ref[...], b_ref[...]).astype(out_dtype)

    dep_specs, dep_ops = _dep_args(dep)
    return pl.pallas_call(
        body, out_shape=jax.ShapeDtypeStruct((m, n), out_dtype), grid=(n // tn, m // tm),
        in_specs=[pl.BlockSpec((tm, k), lambda j, i: (i, 0)),
                  pl.BlockSpec((k, tn), lambda j, i: (0, j))] + dep_specs,
        out_specs=pl.BlockSpec((tm, tn), lambda j, i: (i, j)), name=name,
        compiler_params=_cparams(("parallel", "parallel")))(a, b, *dep_ops)


def _mm_nt(a, w, tm, tn, name, add=None, add_scale=1.0, dep=None):
    m, k = a.shape
    n = w.shape[0]

    def body(a_ref, w_ref, *rest):
        o_ref = rest[-1]
        if add is None:
            o_ref[...] = _dot_nt(a_ref[...], w_ref[...])
        else:
            o_ref[...] = _dot_nt(a_ref[...], w_ref[...]) + add_scale * rest[0][...]

    in_specs = [pl.BlockSpec((tm, k), lambda j, i: (i, 0)), pl.BlockSpec((tn, k), lambda j, i: (j, 0))]
    args = [a, w]
    if add is not None:
        in_specs.append(pl.BlockSpec((tm, tn), lambda j, i: (i, j)))
        args.append(add)
    dep_specs, dep_ops = _dep_args(dep)
    return pl.pallas_call(
        body, out_shape=jax.ShapeDtypeStruct((m, n), F32), grid=(n // tn, m // tm),
        in_specs=in_specs + dep_specs, out_specs=pl.BlockSpec((tm, tn), lambda j, i: (i, j)), name=name,
        compiler_params=_cparams(("parallel", "parallel")))(*args, *dep_ops)


def _mm_tn(a, b, tm, tn, ts, name):
    s, m = a.shape
    n = b.shape[1]

    def body(a_ref, b_ref, o_ref):
        @pl.when(pl.program_id(2) == 0)
        def _():
            o_ref[...] = jnp.zeros_like(o_ref)
        o_ref[...] += _dot_tn(a_ref[...], b_ref[...])

    return pl.pallas_call(
        body, out_shape=jax.ShapeDtypeStruct((m, n), F32), grid=(m // tm, n // tn, s // ts),
        in_specs=[pl.BlockSpec((ts, tm), lambda i, j, k: (k, i)), pl.BlockSpec((ts, tn), lambda i, j, k: (k, j))],
        out_specs=pl.BlockSpec((tm, tn), lambda i, j, k: (i, j)), name=name,
        compiler_params=_cparams(("parallel", "parallel", "arbitrary")))(a, b)


def _mm_up(h, wg, tm, name, dep=None):
    s, d = h.shape

    def body(a_ref, w_ref, *rest):
        rest[-1][...] = _dot(a_ref[...], w_ref[...])

    dep_specs, dep_ops = _dep_args(dep)
    return pl.pallas_call(
        body, out_shape=jax.ShapeDtypeStruct((s, 2 * D_FF), F32), grid=(s // tm, N_DEV),
        in_specs=[pl.BlockSpec((tm, d), lambda i, j: (i, 0)),
                  pl.BlockSpec((None, d, FFN_CB), lambda i, j: (_ffn_dev(j), 0, 0))] + dep_specs,
        out_specs=pl.BlockSpec((tm, FFN_CB), lambda i, j: (i, j)), name=name,
        compiler_params=_cparams(("parallel", "parallel")))(h, wg, *dep_ops)


def _mm_up_dx(dup, wg, add, add_scale, tm, name):
    s = dup.shape[0]
    d = wg.shape[1]

    def body(a_ref, w_ref, c_ref, o_ref):
        @pl.when(pl.program_id(1) == 0)
        def _():
            o_ref[...] = add_scale * c_ref[...]
        o_ref[...] += _dot_nt(a_ref[...], w_ref[...])

    return pl.pallas_call(
        body, out_shape=jax.ShapeDtypeStruct((s, d), F32), grid=(s // tm, N_DEV),
        in_specs=[pl.BlockSpec((tm, FFN_CB), lambda i, j: (i, j)),
                  pl.BlockSpec((None, d, FFN_CB), lambda i, j: (_ffn_dev(j), 0, 0)),
                  pl.BlockSpec((tm, d), lambda i, j: (i, 0))],
        out_specs=pl.BlockSpec((tm, d), lambda i, j: (i, 0)), name=name,
        compiler_params=_cparams(("parallel", "arbitrary")))(dup, wg, add)


def _mm_up_dw(h, dup, ts, name):
    s, d = h.shape

    def body(a_ref, b_ref, o_ref):
        @pl.when(pl.program_id(1) == 0)
        def _():
            o_ref[...] = jnp.zeros_like(o_ref)
        o_ref[...] += _dot_tn(a_ref[...], b_ref[...])

    return pl.pallas_call(
        body, out_shape=jax.ShapeDtypeStruct((N_DEV, d, FFN_CB), F32), grid=(N_DEV, s // ts),
        in_specs=[pl.BlockSpec((ts, d), lambda j, k: (k, 0)), pl.BlockSpec((ts, FFN_CB), lambda j, k: (k, j))],
        out_specs=pl.BlockSpec((None, d, FFN_CB), lambda j, k: (_ffn_dev(j), 0, 0)), name=name,
        compiler_params=_cparams(("parallel", "arbitrary")))(h, dup)


def _ln_fwd(a, b, g, bias, name):
    s, d = a.shape
    tm = 512

    def body(a_ref, b_ref, g_ref, bias_ref, r_ref, h_ref):
        r = ALPHA * a_ref[...] + b_ref[...]
        mu = jnp.mean(r, axis=-1, keepdims=True)
        xc = r - mu
        var = jnp.mean(xc * xc, axis=-1, keepdims=True)
        r_ref[...] = r
        h_ref[...] = xc * lax.rsqrt(var + LN_EPS) * g_ref[...] + bias_ref[...]

    row = pl.BlockSpec((tm, d), lambda i: (i, 0))
    vec = pl.BlockSpec((1, d), lambda i: (0, 0))
    return pl.pallas_call(
        body, out_shape=(jax.ShapeDtypeStruct((s, d), F32), jax.ShapeDtypeStruct((s, d), F32)),
        grid=(s // tm,), in_specs=[row, row, vec, vec], out_specs=(row, row), name=name,
        compiler_params=_cparams(("parallel",)))(a, b, g, bias)


def _ln_bwd(r, dh, g, name, dep=None):
    s, d = r.shape
    tm = 512

    def body(r_ref, dh_ref, g_ref, *rest):
        dr_ref, dg_ref, db_ref = rest[-3:]

        @pl.when(pl.program_id(0) == 0)
        def _():
            dg_ref[...] = jnp.zeros_like(dg_ref)
            db_ref[...] = jnp.zeros_like(db_ref)
        rr = r_ref[...]
        dh_ = dh_ref[...]
        mu = jnp.mean(rr, axis=-1, keepdims=True)
        xc = rr - mu
        var = jnp.mean(xc * xc, axis=-1, keepdims=True)
        rstd = lax.rsqrt(var + LN_EPS)
        xh = xc * rstd
        dxh = dh_ * g_ref[...]
        m1 = jnp.mean(dxh, axis=-1, keepdims=True)
        m2 = jnp.mean(dxh * xh, axis=-1, keepdims=True)
        dr_ref[...] = rstd * (dxh - m1 - xh * m2)
        dg_ref[...] += jnp.sum(dh_ * xh, axis=0, keepdims=True)
        db_ref[...] += jnp.sum(dh_, axis=0, keepdims=True)

    row = pl.BlockSpec((tm, d), lambda i: (i, 0))
    vec = pl.BlockSpec((1, d), lambda i: (0, 0))
    dep_specs, dep_ops = _dep_args(dep)
    return pl.pallas_call(
        body, out_shape=(jax.ShapeDtypeStruct((s, d), F32), jax.ShapeDtypeStruct((1, d), F32),
                         jax.ShapeDtypeStruct((1, d), F32)),
        grid=(s // tm,), in_specs=[row, row, vec] + dep_specs, out_specs=(row, vec, vec), name=name,
        compiler_params=_cparams(("arbitrary",)))(r, dh, g, *dep_ops)


def _loss_head(y, target):
    s, d = y.shape
    tm = 512

    def body(y_ref, t_ref, dy_ref, l_ref):
        @pl.when(pl.program_id(0) == 0)
        def _():
            l_ref[...] = jnp.zeros_like(l_ref)
        e = y_ref[...] - t_ref[...]
        dy_ref[...] = e * (1.0 / d)
        part = 0.5 * jnp.sum(jnp.mean(e * e, axis=-1, keepdims=True), axis=0, keepdims=True)
        l_ref[...] += jnp.broadcast_to(part, l_ref.shape)

    row = pl.BlockSpec((tm, d), lambda i: (i, 0))
    return pl.pallas_call(
        body, out_shape=(jax.ShapeDtypeStruct((s, d), F32), jax.ShapeDtypeStruct((1, LANE), F32)),
        grid=(s // tm,), in_specs=[row, row], out_specs=(row, pl.BlockSpec((1, LANE), lambda i: (0, 0))),
        name="loss_head", compiler_params=_cparams(("arbitrary",)))(y, target)


def _rope_tables(s):
    half = HEAD // 2
    pos = jnp.arange(s, dtype=F32)
    inv = ROPE_THETA ** (-jnp.arange(half, dtype=F32) * 2.0 / HEAD)
    ang = pos[:, None] * inv[None, :]
    cos, sin = jnp.cos(ang), jnp.sin(ang)
    cos = jnp.concatenate([cos, cos, cos, cos], axis=1)
    sin = jnp.concatenate([-sin, sin, -sin, sin], axis=1)
    return cos, sin


def _rotate(x, cos, sin):
    lane = lax.broadcasted_iota(jnp.int32, x.shape, 1)
    partner = jnp.where((lane % HEAD) < HEAD // 2, pltpu.roll(x, LANE - HEAD // 2, axis=1),
                        pltpu.roll(x, HEAD // 2, axis=1))
    return x * cos + partner * sin


def _rope_fwd(proj, cos, sin, name):
    s = proj.shape[0]
    tm = 512
    w = 3 * ATTN_W

    def body(p_ref, c_ref, s_ref, o_ref):
        c, sn = c_ref[...], s_ref[...]
        for j in range(w // LANE):
            x = p_ref[:, j * LANE:(j + 1) * LANE]
            if j < 2 * ATTN_W // LANE:
                x = _rotate(x, c, sn)
            o_ref[:, j * LANE:(j + 1) * LANE] = x.astype(BF16)

    tab = pl.BlockSpec((tm, LANE), lambda i: (i, 0))
    return pl.pallas_call(
        body, out_shape=jax.ShapeDtypeStruct((s, w), BF16), grid=(s // tm,),
        in_specs=[pl.BlockSpec((tm, w), lambda i: (i, 0)), tab, tab],
        out_specs=pl.BlockSpec((tm, w), lambda i: (i, 0)), name=name,
        compiler_params=_cparams(("parallel",)))(proj, cos, sin)


def _dproj_assemble(dqkv_list, dxr, dgate, du, cos, sin, name):
    s = dxr.shape[0]
    tm = 512
    nq = 3 * ATTN_W // LANE

    def body(*refs):
        br = refs[:9]
        dxr_ref, dg_ref, du_ref, c_ref, s_ref, o_ref = refs[9:]
        c, sn = c_ref[...], -s_ref[...]
        for j in range(nq):
            part, jj = divmod(j, ATTN_W // LANE)
            cols = slice(jj * LANE, (jj + 1) * LANE)
            x = br[part][:, cols] + br[3 + part][:, cols] + br[6 + part][:, cols]
            if part < 2:
                x = _rotate(x, c, sn)
            o_ref[:, j * LANE:(j + 1) * LANE] = x.astype(BF16)
        o_ref[:, 3 * ATTN_W:3 * ATTN_W + LRU_W] = dxr_ref[...].astype(BF16)
        o_ref[:, 3 * ATTN_W + LRU_W:3 * ATTN_W + 2 * LRU_W] = dg_ref[...].astype(BF16)
        o_ref[:, 3 * ATTN_W + 2 * LRU_W:] = du_ref[...].astype(BF16)

    a_spec = pl.BlockSpec((tm, ATTN_W), lambda i: (i, 0))
    tab = pl.BlockSpec((tm, LANE), lambda i: (i, 0))
    ordered = [dqkv_list[b][p] for b in range(3) for p in range(3)]
    return pl.pallas_call(
        body, out_shape=jax.ShapeDtypeStruct((s, D_IN), BF16), grid=(s // tm,),
        in_specs=[a_spec] * 9 + [a_spec, a_spec, pl.BlockSpec((tm, S5_W), lambda i: (i, 0)), tab, tab],
        out_specs=pl.BlockSpec((tm, D_IN), lambda i: (i, 0)), name=name,
        compiler_params=_cparams(("parallel",)))(*ordered, dxr, dgate, du, cos, sin)


def _attn_tiles(s, d):
    m = s // d
    tq = min(m, ATTN_TILE)
    return m, tq, tq // ATTN_BLK


def _band_mask(qb):
    qi = lax.broadcasted_iota(jnp.int32, (ATTN_BLK, 2 * ATTN_BLK), 0)
    ki = lax.broadcasted_iota(jnp.int32, (ATTN_BLK, 2 * ATTN_BLK), 1)
    dist = qi + ATTN_BLK - ki
    return (dist >= 0) & (dist <= ATTN_BLK) & ((ki >= ATTN_BLK) | (qb > 0))


def _head_cols(h):
    return (slice(h * HEAD, (h + 1) * HEAD), slice(ATTN_W + h * HEAD, ATTN_W + (h + 1) * HEAD),
            slice(2 * ATTN_W + h * HEAD, 2 * ATTN_W + (h + 1) * HEAD))


def _attn_fwd(qkv, d, name):
    s = qkv.shape[0]
    w3 = 3 * ATTN_W
    m, tq, n = _attn_tiles(s, d)
    qv = qkv.reshape(m, d * w3)
    scale = HEAD ** -0.5

    def body(x_ref, p_ref, o_ref, l_ref):
        b = pl.program_id(1)

        def block(i, first):
            r0 = 0 if first else pl.multiple_of(i * ATTN_BLK, ATTN_BLK)
            rows = pl.ds(r0, ATTN_BLK)
            valid = _band_mask(b * n + i)
            if not first:
                krows = pl.ds(pl.multiple_of(i * ATTN_BLK - ATTN_BLK, ATTN_BLK), 2 * ATTN_BLK)
            for h in range(ATTN_W // HEAD):
                qs, ks, vs = _head_cols(h)
                q = x_ref[rows, qs]
                if first:
                    k = jnp.concatenate([p_ref[:, ks], x_ref[0:ATTN_BLK, ks]], axis=0)
                    v = jnp.concatenate([p_ref[:, vs], x_ref[0:ATTN_BLK, vs]], axis=0)
                else:
                    k = x_ref[krows, ks]
                    v = x_ref[krows, vs]
                sc = jnp.where(valid, _dot_nt(q, k) * scale, -1e30)
                mx = jnp.max(sc, axis=-1, keepdims=True)
                p = jnp.exp(sc - mx)
                l = jnp.sum(p, axis=-1, keepdims=True)
                o_ref[rows, qs] = _dot(p, v) / l
                l_ref[rows, qs] = jnp.broadcast_to(mx + jnp.log(l), (ATTN_BLK, HEAD))

        block(0, True)
        if n > 1:
            def loop(i, carry):
                block(i, False)
                return carry
            lax.fori_loop(1, n, loop, 0)

    shp = jax.ShapeDtypeStruct((m, d * ATTN_W), F32)
    ospec = pl.BlockSpec((tq, ATTN_W), lambda c, b: (b, c))
    out, lse = pl.pallas_call(
        body, out_shape=(shp, shp), grid=(d, m // tq),
        in_specs=[pl.BlockSpec((tq, w3), lambda c, b: (b, c)),
                  pl.BlockSpec((ATTN_BLK, w3), lambda c, b: (jnp.maximum(b * n - 1, 0), c))],
        out_specs=(ospec, ospec), name=name,
        compiler_params=_cparams(("parallel", "parallel")))(qv, qv)
    return out.reshape(s, ATTN_W), lse.reshape(s, ATTN_W)


def _attn_bwd(qkv, o_all, do_all, lse_all, d, name):
    s = qkv.shape[0]
    w3 = 3 * ATTN_W
    m, tq, n = _attn_tiles(s, d)
    nb = m // ATTN_BLK
    qv = qkv.reshape(m, d * w3)
    view = lambda t: t.reshape(m, d * ATTN_W)
    scale = HEAD ** -0.5

    def body(x_ref, p_ref, nx_ref, o_ref, do_ref, l_ref, on_ref, don_ref, ln_ref, dq_ref, dk_ref, dv_ref):
        b = pl.program_id(1)
        dk_ref[...] = jnp.zeros_like(dk_ref)
        dv_ref[...] = jnp.zeros_like(dv_ref)

        def grads(q, k, v, o, do, lse, valid):
            sc = jnp.where(valid, _dot_nt(q, k) * scale, -1e30)
            p = jnp.exp(sc - lse)
            delta = jnp.sum(do * o, axis=-1, keepdims=True)
            return p, p * (_dot_nt(do, v) - delta) * scale

        def block(i, first):
            r0 = 0 if first else pl.multiple_of(i * ATTN_BLK, ATTN_BLK)
            rows = pl.ds(r0, ATTN_BLK)
            valid = _band_mask(b * n + i)
            if not first:
                krows = pl.ds(pl.multiple_of(i * ATTN_BLK - ATTN_BLK, ATTN_BLK), 2 * ATTN_BLK)
            for h in range(ATTN_W // HEAD):
                qs, ks, vs = _head_cols(h)
                q = x_ref[rows, qs]
                do = do_ref[rows, qs]
                if first:
                    k = jnp.concatenate([p_ref[:, ks], x_ref[0:ATTN_BLK, ks]], axis=0)
                    v = jnp.concatenate([p_ref[:, vs], x_ref[0:ATTN_BLK, vs]], axis=0)
                else:
                    k = x_ref[krows, ks]
                    v = x_ref[krows, vs]
                p, ds = grads(q, k, v, o_ref[rows, qs], do, l_ref[rows, qs][:, 0:1], valid)
                dq_ref[rows, qs] = _dot(ds, k)
                if first:
                    dk_ref[0:ATTN_BLK, qs] += _dot_tn(ds[:, ATTN_BLK:], q)
                    dv_ref[0:ATTN_BLK, qs] += _dot_tn(p[:, ATTN_BLK:], do)
                else:
                    dk_ref[krows, qs] += _dot_tn(ds, q)
                    dv_ref[krows, qs] += _dot_tn(p, do)

        block(0, True)
        if n > 1:
            def loop(i, carry):
                block(i, False)
                return carry
            lax.fori_loop(1, n, loop, 0)

        last = slice((n - 1) * ATTN_BLK, n * ATTN_BLK)
        qi = lax.broadcasted_iota(jnp.int32, (ATTN_BLK, ATTN_BLK), 0)
        ki = lax.broadcasted_iota(jnp.int32, (ATTN_BLK, ATTN_BLK), 1)
        valid_next = (qi <= ki) & ((b + 1) * n < nb)
        for h in range(ATTN_W // HEAD):
            qs, ks, vs = _head_cols(h)
            q = nx_ref[:, qs]
            do = don_ref[:, qs]
            p, ds = grads(q, x_ref[last, ks], x_ref[last, vs], on_ref[:, qs], do, ln_ref[:, qs][:, 0:1],
                          valid_next)
            dk_ref[last, qs] += _dot_tn(ds, q)
            dv_ref[last, qs] += _dot_tn(p, do)

    nxt = lambda b: jnp.minimum((b + 1) * n, nb - 1)
    xs = pl.BlockSpec((tq, w3), lambda c, b: (b, c))
    xp = pl.BlockSpec((ATTN_BLK, w3), lambda c, b: (jnp.maximum(b * n - 1, 0), c))
    xn = pl.BlockSpec((ATTN_BLK, w3), lambda c, b: (nxt(b), c))
    a = pl.BlockSpec((tq, ATTN_W), lambda c, b: (b, c))
    an = pl.BlockSpec((ATTN_BLK, ATTN_W), lambda c, b: (nxt(b), c))
    shp = jax.ShapeDtypeStruct((m, d * ATTN_W), F32)
    ov, dov, lv = view(o_all), view(do_all), view(lse_all)
    dq, dk, dv = pl.pallas_call(
        body, out_shape=(shp, shp, shp), grid=(d, m // tq),
        in_specs=[xs, xp, xn, a, a, a, an, an, an], out_specs=(a, a, a), name=name,
        compiler_params=_cparams(("parallel", "parallel")))(qv, qv, qv, ov, dov, lv, ov, dov, lv)
    return dq.reshape(s, ATTN_W), dk.reshape(s, ATTN_W), dv.reshape(s, ATTN_W)


def _rms(x, g):
    ms = jnp.mean(x * x, axis=-1, keepdims=True)
    return x * lax.rsqrt(ms + RMS_EPS) * g


def _rms_bwd(x, g, dy):
    ms = jnp.mean(x * x, axis=-1, keepdims=True)
    r = lax.rsqrt(ms + RMS_EPS)
    dyg = dy * g
    dx = r * dyg - x * (r * r * r) * jnp.mean(x * dyg, axis=-1, keepdims=True)
    return dx, dy * x * r


def _mix_fwd(outs, lses, lru, s5, g, name):
    s = lru.shape[0]
    tm = 256

    def body(o1, o2, o3, l1, l2, l3, lru_ref, s5_ref, g_ref, mixed_ref, o_ref, lse_ref):
        a1, a2, a3 = l1[...], l2[...], l3[...]
        mx = jnp.maximum(jnp.maximum(a1, a2), a3)
        e1, e2, e3 = jnp.exp(a1 - mx), jnp.exp(a2 - mx), jnp.exp(a3 - mx)
        den = e1 + e2 + e3
        o = (e1 * o1[...] + e2 * o2[...] + e3 * o3[...]) / den
        o_ref[...] = o
        lse_ref[...] = mx + jnp.log(den)
        gg = g_ref[...]
        mixed_ref[:, :ATTN_W] = _rms(o, gg[:, :ATTN_W]).astype(BF16)
        mixed_ref[:, ATTN_W:ATTN_W + LRU_W] = _rms(lru_ref[...], gg[:, ATTN_W:ATTN_W + LRU_W]).astype(BF16)
        mixed_ref[:, ATTN_W + LRU_W:] = _rms(s5_ref[...], gg[:, ATTN_W + LRU_W:]).astype(BF16)

    a = pl.BlockSpec((tm, ATTN_W), lambda i: (i, 0))
    s5s = pl.BlockSpec((tm, S5_W), lambda i: (i, 0))
    full = pl.BlockSpec((tm, D_MODEL), lambda i: (i, 0))
    vec = pl.BlockSpec((1, D_MODEL), lambda i: (0, 0))
    return pl.pallas_call(
        body, out_shape=(jax.ShapeDtypeStruct((s, D_MODEL), BF16), jax.ShapeDtypeStruct((s, ATTN_W), F32),
                         jax.ShapeDtypeStruct((s, ATTN_W), F32)),
        grid=(s // tm,), in_specs=[a] * 6 + [a, s5s, vec], out_specs=(full, a, a), name=name,
        compiler_params=_cparams(("parallel",)))(*outs, *lses, lru, s5, g)


def _mix_bwd(dmixed, o, lru, s5, g, name):
    s = lru.shape[0]
    tm = 256

    def body(dm_ref, o_ref, lru_ref, s5_ref, g_ref, do_ref, dlru_ref, ds5_ref, dg_ref):
        @pl.when(pl.program_id(0) == 0)
        def _():
            dg_ref[...] = jnp.zeros_like(dg_ref)
        gg = g_ref[...]
        dm = dm_ref[...]
        dx, dgr = _rms_bwd(o_ref[...], gg[:, :ATTN_W], dm[:, :ATTN_W])
        do_ref[...] = dx
        dg_ref[:, :ATTN_W] += jnp.sum(dgr, axis=0, keepdims=True)
        dx, dgr = _rms_bwd(lru_ref[...], gg[:, ATTN_W:ATTN_W + LRU_W], dm[:, ATTN_W:ATTN_W + LRU_W])
        dlru_ref[...] = dx
        dg_ref[:, ATTN_W:ATTN_W + LRU_W] += jnp.sum(dgr, axis=0, keepdims=True)
        dx, dgr = _rms_bwd(s5_ref[...], gg[:, ATTN_W + LRU_W:], dm[:, ATTN_W + LRU_W:])
        ds5_ref[...] = dx
        dg_ref[:, ATTN_W + LRU_W:] += jnp.sum(dgr, axis=0, keepdims=True)

    a = pl.BlockSpec((tm, ATTN_W), lambda i: (i, 0))
    s5s = pl.BlockSpec((tm, S5_W), lambda i: (i, 0))
    full = pl.BlockSpec((tm, D_MODEL), lambda i: (i, 0))
    vec = pl.BlockSpec((1, D_MODEL), lambda i: (0, 0))
    return pl.pallas_call(
        body, out_shape=(jax.ShapeDtypeStruct((s, ATTN_W), F32), jax.ShapeDtypeStruct((s, LRU_W), F32),
                         jax.ShapeDtypeStruct((s, S5_W), F32), jax.ShapeDtypeStruct((1, D_MODEL), F32)),
        grid=(s // tm,), in_specs=[full, a, a, s5s, vec], out_specs=(a, a, s5s, vec), name=name,
        compiler_params=_cparams(("arbitrary",)))(dmixed, o, lru, s5, g)


def _lru_gate_math(xc, pre_r, pre_i, lam):
    r = _sigmoid(pre_r)
    i = _sigmoid(pre_i)
    log_a = -LRU_C * r * _softplus(-lam)
    a = jnp.exp(log_a)
    u = jnp.sqrt(-_expm1(2.0 * log_a)) * (i * xc)
    return a, u


def _lru_conv(x, prev8, cw, cb):
    y = cb + cw[LRU_CONV - 1:LRU_CONV, :] * x
    for k in range(LRU_CONV - 1):
        y = y + cw[k:k + 1, :] * _shift_down_prev(x, LRU_CONV - 1 - k, prev8)
    return y


def _lru_specs(s):
    xo = 3 * ATTN_W // LANE
    go = xo + LRU_W // LANE
    xr = pl.BlockSpec((s, LANE), lambda j: (0, xo + j))
    gt = pl.BlockSpec((s, LANE), lambda j: (0, go + j))
    cw = pl.BlockSpec((LRU_CONV, LANE), lambda j: (0, j))
    vec = pl.BlockSpec((1, LANE), lambda j: (0, j))
    wbd = pl.BlockSpec((LANE, LANE), lambda j: (j, j))
    col = pl.BlockSpec((s, LANE), lambda j: (0, j))
    return xr, gt, cw, vec, wbd, col


def _lru_fwd(proj, cw, cb, wr, br, wi, bi, lam, name):
    s = proj.shape[0]
    t = SCAN_T

    def body(xr_ref, gt_ref, cw_ref, cb_ref, wr_ref, br_ref, wi_ref, bi_ref, lam_ref, o_ref):
        cwv, cbv, lamv = cw_ref[...], cb_ref[...], lam_ref[...]
        wrv, wiv, brv, biv = wr_ref[...], wi_ref[...], br_ref[...], bi_ref[...]

        def chunk(c, carry):
            h_c, prev8 = carry
            rows = pl.ds(pl.multiple_of(c * t, t), t)
            x = xr_ref[rows, :]
            xc = _lru_conv(x, prev8, cwv, cbv)
            a, u = _lru_gate_math(xc, _dot(xc, wrv) + brv, _dot(xc, wiv) + biv, lamv)
            acum, hloc = _scan_fwd(a, u)
            h = hloc + acum * h_c
            o_ref[rows, :] = h * _gelu(gt_ref[rows, :])
            return h[t - 1:t, :], x[t - 8:t, :]

        lax.fori_loop(0, s // t, chunk, (jnp.zeros((1, LANE), F32), jnp.zeros((8, LANE), F32)))

    xr, gt, cws, vec, wbd, col = _lru_specs(s)
    return pl.pallas_call(
        body, out_shape=jax.ShapeDtypeStruct((s, LRU_W), F32), grid=(LRU_W // LANE,),
        in_specs=[xr, gt, cws, vec, wbd, vec, wbd, vec, vec], out_specs=col, name=name,
        compiler_params=_cparams(("parallel",)))(proj, proj, cw, cb, wr, br, wi, bi, lam)


def _lru_bwd(proj, dout, cw, cb, wr, br, wi, bi, lam, name):
    s = proj.shape[0]
    t = SCAN_T
    nc = s // t

    def body(xr_ref, gt_ref, do_ref, cw_ref, cb_ref, wr_ref, br_ref, wi_ref, bi_ref, lam_ref,
             dxr_ref, dgt_ref, dcw_ref, dcb_ref, dwr_ref, dbr_ref, dwi_ref, dbi_ref, dlam_ref,
             xc_s, a_s, h_s):
        cwv, cbv, lamv = cw_ref[...], cb_ref[...], lam_ref[...]
        wrv, wiv, brv, biv = wr_ref[...], wi_ref[...], br_ref[...], bi_ref[...]

        def fchunk(c, carry):
            h_c, prev8 = carry
            rows = pl.ds(pl.multiple_of(c * t, t), t)
            x = xr_ref[rows, :]
            xc = _lru_conv(x, prev8, cwv, cbv)
            a, u = _lru_gate_math(xc, _dot(xc, wrv) + brv, _dot(xc, wiv) + biv, lamv)
            acum, hloc = _scan_fwd(a, u)
            h = hloc + acum * h_c
            xc_s[rows, :] = xc
            a_s[rows, :] = a
            h_s[rows, :] = h
            return h[t - 1:t, :], x[t - 8:t, :]

        lax.fori_loop(0, nc, fchunk, (jnp.zeros((1, LANE), F32), jnp.zeros((8, LANE), F32)))

        z1 = jnp.zeros((1, LANE), F32)
        zw = jnp.zeros((LANE, LANE), F32)

        def bchunk(ci, carry):
            g_next, a_next, dxc_next8, dcw, dcb, dwr, dbr, dwi, dbi, dlam = carry
            c = nc - 1 - ci
            t0 = pl.multiple_of(c * t, t)
            rows = pl.ds(t0, t)
            before = pl.ds(pl.multiple_of(jnp.maximum(t0 - 8, 0), 8), 8)
            has_prev = (c > 0).astype(F32)
            x, gt, do = xr_ref[rows, :], gt_ref[rows, :], do_ref[rows, :]
            xc, a, h = xc_s[rows, :], a_s[rows, :], h_s[rows, :]
            prev8_x = xr_ref[before, :] * has_prev
            prev8_h = h_s[before, :] * has_prev
            dgt_ref[rows, :] = do * h * _gelu_grad(gt)
            dh = do * _gelu(gt)
            a_plus = _shift_up_next(a, 1, jnp.broadcast_to(a_next, (8, LANE)))
            acum, gloc = _scan_rev(a_plus, dh)
            g = gloc + acum * g_next
            da = g * _shift_down_prev(h, 1, prev8_h)
            pre_r = _dot(xc, wrv) + brv
            pre_i = _dot(xc, wiv) + biv
            _, vjp = jax.vjp(_lru_gate_math, xc, pre_r, pre_i, lamv)
            dxc, dpre_r, dpre_i, dlam_c = vjp((da, g))
            dxc = dxc + _dot_nt(dpre_r, wrv) + _dot_nt(dpre_i, wiv)
            dx = cwv[LRU_CONV - 1:LRU_CONV, :] * dxc
            dcw_rows = [None] * LRU_CONV
            dcw_rows[LRU_CONV - 1] = jnp.sum(dxc * x, axis=0, keepdims=True)
            for k in range(LRU_CONV - 1):
                sh = LRU_CONV - 1 - k
                dx = dx + cwv[k:k + 1, :] * _shift_up_next(dxc, sh, dxc_next8)
                dcw_rows[k] = jnp.sum(dxc * _shift_down_prev(x, sh, prev8_x), axis=0, keepdims=True)
            dxr_ref[rows, :] = dx
            return (g[0:1, :], a[0:1, :], dxc[0:8, :],
                    dcw + jnp.concatenate(dcw_rows, axis=0),
                    dcb + jnp.sum(dxc, axis=0, keepdims=True),
                    dwr + _dot_tn(xc, dpre_r), dbr + jnp.sum(dpre_r, axis=0, keepdims=True),
                    dwi + _dot_tn(xc, dpre_i), dbi + jnp.sum(dpre_i, axis=0, keepdims=True),
                    dlam + dlam_c)

        init = (z1, z1, jnp.zeros((8, LANE), F32), jnp.zeros((LRU_CONV, LANE), F32), z1, zw, z1, zw, z1, z1)
        res = lax.fori_loop(0, nc, bchunk, init)
        dcw_ref[...] = res[3]
        dcb_ref[...] = res[4]
        dwr_ref[...] = res[5]
        dbr_ref[...] = res[6]
        dwi_ref[...] = res[7]
        dbi_ref[...] = res[8]
        dlam_ref[...] = res[9]

    xr, gt, cws, vec, wbd, col = _lru_specs(s)
    vshape = jax.ShapeDtypeStruct((1, LRU_W), F32)
    wshape = jax.ShapeDtypeStruct((LRU_W, LRU_W), F32)
    return pl.pallas_call(
        body,
        out_shape=(jax.ShapeDtypeStruct((s, LRU_W), F32), jax.ShapeDtypeStruct((s, LRU_W), F32),
                   jax.ShapeDtypeStruct((LRU_CONV, LRU_W), F32), vshape, wshape, vshape, wshape, vshape, vshape),
        grid=(LRU_W // LANE,),
        in_specs=[xr, gt, col, cws, vec, wbd, vec, wbd, vec, vec],
        out_specs=(col, col, cws, vec, wbd, vec, wbd, vec, vec),
        scratch_shapes=[pltpu.VMEM((s, LANE), F32)] * 3, name=name,
        compiler_params=_cparams(("parallel",)))(proj, proj, dout, cw, cb, wr, br, wi, bi, lam)


def _s5_disc_math(a_re, a_im, log_step, bt_re, bt_im):
    step = jnp.exp(log_step)
    dt_re, dt_im = step * a_re, step * a_im
    mag = jnp.exp(dt_re)
    ab_re, ab_im = mag * jnp.cos(dt_im), mag * jnp.sin(dt_im)
    z_re, z_im = ab_re - 1.0, ab_im
    den = a_re * a_re + a_im * a_im
    f_re = (z_re * a_re + z_im * a_im) / den
    f_im = (z_im * a_re - z_re * a_im) / den
    bb_re = f_re * bt_re - f_im * bt_im
    bb_im = f_re * bt_im + f_im * bt_re
    return ab_re, ab_im, bb_re, bb_im


def _s5_disc_fwd(a_re, a_im, log_step, bt_re, bt_im, name):
    def body(ar, ai, ls, br, bi, o1, o2, o3, o4):
        r = _s5_disc_math(ar[...], ai[...], ls[...], br[...], bi[...])
        o1[...], o2[...], o3[...], o4[...] = r

    shp = jax.ShapeDtypeStruct(a_re.shape, F32)
    return pl.pallas_call(body, out_shape=(shp,) * 4, name=name)(a_re, a_im, log_step, bt_re, bt_im)


def _s5_disc_bwd(a_re, a_im, log_step, bt_re, bt_im, cts, name):
    def body(ar, ai, ls, br, bi, c1, c2, c3, c4, o1, o2, o3, o4, o5):
        _, vjp = jax.vjp(_s5_disc_math, ar[...], ai[...], ls[...], br[...], bi[...])
        r = vjp((c1[...], c2[...], c3[...], c4[...]))
        o1[...], o2[...], o3[...], o4[...], o5[...] = r

    shp = jax.ShapeDtypeStruct(a_re.shape, F32)
    return pl.pallas_call(body, out_shape=(shp,) * 5, name=name)(a_re, a_im, log_step, bt_re, bt_im, *cts)


def _s5_u_specs(s):
    uo = (3 * ATTN_W + 2 * LRU_W) // LANE
    return (pl.BlockSpec((s, LANE), lambda j: (0, uo)), pl.BlockSpec((s, LANE), lambda j: (0, uo + 1)))


def _s5_scan_fwd(proj, b_re, b_im, lam_re, lam_im, c_re, c_im, name):
    s = proj.shape[0]
    t = SCAN_T
    nlog = int(math.log2(t))

    def body(u0_ref, u1_ref, bre_ref, bim_ref, lre_ref, lim_ref, cre_ref, cim_ref, xre_ref, xim_ref, y_ref):
        @pl.when(pl.program_id(0) == 0)
        def _():
            y_ref[...] = jnp.zeros_like(y_ref)
        lr, li = lre_ref[...], lim_ref[...]
        pows = _cpowers(lr, li, nlog)
        first = _rows((t, LANE)) == 0
        tab_r, tab_i = _cscan(jnp.where(first, lr, 0.0), jnp.where(first, li, 0.0), pows, False)
        bre, bim, cre, cim = bre_ref[...], bim_ref[...], cre_ref[...], cim_ref[...]

        def chunk(c, carry):
            cr, ci = carry
            rows = pl.ds(pl.multiple_of(c * t, t), t)
            u = jnp.concatenate([u0_ref[rows, :], u1_ref[rows, :]], axis=1).astype(BF16)
            xr, xi = _cscan(_dot(u, bre), _dot(u, bim), pows, False)
            xr, xi = xr + tab_r * cr - tab_i * ci, xi + tab_r * ci + tab_i * cr
            xre_ref[rows, :] = xr
            xim_ref[rows, :] = xi
            y_ref[rows, :] += _dot(xr, cre) - _dot(xi, cim)
            return xr[t - 1:t, :], xi[t - 1:t, :]

        z = jnp.zeros((1, LANE), F32)
        lax.fori_loop(0, s // t, chunk, (z, z))

    u0, u1 = _s5_u_specs(s)
    bsp = pl.BlockSpec((S5_W, LANE), lambda j: (0, j))
    csp = pl.BlockSpec((LANE, S5_W), lambda j: (j, 0))
    vec = pl.BlockSpec((1, LANE), lambda j: (0, j))
    xsp = pl.BlockSpec((s, LANE), lambda j: (0, j))
    ysp = pl.BlockSpec((s, S5_W), lambda j: (0, 0))
    xshape = jax.ShapeDtypeStruct((s, S5_STATES), F32)
    return pl.pallas_call(
        body, out_shape=(xshape, xshape, jax.ShapeDtypeStruct((s, S5_W), F32)),
        grid=(S5_STATES // LANE,), in_specs=[u0, u1, bsp, bsp, vec, vec, csp, csp],
        out_specs=(xsp, xsp, ysp), name=name,
        compiler_params=_cparams(("arbitrary",)))(proj, proj, b_re, b_im, lam_re, lam_im, c_re, c_im)


def _s5_scan_bwd(proj, dy, du_init, x_re, x_im, b_re, b_im, lam_re, lam_im, c_re, c_im, name):
    s = proj.shape[0]
    t = SCAN_T
    nc = s // t
    nlog = int(math.log2(t))

    def body(u0_ref, u1_ref, dy_ref, dui_ref, xre_ref, xim_ref, bre_ref, bim_ref, lre_ref, lim_ref,
             cre_ref, cim_ref, du_ref, dlr_ref, dli_ref, dbr_ref, dbi_ref, dcr_ref, dci_ref):
        @pl.when(pl.program_id(0) == 0)
        def _():
            du_ref[...] = dui_ref[...]
        mr, mi = lre_ref[...], -lim_ref[...]
        pows = _cpowers(mr, mi, nlog)
        last = _rows((t, LANE)) == t - 1
        tab_r, tab_i = _cscan(jnp.where(last, mr, 0.0), jnp.where(last, mi, 0.0), pows, True)
        bre, bim, cre, cim = bre_ref[...], bim_ref[...], cre_ref[...], cim_ref[...]
        dbr_ref[...] = jnp.zeros_like(dbr_ref)
        dbi_ref[...] = jnp.zeros_like(dbi_ref)
        dcr_ref[...] = jnp.zeros_like(dcr_ref)
        dci_ref[...] = jnp.zeros_like(dci_ref)

        def chunk(ci_, carry):
            gnr, gni, dlr, dli = carry
            c = nc - 1 - ci_
            t0 = pl.multiple_of(c * t, t)
            rows = pl.ds(t0, t)
            before = pl.ds(pl.multiple_of(jnp.maximum(t0 - 8, 0), 8), 8)
            has_prev = (c > 0).astype(F32)
            dyc = dy_ref[rows, :].astype(BF16)
            u = jnp.concatenate([u0_ref[rows, :], u1_ref[rows, :]], axis=1).astype(BF16)
            gr, gi = _cscan(_dot_nt(dyc, cre), -_dot_nt(dyc, cim), pows, True)
            gr, gi = gr + tab_r * gnr - tab_i * gni, gi + tab_r * gni + tab_i * gnr
            xr, xi = xre_ref[rows, :], xim_ref[rows, :]
            xpr = _shift_down_prev(xr, 1, xre_ref[before, :] * has_prev)
            xpi = _shift_down_prev(xi, 1, xim_ref[before, :] * has_prev)
            dlr = dlr + jnp.sum(gr * xpr + gi * xpi, axis=0, keepdims=True)
            dli = dli + jnp.sum(gi * xpr - gr * xpi, axis=0, keepdims=True)
            du_ref[rows, :] += _dot_nt(gr, bre) + _dot_nt(gi, bim)
            dbr_ref[...] += _dot_tn(u, gr)
            dbi_ref[...] += _dot_tn(u, gi)
            dcr_ref[...] += _dot_tn(xr, dyc)
            dci_ref[...] -= _dot_tn(xi, dyc)
            return gr[0:1, :], gi[0:1, :], dlr, dli

        z = jnp.zeros((1, LANE), F32)
        res = lax.fori_loop(0, nc, chunk, (z, z, z, z))
        dlr_ref[...] = res[2]
        dli_ref[...] = res[3]

    u0, u1 = _s5_u_specs(s)
    bsp = pl.BlockSpec((S5_W, LANE), lambda j: (0, j))
    csp = pl.BlockSpec((LANE, S5_W), lambda j: (j, 0))
    vec = pl.BlockSpec((1, LANE), lambda j: (0, j))
    xsp = pl.BlockSpec((s, LANE), lambda j: (0, j))
    ysp = pl.BlockSpec((s, S5_W), lambda j: (0, 0))
    return pl.pallas_call(
        body,
        out_shape=(jax.ShapeDtypeStruct((s, S5_W), F32),
                   jax.ShapeDtypeStruct((1, S5_STATES), F32), jax.ShapeDtypeStruct((1, S5_STATES), F32),
                   jax.ShapeDtypeStruct((S5_W, S5_STATES), F32), jax.ShapeDtypeStruct((S5_W, S5_STATES), F32),
                   jax.ShapeDtypeStruct((S5_STATES, S5_W), F32), jax.ShapeDtypeStruct((S5_STATES, S5_W), F32)),
        grid=(S5_STATES // LANE,),
        in_specs=[u0, u1, ysp, ysp, xsp, xsp, bsp, bsp, vec, vec, csp, csp],
        out_specs=(ysp, vec, vec, bsp, bsp, csp, csp), name=name,
        compiler_params=_cparams(("arbitrary",)))(
            proj, proj, dy, du_init, x_re, x_im, b_re, b_im, lam_re, lam_im, c_re, c_im)


def _s5_out_fwd(proj, y_acc, dvec, w_glu, b_glu, name):
    s = proj.shape[0]
    tm = 512
    uo = (3 * ATTN_W + 2 * LRU_W) // LANE

    def body(u0_ref, u1_ref, y_ref, d_ref, w_ref, b_ref, o_ref, yp_ref):
        u = jnp.concatenate([u0_ref[...], u1_ref[...]], axis=1)
        y = y_ref[...] + d_ref[...] * u
        yp_ref[...] = y
        yg = _gelu(y)
        o_ref[...] = yg * _sigmoid(_dot(yg, w_ref[...]) + b_ref[...])

    u0 = pl.BlockSpec((tm, LANE), lambda i: (i, uo))
    u1 = pl.BlockSpec((tm, LANE), lambda i: (i, uo + 1))
    row = pl.BlockSpec((tm, S5_W), lambda i: (i, 0))
    vec = pl.BlockSpec((1, S5_W), lambda i: (0, 0))
    wsp = pl.BlockSpec((S5_W, S5_W), lambda i: (0, 0))
    shp = jax.ShapeDtypeStruct((s, S5_W), F32)
    return pl.pallas_call(
        body, out_shape=(shp, shp), grid=(s // tm,), in_specs=[u0, u1, row, vec, wsp, vec],
        out_specs=(row, row), name=name,
        compiler_params=_cparams(("parallel",)))(proj, proj, y_acc, dvec, w_glu, b_glu)


def _s5_out_bwd(proj, y_pre, dout, dvec, w_glu, b_glu, name):
    s = proj.shape[0]
    tm = 512
    uo = (3 * ATTN_W + 2 * LRU_W) // LANE

    def body(u0_ref, u1_ref, y_ref, do_ref, d_ref, w_ref, b_ref, dy_ref, dud_ref, dd_ref, dw_ref, db_ref):
        @pl.when(pl.program_id(0) == 0)
        def _():
            dd_ref[...] = jnp.zeros_like(dd_ref)
            dw_ref[...] = jnp.zeros_like(dw_ref)
            db_ref[...] = jnp.zeros_like(db_ref)
        u = jnp.concatenate([u0_ref[...], u1_ref[...]], axis=1)
        y = y_ref[...]
        do = do_ref[...]
        yg = _gelu(y)
        sg = _sigmoid(_dot(yg, w_ref[...]) + b_ref[...])
        dz = do * yg * sg * (1.0 - sg)
        dyg = do * sg + _dot_nt(dz, w_ref[...])
        dy = dyg * _gelu_grad(y)
        dy_ref[...] = dy
        dud_ref[...] = d_ref[...] * dy
        dd_ref[...] += jnp.sum(dy * u, axis=0, keepdims=True)
        dw_ref[...] += _dot_tn(yg, dz)
        db_ref[...] += jnp.sum(dz, axis=0, keepdims=True)

    u0 = pl.BlockSpec((tm, LANE), lambda i: (i, uo))
    u1 = pl.BlockSpec((tm, LANE), lambda i: (i, uo + 1))
    row = pl.BlockSpec((tm, S5_W), lambda i: (i, 0))
    vec = pl.BlockSpec((1, S5_W), lambda i: (0, 0))
    wsp = pl.BlockSpec((S5_W, S5_W), lambda i: (0, 0))
    shp = jax.ShapeDtypeStruct((s, S5_W), F32)
    vshape = jax.ShapeDtypeStruct((1, S5_W), F32)
    return pl.pallas_call(
        body, out_shape=(shp, shp, vshape, jax.ShapeDtypeStruct((S5_W, S5_W), F32), vshape),
        grid=(s // tm,), in_specs=[u0, u1, row, row, vec, wsp, vec],
        out_specs=(row, row, vec, wsp, vec), name=name,
        compiler_params=_cparams(("arbitrary",)))(proj, proj, y_pre, dout, dvec, w_glu, b_glu)


def _ffn_conv(x, prev8, cw, cb):
    y = cb + cw[FFN_CONV - 1:FFN_CONV, :] * x
    for k in range(FFN_CONV - 1):
        y = y + cw[k:k + 1, :] * _shift_down_prev(x, FFN_CONV - 1 - k, prev8)
    return y


def _ffn_act_fwd(up, cw, cb, name):
    s = up.shape[0]
    tm = 256
    tb = 2 * FFN_CB

    def body(x_ref, p_ref, cw_ref, cb_ref, o_ref):
        prev8 = p_ref[...] * (pl.program_id(1) > 0).astype(F32)
        y = _ffn_conv(x_ref[...], prev8, cw_ref[...], cb_ref[...])
        o_ref[...] = (_gelu(y[:, :FFN_CB]) * y[:, FFN_CB:]).astype(BF16)

    main = pl.BlockSpec((tm, tb), lambda j, i: (i, j))
    prev = pl.BlockSpec((8, tb), lambda j, i: (jnp.maximum(i * (tm // 8) - 1, 0), j))
    return pl.pallas_call(
        body, out_shape=jax.ShapeDtypeStruct((s, D_FF), BF16), grid=(D_FF // FFN_CB, s // tm),
        in_specs=[main, prev, pl.BlockSpec((FFN_CONV, tb), lambda j, i: (0, j)),
                  pl.BlockSpec((1, tb), lambda j, i: (0, j))],
        out_specs=pl.BlockSpec((tm, FFN_CB), lambda j, i: (i, j)), name=name,
        compiler_params=_cparams(("parallel", "parallel")))(up, up, cw, cb)


def _ffn_act_bwd(up, dact, cw, cb, name):
    s = up.shape[0]
    tm = 256
    tb = 2 * FFN_CB
    nr = s // tm

    def body(x_ref, p_ref, n_ref, da_ref, dan_ref, cw_ref, cb_ref, dup_ref, dcw_ref, dcb_ref):
        i = pl.program_id(1)

        @pl.when(i == 0)
        def _():
            dcw_ref[...] = jnp.zeros_like(dcw_ref)
            dcb_ref[...] = jnp.zeros_like(dcb_ref)
        has_next = (i < nr - 1).astype(F32)
        prev8 = p_ref[...] * (i > 0).astype(F32)
        cwv = cw_ref[...]
        x = x_ref[...]
        xe = jnp.concatenate([x, n_ref[...]], axis=0)
        dae = jnp.concatenate([da_ref[...], dan_ref[...] * has_next], axis=0)
        y = _ffn_conv(xe, prev8, cwv, cb_ref[...])
        gate, val = y[:, :FFN_CB], y[:, FFN_CB:]
        dy = jnp.concatenate([dae * val * _gelu_grad(gate), dae * _gelu(gate)], axis=1)
        dym = dy[:tm, :]
        dx = cwv[FFN_CONV - 1:FFN_CONV, :] * dym
        dcw_rows = [None] * FFN_CONV
        dcw_rows[FFN_CONV - 1] = jnp.sum(dym * x, axis=0, keepdims=True)
        for k in range(FFN_CONV - 1):
            sh = FFN_CONV - 1 - k
            dx = dx + cwv[k:k + 1, :] * pltpu.roll(dy, tm + 8 - sh, axis=0)[:tm, :]
            dcw_rows[k] = jnp.sum(dym * _shift_down_prev(x, sh, prev8), axis=0, keepdims=True)
        dup_ref[...] = dx.astype(BF16)
        dcw_ref[...] += jnp.concatenate(dcw_rows, axis=0)
        dcb_ref[...] += jnp.sum(dym, axis=0, keepdims=True)

    main = pl.BlockSpec((tm, tb), lambda j, i: (i, j))
    prev = pl.BlockSpec((8, tb), lambda j, i: (jnp.maximum(i * (tm // 8) - 1, 0), j))
    nxt = pl.BlockSpec((8, tb), lambda j, i: (jnp.minimum((i + 1) * (tm // 8), s // 8 - 1), j))
    da = pl.BlockSpec((tm, FFN_CB), lambda j, i: (i, j))
    dan = pl.BlockSpec((8, FFN_CB), lambda j, i: (jnp.minimum((i + 1) * (tm // 8), s // 8 - 1), j))
    cws = pl.BlockSpec((FFN_CONV, tb), lambda j, i: (0, j))
    cbs = pl.BlockSpec((1, tb), lambda j, i: (0, j))
    return pl.pallas_call(
        body, out_shape=(jax.ShapeDtypeStruct((s, 2 * D_FF), BF16),
                         jax.ShapeDtypeStruct((FFN_CONV, 2 * D_FF), F32),
                         jax.ShapeDtypeStruct((1, 2 * D_FF), F32)),
        grid=(D_FF // FFN_CB, nr), in_specs=[main, prev, nxt, da, dan, cws, cbs],
        out_specs=(main, cws, cbs), name=name,
        compiler_params=_cparams(("parallel", "arbitrary")))(up, up, up, dact, dact, cw, cb)


def _adamw_sum(landed, w, m, v, layer, prev, name):
    _, r, c = landed.shape
    nl = w.shape[0]
    tm = 8
    for cand in (512, 256, 128, 64, 32, 16):
        if r % cand == 0 and N_DEV * cand * c * 4 <= 4 * 1024 * 1024:
            tm = cand
            break

    def body(*refs):
        ld_ref, w_ref, m_ref, v_ref = refs[:4]
        g_ref, d_ref, mo_ref, vo_ref = refs[-4:]
        gg = ld_ref[0]
        for k in range(1, N_DEV):
            gg = gg + ld_ref[k]
        mn = ADAM_B1 * m_ref[...] + (1.0 - ADAM_B1) * gg
        vn = ADAM_B2 * v_ref[...] + (1.0 - ADAM_B2) * (gg * gg)
        m_hat = mn / (1.0 - ADAM_B1 ** ADAM_STEP)
        v_hat = vn / (1.0 - ADAM_B2 ** ADAM_STEP)
        g_ref[...] = gg
        d_ref[...] = -ADAM_LR * (m_hat / (jnp.sqrt(v_hat) + ADAM_EPS) + ADAM_WD * w_ref[...])
        mo_ref[...] = mn
        vo_ref[...] = vn

    blk = pl.BlockSpec((None, tm, c), lambda i: (layer, i, 0))
    in_specs = [pl.BlockSpec((N_DEV, tm, c), lambda i: (0, i, 0)), blk, blk, blk]
    args = [landed, w, m, v]
    aliases = {}
    if prev is not None:
        in_specs += [pl.BlockSpec(memory_space=pl.ANY)] * 4
        args += list(prev)
        aliases = {4 + k: k for k in range(4)}
    shp = jax.ShapeDtypeStruct((nl, r, c), F32)
    return pl.pallas_call(
        body, out_shape=(shp,) * 4, grid=(r // tm,), in_specs=in_specs, out_specs=(blk,) * 4,
        input_output_aliases=aliases, name=name, compiler_params=_cparams(("parallel",)))(*args)


def _all_gather(shards, name):
    na = len(shards)

    def body(*refs):
        x_refs, out_refs = refs[:na], refs[na:2 * na]
        send_sems, recv_sems, local_sems = refs[2 * na:]
        x, y, c = lax.axis_index("x"), lax.axis_index("y"), lax.axis_index("c")
        me, sibling = (x, y, c), (x, y, 1 - c)
        chips = [(1 - x, y), (x, 1 - y), (1 - x, 1 - y)]

        def copy(a, k, block, to, src=None):
            dst = out_refs[a].at[4 * block[0] + 2 * block[1] + block[2]]
            return pltpu.make_async_remote_copy(
                src_ref=dst if src is None else src, dst_ref=dst,
                send_sem=send_sems.at[7 * a + k], recv_sem=recv_sems.at[7 * a + k],
                device_id=to, device_id_type=pl.DeviceIdType.MESH)

        mine, first, passed = [], [], []
        for a in range(na):
            cp = pltpu.make_async_copy(x_refs[a], out_refs[a].at[4 * x + 2 * y + c], local_sems.at[a])
            cp.start()
            mine.append(cp)
            cps = [copy(a, 0, me, sibling, src=x_refs[a])]
            cps += [copy(a, 1 + j, me, (*chip, c), src=x_refs[a]) for j, chip in enumerate(chips)]
            for cp in cps:
                cp.start()
            first += cps
        for j, chip in enumerate(chips):
            for a in range(na):
                copy(a, 1 + j, (*chip, c), me).wait_recv()
                cp = copy(a, 4 + j, (*chip, c), sibling)
                cp.start()
                passed.append(cp)
        for a in range(na):
            copy(a, 0, sibling, me).wait_recv()
            for j, chip in enumerate(chips):
                copy(a, 4 + j, (*chip, 1 - c), me).wait_recv()
        for cp in first + passed:
            cp.wait_send()
        for cp in mine:
            cp.wait()

    anyspec = pl.BlockSpec(memory_space=pl.ANY)
    return pl.pallas_call(
        body, out_shape=tuple(jax.ShapeDtypeStruct((N_DEV,) + t.shape, t.dtype) for t in shards),
        in_specs=[anyspec] * na, out_specs=tuple([anyspec] * na),
        scratch_shapes=[pltpu.SemaphoreType.DMA((7 * na,)), pltpu.SemaphoreType.DMA((7 * na,)),
                        pltpu.SemaphoreType.DMA((na,))],
        name=name)(*shards)


def _all_to_all(bufs, name):
    na = len(bufs)

    def body(*refs):
        b_refs, out_refs = refs[:na], refs[na:2 * na]
        send_sems, recv_sems, local_sems = refs[2 * na:]
        x, y, c = lax.axis_index("x"), lax.axis_index("y"), lax.axis_index("c")
        me = 4 * x + 2 * y + c
        copies = []
        for a in range(na):
            cp = pltpu.make_async_copy(b_refs[a].at[me], out_refs[a].at[me], local_sems.at[a])
            cp.start()
            copies.append(cp)
        for k in range(1, N_DEV):
            px = x ^ ((k >> 2) & 1)
            py = y ^ ((k >> 1) & 1)
            pc = c ^ (k & 1)
            for a in range(na):
                cp = pltpu.make_async_remote_copy(
                    src_ref=b_refs[a].at[4 * px + 2 * py + pc], dst_ref=out_refs[a].at[me],
                    send_sem=send_sems.at[7 * a + k - 1], recv_sem=recv_sems.at[7 * a + k - 1],
                    device_id=(px, py, pc), device_id_type=pl.DeviceIdType.MESH)
                cp.start()
                copies.append(cp)
        for cp in copies:
            cp.wait()

    anyspec = pl.BlockSpec(memory_space=pl.ANY)
    return pl.pallas_call(
        body, out_shape=tuple(jax.ShapeDtypeStruct(t.shape, t.dtype) for t in bufs),
        in_specs=[anyspec] * na, out_specs=tuple([anyspec] * na),
        scratch_shapes=[pltpu.SemaphoreType.DMA((7 * na,)), pltpu.SemaphoreType.DMA((7 * na,)),
                        pltpu.SemaphoreType.DMA((na,))],
        name=name)(*bufs)


_HBM = pl.BlockSpec(memory_space=pltpu.HBM)
_SEM = pl.BlockSpec(memory_space=pltpu.SEMAPHORE)
_EFFECT = pltpu.SideEffectType.DATAFLOW_SIDE_EFFECTING


def _exchange_copies(src_refs, land_refs, send_sems, recv_sems, local_sems, gather):
    x, y, c = lax.axis_index("x"), lax.axis_index("y"), lax.axis_index("c")
    me = 4 * x + 2 * y + c
    local, remote = [], []
    for a, (src, land) in enumerate(zip(src_refs, land_refs)):
        local.append(pltpu.make_async_copy(src if gather else src.at[me], land.at[me], local_sems.at[a]))
    for k in range(1, N_DEV):
        px = x ^ ((k >> 2) & 1)
        py = y ^ ((k >> 1) & 1)
        pc = c ^ (k & 1)
        for a, (src, land) in enumerate(zip(src_refs, land_refs)):
            remote.append(pltpu.make_async_remote_copy(
                src_ref=src if gather else src.at[4 * px + 2 * py + pc], dst_ref=land.at[me],
                send_sem=send_sems.at[7 * a + k - 1], recv_sem=recv_sems.at[7 * a + k - 1],
                device_id=(px, py, pc), device_id_type=pl.DeviceIdType.MESH))
    return local, remote


def _exchange_start(srcs, gather, name, dep=None):
    na = len(srcs)
    lands = [lax.empty(((N_DEV,) + t.shape) if gather else t.shape, t.dtype) for t in srcs]

    def body(*refs):
        src_refs, land_refs = refs[:na], refs[na:2 * na]
        nin = 2 * na + (0 if dep is None else 1)
        send_sems, recv_sems, local_sems = refs[nin:nin + 3]
        token = refs[-1]
        local, remote = _exchange_copies(src_refs, land_refs, send_sems, recv_sems, local_sems, gather)
        for cp in local + remote:
            cp.start()
        token[...] = jnp.zeros_like(token)

    dep_specs, dep_ops = _dep_args(dep)
    hbm = lambda t: pltpu.HBM(t.shape, t.dtype)
    out = pl.pallas_call(
        body, name=name,
        out_shape=(pltpu.SemaphoreType.DMA((7 * na,)), pltpu.SemaphoreType.DMA((7 * na,)),
                   pltpu.SemaphoreType.DMA((na,)), *[hbm(t) for t in srcs], *[hbm(t) for t in lands],
                   jax.ShapeDtypeStruct((8, LANE), F32)),
        in_specs=[_HBM] * (2 * na) + dep_specs,
        out_specs=(_SEM, _SEM, _SEM, *[_HBM] * (2 * na), pl.BlockSpec(memory_space=pltpu.VMEM)),
        input_output_aliases={i: 3 + i for i in range(2 * na)},
        compiler_params=pltpu.CompilerParams(has_side_effects=_EFFECT),
    )(*[pltpu.with_memory_space_constraint(t, pltpu.HBM) for t in srcs + lands], *dep_ops)
    return (out[:3], out[3:3 + na], out[3 + na:3 + 2 * na]), out[-1]


def _exchange_wait(handle, gather, after, name):
    sems, srcs, lands = handle
    na = len(srcs)

    def body(*refs):
        src_refs, land_refs = refs[:na], refs[na:2 * na]
        send_sems, recv_sems, local_sems = refs[2 * na:2 * na + 3]
        local, remote = _exchange_copies(src_refs, land_refs, send_sems, recv_sems, local_sems, gather)
        for cp in remote:
            cp.wait_send()
            cp.wait_recv()
        for cp in local:
            cp.wait()

    hbm = lambda t: pltpu.HBM(t.shape, t.dtype)
    out = pl.pallas_call(
        body, name=name, out_shape=(*[hbm(t) for t in srcs], *[hbm(t) for t in lands]),
        in_specs=[_HBM] * (2 * na) + [_SEM] * 3 + [pl.BlockSpec(memory_space=pl.ANY)],
        out_specs=tuple([_HBM] * (2 * na)), input_output_aliases={i: i for i in range(2 * na)},
        compiler_params=pltpu.CompilerParams(has_side_effects=_EFFECT),
    )(*srcs, *lands, *sems, after)
    return out[na:]


def _block_diag(w):
    h, a, b = w.shape
    eye = jnp.eye(h, dtype=w.dtype)
    return (w[:, :, None, :] * eye[:, None, :, None]).reshape(h * a, h * b)


def _block_diag_extract(m, h):
    a, b = m.shape[0] // h, m.shape[1] // h
    return jnp.stack([m[i * a:(i + 1) * a, i * b:(i + 1) * b] for i in range(h)], axis=0)


def _ffn_interleave(w):
    lead = w.shape[:-1]
    nb = D_FF // FFN_CB
    return jnp.swapaxes(w.reshape(*lead, 2, nb, FFN_CB), -3, -2).reshape(*lead, 2 * D_FF)


def _ffn_deinterleave(w):
    lead = w.shape[:-1]
    nb = D_FF // FFN_CB
    return jnp.swapaxes(w.reshape(*lead, nb, 2, FFN_CB), -3, -2).reshape(*lead, 2 * D_FF)


def _gather_full(gathered, axis):
    shape = list(gathered.shape[1:])
    shape[axis] *= N_DEV
    return jnp.moveaxis(gathered, 0, axis).reshape(shape)


def _scatter_blocks(full, axis):
    shape = list(full.shape)
    shape[axis:axis + 1] = [N_DEV, shape[axis] // N_DEV]
    return jnp.moveaxis(full.reshape(shape), axis, 0)


def _pad_to(flat, mult):
    pad = (-flat.shape[-1]) % mult
    if pad:
        flat = jnp.concatenate([flat, jnp.zeros(flat.shape[:-1] + (pad,), flat.dtype)], axis=-1)
    return flat


def _layer_fwd(h_in, w, cos, sin, l, dep, get_ffn):
    tag = "l%d_" % l
    proj = _mm_nn(h_in, w['w_in'], 512, D_IN, tag + "proj", dep=dep)
    qkv = _rope_fwd(proj, cos, sin, tag + "rope")
    outs, lses = [], []
    for d in DILATIONS:
        o, ls = _attn_fwd(qkv, d, tag + "attn_d%d" % d)
        outs.append(o)
        lses.append(ls)
    lru = _lru_fwd(proj, w['lru_conv_w'], w['lru_conv_b'], w['lru_wr'], w['lru_br'], w['lru_wi'],
                   w['lru_bi'], w['lru_lambda'], tag + "lru")
    x_re, x_im, y_acc = _s5_scan_fwd(proj, w['s5_bb_re'], w['s5_bb_im'], w['s5_lam_re'], w['s5_lam_im'],
                                     w['s5_cc_re'], w['s5_cc_im'], tag + "s5_scan")
    s5, y_pre = _s5_out_fwd(proj, y_acc, w['s5_d'], w['s5_w_glu'], w['s5_b_glu'], tag + "s5_out")
    mixed, attn_o, attn_lse = _mix_fwd(outs, lses, lru, s5, w['mix_norm_g'], tag + "mix")
    mixo = _mm_nn(mixed, w['w_out'], 512, D_MODEL, tag + "out_proj")
    r1, h1 = _ln_fwd(h_in, mixo, w['ln1_g'], w['ln1_b'], tag + "ln1")
    w['w_up_g'], w['w_down'], ffn_dep = get_ffn(l, h1)
    up = _mm_up(h1, w['w_up_g'], 1024, tag + "up_proj", dep=ffn_dep)
    act = _ffn_act_fwd(up, w['ffn_conv_w'], w['ffn_conv_b'], tag + "ffn_act")
    ffn = _mm_nn(act, w['w_down'], 512, D_MODEL, tag + "down_proj")
    r2, h2 = _ln_fwd(h1, ffn, w['ln2_g'], w['ln2_b'], tag + "ln2")
    saved = dict(h_in=h_in, proj=proj, qkv=qkv, lru=lru, x_re=x_re, x_im=x_im, y_pre=y_pre, s5=s5,
                 mixed=mixed, attn_o=attn_o, attn_lse=attn_lse, r1=r1, h1=h1, up=up, act=act, r2=r2)
    return h2, saved


def _layer_bwd_ffn(dh2, sv, w, l, dep=None):
    tag = "l%d_" % l
    g = {}
    dr2, g['ln2_g'], g['ln2_b'] = _ln_bwd(sv['r2'], dh2, w['ln2_g'], tag + "ln2_bwd", dep=dep)
    g['w_down'] = _mm_tn(sv['act'], dr2, 1024, D_MODEL, 512, tag + "down_dw")
    dact = _mm_nt(dr2, w['w_down'], 512, D_FF, tag + "down_dx")
    dup, g['ffn_conv_w'], g['ffn_conv_b'] = _ffn_act_bwd(sv['up'], dact, w['ffn_conv_w'], w['ffn_conv_b'],
                                                        tag + "ffn_act_bwd")
    g['w_up_g'] = _mm_up_dw(sv['h1'], dup, 512, tag + "up_dw")
    dh1 = _mm_up_dx(dup, w['w_up_g'], dr2, ALPHA, 1024, tag + "up_dx")
    return dh1, g


def _layer_bwd_mix(dh1, sv, w, cos, sin, l, dep, after_out_grad):
    tag = "l%d_" % l
    g = {}
    dr1, g['ln1_g'], g['ln1_b'] = _ln_bwd(sv['r1'], dh1, w['ln1_g'], tag + "ln1_bwd", dep=dep)
    g['w_out'] = _mm_tn(sv['mixed'], dr1, 1024, D_MODEL, 512, tag + "out_dw")
    dmixed = _mm_nt(dr1, w['w_out'], 512, D_MODEL, tag + "out_dx", dep=after_out_grad(l, g['w_out']))
    d_o, dlru, ds5, g['mix_norm_g'] = _mix_bwd(dmixed, sv['attn_o'], sv['lru'], sv['s5'], w['mix_norm_g'],
                                               tag + "mix_bwd")
    dy, dud, g['s5_d'], g['s5_w_glu'], g['s5_b_glu'] = _s5_out_bwd(
        sv['proj'], sv['y_pre'], ds5, w['s5_d'], w['s5_w_glu'], w['s5_b_glu'], tag + "s5_out_bwd")
    du, g['s5_lam_re'], g['s5_lam_im'], g['s5_bb_re'], g['s5_bb_im'], g['s5_cc_re'], g['s5_cc_im'] = \
        _s5_scan_bwd(sv['proj'], dy, dud, sv['x_re'], sv['x_im'], w['s5_bb_re'], w['s5_bb_im'],
                     w['s5_lam_re'], w['s5_lam_im'], w['s5_cc_re'], w['s5_cc_im'], tag + "s5_scan_bwd")
    (dxr, dgate, g['lru_conv_w'], g['lru_conv_b'], g['lru_wr'], g['lru_br'], g['lru_wi'], g['lru_bi'],
     g['lru_lambda']) = _lru_bwd(sv['proj'], dlru, w['lru_conv_w'], w['lru_conv_b'], w['lru_wr'],
                                 w['lru_br'], w['lru_wi'], w['lru_bi'], w['lru_lambda'], tag + "lru_bwd")
    dqkv = [_attn_bwd(sv['qkv'], sv['attn_o'], d_o, sv['attn_lse'], d, tag + "attn_bwd_d%d" % d)
            for d in DILATIONS]
    dproj = _dproj_assemble(dqkv, dxr, dgate, du, cos, sin, tag + "dproj")
    g['w_in'] = _mm_tn(sv['h_in'], dproj, 1024, D_IN, 512, tag + "in_dw")
    dh_in = _mm_nt(dproj, w['w_in'], 512, D_MODEL, tag + "in_dx", add=dr1, add_scale=ALPHA)
    return dh_in, g


def _s5_rep(a):
    return jnp.repeat(a, S5_C, axis=0)


def _prepare_layer(p, l):
    w = {}
    for n in ('w_in', 'w_out', 's5_w_glu'):
        w[n] = p[n].astype(BF16)
    w['ffn_conv_w'] = _ffn_interleave(p['ffn_conv_w'])
    w['ffn_conv_b'] = _ffn_interleave(p['ffn_conv_b'])[None, :]
    w['lru_conv_w'] = p['lru_conv_w']
    for n in ('lru_conv_b', 'lru_br', 'lru_bi', 'lru_lambda', 's5_b_glu', 'mix_norm_g',
              'ln1_g', 'ln1_b', 'ln2_g', 'ln2_b'):
        w[n] = p[n][None, :]
    w['lru_wr'] = _block_diag(p['lru_wr']).astype(BF16)
    w['lru_wi'] = _block_diag(p['lru_wi']).astype(BF16)
    w['s5_d'] = p['s5_d'].reshape(1, S5_W)
    disc_in = (_s5_rep(p['s5_a_re']), _s5_rep(p['s5_a_im']),
               _s5_rep(jnp.broadcast_to(p['s5_log_step'][:, None], (S5_G, S5_P))),
               jnp.swapaxes(p['s5_b_re'], 1, 2).reshape(S5_W, S5_P),
               jnp.swapaxes(p['s5_b_im'], 1, 2).reshape(S5_W, S5_P))
    ab_re, ab_im, bb_re, bb_im = _s5_disc_fwd(*disc_in, "l%d_s5_disc" % l)
    w['s5_disc_in'] = disc_in
    w['s5_lam_re'] = ab_re.reshape(S5_G, S5_C, S5_P)[:, 0, :].reshape(1, S5_STATES)
    w['s5_lam_im'] = ab_im.reshape(S5_G, S5_C, S5_P)[:, 0, :].reshape(1, S5_STATES)
    w['s5_bb_re'] = _block_diag(bb_re.reshape(S5_G, S5_C, S5_P)).astype(BF16)
    w['s5_bb_im'] = _block_diag(bb_im.reshape(S5_G, S5_C, S5_P)).astype(BF16)
    w['s5_cc_re'] = _block_diag(jnp.swapaxes(p['s5_c_re'], 1, 2)).astype(BF16)
    w['s5_cc_im'] = _block_diag(jnp.swapaxes(p['s5_c_im'], 1, 2)).astype(BF16)
    return w


def _finish_layer_grads(g, w, l):
    out = {}
    for n in ('w_in', 'w_out', 'w_down', 'w_up_g', 's5_w_glu', 'lru_conv_w'):
        out[n] = g[n]
    out['ffn_conv_w'] = _ffn_deinterleave(g['ffn_conv_w'])
    out['ffn_conv_b'] = _ffn_deinterleave(g['ffn_conv_b'])[0]
    for n in ('lru_conv_b', 'lru_br', 'lru_bi', 'lru_lambda', 's5_b_glu', 'mix_norm_g',
              'ln1_g', 'ln1_b', 'ln2_g', 'ln2_b'):
        out[n] = g[n][0]
    out['lru_wr'] = _block_diag_extract(g['lru_wr'], LRU_W // HEAD)
    out['lru_wi'] = _block_diag_extract(g['lru_wi'], LRU_W // HEAD)
    out['s5_d'] = g['s5_d'].reshape(S5_G, S5_C)
    out['s5_c_re'] = jnp.swapaxes(_block_diag_extract(g['s5_cc_re'], S5_G), 1, 2)
    out['s5_c_im'] = jnp.swapaxes(_block_diag_extract(g['s5_cc_im'], S5_G), 1, 2)
    rep = lambda v: _s5_rep(v.reshape(S5_G, S5_P)) * (1.0 / S5_C)
    cts = (rep(g['s5_lam_re']), rep(g['s5_lam_im']),
           _block_diag_extract(g['s5_bb_re'], S5_G).reshape(S5_W, S5_P),
           _block_diag_extract(g['s5_bb_im'], S5_G).reshape(S5_W, S5_P))
    da_re, da_im, dls, dbt_re, dbt_im = _s5_disc_bwd(*w['s5_disc_in'], cts, "l%d_s5_disc_bwd" % l)
    out['s5_a_re'] = da_re.reshape(S5_G, S5_C, S5_P).sum(axis=1)
    out['s5_a_im'] = da_im.reshape(S5_G, S5_C, S5_P).sum(axis=1)
    out['s5_log_step'] = dls.reshape(S5_G, S5_C * S5_P).sum(axis=1)
    out['s5_b_re'] = jnp.swapaxes(dbt_re.reshape(S5_G, S5_C, S5_P), 1, 2)
    out['s5_b_im'] = jnp.swapaxes(dbt_im.reshape(S5_G, S5_C, S5_P), 1, 2)
    return out


def _run_step(x, target, get_layer, get_ffn, after_ffn_grads, after_out_grad, after_mix_grads):
    cos, sin = _rope_tables(x.shape[0])
    h = x
    ws, saved = [], []
    for l in range(DEPTH):
        p, dep = get_layer(l, h)
        ws.append(_prepare_layer(p, l))
        h, sv = _layer_fwd(h, ws[l], cos, sin, l, dep, get_ffn)
        saved.append(sv)
    dh, loss_vec = _loss_head(h, target)
    dep = None
    for l in reversed(range(DEPTH)):
        dh1, g = _layer_bwd_ffn(dh, saved[l], ws[l], l, dep)
        dep = after_ffn_grads(l, g)
        dh, g_mix = _layer_bwd_mix(dh1, saved[l], ws[l], cos, sin, l, dep, after_out_grad)
        g.update(g_mix)
        dep = after_mix_grads(l, _finish_layer_grads(g, ws[l], l))
    return loss_vec[0, 0], dh


def _local_step(x, target, layers):
    grads = [None] * DEPTH

    def keep(l, g):
        grads[l] = g

    def ffn(l, h1):
        return layers[l]['w_up_g'].astype(BF16), layers[l]['w_down'].astype(BF16), None

    none = lambda l, g: None
    loss, dx = _run_step(x, target, lambda l, h: (layers[l], None), ffn, none, none, keep)
    return loss, dx, grads


def kernel(x, w_in, lru_conv_w, lru_conv_b, lru_wr, lru_br, lru_wi, lru_bi, lru_lambda, s5_a_re, s5_a_im, s5_b_re, s5_b_im, s5_c_re, s5_c_im, s5_d, s5_log_step, s5_w_glu, s5_b_glu, mix_norm_g, w_out, ln1_g, ln1_b, w_up, ffn_conv_w, ffn_conv_b, w_down, ln2_g, ln2_b, loss_target, m_w_in, m_lru_conv_w, m_lru_conv_b, m_lru_wr, m_lru_br, m_lru_wi, m_lru_bi, m_lru_lambda, m_s5_a_re, m_s5_a_im, m_s5_b_re, m_s5_b_im, m_s5_c_re, m_s5_c_im, m_s5_d, m_s5_log_step, m_s5_w_glu, m_s5_b_glu, m_mix_norm_g, m_w_out, m_ln1_g, m_ln1_b, m_w_up, m_ffn_conv_w, m_ffn_conv_b, m_w_down, m_ln2_g, m_ln2_b, v_w_in, v_lru_conv_w, v_lru_conv_b, v_lru_wr, v_lru_br, v_lru_wi, v_lru_bi, v_lru_lambda, v_s5_a_re, v_s5_a_im, v_s5_b_re, v_s5_b_im, v_s5_c_re, v_s5_c_im, v_s5_d, v_s5_log_step, v_s5_w_glu, v_s5_b_glu, v_mix_norm_g, v_w_out, v_ln1_g, v_ln1_b, v_w_up, v_ffn_conv_w, v_ffn_conv_b, v_w_down, v_ln2_g, v_ln2_b):
    args = locals()
    wl = {n: args[n] for n in WEIGHTS}
    ml = {n: args['m_' + n] for n in WEIGHTS}
    vl = {n: args['v_' + n] for n in WEIGHTS}
    me = 4 * lax.axis_index("x") + 2 * lax.axis_index("y") + lax.axis_index("c")

    small_sizes = [int(wl[n].size) for n in SMALL_SHARDED]
    small_flat = _pad_to(jnp.concatenate([wl[n].reshape(-1) for n in SMALL_SHARDED]), 8 * 1024)
    small_all, = _all_gather([small_flat.reshape(-1, 1024)], "gather_small")
    small_all = small_all.reshape(N_DEV, -1)
    small_full, off = {}, 0
    for n, sz in zip(SMALL_SHARDED, small_sizes):
        small_full[n] = _gather_full(small_all[:, off:off + sz].reshape((N_DEV,) + wl[n].shape), SHARD_AXIS[n])
        off += sz
    def mixer_params(l, gathered):
        g_in, g_out = gathered
        p = {n: wl[n][l] for n in REPLICATED}
        p.update({n: small_full[n][l] for n in SMALL_SHARDED})
        p['w_in'] = _gather_full(g_in, 1)
        p['w_out'] = g_out.reshape(D_MODEL, D_MODEL)
        return p

    mix_names, ffn_names = ('w_in', 'w_out'), ('w_up', 'w_down')
    shards = lambda names, l: [wl[n][l].astype(BF16) for n in names]
    mix0 = _all_gather(shards(mix_names, 0), "gather_mix_l0")
    gathers = {}
    gathers[0, 'ffn'], ffn0_token = _exchange_start(shards(ffn_names, 0), True, "gather_ffn_l0_start", dep=mix0[0])
    def get_layer(l, h):
        if l == 0:
            return mixer_params(0, mix0), ffn0_token
        return mixer_params(1, _exchange_wait(gathers[1, 'mix'], True, h, "gather_mix_l1_wait")), None

    def get_ffn(l, h1):
        g_up, g_down = _exchange_wait(gathers[l, 'ffn'], True, h1, "gather_ffn_l%d_wait" % l)
        token = None
        if l == 0:
            gathers[1, 'mix'], token = _exchange_start(shards(mix_names, 1), True, "gather_mix_l1_start", dep=g_up)
            gathers[1, 'ffn'], token = _exchange_start(shards(ffn_names, 1), True, "gather_ffn_l1_start", dep=token)
        return g_up, g_down.reshape(D_FF, D_MODEL), token

    scatters, grads = {}, [None] * DEPTH

    def after_ffn_grads(l, g):
        send = [g['w_up_g'], g['w_down'].reshape(N_DEV, D_FF // N_DEV, D_MODEL)]
        scatters[l, 'ffn'], token = _exchange_start(send, False, "scatter_ffn_l%d_start" % l)
        return token

    def after_out_grad(l, g_out):
        send = [g_out.reshape(N_DEV, D_MODEL // N_DEV, D_MODEL)]
        scatters[l, 'out'], token = _exchange_start(send, False, "scatter_out_l%d_start" % l)
        return token

    def after_mix_grads(l, g):
        grads[l] = g
        scatters[l, 'in'], token = _exchange_start([_scatter_blocks(g['w_in'], 1)], False,
                                                   "scatter_in_l%d_start" % l)
        return token

    loss_local, grad_x = _run_step(x[0], loss_target[0], get_layer, get_ffn, after_ffn_grads, after_out_grad,
                                   after_mix_grads)
    loss = lax.psum(loss_local, AXES)

    results = {}
    big_prev = {n: None for n in BIG}

    def finish_big(l, part, names, after):
        landed = _exchange_wait(scatters[l, part], False, after, "scatter_%s_l%d_wait" % (part, l))
        for n, ld in zip(names, landed):
            big_prev[n] = _adamw_sum(ld, wl[n], ml[n], vl[n], l, big_prev[n], "adamw_%s_l%d" % (n, l))

    for l, part, names in ((1, 'ffn', ffn_names), (1, 'out', ('w_out',)), (1, 'in', ('w_in',)),
                           (0, 'ffn', ffn_names), (0, 'out', ('w_out',))):
        finish_big(l, part, names, grad_x)

    stacked = {n: jnp.stack([grads[l][n] for l in range(DEPTH)], axis=0) for n in SMALL_SHARDED + REPLICATED}
    rep_sizes = [int(wl[n].size) for n in REPLICATED]
    rep_per = -(-sum(rep_sizes) // (N_DEV * 1024)) * 1024

    def rep_flat(tree):
        return _pad_to(jnp.concatenate([tree[n].reshape(-1) for n in REPLICATED]), N_DEV * rep_per)

    rows = [_scatter_blocks(stacked[n], SHARD_AXIS[n]).reshape(N_DEV, -1) for n in SMALL_SHARDED]
    rows.append(rep_flat(stacked).reshape(N_DEV, rep_per))
    small_send = _pad_to(jnp.concatenate(rows, axis=1), 8 * 1024)
    n_own = small_send.shape[1]
    small_landed, = _all_to_all([small_send.reshape(N_DEV, n_own // 1024, 1024)], "scatter_small")

    def own_flat(tree):
        parts = [tree[n].reshape(-1) for n in SMALL_SHARDED]
        parts.append(lax.dynamic_slice(rep_flat(tree), (me * rep_per,), (rep_per,)))
        return _pad_to(jnp.concatenate(parts), 8 * 1024).reshape(1, n_own // 1024, 1024)

    small_res = _adamw_sum(small_landed, own_flat(wl), own_flat(ml), own_flat(vl), 0, None, "adamw_small")

    kinds = ('grad', 'delta', 'm', 'v')
    sh_total = sum(small_sizes)
    for kind, arr in zip(kinds, small_res):
        flat = arr.reshape(-1)
        off = 0
        for n, sz in zip(SMALL_SHARDED, small_sizes):
            results[kind, n] = flat[off:off + sz].reshape(wl[n].shape)
            off += sz
    rep_own = jnp.stack([a.reshape(-1)[sh_total:sh_total + rep_per] for a in small_res])
    rep_all, = _all_gather([rep_own.reshape(4 * rep_per // 1024, 1024)], "gather_replicated")
    finish_big(0, 'in', ('w_in',), rep_all)
    for n in BIG:
        results['grad', n], results['delta', n], results['m', n], results['v', n] = big_prev[n]
    rep_all = rep_all.reshape(N_DEV, 4, rep_per)
    for k, kind in enumerate(kinds):
        flat = rep_all[:, k, :].reshape(-1)
        off = 0
        for n, sz in zip(REPLICATED, rep_sizes):
            results[kind, n] = flat[off:off + sz].reshape(wl[n].shape)
            off += sz

    out = [loss, grad_x[None]]
    for kind in kinds:
        out.extend(results[kind, n] for n in WEIGHTS)
    return tuple(out)
```

```python
import functools
import math

import jax
import jax.numpy as jnp
from jax import lax
from jax.experimental import pallas as pl
from jax.experimental.pallas import tpu as pltpu

F32 = jnp.float32
BF16 = jnp.bfloat16

N_DEV = 8
DEPTH = 2
D_MODEL = 1024
ATTN_W = 384
LRU_W = 384
S5_W = 256
D_IN = 2176
D_FF = 3072
HEAD = 64
ATTN_BLK = 128
ATTN_TILE = 1024
DILATIONS = (1, 4, 16)
S5_G = 16
S5_P = 64
S5_C = 16
S5_STATES = S5_G * S5_P
LRU_C = 8.0
LRU_CONV = 4
FFN_CONV = 3
ROPE_THETA = 10000.0
ALPHA = (2 * DEPTH) ** 0.25
LN_EPS = 1e-5
RMS_EPS = 1e-6
ADAM_LR, ADAM_B1, ADAM_B2, ADAM_EPS, ADAM_WD, ADAM_STEP = 0.001, 0.9, 0.999, 1e-8, 0.01, 10

LANE = 128
SCAN_T = 256
FFN_CB = 2 * D_FF // N_DEV
VMEM_LIMIT = 56 * 1024 * 1024

AXES = ("x", "y", "c")

WEIGHTS = ['w_in', 'lru_conv_w', 'lru_conv_b', 'lru_wr', 'lru_br', 'lru_wi', 'lru_bi', 'lru_lambda',
           's5_a_re', 's5_a_im', 's5_b_re', 's5_b_im', 's5_c_re', 's5_c_im', 's5_d', 's5_log_step',
           's5_w_glu', 's5_b_glu', 'mix_norm_g', 'w_out', 'ln1_g', 'ln1_b', 'w_up', 'ffn_conv_w',
           'ffn_conv_b', 'w_down', 'ln2_g', 'ln2_b']
SHARD_AXIS = {'w_in': 2, 'lru_conv_w': 2, 's5_w_glu': 1, 'w_out': 1, 'w_up': 2, 'ffn_conv_w': 2, 'w_down': 1}
BIG = ['w_in', 'w_out', 'w_up', 'w_down']
SMALL_SHARDED = ['lru_conv_w', 'ffn_conv_w', 's5_w_glu']
REPLICATED = [n for n in WEIGHTS if n not in SHARD_AXIS]


def _cparams(sem=None):
    return pltpu.CompilerParams(dimension_semantics=sem, vmem_limit_bytes=VMEM_LIMIT)


def _ffn_dev(jb):
    return jb // 2 + (N_DEV // 2) * (jb % 2)


def _gelu(x):
    c = math.sqrt(2.0 / math.pi)
    t = jnp.tanh(c * (x + 0.044715 * (x * x * x)))
    return 0.5 * x * (1.0 + t)


def _gelu_grad(x):
    c = math.sqrt(2.0 / math.pi)
    x2 = x * x
    t = jnp.tanh(c * (x + 0.044715 * (x2 * x)))
    return 0.5 * (1.0 + t) + 0.5 * x * (1.0 - t * t) * (c * (1.0 + 3.0 * 0.044715 * x2))


def _sigmoid(x):
    return 1.0 / (1.0 + jnp.exp(-x))


def _log1p(x):
    u = 1.0 + x
    d = u - 1.0
    return jnp.where(d == 0.0, x, jnp.log(u) * (x / jnp.where(d == 0.0, 1.0, d)))


def _softplus(x):
    return jnp.maximum(x, 0.0) + _log1p(jnp.exp(-jnp.abs(x)))


def _expm1(x):
    return jnp.tanh(0.5 * x) * (jnp.exp(x) + 1.0)


def _dot(a, b):
    return jnp.dot(a.astype(BF16), b.astype(BF16), preferred_element_type=F32)


def _dot_nt(a, b):
    return lax.dot_general(a.astype(BF16), b.astype(BF16), (((1,), (1,)), ((), ())),
                           preferred_element_type=F32)


def _dot_tn(a, b):
    return lax.dot_general(a.astype(BF16), b.astype(BF16), (((0,), (0,)), ((), ())),
                           preferred_element_type=F32)


def _rows(shape):
    return lax.broadcasted_iota(jnp.int32, shape, 0)


def _shift_down(x, s, fill):
    r = pltpu.roll(x, s, axis=0)
    return jnp.where(_rows(x.shape) >= s, r, fill)


def _shift_up(x, s, fill):
    t = x.shape[0]
    r = pltpu.roll(x, t - s, axis=0)
    return jnp.where(_rows(x.shape) < t - s, r, fill)


def _shift_down_prev(x, s, prev8):
    if s == 0:
        return x
    t, l = x.shape
    r = pltpu.roll(x, s, axis=0)
    pr = pltpu.roll(prev8, s, axis=0)
    pad = jnp.concatenate([pr, jnp.zeros((t - 8, l), x.dtype)], axis=0)
    return jnp.where(_rows(x.shape) < s, pad, r)


def _shift_up_next(x, s, next8):
    if s == 0:
        return x
    t, l = x.shape
    r = pltpu.roll(x, t - s, axis=0)
    nx = pltpu.roll(next8, 8 - s, axis=0)
    pad = jnp.concatenate([jnp.zeros((t - 8, l), x.dtype), nx], axis=0)
    return jnp.where(_rows(x.shape) >= t - s, pad, r)


def _scan_fwd(a, x):
    t = x.shape[0]
    s = 1
    while s < t:
        x = x + a * _shift_down(x, s, 0.0)
        a = a * _shift_down(a, s, 1.0)
        s *= 2
    return a, x


def _scan_rev(a, x):
    t = x.shape[0]
    s = 1
    while s < t:
        x = x + a * _shift_up(x, s, 0.0)
        a = a * _shift_up(a, s, 1.0)
        s *= 2
    return a, x


def _cpowers(lr, li, n):
    out = [(lr, li)]
    for _ in range(n - 1):
        lr, li = lr * lr - li * li, 2.0 * lr * li
        out.append((lr, li))
    return out


def _cscan(xr, xi, pows, reverse):
    shift = _shift_up if reverse else _shift_down
    s = 1
    for pr, pi in pows:
        sr = shift(xr, s, 0.0)
        si = shift(xi, s, 0.0)
        xr, xi = xr + pr * sr - pi * si, xi + pr * si + pi * sr
        s *= 2
    return xr, xi


def _dep_args(dep):
    return ([], []) if dep is None else ([pl.BlockSpec(memory_space=pl.ANY)], [dep])


def _mm_nn(a, b, tm, tn, name, out_dtype=F32, dep=None):
    m, k = a.shape
    n = b.shape[1]

    def body(a_ref, b_ref, *rest):
        o_ref = rest[-1]
        o_ref[...] = _dot(a_ref[...], b_ref[...]).astype(out_dtype)

    dep_specs, dep_ops = _dep_args(dep)
    return pl.pallas_call(
        body, out_shape=jax.ShapeDtypeStruct((m, n), out_dtype), grid=(n // tn, m // tm),
        in_specs=[pl.BlockSpec((tm, k), lambda j, i: (i, 0)),
                  pl.BlockSpec((k, tn), lambda j, i: (0, j))] + dep_specs,
        out_specs=pl.BlockSpec((tm, tn), lambda j, i: (i, j)), name=name,
        compiler_params=_cparams(("parallel", "parallel")))(a, b, *dep_ops)


def _mm_nt(a, w, tm, tn, name, add=None, add_scale=1.0, dep=None):
    m, k = a.shape
    n = w.shape[0]

    def body(a_ref, w_ref, *rest):
        o_ref = rest[-1]
        if add is None:
            o_ref[...] = _dot_nt(a_ref[...], w_ref[...])
        else:
            o_ref[...] = _dot_nt(a_ref[...], w_ref[...]) + add_scale * rest[0][...]

    in_specs = [pl.BlockSpec((tm, k), lambda j, i: (i, 0)), pl.BlockSpec((tn, k), lambda j, i: (j, 0))]
    args = [a, w]
    if add is not None:
        in_specs.append(pl.BlockSpec((tm, tn), lambda j, i: (i, j)))
        args.append(add)
    dep_specs, dep_ops = _dep_args(dep)
    return pl.pallas_call(
        body, out_shape=jax.ShapeDtypeStruct((m, n), F32), grid=(n // tn, m // tm),
        in_specs=in_specs + dep_specs, out_specs=pl.BlockSpec((tm, tn), lambda j, i: (i, j)), name=name,
        compiler_params=_cparams(("parallel", "parallel")))(*args, *dep_ops)


def _mm_tn(a, b, tm, tn, ts, name):
    s, m = a.shape
    n = b.shape[1]

    def body(a_ref, b_ref, o_ref):
        @pl.when(pl.program_id(2) == 0)
        def _():
            o_ref[...] = jnp.zeros_like(o_ref)
        o_ref[...] += _dot_tn(a_ref[...], b_ref[...])

    return pl.pallas_call(
        body, out_shape=jax.ShapeDtypeStruct((m, n), F32), grid=(m // tm, n // tn, s // ts),
        in_specs=[pl.BlockSpec((ts, tm), lambda i, j, k: (k, i)), pl.BlockSpec((ts, tn), lambda i, j, k: (k, j))],
        out_specs=pl.BlockSpec((tm, tn), lambda i, j, k: (i, j)), name=name,
        compiler_params=_cparams(("parallel", "parallel", "arbitrary")))(a, b)


def _mm_up(h, wg, tm, name, dep=None):
    s, d = h.shape

    def body(a_ref, w_ref, *rest):
        rest[-1][...] = _dot(a_ref[...], w_ref[...])

    dep_specs, dep_ops = _dep_args(dep)
    return pl.pallas_call(
        body, out_shape=jax.ShapeDtypeStruct((s, 2 * D_FF), F32), grid=(s // tm, N_DEV),
        in_specs=[pl.BlockSpec((tm, d), lambda i, j: (i, 0)),
                  pl.BlockSpec((None, d, FFN_CB), lambda i, j: (_ffn_dev(j), 0, 0))] + dep_specs,
        out_specs=pl.BlockSpec((tm, FFN_CB), lambda i, j: (i, j)), name=name,
        compiler_params=_cparams(("parallel", "parallel")))(h, wg, *dep_ops)


def _mm_up_dx(dup, wg, add, add_scale, tm, name):
    s = dup.shape[0]
    d = wg.shape[1]

    def body(a_ref, w_ref, c_ref, o_ref):
        @pl.when(pl.program_id(1) == 0)
        def _():
            o_ref[...] = add_scale * c_ref[...]
        o_ref[...] += _dot_nt(a_ref[...], w_ref[...])

    return pl.pallas_call(
        body, out_shape=jax.ShapeDtypeStruct((s, d), F32), grid=(s // tm, N_DEV),
        in_specs=[pl.BlockSpec((tm, FFN_CB), lambda i, j: (i, j)),
                  pl.BlockSpec((None, d, FFN_CB), lambda i, j: (_ffn_dev(j), 0, 0)),
                  pl.BlockSpec((tm, d), lambda i, j: (i, 0))],
        out_specs=pl.BlockSpec((tm, d), lambda i, j: (i, 0)), name=name,
        compiler_params=_cparams(("parallel", "arbitrary")))(dup, wg, add)


def _mm_up_dw(h, dup, ts, name):
    s, d = h.shape

    def body(a_ref, b_ref, o_ref):
        @pl.when(pl.program_id(1) == 0)
        def _():
            o_ref[...] = jnp.zeros_like(o_ref)
        o_ref[...] += _dot_tn(a_ref[...], b_ref[...])

    return pl.pallas_call(
        body, out_shape=jax.ShapeDtypeStruct((N_DEV, d, FFN_CB), F32), grid=(N_DEV, s // ts),
        in_specs=[pl.BlockSpec((ts, d), lambda j, k: (k, 0)), pl.BlockSpec((ts, FFN_CB), lambda j, k: (k, j))],
        out_specs=pl.BlockSpec((None, d, FFN_CB), lambda j, k: (_ffn_dev(j), 0, 0)), name=name,
        compiler_params=_cparams(("parallel", "arbitrary")))(h, dup)


def _ln_fwd(a, b, g, bias, name):
    s, d = a.shape
    tm = 512

    def body(a_ref, b_ref, g_ref, bias_ref, r_ref, h_ref):
        r = ALPHA * a_ref[...] + b_ref[...]
        mu = jnp.mean(r, axis=-1, keepdims=True)
        xc = r - mu
        var = jnp.mean(xc * xc, axis=-1, keepdims=True)
        r_ref[...] = r
        h_ref[...] = xc * lax.rsqrt(var + LN_EPS) * g_ref[...] + bias_ref[...]

    row = pl.BlockSpec((tm, d), lambda i: (i, 0))
    vec = pl.BlockSpec((1, d), lambda i: (0, 0))
    return pl.pallas_call(
        body, out_shape=(jax.ShapeDtypeStruct((s, d), F32), jax.ShapeDtypeStruct((s, d), F32)),
        grid=(s // tm,), in_specs=[row, row, vec, vec], out_specs=(row, row), name=name,
        compiler_params=_cparams(("parallel",)))(a, b, g, bias)


def _ln_bwd(r, dh, g, name, dep=None):
    s, d = r.shape
    tm = 512

    def body(r_ref, dh_ref, g_ref, *rest):
        dr_ref, dg_ref, db_ref = rest[-3:]

        @pl.when(pl.program_id(0) == 0)
        def _():
            dg_ref[...] = jnp.zeros_like(dg_ref)
            db_ref[...] = jnp.zeros_like(db_ref)
        rr = r_ref[...]
        dh_ = dh_ref[...]
        mu = jnp.mean(rr, axis=-1, keepdims=True)
        xc = rr - mu
        var = jnp.mean(xc * xc, axis=-1, keepdims=True)
        rstd = lax.rsqrt(var + LN_EPS)
        xh = xc * rstd
        dxh = dh_ * g_ref[...]
        m1 = jnp.mean(dxh, axis=-1, keepdims=True)
        m2 = jnp.mean(dxh * xh, axis=-1, keepdims=True)
        dr_ref[...] = rstd * (dxh - m1 - xh * m2)
        dg_ref[...] += jnp.sum(dh_ * xh, axis=0, keepdims=True)
        db_ref[...] += jnp.sum(dh_, axis=0, keepdims=True)

    row = pl.BlockSpec((tm, d), lambda i: (i, 0))
    vec = pl.BlockSpec((1, d), lambda i: (0, 0))
    dep_specs, dep_ops = _dep_args(dep)
    return pl.pallas_call(
        body, out_shape=(jax.ShapeDtypeStruct((s, d), F32), jax.ShapeDtypeStruct((1, d), F32),
                         jax.ShapeDtypeStruct((1, d), F32)),
        grid=(s // tm,), in_specs=[row, row, vec] + dep_specs, out_specs=(row, vec, vec), name=name,
        compiler_params=_cparams(("arbitrary",)))(r, dh, g, *dep_ops)


def _loss_head(y, target):
    s, d = y.shape
    tm = 512

    def body(y_ref, t_ref, dy_ref, l_ref):
        @pl.when(pl.program_id(0) == 0)
        def _():
            l_ref[...] = jnp.zeros_like(l_ref)
        e = y_ref[...] - t_ref[...]
        dy_ref[...] = e * (1.0 / d)
        part = 0.5 * jnp.sum(jnp.mean(e * e, axis=-1, keepdims=True), axis=0, keepdims=True)
        l_ref[...] += jnp.broadcast_to(part, l_ref.shape)

    row = pl.BlockSpec((tm, d), lambda i: (i, 0))
    return pl.pallas_call(
        body, out_shape=(jax.ShapeDtypeStruct((s, d), F32), jax.ShapeDtypeStruct((1, LANE), F32)),
        grid=(s // tm,), in_specs=[row, row], out_specs=(row, pl.BlockSpec((1, LANE), lambda i: (0, 0))),
        name="loss_head", compiler_params=_cparams(("arbitrary",)))(y, target)


def _rope_tables(s):
    half = HEAD // 2
    pos = jnp.arange(s, dtype=F32)
    inv = ROPE_THETA ** (-jnp.arange(half, dtype=F32) * 2.0 / HEAD)
    ang = pos[:, None] * inv[None, :]
    cos, sin = jnp.cos(ang), jnp.sin(ang)
    cos = jnp.concatenate([cos, cos, cos, cos], axis=1)
    sin = jnp.concatenate([-sin, sin, -sin, sin], axis=1)
    return cos, sin


def _rotate(x, cos, sin):
    lane = lax.broadcasted_iota(jnp.int32, x.shape, 1)
    partner = jnp.where((lane % HEAD) < HEAD // 2, pltpu.roll(x, LANE - HEAD // 2, axis=1),
                        pltpu.roll(x, HEAD // 2, axis=1))
    return x * cos + partner * sin


def _class_rows(c, d, tm):
    return pl.ds(c, tm // d, stride=d) if d > 1 else pl.ds(0, tm)


def _dilated_spec(tm, d, w):
    return pl.BlockSpec((tm // d, d * w), lambda i: (i, 0))


def _token_scratch(tm, w):
    return pltpu.VMEM((w // LANE, tm, LANE), F32)


def _to_tokens(src_ref, dst3, d, tm):
    nj = dst3.shape[0]
    for cls in range(d):
        for j in range(nj):
            col = (cls * nj + j) * LANE
            dst3.at[j][_class_rows(cls, d, tm), :] = src_ref[:, col:col + LANE]


def _to_dilated(src3, dst_ref, d, tm):
    nj = src3.shape[0]
    for cls in range(d):
        for j in range(nj):
            col = (cls * nj + j) * LANE
            dst_ref[:, col:col + LANE] = src3.at[j][_class_rows(cls, d, tm), :].astype(dst_ref.dtype)


def _token_value(src3):
    return jnp.concatenate([src3[j] for j in range(src3.shape[0])], axis=1)


def _rope_fwd(proj, cos, sin, name):
    s = proj.shape[0]
    tm = 512
    w = 3 * ATTN_W
    nj = w // LANE

    def body(*refs):
        p_refs, (c_ref, s_ref), o_refs, rot = refs[:nj], refs[nj:nj + 2], refs[nj + 2:nj + 5], refs[nj + 5]
        c, sn = c_ref[...], s_ref[...]
        for j in range(nj):
            x = p_refs[j][...]
            rot[j] = _rotate(x, c, sn) if j < 2 * ATTN_W // LANE else x
        for d, o_ref in zip(DILATIONS, o_refs):
            _to_dilated(rot, o_ref, d, tm)

    tab = pl.BlockSpec((tm, LANE), lambda i: (i, 0))
    cols = [pl.BlockSpec((tm, LANE), functools.partial(lambda i, j: (i, j), j=j)) for j in range(nj)]
    return pl.pallas_call(
        body, out_shape=tuple(jax.ShapeDtypeStruct((s // d, d * w), BF16) for d in DILATIONS),
        grid=(s // tm,), in_specs=cols + [tab, tab],
        out_specs=tuple(_dilated_spec(tm, d, w) for d in DILATIONS),
        scratch_shapes=[_token_scratch(tm, w)], name=name,
        compiler_params=_cparams(("parallel",)))(*[proj] * nj, cos, sin)


def _dproj_assemble(dqkv_list, dxr, dgate, du, cos, sin, name):
    s = dxr.shape[0]
    tm = 512
    nq = 3 * ATTN_W // LANE

    def body(*refs):
        br = refs[:9]
        dxr_ref, dg_ref, du_ref, c_ref, s_ref, o_ref = refs[9:15]
        tok = refs[15:]
        c, sn = c_ref[...], -s_ref[...]
        for part in range(3):
            for b, d in enumerate(DILATIONS[1:], start=1):
                _to_tokens(br[3 * b + part], tok[2 * part + b - 1], d, tm)
        for j in range(nq):
            part, jj = divmod(j, ATTN_W // LANE)
            x = br[part][:, jj * LANE:(jj + 1) * LANE] + tok[2 * part][jj] + tok[2 * part + 1][jj]
            if part < 2:
                x = _rotate(x, c, sn)
            o_ref[:, j * LANE:(j + 1) * LANE] = x.astype(BF16)
        o_ref[:, 3 * ATTN_W:3 * ATTN_W + LRU_W] = dxr_ref[...].astype(BF16)
        o_ref[:, 3 * ATTN_W + LRU_W:3 * ATTN_W + 2 * LRU_W] = dg_ref[...].astype(BF16)
        o_ref[:, 3 * ATTN_W + 2 * LRU_W:] = du_ref[...].astype(BF16)

    a_spec = pl.BlockSpec((tm, ATTN_W), lambda i: (i, 0))
    tab = pl.BlockSpec((tm, LANE), lambda i: (i, 0))
    ordered = [dqkv_list[b][p] for b in range(3) for p in range(3)]
    d_specs = [_dilated_spec(tm, d, ATTN_W) for d in DILATIONS for _ in range(3)]
    return pl.pallas_call(
        body, out_shape=jax.ShapeDtypeStruct((s, D_IN), BF16), grid=(s // tm,),
        in_specs=d_specs + [a_spec, a_spec, pl.BlockSpec((tm, S5_W), lambda i: (i, 0)), tab, tab],
        out_specs=pl.BlockSpec((tm, D_IN), lambda i: (i, 0)),
        scratch_shapes=[_token_scratch(tm, ATTN_W)] * 6, name=name,
        compiler_params=_cparams(("parallel",)))(*ordered, dxr, dgate, du, cos, sin)


def _attn_tiles(s, d):
    m = s // d
    tq = min(m, ATTN_TILE)
    return m, tq, tq // ATTN_BLK


def _band_mask(qb):
    qi = lax.broadcasted_iota(jnp.int32, (ATTN_BLK, 2 * ATTN_BLK), 0)
    ki = lax.broadcasted_iota(jnp.int32, (ATTN_BLK, 2 * ATTN_BLK), 1)
    dist = qi + ATTN_BLK - ki
    return (dist >= 0) & (dist <= ATTN_BLK) & ((ki >= ATTN_BLK) | (qb > 0))


def _head_cols(h):
    return (slice(h * HEAD, (h + 1) * HEAD), slice(ATTN_W + h * HEAD, ATTN_W + (h + 1) * HEAD),
            slice(2 * ATTN_W + h * HEAD, 2 * ATTN_W + (h + 1) * HEAD))


def _attn_fwd(qv, d, name):
    m = qv.shape[0]
    w3 = 3 * ATTN_W
    _, tq, n = _attn_tiles(m * d, d)
    scale = HEAD ** -0.5

    def body(x_ref, p_ref, o_ref, l_ref):
        b = pl.program_id(1)

        def block(i, first):
            r0 = 0 if first else pl.multiple_of(i * ATTN_BLK, ATTN_BLK)
            rows = pl.ds(r0, ATTN_BLK)
            valid = _band_mask(b * n + i)
            if not first:
                krows = pl.ds(pl.multiple_of(i * ATTN_BLK - ATTN_BLK, ATTN_BLK), 2 * ATTN_BLK)
            for h in range(ATTN_W // HEAD):
                qs, ks, vs = _head_cols(h)
                q = x_ref[rows, qs]
                if first:
                    k = jnp.concatenate([p_ref[:, ks], x_ref[0:ATTN_BLK, ks]], axis=0)
                    v = jnp.concatenate([p_ref[:, vs], x_ref[0:ATTN_BLK, vs]], axis=0)
                else:
                    k = x_ref[krows, ks]
                    v = x_ref[krows, vs]
                sc = jnp.where(valid, _dot_nt(q, k) * scale, -1e30)
                mx = jnp.max(sc, axis=-1, keepdims=True)
                p = jnp.exp(sc - mx)
                l = jnp.sum(p, axis=-1, keepdims=True)
                o_ref[rows, qs] = _dot(p, v) / l
                l_ref[rows, qs] = jnp.broadcast_to(mx + jnp.log(l), (ATTN_BLK, HEAD))

        block(0, True)
        if n > 1:
            def loop(i, carry):
                block(i, False)
                return carry
            lax.fori_loop(1, n, loop, 0)

    shp = jax.ShapeDtypeStruct((m, d * ATTN_W), F32)
    ospec = pl.BlockSpec((tq, ATTN_W), lambda c, b: (b, c))
    out, lse = pl.pallas_call(
        body, out_shape=(shp, shp), grid=(d, m // tq),
        in_specs=[pl.BlockSpec((tq, w3), lambda c, b: (b, c)),
                  pl.BlockSpec((ATTN_BLK, w3), lambda c, b: (jnp.maximum(b * n - 1, 0), c))],
        out_specs=(ospec, ospec), name=name,
        compiler_params=_cparams(("parallel", "parallel")))(qv, qv)
    return out, lse


def _attn_bwd(qv, ov, dov, lv, d, name):
    m = qv.shape[0]
    w3 = 3 * ATTN_W
    _, tq, n = _attn_tiles(m * d, d)
    nb = m // ATTN_BLK
    scale = HEAD ** -0.5

    def body(x_ref, p_ref, nx_ref, o_ref, do_ref, l_ref, on_ref, don_ref, ln_ref, dq_ref, dk_ref, dv_ref):
        b = pl.program_id(1)
        dk_ref[...] = jnp.zeros_like(dk_ref)
        dv_ref[...] = jnp.zeros_like(dv_ref)

        def grads(q, k, v, o, do, lse, valid):
            sc = jnp.where(valid, _dot_nt(q, k) * scale, -1e30)
            p = jnp.exp(sc - lse)
            delta = jnp.sum(do * o, axis=-1, keepdims=True)
            return p, p * (_dot_nt(do, v) - delta) * scale

        def block(i, first):
            r0 = 0 if first else pl.multiple_of(i * ATTN_BLK, ATTN_BLK)
            rows = pl.ds(r0, ATTN_BLK)
            valid = _band_mask(b * n + i)
            if not first:
                krows = pl.ds(pl.multiple_of(i * ATTN_BLK - ATTN_BLK, ATTN_BLK), 2 * ATTN_BLK)
            for h in range(ATTN_W // HEAD):
                qs, ks, vs = _head_cols(h)
                q = x_ref[rows, qs]
                do = do_ref[rows, qs]
                if first:
                    k = jnp.concatenate([p_ref[:, ks], x_ref[0:ATTN_BLK, ks]], axis=0)
                    v = jnp.concatenate([p_ref[:, vs], x_ref[0:ATTN_BLK, vs]], axis=0)
                else:
                    k = x_ref[krows, ks]
                    v = x_ref[krows, vs]
                p, ds = grads(q, k, v, o_ref[rows, qs], do, l_ref[rows, qs][:, 0:1], valid)
                dq_ref[rows, qs] = _dot(ds, k)
                if first:
                    dk_ref[0:ATTN_BLK, qs] += _dot_tn(ds[:, ATTN_BLK:], q)
                    dv_ref[0:ATTN_BLK, qs] += _dot_tn(p[:, ATTN_BLK:], do)
                else:
                    dk_ref[krows, qs] += _dot_tn(ds, q)
                    dv_ref[krows, qs] += _dot_tn(p, do)

        block(0, True)
        if n > 1:
            def loop(i, carry):
                block(i, False)
                return carry
            lax.fori_loop(1, n, loop, 0)

        last = slice((n - 1) * ATTN_BLK, n * ATTN_BLK)
        qi = lax.broadcasted_iota(jnp.int32, (ATTN_BLK, ATTN_BLK), 0)
        ki = lax.broadcasted_iota(jnp.int32, (ATTN_BLK, ATTN_BLK), 1)
        valid_next = (qi <= ki) & ((b + 1) * n < nb)
        for h in range(ATTN_W // HEAD):
            qs, ks, vs = _head_cols(h)
            q = nx_ref[:, qs]
            do = don_ref[:, qs]
            p, ds = grads(q, x_ref[last, ks], x_ref[last, vs], on_ref[:, qs], do, ln_ref[:, qs][:, 0:1],
                          valid_next)
            dk_ref[last, qs] += _dot_tn(ds, q)
            dv_ref[last, qs] += _dot_tn(p, do)

    nxt = lambda b: jnp.minimum((b + 1) * n, nb - 1)
    xs = pl.BlockSpec((tq, w3), lambda c, b: (b, c))
    xp = pl.BlockSpec((ATTN_BLK, w3), lambda c, b: (jnp.maximum(b * n - 1, 0), c))
    xn = pl.BlockSpec((ATTN_BLK, w3), lambda c, b: (nxt(b), c))
    a = pl.BlockSpec((tq, ATTN_W), lambda c, b: (b, c))
    an = pl.BlockSpec((ATTN_BLK, ATTN_W), lambda c, b: (nxt(b), c))
    shp = jax.ShapeDtypeStruct((m, d * ATTN_W), F32)
    return pl.pallas_call(
        body, out_shape=(shp, shp, shp), grid=(d, m // tq),
        in_specs=[xs, xp, xn, a, a, a, an, an, an], out_specs=(a, a, a), name=name,
        compiler_params=_cparams(("parallel", "parallel")))(qv, qv, qv, ov, dov, lv, ov, dov, lv)


def _rms(x, g):
    ms = jnp.mean(x * x, axis=-1, keepdims=True)
    return x * lax.rsqrt(ms + RMS_EPS) * g


def _rms_bwd(x, g, dy):
    ms = jnp.mean(x * x, axis=-1, keepdims=True)
    r = lax.rsqrt(ms + RMS_EPS)
    dyg = dy * g
    dx = r * dyg - x * (r * r * r) * jnp.mean(x * dyg, axis=-1, keepdims=True)
    return dx, dy * x * r


def _mix_fwd(outs, lses, lru, s5, g, name):
    s = lru.shape[0]
    tm = 256

    def body(o1, o2, o3, l1, l2, l3, lru_ref, s5_ref, g_ref, mixed_ref, ov1, ov2, ov3, lv1, lv2, lv3,
             so2, so3, sl2, sl3):
        for d, src, dst in ((DILATIONS[1], o2, so2), (DILATIONS[2], o3, so3),
                            (DILATIONS[1], l2, sl2), (DILATIONS[2], l3, sl3)):
            _to_tokens(src, dst, d, tm)
        a1, a2, a3 = l1[...], _token_value(sl2), _token_value(sl3)
        mx = jnp.maximum(jnp.maximum(a1, a2), a3)
        e1, e2, e3 = jnp.exp(a1 - mx), jnp.exp(a2 - mx), jnp.exp(a3 - mx)
        den = e1 + e2 + e3
        o = (e1 * o1[...] + e2 * _token_value(so2) + e3 * _token_value(so3)) / den
        lse = mx + jnp.log(den)
        ov1[...] = o
        lv1[...] = lse
        for j in range(ATTN_W // LANE):
            so2[j] = o[:, j * LANE:(j + 1) * LANE]
            sl2[j] = lse[:, j * LANE:(j + 1) * LANE]
        for d, o_dst, l_dst in ((DILATIONS[1], ov2, lv2), (DILATIONS[2], ov3, lv3)):
            _to_dilated(so2, o_dst, d, tm)
            _to_dilated(sl2, l_dst, d, tm)
        gg = g_ref[...]
        mixed_ref[:, :ATTN_W] = _rms(o, gg[:, :ATTN_W]).astype(BF16)
        mixed_ref[:, ATTN_W:ATTN_W + LRU_W] = _rms(lru_ref[...], gg[:, ATTN_W:ATTN_W + LRU_W]).astype(BF16)
        mixed_ref[:, ATTN_W + LRU_W:] = _rms(s5_ref[...], gg[:, ATTN_W + LRU_W:]).astype(BF16)

    a = pl.BlockSpec((tm, ATTN_W), lambda i: (i, 0))
    s5s = pl.BlockSpec((tm, S5_W), lambda i: (i, 0))
    full = pl.BlockSpec((tm, D_MODEL), lambda i: (i, 0))
    vec = pl.BlockSpec((1, D_MODEL), lambda i: (0, 0))
    dil = [_dilated_spec(tm, d, ATTN_W) for d in DILATIONS]
    dshape = [jax.ShapeDtypeStruct((s // d, d * ATTN_W), F32) for d in DILATIONS]
    res = pl.pallas_call(
        body, out_shape=(jax.ShapeDtypeStruct((s, D_MODEL), BF16), *dshape, *dshape),
        grid=(s // tm,), in_specs=dil + dil + [a, s5s, vec], out_specs=(full, *dil, *dil),
        scratch_shapes=[_token_scratch(tm, ATTN_W)] * 4, name=name,
        compiler_params=_cparams(("parallel",)))(*outs, *lses, lru, s5, g)
    return res[0], res[1:4], res[4:7]


def _mix_bwd(dmixed, o, lru, s5, g, name):
    s = lru.shape[0]
    tm = 256

    def body(dm_ref, o_ref, lru_ref, s5_ref, g_ref, do_ref, do2_ref, do3_ref, dlru_ref, ds5_ref, dg_ref, stage):
        @pl.when(pl.program_id(0) == 0)
        def _():
            dg_ref[...] = jnp.zeros_like(dg_ref)
        gg = g_ref[...]
        dm = dm_ref[...]
        dx, dgr = _rms_bwd(o_ref[...], gg[:, :ATTN_W], dm[:, :ATTN_W])
        do_ref[...] = dx
        for j in range(ATTN_W // LANE):
            stage[j] = dx[:, j * LANE:(j + 1) * LANE]
        _to_dilated(stage, do2_ref, DILATIONS[1], tm)
        _to_dilated(stage, do3_ref, DILATIONS[2], tm)
        dg_ref[:, :ATTN_W] += jnp.sum(dgr, axis=0, keepdims=True)
        dx, dgr = _rms_bwd(lru_ref[...], gg[:, ATTN_W:ATTN_W + LRU_W], dm[:, ATTN_W:ATTN_W + LRU_W])
        dlru_ref[...] = dx
        dg_ref[:, ATTN_W:ATTN_W + LRU_W] += jnp.sum(dgr, axis=0, keepdims=True)
        dx, dgr = _rms_bwd(s5_ref[...], gg[:, ATTN_W + LRU_W:], dm[:, ATTN_W + LRU_W:])
        ds5_ref[...] = dx
        dg_ref[:, ATTN_W + LRU_W:] += jnp.sum(dgr, axis=0, keepdims=True)

    a = pl.BlockSpec((tm, ATTN_W), lambda i: (i, 0))
    s5s = pl.BlockSpec((tm, S5_W), lambda i: (i, 0))
    full = pl.BlockSpec((tm, D_MODEL), lambda i: (i, 0))
    vec = pl.BlockSpec((1, D_MODEL), lambda i: (0, 0))
    dil = [_dilated_spec(tm, d, ATTN_W) for d in DILATIONS]
    dshape = [jax.ShapeDtypeStruct((s // d, d * ATTN_W), F32) for d in DILATIONS]
    res = pl.pallas_call(
        body, out_shape=(*dshape, jax.ShapeDtypeStruct((s, LRU_W), F32),
                         jax.ShapeDtypeStruct((s, S5_W), F32), jax.ShapeDtypeStruct((1, D_MODEL), F32)),
        grid=(s // tm,), in_specs=[full, a, a, s5s, vec], out_specs=(*dil, a, s5s, vec),
        scratch_shapes=[_token_scratch(tm, ATTN_W)], name=name,
        compiler_params=_cparams(("arbitrary",)))(dmixed, o, lru, s5, g)
    return res[0:3], res[3], res[4], res[5]


def _lru_gate_math(xc, pre_r, pre_i, lam):
    r = _sigmoid(pre_r)
    i = _sigmoid(pre_i)
    log_a = -LRU_C * r * _softplus(-lam)
    a = jnp.exp(log_a)
    u = jnp.sqrt(-_expm1(2.0 * log_a)) * (i * xc)
    return a, u


def _lru_conv(x, prev8, cw, cb):
    y = cb + cw[LRU_CONV - 1:LRU_CONV, :] * x
    for k in range(LRU_CONV - 1):
        y = y + cw[k:k + 1, :] * _shift_down_prev(x, LRU_CONV - 1 - k, prev8)
    return y


def _lru_specs(s):
    xo = 3 * ATTN_W // LANE
    go = xo + LRU_W // LANE
    xr = pl.BlockSpec((s, LANE), lambda j: (0, xo + j))
    gt = pl.BlockSpec((s, LANE), lambda j: (0, go + j))
    cw = pl.BlockSpec((LRU_CONV, LANE), lambda j: (0, j))
    vec = pl.BlockSpec((1, LANE), lambda j: (0, j))
    wbd = pl.BlockSpec((LANE, LANE), lambda j: (j, j))
    col = pl.BlockSpec((s, LANE), lambda j: (0, j))
    return xr, gt, cw, vec, wbd, col


def _lru_fwd(proj, cw, cb, wr, br, wi, bi, lam, name):
    s = proj.shape[0]
    t = SCAN_T

    def body(xr_ref, gt_ref, cw_ref, cb_ref, wr_ref, br_ref, wi_ref, bi_ref, lam_ref, o_ref):
        cwv, cbv, lamv = cw_ref[...], cb_ref[...], lam_ref[...]
        wrv, wiv, brv, biv = wr_ref[...], wi_ref[...], br_ref[...], bi_ref[...]

        def chunk(c, carry):
            h_c, prev8 = carry
            rows = pl.ds(pl.multiple_of(c * t, t), t)
            x = xr_ref[rows, :]
            xc = _lru_conv(x, prev8, cwv, cbv)
            a, u = _lru_gate_math(xc, _dot(xc, wrv) + brv, _dot(xc, wiv) + biv, lamv)
            acum, hloc = _scan_fwd(a, u)
            h = hloc + acum * h_c
            o_ref[rows, :] = h * _gelu(gt_ref[rows, :])
            return h[t - 1:t, :], x[t - 8:t, :]

        lax.fori_loop(0, s // t, chunk, (jnp.zeros((1, LANE), F32), jnp.zeros((8, LANE), F32)))

    xr, gt, cws, vec, wbd, col = _lru_specs(s)
    return pl.pallas_call(
        body, out_shape=jax.ShapeDtypeStruct((s, LRU_W), F32), grid=(LRU_W // LANE,),
        in_specs=[xr, gt, cws, vec, wbd, vec, wbd, vec, vec], out_specs=col, name=name,
        compiler_params=_cparams(("parallel",)))(proj, proj, cw, cb, wr, br, wi, bi, lam)


def _lru_bwd(proj, dout, cw, cb, wr, br, wi, bi, lam, name):
    s = proj.shape[0]
    t = SCAN_T
    nc = s // t

    def body(xr_ref, gt_ref, do_ref, cw_ref, cb_ref, wr_ref, br_ref, wi_ref, bi_ref, lam_ref,
             dxr_ref, dgt_ref, dcw_ref, dcb_ref, dwr_ref, dbr_ref, dwi_ref, dbi_ref, dlam_ref,
             xc_s, a_s, h_s):
        cwv, cbv, lamv = cw_ref[...], cb_ref[...], lam_ref[...]
        wrv, wiv, brv, biv = wr_ref[...], wi_ref[...], br_ref[...], bi_ref[...]

        def fchunk(c, carry):
            h_c, prev8 = carry
            rows = pl.ds(pl.multiple_of(c * t, t), t)
            x = xr_ref[rows, :]
            xc = _lru_conv(x, prev8, cwv, cbv)
            a, u = _lru_gate_math(xc, _dot(xc, wrv) + brv, _dot(xc, wiv) + biv, lamv)
            acum, hloc = _scan_fwd(a, u)
            h = hloc + acum * h_c
            xc_s[rows, :] = xc
            a_s[rows, :] = a
            h_s[rows, :] = h
            return h[t - 1:t, :], x[t - 8:t, :]

        lax.fori_loop(0, nc, fchunk, (jnp.zeros((1, LANE), F32), jnp.zeros((8, LANE), F32)))

        z1 = jnp.zeros((1, LANE), F32)
        zw = jnp.zeros((LANE, LANE), F32)

        def bchunk(ci, carry):
            g_next, a_next, dxc_next8, dcw, dcb, dwr, dbr, dwi, dbi, dlam = carry
            c = nc - 1 - ci
            t0 = pl.multiple_of(c * t, t)
            rows = pl.ds(t0, t)
            before = pl.ds(pl.multiple_of(jnp.maximum(t0 - 8, 0), 8), 8)
            has_prev = (c > 0).astype(F32)
            x, gt, do = xr_ref[rows, :], gt_ref[rows, :], do_ref[rows, :]
            xc, a, h = xc_s[rows, :], a_s[rows, :], h_s[rows, :]
            prev8_x = xr_ref[before, :] * has_prev
            prev8_h = h_s[before, :] * has_prev
            dgt_ref[rows, :] = do * h * _gelu_grad(gt)
            dh = do * _gelu(gt)
            a_plus = _shift_up_next(a, 1, jnp.broadcast_to(a_next, (8, LANE)))
            acum, gloc = _scan_rev(a_plus, dh)
            g = gloc + acum * g_next
            da = g * _shift_down_prev(h, 1, prev8_h)
            pre_r = _dot(xc, wrv) + brv
            pre_i = _dot(xc, wiv) + biv
            _, vjp = jax.vjp(_lru_gate_math, xc, pre_r, pre_i, lamv)
            dxc, dpre_r, dpre_i, dlam_c = vjp((da, g))
            dxc = dxc + _dot_nt(dpre_r, wrv) + _dot_nt(dpre_i, wiv)
            dx = cwv[LRU_CONV - 1:LRU_CONV, :] * dxc
            dcw_rows = [None] * LRU_CONV
            dcw_rows[LRU_CONV - 1] = jnp.sum(dxc * x, axis=0, keepdims=True)
            for k in range(LRU_CONV - 1):
                sh = LRU_CONV - 1 - k
                dx = dx + cwv[k:k + 1, :] * _shift_up_next(dxc, sh, dxc_next8)
                dcw_rows[k] = jnp.sum(dxc * _shift_down_prev(x, sh, prev8_x), axis=0, keepdims=True)
            dxr_ref[rows, :] = dx
            return (g[0:1, :], a[0:1, :], dxc[0:8, :],
                    dcw + jnp.concatenate(dcw_rows, axis=0),
                    dcb + jnp.sum(dxc, axis=0, keepdims=True),
                    dwr + _dot_tn(xc, dpre_r), dbr + jnp.sum(dpre_r, axis=0, keepdims=True),
                    dwi + _dot_tn(xc, dpre_i), dbi + jnp.sum(dpre_i, axis=0, keepdims=True),
                    dlam + dlam_c)

        init = (z1, z1, jnp.zeros((8, LANE), F32), jnp.zeros((LRU_CONV, LANE), F32), z1, zw, z1, zw, z1, z1)
        res = lax.fori_loop(0, nc, bchunk, init)
        dcw_ref[...] = res[3]
        dcb_ref[...] = res[4]
        dwr_ref[...] = res[5]
        dbr_ref[...] = res[6]
        dwi_ref[...] = res[7]
        dbi_ref[...] = res[8]
        dlam_ref[...] = res[9]

    xr, gt, cws, vec, wbd, col = _lru_specs(s)
    vshape = jax.ShapeDtypeStruct((1, LRU_W), F32)
    wshape = jax.ShapeDtypeStruct((LRU_W, LRU_W), F32)
    return pl.pallas_call(
        body,
        out_shape=(jax.ShapeDtypeStruct((s, LRU_W), F32), jax.ShapeDtypeStruct((s, LRU_W), F32),
                   jax.ShapeDtypeStruct((LRU_CONV, LRU_W), F32), vshape, wshape, vshape, wshape, vshape, vshape),
        grid=(LRU_W // LANE,),
        in_specs=[xr, gt, col, cws, vec, wbd, vec, wbd, vec, vec],
        out_specs=(col, col, cws, vec, wbd, vec, wbd, vec, vec),
        scratch_shapes=[pltpu.VMEM((s, LANE), F32)] * 3, name=name,
        compiler_params=_cparams(("parallel",)))(proj, proj, dout, cw, cb, wr, br, wi, bi, lam)


def _s5_disc_math(a_re, a_im, log_step, bt_re, bt_im):
    step = jnp.exp(log_step)
    dt_re, dt_im = step * a_re, step * a_im
    mag = jnp.exp(dt_re)
    ab_re, ab_im = mag * jnp.cos(dt_im), mag * jnp.sin(dt_im)
    z_re, z_im = ab_re - 1.0, ab_im
    den = a_re * a_re + a_im * a_im
    f_re = (z_re * a_re + z_im * a_im) / den
    f_im = (z_im * a_re - z_re * a_im) / den
    bb_re = f_re * bt_re - f_im * bt_im
    bb_im = f_re * bt_im + f_im * bt_re
    return ab_re, ab_im, bb_re, bb_im


def _s5_disc_fwd(a_re, a_im, log_step, bt_re, bt_im, name):
    def body(ar, ai, ls, br, bi, o1, o2, o3, o4):
        r = _s5_disc_math(ar[...], ai[...], ls[...], br[...], bi[...])
        o1[...], o2[...], o3[...], o4[...] = r

    shp = jax.ShapeDtypeStruct(a_re.shape, F32)
    return pl.pallas_call(body, out_shape=(shp,) * 4, name=name)(a_re, a_im, log_step, bt_re, bt_im)


def _s5_disc_bwd(a_re, a_im, log_step, bt_re, bt_im, cts, name):
    def body(ar, ai, ls, br, bi, c1, c2, c3, c4, o1, o2, o3, o4, o5):
        _, vjp = jax.vjp(_s5_disc_math, ar[...], ai[...], ls[...], br[...], bi[...])
        r = vjp((c1[...], c2[...], c3[...], c4[...]))
        o1[...], o2[...], o3[...], o4[...], o5[...] = r

    shp = jax.ShapeDtypeStruct(a_re.shape, F32)
    return pl.pallas_call(body, out_shape=(shp,) * 5, name=name)(a_re, a_im, log_step, bt_re, bt_im, *cts)


def _s5_u_specs(s):
    uo = (3 * ATTN_W + 2 * LRU_W) // LANE
    return (pl.BlockSpec((s, LANE), lambda j: (0, uo)), pl.BlockSpec((s, LANE), lambda j: (0, uo + 1)))


def _s5_scan_fwd(proj, b_re, b_im, lam_re, lam_im, c_re, c_im, name):
    s = proj.shape[0]
    t = SCAN_T
    nlog = int(math.log2(t))

    def body(u0_ref, u1_ref, bre_ref, bim_ref, lre_ref, lim_ref, cre_ref, cim_ref, xre_ref, xim_ref, y_ref):
        @pl.when(pl.program_id(0) == 0)
        def _():
            y_ref[...] = jnp.zeros_like(y_ref)
        lr, li = lre_ref[...], lim_ref[...]
        pows = _cpowers(lr, li, nlog)
        first = _rows((t, LANE)) == 0
        tab_r, tab_i = _cscan(jnp.where(first, lr, 0.0), jnp.where(first, li, 0.0), pows, False)
        bre, bim, cre, cim = bre_ref[...], bim_ref[...], cre_ref[...], cim_ref[...]

        def chunk(c, carry):
            cr, ci = carry
            rows = pl.ds(pl.multiple_of(c * t, t), t)
            u = jnp.concatenate([u0_ref[rows, :], u1_ref[rows, :]], axis=1).astype(BF16)
            xr, xi = _cscan(_dot(u, bre), _dot(u, bim), pows, False)
            xr, xi = xr + tab_r * cr - tab_i * ci, xi + tab_r * ci + tab_i * cr
            xre_ref[rows, :] = xr
            xim_ref[rows, :] = xi
            y_ref[rows, :] += _dot(xr, cre) - _dot(xi, cim)
            return xr[t - 1:t, :], xi[t - 1:t, :]

        z = jnp.zeros((1, LANE), F32)
        lax.fori_loop(0, s // t, chunk, (z, z))

    u0, u1 = _s5_u_specs(s)
    bsp = pl.BlockSpec((S5_W, LANE), lambda j: (0, j))
    csp = pl.BlockSpec((LANE, S5_W), lambda j: (j, 0))
    vec = pl.BlockSpec((1, LANE), lambda j: (0, j))
    xsp = pl.BlockSpec((s, LANE), lambda j: (0, j))
    ysp = pl.BlockSpec((s, S5_W), lambda j: (0, 0))
    xshape = jax.ShapeDtypeStruct((s, S5_STATES), F32)
    return pl.pallas_call(
        body, out_shape=(xshape, xshape, jax.ShapeDtypeStruct((s, S5_W), F32)),
        grid=(S5_STATES // LANE,), in_specs=[u0, u1, bsp, bsp, vec, vec, csp, csp],
        out_specs=(xsp, xsp, ysp), name=name,
        compiler_params=_cparams(("arbitrary",)))(proj, proj, b_re, b_im, lam_re, lam_im, c_re, c_im)


def _s5_scan_bwd(proj, dy, du_init, x_re, x_im, b_re, b_im, lam_re, lam_im, c_re, c_im, name):
    s = proj.shape[0]
    t = SCAN_T
    nc = s // t
    nlog = int(math.log2(t))

    def body(u0_ref, u1_ref, dy_ref, dui_ref, xre_ref, xim_ref, bre_ref, bim_ref, lre_ref, lim_ref,
             cre_ref, cim_ref, du_ref, dlr_ref, dli_ref, dbr_ref, dbi_ref, dcr_ref, dci_ref):
        @pl.when(pl.program_id(0) == 0)
        def _():
            du_ref[...] = dui_ref[...]
        mr, mi = lre_ref[...], -lim_ref[...]
        pows = _cpowers(mr, mi, nlog)
        last = _rows((t, LANE)) == t - 1
        tab_r, tab_i = _cscan(jnp.where(last, mr, 0.0), jnp.where(last, mi, 0.0), pows, True)
        bre, bim, cre, cim = bre_ref[...], bim_ref[...], cre_ref[...], cim_ref[...]
        dbr_ref[...] = jnp.zeros_like(dbr_ref)
        dbi_ref[...] = jnp.zeros_like(dbi_ref)
        dcr_ref[...] = jnp.zeros_like(dcr_ref)
        dci_ref[...] = jnp.zeros_like(dci_ref)

        def chunk(ci_, carry):
            gnr, gni, dlr, dli = carry
            c = nc - 1 - ci_
            t0 = pl.multiple_of(c * t, t)
            rows = pl.ds(t0, t)
            before = pl.ds(pl.multiple_of(jnp.maximum(t0 - 8, 0), 8), 8)
            has_prev = (c > 0).astype(F32)
            dyc = dy_ref[rows, :].astype(BF16)
            u = jnp.concatenate([u0_ref[rows, :], u1_ref[rows, :]], axis=1).astype(BF16)
            gr, gi = _cscan(_dot_nt(dyc, cre), -_dot_nt(dyc, cim), pows, True)
            gr, gi = gr + tab_r * gnr - tab_i * gni, gi + tab_r * gni + tab_i * gnr
            xr, xi = xre_ref[rows, :], xim_ref[rows, :]
            xpr = _shift_down_prev(xr, 1, xre_ref[before, :] * has_prev)
            xpi = _shift_down_prev(xi, 1, xim_ref[before, :] * has_prev)
            dlr = dlr + jnp.sum(gr * xpr + gi * xpi, axis=0, keepdims=True)
            dli = dli + jnp.sum(gi * xpr - gr * xpi, axis=0, keepdims=True)
            du_ref[rows, :] += _dot_nt(gr, bre) + _dot_nt(gi, bim)
            dbr_ref[...] += _dot_tn(u, gr)
            dbi_ref[...] += _dot_tn(u, gi)
            dcr_ref[...] += _dot_tn(xr, dyc)
            dci_ref[...] -= _dot_tn(xi, dyc)
            return gr[0:1, :], gi[0:1, :], dlr, dli

        z = jnp.zeros((1, LANE), F32)
        res = lax.fori_loop(0, nc, chunk, (z, z, z, z))
        dlr_ref[...] = res[2]
        dli_ref[...] = res[3]

    u0, u1 = _s5_u_specs(s)
    bsp = pl.BlockSpec((S5_W, LANE), lambda j: (0, j))
    csp = pl.BlockSpec((LANE, S5_W), lambda j: (j, 0))
    vec = pl.BlockSpec((1, LANE), lambda j: (0, j))
    xsp = pl.BlockSpec((s, LANE), lambda j: (0, j))
    ysp = pl.BlockSpec((s, S5_W), lambda j: (0, 0))
    return pl.pallas_call(
        body,
        out_shape=(jax.ShapeDtypeStruct((s, S5_W), F32),
                   jax.ShapeDtypeStruct((1, S5_STATES), F32), jax.ShapeDtypeStruct((1, S5_STATES), F32),
                   jax.ShapeDtypeStruct((S5_W, S5_STATES), F32), jax.ShapeDtypeStruct((S5_W, S5_STATES), F32),
                   jax.ShapeDtypeStruct((S5_STATES, S5_W), F32), jax.ShapeDtypeStruct((S5_STATES, S5_W), F32)),
        grid=(S5_STATES // LANE,),
        in_specs=[u0, u1, ysp, ysp, xsp, xsp, bsp, bsp, vec, vec, csp, csp],
        out_specs=(ysp, vec, vec, bsp, bsp, csp, csp), name=name,
        compiler_params=_cparams(("arbitrary",)))(
            proj, proj, dy, du_init, x_re, x_im, b_re, b_im, lam_re, lam_im, c_re, c_im)


def _s5_out_fwd(proj, y_acc, dvec, w_glu, b_glu, name):
    s = proj.shape[0]
    tm = 512
    uo = (3 * ATTN_W + 2 * LRU_W) // LANE

    def body(u0_ref, u1_ref, y_ref, d_ref, w_ref, b_ref, o_ref, yp_ref):
        u = jnp.concatenate([u0_ref[...], u1_ref[...]], axis=1)
        y = y_ref[...] + d_ref[...] * u
        yp_ref[...] = y
        yg = _gelu(y)
        o_ref[...] = yg * _sigmoid(_dot(yg, w_ref[...]) + b_ref[...])

    u0 = pl.BlockSpec((tm, LANE), lambda i: (i, uo))
    u1 = pl.BlockSpec((tm, LANE), lambda i: (i, uo + 1))
    row = pl.BlockSpec((tm, S5_W), lambda i: (i, 0))
    vec = pl.BlockSpec((1, S5_W), lambda i: (0, 0))
    wsp = pl.BlockSpec((S5_W, S5_W), lambda i: (0, 0))
    shp = jax.ShapeDtypeStruct((s, S5_W), F32)
    return pl.pallas_call(
        body, out_shape=(shp, shp), grid=(s // tm,), in_specs=[u0, u1, row, vec, wsp, vec],
        out_specs=(row, row), name=name,
        compiler_params=_cparams(("parallel",)))(proj, proj, y_acc, dvec, w_glu, b_glu)


def _s5_out_bwd(proj, y_pre, dout, dvec, w_glu, b_glu, name):
    s = proj.shape[0]
    tm = 512
    uo = (3 * ATTN_W + 2 * LRU_W) // LANE

    def body(u0_ref, u1_ref, y_ref, do_ref, d_ref, w_ref, b_ref, dy_ref, dud_ref, dd_ref, dw_ref, db_ref):
        @pl.when(pl.program_id(0) == 0)
        def _():
            dd_ref[...] = jnp.zeros_like(dd_ref)
            dw_ref[...] = jnp.zeros_like(dw_ref)
            db_ref[...] = jnp.zeros_like(db_ref)
        u = jnp.concatenate([u0_ref[...], u1_ref[...]], axis=1)
        y = y_ref[...]
        do = do_ref[...]
        yg = _gelu(y)
        sg = _sigmoid(_dot(yg, w_ref[...]) + b_ref[...])
        dz = do * yg * sg * (1.0 - sg)
        dyg = do * sg + _dot_nt(dz, w_ref[...])
        dy = dyg * _gelu_grad(y)
        dy_ref[...] = dy
        dud_ref[...] = d_ref[...] * dy
        dd_ref[...] += jnp.sum(dy * u, axis=0, keepdims=True)
        dw_ref[...] += _dot_tn(yg, dz)
        db_ref[...] += jnp.sum(dz, axis=0, keepdims=True)

    u0 = pl.BlockSpec((tm, LANE), lambda i: (i, uo))
    u1 = pl.BlockSpec((tm, LANE), lambda i: (i, uo + 1))
    row = pl.BlockSpec((tm, S5_W), lambda i: (i, 0))
    vec = pl.BlockSpec((1, S5_W), lambda i: (0, 0))
    wsp = pl.BlockSpec((S5_W, S5_W), lambda i: (0, 0))
    shp = jax.ShapeDtypeStruct((s, S5_W), F32)
    vshape = jax.ShapeDtypeStruct((1, S5_W), F32)
    return pl.pallas_call(
        body, out_shape=(shp, shp, vshape, jax.ShapeDtypeStruct((S5_W, S5_W), F32), vshape),
        grid=(s // tm,), in_specs=[u0, u1, row, row, vec, wsp, vec],
        out_specs=(row, row, vec, wsp, vec), name=name,
        compiler_params=_cparams(("arbitrary",)))(proj, proj, y_pre, dout, dvec, w_glu, b_glu)


def _ffn_conv(x, prev8, cw, cb):
    y = cb + cw[FFN_CONV - 1:FFN_CONV, :] * x
    for k in range(FFN_CONV - 1):
        y = y + cw[k:k + 1, :] * _shift_down_prev(x, FFN_CONV - 1 - k, prev8)
    return y


def _ffn_act_fwd(up, cw, cb, name):
    s = up.shape[0]
    tm = 256
    tb = 2 * FFN_CB

    def body(x_ref, p_ref, cw_ref, cb_ref, o_ref):
        prev8 = p_ref[...] * (pl.program_id(1) > 0).astype(F32)
        y = _ffn_conv(x_ref[...], prev8, cw_ref[...], cb_ref[...])
        o_ref[...] = (_gelu(y[:, :FFN_CB]) * y[:, FFN_CB:]).astype(BF16)

    main = pl.BlockSpec((tm, tb), lambda j, i: (i, j))
    prev = pl.BlockSpec((8, tb), lambda j, i: (jnp.maximum(i * (tm // 8) - 1, 0), j))
    return pl.pallas_call(
        body, out_shape=jax.ShapeDtypeStruct((s, D_FF), BF16), grid=(D_FF // FFN_CB, s // tm),
        in_specs=[main, prev, pl.BlockSpec((FFN_CONV, tb), lambda j, i: (0, j)),
                  pl.BlockSpec((1, tb), lambda j, i: (0, j))],
        out_specs=pl.BlockSpec((tm, FFN_CB), lambda j, i: (i, j)), name=name,
        compiler_params=_cparams(("parallel", "parallel")))(up, up, cw, cb)


def _ffn_act_bwd(up, dact, cw, cb, name):
    s = up.shape[0]
    tm = 256
    tb = 2 * FFN_CB
    nr = s // tm

    def body(x_ref, p_ref, n_ref, da_ref, dan_ref, cw_ref, cb_ref, dup_ref, dcw_ref, dcb_ref):
        i = pl.program_id(1)

        @pl.when(i == 0)
        def _():
            dcw_ref[...] = jnp.zeros_like(dcw_ref)
            dcb_ref[...] = jnp.zeros_like(dcb_ref)
        has_next = (i < nr - 1).astype(F32)
        prev8 = p_ref[...] * (i > 0).astype(F32)
        cwv = cw_ref[...]
        x = x_ref[...]
        xe = jnp.concatenate([x, n_ref[...]], axis=0)
        dae = jnp.concatenate([da_ref[...], dan_ref[...] * has_next], axis=0)
        y = _ffn_conv(xe, prev8, cwv, cb_ref[...])
        gate, val = y[:, :FFN_CB], y[:, FFN_CB:]
        dy = jnp.concatenate([dae * val * _gelu_grad(gate), dae * _gelu(gate)], axis=1)
        dym = dy[:tm, :]
        dx = cwv[FFN_CONV - 1:FFN_CONV, :] * dym
        dcw_rows = [None] * FFN_CONV
        dcw_rows[FFN_CONV - 1] = jnp.sum(dym * x, axis=0, keepdims=True)
        for k in range(FFN_CONV - 1):
            sh = FFN_CONV - 1 - k
            dx = dx + cwv[k:k + 1, :] * pltpu.roll(dy, tm + 8 - sh, axis=0)[:tm, :]
            dcw_rows[k] = jnp.sum(dym * _shift_down_prev(x, sh, prev8), axis=0, keepdims=True)
        dup_ref[...] = dx.astype(BF16)
        dcw_ref[...] += jnp.concatenate(dcw_rows, axis=0)
        dcb_ref[...] += jnp.sum(dym, axis=0, keepdims=True)

    main = pl.BlockSpec((tm, tb), lambda j, i: (i, j))
    prev = pl.BlockSpec((8, tb), lambda j, i: (jnp.maximum(i * (tm // 8) - 1, 0), j))
    nxt = pl.BlockSpec((8, tb), lambda j, i: (jnp.minimum((i + 1) * (tm // 8), s // 8 - 1), j))
    da = pl.BlockSpec((tm, FFN_CB), lambda j, i: (i, j))
    dan = pl.BlockSpec((8, FFN_CB), lambda j, i: (jnp.minimum((i + 1) * (tm // 8), s // 8 - 1), j))
    cws = pl.BlockSpec((FFN_CONV, tb), lambda j, i: (0, j))
    cbs = pl.BlockSpec((1, tb), lambda j, i: (0, j))
    return pl.pallas_call(
        body, out_shape=(jax.ShapeDtypeStruct((s, 2 * D_FF), BF16),
                         jax.ShapeDtypeStruct((FFN_CONV, 2 * D_FF), F32),
                         jax.ShapeDtypeStruct((1, 2 * D_FF), F32)),
        grid=(D_FF // FFN_CB, nr), in_specs=[main, prev, nxt, da, dan, cws, cbs],
        out_specs=(main, cws, cbs), name=name,
        compiler_params=_cparams(("parallel", "arbitrary")))(up, up, up, dact, dact, cw, cb)


def _adamw_sum(landed, w, m, v, layer, prev, name):
    _, r, c = landed.shape
    nl = w.shape[0]
    tm = 8
    for cand in (512, 256, 128, 64, 32, 16):
        if r % cand == 0 and N_DEV * cand * c * 4 <= 4 * 1024 * 1024:
            tm = cand
            break

    def body(*refs):
        ld_ref, w_ref, m_ref, v_ref = refs[:4]
        g_ref, d_ref, mo_ref, vo_ref = refs[-4:]
        gg = ld_ref[0]
        for k in range(1, N_DEV):
            gg = gg + ld_ref[k]
        mn = ADAM_B1 * m_ref[...] + (1.0 - ADAM_B1) * gg
        vn = ADAM_B2 * v_ref[...] + (1.0 - ADAM_B2) * (gg * gg)
        m_hat = mn / (1.0 - ADAM_B1 ** ADAM_STEP)
        v_hat = vn / (1.0 - ADAM_B2 ** ADAM_STEP)
        g_ref[...] = gg
        d_ref[...] = -ADAM_LR * (m_hat / (jnp.sqrt(v_hat) + ADAM_EPS) + ADAM_WD * w_ref[...])
        mo_ref[...] = mn
        vo_ref[...] = vn

    blk = pl.BlockSpec((None, tm, c), lambda i: (layer, i, 0))
    in_specs = [pl.BlockSpec((N_DEV, tm, c), lambda i: (0, i, 0)), blk, blk, blk]
    args = [landed, w, m, v]
    aliases = {}
    if prev is not None:
        in_specs += [pl.BlockSpec(memory_space=pl.ANY)] * 4
        args += list(prev)
        aliases = {4 + k: k for k in range(4)}
    shp = jax.ShapeDtypeStruct((nl, r, c), F32)
    return pl.pallas_call(
        body, out_shape=(shp,) * 4, grid=(r // tm,), in_specs=in_specs, out_specs=(blk,) * 4,
        input_output_aliases=aliases, name=name, compiler_params=_cparams(("parallel",)))(*args)


def _all_gather(shards, name):
    na = len(shards)

    def body(*refs):
        x_refs, out_refs = refs[:na], refs[na:2 * na]
        send_sems, recv_sems, local_sems = refs[2 * na:]
        x, y, c = lax.axis_index("x"), lax.axis_index("y"), lax.axis_index("c")
        me, sibling = (x, y, c), (x, y, 1 - c)
        chips = [(1 - x, y), (x, 1 - y), (1 - x, 1 - y)]

        def copy(a, k, block, to, src=None):
            dst = out_refs[a].at[4 * block[0] + 2 * block[1] + block[2]]
            return pltpu.make_async_remote_copy(
                src_ref=dst if src is None else src, dst_ref=dst,
                send_sem=send_sems.at[7 * a + k], recv_sem=recv_sems.at[7 * a + k],
                device_id=to, device_id_type=pl.DeviceIdType.MESH)

        mine, first, passed = [], [], []
        for a in range(na):
            cp = pltpu.make_async_copy(x_refs[a], out_refs[a].at[4 * x + 2 * y + c], local_sems.at[a])
            cp.start()
            mine.append(cp)
            cps = [copy(a, 0, me, sibling, src=x_refs[a])]
            cps += [copy(a, 1 + j, me, (*chip, c), src=x_refs[a]) for j, chip in enumerate(chips)]
            for cp in cps:
                cp.start()
            first += cps
        for j, chip in enumerate(chips):
            for a in range(na):
                copy(a, 1 + j, (*chip, c), me).wait_recv()
                cp = copy(a, 4 + j, (*chip, c), sibling)
                cp.start()
                passed.append(cp)
        for a in range(na):
            copy(a, 0, sibling, me).wait_recv()
            for j, chip in enumerate(chips):
                copy(a, 4 + j, (*chip, 1 - c), me).wait_recv()
        for cp in first + passed:
            cp.wait_send()
        for cp in mine:
            cp.wait()

    anyspec = pl.BlockSpec(memory_space=pl.ANY)
    return pl.pallas_call(
        body, out_shape=tuple(jax.ShapeDtypeStruct((N_DEV,) + t.shape, t.dtype) for t in shards),
        in_specs=[anyspec] * na, out_specs=tuple([anyspec] * na),
        scratch_shapes=[pltpu.SemaphoreType.DMA((7 * na,)), pltpu.SemaphoreType.DMA((7 * na,)),
                        pltpu.SemaphoreType.DMA((na,))],
        name=name)(*shards)


def _all_to_all(bufs, name):
    na = len(bufs)

    def body(*refs):
        b_refs, out_refs = refs[:na], refs[na:2 * na]
        send_sems, recv_sems, local_sems = refs[2 * na:]
        x, y, c = lax.axis_index("x"), lax.axis_index("y"), lax.axis_index("c")
        me = 4 * x + 2 * y + c
        copies = []
        for a in range(na):
            cp = pltpu.make_async_copy(b_refs[a].at[me], out_refs[a].at[me], local_sems.at[a])
            cp.start()
            copies.append(cp)
        for k in range(1, N_DEV):
            px = x ^ ((k >> 2) & 1)
            py = y ^ ((k >> 1) & 1)
            pc = c ^ (k & 1)
            for a in range(na):
                cp = pltpu.make_async_remote_copy(
                    src_ref=b_refs[a].at[4 * px + 2 * py + pc], dst_ref=out_refs[a].at[me],
                    send_sem=send_sems.at[7 * a + k - 1], recv_sem=recv_sems.at[7 * a + k - 1],
                    device_id=(px, py, pc), device_id_type=pl.DeviceIdType.MESH)
                cp.start()
                copies.append(cp)
        for cp in copies:
            cp.wait()

    anyspec = pl.BlockSpec(memory_space=pl.ANY)
    return pl.pallas_call(
        body, out_shape=tuple(jax.ShapeDtypeStruct(t.shape, t.dtype) for t in bufs),
        in_specs=[anyspec] * na, out_specs=tuple([anyspec] * na),
        scratch_shapes=[pltpu.SemaphoreType.DMA((7 * na,)), pltpu.SemaphoreType.DMA((7 * na,)),
                        pltpu.SemaphoreType.DMA((na,))],
        name=name)(*bufs)


_HBM = pl.BlockSpec(memory_space=pltpu.HBM)
_SEM = pl.BlockSpec(memory_space=pltpu.SEMAPHORE)
_EFFECT = pltpu.SideEffectType.DATAFLOW_SIDE_EFFECTING


def _exchange_copies(src_refs, land_refs, send_sems, recv_sems, local_sems, gather):
    x, y, c = lax.axis_index("x"), lax.axis_index("y"), lax.axis_index("c")
    me = 4 * x + 2 * y + c
    local, remote = [], []
    for a, (src, land) in enumerate(zip(src_refs, land_refs)):
        local.append(pltpu.make_async_copy(src if gather else src.at[me], land.at[me], local_sems.at[a]))
    for k in range(1, N_DEV):
        px = x ^ ((k >> 2) & 1)
        py = y ^ ((k >> 1) & 1)
        pc = c ^ (k & 1)
        for a, (src, land) in enumerate(zip(src_refs, land_refs)):
            remote.append(pltpu.make_async_remote_copy(
                src_ref=src if gather else src.at[4 * px + 2 * py + pc], dst_ref=land.at[me],
                send_sem=send_sems.at[7 * a + k - 1], recv_sem=recv_sems.at[7 * a + k - 1],
                device_id=(px, py, pc), device_id_type=pl.DeviceIdType.MESH))
    return local, remote


def _exchange_start(srcs, gather, name, dep=None):
    na = len(srcs)
    lands = [lax.empty(((N_DEV,) + t.shape) if gather else t.shape, t.dtype) for t in srcs]

    def body(*refs):
        src_refs, land_refs = refs[:na], refs[na:2 * na]
        nin = 2 * na + (0 if dep is None else 1)
        send_sems, recv_sems, local_sems = refs[nin:nin + 3]
        token = refs[-1]
        local, remote = _exchange_copies(src_refs, land_refs, send_sems, recv_sems, local_sems, gather)
        for cp in local + remote:
            cp.start()
        token[...] = jnp.zeros_like(token)

    dep_specs, dep_ops = _dep_args(dep)
    hbm = lambda t: pltpu.HBM(t.shape, t.dtype)
    out = pl.pallas_call(
        body, name=name,
        out_shape=(pltpu.SemaphoreType.DMA((7 * na,)), pltpu.SemaphoreType.DMA((7 * na,)),
                   pltpu.SemaphoreType.DMA((na,)), *[hbm(t) for t in srcs], *[hbm(t) for t in lands],
                   jax.ShapeDtypeStruct((8, LANE), F32)),
        in_specs=[_HBM] * (2 * na) + dep_specs,
        out_specs=(_SEM, _SEM, _SEM, *[_HBM] * (2 * na), pl.BlockSpec(memory_space=pltpu.VMEM)),
        input_output_aliases={i: 3 + i for i in range(2 * na)},
        compiler_params=pltpu.CompilerParams(has_side_effects=_EFFECT),
    )(*[pltpu.with_memory_space_constraint(t, pltpu.HBM) for t in srcs + lands], *dep_ops)
    return (out[:3], out[3:3 + na], out[3 + na:3 + 2 * na]), out[-1]


def _exchange_wait(handle, gather, after, name):
    sems, srcs, lands = handle
    na = len(srcs)

    def body(*refs):
        src_refs, land_refs = refs[:na], refs[na:2 * na]
        send_sems, recv_sems, local_sems = refs[2 * na:2 * na + 3]
        local, remote = _exchange_copies(src_refs, land_refs, send_sems, recv_sems, local_sems, gather)
        for cp in remote:
            cp.wait_send()
            cp.wait_recv()
        for cp in local:
            cp.wait()

    hbm = lambda t: pltpu.HBM(t.shape, t.dtype)
    out = pl.pallas_call(
        body, name=name, out_shape=(*[hbm(t) for t in srcs], *[hbm(t) for t in lands]),
        in_specs=[_HBM] * (2 * na) + [_SEM] * 3 + [pl.BlockSpec(memory_space=pl.ANY)],
        out_specs=tuple([_HBM] * (2 * na)), input_output_aliases={i: i for i in range(2 * na)},
        compiler_params=pltpu.CompilerParams(has_side_effects=_EFFECT),
    )(*srcs, *lands, *sems, after)
    return out[na:]


def _block_diag(w):
    h, a, b = w.shape
    eye = jnp.eye(h, dtype=w.dtype)
    return (w[:, :, None, :] * eye[:, None, :, None]).reshape(h * a, h * b)


def _block_diag_extract(m, h):
    a, b = m.shape[0] // h, m.shape[1] // h
    return jnp.stack([m[i * a:(i + 1) * a, i * b:(i + 1) * b] for i in range(h)], axis=0)


def _ffn_interleave(w):
    lead = w.shape[:-1]
    nb = D_FF // FFN_CB
    return jnp.swapaxes(w.reshape(*lead, 2, nb, FFN_CB), -3, -2).reshape(*lead, 2 * D_FF)


def _ffn_deinterleave(w):
    lead = w.shape[:-1]
    nb = D_FF // FFN_CB
    return jnp.swapaxes(w.reshape(*lead, nb, 2, FFN_CB), -3, -2).reshape(*lead, 2 * D_FF)


def _gather_full(gathered, axis):
    shape = list(gathered.shape[1:])
    shape[axis] *= N_DEV
    return jnp.moveaxis(gathered, 0, axis).reshape(shape)


def _scatter_blocks(full, axis):
    shape = list(full.shape)
    shape[axis:axis + 1] = [N_DEV, shape[axis] // N_DEV]
    return jnp.moveaxis(full.reshape(shape), axis, 0)


def _pad_to(flat, mult):
    pad = (-flat.shape[-1]) % mult
    if pad:
        flat = jnp.concatenate([flat, jnp.zeros(flat.shape[:-1] + (pad,), flat.dtype)], axis=-1)
    return flat


def _layer_fwd(h_in, w, cos, sin, l, dep, get_ffn):
    tag = "l%d_" % l
    proj = _mm_nn(h_in, w['w_in'], 512, D_IN, tag + "proj", dep=dep)
    qkv = _rope_fwd(proj, cos, sin, tag + "rope")
    outs, lses = [], []
    for d, qv in zip(DILATIONS, qkv):
        o, ls = _attn_fwd(qv, d, tag + "attn_d%d" % d)
        outs.append(o)
        lses.append(ls)
    lru = _lru_fwd(proj, w['lru_conv_w'], w['lru_conv_b'], w['lru_wr'], w['lru_br'], w['lru_wi'],
                   w['lru_bi'], w['lru_lambda'], tag + "lru")
    x_re, x_im, y_acc = _s5_scan_fwd(proj, w['s5_bb_re'], w['s5_bb_im'], w['s5_lam_re'], w['s5_lam_im'],
                                     w['s5_cc_re'], w['s5_cc_im'], tag + "s5_scan")
    s5, y_pre = _s5_out_fwd(proj, y_acc, w['s5_d'], w['s5_w_glu'], w['s5_b_glu'], tag + "s5_out")
    mixed, attn_o, attn_lse = _mix_fwd(outs, lses, lru, s5, w['mix_norm_g'], tag + "mix")
    mixo = _mm_nn(mixed, w['w_out'], 512, D_MODEL, tag + "out_proj")
    r1, h1 = _ln_fwd(h_in, mixo, w['ln1_g'], w['ln1_b'], tag + "ln1")
    w['w_up_g'], w['w_down'], ffn_dep = get_ffn(l, h1)
    up = _mm_up(h1, w['w_up_g'], 1024, tag + "up_proj", dep=ffn_dep)
    act = _ffn_act_fwd(up, w['ffn_conv_w'], w['ffn_conv_b'], tag + "ffn_act")
    ffn = _mm_nn(act, w['w_down'], 512, D_MODEL, tag + "down_proj")
    r2, h2 = _ln_fwd(h1, ffn, w['ln2_g'], w['ln2_b'], tag + "ln2")
    saved = dict(h_in=h_in, proj=proj, qkv=qkv, lru=lru, x_re=x_re, x_im=x_im, y_pre=y_pre, s5=s5,
                 mixed=mixed, attn_o=attn_o, attn_lse=attn_lse, r1=r1, h1=h1, up=up, act=act, r2=r2)
    return h2, saved


def _layer_bwd_ffn(dh2, sv, w, l, dep=None):
    tag = "l%d_" % l
    g = {}
    dr2, g['ln2_g'], g['ln2_b'] = _ln_bwd(sv['r2'], dh2, w['ln2_g'], tag + "ln2_bwd", dep=dep)
    g['w_down'] = _mm_tn(sv['act'], dr2, 1024, D_MODEL, 512, tag + "down_dw")
    dact = _mm_nt(dr2, w['w_down'], 512, D_FF, tag + "down_dx")
    dup, g['ffn_conv_w'], g['ffn_conv_b'] = _ffn_act_bwd(sv['up'], dact, w['ffn_conv_w'], w['ffn_conv_b'],
                                                        tag + "ffn_act_bwd")
    g['w_up_g'] = _mm_up_dw(sv['h1'], dup, 512, tag + "up_dw")
    dh1 = _mm_up_dx(dup, w['w_up_g'], dr2, ALPHA, 1024, tag + "up_dx")
    return dh1, g


def _layer_bwd_mix(dh1, sv, w, cos, sin, l, dep, after_out_grad, after_in_grad):
    tag = "l%d_" % l
    g = {}
    dr1, g['ln1_g'], g['ln1_b'] = _ln_bwd(sv['r1'], dh1, w['ln1_g'], tag + "ln1_bwd", dep=dep)
    g['w_out'] = _mm_tn(sv['mixed'], dr1, 1024, D_MODEL, 512, tag + "out_dw")
    dmixed = _mm_nt(dr1, w['w_out'], 512, D_MODEL, tag + "out_dx", dep=after_out_grad(l, g['w_out']))
    d_o, dlru, ds5, g['mix_norm_g'] = _mix_bwd(dmixed, sv['attn_o'][0], sv['lru'], sv['s5'], w['mix_norm_g'],
                                               tag + "mix_bwd")
    dy, dud, g['s5_d'], g['s5_w_glu'], g['s5_b_glu'] = _s5_out_bwd(
        sv['proj'], sv['y_pre'], ds5, w['s5_d'], w['s5_w_glu'], w['s5_b_glu'], tag + "s5_out_bwd")
    du, g['s5_lam_re'], g['s5_lam_im'], g['s5_bb_re'], g['s5_bb_im'], g['s5_cc_re'], g['s5_cc_im'] = \
        _s5_scan_bwd(sv['proj'], dy, dud, sv['x_re'], sv['x_im'], w['s5_bb_re'], w['s5_bb_im'],
                     w['s5_lam_re'], w['s5_lam_im'], w['s5_cc_re'], w['s5_cc_im'], tag + "s5_scan_bwd")
    (dxr, dgate, g['lru_conv_w'], g['lru_conv_b'], g['lru_wr'], g['lru_br'], g['lru_wi'], g['lru_bi'],
     g['lru_lambda']) = _lru_bwd(sv['proj'], dlru, w['lru_conv_w'], w['lru_conv_b'], w['lru_wr'],
                                 w['lru_br'], w['lru_wi'], w['lru_bi'], w['lru_lambda'], tag + "lru_bwd")
    dqkv = [_attn_bwd(sv['qkv'][b], sv['attn_o'][b], d_o[b], sv['attn_lse'][b], d, tag + "attn_bwd_d%d" % d)
            for b, d in enumerate(DILATIONS)]
    dproj = _dproj_assemble(dqkv, dxr, dgate, du, cos, sin, tag + "dproj")
    g['w_in'] = _mm_tn(sv['h_in'], dproj, 1024, D_IN, 512, tag + "in_dw")
    dh_in = _mm_nt(dproj, w['w_in'], 512, D_MODEL, tag + "in_dx", add=dr1, add_scale=ALPHA,
                   dep=after_in_grad(l, g['w_in']))
    return dh_in, g


def _s5_rep(a):
    return jnp.repeat(a, S5_C, axis=0)


def _prepare_layer(p, l):
    w = {}
    for n in ('w_in', 'w_out', 's5_w_glu'):
        w[n] = p[n].astype(BF16)
    w['ffn_conv_w'] = _ffn_interleave(p['ffn_conv_w'])
    w['ffn_conv_b'] = _ffn_interleave(p['ffn_conv_b'])[None, :]
    w['lru_conv_w'] = p['lru_conv_w']
    for n in ('lru_conv_b', 'lru_br', 'lru_bi', 'lru_lambda', 's5_b_glu', 'mix_norm_g',
              'ln1_g', 'ln1_b', 'ln2_g', 'ln2_b'):
        w[n] = p[n][None, :]
    w['lru_wr'] = _block_diag(p['lru_wr']).astype(BF16)
    w['lru_wi'] = _block_diag(p['lru_wi']).astype(BF16)
    w['s5_d'] = p['s5_d'].reshape(1, S5_W)
    disc_in = (_s5_rep(p['s5_a_re']), _s5_rep(p['s5_a_im']),
               _s5_rep(jnp.broadcast_to(p['s5_log_step'][:, None], (S5_G, S5_P))),
               jnp.swapaxes(p['s5_b_re'], 1, 2).reshape(S5_W, S5_P),
               jnp.swapaxes(p['s5_b_im'], 1, 2).reshape(S5_W, S5_P))
    ab_re, ab_im, bb_re, bb_im = _s5_disc_fwd(*disc_in, "l%d_s5_disc" % l)
    w['s5_disc_in'] = disc_in
    w['s5_lam_re'] = ab_re.reshape(S5_G, S5_C, S5_P)[:, 0, :].reshape(1, S5_STATES)
    w['s5_lam_im'] = ab_im.reshape(S5_G, S5_C, S5_P)[:, 0, :].reshape(1, S5_STATES)
    w['s5_bb_re'] = _block_diag(bb_re.reshape(S5_G, S5_C, S5_P)).astype(BF16)
    w['s5_bb_im'] = _block_diag(bb_im.reshape(S5_G, S5_C, S5_P)).astype(BF16)
    w['s5_cc_re'] = _block_diag(jnp.swapaxes(p['s5_c_re'], 1, 2)).astype(BF16)
    w['s5_cc_im'] = _block_diag(jnp.swapaxes(p['s5_c_im'], 1, 2)).astype(BF16)
    return w


def _finish_layer_grads(g, w, l):
    out = {}
    for n in ('w_in', 'w_out', 'w_down', 'w_up_g', 's5_w_glu', 'lru_conv_w'):
        out[n] = g[n]
    out['ffn_conv_w'] = _ffn_deinterleave(g['ffn_conv_w'])
    out['ffn_conv_b'] = _ffn_deinterleave(g['ffn_conv_b'])[0]
    for n in ('lru_conv_b', 'lru_br', 'lru_bi', 'lru_lambda', 's5_b_glu', 'mix_norm_g',
              'ln1_g', 'ln1_b', 'ln2_g', 'ln2_b'):
        out[n] = g[n][0]
    out['lru_wr'] = _block_diag_extract(g['lru_wr'], LRU_W // HEAD)
    out['lru_wi'] = _block_diag_extract(g['lru_wi'], LRU_W // HEAD)
    out['s5_d'] = g['s5_d'].reshape(S5_G, S5_C)
    out['s5_c_re'] = jnp.swapaxes(_block_diag_extract(g['s5_cc_re'], S5_G), 1, 2)
    out['s5_c_im'] = jnp.swapaxes(_block_diag_extract(g['s5_cc_im'], S5_G), 1, 2)
    rep = lambda v: _s5_rep(v.reshape(S5_G, S5_P)) * (1.0 / S5_C)
    cts = (rep(g['s5_lam_re']), rep(g['s5_lam_im']),
           _block_diag_extract(g['s5_bb_re'], S5_G).reshape(S5_W, S5_P),
           _block_diag_extract(g['s5_bb_im'], S5_G).reshape(S5_W, S5_P))
    da_re, da_im, dls, dbt_re, dbt_im = _s5_disc_bwd(*w['s5_disc_in'], cts, "l%d_s5_disc_bwd" % l)
    out['s5_a_re'] = da_re.reshape(S5_G, S5_C, S5_P).sum(axis=1)
    out['s5_a_im'] = da_im.reshape(S5_G, S5_C, S5_P).sum(axis=1)
    out['s5_log_step'] = dls.reshape(S5_G, S5_C * S5_P).sum(axis=1)
    out['s5_b_re'] = jnp.swapaxes(dbt_re.reshape(S5_G, S5_C, S5_P), 1, 2)
    out['s5_b_im'] = jnp.swapaxes(dbt_im.reshape(S5_G, S5_C, S5_P), 1, 2)
    return out


def _run_step(x, target, get_layer, get_ffn, after_ffn_grads, after_out_grad, after_in_grad, after_layer):
    cos, sin = _rope_tables(x.shape[0])
    h = x
    ws, saved = [], []
    for l in range(DEPTH):
        p, dep = get_layer(l, h)
        ws.append(_prepare_layer(p, l))
        h, sv = _layer_fwd(h, ws[l], cos, sin, l, dep, get_ffn)
        saved.append(sv)
    dh, loss_vec = _loss_head(h, target)
    dep = None
    for l in reversed(range(DEPTH)):
        dh1, g = _layer_bwd_ffn(dh, saved[l], ws[l], l, dep)
        dep = after_ffn_grads(l, g)
        dh, g_mix = _layer_bwd_mix(dh1, saved[l], ws[l], cos, sin, l, dep, after_out_grad, after_in_grad)
        g.update(g_mix)
        after_layer(l, _finish_layer_grads(g, ws[l], l))
        dep = None
    return loss_vec[0, 0], dh


def _local_step(x, target, layers):
    grads = [None] * DEPTH

    def keep(l, g):
        grads[l] = g

    def ffn(l, h1):
        return layers[l]['w_up_g'].astype(BF16), layers[l]['w_down'].astype(BF16), None

    none = lambda l, g: None
    loss, dx = _run_step(x, target, lambda l, h: (layers[l], None), ffn, none, none, none, keep)
    return loss, dx, grads


def kernel(x, w_in, lru_conv_w, lru_conv_b, lru_wr, lru_br, lru_wi, lru_bi, lru_lambda, s5_a_re, s5_a_im, s5_b_re, s5_b_im, s5_c_re, s5_c_im, s5_d, s5_log_step, s5_w_glu, s5_b_glu, mix_norm_g, w_out, ln1_g, ln1_b, w_up, ffn_conv_w, ffn_conv_b, w_down, ln2_g, ln2_b, loss_target, m_w_in, m_lru_conv_w, m_lru_conv_b, m_lru_wr, m_lru_br, m_lru_wi, m_lru_bi, m_lru_lambda, m_s5_a_re, m_s5_a_im, m_s5_b_re, m_s5_b_im, m_s5_c_re, m_s5_c_im, m_s5_d, m_s5_log_step, m_s5_w_glu, m_s5_b_glu, m_mix_norm_g, m_w_out, m_ln1_g, m_ln1_b, m_w_up, m_ffn_conv_w, m_ffn_conv_b, m_w_down, m_ln2_g, m_ln2_b, v_w_in, v_lru_conv_w, v_lru_conv_b, v_lru_wr, v_lru_br, v_lru_wi, v_lru_bi, v_lru_lambda, v_s5_a_re, v_s5_a_im, v_s5_b_re, v_s5_b_im, v_s5_c_re, v_s5_c_im, v_s5_d, v_s5_log_step, v_s5_w_glu, v_s5_b_glu, v_mix_norm_g, v_w_out, v_ln1_g, v_ln1_b, v_w_up, v_ffn_conv_w, v_ffn_conv_b, v_w_down, v_ln2_g, v_ln2_b):
    args = locals()
    wl = {n: args[n] for n in WEIGHTS}
    ml = {n: args['m_' + n] for n in WEIGHTS}
    vl = {n: args['v_' + n] for n in WEIGHTS}
    me = 4 * lax.axis_index("x") + 2 * lax.axis_index("y") + lax.axis_index("c")

    small_sizes = [int(wl[n].size) for n in SMALL_SHARDED]
    small_flat = _pad_to(jnp.concatenate([wl[n].reshape(-1) for n in SMALL_SHARDED]), 8 * 1024)
    small_all, = _all_gather([small_flat.reshape(-1, 1024)], "gather_small")
    small_all = small_all.reshape(N_DEV, -1)
    small_full, off = {}, 0
    for n, sz in zip(SMALL_SHARDED, small_sizes):
        small_full[n] = _gather_full(small_all[:, off:off + sz].reshape((N_DEV,) + wl[n].shape), SHARD_AXIS[n])
        off += sz
    def mixer_params(l, gathered):
        g_in, g_out = gathered
        p = {n: wl[n][l] for n in REPLICATED}
        p.update({n: small_full[n][l] for n in SMALL_SHARDED})
        p['w_in'] = _gather_full(g_in, 1)
        p['w_out'] = g_out.reshape(D_MODEL, D_MODEL)
        return p

    mix_names, ffn_names = ('w_in', 'w_out'), ('w_up', 'w_down')
    shards = lambda names, l: [wl[n][l].astype(BF16) for n in names]
    mix0 = _all_gather(shards(mix_names, 0), "gather_mix_l0")
    gathers = {}
    gathers[0, 'ffn'], ffn0_token = _exchange_start(shards(ffn_names, 0), True, "gather_ffn_l0_start", dep=mix0[0])
    def get_layer(l, h):
        if l == 0:
            return mixer_params(0, mix0), ffn0_token
        return mixer_params(1, _exchange_wait(gathers[1, 'mix'], True, h, "gather_mix_l1_wait")), None

    def get_ffn(l, h1):
        g_up, g_down = _exchange_wait(gathers[l, 'ffn'], True, h1, "gather_ffn_l%d_wait" % l)
        token = None
        if l == 0:
            gathers[1, 'mix'], token = _exchange_start(shards(mix_names, 1), True, "gather_mix_l1_start", dep=g_up)
            gathers[1, 'ffn'], token = _exchange_start(shards(ffn_names, 1), True, "gather_ffn_l1_start", dep=token)
        return g_up, g_down.reshape(D_FF, D_MODEL), token

    scatters, grads = {}, [None] * DEPTH

    def after_ffn_grads(l, g):
        send = [g['w_up_g'], g['w_down'].reshape(N_DEV, D_FF // N_DEV, D_MODEL)]
        scatters[l, 'ffn'], token = _exchange_start(send, False, "scatter_ffn_l%d_start" % l)
        return token

    def after_out_grad(l, g_out):
        send = [g_out.reshape(N_DEV, D_MODEL // N_DEV, D_MODEL)]
        scatters[l, 'out'], token = _exchange_start(send, False, "scatter_out_l%d_start" % l)
        return token

    def after_in_grad(l, g_in):
        scatters[l, 'in'], token = _exchange_start([_scatter_blocks(g_in, 1)], False, "scatter_in_l%d_start" % l)
        return token

    def after_layer(l, g):
        grads[l] = g

    loss_local, grad_x = _run_step(x[0], loss_target[0], get_layer, get_ffn, after_ffn_grads, after_out_grad,
                                   after_in_grad, after_layer)
    loss = lax.psum(loss_local, AXES)

    results = {}
    big_prev = {n: None for n in BIG}

    def finish_big(l, part, names, after):
        landed = _exchange_wait(scatters[l, part], False, after, "scatter_%s_l%d_wait" % (part, l))
        for n, ld in zip(names, landed):
            big_prev[n] = _adamw_sum(ld, wl[n], ml[n], vl[n], l, big_prev[n], "adamw_%s_l%d" % (n, l))

    for l, part, names in ((1, 'ffn', ffn_names), (1, 'out', ('w_out',)), (1, 'in', ('w_in',)),
                           (0, 'ffn', ffn_names), (0, 'out', ('w_out',))):
        finish_big(l, part, names, grad_x)

    stacked = {n: jnp.stack([grads[l][n] for l in range(DEPTH)], axis=0) for n in SMALL_SHARDED + REPLICATED}
    rep_sizes = [int(wl[n].size) for n in REPLICATED]
    rep_per = -(-sum(rep_sizes) // (N_DEV * 1024)) * 1024

    def rep_flat(tree):
        return _pad_to(jnp.concatenate([tree[n].reshape(-1) for n in REPLICATED]), N_DEV * rep_per)

    rows = [_scatter_blocks(stacked[n], SHARD_AXIS[n]).reshape(N_DEV, -1) for n in SMALL_SHARDED]
    rows.append(rep_flat(stacked).reshape(N_DEV, rep_per))
    small_send = _pad_to(jnp.concatenate(rows, axis=1), 8 * 1024)
    n_own = small_send.shape[1]
    small_landed, = _all_to_all([small_send.reshape(N_DEV, n_own // 1024, 1024)], "scatter_small")

    def own_flat(tree):
        parts = [tree[n].reshape(-1) for n in SMALL_SHARDED]
        parts.append(lax.dynamic_slice(rep_flat(tree), (me * rep_per,), (rep_per,)))
        return _pad_to(jnp.concatenate(parts), 8 * 1024).reshape(1, n_own // 1024, 1024)

    small_res = _adamw_sum(small_landed, own_flat(wl), own_flat(ml), own_flat(vl), 0, None, "adamw_small")

    kinds = ('grad', 'delta', 'm', 'v')
    sh_total = sum(small_sizes)
    for kind, arr in zip(kinds, small_res):
        flat = arr.reshape(-1)
        off = 0
        for n, sz in zip(SMALL_SHARDED, small_sizes):
            results[kind, n] = flat[off:off + sz].reshape(wl[n].shape)
            off += sz
    rep_own = jnp.stack([a.reshape(-1)[sh_total:sh_total + rep_per] for a in small_res])
    rep_all, = _all_gather([rep_own.reshape(4 * rep_per // 1024, 1024)], "gather_replicated")
    finish_big(0, 'in', ('w_in',), rep_all)
    for n in BIG:
        results['grad', n], results['delta', n], results['m', n], results['v', n] = big_prev[n]
    rep_all = rep_all.reshape(N_DEV, 4, rep_per)
    for k, kind in enumerate(kinds):
        flat = rep_all[:, k, :].reshape(-1)
        off = 0
        for n, sz in zip(REPLICATED, rep_sizes):
            results[kind, n] = flat[off:off + sz].reshape(wl[n].shape)
            off += sz

    out = [loss, grad_x[None]]
    for kind in kinds:
        out.extend(results[kind, n] for n in WEIGHTS)
    return tuple(out)
```

```python
import functools
import math

import jax
import jax.numpy as jnp
from jax import lax
from jax.experimental import pallas as pl
from jax.experimental.pallas import tpu as pltpu

F32 = jnp.float32
BF16 = jnp.bfloat16

N_DEV = 8
DEPTH = 2
D_MODEL = 1024
ATTN_W = 384
LRU_W = 384
S5_W = 256
D_IN = 2176
D_FF = 3072
HEAD = 64
ATTN_BLK = 128
ATTN_TILE = 1024
DILATIONS = (1, 4, 16)
S5_G = 16
S5_P = 64
S5_C = 16
S5_STATES = S5_G * S5_P
LRU_C = 8.0
LRU_CONV = 4
FFN_CONV = 3
ROPE_THETA = 10000.0
ALPHA = (2 * DEPTH) ** 0.25
LN_EPS = 1e-5
RMS_EPS = 1e-6
ADAM_LR, ADAM_B1, ADAM_B2, ADAM_EPS, ADAM_WD, ADAM_STEP = 0.001, 0.9, 0.999, 1e-8, 0.01, 10

LANE = 128
SCAN_T = 256
S5_BLK = 256
FFN_CB = 2 * D_FF // N_DEV
VMEM_LIMIT = 56 * 1024 * 1024

AXES = ("x", "y", "c")

WEIGHTS = ['w_in', 'lru_conv_w', 'lru_conv_b', 'lru_wr', 'lru_br', 'lru_wi', 'lru_bi', 'lru_lambda',
           's5_a_re', 's5_a_im', 's5_b_re', 's5_b_im', 's5_c_re', 's5_c_im', 's5_d', 's5_log_step',
           's5_w_glu', 's5_b_glu', 'mix_norm_g', 'w_out', 'ln1_g', 'ln1_b', 'w_up', 'ffn_conv_w',
           'ffn_conv_b', 'w_down', 'ln2_g', 'ln2_b']
SHARD_AXIS = {'w_in': 2, 'lru_conv_w': 2, 's5_w_glu': 1, 'w_out': 1, 'w_up': 2, 'ffn_conv_w': 2, 'w_down': 1}
BIG = ['w_in', 'w_out', 'w_up', 'w_down']
SMALL_SHARDED = ['lru_conv_w', 'ffn_conv_w', 's5_w_glu']
REPLICATED = [n for n in WEIGHTS if n not in SHARD_AXIS]


def _cparams(sem=None):
    return pltpu.CompilerParams(dimension_semantics=sem, vmem_limit_bytes=VMEM_LIMIT)


def _ffn_dev(jb):
    return jb // 2 + (N_DEV // 2) * (jb % 2)


def _gelu(x):
    c = math.sqrt(2.0 / math.pi)
    t = jnp.tanh(c * (x + 0.044715 * (x * x * x)))
    return 0.5 * x * (1.0 + t)


def _gelu_grad(x):
    c = math.sqrt(2.0 / math.pi)
    x2 = x * x
    t = jnp.tanh(c * (x + 0.044715 * (x2 * x)))
    return 0.5 * (1.0 + t) + 0.5 * x * (1.0 - t * t) * (c * (1.0 + 3.0 * 0.044715 * x2))


def _sigmoid(x):
    return 1.0 / (1.0 + jnp.exp(-x))


def _log1p(x):
    u = 1.0 + x
    d = u - 1.0
    return jnp.where(d == 0.0, x, jnp.log(u) * (x / jnp.where(d == 0.0, 1.0, d)))


def _softplus(x):
    return jnp.maximum(x, 0.0) + _log1p(jnp.exp(-jnp.abs(x)))


def _expm1(x):
    return jnp.tanh(0.5 * x) * (jnp.exp(x) + 1.0)


def _dot(a, b):
    return jnp.dot(a.astype(BF16), b.astype(BF16), preferred_element_type=F32)


def _dot_nt(a, b):
    return lax.dot_general(a.astype(BF16), b.astype(BF16), (((1,), (1,)), ((), ())),
                           preferred_element_type=F32)


def _dot_tn(a, b):
    return lax.dot_general(a.astype(BF16), b.astype(BF16), (((0,), (0,)), ((), ())),
                           preferred_element_type=F32)


def _rows(shape):
    return lax.broadcasted_iota(jnp.int32, shape, 0)


def _shift_down_prev(x, s, prev8):
    if s == 0:
        return x
    t, l = x.shape
    r = pltpu.roll(x, s, axis=0)
    pr = pltpu.roll(prev8, s, axis=0)
    pad = jnp.concatenate([pr, jnp.zeros((t - 8, l), x.dtype)], axis=0)
    return jnp.where(_rows(x.shape) < s, pad, r)


def _shift_up_next(x, s, next8):
    if s == 0:
        return x
    t, l = x.shape
    r = pltpu.roll(x, t - s, axis=0)
    nx = pltpu.roll(next8, 8 - s, axis=0)
    pad = jnp.concatenate([jnp.zeros((t - 8, l), x.dtype), nx], axis=0)
    return jnp.where(_rows(x.shape) >= t - s, pad, r)


SUB = 8


def _tile_shift(x, s, fill, reverse):
    t = x.shape[0]
    pos = _rows(x.shape) & (SUB - 1)
    if reverse:
        return jnp.where(pos < SUB - s, pltpu.roll(x, t - s, axis=0), fill)
    return jnp.where(pos >= s, pltpu.roll(x, s, axis=0), fill)


def _scan_chunk(a, x, carry, reverse=False):
    s = 1
    while s < SUB:
        x = x + a * _tile_shift(x, s, 0.0, reverse)
        a = a * _tile_shift(a, s, 1.0, reverse)
        s *= 2
    nv = x.shape[0] // SUB
    out = [None] * nv
    for v in (reversed(range(nv)) if reverse else range(nv)):
        rows = slice(v * SUB, (v + 1) * SUB)
        out[v] = x[rows, :] + a[rows, :] * carry
        carry = out[v][0:1, :] if reverse else out[v][SUB - 1:SUB, :]
    return jnp.concatenate(out, axis=0)


def _cmul(ar, ai, br, bi):
    return ar * br - ai * bi, ar * bi + ai * br


def _cscan_consts(lr, li, reverse):
    pows = [(lr, li)]
    for _ in range(2):
        pows.append(_cmul(*pows[-1], *pows[-1]))
    rows = [(lr, li)]
    for _ in range(SUB - 1):
        rows.append(_cmul(*rows[-1], lr, li))
    if reverse:
        rows = rows[::-1]
    return pows, (jnp.concatenate([r for r, _ in rows], axis=0), jnp.concatenate([i for _, i in rows], axis=0))


def _cscan_chunk(xr, xi, consts, carry, reverse=False):
    pows, (p8r, p8i) = consts
    s = 1
    for pr, pi in pows:
        sr = _tile_shift(xr, s, 0.0, reverse)
        si = _tile_shift(xi, s, 0.0, reverse)
        xr, xi = xr + pr * sr - pi * si, xi + pr * si + pi * sr
        s *= 2
    nv = xr.shape[0] // SUB
    out_r, out_i = [None] * nv, [None] * nv
    cr, ci = carry
    for v in (reversed(range(nv)) if reverse else range(nv)):
        rows = slice(v * SUB, (v + 1) * SUB)
        out_r[v] = xr[rows, :] + p8r * cr - p8i * ci
        out_i[v] = xi[rows, :] + p8r * ci + p8i * cr
        edge = slice(0, 1) if reverse else slice(SUB - 1, SUB)
        cr, ci = out_r[v][edge, :], out_i[v][edge, :]
    return jnp.concatenate(out_r, axis=0), jnp.concatenate(out_i, axis=0)


def _dep_args(dep):
    return ([], []) if dep is None else ([pl.BlockSpec(memory_space=pl.ANY)], [dep])


def _mm_nn(a, b, tm, tn, name, out_dtype=F32, dep=None):
    m, k = a.shape
    n = b.shape[1]

    def body(a_ref, b_ref, *rest):
        o_ref = rest[-1]
        o_ref[...] = _dot(a_ref[...], b_ref[...]).astype(out_dtype)

    dep_specs, dep_ops = _dep_args(dep)
    return pl.pallas_call(
        body, out_shape=jax.ShapeDtypeStruct((m, n), out_dtype), grid=(n // tn, m // tm),
        in_specs=[pl.BlockSpec((tm, k), lambda j, i: (i, 0)),
                  pl.BlockSpec((k, tn), lambda j, i: (0, j))] + dep_specs,
        out_specs=pl.BlockSpec((tm, tn), lambda j, i: (i, j)), name=name,
        compiler_params=_cparams(("parallel", "parallel")))(a, b, *dep_ops)


def _mm_nt(a, w, tm, tn, name, add=None, add_scale=1.0, dep=None):
    m, k = a.shape
    n = w.shape[0]

    def body(a_ref, w_ref, *rest):
        o_ref = rest[-1]
        if add is None:
            o_ref[...] = _dot_nt(a_ref[...], w_ref[...])
        else:
            o_ref[...] = _dot_nt(a_ref[...], w_ref[...]) + add_scale * rest[0][...]

    in_specs = [pl.BlockSpec((tm, k), lambda j, i: (i, 0)), pl.BlockSpec((tn, k), lambda j, i: (j, 0))]
    args = [a, w]
    if add is not None:
        in_specs.append(pl.BlockSpec((tm, tn), lambda j, i: (i, j)))
        args.append(add)
    dep_specs, dep_ops = _dep_args(dep)
    return pl.pallas_call(
        body, out_shape=jax.ShapeDtypeStruct((m, n), F32), grid=(n // tn, m // tm),
        in_specs=in_specs + dep_specs, out_specs=pl.BlockSpec((tm, tn), lambda j, i: (i, j)), name=name,
        compiler_params=_cparams(("parallel", "parallel")))(*args, *dep_ops)


def _mm_tn(a, b, tm, tn, ts, name):
    s, m = a.shape
    n = b.shape[1]

    def body(a_ref, b_ref, o_ref):
        @pl.when(pl.program_id(2) == 0)
        def _():
            o_ref[...] = jnp.zeros_like(o_ref)
        o_ref[...] += _dot_tn(a_ref[...], b_ref[...])

    return pl.pallas_call(
        body, out_shape=jax.ShapeDtypeStruct((m, n), F32), grid=(m // tm, n // tn, s // ts),
        in_specs=[pl.BlockSpec((ts, tm), lambda i, j, k: (k, i)), pl.BlockSpec((ts, tn), lambda i, j, k: (k, j))],
        out_specs=pl.BlockSpec((tm, tn), lambda i, j, k: (i, j)), name=name,
        compiler_params=_cparams(("parallel", "parallel", "arbitrary")))(a, b)


def _mm_up(h, wg, tm, name, dep=None):
    s, d = h.shape

    def body(a_ref, w_ref, *rest):
        rest[-1][...] = _dot(a_ref[...], w_ref[...])

    dep_specs, dep_ops = _dep_args(dep)
    return pl.pallas_call(
        body, out_shape=jax.ShapeDtypeStruct((s, 2 * D_FF), F32), grid=(s // tm, N_DEV),
        in_specs=[pl.BlockSpec((tm, d), lambda i, j: (i, 0)),
                  pl.BlockSpec((None, d, FFN_CB), lambda i, j: (_ffn_dev(j), 0, 0))] + dep_specs,
        out_specs=pl.BlockSpec((tm, FFN_CB), lambda i, j: (i, j)), name=name,
        compiler_params=_cparams(("parallel", "parallel")))(h, wg, *dep_ops)


def _mm_up_dx(dup, wg, add, add_scale, tm, name):
    s = dup.shape[0]
    d = wg.shape[1]

    def body(a_ref, w_ref, c_ref, o_ref):
        @pl.when(pl.program_id(1) == 0)
        def _():
            o_ref[...] = add_scale * c_ref[...]
        o_ref[...] += _dot_nt(a_ref[...], w_ref[...])

    return pl.pallas_call(
        body, out_shape=jax.ShapeDtypeStruct((s, d), F32), grid=(s // tm, N_DEV),
        in_specs=[pl.BlockSpec((tm, FFN_CB), lambda i, j: (i, j)),
                  pl.BlockSpec((None, d, FFN_CB), lambda i, j: (_ffn_dev(j), 0, 0)),
                  pl.BlockSpec((tm, d), lambda i, j: (i, 0))],
        out_specs=pl.BlockSpec((tm, d), lambda i, j: (i, 0)), name=name,
        compiler_params=_cparams(("parallel", "arbitrary")))(dup, wg, add)


def _mm_up_dw(h, dup, ts, name):
    s, d = h.shape

    def body(a_ref, b_ref, o_ref):
        @pl.when(pl.program_id(1) == 0)
        def _():
            o_ref[...] = jnp.zeros_like(o_ref)
        o_ref[...] += _dot_tn(a_ref[...], b_ref[...])

    return pl.pallas_call(
        body, out_shape=jax.ShapeDtypeStruct((N_DEV, d, FFN_CB), F32), grid=(N_DEV, s // ts),
        in_specs=[pl.BlockSpec((ts, d), lambda j, k: (k, 0)), pl.BlockSpec((ts, FFN_CB), lambda j, k: (k, j))],
        out_specs=pl.BlockSpec((None, d, FFN_CB), lambda j, k: (_ffn_dev(j), 0, 0)), name=name,
        compiler_params=_cparams(("parallel", "arbitrary")))(h, dup)


def _ln_fwd(a, b, g, bias, name):
    s, d = a.shape
    tm = 512

    def body(a_ref, b_ref, g_ref, bias_ref, r_ref, h_ref):
        r = ALPHA * a_ref[...] + b_ref[...]
        mu = jnp.mean(r, axis=-1, keepdims=True)
        xc = r - mu
        var = jnp.mean(xc * xc, axis=-1, keepdims=True)
        r_ref[...] = r
        h_ref[...] = xc * lax.rsqrt(var + LN_EPS) * g_ref[...] + bias_ref[...]

    row = pl.BlockSpec((tm, d), lambda i: (i, 0))
    vec = pl.BlockSpec((1, d), lambda i: (0, 0))
    return pl.pallas_call(
        body, out_shape=(jax.ShapeDtypeStruct((s, d), F32), jax.ShapeDtypeStruct((s, d), F32)),
        grid=(s // tm,), in_specs=[row, row, vec, vec], out_specs=(row, row), name=name,
        compiler_params=_cparams(("parallel",)))(a, b, g, bias)


def _ln_bwd(r, dh, g, name, dep=None):
    s, d = r.shape
    tm = 512

    def body(r_ref, dh_ref, g_ref, *rest):
        dr_ref, dg_ref, db_ref = rest[-3:]

        @pl.when(pl.program_id(0) == 0)
        def _():
            dg_ref[...] = jnp.zeros_like(dg_ref)
            db_ref[...] = jnp.zeros_like(db_ref)
        rr = r_ref[...]
        dh_ = dh_ref[...]
        mu = jnp.mean(rr, axis=-1, keepdims=True)
        xc = rr - mu
        var = jnp.mean(xc * xc, axis=-1, keepdims=True)
        rstd = lax.rsqrt(var + LN_EPS)
        xh = xc * rstd
        dxh = dh_ * g_ref[...]
        m1 = jnp.mean(dxh, axis=-1, keepdims=True)
        m2 = jnp.mean(dxh * xh, axis=-1, keepdims=True)
        dr_ref[...] = rstd * (dxh - m1 - xh * m2)
        dg_ref[...] += jnp.sum(dh_ * xh, axis=0, keepdims=True)
        db_ref[...] += jnp.sum(dh_, axis=0, keepdims=True)

    row = pl.BlockSpec((tm, d), lambda i: (i, 0))
    vec = pl.BlockSpec((1, d), lambda i: (0, 0))
    dep_specs, dep_ops = _dep_args(dep)
    return pl.pallas_call(
        body, out_shape=(jax.ShapeDtypeStruct((s, d), F32), jax.ShapeDtypeStruct((1, d), F32),
                         jax.ShapeDtypeStruct((1, d), F32)),
        grid=(s // tm,), in_specs=[row, row, vec] + dep_specs, out_specs=(row, vec, vec), name=name,
        compiler_params=_cparams(("arbitrary",)))(r, dh, g, *dep_ops)


def _loss_head(y, target):
    s, d = y.shape
    tm = 512

    def body(y_ref, t_ref, dy_ref, l_ref):
        @pl.when(pl.program_id(0) == 0)
        def _():
            l_ref[...] = jnp.zeros_like(l_ref)
        e = y_ref[...] - t_ref[...]
        dy_ref[...] = e * (1.0 / d)
        part = 0.5 * jnp.sum(jnp.mean(e * e, axis=-1, keepdims=True), axis=0, keepdims=True)
        l_ref[...] += jnp.broadcast_to(part, l_ref.shape)

    row = pl.BlockSpec((tm, d), lambda i: (i, 0))
    return pl.pallas_call(
        body, out_shape=(jax.ShapeDtypeStruct((s, d), F32), jax.ShapeDtypeStruct((1, LANE), F32)),
        grid=(s // tm,), in_specs=[row, row], out_specs=(row, pl.BlockSpec((1, LANE), lambda i: (0, 0))),
        name="loss_head", compiler_params=_cparams(("arbitrary",)))(y, target)


def _rope_tables(s):
    half = HEAD // 2
    pos = jnp.arange(s, dtype=F32)
    inv = ROPE_THETA ** (-jnp.arange(half, dtype=F32) * 2.0 / HEAD)
    ang = pos[:, None] * inv[None, :]
    cos, sin = jnp.cos(ang), jnp.sin(ang)
    cos = jnp.concatenate([cos, cos, cos, cos], axis=1)
    sin = jnp.concatenate([-sin, sin, -sin, sin], axis=1)
    return cos, sin


def _rotate(x, cos, sin):
    lane = lax.broadcasted_iota(jnp.int32, x.shape, 1)
    partner = jnp.where((lane % HEAD) < HEAD // 2, pltpu.roll(x, LANE - HEAD // 2, axis=1),
                        pltpu.roll(x, HEAD // 2, axis=1))
    return x * cos + partner * sin


def _class_rows(c, d, tm):
    return pl.ds(c, tm // d, stride=d) if d > 1 else pl.ds(0, tm)


def _dilated_spec(tm, d, w):
    return pl.BlockSpec((tm // d, d * w), lambda i: (i, 0))


def _token_scratch(tm, w):
    return pltpu.VMEM((w // LANE, tm, LANE), F32)


def _to_tokens(src_ref, dst3, d, tm):
    nj = dst3.shape[0]
    for cls in range(d):
        for j in range(nj):
            col = (cls * nj + j) * LANE
            dst3.at[j][_class_rows(cls, d, tm), :] = src_ref[:, col:col + LANE]


def _to_dilated(src3, dst_ref, d, tm):
    nj = src3.shape[0]
    for cls in range(d):
        for j in range(nj):
            col = (cls * nj + j) * LANE
            dst_ref[:, col:col + LANE] = src3.at[j][_class_rows(cls, d, tm), :].astype(dst_ref.dtype)


def _token_value(src3):
    return jnp.concatenate([src3[j] for j in range(src3.shape[0])], axis=1)


def _rope_fwd(proj, cos, sin, name):
    s = proj.shape[0]
    tm = 512
    w = 3 * ATTN_W
    nj = w // LANE

    def body(*refs):
        p_refs, (c_ref, s_ref), o_refs, rot = refs[:nj], refs[nj:nj + 2], refs[nj + 2:nj + 5], refs[nj + 5]
        c, sn = c_ref[...], s_ref[...]
        for j in range(nj):
            x = p_refs[j][...]
            rot[j] = _rotate(x, c, sn) if j < 2 * ATTN_W // LANE else x
        for d, o_ref in zip(DILATIONS, o_refs):
            _to_dilated(rot, o_ref, d, tm)

    tab = pl.BlockSpec((tm, LANE), lambda i: (i, 0))
    cols = [pl.BlockSpec((tm, LANE), functools.partial(lambda i, j: (i, j), j=j)) for j in range(nj)]
    return pl.pallas_call(
        body, out_shape=tuple(jax.ShapeDtypeStruct((s // d, d * w), BF16) for d in DILATIONS),
        grid=(s // tm,), in_specs=cols + [tab, tab],
        out_specs=tuple(_dilated_spec(tm, d, w) for d in DILATIONS),
        scratch_shapes=[_token_scratch(tm, w)], name=name,
        compiler_params=_cparams(("parallel",)))(*[proj] * nj, cos, sin)


def _dproj_assemble(dqkv_list, dxr, dgate, du, cos, sin, name):
    s = dxr.shape[0]
    tm = 512
    nq = 3 * ATTN_W // LANE

    def body(*refs):
        br = refs[:9]
        dxr_ref, dg_ref, du_ref, c_ref, s_ref, o_ref = refs[9:15]
        tok = refs[15:]
        c, sn = c_ref[...], -s_ref[...]
        for part in range(3):
            for b, d in enumerate(DILATIONS[1:], start=1):
                _to_tokens(br[3 * b + part], tok[2 * part + b - 1], d, tm)
        for j in range(nq):
            part, jj = divmod(j, ATTN_W // LANE)
            x = br[part][:, jj * LANE:(jj + 1) * LANE] + tok[2 * part][jj] + tok[2 * part + 1][jj]
            if part < 2:
                x = _rotate(x, c, sn)
            o_ref[:, j * LANE:(j + 1) * LANE] = x.astype(BF16)
        o_ref[:, 3 * ATTN_W:3 * ATTN_W + LRU_W] = dxr_ref[...].astype(BF16)
        o_ref[:, 3 * ATTN_W + LRU_W:3 * ATTN_W + 2 * LRU_W] = dg_ref[...].astype(BF16)
        o_ref[:, 3 * ATTN_W + 2 * LRU_W:] = du_ref[...].astype(BF16)

    a_spec = pl.BlockSpec((tm, ATTN_W), lambda i: (i, 0))
    tab = pl.BlockSpec((tm, LANE), lambda i: (i, 0))
    ordered = [dqkv_list[b][p] for b in range(3) for p in range(3)]
    d_specs = [_dilated_spec(tm, d, ATTN_W) for d in DILATIONS for _ in range(3)]
    return pl.pallas_call(
        body, out_shape=jax.ShapeDtypeStruct((s, D_IN), BF16), grid=(s // tm,),
        in_specs=d_specs + [a_spec, a_spec, pl.BlockSpec((tm, S5_W), lambda i: (i, 0)), tab, tab],
        out_specs=pl.BlockSpec((tm, D_IN), lambda i: (i, 0)),
        scratch_shapes=[_token_scratch(tm, ATTN_W)] * 6, name=name,
        compiler_params=_cparams(("parallel",)))(*ordered, dxr, dgate, du, cos, sin)


def _attn_tiles(s, d):
    m = s // d
    tq = min(m, ATTN_TILE)
    return m, tq, tq // ATTN_BLK


def _band_mask(qb):
    qi = lax.broadcasted_iota(jnp.int32, (ATTN_BLK, 2 * ATTN_BLK), 0)
    ki = lax.broadcasted_iota(jnp.int32, (ATTN_BLK, 2 * ATTN_BLK), 1)
    dist = qi + ATTN_BLK - ki
    return (dist >= 0) & (dist <= ATTN_BLK) & ((ki >= ATTN_BLK) | (qb > 0))


def _head_cols(h):
    return (slice(h * HEAD, (h + 1) * HEAD), slice(ATTN_W + h * HEAD, ATTN_W + (h + 1) * HEAD),
            slice(2 * ATTN_W + h * HEAD, 2 * ATTN_W + (h + 1) * HEAD))


def _attn_fwd(qv, d, name):
    m = qv.shape[0]
    w3 = 3 * ATTN_W
    _, tq, n = _attn_tiles(m * d, d)
    scale = HEAD ** -0.5

    def body(x_ref, p_ref, o_ref, l_ref):
        b = pl.program_id(1)

        def block(i, first):
            r0 = 0 if first else pl.multiple_of(i * ATTN_BLK, ATTN_BLK)
            rows = pl.ds(r0, ATTN_BLK)
            valid = _band_mask(b * n + i)
            if not first:
                krows = pl.ds(pl.multiple_of(i * ATTN_BLK - ATTN_BLK, ATTN_BLK), 2 * ATTN_BLK)
            for h in range(ATTN_W // HEAD):
                qs, ks, vs = _head_cols(h)
                q = x_ref[rows, qs]
                if first:
                    k = jnp.concatenate([p_ref[:, ks], x_ref[0:ATTN_BLK, ks]], axis=0)
                    v = jnp.concatenate([p_ref[:, vs], x_ref[0:ATTN_BLK, vs]], axis=0)
                else:
                    k = x_ref[krows, ks]
                    v = x_ref[krows, vs]
                sc = jnp.where(valid, _dot_nt(q, k) * scale, -1e30)
                mx = jnp.max(sc, axis=-1, keepdims=True)
                p = jnp.exp(sc - mx)
                l = jnp.sum(p, axis=-1, keepdims=True)
                o_ref[rows, qs] = _dot(p, v) / l
                l_ref[rows, qs] = jnp.broadcast_to(mx + jnp.log(l), (ATTN_BLK, HEAD))

        block(0, True)
        if n > 1:
            def loop(i, carry):
                block(i, False)
                return carry
            lax.fori_loop(1, n, loop, 0)

    shp = jax.ShapeDtypeStruct((m, d * ATTN_W), F32)
    ospec = pl.BlockSpec((tq, ATTN_W), lambda c, b: (b, c))
    out, lse = pl.pallas_call(
        body, out_shape=(shp, shp), grid=(d, m // tq),
        in_specs=[pl.BlockSpec((tq, w3), lambda c, b: (b, c)),
                  pl.BlockSpec((ATTN_BLK, w3), lambda c, b: (jnp.maximum(b * n - 1, 0), c))],
        out_specs=(ospec, ospec), name=name,
        compiler_params=_cparams(("parallel", "parallel")))(qv, qv)
    return out, lse


def _attn_bwd(qv, ov, dov, lv, d, name):
    m = qv.shape[0]
    w3 = 3 * ATTN_W
    _, tq, n = _attn_tiles(m * d, d)
    nb = m // ATTN_BLK
    scale = HEAD ** -0.5

    def body(x_ref, p_ref, nx_ref, o_ref, do_ref, l_ref, on_ref, don_ref, ln_ref, dq_ref, dk_ref, dv_ref):
        b = pl.program_id(1)
        dk_ref[...] = jnp.zeros_like(dk_ref)
        dv_ref[...] = jnp.zeros_like(dv_ref)

        def grads(q, k, v, o, do, lse, valid):
            sc = jnp.where(valid, _dot_nt(q, k) * scale, -1e30)
            p = jnp.exp(sc - lse)
            delta = jnp.sum(do * o, axis=-1, keepdims=True)
            return p, p * (_dot_nt(do, v) - delta) * scale

        def block(i, first):
            r0 = 0 if first else pl.multiple_of(i * ATTN_BLK, ATTN_BLK)
            rows = pl.ds(r0, ATTN_BLK)
            valid = _band_mask(b * n + i)
            if not first:
                krows = pl.ds(pl.multiple_of(i * ATTN_BLK - ATTN_BLK, ATTN_BLK), 2 * ATTN_BLK)
            for h in range(ATTN_W // HEAD):
                qs, ks, vs = _head_cols(h)
                q = x_ref[rows, qs]
                do = do_ref[rows, qs]
                if first:
                    k = jnp.concatenate([p_ref[:, ks], x_ref[0:ATTN_BLK, ks]], axis=0)
                    v = jnp.concatenate([p_ref[:, vs], x_ref[0:ATTN_BLK, vs]], axis=0)
                else:
                    k = x_ref[krows, ks]
                    v = x_ref[krows, vs]
                p, ds = grads(q, k, v, o_ref[rows, qs], do, l_ref[rows, qs][:, 0:1], valid)
                dq_ref[rows, qs] = _dot(ds, k)
                if first:
                    dk_ref[0:ATTN_BLK, qs] += _dot_tn(ds[:, ATTN_BLK:], q)
                    dv_ref[0:ATTN_BLK, qs] += _dot_tn(p[:, ATTN_BLK:], do)
                else:
                    dk_ref[krows, qs] += _dot_tn(ds, q)
                    dv_ref[krows, qs] += _dot_tn(p, do)

        block(0, True)
        if n > 1:
            def loop(i, carry):
                block(i, False)
                return carry
            lax.fori_loop(1, n, loop, 0)

        last = slice((n - 1) * ATTN_BLK, n * ATTN_BLK)
        qi = lax.broadcasted_iota(jnp.int32, (ATTN_BLK, ATTN_BLK), 0)
        ki = lax.broadcasted_iota(jnp.int32, (ATTN_BLK, ATTN_BLK), 1)
        valid_next = (qi <= ki) & ((b + 1) * n < nb)
        for h in range(ATTN_W // HEAD):
            qs, ks, vs = _head_cols(h)
            q = nx_ref[:, qs]
            do = don_ref[:, qs]
            p, ds = grads(q, x_ref[last, ks], x_ref[last, vs], on_ref[:, qs], do, ln_ref[:, qs][:, 0:1],
                          valid_next)
            dk_ref[last, qs] += _dot_tn(ds, q)
            dv_ref[last, qs] += _dot_tn(p, do)

    nxt = lambda b: jnp.minimum((b + 1) * n, nb - 1)
    xs = pl.BlockSpec((tq, w3), lambda c, b: (b, c))
    xp = pl.BlockSpec((ATTN_BLK, w3), lambda c, b: (jnp.maximum(b * n - 1, 0), c))
    xn = pl.BlockSpec((ATTN_BLK, w3), lambda c, b: (nxt(b), c))
    a = pl.BlockSpec((tq, ATTN_W), lambda c, b: (b, c))
    an = pl.BlockSpec((ATTN_BLK, ATTN_W), lambda c, b: (nxt(b), c))
    shp = jax.ShapeDtypeStruct((m, d * ATTN_W), F32)
    return pl.pallas_call(
        body, out_shape=(shp, shp, shp), grid=(d, m // tq),
        in_specs=[xs, xp, xn, a, a, a, an, an, an], out_specs=(a, a, a), name=name,
        compiler_params=_cparams(("parallel", "parallel")))(qv, qv, qv, ov, dov, lv, ov, dov, lv)


def _rms(x, g):
    ms = jnp.mean(x * x, axis=-1, keepdims=True)
    return x * lax.rsqrt(ms + RMS_EPS) * g


def _rms_bwd(x, g, dy):
    ms = jnp.mean(x * x, axis=-1, keepdims=True)
    r = lax.rsqrt(ms + RMS_EPS)
    dyg = dy * g
    dx = r * dyg - x * (r * r * r) * jnp.mean(x * dyg, axis=-1, keepdims=True)
    return dx, dy * x * r


def _mix_fwd(outs, lses, lru, s5, g, name):
    s = lru.shape[0]
    tm = 256

    def body(o1, o2, o3, l1, l2, l3, lru_ref, s5_ref, g_ref, mixed_ref, ov1, ov2, ov3, lv1, lv2, lv3,
             so2, so3, sl2, sl3):
        for d, src, dst in ((DILATIONS[1], o2, so2), (DILATIONS[2], o3, so3),
                            (DILATIONS[1], l2, sl2), (DILATIONS[2], l3, sl3)):
            _to_tokens(src, dst, d, tm)
        a1, a2, a3 = l1[...], _token_value(sl2), _token_value(sl3)
        mx = jnp.maximum(jnp.maximum(a1, a2), a3)
        e1, e2, e3 = jnp.exp(a1 - mx), jnp.exp(a2 - mx), jnp.exp(a3 - mx)
        den = e1 + e2 + e3
        o = (e1 * o1[...] + e2 * _token_value(so2) + e3 * _token_value(so3)) / den
        lse = mx + jnp.log(den)
        ov1[...] = o
        lv1[...] = lse
        for j in range(ATTN_W // LANE):
            so2[j] = o[:, j * LANE:(j + 1) * LANE]
            sl2[j] = lse[:, j * LANE:(j + 1) * LANE]
        for d, o_dst, l_dst in ((DILATIONS[1], ov2, lv2), (DILATIONS[2], ov3, lv3)):
            _to_dilated(so2, o_dst, d, tm)
            _to_dilated(sl2, l_dst, d, tm)
        gg = g_ref[...]
        mixed_ref[:, :ATTN_W] = _rms(o, gg[:, :ATTN_W]).astype(BF16)
        mixed_ref[:, ATTN_W:ATTN_W + LRU_W] = _rms(lru_ref[...], gg[:, ATTN_W:ATTN_W + LRU_W]).astype(BF16)
        mixed_ref[:, ATTN_W + LRU_W:] = _rms(s5_ref[...], gg[:, ATTN_W + LRU_W:]).astype(BF16)

    a = pl.BlockSpec((tm, ATTN_W), lambda i: (i, 0))
    s5s = pl.BlockSpec((tm, S5_W), lambda i: (i, 0))
    full = pl.BlockSpec((tm, D_MODEL), lambda i: (i, 0))
    vec = pl.BlockSpec((1, D_MODEL), lambda i: (0, 0))
    dil = [_dilated_spec(tm, d, ATTN_W) for d in DILATIONS]
    dshape = [jax.ShapeDtypeStruct((s // d, d * ATTN_W), F32) for d in DILATIONS]
    res = pl.pallas_call(
        body, out_shape=(jax.ShapeDtypeStruct((s, D_MODEL), BF16), *dshape, *dshape),
        grid=(s // tm,), in_specs=dil + dil + [a, s5s, vec], out_specs=(full, *dil, *dil),
        scratch_shapes=[_token_scratch(tm, ATTN_W)] * 4, name=name,
        compiler_params=_cparams(("parallel",)))(*outs, *lses, lru, s5, g)
    return res[0], res[1:4], res[4:7]


def _mix_bwd(dmixed, o, lru, s5, g, name):
    s = lru.shape[0]
    tm = 256

    def body(dm_ref, o_ref, lru_ref, s5_ref, g_ref, do_ref, do2_ref, do3_ref, dlru_ref, ds5_ref, dg_ref, stage):
        @pl.when(pl.program_id(0) == 0)
        def _():
            dg_ref[...] = jnp.zeros_like(dg_ref)
        gg = g_ref[...]
        dm = dm_ref[...]
        dx, dgr = _rms_bwd(o_ref[...], gg[:, :ATTN_W], dm[:, :ATTN_W])
        do_ref[...] = dx
        for j in range(ATTN_W // LANE):
            stage[j] = dx[:, j * LANE:(j + 1) * LANE]
        _to_dilated(stage, do2_ref, DILATIONS[1], tm)
        _to_dilated(stage, do3_ref, DILATIONS[2], tm)
        dg_ref[:, :ATTN_W] += jnp.sum(dgr, axis=0, keepdims=True)
        dx, dgr = _rms_bwd(lru_ref[...], gg[:, ATTN_W:ATTN_W + LRU_W], dm[:, ATTN_W:ATTN_W + LRU_W])
        dlru_ref[...] = dx
        dg_ref[:, ATTN_W:ATTN_W + LRU_W] += jnp.sum(dgr, axis=0, keepdims=True)
        dx, dgr = _rms_bwd(s5_ref[...], gg[:, ATTN_W + LRU_W:], dm[:, ATTN_W + LRU_W:])
        ds5_ref[...] = dx
        dg_ref[:, ATTN_W + LRU_W:] += jnp.sum(dgr, axis=0, keepdims=True)

    a = pl.BlockSpec((tm, ATTN_W), lambda i: (i, 0))
    s5s = pl.BlockSpec((tm, S5_W), lambda i: (i, 0))
    full = pl.BlockSpec((tm, D_MODEL), lambda i: (i, 0))
    vec = pl.BlockSpec((1, D_MODEL), lambda i: (0, 0))
    dil = [_dilated_spec(tm, d, ATTN_W) for d in DILATIONS]
    dshape = [jax.ShapeDtypeStruct((s // d, d * ATTN_W), F32) for d in DILATIONS]
    res = pl.pallas_call(
        body, out_shape=(*dshape, jax.ShapeDtypeStruct((s, LRU_W), F32),
                         jax.ShapeDtypeStruct((s, S5_W), F32), jax.ShapeDtypeStruct((1, D_MODEL), F32)),
        grid=(s // tm,), in_specs=[full, a, a, s5s, vec], out_specs=(*dil, a, s5s, vec),
        scratch_shapes=[_token_scratch(tm, ATTN_W)], name=name,
        compiler_params=_cparams(("arbitrary",)))(dmixed, o, lru, s5, g)
    return res[0:3], res[3], res[4], res[5]


def _lru_gate_math(xc, pre_r, pre_i, lam):
    r = _sigmoid(pre_r)
    i = _sigmoid(pre_i)
    log_a = -LRU_C * r * _softplus(-lam)
    a = jnp.exp(log_a)
    u = jnp.sqrt(-_expm1(2.0 * log_a)) * (i * xc)
    return a, u


def _lru_conv(x, prev8, cw, cb):
    y = cb + cw[LRU_CONV - 1:LRU_CONV, :] * x
    for k in range(LRU_CONV - 1):
        y = y + cw[k:k + 1, :] * _shift_down_prev(x, LRU_CONV - 1 - k, prev8)
    return y


def _lru_specs(s):
    xo = 3 * ATTN_W // LANE
    go = xo + LRU_W // LANE
    xr = pl.BlockSpec((s, LANE), lambda j: (0, xo + j))
    gt = pl.BlockSpec((s, LANE), lambda j: (0, go + j))
    cw = pl.BlockSpec((LRU_CONV, LANE), lambda j: (0, j))
    vec = pl.BlockSpec((1, LANE), lambda j: (0, j))
    wbd = pl.BlockSpec((LANE, LANE), lambda j: (j, j))
    col = pl.BlockSpec((s, LANE), lambda j: (0, j))
    return xr, gt, cw, vec, wbd, col


def _lru_fwd(proj, cw, cb, wr, br, wi, bi, lam, name):
    s = proj.shape[0]
    t = SCAN_T

    def body(xr_ref, gt_ref, cw_ref, cb_ref, wr_ref, br_ref, wi_ref, bi_ref, lam_ref, o_ref):
        cwv, cbv, lamv = cw_ref[...], cb_ref[...], lam_ref[...]
        wrv, wiv, brv, biv = wr_ref[...], wi_ref[...], br_ref[...], bi_ref[...]

        def chunk(c, carry):
            h_c, prev8 = carry
            rows = pl.ds(pl.multiple_of(c * t, t), t)
            x = xr_ref[rows, :]
            xc = _lru_conv(x, prev8, cwv, cbv)
            a, u = _lru_gate_math(xc, _dot(xc, wrv) + brv, _dot(xc, wiv) + biv, lamv)
            h = _scan_chunk(a, u, h_c)
            o_ref[rows, :] = h * _gelu(gt_ref[rows, :])
            return h[t - 1:t, :], x[t - 8:t, :]

        lax.fori_loop(0, s // t, chunk, (jnp.zeros((1, LANE), F32), jnp.zeros((8, LANE), F32)))

    xr, gt, cws, vec, wbd, col = _lru_specs(s)
    return pl.pallas_call(
        body, out_shape=jax.ShapeDtypeStruct((s, LRU_W), F32), grid=(LRU_W // LANE,),
        in_specs=[xr, gt, cws, vec, wbd, vec, wbd, vec, vec], out_specs=col, name=name,
        compiler_params=_cparams(("parallel",)))(proj, proj, cw, cb, wr, br, wi, bi, lam)


def _lru_bwd(proj, dout, cw, cb, wr, br, wi, bi, lam, name):
    s = proj.shape[0]
    t = SCAN_T
    nc = s // t

    def body(xr_ref, gt_ref, do_ref, cw_ref, cb_ref, wr_ref, br_ref, wi_ref, bi_ref, lam_ref,
             dxr_ref, dgt_ref, dcw_ref, dcb_ref, dwr_ref, dbr_ref, dwi_ref, dbi_ref, dlam_ref,
             xc_s, a_s, h_s):
        cwv, cbv, lamv = cw_ref[...], cb_ref[...], lam_ref[...]
        wrv, wiv, brv, biv = wr_ref[...], wi_ref[...], br_ref[...], bi_ref[...]

        def fchunk(c, carry):
            h_c, prev8 = carry
            rows = pl.ds(pl.multiple_of(c * t, t), t)
            x = xr_ref[rows, :]
            xc = _lru_conv(x, prev8, cwv, cbv)
            a, u = _lru_gate_math(xc, _dot(xc, wrv) + brv, _dot(xc, wiv) + biv, lamv)
            h = _scan_chunk(a, u, h_c)
            xc_s[rows, :] = xc
            a_s[rows, :] = a
            h_s[rows, :] = h
            return h[t - 1:t, :], x[t - 8:t, :]

        lax.fori_loop(0, nc, fchunk, (jnp.zeros((1, LANE), F32), jnp.zeros((8, LANE), F32)))

        z1 = jnp.zeros((1, LANE), F32)
        zw = jnp.zeros((LANE, LANE), F32)

        def bchunk(ci, carry):
            g_next, a_next, dxc_next8, dcw, dcb, dwr, dbr, dwi, dbi, dlam = carry
            c = nc - 1 - ci
            t0 = pl.multiple_of(c * t, t)
            rows = pl.ds(t0, t)
            before = pl.ds(pl.multiple_of(jnp.maximum(t0 - 8, 0), 8), 8)
            has_prev = (c > 0).astype(F32)
            x, gt, do = xr_ref[rows, :], gt_ref[rows, :], do_ref[rows, :]
            xc, a, h = xc_s[rows, :], a_s[rows, :], h_s[rows, :]
            prev8_x = xr_ref[before, :] * has_prev
            prev8_h = h_s[before, :] * has_prev
            dgt_ref[rows, :] = do * h * _gelu_grad(gt)
            dh = do * _gelu(gt)
            a_plus = _shift_up_next(a, 1, jnp.broadcast_to(a_next, (8, LANE)))
            g = _scan_chunk(a_plus, dh, g_next, reverse=True)
            da = g * _shift_down_prev(h, 1, prev8_h)
            pre_r = _dot(xc, wrv) + brv
            pre_i = _dot(xc, wiv) + biv
            _, vjp = jax.vjp(_lru_gate_math, xc, pre_r, pre_i, lamv)
            dxc, dpre_r, dpre_i, dlam_c = vjp((da, g))
            dxc = dxc + _dot_nt(dpre_r, wrv) + _dot_nt(dpre_i, wiv)
            dx = cwv[LRU_CONV - 1:LRU_CONV, :] * dxc
            dcw_rows = [None] * LRU_CONV
            dcw_rows[LRU_CONV - 1] = jnp.sum(dxc * x, axis=0, keepdims=True)
            for k in range(LRU_CONV - 1):
                sh = LRU_CONV - 1 - k
                dx = dx + cwv[k:k + 1, :] * _shift_up_next(dxc, sh, dxc_next8)
                dcw_rows[k] = jnp.sum(dxc * _shift_down_prev(x, sh, prev8_x), axis=0, keepdims=True)
            dxr_ref[rows, :] = dx
            return (g[0:1, :], a[0:1, :], dxc[0:8, :],
                    dcw + jnp.concatenate(dcw_rows, axis=0),
                    dcb + jnp.sum(dxc, axis=0, keepdims=True),
                    dwr + _dot_tn(xc, dpre_r), dbr + jnp.sum(dpre_r, axis=0, keepdims=True),
                    dwi + _dot_tn(xc, dpre_i), dbi + jnp.sum(dpre_i, axis=0, keepdims=True),
                    dlam + dlam_c)

        init = (z1, z1, jnp.zeros((8, LANE), F32), jnp.zeros((LRU_CONV, LANE), F32), z1, zw, z1, zw, z1, z1)
        res = lax.fori_loop(0, nc, bchunk, init)
        dcw_ref[...] = res[3]
        dcb_ref[...] = res[4]
        dwr_ref[...] = res[5]
        dbr_ref[...] = res[6]
        dwi_ref[...] = res[7]
        dbi_ref[...] = res[8]
        dlam_ref[...] = res[9]

    xr, gt, cws, vec, wbd, col = _lru_specs(s)
    vshape = jax.ShapeDtypeStruct((1, LRU_W), F32)
    wshape = jax.ShapeDtypeStruct((LRU_W, LRU_W), F32)
    return pl.pallas_call(
        body,
        out_shape=(jax.ShapeDtypeStruct((s, LRU_W), F32), jax.ShapeDtypeStruct((s, LRU_W), F32),
                   jax.ShapeDtypeStruct((LRU_CONV, LRU_W), F32), vshape, wshape, vshape, wshape, vshape, vshape),
        grid=(LRU_W // LANE,),
        in_specs=[xr, gt, col, cws, vec, wbd, vec, wbd, vec, vec],
        out_specs=(col, col, cws, vec, wbd, vec, wbd, vec, vec),
        scratch_shapes=[pltpu.VMEM((s, LANE), F32)] * 3, name=name,
        compiler_params=_cparams(("parallel",)))(proj, proj, dout, cw, cb, wr, br, wi, bi, lam)


def _s5_disc_math(a_re, a_im, log_step, bt_re, bt_im):
    step = jnp.exp(log_step)
    dt_re, dt_im = step * a_re, step * a_im
    mag = jnp.exp(dt_re)
    ab_re, ab_im = mag * jnp.cos(dt_im), mag * jnp.sin(dt_im)
    z_re, z_im = ab_re - 1.0, ab_im
    den = a_re * a_re + a_im * a_im
    f_re = (z_re * a_re + z_im * a_im) / den
    f_im = (z_im * a_re - z_re * a_im) / den
    bb_re = f_re * bt_re - f_im * bt_im
    bb_im = f_re * bt_im + f_im * bt_re
    return ab_re, ab_im, bb_re, bb_im


def _s5_disc_fwd(a_re, a_im, log_step, bt_re, bt_im, name):
    def body(ar, ai, ls, br, bi, o1, o2, o3, o4):
        r = _s5_disc_math(ar[...], ai[...], ls[...], br[...], bi[...])
        o1[...], o2[...], o3[...], o4[...] = r

    shp = jax.ShapeDtypeStruct(a_re.shape, F32)
    return pl.pallas_call(body, out_shape=(shp,) * 4, name=name)(a_re, a_im, log_step, bt_re, bt_im)


def _s5_disc_bwd(a_re, a_im, log_step, bt_re, bt_im, cts, name):
    def body(ar, ai, ls, br, bi, c1, c2, c3, c4, o1, o2, o3, o4, o5):
        _, vjp = jax.vjp(_s5_disc_math, ar[...], ai[...], ls[...], br[...], bi[...])
        r = vjp((c1[...], c2[...], c3[...], c4[...]))
        o1[...], o2[...], o3[...], o4[...], o5[...] = r

    shp = jax.ShapeDtypeStruct(a_re.shape, F32)
    return pl.pallas_call(body, out_shape=(shp,) * 5, name=name)(a_re, a_im, log_step, bt_re, bt_im, *cts)


def _s5_u_specs(s):
    uo = (3 * ATTN_W + 2 * LRU_W) // LANE
    return (pl.BlockSpec((s, LANE), lambda j: (0, uo)), pl.BlockSpec((s, LANE), lambda j: (0, uo + 1)))


def _s5_scan_fwd(proj, b_re, b_im, lam_re, lam_im, c_re, c_im, name):
    s = proj.shape[0]
    t = SCAN_T

    def body(u0_ref, u1_ref, bre_ref, bim_ref, lre_ref, lim_ref, cre_ref, cim_ref, xre_ref, xim_ref, y_ref):
        @pl.when(pl.program_id(0) == 0)
        def _():
            y_ref[...] = jnp.zeros_like(y_ref)
        lr, li = lre_ref[...], lim_ref[...]
        consts = _cscan_consts(lr, li, False)
        bre, bim, cre, cim = bre_ref[...], bim_ref[...], cre_ref[...], cim_ref[...]

        def chunk(c, carry):
            cr, ci = carry
            rows = pl.ds(pl.multiple_of(c * t, t), t)
            u = jnp.concatenate([u0_ref[rows, :], u1_ref[rows, :]], axis=1).astype(BF16)
            xr, xi = _cscan_chunk(_dot(u, bre), _dot(u, bim), consts, (cr, ci))
            xre_ref[rows, :] = xr
            xim_ref[rows, :] = xi
            y_ref[rows, :] += _dot(xr, cre) - _dot(xi, cim)
            return xr[t - 1:t, :], xi[t - 1:t, :]

        z = jnp.zeros((1, S5_BLK), F32)
        lax.fori_loop(0, s // t, chunk, (z, z))

    u0, u1 = _s5_u_specs(s)
    bsp = pl.BlockSpec((S5_W, S5_BLK), lambda j: (0, j))
    csp = pl.BlockSpec((S5_BLK, S5_W), lambda j: (j, 0))
    vec = pl.BlockSpec((1, S5_BLK), lambda j: (0, j))
    xsp = pl.BlockSpec((s, S5_BLK), lambda j: (0, j))
    ysp = pl.BlockSpec((s, S5_W), lambda j: (0, 0))
    xshape = jax.ShapeDtypeStruct((s, S5_STATES), F32)
    return pl.pallas_call(
        body, out_shape=(xshape, xshape, jax.ShapeDtypeStruct((s, S5_W), F32)),
        grid=(S5_STATES // S5_BLK,), in_specs=[u0, u1, bsp, bsp, vec, vec, csp, csp],
        out_specs=(xsp, xsp, ysp), name=name,
        compiler_params=_cparams(("arbitrary",)))(proj, proj, b_re, b_im, lam_re, lam_im, c_re, c_im)


def _s5_scan_bwd(proj, dy, du_init, x_re, x_im, b_re, b_im, lam_re, lam_im, c_re, c_im, name):
    s = proj.shape[0]
    t = SCAN_T
    nc = s // t

    def body(u0_ref, u1_ref, dy_ref, dui_ref, xre_ref, xim_ref, bre_ref, bim_ref, lre_ref, lim_ref,
             cre_ref, cim_ref, du_ref, dlr_ref, dli_ref, dbr_ref, dbi_ref, dcr_ref, dci_ref):
        @pl.when(pl.program_id(0) == 0)
        def _():
            du_ref[...] = dui_ref[...]
        mr, mi = lre_ref[...], -lim_ref[...]
        consts = _cscan_consts(mr, mi, True)
        bre, bim, cre, cim = bre_ref[...], bim_ref[...], cre_ref[...], cim_ref[...]
        dbr_ref[...] = jnp.zeros_like(dbr_ref)
        dbi_ref[...] = jnp.zeros_like(dbi_ref)
        dcr_ref[...] = jnp.zeros_like(dcr_ref)
        dci_ref[...] = jnp.zeros_like(dci_ref)

        def chunk(ci_, carry):
            gnr, gni, dlr, dli = carry
            c = nc - 1 - ci_
            t0 = pl.multiple_of(c * t, t)
            rows = pl.ds(t0, t)
            before = pl.ds(pl.multiple_of(jnp.maximum(t0 - 8, 0), 8), 8)
            has_prev = (c > 0).astype(F32)
            dyc = dy_ref[rows, :].astype(BF16)
            u = jnp.concatenate([u0_ref[rows, :], u1_ref[rows, :]], axis=1).astype(BF16)
            gr, gi = _cscan_chunk(_dot_nt(dyc, cre), -_dot_nt(dyc, cim), consts, (gnr, gni), reverse=True)
            xr, xi = xre_ref[rows, :], xim_ref[rows, :]
            xpr = _shift_down_prev(xr, 1, xre_ref[before, :] * has_prev)
            xpi = _shift_down_prev(xi, 1, xim_ref[before, :] * has_prev)
            dlr = dlr + jnp.sum(gr * xpr + gi * xpi, axis=0, keepdims=True)
            dli = dli + jnp.sum(gi * xpr - gr * xpi, axis=0, keepdims=True)
            du_ref[rows, :] += _dot_nt(gr, bre) + _dot_nt(gi, bim)
            dbr_ref[...] += _dot_tn(u, gr)
            dbi_ref[...] += _dot_tn(u, gi)
            dcr_ref[...] += _dot_tn(xr, dyc)
            dci_ref[...] -= _dot_tn(xi, dyc)
            return gr[0:1, :], gi[0:1, :], dlr, dli

        z = jnp.zeros((1, S5_BLK), F32)
        res = lax.fori_loop(0, nc, chunk, (z, z, z, z))
        dlr_ref[...] = res[2]
        dli_ref[...] = res[3]

    u0, u1 = _s5_u_specs(s)
    bsp = pl.BlockSpec((S5_W, S5_BLK), lambda j: (0, j))
    csp = pl.BlockSpec((S5_BLK, S5_W), lambda j: (j, 0))
    vec = pl.BlockSpec((1, S5_BLK), lambda j: (0, j))
    xsp = pl.BlockSpec((s, S5_BLK), lambda j: (0, j))
    ysp = pl.BlockSpec((s, S5_W), lambda j: (0, 0))
    return pl.pallas_call(
        body,
        out_shape=(jax.ShapeDtypeStruct((s, S5_W), F32),
                   jax.ShapeDtypeStruct((1, S5_STATES), F32), jax.ShapeDtypeStruct((1, S5_STATES), F32),
                   jax.ShapeDtypeStruct((S5_W, S5_STATES), F32), jax.ShapeDtypeStruct((S5_W, S5_STATES), F32),
                   jax.ShapeDtypeStruct((S5_STATES, S5_W), F32), jax.ShapeDtypeStruct((S5_STATES, S5_W), F32)),
        grid=(S5_STATES // S5_BLK,),
        in_specs=[u0, u1, ysp, ysp, xsp, xsp, bsp, bsp, vec, vec, csp, csp],
        out_specs=(ysp, vec, vec, bsp, bsp, csp, csp), name=name,
        compiler_params=_cparams(("arbitrary",)))(
            proj, proj, dy, du_init, x_re, x_im, b_re, b_im, lam_re, lam_im, c_re, c_im)


def _s5_out_fwd(proj, y_acc, dvec, w_glu, b_glu, name):
    s = proj.shape[0]
    tm = 512
    uo = (3 * ATTN_W + 2 * LRU_W) // LANE

    def body(u0_ref, u1_ref, y_ref, d_ref, w_ref, b_ref, o_ref, yp_ref):
        u = jnp.concatenate([u0_ref[...], u1_ref[...]], axis=1)
        y = y_ref[...] + d_ref[...] * u
        yp_ref[...] = y
        yg = _gelu(y)
        o_ref[...] = yg * _sigmoid(_dot(yg, w_ref[...]) + b_ref[...])

    u0 = pl.BlockSpec((tm, LANE), lambda i: (i, uo))
    u1 = pl.BlockSpec((tm, LANE), lambda i: (i, uo + 1))
    row = pl.BlockSpec((tm, S5_W), lambda i: (i, 0))
    vec = pl.BlockSpec((1, S5_W), lambda i: (0, 0))
    wsp = pl.BlockSpec((S5_W, S5_W), lambda i: (0, 0))
    shp = jax.ShapeDtypeStruct((s, S5_W), F32)
    return pl.pallas_call(
        body, out_shape=(shp, shp), grid=(s // tm,), in_specs=[u0, u1, row, vec, wsp, vec],
        out_specs=(row, row), name=name,
        compiler_params=_cparams(("parallel",)))(proj, proj, y_acc, dvec, w_glu, b_glu)


def _s5_out_bwd(proj, y_pre, dout, dvec, w_glu, b_glu, name):
    s = proj.shape[0]
    tm = 512
    uo = (3 * ATTN_W + 2 * LRU_W) // LANE

    def body(u0_ref, u1_ref, y_ref, do_ref, d_ref, w_ref, b_ref, dy_ref, dud_ref, dd_ref, dw_ref, db_ref):
        @pl.when(pl.program_id(0) == 0)
        def _():
            dd_ref[...] = jnp.zeros_like(dd_ref)
            dw_ref[...] = jnp.zeros_like(dw_ref)
            db_ref[...] = jnp.zeros_like(db_ref)
        u = jnp.concatenate([u0_ref[...], u1_ref[...]], axis=1)
        y = y_ref[...]
        do = do_ref[...]
        yg = _gelu(y)
        sg = _sigmoid(_dot(yg, w_ref[...]) + b_ref[...])
        dz = do * yg * sg * (1.0 - sg)
        dyg = do * sg + _dot_nt(dz, w_ref[...])
        dy = dyg * _gelu_grad(y)
        dy_ref[...] = dy
        dud_ref[...] = d_ref[...] * dy
        dd_ref[...] += jnp.sum(dy * u, axis=0, keepdims=True)
        dw_ref[...] += _dot_tn(yg, dz)
        db_ref[...] += jnp.sum(dz, axis=0, keepdims=True)

    u0 = pl.BlockSpec((tm, LANE), lambda i: (i, uo))
    u1 = pl.BlockSpec((tm, LANE), lambda i: (i, uo + 1))
    row = pl.BlockSpec((tm, S5_W), lambda i: (i, 0))
    vec = pl.BlockSpec((1, S5_W), lambda i: (0, 0))
    wsp = pl.BlockSpec((S5_W, S5_W), lambda i: (0, 0))
    shp = jax.ShapeDtypeStruct((s, S5_W), F32)
    vshape = jax.ShapeDtypeStruct((1, S5_W), F32)
    return pl.pallas_call(
        body, out_shape=(shp, shp, vshape, jax.ShapeDtypeStruct((S5_W, S5_W), F32), vshape),
        grid=(s // tm,), in_specs=[u0, u1, row, row, vec, wsp, vec],
        out_specs=(row, row, vec, wsp, vec), name=name,
        compiler_params=_cparams(("arbitrary",)))(proj, proj, y_pre, dout, dvec, w_glu, b_glu)


def _ffn_conv(x, prev8, cw, cb):
    y = cb + cw[FFN_CONV - 1:FFN_CONV, :] * x
    for k in range(FFN_CONV - 1):
        y = y + cw[k:k + 1, :] * _shift_down_prev(x, FFN_CONV - 1 - k, prev8)
    return y


def _ffn_act_fwd(up, cw, cb, name):
    s = up.shape[0]
    tm = 256
    tb = 2 * FFN_CB

    def body(x_ref, p_ref, cw_ref, cb_ref, o_ref):
        prev8 = p_ref[...] * (pl.program_id(1) > 0).astype(F32)
        y = _ffn_conv(x_ref[...], prev8, cw_ref[...], cb_ref[...])
        o_ref[...] = (_gelu(y[:, :FFN_CB]) * y[:, FFN_CB:]).astype(BF16)

    main = pl.BlockSpec((tm, tb), lambda j, i: (i, j))
    prev = pl.BlockSpec((8, tb), lambda j, i: (jnp.maximum(i * (tm // 8) - 1, 0), j))
    return pl.pallas_call(
        body, out_shape=jax.ShapeDtypeStruct((s, D_FF), BF16), grid=(D_FF // FFN_CB, s // tm),
        in_specs=[main, prev, pl.BlockSpec((FFN_CONV, tb), lambda j, i: (0, j)),
                  pl.BlockSpec((1, tb), lambda j, i: (0, j))],
        out_specs=pl.BlockSpec((tm, FFN_CB), lambda j, i: (i, j)), name=name,
        compiler_params=_cparams(("parallel", "parallel")))(up, up, cw, cb)


def _ffn_act_bwd(up, dact, cw, cb, name):
    s = up.shape[0]
    tm = 256
    tb = 2 * FFN_CB
    nr = s // tm

    def body(x_ref, p_ref, n_ref, da_ref, dan_ref, cw_ref, cb_ref, dup_ref, dcw_ref, dcb_ref):
        i = pl.program_id(1)

        @pl.when(i == 0)
        def _():
            dcw_ref[...] = jnp.zeros_like(dcw_ref)
            dcb_ref[...] = jnp.zeros_like(dcb_ref)
        has_next = (i < nr - 1).astype(F32)
        prev8 = p_ref[...] * (i > 0).astype(F32)
        cwv = cw_ref[...]
        x = x_ref[...]
        xe = jnp.concatenate([x, n_ref[...]], axis=0)
        dae = jnp.concatenate([da_ref[...], dan_ref[...] * has_next], axis=0)
        y = _ffn_conv(xe, prev8, cwv, cb_ref[...])
        gate, val = y[:, :FFN_CB], y[:, FFN_CB:]
        dy = jnp.concatenate([dae * val * _gelu_grad(gate), dae * _gelu(gate)], axis=1)
        dym = dy[:tm, :]
        dx = cwv[FFN_CONV - 1:FFN_CONV, :] * dym
        dcw_rows = [None] * FFN_CONV
        dcw_rows[FFN_CONV - 1] = jnp.sum(dym * x, axis=0, keepdims=True)
        for k in range(FFN_CONV - 1):
            sh = FFN_CONV - 1 - k
            dx = dx + cwv[k:k + 1, :] * pltpu.roll(dy, tm + 8 - sh, axis=0)[:tm, :]
            dcw_rows[k] = jnp.sum(dym * _shift_down_prev(x, sh, prev8), axis=0, keepdims=True)
        dup_ref[...] = dx.astype(BF16)
        dcw_ref[...] += jnp.concatenate(dcw_rows, axis=0)
        dcb_ref[...] += jnp.sum(dym, axis=0, keepdims=True)

    main = pl.BlockSpec((tm, tb), lambda j, i: (i, j))
    prev = pl.BlockSpec((8, tb), lambda j, i: (jnp.maximum(i * (tm // 8) - 1, 0), j))
    nxt = pl.BlockSpec((8, tb), lambda j, i: (jnp.minimum((i + 1) * (tm // 8), s // 8 - 1), j))
    da = pl.BlockSpec((tm, FFN_CB), lambda j, i: (i, j))
    dan = pl.BlockSpec((8, FFN_CB), lambda j, i: (jnp.minimum((i + 1) * (tm // 8), s // 8 - 1), j))
    cws = pl.BlockSpec((FFN_CONV, tb), lambda j, i: (0, j))
    cbs = pl.BlockSpec((1, tb), lambda j, i: (0, j))
    return pl.pallas_call(
        body, out_shape=(jax.ShapeDtypeStruct((s, 2 * D_FF), BF16),
                         jax.ShapeDtypeStruct((FFN_CONV, 2 * D_FF), F32),
                         jax.ShapeDtypeStruct((1, 2 * D_FF), F32)),
        grid=(D_FF // FFN_CB, nr), in_specs=[main, prev, nxt, da, dan, cws, cbs],
        out_specs=(main, cws, cbs), name=name,
        compiler_params=_cparams(("parallel", "arbitrary")))(up, up, up, dact, dact, cw, cb)


def _adamw_sum(landed, w, m, v, layer, prev, name):
    _, r, c = landed.shape
    nl = w.shape[0]
    tm = 8
    for cand in (512, 256, 128, 64, 32, 16):
        if r % cand == 0 and N_DEV * cand * c * 4 <= 4 * 1024 * 1024:
            tm = cand
            break

    def body(*refs):
        ld_ref, w_ref, m_ref, v_ref = refs[:4]
        g_ref, d_ref, mo_ref, vo_ref = refs[-4:]
        gg = ld_ref[0]
        for k in range(1, N_DEV):
            gg = gg + ld_ref[k]
        mn = ADAM_B1 * m_ref[...] + (1.0 - ADAM_B1) * gg
        vn = ADAM_B2 * v_ref[...] + (1.0 - ADAM_B2) * (gg * gg)
        m_hat = mn / (1.0 - ADAM_B1 ** ADAM_STEP)
        v_hat = vn / (1.0 - ADAM_B2 ** ADAM_STEP)
        g_ref[...] = gg
        d_ref[...] = -ADAM_LR * (m_hat / (jnp.sqrt(v_hat) + ADAM_EPS) + ADAM_WD * w_ref[...])
        mo_ref[...] = mn
        vo_ref[...] = vn

    blk = pl.BlockSpec((None, tm, c), lambda i: (layer, i, 0))
    in_specs = [pl.BlockSpec((N_DEV, tm, c), lambda i: (0, i, 0)), blk, blk, blk]
    args = [landed, w, m, v]
    aliases = {}
    if prev is not None:
        in_specs += [pl.BlockSpec(memory_space=pl.ANY)] * 4
        args += list(prev)
        aliases = {4 + k: k for k in range(4)}
    shp = jax.ShapeDtypeStruct((nl, r, c), F32)
    return pl.pallas_call(
        body, out_shape=(shp,) * 4, grid=(r // tm,), in_specs=in_specs, out_specs=(blk,) * 4,
        input_output_aliases=aliases, name=name, compiler_params=_cparams(("parallel",)))(*args)


def _all_gather(shards, name):
    na = len(shards)

    def body(*refs):
        x_refs, out_refs = refs[:na], refs[na:2 * na]
        send_sems, recv_sems, local_sems = refs[2 * na:]
        x, y, c = lax.axis_index("x"), lax.axis_index("y"), lax.axis_index("c")
        me, sibling = (x, y, c), (x, y, 1 - c)
        chips = [(1 - x, y), (x, 1 - y), (1 - x, 1 - y)]

        def copy(a, k, block, to, src=None):
            dst = out_refs[a].at[4 * block[0] + 2 * block[1] + block[2]]
            return pltpu.make_async_remote_copy(
                src_ref=dst if src is None else src, dst_ref=dst,
                send_sem=send_sems.at[7 * a + k], recv_sem=recv_sems.at[7 * a + k],
                device_id=to, device_id_type=pl.DeviceIdType.MESH)

        mine, first, passed = [], [], []
        for a in range(na):
            cp = pltpu.make_async_copy(x_refs[a], out_refs[a].at[4 * x + 2 * y + c], local_sems.at[a])
            cp.start()
            mine.append(cp)
            cps = [copy(a, 0, me, sibling, src=x_refs[a])]
            cps += [copy(a, 1 + j, me, (*chip, c), src=x_refs[a]) for j, chip in enumerate(chips)]
            for cp in cps:
                cp.start()
            first += cps
        for j, chip in enumerate(chips):
            for a in range(na):
                copy(a, 1 + j, (*chip, c), me).wait_recv()
                cp = copy(a, 4 + j, (*chip, c), sibling)
                cp.start()
                passed.append(cp)
        for a in range(na):
            copy(a, 0, sibling, me).wait_recv()
            for j, chip in enumerate(chips):
                copy(a, 4 + j, (*chip, 1 - c), me).wait_recv()
        for cp in first + passed:
            cp.wait_send()
        for cp in mine:
            cp.wait()

    anyspec = pl.BlockSpec(memory_space=pl.ANY)
    return pl.pallas_call(
        body, out_shape=tuple(jax.ShapeDtypeStruct((N_DEV,) + t.shape, t.dtype) for t in shards),
        in_specs=[anyspec] * na, out_specs=tuple([anyspec] * na),
        scratch_shapes=[pltpu.SemaphoreType.DMA((7 * na,)), pltpu.SemaphoreType.DMA((7 * na,)),
                        pltpu.SemaphoreType.DMA((na,))],
        name=name)(*shards)


def _all_to_all(bufs, name):
    na = len(bufs)

    def body(*refs):
        b_refs, out_refs = refs[:na], refs[na:2 * na]
        send_sems, recv_sems, local_sems = refs[2 * na:]
        x, y, c = lax.axis_index("x"), lax.axis_index("y"), lax.axis_index("c")
        me = 4 * x + 2 * y + c
        copies = []
        for a in range(na):
            cp = pltpu.make_async_copy(b_refs[a].at[me], out_refs[a].at[me], local_sems.at[a])
            cp.start()
            copies.append(cp)
        for k in range(1, N_DEV):
            px = x ^ ((k >> 2) & 1)
            py = y ^ ((k >> 1) & 1)
            pc = c ^ (k & 1)
            for a in range(na):
                cp = pltpu.make_async_remote_copy(
                    src_ref=b_refs[a].at[4 * px + 2 * py + pc], dst_ref=out_refs[a].at[me],
                    send_sem=send_sems.at[7 * a + k - 1], recv_sem=recv_sems.at[7 * a + k - 1],
                    device_id=(px, py, pc), device_id_type=pl.DeviceIdType.MESH)
                cp.start()
                copies.append(cp)
        for cp in copies:
            cp.wait()

    anyspec = pl.BlockSpec(memory_space=pl.ANY)
    return pl.pallas_call(
        body, out_shape=tuple(jax.ShapeDtypeStruct(t.shape, t.dtype) for t in bufs),
        in_specs=[anyspec] * na, out_specs=tuple([anyspec] * na),
        scratch_shapes=[pltpu.SemaphoreType.DMA((7 * na,)), pltpu.SemaphoreType.DMA((7 * na,)),
                        pltpu.SemaphoreType.DMA((na,))],
        name=name)(*bufs)


_HBM = pl.BlockSpec(memory_space=pltpu.HBM)
_SEM = pl.BlockSpec(memory_space=pltpu.SEMAPHORE)
_EFFECT = pltpu.SideEffectType.DATAFLOW_SIDE_EFFECTING


def _exchange_copies(src_refs, land_refs, send_sems, recv_sems, local_sems, gather):
    x, y, c = lax.axis_index("x"), lax.axis_index("y"), lax.axis_index("c")
    me = 4 * x + 2 * y + c
    local, remote = [], []
    for a, (src, land) in enumerate(zip(src_refs, land_refs)):
        local.append(pltpu.make_async_copy(src if gather else src.at[me], land.at[me], local_sems.at[a]))
    for k in range(1, N_DEV):
        px = x ^ ((k >> 2) & 1)
        py = y ^ ((k >> 1) & 1)
        pc = c ^ (k & 1)
        for a, (src, land) in enumerate(zip(src_refs, land_refs)):
            remote.append(pltpu.make_async_remote_copy(
                src_ref=src if gather else src.at[4 * px + 2 * py + pc], dst_ref=land.at[me],
                send_sem=send_sems.at[7 * a + k - 1], recv_sem=recv_sems.at[7 * a + k - 1],
                device_id=(px, py, pc), device_id_type=pl.DeviceIdType.MESH))
    return local, remote


def _exchange_start(srcs, gather, name, dep=None):
    na = len(srcs)
    lands = [lax.empty(((N_DEV,) + t.shape) if gather else t.shape, t.dtype) for t in srcs]

    def body(*refs):
        src_refs, land_refs = refs[:na], refs[na:2 * na]
        nin = 2 * na + (0 if dep is None else 1)
        send_sems, recv_sems, local_sems = refs[nin:nin + 3]
        token = refs[-1]
        local, remote = _exchange_copies(src_refs, land_refs, send_sems, recv_sems, local_sems, gather)
        for cp in local + remote:
            cp.start()
        token[...] = jnp.zeros_like(token)

    dep_specs, dep_ops = _dep_args(dep)
    hbm = lambda t: pltpu.HBM(t.shape, t.dtype)
    out = pl.pallas_call(
        body, name=name,
        out_shape=(pltpu.SemaphoreType.DMA((7 * na,)), pltpu.SemaphoreType.DMA((7 * na,)),
                   pltpu.SemaphoreType.DMA((na,)), *[hbm(t) for t in srcs], *[hbm(t) for t in lands],
                   jax.ShapeDtypeStruct((8, LANE), F32)),
        in_specs=[_HBM] * (2 * na) + dep_specs,
        out_specs=(_SEM, _SEM, _SEM, *[_HBM] * (2 * na), pl.BlockSpec(memory_space=pltpu.VMEM)),
        input_output_aliases={i: 3 + i for i in range(2 * na)},
        compiler_params=pltpu.CompilerParams(has_side_effects=_EFFECT),
    )(*[pltpu.with_memory_space_constraint(t, pltpu.HBM) for t in srcs + lands], *dep_ops)
    return (out[:3], out[3:3 + na], out[3 + na:3 + 2 * na]), out[-1]


def _exchange_wait(handle, gather, after, name):
    sems, srcs, lands = handle
    na = len(srcs)

    def body(*refs):
        src_refs, land_refs = refs[:na], refs[na:2 * na]
        send_sems, recv_sems, local_sems = refs[2 * na:2 * na + 3]
        local, remote = _exchange_copies(src_refs, land_refs, send_sems, recv_sems, local_sems, gather)
        for cp in remote:
            cp.wait_send()
            cp.wait_recv()
        for cp in local:
            cp.wait()

    hbm = lambda t: pltpu.HBM(t.shape, t.dtype)
    out = pl.pallas_call(
        body, name=name, out_shape=(*[hbm(t) for t in srcs], *[hbm(t) for t in lands]),
        in_specs=[_HBM] * (2 * na) + [_SEM] * 3 + [pl.BlockSpec(memory_space=pl.ANY)],
        out_specs=tuple([_HBM] * (2 * na)), input_output_aliases={i: i for i in range(2 * na)},
        compiler_params=pltpu.CompilerParams(has_side_effects=_EFFECT),
    )(*srcs, *lands, *sems, after)
    return out[na:]


def _block_diag(w):
    h, a, b = w.shape
    eye = jnp.eye(h, dtype=w.dtype)
    return (w[:, :, None, :] * eye[:, None, :, None]).reshape(h * a, h * b)


def _block_diag_extract(m, h):
    a, b = m.shape[0] // h, m.shape[1] // h
    return jnp.stack([m[i * a:(i + 1) * a, i * b:(i + 1) * b] for i in range(h)], axis=0)


def _ffn_interleave(w):
    lead = w.shape[:-1]
    nb = D_FF // FFN_CB
    return jnp.swapaxes(w.reshape(*lead, 2, nb, FFN_CB), -3, -2).reshape(*lead, 2 * D_FF)


def _ffn_deinterleave(w):
    lead = w.shape[:-1]
    nb = D_FF // FFN_CB
    return jnp.swapaxes(w.reshape(*lead, nb, 2, FFN_CB), -3, -2).reshape(*lead, 2 * D_FF)


def _gather_full(gathered, axis):
    shape = list(gathered.shape[1:])
    shape[axis] *= N_DEV
    return jnp.moveaxis(gathered, 0, axis).reshape(shape)


def _scatter_blocks(full, axis):
    shape = list(full.shape)
    shape[axis:axis + 1] = [N_DEV, shape[axis] // N_DEV]
    return jnp.moveaxis(full.reshape(shape), axis, 0)


def _pad_to(flat, mult):
    pad = (-flat.shape[-1]) % mult
    if pad:
        flat = jnp.concatenate([flat, jnp.zeros(flat.shape[:-1] + (pad,), flat.dtype)], axis=-1)
    return flat


def _layer_fwd(h_in, w, cos, sin, l, dep, get_ffn):
    tag = "l%d_" % l
    proj = _mm_nn(h_in, w['w_in'], 512, D_IN, tag + "proj", dep=dep)
    qkv = _rope_fwd(proj, cos, sin, tag + "rope")
    outs, lses = [], []
    for d, qv in zip(DILATIONS, qkv):
        o, ls = _attn_fwd(qv, d, tag + "attn_d%d" % d)
        outs.append(o)
        lses.append(ls)
    lru = _lru_fwd(proj, w['lru_conv_w'], w['lru_conv_b'], w['lru_wr'], w['lru_br'], w['lru_wi'],
                   w['lru_bi'], w['lru_lambda'], tag + "lru")
    x_re, x_im, y_acc = _s5_scan_fwd(proj, w['s5_bb_re'], w['s5_bb_im'], w['s5_lam_re'], w['s5_lam_im'],
                                     w['s5_cc_re'], w['s5_cc_im'], tag + "s5_scan")
    s5, y_pre = _s5_out_fwd(proj, y_acc, w['s5_d'], w['s5_w_glu'], w['s5_b_glu'], tag + "s5_out")
    mixed, attn_o, attn_lse = _mix_fwd(outs, lses, lru, s5, w['mix_norm_g'], tag + "mix")
    mixo = _mm_nn(mixed, w['w_out'], 512, D_MODEL, tag + "out_proj")
    r1, h1 = _ln_fwd(h_in, mixo, w['ln1_g'], w['ln1_b'], tag + "ln1")
    w['w_up_g'], w['w_down'], ffn_dep = get_ffn(l, h1)
    up = _mm_up(h1, w['w_up_g'], 1024, tag + "up_proj", dep=ffn_dep)
    act = _ffn_act_fwd(up, w['ffn_conv_w'], w['ffn_conv_b'], tag + "ffn_act")
    ffn = _mm_nn(act, w['w_down'], 512, D_MODEL, tag + "down_proj")
    r2, h2 = _ln_fwd(h1, ffn, w['ln2_g'], w['ln2_b'], tag + "ln2")
    saved = dict(h_in=h_in, proj=proj, qkv=qkv, lru=lru, x_re=x_re, x_im=x_im, y_pre=y_pre, s5=s5,
                 mixed=mixed, attn_o=attn_o, attn_lse=attn_lse, r1=r1, h1=h1, up=up, act=act, r2=r2)
    return h2, saved


def _layer_bwd_ffn(dh2, sv, w, l, dep=None):
    tag = "l%d_" % l
    g = {}
    dr2, g['ln2_g'], g['ln2_b'] = _ln_bwd(sv['r2'], dh2, w['ln2_g'], tag + "ln2_bwd", dep=dep)
    g['w_down'] = _mm_tn(sv['act'], dr2, 1024, D_MODEL, 512, tag + "down_dw")
    dact = _mm_nt(dr2, w['w_down'], 512, D_FF, tag + "down_dx")
    dup, g['ffn_conv_w'], g['ffn_conv_b'] = _ffn_act_bwd(sv['up'], dact, w['ffn_conv_w'], w['ffn_conv_b'],
                                                        tag + "ffn_act_bwd")
    g['w_up_g'] = _mm_up_dw(sv['h1'], dup, 512, tag + "up_dw")
    dh1 = _mm_up_dx(dup, w['w_up_g'], dr2, ALPHA, 1024, tag + "up_dx")
    return dh1, g


def _layer_bwd_mix(dh1, sv, w, cos, sin, l, dep, after_out_grad, after_in_grad):
    tag = "l%d_" % l
    g = {}
    dr1, g['ln1_g'], g['ln1_b'] = _ln_bwd(sv['r1'], dh1, w['ln1_g'], tag + "ln1_bwd", dep=dep)
    g['w_out'] = _mm_tn(sv['mixed'], dr1, 1024, D_MODEL, 512, tag + "out_dw")
    dmixed = _mm_nt(dr1, w['w_out'], 512, D_MODEL, tag + "out_dx", dep=after_out_grad(l, g['w_out']))
    d_o, dlru, ds5, g['mix_norm_g'] = _mix_bwd(dmixed, sv['attn_o'][0], sv['lru'], sv['s5'], w['mix_norm_g'],
                                               tag + "mix_bwd")
    dy, dud, g['s5_d'], g['s5_w_glu'], g['s5_b_glu'] = _s5_out_bwd(
        sv['proj'], sv['y_pre'], ds5, w['s5_d'], w['s5_w_glu'], w['s5_b_glu'], tag + "s5_out_bwd")
    du, g['s5_lam_re'], g['s5_lam_im'], g['s5_bb_re'], g['s5_bb_im'], g['s5_cc_re'], g['s5_cc_im'] = \
        _s5_scan_bwd(sv['proj'], dy, dud, sv['x_re'], sv['x_im'], w['s5_bb_re'], w['s5_bb_im'],
                     w['s5_lam_re'], w['s5_lam_im'], w['s5_cc_re'], w['s5_cc_im'], tag + "s5_scan_bwd")
    (dxr, dgate, g['lru_conv_w'], g['lru_conv_b'], g['lru_wr'], g['lru_br'], g['lru_wi'], g['lru_bi'],
     g['lru_lambda']) = _lru_bwd(sv['proj'], dlru, w['lru_conv_w'], w['lru_conv_b'], w['lru_wr'],
                                 w['lru_br'], w['lru_wi'], w['lru_bi'], w['lru_lambda'], tag + "lru_bwd")
    dqkv = [_attn_bwd(sv['qkv'][b], sv['attn_o'][b], d_o[b], sv['attn_lse'][b], d, tag + "attn_bwd_d%d" % d)
            for b, d in enumerate(DILATIONS)]
    dproj = _dproj_assemble(dqkv, dxr, dgate, du, cos, sin, tag + "dproj")
    g['w_in'] = _mm_tn(sv['h_in'], dproj, 1024, D_IN, 512, tag + "in_dw")
    dh_in = _mm_nt(dproj, w['w_in'], 512, D_MODEL, tag + "in_dx", add=dr1, add_scale=ALPHA,
                   dep=after_in_grad(l, g['w_in']))
    return dh_in, g


def _s5_rep(a):
    return jnp.repeat(a, S5_C, axis=0)


def _prepare_layer(p, l):
    w = {}
    for n in ('w_in', 'w_out', 's5_w_glu'):
        w[n] = p[n].astype(BF16)
    w['ffn_conv_w'] = _ffn_interleave(p['ffn_conv_w'])
    w['ffn_conv_b'] = _ffn_interleave(p['ffn_conv_b'])[None, :]
    w['lru_conv_w'] = p['lru_conv_w']
    for n in ('lru_conv_b', 'lru_br', 'lru_bi', 'lru_lambda', 's5_b_glu', 'mix_norm_g',
              'ln1_g', 'ln1_b', 'ln2_g', 'ln2_b'):
        w[n] = p[n][None, :]
    w['lru_wr'] = _block_diag(p['lru_wr']).astype(BF16)
    w['lru_wi'] = _block_diag(p['lru_wi']).astype(BF16)
    w['s5_d'] = p['s5_d'].reshape(1, S5_W)
    disc_in = (_s5_rep(p['s5_a_re']), _s5_rep(p['s5_a_im']),
               _s5_rep(jnp.broadcast_to(p['s5_log_step'][:, None], (S5_G, S5_P))),
               jnp.swapaxes(p['s5_b_re'], 1, 2).reshape(S5_W, S5_P),
               jnp.swapaxes(p['s5_b_im'], 1, 2).reshape(S5_W, S5_P))
    ab_re, ab_im, bb_re, bb_im = _s5_disc_fwd(*disc_in, "l%d_s5_disc" % l)
    w['s5_disc_in'] = disc_in
    w['s5_lam_re'] = ab_re.reshape(S5_G, S5_C, S5_P)[:, 0, :].reshape(1, S5_STATES)
    w['s5_lam_im'] = ab_im.reshape(S5_G, S5_C, S5_P)[:, 0, :].reshape(1, S5_STATES)
    w['s5_bb_re'] = _block_diag(bb_re.reshape(S5_G, S5_C, S5_P)).astype(BF16)
    w['s5_bb_im'] = _block_diag(bb_im.reshape(S5_G, S5_C, S5_P)).astype(BF16)
    w['s5_cc_re'] = _block_diag(jnp.swapaxes(p['s5_c_re'], 1, 2)).astype(BF16)
    w['s5_cc_im'] = _block_diag(jnp.swapaxes(p['s5_c_im'], 1, 2)).astype(BF16)
    return w


def _finish_layer_grads(g, w, l):
    out = {}
    for n in ('w_in', 'w_out', 'w_down', 'w_up_g', 's5_w_glu', 'lru_conv_w'):
        out[n] = g[n]
    out['ffn_conv_w'] = _ffn_deinterleave(g['ffn_conv_w'])
    out['ffn_conv_b'] = _ffn_deinterleave(g['ffn_conv_b'])[0]
    for n in ('lru_conv_b', 'lru_br', 'lru_bi', 'lru_lambda', 's5_b_glu', 'mix_norm_g',
              'ln1_g', 'ln1_b', 'ln2_g', 'ln2_b'):
        out[n] = g[n][0]
    out['lru_wr'] = _block_diag_extract(g['lru_wr'], LRU_W // HEAD)
    out['lru_wi'] = _block_diag_extract(g['lru_wi'], LRU_W // HEAD)
    out['s5_d'] = g['s5_d'].reshape(S5_G, S5_C)
    out['s5_c_re'] = jnp.swapaxes(_block_diag_extract(g['s5_cc_re'], S5_G), 1, 2)
    out['s5_c_im'] = jnp.swapaxes(_block_diag_extract(g['s5_cc_im'], S5_G), 1, 2)
    rep = lambda v: _s5_rep(v.reshape(S5_G, S5_P)) * (1.0 / S5_C)
    cts = (rep(g['s5_lam_re']), rep(g['s5_lam_im']),
           _block_diag_extract(g['s5_bb_re'], S5_G).reshape(S5_W, S5_P),
           _block_diag_extract(g['s5_bb_im'], S5_G).reshape(S5_W, S5_P))
    da_re, da_im, dls, dbt_re, dbt_im = _s5_disc_bwd(*w['s5_disc_in'], cts, "l%d_s5_disc_bwd" % l)
    out['s5_a_re'] = da_re.reshape(S5_G, S5_C, S5_P).sum(axis=1)
    out['s5_a_im'] = da_im.reshape(S5_G, S5_C, S5_P).sum(axis=1)
    out['s5_log_step'] = dls.reshape(S5_G, S5_C * S5_P).sum(axis=1)
    out['s5_b_re'] = jnp.swapaxes(dbt_re.reshape(S5_G, S5_C, S5_P), 1, 2)
    out['s5_b_im'] = jnp.swapaxes(dbt_im.reshape(S5_G, S5_C, S5_P), 1, 2)
    return out


def _run_step(x, target, get_layer, get_ffn, after_ffn_grads, after_out_grad, after_in_grad, after_layer):
    cos, sin = _rope_tables(x.shape[0])
    h = x
    ws, saved = [], []
    for l in range(DEPTH):
        p, dep = get_layer(l, h)
        ws.append(_prepare_layer(p, l))
        h, sv = _layer_fwd(h, ws[l], cos, sin, l, dep, get_ffn)
        saved.append(sv)
    dh, loss_vec = _loss_head(h, target)
    dep = None
    for l in reversed(range(DEPTH)):
        dh1, g = _layer_bwd_ffn(dh, saved[l], ws[l], l, dep)
        dep = after_ffn_grads(l, g)
        dh, g_mix = _layer_bwd_mix(dh1, saved[l], ws[l], cos, sin, l, dep, after_out_grad, after_in_grad)
        g.update(g_mix)
        after_layer(l, _finish_layer_grads(g, ws[l], l))
        dep = None
    return loss_vec[0, 0], dh


def _local_step(x, target, layers):
    grads = [None] * DEPTH

    def keep(l, g):
        grads[l] = g

    def ffn(l, h1):
        return layers[l]['w_up_g'].astype(BF16), layers[l]['w_down'].astype(BF16), None

    none = lambda l, g: None
    loss, dx = _run_step(x, target, lambda l, h: (layers[l], None), ffn, none, none, none, keep)
    return loss, dx, grads


def kernel(x, w_in, lru_conv_w, lru_conv_b, lru_wr, lru_br, lru_wi, lru_bi, lru_lambda, s5_a_re, s5_a_im, s5_b_re, s5_b_im, s5_c_re, s5_c_im, s5_d, s5_log_step, s5_w_glu, s5_b_glu, mix_norm_g, w_out, ln1_g, ln1_b, w_up, ffn_conv_w, ffn_conv_b, w_down, ln2_g, ln2_b, loss_target, m_w_in, m_lru_conv_w, m_lru_conv_b, m_lru_wr, m_lru_br, m_lru_wi, m_lru_bi, m_lru_lambda, m_s5_a_re, m_s5_a_im, m_s5_b_re, m_s5_b_im, m_s5_c_re, m_s5_c_im, m_s5_d, m_s5_log_step, m_s5_w_glu, m_s5_b_glu, m_mix_norm_g, m_w_out, m_ln1_g, m_ln1_b, m_w_up, m_ffn_conv_w, m_ffn_conv_b, m_w_down, m_ln2_g, m_ln2_b, v_w_in, v_lru_conv_w, v_lru_conv_b, v_lru_wr, v_lru_br, v_lru_wi, v_lru_bi, v_lru_lambda, v_s5_a_re, v_s5_a_im, v_s5_b_re, v_s5_b_im, v_s5_c_re, v_s5_c_im, v_s5_d, v_s5_log_step, v_s5_w_glu, v_s5_b_glu, v_mix_norm_g, v_w_out, v_ln1_g, v_ln1_b, v_w_up, v_ffn_conv_w, v_ffn_conv_b, v_w_down, v_ln2_g, v_ln2_b):
    args = locals()
    wl = {n: args[n] for n in WEIGHTS}
    ml = {n: args['m_' + n] for n in WEIGHTS}
    vl = {n: args['v_' + n] for n in WEIGHTS}
    me = 4 * lax.axis_index("x") + 2 * lax.axis_index("y") + lax.axis_index("c")

    small_sizes = [int(wl[n].size) for n in SMALL_SHARDED]
    small_flat = _pad_to(jnp.concatenate([wl[n].reshape(-1) for n in SMALL_SHARDED]), 8 * 1024)
    small_all, = _all_gather([small_flat.reshape(-1, 1024)], "gather_small")
    small_all = small_all.reshape(N_DEV, -1)
    small_full, off = {}, 0
    for n, sz in zip(SMALL_SHARDED, small_sizes):
        small_full[n] = _gather_full(small_all[:, off:off + sz].reshape((N_DEV,) + wl[n].shape), SHARD_AXIS[n])
        off += sz
    def mixer_params(l, gathered):
        g_in, g_out = gathered
        p = {n: wl[n][l] for n in REPLICATED}
        p.update({n: small_full[n][l] for n in SMALL_SHARDED})
        p['w_in'] = _gather_full(g_in, 1)
        p['w_out'] = g_out.reshape(D_MODEL, D_MODEL)
        return p

    mix_names, ffn_names = ('w_in', 'w_out'), ('w_up', 'w_down')
    shards = lambda names, l: [wl[n][l].astype(BF16) for n in names]
    mix0 = _all_gather(shards(mix_names, 0), "gather_mix_l0")
    gathers = {}
    gathers[0, 'ffn'], ffn0_token = _exchange_start(shards(ffn_names, 0), True, "gather_ffn_l0_start", dep=mix0[0])
    def get_layer(l, h):
        if l == 0:
            return mixer_params(0, mix0), ffn0_token
        return mixer_params(1, _exchange_wait(gathers[1, 'mix'], True, h, "gather_mix_l1_wait")), None

    def get_ffn(l, h1):
        g_up, g_down = _exchange_wait(gathers[l, 'ffn'], True, h1, "gather_ffn_l%d_wait" % l)
        token = None
        if l == 0:
            gathers[1, 'mix'], token = _exchange_start(shards(mix_names, 1), True, "gather_mix_l1_start", dep=g_up)
            gathers[1, 'ffn'], token = _exchange_start(shards(ffn_names, 1), True, "gather_ffn_l1_start", dep=token)
        return g_up, g_down.reshape(D_FF, D_MODEL), token

    scatters, grads = {}, [None] * DEPTH

    def after_ffn_grads(l, g):
        send = [g['w_up_g'], g['w_down'].reshape(N_DEV, D_FF // N_DEV, D_MODEL)]
        scatters[l, 'ffn'], token = _exchange_start(send, False, "scatter_ffn_l%d_start" % l)
        return token

    def after_out_grad(l, g_out):
        send = [g_out.reshape(N_DEV, D_MODEL // N_DEV, D_MODEL)]
        scatters[l, 'out'], token = _exchange_start(send, False, "scatter_out_l%d_start" % l)
        return token

    def after_in_grad(l, g_in):
        scatters[l, 'in'], token = _exchange_start([_scatter_blocks(g_in, 1)], False, "scatter_in_l%d_start" % l)
        return token

    def after_layer(l, g):
        grads[l] = g

    loss_local, grad_x = _run_step(x[0], loss_target[0], get_layer, get_ffn, after_ffn_grads, after_out_grad,
                                   after_in_grad, after_layer)
    loss = lax.psum(loss_local, AXES)

    results = {}
    big_prev = {n: None for n in BIG}

    def finish_big(l, part, names, after):
        landed = _exchange_wait(scatters[l, part], False, after, "scatter_%s_l%d_wait" % (part, l))
        for n, ld in zip(names, landed):
            big_prev[n] = _adamw_sum(ld, wl[n], ml[n], vl[n], l, big_prev[n], "adamw_%s_l%d" % (n, l))

    for l, part, names in ((1, 'ffn', ffn_names), (1, 'out', ('w_out',)), (1, 'in', ('w_in',)),
                           (0, 'ffn', ffn_names), (0, 'out', ('w_out',))):
        finish_big(l, part, names, grad_x)

    stacked = {n: jnp.stack([grads[l][n] for l in range(DEPTH)], axis=0) for n in SMALL_SHARDED + REPLICATED}
    rep_sizes = [int(wl[n].size) for n in REPLICATED]
    rep_per = -(-sum(rep_sizes) // (N_DEV * 1024)) * 1024

    def rep_flat(tree):
        return _pad_to(jnp.concatenate([tree[n].reshape(-1) for n in REPLICATED]), N_DEV * rep_per)

    rows = [_scatter_blocks(stacked[n], SHARD_AXIS[n]).reshape(N_DEV, -1) for n in SMALL_SHARDED]
    rows.append(rep_flat(stacked).reshape(N_DEV, rep_per))
    small_send = _pad_to(jnp.concatenate(rows, axis=1), 8 * 1024)
    n_own = small_send.shape[1]
    small_landed, = _all_to_all([small_send.reshape(N_DEV, n_own // 1024, 1024)], "scatter_small")

    def own_flat(tree):
        parts = [tree[n].reshape(-1) for n in SMALL_SHARDED]
        parts.append(lax.dynamic_slice(rep_flat(tree), (me * rep_per,), (rep_per,)))
        return _pad_to(jnp.concatenate(parts), 8 * 1024).reshape(1, n_own // 1024, 1024)

    small_res = _adamw_sum(small_landed, own_flat(wl), own_flat(ml), own_flat(vl), 0, None, "adamw_small")

    kinds = ('grad', 'delta', 'm', 'v')
    sh_total = sum(small_sizes)
    for kind, arr in zip(kinds, small_res):
        flat = arr.reshape(-1)
        off = 0
        for n, sz in zip(SMALL_SHARDED, small_sizes):
            results[kind, n] = flat[off:off + sz].reshape(wl[n].shape)
            off += sz
    rep_own = jnp.stack([a.reshape(-1)[sh_total:sh_total + rep_per] for a in small_res])
    rep_all, = _all_gather([rep_own.reshape(4 * rep_per // 1024, 1024)], "gather_replicated")
    finish_big(0, 'in', ('w_in',), rep_all)
    for n in BIG:
        results['grad', n], results['delta', n], results['m', n], results['v', n] = big_prev[n]
    rep_all = rep_all.reshape(N_DEV, 4, rep_per)
    for k, kind in enumerate(kinds):
        flat = rep_all[:, k, :].reshape(-1)
        off = 0
        for n, sz in zip(REPLICATED, rep_sizes):
            results[kind, n] = flat[off:off + sz].reshape(wl[n].shape)
            off += sz

    out = [loss, grad_x[None]]
    for kind in kinds:
        out.extend(results[kind, n] for n in WEIGHTS)
    return tuple(out)
```

```python
import functools
import math

import jax
import jax.numpy as jnp
from jax import lax
from jax.experimental import pallas as pl
from jax.experimental.pallas import tpu as pltpu

F32 = jnp.float32
BF16 = jnp.bfloat16

N_DEV = 8
DEPTH = 2
D_MODEL = 1024
ATTN_W = 384
LRU_W = 384
S5_W = 256
D_IN = 2176
D_FF = 3072
HEAD = 64
ATTN_BLK = 128
ATTN_TILE = 1024
DILATIONS = (1, 4, 16)
S5_G = 16
S5_P = 64
S5_C = 16
S5_STATES = S5_G * S5_P
LRU_C = 8.0
LRU_CONV = 4
FFN_CONV = 3
ROPE_THETA = 10000.0
ALPHA = (2 * DEPTH) ** 0.25
LN_EPS = 1e-5
RMS_EPS = 1e-6
ADAM_LR, ADAM_B1, ADAM_B2, ADAM_EPS, ADAM_WD, ADAM_STEP = 0.001, 0.9, 0.999, 1e-8, 0.01, 10

LANE = 128
SCAN_T = 256
S5_BLK = 256
FFN_CB = 2 * D_FF // N_DEV
VMEM_LIMIT = 56 * 1024 * 1024

AXES = ("x", "y", "c")

WEIGHTS = ['w_in', 'lru_conv_w', 'lru_conv_b', 'lru_wr', 'lru_br', 'lru_wi', 'lru_bi', 'lru_lambda',
           's5_a_re', 's5_a_im', 's5_b_re', 's5_b_im', 's5_c_re', 's5_c_im', 's5_d', 's5_log_step',
           's5_w_glu', 's5_b_glu', 'mix_norm_g', 'w_out', 'ln1_g', 'ln1_b', 'w_up', 'ffn_conv_w',
           'ffn_conv_b', 'w_down', 'ln2_g', 'ln2_b']
SHARD_AXIS = {'w_in': 2, 'lru_conv_w': 2, 's5_w_glu': 1, 'w_out': 1, 'w_up': 2, 'ffn_conv_w': 2, 'w_down': 1}
BIG = ['w_in', 'w_out', 'w_up', 'w_down']
SMALL_SHARDED = ['lru_conv_w', 'ffn_conv_w', 's5_w_glu']
REPLICATED = [n for n in WEIGHTS if n not in SHARD_AXIS]


def _cparams(sem=None):
    return pltpu.CompilerParams(dimension_semantics=sem, vmem_limit_bytes=VMEM_LIMIT)


def _ffn_dev(jb):
    return jb // 2 + (N_DEV // 2) * (jb % 2)


def _gelu(x):
    c = math.sqrt(2.0 / math.pi)
    t = jnp.tanh(c * (x + 0.044715 * (x * x * x)))
    return 0.5 * x * (1.0 + t)


def _gelu_grad(x):
    c = math.sqrt(2.0 / math.pi)
    x2 = x * x
    t = jnp.tanh(c * (x + 0.044715 * (x2 * x)))
    return 0.5 * (1.0 + t) + 0.5 * x * (1.0 - t * t) * (c * (1.0 + 3.0 * 0.044715 * x2))


def _sigmoid(x):
    return 1.0 / (1.0 + jnp.exp(-x))


def _log1p(x):
    u = 1.0 + x
    d = u - 1.0
    return jnp.where(d == 0.0, x, jnp.log(u) * (x / jnp.where(d == 0.0, 1.0, d)))


def _softplus(x):
    return jnp.maximum(x, 0.0) + _log1p(jnp.exp(-jnp.abs(x)))


def _expm1(x):
    return jnp.tanh(0.5 * x) * (jnp.exp(x) + 1.0)


def _dot(a, b):
    return jnp.dot(a.astype(BF16), b.astype(BF16), preferred_element_type=F32)


def _dot_nt(a, b):
    return lax.dot_general(a.astype(BF16), b.astype(BF16), (((1,), (1,)), ((), ())),
                           preferred_element_type=F32)


def _dot_tn(a, b):
    return lax.dot_general(a.astype(BF16), b.astype(BF16), (((0,), (0,)), ((), ())),
                           preferred_element_type=F32)


def _rows(shape):
    return lax.broadcasted_iota(jnp.int32, shape, 0)


def _shift_down_prev(x, s, prev8):
    if s == 0:
        return x
    t, l = x.shape
    r = pltpu.roll(x, s, axis=0)
    pr = pltpu.roll(prev8, s, axis=0)
    pad = jnp.concatenate([pr, jnp.zeros((t - 8, l), x.dtype)], axis=0)
    return jnp.where(_rows(x.shape) < s, pad, r)


def _shift_up_next(x, s, next8):
    if s == 0:
        return x
    t, l = x.shape
    r = pltpu.roll(x, t - s, axis=0)
    nx = pltpu.roll(next8, 8 - s, axis=0)
    pad = jnp.concatenate([jnp.zeros((t - 8, l), x.dtype), nx], axis=0)
    return jnp.where(_rows(x.shape) >= t - s, pad, r)


SUB = 8


def _tile_shift(x, s, fill, reverse):
    t = x.shape[0]
    pos = _rows(x.shape) & (SUB - 1)
    if reverse:
        return jnp.where(pos < SUB - s, pltpu.roll(x, t - s, axis=0), fill)
    return jnp.where(pos >= s, pltpu.roll(x, s, axis=0), fill)


def _scan_chunk(a, x, carry, reverse=False):
    s = 1
    while s < SUB:
        x = x + a * _tile_shift(x, s, 0.0, reverse)
        a = a * _tile_shift(a, s, 1.0, reverse)
        s *= 2
    nv = x.shape[0] // SUB
    out = [None] * nv
    for v in (reversed(range(nv)) if reverse else range(nv)):
        rows = slice(v * SUB, (v + 1) * SUB)
        out[v] = x[rows, :] + a[rows, :] * carry
        carry = out[v][0:1, :] if reverse else out[v][SUB - 1:SUB, :]
    return jnp.concatenate(out, axis=0)


def _cmul(ar, ai, br, bi):
    return ar * br - ai * bi, ar * bi + ai * br


def _cscan_consts(lr, li, reverse):
    pows = [(lr, li)]
    for _ in range(2):
        pows.append(_cmul(*pows[-1], *pows[-1]))
    rows = [(lr, li)]
    for _ in range(SUB - 1):
        rows.append(_cmul(*rows[-1], lr, li))
    if reverse:
        rows = rows[::-1]
    return pows, (jnp.concatenate([r for r, _ in rows], axis=0), jnp.concatenate([i for _, i in rows], axis=0))


def _cscan_chunk(xr, xi, consts, carry, reverse=False):
    pows, (p8r, p8i) = consts
    s = 1
    for pr, pi in pows:
        sr = _tile_shift(xr, s, 0.0, reverse)
        si = _tile_shift(xi, s, 0.0, reverse)
        xr, xi = xr + pr * sr - pi * si, xi + pr * si + pi * sr
        s *= 2
    nv = xr.shape[0] // SUB
    out_r, out_i = [None] * nv, [None] * nv
    cr, ci = carry
    for v in (reversed(range(nv)) if reverse else range(nv)):
        rows = slice(v * SUB, (v + 1) * SUB)
        out_r[v] = xr[rows, :] + p8r * cr - p8i * ci
        out_i[v] = xi[rows, :] + p8r * ci + p8i * cr
        edge = slice(0, 1) if reverse else slice(SUB - 1, SUB)
        cr, ci = out_r[v][edge, :], out_i[v][edge, :]
    return jnp.concatenate(out_r, axis=0), jnp.concatenate(out_i, axis=0)


def _dep_args(dep):
    return ([], []) if dep is None else ([pl.BlockSpec(memory_space=pl.ANY)], [dep])


def _mm_nn(a, b, tm, tn, name, out_dtype=F32, dep=None):
    m, k = a.shape
    n = b.shape[1]

    def body(a_ref, b_ref, *rest):
        o_ref = rest[-1]
        o_ref[...] = _dot(a_ref[...], b_ref[...]).astype(out_dtype)

    dep_specs, dep_ops = _dep_args(dep)
    return pl.pallas_call(
        body, out_shape=jax.ShapeDtypeStruct((m, n), out_dtype), grid=(n // tn, m // tm),
        in_specs=[pl.BlockSpec((tm, k), lambda j, i: (i, 0)),
                  pl.BlockSpec((k, tn), lambda j, i: (0, j))] + dep_specs,
        out_specs=pl.BlockSpec((tm, tn), lambda j, i: (i, j)), name=name,
        compiler_params=_cparams(("parallel", "parallel")))(a, b, *dep_ops)


def _mm_nt(a, w, tm, tn, name, add=None, add_scale=1.0, dep=None):
    m, k = a.shape
    n = w.shape[0]

    def body(a_ref, w_ref, *rest):
        o_ref = rest[-1]
        if add is None:
            o_ref[...] = _dot_nt(a_ref[...], w_ref[...])
        else:
            o_ref[...] = _dot_nt(a_ref[...], w_ref[...]) + add_scale * rest[0][...]

    in_specs = [pl.BlockSpec((tm, k), lambda j, i: (i, 0)), pl.BlockSpec((tn, k), lambda j, i: (j, 0))]
    args = [a, w]
    if add is not None:
        in_specs.append(pl.BlockSpec((tm, tn), lambda j, i: (i, j)))
        args.append(add)
    dep_specs, dep_ops = _dep_args(dep)
    return pl.pallas_call(
        body, out_shape=jax.ShapeDtypeStruct((m, n), F32), grid=(n // tn, m // tm),
        in_specs=in_specs + dep_specs, out_specs=pl.BlockSpec((tm, tn), lambda j, i: (i, j)), name=name,
        compiler_params=_cparams(("parallel", "parallel")))(*args, *dep_ops)


def _mm_dw(at, b, tm, tn, ts, name):
    m, s = at.shape
    n = b.shape[1]

    def body(a_ref, b_ref, o_ref):
        @pl.when(pl.program_id(2) == 0)
        def _():
            o_ref[...] = jnp.zeros_like(o_ref)
        o_ref[...] += _dot(a_ref[...], b_ref[...])

    return pl.pallas_call(
        body, out_shape=jax.ShapeDtypeStruct((m, n), F32), grid=(m // tm, n // tn, s // ts),
        in_specs=[pl.BlockSpec((tm, ts), lambda i, j, k: (i, k)), pl.BlockSpec((ts, tn), lambda i, j, k: (k, j))],
        out_specs=pl.BlockSpec((tm, tn), lambda i, j, k: (i, j)), name=name,
        compiler_params=_cparams(("parallel", "parallel", "arbitrary")))(at, b)


def _transpose_bf16(x, name):
    s, d = x.shape
    tm = 512

    def body(x_ref, o_ref):
        o_ref[...] = x_ref[...].T.astype(BF16)

    return pl.pallas_call(
        body, out_shape=jax.ShapeDtypeStruct((d, s), BF16), grid=(s // tm,),
        in_specs=[pl.BlockSpec((tm, d), lambda i: (i, 0))], out_specs=pl.BlockSpec((d, tm), lambda i: (0, i)),
        name=name, compiler_params=_cparams(("parallel",)))(x)


def _mm_up(h, wg, tm, name, dep=None):
    s, d = h.shape

    def body(a_ref, w_ref, *rest):
        rest[-1][...] = _dot(a_ref[...], w_ref[...])

    dep_specs, dep_ops = _dep_args(dep)
    return pl.pallas_call(
        body, out_shape=jax.ShapeDtypeStruct((s, 2 * D_FF), F32), grid=(s // tm, N_DEV),
        in_specs=[pl.BlockSpec((tm, d), lambda i, j: (i, 0)),
                  pl.BlockSpec((None, d, FFN_CB), lambda i, j: (_ffn_dev(j), 0, 0))] + dep_specs,
        out_specs=pl.BlockSpec((tm, FFN_CB), lambda i, j: (i, j)), name=name,
        compiler_params=_cparams(("parallel", "parallel")))(h, wg, *dep_ops)


def _mm_up_dx(dup, wg, add, add_scale, tm, name):
    s = dup.shape[0]
    d = wg.shape[1]

    def body(a_ref, w_ref, c_ref, o_ref):
        @pl.when(pl.program_id(1) == 0)
        def _():
            o_ref[...] = add_scale * c_ref[...]
        o_ref[...] += _dot_nt(a_ref[...], w_ref[...])

    return pl.pallas_call(
        body, out_shape=jax.ShapeDtypeStruct((s, d), F32), grid=(s // tm, N_DEV),
        in_specs=[pl.BlockSpec((tm, FFN_CB), lambda i, j: (i, j)),
                  pl.BlockSpec((None, d, FFN_CB), lambda i, j: (_ffn_dev(j), 0, 0)),
                  pl.BlockSpec((tm, d), lambda i, j: (i, 0))],
        out_specs=pl.BlockSpec((tm, d), lambda i, j: (i, 0)), name=name,
        compiler_params=_cparams(("parallel", "arbitrary")))(dup, wg, add)


def _mm_up_dw(ht, dup, name):
    d, s = ht.shape

    def body(a_ref, b_ref, o_ref):
        o_ref[...] = _dot(a_ref[...], b_ref[...])

    return pl.pallas_call(
        body, out_shape=jax.ShapeDtypeStruct((N_DEV, d, FFN_CB), F32), grid=(N_DEV,),
        in_specs=[pl.BlockSpec((d, s), lambda j: (0, 0)), pl.BlockSpec((s, FFN_CB), lambda j: (0, j))],
        out_specs=pl.BlockSpec((None, d, FFN_CB), lambda j: (_ffn_dev(j), 0, 0)), name=name,
        compiler_params=_cparams(("parallel",)))(ht, dup)


def _ln_fwd(a, b, g, bias, name, transposed=True):
    s, d = a.shape
    tm = 512

    def body(a_ref, b_ref, g_ref, bias_ref, r_ref, h_ref, *ht_ref):
        r = ALPHA * a_ref[...] + b_ref[...]
        mu = jnp.mean(r, axis=-1, keepdims=True)
        xc = r - mu
        var = jnp.mean(xc * xc, axis=-1, keepdims=True)
        r_ref[...] = r
        h = xc * lax.rsqrt(var + LN_EPS) * g_ref[...] + bias_ref[...]
        h_ref[...] = h
        if transposed:
            ht_ref[0][...] = h.T.astype(BF16)

    row = pl.BlockSpec((tm, d), lambda i: (i, 0))
    vec = pl.BlockSpec((1, d), lambda i: (0, 0))
    shapes = [jax.ShapeDtypeStruct((s, d), F32), jax.ShapeDtypeStruct((s, d), F32)]
    specs = [row, row]
    if transposed:
        shapes.append(jax.ShapeDtypeStruct((d, s), BF16))
        specs.append(pl.BlockSpec((d, tm), lambda i: (0, i)))
    return pl.pallas_call(
        body, out_shape=tuple(shapes), grid=(s // tm,), in_specs=[row, row, vec, vec],
        out_specs=tuple(specs), name=name, compiler_params=_cparams(("parallel",)))(a, b, g, bias)


def _ln_bwd(r, dh, g, name, dep=None):
    s, d = r.shape
    tm = 512

    def body(r_ref, dh_ref, g_ref, *rest):
        dr_ref, dg_ref, db_ref = rest[-3:]

        @pl.when(pl.program_id(0) == 0)
        def _():
            dg_ref[...] = jnp.zeros_like(dg_ref)
            db_ref[...] = jnp.zeros_like(db_ref)
        rr = r_ref[...]
        dh_ = dh_ref[...]
        mu = jnp.mean(rr, axis=-1, keepdims=True)
        xc = rr - mu
        var = jnp.mean(xc * xc, axis=-1, keepdims=True)
        rstd = lax.rsqrt(var + LN_EPS)
        xh = xc * rstd
        dxh = dh_ * g_ref[...]
        m1 = jnp.mean(dxh, axis=-1, keepdims=True)
        m2 = jnp.mean(dxh * xh, axis=-1, keepdims=True)
        dr_ref[...] = rstd * (dxh - m1 - xh * m2)
        dg_ref[...] += jnp.sum(dh_ * xh, axis=0, keepdims=True)
        db_ref[...] += jnp.sum(dh_, axis=0, keepdims=True)

    row = pl.BlockSpec((tm, d), lambda i: (i, 0))
    vec = pl.BlockSpec((1, d), lambda i: (0, 0))
    dep_specs, dep_ops = _dep_args(dep)
    return pl.pallas_call(
        body, out_shape=(jax.ShapeDtypeStruct((s, d), F32), jax.ShapeDtypeStruct((1, d), F32),
                         jax.ShapeDtypeStruct((1, d), F32)),
        grid=(s // tm,), in_specs=[row, row, vec] + dep_specs, out_specs=(row, vec, vec), name=name,
        compiler_params=_cparams(("arbitrary",)))(r, dh, g, *dep_ops)


def _loss_head(y, target):
    s, d = y.shape
    tm = 512

    def body(y_ref, t_ref, dy_ref, l_ref):
        @pl.when(pl.program_id(0) == 0)
        def _():
            l_ref[...] = jnp.zeros_like(l_ref)
        e = y_ref[...] - t_ref[...]
        dy_ref[...] = e * (1.0 / d)
        part = 0.5 * jnp.sum(jnp.mean(e * e, axis=-1, keepdims=True), axis=0, keepdims=True)
        l_ref[...] += jnp.broadcast_to(part, l_ref.shape)

    row = pl.BlockSpec((tm, d), lambda i: (i, 0))
    return pl.pallas_call(
        body, out_shape=(jax.ShapeDtypeStruct((s, d), F32), jax.ShapeDtypeStruct((1, LANE), F32)),
        grid=(s // tm,), in_specs=[row, row], out_specs=(row, pl.BlockSpec((1, LANE), lambda i: (0, 0))),
        name="loss_head", compiler_params=_cparams(("arbitrary",)))(y, target)


def _rope_tables(s):
    half = HEAD // 2
    pos = jnp.arange(s, dtype=F32)
    inv = ROPE_THETA ** (-jnp.arange(half, dtype=F32) * 2.0 / HEAD)
    ang = pos[:, None] * inv[None, :]
    cos, sin = jnp.cos(ang), jnp.sin(ang)
    cos = jnp.concatenate([cos, cos, cos, cos], axis=1)
    sin = jnp.concatenate([-sin, sin, -sin, sin], axis=1)
    return cos, sin


def _rotate(x, cos, sin):
    lane = lax.broadcasted_iota(jnp.int32, x.shape, 1)
    partner = jnp.where((lane % HEAD) < HEAD // 2, pltpu.roll(x, LANE - HEAD // 2, axis=1),
                        pltpu.roll(x, HEAD // 2, axis=1))
    return x * cos + partner * sin


def _class_rows(c, d, tm):
    return pl.ds(c, tm // d, stride=d) if d > 1 else pl.ds(0, tm)


def _dilated_spec(tm, d, w):
    return pl.BlockSpec((tm // d, d * w), lambda i: (i, 0))


def _token_scratch(tm, w):
    return pltpu.VMEM((w // LANE, tm, LANE), F32)


def _to_tokens(src_ref, dst3, d, tm):
    nj = dst3.shape[0]
    for cls in range(d):
        for j in range(nj):
            col = (cls * nj + j) * LANE
            dst3.at[j][_class_rows(cls, d, tm), :] = src_ref[:, col:col + LANE]


def _to_dilated(src3, dst_ref, d, tm):
    nj = src3.shape[0]
    for cls in range(d):
        for j in range(nj):
            col = (cls * nj + j) * LANE
            dst_ref[:, col:col + LANE] = src3.at[j][_class_rows(cls, d, tm), :].astype(dst_ref.dtype)


def _token_value(src3):
    return jnp.concatenate([src3[j] for j in range(src3.shape[0])], axis=1)


def _rope_fwd(proj, cos, sin, name):
    s = proj.shape[0]
    tm = 512
    w = 3 * ATTN_W
    nj = w // LANE

    def body(*refs):
        p_refs, (c_ref, s_ref), o_refs, rot = refs[:nj], refs[nj:nj + 2], refs[nj + 2:nj + 5], refs[nj + 5]
        c, sn = c_ref[...], s_ref[...]
        for j in range(nj):
            x = p_refs[j][...]
            rot[j] = _rotate(x, c, sn) if j < 2 * ATTN_W // LANE else x
        for d, o_ref in zip(DILATIONS, o_refs):
            _to_dilated(rot, o_ref, d, tm)

    tab = pl.BlockSpec((tm, LANE), lambda i: (i, 0))
    cols = [pl.BlockSpec((tm, LANE), functools.partial(lambda i, j: (i, j), j=j)) for j in range(nj)]
    return pl.pallas_call(
        body, out_shape=tuple(jax.ShapeDtypeStruct((s // d, d * w), BF16) for d in DILATIONS),
        grid=(s // tm,), in_specs=cols + [tab, tab],
        out_specs=tuple(_dilated_spec(tm, d, w) for d in DILATIONS),
        scratch_shapes=[_token_scratch(tm, w)], name=name,
        compiler_params=_cparams(("parallel",)))(*[proj] * nj, cos, sin)


def _dproj_assemble(dqkv_list, dxr, dgate, du, cos, sin, name):
    s = dxr.shape[0]
    tm = 512
    nq = 3 * ATTN_W // LANE

    def body(*refs):
        br = refs[:9]
        dxr_ref, dg_ref, du_ref, c_ref, s_ref, o_ref = refs[9:15]
        tok = refs[15:]
        c, sn = c_ref[...], -s_ref[...]
        for part in range(3):
            for b, d in enumerate(DILATIONS[1:], start=1):
                _to_tokens(br[3 * b + part], tok[2 * part + b - 1], d, tm)
        for j in range(nq):
            part, jj = divmod(j, ATTN_W // LANE)
            x = br[part][:, jj * LANE:(jj + 1) * LANE] + tok[2 * part][jj] + tok[2 * part + 1][jj]
            if part < 2:
                x = _rotate(x, c, sn)
            o_ref[:, j * LANE:(j + 1) * LANE] = x.astype(BF16)
        o_ref[:, 3 * ATTN_W:3 * ATTN_W + LRU_W] = dxr_ref[...].astype(BF16)
        o_ref[:, 3 * ATTN_W + LRU_W:3 * ATTN_W + 2 * LRU_W] = dg_ref[...].astype(BF16)
        o_ref[:, 3 * ATTN_W + 2 * LRU_W:] = du_ref[...].astype(BF16)

    a_spec = pl.BlockSpec((tm, ATTN_W), lambda i: (i, 0))
    tab = pl.BlockSpec((tm, LANE), lambda i: (i, 0))
    ordered = [dqkv_list[b][p] for b in range(3) for p in range(3)]
    d_specs = [_dilated_spec(tm, d, ATTN_W) for d in DILATIONS for _ in range(3)]
    return pl.pallas_call(
        body, out_shape=jax.ShapeDtypeStruct((s, D_IN), BF16), grid=(s // tm,),
        in_specs=d_specs + [a_spec, a_spec, pl.BlockSpec((tm, S5_W), lambda i: (i, 0)), tab, tab],
        out_specs=pl.BlockSpec((tm, D_IN), lambda i: (i, 0)),
        scratch_shapes=[_token_scratch(tm, ATTN_W)] * 6, name=name,
        compiler_params=_cparams(("parallel",)))(*ordered, dxr, dgate, du, cos, sin)


def _attn_tiles(s, d):
    m = s // d
    tq = min(m, ATTN_TILE)
    return m, tq, tq // ATTN_BLK


def _band_mask(qb):
    qi = lax.broadcasted_iota(jnp.int32, (ATTN_BLK, 2 * ATTN_BLK), 0)
    ki = lax.broadcasted_iota(jnp.int32, (ATTN_BLK, 2 * ATTN_BLK), 1)
    dist = qi + ATTN_BLK - ki
    return (dist >= 0) & (dist <= ATTN_BLK) & ((ki >= ATTN_BLK) | (qb > 0))


def _head_cols(h):
    return (slice(h * HEAD, (h + 1) * HEAD), slice(ATTN_W + h * HEAD, ATTN_W + (h + 1) * HEAD),
            slice(2 * ATTN_W + h * HEAD, 2 * ATTN_W + (h + 1) * HEAD))


def _attn_fwd(qv, d, name):
    m = qv.shape[0]
    w3 = 3 * ATTN_W
    _, tq, n = _attn_tiles(m * d, d)
    scale = HEAD ** -0.5

    def body(x_ref, p_ref, o_ref, l_ref):
        b = pl.program_id(1)

        def block(i, first):
            r0 = 0 if first else pl.multiple_of(i * ATTN_BLK, ATTN_BLK)
            rows = pl.ds(r0, ATTN_BLK)
            valid = _band_mask(b * n + i)
            if not first:
                krows = pl.ds(pl.multiple_of(i * ATTN_BLK - ATTN_BLK, ATTN_BLK), 2 * ATTN_BLK)
            for h in range(ATTN_W // HEAD):
                qs, ks, vs = _head_cols(h)
                q = x_ref[rows, qs]
                if first:
                    k = jnp.concatenate([p_ref[:, ks], x_ref[0:ATTN_BLK, ks]], axis=0)
                    v = jnp.concatenate([p_ref[:, vs], x_ref[0:ATTN_BLK, vs]], axis=0)
                else:
                    k = x_ref[krows, ks]
                    v = x_ref[krows, vs]
                sc = jnp.where(valid, _dot_nt(q, k) * scale, -1e30)
                mx = jnp.max(sc, axis=-1, keepdims=True)
                p = jnp.exp(sc - mx)
                l = jnp.sum(p, axis=-1, keepdims=True)
                o_ref[rows, qs] = _dot(p, v) / l
                l_ref[rows, qs] = jnp.broadcast_to(mx + jnp.log(l), (ATTN_BLK, HEAD))

        block(0, True)
        if n > 1:
            def loop(i, carry):
                block(i, False)
                return carry
            lax.fori_loop(1, n, loop, 0)

    shp = jax.ShapeDtypeStruct((m, d * ATTN_W), F32)
    ospec = pl.BlockSpec((tq, ATTN_W), lambda c, b: (b, c))
    out, lse = pl.pallas_call(
        body, out_shape=(shp, shp), grid=(d, m // tq),
        in_specs=[pl.BlockSpec((tq, w3), lambda c, b: (b, c)),
                  pl.BlockSpec((ATTN_BLK, w3), lambda c, b: (jnp.maximum(b * n - 1, 0), c))],
        out_specs=(ospec, ospec), name=name,
        compiler_params=_cparams(("parallel", "parallel")))(qv, qv)
    return out, lse


def _attn_bwd(qv, ov, dov, lv, d, name):
    m = qv.shape[0]
    w3 = 3 * ATTN_W
    _, tq, n = _attn_tiles(m * d, d)
    nb = m // ATTN_BLK
    scale = HEAD ** -0.5

    def body(x_ref, p_ref, nx_ref, o_ref, do_ref, l_ref, on_ref, don_ref, ln_ref, dq_ref, dk_ref, dv_ref):
        b = pl.program_id(1)
        dk_ref[...] = jnp.zeros_like(dk_ref)
        dv_ref[...] = jnp.zeros_like(dv_ref)

        def grads(q, k, v, o, do, lse, valid):
            sc = jnp.where(valid, _dot_nt(q, k) * scale, -1e30)
            p = jnp.exp(sc - lse)
            delta = jnp.sum(do * o, axis=-1, keepdims=True)
            return p, p * (_dot_nt(do, v) - delta) * scale

        def block(i, first):
            r0 = 0 if first else pl.multiple_of(i * ATTN_BLK, ATTN_BLK)
            rows = pl.ds(r0, ATTN_BLK)
            valid = _band_mask(b * n + i)
            if not first:
                krows = pl.ds(pl.multiple_of(i * ATTN_BLK - ATTN_BLK, ATTN_BLK), 2 * ATTN_BLK)
            for h in range(ATTN_W // HEAD):
                qs, ks, vs = _head_cols(h)
                q = x_ref[rows, qs]
                do = do_ref[rows, qs]
                if first:
                    k = jnp.concatenate([p_ref[:, ks], x_ref[0:ATTN_BLK, ks]], axis=0)
                    v = jnp.concatenate([p_ref[:, vs], x_ref[0:ATTN_BLK, vs]], axis=0)
                else:
                    k = x_ref[krows, ks]
                    v = x_ref[krows, vs]
                p, ds = grads(q, k, v, o_ref[rows, qs], do, l_ref[rows, qs][:, 0:1], valid)
                dq_ref[rows, qs] = _dot(ds, k)
                if first:
                    dk_ref[0:ATTN_BLK, qs] += _dot_tn(ds[:, ATTN_BLK:], q)
                    dv_ref[0:ATTN_BLK, qs] += _dot_tn(p[:, ATTN_BLK:], do)
                else:
                    dk_ref[krows, qs] += _dot_tn(ds, q)
                    dv_ref[krows, qs] += _dot_tn(p, do)

        block(0, True)
        if n > 1:
            def loop(i, carry):
                block(i, False)
                return carry
            lax.fori_loop(1, n, loop, 0)

        last = slice((n - 1) * ATTN_BLK, n * ATTN_BLK)
        qi = lax.broadcasted_iota(jnp.int32, (ATTN_BLK, ATTN_BLK), 0)
        ki = lax.broadcasted_iota(jnp.int32, (ATTN_BLK, ATTN_BLK), 1)
        valid_next = (qi <= ki) & ((b + 1) * n < nb)
        for h in range(ATTN_W // HEAD):
            qs, ks, vs = _head_cols(h)
            q = nx_ref[:, qs]
            do = don_ref[:, qs]
            p, ds = grads(q, x_ref[last, ks], x_ref[last, vs], on_ref[:, qs], do, ln_ref[:, qs][:, 0:1],
                          valid_next)
            dk_ref[last, qs] += _dot_tn(ds, q)
            dv_ref[last, qs] += _dot_tn(p, do)

    nxt = lambda b: jnp.minimum((b + 1) * n, nb - 1)
    xs = pl.BlockSpec((tq, w3), lambda c, b: (b, c))
    xp = pl.BlockSpec((ATTN_BLK, w3), lambda c, b: (jnp.maximum(b * n - 1, 0), c))
    xn = pl.BlockSpec((ATTN_BLK, w3), lambda c, b: (nxt(b), c))
    a = pl.BlockSpec((tq, ATTN_W), lambda c, b: (b, c))
    an = pl.BlockSpec((ATTN_BLK, ATTN_W), lambda c, b: (nxt(b), c))
    shp = jax.ShapeDtypeStruct((m, d * ATTN_W), F32)
    return pl.pallas_call(
        body, out_shape=(shp, shp, shp), grid=(d, m // tq),
        in_specs=[xs, xp, xn, a, a, a, an, an, an], out_specs=(a, a, a), name=name,
        compiler_params=_cparams(("parallel", "parallel")))(qv, qv, qv, ov, dov, lv, ov, dov, lv)


def _rms(x, g):
    ms = jnp.mean(x * x, axis=-1, keepdims=True)
    return x * lax.rsqrt(ms + RMS_EPS) * g


def _rms_bwd(x, g, dy):
    ms = jnp.mean(x * x, axis=-1, keepdims=True)
    r = lax.rsqrt(ms + RMS_EPS)
    dyg = dy * g
    dx = r * dyg - x * (r * r * r) * jnp.mean(x * dyg, axis=-1, keepdims=True)
    return dx, dy * x * r


def _mix_fwd(outs, lses, lru, s5, g, name):
    s = lru.shape[0]
    tm = 256

    def body(o1, o2, o3, l1, l2, l3, lru_ref, s5_ref, g_ref, mixed_ref, mixed_t_ref, ov1, ov2, ov3,
             lv1, lv2, lv3, so2, so3, sl2, sl3):
        for d, src, dst in ((DILATIONS[1], o2, so2), (DILATIONS[2], o3, so3),
                            (DILATIONS[1], l2, sl2), (DILATIONS[2], l3, sl3)):
            _to_tokens(src, dst, d, tm)
        a1, a2, a3 = l1[...], _token_value(sl2), _token_value(sl3)
        mx = jnp.maximum(jnp.maximum(a1, a2), a3)
        e1, e2, e3 = jnp.exp(a1 - mx), jnp.exp(a2 - mx), jnp.exp(a3 - mx)
        den = e1 + e2 + e3
        o = (e1 * o1[...] + e2 * _token_value(so2) + e3 * _token_value(so3)) / den
        lse = mx + jnp.log(den)
        ov1[...] = o
        lv1[...] = lse
        for j in range(ATTN_W // LANE):
            so2[j] = o[:, j * LANE:(j + 1) * LANE]
            sl2[j] = lse[:, j * LANE:(j + 1) * LANE]
        for d, o_dst, l_dst in ((DILATIONS[1], ov2, lv2), (DILATIONS[2], ov3, lv3)):
            _to_dilated(so2, o_dst, d, tm)
            _to_dilated(sl2, l_dst, d, tm)
        gg = g_ref[...]
        mixed = jnp.concatenate([_rms(o, gg[:, :ATTN_W]),
                                 _rms(lru_ref[...], gg[:, ATTN_W:ATTN_W + LRU_W]),
                                 _rms(s5_ref[...], gg[:, ATTN_W + LRU_W:])], axis=1)
        mixed_ref[...] = mixed.astype(BF16)
        mixed_t_ref[...] = mixed.T.astype(BF16)

    a = pl.BlockSpec((tm, ATTN_W), lambda i: (i, 0))
    s5s = pl.BlockSpec((tm, S5_W), lambda i: (i, 0))
    full = pl.BlockSpec((tm, D_MODEL), lambda i: (i, 0))
    vec = pl.BlockSpec((1, D_MODEL), lambda i: (0, 0))
    dil = [_dilated_spec(tm, d, ATTN_W) for d in DILATIONS]
    dshape = [jax.ShapeDtypeStruct((s // d, d * ATTN_W), F32) for d in DILATIONS]
    res = pl.pallas_call(
        body, out_shape=(jax.ShapeDtypeStruct((s, D_MODEL), BF16), jax.ShapeDtypeStruct((D_MODEL, s), BF16),
                         *dshape, *dshape),
        grid=(s // tm,), in_specs=dil + dil + [a, s5s, vec],
        out_specs=(full, pl.BlockSpec((D_MODEL, tm), lambda i: (0, i)), *dil, *dil),
        scratch_shapes=[_token_scratch(tm, ATTN_W)] * 4, name=name,
        compiler_params=_cparams(("parallel",)))(*outs, *lses, lru, s5, g)
    return res[0], res[1], res[2:5], res[5:8]


def _mix_bwd(dmixed, o, lru, s5, g, name):
    s = lru.shape[0]
    tm = 256

    def body(dm_ref, o_ref, lru_ref, s5_ref, g_ref, do_ref, do2_ref, do3_ref, dlru_ref, ds5_ref, dg_ref, stage):
        @pl.when(pl.program_id(0) == 0)
        def _():
            dg_ref[...] = jnp.zeros_like(dg_ref)
        gg = g_ref[...]
        dm = dm_ref[...]
        dx, dgr = _rms_bwd(o_ref[...], gg[:, :ATTN_W], dm[:, :ATTN_W])
        do_ref[...] = dx
        for j in range(ATTN_W // LANE):
            stage[j] = dx[:, j * LANE:(j + 1) * LANE]
        _to_dilated(stage, do2_ref, DILATIONS[1], tm)
        _to_dilated(stage, do3_ref, DILATIONS[2], tm)
        dg_ref[:, :ATTN_W] += jnp.sum(dgr, axis=0, keepdims=True)
        dx, dgr = _rms_bwd(lru_ref[...], gg[:, ATTN_W:ATTN_W + LRU_W], dm[:, ATTN_W:ATTN_W + LRU_W])
        dlru_ref[...] = dx
        dg_ref[:, ATTN_W:ATTN_W + LRU_W] += jnp.sum(dgr, axis=0, keepdims=True)
        dx, dgr = _rms_bwd(s5_ref[...], gg[:, ATTN_W + LRU_W:], dm[:, ATTN_W + LRU_W:])
        ds5_ref[...] = dx
        dg_ref[:, ATTN_W + LRU_W:] += jnp.sum(dgr, axis=0, keepdims=True)

    a = pl.BlockSpec((tm, ATTN_W), lambda i: (i, 0))
    s5s = pl.BlockSpec((tm, S5_W), lambda i: (i, 0))
    full = pl.BlockSpec((tm, D_MODEL), lambda i: (i, 0))
    vec = pl.BlockSpec((1, D_MODEL), lambda i: (0, 0))
    dil = [_dilated_spec(tm, d, ATTN_W) for d in DILATIONS]
    dshape = [jax.ShapeDtypeStruct((s // d, d * ATTN_W), F32) for d in DILATIONS]
    res = pl.pallas_call(
        body, out_shape=(*dshape, jax.ShapeDtypeStruct((s, LRU_W), F32),
                         jax.ShapeDtypeStruct((s, S5_W), F32), jax.ShapeDtypeStruct((1, D_MODEL), F32)),
        grid=(s // tm,), in_specs=[full, a, a, s5s, vec], out_specs=(*dil, a, s5s, vec),
        scratch_shapes=[_token_scratch(tm, ATTN_W)], name=name,
        compiler_params=_cparams(("arbitrary",)))(dmixed, o, lru, s5, g)
    return res[0:3], res[3], res[4], res[5]


def _lru_gate_math(xc, pre_r, pre_i, lam):
    r = _sigmoid(pre_r)
    i = _sigmoid(pre_i)
    log_a = -LRU_C * r * _softplus(-lam)
    a = jnp.exp(log_a)
    u = jnp.sqrt(-_expm1(2.0 * log_a)) * (i * xc)
    return a, u


def _lru_conv(x, prev8, cw, cb):
    y = cb + cw[LRU_CONV - 1:LRU_CONV, :] * x
    for k in range(LRU_CONV - 1):
        y = y + cw[k:k + 1, :] * _shift_down_prev(x, LRU_CONV - 1 - k, prev8)
    return y


def _lru_specs(s):
    xo = 3 * ATTN_W // LANE
    go = xo + LRU_W // LANE
    xr = pl.BlockSpec((s, LANE), lambda j: (0, xo + j))
    gt = pl.BlockSpec((s, LANE), lambda j: (0, go + j))
    cw = pl.BlockSpec((LRU_CONV, LANE), lambda j: (0, j))
    vec = pl.BlockSpec((1, LANE), lambda j: (0, j))
    wbd = pl.BlockSpec((LANE, LANE), lambda j: (j, j))
    col = pl.BlockSpec((s, LANE), lambda j: (0, j))
    return xr, gt, cw, vec, wbd, col


def _lru_fwd(proj, cw, cb, wr, br, wi, bi, lam, name):
    s = proj.shape[0]
    t = SCAN_T

    def body(xr_ref, gt_ref, cw_ref, cb_ref, wr_ref, br_ref, wi_ref, bi_ref, lam_ref, o_ref):
        cwv, cbv, lamv = cw_ref[...], cb_ref[...], lam_ref[...]
        wrv, wiv, brv, biv = wr_ref[...], wi_ref[...], br_ref[...], bi_ref[...]

        def chunk(c, carry):
            h_c, prev8 = carry
            rows = pl.ds(pl.multiple_of(c * t, t), t)
            x = xr_ref[rows, :]
            xc = _lru_conv(x, prev8, cwv, cbv)
            a, u = _lru_gate_math(xc, _dot(xc, wrv) + brv, _dot(xc, wiv) + biv, lamv)
            h = _scan_chunk(a, u, h_c)
            o_ref[rows, :] = h * _gelu(gt_ref[rows, :])
            return h[t - 1:t, :], x[t - 8:t, :]

        lax.fori_loop(0, s // t, chunk, (jnp.zeros((1, LANE), F32), jnp.zeros((8, LANE), F32)))

    xr, gt, cws, vec, wbd, col = _lru_specs(s)
    return pl.pallas_call(
        body, out_shape=jax.ShapeDtypeStruct((s, LRU_W), F32), grid=(LRU_W // LANE,),
        in_specs=[xr, gt, cws, vec, wbd, vec, wbd, vec, vec], out_specs=col, name=name,
        compiler_params=_cparams(("parallel",)))(proj, proj, cw, cb, wr, br, wi, bi, lam)


def _lru_bwd(proj, dout, cw, cb, wr, br, wi, bi, lam, name):
    s = proj.shape[0]
    t = SCAN_T
    nc = s // t

    def body(xr_ref, gt_ref, do_ref, cw_ref, cb_ref, wr_ref, br_ref, wi_ref, bi_ref, lam_ref,
             dxr_ref, dgt_ref, dcw_ref, dcb_ref, dwr_ref, dbr_ref, dwi_ref, dbi_ref, dlam_ref,
             xc_s, a_s, h_s):
        cwv, cbv, lamv = cw_ref[...], cb_ref[...], lam_ref[...]
        wrv, wiv, brv, biv = wr_ref[...], wi_ref[...], br_ref[...], bi_ref[...]

        def fchunk(c, carry):
            h_c, prev8 = carry
            rows = pl.ds(pl.multiple_of(c * t, t), t)
            x = xr_ref[rows, :]
            xc = _lru_conv(x, prev8, cwv, cbv)
            a, u = _lru_gate_math(xc, _dot(xc, wrv) + brv, _dot(xc, wiv) + biv, lamv)
            h = _scan_chunk(a, u, h_c)
            xc_s[rows, :] = xc
            a_s[rows, :] = a
            h_s[rows, :] = h
            return h[t - 1:t, :], x[t - 8:t, :]

        lax.fori_loop(0, nc, fchunk, (jnp.zeros((1, LANE), F32), jnp.zeros((8, LANE), F32)))

        z1 = jnp.zeros((1, LANE), F32)
        zw = jnp.zeros((LANE, LANE), F32)

        def bchunk(ci, carry):
            g_next, a_next, dxc_next8, dcw, dcb, dwr, dbr, dwi, dbi, dlam = carry
            c = nc - 1 - ci
            t0 = pl.multiple_of(c * t, t)
            rows = pl.ds(t0, t)
            before = pl.ds(pl.multiple_of(jnp.maximum(t0 - 8, 0), 8), 8)
            has_prev = (c > 0).astype(F32)
            x, gt, do = xr_ref[rows, :], gt_ref[rows, :], do_ref[rows, :]
            xc, a, h = xc_s[rows, :], a_s[rows, :], h_s[rows, :]
            prev8_x = xr_ref[before, :] * has_prev
            prev8_h = h_s[before, :] * has_prev
            dgt_ref[rows, :] = do * h * _gelu_grad(gt)
            dh = do * _gelu(gt)
            a_plus = _shift_up_next(a, 1, jnp.broadcast_to(a_next, (8, LANE)))
            g = _scan_chunk(a_plus, dh, g_next, reverse=True)
            da = g * _shift_down_prev(h, 1, prev8_h)
            pre_r = _dot(xc, wrv) + brv
            pre_i = _dot(xc, wiv) + biv
            _, vjp = jax.vjp(_lru_gate_math, xc, pre_r, pre_i, lamv)
            dxc, dpre_r, dpre_i, dlam_c = vjp((da, g))
            dxc = dxc + _dot_nt(dpre_r, wrv) + _dot_nt(dpre_i, wiv)
            dx = cwv[LRU_CONV - 1:LRU_CONV, :] * dxc
            dcw_rows = [None] * LRU_CONV
            dcw_rows[LRU_CONV - 1] = jnp.sum(dxc * x, axis=0, keepdims=True)
            for k in range(LRU_CONV - 1):
                sh = LRU_CONV - 1 - k
                dx = dx + cwv[k:k + 1, :] * _shift_up_next(dxc, sh, dxc_next8)
                dcw_rows[k] = jnp.sum(dxc * _shift_down_prev(x, sh, prev8_x), axis=0, keepdims=True)
            dxr_ref[rows, :] = dx
            return (g[0:1, :], a[0:1, :], dxc[0:8, :],
                    dcw + jnp.concatenate(dcw_rows, axis=0),
                    dcb + jnp.sum(dxc, axis=0, keepdims=True),
                    dwr + _dot_tn(xc, dpre_r), dbr + jnp.sum(dpre_r, axis=0, keepdims=True),
                    dwi + _dot_tn(xc, dpre_i), dbi + jnp.sum(dpre_i, axis=0, keepdims=True),
                    dlam + dlam_c)

        init = (z1, z1, jnp.zeros((8, LANE), F32), jnp.zeros((LRU_CONV, LANE), F32), z1, zw, z1, zw, z1, z1)
        res = lax.fori_loop(0, nc, bchunk, init)
        dcw_ref[...] = res[3]
        dcb_ref[...] = res[4]
        dwr_ref[...] = res[5]
        dbr_ref[...] = res[6]
        dwi_ref[...] = res[7]
        dbi_ref[...] = res[8]
        dlam_ref[...] = res[9]

    xr, gt, cws, vec, wbd, col = _lru_specs(s)
    vshape = jax.ShapeDtypeStruct((1, LRU_W), F32)
    wshape = jax.ShapeDtypeStruct((LRU_W, LRU_W), F32)
    return pl.pallas_call(
        body,
        out_shape=(jax.ShapeDtypeStruct((s, LRU_W), F32), jax.ShapeDtypeStruct((s, LRU_W), F32),
                   jax.ShapeDtypeStruct((LRU_CONV, LRU_W), F32), vshape, wshape, vshape, wshape, vshape, vshape),
        grid=(LRU_W // LANE,),
        in_specs=[xr, gt, col, cws, vec, wbd, vec, wbd, vec, vec],
        out_specs=(col, col, cws, vec, wbd, vec, wbd, vec, vec),
        scratch_shapes=[pltpu.VMEM((s, LANE), F32)] * 3, name=name,
        compiler_params=_cparams(("parallel",)))(proj, proj, dout, cw, cb, wr, br, wi, bi, lam)


def _s5_disc_math(a_re, a_im, log_step, bt_re, bt_im):
    step = jnp.exp(log_step)
    dt_re, dt_im = step * a_re, step * a_im
    mag = jnp.exp(dt_re)
    ab_re, ab_im = mag * jnp.cos(dt_im), mag * jnp.sin(dt_im)
    z_re, z_im = ab_re - 1.0, ab_im
    den = a_re * a_re + a_im * a_im
    f_re = (z_re * a_re + z_im * a_im) / den
    f_im = (z_im * a_re - z_re * a_im) / den
    bb_re = f_re * bt_re - f_im * bt_im
    bb_im = f_re * bt_im + f_im * bt_re
    return ab_re, ab_im, bb_re, bb_im


def _s5_disc_fwd(a_re, a_im, log_step, bt_re, bt_im, name):
    def body(ar, ai, ls, br, bi, o1, o2, o3, o4):
        r = _s5_disc_math(ar[...], ai[...], ls[...], br[...], bi[...])
        o1[...], o2[...], o3[...], o4[...] = r

    shp = jax.ShapeDtypeStruct(a_re.shape, F32)
    return pl.pallas_call(body, out_shape=(shp,) * 4, name=name)(a_re, a_im, log_step, bt_re, bt_im)


def _s5_disc_bwd(a_re, a_im, log_step, bt_re, bt_im, cts, name):
    def body(ar, ai, ls, br, bi, c1, c2, c3, c4, o1, o2, o3, o4, o5):
        _, vjp = jax.vjp(_s5_disc_math, ar[...], ai[...], ls[...], br[...], bi[...])
        r = vjp((c1[...], c2[...], c3[...], c4[...]))
        o1[...], o2[...], o3[...], o4[...], o5[...] = r

    shp = jax.ShapeDtypeStruct(a_re.shape, F32)
    return pl.pallas_call(body, out_shape=(shp,) * 5, name=name)(a_re, a_im, log_step, bt_re, bt_im, *cts)


def _s5_u_specs(s):
    uo = (3 * ATTN_W + 2 * LRU_W) // LANE
    return (pl.BlockSpec((s, LANE), lambda j: (0, uo)), pl.BlockSpec((s, LANE), lambda j: (0, uo + 1)))


def _s5_scan_fwd(proj, b_re, b_im, lam_re, lam_im, c_re, c_im, name):
    s = proj.shape[0]
    t = SCAN_T

    def body(u0_ref, u1_ref, bre_ref, bim_ref, lre_ref, lim_ref, cre_ref, cim_ref, xre_ref, xim_ref, y_ref):
        @pl.when(pl.program_id(0) == 0)
        def _():
            y_ref[...] = jnp.zeros_like(y_ref)
        lr, li = lre_ref[...], lim_ref[...]
        consts = _cscan_consts(lr, li, False)
        bre, bim, cre, cim = bre_ref[...], bim_ref[...], cre_ref[...], cim_ref[...]

        def chunk(c, carry):
            cr, ci = carry
            rows = pl.ds(pl.multiple_of(c * t, t), t)
            u = jnp.concatenate([u0_ref[rows, :], u1_ref[rows, :]], axis=1).astype(BF16)
            xr, xi = _cscan_chunk(_dot(u, bre), _dot(u, bim), consts, (cr, ci))
            xre_ref[rows, :] = xr
            xim_ref[rows, :] = xi
            y_ref[rows, :] += _dot(xr, cre) - _dot(xi, cim)
            return xr[t - 1:t, :], xi[t - 1:t, :]

        z = jnp.zeros((1, S5_BLK), F32)
        lax.fori_loop(0, s // t, chunk, (z, z))

    u0, u1 = _s5_u_specs(s)
    bsp = pl.BlockSpec((S5_W, S5_BLK), lambda j: (0, j))
    csp = pl.BlockSpec((S5_BLK, S5_W), lambda j: (j, 0))
    vec = pl.BlockSpec((1, S5_BLK), lambda j: (0, j))
    xsp = pl.BlockSpec((s, S5_BLK), lambda j: (0, j))
    ysp = pl.BlockSpec((s, S5_W), lambda j: (0, 0))
    xshape = jax.ShapeDtypeStruct((s, S5_STATES), F32)
    return pl.pallas_call(
        body, out_shape=(xshape, xshape, jax.ShapeDtypeStruct((s, S5_W), F32)),
        grid=(S5_STATES // S5_BLK,), in_specs=[u0, u1, bsp, bsp, vec, vec, csp, csp],
        out_specs=(xsp, xsp, ysp), name=name,
        compiler_params=_cparams(("arbitrary",)))(proj, proj, b_re, b_im, lam_re, lam_im, c_re, c_im)


def _s5_scan_bwd(proj, dy, du_init, x_re, x_im, b_re, b_im, lam_re, lam_im, c_re, c_im, name):
    s = proj.shape[0]
    t = SCAN_T
    nc = s // t

    def body(u0_ref, u1_ref, dy_ref, dui_ref, xre_ref, xim_ref, bre_ref, bim_ref, lre_ref, lim_ref,
             cre_ref, cim_ref, du_ref, dlr_ref, dli_ref, dbr_ref, dbi_ref, dcr_ref, dci_ref):
        @pl.when(pl.program_id(0) == 0)
        def _():
            du_ref[...] = dui_ref[...]
        mr, mi = lre_ref[...], -lim_ref[...]
        consts = _cscan_consts(mr, mi, True)
        bre, bim, cre, cim = bre_ref[...], bim_ref[...], cre_ref[...], cim_ref[...]
        dbr_ref[...] = jnp.zeros_like(dbr_ref)
        dbi_ref[...] = jnp.zeros_like(dbi_ref)
        dcr_ref[...] = jnp.zeros_like(dcr_ref)
        dci_ref[...] = jnp.zeros_like(dci_ref)

        def chunk(ci_, carry):
            gnr, gni, dlr, dli = carry
            c = nc - 1 - ci_
            t0 = pl.multiple_of(c * t, t)
            rows = pl.ds(t0, t)
            before = pl.ds(pl.multiple_of(jnp.maximum(t0 - 8, 0), 8), 8)
            has_prev = (c > 0).astype(F32)
            dyc = dy_ref[rows, :].astype(BF16)
            u = jnp.concatenate([u0_ref[rows, :], u1_ref[rows, :]], axis=1).astype(BF16)
            gr, gi = _cscan_chunk(_dot_nt(dyc, cre), -_dot_nt(dyc, cim), consts, (gnr, gni), reverse=True)
            xr, xi = xre_ref[rows, :], xim_ref[rows, :]
            xpr = _shift_down_prev(xr, 1, xre_ref[before, :] * has_prev)
            xpi = _shift_down_prev(xi, 1, xim_ref[before, :] * has_prev)
            dlr = dlr + jnp.sum(gr * xpr + gi * xpi, axis=0, keepdims=True)
            dli = dli + jnp.sum(gi * xpr - gr * xpi, axis=0, keepdims=True)
            du_ref[rows, :] += _dot_nt(gr, bre) + _dot_nt(gi, bim)
            dbr_ref[...] += _dot_tn(u, gr)
            dbi_ref[...] += _dot_tn(u, gi)
            dcr_ref[...] += _dot_tn(xr, dyc)
            dci_ref[...] -= _dot_tn(xi, dyc)
            return gr[0:1, :], gi[0:1, :], dlr, dli

        z = jnp.zeros((1, S5_BLK), F32)
        res = lax.fori_loop(0, nc, chunk, (z, z, z, z))
        dlr_ref[...] = res[2]
        dli_ref[...] = res[3]

    u0, u1 = _s5_u_specs(s)
    bsp = pl.BlockSpec((S5_W, S5_BLK), lambda j: (0, j))
    csp = pl.BlockSpec((S5_BLK, S5_W), lambda j: (j, 0))
    vec = pl.BlockSpec((1, S5_BLK), lambda j: (0, j))
    xsp = pl.BlockSpec((s, S5_BLK), lambda j: (0, j))
    ysp = pl.BlockSpec((s, S5_W), lambda j: (0, 0))
    return pl.pallas_call(
        body,
        out_shape=(jax.ShapeDtypeStruct((s, S5_W), F32),
                   jax.ShapeDtypeStruct((1, S5_STATES), F32), jax.ShapeDtypeStruct((1, S5_STATES), F32),
                   jax.ShapeDtypeStruct((S5_W, S5_STATES), F32), jax.ShapeDtypeStruct((S5_W, S5_STATES), F32),
                   jax.ShapeDtypeStruct((S5_STATES, S5_W), F32), jax.ShapeDtypeStruct((S5_STATES, S5_W), F32)),
        grid=(S5_STATES // S5_BLK,),
        in_specs=[u0, u1, ysp, ysp, xsp, xsp, bsp, bsp, vec, vec, csp, csp],
        out_specs=(ysp, vec, vec, bsp, bsp, csp, csp), name=name,
        compiler_params=_cparams(("arbitrary",)))(
            proj, proj, dy, du_init, x_re, x_im, b_re, b_im, lam_re, lam_im, c_re, c_im)


def _s5_out_fwd(proj, y_acc, dvec, w_glu, b_glu, name):
    s = proj.shape[0]
    tm = 512
    uo = (3 * ATTN_W + 2 * LRU_W) // LANE

    def body(u0_ref, u1_ref, y_ref, d_ref, w_ref, b_ref, o_ref, yp_ref):
        u = jnp.concatenate([u0_ref[...], u1_ref[...]], axis=1)
        y = y_ref[...] + d_ref[...] * u
        yp_ref[...] = y
        yg = _gelu(y)
        o_ref[...] = yg * _sigmoid(_dot(yg, w_ref[...]) + b_ref[...])

    u0 = pl.BlockSpec((tm, LANE), lambda i: (i, uo))
    u1 = pl.BlockSpec((tm, LANE), lambda i: (i, uo + 1))
    row = pl.BlockSpec((tm, S5_W), lambda i: (i, 0))
    vec = pl.BlockSpec((1, S5_W), lambda i: (0, 0))
    wsp = pl.BlockSpec((S5_W, S5_W), lambda i: (0, 0))
    shp = jax.ShapeDtypeStruct((s, S5_W), F32)
    return pl.pallas_call(
        body, out_shape=(shp, shp), grid=(s // tm,), in_specs=[u0, u1, row, vec, wsp, vec],
        out_specs=(row, row), name=name,
        compiler_params=_cparams(("parallel",)))(proj, proj, y_acc, dvec, w_glu, b_glu)


def _s5_out_bwd(proj, y_pre, dout, dvec, w_glu, b_glu, name):
    s = proj.shape[0]
    tm = 512
    uo = (3 * ATTN_W + 2 * LRU_W) // LANE

    def body(u0_ref, u1_ref, y_ref, do_ref, d_ref, w_ref, b_ref, dy_ref, dud_ref, dd_ref, dw_ref, db_ref):
        @pl.when(pl.program_id(0) == 0)
        def _():
            dd_ref[...] = jnp.zeros_like(dd_ref)
            dw_ref[...] = jnp.zeros_like(dw_ref)
            db_ref[...] = jnp.zeros_like(db_ref)
        u = jnp.concatenate([u0_ref[...], u1_ref[...]], axis=1)
        y = y_ref[...]
        do = do_ref[...]
        yg = _gelu(y)
        sg = _sigmoid(_dot(yg, w_ref[...]) + b_ref[...])
        dz = do * yg * sg * (1.0 - sg)
        dyg = do * sg + _dot_nt(dz, w_ref[...])
        dy = dyg * _gelu_grad(y)
        dy_ref[...] = dy
        dud_ref[...] = d_ref[...] * dy
        dd_ref[...] += jnp.sum(dy * u, axis=0, keepdims=True)
        dw_ref[...] += _dot_tn(yg, dz)
        db_ref[...] += jnp.sum(dz, axis=0, keepdims=True)

    u0 = pl.BlockSpec((tm, LANE), lambda i: (i, uo))
    u1 = pl.BlockSpec((tm, LANE), lambda i: (i, uo + 1))
    row = pl.BlockSpec((tm, S5_W), lambda i: (i, 0))
    vec = pl.BlockSpec((1, S5_W), lambda i: (0, 0))
    wsp = pl.BlockSpec((S5_W, S5_W), lambda i: (0, 0))
    shp = jax.ShapeDtypeStruct((s, S5_W), F32)
    vshape = jax.ShapeDtypeStruct((1, S5_W), F32)
    return pl.pallas_call(
        body, out_shape=(shp, shp, vshape, jax.ShapeDtypeStruct((S5_W, S5_W), F32), vshape),
        grid=(s // tm,), in_specs=[u0, u1, row, row, vec, wsp, vec],
        out_specs=(row, row, vec, wsp, vec), name=name,
        compiler_params=_cparams(("arbitrary",)))(proj, proj, y_pre, dout, dvec, w_glu, b_glu)


def _ffn_conv(x, prev8, cw, cb):
    y = cb + cw[FFN_CONV - 1:FFN_CONV, :] * x
    for k in range(FFN_CONV - 1):
        y = y + cw[k:k + 1, :] * _shift_down_prev(x, FFN_CONV - 1 - k, prev8)
    return y


def _ffn_act_fwd(up, cw, cb, name):
    s = up.shape[0]
    tm = 256
    tb = 2 * FFN_CB

    def body(x_ref, p_ref, cw_ref, cb_ref, o_ref, ot_ref):
        prev8 = p_ref[...] * (pl.program_id(1) > 0).astype(F32)
        y = _ffn_conv(x_ref[...], prev8, cw_ref[...], cb_ref[...])
        act = _gelu(y[:, :FFN_CB]) * y[:, FFN_CB:]
        o_ref[...] = act.astype(BF16)
        ot_ref[...] = act.T.astype(BF16)

    main = pl.BlockSpec((tm, tb), lambda j, i: (i, j))
    prev = pl.BlockSpec((8, tb), lambda j, i: (jnp.maximum(i * (tm // 8) - 1, 0), j))
    return pl.pallas_call(
        body, out_shape=(jax.ShapeDtypeStruct((s, D_FF), BF16), jax.ShapeDtypeStruct((D_FF, s), BF16)),
        grid=(D_FF // FFN_CB, s // tm),
        in_specs=[main, prev, pl.BlockSpec((FFN_CONV, tb), lambda j, i: (0, j)),
                  pl.BlockSpec((1, tb), lambda j, i: (0, j))],
        out_specs=(pl.BlockSpec((tm, FFN_CB), lambda j, i: (i, j)), pl.BlockSpec((FFN_CB, tm), lambda j, i: (j, i))),
        name=name, compiler_params=_cparams(("parallel", "parallel")))(up, up, cw, cb)


def _ffn_act_bwd(up, dact, cw, cb, name):
    s = up.shape[0]
    tm = 256
    tb = 2 * FFN_CB
    nr = s // tm

    def body(x_ref, p_ref, n_ref, da_ref, dan_ref, cw_ref, cb_ref, dup_ref, dcw_ref, dcb_ref):
        i = pl.program_id(1)

        @pl.when(i == 0)
        def _():
            dcw_ref[...] = jnp.zeros_like(dcw_ref)
            dcb_ref[...] = jnp.zeros_like(dcb_ref)
        has_next = (i < nr - 1).astype(F32)
        prev8 = p_ref[...] * (i > 0).astype(F32)
        cwv = cw_ref[...]
        x = x_ref[...]
        xe = jnp.concatenate([x, n_ref[...]], axis=0)
        dae = jnp.concatenate([da_ref[...], dan_ref[...] * has_next], axis=0)
        shifted = [_shift_down_prev(xe, FFN_CONV - 1 - k, prev8) for k in range(FFN_CONV - 1)]
        y = cb_ref[...] + cwv[FFN_CONV - 1:FFN_CONV, :] * xe
        for k in range(FFN_CONV - 1):
            y = y + cwv[k:k + 1, :] * shifted[k]
        gate, val = y[:, :FFN_CB], y[:, FFN_CB:]
        dy = jnp.concatenate([dae * val * _gelu_grad(gate), dae * _gelu(gate)], axis=1)
        dym = dy[:tm, :]
        dx = cwv[FFN_CONV - 1:FFN_CONV, :] * dym
        dcw_rows = [None] * FFN_CONV
        dcw_rows[FFN_CONV - 1] = jnp.sum(dym * x, axis=0, keepdims=True)
        for k in range(FFN_CONV - 1):
            sh = FFN_CONV - 1 - k
            dx = dx + cwv[k:k + 1, :] * pltpu.roll(dy, tm + 8 - sh, axis=0)[:tm, :]
            dcw_rows[k] = jnp.sum(dym * shifted[k][:tm, :], axis=0, keepdims=True)
        dup_ref[...] = dx.astype(BF16)
        dcw_ref[...] += jnp.concatenate(dcw_rows, axis=0)
        dcb_ref[...] += jnp.sum(dym, axis=0, keepdims=True)

    main = pl.BlockSpec((tm, tb), lambda j, i: (i, j))
    prev = pl.BlockSpec((8, tb), lambda j, i: (jnp.maximum(i * (tm // 8) - 1, 0), j))
    nxt = pl.BlockSpec((8, tb), lambda j, i: (jnp.minimum((i + 1) * (tm // 8), s // 8 - 1), j))
    da = pl.BlockSpec((tm, FFN_CB), lambda j, i: (i, j))
    dan = pl.BlockSpec((8, FFN_CB), lambda j, i: (jnp.minimum((i + 1) * (tm // 8), s // 8 - 1), j))
    cws = pl.BlockSpec((FFN_CONV, tb), lambda j, i: (0, j))
    cbs = pl.BlockSpec((1, tb), lambda j, i: (0, j))
    return pl.pallas_call(
        body, out_shape=(jax.ShapeDtypeStruct((s, 2 * D_FF), BF16),
                         jax.ShapeDtypeStruct((FFN_CONV, 2 * D_FF), F32),
                         jax.ShapeDtypeStruct((1, 2 * D_FF), F32)),
        grid=(D_FF // FFN_CB, nr), in_specs=[main, prev, nxt, da, dan, cws, cbs],
        out_specs=(main, cws, cbs), name=name,
        compiler_params=_cparams(("parallel", "arbitrary")))(up, up, up, dact, dact, cw, cb)


def _adamw_sum(landed, w, m, v, layer, prev, name):
    _, r, c = landed.shape
    nl = w.shape[0]
    tm = 8
    for cand in (512, 256, 128, 64, 32, 16):
        if r % cand == 0 and N_DEV * cand * c * 4 <= 4 * 1024 * 1024:
            tm = cand
            break

    def body(*refs):
        ld_ref, w_ref, m_ref, v_ref = refs[:4]
        g_ref, d_ref, mo_ref, vo_ref = refs[-4:]
        gg = ld_ref[0].astype(F32)
        for k in range(1, N_DEV):
            gg = gg + ld_ref[k].astype(F32)
        mn = ADAM_B1 * m_ref[...] + (1.0 - ADAM_B1) * gg
        vn = ADAM_B2 * v_ref[...] + (1.0 - ADAM_B2) * (gg * gg)
        m_hat = mn / (1.0 - ADAM_B1 ** ADAM_STEP)
        v_hat = vn / (1.0 - ADAM_B2 ** ADAM_STEP)
        g_ref[...] = gg
        d_ref[...] = -ADAM_LR * (m_hat / (jnp.sqrt(v_hat) + ADAM_EPS) + ADAM_WD * w_ref[...])
        mo_ref[...] = mn
        vo_ref[...] = vn

    blk = pl.BlockSpec((None, tm, c), lambda i: (layer, i, 0))
    in_specs = [pl.BlockSpec((N_DEV, tm, c), lambda i: (0, i, 0)), blk, blk, blk]
    args = [landed, w, m, v]
    aliases = {}
    if prev is not None:
        in_specs += [pl.BlockSpec(memory_space=pl.ANY)] * 4
        args += list(prev)
        aliases = {4 + k: k for k in range(4)}
    shp = jax.ShapeDtypeStruct((nl, r, c), F32)
    return pl.pallas_call(
        body, out_shape=(shp,) * 4, grid=(r // tm,), in_specs=in_specs, out_specs=(blk,) * 4,
        input_output_aliases=aliases, name=name, compiler_params=_cparams(("parallel",)))(*args)


def _all_gather(shards, name):
    na = len(shards)

    def body(*refs):
        x_refs, out_refs = refs[:na], refs[na:2 * na]
        send_sems, recv_sems, local_sems = refs[2 * na:]
        x, y, c = lax.axis_index("x"), lax.axis_index("y"), lax.axis_index("c")
        me, sibling = (x, y, c), (x, y, 1 - c)
        chips = [(1 - x, y), (x, 1 - y), (1 - x, 1 - y)]

        def copy(a, k, block, to, src=None):
            dst = out_refs[a].at[4 * block[0] + 2 * block[1] + block[2]]
            return pltpu.make_async_remote_copy(
                src_ref=dst if src is None else src, dst_ref=dst,
                send_sem=send_sems.at[7 * a + k], recv_sem=recv_sems.at[7 * a + k],
                device_id=to, device_id_type=pl.DeviceIdType.MESH)

        mine, first, passed = [], [], []
        for a in range(na):
            cp = pltpu.make_async_copy(x_refs[a], out_refs[a].at[4 * x + 2 * y + c], local_sems.at[a])
            cp.start()
            mine.append(cp)
            cps = [copy(a, 0, me, sibling, src=x_refs[a])]
            cps += [copy(a, 1 + j, me, (*chip, c), src=x_refs[a]) for j, chip in enumerate(chips)]
            for cp in cps:
                cp.start()
            first += cps
        for j, chip in enumerate(chips):
            for a in range(na):
                copy(a, 1 + j, (*chip, c), me).wait_recv()
                cp = copy(a, 4 + j, (*chip, c), sibling)
                cp.start()
                passed.append(cp)
        for a in range(na):
            copy(a, 0, sibling, me).wait_recv()
            for j, chip in enumerate(chips):
                copy(a, 4 + j, (*chip, 1 - c), me).wait_recv()
        for cp in first + passed:
            cp.wait_send()
        for cp in mine:
            cp.wait()

    anyspec = pl.BlockSpec(memory_space=pl.ANY)
    return pl.pallas_call(
        body, out_shape=tuple(jax.ShapeDtypeStruct((N_DEV,) + t.shape, t.dtype) for t in shards),
        in_specs=[anyspec] * na, out_specs=tuple([anyspec] * na),
        scratch_shapes=[pltpu.SemaphoreType.DMA((7 * na,)), pltpu.SemaphoreType.DMA((7 * na,)),
                        pltpu.SemaphoreType.DMA((na,))],
        name=name)(*shards)


def _all_to_all(bufs, name):
    na = len(bufs)

    def body(*refs):
        b_refs, out_refs = refs[:na], refs[na:2 * na]
        send_sems, recv_sems, local_sems = refs[2 * na:]
        x, y, c = lax.axis_index("x"), lax.axis_index("y"), lax.axis_index("c")
        me = 4 * x + 2 * y + c
        copies = []
        for a in range(na):
            cp = pltpu.make_async_copy(b_refs[a].at[me], out_refs[a].at[me], local_sems.at[a])
            cp.start()
            copies.append(cp)
        for k in range(1, N_DEV):
            px = x ^ ((k >> 2) & 1)
            py = y ^ ((k >> 1) & 1)
            pc = c ^ (k & 1)
            for a in range(na):
                cp = pltpu.make_async_remote_copy(
                    src_ref=b_refs[a].at[4 * px + 2 * py + pc], dst_ref=out_refs[a].at[me],
                    send_sem=send_sems.at[7 * a + k - 1], recv_sem=recv_sems.at[7 * a + k - 1],
                    device_id=(px, py, pc), device_id_type=pl.DeviceIdType.MESH)
                cp.start()
                copies.append(cp)
        for cp in copies:
            cp.wait()

    anyspec = pl.BlockSpec(memory_space=pl.ANY)
    return pl.pallas_call(
        body, out_shape=tuple(jax.ShapeDtypeStruct(t.shape, t.dtype) for t in bufs),
        in_specs=[anyspec] * na, out_specs=tuple([anyspec] * na),
        scratch_shapes=[pltpu.SemaphoreType.DMA((7 * na,)), pltpu.SemaphoreType.DMA((7 * na,)),
                        pltpu.SemaphoreType.DMA((na,))],
        name=name)(*bufs)


_HBM = pl.BlockSpec(memory_space=pltpu.HBM)
_SEM = pl.BlockSpec(memory_space=pltpu.SEMAPHORE)
_EFFECT = pltpu.SideEffectType.DATAFLOW_SIDE_EFFECTING


def _exchange_copies(src_refs, land_refs, send_sems, recv_sems, local_sems, gather):
    x, y, c = lax.axis_index("x"), lax.axis_index("y"), lax.axis_index("c")
    me = 4 * x + 2 * y + c
    local, remote = [], []
    for a, (src, land) in enumerate(zip(src_refs, land_refs)):
        local.append(pltpu.make_async_copy(src if gather else src.at[me], land.at[me], local_sems.at[a]))
    for k in range(1, N_DEV):
        px = x ^ ((k >> 2) & 1)
        py = y ^ ((k >> 1) & 1)
        pc = c ^ (k & 1)
        for a, (src, land) in enumerate(zip(src_refs, land_refs)):
            remote.append(pltpu.make_async_remote_copy(
                src_ref=src if gather else src.at[4 * px + 2 * py + pc], dst_ref=land.at[me],
                send_sem=send_sems.at[7 * a + k - 1], recv_sem=recv_sems.at[7 * a + k - 1],
                device_id=(px, py, pc), device_id_type=pl.DeviceIdType.MESH))
    return local, remote


def _exchange_start(srcs, gather, name, dep=None):
    na = len(srcs)
    lands = [lax.empty(((N_DEV,) + t.shape) if gather else t.shape, t.dtype) for t in srcs]

    def body(*refs):
        src_refs, land_refs = refs[:na], refs[na:2 * na]
        nin = 2 * na + (0 if dep is None else 1)
        send_sems, recv_sems, local_sems = refs[nin:nin + 3]
        token = refs[-1]
        local, remote = _exchange_copies(src_refs, land_refs, send_sems, recv_sems, local_sems, gather)
        for cp in local + remote:
            cp.start()
        token[...] = jnp.zeros_like(token)

    dep_specs, dep_ops = _dep_args(dep)
    hbm = lambda t: pltpu.HBM(t.shape, t.dtype)
    out = pl.pallas_call(
        body, name=name,
        out_shape=(pltpu.SemaphoreType.DMA((7 * na,)), pltpu.SemaphoreType.DMA((7 * na,)),
                   pltpu.SemaphoreType.DMA((na,)), *[hbm(t) for t in srcs], *[hbm(t) for t in lands],
                   jax.ShapeDtypeStruct((8, LANE), F32)),
        in_specs=[_HBM] * (2 * na) + dep_specs,
        out_specs=(_SEM, _SEM, _SEM, *[_HBM] * (2 * na), pl.BlockSpec(memory_space=pltpu.VMEM)),
        input_output_aliases={i: 3 + i for i in range(2 * na)},
        compiler_params=pltpu.CompilerParams(has_side_effects=_EFFECT),
    )(*[pltpu.with_memory_space_constraint(t, pltpu.HBM) for t in srcs + lands], *dep_ops)
    return (out[:3], out[3:3 + na], out[3 + na:3 + 2 * na]), out[-1]


def _exchange_wait(handle, gather, after, name):
    sems, srcs, lands = handle
    na = len(srcs)

    def body(*refs):
        src_refs, land_refs = refs[:na], refs[na:2 * na]
        send_sems, recv_sems, local_sems = refs[2 * na:2 * na + 3]
        local, remote = _exchange_copies(src_refs, land_refs, send_sems, recv_sems, local_sems, gather)
        for cp in remote:
            cp.wait_send()
            cp.wait_recv()
        for cp in local:
            cp.wait()

    hbm = lambda t: pltpu.HBM(t.shape, t.dtype)
    out = pl.pallas_call(
        body, name=name, out_shape=(*[hbm(t) for t in srcs], *[hbm(t) for t in lands]),
        in_specs=[_HBM] * (2 * na) + [_SEM] * 3 + [pl.BlockSpec(memory_space=pl.ANY)],
        out_specs=tuple([_HBM] * (2 * na)), input_output_aliases={i: i for i in range(2 * na)},
        compiler_params=pltpu.CompilerParams(has_side_effects=_EFFECT),
    )(*srcs, *lands, *sems, after)
    return out[na:]


def _block_diag(w):
    h, a, b = w.shape
    eye = jnp.eye(h, dtype=w.dtype)
    return (w[:, :, None, :] * eye[:, None, :, None]).reshape(h * a, h * b)


def _block_diag_extract(m, h):
    a, b = m.shape[0] // h, m.shape[1] // h
    return jnp.stack([m[i * a:(i + 1) * a, i * b:(i + 1) * b] for i in range(h)], axis=0)


def _block_diag_take(m, h):
    a, b = m.shape[0] // h, m.shape[1] // h
    eye = jnp.eye(h, dtype=m.dtype)
    return (m.reshape(h, a, h, b) * eye[:, None, :, None]).sum(axis=2)


def _ffn_interleave(w):
    lead = w.shape[:-1]
    nb = D_FF // FFN_CB
    return jnp.swapaxes(w.reshape(*lead, 2, nb, FFN_CB), -3, -2).reshape(*lead, 2 * D_FF)


def _ffn_deinterleave(w):
    lead = w.shape[:-1]
    nb = D_FF // FFN_CB
    return jnp.swapaxes(w.reshape(*lead, nb, 2, FFN_CB), -3, -2).reshape(*lead, 2 * D_FF)


def _gather_full(gathered, axis):
    shape = list(gathered.shape[1:])
    shape[axis] *= N_DEV
    return jnp.moveaxis(gathered, 0, axis).reshape(shape)


def _scatter_blocks(full, axis):
    shape = list(full.shape)
    shape[axis:axis + 1] = [N_DEV, shape[axis] // N_DEV]
    return jnp.moveaxis(full.reshape(shape), axis, 0)


def _pad_to(flat, mult):
    pad = (-flat.shape[-1]) % mult
    if pad:
        flat = jnp.concatenate([flat, jnp.zeros(flat.shape[:-1] + (pad,), flat.dtype)], axis=-1)
    return flat


def _layer_fwd(h_in, h_in_t, w, cos, sin, l, dep, get_ffn):
    tag = "l%d_" % l
    proj = _mm_nn(h_in, w['w_in'], 512, D_IN, tag + "proj", dep=dep)
    qkv = _rope_fwd(proj, cos, sin, tag + "rope")
    outs, lses = [], []
    for d, qv in zip(DILATIONS, qkv):
        o, ls = _attn_fwd(qv, d, tag + "attn_d%d" % d)
        outs.append(o)
        lses.append(ls)
    lru = _lru_fwd(proj, w['lru_conv_w'], w['lru_conv_b'], w['lru_wr'], w['lru_br'], w['lru_wi'],
                   w['lru_bi'], w['lru_lambda'], tag + "lru")
    x_re, x_im, y_acc = _s5_scan_fwd(proj, w['s5_bb_re'], w['s5_bb_im'], w['s5_lam_re'], w['s5_lam_im'],
                                     w['s5_cc_re'], w['s5_cc_im'], tag + "s5_scan")
    s5, y_pre = _s5_out_fwd(proj, y_acc, w['s5_d'], w['s5_w_glu'], w['s5_b_glu'], tag + "s5_out")
    mixed, mixed_t, attn_o, attn_lse = _mix_fwd(outs, lses, lru, s5, w['mix_norm_g'], tag + "mix")
    mixo = _mm_nn(mixed, w['w_out'], 512, D_MODEL, tag + "out_proj")
    r1, h1, h1_t = _ln_fwd(h_in, mixo, w['ln1_g'], w['ln1_b'], tag + "ln1")
    w['w_up_g'], w['w_down'], ffn_dep = get_ffn(l, h1)
    up = _mm_up(h1, w['w_up_g'], 1024, tag + "up_proj", dep=ffn_dep)
    act, act_t = _ffn_act_fwd(up, w['ffn_conv_w'], w['ffn_conv_b'], tag + "ffn_act")
    ffn = _mm_nn(act, w['w_down'], 512, D_MODEL, tag + "down_proj")
    r2, h2, *rest = _ln_fwd(h1, ffn, w['ln2_g'], w['ln2_b'], tag + "ln2", transposed=l + 1 < DEPTH)
    h2_t = rest[0] if rest else None
    saved = dict(h_in_t=h_in_t, proj=proj, qkv=qkv, lru=lru, x_re=x_re, x_im=x_im, y_pre=y_pre, s5=s5,
                 mixed_t=mixed_t, attn_o=attn_o, attn_lse=attn_lse, r1=r1, h1_t=h1_t, up=up, act_t=act_t, r2=r2)
    return h2, h2_t, saved


def _layer_bwd_ffn(dh2, sv, w, l, dep=None):
    tag = "l%d_" % l
    g = {}
    dr2, g['ln2_g'], g['ln2_b'] = _ln_bwd(sv['r2'], dh2, w['ln2_g'], tag + "ln2_bwd", dep=dep)
    g['w_down'] = _mm_dw(sv['act_t'], dr2, 1024, D_MODEL, 1024, tag + "down_dw")
    dact = _mm_nt(dr2, w['w_down'], 512, D_FF, tag + "down_dx")
    dup, g['ffn_conv_w'], g['ffn_conv_b'] = _ffn_act_bwd(sv['up'], dact, w['ffn_conv_w'], w['ffn_conv_b'],
                                                        tag + "ffn_act_bwd")
    g['w_up_g'] = _mm_up_dw(sv['h1_t'], dup, tag + "up_dw")
    dh1 = _mm_up_dx(dup, w['w_up_g'], dr2, ALPHA, 1024, tag + "up_dx")
    return dh1, g


def _layer_bwd_mix(dh1, sv, w, cos, sin, l, dep, after_out_grad, after_in_grad):
    tag = "l%d_" % l
    g = {}
    dr1, g['ln1_g'], g['ln1_b'] = _ln_bwd(sv['r1'], dh1, w['ln1_g'], tag + "ln1_bwd", dep=dep)
    g['w_out'] = _mm_dw(sv['mixed_t'], dr1, 1024, D_MODEL, 1024, tag + "out_dw")
    dmixed = _mm_nt(dr1, w['w_out'], 512, D_MODEL, tag + "out_dx", dep=after_out_grad(l, g['w_out']))
    d_o, dlru, ds5, g['mix_norm_g'] = _mix_bwd(dmixed, sv['attn_o'][0], sv['lru'], sv['s5'], w['mix_norm_g'],
                                               tag + "mix_bwd")
    dy, dud, g['s5_d'], g['s5_w_glu'], g['s5_b_glu'] = _s5_out_bwd(
        sv['proj'], sv['y_pre'], ds5, w['s5_d'], w['s5_w_glu'], w['s5_b_glu'], tag + "s5_out_bwd")
    du, g['s5_lam_re'], g['s5_lam_im'], g['s5_bb_re'], g['s5_bb_im'], g['s5_cc_re'], g['s5_cc_im'] = \
        _s5_scan_bwd(sv['proj'], dy, dud, sv['x_re'], sv['x_im'], w['s5_bb_re'], w['s5_bb_im'],
                     w['s5_lam_re'], w['s5_lam_im'], w['s5_cc_re'], w['s5_cc_im'], tag + "s5_scan_bwd")
    (dxr, dgate, g['lru_conv_w'], g['lru_conv_b'], g['lru_wr'], g['lru_br'], g['lru_wi'], g['lru_bi'],
     g['lru_lambda']) = _lru_bwd(sv['proj'], dlru, w['lru_conv_w'], w['lru_conv_b'], w['lru_wr'],
                                 w['lru_br'], w['lru_wi'], w['lru_bi'], w['lru_lambda'], tag + "lru_bwd")
    dqkv = [_attn_bwd(sv['qkv'][b], sv['attn_o'][b], d_o[b], sv['attn_lse'][b], d, tag + "attn_bwd_d%d" % d)
            for b, d in enumerate(DILATIONS)]
    dproj = _dproj_assemble(dqkv, dxr, dgate, du, cos, sin, tag + "dproj")
    g['w_in'] = _mm_dw(sv['h_in_t'], dproj, 1024, D_IN, 1024, tag + "in_dw")
    dh_in = _mm_nt(dproj, w['w_in'], 512, D_MODEL, tag + "in_dx", add=dr1, add_scale=ALPHA,
                   dep=after_in_grad(l, g['w_in']))
    return dh_in, g


def _s5_rep(a):
    return jnp.repeat(a, S5_C, axis=0)


def _prepare_layer(p, l):
    w = {}
    for n in ('w_in', 'w_out', 's5_w_glu'):
        w[n] = p[n].astype(BF16)
    w['ffn_conv_w'] = _ffn_interleave(p['ffn_conv_w'])
    w['ffn_conv_b'] = _ffn_interleave(p['ffn_conv_b'])[None, :]
    w['lru_conv_w'] = p['lru_conv_w']
    for n in ('lru_conv_b', 'lru_br', 'lru_bi', 'lru_lambda', 's5_b_glu', 'mix_norm_g',
              'ln1_g', 'ln1_b', 'ln2_g', 'ln2_b'):
        w[n] = p[n][None, :]
    w['lru_wr'] = _block_diag(p['lru_wr']).astype(BF16)
    w['lru_wi'] = _block_diag(p['lru_wi']).astype(BF16)
    w['s5_d'] = p['s5_d'].reshape(1, S5_W)
    disc_in = (_s5_rep(p['s5_a_re']), _s5_rep(p['s5_a_im']),
               _s5_rep(jnp.broadcast_to(p['s5_log_step'][:, None], (S5_G, S5_P))),
               jnp.swapaxes(p['s5_b_re'], 1, 2).reshape(S5_W, S5_P),
               jnp.swapaxes(p['s5_b_im'], 1, 2).reshape(S5_W, S5_P))
    ab_re, ab_im, bb_re, bb_im = _s5_disc_fwd(*disc_in, "l%d_s5_disc" % l)
    w['s5_disc_in'] = disc_in
    w['s5_lam_re'] = ab_re.reshape(S5_G, S5_C, S5_P)[:, 0, :].reshape(1, S5_STATES)
    w['s5_lam_im'] = ab_im.reshape(S5_G, S5_C, S5_P)[:, 0, :].reshape(1, S5_STATES)
    w['s5_bb_re'] = _block_diag(bb_re.reshape(S5_G, S5_C, S5_P)).astype(BF16)
    w['s5_bb_im'] = _block_diag(bb_im.reshape(S5_G, S5_C, S5_P)).astype(BF16)
    w['s5_cc_re'] = _block_diag(jnp.swapaxes(p['s5_c_re'], 1, 2)).astype(BF16)
    w['s5_cc_im'] = _block_diag(jnp.swapaxes(p['s5_c_im'], 1, 2)).astype(BF16)
    return w


def _finish_layer_grads(g, w, l):
    out = {}
    for n in ('w_in', 'w_out', 'w_down', 'w_up_g', 's5_w_glu', 'lru_conv_w'):
        out[n] = g[n]
    out['ffn_conv_w'] = _ffn_deinterleave(g['ffn_conv_w'])
    out['ffn_conv_b'] = _ffn_deinterleave(g['ffn_conv_b'])[0]
    for n in ('lru_conv_b', 'lru_br', 'lru_bi', 'lru_lambda', 's5_b_glu', 'mix_norm_g',
              'ln1_g', 'ln1_b', 'ln2_g', 'ln2_b'):
        out[n] = g[n][0]
    out['lru_wr'] = _block_diag_extract(g['lru_wr'], LRU_W // HEAD)
    out['lru_wi'] = _block_diag_extract(g['lru_wi'], LRU_W // HEAD)
    out['s5_d'] = g['s5_d'].reshape(S5_G, S5_C)
    out['s5_c_re'] = jnp.swapaxes(_block_diag_take(g['s5_cc_re'], S5_G), 1, 2)
    out['s5_c_im'] = jnp.swapaxes(_block_diag_take(g['s5_cc_im'], S5_G), 1, 2)
    rep = lambda v: _s5_rep(v.reshape(S5_G, S5_P)) * (1.0 / S5_C)
    cts = (rep(g['s5_lam_re']), rep(g['s5_lam_im']),
           _block_diag_take(g['s5_bb_re'], S5_G).reshape(S5_W, S5_P),
           _block_diag_take(g['s5_bb_im'], S5_G).reshape(S5_W, S5_P))
    da_re, da_im, dls, dbt_re, dbt_im = _s5_disc_bwd(*w['s5_disc_in'], cts, "l%d_s5_disc_bwd" % l)
    out['s5_a_re'] = da_re.reshape(S5_G, S5_C, S5_P).sum(axis=1)
    out['s5_a_im'] = da_im.reshape(S5_G, S5_C, S5_P).sum(axis=1)
    out['s5_log_step'] = dls.reshape(S5_G, S5_C * S5_P).sum(axis=1)
    out['s5_b_re'] = jnp.swapaxes(dbt_re.reshape(S5_G, S5_C, S5_P), 1, 2)
    out['s5_b_im'] = jnp.swapaxes(dbt_im.reshape(S5_G, S5_C, S5_P), 1, 2)
    return out


def _run_step(x, target, get_layer, get_ffn, after_ffn_grads, after_out_grad, after_in_grad, after_layer):
    cos, sin = _rope_tables(x.shape[0])
    h, h_t = x, _transpose_bf16(x, "x_transpose")
    ws, saved = [], []
    for l in range(DEPTH):
        p, dep = get_layer(l, h)
        ws.append(_prepare_layer(p, l))
        h, h_t, sv = _layer_fwd(h, h_t, ws[l], cos, sin, l, dep, get_ffn)
        saved.append(sv)
    dh, loss_vec = _loss_head(h, target)
    dep = None
    for l in reversed(range(DEPTH)):
        dh1, g = _layer_bwd_ffn(dh, saved[l], ws[l], l, dep)
        dep = after_ffn_grads(l, g)
        dh, g_mix = _layer_bwd_mix(dh1, saved[l], ws[l], cos, sin, l, dep, after_out_grad, after_in_grad)
        g.update(g_mix)
        after_layer(l, _finish_layer_grads(g, ws[l], l))
        dep = None
    return loss_vec[0, 0], dh


def _local_step(x, target, layers):
    grads = [None] * DEPTH

    def keep(l, g):
        grads[l] = g

    def ffn(l, h1):
        return layers[l]['w_up_g'].astype(BF16), layers[l]['w_down'].astype(BF16), None

    none = lambda l, g: None
    loss, dx = _run_step(x, target, lambda l, h: (layers[l], None), ffn, none, none, none, keep)
    return loss, dx, grads


def kernel(x, w_in, lru_conv_w, lru_conv_b, lru_wr, lru_br, lru_wi, lru_bi, lru_lambda, s5_a_re, s5_a_im, s5_b_re, s5_b_im, s5_c_re, s5_c_im, s5_d, s5_log_step, s5_w_glu, s5_b_glu, mix_norm_g, w_out, ln1_g, ln1_b, w_up, ffn_conv_w, ffn_conv_b, w_down, ln2_g, ln2_b, loss_target, m_w_in, m_lru_conv_w, m_lru_conv_b, m_lru_wr, m_lru_br, m_lru_wi, m_lru_bi, m_lru_lambda, m_s5_a_re, m_s5_a_im, m_s5_b_re, m_s5_b_im, m_s5_c_re, m_s5_c_im, m_s5_d, m_s5_log_step, m_s5_w_glu, m_s5_b_glu, m_mix_norm_g, m_w_out, m_ln1_g, m_ln1_b, m_w_up, m_ffn_conv_w, m_ffn_conv_b, m_w_down, m_ln2_g, m_ln2_b, v_w_in, v_lru_conv_w, v_lru_conv_b, v_lru_wr, v_lru_br, v_lru_wi, v_lru_bi, v_lru_lambda, v_s5_a_re, v_s5_a_im, v_s5_b_re, v_s5_b_im, v_s5_c_re, v_s5_c_im, v_s5_d, v_s5_log_step, v_s5_w_glu, v_s5_b_glu, v_mix_norm_g, v_w_out, v_ln1_g, v_ln1_b, v_w_up, v_ffn_conv_w, v_ffn_conv_b, v_w_down, v_ln2_g, v_ln2_b):
    args = locals()
    wl = {n: args[n] for n in WEIGHTS}
    ml = {n: args['m_' + n] for n in WEIGHTS}
    vl = {n: args['v_' + n] for n in WEIGHTS}
    me = 4 * lax.axis_index("x") + 2 * lax.axis_index("y") + lax.axis_index("c")

    small_sizes = [int(wl[n].size) for n in SMALL_SHARDED]
    small_flat = _pad_to(jnp.concatenate([wl[n].reshape(-1) for n in SMALL_SHARDED]), 8 * 1024)
    small_all, = _all_gather([small_flat.reshape(-1, 1024)], "gather_small")
    small_all = small_all.reshape(N_DEV, -1)
    small_full, off = {}, 0
    for n, sz in zip(SMALL_SHARDED, small_sizes):
        small_full[n] = _gather_full(small_all[:, off:off + sz].reshape((N_DEV,) + wl[n].shape), SHARD_AXIS[n])
        off += sz
    def mixer_params(l, gathered):
        g_in, g_out = gathered
        p = {n: wl[n][l] for n in REPLICATED}
        p.update({n: small_full[n][l] for n in SMALL_SHARDED})
        p['w_in'] = _gather_full(g_in, 1)
        p['w_out'] = g_out.reshape(D_MODEL, D_MODEL)
        return p

    mix_names, ffn_names = ('w_in', 'w_out'), ('w_up', 'w_down')
    shards = lambda names, l: [wl[n][l].astype(BF16) for n in names]
    mix0 = _all_gather(shards(mix_names, 0), "gather_mix_l0")
    gathers = {}
    gathers[0, 'ffn'], ffn0_token = _exchange_start(shards(ffn_names, 0), True, "gather_ffn_l0_start", dep=mix0[0])
    def get_layer(l, h):
        if l == 0:
            return mixer_params(0, mix0), ffn0_token
        return mixer_params(1, _exchange_wait(gathers[1, 'mix'], True, h, "gather_mix_l1_wait")), None

    def get_ffn(l, h1):
        g_up, g_down = _exchange_wait(gathers[l, 'ffn'], True, h1, "gather_ffn_l%d_wait" % l)
        token = None
        if l == 0:
            gathers[1, 'mix'], token = _exchange_start(shards(mix_names, 1), True, "gather_mix_l1_start", dep=g_up)
            gathers[1, 'ffn'], token = _exchange_start(shards(ffn_names, 1), True, "gather_ffn_l1_start", dep=token)
        return g_up, g_down.reshape(D_FF, D_MODEL), token

    scatters, grads = {}, [None] * DEPTH

    def after_ffn_grads(l, g):
        send = [g['w_up_g'], g['w_down'].reshape(N_DEV, D_FF // N_DEV, D_MODEL)]
        scatters[l, 'ffn'], token = _exchange_start(send, False, "scatter_ffn_l%d_start" % l)
        return token

    def after_out_grad(l, g_out):
        send = [g_out.reshape(N_DEV, D_MODEL // N_DEV, D_MODEL)]
        scatters[l, 'out'], token = _exchange_start(send, False, "scatter_out_l%d_start" % l)
        return token

    def after_in_grad(l, g_in):
        send = _scatter_blocks(g_in, 1)
        if l == 0:
            send = send.astype(BF16)
        scatters[l, 'in'], token = _exchange_start([send], False, "scatter_in_l%d_start" % l)
        return token

    def after_layer(l, g):
        grads[l] = g

    loss_local, grad_x = _run_step(x[0], loss_target[0], get_layer, get_ffn, after_ffn_grads, after_out_grad,
                                   after_in_grad, after_layer)
    loss = lax.psum(loss_local, AXES)

    results = {}
    big_prev = {n: None for n in BIG}

    def finish_big(l, part, names, after):
        landed = _exchange_wait(scatters[l, part], False, after, "scatter_%s_l%d_wait" % (part, l))
        for n, ld in zip(names, landed):
            big_prev[n] = _adamw_sum(ld, wl[n], ml[n], vl[n], l, big_prev[n], "adamw_%s_l%d" % (n, l))

    for l, part, names in ((1, 'ffn', ffn_names), (1, 'out', ('w_out',)), (1, 'in', ('w_in',)),
                           (0, 'ffn', ffn_names), (0, 'out', ('w_out',))):
        finish_big(l, part, names, grad_x)

    stacked = {n: jnp.stack([grads[l][n] for l in range(DEPTH)], axis=0) for n in SMALL_SHARDED + REPLICATED}
    rep_sizes = [int(wl[n].size) for n in REPLICATED]
    rep_per = -(-sum(rep_sizes) // (N_DEV * 1024)) * 1024

    def rep_flat(tree):
        return _pad_to(jnp.concatenate([tree[n].reshape(-1) for n in REPLICATED]), N_DEV * rep_per)

    rows = [_scatter_blocks(stacked[n], SHARD_AXIS[n]).reshape(N_DEV, -1) for n in SMALL_SHARDED]
    rows.append(rep_flat(stacked).reshape(N_DEV, rep_per))
    small_send = _pad_to(jnp.concatenate(rows, axis=1), 8 * 1024)
    n_own = small_send.shape[1]
    small_landed, = _all_to_all([small_send.reshape(N_DEV, n_own // 1024, 1024)], "scatter_small")

    def own_flat(tree):
        parts = [tree[n].reshape(-1) for n in SMALL_SHARDED]
        parts.append(lax.dynamic_slice(rep_flat(tree), (me * rep_per,), (rep_per,)))
        return _pad_to(jnp.concatenate(parts), 8 * 1024).reshape(1, n_own // 1024, 1024)

    small_res = _adamw_sum(small_landed, own_flat(wl), own_flat(ml), own_flat(vl), 0, None, "adamw_small")

    kinds = ('grad', 'delta', 'm', 'v')
    sh_total = sum(small_sizes)
    for kind, arr in zip(kinds, small_res):
        flat = arr.reshape(-1)
        off = 0
        for n, sz in zip(SMALL_SHARDED, small_sizes):
            results[kind, n] = flat[off:off + sz].reshape(wl[n].shape)
            off += sz
    rep_own = jnp.stack([a.reshape(-1)[sh_total:sh_total + rep_per] for a in small_res])
    rep_all, = _all_gather([rep_own.reshape(4 * rep_per // 1024, 1024)], "gather_replicated")
    finish_big(0, 'in', ('w_in',), rep_all)
    for n in BIG:
        results['grad', n], results['delta', n], results['m', n], results['v', n] = big_prev[n]
    rep_all = rep_all.reshape(N_DEV, 4, rep_per)
    for k, kind in enumerate(kinds):
        flat = rep_all[:, k, :].reshape(-1)
        off = 0
        for n, sz in zip(REPLICATED, rep_sizes):
            results[kind, n] = flat[off:off + sz].reshape(wl[n].shape)
            off += sz

    out = [loss, grad_x[None]]
    for kind in kinds:
        out.extend(results[kind, n] for n in WEIGHTS)
    return tuple(out)
```

```python
import functools
import math

import jax
import jax.numpy as jnp
from jax import lax
from jax.experimental import pallas as pl
from jax.experimental.pallas import tpu as pltpu

F32 = jnp.float32
BF16 = jnp.bfloat16

N_DEV = 8
DEPTH = 2
D_MODEL = 1024
ATTN_W = 384
LRU_W = 384
S5_W = 256
D_IN = 2176
D_FF = 3072
HEAD = 64
ATTN_BLK = 128
ATTN_TILE = 1024
DILATIONS = (1, 4, 16)
S5_G = 16
S5_P = 64
S5_C = 16
S5_STATES = S5_G * S5_P
LRU_C = 8.0
LRU_CONV = 4
FFN_CONV = 3
ROPE_THETA = 10000.0
ALPHA = (2 * DEPTH) ** 0.25
LN_EPS = 1e-5
RMS_EPS = 1e-6
ADAM_LR, ADAM_B1, ADAM_B2, ADAM_EPS, ADAM_WD, ADAM_STEP = 0.001, 0.9, 0.999, 1e-8, 0.01, 10

LANE = 128
SCAN_T = 256
S5_BLK = 256
FFN_CB = 2 * D_FF // N_DEV
VMEM_LIMIT = 56 * 1024 * 1024

AXES = ("x", "y", "c")

WEIGHTS = ['w_in', 'lru_conv_w', 'lru_conv_b', 'lru_wr', 'lru_br', 'lru_wi', 'lru_bi', 'lru_lambda',
           's5_a_re', 's5_a_im', 's5_b_re', 's5_b_im', 's5_c_re', 's5_c_im', 's5_d', 's5_log_step',
           's5_w_glu', 's5_b_glu', 'mix_norm_g', 'w_out', 'ln1_g', 'ln1_b', 'w_up', 'ffn_conv_w',
           'ffn_conv_b', 'w_down', 'ln2_g', 'ln2_b']
SHARD_AXIS = {'w_in': 2, 'lru_conv_w': 2, 's5_w_glu': 1, 'w_out': 1, 'w_up': 2, 'ffn_conv_w': 2, 'w_down': 1}
BIG = ['w_in', 'w_out', 'w_up', 'w_down']
SMALL_SHARDED = ['lru_conv_w', 'ffn_conv_w', 's5_w_glu']
REPLICATED = [n for n in WEIGHTS if n not in SHARD_AXIS]


def _cparams(sem=None):
    return pltpu.CompilerParams(dimension_semantics=sem, vmem_limit_bytes=VMEM_LIMIT)


def _ffn_dev(jb):
    return jb // 2 + (N_DEV // 2) * (jb % 2)


def _gelu(x):
    c = math.sqrt(2.0 / math.pi)
    t = jnp.tanh(c * (x + 0.044715 * (x * x * x)))
    return 0.5 * x * (1.0 + t)


def _gelu_grad(x):
    c = math.sqrt(2.0 / math.pi)
    x2 = x * x
    t = jnp.tanh(c * (x + 0.044715 * (x2 * x)))
    return 0.5 * (1.0 + t) + 0.5 * x * (1.0 - t * t) * (c * (1.0 + 3.0 * 0.044715 * x2))


def _sigmoid(x):
    return 1.0 / (1.0 + jnp.exp(-x))


def _log1p(x):
    u = 1.0 + x
    d = u - 1.0
    return jnp.where(d == 0.0, x, jnp.log(u) * (x / jnp.where(d == 0.0, 1.0, d)))


def _softplus(x):
    return jnp.maximum(x, 0.0) + _log1p(jnp.exp(-jnp.abs(x)))


def _expm1(x):
    return jnp.tanh(0.5 * x) * (jnp.exp(x) + 1.0)


def _dot(a, b):
    return jnp.dot(a.astype(BF16), b.astype(BF16), preferred_element_type=F32)


def _dot_nt(a, b):
    return lax.dot_general(a.astype(BF16), b.astype(BF16), (((1,), (1,)), ((), ())),
                           preferred_element_type=F32)


def _dot_tn(a, b):
    return lax.dot_general(a.astype(BF16), b.astype(BF16), (((0,), (0,)), ((), ())),
                           preferred_element_type=F32)


def _rows(shape):
    return lax.broadcasted_iota(jnp.int32, shape, 0)


def _shift_down_prev(x, s, prev8):
    if s == 0:
        return x
    t, l = x.shape
    r = pltpu.roll(x, s, axis=0)
    pr = pltpu.roll(prev8, s, axis=0)
    pad = jnp.concatenate([pr, jnp.zeros((t - 8, l), x.dtype)], axis=0)
    return jnp.where(_rows(x.shape) < s, pad, r)


def _shift_up_next(x, s, next8):
    if s == 0:
        return x
    t, l = x.shape
    r = pltpu.roll(x, t - s, axis=0)
    nx = pltpu.roll(next8, 8 - s, axis=0)
    pad = jnp.concatenate([jnp.zeros((t - 8, l), x.dtype), nx], axis=0)
    return jnp.where(_rows(x.shape) >= t - s, pad, r)


SUB = 8


def _tile_shift(x, s, fill, reverse):
    t = x.shape[0]
    pos = _rows(x.shape) & (SUB - 1)
    if reverse:
        return jnp.where(pos < SUB - s, pltpu.roll(x, t - s, axis=0), fill)
    return jnp.where(pos >= s, pltpu.roll(x, s, axis=0), fill)


def _scan_chunk(a, x, carry, reverse=False):
    s = 1
    while s < SUB:
        x = x + a * _tile_shift(x, s, 0.0, reverse)
        a = a * _tile_shift(a, s, 1.0, reverse)
        s *= 2
    nv = x.shape[0] // SUB
    out = [None] * nv
    for v in (reversed(range(nv)) if reverse else range(nv)):
        rows = slice(v * SUB, (v + 1) * SUB)
        out[v] = x[rows, :] + a[rows, :] * carry
        carry = out[v][0:1, :] if reverse else out[v][SUB - 1:SUB, :]
    return jnp.concatenate(out, axis=0)


def _cmul(ar, ai, br, bi):
    return ar * br - ai * bi, ar * bi + ai * br


def _cscan_consts(lr, li, reverse):
    pows = [(lr, li)]
    for _ in range(2):
        pows.append(_cmul(*pows[-1], *pows[-1]))
    rows = [(lr, li)]
    for _ in range(SUB - 1):
        rows.append(_cmul(*rows[-1], lr, li))
    if reverse:
        rows = rows[::-1]
    return pows, (jnp.concatenate([r for r, _ in rows], axis=0), jnp.concatenate([i for _, i in rows], axis=0))


def _cscan_chunk(xr, xi, consts, carry, reverse=False):
    pows, (p8r, p8i) = consts
    s = 1
    for pr, pi in pows:
        sr = _tile_shift(xr, s, 0.0, reverse)
        si = _tile_shift(xi, s, 0.0, reverse)
        xr, xi = xr + pr * sr - pi * si, xi + pr * si + pi * sr
        s *= 2
    nv = xr.shape[0] // SUB
    out_r, out_i = [None] * nv, [None] * nv
    cr, ci = carry
    for v in (reversed(range(nv)) if reverse else range(nv)):
        rows = slice(v * SUB, (v + 1) * SUB)
        out_r[v] = xr[rows, :] + p8r * cr - p8i * ci
        out_i[v] = xi[rows, :] + p8r * ci + p8i * cr
        edge = slice(0, 1) if reverse else slice(SUB - 1, SUB)
        cr, ci = out_r[v][edge, :], out_i[v][edge, :]
    return jnp.concatenate(out_r, axis=0), jnp.concatenate(out_i, axis=0)


def _dep_args(dep):
    return ([], []) if dep is None else ([pl.BlockSpec(memory_space=pl.ANY)], [dep])


def _mm_nn(a, b, tm, tn, name, out_dtype=F32, dep=None):
    m, k = a.shape
    n = b.shape[1]

    def body(a_ref, b_ref, *rest):
        o_ref = rest[-1]
        o_ref[...] = _dot(a_ref[...], b_ref[...]).astype(out_dtype)

    dep_specs, dep_ops = _dep_args(dep)
    return pl.pallas_call(
        body, out_shape=jax.ShapeDtypeStruct((m, n), out_dtype), grid=(n // tn, m // tm),
        in_specs=[pl.BlockSpec((tm, k), lambda j, i: (i, 0)),
                  pl.BlockSpec((k, tn), lambda j, i: (0, j))] + dep_specs,
        out_specs=pl.BlockSpec((tm, tn), lambda j, i: (i, j)), name=name,
        compiler_params=_cparams(("parallel", "parallel")))(a, b, *dep_ops)


def _mm_nt(a, w, tm, tn, name, add=None, add_scale=1.0, dep=None):
    m, k = a.shape
    n = w.shape[0]

    def body(a_ref, w_ref, *rest):
        o_ref = rest[-1]
        if add is None:
            o_ref[...] = _dot_nt(a_ref[...], w_ref[...])
        else:
            o_ref[...] = _dot_nt(a_ref[...], w_ref[...]) + add_scale * rest[0][...]

    in_specs = [pl.BlockSpec((tm, k), lambda j, i: (i, 0)), pl.BlockSpec((tn, k), lambda j, i: (j, 0))]
    args = [a, w]
    if add is not None:
        in_specs.append(pl.BlockSpec((tm, tn), lambda j, i: (i, j)))
        args.append(add)
    dep_specs, dep_ops = _dep_args(dep)
    return pl.pallas_call(
        body, out_shape=jax.ShapeDtypeStruct((m, n), F32), grid=(n // tn, m // tm),
        in_specs=in_specs + dep_specs, out_specs=pl.BlockSpec((tm, tn), lambda j, i: (i, j)), name=name,
        compiler_params=_cparams(("parallel", "parallel")))(*args, *dep_ops)


def _mm_dw(at, b, tm, tn, ts, name):
    m, s = at.shape
    n = b.shape[1]

    def body(a_ref, b_ref, o_ref):
        @pl.when(pl.program_id(2) == 0)
        def _():
            o_ref[...] = jnp.zeros_like(o_ref)
        o_ref[...] += _dot(a_ref[...], b_ref[...])

    return pl.pallas_call(
        body, out_shape=jax.ShapeDtypeStruct((m, n), F32), grid=(m // tm, n // tn, s // ts),
        in_specs=[pl.BlockSpec((tm, ts), lambda i, j, k: (i, k)), pl.BlockSpec((ts, tn), lambda i, j, k: (k, j))],
        out_specs=pl.BlockSpec((tm, tn), lambda i, j, k: (i, j)), name=name,
        compiler_params=_cparams(("parallel", "parallel", "arbitrary")))(at, b)


def _transpose_bf16(x, name):
    s, d = x.shape
    tm = 512

    def body(x_ref, o_ref):
        o_ref[...] = x_ref[...].T.astype(BF16)

    return pl.pallas_call(
        body, out_shape=jax.ShapeDtypeStruct((d, s), BF16), grid=(s // tm,),
        in_specs=[pl.BlockSpec((tm, d), lambda i: (i, 0))], out_specs=pl.BlockSpec((d, tm), lambda i: (0, i)),
        name=name, compiler_params=_cparams(("parallel",)))(x)


def _mm_up(h, wg, tm, name, dep=None):
    s, d = h.shape

    def body(a_ref, w_ref, *rest):
        rest[-1][...] = _dot(a_ref[...], w_ref[...])

    dep_specs, dep_ops = _dep_args(dep)
    return pl.pallas_call(
        body, out_shape=jax.ShapeDtypeStruct((s, 2 * D_FF), F32), grid=(s // tm, N_DEV),
        in_specs=[pl.BlockSpec((tm, d), lambda i, j: (i, 0)),
                  pl.BlockSpec((None, d, FFN_CB), lambda i, j: (_ffn_dev(j), 0, 0))] + dep_specs,
        out_specs=pl.BlockSpec((tm, FFN_CB), lambda i, j: (i, j)), name=name,
        compiler_params=_cparams(("parallel", "parallel")))(h, wg, *dep_ops)


def _mm_up_dx(dup, wg, add, add_scale, tm, name):
    s = dup.shape[0]
    d = wg.shape[1]

    def body(a_ref, w_ref, c_ref, o_ref):
        @pl.when(pl.program_id(1) == 0)
        def _():
            o_ref[...] = add_scale * c_ref[...]
        o_ref[...] += _dot_nt(a_ref[...], w_ref[...])

    return pl.pallas_call(
        body, out_shape=jax.ShapeDtypeStruct((s, d), F32), grid=(s // tm, N_DEV),
        in_specs=[pl.BlockSpec((tm, FFN_CB), lambda i, j: (i, j)),
                  pl.BlockSpec((None, d, FFN_CB), lambda i, j: (_ffn_dev(j), 0, 0)),
                  pl.BlockSpec((tm, d), lambda i, j: (i, 0))],
        out_specs=pl.BlockSpec((tm, d), lambda i, j: (i, 0)), name=name,
        compiler_params=_cparams(("parallel", "arbitrary")))(dup, wg, add)


def _mm_up_dw(ht, dup, name):
    d, s = ht.shape

    def body(a_ref, b_ref, o_ref):
        o_ref[...] = _dot(a_ref[...], b_ref[...])

    return pl.pallas_call(
        body, out_shape=jax.ShapeDtypeStruct((N_DEV, d, FFN_CB), F32), grid=(N_DEV,),
        in_specs=[pl.BlockSpec((d, s), lambda j: (0, 0)), pl.BlockSpec((s, FFN_CB), lambda j: (0, j))],
        out_specs=pl.BlockSpec((None, d, FFN_CB), lambda j: (_ffn_dev(j), 0, 0)), name=name,
        compiler_params=_cparams(("parallel",)))(ht, dup)


def _ln_fwd(a, b, g, bias, name, transposed=True):
    s, d = a.shape
    tm = 512

    def body(a_ref, b_ref, g_ref, bias_ref, r_ref, h_ref, *ht_ref):
        r = ALPHA * a_ref[...] + b_ref[...]
        mu = jnp.mean(r, axis=-1, keepdims=True)
        xc = r - mu
        var = jnp.mean(xc * xc, axis=-1, keepdims=True)
        r_ref[...] = r
        h = xc * lax.rsqrt(var + LN_EPS) * g_ref[...] + bias_ref[...]
        h_ref[...] = h
        if transposed:
            ht_ref[0][...] = h.T.astype(BF16)

    row = pl.BlockSpec((tm, d), lambda i: (i, 0))
    vec = pl.BlockSpec((1, d), lambda i: (0, 0))
    shapes = [jax.ShapeDtypeStruct((s, d), F32), jax.ShapeDtypeStruct((s, d), F32)]
    specs = [row, row]
    if transposed:
        shapes.append(jax.ShapeDtypeStruct((d, s), BF16))
        specs.append(pl.BlockSpec((d, tm), lambda i: (0, i)))
    return pl.pallas_call(
        body, out_shape=tuple(shapes), grid=(s // tm,), in_specs=[row, row, vec, vec],
        out_specs=tuple(specs), name=name, compiler_params=_cparams(("parallel",)))(a, b, g, bias)


def _ln_bwd(r, dh, g, name, dep=None):
    s, d = r.shape
    tm = 512

    def body(r_ref, dh_ref, g_ref, *rest):
        dr_ref, dg_ref, db_ref = rest[-3:]

        @pl.when(pl.program_id(0) == 0)
        def _():
            dg_ref[...] = jnp.zeros_like(dg_ref)
            db_ref[...] = jnp.zeros_like(db_ref)
        rr = r_ref[...]
        dh_ = dh_ref[...]
        mu = jnp.mean(rr, axis=-1, keepdims=True)
        xc = rr - mu
        var = jnp.mean(xc * xc, axis=-1, keepdims=True)
        rstd = lax.rsqrt(var + LN_EPS)
        xh = xc * rstd
        dxh = dh_ * g_ref[...]
        m1 = jnp.mean(dxh, axis=-1, keepdims=True)
        m2 = jnp.mean(dxh * xh, axis=-1, keepdims=True)
        dr_ref[...] = rstd * (dxh - m1 - xh * m2)
        dg_ref[...] += jnp.sum(dh_ * xh, axis=0, keepdims=True)
        db_ref[...] += jnp.sum(dh_, axis=0, keepdims=True)

    row = pl.BlockSpec((tm, d), lambda i: (i, 0))
    vec = pl.BlockSpec((1, d), lambda i: (0, 0))
    dep_specs, dep_ops = _dep_args(dep)
    return pl.pallas_call(
        body, out_shape=(jax.ShapeDtypeStruct((s, d), F32), jax.ShapeDtypeStruct((1, d), F32),
                         jax.ShapeDtypeStruct((1, d), F32)),
        grid=(s // tm,), in_specs=[row, row, vec] + dep_specs, out_specs=(row, vec, vec), name=name,
        compiler_params=_cparams(("arbitrary",)))(r, dh, g, *dep_ops)


def _loss_head(y, target):
    s, d = y.shape
    tm = 512

    def body(y_ref, t_ref, dy_ref, l_ref):
        @pl.when(pl.program_id(0) == 0)
        def _():
            l_ref[...] = jnp.zeros_like(l_ref)
        e = y_ref[...] - t_ref[...]
        dy_ref[...] = e * (1.0 / d)
        part = 0.5 * jnp.sum(jnp.mean(e * e, axis=-1, keepdims=True), axis=0, keepdims=True)
        l_ref[...] += jnp.broadcast_to(part, l_ref.shape)

    row = pl.BlockSpec((tm, d), lambda i: (i, 0))
    return pl.pallas_call(
        body, out_shape=(jax.ShapeDtypeStruct((s, d), F32), jax.ShapeDtypeStruct((1, LANE), F32)),
        grid=(s // tm,), in_specs=[row, row], out_specs=(row, pl.BlockSpec((1, LANE), lambda i: (0, 0))),
        name="loss_head", compiler_params=_cparams(("arbitrary",)))(y, target)


def _rope_tables(s):
    half = HEAD // 2
    pos = jnp.arange(s, dtype=F32)
    inv = ROPE_THETA ** (-jnp.arange(half, dtype=F32) * 2.0 / HEAD)
    ang = pos[:, None] * inv[None, :]
    cos, sin = jnp.cos(ang), jnp.sin(ang)
    cos = jnp.concatenate([cos, cos, cos, cos], axis=1)
    sin = jnp.concatenate([-sin, sin, -sin, sin], axis=1)
    return cos, sin


def _rotate(x, cos, sin):
    lane = lax.broadcasted_iota(jnp.int32, x.shape, 1)
    partner = jnp.where((lane % HEAD) < HEAD // 2, pltpu.roll(x, LANE - HEAD // 2, axis=1),
                        pltpu.roll(x, HEAD // 2, axis=1))
    return x * cos + partner * sin


def _class_rows(c, d, tm):
    return pl.ds(c, tm // d, stride=d) if d > 1 else pl.ds(0, tm)


def _dilated_spec(tm, d, w):
    return pl.BlockSpec((tm // d, d * w), lambda i: (i, 0))


def _token_scratch(tm, w):
    return pltpu.VMEM((w // LANE, tm, LANE), F32)


def _to_tokens(src_ref, dst3, d, tm):
    nj = dst3.shape[0]
    for cls in range(d):
        for j in range(nj):
            col = (cls * nj + j) * LANE
            dst3.at[j][_class_rows(cls, d, tm), :] = src_ref[:, col:col + LANE]


def _to_dilated(src3, dst_ref, d, tm):
    nj = src3.shape[0]
    for cls in range(d):
        for j in range(nj):
            col = (cls * nj + j) * LANE
            dst_ref[:, col:col + LANE] = src3.at[j][_class_rows(cls, d, tm), :].astype(dst_ref.dtype)


def _token_value(src3):
    return jnp.concatenate([src3[j] for j in range(src3.shape[0])], axis=1)


def _rope_fwd(proj, cos, sin, name):
    s = proj.shape[0]
    tm = 512
    w = 3 * ATTN_W
    nj = w // LANE

    def body(*refs):
        p_refs, (c_ref, s_ref), o_refs, rot = refs[:nj], refs[nj:nj + 2], refs[nj + 2:nj + 5], refs[nj + 5]
        c, sn = c_ref[...], s_ref[...]
        for j in range(nj):
            x = p_refs[j][...]
            rot[j] = _rotate(x, c, sn) if j < 2 * ATTN_W // LANE else x
        for d, o_ref in zip(DILATIONS, o_refs):
            _to_dilated(rot, o_ref, d, tm)

    tab = pl.BlockSpec((tm, LANE), lambda i: (i, 0))
    cols = [pl.BlockSpec((tm, LANE), functools.partial(lambda i, j: (i, j), j=j)) for j in range(nj)]
    return pl.pallas_call(
        body, out_shape=tuple(jax.ShapeDtypeStruct((s // d, d * w), BF16) for d in DILATIONS),
        grid=(s // tm,), in_specs=cols + [tab, tab],
        out_specs=tuple(_dilated_spec(tm, d, w) for d in DILATIONS),
        scratch_shapes=[_token_scratch(tm, w)], name=name,
        compiler_params=_cparams(("parallel",)))(*[proj] * nj, cos, sin)


def _dproj_assemble(dqkv_list, dxr, dgate, du, cos, sin, name):
    s = dxr.shape[0]
    tm = 512
    nq = 3 * ATTN_W // LANE

    def body(*refs):
        br = refs[:9]
        dxr_ref, dg_ref, du_ref, c_ref, s_ref, o_ref = refs[9:15]
        tok = refs[15:]
        c, sn = c_ref[...], -s_ref[...]
        for part in range(3):
            for b, d in enumerate(DILATIONS[1:], start=1):
                _to_tokens(br[3 * b + part], tok[2 * part + b - 1], d, tm)
        for j in range(nq):
            part, jj = divmod(j, ATTN_W // LANE)
            x = br[part][:, jj * LANE:(jj + 1) * LANE] + tok[2 * part][jj] + tok[2 * part + 1][jj]
            if part < 2:
                x = _rotate(x, c, sn)
            o_ref[:, j * LANE:(j + 1) * LANE] = x.astype(BF16)
        o_ref[:, 3 * ATTN_W:3 * ATTN_W + LRU_W] = dxr_ref[...].astype(BF16)
        o_ref[:, 3 * ATTN_W + LRU_W:3 * ATTN_W + 2 * LRU_W] = dg_ref[...].astype(BF16)
        o_ref[:, 3 * ATTN_W + 2 * LRU_W:] = du_ref[...].astype(BF16)

    a_spec = pl.BlockSpec((tm, ATTN_W), lambda i: (i, 0))
    tab = pl.BlockSpec((tm, LANE), lambda i: (i, 0))
    ordered = [dqkv_list[b][p] for b in range(3) for p in range(3)]
    d_specs = [_dilated_spec(tm, d, ATTN_W) for d in DILATIONS for _ in range(3)]
    return pl.pallas_call(
        body, out_shape=jax.ShapeDtypeStruct((s, D_IN), BF16), grid=(s // tm,),
        in_specs=d_specs + [a_spec, a_spec, pl.BlockSpec((tm, S5_W), lambda i: (i, 0)), tab, tab],
        out_specs=pl.BlockSpec((tm, D_IN), lambda i: (i, 0)),
        scratch_shapes=[_token_scratch(tm, ATTN_W)] * 6, name=name,
        compiler_params=_cparams(("parallel",)))(*ordered, dxr, dgate, du, cos, sin)


def _attn_tiles(s, d):
    m = s // d
    tq = min(m, ATTN_TILE)
    return m, tq, tq // ATTN_BLK


def _band_mask(qb):
    qi = lax.broadcasted_iota(jnp.int32, (ATTN_BLK, 2 * ATTN_BLK), 0)
    ki = lax.broadcasted_iota(jnp.int32, (ATTN_BLK, 2 * ATTN_BLK), 1)
    dist = qi + ATTN_BLK - ki
    return (dist >= 0) & (dist <= ATTN_BLK) & ((ki >= ATTN_BLK) | (qb > 0))


def _head_cols(h):
    return (slice(h * HEAD, (h + 1) * HEAD), slice(ATTN_W + h * HEAD, ATTN_W + (h + 1) * HEAD),
            slice(2 * ATTN_W + h * HEAD, 2 * ATTN_W + (h + 1) * HEAD))


def _attn_fwd(qv, d, name):
    m = qv.shape[0]
    w3 = 3 * ATTN_W
    _, tq, n = _attn_tiles(m * d, d)
    scale = HEAD ** -0.5

    def body(x_ref, p_ref, o_ref, l_ref):
        b = pl.program_id(1)

        def block(i, first):
            r0 = 0 if first else pl.multiple_of(i * ATTN_BLK, ATTN_BLK)
            rows = pl.ds(r0, ATTN_BLK)
            valid = _band_mask(b * n + i)
            if not first:
                krows = pl.ds(pl.multiple_of(i * ATTN_BLK - ATTN_BLK, ATTN_BLK), 2 * ATTN_BLK)
            for h in range(ATTN_W // HEAD):
                qs, ks, vs = _head_cols(h)
                q = x_ref[rows, qs]
                if first:
                    k = jnp.concatenate([p_ref[:, ks], x_ref[0:ATTN_BLK, ks]], axis=0)
                    v = jnp.concatenate([p_ref[:, vs], x_ref[0:ATTN_BLK, vs]], axis=0)
                else:
                    k = x_ref[krows, ks]
                    v = x_ref[krows, vs]
                sc = jnp.where(valid, _dot_nt(q, k) * scale, -1e30)
                mx = jnp.max(sc, axis=-1, keepdims=True)
                p = jnp.exp(sc - mx)
                l = jnp.sum(p, axis=-1, keepdims=True)
                o_ref[rows, qs] = _dot(p, v) / l
                l_ref[rows, qs] = jnp.broadcast_to(mx + jnp.log(l), (ATTN_BLK, HEAD))

        block(0, True)
        if n > 1:
            def loop(i, carry):
                block(i, False)
                return carry
            lax.fori_loop(1, n, loop, 0)

    shp = jax.ShapeDtypeStruct((m, d * ATTN_W), F32)
    ospec = pl.BlockSpec((tq, ATTN_W), lambda c, b: (b, c))
    out, lse = pl.pallas_call(
        body, out_shape=(shp, shp), grid=(d, m // tq),
        in_specs=[pl.BlockSpec((tq, w3), lambda c, b: (b, c)),
                  pl.BlockSpec((ATTN_BLK, w3), lambda c, b: (jnp.maximum(b * n - 1, 0), c))],
        out_specs=(ospec, ospec), name=name,
        compiler_params=_cparams(("parallel", "parallel")))(qv, qv)
    return out, lse


def _attn_bwd(qv, ov, dov, lv, d, name, dep=None):
    m = qv.shape[0]
    w3 = 3 * ATTN_W
    _, tq, n = _attn_tiles(m * d, d)
    nb = m // ATTN_BLK
    scale = HEAD ** -0.5

    def body(x_ref, p_ref, nx_ref, o_ref, do_ref, l_ref, on_ref, don_ref, ln_ref, *rest):
        dq_ref, dk_ref, dv_ref = rest[-3:]
        b = pl.program_id(1)
        dk_ref[...] = jnp.zeros_like(dk_ref)
        dv_ref[...] = jnp.zeros_like(dv_ref)

        def grads(q, k, v, o, do, lse, valid):
            sc = jnp.where(valid, _dot_nt(q, k) * scale, -1e30)
            p = jnp.exp(sc - lse)
            delta = jnp.sum(do * o, axis=-1, keepdims=True)
            return p, p * (_dot_nt(do, v) - delta) * scale

        def block(i, first):
            r0 = 0 if first else pl.multiple_of(i * ATTN_BLK, ATTN_BLK)
            rows = pl.ds(r0, ATTN_BLK)
            valid = _band_mask(b * n + i)
            if not first:
                krows = pl.ds(pl.multiple_of(i * ATTN_BLK - ATTN_BLK, ATTN_BLK), 2 * ATTN_BLK)
            for h in range(ATTN_W // HEAD):
                qs, ks, vs = _head_cols(h)
                q = x_ref[rows, qs]
                do = do_ref[rows, qs]
                if first:
                    k = jnp.concatenate([p_ref[:, ks], x_ref[0:ATTN_BLK, ks]], axis=0)
                    v = jnp.concatenate([p_ref[:, vs], x_ref[0:ATTN_BLK, vs]], axis=0)
                else:
                    k = x_ref[krows, ks]
                    v = x_ref[krows, vs]
                p, ds = grads(q, k, v, o_ref[rows, qs], do, l_ref[rows, qs][:, 0:1], valid)
                dq_ref[rows, qs] = _dot(ds, k)
                if first:
                    dk_ref[0:ATTN_BLK, qs] += _dot_tn(ds[:, ATTN_BLK:], q)
                    dv_ref[0:ATTN_BLK, qs] += _dot_tn(p[:, ATTN_BLK:], do)
                else:
                    dk_ref[krows, qs] += _dot_tn(ds, q)
                    dv_ref[krows, qs] += _dot_tn(p, do)

        block(0, True)
        if n > 1:
            def loop(i, carry):
                block(i, False)
                return carry
            lax.fori_loop(1, n, loop, 0)

        last = slice((n - 1) * ATTN_BLK, n * ATTN_BLK)
        qi = lax.broadcasted_iota(jnp.int32, (ATTN_BLK, ATTN_BLK), 0)
        ki = lax.broadcasted_iota(jnp.int32, (ATTN_BLK, ATTN_BLK), 1)
        valid_next = (qi <= ki) & ((b + 1) * n < nb)
        for h in range(ATTN_W // HEAD):
            qs, ks, vs = _head_cols(h)
            q = nx_ref[:, qs]
            do = don_ref[:, qs]
            p, ds = grads(q, x_ref[last, ks], x_ref[last, vs], on_ref[:, qs], do, ln_ref[:, qs][:, 0:1],
                          valid_next)
            dk_ref[last, qs] += _dot_tn(ds, q)
            dv_ref[last, qs] += _dot_tn(p, do)

    nxt = lambda b: jnp.minimum((b + 1) * n, nb - 1)
    xs = pl.BlockSpec((tq, w3), lambda c, b: (b, c))
    xp = pl.BlockSpec((ATTN_BLK, w3), lambda c, b: (jnp.maximum(b * n - 1, 0), c))
    xn = pl.BlockSpec((ATTN_BLK, w3), lambda c, b: (nxt(b), c))
    a = pl.BlockSpec((tq, ATTN_W), lambda c, b: (b, c))
    an = pl.BlockSpec((ATTN_BLK, ATTN_W), lambda c, b: (nxt(b), c))
    shp = jax.ShapeDtypeStruct((m, d * ATTN_W), F32)
    dep_specs, dep_ops = _dep_args(dep)
    return pl.pallas_call(
        body, out_shape=(shp, shp, shp), grid=(d, m // tq),
        in_specs=[xs, xp, xn, a, a, a, an, an, an] + dep_specs, out_specs=(a, a, a), name=name,
        compiler_params=_cparams(("parallel", "parallel")))(qv, qv, qv, ov, dov, lv, ov, dov, lv, *dep_ops)


def _rms(x, g):
    ms = jnp.mean(x * x, axis=-1, keepdims=True)
    return x * lax.rsqrt(ms + RMS_EPS) * g


def _rms_bwd(x, g, dy):
    ms = jnp.mean(x * x, axis=-1, keepdims=True)
    r = lax.rsqrt(ms + RMS_EPS)
    dyg = dy * g
    dx = r * dyg - x * (r * r * r) * jnp.mean(x * dyg, axis=-1, keepdims=True)
    return dx, dy * x * r


def _mix_fwd(outs, lses, lru, s5, g, name):
    s = lru.shape[0]
    tm = 256

    def body(o1, o2, o3, l1, l2, l3, lru_ref, s5_ref, g_ref, mixed_ref, mixed_t_ref, ov1, ov2, ov3,
             lv1, lv2, lv3, so2, so3, sl2, sl3):
        for d, src, dst in ((DILATIONS[1], o2, so2), (DILATIONS[2], o3, so3),
                            (DILATIONS[1], l2, sl2), (DILATIONS[2], l3, sl3)):
            _to_tokens(src, dst, d, tm)
        a1, a2, a3 = l1[...], _token_value(sl2), _token_value(sl3)
        mx = jnp.maximum(jnp.maximum(a1, a2), a3)
        e1, e2, e3 = jnp.exp(a1 - mx), jnp.exp(a2 - mx), jnp.exp(a3 - mx)
        den = e1 + e2 + e3
        o = (e1 * o1[...] + e2 * _token_value(so2) + e3 * _token_value(so3)) / den
        lse = mx + jnp.log(den)
        ov1[...] = o
        lv1[...] = lse
        for j in range(ATTN_W // LANE):
            so2[j] = o[:, j * LANE:(j + 1) * LANE]
            sl2[j] = lse[:, j * LANE:(j + 1) * LANE]
        for d, o_dst, l_dst in ((DILATIONS[1], ov2, lv2), (DILATIONS[2], ov3, lv3)):
            _to_dilated(so2, o_dst, d, tm)
            _to_dilated(sl2, l_dst, d, tm)
        gg = g_ref[...]
        mixed = jnp.concatenate([_rms(o, gg[:, :ATTN_W]),
                                 _rms(lru_ref[...], gg[:, ATTN_W:ATTN_W + LRU_W]),
                                 _rms(s5_ref[...], gg[:, ATTN_W + LRU_W:])], axis=1)
        mixed_ref[...] = mixed.astype(BF16)
        mixed_t_ref[...] = mixed.T.astype(BF16)

    a = pl.BlockSpec((tm, ATTN_W), lambda i: (i, 0))
    s5s = pl.BlockSpec((tm, S5_W), lambda i: (i, 0))
    full = pl.BlockSpec((tm, D_MODEL), lambda i: (i, 0))
    vec = pl.BlockSpec((1, D_MODEL), lambda i: (0, 0))
    dil = [_dilated_spec(tm, d, ATTN_W) for d in DILATIONS]
    dshape = [jax.ShapeDtypeStruct((s // d, d * ATTN_W), F32) for d in DILATIONS]
    res = pl.pallas_call(
        body, out_shape=(jax.ShapeDtypeStruct((s, D_MODEL), BF16), jax.ShapeDtypeStruct((D_MODEL, s), BF16),
                         *dshape, *dshape),
        grid=(s // tm,), in_specs=dil + dil + [a, s5s, vec],
        out_specs=(full, pl.BlockSpec((D_MODEL, tm), lambda i: (0, i)), *dil, *dil),
        scratch_shapes=[_token_scratch(tm, ATTN_W)] * 4, name=name,
        compiler_params=_cparams(("parallel",)))(*outs, *lses, lru, s5, g)
    return res[0], res[1], res[2:5], res[5:8]


def _mix_bwd(dmixed, o, lru, s5, g, name):
    s = lru.shape[0]
    tm = 256

    def body(dm_ref, o_ref, lru_ref, s5_ref, g_ref, do_ref, do2_ref, do3_ref, dlru_ref, ds5_ref, dg_ref, stage):
        @pl.when(pl.program_id(0) == 0)
        def _():
            dg_ref[...] = jnp.zeros_like(dg_ref)
        gg = g_ref[...]
        dm = dm_ref[...]
        dx, dgr = _rms_bwd(o_ref[...], gg[:, :ATTN_W], dm[:, :ATTN_W])
        do_ref[...] = dx
        for j in range(ATTN_W // LANE):
            stage[j] = dx[:, j * LANE:(j + 1) * LANE]
        _to_dilated(stage, do2_ref, DILATIONS[1], tm)
        _to_dilated(stage, do3_ref, DILATIONS[2], tm)
        dg_ref[:, :ATTN_W] += jnp.sum(dgr, axis=0, keepdims=True)
        dx, dgr = _rms_bwd(lru_ref[...], gg[:, ATTN_W:ATTN_W + LRU_W], dm[:, ATTN_W:ATTN_W + LRU_W])
        dlru_ref[...] = dx
        dg_ref[:, ATTN_W:ATTN_W + LRU_W] += jnp.sum(dgr, axis=0, keepdims=True)
        dx, dgr = _rms_bwd(s5_ref[...], gg[:, ATTN_W + LRU_W:], dm[:, ATTN_W + LRU_W:])
        ds5_ref[...] = dx
        dg_ref[:, ATTN_W + LRU_W:] += jnp.sum(dgr, axis=0, keepdims=True)

    a = pl.BlockSpec((tm, ATTN_W), lambda i: (i, 0))
    s5s = pl.BlockSpec((tm, S5_W), lambda i: (i, 0))
    full = pl.BlockSpec((tm, D_MODEL), lambda i: (i, 0))
    vec = pl.BlockSpec((1, D_MODEL), lambda i: (0, 0))
    dil = [_dilated_spec(tm, d, ATTN_W) for d in DILATIONS]
    dshape = [jax.ShapeDtypeStruct((s // d, d * ATTN_W), F32) for d in DILATIONS]
    res = pl.pallas_call(
        body, out_shape=(*dshape, jax.ShapeDtypeStruct((s, LRU_W), F32),
                         jax.ShapeDtypeStruct((s, S5_W), F32), jax.ShapeDtypeStruct((1, D_MODEL), F32)),
        grid=(s // tm,), in_specs=[full, a, a, s5s, vec], out_specs=(*dil, a, s5s, vec),
        scratch_shapes=[_token_scratch(tm, ATTN_W)], name=name,
        compiler_params=_cparams(("arbitrary",)))(dmixed, o, lru, s5, g)
    return res[0:3], res[3], res[4], res[5]


def _lru_gate_math(xc, pre_r, pre_i, lam):
    r = _sigmoid(pre_r)
    i = _sigmoid(pre_i)
    log_a = -LRU_C * r * _softplus(-lam)
    a = jnp.exp(log_a)
    u = jnp.sqrt(-_expm1(2.0 * log_a)) * (i * xc)
    return a, u


def _lru_conv(x, prev8, cw, cb):
    y = cb + cw[LRU_CONV - 1:LRU_CONV, :] * x
    for k in range(LRU_CONV - 1):
        y = y + cw[k:k + 1, :] * _shift_down_prev(x, LRU_CONV - 1 - k, prev8)
    return y


def _lru_specs(s):
    xo = 3 * ATTN_W // LANE
    go = xo + LRU_W // LANE
    xr = pl.BlockSpec((s, LANE), lambda j: (0, xo + j))
    gt = pl.BlockSpec((s, LANE), lambda j: (0, go + j))
    cw = pl.BlockSpec((LRU_CONV, LANE), lambda j: (0, j))
    vec = pl.BlockSpec((1, LANE), lambda j: (0, j))
    wbd = pl.BlockSpec((LANE, LANE), lambda j: (j, j))
    col = pl.BlockSpec((s, LANE), lambda j: (0, j))
    return xr, gt, cw, vec, wbd, col


def _lru_fwd(proj, cw, cb, wr, br, wi, bi, lam, name):
    s = proj.shape[0]
    t = SCAN_T

    def body(xr_ref, gt_ref, cw_ref, cb_ref, wr_ref, br_ref, wi_ref, bi_ref, lam_ref, o_ref):
        cwv, cbv, lamv = cw_ref[...], cb_ref[...], lam_ref[...]
        wrv, wiv, brv, biv = wr_ref[...], wi_ref[...], br_ref[...], bi_ref[...]

        def chunk(c, carry):
            h_c, prev8 = carry
            rows = pl.ds(pl.multiple_of(c * t, t), t)
            x = xr_ref[rows, :]
            xc = _lru_conv(x, prev8, cwv, cbv)
            a, u = _lru_gate_math(xc, _dot(xc, wrv) + brv, _dot(xc, wiv) + biv, lamv)
            h = _scan_chunk(a, u, h_c)
            o_ref[rows, :] = h * _gelu(gt_ref[rows, :])
            return h[t - 1:t, :], x[t - 8:t, :]

        lax.fori_loop(0, s // t, chunk, (jnp.zeros((1, LANE), F32), jnp.zeros((8, LANE), F32)))

    xr, gt, cws, vec, wbd, col = _lru_specs(s)
    return pl.pallas_call(
        body, out_shape=jax.ShapeDtypeStruct((s, LRU_W), F32), grid=(LRU_W // LANE,),
        in_specs=[xr, gt, cws, vec, wbd, vec, wbd, vec, vec], out_specs=col, name=name,
        compiler_params=_cparams(("parallel",)))(proj, proj, cw, cb, wr, br, wi, bi, lam)


def _lru_bwd(proj, dout, cw, cb, wr, br, wi, bi, lam, name):
    s = proj.shape[0]
    t = SCAN_T
    nc = s // t

    def body(xr_ref, gt_ref, do_ref, cw_ref, cb_ref, wr_ref, br_ref, wi_ref, bi_ref, lam_ref,
             dxr_ref, dgt_ref, dcw_ref, dcb_ref, dwr_ref, dbr_ref, dwi_ref, dbi_ref, dlam_ref,
             xc_s, a_s, h_s):
        cwv, cbv, lamv = cw_ref[...], cb_ref[...], lam_ref[...]
        wrv, wiv, brv, biv = wr_ref[...], wi_ref[...], br_ref[...], bi_ref[...]

        def fchunk(c, carry):
            h_c, prev8 = carry
            rows = pl.ds(pl.multiple_of(c * t, t), t)
            x = xr_ref[rows, :]
            xc = _lru_conv(x, prev8, cwv, cbv)
            a, u = _lru_gate_math(xc, _dot(xc, wrv) + brv, _dot(xc, wiv) + biv, lamv)
            h = _scan_chunk(a, u, h_c)
            xc_s[rows, :] = xc
            a_s[rows, :] = a
            h_s[rows, :] = h
            return h[t - 1:t, :], x[t - 8:t, :]

        lax.fori_loop(0, nc, fchunk, (jnp.zeros((1, LANE), F32), jnp.zeros((8, LANE), F32)))

        z1 = jnp.zeros((1, LANE), F32)
        zw = jnp.zeros((LANE, LANE), F32)

        def bchunk(ci, carry):
            g_next, a_next, dxc_next8, dcw, dcb, dwr, dbr, dwi, dbi, dlam = carry
            c = nc - 1 - ci
            t0 = pl.multiple_of(c * t, t)
            rows = pl.ds(t0, t)
            before = pl.ds(pl.multiple_of(jnp.maximum(t0 - 8, 0), 8), 8)
            has_prev = (c > 0).astype(F32)
            x, gt, do = xr_ref[rows, :], gt_ref[rows, :], do_ref[rows, :]
            xc, a, h = xc_s[rows, :], a_s[rows, :], h_s[rows, :]
            prev8_x = xr_ref[before, :] * has_prev
            prev8_h = h_s[before, :] * has_prev
            dgt_ref[rows, :] = do * h * _gelu_grad(gt)
            dh = do * _gelu(gt)
            a_plus = _shift_up_next(a, 1, jnp.broadcast_to(a_next, (8, LANE)))
            g = _scan_chunk(a_plus, dh, g_next, reverse=True)
            da = g * _shift_down_prev(h, 1, prev8_h)
            pre_r = _dot(xc, wrv) + brv
            pre_i = _dot(xc, wiv) + biv
            _, vjp = jax.vjp(_lru_gate_math, xc, pre_r, pre_i, lamv)
            dxc, dpre_r, dpre_i, dlam_c = vjp((da, g))
            dxc = dxc + _dot_nt(dpre_r, wrv) + _dot_nt(dpre_i, wiv)
            dx = cwv[LRU_CONV - 1:LRU_CONV, :] * dxc
            dcw_rows = [None] * LRU_CONV
            dcw_rows[LRU_CONV - 1] = jnp.sum(dxc * x, axis=0, keepdims=True)
            for k in range(LRU_CONV - 1):
                sh = LRU_CONV - 1 - k
                dx = dx + cwv[k:k + 1, :] * _shift_up_next(dxc, sh, dxc_next8)
                dcw_rows[k] = jnp.sum(dxc * _shift_down_prev(x, sh, prev8_x), axis=0, keepdims=True)
            dxr_ref[rows, :] = dx
            return (g[0:1, :], a[0:1, :], dxc[0:8, :],
                    dcw + jnp.concatenate(dcw_rows, axis=0),
                    dcb + jnp.sum(dxc, axis=0, keepdims=True),
                    dwr + _dot_tn(xc, dpre_r), dbr + jnp.sum(dpre_r, axis=0, keepdims=True),
                    dwi + _dot_tn(xc, dpre_i), dbi + jnp.sum(dpre_i, axis=0, keepdims=True),
                    dlam + dlam_c)

        init = (z1, z1, jnp.zeros((8, LANE), F32), jnp.zeros((LRU_CONV, LANE), F32), z1, zw, z1, zw, z1, z1)
        res = lax.fori_loop(0, nc, bchunk, init)
        dcw_ref[...] = res[3]
        dcb_ref[...] = res[4]
        dwr_ref[...] = res[5]
        dbr_ref[...] = res[6]
        dwi_ref[...] = res[7]
        dbi_ref[...] = res[8]
        dlam_ref[...] = res[9]

    xr, gt, cws, vec, wbd, col = _lru_specs(s)
    vshape = jax.ShapeDtypeStruct((1, LRU_W), F32)
    wshape = jax.ShapeDtypeStruct((LRU_W, LRU_W), F32)
    return pl.pallas_call(
        body,
        out_shape=(jax.ShapeDtypeStruct((s, LRU_W), F32), jax.ShapeDtypeStruct((s, LRU_W), F32),
                   jax.ShapeDtypeStruct((LRU_CONV, LRU_W), F32), vshape, wshape, vshape, wshape, vshape, vshape),
        grid=(LRU_W // LANE,),
        in_specs=[xr, gt, col, cws, vec, wbd, vec, wbd, vec, vec],
        out_specs=(col, col, cws, vec, wbd, vec, wbd, vec, vec),
        scratch_shapes=[pltpu.VMEM((s, LANE), F32)] * 3, name=name,
        compiler_params=_cparams(("parallel",)))(proj, proj, dout, cw, cb, wr, br, wi, bi, lam)


def _s5_disc_math(a_re, a_im, log_step, bt_re, bt_im):
    step = jnp.exp(log_step)
    dt_re, dt_im = step * a_re, step * a_im
    mag = jnp.exp(dt_re)
    ab_re, ab_im = mag * jnp.cos(dt_im), mag * jnp.sin(dt_im)
    z_re, z_im = ab_re - 1.0, ab_im
    den = a_re * a_re + a_im * a_im
    f_re = (z_re * a_re + z_im * a_im) / den
    f_im = (z_im * a_re - z_re * a_im) / den
    bb_re = f_re * bt_re - f_im * bt_im
    bb_im = f_re * bt_im + f_im * bt_re
    return ab_re, ab_im, bb_re, bb_im


def _s5_disc_fwd(a_re, a_im, log_step, bt_re, bt_im, name):
    def body(ar, ai, ls, br, bi, o1, o2, o3, o4):
        r = _s5_disc_math(ar[...], ai[...], ls[...], br[...], bi[...])
        o1[...], o2[...], o3[...], o4[...] = r

    shp = jax.ShapeDtypeStruct(a_re.shape, F32)
    return pl.pallas_call(body, out_shape=(shp,) * 4, name=name)(a_re, a_im, log_step, bt_re, bt_im)


def _s5_disc_bwd(a_re, a_im, log_step, bt_re, bt_im, cts, name):
    def body(ar, ai, ls, br, bi, c1, c2, c3, c4, o1, o2, o3, o4, o5):
        _, vjp = jax.vjp(_s5_disc_math, ar[...], ai[...], ls[...], br[...], bi[...])
        r = vjp((c1[...], c2[...], c3[...], c4[...]))
        o1[...], o2[...], o3[...], o4[...], o5[...] = r

    shp = jax.ShapeDtypeStruct(a_re.shape, F32)
    return pl.pallas_call(body, out_shape=(shp,) * 5, name=name)(a_re, a_im, log_step, bt_re, bt_im, *cts)


def _s5_u_specs(s):
    uo = (3 * ATTN_W + 2 * LRU_W) // LANE
    return (pl.BlockSpec((s, LANE), lambda j: (0, uo)), pl.BlockSpec((s, LANE), lambda j: (0, uo + 1)))


def _s5_scan_fwd(proj, b_re, b_im, lam_re, lam_im, c_re, c_im, name):
    s = proj.shape[0]
    t = SCAN_T

    def body(u0_ref, u1_ref, bre_ref, bim_ref, lre_ref, lim_ref, cre_ref, cim_ref, xre_ref, xim_ref, y_ref):
        @pl.when(pl.program_id(0) == 0)
        def _():
            y_ref[...] = jnp.zeros_like(y_ref)
        lr, li = lre_ref[...], lim_ref[...]
        consts = _cscan_consts(lr, li, False)
        bre, bim, cre, cim = bre_ref[...], bim_ref[...], cre_ref[...], cim_ref[...]

        def chunk(c, carry):
            cr, ci = carry
            rows = pl.ds(pl.multiple_of(c * t, t), t)
            u = jnp.concatenate([u0_ref[rows, :], u1_ref[rows, :]], axis=1).astype(BF16)
            xr, xi = _cscan_chunk(_dot(u, bre), _dot(u, bim), consts, (cr, ci))
            xre_ref[rows, :] = xr
            xim_ref[rows, :] = xi
            y_ref[rows, :] += _dot(xr, cre) - _dot(xi, cim)
            return xr[t - 1:t, :], xi[t - 1:t, :]

        z = jnp.zeros((1, S5_BLK), F32)
        lax.fori_loop(0, s // t, chunk, (z, z))

    u0, u1 = _s5_u_specs(s)
    bsp = pl.BlockSpec((S5_W, S5_BLK), lambda j: (0, j))
    csp = pl.BlockSpec((S5_BLK, S5_W), lambda j: (j, 0))
    vec = pl.BlockSpec((1, S5_BLK), lambda j: (0, j))
    xsp = pl.BlockSpec((s, S5_BLK), lambda j: (0, j))
    ysp = pl.BlockSpec((s, S5_W), lambda j: (0, 0))
    xshape = jax.ShapeDtypeStruct((s, S5_STATES), F32)
    return pl.pallas_call(
        body, out_shape=(xshape, xshape, jax.ShapeDtypeStruct((s, S5_W), F32)),
        grid=(S5_STATES // S5_BLK,), in_specs=[u0, u1, bsp, bsp, vec, vec, csp, csp],
        out_specs=(xsp, xsp, ysp), name=name,
        compiler_params=_cparams(("arbitrary",)))(proj, proj, b_re, b_im, lam_re, lam_im, c_re, c_im)


def _s5_scan_bwd(proj, dy, du_init, x_re, x_im, b_re, b_im, lam_re, lam_im, c_re, c_im, name):
    s = proj.shape[0]
    t = SCAN_T
    nc = s // t

    def body(u0_ref, u1_ref, dy_ref, dui_ref, xre_ref, xim_ref, bre_ref, bim_ref, lre_ref, lim_ref,
             cre_ref, cim_ref, du_ref, dlr_ref, dli_ref, dbr_ref, dbi_ref, dcr_ref, dci_ref):
        @pl.when(pl.program_id(0) == 0)
        def _():
            du_ref[...] = dui_ref[...]
        mr, mi = lre_ref[...], -lim_ref[...]
        consts = _cscan_consts(mr, mi, True)
        bre, bim, cre, cim = bre_ref[...], bim_ref[...], cre_ref[...], cim_ref[...]
        dbr_ref[...] = jnp.zeros_like(dbr_ref)
        dbi_ref[...] = jnp.zeros_like(dbi_ref)
        dcr_ref[...] = jnp.zeros_like(dcr_ref)
        dci_ref[...] = jnp.zeros_like(dci_ref)

        def chunk(ci_, carry):
            gnr, gni, dlr, dli = carry
            c = nc - 1 - ci_
            t0 = pl.multiple_of(c * t, t)
            rows = pl.ds(t0, t)
            before = pl.ds(pl.multiple_of(jnp.maximum(t0 - 8, 0), 8), 8)
            has_prev = (c > 0).astype(F32)
            dyc = dy_ref[rows, :].astype(BF16)
            u = jnp.concatenate([u0_ref[rows, :], u1_ref[rows, :]], axis=1).astype(BF16)
            gr, gi = _cscan_chunk(_dot_nt(dyc, cre), -_dot_nt(dyc, cim), consts, (gnr, gni), reverse=True)
            xr, xi = xre_ref[rows, :], xim_ref[rows, :]
            xpr = _shift_down_prev(xr, 1, xre_ref[before, :] * has_prev)
            xpi = _shift_down_prev(xi, 1, xim_ref[before, :] * has_prev)
            dlr = dlr + jnp.sum(gr * xpr + gi * xpi, axis=0, keepdims=True)
            dli = dli + jnp.sum(gi * xpr - gr * xpi, axis=0, keepdims=True)
            du_ref[rows, :] += _dot_nt(gr, bre) + _dot_nt(gi, bim)
            dbr_ref[...] += _dot_tn(u, gr)
            dbi_ref[...] += _dot_tn(u, gi)
            dcr_ref[...] += _dot_tn(xr, dyc)
            dci_ref[...] -= _dot_tn(xi, dyc)
            return gr[0:1, :], gi[0:1, :], dlr, dli

        z = jnp.zeros((1, S5_BLK), F32)
        res = lax.fori_loop(0, nc, chunk, (z, z, z, z))
        dlr_ref[...] = res[2]
        dli_ref[...] = res[3]

    u0, u1 = _s5_u_specs(s)
    bsp = pl.BlockSpec((S5_W, S5_BLK), lambda j: (0, j))
    csp = pl.BlockSpec((S5_BLK, S5_W), lambda j: (j, 0))
    vec = pl.BlockSpec((1, S5_BLK), lambda j: (0, j))
    xsp = pl.BlockSpec((s, S5_BLK), lambda j: (0, j))
    ysp = pl.BlockSpec((s, S5_W), lambda j: (0, 0))
    return pl.pallas_call(
        body,
        out_shape=(jax.ShapeDtypeStruct((s, S5_W), F32),
                   jax.ShapeDtypeStruct((1, S5_STATES), F32), jax.ShapeDtypeStruct((1, S5_STATES), F32),
                   jax.ShapeDtypeStruct((S5_W, S5_STATES), F32), jax.ShapeDtypeStruct((S5_W, S5_STATES), F32),
                   jax.ShapeDtypeStruct((S5_STATES, S5_W), F32), jax.ShapeDtypeStruct((S5_STATES, S5_W), F32)),
        grid=(S5_STATES // S5_BLK,),
        in_specs=[u0, u1, ysp, ysp, xsp, xsp, bsp, bsp, vec, vec, csp, csp],
        out_specs=(ysp, vec, vec, bsp, bsp, csp, csp), name=name,
        compiler_params=_cparams(("arbitrary",)))(
            proj, proj, dy, du_init, x_re, x_im, b_re, b_im, lam_re, lam_im, c_re, c_im)


def _s5_out_fwd(proj, y_acc, dvec, w_glu, b_glu, name):
    s = proj.shape[0]
    tm = 512
    uo = (3 * ATTN_W + 2 * LRU_W) // LANE

    def body(u0_ref, u1_ref, y_ref, d_ref, w_ref, b_ref, o_ref, yp_ref):
        u = jnp.concatenate([u0_ref[...], u1_ref[...]], axis=1)
        y = y_ref[...] + d_ref[...] * u
        yp_ref[...] = y
        yg = _gelu(y)
        o_ref[...] = yg * _sigmoid(_dot(yg, w_ref[...]) + b_ref[...])

    u0 = pl.BlockSpec((tm, LANE), lambda i: (i, uo))
    u1 = pl.BlockSpec((tm, LANE), lambda i: (i, uo + 1))
    row = pl.BlockSpec((tm, S5_W), lambda i: (i, 0))
    vec = pl.BlockSpec((1, S5_W), lambda i: (0, 0))
    wsp = pl.BlockSpec((S5_W, S5_W), lambda i: (0, 0))
    shp = jax.ShapeDtypeStruct((s, S5_W), F32)
    return pl.pallas_call(
        body, out_shape=(shp, shp), grid=(s // tm,), in_specs=[u0, u1, row, vec, wsp, vec],
        out_specs=(row, row), name=name,
        compiler_params=_cparams(("parallel",)))(proj, proj, y_acc, dvec, w_glu, b_glu)


def _s5_out_bwd(proj, y_pre, dout, dvec, w_glu, b_glu, name):
    s = proj.shape[0]
    tm = 512
    uo = (3 * ATTN_W + 2 * LRU_W) // LANE

    def body(u0_ref, u1_ref, y_ref, do_ref, d_ref, w_ref, b_ref, dy_ref, dud_ref, dd_ref, dw_ref, db_ref):
        @pl.when(pl.program_id(0) == 0)
        def _():
            dd_ref[...] = jnp.zeros_like(dd_ref)
            dw_ref[...] = jnp.zeros_like(dw_ref)
            db_ref[...] = jnp.zeros_like(db_ref)
        u = jnp.concatenate([u0_ref[...], u1_ref[...]], axis=1)
        y = y_ref[...]
        do = do_ref[...]
        yg = _gelu(y)
        sg = _sigmoid(_dot(yg, w_ref[...]) + b_ref[...])
        dz = do * yg * sg * (1.0 - sg)
        dyg = do * sg + _dot_nt(dz, w_ref[...])
        dy = dyg * _gelu_grad(y)
        dy_ref[...] = dy
        dud_ref[...] = d_ref[...] * dy
        dd_ref[...] += jnp.sum(dy * u, axis=0, keepdims=True)
        dw_ref[...] += _dot_tn(yg, dz)
        db_ref[...] += jnp.sum(dz, axis=0, keepdims=True)

    u0 = pl.BlockSpec((tm, LANE), lambda i: (i, uo))
    u1 = pl.BlockSpec((tm, LANE), lambda i: (i, uo + 1))
    row = pl.BlockSpec((tm, S5_W), lambda i: (i, 0))
    vec = pl.BlockSpec((1, S5_W), lambda i: (0, 0))
    wsp = pl.BlockSpec((S5_W, S5_W), lambda i: (0, 0))
    shp = jax.ShapeDtypeStruct((s, S5_W), F32)
    vshape = jax.ShapeDtypeStruct((1, S5_W), F32)
    return pl.pallas_call(
        body, out_shape=(shp, shp, vshape, jax.ShapeDtypeStruct((S5_W, S5_W), F32), vshape),
        grid=(s // tm,), in_specs=[u0, u1, row, row, vec, wsp, vec],
        out_specs=(row, row, vec, wsp, vec), name=name,
        compiler_params=_cparams(("arbitrary",)))(proj, proj, y_pre, dout, dvec, w_glu, b_glu)


def _ffn_conv(x, prev8, cw, cb):
    y = cb + cw[FFN_CONV - 1:FFN_CONV, :] * x
    for k in range(FFN_CONV - 1):
        y = y + cw[k:k + 1, :] * _shift_down_prev(x, FFN_CONV - 1 - k, prev8)
    return y


def _ffn_act_fwd(up, cw, cb, name):
    s = up.shape[0]
    tm = 256
    tb = 2 * FFN_CB

    def body(x_ref, p_ref, cw_ref, cb_ref, o_ref, ot_ref):
        prev8 = p_ref[...] * (pl.program_id(1) > 0).astype(F32)
        y = _ffn_conv(x_ref[...], prev8, cw_ref[...], cb_ref[...])
        act = _gelu(y[:, :FFN_CB]) * y[:, FFN_CB:]
        o_ref[...] = act.astype(BF16)
        ot_ref[...] = act.T.astype(BF16)

    main = pl.BlockSpec((tm, tb), lambda j, i: (i, j))
    prev = pl.BlockSpec((8, tb), lambda j, i: (jnp.maximum(i * (tm // 8) - 1, 0), j))
    return pl.pallas_call(
        body, out_shape=(jax.ShapeDtypeStruct((s, D_FF), BF16), jax.ShapeDtypeStruct((D_FF, s), BF16)),
        grid=(D_FF // FFN_CB, s // tm),
        in_specs=[main, prev, pl.BlockSpec((FFN_CONV, tb), lambda j, i: (0, j)),
                  pl.BlockSpec((1, tb), lambda j, i: (0, j))],
        out_specs=(pl.BlockSpec((tm, FFN_CB), lambda j, i: (i, j)), pl.BlockSpec((FFN_CB, tm), lambda j, i: (j, i))),
        name=name, compiler_params=_cparams(("parallel", "parallel")))(up, up, cw, cb)


def _ffn_act_bwd(up, dact, cw, cb, name):
    s = up.shape[0]
    tm = 256
    tb = 2 * FFN_CB
    nr = s // tm

    def body(x_ref, p_ref, n_ref, da_ref, dan_ref, cw_ref, cb_ref, dup_ref, dcw_ref, dcb_ref):
        i = pl.program_id(1)

        @pl.when(i == 0)
        def _():
            dcw_ref[...] = jnp.zeros_like(dcw_ref)
            dcb_ref[...] = jnp.zeros_like(dcb_ref)
        has_next = (i < nr - 1).astype(F32)
        prev8 = p_ref[...] * (i > 0).astype(F32)
        cwv = cw_ref[...]
        x = x_ref[...]
        xe = jnp.concatenate([x, n_ref[...]], axis=0)
        dae = jnp.concatenate([da_ref[...], dan_ref[...] * has_next], axis=0)
        shifted = [_shift_down_prev(xe, FFN_CONV - 1 - k, prev8) for k in range(FFN_CONV - 1)]
        y = cb_ref[...] + cwv[FFN_CONV - 1:FFN_CONV, :] * xe
        for k in range(FFN_CONV - 1):
            y = y + cwv[k:k + 1, :] * shifted[k]
        gate, val = y[:, :FFN_CB], y[:, FFN_CB:]
        dy = jnp.concatenate([dae * val * _gelu_grad(gate), dae * _gelu(gate)], axis=1)
        dym = dy[:tm, :]
        dx = cwv[FFN_CONV - 1:FFN_CONV, :] * dym
        dcw_rows = [None] * FFN_CONV
        dcw_rows[FFN_CONV - 1] = jnp.sum(dym * x, axis=0, keepdims=True)
        for k in range(FFN_CONV - 1):
            sh = FFN_CONV - 1 - k
            dx = dx + cwv[k:k + 1, :] * pltpu.roll(dy, tm + 8 - sh, axis=0)[:tm, :]
            dcw_rows[k] = jnp.sum(dym * shifted[k][:tm, :], axis=0, keepdims=True)
        dup_ref[...] = dx.astype(BF16)
        dcw_ref[...] += jnp.concatenate(dcw_rows, axis=0)
        dcb_ref[...] += jnp.sum(dym, axis=0, keepdims=True)

    main = pl.BlockSpec((tm, tb), lambda j, i: (i, j))
    prev = pl.BlockSpec((8, tb), lambda j, i: (jnp.maximum(i * (tm // 8) - 1, 0), j))
    nxt = pl.BlockSpec((8, tb), lambda j, i: (jnp.minimum((i + 1) * (tm // 8), s // 8 - 1), j))
    da = pl.BlockSpec((tm, FFN_CB), lambda j, i: (i, j))
    dan = pl.BlockSpec((8, FFN_CB), lambda j, i: (jnp.minimum((i + 1) * (tm // 8), s // 8 - 1), j))
    cws = pl.BlockSpec((FFN_CONV, tb), lambda j, i: (0, j))
    cbs = pl.BlockSpec((1, tb), lambda j, i: (0, j))
    return pl.pallas_call(
        body, out_shape=(jax.ShapeDtypeStruct((s, 2 * D_FF), BF16),
                         jax.ShapeDtypeStruct((FFN_CONV, 2 * D_FF), F32),
                         jax.ShapeDtypeStruct((1, 2 * D_FF), F32)),
        grid=(D_FF // FFN_CB, nr), in_specs=[main, prev, nxt, da, dan, cws, cbs],
        out_specs=(main, cws, cbs), name=name,
        compiler_params=_cparams(("parallel", "arbitrary")))(up, up, up, dact, dact, cw, cb)


def _sum_partials(ld_ref):
    gg = ld_ref[0].astype(F32)
    for k in range(1, N_DEV):
        gg = gg + ld_ref[k].astype(F32)
    return gg


def _adam_update(w, g, m, v):
    mn = ADAM_B1 * m + (1.0 - ADAM_B1) * g
    vn = ADAM_B2 * v + (1.0 - ADAM_B2) * (g * g)
    m_hat = mn / (1.0 - ADAM_B1 ** ADAM_STEP)
    v_hat = vn / (1.0 - ADAM_B2 ** ADAM_STEP)
    return -ADAM_LR * (m_hat / (jnp.sqrt(v_hat) + ADAM_EPS) + ADAM_WD * w), mn, vn


def _adamw_many(landed, ws, ms, vs, name):
    n, nl = len(ws), len(landed)

    def body(*refs):
        ld = refs[:nl * n]
        w_refs, m_refs, v_refs = (refs[(nl + k) * n:(nl + k + 1) * n] for k in range(3))
        outs = refs[(nl + 3) * n:]
        for i in range(n):
            for l in range(nl):
                one = slice(l, l + 1)
                gg = _sum_partials(ld[l * n + i])
                outs[i][one] = gg
                outs[n + i][one], outs[2 * n + i][one], outs[3 * n + i][one] = _adam_update(
                    w_refs[i][one], gg, m_refs[i][one], v_refs[i][one])

    vm = pl.BlockSpec(memory_space=pltpu.VMEM)
    shapes = [jax.ShapeDtypeStruct(w.shape, F32) for w in ws] * 4
    res = pl.pallas_call(
        body, out_shape=tuple(shapes), in_specs=[vm] * ((nl + 3) * n), out_specs=tuple([vm] * (4 * n)),
        name=name, compiler_params=_cparams())(*[a for layer in landed for a in layer], *ws, *ms, *vs)
    return res[:n], res[n:2 * n], res[2 * n:3 * n], res[3 * n:]


def _adamw_sum(landed, w, m, v, layer, prev, name):
    _, r, c = landed.shape
    nl = w.shape[0]
    tm = 8
    for cand in (512, 256, 128, 64, 32, 16):
        if r % cand == 0 and N_DEV * cand * c * 4 <= 4 * 1024 * 1024:
            tm = cand
            break

    def body(*refs):
        ld_ref, w_ref, m_ref, v_ref = refs[:4]
        g_ref, d_ref, mo_ref, vo_ref = refs[-4:]
        gg = _sum_partials(ld_ref)
        g_ref[...] = gg
        d_ref[...], mo_ref[...], vo_ref[...] = _adam_update(w_ref[...], gg, m_ref[...], v_ref[...])

    blk = pl.BlockSpec((None, tm, c), lambda i: (layer, i, 0))
    in_specs = [pl.BlockSpec((N_DEV, tm, c), lambda i: (0, i, 0)), blk, blk, blk]
    args = [landed, w, m, v]
    aliases = {}
    if prev is not None:
        in_specs += [pl.BlockSpec(memory_space=pl.ANY)] * 4
        args += list(prev)
        aliases = {4 + k: k for k in range(4)}
    shp = jax.ShapeDtypeStruct((nl, r, c), F32)
    return pl.pallas_call(
        body, out_shape=(shp,) * 4, grid=(r // tm,), in_specs=in_specs, out_specs=(blk,) * 4,
        input_output_aliases=aliases, name=name, compiler_params=_cparams(("parallel",)))(*args)


def _all_gather(shards, name):
    na = len(shards)

    def body(*refs):
        x_refs, out_refs = refs[:na], refs[na:2 * na]
        send_sems, recv_sems, local_sems = refs[2 * na:]
        x, y, c = lax.axis_index("x"), lax.axis_index("y"), lax.axis_index("c")
        me, sibling = (x, y, c), (x, y, 1 - c)
        chips = [(1 - x, y), (x, 1 - y), (1 - x, 1 - y)]

        def copy(a, k, block, to, src=None):
            dst = out_refs[a].at[4 * block[0] + 2 * block[1] + block[2]]
            return pltpu.make_async_remote_copy(
                src_ref=dst if src is None else src, dst_ref=dst,
                send_sem=send_sems.at[7 * a + k], recv_sem=recv_sems.at[7 * a + k],
                device_id=to, device_id_type=pl.DeviceIdType.MESH)

        mine, first, passed = [], [], []
        for a in range(na):
            cp = pltpu.make_async_copy(x_refs[a], out_refs[a].at[4 * x + 2 * y + c], local_sems.at[a])
            cp.start()
            mine.append(cp)
            cps = [copy(a, 0, me, sibling, src=x_refs[a])]
            cps += [copy(a, 1 + j, me, (*chip, c), src=x_refs[a]) for j, chip in enumerate(chips)]
            for cp in cps:
                cp.start()
            first += cps
        for j, chip in enumerate(chips):
            for a in range(na):
                copy(a, 1 + j, (*chip, c), me).wait_recv()
                cp = copy(a, 4 + j, (*chip, c), sibling)
                cp.start()
                passed.append(cp)
        for a in range(na):
            copy(a, 0, sibling, me).wait_recv()
            for j, chip in enumerate(chips):
                copy(a, 4 + j, (*chip, 1 - c), me).wait_recv()
        for cp in first + passed:
            cp.wait_send()
        for cp in mine:
            cp.wait()

    anyspec = pl.BlockSpec(memory_space=pl.ANY)
    return pl.pallas_call(
        body, out_shape=tuple(jax.ShapeDtypeStruct((N_DEV,) + t.shape, t.dtype) for t in shards),
        in_specs=[anyspec] * na, out_specs=tuple([anyspec] * na),
        scratch_shapes=[pltpu.SemaphoreType.DMA((7 * na,)), pltpu.SemaphoreType.DMA((7 * na,)),
                        pltpu.SemaphoreType.DMA((na,))],
        name=name)(*shards)


_HBM = pl.BlockSpec(memory_space=pltpu.HBM)
_SEM = pl.BlockSpec(memory_space=pltpu.SEMAPHORE)
_EFFECT = pltpu.SideEffectType.DATAFLOW_SIDE_EFFECTING


def _exchange_copies(src_refs, land_refs, send_sems, recv_sems, local_sems, gather):
    x, y, c = lax.axis_index("x"), lax.axis_index("y"), lax.axis_index("c")
    me = 4 * x + 2 * y + c
    per_array = send_sems.shape[0] > N_DEV - 1
    local, remote = [], []
    for a, (src, land) in enumerate(zip(src_refs, land_refs)):
        local.append(pltpu.make_async_copy(src if gather else src.at[me], land.at[me],
                                           local_sems.at[a if per_array else 0]))
    for k in range(1, N_DEV):
        px = x ^ ((k >> 2) & 1)
        py = y ^ ((k >> 1) & 1)
        pc = c ^ (k & 1)
        for a, (src, land) in enumerate(zip(src_refs, land_refs)):
            remote.append(pltpu.make_async_remote_copy(
                src_ref=src if gather else src.at[4 * px + 2 * py + pc], dst_ref=land.at[me],
                send_sem=send_sems.at[(7 * a if per_array else 0) + k - 1],
                recv_sem=recv_sems.at[(7 * a if per_array else 0) + k - 1],
                device_id=(px, py, pc), device_id_type=pl.DeviceIdType.MESH))
    return local, remote


def _exchange_start(srcs, gather, name, dep=None):
    na = len(srcs)
    ns = na if na <= 4 else 1
    lands = [lax.empty(((N_DEV,) + t.shape) if gather else t.shape, t.dtype) for t in srcs]

    def body(*refs):
        src_refs, land_refs = refs[:na], refs[na:2 * na]
        nin = 2 * na + (0 if dep is None else 1)
        send_sems, recv_sems, local_sems = refs[nin:nin + 3]
        token = refs[-1]
        local, remote = _exchange_copies(src_refs, land_refs, send_sems, recv_sems, local_sems, gather)
        for cp in local + remote:
            cp.start()
        token[...] = jnp.zeros_like(token)

    dep_specs, dep_ops = _dep_args(dep)
    hbm = lambda t: pltpu.HBM(t.shape, t.dtype)
    out = pl.pallas_call(
        body, name=name,
        out_shape=(pltpu.SemaphoreType.DMA((7 * ns,)), pltpu.SemaphoreType.DMA((7 * ns,)),
                   pltpu.SemaphoreType.DMA((ns,)), *[hbm(t) for t in srcs], *[hbm(t) for t in lands],
                   jax.ShapeDtypeStruct((8, LANE), F32)),
        in_specs=[_HBM] * (2 * na) + dep_specs,
        out_specs=(_SEM, _SEM, _SEM, *[_HBM] * (2 * na), pl.BlockSpec(memory_space=pltpu.VMEM)),
        input_output_aliases={i: 3 + i for i in range(2 * na)},
        compiler_params=pltpu.CompilerParams(has_side_effects=_EFFECT),
    )(*[pltpu.with_memory_space_constraint(t, pltpu.HBM) for t in srcs + lands], *dep_ops)
    return (out[:3], out[3:3 + na], out[3 + na:3 + 2 * na]), out[-1]


def _exchange_wait(handle, gather, after, name):
    sems, srcs, lands = handle
    na = len(srcs)

    def body(*refs):
        src_refs, land_refs = refs[:na], refs[na:2 * na]
        send_sems, recv_sems, local_sems = refs[2 * na:2 * na + 3]
        local, remote = _exchange_copies(src_refs, land_refs, send_sems, recv_sems, local_sems, gather)
        for cp in remote:
            cp.wait_send()
            cp.wait_recv()
        for cp in local:
            cp.wait()

    hbm = lambda t: pltpu.HBM(t.shape, t.dtype)
    out = pl.pallas_call(
        body, name=name, out_shape=(*[hbm(t) for t in srcs], *[hbm(t) for t in lands]),
        in_specs=[_HBM] * (2 * na) + [_SEM] * 3 + [pl.BlockSpec(memory_space=pl.ANY)],
        out_specs=tuple([_HBM] * (2 * na)), input_output_aliases={i: i for i in range(2 * na)},
        compiler_params=pltpu.CompilerParams(has_side_effects=_EFFECT),
    )(*srcs, *lands, *sems, after)
    return out[na:]


def _block_diag(w):
    h, a, b = w.shape
    eye = jnp.eye(h, dtype=w.dtype)
    return (w[:, :, None, :] * eye[:, None, :, None]).reshape(h * a, h * b)


def _block_diag_extract(m, h):
    a, b = m.shape[0] // h, m.shape[1] // h
    return jnp.stack([m[i * a:(i + 1) * a, i * b:(i + 1) * b] for i in range(h)], axis=0)


def _block_diag_take(m, h):
    a, b = m.shape[0] // h, m.shape[1] // h
    eye = jnp.eye(h, dtype=m.dtype)
    return (m.reshape(h, a, h, b) * eye[:, None, :, None]).sum(axis=2)


def _ffn_interleave(w):
    lead = w.shape[:-1]
    nb = D_FF // FFN_CB
    return jnp.swapaxes(w.reshape(*lead, 2, nb, FFN_CB), -3, -2).reshape(*lead, 2 * D_FF)


def _ffn_deinterleave(w):
    lead = w.shape[:-1]
    nb = D_FF // FFN_CB
    return jnp.swapaxes(w.reshape(*lead, nb, 2, FFN_CB), -3, -2).reshape(*lead, 2 * D_FF)


def _gather_full(gathered, axis):
    shape = list(gathered.shape[1:])
    shape[axis] *= N_DEV
    return jnp.moveaxis(gathered, 0, axis).reshape(shape)


def _scatter_blocks(full, axis):
    shape = list(full.shape)
    shape[axis:axis + 1] = [N_DEV, shape[axis] // N_DEV]
    return jnp.moveaxis(full.reshape(shape), axis, 0)


def _pad_to(flat, mult):
    pad = (-flat.shape[-1]) % mult
    if pad:
        flat = jnp.concatenate([flat, jnp.zeros(flat.shape[:-1] + (pad,), flat.dtype)], axis=-1)
    return flat


def _layer_fwd(h_in, h_in_t, w, cos, sin, l, dep, get_ffn):
    tag = "l%d_" % l
    proj = _mm_nn(h_in, w['w_in'], 512, D_IN, tag + "proj", dep=dep)
    qkv = _rope_fwd(proj, cos, sin, tag + "rope")
    outs, lses = [], []
    for d, qv in zip(DILATIONS, qkv):
        o, ls = _attn_fwd(qv, d, tag + "attn_d%d" % d)
        outs.append(o)
        lses.append(ls)
    lru = _lru_fwd(proj, w['lru_conv_w'], w['lru_conv_b'], w['lru_wr'], w['lru_br'], w['lru_wi'],
                   w['lru_bi'], w['lru_lambda'], tag + "lru")
    x_re, x_im, y_acc = _s5_scan_fwd(proj, w['s5_bb_re'], w['s5_bb_im'], w['s5_lam_re'], w['s5_lam_im'],
                                     w['s5_cc_re'], w['s5_cc_im'], tag + "s5_scan")
    s5, y_pre = _s5_out_fwd(proj, y_acc, w['s5_d'], w['s5_w_glu'], w['s5_b_glu'], tag + "s5_out")
    mixed, mixed_t, attn_o, attn_lse = _mix_fwd(outs, lses, lru, s5, w['mix_norm_g'], tag + "mix")
    mixo = _mm_nn(mixed, w['w_out'], 512, D_MODEL, tag + "out_proj")
    r1, h1, h1_t = _ln_fwd(h_in, mixo, w['ln1_g'], w['ln1_b'], tag + "ln1")
    w['w_up_g'], w['w_down'], ffn_dep = get_ffn(l, h1)
    up = _mm_up(h1, w['w_up_g'], 1024, tag + "up_proj", dep=ffn_dep)
    act, act_t = _ffn_act_fwd(up, w['ffn_conv_w'], w['ffn_conv_b'], tag + "ffn_act")
    ffn = _mm_nn(act, w['w_down'], 512, D_MODEL, tag + "down_proj")
    r2, h2, *rest = _ln_fwd(h1, ffn, w['ln2_g'], w['ln2_b'], tag + "ln2", transposed=l + 1 < DEPTH)
    h2_t = rest[0] if rest else None
    saved = dict(h_in_t=h_in_t, proj=proj, qkv=qkv, lru=lru, x_re=x_re, x_im=x_im, y_pre=y_pre, s5=s5,
                 mixed_t=mixed_t, attn_o=attn_o, attn_lse=attn_lse, r1=r1, h1_t=h1_t, up=up, act_t=act_t, r2=r2)
    return h2, h2_t, saved


def _layer_bwd_ffn(dh2, sv, w, l, dep=None):
    tag = "l%d_" % l
    g = {}
    dr2, g['ln2_g'], g['ln2_b'] = _ln_bwd(sv['r2'], dh2, w['ln2_g'], tag + "ln2_bwd", dep=dep)
    g['w_down'] = _mm_dw(sv['act_t'], dr2, 1024, D_MODEL, 1024, tag + "down_dw")
    dact = _mm_nt(dr2, w['w_down'], 512, D_FF, tag + "down_dx")
    dup, g['ffn_conv_w'], g['ffn_conv_b'] = _ffn_act_bwd(sv['up'], dact, w['ffn_conv_w'], w['ffn_conv_b'],
                                                        tag + "ffn_act_bwd")
    g['w_up_g'] = _mm_up_dw(sv['h1_t'], dup, tag + "up_dw")
    dh1 = _mm_up_dx(dup, w['w_up_g'], dr2, ALPHA, 1024, tag + "up_dx")
    return dh1, g


def _layer_bwd_mix(dh1, sv, w, cos, sin, l, dep, g_ffn, after_out_grad, after_small_grads, after_in_grad):
    tag = "l%d_" % l
    g = {}
    dr1, g['ln1_g'], g['ln1_b'] = _ln_bwd(sv['r1'], dh1, w['ln1_g'], tag + "ln1_bwd", dep=dep)
    g['w_out'] = _mm_dw(sv['mixed_t'], dr1, 1024, D_MODEL, 1024, tag + "out_dw")
    dmixed = _mm_nt(dr1, w['w_out'], 512, D_MODEL, tag + "out_dx", dep=after_out_grad(l, g['w_out']))
    d_o, dlru, ds5, g['mix_norm_g'] = _mix_bwd(dmixed, sv['attn_o'][0], sv['lru'], sv['s5'], w['mix_norm_g'],
                                               tag + "mix_bwd")
    dy, dud, g['s5_d'], g['s5_w_glu'], g['s5_b_glu'] = _s5_out_bwd(
        sv['proj'], sv['y_pre'], ds5, w['s5_d'], w['s5_w_glu'], w['s5_b_glu'], tag + "s5_out_bwd")
    du, g['s5_lam_re'], g['s5_lam_im'], g['s5_bb_re'], g['s5_bb_im'], g['s5_cc_re'], g['s5_cc_im'] = \
        _s5_scan_bwd(sv['proj'], dy, dud, sv['x_re'], sv['x_im'], w['s5_bb_re'], w['s5_bb_im'],
                     w['s5_lam_re'], w['s5_lam_im'], w['s5_cc_re'], w['s5_cc_im'], tag + "s5_scan_bwd")
    (dxr, dgate, g['lru_conv_w'], g['lru_conv_b'], g['lru_wr'], g['lru_br'], g['lru_wi'], g['lru_bi'],
     g['lru_lambda']) = _lru_bwd(sv['proj'], dlru, w['lru_conv_w'], w['lru_conv_b'], w['lru_wr'],
                                 w['lru_br'], w['lru_wi'], w['lru_bi'], w['lru_lambda'], tag + "lru_bwd")
    token = after_small_grads(l, _finish_layer_grads({**g_ffn, **g}, w, l))
    dqkv = [_attn_bwd(sv['qkv'][b], sv['attn_o'][b], d_o[b], sv['attn_lse'][b], d, tag + "attn_bwd_d%d" % d,
                      dep=token if b == 0 else None)
            for b, d in enumerate(DILATIONS)]
    dproj = _dproj_assemble(dqkv, dxr, dgate, du, cos, sin, tag + "dproj")
    g_in = _mm_dw(sv['h_in_t'], dproj, 1024, D_IN, 1024, tag + "in_dw")
    return _mm_nt(dproj, w['w_in'], 512, D_MODEL, tag + "in_dx", add=dr1, add_scale=ALPHA,
                  dep=after_in_grad(l, g_in))


def _s5_rep(a):
    return jnp.repeat(a, S5_C, axis=0)


def _prepare_layer(p, l):
    w = {}
    for n in ('w_in', 'w_out', 's5_w_glu'):
        w[n] = p[n].astype(BF16)
    w['ffn_conv_w'] = _ffn_interleave(p['ffn_conv_w'])
    w['ffn_conv_b'] = _ffn_interleave(p['ffn_conv_b'])[None, :]
    w['lru_conv_w'] = p['lru_conv_w']
    for n in ('lru_conv_b', 'lru_br', 'lru_bi', 'lru_lambda', 's5_b_glu', 'mix_norm_g',
              'ln1_g', 'ln1_b', 'ln2_g', 'ln2_b'):
        w[n] = p[n][None, :]
    w['lru_wr'] = _block_diag(p['lru_wr']).astype(BF16)
    w['lru_wi'] = _block_diag(p['lru_wi']).astype(BF16)
    w['s5_d'] = p['s5_d'].reshape(1, S5_W)
    disc_in = (_s5_rep(p['s5_a_re']), _s5_rep(p['s5_a_im']),
               _s5_rep(jnp.broadcast_to(p['s5_log_step'][:, None], (S5_G, S5_P))),
               jnp.swapaxes(p['s5_b_re'], 1, 2).reshape(S5_W, S5_P),
               jnp.swapaxes(p['s5_b_im'], 1, 2).reshape(S5_W, S5_P))
    ab_re, ab_im, bb_re, bb_im = _s5_disc_fwd(*disc_in, "l%d_s5_disc" % l)
    w['s5_disc_in'] = disc_in
    w['s5_lam_re'] = ab_re.reshape(S5_G, S5_C, S5_P)[:, 0, :].reshape(1, S5_STATES)
    w['s5_lam_im'] = ab_im.reshape(S5_G, S5_C, S5_P)[:, 0, :].reshape(1, S5_STATES)
    w['s5_bb_re'] = _block_diag(bb_re.reshape(S5_G, S5_C, S5_P)).astype(BF16)
    w['s5_bb_im'] = _block_diag(bb_im.reshape(S5_G, S5_C, S5_P)).astype(BF16)
    w['s5_cc_re'] = _block_diag(jnp.swapaxes(p['s5_c_re'], 1, 2)).astype(BF16)
    w['s5_cc_im'] = _block_diag(jnp.swapaxes(p['s5_c_im'], 1, 2)).astype(BF16)
    return w


def _finish_layer_grads(g, w, l):
    out = {}
    for n in ('s5_w_glu', 'lru_conv_w'):
        out[n] = g[n]
    out['ffn_conv_w'] = _ffn_deinterleave(g['ffn_conv_w'])
    out['ffn_conv_b'] = _ffn_deinterleave(g['ffn_conv_b'])[0]
    for n in ('lru_conv_b', 'lru_br', 'lru_bi', 'lru_lambda', 's5_b_glu', 'mix_norm_g',
              'ln1_g', 'ln1_b', 'ln2_g', 'ln2_b'):
        out[n] = g[n][0]
    out['lru_wr'] = _block_diag_extract(g['lru_wr'], LRU_W // HEAD)
    out['lru_wi'] = _block_diag_extract(g['lru_wi'], LRU_W // HEAD)
    out['s5_d'] = g['s5_d'].reshape(S5_G, S5_C)
    out['s5_c_re'] = jnp.swapaxes(_block_diag_take(g['s5_cc_re'], S5_G), 1, 2)
    out['s5_c_im'] = jnp.swapaxes(_block_diag_take(g['s5_cc_im'], S5_G), 1, 2)
    rep = lambda v: _s5_rep(v.reshape(S5_G, S5_P)) * (1.0 / S5_C)
    cts = (rep(g['s5_lam_re']), rep(g['s5_lam_im']),
           _block_diag_take(g['s5_bb_re'], S5_G).reshape(S5_W, S5_P),
           _block_diag_take(g['s5_bb_im'], S5_G).reshape(S5_W, S5_P))
    da_re, da_im, dls, dbt_re, dbt_im = _s5_disc_bwd(*w['s5_disc_in'], cts, "l%d_s5_disc_bwd" % l)
    out['s5_a_re'] = da_re.reshape(S5_G, S5_C, S5_P).sum(axis=1)
    out['s5_a_im'] = da_im.reshape(S5_G, S5_C, S5_P).sum(axis=1)
    out['s5_log_step'] = dls.reshape(S5_G, S5_C * S5_P).sum(axis=1)
    out['s5_b_re'] = jnp.swapaxes(dbt_re.reshape(S5_G, S5_C, S5_P), 1, 2)
    out['s5_b_im'] = jnp.swapaxes(dbt_im.reshape(S5_G, S5_C, S5_P), 1, 2)
    return out


def _run_step(x, target, get_layer, get_ffn, after_ffn_grads, after_out_grad, after_small_grads, after_in_grad):
    cos, sin = _rope_tables(x.shape[0])
    h, h_t = x, _transpose_bf16(x, "x_transpose")
    ws, saved = [], []
    for l in range(DEPTH):
        p, dep = get_layer(l, h)
        ws.append(_prepare_layer(p, l))
        h, h_t, sv = _layer_fwd(h, h_t, ws[l], cos, sin, l, dep, get_ffn)
        saved.append(sv)
    dh, loss_vec = _loss_head(h, target)
    dep = None
    for l in reversed(range(DEPTH)):
        dh1, g = _layer_bwd_ffn(dh, saved[l], ws[l], l, dep)
        dep = after_ffn_grads(l, g)
        dh = _layer_bwd_mix(dh1, saved[l], ws[l], cos, sin, l, dep, g, after_out_grad, after_small_grads,
                            after_in_grad)
        dep = None
    return loss_vec[0, 0], dh


def _local_step(x, target, layers):
    grads = [{} for _ in range(DEPTH)]

    def ffn(l, h1):
        return layers[l]['w_up_g'].astype(BF16), layers[l]['w_down'].astype(BF16), None

    def keep_ffn(l, g):
        grads[l].update(w_up_g=g['w_up_g'], w_down=g['w_down'])

    def keep_small(l, g):
        grads[l].update(g)

    loss, dx = _run_step(x, target, lambda l, h: (layers[l], None), ffn, keep_ffn,
                         lambda l, g: grads[l].update(w_out=g), keep_small, lambda l, g: grads[l].update(w_in=g))
    return loss, dx, grads


def kernel(x, w_in, lru_conv_w, lru_conv_b, lru_wr, lru_br, lru_wi, lru_bi, lru_lambda, s5_a_re, s5_a_im, s5_b_re, s5_b_im, s5_c_re, s5_c_im, s5_d, s5_log_step, s5_w_glu, s5_b_glu, mix_norm_g, w_out, ln1_g, ln1_b, w_up, ffn_conv_w, ffn_conv_b, w_down, ln2_g, ln2_b, loss_target, m_w_in, m_lru_conv_w, m_lru_conv_b, m_lru_wr, m_lru_br, m_lru_wi, m_lru_bi, m_lru_lambda, m_s5_a_re, m_s5_a_im, m_s5_b_re, m_s5_b_im, m_s5_c_re, m_s5_c_im, m_s5_d, m_s5_log_step, m_s5_w_glu, m_s5_b_glu, m_mix_norm_g, m_w_out, m_ln1_g, m_ln1_b, m_w_up, m_ffn_conv_w, m_ffn_conv_b, m_w_down, m_ln2_g, m_ln2_b, v_w_in, v_lru_conv_w, v_lru_conv_b, v_lru_wr, v_lru_br, v_lru_wi, v_lru_bi, v_lru_lambda, v_s5_a_re, v_s5_a_im, v_s5_b_re, v_s5_b_im, v_s5_c_re, v_s5_c_im, v_s5_d, v_s5_log_step, v_s5_w_glu, v_s5_b_glu, v_mix_norm_g, v_w_out, v_ln1_g, v_ln1_b, v_w_up, v_ffn_conv_w, v_ffn_conv_b, v_w_down, v_ln2_g, v_ln2_b):
    args = locals()
    wl = {n: args[n] for n in WEIGHTS}
    ml = {n: args['m_' + n] for n in WEIGHTS}
    vl = {n: args['v_' + n] for n in WEIGHTS}

    small_sizes = [int(wl[n].size) for n in SMALL_SHARDED]
    small_flat = _pad_to(jnp.concatenate([wl[n].reshape(-1) for n in SMALL_SHARDED]), 8 * 1024)
    small_all, = _all_gather([small_flat.reshape(-1, 1024)], "gather_small")
    small_all = small_all.reshape(N_DEV, -1)
    small_full, off = {}, 0
    for n, sz in zip(SMALL_SHARDED, small_sizes):
        small_full[n] = _gather_full(small_all[:, off:off + sz].reshape((N_DEV,) + wl[n].shape), SHARD_AXIS[n])
        off += sz
    def mixer_params(l, gathered):
        g_in, g_out = gathered
        p = {n: wl[n][l] for n in REPLICATED}
        p.update({n: small_full[n][l] for n in SMALL_SHARDED})
        p['w_in'] = _gather_full(g_in, 1)
        p['w_out'] = g_out.reshape(D_MODEL, D_MODEL)
        return p

    mix_names, ffn_names = ('w_in', 'w_out'), ('w_up', 'w_down')
    shards = lambda names, l: [wl[n][l].astype(BF16) for n in names]
    mix0 = _all_gather(shards(mix_names, 0), "gather_mix_l0")
    gathers = {}
    gathers[0, 'ffn'], ffn0_token = _exchange_start(shards(ffn_names, 0), True, "gather_ffn_l0_start", dep=mix0[0])
    def get_layer(l, h):
        if l == 0:
            return mixer_params(0, mix0), ffn0_token
        return mixer_params(1, _exchange_wait(gathers[1, 'mix'], True, h, "gather_mix_l1_wait")), None

    def get_ffn(l, h1):
        g_up, g_down = _exchange_wait(gathers[l, 'ffn'], True, h1, "gather_ffn_l%d_wait" % l)
        token = None
        if l == 0:
            gathers[1, 'mix'], token = _exchange_start(shards(mix_names, 1), True, "gather_mix_l1_start", dep=g_up)
            gathers[1, 'ffn'], token = _exchange_start(shards(ffn_names, 1), True, "gather_ffn_l1_start", dep=token)
        return g_up, g_down.reshape(D_FF, D_MODEL), token

    scatters = {}

    def after_ffn_grads(l, g):
        send = [g['w_up_g'], g['w_down'].reshape(N_DEV, D_FF // N_DEV, D_MODEL)]
        scatters[l, 'ffn'], token = _exchange_start(send, False, "scatter_ffn_l%d_start" % l)
        return token

    def after_out_grad(l, g_out):
        send = [g_out.reshape(N_DEV, D_MODEL // N_DEV, D_MODEL)]
        scatters[l, 'out'], token = _exchange_start(send, False, "scatter_out_l%d_start" % l)
        return token

    def after_in_grad(l, g_in):
        send = _scatter_blocks(g_in, 1)
        if l == 0:
            send = send.astype(BF16)
        scatters[l, 'in'], token = _exchange_start([send], False, "scatter_in_l%d_start" % l)
        return token

    def after_small_grads(l, g):
        rep = [g[n][None] for n in REPLICATED]
        shd = [_scatter_blocks(g[n], SHARD_AXIS[n] - 1)[:, None] for n in SMALL_SHARDED]
        scatters[l, 'rep'], token = _exchange_start(rep, True, "gather_rep_grads_l%d_start" % l)
        scatters[l, 'small'], token = _exchange_start(shd, False, "scatter_small_l%d_start" % l, dep=token)
        return token

    loss_local, grad_x = _run_step(x[0], loss_target[0], get_layer, get_ffn, after_ffn_grads, after_out_grad,
                                   after_small_grads, after_in_grad)
    loss = lax.psum(loss_local, AXES)

    results = {}
    big_prev = {n: None for n in BIG}

    def finish_big(l, part, names, after):
        landed = _exchange_wait(scatters[l, part], False, after, "scatter_%s_l%d_wait" % (part, l))
        for n, ld in zip(names, landed):
            big_prev[n] = _adamw_sum(ld, wl[n], ml[n], vl[n], l, big_prev[n], "adamw_%s_l%d" % (n, l))

    for l, part, names in ((1, 'ffn', ffn_names), (1, 'out', ('w_out',)), (1, 'in', ('w_in',)),
                           (0, 'ffn', ffn_names), (0, 'out', ('w_out',))):
        finish_big(l, part, names, grad_x)

    kinds = ('grad', 'delta', 'm', 'v')
    landed = [dict(zip(REPLICATED + SMALL_SHARDED,
                       list(_exchange_wait(scatters[l, 'rep'], True, grad_x, "gather_rep_grads_l%d_wait" % l)) +
                       list(_exchange_wait(scatters[l, 'small'], False, grad_x, "scatter_small_l%d_wait" % l))))
              for l in range(DEPTH)]
    matrices = ['lru_wr', 'lru_wi', 's5_a_re', 's5_a_im', 's5_c_re', 's5_c_im', 's5_d']
    widest = ['s5_b_re', 's5_b_im']
    vectors = [n for n in REPLICATED + SMALL_SHARDED if n not in matrices + widest]
    last = None
    for tag, names in (("vectors", vectors), ("matrices", matrices), ("s5_b", widest)):
        res = _adamw_many([[landed[l][n] for n in names] for l in range(DEPTH)], [wl[n] for n in names],
                          [ml[n] for n in names], [vl[n] for n in names], "adamw_" + tag)
        for kind, arrs in zip(kinds, res):
            for n, a in zip(names, arrs):
                results[kind, n] = a
        last = res[0][0]
    finish_big(0, 'in', ('w_in',), last)
    for n in BIG:
        results['grad', n], results['delta', n], results['m', n], results['v', n] = big_prev[n]

    out = [loss, grad_x[None]]
    for kind in kinds:
        out.extend(results[kind, n] for n in WEIGHTS)
    return tuple(out)
```

```python
import functools
import math

import jax
import jax.numpy as jnp
from jax import lax
from jax.experimental import pallas as pl
from jax.experimental.pallas import tpu as pltpu

F32 = jnp.float32
BF16 = jnp.bfloat16

N_DEV = 8
DEPTH = 2
D_MODEL = 1024
ATTN_W = 384
LRU_W = 384
S5_W = 256
D_IN = 2176
D_FF = 3072
HEAD = 64
ATTN_BLK = 128
ATTN_TILE = 1024
DILATIONS = (1, 4, 16)
S5_G = 16
S5_P = 64
S5_C = 16
S5_STATES = S5_G * S5_P
LRU_C = 8.0
LRU_CONV = 4
FFN_CONV = 3
ROPE_THETA = 10000.0
ALPHA = (2 * DEPTH) ** 0.25
LN_EPS = 1e-5
RMS_EPS = 1e-6
ADAM_LR, ADAM_B1, ADAM_B2, ADAM_EPS, ADAM_WD, ADAM_STEP = 0.001, 0.9, 0.999, 1e-8, 0.01, 10

LANE = 128
SCAN_T = 256
S5_BLK = 256
FFN_CB = 2 * D_FF // N_DEV
VMEM_LIMIT = 56 * 1024 * 1024

AXES = ("x", "y", "c")

WEIGHTS = ['w_in', 'lru_conv_w', 'lru_conv_b', 'lru_wr', 'lru_br', 'lru_wi', 'lru_bi', 'lru_lambda',
           's5_a_re', 's5_a_im', 's5_b_re', 's5_b_im', 's5_c_re', 's5_c_im', 's5_d', 's5_log_step',
           's5_w_glu', 's5_b_glu', 'mix_norm_g', 'w_out', 'ln1_g', 'ln1_b', 'w_up', 'ffn_conv_w',
           'ffn_conv_b', 'w_down', 'ln2_g', 'ln2_b']
SHARD_AXIS = {'w_in': 2, 'lru_conv_w': 2, 's5_w_glu': 1, 'w_out': 1, 'w_up': 2, 'ffn_conv_w': 2, 'w_down': 1}
BIG = ['w_in', 'w_out', 'w_up', 'w_down']
SMALL_SHARDED = ['lru_conv_w', 'ffn_conv_w', 's5_w_glu']
REPLICATED = [n for n in WEIGHTS if n not in SHARD_AXIS]


def _cparams(sem=None):
    return pltpu.CompilerParams(dimension_semantics=sem, vmem_limit_bytes=VMEM_LIMIT)


def _ffn_dev(jb):
    return jb // 2 + (N_DEV // 2) * (jb % 2)


def _gelu(x):
    c = math.sqrt(2.0 / math.pi)
    t = jnp.tanh(c * (x + 0.044715 * (x * x * x)))
    return 0.5 * x * (1.0 + t)


def _gelu_grad(x):
    c = math.sqrt(2.0 / math.pi)
    x2 = x * x
    t = jnp.tanh(c * (x + 0.044715 * (x2 * x)))
    return 0.5 * (1.0 + t) + 0.5 * x * (1.0 - t * t) * (c * (1.0 + 3.0 * 0.044715 * x2))


def _sigmoid(x):
    return 1.0 / (1.0 + jnp.exp(-x))


def _log1p(x):
    u = 1.0 + x
    d = u - 1.0
    return jnp.where(d == 0.0, x, jnp.log(u) * (x / jnp.where(d == 0.0, 1.0, d)))


def _softplus(x):
    return jnp.maximum(x, 0.0) + _log1p(jnp.exp(-jnp.abs(x)))


def _expm1(x):
    return jnp.tanh(0.5 * x) * (jnp.exp(x) + 1.0)


def _dot(a, b):
    return jnp.dot(a.astype(BF16), b.astype(BF16), preferred_element_type=F32)


def _dot_nt(a, b):
    return lax.dot_general(a.astype(BF16), b.astype(BF16), (((1,), (1,)), ((), ())),
                           preferred_element_type=F32)


def _dot_tn(a, b):
    return lax.dot_general(a.astype(BF16), b.astype(BF16), (((0,), (0,)), ((), ())),
                           preferred_element_type=F32)


def _rows(shape):
    return lax.broadcasted_iota(jnp.int32, shape, 0)


def _shift_down_prev(x, s, prev8):
    if s == 0:
        return x
    t, l = x.shape
    r = pltpu.roll(x, s, axis=0)
    pr = pltpu.roll(prev8, s, axis=0)
    pad = jnp.concatenate([pr, jnp.zeros((t - 8, l), x.dtype)], axis=0)
    return jnp.where(_rows(x.shape) < s, pad, r)


def _shift_up_next(x, s, next8):
    if s == 0:
        return x
    t, l = x.shape
    r = pltpu.roll(x, t - s, axis=0)
    nx = pltpu.roll(next8, 8 - s, axis=0)
    pad = jnp.concatenate([jnp.zeros((t - 8, l), x.dtype), nx], axis=0)
    return jnp.where(_rows(x.shape) >= t - s, pad, r)


SUB = 8


def _tile_shift(x, s, fill, reverse):
    t = x.shape[0]
    pos = _rows(x.shape) & (SUB - 1)
    if reverse:
        return jnp.where(pos < SUB - s, pltpu.roll(x, t - s, axis=0), fill)
    return jnp.where(pos >= s, pltpu.roll(x, s, axis=0), fill)


def _scan_chunk(a, x, carry, reverse=False):
    s = 1
    while s < SUB:
        x = x + a * _tile_shift(x, s, 0.0, reverse)
        a = a * _tile_shift(a, s, 1.0, reverse)
        s *= 2
    nv = x.shape[0] // SUB
    out = [None] * nv
    for v in (reversed(range(nv)) if reverse else range(nv)):
        rows = slice(v * SUB, (v + 1) * SUB)
        out[v] = x[rows, :] + a[rows, :] * carry
        carry = out[v][0:1, :] if reverse else out[v][SUB - 1:SUB, :]
    return jnp.concatenate(out, axis=0)


def _cmul(ar, ai, br, bi):
    return ar * br - ai * bi, ar * bi + ai * br


def _cscan_consts(lr, li, reverse):
    pows = [(lr, li)]
    for _ in range(2):
        pows.append(_cmul(*pows[-1], *pows[-1]))
    rows = [(lr, li)]
    for _ in range(SUB - 1):
        rows.append(_cmul(*rows[-1], lr, li))
    if reverse:
        rows = rows[::-1]
    return pows, (jnp.concatenate([r for r, _ in rows], axis=0), jnp.concatenate([i for _, i in rows], axis=0))


def _cscan_chunk(xr, xi, consts, carry, reverse=False):
    pows, (p8r, p8i) = consts
    s = 1
    for pr, pi in pows:
        sr = _tile_shift(xr, s, 0.0, reverse)
        si = _tile_shift(xi, s, 0.0, reverse)
        xr, xi = xr + pr * sr - pi * si, xi + pr * si + pi * sr
        s *= 2
    nv = xr.shape[0] // SUB
    out_r, out_i = [None] * nv, [None] * nv
    cr, ci = carry
    for v in (reversed(range(nv)) if reverse else range(nv)):
        rows = slice(v * SUB, (v + 1) * SUB)
        out_r[v] = xr[rows, :] + p8r * cr - p8i * ci
        out_i[v] = xi[rows, :] + p8r * ci + p8i * cr
        edge = slice(0, 1) if reverse else slice(SUB - 1, SUB)
        cr, ci = out_r[v][edge, :], out_i[v][edge, :]
    return jnp.concatenate(out_r, axis=0), jnp.concatenate(out_i, axis=0)


def _dep_args(dep):
    return ([], []) if dep is None else ([pl.BlockSpec(memory_space=pl.ANY)], [dep])


def _mm_nn(a, b, tm, tn, name, out_dtype=F32, dep=None):
    m, k = a.shape
    n = b.shape[1]

    def body(a_ref, b_ref, *rest):
        o_ref = rest[-1]
        o_ref[...] = _dot(a_ref[...], b_ref[...]).astype(out_dtype)

    dep_specs, dep_ops = _dep_args(dep)
    return pl.pallas_call(
        body, out_shape=jax.ShapeDtypeStruct((m, n), out_dtype), grid=(n // tn, m // tm),
        in_specs=[pl.BlockSpec((tm, k), lambda j, i: (i, 0)),
                  pl.BlockSpec((k, tn), lambda j, i: (0, j))] + dep_specs,
        out_specs=pl.BlockSpec((tm, tn), lambda j, i: (i, j)), name=name,
        compiler_params=_cparams(("parallel", "parallel")))(a, b, *dep_ops)


def _mm_nt(a, w, tm, tn, name, add=None, add_scale=1.0, dep=None):
    m, k = a.shape
    n = w.shape[0]

    def body(a_ref, w_ref, *rest):
        o_ref = rest[-1]
        if add is None:
            o_ref[...] = _dot_nt(a_ref[...], w_ref[...])
        else:
            o_ref[...] = _dot_nt(a_ref[...], w_ref[...]) + add_scale * rest[0][...]

    in_specs = [pl.BlockSpec((tm, k), lambda j, i: (i, 0)), pl.BlockSpec((tn, k), lambda j, i: (j, 0))]
    args = [a, w]
    if add is not None:
        in_specs.append(pl.BlockSpec((tm, tn), lambda j, i: (i, j)))
        args.append(add)
    dep_specs, dep_ops = _dep_args(dep)
    return pl.pallas_call(
        body, out_shape=jax.ShapeDtypeStruct((m, n), F32), grid=(n // tn, m // tm),
        in_specs=in_specs + dep_specs, out_specs=pl.BlockSpec((tm, tn), lambda j, i: (i, j)), name=name,
        compiler_params=_cparams(("parallel", "parallel")))(*args, *dep_ops)


def _mm_dw(at, b, tm, tn, ts, name):
    m, s = at.shape
    n = b.shape[1]

    def body(a_ref, b_ref, o_ref):
        @pl.when(pl.program_id(2) == 0)
        def _():
            o_ref[...] = jnp.zeros_like(o_ref)
        o_ref[...] += _dot(a_ref[...], b_ref[...])

    return pl.pallas_call(
        body, out_shape=jax.ShapeDtypeStruct((m, n), F32), grid=(m // tm, n // tn, s // ts),
        in_specs=[pl.BlockSpec((tm, ts), lambda i, j, k: (i, k)), pl.BlockSpec((ts, tn), lambda i, j, k: (k, j))],
        out_specs=pl.BlockSpec((tm, tn), lambda i, j, k: (i, j)), name=name,
        compiler_params=_cparams(("parallel", "parallel", "arbitrary")))(at, b)


def _transpose_bf16(x, name):
    s, d = x.shape
    tm = 512

    def body(x_ref, o_ref):
        o_ref[...] = x_ref[...].T.astype(BF16)

    return pl.pallas_call(
        body, out_shape=jax.ShapeDtypeStruct((d, s), BF16), grid=(s // tm,),
        in_specs=[pl.BlockSpec((tm, d), lambda i: (i, 0))], out_specs=pl.BlockSpec((d, tm), lambda i: (0, i)),
        name=name, compiler_params=_cparams(("parallel",)))(x)


def _mm_up_dx(dup, wg, add, add_scale, tm, name):
    s = dup.shape[0]
    d = wg.shape[1]

    def body(a_ref, w_ref, c_ref, o_ref):
        @pl.when(pl.program_id(1) == 0)
        def _():
            o_ref[...] = add_scale * c_ref[...]
        o_ref[...] += _dot_nt(a_ref[...], w_ref[...])

    return pl.pallas_call(
        body, out_shape=jax.ShapeDtypeStruct((s, d), F32), grid=(s // tm, N_DEV),
        in_specs=[pl.BlockSpec((tm, FFN_CB), lambda i, j: (i, j)),
                  pl.BlockSpec((None, d, FFN_CB), lambda i, j: (_ffn_dev(j), 0, 0)),
                  pl.BlockSpec((tm, d), lambda i, j: (i, 0))],
        out_specs=pl.BlockSpec((tm, d), lambda i, j: (i, 0)), name=name,
        compiler_params=_cparams(("parallel", "arbitrary")))(dup, wg, add)


def _mm_up_dw(ht, dup, name):
    d, s = ht.shape

    def body(a_ref, b_ref, o_ref):
        o_ref[...] = _dot(a_ref[...], b_ref[...])

    return pl.pallas_call(
        body, out_shape=jax.ShapeDtypeStruct((N_DEV, d, FFN_CB), F32), grid=(N_DEV,),
        in_specs=[pl.BlockSpec((d, s), lambda j: (0, 0)), pl.BlockSpec((s, FFN_CB), lambda j: (0, j))],
        out_specs=pl.BlockSpec((None, d, FFN_CB), lambda j: (_ffn_dev(j), 0, 0)), name=name,
        compiler_params=_cparams(("parallel",)))(ht, dup)


def _ln_fwd(a, b, g, bias, name, transposed=True):
    s, d = a.shape
    tm = 512

    def body(a_ref, b_ref, g_ref, bias_ref, r_ref, h_ref, *ht_ref):
        r = ALPHA * a_ref[...] + b_ref[...]
        mu = jnp.mean(r, axis=-1, keepdims=True)
        xc = r - mu
        var = jnp.mean(xc * xc, axis=-1, keepdims=True)
        r_ref[...] = r
        h = xc * lax.rsqrt(var + LN_EPS) * g_ref[...] + bias_ref[...]
        h_ref[...] = h
        if transposed:
            ht_ref[0][...] = h.T.astype(BF16)

    row = pl.BlockSpec((tm, d), lambda i: (i, 0))
    vec = pl.BlockSpec((1, d), lambda i: (0, 0))
    shapes = [jax.ShapeDtypeStruct((s, d), F32), jax.ShapeDtypeStruct((s, d), F32)]
    specs = [row, row]
    if transposed:
        shapes.append(jax.ShapeDtypeStruct((d, s), BF16))
        specs.append(pl.BlockSpec((d, tm), lambda i: (0, i)))
    return pl.pallas_call(
        body, out_shape=tuple(shapes), grid=(s // tm,), in_specs=[row, row, vec, vec],
        out_specs=tuple(specs), name=name, compiler_params=_cparams(("parallel",)))(a, b, g, bias)


def _ln_bwd(r, dh, g, name, dep=None):
    s, d = r.shape
    tm = 512

    def body(r_ref, dh_ref, g_ref, *rest):
        dr_ref, dg_ref, db_ref = rest[-3:]

        @pl.when(pl.program_id(0) == 0)
        def _():
            dg_ref[...] = jnp.zeros_like(dg_ref)
            db_ref[...] = jnp.zeros_like(db_ref)
        rr = r_ref[...]
        dh_ = dh_ref[...]
        mu = jnp.mean(rr, axis=-1, keepdims=True)
        xc = rr - mu
        var = jnp.mean(xc * xc, axis=-1, keepdims=True)
        rstd = lax.rsqrt(var + LN_EPS)
        xh = xc * rstd
        dxh = dh_ * g_ref[...]
        m1 = jnp.mean(dxh, axis=-1, keepdims=True)
        m2 = jnp.mean(dxh * xh, axis=-1, keepdims=True)
        dr_ref[...] = rstd * (dxh - m1 - xh * m2)
        dg_ref[...] += jnp.sum(dh_ * xh, axis=0, keepdims=True)
        db_ref[...] += jnp.sum(dh_, axis=0, keepdims=True)

    row = pl.BlockSpec((tm, d), lambda i: (i, 0))
    vec = pl.BlockSpec((1, d), lambda i: (0, 0))
    dep_specs, dep_ops = _dep_args(dep)
    return pl.pallas_call(
        body, out_shape=(jax.ShapeDtypeStruct((s, d), F32), jax.ShapeDtypeStruct((1, d), F32),
                         jax.ShapeDtypeStruct((1, d), F32)),
        grid=(s // tm,), in_specs=[row, row, vec] + dep_specs, out_specs=(row, vec, vec), name=name,
        compiler_params=_cparams(("arbitrary",)))(r, dh, g, *dep_ops)


def _loss_head(y, target):
    s, d = y.shape
    tm = 512

    def body(y_ref, t_ref, dy_ref, l_ref):
        @pl.when(pl.program_id(0) == 0)
        def _():
            l_ref[...] = jnp.zeros_like(l_ref)
        e = y_ref[...] - t_ref[...]
        dy_ref[...] = e * (1.0 / d)
        part = 0.5 * jnp.sum(jnp.mean(e * e, axis=-1, keepdims=True), axis=0, keepdims=True)
        l_ref[...] += jnp.broadcast_to(part, l_ref.shape)

    row = pl.BlockSpec((tm, d), lambda i: (i, 0))
    return pl.pallas_call(
        body, out_shape=(jax.ShapeDtypeStruct((s, d), F32), jax.ShapeDtypeStruct((1, LANE), F32)),
        grid=(s // tm,), in_specs=[row, row], out_specs=(row, pl.BlockSpec((1, LANE), lambda i: (0, 0))),
        name="loss_head", compiler_params=_cparams(("arbitrary",)))(y, target)


def _rope_tables(s):
    half = HEAD // 2
    pos = jnp.arange(s, dtype=F32)
    inv = ROPE_THETA ** (-jnp.arange(half, dtype=F32) * 2.0 / HEAD)
    ang = pos[:, None] * inv[None, :]
    cos, sin = jnp.cos(ang), jnp.sin(ang)
    cos = jnp.concatenate([cos, cos, cos, cos], axis=1)
    sin = jnp.concatenate([-sin, sin, -sin, sin], axis=1)
    return cos, sin


def _rotate(x, cos, sin):
    lane = lax.broadcasted_iota(jnp.int32, x.shape, 1)
    partner = jnp.where((lane % HEAD) < HEAD // 2, pltpu.roll(x, LANE - HEAD // 2, axis=1),
                        pltpu.roll(x, HEAD // 2, axis=1))
    return x * cos + partner * sin


def _class_rows(c, d, tm):
    return pl.ds(c, tm // d, stride=d) if d > 1 else pl.ds(0, tm)


def _dilated_spec(tm, d, w):
    return pl.BlockSpec((tm // d, d * w), lambda i: (i, 0))


def _token_scratch(tm, w):
    return pltpu.VMEM((w // LANE, tm, LANE), F32)


def _to_tokens(src_ref, dst3, d, tm):
    nj = dst3.shape[0]
    for cls in range(d):
        for j in range(nj):
            col = (cls * nj + j) * LANE
            dst3.at[j][_class_rows(cls, d, tm), :] = src_ref[:, col:col + LANE]


def _to_dilated(src3, dst_ref, d, tm):
    nj = src3.shape[0]
    for cls in range(d):
        for j in range(nj):
            col = (cls * nj + j) * LANE
            dst_ref[:, col:col + LANE] = src3.at[j][_class_rows(cls, d, tm), :].astype(dst_ref.dtype)


def _token_value(src3):
    return jnp.concatenate([src3[j] for j in range(src3.shape[0])], axis=1)


def _rope_fwd(proj, cos, sin, name):
    s = proj.shape[0]
    tm = 512
    w = 3 * ATTN_W
    nj = w // LANE

    def body(*refs):
        p_refs, (c_ref, s_ref), o_refs, rot = refs[:nj], refs[nj:nj + 2], refs[nj + 2:nj + 5], refs[nj + 5]
        c, sn = c_ref[...], s_ref[...]
        for j in range(nj):
            x = p_refs[j][...]
            rot[j] = _rotate(x, c, sn) if j < 2 * ATTN_W // LANE else x
        for d, o_ref in zip(DILATIONS, o_refs):
            _to_dilated(rot, o_ref, d, tm)

    tab = pl.BlockSpec((tm, LANE), lambda i: (i, 0))
    cols = [pl.BlockSpec((tm, LANE), functools.partial(lambda i, j: (i, j), j=j)) for j in range(nj)]
    return pl.pallas_call(
        body, out_shape=tuple(jax.ShapeDtypeStruct((s // d, d * w), BF16) for d in DILATIONS),
        grid=(s // tm,), in_specs=cols + [tab, tab],
        out_specs=tuple(_dilated_spec(tm, d, w) for d in DILATIONS),
        scratch_shapes=[_token_scratch(tm, w)], name=name,
        compiler_params=_cparams(("parallel",)))(*[proj] * nj, cos, sin)


def _dproj_assemble(dqkv_list, dxr, dgate, du, cos, sin, name):
    s = dxr.shape[0]
    tm = 512
    nq = 3 * ATTN_W // LANE

    def body(*refs):
        br = refs[:9]
        dxr_ref, dg_ref, du_ref, c_ref, s_ref, o_ref = refs[9:15]
        tok = refs[15:]
        c, sn = c_ref[...], -s_ref[...]
        for part in range(3):
            for b, d in enumerate(DILATIONS[1:], start=1):
                _to_tokens(br[3 * b + part], tok[2 * part + b - 1], d, tm)
        for j in range(nq):
            part, jj = divmod(j, ATTN_W // LANE)
            x = br[part][:, jj * LANE:(jj + 1) * LANE] + tok[2 * part][jj] + tok[2 * part + 1][jj]
            if part < 2:
                x = _rotate(x, c, sn)
            o_ref[:, j * LANE:(j + 1) * LANE] = x.astype(BF16)
        o_ref[:, 3 * ATTN_W:3 * ATTN_W + LRU_W] = dxr_ref[...].astype(BF16)
        o_ref[:, 3 * ATTN_W + LRU_W:3 * ATTN_W + 2 * LRU_W] = dg_ref[...].astype(BF16)
        o_ref[:, 3 * ATTN_W + 2 * LRU_W:] = du_ref[...].astype(BF16)

    a_spec = pl.BlockSpec((tm, ATTN_W), lambda i: (i, 0))
    tab = pl.BlockSpec((tm, LANE), lambda i: (i, 0))
    ordered = [dqkv_list[b][p] for b in range(3) for p in range(3)]
    d_specs = [_dilated_spec(tm, d, ATTN_W) for d in DILATIONS for _ in range(3)]
    return pl.pallas_call(
        body, out_shape=jax.ShapeDtypeStruct((s, D_IN), BF16), grid=(s // tm,),
        in_specs=d_specs + [a_spec, a_spec, pl.BlockSpec((tm, S5_W), lambda i: (i, 0)), tab, tab],
        out_specs=pl.BlockSpec((tm, D_IN), lambda i: (i, 0)),
        scratch_shapes=[_token_scratch(tm, ATTN_W)] * 6, name=name,
        compiler_params=_cparams(("parallel",)))(*ordered, dxr, dgate, du, cos, sin)


def _attn_tiles(s, d):
    m = s // d
    tq = min(m, ATTN_TILE)
    return m, tq, tq // ATTN_BLK


def _band_mask(qb):
    qi = lax.broadcasted_iota(jnp.int32, (ATTN_BLK, 2 * ATTN_BLK), 0)
    ki = lax.broadcasted_iota(jnp.int32, (ATTN_BLK, 2 * ATTN_BLK), 1)
    dist = qi + ATTN_BLK - ki
    return (dist >= 0) & (dist <= ATTN_BLK) & ((ki >= ATTN_BLK) | (qb > 0))


def _head_cols(h):
    return (slice(h * HEAD, (h + 1) * HEAD), slice(ATTN_W + h * HEAD, ATTN_W + (h + 1) * HEAD),
            slice(2 * ATTN_W + h * HEAD, 2 * ATTN_W + (h + 1) * HEAD))


def _attn_fwd(qv, d, name):
    m = qv.shape[0]
    w3 = 3 * ATTN_W
    _, tq, n = _attn_tiles(m * d, d)
    scale = HEAD ** -0.5

    def body(x_ref, p_ref, o_ref, l_ref):
        b = pl.program_id(1)

        def block(i, first):
            r0 = 0 if first else pl.multiple_of(i * ATTN_BLK, ATTN_BLK)
            rows = pl.ds(r0, ATTN_BLK)
            valid = _band_mask(b * n + i)
            if not first:
                krows = pl.ds(pl.multiple_of(i * ATTN_BLK - ATTN_BLK, ATTN_BLK), 2 * ATTN_BLK)
            for h in range(ATTN_W // HEAD):
                qs, ks, vs = _head_cols(h)
                q = x_ref[rows, qs]
                if first:
                    k = jnp.concatenate([p_ref[:, ks], x_ref[0:ATTN_BLK, ks]], axis=0)
                    v = jnp.concatenate([p_ref[:, vs], x_ref[0:ATTN_BLK, vs]], axis=0)
                else:
                    k = x_ref[krows, ks]
                    v = x_ref[krows, vs]
                sc = jnp.where(valid, _dot_nt(q, k) * scale, -1e30)
                mx = jnp.max(sc, axis=-1, keepdims=True)
                p = jnp.exp(sc - mx)
                l = jnp.sum(p, axis=-1, keepdims=True)
                o_ref[rows, qs] = _dot(p, v) / l
                l_ref[rows, qs] = jnp.broadcast_to(mx + jnp.log(l), (ATTN_BLK, HEAD))

        block(0, True)
        if n > 1:
            def loop(i, carry):
                block(i, False)
                return carry
            lax.fori_loop(1, n, loop, 0)

    shp = jax.ShapeDtypeStruct((m, d * ATTN_W), F32)
    ospec = pl.BlockSpec((tq, ATTN_W), lambda c, b: (b, c))
    out, lse = pl.pallas_call(
        body, out_shape=(shp, shp), grid=(d, m // tq),
        in_specs=[pl.BlockSpec((tq, w3), lambda c, b: (b, c)),
                  pl.BlockSpec((ATTN_BLK, w3), lambda c, b: (jnp.maximum(b * n - 1, 0), c))],
        out_specs=(ospec, ospec), name=name,
        compiler_params=_cparams(("parallel", "parallel")))(qv, qv)
    return out, lse


def _attn_bwd(qv, ov, dov, lv, d, name, dep=None):
    m = qv.shape[0]
    w3 = 3 * ATTN_W
    _, tq, n = _attn_tiles(m * d, d)
    nb = m // ATTN_BLK
    scale = HEAD ** -0.5

    def body(x_ref, p_ref, nx_ref, o_ref, do_ref, l_ref, on_ref, don_ref, ln_ref, *rest):
        dq_ref, dk_ref, dv_ref = rest[-3:]
        b = pl.program_id(1)
        dk_ref[...] = jnp.zeros_like(dk_ref)
        dv_ref[...] = jnp.zeros_like(dv_ref)

        def grads(q, k, v, o, do, lse, valid):
            sc = jnp.where(valid, _dot_nt(q, k) * scale, -1e30)
            p = jnp.exp(sc - lse)
            delta = jnp.sum(do * o, axis=-1, keepdims=True)
            return p, p * (_dot_nt(do, v) - delta) * scale

        def block(i, first):
            r0 = 0 if first else pl.multiple_of(i * ATTN_BLK, ATTN_BLK)
            rows = pl.ds(r0, ATTN_BLK)
            valid = _band_mask(b * n + i)
            if not first:
                krows = pl.ds(pl.multiple_of(i * ATTN_BLK - ATTN_BLK, ATTN_BLK), 2 * ATTN_BLK)
            for h in range(ATTN_W // HEAD):
                qs, ks, vs = _head_cols(h)
                q = x_ref[rows, qs]
                do = do_ref[rows, qs]
                if first:
                    k = jnp.concatenate([p_ref[:, ks], x_ref[0:ATTN_BLK, ks]], axis=0)
                    v = jnp.concatenate([p_ref[:, vs], x_ref[0:ATTN_BLK, vs]], axis=0)
                else:
                    k = x_ref[krows, ks]
                    v = x_ref[krows, vs]
                p, ds = grads(q, k, v, o_ref[rows, qs], do, l_ref[rows, qs][:, 0:1], valid)
                dq_ref[rows, qs] = _dot(ds, k)
                if first:
                    dk_ref[0:ATTN_BLK, qs] += _dot_tn(ds[:, ATTN_BLK:], q)
                    dv_ref[0:ATTN_BLK, qs] += _dot_tn(p[:, ATTN_BLK:], do)
                else:
                    dk_ref[krows, qs] += _dot_tn(ds, q)
                    dv_ref[krows, qs] += _dot_tn(p, do)

        block(0, True)
        if n > 1:
            def loop(i, carry):
                block(i, False)
                return carry
            lax.fori_loop(1, n, loop, 0)

        last = slice((n - 1) * ATTN_BLK, n * ATTN_BLK)
        qi = lax.broadcasted_iota(jnp.int32, (ATTN_BLK, ATTN_BLK), 0)
        ki = lax.broadcasted_iota(jnp.int32, (ATTN_BLK, ATTN_BLK), 1)
        valid_next = (qi <= ki) & ((b + 1) * n < nb)
        for h in range(ATTN_W // HEAD):
            qs, ks, vs = _head_cols(h)
            q = nx_ref[:, qs]
            do = don_ref[:, qs]
            p, ds = grads(q, x_ref[last, ks], x_ref[last, vs], on_ref[:, qs], do, ln_ref[:, qs][:, 0:1],
                          valid_next)
            dk_ref[last, qs] += _dot_tn(ds, q)
            dv_ref[last, qs] += _dot_tn(p, do)

    nxt = lambda b: jnp.minimum((b + 1) * n, nb - 1)
    xs = pl.BlockSpec((tq, w3), lambda c, b: (b, c))
    xp = pl.BlockSpec((ATTN_BLK, w3), lambda c, b: (jnp.maximum(b * n - 1, 0), c))
    xn = pl.BlockSpec((ATTN_BLK, w3), lambda c, b: (nxt(b), c))
    a = pl.BlockSpec((tq, ATTN_W), lambda c, b: (b, c))
    an = pl.BlockSpec((ATTN_BLK, ATTN_W), lambda c, b: (nxt(b), c))
    shp = jax.ShapeDtypeStruct((m, d * ATTN_W), F32)
    dep_specs, dep_ops = _dep_args(dep)
    return pl.pallas_call(
        body, out_shape=(shp, shp, shp), grid=(d, m // tq),
        in_specs=[xs, xp, xn, a, a, a, an, an, an] + dep_specs, out_specs=(a, a, a), name=name,
        compiler_params=_cparams(("parallel", "parallel")))(qv, qv, qv, ov, dov, lv, ov, dov, lv, *dep_ops)


def _rms(x, g):
    ms = jnp.mean(x * x, axis=-1, keepdims=True)
    return x * lax.rsqrt(ms + RMS_EPS) * g


def _rms_bwd(x, g, dy):
    ms = jnp.mean(x * x, axis=-1, keepdims=True)
    r = lax.rsqrt(ms + RMS_EPS)
    dyg = dy * g
    dx = r * dyg - x * (r * r * r) * jnp.mean(x * dyg, axis=-1, keepdims=True)
    return dx, dy * x * r


def _mix_fwd(outs, lses, lru, s5, g, name):
    s = lru.shape[0]
    tm = 256

    def body(o1, o2, o3, l1, l2, l3, lru_ref, s5_ref, g_ref, mixed_ref, mixed_t_ref, ov1, ov2, ov3,
             lv1, lv2, lv3, so2, so3, sl2, sl3):
        for d, src, dst in ((DILATIONS[1], o2, so2), (DILATIONS[2], o3, so3),
                            (DILATIONS[1], l2, sl2), (DILATIONS[2], l3, sl3)):
            _to_tokens(src, dst, d, tm)
        a1, a2, a3 = l1[...], _token_value(sl2), _token_value(sl3)
        mx = jnp.maximum(jnp.maximum(a1, a2), a3)
        e1, e2, e3 = jnp.exp(a1 - mx), jnp.exp(a2 - mx), jnp.exp(a3 - mx)
        den = e1 + e2 + e3
        o = (e1 * o1[...] + e2 * _token_value(so2) + e3 * _token_value(so3)) / den
        lse = mx + jnp.log(den)
        ov1[...] = o
        lv1[...] = lse
        for j in range(ATTN_W // LANE):
            so2[j] = o[:, j * LANE:(j + 1) * LANE]
            sl2[j] = lse[:, j * LANE:(j + 1) * LANE]
        for d, o_dst, l_dst in ((DILATIONS[1], ov2, lv2), (DILATIONS[2], ov3, lv3)):
            _to_dilated(so2, o_dst, d, tm)
            _to_dilated(sl2, l_dst, d, tm)
        gg = g_ref[...]
        mixed = jnp.concatenate([_rms(o, gg[:, :ATTN_W]),
                                 _rms(lru_ref[...], gg[:, ATTN_W:ATTN_W + LRU_W]),
                                 _rms(s5_ref[...], gg[:, ATTN_W + LRU_W:])], axis=1)
        mixed_ref[...] = mixed.astype(BF16)
        mixed_t_ref[...] = mixed.T.astype(BF16)

    a = pl.BlockSpec((tm, ATTN_W), lambda i: (i, 0))
    s5s = pl.BlockSpec((tm, S5_W), lambda i: (i, 0))
    full = pl.BlockSpec((tm, D_MODEL), lambda i: (i, 0))
    vec = pl.BlockSpec((1, D_MODEL), lambda i: (0, 0))
    dil = [_dilated_spec(tm, d, ATTN_W) for d in DILATIONS]
    dshape = [jax.ShapeDtypeStruct((s // d, d * ATTN_W), F32) for d in DILATIONS]
    res = pl.pallas_call(
        body, out_shape=(jax.ShapeDtypeStruct((s, D_MODEL), BF16), jax.ShapeDtypeStruct((D_MODEL, s), BF16),
                         *dshape, *dshape),
        grid=(s // tm,), in_specs=dil + dil + [a, s5s, vec],
        out_specs=(full, pl.BlockSpec((D_MODEL, tm), lambda i: (0, i)), *dil, *dil),
        scratch_shapes=[_token_scratch(tm, ATTN_W)] * 4, name=name,
        compiler_params=_cparams(("parallel",)))(*outs, *lses, lru, s5, g)
    return res[0], res[1], res[2:5], res[5:8]


def _mix_bwd(dmixed, o, lru, s5, g, name):
    s = lru.shape[0]
    tm = 256

    def body(dm_ref, o_ref, lru_ref, s5_ref, g_ref, do_ref, do2_ref, do3_ref, dlru_ref, ds5_ref, dg_ref, stage):
        @pl.when(pl.program_id(0) == 0)
        def _():
            dg_ref[...] = jnp.zeros_like(dg_ref)
        gg = g_ref[...]
        dm = dm_ref[...]
        dx, dgr = _rms_bwd(o_ref[...], gg[:, :ATTN_W], dm[:, :ATTN_W])
        do_ref[...] = dx
        for j in range(ATTN_W // LANE):
            stage[j] = dx[:, j * LANE:(j + 1) * LANE]
        _to_dilated(stage, do2_ref, DILATIONS[1], tm)
        _to_dilated(stage, do3_ref, DILATIONS[2], tm)
        dg_ref[:, :ATTN_W] += jnp.sum(dgr, axis=0, keepdims=True)
        dx, dgr = _rms_bwd(lru_ref[...], gg[:, ATTN_W:ATTN_W + LRU_W], dm[:, ATTN_W:ATTN_W + LRU_W])
        dlru_ref[...] = dx
        dg_ref[:, ATTN_W:ATTN_W + LRU_W] += jnp.sum(dgr, axis=0, keepdims=True)
        dx, dgr = _rms_bwd(s5_ref[...], gg[:, ATTN_W + LRU_W:], dm[:, ATTN_W + LRU_W:])
        ds5_ref[...] = dx
        dg_ref[:, ATTN_W + LRU_W:] += jnp.sum(dgr, axis=0, keepdims=True)

    a = pl.BlockSpec((tm, ATTN_W), lambda i: (i, 0))
    s5s = pl.BlockSpec((tm, S5_W), lambda i: (i, 0))
    full = pl.BlockSpec((tm, D_MODEL), lambda i: (i, 0))
    vec = pl.BlockSpec((1, D_MODEL), lambda i: (0, 0))
    dil = [_dilated_spec(tm, d, ATTN_W) for d in DILATIONS]
    dshape = [jax.ShapeDtypeStruct((s // d, d * ATTN_W), F32) for d in DILATIONS]
    res = pl.pallas_call(
        body, out_shape=(*dshape, jax.ShapeDtypeStruct((s, LRU_W), F32),
                         jax.ShapeDtypeStruct((s, S5_W), F32), jax.ShapeDtypeStruct((1, D_MODEL), F32)),
        grid=(s // tm,), in_specs=[full, a, a, s5s, vec], out_specs=(*dil, a, s5s, vec),
        scratch_shapes=[_token_scratch(tm, ATTN_W)], name=name,
        compiler_params=_cparams(("arbitrary",)))(dmixed, o, lru, s5, g)
    return res[0:3], res[3], res[4], res[5]


def _lru_gate_math(xc, pre_r, pre_i, lam):
    r = _sigmoid(pre_r)
    i = _sigmoid(pre_i)
    log_a = -LRU_C * r * _softplus(-lam)
    a = jnp.exp(log_a)
    u = jnp.sqrt(-_expm1(2.0 * log_a)) * (i * xc)
    return a, u


def _lru_conv(x, prev8, cw, cb):
    y = cb + cw[LRU_CONV - 1:LRU_CONV, :] * x
    for k in range(LRU_CONV - 1):
        y = y + cw[k:k + 1, :] * _shift_down_prev(x, LRU_CONV - 1 - k, prev8)
    return y


def _lru_specs(s):
    xo = 3 * ATTN_W // LANE
    go = xo + LRU_W // LANE
    xr = pl.BlockSpec((s, LANE), lambda j: (0, xo + j))
    gt = pl.BlockSpec((s, LANE), lambda j: (0, go + j))
    cw = pl.BlockSpec((LRU_CONV, LANE), lambda j: (0, j))
    vec = pl.BlockSpec((1, LANE), lambda j: (0, j))
    wbd = pl.BlockSpec((LANE, LANE), lambda j: (j, j))
    col = pl.BlockSpec((s, LANE), lambda j: (0, j))
    return xr, gt, cw, vec, wbd, col


def _lru_fwd(proj, cw, cb, wr, br, wi, bi, lam, name):
    s = proj.shape[0]
    t = SCAN_T

    def body(xr_ref, gt_ref, cw_ref, cb_ref, wr_ref, br_ref, wi_ref, bi_ref, lam_ref, o_ref):
        cwv, cbv, lamv = cw_ref[...], cb_ref[...], lam_ref[...]
        wrv, wiv, brv, biv = wr_ref[...], wi_ref[...], br_ref[...], bi_ref[...]

        def chunk(c, carry):
            h_c, prev8 = carry
            rows = pl.ds(pl.multiple_of(c * t, t), t)
            x = xr_ref[rows, :]
            xc = _lru_conv(x, prev8, cwv, cbv)
            a, u = _lru_gate_math(xc, _dot(xc, wrv) + brv, _dot(xc, wiv) + biv, lamv)
            h = _scan_chunk(a, u, h_c)
            o_ref[rows, :] = h * _gelu(gt_ref[rows, :])
            return h[t - 1:t, :], x[t - 8:t, :]

        lax.fori_loop(0, s // t, chunk, (jnp.zeros((1, LANE), F32), jnp.zeros((8, LANE), F32)))

    xr, gt, cws, vec, wbd, col = _lru_specs(s)
    return pl.pallas_call(
        body, out_shape=jax.ShapeDtypeStruct((s, LRU_W), F32), grid=(LRU_W // LANE,),
        in_specs=[xr, gt, cws, vec, wbd, vec, wbd, vec, vec], out_specs=col, name=name,
        compiler_params=_cparams(("parallel",)))(proj, proj, cw, cb, wr, br, wi, bi, lam)


def _lru_bwd(proj, dout, cw, cb, wr, br, wi, bi, lam, name):
    s = proj.shape[0]
    t = SCAN_T
    nc = s // t

    def body(xr_ref, gt_ref, do_ref, cw_ref, cb_ref, wr_ref, br_ref, wi_ref, bi_ref, lam_ref,
             dxr_ref, dgt_ref, dcw_ref, dcb_ref, dwr_ref, dbr_ref, dwi_ref, dbi_ref, dlam_ref,
             xc_s, a_s, h_s):
        cwv, cbv, lamv = cw_ref[...], cb_ref[...], lam_ref[...]
        wrv, wiv, brv, biv = wr_ref[...], wi_ref[...], br_ref[...], bi_ref[...]

        def fchunk(c, carry):
            h_c, prev8 = carry
            rows = pl.ds(pl.multiple_of(c * t, t), t)
            x = xr_ref[rows, :]
            xc = _lru_conv(x, prev8, cwv, cbv)
            a, u = _lru_gate_math(xc, _dot(xc, wrv) + brv, _dot(xc, wiv) + biv, lamv)
            h = _scan_chunk(a, u, h_c)
            xc_s[rows, :] = xc
            a_s[rows, :] = a
            h_s[rows, :] = h
            return h[t - 1:t, :], x[t - 8:t, :]

        lax.fori_loop(0, nc, fchunk, (jnp.zeros((1, LANE), F32), jnp.zeros((8, LANE), F32)))

        z1 = jnp.zeros((1, LANE), F32)
        zw = jnp.zeros((LANE, LANE), F32)

        def bchunk(ci, carry):
            g_next, a_next, dxc_next8, dcw, dcb, dwr, dbr, dwi, dbi, dlam = carry
            c = nc - 1 - ci
            t0 = pl.multiple_of(c * t, t)
            rows = pl.ds(t0, t)
            before = pl.ds(pl.multiple_of(jnp.maximum(t0 - 8, 0), 8), 8)
            has_prev = (c > 0).astype(F32)
            x, gt, do = xr_ref[rows, :], gt_ref[rows, :], do_ref[rows, :]
            xc, a, h = xc_s[rows, :], a_s[rows, :], h_s[rows, :]
            prev8_x = xr_ref[before, :] * has_prev
            prev8_h = h_s[before, :] * has_prev
            dgt_ref[rows, :] = do * h * _gelu_grad(gt)
            dh = do * _gelu(gt)
            a_plus = _shift_up_next(a, 1, jnp.broadcast_to(a_next, (8, LANE)))
            g = _scan_chunk(a_plus, dh, g_next, reverse=True)
            da = g * _shift_down_prev(h, 1, prev8_h)
            pre_r = _dot(xc, wrv) + brv
            pre_i = _dot(xc, wiv) + biv
            _, vjp = jax.vjp(_lru_gate_math, xc, pre_r, pre_i, lamv)
            dxc, dpre_r, dpre_i, dlam_c = vjp((da, g))
            dxc = dxc + _dot_nt(dpre_r, wrv) + _dot_nt(dpre_i, wiv)
            dx = cwv[LRU_CONV - 1:LRU_CONV, :] * dxc
            dcw_rows = [None] * LRU_CONV
            dcw_rows[LRU_CONV - 1] = jnp.sum(dxc * x, axis=0, keepdims=True)
            for k in range(LRU_CONV - 1):
                sh = LRU_CONV - 1 - k
                dx = dx + cwv[k:k + 1, :] * _shift_up_next(dxc, sh, dxc_next8)
                dcw_rows[k] = jnp.sum(dxc * _shift_down_prev(x, sh, prev8_x), axis=0, keepdims=True)
            dxr_ref[rows, :] = dx
            return (g[0:1, :], a[0:1, :], dxc[0:8, :],
                    dcw + jnp.concatenate(dcw_rows, axis=0),
                    dcb + jnp.sum(dxc, axis=0, keepdims=True),
                    dwr + _dot_tn(xc, dpre_r), dbr + jnp.sum(dpre_r, axis=0, keepdims=True),
                    dwi + _dot_tn(xc, dpre_i), dbi + jnp.sum(dpre_i, axis=0, keepdims=True),
                    dlam + dlam_c)

        init = (z1, z1, jnp.zeros((8, LANE), F32), jnp.zeros((LRU_CONV, LANE), F32), z1, zw, z1, zw, z1, z1)
        res = lax.fori_loop(0, nc, bchunk, init)
        dcw_ref[...] = res[3]
        dcb_ref[...] = res[4]
        dwr_ref[...] = res[5]
        dbr_ref[...] = res[6]
        dwi_ref[...] = res[7]
        dbi_ref[...] = res[8]
        dlam_ref[...] = res[9]

    xr, gt, cws, vec, wbd, col = _lru_specs(s)
    vshape = jax.ShapeDtypeStruct((1, LRU_W), F32)
    wshape = jax.ShapeDtypeStruct((LRU_W, LRU_W), F32)
    return pl.pallas_call(
        body,
        out_shape=(jax.ShapeDtypeStruct((s, LRU_W), F32), jax.ShapeDtypeStruct((s, LRU_W), F32),
                   jax.ShapeDtypeStruct((LRU_CONV, LRU_W), F32), vshape, wshape, vshape, wshape, vshape, vshape),
        grid=(LRU_W // LANE,),
        in_specs=[xr, gt, col, cws, vec, wbd, vec, wbd, vec, vec],
        out_specs=(col, col, cws, vec, wbd, vec, wbd, vec, vec),
        scratch_shapes=[pltpu.VMEM((s, LANE), F32)] * 3, name=name,
        compiler_params=_cparams(("parallel",)))(proj, proj, dout, cw, cb, wr, br, wi, bi, lam)


def _s5_disc_math(a_re, a_im, log_step, bt_re, bt_im):
    step = jnp.exp(log_step)
    dt_re, dt_im = step * a_re, step * a_im
    mag = jnp.exp(dt_re)
    ab_re, ab_im = mag * jnp.cos(dt_im), mag * jnp.sin(dt_im)
    z_re, z_im = ab_re - 1.0, ab_im
    den = a_re * a_re + a_im * a_im
    f_re = (z_re * a_re + z_im * a_im) / den
    f_im = (z_im * a_re - z_re * a_im) / den
    bb_re = f_re * bt_re - f_im * bt_im
    bb_im = f_re * bt_im + f_im * bt_re
    return ab_re, ab_im, bb_re, bb_im


def _s5_disc_fwd(a_re, a_im, log_step, bt_re, bt_im, name):
    def body(ar, ai, ls, br, bi, o1, o2, o3, o4):
        r = _s5_disc_math(ar[...], ai[...], ls[...], br[...], bi[...])
        o1[...], o2[...], o3[...], o4[...] = r

    shp = jax.ShapeDtypeStruct(a_re.shape, F32)
    return pl.pallas_call(body, out_shape=(shp,) * 4, name=name)(a_re, a_im, log_step, bt_re, bt_im)


def _s5_disc_bwd(a_re, a_im, log_step, bt_re, bt_im, cts, name):
    def body(ar, ai, ls, br, bi, c1, c2, c3, c4, o1, o2, o3, o4, o5):
        _, vjp = jax.vjp(_s5_disc_math, ar[...], ai[...], ls[...], br[...], bi[...])
        r = vjp((c1[...], c2[...], c3[...], c4[...]))
        o1[...], o2[...], o3[...], o4[...], o5[...] = r

    shp = jax.ShapeDtypeStruct(a_re.shape, F32)
    return pl.pallas_call(body, out_shape=(shp,) * 5, name=name)(a_re, a_im, log_step, bt_re, bt_im, *cts)


def _s5_u_specs(s):
    uo = (3 * ATTN_W + 2 * LRU_W) // LANE
    return (pl.BlockSpec((s, LANE), lambda j: (0, uo)), pl.BlockSpec((s, LANE), lambda j: (0, uo + 1)))


def _s5_scan_fwd(proj, b_re, b_im, lam_re, lam_im, c_re, c_im, name):
    s = proj.shape[0]
    t = SCAN_T

    def body(u0_ref, u1_ref, bre_ref, bim_ref, lre_ref, lim_ref, cre_ref, cim_ref, xre_ref, xim_ref, y_ref):
        @pl.when(pl.program_id(0) == 0)
        def _():
            y_ref[...] = jnp.zeros_like(y_ref)
        lr, li = lre_ref[...], lim_ref[...]
        consts = _cscan_consts(lr, li, False)
        bre, bim, cre, cim = bre_ref[...], bim_ref[...], cre_ref[...], cim_ref[...]

        def chunk(c, carry):
            cr, ci = carry
            rows = pl.ds(pl.multiple_of(c * t, t), t)
            u = jnp.concatenate([u0_ref[rows, :], u1_ref[rows, :]], axis=1).astype(BF16)
            xr, xi = _cscan_chunk(_dot(u, bre), _dot(u, bim), consts, (cr, ci))
            xre_ref[rows, :] = xr
            xim_ref[rows, :] = xi
            y_ref[rows, :] += _dot(xr, cre) - _dot(xi, cim)
            return xr[t - 1:t, :], xi[t - 1:t, :]

        z = jnp.zeros((1, S5_BLK), F32)
        lax.fori_loop(0, s // t, chunk, (z, z))

    u0, u1 = _s5_u_specs(s)
    bsp = pl.BlockSpec((S5_W, S5_BLK), lambda j: (0, j))
    csp = pl.BlockSpec((S5_BLK, S5_W), lambda j: (j, 0))
    vec = pl.BlockSpec((1, S5_BLK), lambda j: (0, j))
    xsp = pl.BlockSpec((s, S5_BLK), lambda j: (0, j))
    ysp = pl.BlockSpec((s, S5_W), lambda j: (0, 0))
    xshape = jax.ShapeDtypeStruct((s, S5_STATES), F32)
    return pl.pallas_call(
        body, out_shape=(xshape, xshape, jax.ShapeDtypeStruct((s, S5_W), F32)),
        grid=(S5_STATES // S5_BLK,), in_specs=[u0, u1, bsp, bsp, vec, vec, csp, csp],
        out_specs=(xsp, xsp, ysp), name=name,
        compiler_params=_cparams(("arbitrary",)))(proj, proj, b_re, b_im, lam_re, lam_im, c_re, c_im)


def _s5_scan_bwd(proj, dy, du_init, x_re, x_im, b_re, b_im, lam_re, lam_im, c_re, c_im, name):
    s = proj.shape[0]
    t = SCAN_T
    nc = s // t

    def body(u0_ref, u1_ref, dy_ref, dui_ref, xre_ref, xim_ref, bre_ref, bim_ref, lre_ref, lim_ref,
             cre_ref, cim_ref, du_ref, dlr_ref, dli_ref, dbr_ref, dbi_ref, dcr_ref, dci_ref):
        @pl.when(pl.program_id(0) == 0)
        def _():
            du_ref[...] = dui_ref[...]
        mr, mi = lre_ref[...], -lim_ref[...]
        consts = _cscan_consts(mr, mi, True)
        bre, bim, cre, cim = bre_ref[...], bim_ref[...], cre_ref[...], cim_ref[...]
        dbr_ref[...] = jnp.zeros_like(dbr_ref)
        dbi_ref[...] = jnp.zeros_like(dbi_ref)
        dcr_ref[...] = jnp.zeros_like(dcr_ref)
        dci_ref[...] = jnp.zeros_like(dci_ref)

        def chunk(ci_, carry):
            gnr, gni, dlr, dli = carry
            c = nc - 1 - ci_
            t0 = pl.multiple_of(c * t, t)
            rows = pl.ds(t0, t)
            before = pl.ds(pl.multiple_of(jnp.maximum(t0 - 8, 0), 8), 8)
            has_prev = (c > 0).astype(F32)
            dyc = dy_ref[rows, :].astype(BF16)
            u = jnp.concatenate([u0_ref[rows, :], u1_ref[rows, :]], axis=1).astype(BF16)
            gr, gi = _cscan_chunk(_dot_nt(dyc, cre), -_dot_nt(dyc, cim), consts, (gnr, gni), reverse=True)
            xr, xi = xre_ref[rows, :], xim_ref[rows, :]
            xpr = _shift_down_prev(xr, 1, xre_ref[before, :] * has_prev)
            xpi = _shift_down_prev(xi, 1, xim_ref[before, :] * has_prev)
            dlr = dlr + jnp.sum(gr * xpr + gi * xpi, axis=0, keepdims=True)
            dli = dli + jnp.sum(gi * xpr - gr * xpi, axis=0, keepdims=True)
            du_ref[rows, :] += _dot_nt(gr, bre) + _dot_nt(gi, bim)
            dbr_ref[...] += _dot_tn(u, gr)
            dbi_ref[...] += _dot_tn(u, gi)
            dcr_ref[...] += _dot_tn(xr, dyc)
            dci_ref[...] -= _dot_tn(xi, dyc)
            return gr[0:1, :], gi[0:1, :], dlr, dli

        z = jnp.zeros((1, S5_BLK), F32)
        res = lax.fori_loop(0, nc, chunk, (z, z, z, z))
        dlr_ref[...] = res[2]
        dli_ref[...] = res[3]

    u0, u1 = _s5_u_specs(s)
    bsp = pl.BlockSpec((S5_W, S5_BLK), lambda j: (0, j))
    csp = pl.BlockSpec((S5_BLK, S5_W), lambda j: (j, 0))
    vec = pl.BlockSpec((1, S5_BLK), lambda j: (0, j))
    xsp = pl.BlockSpec((s, S5_BLK), lambda j: (0, j))
    ysp = pl.BlockSpec((s, S5_W), lambda j: (0, 0))
    return pl.pallas_call(
        body,
        out_shape=(jax.ShapeDtypeStruct((s, S5_W), F32),
                   jax.ShapeDtypeStruct((1, S5_STATES), F32), jax.ShapeDtypeStruct((1, S5_STATES), F32),
                   jax.ShapeDtypeStruct((S5_W, S5_STATES), F32), jax.ShapeDtypeStruct((S5_W, S5_STATES), F32),
                   jax.ShapeDtypeStruct((S5_STATES, S5_W), F32), jax.ShapeDtypeStruct((S5_STATES, S5_W), F32)),
        grid=(S5_STATES // S5_BLK,),
        in_specs=[u0, u1, ysp, ysp, xsp, xsp, bsp, bsp, vec, vec, csp, csp],
        out_specs=(ysp, vec, vec, bsp, bsp, csp, csp), name=name,
        compiler_params=_cparams(("arbitrary",)))(
            proj, proj, dy, du_init, x_re, x_im, b_re, b_im, lam_re, lam_im, c_re, c_im)


def _s5_out_fwd(proj, y_acc, dvec, w_glu, b_glu, name):
    s = proj.shape[0]
    tm = 512
    uo = (3 * ATTN_W + 2 * LRU_W) // LANE

    def body(u0_ref, u1_ref, y_ref, d_ref, w_ref, b_ref, o_ref, yp_ref):
        u = jnp.concatenate([u0_ref[...], u1_ref[...]], axis=1)
        y = y_ref[...] + d_ref[...] * u
        yp_ref[...] = y
        yg = _gelu(y)
        o_ref[...] = yg * _sigmoid(_dot(yg, w_ref[...]) + b_ref[...])

    u0 = pl.BlockSpec((tm, LANE), lambda i: (i, uo))
    u1 = pl.BlockSpec((tm, LANE), lambda i: (i, uo + 1))
    row = pl.BlockSpec((tm, S5_W), lambda i: (i, 0))
    vec = pl.BlockSpec((1, S5_W), lambda i: (0, 0))
    wsp = pl.BlockSpec((S5_W, S5_W), lambda i: (0, 0))
    shp = jax.ShapeDtypeStruct((s, S5_W), F32)
    return pl.pallas_call(
        body, out_shape=(shp, shp), grid=(s // tm,), in_specs=[u0, u1, row, vec, wsp, vec],
        out_specs=(row, row), name=name,
        compiler_params=_cparams(("parallel",)))(proj, proj, y_acc, dvec, w_glu, b_glu)


def _s5_out_bwd(proj, y_pre, dout, dvec, w_glu, b_glu, name):
    s = proj.shape[0]
    tm = 512
    uo = (3 * ATTN_W + 2 * LRU_W) // LANE

    def body(u0_ref, u1_ref, y_ref, do_ref, d_ref, w_ref, b_ref, dy_ref, dud_ref, dd_ref, dw_ref, db_ref):
        @pl.when(pl.program_id(0) == 0)
        def _():
            dd_ref[...] = jnp.zeros_like(dd_ref)
            dw_ref[...] = jnp.zeros_like(dw_ref)
            db_ref[...] = jnp.zeros_like(db_ref)
        u = jnp.concatenate([u0_ref[...], u1_ref[...]], axis=1)
        y = y_ref[...]
        do = do_ref[...]
        yg = _gelu(y)
        sg = _sigmoid(_dot(yg, w_ref[...]) + b_ref[...])
        dz = do * yg * sg * (1.0 - sg)
        dyg = do * sg + _dot_nt(dz, w_ref[...])
        dy = dyg * _gelu_grad(y)
        dy_ref[...] = dy
        dud_ref[...] = d_ref[...] * dy
        dd_ref[...] += jnp.sum(dy * u, axis=0, keepdims=True)
        dw_ref[...] += _dot_tn(yg, dz)
        db_ref[...] += jnp.sum(dz, axis=0, keepdims=True)

    u0 = pl.BlockSpec((tm, LANE), lambda i: (i, uo))
    u1 = pl.BlockSpec((tm, LANE), lambda i: (i, uo + 1))
    row = pl.BlockSpec((tm, S5_W), lambda i: (i, 0))
    vec = pl.BlockSpec((1, S5_W), lambda i: (0, 0))
    wsp = pl.BlockSpec((S5_W, S5_W), lambda i: (0, 0))
    shp = jax.ShapeDtypeStruct((s, S5_W), F32)
    vshape = jax.ShapeDtypeStruct((1, S5_W), F32)
    return pl.pallas_call(
        body, out_shape=(shp, shp, vshape, jax.ShapeDtypeStruct((S5_W, S5_W), F32), vshape),
        grid=(s // tm,), in_specs=[u0, u1, row, row, vec, wsp, vec],
        out_specs=(row, row, vec, wsp, vec), name=name,
        compiler_params=_cparams(("arbitrary",)))(proj, proj, y_pre, dout, dvec, w_glu, b_glu)


def _ffn_conv(x, prev8, cw, cb):
    y = cb + cw[FFN_CONV - 1:FFN_CONV, :] * x
    for k in range(FFN_CONV - 1):
        y = y + cw[k:k + 1, :] * _shift_down_prev(x, FFN_CONV - 1 - k, prev8)
    return y


def _ffn_up_act(h, wg, cw, cb, name, dep=None):
    s, d = h.shape
    tm = 512
    tb = 2 * FFN_CB
    nt = D_FF // FFN_CB

    def body(h_ref, wgate_ref, wval_ref, cw_ref, cb_ref, *rest):
        up_ref, o_ref, ot_ref, carry = rest[-4:]
        t = pl.program_id(1)

        @pl.when(pl.program_id(0) == 0)
        def _():
            carry[t] = jnp.zeros((8, tb), F32)
        hb = h_ref[...].astype(BF16)
        x = jnp.concatenate([_dot(hb, wgate_ref[...]), _dot(hb, wval_ref[...])], axis=1)
        up_ref[...] = x
        y = _ffn_conv(x, carry[t], cw_ref[...], cb_ref[...])
        carry[t] = x[tm - 8:tm, :]
        act = _gelu(y[:, :FFN_CB]) * y[:, FFN_CB:]
        o_ref[...] = act.astype(BF16)
        ot_ref[...] = act.T.astype(BF16)

    dep_specs, dep_ops = _dep_args(dep)
    return pl.pallas_call(
        body, out_shape=(jax.ShapeDtypeStruct((s, 2 * D_FF), F32), jax.ShapeDtypeStruct((s, D_FF), BF16),
                         jax.ShapeDtypeStruct((D_FF, s), BF16)),
        grid=(s // tm, nt),
        in_specs=[pl.BlockSpec((tm, d), lambda i, t: (i, 0)),
                  pl.BlockSpec((None, d, FFN_CB), lambda i, t: (t, 0, 0)),
                  pl.BlockSpec((None, d, FFN_CB), lambda i, t: (t + nt, 0, 0)),
                  pl.BlockSpec((FFN_CONV, tb), lambda i, t: (0, t)),
                  pl.BlockSpec((1, tb), lambda i, t: (0, t))] + dep_specs,
        out_specs=(pl.BlockSpec((tm, tb), lambda i, t: (i, t)), pl.BlockSpec((tm, FFN_CB), lambda i, t: (i, t)),
                   pl.BlockSpec((FFN_CB, tm), lambda i, t: (t, i))),
        scratch_shapes=[pltpu.VMEM((nt, 8, tb), F32)], name=name,
        compiler_params=_cparams(("arbitrary", "arbitrary")))(h, wg, wg, cw, cb, *dep_ops)


def _ffn_act_bwd(up, dact, cw, cb, name):
    s = up.shape[0]
    tm = 256
    tb = 2 * FFN_CB
    nr = s // tm

    def body(x_ref, p_ref, n_ref, da_ref, dan_ref, cw_ref, cb_ref, dup_ref, dcw_ref, dcb_ref):
        i = pl.program_id(1)

        @pl.when(i == 0)
        def _():
            dcw_ref[...] = jnp.zeros_like(dcw_ref)
            dcb_ref[...] = jnp.zeros_like(dcb_ref)
        has_next = (i < nr - 1).astype(F32)
        prev8 = p_ref[...] * (i > 0).astype(F32)
        cwv = cw_ref[...]
        x = x_ref[...]
        xe = jnp.concatenate([x, n_ref[...]], axis=0)
        dae = jnp.concatenate([da_ref[...], dan_ref[...] * has_next], axis=0)
        shifted = [_shift_down_prev(xe, FFN_CONV - 1 - k, prev8) for k in range(FFN_CONV - 1)]
        y = cb_ref[...] + cwv[FFN_CONV - 1:FFN_CONV, :] * xe
        for k in range(FFN_CONV - 1):
            y = y + cwv[k:k + 1, :] * shifted[k]
        gate, val = y[:, :FFN_CB], y[:, FFN_CB:]
        dy = jnp.concatenate([dae * val * _gelu_grad(gate), dae * _gelu(gate)], axis=1)
        dym = dy[:tm, :]
        dx = cwv[FFN_CONV - 1:FFN_CONV, :] * dym
        dcw_rows = [None] * FFN_CONV
        dcw_rows[FFN_CONV - 1] = jnp.sum(dym * x, axis=0, keepdims=True)
        for k in range(FFN_CONV - 1):
            sh = FFN_CONV - 1 - k
            dx = dx + cwv[k:k + 1, :] * pltpu.roll(dy, tm + 8 - sh, axis=0)[:tm, :]
            dcw_rows[k] = jnp.sum(dym * shifted[k][:tm, :], axis=0, keepdims=True)
        dup_ref[...] = dx.astype(BF16)
        dcw_ref[...] += jnp.concatenate(dcw_rows, axis=0)
        dcb_ref[...] += jnp.sum(dym, axis=0, keepdims=True)

    main = pl.BlockSpec((tm, tb), lambda j, i: (i, j))
    prev = pl.BlockSpec((8, tb), lambda j, i: (jnp.maximum(i * (tm // 8) - 1, 0), j))
    nxt = pl.BlockSpec((8, tb), lambda j, i: (jnp.minimum((i + 1) * (tm // 8), s // 8 - 1), j))
    da = pl.BlockSpec((tm, FFN_CB), lambda j, i: (i, j))
    dan = pl.BlockSpec((8, FFN_CB), lambda j, i: (jnp.minimum((i + 1) * (tm // 8), s // 8 - 1), j))
    cws = pl.BlockSpec((FFN_CONV, tb), lambda j, i: (0, j))
    cbs = pl.BlockSpec((1, tb), lambda j, i: (0, j))
    return pl.pallas_call(
        body, out_shape=(jax.ShapeDtypeStruct((s, 2 * D_FF), BF16),
                         jax.ShapeDtypeStruct((FFN_CONV, 2 * D_FF), F32),
                         jax.ShapeDtypeStruct((1, 2 * D_FF), F32)),
        grid=(D_FF // FFN_CB, nr), in_specs=[main, prev, nxt, da, dan, cws, cbs],
        out_specs=(main, cws, cbs), name=name,
        compiler_params=_cparams(("parallel", "arbitrary")))(up, up, up, dact, dact, cw, cb)


def _sum_partials(ld_ref):
    gg = ld_ref[0].astype(F32)
    for k in range(1, N_DEV):
        gg = gg + ld_ref[k].astype(F32)
    return gg


def _adam_update(w, g, m, v):
    mn = ADAM_B1 * m + (1.0 - ADAM_B1) * g
    vn = ADAM_B2 * v + (1.0 - ADAM_B2) * (g * g)
    m_hat = mn / (1.0 - ADAM_B1 ** ADAM_STEP)
    v_hat = vn / (1.0 - ADAM_B2 ** ADAM_STEP)
    return -ADAM_LR * (m_hat / (jnp.sqrt(v_hat) + ADAM_EPS) + ADAM_WD * w), mn, vn


def _adamw_many(landed, ws, ms, vs, name):
    n, nl = len(ws), len(landed)

    def body(*refs):
        ld = refs[:nl * n]
        w_refs, m_refs, v_refs = (refs[(nl + k) * n:(nl + k + 1) * n] for k in range(3))
        outs = refs[(nl + 3) * n:]
        for i in range(n):
            for l in range(nl):
                one = slice(l, l + 1)
                gg = _sum_partials(ld[l * n + i])
                outs[i][one] = gg
                outs[n + i][one], outs[2 * n + i][one], outs[3 * n + i][one] = _adam_update(
                    w_refs[i][one], gg, m_refs[i][one], v_refs[i][one])

    vm = pl.BlockSpec(memory_space=pltpu.VMEM)
    shapes = [jax.ShapeDtypeStruct(w.shape, F32) for w in ws] * 4
    res = pl.pallas_call(
        body, out_shape=tuple(shapes), in_specs=[vm] * ((nl + 3) * n), out_specs=tuple([vm] * (4 * n)),
        name=name, compiler_params=_cparams())(*[a for layer in landed for a in layer], *ws, *ms, *vs)
    return res[:n], res[n:2 * n], res[2 * n:3 * n], res[3 * n:]


def _adamw_sum(landed, w, m, v, layer, prev, name):
    _, r, c = landed.shape
    nl = w.shape[0]
    tm = 8
    for cand in (512, 256, 128, 64, 32, 16):
        if r % cand == 0 and N_DEV * cand * c * 4 <= 4 * 1024 * 1024:
            tm = cand
            break

    def body(*refs):
        ld_ref, w_ref, m_ref, v_ref = refs[:4]
        g_ref, d_ref, mo_ref, vo_ref = refs[-4:]
        gg = _sum_partials(ld_ref)
        g_ref[...] = gg
        d_ref[...], mo_ref[...], vo_ref[...] = _adam_update(w_ref[...], gg, m_ref[...], v_ref[...])

    blk = pl.BlockSpec((None, tm, c), lambda i: (layer, i, 0))
    in_specs = [pl.BlockSpec((N_DEV, tm, c), lambda i: (0, i, 0)), blk, blk, blk]
    args = [landed, w, m, v]
    aliases = {}
    if prev is not None:
        in_specs += [pl.BlockSpec(memory_space=pl.ANY)] * 4
        args += list(prev)
        aliases = {4 + k: k for k in range(4)}
    shp = jax.ShapeDtypeStruct((nl, r, c), F32)
    return pl.pallas_call(
        body, out_shape=(shp,) * 4, grid=(r // tm,), in_specs=in_specs, out_specs=(blk,) * 4,
        input_output_aliases=aliases, name=name, compiler_params=_cparams(("parallel",)))(*args)


def _all_gather(shards, name):
    na = len(shards)

    def body(*refs):
        x_refs, out_refs = refs[:na], refs[na:2 * na]
        send_sems, recv_sems, local_sems = refs[2 * na:]
        x, y, c = lax.axis_index("x"), lax.axis_index("y"), lax.axis_index("c")
        me, sibling = (x, y, c), (x, y, 1 - c)
        chips = [(1 - x, y), (x, 1 - y), (1 - x, 1 - y)]

        def copy(a, k, block, to, src=None):
            dst = out_refs[a].at[4 * block[0] + 2 * block[1] + block[2]]
            return pltpu.make_async_remote_copy(
                src_ref=dst if src is None else src, dst_ref=dst,
                send_sem=send_sems.at[7 * a + k], recv_sem=recv_sems.at[7 * a + k],
                device_id=to, device_id_type=pl.DeviceIdType.MESH)

        mine, first, passed = [], [], []
        for a in range(na):
            cp = pltpu.make_async_copy(x_refs[a], out_refs[a].at[4 * x + 2 * y + c], local_sems.at[a])
            cp.start()
            mine.append(cp)
            cps = [copy(a, 0, me, sibling, src=x_refs[a])]
            cps += [copy(a, 1 + j, me, (*chip, c), src=x_refs[a]) for j, chip in enumerate(chips)]
            for cp in cps:
                cp.start()
            first += cps
        for j, chip in enumerate(chips):
            for a in range(na):
                copy(a, 1 + j, (*chip, c), me).wait_recv()
                cp = copy(a, 4 + j, (*chip, c), sibling)
                cp.start()
                passed.append(cp)
        for a in range(na):
            copy(a, 0, sibling, me).wait_recv()
            for j, chip in enumerate(chips):
                copy(a, 4 + j, (*chip, 1 - c), me).wait_recv()
        for cp in first + passed:
            cp.wait_send()
        for cp in mine:
            cp.wait()

    anyspec = pl.BlockSpec(memory_space=pl.ANY)
    return pl.pallas_call(
        body, out_shape=tuple(jax.ShapeDtypeStruct((N_DEV,) + t.shape, t.dtype) for t in shards),
        in_specs=[anyspec] * na, out_specs=tuple([anyspec] * na),
        scratch_shapes=[pltpu.SemaphoreType.DMA((7 * na,)), pltpu.SemaphoreType.DMA((7 * na,)),
                        pltpu.SemaphoreType.DMA((na,))],
        name=name)(*shards)


_HBM = pl.BlockSpec(memory_space=pltpu.HBM)
_SEM = pl.BlockSpec(memory_space=pltpu.SEMAPHORE)
_EFFECT = pltpu.SideEffectType.DATAFLOW_SIDE_EFFECTING


def _exchange_copies(src_refs, land_refs, send_sems, recv_sems, local_sems, gather):
    x, y, c = lax.axis_index("x"), lax.axis_index("y"), lax.axis_index("c")
    me = 4 * x + 2 * y + c
    per_array = send_sems.shape[0] > N_DEV - 1
    local, remote = [], []
    for a, (src, land) in enumerate(zip(src_refs, land_refs)):
        local.append(pltpu.make_async_copy(src if gather else src.at[me], land.at[me],
                                           local_sems.at[a if per_array else 0]))
    for k in range(1, N_DEV):
        px = x ^ ((k >> 2) & 1)
        py = y ^ ((k >> 1) & 1)
        pc = c ^ (k & 1)
        for a, (src, land) in enumerate(zip(src_refs, land_refs)):
            remote.append(pltpu.make_async_remote_copy(
                src_ref=src if gather else src.at[4 * px + 2 * py + pc], dst_ref=land.at[me],
                send_sem=send_sems.at[(7 * a if per_array else 0) + k - 1],
                recv_sem=recv_sems.at[(7 * a if per_array else 0) + k - 1],
                device_id=(px, py, pc), device_id_type=pl.DeviceIdType.MESH))
    return local, remote


def _exchange_start(srcs, gather, name, dep=None):
    na = len(srcs)
    ns = na if na <= 4 else 1
    lands = [lax.empty(((N_DEV,) + t.shape) if gather else t.shape, t.dtype) for t in srcs]

    def body(*refs):
        src_refs, land_refs = refs[:na], refs[na:2 * na]
        nin = 2 * na + (0 if dep is None else 1)
        send_sems, recv_sems, local_sems = refs[nin:nin + 3]
        token = refs[-1]
        local, remote = _exchange_copies(src_refs, land_refs, send_sems, recv_sems, local_sems, gather)
        for cp in local + remote:
            cp.start()
        token[...] = jnp.zeros_like(token)

    dep_specs, dep_ops = _dep_args(dep)
    hbm = lambda t: pltpu.HBM(t.shape, t.dtype)
    out = pl.pallas_call(
        body, name=name,
        out_shape=(pltpu.SemaphoreType.DMA((7 * ns,)), pltpu.SemaphoreType.DMA((7 * ns,)),
                   pltpu.SemaphoreType.DMA((ns,)), *[hbm(t) for t in srcs], *[hbm(t) for t in lands],
                   jax.ShapeDtypeStruct((8, LANE), F32)),
        in_specs=[_HBM] * (2 * na) + dep_specs,
        out_specs=(_SEM, _SEM, _SEM, *[_HBM] * (2 * na), pl.BlockSpec(memory_space=pltpu.VMEM)),
        input_output_aliases={i: 3 + i for i in range(2 * na)},
        compiler_params=pltpu.CompilerParams(has_side_effects=_EFFECT),
    )(*[pltpu.with_memory_space_constraint(t, pltpu.HBM) for t in srcs + lands], *dep_ops)
    return (out[:3], out[3:3 + na], out[3 + na:3 + 2 * na]), out[-1]


def _exchange_wait(handle, gather, after, name):
    sems, srcs, lands = handle
    na = len(srcs)

    def body(*refs):
        src_refs, land_refs = refs[:na], refs[na:2 * na]
        send_sems, recv_sems, local_sems = refs[2 * na:2 * na + 3]
        local, remote = _exchange_copies(src_refs, land_refs, send_sems, recv_sems, local_sems, gather)
        for cp in remote:
            cp.wait_send()
            cp.wait_recv()
        for cp in local:
            cp.wait()

    hbm = lambda t: pltpu.HBM(t.shape, t.dtype)
    out = pl.pallas_call(
        body, name=name, out_shape=(*[hbm(t) for t in srcs], *[hbm(t) for t in lands]),
        in_specs=[_HBM] * (2 * na) + [_SEM] * 3 + [pl.BlockSpec(memory_space=pl.ANY)],
        out_specs=tuple([_HBM] * (2 * na)), input_output_aliases={i: i for i in range(2 * na)},
        compiler_params=pltpu.CompilerParams(has_side_effects=_EFFECT),
    )(*srcs, *lands, *sems, after)
    return out[na:]


def _block_diag(w):
    h, a, b = w.shape
    eye = jnp.eye(h, dtype=w.dtype)
    return (w[:, :, None, :] * eye[:, None, :, None]).reshape(h * a, h * b)


def _block_diag_extract(m, h):
    a, b = m.shape[0] // h, m.shape[1] // h
    return jnp.stack([m[i * a:(i + 1) * a, i * b:(i + 1) * b] for i in range(h)], axis=0)


def _block_diag_take(m, h):
    a, b = m.shape[0] // h, m.shape[1] // h
    eye = jnp.eye(h, dtype=m.dtype)
    return (m.reshape(h, a, h, b) * eye[:, None, :, None]).sum(axis=2)


def _ffn_interleave(w):
    lead = w.shape[:-1]
    nb = D_FF // FFN_CB
    return jnp.swapaxes(w.reshape(*lead, 2, nb, FFN_CB), -3, -2).reshape(*lead, 2 * D_FF)


def _ffn_deinterleave(w):
    lead = w.shape[:-1]
    nb = D_FF // FFN_CB
    return jnp.swapaxes(w.reshape(*lead, nb, 2, FFN_CB), -3, -2).reshape(*lead, 2 * D_FF)


def _gather_full(gathered, axis):
    shape = list(gathered.shape[1:])
    shape[axis] *= N_DEV
    return jnp.moveaxis(gathered, 0, axis).reshape(shape)


def _scatter_blocks(full, axis):
    shape = list(full.shape)
    shape[axis:axis + 1] = [N_DEV, shape[axis] // N_DEV]
    return jnp.moveaxis(full.reshape(shape), axis, 0)


def _pad_to(flat, mult):
    pad = (-flat.shape[-1]) % mult
    if pad:
        flat = jnp.concatenate([flat, jnp.zeros(flat.shape[:-1] + (pad,), flat.dtype)], axis=-1)
    return flat


def _layer_fwd(h_in, h_in_t, w, cos, sin, l, dep, get_ffn):
    tag = "l%d_" % l
    proj = _mm_nn(h_in, w['w_in'], 512, D_IN, tag + "proj", dep=dep)
    qkv = _rope_fwd(proj, cos, sin, tag + "rope")
    outs, lses = [], []
    for d, qv in zip(DILATIONS, qkv):
        o, ls = _attn_fwd(qv, d, tag + "attn_d%d" % d)
        outs.append(o)
        lses.append(ls)
    lru = _lru_fwd(proj, w['lru_conv_w'], w['lru_conv_b'], w['lru_wr'], w['lru_br'], w['lru_wi'],
                   w['lru_bi'], w['lru_lambda'], tag + "lru")
    x_re, x_im, y_acc = _s5_scan_fwd(proj, w['s5_bb_re'], w['s5_bb_im'], w['s5_lam_re'], w['s5_lam_im'],
                                     w['s5_cc_re'], w['s5_cc_im'], tag + "s5_scan")
    s5, y_pre = _s5_out_fwd(proj, y_acc, w['s5_d'], w['s5_w_glu'], w['s5_b_glu'], tag + "s5_out")
    mixed, mixed_t, attn_o, attn_lse = _mix_fwd(outs, lses, lru, s5, w['mix_norm_g'], tag + "mix")
    mixo = _mm_nn(mixed, w['w_out'], 512, D_MODEL, tag + "out_proj")
    r1, h1, h1_t = _ln_fwd(h_in, mixo, w['ln1_g'], w['ln1_b'], tag + "ln1")
    w['w_up_g'], w['w_down'], ffn_dep = get_ffn(l, h1)
    up, act, act_t = _ffn_up_act(h1, w['w_up_g'], w['ffn_conv_w'], w['ffn_conv_b'], tag + "up_act", dep=ffn_dep)
    ffn = _mm_nn(act, w['w_down'], 512, D_MODEL, tag + "down_proj")
    r2, h2, *rest = _ln_fwd(h1, ffn, w['ln2_g'], w['ln2_b'], tag + "ln2", transposed=l + 1 < DEPTH)
    h2_t = rest[0] if rest else None
    saved = dict(h_in_t=h_in_t, proj=proj, qkv=qkv, lru=lru, x_re=x_re, x_im=x_im, y_pre=y_pre, s5=s5,
                 mixed_t=mixed_t, attn_o=attn_o, attn_lse=attn_lse, r1=r1, h1_t=h1_t, up=up, act_t=act_t, r2=r2)
    return h2, h2_t, saved


def _layer_bwd_ffn(dh2, sv, w, l, dep=None):
    tag = "l%d_" % l
    g = {}
    dr2, g['ln2_g'], g['ln2_b'] = _ln_bwd(sv['r2'], dh2, w['ln2_g'], tag + "ln2_bwd", dep=dep)
    g['w_down'] = _mm_dw(sv['act_t'], dr2, 1024, D_MODEL, 1024, tag + "down_dw")
    dact = _mm_nt(dr2, w['w_down'], 512, D_FF, tag + "down_dx")
    dup, g['ffn_conv_w'], g['ffn_conv_b'] = _ffn_act_bwd(sv['up'], dact, w['ffn_conv_w'], w['ffn_conv_b'],
                                                        tag + "ffn_act_bwd")
    g['w_up_g'] = _mm_up_dw(sv['h1_t'], dup, tag + "up_dw")
    dh1 = _mm_up_dx(dup, w['w_up_g'], dr2, ALPHA, 1024, tag + "up_dx")
    return dh1, g


def _layer_bwd_mix(dh1, sv, w, cos, sin, l, dep, g_ffn, after_out_grad, after_small_grads, after_in_grad):
    tag = "l%d_" % l
    g = {}
    dr1, g['ln1_g'], g['ln1_b'] = _ln_bwd(sv['r1'], dh1, w['ln1_g'], tag + "ln1_bwd", dep=dep)
    g['w_out'] = _mm_dw(sv['mixed_t'], dr1, 1024, D_MODEL, 1024, tag + "out_dw")
    dmixed = _mm_nt(dr1, w['w_out'], 512, D_MODEL, tag + "out_dx", dep=after_out_grad(l, g['w_out']))
    d_o, dlru, ds5, g['mix_norm_g'] = _mix_bwd(dmixed, sv['attn_o'][0], sv['lru'], sv['s5'], w['mix_norm_g'],
                                               tag + "mix_bwd")
    dy, dud, g['s5_d'], g['s5_w_glu'], g['s5_b_glu'] = _s5_out_bwd(
        sv['proj'], sv['y_pre'], ds5, w['s5_d'], w['s5_w_glu'], w['s5_b_glu'], tag + "s5_out_bwd")
    du, g['s5_lam_re'], g['s5_lam_im'], g['s5_bb_re'], g['s5_bb_im'], g['s5_cc_re'], g['s5_cc_im'] = \
        _s5_scan_bwd(sv['proj'], dy, dud, sv['x_re'], sv['x_im'], w['s5_bb_re'], w['s5_bb_im'],
                     w['s5_lam_re'], w['s5_lam_im'], w['s5_cc_re'], w['s5_cc_im'], tag + "s5_scan_bwd")
    (dxr, dgate, g['lru_conv_w'], g['lru_conv_b'], g['lru_wr'], g['lru_br'], g['lru_wi'], g['lru_bi'],
     g['lru_lambda']) = _lru_bwd(sv['proj'], dlru, w['lru_conv_w'], w['lru_conv_b'], w['lru_wr'],
                                 w['lru_br'], w['lru_wi'], w['lru_bi'], w['lru_lambda'], tag + "lru_bwd")
    token = after_small_grads(l, _finish_layer_grads({**g_ffn, **g}, w, l))
    dqkv = [_attn_bwd(sv['qkv'][b], sv['attn_o'][b], d_o[b], sv['attn_lse'][b], d, tag + "attn_bwd_d%d" % d,
                      dep=token if b == 0 else None)
            for b, d in enumerate(DILATIONS)]
    dproj = _dproj_assemble(dqkv, dxr, dgate, du, cos, sin, tag + "dproj")
    g_in = _mm_dw(sv['h_in_t'], dproj, 1024, D_IN, 1024, tag + "in_dw")
    return _mm_nt(dproj, w['w_in'], 512, D_MODEL, tag + "in_dx", add=dr1, add_scale=ALPHA,
                  dep=after_in_grad(l, g_in))


def _s5_rep(a):
    return jnp.repeat(a, S5_C, axis=0)


def _prepare_layer(p, l):
    w = {}
    for n in ('w_in', 'w_out', 's5_w_glu'):
        w[n] = p[n].astype(BF16)
    w['ffn_conv_w'] = _ffn_interleave(p['ffn_conv_w'])
    w['ffn_conv_b'] = _ffn_interleave(p['ffn_conv_b'])[None, :]
    w['lru_conv_w'] = p['lru_conv_w']
    for n in ('lru_conv_b', 'lru_br', 'lru_bi', 'lru_lambda', 's5_b_glu', 'mix_norm_g',
              'ln1_g', 'ln1_b', 'ln2_g', 'ln2_b'):
        w[n] = p[n][None, :]
    w['lru_wr'] = _block_diag(p['lru_wr']).astype(BF16)
    w['lru_wi'] = _block_diag(p['lru_wi']).astype(BF16)
    w['s5_d'] = p['s5_d'].reshape(1, S5_W)
    disc_in = (_s5_rep(p['s5_a_re']), _s5_rep(p['s5_a_im']),
               _s5_rep(jnp.broadcast_to(p['s5_log_step'][:, None], (S5_G, S5_P))),
               jnp.swapaxes(p['s5_b_re'], 1, 2).reshape(S5_W, S5_P),
               jnp.swapaxes(p['s5_b_im'], 1, 2).reshape(S5_W, S5_P))
    ab_re, ab_im, bb_re, bb_im = _s5_disc_fwd(*disc_in, "l%d_s5_disc" % l)
    w['s5_disc_in'] = disc_in
    w['s5_lam_re'] = ab_re.reshape(S5_G, S5_C, S5_P)[:, 0, :].reshape(1, S5_STATES)
    w['s5_lam_im'] = ab_im.reshape(S5_G, S5_C, S5_P)[:, 0, :].reshape(1, S5_STATES)
    w['s5_bb_re'] = _block_diag(bb_re.reshape(S5_G, S5_C, S5_P)).astype(BF16)
    w['s5_bb_im'] = _block_diag(bb_im.reshape(S5_G, S5_C, S5_P)).astype(BF16)
    w['s5_cc_re'] = _block_diag(jnp.swapaxes(p['s5_c_re'], 1, 2)).astype(BF16)
    w['s5_cc_im'] = _block_diag(jnp.swapaxes(p['s5_c_im'], 1, 2)).astype(BF16)
    return w


def _finish_layer_grads(g, w, l):
    out = {}
    for n in ('s5_w_glu', 'lru_conv_w'):
        out[n] = g[n]
    out['ffn_conv_w'] = _ffn_deinterleave(g['ffn_conv_w'])
    out['ffn_conv_b'] = _ffn_deinterleave(g['ffn_conv_b'])[0]
    for n in ('lru_conv_b', 'lru_br', 'lru_bi', 'lru_lambda', 's5_b_glu', 'mix_norm_g',
              'ln1_g', 'ln1_b', 'ln2_g', 'ln2_b'):
        out[n] = g[n][0]
    out['lru_wr'] = _block_diag_extract(g['lru_wr'], LRU_W // HEAD)
    out['lru_wi'] = _block_diag_extract(g['lru_wi'], LRU_W // HEAD)
    out['s5_d'] = g['s5_d'].reshape(S5_G, S5_C)
    out['s5_c_re'] = jnp.swapaxes(_block_diag_take(g['s5_cc_re'], S5_G), 1, 2)
    out['s5_c_im'] = jnp.swapaxes(_block_diag_take(g['s5_cc_im'], S5_G), 1, 2)
    rep = lambda v: _s5_rep(v.reshape(S5_G, S5_P)) * (1.0 / S5_C)
    cts = (rep(g['s5_lam_re']), rep(g['s5_lam_im']),
           _block_diag_take(g['s5_bb_re'], S5_G).reshape(S5_W, S5_P),
           _block_diag_take(g['s5_bb_im'], S5_G).reshape(S5_W, S5_P))
    da_re, da_im, dls, dbt_re, dbt_im = _s5_disc_bwd(*w['s5_disc_in'], cts, "l%d_s5_disc_bwd" % l)
    out['s5_a_re'] = da_re.reshape(S5_G, S5_C, S5_P).sum(axis=1)
    out['s5_a_im'] = da_im.reshape(S5_G, S5_C, S5_P).sum(axis=1)
    out['s5_log_step'] = dls.reshape(S5_G, S5_C * S5_P).sum(axis=1)
    out['s5_b_re'] = jnp.swapaxes(dbt_re.reshape(S5_G, S5_C, S5_P), 1, 2)
    out['s5_b_im'] = jnp.swapaxes(dbt_im.reshape(S5_G, S5_C, S5_P), 1, 2)
    return out


def _run_step(x, target, get_layer, get_ffn, after_ffn_grads, after_out_grad, after_small_grads, after_in_grad):
    cos, sin = _rope_tables(x.shape[0])
    h, h_t = x, _transpose_bf16(x, "x_transpose")
    ws, saved = [], []
    for l in range(DEPTH):
        p, dep = get_layer(l, h)
        ws.append(_prepare_layer(p, l))
        h, h_t, sv = _layer_fwd(h, h_t, ws[l], cos, sin, l, dep, get_ffn)
        saved.append(sv)
    dh, loss_vec = _loss_head(h, target)
    dep = None
    for l in reversed(range(DEPTH)):
        dh1, g = _layer_bwd_ffn(dh, saved[l], ws[l], l, dep)
        dep = after_ffn_grads(l, g)
        dh = _layer_bwd_mix(dh1, saved[l], ws[l], cos, sin, l, dep, g, after_out_grad, after_small_grads,
                            after_in_grad)
        dep = None
    return loss_vec[0, 0], dh


def _local_step(x, target, layers):
    grads = [{} for _ in range(DEPTH)]

    def ffn(l, h1):
        return layers[l]['w_up_g'].astype(BF16), layers[l]['w_down'].astype(BF16), None

    def keep_ffn(l, g):
        grads[l].update(w_up_g=g['w_up_g'], w_down=g['w_down'])

    def keep_small(l, g):
        grads[l].update(g)

    loss, dx = _run_step(x, target, lambda l, h: (layers[l], None), ffn, keep_ffn,
                         lambda l, g: grads[l].update(w_out=g), keep_small, lambda l, g: grads[l].update(w_in=g))
    return loss, dx, grads


def kernel(x, w_in, lru_conv_w, lru_conv_b, lru_wr, lru_br, lru_wi, lru_bi, lru_lambda, s5_a_re, s5_a_im, s5_b_re, s5_b_im, s5_c_re, s5_c_im, s5_d, s5_log_step, s5_w_glu, s5_b_glu, mix_norm_g, w_out, ln1_g, ln1_b, w_up, ffn_conv_w, ffn_conv_b, w_down, ln2_g, ln2_b, loss_target, m_w_in, m_lru_conv_w, m_lru_conv_b, m_lru_wr, m_lru_br, m_lru_wi, m_lru_bi, m_lru_lambda, m_s5_a_re, m_s5_a_im, m_s5_b_re, m_s5_b_im, m_s5_c_re, m_s5_c_im, m_s5_d, m_s5_log_step, m_s5_w_glu, m_s5_b_glu, m_mix_norm_g, m_w_out, m_ln1_g, m_ln1_b, m_w_up, m_ffn_conv_w, m_ffn_conv_b, m_w_down, m_ln2_g, m_ln2_b, v_w_in, v_lru_conv_w, v_lru_conv_b, v_lru_wr, v_lru_br, v_lru_wi, v_lru_bi, v_lru_lambda, v_s5_a_re, v_s5_a_im, v_s5_b_re, v_s5_b_im, v_s5_c_re, v_s5_c_im, v_s5_d, v_s5_log_step, v_s5_w_glu, v_s5_b_glu, v_mix_norm_g, v_w_out, v_ln1_g, v_ln1_b, v_w_up, v_ffn_conv_w, v_ffn_conv_b, v_w_down, v_ln2_g, v_ln2_b):
    args = locals()
    wl = {n: args[n] for n in WEIGHTS}
    ml = {n: args['m_' + n] for n in WEIGHTS}
    vl = {n: args['v_' + n] for n in WEIGHTS}

    small_sizes = [int(wl[n].size) for n in SMALL_SHARDED]
    small_flat = _pad_to(jnp.concatenate([wl[n].reshape(-1) for n in SMALL_SHARDED]), 8 * 1024)
    small_all, = _all_gather([small_flat.reshape(-1, 1024)], "gather_small")
    small_all = small_all.reshape(N_DEV, -1)
    small_full, off = {}, 0
    for n, sz in zip(SMALL_SHARDED, small_sizes):
        small_full[n] = _gather_full(small_all[:, off:off + sz].reshape((N_DEV,) + wl[n].shape), SHARD_AXIS[n])
        off += sz
    def mixer_params(l, gathered):
        g_in, g_out = gathered
        p = {n: wl[n][l] for n in REPLICATED}
        p.update({n: small_full[n][l] for n in SMALL_SHARDED})
        p['w_in'] = _gather_full(g_in, 1)
        p['w_out'] = g_out.reshape(D_MODEL, D_MODEL)
        return p

    mix_names, ffn_names = ('w_in', 'w_out'), ('w_up', 'w_down')
    shards = lambda names, l: [wl[n][l].astype(BF16) for n in names]
    mix0 = _all_gather(shards(mix_names, 0), "gather_mix_l0")
    gathers = {}
    gathers[0, 'ffn'], ffn0_token = _exchange_start(shards(ffn_names, 0), True, "gather_ffn_l0_start", dep=mix0[0])
    def get_layer(l, h):
        if l == 0:
            return mixer_params(0, mix0), ffn0_token
        return mixer_params(1, _exchange_wait(gathers[1, 'mix'], True, h, "gather_mix_l1_wait")), None

    def get_ffn(l, h1):
        g_up, g_down = _exchange_wait(gathers[l, 'ffn'], True, h1, "gather_ffn_l%d_wait" % l)
        token = None
        if l == 0:
            gathers[1, 'mix'], token = _exchange_start(shards(mix_names, 1), True, "gather_mix_l1_start", dep=g_up)
            gathers[1, 'ffn'], token = _exchange_start(shards(ffn_names, 1), True, "gather_ffn_l1_start", dep=token)
        return g_up, g_down.reshape(D_FF, D_MODEL), token

    scatters = {}

    def after_ffn_grads(l, g):
        send = [g['w_up_g'], g['w_down'].reshape(N_DEV, D_FF // N_DEV, D_MODEL)]
        scatters[l, 'ffn'], token = _exchange_start(send, False, "scatter_ffn_l%d_start" % l)
        return token

    def after_out_grad(l, g_out):
        send = [g_out.reshape(N_DEV, D_MODEL // N_DEV, D_MODEL)]
        scatters[l, 'out'], token = _exchange_start(send, False, "scatter_out_l%d_start" % l)
        return token

    def after_in_grad(l, g_in):
        send = _scatter_blocks(g_in, 1)
        if l == 0:
            send = send.astype(BF16)
        scatters[l, 'in'], token = _exchange_start([send], False, "scatter_in_l%d_start" % l)
        return token

    def after_small_grads(l, g):
        rep = [g[n][None] for n in REPLICATED]
        shd = [_scatter_blocks(g[n], SHARD_AXIS[n] - 1)[:, None] for n in SMALL_SHARDED]
        scatters[l, 'rep'], token = _exchange_start(rep, True, "gather_rep_grads_l%d_start" % l)
        scatters[l, 'small'], token = _exchange_start(shd, False, "scatter_small_l%d_start" % l, dep=token)
        return token

    loss_local, grad_x = _run_step(x[0], loss_target[0], get_layer, get_ffn, after_ffn_grads, after_out_grad,
                                   after_small_grads, after_in_grad)
    loss = lax.psum(loss_local, AXES)

    results = {}
    big_prev = {n: None for n in BIG}

    def finish_big(l, part, names, after):
        landed = _exchange_wait(scatters[l, part], False, after, "scatter_%s_l%d_wait" % (part, l))
        for n, ld in zip(names, landed):
            big_prev[n] = _adamw_sum(ld, wl[n], ml[n], vl[n], l, big_prev[n], "adamw_%s_l%d" % (n, l))

    for l, part, names in ((1, 'ffn', ffn_names), (1, 'out', ('w_out',)), (1, 'in', ('w_in',)),
                           (0, 'ffn', ffn_names), (0, 'out', ('w_out',))):
        finish_big(l, part, names, grad_x)

    kinds = ('grad', 'delta', 'm', 'v')
    landed = [dict(zip(REPLICATED + SMALL_SHARDED,
                       list(_exchange_wait(scatters[l, 'rep'], True, grad_x, "gather_rep_grads_l%d_wait" % l)) +
                       list(_exchange_wait(scatters[l, 'small'], False, grad_x, "scatter_small_l%d_wait" % l))))
              for l in range(DEPTH)]
    matrices = ['lru_wr', 'lru_wi', 's5_a_re', 's5_a_im', 's5_c_re', 's5_c_im', 's5_d']
    widest = ['s5_b_re', 's5_b_im']
    vectors = [n for n in REPLICATED + SMALL_SHARDED if n not in matrices + widest]
    last = None
    for tag, names in (("vectors", vectors), ("matrices", matrices), ("s5_b", widest)):
        res = _adamw_many([[landed[l][n] for n in names] for l in range(DEPTH)], [wl[n] for n in names],
                          [ml[n] for n in names], [vl[n] for n in names], "adamw_" + tag)
        for kind, arrs in zip(kinds, res):
            for n, a in zip(names, arrs):
                results[kind, n] = a
        last = res[0][0]
    finish_big(0, 'in', ('w_in',), last)
    for n in BIG:
        results['grad', n], results['delta', n], results['m', n], results['v', n] = big_prev[n]

    out = [loss, grad_x[None]]
    for kind in kinds:
        out.extend(results[kind, n] for n in WEIGHTS)
    return tuple(out)
```

```python
import functools
import math

import jax
import jax.numpy as jnp
from jax import lax
from jax.experimental import pallas as pl
from jax.experimental.pallas import tpu as pltpu

F32 = jnp.float32
BF16 = jnp.bfloat16

N_DEV = 8
DEPTH = 2
D_MODEL = 1024
ATTN_W = 384
LRU_W = 384
S5_W = 256
D_IN = 2176
D_FF = 3072
HEAD = 64
ATTN_BLK = 128
ATTN_TILE = 1024
DILATIONS = (1, 4, 16)
S5_G = 16
S5_P = 64
S5_C = 16
S5_STATES = S5_G * S5_P
LRU_C = 8.0
LRU_CONV = 4
FFN_CONV = 3
ROPE_THETA = 10000.0
ALPHA = (2 * DEPTH) ** 0.25
LN_EPS = 1e-5
RMS_EPS = 1e-6
ADAM_LR, ADAM_B1, ADAM_B2, ADAM_EPS, ADAM_WD, ADAM_STEP = 0.001, 0.9, 0.999, 1e-8, 0.01, 10

LANE = 128
SCAN_T = 256
S5_BLK = 256
FFN_CB = 2 * D_FF // N_DEV
VMEM_LIMIT = 56 * 1024 * 1024

AXES = ("x", "y", "c")

WEIGHTS = ['w_in', 'lru_conv_w', 'lru_conv_b', 'lru_wr', 'lru_br', 'lru_wi', 'lru_bi', 'lru_lambda',
           's5_a_re', 's5_a_im', 's5_b_re', 's5_b_im', 's5_c_re', 's5_c_im', 's5_d', 's5_log_step',
           's5_w_glu', 's5_b_glu', 'mix_norm_g', 'w_out', 'ln1_g', 'ln1_b', 'w_up', 'ffn_conv_w',
           'ffn_conv_b', 'w_down', 'ln2_g', 'ln2_b']
SHARD_AXIS = {'w_in': 2, 'lru_conv_w': 2, 's5_w_glu': 1, 'w_out': 1, 'w_up': 2, 'ffn_conv_w': 2, 'w_down': 1}
BIG = ['w_in', 'w_out', 'w_up', 'w_down']
SMALL_SHARDED = ['lru_conv_w', 'ffn_conv_w', 's5_w_glu']
REPLICATED = [n for n in WEIGHTS if n not in SHARD_AXIS]


def _cparams(sem=None):
    return pltpu.CompilerParams(dimension_semantics=sem, vmem_limit_bytes=VMEM_LIMIT)


def _ffn_dev(jb):
    return jb // 2 + (N_DEV // 2) * (jb % 2)


def _gelu(x):
    c = math.sqrt(2.0 / math.pi)
    t = jnp.tanh(c * (x + 0.044715 * (x * x * x)))
    return 0.5 * x * (1.0 + t)


def _gelu_grad(x):
    c = math.sqrt(2.0 / math.pi)
    x2 = x * x
    t = jnp.tanh(c * (x + 0.044715 * (x2 * x)))
    return 0.5 * (1.0 + t) + 0.5 * x * (1.0 - t * t) * (c * (1.0 + 3.0 * 0.044715 * x2))


def _sigmoid(x):
    return 1.0 / (1.0 + jnp.exp(-x))


def _log1p(x):
    u = 1.0 + x
    d = u - 1.0
    return jnp.where(d == 0.0, x, jnp.log(u) * (x / jnp.where(d == 0.0, 1.0, d)))


def _softplus(x):
    return jnp.maximum(x, 0.0) + _log1p(jnp.exp(-jnp.abs(x)))


def _expm1(x):
    return jnp.tanh(0.5 * x) * (jnp.exp(x) + 1.0)


def _dot(a, b):
    return jnp.dot(a.astype(BF16), b.astype(BF16), preferred_element_type=F32)


def _dot_nt(a, b):
    return lax.dot_general(a.astype(BF16), b.astype(BF16), (((1,), (1,)), ((), ())),
                           preferred_element_type=F32)


def _dot_tn(a, b):
    return lax.dot_general(a.astype(BF16), b.astype(BF16), (((0,), (0,)), ((), ())),
                           preferred_element_type=F32)


def _rows(shape):
    return lax.broadcasted_iota(jnp.int32, shape, 0)


def _shift_down_prev(x, s, prev8):
    if s == 0:
        return x
    t, l = x.shape
    r = pltpu.roll(x, s, axis=0)
    pr = pltpu.roll(prev8, s, axis=0)
    pad = jnp.concatenate([pr, jnp.zeros((t - 8, l), x.dtype)], axis=0)
    return jnp.where(_rows(x.shape) < s, pad, r)


def _shift_up_next(x, s, next8):
    if s == 0:
        return x
    t, l = x.shape
    r = pltpu.roll(x, t - s, axis=0)
    nx = pltpu.roll(next8, 8 - s, axis=0)
    pad = jnp.concatenate([jnp.zeros((t - 8, l), x.dtype), nx], axis=0)
    return jnp.where(_rows(x.shape) >= t - s, pad, r)


SUB = 8


def _tile_shift(x, s, fill, reverse):
    t = x.shape[0]
    pos = _rows(x.shape) & (SUB - 1)
    if reverse:
        return jnp.where(pos < SUB - s, pltpu.roll(x, t - s, axis=0), fill)
    return jnp.where(pos >= s, pltpu.roll(x, s, axis=0), fill)


def _scan_chunk(a, x, carry, reverse=False):
    s = 1
    while s < SUB:
        x = x + a * _tile_shift(x, s, 0.0, reverse)
        a = a * _tile_shift(a, s, 1.0, reverse)
        s *= 2
    nv = x.shape[0] // SUB
    out = [None] * nv
    for v in (reversed(range(nv)) if reverse else range(nv)):
        rows = slice(v * SUB, (v + 1) * SUB)
        out[v] = x[rows, :] + a[rows, :] * carry
        carry = out[v][0:1, :] if reverse else out[v][SUB - 1:SUB, :]
    return jnp.concatenate(out, axis=0)


def _cmul(ar, ai, br, bi):
    return ar * br - ai * bi, ar * bi + ai * br


def _cscan_consts(lr, li, reverse):
    pows = [(lr, li)]
    for _ in range(2):
        pows.append(_cmul(*pows[-1], *pows[-1]))
    rows = [(lr, li)]
    for _ in range(SUB - 1):
        rows.append(_cmul(*rows[-1], lr, li))
    if reverse:
        rows = rows[::-1]
    return pows, (jnp.concatenate([r for r, _ in rows], axis=0), jnp.concatenate([i for _, i in rows], axis=0))


def _cscan_chunk(xr, xi, consts, carry, reverse=False):
    pows, (p8r, p8i) = consts
    s = 1
    for pr, pi in pows:
        sr = _tile_shift(xr, s, 0.0, reverse)
        si = _tile_shift(xi, s, 0.0, reverse)
        xr, xi = xr + pr * sr - pi * si, xi + pr * si + pi * sr
        s *= 2
    nv = xr.shape[0] // SUB
    out_r, out_i = [None] * nv, [None] * nv
    cr, ci = carry
    for v in (reversed(range(nv)) if reverse else range(nv)):
        rows = slice(v * SUB, (v + 1) * SUB)
        out_r[v] = xr[rows, :] + p8r * cr - p8i * ci
        out_i[v] = xi[rows, :] + p8r * ci + p8i * cr
        edge = slice(0, 1) if reverse else slice(SUB - 1, SUB)
        cr, ci = out_r[v][edge, :], out_i[v][edge, :]
    return jnp.concatenate(out_r, axis=0), jnp.concatenate(out_i, axis=0)


def _dep_args(dep):
    return ([], []) if dep is None else ([pl.BlockSpec(memory_space=pl.ANY)], [dep])


def _mm_nn(a, b, tm, tn, name, out_dtype=F32, dep=None):
    m, k = a.shape
    n = b.shape[1]

    def body(a_ref, b_ref, *rest):
        o_ref = rest[-1]
        o_ref[...] = _dot(a_ref[...], b_ref[...]).astype(out_dtype)

    dep_specs, dep_ops = _dep_args(dep)
    return pl.pallas_call(
        body, out_shape=jax.ShapeDtypeStruct((m, n), out_dtype), grid=(n // tn, m // tm),
        in_specs=[pl.BlockSpec((tm, k), lambda j, i: (i, 0)),
                  pl.BlockSpec((k, tn), lambda j, i: (0, j))] + dep_specs,
        out_specs=pl.BlockSpec((tm, tn), lambda j, i: (i, j)), name=name,
        compiler_params=_cparams(("parallel", "parallel")))(a, b, *dep_ops)


def _mm_nt(a, w, tm, tn, name, add=None, add_scale=1.0, dep=None):
    m, k = a.shape
    n = w.shape[0]

    def body(a_ref, w_ref, *rest):
        o_ref = rest[-1]
        if add is None:
            o_ref[...] = _dot_nt(a_ref[...], w_ref[...])
        else:
            o_ref[...] = _dot_nt(a_ref[...], w_ref[...]) + add_scale * rest[0][...]

    in_specs = [pl.BlockSpec((tm, k), lambda j, i: (i, 0)), pl.BlockSpec((tn, k), lambda j, i: (j, 0))]
    args = [a, w]
    if add is not None:
        in_specs.append(pl.BlockSpec((tm, tn), lambda j, i: (i, j)))
        args.append(add)
    dep_specs, dep_ops = _dep_args(dep)
    return pl.pallas_call(
        body, out_shape=jax.ShapeDtypeStruct((m, n), F32), grid=(n // tn, m // tm),
        in_specs=in_specs + dep_specs, out_specs=pl.BlockSpec((tm, tn), lambda j, i: (i, j)), name=name,
        compiler_params=_cparams(("parallel", "parallel")))(*args, *dep_ops)


def _mm_dw(at, b, tm, tn, ts, name):
    m, s = at.shape
    n = b.shape[1]

    def body(a_ref, b_ref, o_ref):
        @pl.when(pl.program_id(2) == 0)
        def _():
            o_ref[...] = jnp.zeros_like(o_ref)
        o_ref[...] += _dot(a_ref[...], b_ref[...])

    return pl.pallas_call(
        body, out_shape=jax.ShapeDtypeStruct((m, n), F32), grid=(m // tm, n // tn, s // ts),
        in_specs=[pl.BlockSpec((tm, ts), lambda i, j, k: (i, k)), pl.BlockSpec((ts, tn), lambda i, j, k: (k, j))],
        out_specs=pl.BlockSpec((tm, tn), lambda i, j, k: (i, j)), name=name,
        compiler_params=_cparams(("parallel", "parallel", "arbitrary")))(at, b)


def _transpose_bf16(x, name):
    s, d = x.shape
    tm = 512

    def body(x_ref, o_ref):
        o_ref[...] = x_ref[...].T.astype(BF16)

    return pl.pallas_call(
        body, out_shape=jax.ShapeDtypeStruct((d, s), BF16), grid=(s // tm,),
        in_specs=[pl.BlockSpec((tm, d), lambda i: (i, 0))], out_specs=pl.BlockSpec((d, tm), lambda i: (0, i)),
        name=name, compiler_params=_cparams(("parallel",)))(x)


def _mm_up_dw(ht, dup, name):
    d, s = ht.shape

    def body(a_ref, b_ref, o_ref):
        o_ref[...] = _dot(a_ref[...], b_ref[...])

    return pl.pallas_call(
        body, out_shape=jax.ShapeDtypeStruct((N_DEV, d, FFN_CB), F32), grid=(N_DEV,),
        in_specs=[pl.BlockSpec((d, s), lambda j: (0, 0)), pl.BlockSpec((s, FFN_CB), lambda j: (0, j))],
        out_specs=pl.BlockSpec((None, d, FFN_CB), lambda j: (_ffn_dev(j), 0, 0)), name=name,
        compiler_params=_cparams(("parallel",)))(ht, dup)


def _ln_fwd(a, b, g, bias, name, transposed=True):
    s, d = a.shape
    tm = 512

    def body(a_ref, b_ref, g_ref, bias_ref, r_ref, h_ref, *ht_ref):
        r = ALPHA * a_ref[...] + b_ref[...]
        mu = jnp.mean(r, axis=-1, keepdims=True)
        xc = r - mu
        var = jnp.mean(xc * xc, axis=-1, keepdims=True)
        r_ref[...] = r
        h = xc * lax.rsqrt(var + LN_EPS) * g_ref[...] + bias_ref[...]
        h_ref[...] = h
        if transposed:
            ht_ref[0][...] = h.T.astype(BF16)

    row = pl.BlockSpec((tm, d), lambda i: (i, 0))
    vec = pl.BlockSpec((1, d), lambda i: (0, 0))
    shapes = [jax.ShapeDtypeStruct((s, d), F32), jax.ShapeDtypeStruct((s, d), F32)]
    specs = [row, row]
    if transposed:
        shapes.append(jax.ShapeDtypeStruct((d, s), BF16))
        specs.append(pl.BlockSpec((d, tm), lambda i: (0, i)))
    return pl.pallas_call(
        body, out_shape=tuple(shapes), grid=(s // tm,), in_specs=[row, row, vec, vec],
        out_specs=tuple(specs), name=name, compiler_params=_cparams(("parallel",)))(a, b, g, bias)


def _ln_bwd(r, dh, g, name, dep=None):
    s, d = r.shape
    tm = 512

    def body(r_ref, dh_ref, g_ref, *rest):
        dr_ref, dg_ref, db_ref = rest[-3:]

        @pl.when(pl.program_id(0) == 0)
        def _():
            dg_ref[...] = jnp.zeros_like(dg_ref)
            db_ref[...] = jnp.zeros_like(db_ref)
        rr = r_ref[...]
        dh_ = dh_ref[...]
        mu = jnp.mean(rr, axis=-1, keepdims=True)
        xc = rr - mu
        var = jnp.mean(xc * xc, axis=-1, keepdims=True)
        rstd = lax.rsqrt(var + LN_EPS)
        xh = xc * rstd
        dxh = dh_ * g_ref[...]
        m1 = jnp.mean(dxh, axis=-1, keepdims=True)
        m2 = jnp.mean(dxh * xh, axis=-1, keepdims=True)
        dr_ref[...] = rstd * (dxh - m1 - xh * m2)
        dg_ref[...] += jnp.sum(dh_ * xh, axis=0, keepdims=True)
        db_ref[...] += jnp.sum(dh_, axis=0, keepdims=True)

    row = pl.BlockSpec((tm, d), lambda i: (i, 0))
    vec = pl.BlockSpec((1, d), lambda i: (0, 0))
    dep_specs, dep_ops = _dep_args(dep)
    return pl.pallas_call(
        body, out_shape=(jax.ShapeDtypeStruct((s, d), F32), jax.ShapeDtypeStruct((1, d), F32),
                         jax.ShapeDtypeStruct((1, d), F32)),
        grid=(s // tm,), in_specs=[row, row, vec] + dep_specs, out_specs=(row, vec, vec), name=name,
        compiler_params=_cparams(("arbitrary",)))(r, dh, g, *dep_ops)


def _loss_head(y, target):
    s, d = y.shape
    tm = 512

    def body(y_ref, t_ref, dy_ref, l_ref):
        @pl.when(pl.program_id(0) == 0)
        def _():
            l_ref[...] = jnp.zeros_like(l_ref)
        e = y_ref[...] - t_ref[...]
        dy_ref[...] = e * (1.0 / d)
        part = 0.5 * jnp.sum(jnp.mean(e * e, axis=-1, keepdims=True), axis=0, keepdims=True)
        l_ref[...] += jnp.broadcast_to(part, l_ref.shape)

    row = pl.BlockSpec((tm, d), lambda i: (i, 0))
    return pl.pallas_call(
        body, out_shape=(jax.ShapeDtypeStruct((s, d), F32), jax.ShapeDtypeStruct((1, LANE), F32)),
        grid=(s // tm,), in_specs=[row, row], out_specs=(row, pl.BlockSpec((1, LANE), lambda i: (0, 0))),
        name="loss_head", compiler_params=_cparams(("arbitrary",)))(y, target)


def _rope_tables(s):
    half = HEAD // 2
    pos = jnp.arange(s, dtype=F32)
    inv = ROPE_THETA ** (-jnp.arange(half, dtype=F32) * 2.0 / HEAD)
    ang = pos[:, None] * inv[None, :]
    cos, sin = jnp.cos(ang), jnp.sin(ang)
    cos = jnp.concatenate([cos, cos, cos, cos], axis=1)
    sin = jnp.concatenate([-sin, sin, -sin, sin], axis=1)
    return cos, sin


def _rotate(x, cos, sin):
    lane = lax.broadcasted_iota(jnp.int32, x.shape, 1)
    partner = jnp.where((lane % HEAD) < HEAD // 2, pltpu.roll(x, LANE - HEAD // 2, axis=1),
                        pltpu.roll(x, HEAD // 2, axis=1))
    return x * cos + partner * sin


def _class_rows(c, d, tm):
    return pl.ds(c, tm // d, stride=d) if d > 1 else pl.ds(0, tm)


def _dilated_spec(tm, d, w):
    return pl.BlockSpec((tm // d, d * w), lambda i: (i, 0))


def _token_scratch(tm, w):
    return pltpu.VMEM((w // LANE, tm, LANE), F32)


def _to_tokens(src_ref, dst3, d, tm):
    nj = dst3.shape[0]
    for cls in range(d):
        for j in range(nj):
            col = (cls * nj + j) * LANE
            dst3.at[j][_class_rows(cls, d, tm), :] = src_ref[:, col:col + LANE]


def _to_dilated(src3, dst_ref, d, tm):
    nj = src3.shape[0]
    for cls in range(d):
        for j in range(nj):
            col = (cls * nj + j) * LANE
            dst_ref[:, col:col + LANE] = src3.at[j][_class_rows(cls, d, tm), :].astype(dst_ref.dtype)


def _token_value(src3):
    return jnp.concatenate([src3[j] for j in range(src3.shape[0])], axis=1)


def _rope_fwd(proj, cos, sin, name):
    s = proj.shape[0]
    tm = 512
    w = 3 * ATTN_W
    nj = w // LANE

    def body(*refs):
        p_refs, (c_ref, s_ref), o_refs, rot = refs[:nj], refs[nj:nj + 2], refs[nj + 2:nj + 5], refs[nj + 5]
        c, sn = c_ref[...], s_ref[...]
        for j in range(nj):
            x = p_refs[j][...]
            rot[j] = _rotate(x, c, sn) if j < 2 * ATTN_W // LANE else x
        for d, o_ref in zip(DILATIONS, o_refs):
            _to_dilated(rot, o_ref, d, tm)

    tab = pl.BlockSpec((tm, LANE), lambda i: (i, 0))
    cols = [pl.BlockSpec((tm, LANE), functools.partial(lambda i, j: (i, j), j=j)) for j in range(nj)]
    return pl.pallas_call(
        body, out_shape=tuple(jax.ShapeDtypeStruct((s // d, d * w), BF16) for d in DILATIONS),
        grid=(s // tm,), in_specs=cols + [tab, tab],
        out_specs=tuple(_dilated_spec(tm, d, w) for d in DILATIONS),
        scratch_shapes=[_token_scratch(tm, w)], name=name,
        compiler_params=_cparams(("parallel",)))(*[proj] * nj, cos, sin)


def _dproj_assemble(dqkv_list, dxr, dgate, du, cos, sin, name):
    s = dxr.shape[0]
    tm = 512
    nq = 3 * ATTN_W // LANE

    def body(*refs):
        br = refs[:9]
        dxr_ref, dg_ref, du_ref, c_ref, s_ref, o_ref = refs[9:15]
        tok = refs[15:]
        c, sn = c_ref[...], -s_ref[...]
        for part in range(3):
            for b, d in enumerate(DILATIONS[1:], start=1):
                _to_tokens(br[3 * b + part], tok[2 * part + b - 1], d, tm)
        for j in range(nq):
            part, jj = divmod(j, ATTN_W // LANE)
            x = br[part][:, jj * LANE:(jj + 1) * LANE] + tok[2 * part][jj] + tok[2 * part + 1][jj]
            if part < 2:
                x = _rotate(x, c, sn)
            o_ref[:, j * LANE:(j + 1) * LANE] = x.astype(BF16)
        o_ref[:, 3 * ATTN_W:3 * ATTN_W + LRU_W] = dxr_ref[...].astype(BF16)
        o_ref[:, 3 * ATTN_W + LRU_W:3 * ATTN_W + 2 * LRU_W] = dg_ref[...].astype(BF16)
        o_ref[:, 3 * ATTN_W + 2 * LRU_W:] = du_ref[...].astype(BF16)

    a_spec = pl.BlockSpec((tm, ATTN_W), lambda i: (i, 0))
    tab = pl.BlockSpec((tm, LANE), lambda i: (i, 0))
    ordered = [dqkv_list[b][p] for b in range(3) for p in range(3)]
    d_specs = [_dilated_spec(tm, d, ATTN_W) for d in DILATIONS for _ in range(3)]
    return pl.pallas_call(
        body, out_shape=jax.ShapeDtypeStruct((s, D_IN), BF16), grid=(s // tm,),
        in_specs=d_specs + [a_spec, a_spec, pl.BlockSpec((tm, S5_W), lambda i: (i, 0)), tab, tab],
        out_specs=pl.BlockSpec((tm, D_IN), lambda i: (i, 0)),
        scratch_shapes=[_token_scratch(tm, ATTN_W)] * 6, name=name,
        compiler_params=_cparams(("parallel",)))(*ordered, dxr, dgate, du, cos, sin)


def _attn_tiles(s, d):
    m = s // d
    tq = min(m, ATTN_TILE)
    return m, tq, tq // ATTN_BLK


def _band_mask(qb):
    qi = lax.broadcasted_iota(jnp.int32, (ATTN_BLK, 2 * ATTN_BLK), 0)
    ki = lax.broadcasted_iota(jnp.int32, (ATTN_BLK, 2 * ATTN_BLK), 1)
    dist = qi + ATTN_BLK - ki
    return (dist >= 0) & (dist <= ATTN_BLK) & ((ki >= ATTN_BLK) | (qb > 0))


def _head_cols(h):
    return (slice(h * HEAD, (h + 1) * HEAD), slice(ATTN_W + h * HEAD, ATTN_W + (h + 1) * HEAD),
            slice(2 * ATTN_W + h * HEAD, 2 * ATTN_W + (h + 1) * HEAD))


def _attn_fwd(qv, d, name):
    m = qv.shape[0]
    w3 = 3 * ATTN_W
    _, tq, n = _attn_tiles(m * d, d)
    scale = HEAD ** -0.5

    def body(x_ref, p_ref, o_ref, l_ref):
        b = pl.program_id(1)

        def block(i, first):
            r0 = 0 if first else pl.multiple_of(i * ATTN_BLK, ATTN_BLK)
            rows = pl.ds(r0, ATTN_BLK)
            valid = _band_mask(b * n + i)
            if not first:
                krows = pl.ds(pl.multiple_of(i * ATTN_BLK - ATTN_BLK, ATTN_BLK), 2 * ATTN_BLK)
            for h in range(ATTN_W // HEAD):
                qs, ks, vs = _head_cols(h)
                q = x_ref[rows, qs]
                if first:
                    k = jnp.concatenate([p_ref[:, ks], x_ref[0:ATTN_BLK, ks]], axis=0)
                    v = jnp.concatenate([p_ref[:, vs], x_ref[0:ATTN_BLK, vs]], axis=0)
                else:
                    k = x_ref[krows, ks]
                    v = x_ref[krows, vs]
                sc = jnp.where(valid, _dot_nt(q, k) * scale, -1e30)
                mx = jnp.max(sc, axis=-1, keepdims=True)
                p = jnp.exp(sc - mx)
                l = jnp.sum(p, axis=-1, keepdims=True)
                o_ref[rows, qs] = _dot(p, v) / l
                l_ref[rows, qs] = jnp.broadcast_to(mx + jnp.log(l), (ATTN_BLK, HEAD))

        block(0, True)
        if n > 1:
            def loop(i, carry):
                block(i, False)
                return carry
            lax.fori_loop(1, n, loop, 0)

    shp = jax.ShapeDtypeStruct((m, d * ATTN_W), F32)
    ospec = pl.BlockSpec((tq, ATTN_W), lambda c, b: (b, c))
    out, lse = pl.pallas_call(
        body, out_shape=(shp, shp), grid=(d, m // tq),
        in_specs=[pl.BlockSpec((tq, w3), lambda c, b: (b, c)),
                  pl.BlockSpec((ATTN_BLK, w3), lambda c, b: (jnp.maximum(b * n - 1, 0), c))],
        out_specs=(ospec, ospec), name=name,
        compiler_params=_cparams(("parallel", "parallel")))(qv, qv)
    return out, lse


def _attn_bwd(qv, ov, dov, lv, d, name, dep=None):
    m = qv.shape[0]
    w3 = 3 * ATTN_W
    _, tq, n = _attn_tiles(m * d, d)
    nb = m // ATTN_BLK
    scale = HEAD ** -0.5

    def body(x_ref, p_ref, nx_ref, o_ref, do_ref, l_ref, on_ref, don_ref, ln_ref, *rest):
        dq_ref, dk_ref, dv_ref = rest[-3:]
        b = pl.program_id(1)
        dk_ref[...] = jnp.zeros_like(dk_ref)
        dv_ref[...] = jnp.zeros_like(dv_ref)

        def grads(q, k, v, o, do, lse, valid):
            sc = jnp.where(valid, _dot_nt(q, k) * scale, -1e30)
            p = jnp.exp(sc - lse)
            delta = jnp.sum(do * o, axis=-1, keepdims=True)
            return p, p * (_dot_nt(do, v) - delta) * scale

        def block(i, first):
            r0 = 0 if first else pl.multiple_of(i * ATTN_BLK, ATTN_BLK)
            rows = pl.ds(r0, ATTN_BLK)
            valid = _band_mask(b * n + i)
            if not first:
                krows = pl.ds(pl.multiple_of(i * ATTN_BLK - ATTN_BLK, ATTN_BLK), 2 * ATTN_BLK)
            for h in range(ATTN_W // HEAD):
                qs, ks, vs = _head_cols(h)
                q = x_ref[rows, qs]
                do = do_ref[rows, qs]
                if first:
                    k = jnp.concatenate([p_ref[:, ks], x_ref[0:ATTN_BLK, ks]], axis=0)
                    v = jnp.concatenate([p_ref[:, vs], x_ref[0:ATTN_BLK, vs]], axis=0)
                else:
                    k = x_ref[krows, ks]
                    v = x_ref[krows, vs]
                p, ds = grads(q, k, v, o_ref[rows, qs], do, l_ref[rows, qs][:, 0:1], valid)
                dq_ref[rows, qs] = _dot(ds, k)
                if first:
                    dk_ref[0:ATTN_BLK, qs] += _dot_tn(ds[:, ATTN_BLK:], q)
                    dv_ref[0:ATTN_BLK, qs] += _dot_tn(p[:, ATTN_BLK:], do)
                else:
                    dk_ref[krows, qs] += _dot_tn(ds, q)
                    dv_ref[krows, qs] += _dot_tn(p, do)

        block(0, True)
        if n > 1:
            def loop(i, carry):
                block(i, False)
                return carry
            lax.fori_loop(1, n, loop, 0)

        last = slice((n - 1) * ATTN_BLK, n * ATTN_BLK)
        qi = lax.broadcasted_iota(jnp.int32, (ATTN_BLK, ATTN_BLK), 0)
        ki = lax.broadcasted_iota(jnp.int32, (ATTN_BLK, ATTN_BLK), 1)
        valid_next = (qi <= ki) & ((b + 1) * n < nb)
        for h in range(ATTN_W // HEAD):
            qs, ks, vs = _head_cols(h)
            q = nx_ref[:, qs]
            do = don_ref[:, qs]
            p, ds = grads(q, x_ref[last, ks], x_ref[last, vs], on_ref[:, qs], do, ln_ref[:, qs][:, 0:1],
                          valid_next)
            dk_ref[last, qs] += _dot_tn(ds, q)
            dv_ref[last, qs] += _dot_tn(p, do)

    nxt = lambda b: jnp.minimum((b + 1) * n, nb - 1)
    xs = pl.BlockSpec((tq, w3), lambda c, b: (b, c))
    xp = pl.BlockSpec((ATTN_BLK, w3), lambda c, b: (jnp.maximum(b * n - 1, 0), c))
    xn = pl.BlockSpec((ATTN_BLK, w3), lambda c, b: (nxt(b), c))
    a = pl.BlockSpec((tq, ATTN_W), lambda c, b: (b, c))
    an = pl.BlockSpec((ATTN_BLK, ATTN_W), lambda c, b: (nxt(b), c))
    shp = jax.ShapeDtypeStruct((m, d * ATTN_W), F32)
    dep_specs, dep_ops = _dep_args(dep)
    return pl.pallas_call(
        body, out_shape=(shp, shp, shp), grid=(d, m // tq),
        in_specs=[xs, xp, xn, a, a, a, an, an, an] + dep_specs, out_specs=(a, a, a), name=name,
        compiler_params=_cparams(("parallel", "parallel")))(qv, qv, qv, ov, dov, lv, ov, dov, lv, *dep_ops)


def _rms(x, g):
    ms = jnp.mean(x * x, axis=-1, keepdims=True)
    return x * lax.rsqrt(ms + RMS_EPS) * g


def _rms_bwd(x, g, dy):
    ms = jnp.mean(x * x, axis=-1, keepdims=True)
    r = lax.rsqrt(ms + RMS_EPS)
    dyg = dy * g
    dx = r * dyg - x * (r * r * r) * jnp.mean(x * dyg, axis=-1, keepdims=True)
    return dx, dy * x * r


def _mix_fwd(outs, lses, lru, s5, g, name):
    s = lru.shape[0]
    tm = 256

    def body(o1, o2, o3, l1, l2, l3, lru_ref, s5_ref, g_ref, mixed_ref, mixed_t_ref, ov1, ov2, ov3,
             lv1, lv2, lv3, so2, so3, sl2, sl3):
        for d, src, dst in ((DILATIONS[1], o2, so2), (DILATIONS[2], o3, so3),
                            (DILATIONS[1], l2, sl2), (DILATIONS[2], l3, sl3)):
            _to_tokens(src, dst, d, tm)
        a1, a2, a3 = l1[...], _token_value(sl2), _token_value(sl3)
        mx = jnp.maximum(jnp.maximum(a1, a2), a3)
        e1, e2, e3 = jnp.exp(a1 - mx), jnp.exp(a2 - mx), jnp.exp(a3 - mx)
        den = e1 + e2 + e3
        o = (e1 * o1[...] + e2 * _token_value(so2) + e3 * _token_value(so3)) / den
        lse = mx + jnp.log(den)
        ov1[...] = o
        lv1[...] = lse
        for j in range(ATTN_W // LANE):
            so2[j] = o[:, j * LANE:(j + 1) * LANE]
            sl2[j] = lse[:, j * LANE:(j + 1) * LANE]
        for d, o_dst, l_dst in ((DILATIONS[1], ov2, lv2), (DILATIONS[2], ov3, lv3)):
            _to_dilated(so2, o_dst, d, tm)
            _to_dilated(sl2, l_dst, d, tm)
        gg = g_ref[...]
        mixed = jnp.concatenate([_rms(o, gg[:, :ATTN_W]),
                                 _rms(lru_ref[...], gg[:, ATTN_W:ATTN_W + LRU_W]),
                                 _rms(s5_ref[...], gg[:, ATTN_W + LRU_W:])], axis=1)
        mixed_ref[...] = mixed.astype(BF16)
        mixed_t_ref[...] = mixed.T.astype(BF16)

    a = pl.BlockSpec((tm, ATTN_W), lambda i: (i, 0))
    s5s = pl.BlockSpec((tm, S5_W), lambda i: (i, 0))
    full = pl.BlockSpec((tm, D_MODEL), lambda i: (i, 0))
    vec = pl.BlockSpec((1, D_MODEL), lambda i: (0, 0))
    dil = [_dilated_spec(tm, d, ATTN_W) for d in DILATIONS]
    dshape = [jax.ShapeDtypeStruct((s // d, d * ATTN_W), F32) for d in DILATIONS]
    res = pl.pallas_call(
        body, out_shape=(jax.ShapeDtypeStruct((s, D_MODEL), BF16), jax.ShapeDtypeStruct((D_MODEL, s), BF16),
                         *dshape, *dshape),
        grid=(s // tm,), in_specs=dil + dil + [a, s5s, vec],
        out_specs=(full, pl.BlockSpec((D_MODEL, tm), lambda i: (0, i)), *dil, *dil),
        scratch_shapes=[_token_scratch(tm, ATTN_W)] * 4, name=name,
        compiler_params=_cparams(("parallel",)))(*outs, *lses, lru, s5, g)
    return res[0], res[1], res[2:5], res[5:8]


def _mix_bwd(dmixed, o, lru, s5, g, name):
    s = lru.shape[0]
    tm = 256

    def body(dm_ref, o_ref, lru_ref, s5_ref, g_ref, do_ref, do2_ref, do3_ref, dlru_ref, ds5_ref, dg_ref, stage):
        @pl.when(pl.program_id(0) == 0)
        def _():
            dg_ref[...] = jnp.zeros_like(dg_ref)
        gg = g_ref[...]
        dm = dm_ref[...]
        dx, dgr = _rms_bwd(o_ref[...], gg[:, :ATTN_W], dm[:, :ATTN_W])
        do_ref[...] = dx
        for j in range(ATTN_W // LANE):
            stage[j] = dx[:, j * LANE:(j + 1) * LANE]
        _to_dilated(stage, do2_ref, DILATIONS[1], tm)
        _to_dilated(stage, do3_ref, DILATIONS[2], tm)
        dg_ref[:, :ATTN_W] += jnp.sum(dgr, axis=0, keepdims=True)
        dx, dgr = _rms_bwd(lru_ref[...], gg[:, ATTN_W:ATTN_W + LRU_W], dm[:, ATTN_W:ATTN_W + LRU_W])
        dlru_ref[...] = dx
        dg_ref[:, ATTN_W:ATTN_W + LRU_W] += jnp.sum(dgr, axis=0, keepdims=True)
        dx, dgr = _rms_bwd(s5_ref[...], gg[:, ATTN_W + LRU_W:], dm[:, ATTN_W + LRU_W:])
        ds5_ref[...] = dx
        dg_ref[:, ATTN_W + LRU_W:] += jnp.sum(dgr, axis=0, keepdims=True)

    a = pl.BlockSpec((tm, ATTN_W), lambda i: (i, 0))
    s5s = pl.BlockSpec((tm, S5_W), lambda i: (i, 0))
    full = pl.BlockSpec((tm, D_MODEL), lambda i: (i, 0))
    vec = pl.BlockSpec((1, D_MODEL), lambda i: (0, 0))
    dil = [_dilated_spec(tm, d, ATTN_W) for d in DILATIONS]
    dshape = [jax.ShapeDtypeStruct((s // d, d * ATTN_W), F32) for d in DILATIONS]
    res = pl.pallas_call(
        body, out_shape=(*dshape, jax.ShapeDtypeStruct((s, LRU_W), F32),
                         jax.ShapeDtypeStruct((s, S5_W), F32), jax.ShapeDtypeStruct((1, D_MODEL), F32)),
        grid=(s // tm,), in_specs=[full, a, a, s5s, vec], out_specs=(*dil, a, s5s, vec),
        scratch_shapes=[_token_scratch(tm, ATTN_W)], name=name,
        compiler_params=_cparams(("arbitrary",)))(dmixed, o, lru, s5, g)
    return res[0:3], res[3], res[4], res[5]


def _lru_gate_math(xc, pre_r, pre_i, lam):
    r = _sigmoid(pre_r)
    i = _sigmoid(pre_i)
    log_a = -LRU_C * r * _softplus(-lam)
    a = jnp.exp(log_a)
    u = jnp.sqrt(-_expm1(2.0 * log_a)) * (i * xc)
    return a, u


def _lru_conv(x, prev8, cw, cb):
    y = cb + cw[LRU_CONV - 1:LRU_CONV, :] * x
    for k in range(LRU_CONV - 1):
        y = y + cw[k:k + 1, :] * _shift_down_prev(x, LRU_CONV - 1 - k, prev8)
    return y


def _lru_specs(s):
    xo = 3 * ATTN_W // LANE
    go = xo + LRU_W // LANE
    xr = pl.BlockSpec((s, LANE), lambda j: (0, xo + j))
    gt = pl.BlockSpec((s, LANE), lambda j: (0, go + j))
    cw = pl.BlockSpec((LRU_CONV, LANE), lambda j: (0, j))
    vec = pl.BlockSpec((1, LANE), lambda j: (0, j))
    wbd = pl.BlockSpec((LANE, LANE), lambda j: (j, j))
    col = pl.BlockSpec((s, LANE), lambda j: (0, j))
    return xr, gt, cw, vec, wbd, col


def _lru_fwd(proj, cw, cb, wr, br, wi, bi, lam, name):
    s = proj.shape[0]
    t = SCAN_T

    def body(xr_ref, gt_ref, cw_ref, cb_ref, wr_ref, br_ref, wi_ref, bi_ref, lam_ref, o_ref):
        cwv, cbv, lamv = cw_ref[...], cb_ref[...], lam_ref[...]
        wrv, wiv, brv, biv = wr_ref[...], wi_ref[...], br_ref[...], bi_ref[...]

        def chunk(c, carry):
            h_c, prev8 = carry
            rows = pl.ds(pl.multiple_of(c * t, t), t)
            x = xr_ref[rows, :]
            xc = _lru_conv(x, prev8, cwv, cbv)
            a, u = _lru_gate_math(xc, _dot(xc, wrv) + brv, _dot(xc, wiv) + biv, lamv)
            h = _scan_chunk(a, u, h_c)
            o_ref[rows, :] = h * _gelu(gt_ref[rows, :])
            return h[t - 1:t, :], x[t - 8:t, :]

        lax.fori_loop(0, s // t, chunk, (jnp.zeros((1, LANE), F32), jnp.zeros((8, LANE), F32)))

    xr, gt, cws, vec, wbd, col = _lru_specs(s)
    return pl.pallas_call(
        body, out_shape=jax.ShapeDtypeStruct((s, LRU_W), F32), grid=(LRU_W // LANE,),
        in_specs=[xr, gt, cws, vec, wbd, vec, wbd, vec, vec], out_specs=col, name=name,
        compiler_params=_cparams(("parallel",)))(proj, proj, cw, cb, wr, br, wi, bi, lam)


def _lru_bwd(proj, dout, cw, cb, wr, br, wi, bi, lam, name):
    s = proj.shape[0]
    t = SCAN_T
    nc = s // t

    def body(xr_ref, gt_ref, do_ref, cw_ref, cb_ref, wr_ref, br_ref, wi_ref, bi_ref, lam_ref,
             dxr_ref, dgt_ref, dcw_ref, dcb_ref, dwr_ref, dbr_ref, dwi_ref, dbi_ref, dlam_ref,
             xc_s, a_s, h_s):
        cwv, cbv, lamv = cw_ref[...], cb_ref[...], lam_ref[...]
        wrv, wiv, brv, biv = wr_ref[...], wi_ref[...], br_ref[...], bi_ref[...]

        def fchunk(c, carry):
            h_c, prev8 = carry
            rows = pl.ds(pl.multiple_of(c * t, t), t)
            x = xr_ref[rows, :]
            xc = _lru_conv(x, prev8, cwv, cbv)
            a, u = _lru_gate_math(xc, _dot(xc, wrv) + brv, _dot(xc, wiv) + biv, lamv)
            h = _scan_chunk(a, u, h_c)
            xc_s[rows, :] = xc
            a_s[rows, :] = a
            h_s[rows, :] = h
            return h[t - 1:t, :], x[t - 8:t, :]

        lax.fori_loop(0, nc, fchunk, (jnp.zeros((1, LANE), F32), jnp.zeros((8, LANE), F32)))

        z1 = jnp.zeros((1, LANE), F32)
        zw = jnp.zeros((LANE, LANE), F32)

        def bchunk(ci, carry):
            g_next, a_next, dxc_next8, dcw, dcb, dwr, dbr, dwi, dbi, dlam = carry
            c = nc - 1 - ci
            t0 = pl.multiple_of(c * t, t)
            rows = pl.ds(t0, t)
            before = pl.ds(pl.multiple_of(jnp.maximum(t0 - 8, 0), 8), 8)
            has_prev = (c > 0).astype(F32)
            x, gt, do = xr_ref[rows, :], gt_ref[rows, :], do_ref[rows, :]
            xc, a, h = xc_s[rows, :], a_s[rows, :], h_s[rows, :]
            prev8_x = xr_ref[before, :] * has_prev
            prev8_h = h_s[before, :] * has_prev
            dgt_ref[rows, :] = do * h * _gelu_grad(gt)
            dh = do * _gelu(gt)
            a_plus = _shift_up_next(a, 1, jnp.broadcast_to(a_next, (8, LANE)))
            g = _scan_chunk(a_plus, dh, g_next, reverse=True)
            da = g * _shift_down_prev(h, 1, prev8_h)
            pre_r = _dot(xc, wrv) + brv
            pre_i = _dot(xc, wiv) + biv
            _, vjp = jax.vjp(_lru_gate_math, xc, pre_r, pre_i, lamv)
            dxc, dpre_r, dpre_i, dlam_c = vjp((da, g))
            dxc = dxc + _dot_nt(dpre_r, wrv) + _dot_nt(dpre_i, wiv)
            dx = cwv[LRU_CONV - 1:LRU_CONV, :] * dxc
            dcw_rows = [None] * LRU_CONV
            dcw_rows[LRU_CONV - 1] = jnp.sum(dxc * x, axis=0, keepdims=True)
            for k in range(LRU_CONV - 1):
                sh = LRU_CONV - 1 - k
                dx = dx + cwv[k:k + 1, :] * _shift_up_next(dxc, sh, dxc_next8)
                dcw_rows[k] = jnp.sum(dxc * _shift_down_prev(x, sh, prev8_x), axis=0, keepdims=True)
            dxr_ref[rows, :] = dx
            return (g[0:1, :], a[0:1, :], dxc[0:8, :],
                    dcw + jnp.concatenate(dcw_rows, axis=0),
                    dcb + jnp.sum(dxc, axis=0, keepdims=True),
                    dwr + _dot_tn(xc, dpre_r), dbr + jnp.sum(dpre_r, axis=0, keepdims=True),
                    dwi + _dot_tn(xc, dpre_i), dbi + jnp.sum(dpre_i, axis=0, keepdims=True),
                    dlam + dlam_c)

        init = (z1, z1, jnp.zeros((8, LANE), F32), jnp.zeros((LRU_CONV, LANE), F32), z1, zw, z1, zw, z1, z1)
        res = lax.fori_loop(0, nc, bchunk, init)
        dcw_ref[...] = res[3]
        dcb_ref[...] = res[4]
        dwr_ref[...] = res[5]
        dbr_ref[...] = res[6]
        dwi_ref[...] = res[7]
        dbi_ref[...] = res[8]
        dlam_ref[...] = res[9]

    xr, gt, cws, vec, wbd, col = _lru_specs(s)
    vshape = jax.ShapeDtypeStruct((1, LRU_W), F32)
    wshape = jax.ShapeDtypeStruct((LRU_W, LRU_W), F32)
    return pl.pallas_call(
        body,
        out_shape=(jax.ShapeDtypeStruct((s, LRU_W), F32), jax.ShapeDtypeStruct((s, LRU_W), F32),
                   jax.ShapeDtypeStruct((LRU_CONV, LRU_W), F32), vshape, wshape, vshape, wshape, vshape, vshape),
        grid=(LRU_W // LANE,),
        in_specs=[xr, gt, col, cws, vec, wbd, vec, wbd, vec, vec],
        out_specs=(col, col, cws, vec, wbd, vec, wbd, vec, vec),
        scratch_shapes=[pltpu.VMEM((s, LANE), F32)] * 3, name=name,
        compiler_params=_cparams(("parallel",)))(proj, proj, dout, cw, cb, wr, br, wi, bi, lam)


def _s5_disc_math(a_re, a_im, log_step, bt_re, bt_im):
    step = jnp.exp(log_step)
    dt_re, dt_im = step * a_re, step * a_im
    mag = jnp.exp(dt_re)
    ab_re, ab_im = mag * jnp.cos(dt_im), mag * jnp.sin(dt_im)
    z_re, z_im = ab_re - 1.0, ab_im
    den = a_re * a_re + a_im * a_im
    f_re = (z_re * a_re + z_im * a_im) / den
    f_im = (z_im * a_re - z_re * a_im) / den
    bb_re = f_re * bt_re - f_im * bt_im
    bb_im = f_re * bt_im + f_im * bt_re
    return ab_re, ab_im, bb_re, bb_im


def _s5_disc_fwd(a_re, a_im, log_step, bt_re, bt_im, name):
    def body(ar, ai, ls, br, bi, o1, o2, o3, o4):
        r = _s5_disc_math(ar[...], ai[...], ls[...], br[...], bi[...])
        o1[...], o2[...], o3[...], o4[...] = r

    shp = jax.ShapeDtypeStruct(a_re.shape, F32)
    return pl.pallas_call(body, out_shape=(shp,) * 4, name=name)(a_re, a_im, log_step, bt_re, bt_im)


def _s5_disc_bwd(a_re, a_im, log_step, bt_re, bt_im, cts, name):
    def body(ar, ai, ls, br, bi, c1, c2, c3, c4, o1, o2, o3, o4, o5):
        _, vjp = jax.vjp(_s5_disc_math, ar[...], ai[...], ls[...], br[...], bi[...])
        r = vjp((c1[...], c2[...], c3[...], c4[...]))
        o1[...], o2[...], o3[...], o4[...], o5[...] = r

    shp = jax.ShapeDtypeStruct(a_re.shape, F32)
    return pl.pallas_call(body, out_shape=(shp,) * 5, name=name)(a_re, a_im, log_step, bt_re, bt_im, *cts)


def _s5_u_specs(s):
    uo = (3 * ATTN_W + 2 * LRU_W) // LANE
    return (pl.BlockSpec((s, LANE), lambda j: (0, uo)), pl.BlockSpec((s, LANE), lambda j: (0, uo + 1)))


def _s5_scan_fwd(proj, b_re, b_im, lam_re, lam_im, c_re, c_im, name):
    s = proj.shape[0]
    t = SCAN_T

    def body(u0_ref, u1_ref, bre_ref, bim_ref, lre_ref, lim_ref, cre_ref, cim_ref, xre_ref, xim_ref, y_ref):
        @pl.when(pl.program_id(0) == 0)
        def _():
            y_ref[...] = jnp.zeros_like(y_ref)
        lr, li = lre_ref[...], lim_ref[...]
        consts = _cscan_consts(lr, li, False)
        bre, bim, cre, cim = bre_ref[...], bim_ref[...], cre_ref[...], cim_ref[...]

        def chunk(c, carry):
            cr, ci = carry
            rows = pl.ds(pl.multiple_of(c * t, t), t)
            u = jnp.concatenate([u0_ref[rows, :], u1_ref[rows, :]], axis=1).astype(BF16)
            xr, xi = _cscan_chunk(_dot(u, bre), _dot(u, bim), consts, (cr, ci))
            xre_ref[rows, :] = xr
            xim_ref[rows, :] = xi
            y_ref[rows, :] += _dot(xr, cre) - _dot(xi, cim)
            return xr[t - 1:t, :], xi[t - 1:t, :]

        z = jnp.zeros((1, S5_BLK), F32)
        lax.fori_loop(0, s // t, chunk, (z, z))

    u0, u1 = _s5_u_specs(s)
    bsp = pl.BlockSpec((S5_W, S5_BLK), lambda j: (0, j))
    csp = pl.BlockSpec((S5_BLK, S5_W), lambda j: (j, 0))
    vec = pl.BlockSpec((1, S5_BLK), lambda j: (0, j))
    xsp = pl.BlockSpec((s, S5_BLK), lambda j: (0, j))
    ysp = pl.BlockSpec((s, S5_W), lambda j: (0, 0))
    xshape = jax.ShapeDtypeStruct((s, S5_STATES), F32)
    return pl.pallas_call(
        body, out_shape=(xshape, xshape, jax.ShapeDtypeStruct((s, S5_W), F32)),
        grid=(S5_STATES // S5_BLK,), in_specs=[u0, u1, bsp, bsp, vec, vec, csp, csp],
        out_specs=(xsp, xsp, ysp), name=name,
        compiler_params=_cparams(("arbitrary",)))(proj, proj, b_re, b_im, lam_re, lam_im, c_re, c_im)


def _s5_scan_bwd(proj, dy, du_init, x_re, x_im, b_re, b_im, lam_re, lam_im, c_re, c_im, name):
    s = proj.shape[0]
    t = SCAN_T
    nc = s // t

    def body(u0_ref, u1_ref, dy_ref, dui_ref, xre_ref, xim_ref, bre_ref, bim_ref, lre_ref, lim_ref,
             cre_ref, cim_ref, du_ref, dlr_ref, dli_ref, dbr_ref, dbi_ref, dcr_ref, dci_ref):
        @pl.when(pl.program_id(0) == 0)
        def _():
            du_ref[...] = dui_ref[...]
        mr, mi = lre_ref[...], -lim_ref[...]
        consts = _cscan_consts(mr, mi, True)
        bre, bim, cre, cim = bre_ref[...], bim_ref[...], cre_ref[...], cim_ref[...]
        dbr_ref[...] = jnp.zeros_like(dbr_ref)
        dbi_ref[...] = jnp.zeros_like(dbi_ref)
        dcr_ref[...] = jnp.zeros_like(dcr_ref)
        dci_ref[...] = jnp.zeros_like(dci_ref)

        def chunk(ci_, carry):
            gnr, gni, dlr, dli = carry
            c = nc - 1 - ci_
            t0 = pl.multiple_of(c * t, t)
            rows = pl.ds(t0, t)
            before = pl.ds(pl.multiple_of(jnp.maximum(t0 - 8, 0), 8), 8)
            has_prev = (c > 0).astype(F32)
            dyc = dy_ref[rows, :].astype(BF16)
            u = jnp.concatenate([u0_ref[rows, :], u1_ref[rows, :]], axis=1).astype(BF16)
            gr, gi = _cscan_chunk(_dot_nt(dyc, cre), -_dot_nt(dyc, cim), consts, (gnr, gni), reverse=True)
            xr, xi = xre_ref[rows, :], xim_ref[rows, :]
            xpr = _shift_down_prev(xr, 1, xre_ref[before, :] * has_prev)
            xpi = _shift_down_prev(xi, 1, xim_ref[before, :] * has_prev)
            dlr = dlr + jnp.sum(gr * xpr + gi * xpi, axis=0, keepdims=True)
            dli = dli + jnp.sum(gi * xpr - gr * xpi, axis=0, keepdims=True)
            du_ref[rows, :] += _dot_nt(gr, bre) + _dot_nt(gi, bim)
            dbr_ref[...] += _dot_tn(u, gr)
            dbi_ref[...] += _dot_tn(u, gi)
            dcr_ref[...] += _dot_tn(xr, dyc)
            dci_ref[...] -= _dot_tn(xi, dyc)
            return gr[0:1, :], gi[0:1, :], dlr, dli

        z = jnp.zeros((1, S5_BLK), F32)
        res = lax.fori_loop(0, nc, chunk, (z, z, z, z))
        dlr_ref[...] = res[2]
        dli_ref[...] = res[3]

    u0, u1 = _s5_u_specs(s)
    bsp = pl.BlockSpec((S5_W, S5_BLK), lambda j: (0, j))
    csp = pl.BlockSpec((S5_BLK, S5_W), lambda j: (j, 0))
    vec = pl.BlockSpec((1, S5_BLK), lambda j: (0, j))
    xsp = pl.BlockSpec((s, S5_BLK), lambda j: (0, j))
    ysp = pl.BlockSpec((s, S5_W), lambda j: (0, 0))
    return pl.pallas_call(
        body,
        out_shape=(jax.ShapeDtypeStruct((s, S5_W), F32),
                   jax.ShapeDtypeStruct((1, S5_STATES), F32), jax.ShapeDtypeStruct((1, S5_STATES), F32),
                   jax.ShapeDtypeStruct((S5_W, S5_STATES), F32), jax.ShapeDtypeStruct((S5_W, S5_STATES), F32),
                   jax.ShapeDtypeStruct((S5_STATES, S5_W), F32), jax.ShapeDtypeStruct((S5_STATES, S5_W), F32)),
        grid=(S5_STATES // S5_BLK,),
        in_specs=[u0, u1, ysp, ysp, xsp, xsp, bsp, bsp, vec, vec, csp, csp],
        out_specs=(ysp, vec, vec, bsp, bsp, csp, csp), name=name,
        compiler_params=_cparams(("arbitrary",)))(
            proj, proj, dy, du_init, x_re, x_im, b_re, b_im, lam_re, lam_im, c_re, c_im)


def _s5_out_fwd(proj, y_acc, dvec, w_glu, b_glu, name):
    s = proj.shape[0]
    tm = 512
    uo = (3 * ATTN_W + 2 * LRU_W) // LANE

    def body(u0_ref, u1_ref, y_ref, d_ref, w_ref, b_ref, o_ref, yp_ref):
        u = jnp.concatenate([u0_ref[...], u1_ref[...]], axis=1)
        y = y_ref[...] + d_ref[...] * u
        yp_ref[...] = y
        yg = _gelu(y)
        o_ref[...] = yg * _sigmoid(_dot(yg, w_ref[...]) + b_ref[...])

    u0 = pl.BlockSpec((tm, LANE), lambda i: (i, uo))
    u1 = pl.BlockSpec((tm, LANE), lambda i: (i, uo + 1))
    row = pl.BlockSpec((tm, S5_W), lambda i: (i, 0))
    vec = pl.BlockSpec((1, S5_W), lambda i: (0, 0))
    wsp = pl.BlockSpec((S5_W, S5_W), lambda i: (0, 0))
    shp = jax.ShapeDtypeStruct((s, S5_W), F32)
    return pl.pallas_call(
        body, out_shape=(shp, shp), grid=(s // tm,), in_specs=[u0, u1, row, vec, wsp, vec],
        out_specs=(row, row), name=name,
        compiler_params=_cparams(("parallel",)))(proj, proj, y_acc, dvec, w_glu, b_glu)


def _s5_out_bwd(proj, y_pre, dout, dvec, w_glu, b_glu, name):
    s = proj.shape[0]
    tm = 512
    uo = (3 * ATTN_W + 2 * LRU_W) // LANE

    def body(u0_ref, u1_ref, y_ref, do_ref, d_ref, w_ref, b_ref, dy_ref, dud_ref, dd_ref, dw_ref, db_ref):
        @pl.when(pl.program_id(0) == 0)
        def _():
            dd_ref[...] = jnp.zeros_like(dd_ref)
            dw_ref[...] = jnp.zeros_like(dw_ref)
            db_ref[...] = jnp.zeros_like(db_ref)
        u = jnp.concatenate([u0_ref[...], u1_ref[...]], axis=1)
        y = y_ref[...]
        do = do_ref[...]
        yg = _gelu(y)
        sg = _sigmoid(_dot(yg, w_ref[...]) + b_ref[...])
        dz = do * yg * sg * (1.0 - sg)
        dyg = do * sg + _dot_nt(dz, w_ref[...])
        dy = dyg * _gelu_grad(y)
        dy_ref[...] = dy
        dud_ref[...] = d_ref[...] * dy
        dd_ref[...] += jnp.sum(dy * u, axis=0, keepdims=True)
        dw_ref[...] += _dot_tn(yg, dz)
        db_ref[...] += jnp.sum(dz, axis=0, keepdims=True)

    u0 = pl.BlockSpec((tm, LANE), lambda i: (i, uo))
    u1 = pl.BlockSpec((tm, LANE), lambda i: (i, uo + 1))
    row = pl.BlockSpec((tm, S5_W), lambda i: (i, 0))
    vec = pl.BlockSpec((1, S5_W), lambda i: (0, 0))
    wsp = pl.BlockSpec((S5_W, S5_W), lambda i: (0, 0))
    shp = jax.ShapeDtypeStruct((s, S5_W), F32)
    vshape = jax.ShapeDtypeStruct((1, S5_W), F32)
    return pl.pallas_call(
        body, out_shape=(shp, shp, vshape, jax.ShapeDtypeStruct((S5_W, S5_W), F32), vshape),
        grid=(s // tm,), in_specs=[u0, u1, row, row, vec, wsp, vec],
        out_specs=(row, row, vec, wsp, vec), name=name,
        compiler_params=_cparams(("arbitrary",)))(proj, proj, y_pre, dout, dvec, w_glu, b_glu)


def _ffn_conv(x, prev8, cw, cb):
    y = cb + cw[FFN_CONV - 1:FFN_CONV, :] * x
    for k in range(FFN_CONV - 1):
        y = y + cw[k:k + 1, :] * _shift_down_prev(x, FFN_CONV - 1 - k, prev8)
    return y


def _ffn_up_act(h, wg, cw, cb, name, dep=None):
    s, d = h.shape
    tm = 512
    tb = 2 * FFN_CB
    nt = D_FF // FFN_CB

    def body(h_ref, wgate_ref, wval_ref, cw_ref, cb_ref, *rest):
        up_ref, o_ref, ot_ref, carry = rest[-4:]
        t = pl.program_id(1)

        @pl.when(pl.program_id(0) == 0)
        def _():
            carry[t] = jnp.zeros((8, tb), F32)
        hb = h_ref[...].astype(BF16)
        x = jnp.concatenate([_dot(hb, wgate_ref[...]), _dot(hb, wval_ref[...])], axis=1)
        up_ref[...] = x
        y = _ffn_conv(x, carry[t], cw_ref[...], cb_ref[...])
        carry[t] = x[tm - 8:tm, :]
        act = _gelu(y[:, :FFN_CB]) * y[:, FFN_CB:]
        o_ref[...] = act.astype(BF16)
        ot_ref[...] = act.T.astype(BF16)

    dep_specs, dep_ops = _dep_args(dep)
    return pl.pallas_call(
        body, out_shape=(jax.ShapeDtypeStruct((s, 2 * D_FF), F32), jax.ShapeDtypeStruct((s, D_FF), BF16),
                         jax.ShapeDtypeStruct((D_FF, s), BF16)),
        grid=(s // tm, nt),
        in_specs=[pl.BlockSpec((tm, d), lambda i, t: (i, 0)),
                  pl.BlockSpec((None, d, FFN_CB), lambda i, t: (t, 0, 0)),
                  pl.BlockSpec((None, d, FFN_CB), lambda i, t: (t + nt, 0, 0)),
                  pl.BlockSpec((FFN_CONV, tb), lambda i, t: (0, t)),
                  pl.BlockSpec((1, tb), lambda i, t: (0, t))] + dep_specs,
        out_specs=(pl.BlockSpec((tm, tb), lambda i, t: (i, t)), pl.BlockSpec((tm, FFN_CB), lambda i, t: (i, t)),
                   pl.BlockSpec((FFN_CB, tm), lambda i, t: (t, i))),
        scratch_shapes=[pltpu.VMEM((nt, 8, tb), F32)], name=name,
        compiler_params=_cparams(("arbitrary", "arbitrary")))(h, wg, wg, cw, cb, *dep_ops)


def _ffn_bwd(up, dr, w_down, wg, cw, cb, name):
    s = up.shape[0]
    d = dr.shape[1]
    tm = 256
    tb = 2 * FFN_CB
    nr = s // tm
    nt = D_FF // FFN_CB

    def body(x_ref, p_ref, dr_ref, wd_ref, wgate_ref, wval_ref, cw_ref, cb_ref,
             dup_ref, dh_ref, dcw_ref, dcb_ref, carry):
        i, t = pl.program_id(0), pl.program_id(1)

        @pl.when(i == 0)
        def _():
            carry[t] = jnp.zeros((8, tb), F32)

        @pl.when(t == 0)
        def _():
            dh_ref[...] = ALPHA * dr_ref[...]
        prev8 = p_ref[...] * (i < nr - 1).astype(F32)
        cwv = cw_ref[...]
        x = x_ref[...]
        dact = _dot_nt(dr_ref[...], wd_ref[...])
        shifted = [_shift_down_prev(x, FFN_CONV - 1 - k, prev8) for k in range(FFN_CONV - 1)]
        y = cb_ref[...] + cwv[FFN_CONV - 1:FFN_CONV, :] * x
        for k in range(FFN_CONV - 1):
            y = y + cwv[k:k + 1, :] * shifted[k]
        gate, val = y[:, :FFN_CB], y[:, FFN_CB:]
        dy = jnp.concatenate([dact * val * _gelu_grad(gate), dact * _gelu(gate)], axis=1)
        next8 = carry[t]
        carry[t] = dy[0:8, :]
        dx = cwv[FFN_CONV - 1:FFN_CONV, :] * dy
        dcw_rows = [None] * FFN_CONV
        dcw_rows[FFN_CONV - 1] = jnp.sum(dy * x, axis=0, keepdims=True)
        for k in range(FFN_CONV - 1):
            dx = dx + cwv[k:k + 1, :] * _shift_up_next(dy, FFN_CONV - 1 - k, next8)
            dcw_rows[k] = jnp.sum(dy * shifted[k], axis=0, keepdims=True)
        dup = dx.astype(BF16)
        dup_ref[...] = dup
        dh_ref[...] += _dot_nt(dup[:, :FFN_CB], wgate_ref[...]) + _dot_nt(dup[:, FFN_CB:], wval_ref[...])
        dcw_ref[...] = jnp.concatenate(dcw_rows, axis=0)
        dcb_ref[...] = jnp.sum(dy, axis=0, keepdims=True)

    row = lambda i: nr - 1 - i
    return pl.pallas_call(
        body, out_shape=(jax.ShapeDtypeStruct((s, 2 * D_FF), BF16), jax.ShapeDtypeStruct((s, d), F32),
                         jax.ShapeDtypeStruct((nr, FFN_CONV, 2 * D_FF), F32),
                         jax.ShapeDtypeStruct((nr, 1, 2 * D_FF), F32)),
        grid=(nr, nt),
        in_specs=[pl.BlockSpec((tm, tb), lambda i, t: (row(i), t)),
                  pl.BlockSpec((8, tb), lambda i, t: (jnp.maximum(row(i) * (tm // 8) - 1, 0), t)),
                  pl.BlockSpec((tm, d), lambda i, t: (row(i), 0)),
                  pl.BlockSpec((FFN_CB, d), lambda i, t: (t, 0)),
                  pl.BlockSpec((None, d, FFN_CB), lambda i, t: (t, 0, 0)),
                  pl.BlockSpec((None, d, FFN_CB), lambda i, t: (t + nt, 0, 0)),
                  pl.BlockSpec((FFN_CONV, tb), lambda i, t: (0, t)),
                  pl.BlockSpec((1, tb), lambda i, t: (0, t))],
        out_specs=(pl.BlockSpec((tm, tb), lambda i, t: (row(i), t)),
                   pl.BlockSpec((tm, d), lambda i, t: (row(i), 0)),
                   pl.BlockSpec((None, FFN_CONV, tb), lambda i, t: (row(i), 0, t)),
                   pl.BlockSpec((None, 1, tb), lambda i, t: (row(i), 0, t))),
        scratch_shapes=[pltpu.VMEM((nt, 8, tb), F32)], name=name,
        compiler_params=_cparams(("arbitrary", "arbitrary")))(up, up, dr, w_down, wg, wg, cw, cb)


def _sum_partials(ld_ref):
    gg = ld_ref[0].astype(F32)
    for k in range(1, N_DEV):
        gg = gg + ld_ref[k].astype(F32)
    return gg


def _adam_update(w, g, m, v):
    mn = ADAM_B1 * m + (1.0 - ADAM_B1) * g
    vn = ADAM_B2 * v + (1.0 - ADAM_B2) * (g * g)
    m_hat = mn / (1.0 - ADAM_B1 ** ADAM_STEP)
    v_hat = vn / (1.0 - ADAM_B2 ** ADAM_STEP)
    return -ADAM_LR * (m_hat / (jnp.sqrt(v_hat) + ADAM_EPS) + ADAM_WD * w), mn, vn


def _adamw_many(landed, ws, ms, vs, name):
    n, nl = len(ws), len(landed)

    def body(*refs):
        ld = refs[:nl * n]
        w_refs, m_refs, v_refs = (refs[(nl + k) * n:(nl + k + 1) * n] for k in range(3))
        outs = refs[(nl + 3) * n:]
        for i in range(n):
            for l in range(nl):
                one = slice(l, l + 1)
                gg = _sum_partials(ld[l * n + i])
                outs[i][one] = gg
                outs[n + i][one], outs[2 * n + i][one], outs[3 * n + i][one] = _adam_update(
                    w_refs[i][one], gg, m_refs[i][one], v_refs[i][one])

    vm = pl.BlockSpec(memory_space=pltpu.VMEM)
    shapes = [jax.ShapeDtypeStruct(w.shape, F32) for w in ws] * 4
    res = pl.pallas_call(
        body, out_shape=tuple(shapes), in_specs=[vm] * ((nl + 3) * n), out_specs=tuple([vm] * (4 * n)),
        name=name, compiler_params=_cparams())(*[a for layer in landed for a in layer], *ws, *ms, *vs)
    return res[:n], res[n:2 * n], res[2 * n:3 * n], res[3 * n:]


def _adamw_sum(landed, w, m, v, layer, prev, name):
    _, r, c = landed.shape
    nl = w.shape[0]
    tm = 8
    for cand in (512, 256, 128, 64, 32, 16):
        if r % cand == 0 and N_DEV * cand * c * 4 <= 4 * 1024 * 1024:
            tm = cand
            break

    def body(*refs):
        ld_ref, w_ref, m_ref, v_ref = refs[:4]
        g_ref, d_ref, mo_ref, vo_ref = refs[-4:]
        gg = _sum_partials(ld_ref)
        g_ref[...] = gg
        d_ref[...], mo_ref[...], vo_ref[...] = _adam_update(w_ref[...], gg, m_ref[...], v_ref[...])

    blk = pl.BlockSpec((None, tm, c), lambda i: (layer, i, 0))
    in_specs = [pl.BlockSpec((N_DEV, tm, c), lambda i: (0, i, 0)), blk, blk, blk]
    args = [landed, w, m, v]
    aliases = {}
    if prev is not None:
        in_specs += [pl.BlockSpec(memory_space=pl.ANY)] * 4
        args += list(prev)
        aliases = {4 + k: k for k in range(4)}
    shp = jax.ShapeDtypeStruct((nl, r, c), F32)
    return pl.pallas_call(
        body, out_shape=(shp,) * 4, grid=(r // tm,), in_specs=in_specs, out_specs=(blk,) * 4,
        input_output_aliases=aliases, name=name, compiler_params=_cparams(("parallel",)))(*args)


def _all_gather(shards, name):
    na = len(shards)

    def body(*refs):
        x_refs, out_refs = refs[:na], refs[na:2 * na]
        send_sems, recv_sems, local_sems = refs[2 * na:]
        x, y, c = lax.axis_index("x"), lax.axis_index("y"), lax.axis_index("c")
        me, sibling = (x, y, c), (x, y, 1 - c)
        chips = [(1 - x, y), (x, 1 - y), (1 - x, 1 - y)]

        def copy(a, k, block, to, src=None):
            dst = out_refs[a].at[4 * block[0] + 2 * block[1] + block[2]]
            return pltpu.make_async_remote_copy(
                src_ref=dst if src is None else src, dst_ref=dst,
                send_sem=send_sems.at[7 * a + k], recv_sem=recv_sems.at[7 * a + k],
                device_id=to, device_id_type=pl.DeviceIdType.MESH)

        mine, first, passed = [], [], []
        for a in range(na):
            cp = pltpu.make_async_copy(x_refs[a], out_refs[a].at[4 * x + 2 * y + c], local_sems.at[a])
            cp.start()
            mine.append(cp)
            cps = [copy(a, 0, me, sibling, src=x_refs[a])]
            cps += [copy(a, 1 + j, me, (*chip, c), src=x_refs[a]) for j, chip in enumerate(chips)]
            for cp in cps:
                cp.start()
            first += cps
        for j, chip in enumerate(chips):
            for a in range(na):
                copy(a, 1 + j, (*chip, c), me).wait_recv()
                cp = copy(a, 4 + j, (*chip, c), sibling)
                cp.start()
                passed.append(cp)
        for a in range(na):
            copy(a, 0, sibling, me).wait_recv()
            for j, chip in enumerate(chips):
                copy(a, 4 + j, (*chip, 1 - c), me).wait_recv()
        for cp in first + passed:
            cp.wait_send()
        for cp in mine:
            cp.wait()

    anyspec = pl.BlockSpec(memory_space=pl.ANY)
    return pl.pallas_call(
        body, out_shape=tuple(jax.ShapeDtypeStruct((N_DEV,) + t.shape, t.dtype) for t in shards),
        in_specs=[anyspec] * na, out_specs=tuple([anyspec] * na),
        scratch_shapes=[pltpu.SemaphoreType.DMA((7 * na,)), pltpu.SemaphoreType.DMA((7 * na,)),
                        pltpu.SemaphoreType.DMA((na,))],
        name=name)(*shards)


_HBM = pl.BlockSpec(memory_space=pltpu.HBM)
_SEM = pl.BlockSpec(memory_space=pltpu.SEMAPHORE)
_EFFECT = pltpu.SideEffectType.DATAFLOW_SIDE_EFFECTING


def _exchange_copies(src_refs, land_refs, send_sems, recv_sems, local_sems, gather):
    x, y, c = lax.axis_index("x"), lax.axis_index("y"), lax.axis_index("c")
    me = 4 * x + 2 * y + c
    per_array = send_sems.shape[0] > N_DEV - 1
    local, remote = [], []
    for a, (src, land) in enumerate(zip(src_refs, land_refs)):
        local.append(pltpu.make_async_copy(src if gather else src.at[me], land.at[me],
                                           local_sems.at[a if per_array else 0]))
    for k in range(1, N_DEV):
        px = x ^ ((k >> 2) & 1)
        py = y ^ ((k >> 1) & 1)
        pc = c ^ (k & 1)
        for a, (src, land) in enumerate(zip(src_refs, land_refs)):
            remote.append(pltpu.make_async_remote_copy(
                src_ref=src if gather else src.at[4 * px + 2 * py + pc], dst_ref=land.at[me],
                send_sem=send_sems.at[(7 * a if per_array else 0) + k - 1],
                recv_sem=recv_sems.at[(7 * a if per_array else 0) + k - 1],
                device_id=(px, py, pc), device_id_type=pl.DeviceIdType.MESH))
    return local, remote


def _exchange_start(srcs, gather, name, dep=None):
    na = len(srcs)
    ns = na if na <= 4 else 1
    lands = [lax.empty(((N_DEV,) + t.shape) if gather else t.shape, t.dtype) for t in srcs]

    def body(*refs):
        src_refs, land_refs = refs[:na], refs[na:2 * na]
        nin = 2 * na + (0 if dep is None else 1)
        send_sems, recv_sems, local_sems = refs[nin:nin + 3]
        token = refs[-1]
        local, remote = _exchange_copies(src_refs, land_refs, send_sems, recv_sems, local_sems, gather)
        for cp in local + remote:
            cp.start()
        token[...] = jnp.zeros_like(token)

    dep_specs, dep_ops = _dep_args(dep)
    hbm = lambda t: pltpu.HBM(t.shape, t.dtype)
    out = pl.pallas_call(
        body, name=name,
        out_shape=(pltpu.SemaphoreType.DMA((7 * ns,)), pltpu.SemaphoreType.DMA((7 * ns,)),
                   pltpu.SemaphoreType.DMA((ns,)), *[hbm(t) for t in srcs], *[hbm(t) for t in lands],
                   jax.ShapeDtypeStruct((8, LANE), F32)),
        in_specs=[_HBM] * (2 * na) + dep_specs,
        out_specs=(_SEM, _SEM, _SEM, *[_HBM] * (2 * na), pl.BlockSpec(memory_space=pltpu.VMEM)),
        input_output_aliases={i: 3 + i for i in range(2 * na)},
        compiler_params=pltpu.CompilerParams(has_side_effects=_EFFECT),
    )(*[pltpu.with_memory_space_constraint(t, pltpu.HBM) for t in srcs + lands], *dep_ops)
    return (out[:3], out[3:3 + na], out[3 + na:3 + 2 * na]), out[-1]


def _exchange_wait(handle, gather, after, name):
    sems, srcs, lands = handle
    na = len(srcs)

    def body(*refs):
        src_refs, land_refs = refs[:na], refs[na:2 * na]
        send_sems, recv_sems, local_sems = refs[2 * na:2 * na + 3]
        local, remote = _exchange_copies(src_refs, land_refs, send_sems, recv_sems, local_sems, gather)
        for cp in remote:
            cp.wait_send()
            cp.wait_recv()
        for cp in local:
            cp.wait()

    hbm = lambda t: pltpu.HBM(t.shape, t.dtype)
    out = pl.pallas_call(
        body, name=name, out_shape=(*[hbm(t) for t in srcs], *[hbm(t) for t in lands]),
        in_specs=[_HBM] * (2 * na) + [_SEM] * 3 + [pl.BlockSpec(memory_space=pl.ANY)],
        out_specs=tuple([_HBM] * (2 * na)), input_output_aliases={i: i for i in range(2 * na)},
        compiler_params=pltpu.CompilerParams(has_side_effects=_EFFECT),
    )(*srcs, *lands, *sems, after)
    return out[na:]


def _block_diag(w):
    h, a, b = w.shape
    eye = jnp.eye(h, dtype=w.dtype)
    return (w[:, :, None, :] * eye[:, None, :, None]).reshape(h * a, h * b)


def _block_diag_extract(m, h):
    a, b = m.shape[0] // h, m.shape[1] // h
    return jnp.stack([m[i * a:(i + 1) * a, i * b:(i + 1) * b] for i in range(h)], axis=0)


def _block_diag_take(m, h):
    a, b = m.shape[0] // h, m.shape[1] // h
    eye = jnp.eye(h, dtype=m.dtype)
    return (m.reshape(h, a, h, b) * eye[:, None, :, None]).sum(axis=2)


def _ffn_interleave(w):
    lead = w.shape[:-1]
    nb = D_FF // FFN_CB
    return jnp.swapaxes(w.reshape(*lead, 2, nb, FFN_CB), -3, -2).reshape(*lead, 2 * D_FF)


def _ffn_deinterleave(w):
    lead = w.shape[:-1]
    nb = D_FF // FFN_CB
    return jnp.swapaxes(w.reshape(*lead, nb, 2, FFN_CB), -3, -2).reshape(*lead, 2 * D_FF)


def _gather_full(gathered, axis):
    shape = list(gathered.shape[1:])
    shape[axis] *= N_DEV
    return jnp.moveaxis(gathered, 0, axis).reshape(shape)


def _scatter_blocks(full, axis):
    shape = list(full.shape)
    shape[axis:axis + 1] = [N_DEV, shape[axis] // N_DEV]
    return jnp.moveaxis(full.reshape(shape), axis, 0)


def _pad_to(flat, mult):
    pad = (-flat.shape[-1]) % mult
    if pad:
        flat = jnp.concatenate([flat, jnp.zeros(flat.shape[:-1] + (pad,), flat.dtype)], axis=-1)
    return flat


def _layer_fwd(h_in, h_in_t, w, cos, sin, l, dep, get_ffn):
    tag = "l%d_" % l
    proj = _mm_nn(h_in, w['w_in'], 512, D_IN, tag + "proj", dep=dep)
    qkv = _rope_fwd(proj, cos, sin, tag + "rope")
    outs, lses = [], []
    for d, qv in zip(DILATIONS, qkv):
        o, ls = _attn_fwd(qv, d, tag + "attn_d%d" % d)
        outs.append(o)
        lses.append(ls)
    lru = _lru_fwd(proj, w['lru_conv_w'], w['lru_conv_b'], w['lru_wr'], w['lru_br'], w['lru_wi'],
                   w['lru_bi'], w['lru_lambda'], tag + "lru")
    x_re, x_im, y_acc = _s5_scan_fwd(proj, w['s5_bb_re'], w['s5_bb_im'], w['s5_lam_re'], w['s5_lam_im'],
                                     w['s5_cc_re'], w['s5_cc_im'], tag + "s5_scan")
    s5, y_pre = _s5_out_fwd(proj, y_acc, w['s5_d'], w['s5_w_glu'], w['s5_b_glu'], tag + "s5_out")
    mixed, mixed_t, attn_o, attn_lse = _mix_fwd(outs, lses, lru, s5, w['mix_norm_g'], tag + "mix")
    mixo = _mm_nn(mixed, w['w_out'], 512, D_MODEL, tag + "out_proj")
    r1, h1, h1_t = _ln_fwd(h_in, mixo, w['ln1_g'], w['ln1_b'], tag + "ln1")
    w['w_up_g'], w['w_down'], ffn_dep = get_ffn(l, h1)
    up, act, act_t = _ffn_up_act(h1, w['w_up_g'], w['ffn_conv_w'], w['ffn_conv_b'], tag + "up_act", dep=ffn_dep)
    ffn = _mm_nn(act, w['w_down'], 512, D_MODEL, tag + "down_proj")
    r2, h2, *rest = _ln_fwd(h1, ffn, w['ln2_g'], w['ln2_b'], tag + "ln2", transposed=l + 1 < DEPTH)
    h2_t = rest[0] if rest else None
    saved = dict(h_in_t=h_in_t, proj=proj, qkv=qkv, lru=lru, x_re=x_re, x_im=x_im, y_pre=y_pre, s5=s5,
                 mixed_t=mixed_t, attn_o=attn_o, attn_lse=attn_lse, r1=r1, h1_t=h1_t, up=up, act_t=act_t, r2=r2)
    return h2, h2_t, saved


def _layer_bwd_ffn(dh2, sv, w, l, dep=None):
    tag = "l%d_" % l
    g = {}
    dr2, g['ln2_g'], g['ln2_b'] = _ln_bwd(sv['r2'], dh2, w['ln2_g'], tag + "ln2_bwd", dep=dep)
    g['w_down'] = _mm_dw(sv['act_t'], dr2, 1024, D_MODEL, 1024, tag + "down_dw")
    dup, dh1, dcw_parts, dcb_parts = _ffn_bwd(sv['up'], dr2, w['w_down'], w['w_up_g'], w['ffn_conv_w'],
                                              w['ffn_conv_b'], tag + "ffn_bwd")
    g['ffn_conv_w'] = dcw_parts.sum(axis=0)
    g['ffn_conv_b'] = dcb_parts.sum(axis=0)
    g['w_up_g'] = _mm_up_dw(sv['h1_t'], dup, tag + "up_dw")
    return dh1, g


def _layer_bwd_mix(dh1, sv, w, cos, sin, l, dep, g_ffn, after_out_grad, after_small_grads, after_in_grad):
    tag = "l%d_" % l
    g = {}
    dr1, g['ln1_g'], g['ln1_b'] = _ln_bwd(sv['r1'], dh1, w['ln1_g'], tag + "ln1_bwd", dep=dep)
    g['w_out'] = _mm_dw(sv['mixed_t'], dr1, 1024, D_MODEL, 1024, tag + "out_dw")
    dmixed = _mm_nt(dr1, w['w_out'], 512, D_MODEL, tag + "out_dx", dep=after_out_grad(l, g['w_out']))
    d_o, dlru, ds5, g['mix_norm_g'] = _mix_bwd(dmixed, sv['attn_o'][0], sv['lru'], sv['s5'], w['mix_norm_g'],
                                               tag + "mix_bwd")
    dy, dud, g['s5_d'], g['s5_w_glu'], g['s5_b_glu'] = _s5_out_bwd(
        sv['proj'], sv['y_pre'], ds5, w['s5_d'], w['s5_w_glu'], w['s5_b_glu'], tag + "s5_out_bwd")
    du, g['s5_lam_re'], g['s5_lam_im'], g['s5_bb_re'], g['s5_bb_im'], g['s5_cc_re'], g['s5_cc_im'] = \
        _s5_scan_bwd(sv['proj'], dy, dud, sv['x_re'], sv['x_im'], w['s5_bb_re'], w['s5_bb_im'],
                     w['s5_lam_re'], w['s5_lam_im'], w['s5_cc_re'], w['s5_cc_im'], tag + "s5_scan_bwd")
    (dxr, dgate, g['lru_conv_w'], g['lru_conv_b'], g['lru_wr'], g['lru_br'], g['lru_wi'], g['lru_bi'],
     g['lru_lambda']) = _lru_bwd(sv['proj'], dlru, w['lru_conv_w'], w['lru_conv_b'], w['lru_wr'],
                                 w['lru_br'], w['lru_wi'], w['lru_bi'], w['lru_lambda'], tag + "lru_bwd")
    token = after_small_grads(l, _finish_layer_grads({**g_ffn, **g}, w, l))
    dqkv = [_attn_bwd(sv['qkv'][b], sv['attn_o'][b], d_o[b], sv['attn_lse'][b], d, tag + "attn_bwd_d%d" % d,
                      dep=token if b == 0 else None)
            for b, d in enumerate(DILATIONS)]
    dproj = _dproj_assemble(dqkv, dxr, dgate, du, cos, sin, tag + "dproj")
    g_in = _mm_dw(sv['h_in_t'], dproj, 1024, D_IN, 1024, tag + "in_dw")
    return _mm_nt(dproj, w['w_in'], 512, D_MODEL, tag + "in_dx", add=dr1, add_scale=ALPHA,
                  dep=after_in_grad(l, g_in))


def _s5_rep(a):
    return jnp.repeat(a, S5_C, axis=0)


def _prepare_layer(p, l):
    w = {}
    for n in ('w_in', 'w_out', 's5_w_glu'):
        w[n] = p[n].astype(BF16)
    w['ffn_conv_w'] = _ffn_interleave(p['ffn_conv_w'])
    w['ffn_conv_b'] = _ffn_interleave(p['ffn_conv_b'])[None, :]
    w['lru_conv_w'] = p['lru_conv_w']
    for n in ('lru_conv_b', 'lru_br', 'lru_bi', 'lru_lambda', 's5_b_glu', 'mix_norm_g',
              'ln1_g', 'ln1_b', 'ln2_g', 'ln2_b'):
        w[n] = p[n][None, :]
    w['lru_wr'] = _block_diag(p['lru_wr']).astype(BF16)
    w['lru_wi'] = _block_diag(p['lru_wi']).astype(BF16)
    w['s5_d'] = p['s5_d'].reshape(1, S5_W)
    disc_in = (_s5_rep(p['s5_a_re']), _s5_rep(p['s5_a_im']),
               _s5_rep(jnp.broadcast_to(p['s5_log_step'][:, None], (S5_G, S5_P))),
               jnp.swapaxes(p['s5_b_re'], 1, 2).reshape(S5_W, S5_P),
               jnp.swapaxes(p['s5_b_im'], 1, 2).reshape(S5_W, S5_P))
    ab_re, ab_im, bb_re, bb_im = _s5_disc_fwd(*disc_in, "l%d_s5_disc" % l)
    w['s5_disc_in'] = disc_in
    w['s5_lam_re'] = ab_re.reshape(S5_G, S5_C, S5_P)[:, 0, :].reshape(1, S5_STATES)
    w['s5_lam_im'] = ab_im.reshape(S5_G, S5_C, S5_P)[:, 0, :].reshape(1, S5_STATES)
    w['s5_bb_re'] = _block_diag(bb_re.reshape(S5_G, S5_C, S5_P)).astype(BF16)
    w['s5_bb_im'] = _block_diag(bb_im.reshape(S5_G, S5_C, S5_P)).astype(BF16)
    w['s5_cc_re'] = _block_diag(jnp.swapaxes(p['s5_c_re'], 1, 2)).astype(BF16)
    w['s5_cc_im'] = _block_diag(jnp.swapaxes(p['s5_c_im'], 1, 2)).astype(BF16)
    return w


def _finish_layer_grads(g, w, l):
    out = {}
    for n in ('s5_w_glu', 'lru_conv_w'):
        out[n] = g[n]
    out['ffn_conv_w'] = _ffn_deinterleave(g['ffn_conv_w'])
    out['ffn_conv_b'] = _ffn_deinterleave(g['ffn_conv_b'])[0]
    for n in ('lru_conv_b', 'lru_br', 'lru_bi', 'lru_lambda', 's5_b_glu', 'mix_norm_g',
              'ln1_g', 'ln1_b', 'ln2_g', 'ln2_b'):
        out[n] = g[n][0]
    out['lru_wr'] = _block_diag_extract(g['lru_wr'], LRU_W // HEAD)
    out['lru_wi'] = _block_diag_extract(g['lru_wi'], LRU_W // HEAD)
    out['s5_d'] = g['s5_d'].reshape(S5_G, S5_C)
    out['s5_c_re'] = jnp.swapaxes(_block_diag_take(g['s5_cc_re'], S5_G), 1, 2)
    out['s5_c_im'] = jnp.swapaxes(_block_diag_take(g['s5_cc_im'], S5_G), 1, 2)
    rep = lambda v: _s5_rep(v.reshape(S5_G, S5_P)) * (1.0 / S5_C)
    cts = (rep(g['s5_lam_re']), rep(g['s5_lam_im']),
           _block_diag_take(g['s5_bb_re'], S5_G).reshape(S5_W, S5_P),
           _block_diag_take(g['s5_bb_im'], S5_G).reshape(S5_W, S5_P))
    da_re, da_im, dls, dbt_re, dbt_im = _s5_disc_bwd(*w['s5_disc_in'], cts, "l%d_s5_disc_bwd" % l)
    out['s5_a_re'] = da_re.reshape(S5_G, S5_C, S5_P).sum(axis=1)
    out['s5_a_im'] = da_im.reshape(S5_G, S5_C, S5_P).sum(axis=1)
    out['s5_log_step'] = dls.reshape(S5_G, S5_C * S5_P).sum(axis=1)
    out['s5_b_re'] = jnp.swapaxes(dbt_re.reshape(S5_G, S5_C, S5_P), 1, 2)
    out['s5_b_im'] = jnp.swapaxes(dbt_im.reshape(S5_G, S5_C, S5_P), 1, 2)
    return out


def _run_step(x, target, get_layer, get_ffn, after_ffn_grads, after_out_grad, after_small_grads, after_in_grad):
    cos, sin = _rope_tables(x.shape[0])
    h, h_t = x, _transpose_bf16(x, "x_transpose")
    ws, saved = [], []
    for l in range(DEPTH):
        p, dep = get_layer(l, h)
        ws.append(_prepare_layer(p, l))
        h, h_t, sv = _layer_fwd(h, h_t, ws[l], cos, sin, l, dep, get_ffn)
        saved.append(sv)
    dh, loss_vec = _loss_head(h, target)
    dep = None
    for l in reversed(range(DEPTH)):
        dh1, g = _layer_bwd_ffn(dh, saved[l], ws[l], l, dep)
        dep = after_ffn_grads(l, g)
        dh = _layer_bwd_mix(dh1, saved[l], ws[l], cos, sin, l, dep, g, after_out_grad, after_small_grads,
                            after_in_grad)
        dep = None
    return loss_vec[0, 0], dh


def _local_step(x, target, layers):
    grads = [{} for _ in range(DEPTH)]

    def ffn(l, h1):
        return layers[l]['w_up_g'].astype(BF16), layers[l]['w_down'].astype(BF16), None

    def keep_ffn(l, g):
        grads[l].update(w_up_g=g['w_up_g'], w_down=g['w_down'])

    def keep_small(l, g):
        grads[l].update(g)

    loss, dx = _run_step(x, target, lambda l, h: (layers[l], None), ffn, keep_ffn,
                         lambda l, g: grads[l].update(w_out=g), keep_small, lambda l, g: grads[l].update(w_in=g))
    return loss, dx, grads


def kernel(x, w_in, lru_conv_w, lru_conv_b, lru_wr, lru_br, lru_wi, lru_bi, lru_lambda, s5_a_re, s5_a_im, s5_b_re, s5_b_im, s5_c_re, s5_c_im, s5_d, s5_log_step, s5_w_glu, s5_b_glu, mix_norm_g, w_out, ln1_g, ln1_b, w_up, ffn_conv_w, ffn_conv_b, w_down, ln2_g, ln2_b, loss_target, m_w_in, m_lru_conv_w, m_lru_conv_b, m_lru_wr, m_lru_br, m_lru_wi, m_lru_bi, m_lru_lambda, m_s5_a_re, m_s5_a_im, m_s5_b_re, m_s5_b_im, m_s5_c_re, m_s5_c_im, m_s5_d, m_s5_log_step, m_s5_w_glu, m_s5_b_glu, m_mix_norm_g, m_w_out, m_ln1_g, m_ln1_b, m_w_up, m_ffn_conv_w, m_ffn_conv_b, m_w_down, m_ln2_g, m_ln2_b, v_w_in, v_lru_conv_w, v_lru_conv_b, v_lru_wr, v_lru_br, v_lru_wi, v_lru_bi, v_lru_lambda, v_s5_a_re, v_s5_a_im, v_s5_b_re, v_s5_b_im, v_s5_c_re, v_s5_c_im, v_s5_d, v_s5_log_step, v_s5_w_glu, v_s5_b_glu, v_mix_norm_g, v_w_out, v_ln1_g, v_ln1_b, v_w_up, v_ffn_conv_w, v_ffn_conv_b, v_w_down, v_ln2_g, v_ln2_b):
    args = locals()
    wl = {n: args[n] for n in WEIGHTS}
    ml = {n: args['m_' + n] for n in WEIGHTS}
    vl = {n: args['v_' + n] for n in WEIGHTS}

    small_sizes = [int(wl[n].size) for n in SMALL_SHARDED]
    small_flat = _pad_to(jnp.concatenate([wl[n].reshape(-1) for n in SMALL_SHARDED]), 8 * 1024)
    small_all, = _all_gather([small_flat.reshape(-1, 1024)], "gather_small")
    small_all = small_all.reshape(N_DEV, -1)
    small_full, off = {}, 0
    for n, sz in zip(SMALL_SHARDED, small_sizes):
        small_full[n] = _gather_full(small_all[:, off:off + sz].reshape((N_DEV,) + wl[n].shape), SHARD_AXIS[n])
        off += sz
    def mixer_params(l, gathered):
        g_in, g_out = gathered
        p = {n: wl[n][l] for n in REPLICATED}
        p.update({n: small_full[n][l] for n in SMALL_SHARDED})
        p['w_in'] = _gather_full(g_in, 1)
        p['w_out'] = g_out.reshape(D_MODEL, D_MODEL)
        return p

    mix_names, ffn_names = ('w_in', 'w_out'), ('w_up', 'w_down')
    shards = lambda names, l: [wl[n][l].astype(BF16) for n in names]
    mix0 = _all_gather(shards(mix_names, 0), "gather_mix_l0")
    gathers = {}
    gathers[0, 'ffn'], ffn0_token = _exchange_start(shards(ffn_names, 0), True, "gather_ffn_l0_start", dep=mix0[0])
    def get_layer(l, h):
        if l == 0:
            return mixer_params(0, mix0), ffn0_token
        return mixer_params(1, _exchange_wait(gathers[1, 'mix'], True, h, "gather_mix_l1_wait")), None

    def get_ffn(l, h1):
        g_up, g_down = _exchange_wait(gathers[l, 'ffn'], True, h1, "gather_ffn_l%d_wait" % l)
        token = None
        if l == 0:
            gathers[1, 'mix'], token = _exchange_start(shards(mix_names, 1), True, "gather_mix_l1_start", dep=g_up)
            gathers[1, 'ffn'], token = _exchange_start(shards(ffn_names, 1), True, "gather_ffn_l1_start", dep=token)
        return g_up, g_down.reshape(D_FF, D_MODEL), token

    scatters = {}

    def after_ffn_grads(l, g):
        send = [g['w_up_g'], g['w_down'].reshape(N_DEV, D_FF // N_DEV, D_MODEL)]
        scatters[l, 'ffn'], token = _exchange_start(send, False, "scatter_ffn_l%d_start" % l)
        return token

    def after_out_grad(l, g_out):
        send = [g_out.reshape(N_DEV, D_MODEL // N_DEV, D_MODEL)]
        scatters[l, 'out'], token = _exchange_start(send, False, "scatter_out_l%d_start" % l)
        return token

    def after_in_grad(l, g_in):
        send = _scatter_blocks(g_in, 1)
        if l == 0:
            send = send.astype(BF16)
        scatters[l, 'in'], token = _exchange_start([send], False, "scatter_in_l%d_start" % l)
        return token

    def after_small_grads(l, g):
        rep = [g[n][None] for n in REPLICATED]
        shd = [_scatter_blocks(g[n], SHARD_AXIS[n] - 1)[:, None] for n in SMALL_SHARDED]
        scatters[l, 'rep'], token = _exchange_start(rep, True, "gather_rep_grads_l%d_start" % l)
        scatters[l, 'small'], token = _exchange_start(shd, False, "scatter_small_l%d_start" % l, dep=token)
        return token

    loss_local, grad_x = _run_step(x[0], loss_target[0], get_layer, get_ffn, after_ffn_grads, after_out_grad,
                                   after_small_grads, after_in_grad)
    loss = lax.psum(loss_local, AXES)

    results = {}
    big_prev = {n: None for n in BIG}

    def finish_big(l, part, names, after):
        landed = _exchange_wait(scatters[l, part], False, after, "scatter_%s_l%d_wait" % (part, l))
        for n, ld in zip(names, landed):
            big_prev[n] = _adamw_sum(ld, wl[n], ml[n], vl[n], l, big_prev[n], "adamw_%s_l%d" % (n, l))

    for l, part, names in ((1, 'ffn', ffn_names), (1, 'out', ('w_out',)), (1, 'in', ('w_in',)),
                           (0, 'ffn', ffn_names), (0, 'out', ('w_out',))):
        finish_big(l, part, names, grad_x)

    kinds = ('grad', 'delta', 'm', 'v')
    landed = [dict(zip(REPLICATED + SMALL_SHARDED,
                       list(_exchange_wait(scatters[l, 'rep'], True, grad_x, "gather_rep_grads_l%d_wait" % l)) +
                       list(_exchange_wait(scatters[l, 'small'], False, grad_x, "scatter_small_l%d_wait" % l))))
              for l in range(DEPTH)]
    matrices = ['lru_wr', 'lru_wi', 's5_a_re', 's5_a_im', 's5_c_re', 's5_c_im', 's5_d']
    widest = ['s5_b_re', 's5_b_im']
    vectors = [n for n in REPLICATED + SMALL_SHARDED if n not in matrices + widest]
    last = None
    for tag, names in (("vectors", vectors), ("matrices", matrices), ("s5_b", widest)):
        res = _adamw_many([[landed[l][n] for n in names] for l in range(DEPTH)], [wl[n] for n in names],
                          [ml[n] for n in names], [vl[n] for n in names], "adamw_" + tag)
        for kind, arrs in zip(kinds, res):
            for n, a in zip(names, arrs):
                results[kind, n] = a
        last = res[0][0]
    finish_big(0, 'in', ('w_in',), last)
    for n in BIG:
        results['grad', n], results['delta', n], results['m', n], results['v', n] = big_prev[n]

    out = [loss, grad_x[None]]
    for kind in kinds:
        out.extend(results[kind, n] for n in WEIGHTS)
    return tuple(out)
```

```python
import functools
import math

import jax
import jax.numpy as jnp
from jax import lax
from jax.experimental import pallas as pl
from jax.experimental.pallas import tpu as pltpu

F32 = jnp.float32
BF16 = jnp.bfloat16

N_DEV = 8
DEPTH = 2
D_MODEL = 1024
ATTN_W = 384
LRU_W = 384
S5_W = 256
D_IN = 2176
D_FF = 3072
HEAD = 64
ATTN_BLK = 128
ATTN_TILE = 1024
DILATIONS = (1, 4, 16)
S5_G = 16
S5_P = 64
S5_C = 16
S5_STATES = S5_G * S5_P
LRU_C = 8.0
LRU_CONV = 4
FFN_CONV = 3
ROPE_THETA = 10000.0
ALPHA = (2 * DEPTH) ** 0.25
LN_EPS = 1e-5
RMS_EPS = 1e-6
ADAM_LR, ADAM_B1, ADAM_B2, ADAM_EPS, ADAM_WD, ADAM_STEP = 0.001, 0.9, 0.999, 1e-8, 0.01, 10

LANE = 128
SCAN_T = 256
S5_BLK = 256
FFN_CB = 2 * D_FF // N_DEV
VMEM_LIMIT = 56 * 1024 * 1024

AXES = ("x", "y", "c")

WEIGHTS = ['w_in', 'lru_conv_w', 'lru_conv_b', 'lru_wr', 'lru_br', 'lru_wi', 'lru_bi', 'lru_lambda',
           's5_a_re', 's5_a_im', 's5_b_re', 's5_b_im', 's5_c_re', 's5_c_im', 's5_d', 's5_log_step',
           's5_w_glu', 's5_b_glu', 'mix_norm_g', 'w_out', 'ln1_g', 'ln1_b', 'w_up', 'ffn_conv_w',
           'ffn_conv_b', 'w_down', 'ln2_g', 'ln2_b']
SHARD_AXIS = {'w_in': 2, 'lru_conv_w': 2, 's5_w_glu': 1, 'w_out': 1, 'w_up': 2, 'ffn_conv_w': 2, 'w_down': 1}
BIG = ['w_in', 'w_out', 'w_up', 'w_down']
SMALL_SHARDED = ['lru_conv_w', 'ffn_conv_w', 's5_w_glu']
REPLICATED = [n for n in WEIGHTS if n not in SHARD_AXIS]


def _cparams(sem=None):
    return pltpu.CompilerParams(dimension_semantics=sem, vmem_limit_bytes=VMEM_LIMIT)


def _ffn_dev(jb):
    return jb // 2 + (N_DEV // 2) * (jb % 2)


def _gelu(x):
    c = math.sqrt(2.0 / math.pi)
    t = jnp.tanh(c * (x + 0.044715 * (x * x * x)))
    return 0.5 * x * (1.0 + t)


def _gelu_grad(x):
    c = math.sqrt(2.0 / math.pi)
    x2 = x * x
    t = jnp.tanh(c * (x + 0.044715 * (x2 * x)))
    return 0.5 * (1.0 + t) + 0.5 * x * (1.0 - t * t) * (c * (1.0 + 3.0 * 0.044715 * x2))


def _sigmoid(x):
    return 1.0 / (1.0 + jnp.exp(-x))


def _log1p(x):
    u = 1.0 + x
    d = u - 1.0
    return jnp.where(d == 0.0, x, jnp.log(u) * (x / jnp.where(d == 0.0, 1.0, d)))


def _softplus(x):
    return jnp.maximum(x, 0.0) + _log1p(jnp.exp(-jnp.abs(x)))


def _expm1(x):
    return jnp.tanh(0.5 * x) * (jnp.exp(x) + 1.0)


def _dot(a, b):
    return jnp.dot(a.astype(BF16), b.astype(BF16), preferred_element_type=F32)


def _dot_nt(a, b):
    return lax.dot_general(a.astype(BF16), b.astype(BF16), (((1,), (1,)), ((), ())),
                           preferred_element_type=F32)


def _dot_tn(a, b):
    return lax.dot_general(a.astype(BF16), b.astype(BF16), (((0,), (0,)), ((), ())),
                           preferred_element_type=F32)


def _rows(shape):
    return lax.broadcasted_iota(jnp.int32, shape, 0)


def _shift_down_prev(x, s, prev8):
    if s == 0:
        return x
    t, l = x.shape
    r = pltpu.roll(x, s, axis=0)
    pr = pltpu.roll(prev8, s, axis=0)
    pad = jnp.concatenate([pr, jnp.zeros((t - 8, l), x.dtype)], axis=0)
    return jnp.where(_rows(x.shape) < s, pad, r)


def _shift_up_next(x, s, next8):
    if s == 0:
        return x
    t, l = x.shape
    r = pltpu.roll(x, t - s, axis=0)
    nx = pltpu.roll(next8, 8 - s, axis=0)
    pad = jnp.concatenate([jnp.zeros((t - 8, l), x.dtype), nx], axis=0)
    return jnp.where(_rows(x.shape) >= t - s, pad, r)


SUB = 8


def _tile_shift(x, s, fill, reverse):
    t = x.shape[0]
    pos = _rows(x.shape) & (SUB - 1)
    if reverse:
        return jnp.where(pos < SUB - s, pltpu.roll(x, t - s, axis=0), fill)
    return jnp.where(pos >= s, pltpu.roll(x, s, axis=0), fill)


def _scan_chunk(a, x, carry, reverse=False):
    s = 1
    while s < SUB:
        x = x + a * _tile_shift(x, s, 0.0, reverse)
        a = a * _tile_shift(a, s, 1.0, reverse)
        s *= 2
    nv = x.shape[0] // SUB
    out = [None] * nv
    for v in (reversed(range(nv)) if reverse else range(nv)):
        rows = slice(v * SUB, (v + 1) * SUB)
        out[v] = x[rows, :] + a[rows, :] * carry
        carry = out[v][0:1, :] if reverse else out[v][SUB - 1:SUB, :]
    return jnp.concatenate(out, axis=0)


def _cmul(ar, ai, br, bi):
    return ar * br - ai * bi, ar * bi + ai * br


def _cscan_consts(lr, li, reverse):
    pows = [(lr, li)]
    for _ in range(2):
        pows.append(_cmul(*pows[-1], *pows[-1]))
    rows = [(lr, li)]
    for _ in range(SUB - 1):
        rows.append(_cmul(*rows[-1], lr, li))
    if reverse:
        rows = rows[::-1]
    return pows, (jnp.concatenate([r for r, _ in rows], axis=0), jnp.concatenate([i for _, i in rows], axis=0))


def _cscan_chunk(xr, xi, consts, carry, reverse=False):
    pows, (p8r, p8i) = consts
    s = 1
    for pr, pi in pows:
        sr = _tile_shift(xr, s, 0.0, reverse)
        si = _tile_shift(xi, s, 0.0, reverse)
        xr, xi = xr + pr * sr - pi * si, xi + pr * si + pi * sr
        s *= 2
    nv = xr.shape[0] // SUB
    out_r, out_i = [None] * nv, [None] * nv
    cr, ci = carry
    for v in (reversed(range(nv)) if reverse else range(nv)):
        rows = slice(v * SUB, (v + 1) * SUB)
        out_r[v] = xr[rows, :] + p8r * cr - p8i * ci
        out_i[v] = xi[rows, :] + p8r * ci + p8i * cr
        edge = slice(0, 1) if reverse else slice(SUB - 1, SUB)
        cr, ci = out_r[v][edge, :], out_i[v][edge, :]
    return jnp.concatenate(out_r, axis=0), jnp.concatenate(out_i, axis=0)


def _dep_args(dep):
    return ([], []) if dep is None else ([pl.BlockSpec(memory_space=pl.ANY)], [dep])


def _mm_nn(a, b, tm, tn, name, out_dtype=F32, dep=None):
    m, k = a.shape
    n = b.shape[1]

    def body(a_ref, b_ref, *rest):
        o_ref = rest[-1]
        o_ref[...] = _dot(a_ref[...], b_ref[...]).astype(out_dtype)

    dep_specs, dep_ops = _dep_args(dep)
    return pl.pallas_call(
        body, out_shape=jax.ShapeDtypeStruct((m, n), out_dtype), grid=(n // tn, m // tm),
        in_specs=[pl.BlockSpec((tm, k), lambda j, i: (i, 0)),
                  pl.BlockSpec((k, tn), lambda j, i: (0, j))] + dep_specs,
        out_specs=pl.BlockSpec((tm, tn), lambda j, i: (i, j)), name=name,
        compiler_params=_cparams(("parallel", "parallel")))(a, b, *dep_ops)


def _mm_nt(a, w, tm, tn, name, add=None, add_scale=1.0, dep=None):
    m, k = a.shape
    n = w.shape[0]

    def body(a_ref, w_ref, *rest):
        o_ref = rest[-1]
        if add is None:
            o_ref[...] = _dot_nt(a_ref[...], w_ref[...])
        else:
            o_ref[...] = _dot_nt(a_ref[...], w_ref[...]) + add_scale * rest[0][...]

    in_specs = [pl.BlockSpec((tm, k), lambda j, i: (i, 0)), pl.BlockSpec((tn, k), lambda j, i: (j, 0))]
    args = [a, w]
    if add is not None:
        in_specs.append(pl.BlockSpec((tm, tn), lambda j, i: (i, j)))
        args.append(add)
    dep_specs, dep_ops = _dep_args(dep)
    return pl.pallas_call(
        body, out_shape=jax.ShapeDtypeStruct((m, n), F32), grid=(n // tn, m // tm),
        in_specs=in_specs + dep_specs, out_specs=pl.BlockSpec((tm, tn), lambda j, i: (i, j)), name=name,
        compiler_params=_cparams(("parallel", "parallel")))(*args, *dep_ops)


def _mm_dw(at, b, tm, tn, ts, name):
    m, s = at.shape
    n = b.shape[1]

    def body(a_ref, b_ref, o_ref):
        @pl.when(pl.program_id(2) == 0)
        def _():
            o_ref[...] = jnp.zeros_like(o_ref)
        o_ref[...] += _dot(a_ref[...], b_ref[...])

    return pl.pallas_call(
        body, out_shape=jax.ShapeDtypeStruct((m, n), F32), grid=(m // tm, n // tn, s // ts),
        in_specs=[pl.BlockSpec((tm, ts), lambda i, j, k: (i, k)), pl.BlockSpec((ts, tn), lambda i, j, k: (k, j))],
        out_specs=pl.BlockSpec((tm, tn), lambda i, j, k: (i, j)), name=name,
        compiler_params=_cparams(("parallel", "parallel", "arbitrary")))(at, b)


def _transpose_bf16(x, name):
    s, d = x.shape
    tm = 512

    def body(x_ref, o_ref):
        o_ref[...] = x_ref[...].T.astype(BF16)

    return pl.pallas_call(
        body, out_shape=jax.ShapeDtypeStruct((d, s), BF16), grid=(s // tm,),
        in_specs=[pl.BlockSpec((tm, d), lambda i: (i, 0))], out_specs=pl.BlockSpec((d, tm), lambda i: (0, i)),
        name=name, compiler_params=_cparams(("parallel",)))(x)


def _mm_up_dw(ht, dup, name):
    d, s = ht.shape

    def body(a_ref, b_ref, o_ref):
        o_ref[...] = _dot(a_ref[...], b_ref[...])

    return pl.pallas_call(
        body, out_shape=jax.ShapeDtypeStruct((N_DEV, d, FFN_CB), F32), grid=(N_DEV,),
        in_specs=[pl.BlockSpec((d, s), lambda j: (0, 0)), pl.BlockSpec((s, FFN_CB), lambda j: (0, j))],
        out_specs=pl.BlockSpec((None, d, FFN_CB), lambda j: (_ffn_dev(j), 0, 0)), name=name,
        compiler_params=_cparams(("parallel",)))(ht, dup)


def _proj_ln(a, w, resid, g, bias, name, transposed=True, target=None):
    s, k = a.shape
    d = w.shape[1]
    tm = 512

    def body(a_ref, w_ref, x_ref, g_ref, bias_ref, *rest):
        r = ALPHA * x_ref[...] + _dot(a_ref[...], w_ref[...])
        mu = jnp.mean(r, axis=-1, keepdims=True)
        xc = r - mu
        var = jnp.mean(xc * xc, axis=-1, keepdims=True)
        h = xc * lax.rsqrt(var + LN_EPS) * g_ref[...] + bias_ref[...]
        if target is None:
            r_ref, h_ref = rest[0], rest[1]
            h_ref[...] = h
            if transposed:
                rest[2][...] = h.T.astype(BF16)
        else:
            t_ref, r_ref, dy_ref, l_ref = rest

            @pl.when(pl.program_id(0) == 0)
            def _():
                l_ref[...] = jnp.zeros_like(l_ref)
            e = h - t_ref[...]
            dy_ref[...] = e * (1.0 / d)
            part = 0.5 * jnp.sum(jnp.mean(e * e, axis=-1, keepdims=True), axis=0, keepdims=True)
            l_ref[...] += jnp.broadcast_to(part, l_ref.shape)
        r_ref[...] = r

    row = pl.BlockSpec((tm, d), lambda i: (i, 0))
    vec = pl.BlockSpec((1, d), lambda i: (0, 0))
    in_specs = [pl.BlockSpec((tm, k), lambda i: (i, 0)), pl.BlockSpec((k, d), lambda i: (0, 0)), row, vec, vec]
    args = [a, w, resid, g, bias]
    shapes = [jax.ShapeDtypeStruct((s, d), F32), jax.ShapeDtypeStruct((s, d), F32)]
    specs = [row, row]
    if target is not None:
        in_specs.append(row)
        args.append(target)
        shapes.append(jax.ShapeDtypeStruct((1, LANE), F32))
        specs.append(pl.BlockSpec((1, LANE), lambda i: (0, 0)))
    elif transposed:
        shapes.append(jax.ShapeDtypeStruct((d, s), BF16))
        specs.append(pl.BlockSpec((d, tm), lambda i: (0, i)))
    return pl.pallas_call(
        body, out_shape=tuple(shapes), grid=(s // tm,), in_specs=in_specs, out_specs=tuple(specs), name=name,
        compiler_params=_cparams(("arbitrary",) if target is not None else ("parallel",)))(*args)


def _ln_bwd(r, dh, g, name, dep=None):
    s, d = r.shape
    tm = 512

    def body(r_ref, dh_ref, g_ref, *rest):
        dr_ref, dg_ref, db_ref = rest[-3:]

        @pl.when(pl.program_id(0) == 0)
        def _():
            dg_ref[...] = jnp.zeros_like(dg_ref)
            db_ref[...] = jnp.zeros_like(db_ref)
        rr = r_ref[...]
        dh_ = dh_ref[...]
        mu = jnp.mean(rr, axis=-1, keepdims=True)
        xc = rr - mu
        var = jnp.mean(xc * xc, axis=-1, keepdims=True)
        rstd = lax.rsqrt(var + LN_EPS)
        xh = xc * rstd
        dxh = dh_ * g_ref[...]
        m1 = jnp.mean(dxh, axis=-1, keepdims=True)
        m2 = jnp.mean(dxh * xh, axis=-1, keepdims=True)
        dr_ref[...] = rstd * (dxh - m1 - xh * m2)
        dg_ref[...] += jnp.sum(dh_ * xh, axis=0, keepdims=True)
        db_ref[...] += jnp.sum(dh_, axis=0, keepdims=True)

    row = pl.BlockSpec((tm, d), lambda i: (i, 0))
    vec = pl.BlockSpec((1, d), lambda i: (0, 0))
    dep_specs, dep_ops = _dep_args(dep)
    return pl.pallas_call(
        body, out_shape=(jax.ShapeDtypeStruct((s, d), F32), jax.ShapeDtypeStruct((1, d), F32),
                         jax.ShapeDtypeStruct((1, d), F32)),
        grid=(s // tm,), in_specs=[row, row, vec] + dep_specs, out_specs=(row, vec, vec), name=name,
        compiler_params=_cparams(("arbitrary",)))(r, dh, g, *dep_ops)


def _rope_tables(s):
    half = HEAD // 2
    pos = jnp.arange(s, dtype=F32)
    inv = ROPE_THETA ** (-jnp.arange(half, dtype=F32) * 2.0 / HEAD)
    ang = pos[:, None] * inv[None, :]
    cos, sin = jnp.cos(ang), jnp.sin(ang)
    cos = jnp.concatenate([cos, cos, cos, cos], axis=1)
    sin = jnp.concatenate([-sin, sin, -sin, sin], axis=1)
    return cos, sin


def _rotate(x, cos, sin):
    lane = lax.broadcasted_iota(jnp.int32, x.shape, 1)
    partner = jnp.where((lane % HEAD) < HEAD // 2, pltpu.roll(x, LANE - HEAD // 2, axis=1),
                        pltpu.roll(x, HEAD // 2, axis=1))
    return x * cos + partner * sin


def _class_rows(c, d, tm):
    return pl.ds(c, tm // d, stride=d) if d > 1 else pl.ds(0, tm)


def _dilated_spec(tm, d, w):
    return pl.BlockSpec((tm // d, d * w), lambda i: (i, 0))


def _token_scratch(tm, w):
    return pltpu.VMEM((w // LANE, tm, LANE), F32)


def _to_tokens(src_ref, dst3, d, tm):
    nj = dst3.shape[0]
    for cls in range(d):
        for j in range(nj):
            col = (cls * nj + j) * LANE
            dst3.at[j][_class_rows(cls, d, tm), :] = src_ref[:, col:col + LANE]


def _to_dilated(src3, dst_ref, d, tm):
    nj = src3.shape[0]
    for cls in range(d):
        for j in range(nj):
            col = (cls * nj + j) * LANE
            dst_ref[:, col:col + LANE] = src3.at[j][_class_rows(cls, d, tm), :].astype(dst_ref.dtype)


def _token_value(src3):
    return jnp.concatenate([src3[j] for j in range(src3.shape[0])], axis=1)


def _rope_fwd(proj, cos, sin, name):
    s = proj.shape[0]
    tm = 512
    w = 3 * ATTN_W
    nj = w // LANE

    def body(*refs):
        p_refs, (c_ref, s_ref), o_refs, rot = refs[:nj], refs[nj:nj + 2], refs[nj + 2:nj + 5], refs[nj + 5]
        c, sn = c_ref[...], s_ref[...]
        for j in range(nj):
            x = p_refs[j][...]
            rot[j] = _rotate(x, c, sn) if j < 2 * ATTN_W // LANE else x
        for d, o_ref in zip(DILATIONS, o_refs):
            _to_dilated(rot, o_ref, d, tm)

    tab = pl.BlockSpec((tm, LANE), lambda i: (i, 0))
    cols = [pl.BlockSpec((tm, LANE), functools.partial(lambda i, j: (i, j), j=j)) for j in range(nj)]
    return pl.pallas_call(
        body, out_shape=tuple(jax.ShapeDtypeStruct((s // d, d * w), BF16) for d in DILATIONS),
        grid=(s // tm,), in_specs=cols + [tab, tab],
        out_specs=tuple(_dilated_spec(tm, d, w) for d in DILATIONS),
        scratch_shapes=[_token_scratch(tm, w)], name=name,
        compiler_params=_cparams(("parallel",)))(*[proj] * nj, cos, sin)


def _dproj_assemble(dqkv_list, dxr, dgate, du, cos, sin, name):
    s = dxr.shape[0]
    tm = 512
    nq = 3 * ATTN_W // LANE

    def body(*refs):
        br = refs[:9]
        dxr_ref, dg_ref, du_ref, c_ref, s_ref, o_ref = refs[9:15]
        tok = refs[15:]
        c, sn = c_ref[...], -s_ref[...]
        for part in range(3):
            for b, d in enumerate(DILATIONS[1:], start=1):
                _to_tokens(br[3 * b + part], tok[2 * part + b - 1], d, tm)
        for j in range(nq):
            part, jj = divmod(j, ATTN_W // LANE)
            x = br[part][:, jj * LANE:(jj + 1) * LANE] + tok[2 * part][jj] + tok[2 * part + 1][jj]
            if part < 2:
                x = _rotate(x, c, sn)
            o_ref[:, j * LANE:(j + 1) * LANE] = x.astype(BF16)
        o_ref[:, 3 * ATTN_W:3 * ATTN_W + LRU_W] = dxr_ref[...].astype(BF16)
        o_ref[:, 3 * ATTN_W + LRU_W:3 * ATTN_W + 2 * LRU_W] = dg_ref[...].astype(BF16)
        o_ref[:, 3 * ATTN_W + 2 * LRU_W:] = du_ref[...].astype(BF16)

    a_spec = pl.BlockSpec((tm, ATTN_W), lambda i: (i, 0))
    tab = pl.BlockSpec((tm, LANE), lambda i: (i, 0))
    ordered = [dqkv_list[b][p] for b in range(3) for p in range(3)]
    d_specs = [_dilated_spec(tm, d, ATTN_W) for d in DILATIONS for _ in range(3)]
    return pl.pallas_call(
        body, out_shape=jax.ShapeDtypeStruct((s, D_IN), BF16), grid=(s // tm,),
        in_specs=d_specs + [a_spec, a_spec, pl.BlockSpec((tm, S5_W), lambda i: (i, 0)), tab, tab],
        out_specs=pl.BlockSpec((tm, D_IN), lambda i: (i, 0)),
        scratch_shapes=[_token_scratch(tm, ATTN_W)] * 6, name=name,
        compiler_params=_cparams(("parallel",)))(*ordered, dxr, dgate, du, cos, sin)


def _attn_tiles(s, d):
    m = s // d
    tq = min(m, ATTN_TILE)
    return m, tq, tq // ATTN_BLK


def _band_mask(qb):
    qi = lax.broadcasted_iota(jnp.int32, (ATTN_BLK, 2 * ATTN_BLK), 0)
    ki = lax.broadcasted_iota(jnp.int32, (ATTN_BLK, 2 * ATTN_BLK), 1)
    dist = qi + ATTN_BLK - ki
    return (dist >= 0) & (dist <= ATTN_BLK) & ((ki >= ATTN_BLK) | (qb > 0))


def _head_cols(h):
    return (slice(h * HEAD, (h + 1) * HEAD), slice(ATTN_W + h * HEAD, ATTN_W + (h + 1) * HEAD),
            slice(2 * ATTN_W + h * HEAD, 2 * ATTN_W + (h + 1) * HEAD))


def _attn_fwd(qv, d, name):
    m = qv.shape[0]
    w3 = 3 * ATTN_W
    _, tq, n = _attn_tiles(m * d, d)
    scale = HEAD ** -0.5

    def body(x_ref, p_ref, o_ref, l_ref):
        b = pl.program_id(1)

        def block(i, first):
            r0 = 0 if first else pl.multiple_of(i * ATTN_BLK, ATTN_BLK)
            rows = pl.ds(r0, ATTN_BLK)
            valid = _band_mask(b * n + i)
            if not first:
                krows = pl.ds(pl.multiple_of(i * ATTN_BLK - ATTN_BLK, ATTN_BLK), 2 * ATTN_BLK)
            for h in range(ATTN_W // HEAD):
                qs, ks, vs = _head_cols(h)
                q = x_ref[rows, qs]
                if first:
                    k = jnp.concatenate([p_ref[:, ks], x_ref[0:ATTN_BLK, ks]], axis=0)
                    v = jnp.concatenate([p_ref[:, vs], x_ref[0:ATTN_BLK, vs]], axis=0)
                else:
                    k = x_ref[krows, ks]
                    v = x_ref[krows, vs]
                sc = jnp.where(valid, _dot_nt(q, k) * scale, -1e30)
                mx = jnp.max(sc, axis=-1, keepdims=True)
                p = jnp.exp(sc - mx)
                l = jnp.sum(p, axis=-1, keepdims=True)
                o_ref[rows, qs] = _dot(p, v) / l
                l_ref[rows, qs] = jnp.broadcast_to(mx + jnp.log(l), (ATTN_BLK, HEAD))

        block(0, True)
        if n > 1:
            def loop(i, carry):
                block(i, False)
                return carry
            lax.fori_loop(1, n, loop, 0)

    shp = jax.ShapeDtypeStruct((m, d * ATTN_W), F32)
    ospec = pl.BlockSpec((tq, ATTN_W), lambda c, b: (b, c))
    out, lse = pl.pallas_call(
        body, out_shape=(shp, shp), grid=(d, m // tq),
        in_specs=[pl.BlockSpec((tq, w3), lambda c, b: (b, c)),
                  pl.BlockSpec((ATTN_BLK, w3), lambda c, b: (jnp.maximum(b * n - 1, 0), c))],
        out_specs=(ospec, ospec), name=name,
        compiler_params=_cparams(("parallel", "parallel")))(qv, qv)
    return out, lse


def _attn_bwd(qv, ov, dov, lv, d, name, dep=None):
    m = qv.shape[0]
    w3 = 3 * ATTN_W
    _, tq, n = _attn_tiles(m * d, d)
    nb = m // ATTN_BLK
    scale = HEAD ** -0.5

    def body(x_ref, p_ref, nx_ref, o_ref, do_ref, l_ref, on_ref, don_ref, ln_ref, *rest):
        dq_ref, dk_ref, dv_ref = rest[-3:]
        b = pl.program_id(1)
        dk_ref[...] = jnp.zeros_like(dk_ref)
        dv_ref[...] = jnp.zeros_like(dv_ref)

        def grads(q, k, v, o, do, lse, valid):
            sc = jnp.where(valid, _dot_nt(q, k) * scale, -1e30)
            p = jnp.exp(sc - lse)
            delta = jnp.sum(do * o, axis=-1, keepdims=True)
            return p, p * (_dot_nt(do, v) - delta) * scale

        def block(i, first):
            r0 = 0 if first else pl.multiple_of(i * ATTN_BLK, ATTN_BLK)
            rows = pl.ds(r0, ATTN_BLK)
            valid = _band_mask(b * n + i)
            if not first:
                krows = pl.ds(pl.multiple_of(i * ATTN_BLK - ATTN_BLK, ATTN_BLK), 2 * ATTN_BLK)
            for h in range(ATTN_W // HEAD):
                qs, ks, vs = _head_cols(h)
                q = x_ref[rows, qs]
                do = do_ref[rows, qs]
                if first:
                    k = jnp.concatenate([p_ref[:, ks], x_ref[0:ATTN_BLK, ks]], axis=0)
                    v = jnp.concatenate([p_ref[:, vs], x_ref[0:ATTN_BLK, vs]], axis=0)
                else:
                    k = x_ref[krows, ks]
                    v = x_ref[krows, vs]
                p, ds = grads(q, k, v, o_ref[rows, qs], do, l_ref[rows, qs][:, 0:1], valid)
                dq_ref[rows, qs] = _dot(ds, k)
                if first:
                    dk_ref[0:ATTN_BLK, qs] += _dot_tn(ds[:, ATTN_BLK:], q)
                    dv_ref[0:ATTN_BLK, qs] += _dot_tn(p[:, ATTN_BLK:], do)
                else:
                    dk_ref[krows, qs] += _dot_tn(ds, q)
                    dv_ref[krows, qs] += _dot_tn(p, do)

        block(0, True)
        if n > 1:
            def loop(i, carry):
                block(i, False)
                return carry
            lax.fori_loop(1, n, loop, 0)

        last = slice((n - 1) * ATTN_BLK, n * ATTN_BLK)
        qi = lax.broadcasted_iota(jnp.int32, (ATTN_BLK, ATTN_BLK), 0)
        ki = lax.broadcasted_iota(jnp.int32, (ATTN_BLK, ATTN_BLK), 1)
        valid_next = (qi <= ki) & ((b + 1) * n < nb)
        for h in range(ATTN_W // HEAD):
            qs, ks, vs = _head_cols(h)
            q = nx_ref[:, qs]
            do = don_ref[:, qs]
            p, ds = grads(q, x_ref[last, ks], x_ref[last, vs], on_ref[:, qs], do, ln_ref[:, qs][:, 0:1],
                          valid_next)
            dk_ref[last, qs] += _dot_tn(ds, q)
            dv_ref[last, qs] += _dot_tn(p, do)

    nxt = lambda b: jnp.minimum((b + 1) * n, nb - 1)
    xs = pl.BlockSpec((tq, w3), lambda c, b: (b, c))
    xp = pl.BlockSpec((ATTN_BLK, w3), lambda c, b: (jnp.maximum(b * n - 1, 0), c))
    xn = pl.BlockSpec((ATTN_BLK, w3), lambda c, b: (nxt(b), c))
    a = pl.BlockSpec((tq, ATTN_W), lambda c, b: (b, c))
    an = pl.BlockSpec((ATTN_BLK, ATTN_W), lambda c, b: (nxt(b), c))
    shp = jax.ShapeDtypeStruct((m, d * ATTN_W), F32)
    dep_specs, dep_ops = _dep_args(dep)
    return pl.pallas_call(
        body, out_shape=(shp, shp, shp), grid=(d, m // tq),
        in_specs=[xs, xp, xn, a, a, a, an, an, an] + dep_specs, out_specs=(a, a, a), name=name,
        compiler_params=_cparams(("parallel", "parallel")))(qv, qv, qv, ov, dov, lv, ov, dov, lv, *dep_ops)


def _rms(x, g):
    ms = jnp.mean(x * x, axis=-1, keepdims=True)
    return x * lax.rsqrt(ms + RMS_EPS) * g


def _rms_bwd(x, g, dy):
    ms = jnp.mean(x * x, axis=-1, keepdims=True)
    r = lax.rsqrt(ms + RMS_EPS)
    dyg = dy * g
    dx = r * dyg - x * (r * r * r) * jnp.mean(x * dyg, axis=-1, keepdims=True)
    return dx, dy * x * r


def _mix_fwd(outs, lses, lru, s5, g, name):
    s = lru.shape[0]
    tm = 256

    def body(o1, o2, o3, l1, l2, l3, lru_ref, s5_ref, g_ref, mixed_ref, mixed_t_ref, ov1, ov2, ov3,
             lv1, lv2, lv3, so2, so3, sl2, sl3):
        for d, src, dst in ((DILATIONS[1], o2, so2), (DILATIONS[2], o3, so3),
                            (DILATIONS[1], l2, sl2), (DILATIONS[2], l3, sl3)):
            _to_tokens(src, dst, d, tm)
        a1, a2, a3 = l1[...], _token_value(sl2), _token_value(sl3)
        mx = jnp.maximum(jnp.maximum(a1, a2), a3)
        e1, e2, e3 = jnp.exp(a1 - mx), jnp.exp(a2 - mx), jnp.exp(a3 - mx)
        den = e1 + e2 + e3
        o = (e1 * o1[...] + e2 * _token_value(so2) + e3 * _token_value(so3)) / den
        lse = mx + jnp.log(den)
        ov1[...] = o
        lv1[...] = lse
        for j in range(ATTN_W // LANE):
            so2[j] = o[:, j * LANE:(j + 1) * LANE]
            sl2[j] = lse[:, j * LANE:(j + 1) * LANE]
        for d, o_dst, l_dst in ((DILATIONS[1], ov2, lv2), (DILATIONS[2], ov3, lv3)):
            _to_dilated(so2, o_dst, d, tm)
            _to_dilated(sl2, l_dst, d, tm)
        gg = g_ref[...]
        mixed = jnp.concatenate([_rms(o, gg[:, :ATTN_W]),
                                 _rms(lru_ref[...], gg[:, ATTN_W:ATTN_W + LRU_W]),
                                 _rms(s5_ref[...], gg[:, ATTN_W + LRU_W:])], axis=1)
        mixed_ref[...] = mixed.astype(BF16)
        mixed_t_ref[...] = mixed.T.astype(BF16)

    a = pl.BlockSpec((tm, ATTN_W), lambda i: (i, 0))
    s5s = pl.BlockSpec((tm, S5_W), lambda i: (i, 0))
    full = pl.BlockSpec((tm, D_MODEL), lambda i: (i, 0))
    vec = pl.BlockSpec((1, D_MODEL), lambda i: (0, 0))
    dil = [_dilated_spec(tm, d, ATTN_W) for d in DILATIONS]
    dshape = [jax.ShapeDtypeStruct((s // d, d * ATTN_W), F32) for d in DILATIONS]
    res = pl.pallas_call(
        body, out_shape=(jax.ShapeDtypeStruct((s, D_MODEL), BF16), jax.ShapeDtypeStruct((D_MODEL, s), BF16),
                         *dshape, *dshape),
        grid=(s // tm,), in_specs=dil + dil + [a, s5s, vec],
        out_specs=(full, pl.BlockSpec((D_MODEL, tm), lambda i: (0, i)), *dil, *dil),
        scratch_shapes=[_token_scratch(tm, ATTN_W)] * 4, name=name,
        compiler_params=_cparams(("parallel",)))(*outs, *lses, lru, s5, g)
    return res[0], res[1], res[2:5], res[5:8]


def _mix_bwd(dr, w_out, o, lru, s5, g, name, dep=None):
    s = lru.shape[0]
    tm = 256

    def body(dr_ref, w_ref, o_ref, lru_ref, s5_ref, g_ref, *rest):
        do_ref, do2_ref, do3_ref, dlru_ref, ds5_ref, dg_ref, stage = rest[-7:]

        @pl.when(pl.program_id(0) == 0)
        def _():
            dg_ref[...] = jnp.zeros_like(dg_ref)
        gg = g_ref[...]
        dm = _dot_nt(dr_ref[...], w_ref[...])
        dx, dgr = _rms_bwd(o_ref[...], gg[:, :ATTN_W], dm[:, :ATTN_W])
        do_ref[...] = dx
        for j in range(ATTN_W // LANE):
            stage[j] = dx[:, j * LANE:(j + 1) * LANE]
        _to_dilated(stage, do2_ref, DILATIONS[1], tm)
        _to_dilated(stage, do3_ref, DILATIONS[2], tm)
        dg_ref[:, :ATTN_W] += jnp.sum(dgr, axis=0, keepdims=True)
        dx, dgr = _rms_bwd(lru_ref[...], gg[:, ATTN_W:ATTN_W + LRU_W], dm[:, ATTN_W:ATTN_W + LRU_W])
        dlru_ref[...] = dx
        dg_ref[:, ATTN_W:ATTN_W + LRU_W] += jnp.sum(dgr, axis=0, keepdims=True)
        dx, dgr = _rms_bwd(s5_ref[...], gg[:, ATTN_W + LRU_W:], dm[:, ATTN_W + LRU_W:])
        ds5_ref[...] = dx
        dg_ref[:, ATTN_W + LRU_W:] += jnp.sum(dgr, axis=0, keepdims=True)

    a = pl.BlockSpec((tm, ATTN_W), lambda i: (i, 0))
    s5s = pl.BlockSpec((tm, S5_W), lambda i: (i, 0))
    full = pl.BlockSpec((tm, D_MODEL), lambda i: (i, 0))
    vec = pl.BlockSpec((1, D_MODEL), lambda i: (0, 0))
    dil = [_dilated_spec(tm, d, ATTN_W) for d in DILATIONS]
    dshape = [jax.ShapeDtypeStruct((s // d, d * ATTN_W), F32) for d in DILATIONS]
    dep_specs, dep_ops = _dep_args(dep)
    res = pl.pallas_call(
        body, out_shape=(*dshape, jax.ShapeDtypeStruct((s, LRU_W), F32),
                         jax.ShapeDtypeStruct((s, S5_W), F32), jax.ShapeDtypeStruct((1, D_MODEL), F32)),
        grid=(s // tm,),
        in_specs=[full, pl.BlockSpec((D_MODEL, D_MODEL), lambda i: (0, 0)), a, a, s5s, vec] + dep_specs,
        out_specs=(*dil, a, s5s, vec), scratch_shapes=[_token_scratch(tm, ATTN_W)], name=name,
        compiler_params=_cparams(("arbitrary",)))(dr, w_out, o, lru, s5, g, *dep_ops)
    return res[0:3], res[3], res[4], res[5]


def _lru_gate_math(xc, pre_r, pre_i, lam):
    r = _sigmoid(pre_r)
    i = _sigmoid(pre_i)
    log_a = -LRU_C * r * _softplus(-lam)
    a = jnp.exp(log_a)
    u = jnp.sqrt(-_expm1(2.0 * log_a)) * (i * xc)
    return a, u


def _lru_conv(x, prev8, cw, cb):
    y = cb + cw[LRU_CONV - 1:LRU_CONV, :] * x
    for k in range(LRU_CONV - 1):
        y = y + cw[k:k + 1, :] * _shift_down_prev(x, LRU_CONV - 1 - k, prev8)
    return y


def _lru_specs(s):
    xo = 3 * ATTN_W // LANE
    go = xo + LRU_W // LANE
    xr = pl.BlockSpec((s, LANE), lambda j: (0, xo + j))
    gt = pl.BlockSpec((s, LANE), lambda j: (0, go + j))
    cw = pl.BlockSpec((LRU_CONV, LANE), lambda j: (0, j))
    vec = pl.BlockSpec((1, LANE), lambda j: (0, j))
    wbd = pl.BlockSpec((LANE, LANE), lambda j: (j, j))
    col = pl.BlockSpec((s, LANE), lambda j: (0, j))
    return xr, gt, cw, vec, wbd, col


def _lru_fwd(proj, cw, cb, wr, br, wi, bi, lam, name):
    s = proj.shape[0]
    t = SCAN_T

    def body(xr_ref, gt_ref, cw_ref, cb_ref, wr_ref, br_ref, wi_ref, bi_ref, lam_ref, o_ref, xc_ref, a_ref, h_ref):
        cwv, cbv, lamv = cw_ref[...], cb_ref[...], lam_ref[...]
        wrv, wiv, brv, biv = wr_ref[...], wi_ref[...], br_ref[...], bi_ref[...]

        def chunk(c, carry):
            h_c, prev8 = carry
            rows = pl.ds(pl.multiple_of(c * t, t), t)
            x = xr_ref[rows, :]
            xc = _lru_conv(x, prev8, cwv, cbv)
            a, u = _lru_gate_math(xc, _dot(xc, wrv) + brv, _dot(xc, wiv) + biv, lamv)
            h = _scan_chunk(a, u, h_c)
            xc_ref[rows, :] = xc
            a_ref[rows, :] = a
            h_ref[rows, :] = h
            o_ref[rows, :] = h * _gelu(gt_ref[rows, :])
            return h[t - 1:t, :], x[t - 8:t, :]

        lax.fori_loop(0, s // t, chunk, (jnp.zeros((1, LANE), F32), jnp.zeros((8, LANE), F32)))

    xr, gt, cws, vec, wbd, col = _lru_specs(s)
    shp = jax.ShapeDtypeStruct((s, LRU_W), F32)
    return pl.pallas_call(
        body, out_shape=(shp,) * 4, grid=(LRU_W // LANE,),
        in_specs=[xr, gt, cws, vec, wbd, vec, wbd, vec, vec], out_specs=(col,) * 4, name=name,
        compiler_params=_cparams(("parallel",)))(proj, proj, cw, cb, wr, br, wi, bi, lam)


def _lru_bwd(proj, dout, xc_all, a_all, h_all, cw, cb, wr, br, wi, bi, lam, name):
    s = proj.shape[0]
    t = SCAN_T
    nc = s // t

    def body(xr_ref, gt_ref, do_ref, xc_s, a_s, h_s, cw_ref, cb_ref, wr_ref, br_ref, wi_ref, bi_ref, lam_ref,
             dxr_ref, dgt_ref, dcw_ref, dcb_ref, dwr_ref, dbr_ref, dwi_ref, dbi_ref, dlam_ref):
        cwv, cbv, lamv = cw_ref[...], cb_ref[...], lam_ref[...]
        wrv, wiv, brv, biv = wr_ref[...], wi_ref[...], br_ref[...], bi_ref[...]
        z1 = jnp.zeros((1, LANE), F32)
        zw = jnp.zeros((LANE, LANE), F32)

        def bchunk(ci, carry):
            g_next, a_next, dxc_next8, dcw, dcb, dwr, dbr, dwi, dbi, dlam = carry
            c = nc - 1 - ci
            t0 = pl.multiple_of(c * t, t)
            rows = pl.ds(t0, t)
            before = pl.ds(pl.multiple_of(jnp.maximum(t0 - 8, 0), 8), 8)
            has_prev = (c > 0).astype(F32)
            x, gt, do = xr_ref[rows, :], gt_ref[rows, :], do_ref[rows, :]
            xc, a, h = xc_s[rows, :], a_s[rows, :], h_s[rows, :]
            prev8_x = xr_ref[before, :] * has_prev
            prev8_h = h_s[before, :] * has_prev
            dgt_ref[rows, :] = do * h * _gelu_grad(gt)
            dh = do * _gelu(gt)
            a_plus = _shift_up_next(a, 1, jnp.broadcast_to(a_next, (8, LANE)))
            g = _scan_chunk(a_plus, dh, g_next, reverse=True)
            da = g * _shift_down_prev(h, 1, prev8_h)
            pre_r = _dot(xc, wrv) + brv
            pre_i = _dot(xc, wiv) + biv
            _, vjp = jax.vjp(_lru_gate_math, xc, pre_r, pre_i, lamv)
            dxc, dpre_r, dpre_i, dlam_c = vjp((da, g))
            dxc = dxc + _dot_nt(dpre_r, wrv) + _dot_nt(dpre_i, wiv)
            dx = cwv[LRU_CONV - 1:LRU_CONV, :] * dxc
            dcw_rows = [None] * LRU_CONV
            dcw_rows[LRU_CONV - 1] = jnp.sum(dxc * x, axis=0, keepdims=True)
            for k in range(LRU_CONV - 1):
                sh = LRU_CONV - 1 - k
                dx = dx + cwv[k:k + 1, :] * _shift_up_next(dxc, sh, dxc_next8)
                dcw_rows[k] = jnp.sum(dxc * _shift_down_prev(x, sh, prev8_x), axis=0, keepdims=True)
            dxr_ref[rows, :] = dx
            return (g[0:1, :], a[0:1, :], dxc[0:8, :],
                    dcw + jnp.concatenate(dcw_rows, axis=0),
                    dcb + jnp.sum(dxc, axis=0, keepdims=True),
                    dwr + _dot_tn(xc, dpre_r), dbr + jnp.sum(dpre_r, axis=0, keepdims=True),
                    dwi + _dot_tn(xc, dpre_i), dbi + jnp.sum(dpre_i, axis=0, keepdims=True),
                    dlam + dlam_c)

        init = (z1, z1, jnp.zeros((8, LANE), F32), jnp.zeros((LRU_CONV, LANE), F32), z1, zw, z1, zw, z1, z1)
        res = lax.fori_loop(0, nc, bchunk, init)
        dcw_ref[...] = res[3]
        dcb_ref[...] = res[4]
        dwr_ref[...] = res[5]
        dbr_ref[...] = res[6]
        dwi_ref[...] = res[7]
        dbi_ref[...] = res[8]
        dlam_ref[...] = res[9]

    xr, gt, cws, vec, wbd, col = _lru_specs(s)
    vshape = jax.ShapeDtypeStruct((1, LRU_W), F32)
    wshape = jax.ShapeDtypeStruct((LRU_W, LRU_W), F32)
    return pl.pallas_call(
        body,
        out_shape=(jax.ShapeDtypeStruct((s, LRU_W), F32), jax.ShapeDtypeStruct((s, LRU_W), F32),
                   jax.ShapeDtypeStruct((LRU_CONV, LRU_W), F32), vshape, wshape, vshape, wshape, vshape, vshape),
        grid=(LRU_W // LANE,),
        in_specs=[xr, gt, col, col, col, col, cws, vec, wbd, vec, wbd, vec, vec],
        out_specs=(col, col, cws, vec, wbd, vec, wbd, vec, vec), name=name,
        compiler_params=_cparams(("parallel",)))(proj, proj, dout, xc_all, a_all, h_all, cw, cb, wr, br, wi, bi,
                                                 lam)


def _s5_disc_math(a_re, a_im, log_step, bt_re, bt_im):
    step = jnp.exp(log_step)
    dt_re, dt_im = step * a_re, step * a_im
    mag = jnp.exp(dt_re)
    ab_re, ab_im = mag * jnp.cos(dt_im), mag * jnp.sin(dt_im)
    z_re, z_im = ab_re - 1.0, ab_im
    den = a_re * a_re + a_im * a_im
    f_re = (z_re * a_re + z_im * a_im) / den
    f_im = (z_im * a_re - z_re * a_im) / den
    bb_re = f_re * bt_re - f_im * bt_im
    bb_im = f_re * bt_im + f_im * bt_re
    return ab_re, ab_im, bb_re, bb_im


def _s5_disc_fwd(a_re, a_im, log_step, bt_re, bt_im, name):
    def body(ar, ai, ls, br, bi, o1, o2, o3, o4):
        r = _s5_disc_math(ar[...], ai[...], ls[...], br[...], bi[...])
        o1[...], o2[...], o3[...], o4[...] = r

    shp = jax.ShapeDtypeStruct(a_re.shape, F32)
    return pl.pallas_call(body, out_shape=(shp,) * 4, name=name)(a_re, a_im, log_step, bt_re, bt_im)


def _s5_disc_bwd(a_re, a_im, log_step, bt_re, bt_im, cts, name):
    def body(ar, ai, ls, br, bi, c1, c2, c3, c4, o1, o2, o3, o4, o5):
        _, vjp = jax.vjp(_s5_disc_math, ar[...], ai[...], ls[...], br[...], bi[...])
        r = vjp((c1[...], c2[...], c3[...], c4[...]))
        o1[...], o2[...], o3[...], o4[...], o5[...] = r

    shp = jax.ShapeDtypeStruct(a_re.shape, F32)
    return pl.pallas_call(body, out_shape=(shp,) * 5, name=name)(a_re, a_im, log_step, bt_re, bt_im, *cts)


def _s5_u_specs(s):
    uo = (3 * ATTN_W + 2 * LRU_W) // LANE
    return (pl.BlockSpec((s, LANE), lambda j: (0, uo)), pl.BlockSpec((s, LANE), lambda j: (0, uo + 1)))


def _s5_scan_fwd(proj, b_re, b_im, lam_re, lam_im, c_re, c_im, name):
    s = proj.shape[0]
    t = SCAN_T

    def body(u0_ref, u1_ref, bre_ref, bim_ref, lre_ref, lim_ref, cre_ref, cim_ref, xre_ref, xim_ref, y_ref):
        @pl.when(pl.program_id(0) == 0)
        def _():
            y_ref[...] = jnp.zeros_like(y_ref)
        lr, li = lre_ref[...], lim_ref[...]
        consts = _cscan_consts(lr, li, False)
        bre, bim, cre, cim = bre_ref[...], bim_ref[...], cre_ref[...], cim_ref[...]

        def chunk(c, carry):
            cr, ci = carry
            rows = pl.ds(pl.multiple_of(c * t, t), t)
            u = jnp.concatenate([u0_ref[rows, :], u1_ref[rows, :]], axis=1).astype(BF16)
            xr, xi = _cscan_chunk(_dot(u, bre), _dot(u, bim), consts, (cr, ci))
            xre_ref[rows, :] = xr
            xim_ref[rows, :] = xi
            y_ref[rows, :] += _dot(xr, cre) - _dot(xi, cim)
            return xr[t - 1:t, :], xi[t - 1:t, :]

        z = jnp.zeros((1, S5_BLK), F32)
        lax.fori_loop(0, s // t, chunk, (z, z))

    u0, u1 = _s5_u_specs(s)
    bsp = pl.BlockSpec((S5_W, S5_BLK), lambda j: (0, j))
    csp = pl.BlockSpec((S5_BLK, S5_W), lambda j: (j, 0))
    vec = pl.BlockSpec((1, S5_BLK), lambda j: (0, j))
    xsp = pl.BlockSpec((s, S5_BLK), lambda j: (0, j))
    ysp = pl.BlockSpec((s, S5_W), lambda j: (0, 0))
    xshape = jax.ShapeDtypeStruct((s, S5_STATES), F32)
    return pl.pallas_call(
        body, out_shape=(xshape, xshape, jax.ShapeDtypeStruct((s, S5_W), F32)),
        grid=(S5_STATES // S5_BLK,), in_specs=[u0, u1, bsp, bsp, vec, vec, csp, csp],
        out_specs=(xsp, xsp, ysp), name=name,
        compiler_params=_cparams(("arbitrary",)))(proj, proj, b_re, b_im, lam_re, lam_im, c_re, c_im)


def _s5_scan_bwd(proj, dy, du_init, x_re, x_im, b_re, b_im, lam_re, lam_im, c_re, c_im, name):
    s = proj.shape[0]
    t = SCAN_T
    nc = s // t

    def body(u0_ref, u1_ref, dy_ref, dui_ref, xre_ref, xim_ref, bre_ref, bim_ref, lre_ref, lim_ref,
             cre_ref, cim_ref, du_ref, dlr_ref, dli_ref, dbr_ref, dbi_ref, dcr_ref, dci_ref):
        @pl.when(pl.program_id(0) == 0)
        def _():
            du_ref[...] = dui_ref[...]
        mr, mi = lre_ref[...], -lim_ref[...]
        consts = _cscan_consts(mr, mi, True)
        bre, bim, cre, cim = bre_ref[...], bim_ref[...], cre_ref[...], cim_ref[...]
        dbr_ref[...] = jnp.zeros_like(dbr_ref)
        dbi_ref[...] = jnp.zeros_like(dbi_ref)
        dcr_ref[...] = jnp.zeros_like(dcr_ref)
        dci_ref[...] = jnp.zeros_like(dci_ref)

        def chunk(ci_, carry):
            gnr, gni, dlr, dli = carry
            c = nc - 1 - ci_
            t0 = pl.multiple_of(c * t, t)
            rows = pl.ds(t0, t)
            before = pl.ds(pl.multiple_of(jnp.maximum(t0 - 8, 0), 8), 8)
            has_prev = (c > 0).astype(F32)
            dyc = dy_ref[rows, :].astype(BF16)
            u = jnp.concatenate([u0_ref[rows, :], u1_ref[rows, :]], axis=1).astype(BF16)
            gr, gi = _cscan_chunk(_dot_nt(dyc, cre), -_dot_nt(dyc, cim), consts, (gnr, gni), reverse=True)
            xr, xi = xre_ref[rows, :], xim_ref[rows, :]
            xpr = _shift_down_prev(xr, 1, xre_ref[before, :] * has_prev)
            xpi = _shift_down_prev(xi, 1, xim_ref[before, :] * has_prev)
            dlr = dlr + jnp.sum(gr * xpr + gi * xpi, axis=0, keepdims=True)
            dli = dli + jnp.sum(gi * xpr - gr * xpi, axis=0, keepdims=True)
            du_ref[rows, :] += _dot_nt(gr, bre) + _dot_nt(gi, bim)
            dbr_ref[...] += _dot_tn(u, gr)
            dbi_ref[...] += _dot_tn(u, gi)
            dcr_ref[...] += _dot_tn(xr, dyc)
            dci_ref[...] -= _dot_tn(xi, dyc)
            return gr[0:1, :], gi[0:1, :], dlr, dli

        z = jnp.zeros((1, S5_BLK), F32)
        res = lax.fori_loop(0, nc, chunk, (z, z, z, z))
        dlr_ref[...] = res[2]
        dli_ref[...] = res[3]

    u0, u1 = _s5_u_specs(s)
    bsp = pl.BlockSpec((S5_W, S5_BLK), lambda j: (0, j))
    csp = pl.BlockSpec((S5_BLK, S5_W), lambda j: (j, 0))
    vec = pl.BlockSpec((1, S5_BLK), lambda j: (0, j))
    xsp = pl.BlockSpec((s, S5_BLK), lambda j: (0, j))
    ysp = pl.BlockSpec((s, S5_W), lambda j: (0, 0))
    return pl.pallas_call(
        body,
        out_shape=(jax.ShapeDtypeStruct((s, S5_W), F32),
                   jax.ShapeDtypeStruct((1, S5_STATES), F32), jax.ShapeDtypeStruct((1, S5_STATES), F32),
                   jax.ShapeDtypeStruct((S5_W, S5_STATES), F32), jax.ShapeDtypeStruct((S5_W, S5_STATES), F32),
                   jax.ShapeDtypeStruct((S5_STATES, S5_W), F32), jax.ShapeDtypeStruct((S5_STATES, S5_W), F32)),
        grid=(S5_STATES // S5_BLK,),
        in_specs=[u0, u1, ysp, ysp, xsp, xsp, bsp, bsp, vec, vec, csp, csp],
        out_specs=(ysp, vec, vec, bsp, bsp, csp, csp), name=name,
        compiler_params=_cparams(("arbitrary",)))(
            proj, proj, dy, du_init, x_re, x_im, b_re, b_im, lam_re, lam_im, c_re, c_im)


def _s5_out_fwd(proj, y_acc, dvec, w_glu, b_glu, name):
    s = proj.shape[0]
    tm = 512
    uo = (3 * ATTN_W + 2 * LRU_W) // LANE

    def body(u0_ref, u1_ref, y_ref, d_ref, w_ref, b_ref, o_ref, yp_ref):
        u = jnp.concatenate([u0_ref[...], u1_ref[...]], axis=1)
        y = y_ref[...] + d_ref[...] * u
        yp_ref[...] = y
        yg = _gelu(y)
        o_ref[...] = yg * _sigmoid(_dot(yg, w_ref[...]) + b_ref[...])

    u0 = pl.BlockSpec((tm, LANE), lambda i: (i, uo))
    u1 = pl.BlockSpec((tm, LANE), lambda i: (i, uo + 1))
    row = pl.BlockSpec((tm, S5_W), lambda i: (i, 0))
    vec = pl.BlockSpec((1, S5_W), lambda i: (0, 0))
    wsp = pl.BlockSpec((S5_W, S5_W), lambda i: (0, 0))
    shp = jax.ShapeDtypeStruct((s, S5_W), F32)
    return pl.pallas_call(
        body, out_shape=(shp, shp), grid=(s // tm,), in_specs=[u0, u1, row, vec, wsp, vec],
        out_specs=(row, row), name=name,
        compiler_params=_cparams(("parallel",)))(proj, proj, y_acc, dvec, w_glu, b_glu)


def _s5_out_bwd(proj, y_pre, dout, dvec, w_glu, b_glu, name):
    s = proj.shape[0]
    tm = 512
    uo = (3 * ATTN_W + 2 * LRU_W) // LANE

    def body(u0_ref, u1_ref, y_ref, do_ref, d_ref, w_ref, b_ref, dy_ref, dud_ref, dd_ref, dw_ref, db_ref):
        @pl.when(pl.program_id(0) == 0)
        def _():
            dd_ref[...] = jnp.zeros_like(dd_ref)
            dw_ref[...] = jnp.zeros_like(dw_ref)
            db_ref[...] = jnp.zeros_like(db_ref)
        u = jnp.concatenate([u0_ref[...], u1_ref[...]], axis=1)
        y = y_ref[...]
        do = do_ref[...]
        yg = _gelu(y)
        sg = _sigmoid(_dot(yg, w_ref[...]) + b_ref[...])
        dz = do * yg * sg * (1.0 - sg)
        dyg = do * sg + _dot_nt(dz, w_ref[...])
        dy = dyg * _gelu_grad(y)
        dy_ref[...] = dy
        dud_ref[...] = d_ref[...] * dy
        dd_ref[...] += jnp.sum(dy * u, axis=0, keepdims=True)
        dw_ref[...] += _dot_tn(yg, dz)
        db_ref[...] += jnp.sum(dz, axis=0, keepdims=True)

    u0 = pl.BlockSpec((tm, LANE), lambda i: (i, uo))
    u1 = pl.BlockSpec((tm, LANE), lambda i: (i, uo + 1))
    row = pl.BlockSpec((tm, S5_W), lambda i: (i, 0))
    vec = pl.BlockSpec((1, S5_W), lambda i: (0, 0))
    wsp = pl.BlockSpec((S5_W, S5_W), lambda i: (0, 0))
    shp = jax.ShapeDtypeStruct((s, S5_W), F32)
    vshape = jax.ShapeDtypeStruct((1, S5_W), F32)
    return pl.pallas_call(
        body, out_shape=(shp, shp, vshape, jax.ShapeDtypeStruct((S5_W, S5_W), F32), vshape),
        grid=(s // tm,), in_specs=[u0, u1, row, row, vec, wsp, vec],
        out_specs=(row, row, vec, wsp, vec), name=name,
        compiler_params=_cparams(("arbitrary",)))(proj, proj, y_pre, dout, dvec, w_glu, b_glu)


def _ffn_conv(x, prev8, cw, cb):
    y = cb + cw[FFN_CONV - 1:FFN_CONV, :] * x
    for k in range(FFN_CONV - 1):
        y = y + cw[k:k + 1, :] * _shift_down_prev(x, FFN_CONV - 1 - k, prev8)
    return y


def _ffn_up_act(h, wg, cw, cb, name, dep=None):
    s, d = h.shape
    tm = 512
    tb = 2 * FFN_CB
    nt = D_FF // FFN_CB

    def body(h_ref, wgate_ref, wval_ref, cw_ref, cb_ref, *rest):
        up_ref, o_ref, ot_ref, carry = rest[-4:]
        t = pl.program_id(1)

        @pl.when(pl.program_id(0) == 0)
        def _():
            carry[t] = jnp.zeros((8, tb), F32)
        hb = h_ref[...].astype(BF16)
        x = jnp.concatenate([_dot(hb, wgate_ref[...]), _dot(hb, wval_ref[...])], axis=1)
        up_ref[...] = x
        y = _ffn_conv(x, carry[t], cw_ref[...], cb_ref[...])
        carry[t] = x[tm - 8:tm, :]
        act = _gelu(y[:, :FFN_CB]) * y[:, FFN_CB:]
        o_ref[...] = act.astype(BF16)
        ot_ref[...] = act.T.astype(BF16)

    dep_specs, dep_ops = _dep_args(dep)
    return pl.pallas_call(
        body, out_shape=(jax.ShapeDtypeStruct((s, 2 * D_FF), F32), jax.ShapeDtypeStruct((s, D_FF), BF16),
                         jax.ShapeDtypeStruct((D_FF, s), BF16)),
        grid=(s // tm, nt),
        in_specs=[pl.BlockSpec((tm, d), lambda i, t: (i, 0)),
                  pl.BlockSpec((None, d, FFN_CB), lambda i, t: (t, 0, 0)),
                  pl.BlockSpec((None, d, FFN_CB), lambda i, t: (t + nt, 0, 0)),
                  pl.BlockSpec((FFN_CONV, tb), lambda i, t: (0, t)),
                  pl.BlockSpec((1, tb), lambda i, t: (0, t))] + dep_specs,
        out_specs=(pl.BlockSpec((tm, tb), lambda i, t: (i, t)), pl.BlockSpec((tm, FFN_CB), lambda i, t: (i, t)),
                   pl.BlockSpec((FFN_CB, tm), lambda i, t: (t, i))),
        scratch_shapes=[pltpu.VMEM((nt, 8, tb), F32)], name=name,
        compiler_params=_cparams(("arbitrary", "arbitrary")))(h, wg, wg, cw, cb, *dep_ops)


def _ffn_bwd(up, dr, w_down, wg, cw, cb, name):
    s = up.shape[0]
    d = dr.shape[1]
    tm = 256
    tb = 2 * FFN_CB
    nr = s // tm
    nt = D_FF // FFN_CB

    def body(x_ref, p_ref, dr_ref, wd_ref, wgate_ref, wval_ref, cw_ref, cb_ref,
             dup_ref, dh_ref, dcw_ref, dcb_ref, carry):
        i, t = pl.program_id(0), pl.program_id(1)

        @pl.when(i == 0)
        def _():
            carry[t] = jnp.zeros((8, tb), F32)

        @pl.when(t == 0)
        def _():
            dh_ref[...] = ALPHA * dr_ref[...]
        prev8 = p_ref[...] * (i < nr - 1).astype(F32)
        cwv = cw_ref[...]
        x = x_ref[...]
        dact = _dot_nt(dr_ref[...], wd_ref[...])
        shifted = [_shift_down_prev(x, FFN_CONV - 1 - k, prev8) for k in range(FFN_CONV - 1)]
        y = cb_ref[...] + cwv[FFN_CONV - 1:FFN_CONV, :] * x
        for k in range(FFN_CONV - 1):
            y = y + cwv[k:k + 1, :] * shifted[k]
        gate, val = y[:, :FFN_CB], y[:, FFN_CB:]
        dy = jnp.concatenate([dact * val * _gelu_grad(gate), dact * _gelu(gate)], axis=1)
        next8 = carry[t]
        carry[t] = dy[0:8, :]
        dx = cwv[FFN_CONV - 1:FFN_CONV, :] * dy
        dcw_rows = [None] * FFN_CONV
        dcw_rows[FFN_CONV - 1] = jnp.sum(dy * x, axis=0, keepdims=True)
        for k in range(FFN_CONV - 1):
            dx = dx + cwv[k:k + 1, :] * _shift_up_next(dy, FFN_CONV - 1 - k, next8)
            dcw_rows[k] = jnp.sum(dy * shifted[k], axis=0, keepdims=True)
        dup = dx.astype(BF16)
        dup_ref[...] = dup
        dh_ref[...] += _dot_nt(dup[:, :FFN_CB], wgate_ref[...]) + _dot_nt(dup[:, FFN_CB:], wval_ref[...])
        dcw_ref[...] = jnp.concatenate(dcw_rows, axis=0)
        dcb_ref[...] = jnp.sum(dy, axis=0, keepdims=True)

    row = lambda i: nr - 1 - i
    return pl.pallas_call(
        body, out_shape=(jax.ShapeDtypeStruct((s, 2 * D_FF), BF16), jax.ShapeDtypeStruct((s, d), F32),
                         jax.ShapeDtypeStruct((nr, FFN_CONV, 2 * D_FF), F32),
                         jax.ShapeDtypeStruct((nr, 1, 2 * D_FF), F32)),
        grid=(nr, nt),
        in_specs=[pl.BlockSpec((tm, tb), lambda i, t: (row(i), t)),
                  pl.BlockSpec((8, tb), lambda i, t: (jnp.maximum(row(i) * (tm // 8) - 1, 0), t)),
                  pl.BlockSpec((tm, d), lambda i, t: (row(i), 0)),
                  pl.BlockSpec((FFN_CB, d), lambda i, t: (t, 0)),
                  pl.BlockSpec((None, d, FFN_CB), lambda i, t: (t, 0, 0)),
                  pl.BlockSpec((None, d, FFN_CB), lambda i, t: (t + nt, 0, 0)),
                  pl.BlockSpec((FFN_CONV, tb), lambda i, t: (0, t)),
                  pl.BlockSpec((1, tb), lambda i, t: (0, t))],
        out_specs=(pl.BlockSpec((tm, tb), lambda i, t: (row(i), t)),
                   pl.BlockSpec((tm, d), lambda i, t: (row(i), 0)),
                   pl.BlockSpec((None, FFN_CONV, tb), lambda i, t: (row(i), 0, t)),
                   pl.BlockSpec((None, 1, tb), lambda i, t: (row(i), 0, t))),
        scratch_shapes=[pltpu.VMEM((nt, 8, tb), F32)], name=name,
        compiler_params=_cparams(("arbitrary", "arbitrary")))(up, up, dr, w_down, wg, wg, cw, cb)


def _sum_partials(ld_ref):
    gg = ld_ref[0].astype(F32)
    for k in range(1, N_DEV):
        gg = gg + ld_ref[k].astype(F32)
    return gg


def _adam_update(w, g, m, v):
    mn = ADAM_B1 * m + (1.0 - ADAM_B1) * g
    vn = ADAM_B2 * v + (1.0 - ADAM_B2) * (g * g)
    m_hat = mn / (1.0 - ADAM_B1 ** ADAM_STEP)
    v_hat = vn / (1.0 - ADAM_B2 ** ADAM_STEP)
    return -ADAM_LR * (m_hat / (jnp.sqrt(v_hat) + ADAM_EPS) + ADAM_WD * w), mn, vn


def _adamw_many(landed, ws, ms, vs, name):
    n, nl = len(ws), len(landed)

    def body(*refs):
        ld = refs[:nl * n]
        w_refs, m_refs, v_refs = (refs[(nl + k) * n:(nl + k + 1) * n] for k in range(3))
        outs = refs[(nl + 3) * n:]
        for i in range(n):
            for l in range(nl):
                one = slice(l, l + 1)
                gg = _sum_partials(ld[l * n + i])
                outs[i][one] = gg
                outs[n + i][one], outs[2 * n + i][one], outs[3 * n + i][one] = _adam_update(
                    w_refs[i][one], gg, m_refs[i][one], v_refs[i][one])

    vm = pl.BlockSpec(memory_space=pltpu.VMEM)
    shapes = [jax.ShapeDtypeStruct(w.shape, F32) for w in ws] * 4
    res = pl.pallas_call(
        body, out_shape=tuple(shapes), in_specs=[vm] * ((nl + 3) * n), out_specs=tuple([vm] * (4 * n)),
        name=name, compiler_params=_cparams())(*[a for layer in landed for a in layer], *ws, *ms, *vs)
    return res[:n], res[n:2 * n], res[2 * n:3 * n], res[3 * n:]


def _adamw_sum(landed, w, m, v, layer, prev, name):
    _, r, c = landed.shape
    nl = w.shape[0]
    tm = 8
    for cand in (512, 256, 128, 64, 32, 16):
        if r % cand == 0 and N_DEV * cand * c * 4 <= 4 * 1024 * 1024:
            tm = cand
            break

    def body(*refs):
        ld_ref, w_ref, m_ref, v_ref = refs[:4]
        g_ref, d_ref, mo_ref, vo_ref = refs[-4:]
        gg = _sum_partials(ld_ref)
        g_ref[...] = gg
        d_ref[...], mo_ref[...], vo_ref[...] = _adam_update(w_ref[...], gg, m_ref[...], v_ref[...])

    blk = pl.BlockSpec((None, tm, c), lambda i: (layer, i, 0))
    in_specs = [pl.BlockSpec((N_DEV, tm, c), lambda i: (0, i, 0)), blk, blk, blk]
    args = [landed, w, m, v]
    aliases = {}
    if prev is not None:
        in_specs += [pl.BlockSpec(memory_space=pl.ANY)] * 4
        args += list(prev)
        aliases = {4 + k: k for k in range(4)}
    shp = jax.ShapeDtypeStruct((nl, r, c), F32)
    return pl.pallas_call(
        body, out_shape=(shp,) * 4, grid=(r // tm,), in_specs=in_specs, out_specs=(blk,) * 4,
        input_output_aliases=aliases, name=name, compiler_params=_cparams(("parallel",)))(*args)


def _all_gather(shards, name):
    na = len(shards)

    def body(*refs):
        x_refs, out_refs = refs[:na], refs[na:2 * na]
        send_sems, recv_sems, local_sems = refs[2 * na:]
        x, y, c = lax.axis_index("x"), lax.axis_index("y"), lax.axis_index("c")
        me, sibling = (x, y, c), (x, y, 1 - c)
        chips = [(1 - x, y), (x, 1 - y), (1 - x, 1 - y)]

        def copy(a, k, block, to, src=None):
            dst = out_refs[a].at[4 * block[0] + 2 * block[1] + block[2]]
            return pltpu.make_async_remote_copy(
                src_ref=dst if src is None else src, dst_ref=dst,
                send_sem=send_sems.at[7 * a + k], recv_sem=recv_sems.at[7 * a + k],
                device_id=to, device_id_type=pl.DeviceIdType.MESH)

        mine, first, passed = [], [], []
        for a in range(na):
            cp = pltpu.make_async_copy(x_refs[a], out_refs[a].at[4 * x + 2 * y + c], local_sems.at[a])
            cp.start()
            mine.append(cp)
            cps = [copy(a, 0, me, sibling, src=x_refs[a])]
            cps += [copy(a, 1 + j, me, (*chip, c), src=x_refs[a]) for j, chip in enumerate(chips)]
            for cp in cps:
                cp.start()
            first += cps
        for j, chip in enumerate(chips):
            for a in range(na):
                copy(a, 1 + j, (*chip, c), me).wait_recv()
                cp = copy(a, 4 + j, (*chip, c), sibling)
                cp.start()
                passed.append(cp)
        for a in range(na):
            copy(a, 0, sibling, me).wait_recv()
            for j, chip in enumerate(chips):
                copy(a, 4 + j, (*chip, 1 - c), me).wait_recv()
        for cp in first + passed:
            cp.wait_send()
        for cp in mine:
            cp.wait()

    anyspec = pl.BlockSpec(memory_space=pl.ANY)
    return pl.pallas_call(
        body, out_shape=tuple(jax.ShapeDtypeStruct((N_DEV,) + t.shape, t.dtype) for t in shards),
        in_specs=[anyspec] * na, out_specs=tuple([anyspec] * na),
        scratch_shapes=[pltpu.SemaphoreType.DMA((7 * na,)), pltpu.SemaphoreType.DMA((7 * na,)),
                        pltpu.SemaphoreType.DMA((na,))],
        name=name)(*shards)


_HBM = pl.BlockSpec(memory_space=pltpu.HBM)
_SEM = pl.BlockSpec(memory_space=pltpu.SEMAPHORE)
_EFFECT = pltpu.SideEffectType.DATAFLOW_SIDE_EFFECTING


def _exchange_copies(src_refs, land_refs, send_sems, recv_sems, local_sems, gather):
    x, y, c = lax.axis_index("x"), lax.axis_index("y"), lax.axis_index("c")
    me = 4 * x + 2 * y + c
    per_array = send_sems.shape[0] > N_DEV - 1
    local, remote = [], []
    for a, (src, land) in enumerate(zip(src_refs, land_refs)):
        local.append(pltpu.make_async_copy(src if gather else src.at[me], land.at[me],
                                           local_sems.at[a if per_array else 0]))
    for k in range(1, N_DEV):
        px = x ^ ((k >> 2) & 1)
        py = y ^ ((k >> 1) & 1)
        pc = c ^ (k & 1)
        for a, (src, land) in enumerate(zip(src_refs, land_refs)):
            remote.append(pltpu.make_async_remote_copy(
                src_ref=src if gather else src.at[4 * px + 2 * py + pc], dst_ref=land.at[me],
                send_sem=send_sems.at[(7 * a if per_array else 0) + k - 1],
                recv_sem=recv_sems.at[(7 * a if per_array else 0) + k - 1],
                device_id=(px, py, pc), device_id_type=pl.DeviceIdType.MESH))
    return local, remote


def _exchange_start(srcs, gather, name, dep=None):
    na = len(srcs)
    ns = na if na <= 4 else 1
    lands = [lax.empty(((N_DEV,) + t.shape) if gather else t.shape, t.dtype) for t in srcs]

    def body(*refs):
        src_refs, land_refs = refs[:na], refs[na:2 * na]
        nin = 2 * na + (0 if dep is None else 1)
        send_sems, recv_sems, local_sems = refs[nin:nin + 3]
        token = refs[-1]
        local, remote = _exchange_copies(src_refs, land_refs, send_sems, recv_sems, local_sems, gather)
        for cp in local + remote:
            cp.start()
        token[...] = jnp.zeros_like(token)

    dep_specs, dep_ops = _dep_args(dep)
    hbm = lambda t: pltpu.HBM(t.shape, t.dtype)
    out = pl.pallas_call(
        body, name=name,
        out_shape=(pltpu.SemaphoreType.DMA((7 * ns,)), pltpu.SemaphoreType.DMA((7 * ns,)),
                   pltpu.SemaphoreType.DMA((ns,)), *[hbm(t) for t in srcs], *[hbm(t) for t in lands],
                   jax.ShapeDtypeStruct((8, LANE), F32)),
        in_specs=[_HBM] * (2 * na) + dep_specs,
        out_specs=(_SEM, _SEM, _SEM, *[_HBM] * (2 * na), pl.BlockSpec(memory_space=pltpu.VMEM)),
        input_output_aliases={i: 3 + i for i in range(2 * na)},
        compiler_params=pltpu.CompilerParams(has_side_effects=_EFFECT),
    )(*[pltpu.with_memory_space_constraint(t, pltpu.HBM) for t in srcs + lands], *dep_ops)
    return (out[:3], out[3:3 + na], out[3 + na:3 + 2 * na]), out[-1]


def _exchange_wait(handle, gather, after, name):
    sems, srcs, lands = handle
    na = len(srcs)

    def body(*refs):
        src_refs, land_refs = refs[:na], refs[na:2 * na]
        send_sems, recv_sems, local_sems = refs[2 * na:2 * na + 3]
        local, remote = _exchange_copies(src_refs, land_refs, send_sems, recv_sems, local_sems, gather)
        for cp in remote:
            cp.wait_send()
            cp.wait_recv()
        for cp in local:
            cp.wait()

    hbm = lambda t: pltpu.HBM(t.shape, t.dtype)
    out = pl.pallas_call(
        body, name=name, out_shape=(*[hbm(t) for t in srcs], *[hbm(t) for t in lands]),
        in_specs=[_HBM] * (2 * na) + [_SEM] * 3 + [pl.BlockSpec(memory_space=pl.ANY)],
        out_specs=tuple([_HBM] * (2 * na)), input_output_aliases={i: i for i in range(2 * na)},
        compiler_params=pltpu.CompilerParams(has_side_effects=_EFFECT),
    )(*srcs, *lands, *sems, after)
    return out[na:]


def _block_diag(w):
    h, a, b = w.shape
    eye = jnp.eye(h, dtype=w.dtype)
    return (w[:, :, None, :] * eye[:, None, :, None]).reshape(h * a, h * b)


def _block_diag_extract(m, h):
    a, b = m.shape[0] // h, m.shape[1] // h
    return jnp.stack([m[i * a:(i + 1) * a, i * b:(i + 1) * b] for i in range(h)], axis=0)


def _block_diag_take(m, h):
    a, b = m.shape[0] // h, m.shape[1] // h
    eye = jnp.eye(h, dtype=m.dtype)
    return (m.reshape(h, a, h, b) * eye[:, None, :, None]).sum(axis=2)


def _ffn_interleave(w):
    lead = w.shape[:-1]
    nb = D_FF // FFN_CB
    return jnp.swapaxes(w.reshape(*lead, 2, nb, FFN_CB), -3, -2).reshape(*lead, 2 * D_FF)


def _ffn_deinterleave(w):
    lead = w.shape[:-1]
    nb = D_FF // FFN_CB
    return jnp.swapaxes(w.reshape(*lead, nb, 2, FFN_CB), -3, -2).reshape(*lead, 2 * D_FF)


def _gather_full(gathered, axis):
    shape = list(gathered.shape[1:])
    shape[axis] *= N_DEV
    return jnp.moveaxis(gathered, 0, axis).reshape(shape)


def _scatter_blocks(full, axis):
    shape = list(full.shape)
    shape[axis:axis + 1] = [N_DEV, shape[axis] // N_DEV]
    return jnp.moveaxis(full.reshape(shape), axis, 0)


def _pad_to(flat, mult):
    pad = (-flat.shape[-1]) % mult
    if pad:
        flat = jnp.concatenate([flat, jnp.zeros(flat.shape[:-1] + (pad,), flat.dtype)], axis=-1)
    return flat


def _layer_fwd(h_in, h_in_t, w, cos, sin, l, dep, get_ffn, target=None):
    tag = "l%d_" % l
    proj = _mm_nn(h_in, w['w_in'], 512, D_IN, tag + "proj", dep=dep)
    qkv = _rope_fwd(proj, cos, sin, tag + "rope")
    outs, lses = [], []
    for d, qv in zip(DILATIONS, qkv):
        o, ls = _attn_fwd(qv, d, tag + "attn_d%d" % d)
        outs.append(o)
        lses.append(ls)
    lru, *lru_saved = _lru_fwd(proj, w['lru_conv_w'], w['lru_conv_b'], w['lru_wr'], w['lru_br'], w['lru_wi'],
                               w['lru_bi'], w['lru_lambda'], tag + "lru")
    x_re, x_im, y_acc = _s5_scan_fwd(proj, w['s5_bb_re'], w['s5_bb_im'], w['s5_lam_re'], w['s5_lam_im'],
                                     w['s5_cc_re'], w['s5_cc_im'], tag + "s5_scan")
    s5, y_pre = _s5_out_fwd(proj, y_acc, w['s5_d'], w['s5_w_glu'], w['s5_b_glu'], tag + "s5_out")
    mixed, mixed_t, attn_o, attn_lse = _mix_fwd(outs, lses, lru, s5, w['mix_norm_g'], tag + "mix")
    r1, h1, h1_t = _proj_ln(mixed, w['w_out'], h_in, w['ln1_g'], w['ln1_b'], tag + "out_ln1")
    w['w_up_g'], w['w_down'], ffn_dep = get_ffn(l, h1)
    up, act, act_t = _ffn_up_act(h1, w['w_up_g'], w['ffn_conv_w'], w['ffn_conv_b'], tag + "up_act", dep=ffn_dep)
    r2, out_a, out_b = _proj_ln(act, w['w_down'], h1, w['ln2_g'], w['ln2_b'], tag + "down_ln2", target=target)
    saved = dict(h_in_t=h_in_t, proj=proj, qkv=qkv, lru=lru, lru_saved=lru_saved, x_re=x_re, x_im=x_im,
                 y_pre=y_pre, s5=s5, mixed_t=mixed_t, attn_o=attn_o, attn_lse=attn_lse, r1=r1, h1_t=h1_t, up=up,
                 act_t=act_t, r2=r2)
    return out_a, out_b, saved


def _layer_bwd_ffn(dh2, sv, w, l, dep=None):
    tag = "l%d_" % l
    g = {}
    dr2, g['ln2_g'], g['ln2_b'] = _ln_bwd(sv['r2'], dh2, w['ln2_g'], tag + "ln2_bwd", dep=dep)
    g['w_down'] = _mm_dw(sv['act_t'], dr2, 1024, D_MODEL, 1024, tag + "down_dw")
    dup, dh1, dcw_parts, dcb_parts = _ffn_bwd(sv['up'], dr2, w['w_down'], w['w_up_g'], w['ffn_conv_w'],
                                              w['ffn_conv_b'], tag + "ffn_bwd")
    g['ffn_conv_w'] = dcw_parts.sum(axis=0)
    g['ffn_conv_b'] = dcb_parts.sum(axis=0)
    g['w_up_g'] = _mm_up_dw(sv['h1_t'], dup, tag + "up_dw")
    return dh1, g


def _layer_bwd_mix(dh1, sv, w, cos, sin, l, dep, g_ffn, after_out_grad, after_small_grads, after_in_grad):
    tag = "l%d_" % l
    g = {}
    dr1, g['ln1_g'], g['ln1_b'] = _ln_bwd(sv['r1'], dh1, w['ln1_g'], tag + "ln1_bwd", dep=dep)
    g['w_out'] = _mm_dw(sv['mixed_t'], dr1, 1024, D_MODEL, 1024, tag + "out_dw")
    d_o, dlru, ds5, g['mix_norm_g'] = _mix_bwd(dr1, w['w_out'], sv['attn_o'][0], sv['lru'], sv['s5'],
                                               w['mix_norm_g'], tag + "mix_bwd",
                                               dep=after_out_grad(l, g['w_out']))
    dy, dud, g['s5_d'], g['s5_w_glu'], g['s5_b_glu'] = _s5_out_bwd(
        sv['proj'], sv['y_pre'], ds5, w['s5_d'], w['s5_w_glu'], w['s5_b_glu'], tag + "s5_out_bwd")
    du, g['s5_lam_re'], g['s5_lam_im'], g['s5_bb_re'], g['s5_bb_im'], g['s5_cc_re'], g['s5_cc_im'] = \
        _s5_scan_bwd(sv['proj'], dy, dud, sv['x_re'], sv['x_im'], w['s5_bb_re'], w['s5_bb_im'],
                     w['s5_lam_re'], w['s5_lam_im'], w['s5_cc_re'], w['s5_cc_im'], tag + "s5_scan_bwd")
    (dxr, dgate, g['lru_conv_w'], g['lru_conv_b'], g['lru_wr'], g['lru_br'], g['lru_wi'], g['lru_bi'],
     g['lru_lambda']) = _lru_bwd(sv['proj'], dlru, *sv['lru_saved'], w['lru_conv_w'], w['lru_conv_b'], w['lru_wr'],
                                 w['lru_br'], w['lru_wi'], w['lru_bi'], w['lru_lambda'], tag + "lru_bwd")
    token = after_small_grads(l, _finish_layer_grads({**g_ffn, **g}, w, l))
    dqkv = [_attn_bwd(sv['qkv'][b], sv['attn_o'][b], d_o[b], sv['attn_lse'][b], d, tag + "attn_bwd_d%d" % d,
                      dep=token if b == 0 else None)
            for b, d in enumerate(DILATIONS)]
    dproj = _dproj_assemble(dqkv, dxr, dgate, du, cos, sin, tag + "dproj")
    g_in = _mm_dw(sv['h_in_t'], dproj, 1024, D_IN, 1024, tag + "in_dw")
    return _mm_nt(dproj, w['w_in'], 512, D_MODEL, tag + "in_dx", add=dr1, add_scale=ALPHA,
                  dep=after_in_grad(l, g_in))


def _s5_rep(a):
    return jnp.repeat(a, S5_C, axis=0)


def _prepare_layer(p, l):
    w = {}
    for n in ('w_in', 'w_out', 's5_w_glu'):
        w[n] = p[n].astype(BF16)
    w['ffn_conv_w'] = _ffn_interleave(p['ffn_conv_w'])
    w['ffn_conv_b'] = _ffn_interleave(p['ffn_conv_b'])[None, :]
    w['lru_conv_w'] = p['lru_conv_w']
    for n in ('lru_conv_b', 'lru_br', 'lru_bi', 'lru_lambda', 's5_b_glu', 'mix_norm_g',
              'ln1_g', 'ln1_b', 'ln2_g', 'ln2_b'):
        w[n] = p[n][None, :]
    w['lru_wr'] = _block_diag(p['lru_wr']).astype(BF16)
    w['lru_wi'] = _block_diag(p['lru_wi']).astype(BF16)
    w['s5_d'] = p['s5_d'].reshape(1, S5_W)
    disc_in = (_s5_rep(p['s5_a_re']), _s5_rep(p['s5_a_im']),
               _s5_rep(jnp.broadcast_to(p['s5_log_step'][:, None], (S5_G, S5_P))),
               jnp.swapaxes(p['s5_b_re'], 1, 2).reshape(S5_W, S5_P),
               jnp.swapaxes(p['s5_b_im'], 1, 2).reshape(S5_W, S5_P))
    ab_re, ab_im, bb_re, bb_im = _s5_disc_fwd(*disc_in, "l%d_s5_disc" % l)
    w['s5_disc_in'] = disc_in
    w['s5_lam_re'] = ab_re.reshape(S5_G, S5_C, S5_P)[:, 0, :].reshape(1, S5_STATES)
    w['s5_lam_im'] = ab_im.reshape(S5_G, S5_C, S5_P)[:, 0, :].reshape(1, S5_STATES)
    w['s5_bb_re'] = _block_diag(bb_re.reshape(S5_G, S5_C, S5_P)).astype(BF16)
    w['s5_bb_im'] = _block_diag(bb_im.reshape(S5_G, S5_C, S5_P)).astype(BF16)
    w['s5_cc_re'] = _block_diag(jnp.swapaxes(p['s5_c_re'], 1, 2)).astype(BF16)
    w['s5_cc_im'] = _block_diag(jnp.swapaxes(p['s5_c_im'], 1, 2)).astype(BF16)
    return w


def _finish_layer_grads(g, w, l):
    out = {}
    for n in ('s5_w_glu', 'lru_conv_w'):
        out[n] = g[n]
    out['ffn_conv_w'] = _ffn_deinterleave(g['ffn_conv_w'])
    out['ffn_conv_b'] = _ffn_deinterleave(g['ffn_conv_b'])[0]
    for n in ('lru_conv_b', 'lru_br', 'lru_bi', 'lru_lambda', 's5_b_glu', 'mix_norm_g',
              'ln1_g', 'ln1_b', 'ln2_g', 'ln2_b'):
        out[n] = g[n][0]
    out['lru_wr'] = _block_diag_extract(g['lru_wr'], LRU_W // HEAD)
    out['lru_wi'] = _block_diag_extract(g['lru_wi'], LRU_W // HEAD)
    out['s5_d'] = g['s5_d'].reshape(S5_G, S5_C)
    out['s5_c_re'] = jnp.swapaxes(_block_diag_take(g['s5_cc_re'], S5_G), 1, 2)
    out['s5_c_im'] = jnp.swapaxes(_block_diag_take(g['s5_cc_im'], S5_G), 1, 2)
    rep = lambda v: _s5_rep(v.reshape(S5_G, S5_P)) * (1.0 / S5_C)
    cts = (rep(g['s5_lam_re']), rep(g['s5_lam_im']),
           _block_diag_take(g['s5_bb_re'], S5_G).reshape(S5_W, S5_P),
           _block_diag_take(g['s5_bb_im'], S5_G).reshape(S5_W, S5_P))
    da_re, da_im, dls, dbt_re, dbt_im = _s5_disc_bwd(*w['s5_disc_in'], cts, "l%d_s5_disc_bwd" % l)
    out['s5_a_re'] = da_re.reshape(S5_G, S5_C, S5_P).sum(axis=1)
    out['s5_a_im'] = da_im.reshape(S5_G, S5_C, S5_P).sum(axis=1)
    out['s5_log_step'] = dls.reshape(S5_G, S5_C * S5_P).sum(axis=1)
    out['s5_b_re'] = jnp.swapaxes(dbt_re.reshape(S5_G, S5_C, S5_P), 1, 2)
    out['s5_b_im'] = jnp.swapaxes(dbt_im.reshape(S5_G, S5_C, S5_P), 1, 2)
    return out


def _run_step(x, target, get_layer, get_ffn, after_ffn_grads, after_out_grad, after_small_grads, after_in_grad):
    cos, sin = _rope_tables(x.shape[0])
    h, h_t = x, _transpose_bf16(x, "x_transpose")
    ws, saved = [], []
    for l in range(DEPTH):
        p, dep = get_layer(l, h)
        ws.append(_prepare_layer(p, l))
        h, h_t, sv = _layer_fwd(h, h_t, ws[l], cos, sin, l, dep, get_ffn, target if l == DEPTH - 1 else None)
        saved.append(sv)
    dh, loss_vec = h, h_t
    dep = None
    for l in reversed(range(DEPTH)):
        dh1, g = _layer_bwd_ffn(dh, saved[l], ws[l], l, dep)
        dep = after_ffn_grads(l, g)
        dh = _layer_bwd_mix(dh1, saved[l], ws[l], cos, sin, l, dep, g, after_out_grad, after_small_grads,
                            after_in_grad)
        dep = None
    return loss_vec[0, 0], dh


def _local_step(x, target, layers):
    grads = [{} for _ in range(DEPTH)]

    def ffn(l, h1):
        return layers[l]['w_up_g'].astype(BF16), layers[l]['w_down'].astype(BF16), None

    def keep_ffn(l, g):
        grads[l].update(w_up_g=g['w_up_g'], w_down=g['w_down'])

    def keep_small(l, g):
        grads[l].update(g)

    loss, dx = _run_step(x, target, lambda l, h: (layers[l], None), ffn, keep_ffn,
                         lambda l, g: grads[l].update(w_out=g), keep_small, lambda l, g: grads[l].update(w_in=g))
    return loss, dx, grads


def kernel(x, w_in, lru_conv_w, lru_conv_b, lru_wr, lru_br, lru_wi, lru_bi, lru_lambda, s5_a_re, s5_a_im, s5_b_re, s5_b_im, s5_c_re, s5_c_im, s5_d, s5_log_step, s5_w_glu, s5_b_glu, mix_norm_g, w_out, ln1_g, ln1_b, w_up, ffn_conv_w, ffn_conv_b, w_down, ln2_g, ln2_b, loss_target, m_w_in, m_lru_conv_w, m_lru_conv_b, m_lru_wr, m_lru_br, m_lru_wi, m_lru_bi, m_lru_lambda, m_s5_a_re, m_s5_a_im, m_s5_b_re, m_s5_b_im, m_s5_c_re, m_s5_c_im, m_s5_d, m_s5_log_step, m_s5_w_glu, m_s5_b_glu, m_mix_norm_g, m_w_out, m_ln1_g, m_ln1_b, m_w_up, m_ffn_conv_w, m_ffn_conv_b, m_w_down, m_ln2_g, m_ln2_b, v_w_in, v_lru_conv_w, v_lru_conv_b, v_lru_wr, v_lru_br, v_lru_wi, v_lru_bi, v_lru_lambda, v_s5_a_re, v_s5_a_im, v_s5_b_re, v_s5_b_im, v_s5_c_re, v_s5_c_im, v_s5_d, v_s5_log_step, v_s5_w_glu, v_s5_b_glu, v_mix_norm_g, v_w_out, v_ln1_g, v_ln1_b, v_w_up, v_ffn_conv_w, v_ffn_conv_b, v_w_down, v_ln2_g, v_ln2_b):
    args = locals()
    wl = {n: args[n] for n in WEIGHTS}
    ml = {n: args['m_' + n] for n in WEIGHTS}
    vl = {n: args['v_' + n] for n in WEIGHTS}

    small_sizes = [int(wl[n].size) for n in SMALL_SHARDED]
    small_flat = _pad_to(jnp.concatenate([wl[n].reshape(-1) for n in SMALL_SHARDED]), 8 * 1024)
    small_all, = _all_gather([small_flat.reshape(-1, 1024)], "gather_small")
    small_all = small_all.reshape(N_DEV, -1)
    small_full, off = {}, 0
    for n, sz in zip(SMALL_SHARDED, small_sizes):
        small_full[n] = _gather_full(small_all[:, off:off + sz].reshape((N_DEV,) + wl[n].shape), SHARD_AXIS[n])
        off += sz
    def mixer_params(l, gathered):
        g_in, g_out = gathered
        p = {n: wl[n][l] for n in REPLICATED}
        p.update({n: small_full[n][l] for n in SMALL_SHARDED})
        p['w_in'] = _gather_full(g_in, 1)
        p['w_out'] = g_out.reshape(D_MODEL, D_MODEL)
        return p

    mix_names, ffn_names = ('w_in', 'w_out'), ('w_up', 'w_down')
    shards = lambda names, l: [wl[n][l].astype(BF16) for n in names]
    mix0 = _all_gather(shards(mix_names, 0), "gather_mix_l0")
    gathers = {}
    gathers[0, 'ffn'], ffn0_token = _exchange_start(shards(ffn_names, 0), True, "gather_ffn_l0_start", dep=mix0[0])
    def get_layer(l, h):
        if l == 0:
            return mixer_params(0, mix0), ffn0_token
        return mixer_params(1, _exchange_wait(gathers[1, 'mix'], True, h, "gather_mix_l1_wait")), None

    def get_ffn(l, h1):
        g_up, g_down = _exchange_wait(gathers[l, 'ffn'], True, h1, "gather_ffn_l%d_wait" % l)
        token = None
        if l == 0:
            gathers[1, 'mix'], token = _exchange_start(shards(mix_names, 1), True, "gather_mix_l1_start", dep=g_up)
            gathers[1, 'ffn'], token = _exchange_start(shards(ffn_names, 1), True, "gather_ffn_l1_start", dep=token)
        return g_up, g_down.reshape(D_FF, D_MODEL), token

    scatters = {}

    def after_ffn_grads(l, g):
        send = [g['w_up_g'], g['w_down'].reshape(N_DEV, D_FF // N_DEV, D_MODEL)]
        scatters[l, 'ffn'], token = _exchange_start(send, False, "scatter_ffn_l%d_start" % l)
        return token

    def after_out_grad(l, g_out):
        send = [g_out.reshape(N_DEV, D_MODEL // N_DEV, D_MODEL)]
        scatters[l, 'out'], token = _exchange_start(send, False, "scatter_out_l%d_start" % l)
        return token

    def after_in_grad(l, g_in):
        send = _scatter_blocks(g_in, 1)
        if l == 0:
            send = send.astype(BF16)
        scatters[l, 'in'], token = _exchange_start([send], False, "scatter_in_l%d_start" % l)
        return token

    def after_small_grads(l, g):
        rep = [g[n][None] for n in REPLICATED]
        shd = [_scatter_blocks(g[n], SHARD_AXIS[n] - 1)[:, None] for n in SMALL_SHARDED]
        scatters[l, 'rep'], token = _exchange_start(rep, True, "gather_rep_grads_l%d_start" % l)
        scatters[l, 'small'], token = _exchange_start(shd, False, "scatter_small_l%d_start" % l, dep=token)
        return token

    loss_local, grad_x = _run_step(x[0], loss_target[0], get_layer, get_ffn, after_ffn_grads, after_out_grad,
                                   after_small_grads, after_in_grad)
    loss = lax.psum(loss_local, AXES)

    results = {}
    big_prev = {n: None for n in BIG}

    def finish_big(l, part, names, after):
        landed = _exchange_wait(scatters[l, part], False, after, "scatter_%s_l%d_wait" % (part, l))
        for n, ld in zip(names, landed):
            big_prev[n] = _adamw_sum(ld, wl[n], ml[n], vl[n], l, big_prev[n], "adamw_%s_l%d" % (n, l))

    for l, part, names in ((1, 'ffn', ffn_names), (1, 'out', ('w_out',)), (1, 'in', ('w_in',)),
                           (0, 'ffn', ffn_names), (0, 'out', ('w_out',))):
        finish_big(l, part, names, grad_x)

    kinds = ('grad', 'delta', 'm', 'v')
    landed = [dict(zip(REPLICATED + SMALL_SHARDED,
                       list(_exchange_wait(scatters[l, 'rep'], True, grad_x, "gather_rep_grads_l%d_wait" % l)) +
                       list(_exchange_wait(scatters[l, 'small'], False, grad_x, "scatter_small_l%d_wait" % l))))
              for l in range(DEPTH)]
    matrices = ['lru_wr', 'lru_wi', 's5_a_re', 's5_a_im', 's5_c_re', 's5_c_im', 's5_d']
    widest = ['s5_b_re', 's5_b_im']
    vectors = [n for n in REPLICATED + SMALL_SHARDED if n not in matrices + widest]
    last = None
    for tag, names in (("vectors", vectors), ("matrices", matrices), ("s5_b", widest)):
        res = _adamw_many([[landed[l][n] for n in names] for l in range(DEPTH)], [wl[n] for n in names],
                          [ml[n] for n in names], [vl[n] for n in names], "adamw_" + tag)
        for kind, arrs in zip(kinds, res):
            for n, a in zip(names, arrs):
                results[kind, n] = a
        last = res[0][0]
    finish_big(0, 'in', ('w_in',), last)
    for n in BIG:
        results['grad', n], results['delta', n], results['m', n], results['v', n] = big_prev[n]

    out = [loss, grad_x[None]]
    for kind in kinds:
        out.extend(results[kind, n] for n in WEIGHTS)
    return tuple(out)
```

```python
import math

import jax
import jax.numpy as jnp
from jax import lax
from jax.experimental import pallas as pl
from jax.experimental.pallas import tpu as pltpu

F32 = jnp.float32
BF16 = jnp.bfloat16

N_DEV = 8
DEPTH = 2
D_MODEL = 1024
ATTN_W = 384
LRU_W = 384
S5_W = 256
D_IN = 2176
D_FF = 3072
HEAD = 64
ATTN_BLK = 128
ATTN_TILE = 1024
DILATIONS = (1, 4, 16)
S5_G = 16
S5_P = 64
S5_C = 16
S5_STATES = S5_G * S5_P
LRU_C = 8.0
LRU_CONV = 4
FFN_CONV = 3
ROPE_THETA = 10000.0
ALPHA = (2 * DEPTH) ** 0.25
LN_EPS = 1e-5
RMS_EPS = 1e-6
ADAM_LR, ADAM_B1, ADAM_B2, ADAM_EPS, ADAM_WD, ADAM_STEP = 0.001, 0.9, 0.999, 1e-8, 0.01, 10

LANE = 128
SCAN_T = 256
S5_BLK = 256
FFN_CB = 2 * D_FF // N_DEV
VMEM_LIMIT = 56 * 1024 * 1024

AXES = ("x", "y", "c")

WEIGHTS = ['w_in', 'lru_conv_w', 'lru_conv_b', 'lru_wr', 'lru_br', 'lru_wi', 'lru_bi', 'lru_lambda',
           's5_a_re', 's5_a_im', 's5_b_re', 's5_b_im', 's5_c_re', 's5_c_im', 's5_d', 's5_log_step',
           's5_w_glu', 's5_b_glu', 'mix_norm_g', 'w_out', 'ln1_g', 'ln1_b', 'w_up', 'ffn_conv_w',
           'ffn_conv_b', 'w_down', 'ln2_g', 'ln2_b']
SHARD_AXIS = {'w_in': 2, 'lru_conv_w': 2, 's5_w_glu': 1, 'w_out': 1, 'w_up': 2, 'ffn_conv_w': 2, 'w_down': 1}
BIG = ['w_in', 'w_out', 'w_up', 'w_down']
SMALL_SHARDED = ['lru_conv_w', 'ffn_conv_w', 's5_w_glu']
REPLICATED = [n for n in WEIGHTS if n not in SHARD_AXIS]


def _cparams(sem=None):
    return pltpu.CompilerParams(dimension_semantics=sem, vmem_limit_bytes=VMEM_LIMIT)


def _ffn_dev(jb):
    return jb // 2 + (N_DEV // 2) * (jb % 2)


def _gelu(x):
    c = math.sqrt(2.0 / math.pi)
    t = jnp.tanh(c * (x + 0.044715 * (x * x * x)))
    return 0.5 * x * (1.0 + t)


def _gelu_grad(x):
    c = math.sqrt(2.0 / math.pi)
    x2 = x * x
    t = jnp.tanh(c * (x + 0.044715 * (x2 * x)))
    return 0.5 * (1.0 + t) + 0.5 * x * (1.0 - t * t) * (c * (1.0 + 3.0 * 0.044715 * x2))


def _sigmoid(x):
    return 1.0 / (1.0 + jnp.exp(-x))


def _log1p(x):
    u = 1.0 + x
    d = u - 1.0
    return jnp.where(d == 0.0, x, jnp.log(u) * (x / jnp.where(d == 0.0, 1.0, d)))


def _softplus(x):
    return jnp.maximum(x, 0.0) + _log1p(jnp.exp(-jnp.abs(x)))


def _expm1(x):
    return jnp.tanh(0.5 * x) * (jnp.exp(x) + 1.0)


def _dot(a, b):
    return jnp.dot(a.astype(BF16), b.astype(BF16), preferred_element_type=F32)


def _dot_nt(a, b):
    return lax.dot_general(a.astype(BF16), b.astype(BF16), (((1,), (1,)), ((), ())),
                           preferred_element_type=F32)


def _dot_tn(a, b):
    return lax.dot_general(a.astype(BF16), b.astype(BF16), (((0,), (0,)), ((), ())),
                           preferred_element_type=F32)


def _rows(shape):
    return lax.broadcasted_iota(jnp.int32, shape, 0)


def _shift_down_prev(x, s, prev8):
    if s == 0:
        return x
    t, l = x.shape
    r = pltpu.roll(x, s, axis=0)
    pr = pltpu.roll(prev8, s, axis=0)
    pad = jnp.concatenate([pr, jnp.zeros((t - 8, l), x.dtype)], axis=0)
    return jnp.where(_rows(x.shape) < s, pad, r)


def _shift_up_next(x, s, next8):
    if s == 0:
        return x
    t, l = x.shape
    r = pltpu.roll(x, t - s, axis=0)
    nx = pltpu.roll(next8, 8 - s, axis=0)
    pad = jnp.concatenate([jnp.zeros((t - 8, l), x.dtype), nx], axis=0)
    return jnp.where(_rows(x.shape) >= t - s, pad, r)


SUB = 8


def _tile_shift(x, s, fill, reverse):
    t = x.shape[0]
    pos = _rows(x.shape) & (SUB - 1)
    if reverse:
        return jnp.where(pos < SUB - s, pltpu.roll(x, t - s, axis=0), fill)
    return jnp.where(pos >= s, pltpu.roll(x, s, axis=0), fill)


def _scan_chunk(a, x, carry, reverse=False):
    s = 1
    while s < SUB:
        x = x + a * _tile_shift(x, s, 0.0, reverse)
        a = a * _tile_shift(a, s, 1.0, reverse)
        s *= 2
    nv = x.shape[0] // SUB
    out = [None] * nv
    for v in (reversed(range(nv)) if reverse else range(nv)):
        rows = slice(v * SUB, (v + 1) * SUB)
        out[v] = x[rows, :] + a[rows, :] * carry
        carry = out[v][0:1, :] if reverse else out[v][SUB - 1:SUB, :]
    return jnp.concatenate(out, axis=0)


def _cmul(ar, ai, br, bi):
    return ar * br - ai * bi, ar * bi + ai * br


def _cscan_consts(lr, li, reverse):
    pows = [(lr, li)]
    for _ in range(2):
        pows.append(_cmul(*pows[-1], *pows[-1]))
    rows = [(lr, li)]
    for _ in range(SUB - 1):
        rows.append(_cmul(*rows[-1], lr, li))
    if reverse:
        rows = rows[::-1]
    return pows, (jnp.concatenate([r for r, _ in rows], axis=0), jnp.concatenate([i for _, i in rows], axis=0))


def _cscan_chunk(xr, xi, consts, carry, reverse=False):
    pows, (p8r, p8i) = consts
    s = 1
    for pr, pi in pows:
        sr = _tile_shift(xr, s, 0.0, reverse)
        si = _tile_shift(xi, s, 0.0, reverse)
        xr, xi = xr + pr * sr - pi * si, xi + pr * si + pi * sr
        s *= 2
    nv = xr.shape[0] // SUB
    out_r, out_i = [None] * nv, [None] * nv
    cr, ci = carry
    for v in (reversed(range(nv)) if reverse else range(nv)):
        rows = slice(v * SUB, (v + 1) * SUB)
        out_r[v] = xr[rows, :] + p8r * cr - p8i * ci
        out_i[v] = xi[rows, :] + p8r * ci + p8i * cr
        edge = slice(0, 1) if reverse else slice(SUB - 1, SUB)
        cr, ci = out_r[v][edge, :], out_i[v][edge, :]
    return jnp.concatenate(out_r, axis=0), jnp.concatenate(out_i, axis=0)


def _dep_args(dep):
    return ([], []) if dep is None else ([pl.BlockSpec(memory_space=pl.ANY)], [dep])


def _mm_nt(a, w, tm, tn, name, add=None, add_scale=1.0, dep=None):
    m, k = a.shape
    n = w.shape[0]

    def body(a_ref, w_ref, *rest):
        o_ref = rest[-1]
        if add is None:
            o_ref[...] = _dot_nt(a_ref[...], w_ref[...])
        else:
            o_ref[...] = _dot_nt(a_ref[...], w_ref[...]) + add_scale * rest[0][...]

    in_specs = [pl.BlockSpec((tm, k), lambda j, i: (i, 0)), pl.BlockSpec((tn, k), lambda j, i: (j, 0))]
    args = [a, w]
    if add is not None:
        in_specs.append(pl.BlockSpec((tm, tn), lambda j, i: (i, j)))
        args.append(add)
    dep_specs, dep_ops = _dep_args(dep)
    return pl.pallas_call(
        body, out_shape=jax.ShapeDtypeStruct((m, n), F32), grid=(n // tn, m // tm),
        in_specs=in_specs + dep_specs, out_specs=pl.BlockSpec((tm, tn), lambda j, i: (i, j)), name=name,
        compiler_params=_cparams(("parallel", "parallel")))(*args, *dep_ops)


def _mm_dw(at, b, tm, tn, ts, name):
    m, s = at.shape
    n = b.shape[1]

    def body(a_ref, b_ref, o_ref):
        @pl.when(pl.program_id(2) == 0)
        def _():
            o_ref[...] = jnp.zeros_like(o_ref)
        o_ref[...] += _dot(a_ref[...], b_ref[...])

    return pl.pallas_call(
        body, out_shape=jax.ShapeDtypeStruct((m, n), F32), grid=(m // tm, n // tn, s // ts),
        in_specs=[pl.BlockSpec((tm, ts), lambda i, j, k: (i, k)), pl.BlockSpec((ts, tn), lambda i, j, k: (k, j))],
        out_specs=pl.BlockSpec((tm, tn), lambda i, j, k: (i, j)), name=name,
        compiler_params=_cparams(("parallel", "parallel", "arbitrary")))(at, b)


def _transpose_bf16(x, name):
    s, d = x.shape
    tm = 512

    def body(x_ref, o_ref):
        o_ref[...] = x_ref[...].T.astype(BF16)

    return pl.pallas_call(
        body, out_shape=jax.ShapeDtypeStruct((d, s), BF16), grid=(s // tm,),
        in_specs=[pl.BlockSpec((tm, d), lambda i: (i, 0))], out_specs=pl.BlockSpec((d, tm), lambda i: (0, i)),
        name=name, compiler_params=_cparams(("parallel",)))(x)


def _mm_up_dw(ht, dup, name):
    d, s = ht.shape

    def body(a_ref, b_ref, o_ref):
        o_ref[...] = _dot(a_ref[...], b_ref[...])

    return pl.pallas_call(
        body, out_shape=jax.ShapeDtypeStruct((N_DEV, d, FFN_CB), F32), grid=(N_DEV,),
        in_specs=[pl.BlockSpec((d, s), lambda j: (0, 0)), pl.BlockSpec((s, FFN_CB), lambda j: (0, j))],
        out_specs=pl.BlockSpec((None, d, FFN_CB), lambda j: (_ffn_dev(j), 0, 0)), name=name,
        compiler_params=_cparams(("parallel",)))(ht, dup)


def _layer_norm(r, g, b):
    mu = jnp.mean(r, axis=-1, keepdims=True)
    xc = r - mu
    var = jnp.mean(xc * xc, axis=-1, keepdims=True)
    return xc * lax.rsqrt(var + LN_EPS) * g + b


def _proj_ln(a, w, resid, g, bias, name, transposed=True, target=None):
    s, k = a.shape
    d = w.shape[1]
    tm = 512

    def body(a_ref, w_ref, x_ref, g_ref, bias_ref, *rest):
        r = ALPHA * x_ref[...] + _dot(a_ref[...], w_ref[...])
        h = _layer_norm(r, g_ref[...], bias_ref[...])
        if target is None:
            r_ref, h_ref = rest[0], rest[1]
            h_ref[...] = h
            if transposed:
                rest[2][...] = h.T.astype(BF16)
        else:
            t_ref, r_ref, dy_ref, l_ref = rest

            @pl.when(pl.program_id(0) == 0)
            def _():
                l_ref[...] = jnp.zeros_like(l_ref)
            e = h - t_ref[...]
            dy_ref[...] = e * (1.0 / d)
            part = 0.5 * jnp.sum(jnp.mean(e * e, axis=-1, keepdims=True), axis=0, keepdims=True)
            l_ref[...] += jnp.broadcast_to(part, l_ref.shape)
        r_ref[...] = r

    row = pl.BlockSpec((tm, d), lambda i: (i, 0))
    vec = pl.BlockSpec((1, d), lambda i: (0, 0))
    in_specs = [pl.BlockSpec((tm, k), lambda i: (i, 0)), pl.BlockSpec((k, d), lambda i: (0, 0)), row, vec, vec]
    args = [a, w, resid, g, bias]
    shapes = [jax.ShapeDtypeStruct((s, d), F32), jax.ShapeDtypeStruct((s, d), F32)]
    specs = [row, row]
    if target is not None:
        in_specs.append(row)
        args.append(target)
        shapes.append(jax.ShapeDtypeStruct((1, LANE), F32))
        specs.append(pl.BlockSpec((1, LANE), lambda i: (0, 0)))
    elif transposed:
        shapes.append(jax.ShapeDtypeStruct((d, s), BF16))
        specs.append(pl.BlockSpec((d, tm), lambda i: (0, i)))
    return pl.pallas_call(
        body, out_shape=tuple(shapes), grid=(s // tm,), in_specs=in_specs, out_specs=tuple(specs), name=name,
        compiler_params=_cparams(("arbitrary",) if target is not None else ("parallel",)))(*args)


def _layer_norm_bwd(r, dh, g):
    mu = jnp.mean(r, axis=-1, keepdims=True)
    xc = r - mu
    var = jnp.mean(xc * xc, axis=-1, keepdims=True)
    rstd = lax.rsqrt(var + LN_EPS)
    xh = xc * rstd
    dxh = dh * g
    m1 = jnp.mean(dxh, axis=-1, keepdims=True)
    m2 = jnp.mean(dxh * xh, axis=-1, keepdims=True)
    return (rstd * (dxh - m1 - xh * m2), jnp.sum(dh * xh, axis=0, keepdims=True),
            jnp.sum(dh, axis=0, keepdims=True))


def _ln_bwd(r, dh, g, name, dep=None):
    s, d = r.shape
    tm = 512

    def body(r_ref, dh_ref, g_ref, *rest):
        dr_ref, dg_ref, db_ref = rest[-3:]

        @pl.when(pl.program_id(0) == 0)
        def _():
            dg_ref[...] = jnp.zeros_like(dg_ref)
            db_ref[...] = jnp.zeros_like(db_ref)
        dr_ref[...], dg_rows, db_rows = _layer_norm_bwd(r_ref[...], dh_ref[...], g_ref[...])
        dg_ref[...] += dg_rows
        db_ref[...] += db_rows

    row = pl.BlockSpec((tm, d), lambda i: (i, 0))
    vec = pl.BlockSpec((1, d), lambda i: (0, 0))
    dep_specs, dep_ops = _dep_args(dep)
    return pl.pallas_call(
        body, out_shape=(jax.ShapeDtypeStruct((s, d), F32), jax.ShapeDtypeStruct((1, d), F32),
                         jax.ShapeDtypeStruct((1, d), F32)),
        grid=(s // tm,), in_specs=[row, row, vec] + dep_specs, out_specs=(row, vec, vec), name=name,
        compiler_params=_cparams(("arbitrary",)))(r, dh, g, *dep_ops)


def _rope_tables(s):
    half = HEAD // 2
    pos = jnp.arange(s, dtype=F32)
    inv = ROPE_THETA ** (-jnp.arange(half, dtype=F32) * 2.0 / HEAD)
    ang = pos[:, None] * inv[None, :]
    cos, sin = jnp.cos(ang), jnp.sin(ang)
    cos = jnp.concatenate([cos, cos, cos, cos], axis=1)
    sin = jnp.concatenate([-sin, sin, -sin, sin], axis=1)
    return cos, sin


def _rotate(x, cos, sin):
    lane = lax.broadcasted_iota(jnp.int32, x.shape, 1)
    partner = jnp.where((lane % HEAD) < HEAD // 2, pltpu.roll(x, LANE - HEAD // 2, axis=1),
                        pltpu.roll(x, HEAD // 2, axis=1))
    return x * cos + partner * sin


def _class_rows(c, d, tm):
    return pl.ds(c, tm // d, stride=d) if d > 1 else pl.ds(0, tm)


def _dilated_spec(tm, d, w):
    return pl.BlockSpec((tm // d, d * w), lambda i: (i, 0))


def _token_scratch(tm, w):
    return pltpu.VMEM((w // LANE, tm, LANE), F32)


def _to_tokens(src_ref, dst3, d, tm):
    nj = dst3.shape[0]
    for cls in range(d):
        for j in range(nj):
            col = (cls * nj + j) * LANE
            dst3.at[j][_class_rows(cls, d, tm), :] = src_ref[:, col:col + LANE]


def _to_dilated(src3, dst_ref, d, tm):
    nj = src3.shape[0]
    for cls in range(d):
        for j in range(nj):
            col = (cls * nj + j) * LANE
            dst_ref[:, col:col + LANE] = src3.at[j][_class_rows(cls, d, tm), :].astype(dst_ref.dtype)


def _token_value(src3):
    return jnp.concatenate([src3[j] for j in range(src3.shape[0])], axis=1)


def _proj_rope(h, w_in, cos, sin, name, dep=None):
    s, d_model = h.shape
    tm = 512
    w = 3 * ATTN_W
    nj = w // LANE

    def body(h_ref, w_ref, c_ref, s_ref, *rest):
        p_ref, o_refs, rot = rest[-5], rest[-4:-1], rest[-1]
        y = _dot(h_ref[...], w_ref[...])
        p_ref[...] = y
        c, sn = c_ref[...], s_ref[...]
        for j in range(nj):
            x = y[:, j * LANE:(j + 1) * LANE]
            rot[j] = _rotate(x, c, sn) if j < 2 * ATTN_W // LANE else x
        for d, o_ref in zip(DILATIONS, o_refs):
            _to_dilated(rot, o_ref, d, tm)

    tab = pl.BlockSpec((tm, LANE), lambda i: (i, 0))
    dep_specs, dep_ops = _dep_args(dep)
    res = pl.pallas_call(
        body, out_shape=(jax.ShapeDtypeStruct((s, D_IN), F32),
                         *[jax.ShapeDtypeStruct((s // d, d * w), BF16) for d in DILATIONS]),
        grid=(s // tm,),
        in_specs=[pl.BlockSpec((tm, d_model), lambda i: (i, 0)), pl.BlockSpec((d_model, D_IN), lambda i: (0, 0)),
                  tab, tab] + dep_specs,
        out_specs=(pl.BlockSpec((tm, D_IN), lambda i: (i, 0)), *[_dilated_spec(tm, d, w) for d in DILATIONS]),
        scratch_shapes=[_token_scratch(tm, w)], name=name,
        compiler_params=_cparams(("parallel",)))(h, w_in, cos, sin, *dep_ops)
    return res[0], res[1:]


def _dproj_assemble(dqkv_list, dxr, dgate, du, cos, sin, name):
    s = dxr.shape[0]
    tm = 512
    nq = 3 * ATTN_W // LANE

    def body(*refs):
        br = refs[:9]
        dxr_ref, dg_ref, du_ref, c_ref, s_ref, o_ref = refs[9:15]
        tok = refs[15:]
        c, sn = c_ref[...], -s_ref[...]
        for part in range(3):
            for b, d in enumerate(DILATIONS[1:], start=1):
                _to_tokens(br[3 * b + part], tok[2 * part + b - 1], d, tm)
        for j in range(nq):
            part, jj = divmod(j, ATTN_W // LANE)
            x = br[part][:, jj * LANE:(jj + 1) * LANE] + tok[2 * part][jj] + tok[2 * part + 1][jj]
            if part < 2:
                x = _rotate(x, c, sn)
            o_ref[:, j * LANE:(j + 1) * LANE] = x.astype(BF16)
        o_ref[:, 3 * ATTN_W:3 * ATTN_W + LRU_W] = dxr_ref[...].astype(BF16)
        o_ref[:, 3 * ATTN_W + LRU_W:3 * ATTN_W + 2 * LRU_W] = dg_ref[...].astype(BF16)
        o_ref[:, 3 * ATTN_W + 2 * LRU_W:] = du_ref[...].astype(BF16)

    a_spec = pl.BlockSpec((tm, ATTN_W), lambda i: (i, 0))
    tab = pl.BlockSpec((tm, LANE), lambda i: (i, 0))
    ordered = [dqkv_list[b][p] for b in range(3) for p in range(3)]
    d_specs = [_dilated_spec(tm, d, ATTN_W) for d in DILATIONS for _ in range(3)]
    return pl.pallas_call(
        body, out_shape=jax.ShapeDtypeStruct((s, D_IN), BF16), grid=(s // tm,),
        in_specs=d_specs + [a_spec, a_spec, pl.BlockSpec((tm, S5_W), lambda i: (i, 0)), tab, tab],
        out_specs=pl.BlockSpec((tm, D_IN), lambda i: (i, 0)),
        scratch_shapes=[_token_scratch(tm, ATTN_W)] * 6, name=name,
        compiler_params=_cparams(("parallel",)))(*ordered, dxr, dgate, du, cos, sin)


def _attn_tiles(s, d):
    m = s // d
    tq = min(m, ATTN_TILE)
    return m, tq, tq // ATTN_BLK


def _band_mask(qb):
    qi = lax.broadcasted_iota(jnp.int32, (ATTN_BLK, 2 * ATTN_BLK), 0)
    ki = lax.broadcasted_iota(jnp.int32, (ATTN_BLK, 2 * ATTN_BLK), 1)
    dist = qi + ATTN_BLK - ki
    return (dist >= 0) & (dist <= ATTN_BLK) & ((ki >= ATTN_BLK) | (qb > 0))


def _head_cols(h):
    return (slice(h * HEAD, (h + 1) * HEAD), slice(ATTN_W + h * HEAD, ATTN_W + (h + 1) * HEAD),
            slice(2 * ATTN_W + h * HEAD, 2 * ATTN_W + (h + 1) * HEAD))


def _attn_fwd(qv, d, name):
    m = qv.shape[0]
    w3 = 3 * ATTN_W
    _, tq, n = _attn_tiles(m * d, d)
    scale = HEAD ** -0.5

    def body(x_ref, p_ref, o_ref, l_ref):
        b = pl.program_id(1)

        def block(i, first):
            r0 = 0 if first else pl.multiple_of(i * ATTN_BLK, ATTN_BLK)
            rows = pl.ds(r0, ATTN_BLK)
            valid = _band_mask(b * n + i)
            if not first:
                krows = pl.ds(pl.multiple_of(i * ATTN_BLK - ATTN_BLK, ATTN_BLK), 2 * ATTN_BLK)
            for h in range(ATTN_W // HEAD):
                qs, ks, vs = _head_cols(h)
                q = x_ref[rows, qs]
                if first:
                    k = jnp.concatenate([p_ref[:, ks], x_ref[0:ATTN_BLK, ks]], axis=0)
                    v = jnp.concatenate([p_ref[:, vs], x_ref[0:ATTN_BLK, vs]], axis=0)
                else:
                    k = x_ref[krows, ks]
                    v = x_ref[krows, vs]
                sc = jnp.where(valid, _dot_nt(q, k) * scale, -1e30)
                mx = jnp.max(sc, axis=-1, keepdims=True)
                p = jnp.exp(sc - mx)
                l = jnp.sum(p, axis=-1, keepdims=True)
                o_ref[rows, qs] = _dot(p, v) / l
                l_ref[rows, qs] = jnp.broadcast_to(mx + jnp.log(l), (ATTN_BLK, HEAD))

        block(0, True)
        if n > 1:
            def loop(i, carry):
                block(i, False)
                return carry
            lax.fori_loop(1, n, loop, 0)

    shp = jax.ShapeDtypeStruct((m, d * ATTN_W), F32)
    ospec = pl.BlockSpec((tq, ATTN_W), lambda c, b: (b, c))
    out, lse = pl.pallas_call(
        body, out_shape=(shp, shp), grid=(d, m // tq),
        in_specs=[pl.BlockSpec((tq, w3), lambda c, b: (b, c)),
                  pl.BlockSpec((ATTN_BLK, w3), lambda c, b: (jnp.maximum(b * n - 1, 0), c))],
        out_specs=(ospec, ospec), name=name,
        compiler_params=_cparams(("parallel", "parallel")))(qv, qv)
    return out, lse


def _attn_bwd(qv, ov, dov, lv, d, name, dep=None):
    m = qv.shape[0]
    w3 = 3 * ATTN_W
    _, tq, n = _attn_tiles(m * d, d)
    nb = m // ATTN_BLK
    scale = HEAD ** -0.5

    def body(x_ref, p_ref, nx_ref, o_ref, do_ref, l_ref, on_ref, don_ref, ln_ref, *rest):
        dq_ref, dk_ref, dv_ref = rest[-3:]
        b = pl.program_id(1)
        dk_ref[...] = jnp.zeros_like(dk_ref)
        dv_ref[...] = jnp.zeros_like(dv_ref)

        def grads(q, k, v, o, do, lse, valid):
            sc = jnp.where(valid, _dot_nt(q, k) * scale, -1e30)
            p = jnp.exp(sc - lse)
            delta = jnp.sum(do * o, axis=-1, keepdims=True)
            return p, p * (_dot_nt(do, v) - delta) * scale

        def block(i, first):
            r0 = 0 if first else pl.multiple_of(i * ATTN_BLK, ATTN_BLK)
            rows = pl.ds(r0, ATTN_BLK)
            valid = _band_mask(b * n + i)
            if not first:
                krows = pl.ds(pl.multiple_of(i * ATTN_BLK - ATTN_BLK, ATTN_BLK), 2 * ATTN_BLK)
            for h in range(ATTN_W // HEAD):
                qs, ks, vs = _head_cols(h)
                q = x_ref[rows, qs]
                do = do_ref[rows, qs]
                if first:
                    k = jnp.concatenate([p_ref[:, ks], x_ref[0:ATTN_BLK, ks]], axis=0)
                    v = jnp.concatenate([p_ref[:, vs], x_ref[0:ATTN_BLK, vs]], axis=0)
                else:
                    k = x_ref[krows, ks]
                    v = x_ref[krows, vs]
                p, ds = grads(q, k, v, o_ref[rows, qs], do, l_ref[rows, qs][:, 0:1], valid)
                dq_ref[rows, qs] = _dot(ds, k)
                if first:
                    dk_ref[0:ATTN_BLK, qs] += _dot_tn(ds[:, ATTN_BLK:], q)
                    dv_ref[0:ATTN_BLK, qs] += _dot_tn(p[:, ATTN_BLK:], do)
                else:
                    dk_ref[krows, qs] += _dot_tn(ds, q)
                    dv_ref[krows, qs] += _dot_tn(p, do)

        block(0, True)
        if n > 1:
            def loop(i, carry):
                block(i, False)
                return carry
            lax.fori_loop(1, n, loop, 0)

        last = slice((n - 1) * ATTN_BLK, n * ATTN_BLK)
        qi = lax.broadcasted_iota(jnp.int32, (ATTN_BLK, ATTN_BLK), 0)
        ki = lax.broadcasted_iota(jnp.int32, (ATTN_BLK, ATTN_BLK), 1)
        valid_next = (qi <= ki) & ((b + 1) * n < nb)
        for h in range(ATTN_W // HEAD):
            qs, ks, vs = _head_cols(h)
            q = nx_ref[:, qs]
            do = don_ref[:, qs]
            p, ds = grads(q, x_ref[last, ks], x_ref[last, vs], on_ref[:, qs], do, ln_ref[:, qs][:, 0:1],
                          valid_next)
            dk_ref[last, qs] += _dot_tn(ds, q)
            dv_ref[last, qs] += _dot_tn(p, do)

    nxt = lambda b: jnp.minimum((b + 1) * n, nb - 1)
    xs = pl.BlockSpec((tq, w3), lambda c, b: (b, c))
    xp = pl.BlockSpec((ATTN_BLK, w3), lambda c, b: (jnp.maximum(b * n - 1, 0), c))
    xn = pl.BlockSpec((ATTN_BLK, w3), lambda c, b: (nxt(b), c))
    a = pl.BlockSpec((tq, ATTN_W), lambda c, b: (b, c))
    an = pl.BlockSpec((ATTN_BLK, ATTN_W), lambda c, b: (nxt(b), c))
    shp = jax.ShapeDtypeStruct((m, d * ATTN_W), F32)
    dep_specs, dep_ops = _dep_args(dep)
    return pl.pallas_call(
        body, out_shape=(shp, shp, shp), grid=(d, m // tq),
        in_specs=[xs, xp, xn, a, a, a, an, an, an] + dep_specs, out_specs=(a, a, a), name=name,
        compiler_params=_cparams(("parallel", "parallel")))(qv, qv, qv, ov, dov, lv, ov, dov, lv, *dep_ops)


def _rms(x, g):
    ms = jnp.mean(x * x, axis=-1, keepdims=True)
    return x * lax.rsqrt(ms + RMS_EPS) * g


def _rms_bwd(x, g, dy):
    ms = jnp.mean(x * x, axis=-1, keepdims=True)
    r = lax.rsqrt(ms + RMS_EPS)
    dyg = dy * g
    dx = r * dyg - x * (r * r * r) * jnp.mean(x * dyg, axis=-1, keepdims=True)
    return dx, dy * x * r


def _mix_fwd(outs, lses, lru, s5, g, h_in, w_out, ln_g, ln_b, name):
    s = lru.shape[0]
    tm = 256

    def body(o1, o2, o3, l1, l2, l3, lru_ref, s5_ref, g_ref, x_ref, w_ref, lg_ref, lb_ref,
             mixed_t_ref, r_ref, h_ref, ht_ref, ov1, ov2, ov3, lv1, lv2, lv3, so2, so3, sl2, sl3):
        for d, src, dst in ((DILATIONS[1], o2, so2), (DILATIONS[2], o3, so3),
                            (DILATIONS[1], l2, sl2), (DILATIONS[2], l3, sl3)):
            _to_tokens(src, dst, d, tm)
        a1, a2, a3 = l1[...], _token_value(sl2), _token_value(sl3)
        mx = jnp.maximum(jnp.maximum(a1, a2), a3)
        e1, e2, e3 = jnp.exp(a1 - mx), jnp.exp(a2 - mx), jnp.exp(a3 - mx)
        den = e1 + e2 + e3
        o = (e1 * o1[...] + e2 * _token_value(so2) + e3 * _token_value(so3)) / den
        lse = mx + jnp.log(den)
        ov1[...] = o
        lv1[...] = lse
        for j in range(ATTN_W // LANE):
            so2[j] = o[:, j * LANE:(j + 1) * LANE]
            sl2[j] = lse[:, j * LANE:(j + 1) * LANE]
        for d, o_dst, l_dst in ((DILATIONS[1], ov2, lv2), (DILATIONS[2], ov3, lv3)):
            _to_dilated(so2, o_dst, d, tm)
            _to_dilated(sl2, l_dst, d, tm)
        gg = g_ref[...]
        mixed = jnp.concatenate([_rms(o, gg[:, :ATTN_W]),
                                 _rms(lru_ref[...], gg[:, ATTN_W:ATTN_W + LRU_W]),
                                 _rms(s5_ref[...], gg[:, ATTN_W + LRU_W:])], axis=1)
        mixed_t_ref[...] = mixed.T.astype(BF16)
        r = ALPHA * x_ref[...] + _dot(mixed, w_ref[...])
        h = _layer_norm(r, lg_ref[...], lb_ref[...])
        r_ref[...] = r
        h_ref[...] = h
        ht_ref[...] = h.T.astype(BF16)

    a = pl.BlockSpec((tm, ATTN_W), lambda i: (i, 0))
    s5s = pl.BlockSpec((tm, S5_W), lambda i: (i, 0))
    full = pl.BlockSpec((tm, D_MODEL), lambda i: (i, 0))
    vec = pl.BlockSpec((1, D_MODEL), lambda i: (0, 0))
    dil = [_dilated_spec(tm, d, ATTN_W) for d in DILATIONS]
    dshape = [jax.ShapeDtypeStruct((s // d, d * ATTN_W), F32) for d in DILATIONS]
    tshape = jax.ShapeDtypeStruct((D_MODEL, s), BF16)
    fshape = jax.ShapeDtypeStruct((s, D_MODEL), F32)
    tspec = pl.BlockSpec((D_MODEL, tm), lambda i: (0, i))
    res = pl.pallas_call(
        body, out_shape=(tshape, fshape, fshape, tshape, *dshape, *dshape),
        grid=(s // tm,),
        in_specs=dil + dil + [a, s5s, vec, full, pl.BlockSpec((D_MODEL, D_MODEL), lambda i: (0, 0)), vec, vec],
        out_specs=(tspec, full, full, tspec, *dil, *dil),
        scratch_shapes=[_token_scratch(tm, ATTN_W)] * 4, name=name,
        compiler_params=_cparams(("parallel",)))(*outs, *lses, lru, s5, g, h_in, w_out, ln_g, ln_b)
    return res[0], res[1], res[2], res[3], res[4:7], res[7:10]


def _mix_bwd(r, dh, ln_g, w_out, o, lru, s5, g, name, dep=None):
    s = lru.shape[0]
    tm = 256

    def body(r_ref, dh_ref, lg_ref, w_ref, o_ref, lru_ref, s5_ref, g_ref, *rest):
        dr_ref, dlg_ref, dlb_ref, do_ref, do2_ref, do3_ref, dlru_ref, ds5_ref, dg_ref, stage = rest[-10:]

        @pl.when(pl.program_id(0) == 0)
        def _():
            dg_ref[...] = jnp.zeros_like(dg_ref)
            dlg_ref[...] = jnp.zeros_like(dlg_ref)
            dlb_ref[...] = jnp.zeros_like(dlb_ref)
        gg = g_ref[...]
        dr, dlg_rows, dlb_rows = _layer_norm_bwd(r_ref[...], dh_ref[...], lg_ref[...])
        dr_ref[...] = dr
        dlg_ref[...] += dlg_rows
        dlb_ref[...] += dlb_rows
        dm = _dot_nt(dr, w_ref[...])
        dx, dgr = _rms_bwd(o_ref[...], gg[:, :ATTN_W], dm[:, :ATTN_W])
        do_ref[...] = dx
        for j in range(ATTN_W // LANE):
            stage[j] = dx[:, j * LANE:(j + 1) * LANE]
        _to_dilated(stage, do2_ref, DILATIONS[1], tm)
        _to_dilated(stage, do3_ref, DILATIONS[2], tm)
        dg_ref[:, :ATTN_W] += jnp.sum(dgr, axis=0, keepdims=True)
        dx, dgr = _rms_bwd(lru_ref[...], gg[:, ATTN_W:ATTN_W + LRU_W], dm[:, ATTN_W:ATTN_W + LRU_W])
        dlru_ref[...] = dx
        dg_ref[:, ATTN_W:ATTN_W + LRU_W] += jnp.sum(dgr, axis=0, keepdims=True)
        dx, dgr = _rms_bwd(s5_ref[...], gg[:, ATTN_W + LRU_W:], dm[:, ATTN_W + LRU_W:])
        ds5_ref[...] = dx
        dg_ref[:, ATTN_W + LRU_W:] += jnp.sum(dgr, axis=0, keepdims=True)

    a = pl.BlockSpec((tm, ATTN_W), lambda i: (i, 0))
    s5s = pl.BlockSpec((tm, S5_W), lambda i: (i, 0))
    full = pl.BlockSpec((tm, D_MODEL), lambda i: (i, 0))
    vec = pl.BlockSpec((1, D_MODEL), lambda i: (0, 0))
    dil = [_dilated_spec(tm, d, ATTN_W) for d in DILATIONS]
    dshape = [jax.ShapeDtypeStruct((s // d, d * ATTN_W), F32) for d in DILATIONS]
    dep_specs, dep_ops = _dep_args(dep)
    vshape = jax.ShapeDtypeStruct((1, D_MODEL), F32)
    res = pl.pallas_call(
        body, out_shape=(jax.ShapeDtypeStruct((s, D_MODEL), F32), vshape, vshape, *dshape,
                         jax.ShapeDtypeStruct((s, LRU_W), F32), jax.ShapeDtypeStruct((s, S5_W), F32), vshape),
        grid=(s // tm,),
        in_specs=[full, full, vec, pl.BlockSpec((D_MODEL, D_MODEL), lambda i: (0, 0)), a, a, s5s, vec] + dep_specs,
        out_specs=(full, vec, vec, *dil, a, s5s, vec), scratch_shapes=[_token_scratch(tm, ATTN_W)], name=name,
        compiler_params=_cparams(("arbitrary",)))(r, dh, ln_g, w_out, o, lru, s5, g, *dep_ops)
    return res[0], res[1], res[2], res[3:6], res[6], res[7], res[8]


def _lru_gate_math(xc, pre_r, pre_i, lam):
    r = _sigmoid(pre_r)
    i = _sigmoid(pre_i)
    log_a = -LRU_C * r * _softplus(-lam)
    a = jnp.exp(log_a)
    u = jnp.sqrt(-_expm1(2.0 * log_a)) * (i * xc)
    return a, u


def _lru_conv(x, prev8, cw, cb):
    y = cb + cw[LRU_CONV - 1:LRU_CONV, :] * x
    for k in range(LRU_CONV - 1):
        y = y + cw[k:k + 1, :] * _shift_down_prev(x, LRU_CONV - 1 - k, prev8)
    return y


def _lru_specs(s):
    xo = 3 * ATTN_W // LANE
    go = xo + LRU_W // LANE
    xr = pl.BlockSpec((s, LANE), lambda j: (0, xo + j))
    gt = pl.BlockSpec((s, LANE), lambda j: (0, go + j))
    cw = pl.BlockSpec((LRU_CONV, LANE), lambda j: (0, j))
    vec = pl.BlockSpec((1, LANE), lambda j: (0, j))
    wbd = pl.BlockSpec((LANE, LANE), lambda j: (j, j))
    col = pl.BlockSpec((s, LANE), lambda j: (0, j))
    return xr, gt, cw, vec, wbd, col


def _lru_fwd(proj, cw, cb, wr, br, wi, bi, lam, name):
    s = proj.shape[0]
    t = SCAN_T

    def body(xr_ref, gt_ref, cw_ref, cb_ref, wr_ref, br_ref, wi_ref, bi_ref, lam_ref, o_ref, xc_ref, a_ref, h_ref):
        cwv, cbv, lamv = cw_ref[...], cb_ref[...], lam_ref[...]
        wrv, wiv, brv, biv = wr_ref[...], wi_ref[...], br_ref[...], bi_ref[...]

        def chunk(c, carry):
            h_c, prev8 = carry
            rows = pl.ds(pl.multiple_of(c * t, t), t)
            x = xr_ref[rows, :]
            xc = _lru_conv(x, prev8, cwv, cbv)
            a, u = _lru_gate_math(xc, _dot(xc, wrv) + brv, _dot(xc, wiv) + biv, lamv)
            h = _scan_chunk(a, u, h_c)
            xc_ref[rows, :] = xc
            a_ref[rows, :] = a
            h_ref[rows, :] = h
            o_ref[rows, :] = h * _gelu(gt_ref[rows, :])
            return h[t - 1:t, :], x[t - 8:t, :]

        lax.fori_loop(0, s // t, chunk, (jnp.zeros((1, LANE), F32), jnp.zeros((8, LANE), F32)))

    xr, gt, cws, vec, wbd, col = _lru_specs(s)
    shp = jax.ShapeDtypeStruct((s, LRU_W), F32)
    return pl.pallas_call(
        body, out_shape=(shp,) * 4, grid=(LRU_W // LANE,),
        in_specs=[xr, gt, cws, vec, wbd, vec, wbd, vec, vec], out_specs=(col,) * 4, name=name,
        compiler_params=_cparams(("parallel",)))(proj, proj, cw, cb, wr, br, wi, bi, lam)


def _lru_bwd(proj, dout, xc_all, a_all, h_all, cw, cb, wr, br, wi, bi, lam, name):
    s = proj.shape[0]
    t = SCAN_T
    nc = s // t

    def body(xr_ref, gt_ref, do_ref, xc_s, a_s, h_s, cw_ref, cb_ref, wr_ref, br_ref, wi_ref, bi_ref, lam_ref,
             dxr_ref, dgt_ref, dcw_ref, dcb_ref, dwr_ref, dbr_ref, dwi_ref, dbi_ref, dlam_ref):
        cwv, cbv, lamv = cw_ref[...], cb_ref[...], lam_ref[...]
        wrv, wiv, brv, biv = wr_ref[...], wi_ref[...], br_ref[...], bi_ref[...]
        z1 = jnp.zeros((1, LANE), F32)
        zw = jnp.zeros((LANE, LANE), F32)

        def bchunk(ci, carry):
            g_next, a_next, dxc_next8, dcw, dcb, dwr, dbr, dwi, dbi, dlam = carry
            c = nc - 1 - ci
            t0 = pl.multiple_of(c * t, t)
            rows = pl.ds(t0, t)
            before = pl.ds(pl.multiple_of(jnp.maximum(t0 - 8, 0), 8), 8)
            has_prev = (c > 0).astype(F32)
            x, gt, do = xr_ref[rows, :], gt_ref[rows, :], do_ref[rows, :]
            xc, a, h = xc_s[rows, :], a_s[rows, :], h_s[rows, :]
            prev8_x = xr_ref[before, :] * has_prev
            prev8_h = h_s[before, :] * has_prev
            dgt_ref[rows, :] = do * h * _gelu_grad(gt)
            dh = do * _gelu(gt)
            a_plus = _shift_up_next(a, 1, jnp.broadcast_to(a_next, (8, LANE)))
            g = _scan_chunk(a_plus, dh, g_next, reverse=True)
            da = g * _shift_down_prev(h, 1, prev8_h)
            pre_r = _dot(xc, wrv) + brv
            pre_i = _dot(xc, wiv) + biv
            _, vjp = jax.vjp(_lru_gate_math, xc, pre_r, pre_i, lamv)
            dxc, dpre_r, dpre_i, dlam_c = vjp((da, g))
            dxc = dxc + _dot_nt(dpre_r, wrv) + _dot_nt(dpre_i, wiv)
            dx = cwv[LRU_CONV - 1:LRU_CONV, :] * dxc
            dcw_rows = [None] * LRU_CONV
            dcw_rows[LRU_CONV - 1] = jnp.sum(dxc * x, axis=0, keepdims=True)
            for k in range(LRU_CONV - 1):
                sh = LRU_CONV - 1 - k
                dx = dx + cwv[k:k + 1, :] * _shift_up_next(dxc, sh, dxc_next8)
                dcw_rows[k] = jnp.sum(dxc * _shift_down_prev(x, sh, prev8_x), axis=0, keepdims=True)
            dxr_ref[rows, :] = dx
            return (g[0:1, :], a[0:1, :], dxc[0:8, :],
                    dcw + jnp.concatenate(dcw_rows, axis=0),
                    dcb + jnp.sum(dxc, axis=0, keepdims=True),
                    dwr + _dot_tn(xc, dpre_r), dbr + jnp.sum(dpre_r, axis=0, keepdims=True),
                    dwi + _dot_tn(xc, dpre_i), dbi + jnp.sum(dpre_i, axis=0, keepdims=True),
                    dlam + dlam_c)

        init = (z1, z1, jnp.zeros((8, LANE), F32), jnp.zeros((LRU_CONV, LANE), F32), z1, zw, z1, zw, z1, z1)
        res = lax.fori_loop(0, nc, bchunk, init)
        dcw_ref[...] = res[3]
        dcb_ref[...] = res[4]
        dwr_ref[...] = res[5]
        dbr_ref[...] = res[6]
        dwi_ref[...] = res[7]
        dbi_ref[...] = res[8]
        dlam_ref[...] = res[9]

    xr, gt, cws, vec, wbd, col = _lru_specs(s)
    vshape = jax.ShapeDtypeStruct((1, LRU_W), F32)
    wshape = jax.ShapeDtypeStruct((LRU_W, LRU_W), F32)
    return pl.pallas_call(
        body,
        out_shape=(jax.ShapeDtypeStruct((s, LRU_W), F32), jax.ShapeDtypeStruct((s, LRU_W), F32),
                   jax.ShapeDtypeStruct((LRU_CONV, LRU_W), F32), vshape, wshape, vshape, wshape, vshape, vshape),
        grid=(LRU_W // LANE,),
        in_specs=[xr, gt, col, col, col, col, cws, vec, wbd, vec, wbd, vec, vec],
        out_specs=(col, col, cws, vec, wbd, vec, wbd, vec, vec), name=name,
        compiler_params=_cparams(("parallel",)))(proj, proj, dout, xc_all, a_all, h_all, cw, cb, wr, br, wi, bi,
                                                 lam)


def _s5_disc_math(a_re, a_im, log_step, bt_re, bt_im):
    step = jnp.exp(log_step)
    dt_re, dt_im = step * a_re, step * a_im
    mag = jnp.exp(dt_re)
    ab_re, ab_im = mag * jnp.cos(dt_im), mag * jnp.sin(dt_im)
    z_re, z_im = ab_re - 1.0, ab_im
    den = a_re * a_re + a_im * a_im
    f_re = (z_re * a_re + z_im * a_im) / den
    f_im = (z_im * a_re - z_re * a_im) / den
    bb_re = f_re * bt_re - f_im * bt_im
    bb_im = f_re * bt_im + f_im * bt_re
    return ab_re, ab_im, bb_re, bb_im


def _s5_disc_fwd(a_re, a_im, log_step, bt_re, bt_im, name):
    def body(ar, ai, ls, br, bi, o1, o2, o3, o4):
        r = _s5_disc_math(ar[...], ai[...], ls[...], br[...], bi[...])
        o1[...], o2[...], o3[...], o4[...] = r

    shp = jax.ShapeDtypeStruct(a_re.shape, F32)
    return pl.pallas_call(body, out_shape=(shp,) * 4, name=name)(a_re, a_im, log_step, bt_re, bt_im)


def _s5_disc_bwd(a_re, a_im, log_step, bt_re, bt_im, cts, name):
    def body(ar, ai, ls, br, bi, c1, c2, c3, c4, o1, o2, o3, o4, o5):
        _, vjp = jax.vjp(_s5_disc_math, ar[...], ai[...], ls[...], br[...], bi[...])
        r = vjp((c1[...], c2[...], c3[...], c4[...]))
        o1[...], o2[...], o3[...], o4[...], o5[...] = r

    shp = jax.ShapeDtypeStruct(a_re.shape, F32)
    return pl.pallas_call(body, out_shape=(shp,) * 5, name=name)(a_re, a_im, log_step, bt_re, bt_im, *cts)


def _s5_u_specs(s):
    uo = (3 * ATTN_W + 2 * LRU_W) // LANE
    return (pl.BlockSpec((s, LANE), lambda j: (0, uo)), pl.BlockSpec((s, LANE), lambda j: (0, uo + 1)))


def _s5_scan_fwd(proj, b_re, b_im, lam_re, lam_im, c_re, c_im, name):
    s = proj.shape[0]
    t = SCAN_T

    def body(u0_ref, u1_ref, bre_ref, bim_ref, lre_ref, lim_ref, cre_ref, cim_ref, xre_ref, xim_ref, y_ref):
        @pl.when(pl.program_id(0) == 0)
        def _():
            y_ref[...] = jnp.zeros_like(y_ref)
        lr, li = lre_ref[...], lim_ref[...]
        consts = _cscan_consts(lr, li, False)
        bre, bim, cre, cim = bre_ref[...], bim_ref[...], cre_ref[...], cim_ref[...]

        def chunk(c, carry):
            cr, ci = carry
            rows = pl.ds(pl.multiple_of(c * t, t), t)
            u = jnp.concatenate([u0_ref[rows, :], u1_ref[rows, :]], axis=1).astype(BF16)
            xr, xi = _cscan_chunk(_dot(u, bre), _dot(u, bim), consts, (cr, ci))
            xre_ref[rows, :] = xr
            xim_ref[rows, :] = xi
            y_ref[rows, :] += _dot(xr, cre) - _dot(xi, cim)
            return xr[t - 1:t, :], xi[t - 1:t, :]

        z = jnp.zeros((1, S5_BLK), F32)
        lax.fori_loop(0, s // t, chunk, (z, z))

    u0, u1 = _s5_u_specs(s)
    bsp = pl.BlockSpec((S5_W, S5_BLK), lambda j: (0, j))
    csp = pl.BlockSpec((S5_BLK, S5_W), lambda j: (j, 0))
    vec = pl.BlockSpec((1, S5_BLK), lambda j: (0, j))
    xsp = pl.BlockSpec((s, S5_BLK), lambda j: (0, j))
    ysp = pl.BlockSpec((s, S5_W), lambda j: (0, 0))
    xshape = jax.ShapeDtypeStruct((s, S5_STATES), F32)
    return pl.pallas_call(
        body, out_shape=(xshape, xshape, jax.ShapeDtypeStruct((s, S5_W), F32)),
        grid=(S5_STATES // S5_BLK,), in_specs=[u0, u1, bsp, bsp, vec, vec, csp, csp],
        out_specs=(xsp, xsp, ysp), name=name,
        compiler_params=_cparams(("arbitrary",)))(proj, proj, b_re, b_im, lam_re, lam_im, c_re, c_im)


def _s5_scan_bwd(proj, dy, du_init, x_re, x_im, b_re, b_im, lam_re, lam_im, c_re, c_im, name):
    s = proj.shape[0]
    t = SCAN_T
    nc = s // t

    def body(u0_ref, u1_ref, dy_ref, dui_ref, xre_ref, xim_ref, bre_ref, bim_ref, lre_ref, lim_ref,
             cre_ref, cim_ref, du_ref, dlr_ref, dli_ref, dbr_ref, dbi_ref, dcr_ref, dci_ref):
        @pl.when(pl.program_id(0) == 0)
        def _():
            du_ref[...] = dui_ref[...]
        mr, mi = lre_ref[...], -lim_ref[...]
        consts = _cscan_consts(mr, mi, True)
        bre, bim, cre, cim = bre_ref[...], bim_ref[...], cre_ref[...], cim_ref[...]
        dbr_ref[...] = jnp.zeros_like(dbr_ref)
        dbi_ref[...] = jnp.zeros_like(dbi_ref)
        dcr_ref[...] = jnp.zeros_like(dcr_ref)
        dci_ref[...] = jnp.zeros_like(dci_ref)

        def chunk(ci_, carry):
            gnr, gni, dlr, dli = carry
            c = nc - 1 - ci_
            t0 = pl.multiple_of(c * t, t)
            rows = pl.ds(t0, t)
            before = pl.ds(pl.multiple_of(jnp.maximum(t0 - 8, 0), 8), 8)
            has_prev = (c > 0).astype(F32)
            dyc = dy_ref[rows, :].astype(BF16)
            u = jnp.concatenate([u0_ref[rows, :], u1_ref[rows, :]], axis=1).astype(BF16)
            gr, gi = _cscan_chunk(_dot_nt(dyc, cre), -_dot_nt(dyc, cim), consts, (gnr, gni), reverse=True)
            xr, xi = xre_ref[rows, :], xim_ref[rows, :]
            xpr = _shift_down_prev(xr, 1, xre_ref[before, :] * has_prev)
            xpi = _shift_down_prev(xi, 1, xim_ref[before, :] * has_prev)
            dlr = dlr + jnp.sum(gr * xpr + gi * xpi, axis=0, keepdims=True)
            dli = dli + jnp.sum(gi * xpr - gr * xpi, axis=0, keepdims=True)
            du_ref[rows, :] += _dot_nt(gr, bre) + _dot_nt(gi, bim)
            dbr_ref[...] += _dot_tn(u, gr)
            dbi_ref[...] += _dot_tn(u, gi)
            dcr_ref[...] += _dot_tn(xr, dyc)
            dci_ref[...] -= _dot_tn(xi, dyc)
            return gr[0:1, :], gi[0:1, :], dlr, dli

        z = jnp.zeros((1, S5_BLK), F32)
        res = lax.fori_loop(0, nc, chunk, (z, z, z, z))
        dlr_ref[...] = res[2]
        dli_ref[...] = res[3]

    u0, u1 = _s5_u_specs(s)
    bsp = pl.BlockSpec((S5_W, S5_BLK), lambda j: (0, j))
    csp = pl.BlockSpec((S5_BLK, S5_W), lambda j: (j, 0))
    vec = pl.BlockSpec((1, S5_BLK), lambda j: (0, j))
    xsp = pl.BlockSpec((s, S5_BLK), lambda j: (0, j))
    ysp = pl.BlockSpec((s, S5_W), lambda j: (0, 0))
    return pl.pallas_call(
        body,
        out_shape=(jax.ShapeDtypeStruct((s, S5_W), F32),
                   jax.ShapeDtypeStruct((1, S5_STATES), F32), jax.ShapeDtypeStruct((1, S5_STATES), F32),
                   jax.ShapeDtypeStruct((S5_W, S5_STATES), F32), jax.ShapeDtypeStruct((S5_W, S5_STATES), F32),
                   jax.ShapeDtypeStruct((S5_STATES, S5_W), F32), jax.ShapeDtypeStruct((S5_STATES, S5_W), F32)),
        grid=(S5_STATES // S5_BLK,),
        in_specs=[u0, u1, ysp, ysp, xsp, xsp, bsp, bsp, vec, vec, csp, csp],
        out_specs=(ysp, vec, vec, bsp, bsp, csp, csp), name=name,
        compiler_params=_cparams(("arbitrary",)))(
            proj, proj, dy, du_init, x_re, x_im, b_re, b_im, lam_re, lam_im, c_re, c_im)


def _s5_out_fwd(proj, y_acc, dvec, w_glu, b_glu, name):
    s = proj.shape[0]
    tm = 512
    uo = (3 * ATTN_W + 2 * LRU_W) // LANE

    def body(u0_ref, u1_ref, y_ref, d_ref, w_ref, b_ref, o_ref, yp_ref):
        u = jnp.concatenate([u0_ref[...], u1_ref[...]], axis=1)
        y = y_ref[...] + d_ref[...] * u
        yp_ref[...] = y
        yg = _gelu(y)
        o_ref[...] = yg * _sigmoid(_dot(yg, w_ref[...]) + b_ref[...])

    u0 = pl.BlockSpec((tm, LANE), lambda i: (i, uo))
    u1 = pl.BlockSpec((tm, LANE), lambda i: (i, uo + 1))
    row = pl.BlockSpec((tm, S5_W), lambda i: (i, 0))
    vec = pl.BlockSpec((1, S5_W), lambda i: (0, 0))
    wsp = pl.BlockSpec((S5_W, S5_W), lambda i: (0, 0))
    shp = jax.ShapeDtypeStruct((s, S5_W), F32)
    return pl.pallas_call(
        body, out_shape=(shp, shp), grid=(s // tm,), in_specs=[u0, u1, row, vec, wsp, vec],
        out_specs=(row, row), name=name,
        compiler_params=_cparams(("parallel",)))(proj, proj, y_acc, dvec, w_glu, b_glu)


def _s5_out_bwd(proj, y_pre, dout, dvec, w_glu, b_glu, name, dep=None):
    s = proj.shape[0]
    tm = 512
    uo = (3 * ATTN_W + 2 * LRU_W) // LANE

    def body(u0_ref, u1_ref, y_ref, do_ref, d_ref, w_ref, b_ref, *rest):
        dy_ref, dud_ref, dd_ref, dw_ref, db_ref = rest[-5:]

        @pl.when(pl.program_id(0) == 0)
        def _():
            dd_ref[...] = jnp.zeros_like(dd_ref)
            dw_ref[...] = jnp.zeros_like(dw_ref)
            db_ref[...] = jnp.zeros_like(db_ref)
        u = jnp.concatenate([u0_ref[...], u1_ref[...]], axis=1)
        y = y_ref[...]
        do = do_ref[...]
        yg = _gelu(y)
        sg = _sigmoid(_dot(yg, w_ref[...]) + b_ref[...])
        dz = do * yg * sg * (1.0 - sg)
        dyg = do * sg + _dot_nt(dz, w_ref[...])
        dy = dyg * _gelu_grad(y)
        dy_ref[...] = dy
        dud_ref[...] = d_ref[...] * dy
        dd_ref[...] += jnp.sum(dy * u, axis=0, keepdims=True)
        dw_ref[...] += _dot_tn(yg, dz)
        db_ref[...] += jnp.sum(dz, axis=0, keepdims=True)

    u0 = pl.BlockSpec((tm, LANE), lambda i: (i, uo))
    u1 = pl.BlockSpec((tm, LANE), lambda i: (i, uo + 1))
    row = pl.BlockSpec((tm, S5_W), lambda i: (i, 0))
    vec = pl.BlockSpec((1, S5_W), lambda i: (0, 0))
    wsp = pl.BlockSpec((S5_W, S5_W), lambda i: (0, 0))
    shp = jax.ShapeDtypeStruct((s, S5_W), F32)
    vshape = jax.ShapeDtypeStruct((1, S5_W), F32)
    dep_specs, dep_ops = _dep_args(dep)
    return pl.pallas_call(
        body, out_shape=(shp, shp, vshape, jax.ShapeDtypeStruct((S5_W, S5_W), F32), vshape),
        grid=(s // tm,), in_specs=[u0, u1, row, row, vec, wsp, vec] + dep_specs,
        out_specs=(row, row, vec, wsp, vec), name=name,
        compiler_params=_cparams(("arbitrary",)))(proj, proj, y_pre, dout, dvec, w_glu, b_glu, *dep_ops)


def _ffn_conv(x, prev8, cw, cb):
    y = cb + cw[FFN_CONV - 1:FFN_CONV, :] * x
    for k in range(FFN_CONV - 1):
        y = y + cw[k:k + 1, :] * _shift_down_prev(x, FFN_CONV - 1 - k, prev8)
    return y


def _ffn_up_act(h, wg, cw, cb, name, dep=None):
    s, d = h.shape
    tm = 512
    tb = 2 * FFN_CB
    nt = D_FF // FFN_CB

    def body(h_ref, wgate_ref, wval_ref, cw_ref, cb_ref, *rest):
        up_ref, o_ref, ot_ref, carry = rest[-4:]
        t = pl.program_id(1)

        @pl.when(pl.program_id(0) == 0)
        def _():
            carry[t] = jnp.zeros((8, tb), F32)
        hb = h_ref[...].astype(BF16)
        x = jnp.concatenate([_dot(hb, wgate_ref[...]), _dot(hb, wval_ref[...])], axis=1)
        up_ref[...] = x
        y = _ffn_conv(x, carry[t], cw_ref[...], cb_ref[...])
        carry[t] = x[tm - 8:tm, :]
        act = _gelu(y[:, :FFN_CB]) * y[:, FFN_CB:]
        o_ref[...] = act.astype(BF16)
        ot_ref[...] = act.T.astype(BF16)

    dep_specs, dep_ops = _dep_args(dep)
    return pl.pallas_call(
        body, out_shape=(jax.ShapeDtypeStruct((s, 2 * D_FF), F32), jax.ShapeDtypeStruct((s, D_FF), BF16),
                         jax.ShapeDtypeStruct((D_FF, s), BF16)),
        grid=(s // tm, nt),
        in_specs=[pl.BlockSpec((tm, d), lambda i, t: (i, 0)),
                  pl.BlockSpec((None, d, FFN_CB), lambda i, t: (t, 0, 0)),
                  pl.BlockSpec((None, d, FFN_CB), lambda i, t: (t + nt, 0, 0)),
                  pl.BlockSpec((FFN_CONV, tb), lambda i, t: (0, t)),
                  pl.BlockSpec((1, tb), lambda i, t: (0, t))] + dep_specs,
        out_specs=(pl.BlockSpec((tm, tb), lambda i, t: (i, t)), pl.BlockSpec((tm, FFN_CB), lambda i, t: (i, t)),
                   pl.BlockSpec((FFN_CB, tm), lambda i, t: (t, i))),
        scratch_shapes=[pltpu.VMEM((nt, 8, tb), F32)], name=name,
        compiler_params=_cparams(("arbitrary", "arbitrary")))(h, wg, wg, cw, cb, *dep_ops)


def _ffn_bwd(up, dr, w_down, wg, cw, cb, name):
    s = up.shape[0]
    d = dr.shape[1]
    tm = 256
    tb = 2 * FFN_CB
    nr = s // tm
    nt = D_FF // FFN_CB

    def body(x_ref, p_ref, dr_ref, wd_ref, wgate_ref, wval_ref, cw_ref, cb_ref,
             dup_ref, dh_ref, dcw_ref, dcb_ref, carry):
        i, t = pl.program_id(0), pl.program_id(1)

        @pl.when(i == 0)
        def _():
            carry[t] = jnp.zeros((8, tb), F32)

        @pl.when(t == 0)
        def _():
            dh_ref[...] = ALPHA * dr_ref[...]
        prev8 = p_ref[...] * (i < nr - 1).astype(F32)
        cwv = cw_ref[...]
        x = x_ref[...]
        dact = _dot_nt(dr_ref[...], wd_ref[...])
        shifted = [_shift_down_prev(x, FFN_CONV - 1 - k, prev8) for k in range(FFN_CONV - 1)]
        y = cb_ref[...] + cwv[FFN_CONV - 1:FFN_CONV, :] * x
        for k in range(FFN_CONV - 1):
            y = y + cwv[k:k + 1, :] * shifted[k]
        gate, val = y[:, :FFN_CB], y[:, FFN_CB:]
        dy = jnp.concatenate([dact * val * _gelu_grad(gate), dact * _gelu(gate)], axis=1)
        next8 = carry[t]
        carry[t] = dy[0:8, :]
        dx = cwv[FFN_CONV - 1:FFN_CONV, :] * dy
        dcw_rows = [None] * FFN_CONV
        dcw_rows[FFN_CONV - 1] = jnp.sum(dy * x, axis=0, keepdims=True)
        for k in range(FFN_CONV - 1):
            dx = dx + cwv[k:k + 1, :] * _shift_up_next(dy, FFN_CONV - 1 - k, next8)
            dcw_rows[k] = jnp.sum(dy * shifted[k], axis=0, keepdims=True)
        dup = dx.astype(BF16)
        dup_ref[...] = dup
        dh_ref[...] += _dot_nt(dup[:, :FFN_CB], wgate_ref[...]) + _dot_nt(dup[:, FFN_CB:], wval_ref[...])
        dcw_ref[...] = jnp.concatenate(dcw_rows, axis=0)
        dcb_ref[...] = jnp.sum(dy, axis=0, keepdims=True)

    row = lambda i: nr - 1 - i
    return pl.pallas_call(
        body, out_shape=(jax.ShapeDtypeStruct((s, 2 * D_FF), BF16), jax.ShapeDtypeStruct((s, d), F32),
                         jax.ShapeDtypeStruct((nr, FFN_CONV, 2 * D_FF), F32),
                         jax.ShapeDtypeStruct((nr, 1, 2 * D_FF), F32)),
        grid=(nr, nt),
        in_specs=[pl.BlockSpec((tm, tb), lambda i, t: (row(i), t)),
                  pl.BlockSpec((8, tb), lambda i, t: (jnp.maximum(row(i) * (tm // 8) - 1, 0), t)),
                  pl.BlockSpec((tm, d), lambda i, t: (row(i), 0)),
                  pl.BlockSpec((FFN_CB, d), lambda i, t: (t, 0)),
                  pl.BlockSpec((None, d, FFN_CB), lambda i, t: (t, 0, 0)),
                  pl.BlockSpec((None, d, FFN_CB), lambda i, t: (t + nt, 0, 0)),
                  pl.BlockSpec((FFN_CONV, tb), lambda i, t: (0, t)),
                  pl.BlockSpec((1, tb), lambda i, t: (0, t))],
        out_specs=(pl.BlockSpec((tm, tb), lambda i, t: (row(i), t)),
                   pl.BlockSpec((tm, d), lambda i, t: (row(i), 0)),
                   pl.BlockSpec((None, FFN_CONV, tb), lambda i, t: (row(i), 0, t)),
                   pl.BlockSpec((None, 1, tb), lambda i, t: (row(i), 0, t))),
        scratch_shapes=[pltpu.VMEM((nt, 8, tb), F32)], name=name,
        compiler_params=_cparams(("arbitrary", "arbitrary")))(up, up, dr, w_down, wg, wg, cw, cb)


def _sum_partials(ld_ref):
    gg = ld_ref[0].astype(F32)
    for k in range(1, N_DEV):
        gg = gg + ld_ref[k].astype(F32)
    return gg


def _adam_update(w, g, m, v):
    mn = ADAM_B1 * m + (1.0 - ADAM_B1) * g
    vn = ADAM_B2 * v + (1.0 - ADAM_B2) * (g * g)
    m_hat = mn / (1.0 - ADAM_B1 ** ADAM_STEP)
    v_hat = vn / (1.0 - ADAM_B2 ** ADAM_STEP)
    return -ADAM_LR * (m_hat / (jnp.sqrt(v_hat) + ADAM_EPS) + ADAM_WD * w), mn, vn


def _adamw_many(landed, ws, ms, vs, name):
    n, nl = len(ws), len(landed)

    def body(*refs):
        ld = refs[:nl * n]
        w_refs, m_refs, v_refs = (refs[(nl + k) * n:(nl + k + 1) * n] for k in range(3))
        outs = refs[(nl + 3) * n:]
        for i in range(n):
            for l in range(nl):
                one = slice(l, l + 1)
                gg = _sum_partials(ld[l * n + i])
                outs[i][one] = gg
                outs[n + i][one], outs[2 * n + i][one], outs[3 * n + i][one] = _adam_update(
                    w_refs[i][one], gg, m_refs[i][one], v_refs[i][one])

    vm = pl.BlockSpec(memory_space=pltpu.VMEM)
    shapes = [jax.ShapeDtypeStruct(w.shape, F32) for w in ws] * 4
    res = pl.pallas_call(
        body, out_shape=tuple(shapes), in_specs=[vm] * ((nl + 3) * n), out_specs=tuple([vm] * (4 * n)),
        name=name, compiler_params=_cparams())(*[a for layer in landed for a in layer], *ws, *ms, *vs)
    return res[:n], res[n:2 * n], res[2 * n:3 * n], res[3 * n:]


def _adamw_sum(landed, w, m, v, layer, prev, name):
    _, r, c = landed.shape
    nl = w.shape[0]
    tm = 8
    for cand in (512, 256, 128, 64, 32, 16):
        if r % cand == 0 and N_DEV * cand * c * 4 <= 4 * 1024 * 1024:
            tm = cand
            break

    def body(*refs):
        ld_ref, w_ref, m_ref, v_ref = refs[:4]
        g_ref, d_ref, mo_ref, vo_ref = refs[-4:]
        gg = _sum_partials(ld_ref)
        g_ref[...] = gg
        d_ref[...], mo_ref[...], vo_ref[...] = _adam_update(w_ref[...], gg, m_ref[...], v_ref[...])

    blk = pl.BlockSpec((None, tm, c), lambda i: (layer, i, 0))
    in_specs = [pl.BlockSpec((N_DEV, tm, c), lambda i: (0, i, 0)), blk, blk, blk]
    args = [landed, w, m, v]
    aliases = {}
    if prev is not None:
        in_specs += [pl.BlockSpec(memory_space=pl.ANY)] * 4
        args += list(prev)
        aliases = {4 + k: k for k in range(4)}
    shp = jax.ShapeDtypeStruct((nl, r, c), F32)
    return pl.pallas_call(
        body, out_shape=(shp,) * 4, grid=(r // tm,), in_specs=in_specs, out_specs=(blk,) * 4,
        input_output_aliases=aliases, name=name, compiler_params=_cparams(("parallel",)))(*args)


def _all_gather(shards, name):
    na = len(shards)

    def body(*refs):
        x_refs, out_refs = refs[:na], refs[na:2 * na]
        send_sems, recv_sems, local_sems = refs[2 * na:]
        x, y, c = lax.axis_index("x"), lax.axis_index("y"), lax.axis_index("c")
        me, sibling = (x, y, c), (x, y, 1 - c)
        chips = [(1 - x, y), (x, 1 - y), (1 - x, 1 - y)]

        def copy(a, k, block, to, src=None):
            dst = out_refs[a].at[4 * block[0] + 2 * block[1] + block[2]]
            return pltpu.make_async_remote_copy(
                src_ref=dst if src is None else src, dst_ref=dst,
                send_sem=send_sems.at[7 * a + k], recv_sem=recv_sems.at[7 * a + k],
                device_id=to, device_id_type=pl.DeviceIdType.MESH)

        mine, first, passed = [], [], []
        for a in range(na):
            cp = pltpu.make_async_copy(x_refs[a], out_refs[a].at[4 * x + 2 * y + c], local_sems.at[a])
            cp.start()
            mine.append(cp)
            cps = [copy(a, 0, me, sibling, src=x_refs[a])]
            cps += [copy(a, 1 + j, me, (*chip, c), src=x_refs[a]) for j, chip in enumerate(chips)]
            for cp in cps:
                cp.start()
            first += cps
        for j, chip in enumerate(chips):
            for a in range(na):
                copy(a, 1 + j, (*chip, c), me).wait_recv()
                cp = copy(a, 4 + j, (*chip, c), sibling)
                cp.start()
                passed.append(cp)
        for a in range(na):
            copy(a, 0, sibling, me).wait_recv()
            for j, chip in enumerate(chips):
                copy(a, 4 + j, (*chip, 1 - c), me).wait_recv()
        for cp in first + passed:
            cp.wait_send()
        for cp in mine:
            cp.wait()

    anyspec = pl.BlockSpec(memory_space=pl.ANY)
    return pl.pallas_call(
        body, out_shape=tuple(jax.ShapeDtypeStruct((N_DEV,) + t.shape, t.dtype) for t in shards),
        in_specs=[anyspec] * na, out_specs=tuple([anyspec] * na),
        scratch_shapes=[pltpu.SemaphoreType.DMA((7 * na,)), pltpu.SemaphoreType.DMA((7 * na,)),
                        pltpu.SemaphoreType.DMA((na,))],
        name=name)(*shards)


_HBM = pl.BlockSpec(memory_space=pltpu.HBM)
_SEM = pl.BlockSpec(memory_space=pltpu.SEMAPHORE)
_EFFECT = pltpu.SideEffectType.DATAFLOW_SIDE_EFFECTING


def _exchange_copies(src_refs, land_refs, send_sems, recv_sems, local_sems, gather):
    x, y, c = lax.axis_index("x"), lax.axis_index("y"), lax.axis_index("c")
    me = 4 * x + 2 * y + c
    per_array = send_sems.shape[0] > N_DEV - 1
    local, remote = [], []
    for a, (src, land) in enumerate(zip(src_refs, land_refs)):
        local.append(pltpu.make_async_copy(src if gather else src.at[me], land.at[me],
                                           local_sems.at[a if per_array else 0]))
    for k in range(1, N_DEV):
        px = x ^ ((k >> 2) & 1)
        py = y ^ ((k >> 1) & 1)
        pc = c ^ (k & 1)
        for a, (src, land) in enumerate(zip(src_refs, land_refs)):
            remote.append(pltpu.make_async_remote_copy(
                src_ref=src if gather else src.at[4 * px + 2 * py + pc], dst_ref=land.at[me],
                send_sem=send_sems.at[(7 * a if per_array else 0) + k - 1],
                recv_sem=recv_sems.at[(7 * a if per_array else 0) + k - 1],
                device_id=(px, py, pc), device_id_type=pl.DeviceIdType.MESH))
    return local, remote


def _exchange_start(srcs, gather, name, dep=None):
    na = len(srcs)
    ns = na if na <= 4 else 1
    lands = [lax.empty(((N_DEV,) + t.shape) if gather else t.shape, t.dtype) for t in srcs]

    def body(*refs):
        src_refs, land_refs = refs[:na], refs[na:2 * na]
        nin = 2 * na + (0 if dep is None else 1)
        send_sems, recv_sems, local_sems = refs[nin:nin + 3]
        token = refs[-1]
        local, remote = _exchange_copies(src_refs, land_refs, send_sems, recv_sems, local_sems, gather)
        for cp in local + remote:
            cp.start()
        token[...] = jnp.zeros_like(token)

    dep_specs, dep_ops = _dep_args(dep)
    hbm = lambda t: pltpu.HBM(t.shape, t.dtype)
    out = pl.pallas_call(
        body, name=name,
        out_shape=(pltpu.SemaphoreType.DMA((7 * ns,)), pltpu.SemaphoreType.DMA((7 * ns,)),
                   pltpu.SemaphoreType.DMA((ns,)), *[hbm(t) for t in srcs], *[hbm(t) for t in lands],
                   jax.ShapeDtypeStruct((8, LANE), F32)),
        in_specs=[_HBM] * (2 * na) + dep_specs,
        out_specs=(_SEM, _SEM, _SEM, *[_HBM] * (2 * na), pl.BlockSpec(memory_space=pltpu.VMEM)),
        input_output_aliases={i: 3 + i for i in range(2 * na)},
        compiler_params=pltpu.CompilerParams(has_side_effects=_EFFECT),
    )(*[pltpu.with_memory_space_constraint(t, pltpu.HBM) for t in srcs + lands], *dep_ops)
    return (out[:3], out[3:3 + na], out[3 + na:3 + 2 * na]), out[-1]


def _exchange_wait(handle, gather, after, name):
    sems, srcs, lands = handle
    na = len(srcs)

    def body(*refs):
        src_refs, land_refs = refs[:na], refs[na:2 * na]
        send_sems, recv_sems, local_sems = refs[2 * na:2 * na + 3]
        local, remote = _exchange_copies(src_refs, land_refs, send_sems, recv_sems, local_sems, gather)
        for cp in remote:
            cp.wait_send()
            cp.wait_recv()
        for cp in local:
            cp.wait()

    hbm = lambda t: pltpu.HBM(t.shape, t.dtype)
    out = pl.pallas_call(
        body, name=name, out_shape=(*[hbm(t) for t in srcs], *[hbm(t) for t in lands]),
        in_specs=[_HBM] * (2 * na) + [_SEM] * 3 + [pl.BlockSpec(memory_space=pl.ANY)],
        out_specs=tuple([_HBM] * (2 * na)), input_output_aliases={i: i for i in range(2 * na)},
        compiler_params=pltpu.CompilerParams(has_side_effects=_EFFECT),
    )(*srcs, *lands, *sems, after)
    return out[na:]


def _block_diag(w):
    h, a, b = w.shape
    eye = jnp.eye(h, dtype=w.dtype)
    return (w[:, :, None, :] * eye[:, None, :, None]).reshape(h * a, h * b)


def _block_diag_extract(m, h):
    a, b = m.shape[0] // h, m.shape[1] // h
    return jnp.stack([m[i * a:(i + 1) * a, i * b:(i + 1) * b] for i in range(h)], axis=0)


def _block_diag_take(m, h):
    a, b = m.shape[0] // h, m.shape[1] // h
    eye = jnp.eye(h, dtype=m.dtype)
    return (m.reshape(h, a, h, b) * eye[:, None, :, None]).sum(axis=2)


def _ffn_interleave(w):
    lead = w.shape[:-1]
    nb = D_FF // FFN_CB
    return jnp.swapaxes(w.reshape(*lead, 2, nb, FFN_CB), -3, -2).reshape(*lead, 2 * D_FF)


def _ffn_deinterleave(w):
    lead = w.shape[:-1]
    nb = D_FF // FFN_CB
    return jnp.swapaxes(w.reshape(*lead, nb, 2, FFN_CB), -3, -2).reshape(*lead, 2 * D_FF)


def _gather_full(gathered, axis):
    shape = list(gathered.shape[1:])
    shape[axis] *= N_DEV
    return jnp.moveaxis(gathered, 0, axis).reshape(shape)


def _scatter_blocks(full, axis):
    shape = list(full.shape)
    shape[axis:axis + 1] = [N_DEV, shape[axis] // N_DEV]
    return jnp.moveaxis(full.reshape(shape), axis, 0)


def _pad_to(flat, mult):
    pad = (-flat.shape[-1]) % mult
    if pad:
        flat = jnp.concatenate([flat, jnp.zeros(flat.shape[:-1] + (pad,), flat.dtype)], axis=-1)
    return flat


def _layer_fwd(h_in, h_in_t, w, cos, sin, l, dep, get_ffn, target=None):
    tag = "l%d_" % l
    proj, qkv = _proj_rope(h_in, w['w_in'], cos, sin, tag + "proj_rope", dep=dep)
    outs, lses = [], []
    for d, qv in zip(DILATIONS, qkv):
        o, ls = _attn_fwd(qv, d, tag + "attn_d%d" % d)
        outs.append(o)
        lses.append(ls)
    lru, *lru_saved = _lru_fwd(proj, w['lru_conv_w'], w['lru_conv_b'], w['lru_wr'], w['lru_br'], w['lru_wi'],
                               w['lru_bi'], w['lru_lambda'], tag + "lru")
    x_re, x_im, y_acc = _s5_scan_fwd(proj, w['s5_bb_re'], w['s5_bb_im'], w['s5_lam_re'], w['s5_lam_im'],
                                     w['s5_cc_re'], w['s5_cc_im'], tag + "s5_scan")
    s5, y_pre = _s5_out_fwd(proj, y_acc, w['s5_d'], w['s5_w_glu'], w['s5_b_glu'], tag + "s5_out")
    mixed_t, r1, h1, h1_t, attn_o, attn_lse = _mix_fwd(outs, lses, lru, s5, w['mix_norm_g'], h_in, w['w_out'],
                                                       w['ln1_g'], w['ln1_b'], tag + "mix_out_ln1")
    w['w_up_g'], w['w_down'], ffn_dep = get_ffn(l, h1)
    up, act, act_t = _ffn_up_act(h1, w['w_up_g'], w['ffn_conv_w'], w['ffn_conv_b'], tag + "up_act", dep=ffn_dep)
    r2, out_a, out_b = _proj_ln(act, w['w_down'], h1, w['ln2_g'], w['ln2_b'], tag + "down_ln2", target=target)
    saved = dict(h_in_t=h_in_t, proj=proj, qkv=qkv, lru=lru, lru_saved=lru_saved, x_re=x_re, x_im=x_im,
                 y_pre=y_pre, s5=s5, mixed_t=mixed_t, attn_o=attn_o, attn_lse=attn_lse, r1=r1, h1_t=h1_t, up=up,
                 act_t=act_t, r2=r2)
    return out_a, out_b, saved


def _layer_bwd_ffn(dh2, sv, w, l, dep=None):
    tag = "l%d_" % l
    g = {}
    dr2, g['ln2_g'], g['ln2_b'] = _ln_bwd(sv['r2'], dh2, w['ln2_g'], tag + "ln2_bwd", dep=dep)
    g['w_down'] = _mm_dw(sv['act_t'], dr2, 1024, D_MODEL, 1024, tag + "down_dw")
    dup, dh1, dcw_parts, dcb_parts = _ffn_bwd(sv['up'], dr2, w['w_down'], w['w_up_g'], w['ffn_conv_w'],
                                              w['ffn_conv_b'], tag + "ffn_bwd")
    g['ffn_conv_w'] = dcw_parts.sum(axis=0)
    g['ffn_conv_b'] = dcb_parts.sum(axis=0)
    g['w_up_g'] = _mm_up_dw(sv['h1_t'], dup, tag + "up_dw")
    return dh1, g


def _layer_bwd_mix(dh1, sv, w, cos, sin, l, dep, g_ffn, after_out_grad, after_small_grads, after_in_grad):
    tag = "l%d_" % l
    g = {}
    dr1, g['ln1_g'], g['ln1_b'], d_o, dlru, ds5, g['mix_norm_g'] = _mix_bwd(
        sv['r1'], dh1, w['ln1_g'], w['w_out'], sv['attn_o'][0], sv['lru'], sv['s5'], w['mix_norm_g'],
        tag + "ln1_mix_bwd", dep=dep)
    g['w_out'] = _mm_dw(sv['mixed_t'], dr1, 1024, D_MODEL, 1024, tag + "out_dw")
    dy, dud, g['s5_d'], g['s5_w_glu'], g['s5_b_glu'] = _s5_out_bwd(
        sv['proj'], sv['y_pre'], ds5, w['s5_d'], w['s5_w_glu'], w['s5_b_glu'], tag + "s5_out_bwd",
        dep=after_out_grad(l, g['w_out']))
    du, g['s5_lam_re'], g['s5_lam_im'], g['s5_bb_re'], g['s5_bb_im'], g['s5_cc_re'], g['s5_cc_im'] = \
        _s5_scan_bwd(sv['proj'], dy, dud, sv['x_re'], sv['x_im'], w['s5_bb_re'], w['s5_bb_im'],
                     w['s5_lam_re'], w['s5_lam_im'], w['s5_cc_re'], w['s5_cc_im'], tag + "s5_scan_bwd")
    (dxr, dgate, g['lru_conv_w'], g['lru_conv_b'], g['lru_wr'], g['lru_br'], g['lru_wi'], g['lru_bi'],
     g['lru_lambda']) = _lru_bwd(sv['proj'], dlru, *sv['lru_saved'], w['lru_conv_w'], w['lru_conv_b'], w['lru_wr'],
                                 w['lru_br'], w['lru_wi'], w['lru_bi'], w['lru_lambda'], tag + "lru_bwd")
    token = after_small_grads(l, _finish_layer_grads({**g_ffn, **g}, w, l))
    dqkv = [_attn_bwd(sv['qkv'][b], sv['attn_o'][b], d_o[b], sv['attn_lse'][b], d, tag + "attn_bwd_d%d" % d,
                      dep=token if b == 0 else None)
            for b, d in enumerate(DILATIONS)]
    dproj = _dproj_assemble(dqkv, dxr, dgate, du, cos, sin, tag + "dproj")
    g_in = _mm_dw(sv['h_in_t'], dproj, 1024, D_IN, 1024, tag + "in_dw")
    return _mm_nt(dproj, w['w_in'], 512, D_MODEL, tag + "in_dx", add=dr1, add_scale=ALPHA,
                  dep=after_in_grad(l, g_in))


def _s5_rep(a):
    return jnp.repeat(a, S5_C, axis=0)


def _prepare_layer(p, l):
    w = {}
    for n in ('w_in', 'w_out', 's5_w_glu'):
        w[n] = p[n].astype(BF16)
    w['ffn_conv_w'] = _ffn_interleave(p['ffn_conv_w'])
    w['ffn_conv_b'] = _ffn_interleave(p['ffn_conv_b'])[None, :]
    w['lru_conv_w'] = p['lru_conv_w']
    for n in ('lru_conv_b', 'lru_br', 'lru_bi', 'lru_lambda', 's5_b_glu', 'mix_norm_g',
              'ln1_g', 'ln1_b', 'ln2_g', 'ln2_b'):
        w[n] = p[n][None, :]
    w['lru_wr'] = _block_diag(p['lru_wr']).astype(BF16)
    w['lru_wi'] = _block_diag(p['lru_wi']).astype(BF16)
    w['s5_d'] = p['s5_d'].reshape(1, S5_W)
    disc_in = (_s5_rep(p['s5_a_re']), _s5_rep(p['s5_a_im']),
               _s5_rep(jnp.broadcast_to(p['s5_log_step'][:, None], (S5_G, S5_P))),
               jnp.swapaxes(p['s5_b_re'], 1, 2).reshape(S5_W, S5_P),
               jnp.swapaxes(p['s5_b_im'], 1, 2).reshape(S5_W, S5_P))
    ab_re, ab_im, bb_re, bb_im = _s5_disc_fwd(*disc_in, "l%d_s5_disc" % l)
    w['s5_disc_in'] = disc_in
    w['s5_lam_re'] = ab_re.reshape(S5_G, S5_C, S5_P)[:, 0, :].reshape(1, S5_STATES)
    w['s5_lam_im'] = ab_im.reshape(S5_G, S5_C, S5_P)[:, 0, :].reshape(1, S5_STATES)
    w['s5_bb_re'] = _block_diag(bb_re.reshape(S5_G, S5_C, S5_P)).astype(BF16)
    w['s5_bb_im'] = _block_diag(bb_im.reshape(S5_G, S5_C, S5_P)).astype(BF16)
    w['s5_cc_re'] = _block_diag(jnp.swapaxes(p['s5_c_re'], 1, 2)).astype(BF16)
    w['s5_cc_im'] = _block_diag(jnp.swapaxes(p['s5_c_im'], 1, 2)).astype(BF16)
    return w


def _finish_layer_grads(g, w, l):
    out = {}
    for n in ('s5_w_glu', 'lru_conv_w'):
        out[n] = g[n]
    out['ffn_conv_w'] = _ffn_deinterleave(g['ffn_conv_w'])
    out['ffn_conv_b'] = _ffn_deinterleave(g['ffn_conv_b'])[0]
    for n in ('lru_conv_b', 'lru_br', 'lru_bi', 'lru_lambda', 's5_b_glu', 'mix_norm_g',
              'ln1_g', 'ln1_b', 'ln2_g', 'ln2_b'):
        out[n] = g[n][0]
    out['lru_wr'] = _block_diag_extract(g['lru_wr'], LRU_W // HEAD)
    out['lru_wi'] = _block_diag_extract(g['lru_wi'], LRU_W // HEAD)
    out['s5_d'] = g['s5_d'].reshape(S5_G, S5_C)
    out['s5_c_re'] = jnp.swapaxes(_block_diag_take(g['s5_cc_re'], S5_G), 1, 2)
    out['s5_c_im'] = jnp.swapaxes(_block_diag_take(g['s5_cc_im'], S5_G), 1, 2)
    rep = lambda v: _s5_rep(v.reshape(S5_G, S5_P)) * (1.0 / S5_C)
    cts = (rep(g['s5_lam_re']), rep(g['s5_lam_im']),
           _block_diag_take(g['s5_bb_re'], S5_G).reshape(S5_W, S5_P),
           _block_diag_take(g['s5_bb_im'], S5_G).reshape(S5_W, S5_P))
    da_re, da_im, dls, dbt_re, dbt_im = _s5_disc_bwd(*w['s5_disc_in'], cts, "l%d_s5_disc_bwd" % l)
    out['s5_a_re'] = da_re.reshape(S5_G, S5_C, S5_P).sum(axis=1)
    out['s5_a_im'] = da_im.reshape(S5_G, S5_C, S5_P).sum(axis=1)
    out['s5_log_step'] = dls.reshape(S5_G, S5_C * S5_P).sum(axis=1)
    out['s5_b_re'] = jnp.swapaxes(dbt_re.reshape(S5_G, S5_C, S5_P), 1, 2)
    out['s5_b_im'] = jnp.swapaxes(dbt_im.reshape(S5_G, S5_C, S5_P), 1, 2)
    return out


def _run_step(x, target, get_layer, get_ffn, after_ffn_grads, after_out_grad, after_small_grads, after_in_grad):
    cos, sin = _rope_tables(x.shape[0])
    h, h_t = x, _transpose_bf16(x, "x_transpose")
    ws, saved = [], []
    for l in range(DEPTH):
        p, dep = get_layer(l, h)
        ws.append(_prepare_layer(p, l))
        h, h_t, sv = _layer_fwd(h, h_t, ws[l], cos, sin, l, dep, get_ffn, target if l == DEPTH - 1 else None)
        saved.append(sv)
    dh, loss_vec = h, h_t
    dep = None
    for l in reversed(range(DEPTH)):
        dh1, g = _layer_bwd_ffn(dh, saved[l], ws[l], l, dep)
        dep = after_ffn_grads(l, g)
        dh = _layer_bwd_mix(dh1, saved[l], ws[l], cos, sin, l, dep, g, after_out_grad, after_small_grads,
                            after_in_grad)
        dep = None
    return loss_vec[0, 0], dh


def _local_step(x, target, layers):
    grads = [{} for _ in range(DEPTH)]

    def ffn(l, h1):
        return layers[l]['w_up_g'].astype(BF16), layers[l]['w_down'].astype(BF16), None

    def keep_ffn(l, g):
        grads[l].update(w_up_g=g['w_up_g'], w_down=g['w_down'])

    def keep_small(l, g):
        grads[l].update(g)

    loss, dx = _run_step(x, target, lambda l, h: (layers[l], None), ffn, keep_ffn,
                         lambda l, g: grads[l].update(w_out=g), keep_small, lambda l, g: grads[l].update(w_in=g))
    return loss, dx, grads


def kernel(x, w_in, lru_conv_w, lru_conv_b, lru_wr, lru_br, lru_wi, lru_bi, lru_lambda, s5_a_re, s5_a_im, s5_b_re, s5_b_im, s5_c_re, s5_c_im, s5_d, s5_log_step, s5_w_glu, s5_b_glu, mix_norm_g, w_out, ln1_g, ln1_b, w_up, ffn_conv_w, ffn_conv_b, w_down, ln2_g, ln2_b, loss_target, m_w_in, m_lru_conv_w, m_lru_conv_b, m_lru_wr, m_lru_br, m_lru_wi, m_lru_bi, m_lru_lambda, m_s5_a_re, m_s5_a_im, m_s5_b_re, m_s5_b_im, m_s5_c_re, m_s5_c_im, m_s5_d, m_s5_log_step, m_s5_w_glu, m_s5_b_glu, m_mix_norm_g, m_w_out, m_ln1_g, m_ln1_b, m_w_up, m_ffn_conv_w, m_ffn_conv_b, m_w_down, m_ln2_g, m_ln2_b, v_w_in, v_lru_conv_w, v_lru_conv_b, v_lru_wr, v_lru_br, v_lru_wi, v_lru_bi, v_lru_lambda, v_s5_a_re, v_s5_a_im, v_s5_b_re, v_s5_b_im, v_s5_c_re, v_s5_c_im, v_s5_d, v_s5_log_step, v_s5_w_glu, v_s5_b_glu, v_mix_norm_g, v_w_out, v_ln1_g, v_ln1_b, v_w_up, v_ffn_conv_w, v_ffn_conv_b, v_w_down, v_ln2_g, v_ln2_b):
    args = locals()
    wl = {n: args[n] for n in WEIGHTS}
    ml = {n: args['m_' + n] for n in WEIGHTS}
    vl = {n: args['v_' + n] for n in WEIGHTS}

    small_sizes = [int(wl[n].size) for n in SMALL_SHARDED]
    small_flat = _pad_to(jnp.concatenate([wl[n].reshape(-1) for n in SMALL_SHARDED]), 8 * 1024)
    small_all, = _all_gather([small_flat.reshape(-1, 1024)], "gather_small")
    small_all = small_all.reshape(N_DEV, -1)
    small_full, off = {}, 0
    for n, sz in zip(SMALL_SHARDED, small_sizes):
        small_full[n] = _gather_full(small_all[:, off:off + sz].reshape((N_DEV,) + wl[n].shape), SHARD_AXIS[n])
        off += sz
    def mixer_params(l, gathered):
        g_in, g_out = gathered
        p = {n: wl[n][l] for n in REPLICATED}
        p.update({n: small_full[n][l] for n in SMALL_SHARDED})
        p['w_in'] = _gather_full(g_in, 1)
        p['w_out'] = g_out.reshape(D_MODEL, D_MODEL)
        return p

    mix_names, ffn_names = ('w_in', 'w_out'), ('w_up', 'w_down')
    shards = lambda names, l: [wl[n][l].astype(BF16) for n in names]
    mix0 = _all_gather(shards(mix_names, 0), "gather_mix_l0")
    gathers = {}
    gathers[0, 'ffn'], ffn0_token = _exchange_start(shards(ffn_names, 0), True, "gather_ffn_l0_start", dep=mix0[0])
    def get_layer(l, h):
        if l == 0:
            return mixer_params(0, mix0), ffn0_token
        return mixer_params(1, _exchange_wait(gathers[1, 'mix'], True, h, "gather_mix_l1_wait")), None

    def get_ffn(l, h1):
        g_up, g_down = _exchange_wait(gathers[l, 'ffn'], True, h1, "gather_ffn_l%d_wait" % l)
        token = None
        if l == 0:
            gathers[1, 'mix'], token = _exchange_start(shards(mix_names, 1), True, "gather_mix_l1_start", dep=g_up)
            gathers[1, 'ffn'], token = _exchange_start(shards(ffn_names, 1), True, "gather_ffn_l1_start", dep=token)
        return g_up, g_down.reshape(D_FF, D_MODEL), token

    scatters = {}

    def after_ffn_grads(l, g):
        send = [g['w_up_g'], g['w_down'].reshape(N_DEV, D_FF // N_DEV, D_MODEL)]
        scatters[l, 'ffn'], token = _exchange_start(send, False, "scatter_ffn_l%d_start" % l)
        return token

    def after_out_grad(l, g_out):
        send = [g_out.reshape(N_DEV, D_MODEL // N_DEV, D_MODEL)]
        scatters[l, 'out'], token = _exchange_start(send, False, "scatter_out_l%d_start" % l)
        return token

    def after_in_grad(l, g_in):
        send = _scatter_blocks(g_in, 1)
        if l == 0:
            send = send.astype(BF16)
        scatters[l, 'in'], token = _exchange_start([send], False, "scatter_in_l%d_start" % l)
        return token

    def after_small_grads(l, g):
        rep = [g[n][None] for n in REPLICATED]
        shd = [_scatter_blocks(g[n], SHARD_AXIS[n] - 1)[:, None] for n in SMALL_SHARDED]
        scatters[l, 'rep'], token = _exchange_start(rep, True, "gather_rep_grads_l%d_start" % l)
        scatters[l, 'small'], token = _exchange_start(shd, False, "scatter_small_l%d_start" % l, dep=token)
        return token

    loss_local, grad_x = _run_step(x[0], loss_target[0], get_layer, get_ffn, after_ffn_grads, after_out_grad,
                                   after_small_grads, after_in_grad)
    loss = lax.psum(loss_local, AXES)

    results = {}
    big_prev = {n: None for n in BIG}

    def finish_big(l, part, names, after):
        landed = _exchange_wait(scatters[l, part], False, after, "scatter_%s_l%d_wait" % (part, l))
        for n, ld in zip(names, landed):
            big_prev[n] = _adamw_sum(ld, wl[n], ml[n], vl[n], l, big_prev[n], "adamw_%s_l%d" % (n, l))

    for l, part, names in ((1, 'ffn', ffn_names), (1, 'out', ('w_out',)), (1, 'in', ('w_in',)),
                           (0, 'ffn', ffn_names), (0, 'out', ('w_out',))):
        finish_big(l, part, names, grad_x)

    kinds = ('grad', 'delta', 'm', 'v')
    landed = [dict(zip(REPLICATED + SMALL_SHARDED,
                       list(_exchange_wait(scatters[l, 'rep'], True, grad_x, "gather_rep_grads_l%d_wait" % l)) +
                       list(_exchange_wait(scatters[l, 'small'], False, grad_x, "scatter_small_l%d_wait" % l))))
              for l in range(DEPTH)]
    matrices = ['lru_wr', 'lru_wi', 's5_a_re', 's5_a_im', 's5_c_re', 's5_c_im', 's5_d']
    widest = ['s5_b_re', 's5_b_im']
    vectors = [n for n in REPLICATED + SMALL_SHARDED if n not in matrices + widest]
    last = None
    for tag, names in (("vectors", vectors), ("matrices", matrices), ("s5_b", widest)):
        res = _adamw_many([[landed[l][n] for n in names] for l in range(DEPTH)], [wl[n] for n in names],
                          [ml[n] for n in names], [vl[n] for n in names], "adamw_" + tag)
        for kind, arrs in zip(kinds, res):
            for n, a in zip(names, arrs):
                results[kind, n] = a
        last = res[0][0]
    finish_big(0, 'in', ('w_in',), last)
    for n in BIG:
        results['grad', n], results['delta', n], results['m', n], results['v', n] = big_prev[n]

    out = [loss, grad_x[None]]
    for kind in kinds:
        out.extend(results[kind, n] for n in WEIGHTS)
    return tuple(out)
```

```python
import math

import jax
import jax.numpy as jnp
from jax import lax
from jax.experimental import pallas as pl
from jax.experimental.pallas import tpu as pltpu

F32 = jnp.float32
BF16 = jnp.bfloat16

N_DEV = 8
DEPTH = 2
D_MODEL = 1024
ATTN_W = 384
LRU_W = 384
S5_W = 256
D_IN = 2176
D_FF = 3072
HEAD = 64
ATTN_BLK = 128
ATTN_TILE = 1024
DILATIONS = (1, 4, 16)
S5_G = 16
S5_P = 64
S5_C = 16
S5_STATES = S5_G * S5_P
LRU_C = 8.0
LRU_CONV = 4
FFN_CONV = 3
ROPE_THETA = 10000.0
ALPHA = (2 * DEPTH) ** 0.25
LN_EPS = 1e-5
RMS_EPS = 1e-6
ADAM_LR, ADAM_B1, ADAM_B2, ADAM_EPS, ADAM_WD, ADAM_STEP = 0.001, 0.9, 0.999, 1e-8, 0.01, 10

LANE = 128
SCAN_T = 256
S5_BLK = 256
FFN_CB = 2 * D_FF // N_DEV
VMEM_LIMIT = 56 * 1024 * 1024

AXES = ("x", "y", "c")

WEIGHTS = ['w_in', 'lru_conv_w', 'lru_conv_b', 'lru_wr', 'lru_br', 'lru_wi', 'lru_bi', 'lru_lambda',
           's5_a_re', 's5_a_im', 's5_b_re', 's5_b_im', 's5_c_re', 's5_c_im', 's5_d', 's5_log_step',
           's5_w_glu', 's5_b_glu', 'mix_norm_g', 'w_out', 'ln1_g', 'ln1_b', 'w_up', 'ffn_conv_w',
           'ffn_conv_b', 'w_down', 'ln2_g', 'ln2_b']
SHARD_AXIS = {'w_in': 2, 'lru_conv_w': 2, 's5_w_glu': 1, 'w_out': 1, 'w_up': 2, 'ffn_conv_w': 2, 'w_down': 1}
BIG = ['w_in', 'w_out', 'w_up', 'w_down']
SMALL_SHARDED = ['lru_conv_w', 'ffn_conv_w', 's5_w_glu']
REPLICATED = [n for n in WEIGHTS if n not in SHARD_AXIS]


def _cparams(sem=None):
    return pltpu.CompilerParams(dimension_semantics=sem, vmem_limit_bytes=VMEM_LIMIT)


def _ffn_dev(jb):
    return jb // 2 + (N_DEV // 2) * (jb % 2)


def _gelu(x):
    c = math.sqrt(2.0 / math.pi)
    t = jnp.tanh(c * (x + 0.044715 * (x * x * x)))
    return 0.5 * x * (1.0 + t)


def _gelu_grad(x):
    c = math.sqrt(2.0 / math.pi)
    x2 = x * x
    t = jnp.tanh(c * (x + 0.044715 * (x2 * x)))
    return 0.5 * (1.0 + t) + 0.5 * x * (1.0 - t * t) * (c * (1.0 + 3.0 * 0.044715 * x2))


def _sigmoid(x):
    return 1.0 / (1.0 + jnp.exp(-x))


def _log1p(x):
    u = 1.0 + x
    d = u - 1.0
    return jnp.where(d == 0.0, x, jnp.log(u) * (x / jnp.where(d == 0.0, 1.0, d)))


def _softplus(x):
    return jnp.maximum(x, 0.0) + _log1p(jnp.exp(-jnp.abs(x)))


def _expm1(x):
    return jnp.tanh(0.5 * x) * (jnp.exp(x) + 1.0)


def _dot(a, b):
    return jnp.dot(a.astype(BF16), b.astype(BF16), preferred_element_type=F32)


def _dot_nt(a, b):
    return lax.dot_general(a.astype(BF16), b.astype(BF16), (((1,), (1,)), ((), ())),
                           preferred_element_type=F32)


def _dot_tn(a, b):
    return lax.dot_general(a.astype(BF16), b.astype(BF16), (((0,), (0,)), ((), ())),
                           preferred_element_type=F32)


def _rows(shape):
    return lax.broadcasted_iota(jnp.int32, shape, 0)


def _shift_down_prev(x, s, prev8):
    if s == 0:
        return x
    t, l = x.shape
    r = pltpu.roll(x, s, axis=0)
    pr = pltpu.roll(prev8, s, axis=0)
    pad = jnp.concatenate([pr, jnp.zeros((t - 8, l), x.dtype)], axis=0)
    return jnp.where(_rows(x.shape) < s, pad, r)


def _shift_up_next(x, s, next8):
    if s == 0:
        return x
    t, l = x.shape
    r = pltpu.roll(x, t - s, axis=0)
    nx = pltpu.roll(next8, 8 - s, axis=0)
    pad = jnp.concatenate([jnp.zeros((t - 8, l), x.dtype), nx], axis=0)
    return jnp.where(_rows(x.shape) >= t - s, pad, r)


SUB = 8


def _tile_shift(x, s, fill, reverse):
    t = x.shape[0]
    pos = _rows(x.shape) & (SUB - 1)
    if reverse:
        return jnp.where(pos < SUB - s, pltpu.roll(x, t - s, axis=0), fill)
    return jnp.where(pos >= s, pltpu.roll(x, s, axis=0), fill)


def _scan_chunk(a, x, carry, reverse=False):
    s = 1
    while s < SUB:
        x = x + a * _tile_shift(x, s, 0.0, reverse)
        a = a * _tile_shift(a, s, 1.0, reverse)
        s *= 2
    nv = x.shape[0] // SUB
    out = [None] * nv
    for v in (reversed(range(nv)) if reverse else range(nv)):
        rows = slice(v * SUB, (v + 1) * SUB)
        out[v] = x[rows, :] + a[rows, :] * carry
        carry = out[v][0:1, :] if reverse else out[v][SUB - 1:SUB, :]
    return jnp.concatenate(out, axis=0)


def _cmul(ar, ai, br, bi):
    return ar * br - ai * bi, ar * bi + ai * br


def _cscan_consts(lr, li, reverse):
    pows = [(lr, li)]
    for _ in range(2):
        pows.append(_cmul(*pows[-1], *pows[-1]))
    rows = [(lr, li)]
    for _ in range(SUB - 1):
        rows.append(_cmul(*rows[-1], lr, li))
    if reverse:
        rows = rows[::-1]
    return pows, (jnp.concatenate([r for r, _ in rows], axis=0), jnp.concatenate([i for _, i in rows], axis=0))


def _cscan_chunk(xr, xi, consts, carry, reverse=False):
    pows, (p8r, p8i) = consts
    s = 1
    for pr, pi in pows:
        sr = _tile_shift(xr, s, 0.0, reverse)
        si = _tile_shift(xi, s, 0.0, reverse)
        xr, xi = xr + pr * sr - pi * si, xi + pr * si + pi * sr
        s *= 2
    nv = xr.shape[0] // SUB
    out_r, out_i = [None] * nv, [None] * nv
    cr, ci = carry
    for v in (reversed(range(nv)) if reverse else range(nv)):
        rows = slice(v * SUB, (v + 1) * SUB)
        out_r[v] = xr[rows, :] + p8r * cr - p8i * ci
        out_i[v] = xi[rows, :] + p8r * ci + p8i * cr
        edge = slice(0, 1) if reverse else slice(SUB - 1, SUB)
        cr, ci = out_r[v][edge, :], out_i[v][edge, :]
    return jnp.concatenate(out_r, axis=0), jnp.concatenate(out_i, axis=0)


def _dep_args(dep):
    return ([], []) if dep is None else ([pl.BlockSpec(memory_space=pl.ANY)], [dep])


def _mm_nt(a, w, tm, tn, name, add=None, add_scale=1.0, dep=None):
    m, k = a.shape
    n = w.shape[0]

    def body(a_ref, w_ref, *rest):
        o_ref = rest[-1]
        if add is None:
            o_ref[...] = _dot_nt(a_ref[...], w_ref[...])
        else:
            o_ref[...] = _dot_nt(a_ref[...], w_ref[...]) + add_scale * rest[0][...]

    in_specs = [pl.BlockSpec((tm, k), lambda j, i: (i, 0)), pl.BlockSpec((tn, k), lambda j, i: (j, 0))]
    args = [a, w]
    if add is not None:
        in_specs.append(pl.BlockSpec((tm, tn), lambda j, i: (i, j)))
        args.append(add)
    dep_specs, dep_ops = _dep_args(dep)
    return pl.pallas_call(
        body, out_shape=jax.ShapeDtypeStruct((m, n), F32), grid=(n // tn, m // tm),
        in_specs=in_specs + dep_specs, out_specs=pl.BlockSpec((tm, tn), lambda j, i: (i, j)), name=name,
        compiler_params=_cparams(("parallel", "parallel")))(*args, *dep_ops)


def _mm_dw(at, b, tm, tn, ts, name):
    m, s = at.shape
    n = b.shape[1]

    def body(a_ref, b_ref, o_ref):
        @pl.when(pl.program_id(2) == 0)
        def _():
            o_ref[...] = jnp.zeros_like(o_ref)
        o_ref[...] += _dot(a_ref[...], b_ref[...])

    return pl.pallas_call(
        body, out_shape=jax.ShapeDtypeStruct((m, n), F32), grid=(m // tm, n // tn, s // ts),
        in_specs=[pl.BlockSpec((tm, ts), lambda i, j, k: (i, k)), pl.BlockSpec((ts, tn), lambda i, j, k: (k, j))],
        out_specs=pl.BlockSpec((tm, tn), lambda i, j, k: (i, j)), name=name,
        compiler_params=_cparams(("parallel", "parallel", "arbitrary")))(at, b)


def _transpose_bf16(x, name):
    s, d = x.shape
    tm = 512

    def body(x_ref, o_ref):
        o_ref[...] = x_ref[...].T.astype(BF16)

    return pl.pallas_call(
        body, out_shape=jax.ShapeDtypeStruct((d, s), BF16), grid=(s // tm,),
        in_specs=[pl.BlockSpec((tm, d), lambda i: (i, 0))], out_specs=pl.BlockSpec((d, tm), lambda i: (0, i)),
        name=name, compiler_params=_cparams(("parallel",)))(x)


def _mm_up_dw(ht, dup, name):
    d, s = ht.shape

    def body(a_ref, b_ref, o_ref):
        o_ref[...] = _dot(a_ref[...], b_ref[...])

    return pl.pallas_call(
        body, out_shape=jax.ShapeDtypeStruct((N_DEV, d, FFN_CB), F32), grid=(N_DEV,),
        in_specs=[pl.BlockSpec((d, s), lambda j: (0, 0)), pl.BlockSpec((s, FFN_CB), lambda j: (0, j))],
        out_specs=pl.BlockSpec((None, d, FFN_CB), lambda j: (_ffn_dev(j), 0, 0)), name=name,
        compiler_params=_cparams(("parallel",)))(ht, dup)


def _layer_norm(r, g, b):
    mu = jnp.mean(r, axis=-1, keepdims=True)
    xc = r - mu
    var = jnp.mean(xc * xc, axis=-1, keepdims=True)
    return xc * lax.rsqrt(var + LN_EPS) * g + b


def _proj_ln(a, w, resid, g, bias, name, transposed=True, target=None):
    s, k = a.shape
    d = w.shape[1]
    tm = 512

    def body(a_ref, w_ref, x_ref, g_ref, bias_ref, *rest):
        r = ALPHA * x_ref[...] + _dot(a_ref[...], w_ref[...])
        h = _layer_norm(r, g_ref[...], bias_ref[...])
        if target is None:
            r_ref, h_ref = rest[0], rest[1]
            h_ref[...] = h
            if transposed:
                rest[2][...] = h.T.astype(BF16)
        else:
            t_ref, r_ref, dy_ref, l_ref = rest

            @pl.when(pl.program_id(0) == 0)
            def _():
                l_ref[...] = jnp.zeros_like(l_ref)
            e = h - t_ref[...]
            dy_ref[...] = e * (1.0 / d)
            part = 0.5 * jnp.sum(jnp.mean(e * e, axis=-1, keepdims=True), axis=0, keepdims=True)
            l_ref[...] += jnp.broadcast_to(part, l_ref.shape)
        r_ref[...] = r

    row = pl.BlockSpec((tm, d), lambda i: (i, 0))
    vec = pl.BlockSpec((1, d), lambda i: (0, 0))
    in_specs = [pl.BlockSpec((tm, k), lambda i: (i, 0)), pl.BlockSpec((k, d), lambda i: (0, 0)), row, vec, vec]
    args = [a, w, resid, g, bias]
    shapes = [jax.ShapeDtypeStruct((s, d), F32), jax.ShapeDtypeStruct((s, d), F32)]
    specs = [row, row]
    if target is not None:
        in_specs.append(row)
        args.append(target)
        shapes.append(jax.ShapeDtypeStruct((1, LANE), F32))
        specs.append(pl.BlockSpec((1, LANE), lambda i: (0, 0)))
    elif transposed:
        shapes.append(jax.ShapeDtypeStruct((d, s), BF16))
        specs.append(pl.BlockSpec((d, tm), lambda i: (0, i)))
    return pl.pallas_call(
        body, out_shape=tuple(shapes), grid=(s // tm,), in_specs=in_specs, out_specs=tuple(specs), name=name,
        compiler_params=_cparams(("arbitrary",) if target is not None else ("parallel",)))(*args)


def _layer_norm_bwd(r, dh, g):
    mu = jnp.mean(r, axis=-1, keepdims=True)
    xc = r - mu
    var = jnp.mean(xc * xc, axis=-1, keepdims=True)
    rstd = lax.rsqrt(var + LN_EPS)
    xh = xc * rstd
    dxh = dh * g
    m1 = jnp.mean(dxh, axis=-1, keepdims=True)
    m2 = jnp.mean(dxh * xh, axis=-1, keepdims=True)
    return (rstd * (dxh - m1 - xh * m2), jnp.sum(dh * xh, axis=0, keepdims=True),
            jnp.sum(dh, axis=0, keepdims=True))


def _ln_bwd(r, dh, g, name, dep=None):
    s, d = r.shape
    tm = 512

    def body(r_ref, dh_ref, g_ref, *rest):
        dr_ref, dg_ref, db_ref = rest[-3:]

        @pl.when(pl.program_id(0) == 0)
        def _():
            dg_ref[...] = jnp.zeros_like(dg_ref)
            db_ref[...] = jnp.zeros_like(db_ref)
        dr_ref[...], dg_rows, db_rows = _layer_norm_bwd(r_ref[...], dh_ref[...], g_ref[...])
        dg_ref[...] += dg_rows
        db_ref[...] += db_rows

    row = pl.BlockSpec((tm, d), lambda i: (i, 0))
    vec = pl.BlockSpec((1, d), lambda i: (0, 0))
    dep_specs, dep_ops = _dep_args(dep)
    return pl.pallas_call(
        body, out_shape=(jax.ShapeDtypeStruct((s, d), F32), jax.ShapeDtypeStruct((1, d), F32),
                         jax.ShapeDtypeStruct((1, d), F32)),
        grid=(s // tm,), in_specs=[row, row, vec] + dep_specs, out_specs=(row, vec, vec), name=name,
        compiler_params=_cparams(("arbitrary",)))(r, dh, g, *dep_ops)


def _rope_tables(s):
    half = HEAD // 2
    pos = jnp.arange(s, dtype=F32)
    inv = ROPE_THETA ** (-jnp.arange(half, dtype=F32) * 2.0 / HEAD)
    ang = pos[:, None] * inv[None, :]
    cos, sin = jnp.cos(ang), jnp.sin(ang)
    cos = jnp.concatenate([cos, cos, cos, cos], axis=1)
    sin = jnp.concatenate([-sin, sin, -sin, sin], axis=1)
    return cos, sin


def _rotate(x, cos, sin):
    lane = lax.broadcasted_iota(jnp.int32, x.shape, 1)
    partner = jnp.where((lane % HEAD) < HEAD // 2, pltpu.roll(x, LANE - HEAD // 2, axis=1),
                        pltpu.roll(x, HEAD // 2, axis=1))
    return x * cos + partner * sin


def _class_rows(c, d, tm):
    return pl.ds(c, tm // d, stride=d) if d > 1 else pl.ds(0, tm)


def _dilated_spec(tm, d, w):
    return pl.BlockSpec((tm // d, d * w), lambda i: (i, 0))


def _token_scratch(tm, w):
    return pltpu.VMEM((w // LANE, tm, LANE), F32)


def _to_tokens(src_ref, dst3, d, tm):
    nj = dst3.shape[0]
    for cls in range(d):
        for j in range(nj):
            col = (cls * nj + j) * LANE
            dst3.at[j][_class_rows(cls, d, tm), :] = src_ref[:, col:col + LANE]


def _to_dilated(src3, dst_ref, d, tm):
    nj = src3.shape[0]
    for cls in range(d):
        for j in range(nj):
            col = (cls * nj + j) * LANE
            dst_ref[:, col:col + LANE] = src3.at[j][_class_rows(cls, d, tm), :].astype(dst_ref.dtype)


def _token_value(src3):
    return jnp.concatenate([src3[j] for j in range(src3.shape[0])], axis=1)


def _proj_rope(h, w_in, cos, sin, name, dep=None):
    s, d_model = h.shape
    tm = 512
    w = 3 * ATTN_W
    nj = w // LANE

    def body(h_ref, w_ref, c_ref, s_ref, *rest):
        p_ref, o_refs, rot = rest[-5], rest[-4:-1], rest[-1]
        y = _dot(h_ref[...], w_ref[...])
        p_ref[...] = y
        c, sn = c_ref[...], s_ref[...]
        for j in range(nj):
            x = y[:, j * LANE:(j + 1) * LANE]
            rot[j] = _rotate(x, c, sn) if j < 2 * ATTN_W // LANE else x
        for d, o_ref in zip(DILATIONS, o_refs):
            _to_dilated(rot, o_ref, d, tm)

    tab = pl.BlockSpec((tm, LANE), lambda i: (i, 0))
    dep_specs, dep_ops = _dep_args(dep)
    res = pl.pallas_call(
        body, out_shape=(jax.ShapeDtypeStruct((s, D_IN), F32),
                         *[jax.ShapeDtypeStruct((s // d, d * w), BF16) for d in DILATIONS]),
        grid=(s // tm,),
        in_specs=[pl.BlockSpec((tm, d_model), lambda i: (i, 0)), pl.BlockSpec((d_model, D_IN), lambda i: (0, 0)),
                  tab, tab] + dep_specs,
        out_specs=(pl.BlockSpec((tm, D_IN), lambda i: (i, 0)), *[_dilated_spec(tm, d, w) for d in DILATIONS]),
        scratch_shapes=[_token_scratch(tm, w)], name=name,
        compiler_params=_cparams(("parallel",)))(h, w_in, cos, sin, *dep_ops)
    return res[0], res[1:]


def _dproj_assemble(dqkv_list, dxr, dgate, du, cos, sin, name):
    s = dxr.shape[0]
    tm = 512
    nq = 3 * ATTN_W // LANE

    def body(*refs):
        br = refs[:9]
        dxr_ref, dg_ref, du_ref, c_ref, s_ref, o_ref = refs[9:15]
        tok = refs[15:]
        c, sn = c_ref[...], -s_ref[...]
        for part in range(3):
            for b, d in enumerate(DILATIONS[1:], start=1):
                _to_tokens(br[3 * b + part], tok[2 * part + b - 1], d, tm)
        for j in range(nq):
            part, jj = divmod(j, ATTN_W // LANE)
            x = br[part][:, jj * LANE:(jj + 1) * LANE] + tok[2 * part][jj] + tok[2 * part + 1][jj]
            if part < 2:
                x = _rotate(x, c, sn)
            o_ref[:, j * LANE:(j + 1) * LANE] = x.astype(BF16)
        o_ref[:, 3 * ATTN_W:3 * ATTN_W + LRU_W] = dxr_ref[...].astype(BF16)
        o_ref[:, 3 * ATTN_W + LRU_W:3 * ATTN_W + 2 * LRU_W] = dg_ref[...].astype(BF16)
        o_ref[:, 3 * ATTN_W + 2 * LRU_W:] = du_ref[...].astype(BF16)

    a_spec = pl.BlockSpec((tm, ATTN_W), lambda i: (i, 0))
    tab = pl.BlockSpec((tm, LANE), lambda i: (i, 0))
    ordered = [dqkv_list[b][p] for b in range(3) for p in range(3)]
    d_specs = [_dilated_spec(tm, d, ATTN_W) for d in DILATIONS for _ in range(3)]
    return pl.pallas_call(
        body, out_shape=jax.ShapeDtypeStruct((s, D_IN), BF16), grid=(s // tm,),
        in_specs=d_specs + [a_spec, a_spec, pl.BlockSpec((tm, S5_W), lambda i: (i, 0)), tab, tab],
        out_specs=pl.BlockSpec((tm, D_IN), lambda i: (i, 0)),
        scratch_shapes=[_token_scratch(tm, ATTN_W)] * 6, name=name,
        compiler_params=_cparams(("parallel",)))(*ordered, dxr, dgate, du, cos, sin)


def _attn_tiles(s, d):
    m = s // d
    tq = min(m, ATTN_TILE)
    return m, tq, tq // ATTN_BLK


def _band_mask(qb):
    qi = lax.broadcasted_iota(jnp.int32, (ATTN_BLK, 2 * ATTN_BLK), 0)
    ki = lax.broadcasted_iota(jnp.int32, (ATTN_BLK, 2 * ATTN_BLK), 1)
    dist = qi + ATTN_BLK - ki
    return (dist >= 0) & (dist <= ATTN_BLK) & ((ki >= ATTN_BLK) | (qb > 0))


def _head_cols(h):
    return (slice(h * HEAD, (h + 1) * HEAD), slice(ATTN_W + h * HEAD, ATTN_W + (h + 1) * HEAD),
            slice(2 * ATTN_W + h * HEAD, 2 * ATTN_W + (h + 1) * HEAD))


def _attn_fwd(qv, d, name):
    m = qv.shape[0]
    w3 = 3 * ATTN_W
    _, tq, n = _attn_tiles(m * d, d)
    scale = HEAD ** -0.5

    def body(x_ref, p_ref, o_ref, l_ref):
        b = pl.program_id(1)

        def block(i, first):
            r0 = 0 if first else pl.multiple_of(i * ATTN_BLK, ATTN_BLK)
            rows = pl.ds(r0, ATTN_BLK)
            valid = _band_mask(b * n + i)
            if not first:
                krows = pl.ds(pl.multiple_of(i * ATTN_BLK - ATTN_BLK, ATTN_BLK), 2 * ATTN_BLK)
            for h in range(ATTN_W // HEAD):
                qs, ks, vs = _head_cols(h)
                q = x_ref[rows, qs]
                if first:
                    k = jnp.concatenate([p_ref[:, ks], x_ref[0:ATTN_BLK, ks]], axis=0)
                    v = jnp.concatenate([p_ref[:, vs], x_ref[0:ATTN_BLK, vs]], axis=0)
                else:
                    k = x_ref[krows, ks]
                    v = x_ref[krows, vs]
                sc = jnp.where(valid, _dot_nt(q, k) * scale, -1e30)
                mx = jnp.max(sc, axis=-1, keepdims=True)
                p = jnp.exp(sc - mx)
                l = jnp.sum(p, axis=-1, keepdims=True)
                o_ref[rows, qs] = _dot(p, v) / l
                l_ref[rows, qs] = jnp.broadcast_to(mx + jnp.log(l), (ATTN_BLK, HEAD))

        block(0, True)
        if n > 1:
            def loop(i, carry):
                block(i, False)
                return carry
            lax.fori_loop(1, n, loop, 0)

    shp = jax.ShapeDtypeStruct((m, d * ATTN_W), F32)
    ospec = pl.BlockSpec((tq, ATTN_W), lambda c, b: (b, c))
    out, lse = pl.pallas_call(
        body, out_shape=(shp, shp), grid=(d, m // tq),
        in_specs=[pl.BlockSpec((tq, w3), lambda c, b: (b, c)),
                  pl.BlockSpec((ATTN_BLK, w3), lambda c, b: (jnp.maximum(b * n - 1, 0), c))],
        out_specs=(ospec, ospec), name=name,
        compiler_params=_cparams(("parallel", "parallel")))(qv, qv)
    return out, lse


def _attn_bwd(qv, ov, dov, lv, d, name, dep=None):
    m = qv.shape[0]
    w3 = 3 * ATTN_W
    _, tq, n = _attn_tiles(m * d, d)
    nb = m // ATTN_BLK
    scale = HEAD ** -0.5

    def body(x_ref, p_ref, nx_ref, o_ref, do_ref, l_ref, on_ref, don_ref, ln_ref, *rest):
        dq_ref, dk_ref, dv_ref = rest[-3:]
        b = pl.program_id(1)
        dk_ref[...] = jnp.zeros_like(dk_ref)
        dv_ref[...] = jnp.zeros_like(dv_ref)

        def grads(q, k, v, o, do, lse, valid):
            sc = jnp.where(valid, _dot_nt(q, k) * scale, -1e30)
            p = jnp.exp(sc - lse)
            delta = jnp.sum(do * o, axis=-1, keepdims=True)
            return p, p * (_dot_nt(do, v) - delta) * scale

        def block(i, first):
            r0 = 0 if first else pl.multiple_of(i * ATTN_BLK, ATTN_BLK)
            rows = pl.ds(r0, ATTN_BLK)
            valid = _band_mask(b * n + i)
            if not first:
                krows = pl.ds(pl.multiple_of(i * ATTN_BLK - ATTN_BLK, ATTN_BLK), 2 * ATTN_BLK)
            for h in range(ATTN_W // HEAD):
                qs, ks, vs = _head_cols(h)
                q = x_ref[rows, qs]
                do = do_ref[rows, qs]
                if first:
                    k = jnp.concatenate([p_ref[:, ks], x_ref[0:ATTN_BLK, ks]], axis=0)
                    v = jnp.concatenate([p_ref[:, vs], x_ref[0:ATTN_BLK, vs]], axis=0)
                else:
                    k = x_ref[krows, ks]
                    v = x_ref[krows, vs]
                p, ds = grads(q, k, v, o_ref[rows, qs], do, l_ref[rows, qs][:, 0:1], valid)
                dq_ref[rows, qs] = _dot(ds, k)
                if first:
                    dk_ref[0:ATTN_BLK, qs] += _dot_tn(ds[:, ATTN_BLK:], q)
                    dv_ref[0:ATTN_BLK, qs] += _dot_tn(p[:, ATTN_BLK:], do)
                else:
                    dk_ref[krows, qs] += _dot_tn(ds, q)
                    dv_ref[krows, qs] += _dot_tn(p, do)

        block(0, True)
        if n > 1:
            def loop(i, carry):
                block(i, False)
                return carry
            lax.fori_loop(1, n, loop, 0)

        last = slice((n - 1) * ATTN_BLK, n * ATTN_BLK)
        qi = lax.broadcasted_iota(jnp.int32, (ATTN_BLK, ATTN_BLK), 0)
        ki = lax.broadcasted_iota(jnp.int32, (ATTN_BLK, ATTN_BLK), 1)
        valid_next = (qi <= ki) & ((b + 1) * n < nb)
        for h in range(ATTN_W // HEAD):
            qs, ks, vs = _head_cols(h)
            q = nx_ref[:, qs]
            do = don_ref[:, qs]
            p, ds = grads(q, x_ref[last, ks], x_ref[last, vs], on_ref[:, qs], do, ln_ref[:, qs][:, 0:1],
                          valid_next)
            dk_ref[last, qs] += _dot_tn(ds, q)
            dv_ref[last, qs] += _dot_tn(p, do)

    nxt = lambda b: jnp.minimum((b + 1) * n, nb - 1)
    xs = pl.BlockSpec((tq, w3), lambda c, b: (b, c))
    xp = pl.BlockSpec((ATTN_BLK, w3), lambda c, b: (jnp.maximum(b * n - 1, 0), c))
    xn = pl.BlockSpec((ATTN_BLK, w3), lambda c, b: (nxt(b), c))
    a = pl.BlockSpec((tq, ATTN_W), lambda c, b: (b, c))
    an = pl.BlockSpec((ATTN_BLK, ATTN_W), lambda c, b: (nxt(b), c))
    shp = jax.ShapeDtypeStruct((m, d * ATTN_W), F32)
    dep_specs, dep_ops = _dep_args(dep)
    return pl.pallas_call(
        body, out_shape=(shp, shp, shp), grid=(d, m // tq),
        in_specs=[xs, xp, xn, a, a, a, an, an, an] + dep_specs, out_specs=(a, a, a), name=name,
        compiler_params=_cparams(("parallel", "parallel")))(qv, qv, qv, ov, dov, lv, ov, dov, lv, *dep_ops)


def _rms(x, g):
    ms = jnp.mean(x * x, axis=-1, keepdims=True)
    return x * lax.rsqrt(ms + RMS_EPS) * g


def _rms_bwd(x, g, dy):
    ms = jnp.mean(x * x, axis=-1, keepdims=True)
    r = lax.rsqrt(ms + RMS_EPS)
    dyg = dy * g
    dx = r * dyg - x * (r * r * r) * jnp.mean(x * dyg, axis=-1, keepdims=True)
    return dx, dy * x * r


def _mix_fwd(outs, lses, lru, s5, g, h_in, w_out, ln_g, ln_b, name):
    s = lru.shape[0]
    tm = 256

    def body(o1, o2, o3, l1, l2, l3, lru_ref, s5_ref, g_ref, x_ref, w_ref, lg_ref, lb_ref,
             mixed_t_ref, r_ref, h_ref, ht_ref, ov1, ov2, ov3, lv1, lv2, lv3, so2, so3, sl2, sl3):
        for d, src, dst in ((DILATIONS[1], o2, so2), (DILATIONS[2], o3, so3),
                            (DILATIONS[1], l2, sl2), (DILATIONS[2], l3, sl3)):
            _to_tokens(src, dst, d, tm)
        a1, a2, a3 = l1[...], _token_value(sl2), _token_value(sl3)
        mx = jnp.maximum(jnp.maximum(a1, a2), a3)
        e1, e2, e3 = jnp.exp(a1 - mx), jnp.exp(a2 - mx), jnp.exp(a3 - mx)
        den = e1 + e2 + e3
        o = (e1 * o1[...] + e2 * _token_value(so2) + e3 * _token_value(so3)) / den
        lse = mx + jnp.log(den)
        ov1[...] = o
        lv1[...] = lse
        for j in range(ATTN_W // LANE):
            so2[j] = o[:, j * LANE:(j + 1) * LANE]
            sl2[j] = lse[:, j * LANE:(j + 1) * LANE]
        for d, o_dst, l_dst in ((DILATIONS[1], ov2, lv2), (DILATIONS[2], ov3, lv3)):
            _to_dilated(so2, o_dst, d, tm)
            _to_dilated(sl2, l_dst, d, tm)
        gg = g_ref[...]
        mixed = jnp.concatenate([_rms(o, gg[:, :ATTN_W]),
                                 _rms(lru_ref[...], gg[:, ATTN_W:ATTN_W + LRU_W]),
                                 _rms(s5_ref[...], gg[:, ATTN_W + LRU_W:])], axis=1)
        mixed_t_ref[...] = mixed.T.astype(BF16)
        r = ALPHA * x_ref[...] + _dot(mixed, w_ref[...])
        h = _layer_norm(r, lg_ref[...], lb_ref[...])
        r_ref[...] = r
        h_ref[...] = h
        ht_ref[...] = h.T.astype(BF16)

    a = pl.BlockSpec((tm, ATTN_W), lambda i: (i, 0))
    s5s = pl.BlockSpec((tm, S5_W), lambda i: (i, 0))
    full = pl.BlockSpec((tm, D_MODEL), lambda i: (i, 0))
    vec = pl.BlockSpec((1, D_MODEL), lambda i: (0, 0))
    dil = [_dilated_spec(tm, d, ATTN_W) for d in DILATIONS]
    dshape = [jax.ShapeDtypeStruct((s // d, d * ATTN_W), F32) for d in DILATIONS]
    tshape = jax.ShapeDtypeStruct((D_MODEL, s), BF16)
    fshape = jax.ShapeDtypeStruct((s, D_MODEL), F32)
    tspec = pl.BlockSpec((D_MODEL, tm), lambda i: (0, i))
    res = pl.pallas_call(
        body, out_shape=(tshape, fshape, fshape, tshape, *dshape, *dshape),
        grid=(s // tm,),
        in_specs=dil + dil + [a, s5s, vec, full, pl.BlockSpec((D_MODEL, D_MODEL), lambda i: (0, 0)), vec, vec],
        out_specs=(tspec, full, full, tspec, *dil, *dil),
        scratch_shapes=[_token_scratch(tm, ATTN_W)] * 4, name=name,
        compiler_params=_cparams(("parallel",)))(*outs, *lses, lru, s5, g, h_in, w_out, ln_g, ln_b)
    return res[0], res[1], res[2], res[3], res[4:7], res[7:10]


def _mix_bwd(r, dh, ln_g, w_out, o, lru, s5, g, name, dep=None):
    s = lru.shape[0]
    tm = 256

    def body(r_ref, dh_ref, lg_ref, w_ref, o_ref, lru_ref, s5_ref, g_ref, *rest):
        dr_ref, dlg_ref, dlb_ref, do_ref, do2_ref, do3_ref, dlru_ref, ds5_ref, dg_ref, stage = rest[-10:]

        @pl.when(pl.program_id(0) == 0)
        def _():
            dg_ref[...] = jnp.zeros_like(dg_ref)
            dlg_ref[...] = jnp.zeros_like(dlg_ref)
            dlb_ref[...] = jnp.zeros_like(dlb_ref)
        gg = g_ref[...]
        dr, dlg_rows, dlb_rows = _layer_norm_bwd(r_ref[...], dh_ref[...], lg_ref[...])
        dr_ref[...] = dr
        dlg_ref[...] += dlg_rows
        dlb_ref[...] += dlb_rows
        dm = _dot_nt(dr, w_ref[...])
        dx, dgr = _rms_bwd(o_ref[...], gg[:, :ATTN_W], dm[:, :ATTN_W])
        do_ref[...] = dx
        for j in range(ATTN_W // LANE):
            stage[j] = dx[:, j * LANE:(j + 1) * LANE]
        _to_dilated(stage, do2_ref, DILATIONS[1], tm)
        _to_dilated(stage, do3_ref, DILATIONS[2], tm)
        dg_ref[:, :ATTN_W] += jnp.sum(dgr, axis=0, keepdims=True)
        dx, dgr = _rms_bwd(lru_ref[...], gg[:, ATTN_W:ATTN_W + LRU_W], dm[:, ATTN_W:ATTN_W + LRU_W])
        dlru_ref[...] = dx
        dg_ref[:, ATTN_W:ATTN_W + LRU_W] += jnp.sum(dgr, axis=0, keepdims=True)
        dx, dgr = _rms_bwd(s5_ref[...], gg[:, ATTN_W + LRU_W:], dm[:, ATTN_W + LRU_W:])
        ds5_ref[...] = dx
        dg_ref[:, ATTN_W + LRU_W:] += jnp.sum(dgr, axis=0, keepdims=True)

    a = pl.BlockSpec((tm, ATTN_W), lambda i: (i, 0))
    s5s = pl.BlockSpec((tm, S5_W), lambda i: (i, 0))
    full = pl.BlockSpec((tm, D_MODEL), lambda i: (i, 0))
    vec = pl.BlockSpec((1, D_MODEL), lambda i: (0, 0))
    dil = [_dilated_spec(tm, d, ATTN_W) for d in DILATIONS]
    dshape = [jax.ShapeDtypeStruct((s // d, d * ATTN_W), F32) for d in DILATIONS]
    dep_specs, dep_ops = _dep_args(dep)
    vshape = jax.ShapeDtypeStruct((1, D_MODEL), F32)
    res = pl.pallas_call(
        body, out_shape=(jax.ShapeDtypeStruct((s, D_MODEL), F32), vshape, vshape, *dshape,
                         jax.ShapeDtypeStruct((s, LRU_W), F32), jax.ShapeDtypeStruct((s, S5_W), F32), vshape),
        grid=(s // tm,),
        in_specs=[full, full, vec, pl.BlockSpec((D_MODEL, D_MODEL), lambda i: (0, 0)), a, a, s5s, vec] + dep_specs,
        out_specs=(full, vec, vec, *dil, a, s5s, vec), scratch_shapes=[_token_scratch(tm, ATTN_W)], name=name,
        compiler_params=_cparams(("arbitrary",)))(r, dh, ln_g, w_out, o, lru, s5, g, *dep_ops)
    return res[0], res[1], res[2], res[3:6], res[6], res[7], res[8]


def _lru_gate_math(xc, pre_r, pre_i, lam):
    r = _sigmoid(pre_r)
    i = _sigmoid(pre_i)
    log_a = -LRU_C * r * _softplus(-lam)
    a = jnp.exp(log_a)
    u = jnp.sqrt(-_expm1(2.0 * log_a)) * (i * xc)
    return a, u


def _lru_conv(x, prev8, cw, cb):
    y = cb + cw[LRU_CONV - 1:LRU_CONV, :] * x
    for k in range(LRU_CONV - 1):
        y = y + cw[k:k + 1, :] * _shift_down_prev(x, LRU_CONV - 1 - k, prev8)
    return y


def _lru_specs(s):
    xo = 3 * ATTN_W // LANE
    go = xo + LRU_W // LANE
    xr = pl.BlockSpec((s, LANE), lambda j: (0, xo + j))
    gt = pl.BlockSpec((s, LANE), lambda j: (0, go + j))
    cw = pl.BlockSpec((LRU_CONV, LANE), lambda j: (0, j))
    vec = pl.BlockSpec((1, LANE), lambda j: (0, j))
    wbd = pl.BlockSpec((LANE, LANE), lambda j: (j, j))
    col = pl.BlockSpec((s, LANE), lambda j: (0, j))
    return xr, gt, cw, vec, wbd, col


def _lru_fwd(proj, cw, cb, wr, br, wi, bi, lam, name):
    s = proj.shape[0]
    t = SCAN_T

    def body(xr_ref, gt_ref, cw_ref, cb_ref, wr_ref, br_ref, wi_ref, bi_ref, lam_ref, o_ref, xc_ref, a_ref, h_ref):
        cwv, cbv, lamv = cw_ref[...], cb_ref[...], lam_ref[...]
        wrv, wiv, brv, biv = wr_ref[...], wi_ref[...], br_ref[...], bi_ref[...]

        def chunk(c, carry):
            h_c, prev8 = carry
            rows = pl.ds(pl.multiple_of(c * t, t), t)
            x = xr_ref[rows, :]
            xc = _lru_conv(x, prev8, cwv, cbv)
            a, u = _lru_gate_math(xc, _dot(xc, wrv) + brv, _dot(xc, wiv) + biv, lamv)
            h = _scan_chunk(a, u, h_c)
            xc_ref[rows, :] = xc
            a_ref[rows, :] = a
            h_ref[rows, :] = h
            o_ref[rows, :] = h * _gelu(gt_ref[rows, :])
            return h[t - 1:t, :], x[t - 8:t, :]

        lax.fori_loop(0, s // t, chunk, (jnp.zeros((1, LANE), F32), jnp.zeros((8, LANE), F32)))

    xr, gt, cws, vec, wbd, col = _lru_specs(s)
    shp = jax.ShapeDtypeStruct((s, LRU_W), F32)
    return pl.pallas_call(
        body, out_shape=(shp,) * 4, grid=(LRU_W // LANE,),
        in_specs=[xr, gt, cws, vec, wbd, vec, wbd, vec, vec], out_specs=(col,) * 4, name=name,
        compiler_params=_cparams(("parallel",)))(proj, proj, cw, cb, wr, br, wi, bi, lam)


def _lru_bwd(proj, dout, xc_all, a_all, h_all, cw, cb, wr, br, wi, bi, lam, name):
    s = proj.shape[0]
    t = SCAN_T
    nc = s // t

    def body(xr_ref, gt_ref, do_ref, xc_s, a_s, h_s, cw_ref, cb_ref, wr_ref, br_ref, wi_ref, bi_ref, lam_ref,
             dxr_ref, dgt_ref, dcw_ref, dcb_ref, dwr_ref, dbr_ref, dwi_ref, dbi_ref, dlam_ref):
        cwv, cbv, lamv = cw_ref[...], cb_ref[...], lam_ref[...]
        wrv, wiv, brv, biv = wr_ref[...], wi_ref[...], br_ref[...], bi_ref[...]
        z1 = jnp.zeros((1, LANE), F32)
        zw = jnp.zeros((LANE, LANE), F32)

        def bchunk(ci, carry):
            g_next, a_next, dxc_next8, dcw, dcb, dwr, dbr, dwi, dbi, dlam = carry
            c = nc - 1 - ci
            t0 = pl.multiple_of(c * t, t)
            rows = pl.ds(t0, t)
            before = pl.ds(pl.multiple_of(jnp.maximum(t0 - 8, 0), 8), 8)
            has_prev = (c > 0).astype(F32)
            x, gt, do = xr_ref[rows, :], gt_ref[rows, :], do_ref[rows, :]
            xc, a, h = xc_s[rows, :], a_s[rows, :], h_s[rows, :]
            prev8_x = xr_ref[before, :] * has_prev
            prev8_h = h_s[before, :] * has_prev
            dgt_ref[rows, :] = do * h * _gelu_grad(gt)
            dh = do * _gelu(gt)
            a_plus = _shift_up_next(a, 1, jnp.broadcast_to(a_next, (8, LANE)))
            g = _scan_chunk(a_plus, dh, g_next, reverse=True)
            da = g * _shift_down_prev(h, 1, prev8_h)
            pre_r = _dot(xc, wrv) + brv
            pre_i = _dot(xc, wiv) + biv
            _, vjp = jax.vjp(_lru_gate_math, xc, pre_r, pre_i, lamv)
            dxc, dpre_r, dpre_i, dlam_c = vjp((da, g))
            dxc = dxc + _dot_nt(dpre_r, wrv) + _dot_nt(dpre_i, wiv)
            dx = cwv[LRU_CONV - 1:LRU_CONV, :] * dxc
            dcw_rows = [None] * LRU_CONV
            dcw_rows[LRU_CONV - 1] = jnp.sum(dxc * x, axis=0, keepdims=True)
            for k in range(LRU_CONV - 1):
                sh = LRU_CONV - 1 - k
                dx = dx + cwv[k:k + 1, :] * _shift_up_next(dxc, sh, dxc_next8)
                dcw_rows[k] = jnp.sum(dxc * _shift_down_prev(x, sh, prev8_x), axis=0, keepdims=True)
            dxr_ref[rows, :] = dx
            return (g[0:1, :], a[0:1, :], dxc[0:8, :],
                    dcw + jnp.concatenate(dcw_rows, axis=0),
                    dcb + jnp.sum(dxc, axis=0, keepdims=True),
                    dwr + _dot_tn(xc, dpre_r), dbr + jnp.sum(dpre_r, axis=0, keepdims=True),
                    dwi + _dot_tn(xc, dpre_i), dbi + jnp.sum(dpre_i, axis=0, keepdims=True),
                    dlam + dlam_c)

        init = (z1, z1, jnp.zeros((8, LANE), F32), jnp.zeros((LRU_CONV, LANE), F32), z1, zw, z1, zw, z1, z1)
        res = lax.fori_loop(0, nc, bchunk, init)
        dcw_ref[...] = res[3]
        dcb_ref[...] = res[4]
        dwr_ref[...] = res[5]
        dbr_ref[...] = res[6]
        dwi_ref[...] = res[7]
        dbi_ref[...] = res[8]
        dlam_ref[...] = res[9]

    xr, gt, cws, vec, wbd, col = _lru_specs(s)
    vshape = jax.ShapeDtypeStruct((1, LRU_W), F32)
    wshape = jax.ShapeDtypeStruct((LRU_W, LRU_W), F32)
    return pl.pallas_call(
        body,
        out_shape=(jax.ShapeDtypeStruct((s, LRU_W), F32), jax.ShapeDtypeStruct((s, LRU_W), F32),
                   jax.ShapeDtypeStruct((LRU_CONV, LRU_W), F32), vshape, wshape, vshape, wshape, vshape, vshape),
        grid=(LRU_W // LANE,),
        in_specs=[xr, gt, col, col, col, col, cws, vec, wbd, vec, wbd, vec, vec],
        out_specs=(col, col, cws, vec, wbd, vec, wbd, vec, vec), name=name,
        compiler_params=_cparams(("parallel",)))(proj, proj, dout, xc_all, a_all, h_all, cw, cb, wr, br, wi, bi,
                                                 lam)


def _s5_disc_math(a_re, a_im, log_step, bt_re, bt_im):
    step = jnp.exp(log_step)
    dt_re, dt_im = step * a_re, step * a_im
    mag = jnp.exp(dt_re)
    ab_re, ab_im = mag * jnp.cos(dt_im), mag * jnp.sin(dt_im)
    z_re, z_im = ab_re - 1.0, ab_im
    den = a_re * a_re + a_im * a_im
    f_re = (z_re * a_re + z_im * a_im) / den
    f_im = (z_im * a_re - z_re * a_im) / den
    bb_re = f_re * bt_re - f_im * bt_im
    bb_im = f_re * bt_im + f_im * bt_re
    return ab_re, ab_im, bb_re, bb_im


def _s5_disc_fwd(a_re, a_im, log_step, bt_re, bt_im, name):
    def body(ar, ai, ls, br, bi, o1, o2, o3, o4):
        r = _s5_disc_math(ar[...], ai[...], ls[...], br[...], bi[...])
        o1[...], o2[...], o3[...], o4[...] = r

    shp = jax.ShapeDtypeStruct(a_re.shape, F32)
    return pl.pallas_call(body, out_shape=(shp,) * 4, name=name)(a_re, a_im, log_step, bt_re, bt_im)


def _s5_disc_bwd(a_re, a_im, log_step, bt_re, bt_im, cts, name):
    def body(ar, ai, ls, br, bi, c1, c2, c3, c4, o1, o2, o3, o4, o5):
        _, vjp = jax.vjp(_s5_disc_math, ar[...], ai[...], ls[...], br[...], bi[...])
        r = vjp((c1[...], c2[...], c3[...], c4[...]))
        o1[...], o2[...], o3[...], o4[...], o5[...] = r

    shp = jax.ShapeDtypeStruct(a_re.shape, F32)
    return pl.pallas_call(body, out_shape=(shp,) * 5, name=name)(a_re, a_im, log_step, bt_re, bt_im, *cts)


def _s5_u_specs(s):
    uo = (3 * ATTN_W + 2 * LRU_W) // LANE
    return (pl.BlockSpec((s, LANE), lambda j: (0, uo)), pl.BlockSpec((s, LANE), lambda j: (0, uo + 1)))


def _s5_scan_fwd(proj, b_re, b_im, lam_re, lam_im, c_re, c_im, name):
    s = proj.shape[0]
    t = SCAN_T

    def body(u0_ref, u1_ref, bre_ref, bim_ref, lre_ref, lim_ref, cre_ref, cim_ref, xre_ref, xim_ref, y_ref):
        @pl.when(pl.program_id(0) == 0)
        def _():
            y_ref[...] = jnp.zeros_like(y_ref)
        lr, li = lre_ref[...], lim_ref[...]
        consts = _cscan_consts(lr, li, False)
        bre, bim, cre, cim = bre_ref[...], bim_ref[...], cre_ref[...], cim_ref[...]

        def chunk(c, carry):
            cr, ci = carry
            rows = pl.ds(pl.multiple_of(c * t, t), t)
            u = jnp.concatenate([u0_ref[rows, :], u1_ref[rows, :]], axis=1).astype(BF16)
            xr, xi = _cscan_chunk(_dot(u, bre), _dot(u, bim), consts, (cr, ci))
            xre_ref[rows, :] = xr
            xim_ref[rows, :] = xi
            y_ref[rows, :] += _dot(xr, cre) - _dot(xi, cim)
            return xr[t - 1:t, :], xi[t - 1:t, :]

        z = jnp.zeros((1, S5_BLK), F32)
        lax.fori_loop(0, s // t, chunk, (z, z))

    u0, u1 = _s5_u_specs(s)
    bsp = pl.BlockSpec((S5_W, S5_BLK), lambda j: (0, j))
    csp = pl.BlockSpec((S5_BLK, S5_W), lambda j: (j, 0))
    vec = pl.BlockSpec((1, S5_BLK), lambda j: (0, j))
    xsp = pl.BlockSpec((s, S5_BLK), lambda j: (0, j))
    ysp = pl.BlockSpec((s, S5_W), lambda j: (0, 0))
    xshape = jax.ShapeDtypeStruct((s, S5_STATES), F32)
    return pl.pallas_call(
        body, out_shape=(xshape, xshape, jax.ShapeDtypeStruct((s, S5_W), F32)),
        grid=(S5_STATES // S5_BLK,), in_specs=[u0, u1, bsp, bsp, vec, vec, csp, csp],
        out_specs=(xsp, xsp, ysp), name=name,
        compiler_params=_cparams(("arbitrary",)))(proj, proj, b_re, b_im, lam_re, lam_im, c_re, c_im)


def _s5_scan_bwd(proj, dy, du_init, x_re, x_im, b_re, b_im, lam_re, lam_im, c_re, c_im, name):
    s = proj.shape[0]
    t = SCAN_T
    nc = s // t

    def body(u0_ref, u1_ref, dy_ref, dui_ref, xre_ref, xim_ref, bre_ref, bim_ref, lre_ref, lim_ref,
             cre_ref, cim_ref, du_ref, dlr_ref, dli_ref, dbr_ref, dbi_ref, dcr_ref, dci_ref):
        @pl.when(pl.program_id(0) == 0)
        def _():
            du_ref[...] = dui_ref[...]
        mr, mi = lre_ref[...], -lim_ref[...]
        consts = _cscan_consts(mr, mi, True)
        bre, bim, cre, cim = bre_ref[...], bim_ref[...], cre_ref[...], cim_ref[...]
        dbr_ref[...] = jnp.zeros_like(dbr_ref)
        dbi_ref[...] = jnp.zeros_like(dbi_ref)
        dcr_ref[...] = jnp.zeros_like(dcr_ref)
        dci_ref[...] = jnp.zeros_like(dci_ref)

        def chunk(ci_, carry):
            gnr, gni, dlr, dli = carry
            c = nc - 1 - ci_
            t0 = pl.multiple_of(c * t, t)
            rows = pl.ds(t0, t)
            before = pl.ds(pl.multiple_of(jnp.maximum(t0 - 8, 0), 8), 8)
            has_prev = (c > 0).astype(F32)
            dyc = dy_ref[rows, :].astype(BF16)
            u = jnp.concatenate([u0_ref[rows, :], u1_ref[rows, :]], axis=1).astype(BF16)
            gr, gi = _cscan_chunk(_dot_nt(dyc, cre), -_dot_nt(dyc, cim), consts, (gnr, gni), reverse=True)
            xr, xi = xre_ref[rows, :], xim_ref[rows, :]
            xpr = _shift_down_prev(xr, 1, xre_ref[before, :] * has_prev)
            xpi = _shift_down_prev(xi, 1, xim_ref[before, :] * has_prev)
            dlr = dlr + jnp.sum(gr * xpr + gi * xpi, axis=0, keepdims=True)
            dli = dli + jnp.sum(gi * xpr - gr * xpi, axis=0, keepdims=True)
            du_ref[rows, :] += _dot_nt(gr, bre) + _dot_nt(gi, bim)
            dbr_ref[...] += _dot_tn(u, gr)
            dbi_ref[...] += _dot_tn(u, gi)
            dcr_ref[...] += _dot_tn(xr, dyc)
            dci_ref[...] -= _dot_tn(xi, dyc)
            return gr[0:1, :], gi[0:1, :], dlr, dli

        z = jnp.zeros((1, S5_BLK), F32)
        res = lax.fori_loop(0, nc, chunk, (z, z, z, z))
        dlr_ref[...] = res[2]
        dli_ref[...] = res[3]

    u0, u1 = _s5_u_specs(s)
    bsp = pl.BlockSpec((S5_W, S5_BLK), lambda j: (0, j))
    csp = pl.BlockSpec((S5_BLK, S5_W), lambda j: (j, 0))
    vec = pl.BlockSpec((1, S5_BLK), lambda j: (0, j))
    xsp = pl.BlockSpec((s, S5_BLK), lambda j: (0, j))
    ysp = pl.BlockSpec((s, S5_W), lambda j: (0, 0))
    return pl.pallas_call(
        body,
        out_shape=(jax.ShapeDtypeStruct((s, S5_W), F32),
                   jax.ShapeDtypeStruct((1, S5_STATES), F32), jax.ShapeDtypeStruct((1, S5_STATES), F32),
                   jax.ShapeDtypeStruct((S5_W, S5_STATES), F32), jax.ShapeDtypeStruct((S5_W, S5_STATES), F32),
                   jax.ShapeDtypeStruct((S5_STATES, S5_W), F32), jax.ShapeDtypeStruct((S5_STATES, S5_W), F32)),
        grid=(S5_STATES // S5_BLK,),
        in_specs=[u0, u1, ysp, ysp, xsp, xsp, bsp, bsp, vec, vec, csp, csp],
        out_specs=(ysp, vec, vec, bsp, bsp, csp, csp), name=name,
        compiler_params=_cparams(("arbitrary",)))(
            proj, proj, dy, du_init, x_re, x_im, b_re, b_im, lam_re, lam_im, c_re, c_im)


def _s5_out_fwd(proj, y_acc, dvec, w_glu, b_glu, name):
    s = proj.shape[0]
    tm = 512
    uo = (3 * ATTN_W + 2 * LRU_W) // LANE

    def body(u0_ref, u1_ref, y_ref, d_ref, w_ref, b_ref, o_ref, yp_ref):
        u = jnp.concatenate([u0_ref[...], u1_ref[...]], axis=1)
        y = y_ref[...] + d_ref[...] * u
        yp_ref[...] = y
        yg = _gelu(y)
        o_ref[...] = yg * _sigmoid(_dot(yg, w_ref[...]) + b_ref[...])

    u0 = pl.BlockSpec((tm, LANE), lambda i: (i, uo))
    u1 = pl.BlockSpec((tm, LANE), lambda i: (i, uo + 1))
    row = pl.BlockSpec((tm, S5_W), lambda i: (i, 0))
    vec = pl.BlockSpec((1, S5_W), lambda i: (0, 0))
    wsp = pl.BlockSpec((S5_W, S5_W), lambda i: (0, 0))
    shp = jax.ShapeDtypeStruct((s, S5_W), F32)
    return pl.pallas_call(
        body, out_shape=(shp, shp), grid=(s // tm,), in_specs=[u0, u1, row, vec, wsp, vec],
        out_specs=(row, row), name=name,
        compiler_params=_cparams(("parallel",)))(proj, proj, y_acc, dvec, w_glu, b_glu)


def _s5_out_bwd(proj, y_pre, dout, dvec, w_glu, b_glu, name, dep=None):
    s = proj.shape[0]
    tm = 512
    uo = (3 * ATTN_W + 2 * LRU_W) // LANE

    def body(u0_ref, u1_ref, y_ref, do_ref, d_ref, w_ref, b_ref, *rest):
        dy_ref, dud_ref, dd_ref, dw_ref, db_ref = rest[-5:]

        @pl.when(pl.program_id(0) == 0)
        def _():
            dd_ref[...] = jnp.zeros_like(dd_ref)
            dw_ref[...] = jnp.zeros_like(dw_ref)
            db_ref[...] = jnp.zeros_like(db_ref)
        u = jnp.concatenate([u0_ref[...], u1_ref[...]], axis=1)
        y = y_ref[...]
        do = do_ref[...]
        yg = _gelu(y)
        sg = _sigmoid(_dot(yg, w_ref[...]) + b_ref[...])
        dz = do * yg * sg * (1.0 - sg)
        dyg = do * sg + _dot_nt(dz, w_ref[...])
        dy = dyg * _gelu_grad(y)
        dy_ref[...] = dy
        dud_ref[...] = d_ref[...] * dy
        dd_ref[...] += jnp.sum(dy * u, axis=0, keepdims=True)
        dw_ref[...] += _dot_tn(yg, dz)
        db_ref[...] += jnp.sum(dz, axis=0, keepdims=True)

    u0 = pl.BlockSpec((tm, LANE), lambda i: (i, uo))
    u1 = pl.BlockSpec((tm, LANE), lambda i: (i, uo + 1))
    row = pl.BlockSpec((tm, S5_W), lambda i: (i, 0))
    vec = pl.BlockSpec((1, S5_W), lambda i: (0, 0))
    wsp = pl.BlockSpec((S5_W, S5_W), lambda i: (0, 0))
    shp = jax.ShapeDtypeStruct((s, S5_W), F32)
    vshape = jax.ShapeDtypeStruct((1, S5_W), F32)
    dep_specs, dep_ops = _dep_args(dep)
    return pl.pallas_call(
        body, out_shape=(shp, shp, vshape, jax.ShapeDtypeStruct((S5_W, S5_W), F32), vshape),
        grid=(s // tm,), in_specs=[u0, u1, row, row, vec, wsp, vec] + dep_specs,
        out_specs=(row, row, vec, wsp, vec), name=name,
        compiler_params=_cparams(("arbitrary",)))(proj, proj, y_pre, dout, dvec, w_glu, b_glu, *dep_ops)


def _ffn_conv(x, prev8, cw, cb):
    y = cb + cw[FFN_CONV - 1:FFN_CONV, :] * x
    for k in range(FFN_CONV - 1):
        y = y + cw[k:k + 1, :] * _shift_down_prev(x, FFN_CONV - 1 - k, prev8)
    return y


def _ffn_up_act(h, wg, cw, cb, name, dep=None):
    s, d = h.shape
    tm = 512
    tb = 2 * FFN_CB
    nt = D_FF // FFN_CB

    def body(h_ref, wgate_ref, wval_ref, cw_ref, cb_ref, *rest):
        up_ref, y_ref, o_ref, ot_ref, carry = rest[-5:]
        t = pl.program_id(1)

        @pl.when(pl.program_id(0) == 0)
        def _():
            carry[t] = jnp.zeros((8, tb), F32)
        hb = h_ref[...].astype(BF16)
        x = jnp.concatenate([_dot(hb, wgate_ref[...]), _dot(hb, wval_ref[...])], axis=1)
        up_ref[...] = x
        y = _ffn_conv(x, carry[t], cw_ref[...], cb_ref[...])
        y_ref[...] = y
        carry[t] = x[tm - 8:tm, :]
        act = _gelu(y[:, :FFN_CB]) * y[:, FFN_CB:]
        o_ref[...] = act.astype(BF16)
        ot_ref[...] = act.T.astype(BF16)

    dep_specs, dep_ops = _dep_args(dep)
    return pl.pallas_call(
        body, out_shape=(jax.ShapeDtypeStruct((s, 2 * D_FF), F32), jax.ShapeDtypeStruct((s, 2 * D_FF), F32),
                         jax.ShapeDtypeStruct((s, D_FF), BF16), jax.ShapeDtypeStruct((D_FF, s), BF16)),
        grid=(s // tm, nt),
        in_specs=[pl.BlockSpec((tm, d), lambda i, t: (i, 0)),
                  pl.BlockSpec((None, d, FFN_CB), lambda i, t: (t, 0, 0)),
                  pl.BlockSpec((None, d, FFN_CB), lambda i, t: (t + nt, 0, 0)),
                  pl.BlockSpec((FFN_CONV, tb), lambda i, t: (0, t)),
                  pl.BlockSpec((1, tb), lambda i, t: (0, t))] + dep_specs,
        out_specs=(pl.BlockSpec((tm, tb), lambda i, t: (i, t)), pl.BlockSpec((tm, tb), lambda i, t: (i, t)),
                   pl.BlockSpec((tm, FFN_CB), lambda i, t: (i, t)), pl.BlockSpec((FFN_CB, tm), lambda i, t: (t, i))),
        scratch_shapes=[pltpu.VMEM((nt, 8, tb), F32)], name=name,
        compiler_params=_cparams(("arbitrary", "arbitrary")))(h, wg, wg, cw, cb, *dep_ops)


def _ffn_bwd(up, y_conv, dr, w_down, wg, cw, name):
    s = up.shape[0]
    d = dr.shape[1]
    tm = 256
    tb = 2 * FFN_CB
    nr = s // tm
    nt = D_FF // FFN_CB

    def body(x_ref, p_ref, y_ref, dr_ref, wd_ref, wgate_ref, wval_ref, cw_ref,
             dup_ref, dh_ref, dcw_ref, dcb_ref, carry):
        i, t = pl.program_id(0), pl.program_id(1)

        @pl.when(i == 0)
        def _():
            carry[t] = jnp.zeros((8, tb), F32)

        @pl.when(t == 0)
        def _():
            dh_ref[...] = ALPHA * dr_ref[...]
        prev8 = p_ref[...] * (i < nr - 1).astype(F32)
        cwv = cw_ref[...]
        x = x_ref[...]
        dact = _dot_nt(dr_ref[...], wd_ref[...])
        shifted = [_shift_down_prev(x, FFN_CONV - 1 - k, prev8) for k in range(FFN_CONV - 1)]
        gate, val = y_ref[:, :FFN_CB], y_ref[:, FFN_CB:]
        dy = jnp.concatenate([dact * val * _gelu_grad(gate), dact * _gelu(gate)], axis=1)
        next8 = carry[t]
        carry[t] = dy[0:8, :]
        dx = cwv[FFN_CONV - 1:FFN_CONV, :] * dy
        dcw_rows = [None] * FFN_CONV
        dcw_rows[FFN_CONV - 1] = jnp.sum(dy * x, axis=0, keepdims=True)
        for k in range(FFN_CONV - 1):
            dx = dx + cwv[k:k + 1, :] * _shift_up_next(dy, FFN_CONV - 1 - k, next8)
            dcw_rows[k] = jnp.sum(dy * shifted[k], axis=0, keepdims=True)
        dup = dx.astype(BF16)
        dup_ref[...] = dup
        dh_ref[...] += _dot_nt(dup[:, :FFN_CB], wgate_ref[...]) + _dot_nt(dup[:, FFN_CB:], wval_ref[...])
        dcw_ref[...] = jnp.concatenate(dcw_rows, axis=0)
        dcb_ref[...] = jnp.sum(dy, axis=0, keepdims=True)

    row = lambda i: nr - 1 - i
    return pl.pallas_call(
        body, out_shape=(jax.ShapeDtypeStruct((s, 2 * D_FF), BF16), jax.ShapeDtypeStruct((s, d), F32),
                         jax.ShapeDtypeStruct((nr, FFN_CONV, 2 * D_FF), F32),
                         jax.ShapeDtypeStruct((nr, 1, 2 * D_FF), F32)),
        grid=(nr, nt),
        in_specs=[pl.BlockSpec((tm, tb), lambda i, t: (row(i), t)),
                  pl.BlockSpec((8, tb), lambda i, t: (jnp.maximum(row(i) * (tm // 8) - 1, 0), t)),
                  pl.BlockSpec((tm, tb), lambda i, t: (row(i), t)),
                  pl.BlockSpec((tm, d), lambda i, t: (row(i), 0)),
                  pl.BlockSpec((FFN_CB, d), lambda i, t: (t, 0)),
                  pl.BlockSpec((None, d, FFN_CB), lambda i, t: (t, 0, 0)),
                  pl.BlockSpec((None, d, FFN_CB), lambda i, t: (t + nt, 0, 0)),
                  pl.BlockSpec((FFN_CONV, tb), lambda i, t: (0, t))],
        out_specs=(pl.BlockSpec((tm, tb), lambda i, t: (row(i), t)),
                   pl.BlockSpec((tm, d), lambda i, t: (row(i), 0)),
                   pl.BlockSpec((None, FFN_CONV, tb), lambda i, t: (row(i), 0, t)),
                   pl.BlockSpec((None, 1, tb), lambda i, t: (row(i), 0, t))),
        scratch_shapes=[pltpu.VMEM((nt, 8, tb), F32)], name=name,
        compiler_params=_cparams(("arbitrary", "arbitrary")))(up, up, y_conv, dr, w_down, wg, wg, cw)


def _sum_partials(ld_ref):
    gg = ld_ref[0].astype(F32)
    for k in range(1, N_DEV):
        gg = gg + ld_ref[k].astype(F32)
    return gg


def _adam_update(w, g, m, v):
    mn = ADAM_B1 * m + (1.0 - ADAM_B1) * g
    vn = ADAM_B2 * v + (1.0 - ADAM_B2) * (g * g)
    m_hat = mn / (1.0 - ADAM_B1 ** ADAM_STEP)
    v_hat = vn / (1.0 - ADAM_B2 ** ADAM_STEP)
    return -ADAM_LR * (m_hat / (jnp.sqrt(v_hat) + ADAM_EPS) + ADAM_WD * w), mn, vn


def _adamw_many(landed, ws, ms, vs, name):
    n, nl = len(ws), len(landed)

    def body(*refs):
        ld = refs[:nl * n]
        w_refs, m_refs, v_refs = (refs[(nl + k) * n:(nl + k + 1) * n] for k in range(3))
        outs = refs[(nl + 3) * n:]
        for i in range(n):
            for l in range(nl):
                one = slice(l, l + 1)
                gg = _sum_partials(ld[l * n + i])
                outs[i][one] = gg
                outs[n + i][one], outs[2 * n + i][one], outs[3 * n + i][one] = _adam_update(
                    w_refs[i][one], gg, m_refs[i][one], v_refs[i][one])

    vm = pl.BlockSpec(memory_space=pltpu.VMEM)
    shapes = [jax.ShapeDtypeStruct(w.shape, F32) for w in ws] * 4
    res = pl.pallas_call(
        body, out_shape=tuple(shapes), in_specs=[vm] * ((nl + 3) * n), out_specs=tuple([vm] * (4 * n)),
        name=name, compiler_params=_cparams())(*[a for layer in landed for a in layer], *ws, *ms, *vs)
    return res[:n], res[n:2 * n], res[2 * n:3 * n], res[3 * n:]


def _adamw_sum(landed, w, m, v, layer, prev, name):
    _, r, c = landed.shape
    nl = w.shape[0]
    tm = 8
    for cand in (512, 256, 128, 64, 32, 16):
        if r % cand == 0 and N_DEV * cand * c * 4 <= 4 * 1024 * 1024:
            tm = cand
            break

    def body(*refs):
        ld_ref, w_ref, m_ref, v_ref = refs[:4]
        g_ref, d_ref, mo_ref, vo_ref = refs[-4:]
        gg = _sum_partials(ld_ref)
        g_ref[...] = gg
        d_ref[...], mo_ref[...], vo_ref[...] = _adam_update(w_ref[...], gg, m_ref[...], v_ref[...])

    blk = pl.BlockSpec((None, tm, c), lambda i: (layer, i, 0))
    in_specs = [pl.BlockSpec((N_DEV, tm, c), lambda i: (0, i, 0)), blk, blk, blk]
    args = [landed, w, m, v]
    aliases = {}
    if prev is not None:
        in_specs += [pl.BlockSpec(memory_space=pl.ANY)] * 4
        args += list(prev)
        aliases = {4 + k: k for k in range(4)}
    shp = jax.ShapeDtypeStruct((nl, r, c), F32)
    return pl.pallas_call(
        body, out_shape=(shp,) * 4, grid=(r // tm,), in_specs=in_specs, out_specs=(blk,) * 4,
        input_output_aliases=aliases, name=name, compiler_params=_cparams(("parallel",)))(*args)


def _all_gather(shards, name):
    na = len(shards)

    def body(*refs):
        x_refs, out_refs = refs[:na], refs[na:2 * na]
        send_sems, recv_sems, local_sems = refs[2 * na:]
        x, y, c = lax.axis_index("x"), lax.axis_index("y"), lax.axis_index("c")
        me, sibling = (x, y, c), (x, y, 1 - c)
        chips = [(1 - x, y), (x, 1 - y), (1 - x, 1 - y)]

        def copy(a, k, block, to, src=None):
            dst = out_refs[a].at[4 * block[0] + 2 * block[1] + block[2]]
            return pltpu.make_async_remote_copy(
                src_ref=dst if src is None else src, dst_ref=dst,
                send_sem=send_sems.at[7 * a + k], recv_sem=recv_sems.at[7 * a + k],
                device_id=to, device_id_type=pl.DeviceIdType.MESH)

        mine, first, passed = [], [], []
        for a in range(na):
            cp = pltpu.make_async_copy(x_refs[a], out_refs[a].at[4 * x + 2 * y + c], local_sems.at[a])
            cp.start()
            mine.append(cp)
            cps = [copy(a, 0, me, sibling, src=x_refs[a])]
            cps += [copy(a, 1 + j, me, (*chip, c), src=x_refs[a]) for j, chip in enumerate(chips)]
            for cp in cps:
                cp.start()
            first += cps
        for j, chip in enumerate(chips):
            for a in range(na):
                copy(a, 1 + j, (*chip, c), me).wait_recv()
                cp = copy(a, 4 + j, (*chip, c), sibling)
                cp.start()
                passed.append(cp)
        for a in range(na):
            copy(a, 0, sibling, me).wait_recv()
            for j, chip in enumerate(chips):
                copy(a, 4 + j, (*chip, 1 - c), me).wait_recv()
        for cp in first + passed:
            cp.wait_send()
        for cp in mine:
            cp.wait()

    anyspec = pl.BlockSpec(memory_space=pl.ANY)
    return pl.pallas_call(
        body, out_shape=tuple(jax.ShapeDtypeStruct((N_DEV,) + t.shape, t.dtype) for t in shards),
        in_specs=[anyspec] * na, out_specs=tuple([anyspec] * na),
        scratch_shapes=[pltpu.SemaphoreType.DMA((7 * na,)), pltpu.SemaphoreType.DMA((7 * na,)),
                        pltpu.SemaphoreType.DMA((na,))],
        name=name)(*shards)


_HBM = pl.BlockSpec(memory_space=pltpu.HBM)
_SEM = pl.BlockSpec(memory_space=pltpu.SEMAPHORE)
_EFFECT = pltpu.SideEffectType.DATAFLOW_SIDE_EFFECTING


def _exchange_copies(src_refs, land_refs, send_sems, recv_sems, local_sems, gather):
    x, y, c = lax.axis_index("x"), lax.axis_index("y"), lax.axis_index("c")
    me = 4 * x + 2 * y + c
    per_array = send_sems.shape[0] > N_DEV - 1
    local, remote = [], []
    for a, (src, land) in enumerate(zip(src_refs, land_refs)):
        local.append(pltpu.make_async_copy(src if gather else src.at[me], land.at[me],
                                           local_sems.at[a if per_array else 0]))
    for k in range(1, N_DEV):
        px = x ^ ((k >> 2) & 1)
        py = y ^ ((k >> 1) & 1)
        pc = c ^ (k & 1)
        for a, (src, land) in enumerate(zip(src_refs, land_refs)):
            remote.append(pltpu.make_async_remote_copy(
                src_ref=src if gather else src.at[4 * px + 2 * py + pc], dst_ref=land.at[me],
                send_sem=send_sems.at[(7 * a if per_array else 0) + k - 1],
                recv_sem=recv_sems.at[(7 * a if per_array else 0) + k - 1],
                device_id=(px, py, pc), device_id_type=pl.DeviceIdType.MESH))
    return local, remote


def _exchange_start(srcs, gather, name, dep=None):
    na = len(srcs)
    ns = na if na <= 4 else 1
    lands = [lax.empty(((N_DEV,) + t.shape) if gather else t.shape, t.dtype) for t in srcs]

    def body(*refs):
        src_refs, land_refs = refs[:na], refs[na:2 * na]
        nin = 2 * na + (0 if dep is None else 1)
        send_sems, recv_sems, local_sems = refs[nin:nin + 3]
        token = refs[-1]
        local, remote = _exchange_copies(src_refs, land_refs, send_sems, recv_sems, local_sems, gather)
        for cp in local + remote:
            cp.start()
        token[...] = jnp.zeros_like(token)

    dep_specs, dep_ops = _dep_args(dep)
    hbm = lambda t: pltpu.HBM(t.shape, t.dtype)
    out = pl.pallas_call(
        body, name=name,
        out_shape=(pltpu.SemaphoreType.DMA((7 * ns,)), pltpu.SemaphoreType.DMA((7 * ns,)),
                   pltpu.SemaphoreType.DMA((ns,)), *[hbm(t) for t in srcs], *[hbm(t) for t in lands],
                   jax.ShapeDtypeStruct((8, LANE), F32)),
        in_specs=[_HBM] * (2 * na) + dep_specs,
        out_specs=(_SEM, _SEM, _SEM, *[_HBM] * (2 * na), pl.BlockSpec(memory_space=pltpu.VMEM)),
        input_output_aliases={i: 3 + i for i in range(2 * na)},
        compiler_params=pltpu.CompilerParams(has_side_effects=_EFFECT),
    )(*[pltpu.with_memory_space_constraint(t, pltpu.HBM) for t in srcs + lands], *dep_ops)
    return (out[:3], out[3:3 + na], out[3 + na:3 + 2 * na]), out[-1]


def _exchange_wait(handle, gather, after, name):
    sems, srcs, lands = handle
    na = len(srcs)

    def body(*refs):
        src_refs, land_refs = refs[:na], refs[na:2 * na]
        send_sems, recv_sems, local_sems = refs[2 * na:2 * na + 3]
        local, remote = _exchange_copies(src_refs, land_refs, send_sems, recv_sems, local_sems, gather)
        for cp in remote:
            cp.wait_send()
            cp.wait_recv()
        for cp in local:
            cp.wait()

    hbm = lambda t: pltpu.HBM(t.shape, t.dtype)
    out = pl.pallas_call(
        body, name=name, out_shape=(*[hbm(t) for t in srcs], *[hbm(t) for t in lands]),
        in_specs=[_HBM] * (2 * na) + [_SEM] * 3 + [pl.BlockSpec(memory_space=pl.ANY)],
        out_specs=tuple([_HBM] * (2 * na)), input_output_aliases={i: i for i in range(2 * na)},
        compiler_params=pltpu.CompilerParams(has_side_effects=_EFFECT),
    )(*srcs, *lands, *sems, after)
    return out[na:]


def _block_diag(w):
    h, a, b = w.shape
    eye = jnp.eye(h, dtype=w.dtype)
    return (w[:, :, None, :] * eye[:, None, :, None]).reshape(h * a, h * b)


def _block_diag_extract(m, h):
    a, b = m.shape[0] // h, m.shape[1] // h
    return jnp.stack([m[i * a:(i + 1) * a, i * b:(i + 1) * b] for i in range(h)], axis=0)


def _block_diag_take(m, h):
    a, b = m.shape[0] // h, m.shape[1] // h
    eye = jnp.eye(h, dtype=m.dtype)
    return (m.reshape(h, a, h, b) * eye[:, None, :, None]).sum(axis=2)


def _ffn_interleave(w):
    lead = w.shape[:-1]
    nb = D_FF // FFN_CB
    return jnp.swapaxes(w.reshape(*lead, 2, nb, FFN_CB), -3, -2).reshape(*lead, 2 * D_FF)


def _ffn_deinterleave(w):
    lead = w.shape[:-1]
    nb = D_FF // FFN_CB
    return jnp.swapaxes(w.reshape(*lead, nb, 2, FFN_CB), -3, -2).reshape(*lead, 2 * D_FF)


def _gather_full(gathered, axis):
    shape = list(gathered.shape[1:])
    shape[axis] *= N_DEV
    return jnp.moveaxis(gathered, 0, axis).reshape(shape)


def _scatter_blocks(full, axis):
    shape = list(full.shape)
    shape[axis:axis + 1] = [N_DEV, shape[axis] // N_DEV]
    return jnp.moveaxis(full.reshape(shape), axis, 0)


def _pad_to(flat, mult):
    pad = (-flat.shape[-1]) % mult
    if pad:
        flat = jnp.concatenate([flat, jnp.zeros(flat.shape[:-1] + (pad,), flat.dtype)], axis=-1)
    return flat


def _layer_fwd(h_in, h_in_t, w, cos, sin, l, dep, get_ffn, target=None):
    tag = "l%d_" % l
    proj, qkv = _proj_rope(h_in, w['w_in'], cos, sin, tag + "proj_rope", dep=dep)
    outs, lses = [], []
    for d, qv in zip(DILATIONS, qkv):
        o, ls = _attn_fwd(qv, d, tag + "attn_d%d" % d)
        outs.append(o)
        lses.append(ls)
    lru, *lru_saved = _lru_fwd(proj, w['lru_conv_w'], w['lru_conv_b'], w['lru_wr'], w['lru_br'], w['lru_wi'],
                               w['lru_bi'], w['lru_lambda'], tag + "lru")
    x_re, x_im, y_acc = _s5_scan_fwd(proj, w['s5_bb_re'], w['s5_bb_im'], w['s5_lam_re'], w['s5_lam_im'],
                                     w['s5_cc_re'], w['s5_cc_im'], tag + "s5_scan")
    s5, y_pre = _s5_out_fwd(proj, y_acc, w['s5_d'], w['s5_w_glu'], w['s5_b_glu'], tag + "s5_out")
    mixed_t, r1, h1, h1_t, attn_o, attn_lse = _mix_fwd(outs, lses, lru, s5, w['mix_norm_g'], h_in, w['w_out'],
                                                       w['ln1_g'], w['ln1_b'], tag + "mix_out_ln1")
    w['w_up_g'], w['w_down'], ffn_dep = get_ffn(l, h1)
    up, y_conv, act, act_t = _ffn_up_act(h1, w['w_up_g'], w['ffn_conv_w'], w['ffn_conv_b'], tag + "up_act",
                                         dep=ffn_dep)
    r2, out_a, out_b = _proj_ln(act, w['w_down'], h1, w['ln2_g'], w['ln2_b'], tag + "down_ln2", target=target)
    saved = dict(h_in_t=h_in_t, proj=proj, qkv=qkv, lru=lru, lru_saved=lru_saved, x_re=x_re, x_im=x_im,
                 y_pre=y_pre, s5=s5, mixed_t=mixed_t, attn_o=attn_o, attn_lse=attn_lse, r1=r1, h1_t=h1_t, up=up,
                 act_t=act_t, r2=r2, y_conv=y_conv)
    return out_a, out_b, saved


def _layer_bwd_ffn(dh2, sv, w, l, dep=None):
    tag = "l%d_" % l
    g = {}
    dr2, g['ln2_g'], g['ln2_b'] = _ln_bwd(sv['r2'], dh2, w['ln2_g'], tag + "ln2_bwd", dep=dep)
    g['w_down'] = _mm_dw(sv['act_t'], dr2, 1024, D_MODEL, 1024, tag + "down_dw")
    dup, dh1, dcw_parts, dcb_parts = _ffn_bwd(sv['up'], sv['y_conv'], dr2, w['w_down'], w['w_up_g'],
                                              w['ffn_conv_w'], tag + "ffn_bwd")
    g['ffn_conv_w'] = dcw_parts.sum(axis=0)
    g['ffn_conv_b'] = dcb_parts.sum(axis=0)
    g['w_up_g'] = _mm_up_dw(sv['h1_t'], dup, tag + "up_dw")
    return dh1, g


def _layer_bwd_mix(dh1, sv, w, cos, sin, l, dep, g_ffn, after_out_grad, after_small_grads, after_in_grad):
    tag = "l%d_" % l
    g = {}
    dr1, g['ln1_g'], g['ln1_b'], d_o, dlru, ds5, g['mix_norm_g'] = _mix_bwd(
        sv['r1'], dh1, w['ln1_g'], w['w_out'], sv['attn_o'][0], sv['lru'], sv['s5'], w['mix_norm_g'],
        tag + "ln1_mix_bwd", dep=dep)
    g['w_out'] = _mm_dw(sv['mixed_t'], dr1, 1024, D_MODEL, 1024, tag + "out_dw")
    dy, dud, g['s5_d'], g['s5_w_glu'], g['s5_b_glu'] = _s5_out_bwd(
        sv['proj'], sv['y_pre'], ds5, w['s5_d'], w['s5_w_glu'], w['s5_b_glu'], tag + "s5_out_bwd",
        dep=after_out_grad(l, g['w_out']))
    du, g['s5_lam_re'], g['s5_lam_im'], g['s5_bb_re'], g['s5_bb_im'], g['s5_cc_re'], g['s5_cc_im'] = \
        _s5_scan_bwd(sv['proj'], dy, dud, sv['x_re'], sv['x_im'], w['s5_bb_re'], w['s5_bb_im'],
                     w['s5_lam_re'], w['s5_lam_im'], w['s5_cc_re'], w['s5_cc_im'], tag + "s5_scan_bwd")
    (dxr, dgate, g['lru_conv_w'], g['lru_conv_b'], g['lru_wr'], g['lru_br'], g['lru_wi'], g['lru_bi'],
     g['lru_lambda']) = _lru_bwd(sv['proj'], dlru, *sv['lru_saved'], w['lru_conv_w'], w['lru_conv_b'], w['lru_wr'],
                                 w['lru_br'], w['lru_wi'], w['lru_bi'], w['lru_lambda'], tag + "lru_bwd")
    token = after_small_grads(l, _finish_layer_grads({**g_ffn, **g}, w, l))
    dqkv = [_attn_bwd(sv['qkv'][b], sv['attn_o'][b], d_o[b], sv['attn_lse'][b], d, tag + "attn_bwd_d%d" % d,
                      dep=token if b == 0 else None)
            for b, d in enumerate(DILATIONS)]
    dproj = _dproj_assemble(dqkv, dxr, dgate, du, cos, sin, tag + "dproj")
    g_in = _mm_dw(sv['h_in_t'], dproj, 1024, D_IN, 1024, tag + "in_dw")
    return _mm_nt(dproj, w['w_in'], 512, D_MODEL, tag + "in_dx", add=dr1, add_scale=ALPHA,
                  dep=after_in_grad(l, g_in))


def _s5_rep(a):
    return jnp.repeat(a, S5_C, axis=0)


def _prepare_layer(p, l):
    w = {}
    for n in ('w_in', 'w_out', 's5_w_glu'):
        w[n] = p[n].astype(BF16)
    w['ffn_conv_w'] = _ffn_interleave(p['ffn_conv_w'])
    w['ffn_conv_b'] = _ffn_interleave(p['ffn_conv_b'])[None, :]
    w['lru_conv_w'] = p['lru_conv_w']
    for n in ('lru_conv_b', 'lru_br', 'lru_bi', 'lru_lambda', 's5_b_glu', 'mix_norm_g',
              'ln1_g', 'ln1_b', 'ln2_g', 'ln2_b'):
        w[n] = p[n][None, :]
    w['lru_wr'] = _block_diag(p['lru_wr']).astype(BF16)
    w['lru_wi'] = _block_diag(p['lru_wi']).astype(BF16)
    w['s5_d'] = p['s5_d'].reshape(1, S5_W)
    disc_in = (_s5_rep(p['s5_a_re']), _s5_rep(p['s5_a_im']),
               _s5_rep(jnp.broadcast_to(p['s5_log_step'][:, None], (S5_G, S5_P))),
               jnp.swapaxes(p['s5_b_re'], 1, 2).reshape(S5_W, S5_P),
               jnp.swapaxes(p['s5_b_im'], 1, 2).reshape(S5_W, S5_P))
    ab_re, ab_im, bb_re, bb_im = _s5_disc_fwd(*disc_in, "l%d_s5_disc" % l)
    w['s5_disc_in'] = disc_in
    w['s5_lam_re'] = ab_re.reshape(S5_G, S5_C, S5_P)[:, 0, :].reshape(1, S5_STATES)
    w['s5_lam_im'] = ab_im.reshape(S5_G, S5_C, S5_P)[:, 0, :].reshape(1, S5_STATES)
    w['s5_bb_re'] = _block_diag(bb_re.reshape(S5_G, S5_C, S5_P)).astype(BF16)
    w['s5_bb_im'] = _block_diag(bb_im.reshape(S5_G, S5_C, S5_P)).astype(BF16)
    w['s5_cc_re'] = _block_diag(jnp.swapaxes(p['s5_c_re'], 1, 2)).astype(BF16)
    w['s5_cc_im'] = _block_diag(jnp.swapaxes(p['s5_c_im'], 1, 2)).astype(BF16)
    return w


def _finish_layer_grads(g, w, l):
    out = {}
    for n in ('s5_w_glu', 'lru_conv_w'):
        out[n] = g[n]
    out['ffn_conv_w'] = _ffn_deinterleave(g['ffn_conv_w'])
    out['ffn_conv_b'] = _ffn_deinterleave(g['ffn_conv_b'])[0]
    for n in ('lru_conv_b', 'lru_br', 'lru_bi', 'lru_lambda', 's5_b_glu', 'mix_norm_g',
              'ln1_g', 'ln1_b', 'ln2_g', 'ln2_b'):
        out[n] = g[n][0]
    out['lru_wr'] = _block_diag_extract(g['lru_wr'], LRU_W // HEAD)
    out['lru_wi'] = _block_diag_extract(g['lru_wi'], LRU_W // HEAD)
    out['s5_d'] = g['s5_d'].reshape(S5_G, S5_C)
    out['s5_c_re'] = jnp.swapaxes(_block_diag_take(g['s5_cc_re'], S5_G), 1, 2)
    out['s5_c_im'] = jnp.swapaxes(_block_diag_take(g['s5_cc_im'], S5_G), 1, 2)
    rep = lambda v: _s5_rep(v.reshape(S5_G, S5_P)) * (1.0 / S5_C)
    cts = (rep(g['s5_lam_re']), rep(g['s5_lam_im']),
           _block_diag_take(g['s5_bb_re'], S5_G).reshape(S5_W, S5_P),
           _block_diag_take(g['s5_bb_im'], S5_G).reshape(S5_W, S5_P))
    da_re, da_im, dls, dbt_re, dbt_im = _s5_disc_bwd(*w['s5_disc_in'], cts, "l%d_s5_disc_bwd" % l)
    out['s5_a_re'] = da_re.reshape(S5_G, S5_C, S5_P).sum(axis=1)
    out['s5_a_im'] = da_im.reshape(S5_G, S5_C, S5_P).sum(axis=1)
    out['s5_log_step'] = dls.reshape(S5_G, S5_C * S5_P).sum(axis=1)
    out['s5_b_re'] = jnp.swapaxes(dbt_re.reshape(S5_G, S5_C, S5_P), 1, 2)
    out['s5_b_im'] = jnp.swapaxes(dbt_im.reshape(S5_G, S5_C, S5_P), 1, 2)
    return out


def _run_step(x, target, get_layer, get_ffn, after_ffn_grads, after_out_grad, after_small_grads, after_in_grad):
    cos, sin = _rope_tables(x.shape[0])
    h, h_t = x, _transpose_bf16(x, "x_transpose")
    ws, saved = [], []
    for l in range(DEPTH):
        p, dep = get_layer(l, h)
        ws.append(_prepare_layer(p, l))
        h, h_t, sv = _layer_fwd(h, h_t, ws[l], cos, sin, l, dep, get_ffn, target if l == DEPTH - 1 else None)
        saved.append(sv)
    dh, loss_vec = h, h_t
    dep = None
    for l in reversed(range(DEPTH)):
        dh1, g = _layer_bwd_ffn(dh, saved[l], ws[l], l, dep)
        dep = after_ffn_grads(l, g)
        dh = _layer_bwd_mix(dh1, saved[l], ws[l], cos, sin, l, dep, g, after_out_grad, after_small_grads,
                            after_in_grad)
        dep = None
    return loss_vec[0, 0], dh


def _local_step(x, target, layers):
    grads = [{} for _ in range(DEPTH)]

    def ffn(l, h1):
        return layers[l]['w_up_g'].astype(BF16), layers[l]['w_down'].astype(BF16), None

    def keep_ffn(l, g):
        grads[l].update(w_up_g=g['w_up_g'], w_down=g['w_down'])

    def keep_small(l, g):
        grads[l].update(g)

    loss, dx = _run_step(x, target, lambda l, h: (layers[l], None), ffn, keep_ffn,
                         lambda l, g: grads[l].update(w_out=g), keep_small, lambda l, g: grads[l].update(w_in=g))
    return loss, dx, grads


def kernel(x, w_in, lru_conv_w, lru_conv_b, lru_wr, lru_br, lru_wi, lru_bi, lru_lambda, s5_a_re, s5_a_im, s5_b_re, s5_b_im, s5_c_re, s5_c_im, s5_d, s5_log_step, s5_w_glu, s5_b_glu, mix_norm_g, w_out, ln1_g, ln1_b, w_up, ffn_conv_w, ffn_conv_b, w_down, ln2_g, ln2_b, loss_target, m_w_in, m_lru_conv_w, m_lru_conv_b, m_lru_wr, m_lru_br, m_lru_wi, m_lru_bi, m_lru_lambda, m_s5_a_re, m_s5_a_im, m_s5_b_re, m_s5_b_im, m_s5_c_re, m_s5_c_im, m_s5_d, m_s5_log_step, m_s5_w_glu, m_s5_b_glu, m_mix_norm_g, m_w_out, m_ln1_g, m_ln1_b, m_w_up, m_ffn_conv_w, m_ffn_conv_b, m_w_down, m_ln2_g, m_ln2_b, v_w_in, v_lru_conv_w, v_lru_conv_b, v_lru_wr, v_lru_br, v_lru_wi, v_lru_bi, v_lru_lambda, v_s5_a_re, v_s5_a_im, v_s5_b_re, v_s5_b_im, v_s5_c_re, v_s5_c_im, v_s5_d, v_s5_log_step, v_s5_w_glu, v_s5_b_glu, v_mix_norm_g, v_w_out, v_ln1_g, v_ln1_b, v_w_up, v_ffn_conv_w, v_ffn_conv_b, v_w_down, v_ln2_g, v_ln2_b):
    args = locals()
    wl = {n: args[n] for n in WEIGHTS}
    ml = {n: args['m_' + n] for n in WEIGHTS}
    vl = {n: args['v_' + n] for n in WEIGHTS}

    small_sizes = [int(wl[n].size) for n in SMALL_SHARDED]
    small_flat = _pad_to(jnp.concatenate([wl[n].reshape(-1) for n in SMALL_SHARDED]), 8 * 1024)
    small_all, = _all_gather([small_flat.reshape(-1, 1024)], "gather_small")
    small_all = small_all.reshape(N_DEV, -1)
    small_full, off = {}, 0
    for n, sz in zip(SMALL_SHARDED, small_sizes):
        small_full[n] = _gather_full(small_all[:, off:off + sz].reshape((N_DEV,) + wl[n].shape), SHARD_AXIS[n])
        off += sz
    def mixer_params(l, gathered):
        g_in, g_out = gathered
        p = {n: wl[n][l] for n in REPLICATED}
        p.update({n: small_full[n][l] for n in SMALL_SHARDED})
        p['w_in'] = _gather_full(g_in, 1)
        p['w_out'] = g_out.reshape(D_MODEL, D_MODEL)
        return p

    mix_names, ffn_names = ('w_in', 'w_out'), ('w_up', 'w_down')
    shards = lambda names, l: [wl[n][l].astype(BF16) for n in names]
    mix0 = _all_gather(shards(mix_names, 0), "gather_mix_l0")
    gathers = {}
    gathers[0, 'ffn'], ffn0_token = _exchange_start(shards(ffn_names, 0), True, "gather_ffn_l0_start", dep=mix0[0])
    def get_layer(l, h):
        if l == 0:
            return mixer_params(0, mix0), ffn0_token
        return mixer_params(1, _exchange_wait(gathers[1, 'mix'], True, h, "gather_mix_l1_wait")), None

    def get_ffn(l, h1):
        g_up, g_down = _exchange_wait(gathers[l, 'ffn'], True, h1, "gather_ffn_l%d_wait" % l)
        token = None
        if l == 0:
            gathers[1, 'mix'], token = _exchange_start(shards(mix_names, 1), True, "gather_mix_l1_start", dep=g_up)
            gathers[1, 'ffn'], token = _exchange_start(shards(ffn_names, 1), True, "gather_ffn_l1_start", dep=token)
        return g_up, g_down.reshape(D_FF, D_MODEL), token

    scatters = {}

    def after_ffn_grads(l, g):
        send = [g['w_up_g'], g['w_down'].reshape(N_DEV, D_FF // N_DEV, D_MODEL)]
        scatters[l, 'ffn'], token = _exchange_start(send, False, "scatter_ffn_l%d_start" % l)
        return token

    def after_out_grad(l, g_out):
        send = [g_out.reshape(N_DEV, D_MODEL // N_DEV, D_MODEL)]
        scatters[l, 'out'], token = _exchange_start(send, False, "scatter_out_l%d_start" % l)
        return token

    def after_in_grad(l, g_in):
        send = _scatter_blocks(g_in, 1)
        if l == 0:
            send = send.astype(BF16)
        scatters[l, 'in'], token = _exchange_start([send], False, "scatter_in_l%d_start" % l)
        return token

    def after_small_grads(l, g):
        rep = [g[n][None] for n in REPLICATED]
        shd = [_scatter_blocks(g[n], SHARD_AXIS[n] - 1)[:, None] for n in SMALL_SHARDED]
        scatters[l, 'rep'], token = _exchange_start(rep, True, "gather_rep_grads_l%d_start" % l)
        scatters[l, 'small'], token = _exchange_start(shd, False, "scatter_small_l%d_start" % l, dep=token)
        return token

    loss_local, grad_x = _run_step(x[0], loss_target[0], get_layer, get_ffn, after_ffn_grads, after_out_grad,
                                   after_small_grads, after_in_grad)
    loss = lax.psum(loss_local, AXES)

    results = {}
    big_prev = {n: None for n in BIG}

    def finish_big(l, part, names, after):
        landed = _exchange_wait(scatters[l, part], False, after, "scatter_%s_l%d_wait" % (part, l))
        for n, ld in zip(names, landed):
            big_prev[n] = _adamw_sum(ld, wl[n], ml[n], vl[n], l, big_prev[n], "adamw_%s_l%d" % (n, l))

    for l, part, names in ((1, 'ffn', ffn_names), (1, 'out', ('w_out',)), (1, 'in', ('w_in',)),
                           (0, 'ffn', ffn_names), (0, 'out', ('w_out',))):
        finish_big(l, part, names, grad_x)

    kinds = ('grad', 'delta', 'm', 'v')
    landed = [dict(zip(REPLICATED + SMALL_SHARDED,
                       list(_exchange_wait(scatters[l, 'rep'], True, grad_x, "gather_rep_grads_l%d_wait" % l)) +
                       list(_exchange_wait(scatters[l, 'small'], False, grad_x, "scatter_small_l%d_wait" % l))))
              for l in range(DEPTH)]
    matrices = ['lru_wr', 'lru_wi', 's5_a_re', 's5_a_im', 's5_c_re', 's5_c_im', 's5_d']
    widest = ['s5_b_re', 's5_b_im']
    vectors = [n for n in REPLICATED + SMALL_SHARDED if n not in matrices + widest]
    last = None
    for tag, names in (("vectors", vectors), ("matrices", matrices), ("s5_b", widest)):
        res = _adamw_many([[landed[l][n] for n in names] for l in range(DEPTH)], [wl[n] for n in names],
                          [ml[n] for n in names], [vl[n] for n in names], "adamw_" + tag)
        for kind, arrs in zip(kinds, res):
            for n, a in zip(names, arrs):
                results[kind, n] = a
        last = res[0][0]
    finish_big(0, 'in', ('w_in',), last)
    for n in BIG:
        results['grad', n], results['delta', n], results['m', n], results['v', n] = big_prev[n]

    out = [loss, grad_x[None]]
    for kind in kinds:
        out.extend(results[kind, n] for n in WEIGHTS)
    return tuple(out)
```

```python
import math

import jax
import jax.numpy as jnp
from jax import lax
from jax.experimental import pallas as pl
from jax.experimental.pallas import tpu as pltpu

F32 = jnp.float32
BF16 = jnp.bfloat16

N_DEV = 8
DEPTH = 2
D_MODEL = 1024
ATTN_W = 384
LRU_W = 384
S5_W = 256
D_IN = 2176
D_FF = 3072
HEAD = 64
ATTN_BLK = 128
ATTN_TILE = 1024
DILATIONS = (1, 4, 16)
S5_G = 16
S5_P = 64
S5_C = 16
S5_STATES = S5_G * S5_P
LRU_C = 8.0
LRU_CONV = 4
FFN_CONV = 3
ROPE_THETA = 10000.0
ALPHA = (2 * DEPTH) ** 0.25
LN_EPS = 1e-5
RMS_EPS = 1e-6
ADAM_LR, ADAM_B1, ADAM_B2, ADAM_EPS, ADAM_WD, ADAM_STEP = 0.001, 0.9, 0.999, 1e-8, 0.01, 10

LANE = 128
SCAN_T = 256
S5_BLK = 256
FFN_CB = 2 * D_FF // N_DEV
VMEM_LIMIT = 56 * 1024 * 1024

AXES = ("x", "y", "c")

WEIGHTS = ['w_in', 'lru_conv_w', 'lru_conv_b', 'lru_wr', 'lru_br', 'lru_wi', 'lru_bi', 'lru_lambda',
           's5_a_re', 's5_a_im', 's5_b_re', 's5_b_im', 's5_c_re', 's5_c_im', 's5_d', 's5_log_step',
           's5_w_glu', 's5_b_glu', 'mix_norm_g', 'w_out', 'ln1_g', 'ln1_b', 'w_up', 'ffn_conv_w',
           'ffn_conv_b', 'w_down', 'ln2_g', 'ln2_b']
SHARD_AXIS = {'w_in': 2, 'lru_conv_w': 2, 's5_w_glu': 1, 'w_out': 1, 'w_up': 2, 'ffn_conv_w': 2, 'w_down': 1}
BIG = ['w_in', 'w_out', 'w_up', 'w_down']
SMALL_SHARDED = ['lru_conv_w', 'ffn_conv_w', 's5_w_glu']
REPLICATED = [n for n in WEIGHTS if n not in SHARD_AXIS]


def _cparams(sem=None):
    return pltpu.CompilerParams(dimension_semantics=sem, vmem_limit_bytes=VMEM_LIMIT)


def _ffn_dev(jb):
    return jb // 2 + (N_DEV // 2) * (jb % 2)


def _gelu(x):
    c = math.sqrt(2.0 / math.pi)
    t = jnp.tanh(c * (x + 0.044715 * (x * x * x)))
    return 0.5 * x * (1.0 + t)


def _gelu_grad(x):
    c = math.sqrt(2.0 / math.pi)
    x2 = x * x
    t = jnp.tanh(c * (x + 0.044715 * (x2 * x)))
    return 0.5 * (1.0 + t) + 0.5 * x * (1.0 - t * t) * (c * (1.0 + 3.0 * 0.044715 * x2))


def _sigmoid(x):
    return 1.0 / (1.0 + jnp.exp(-x))


def _log1p(x):
    u = 1.0 + x
    d = u - 1.0
    return jnp.where(d == 0.0, x, jnp.log(u) * (x / jnp.where(d == 0.0, 1.0, d)))


def _softplus(x):
    return jnp.maximum(x, 0.0) + _log1p(jnp.exp(-jnp.abs(x)))


def _expm1(x):
    return jnp.tanh(0.5 * x) * (jnp.exp(x) + 1.0)


def _dot(a, b):
    return jnp.dot(a.astype(BF16), b.astype(BF16), preferred_element_type=F32)


def _dot_nt(a, b):
    return lax.dot_general(a.astype(BF16), b.astype(BF16), (((1,), (1,)), ((), ())),
                           preferred_element_type=F32)


def _dot_tn(a, b):
    return lax.dot_general(a.astype(BF16), b.astype(BF16), (((0,), (0,)), ((), ())),
                           preferred_element_type=F32)


def _rows(shape):
    return lax.broadcasted_iota(jnp.int32, shape, 0)


def _shift_down_prev(x, s, prev8):
    if s == 0:
        return x
    t, l = x.shape
    r = pltpu.roll(x, s, axis=0)
    pr = pltpu.roll(prev8, s, axis=0)
    pad = jnp.concatenate([pr, jnp.zeros((t - 8, l), x.dtype)], axis=0)
    return jnp.where(_rows(x.shape) < s, pad, r)


def _shift_up_next(x, s, next8):
    if s == 0:
        return x
    t, l = x.shape
    r = pltpu.roll(x, t - s, axis=0)
    nx = pltpu.roll(next8, 8 - s, axis=0)
    pad = jnp.concatenate([jnp.zeros((t - 8, l), x.dtype), nx], axis=0)
    return jnp.where(_rows(x.shape) >= t - s, pad, r)


SUB = 8


def _tile_shift(x, s, fill, reverse):
    t = x.shape[0]
    pos = _rows(x.shape) & (SUB - 1)
    if reverse:
        return jnp.where(pos < SUB - s, pltpu.roll(x, t - s, axis=0), fill)
    return jnp.where(pos >= s, pltpu.roll(x, s, axis=0), fill)


def _scan_chunk(a, x, carry, reverse=False):
    s = 1
    while s < SUB:
        x = x + a * _tile_shift(x, s, 0.0, reverse)
        a = a * _tile_shift(a, s, 1.0, reverse)
        s *= 2
    nv = x.shape[0] // SUB
    out = [None] * nv
    for v in (reversed(range(nv)) if reverse else range(nv)):
        rows = slice(v * SUB, (v + 1) * SUB)
        out[v] = x[rows, :] + a[rows, :] * carry
        carry = out[v][0:1, :] if reverse else out[v][SUB - 1:SUB, :]
    return jnp.concatenate(out, axis=0)


def _cmul(ar, ai, br, bi):
    return ar * br - ai * bi, ar * bi + ai * br


def _cscan_consts(lr, li, reverse):
    pows = [(lr, li)]
    for _ in range(2):
        pows.append(_cmul(*pows[-1], *pows[-1]))
    rows = [(lr, li)]
    for _ in range(SUB - 1):
        rows.append(_cmul(*rows[-1], lr, li))
    if reverse:
        rows = rows[::-1]
    return pows, (jnp.concatenate([r for r, _ in rows], axis=0), jnp.concatenate([i for _, i in rows], axis=0))


def _cscan_chunk(xr, xi, consts, carry, reverse=False):
    pows, (p8r, p8i) = consts
    s = 1
    for pr, pi in pows:
        sr = _tile_shift(xr, s, 0.0, reverse)
        si = _tile_shift(xi, s, 0.0, reverse)
        xr, xi = xr + pr * sr - pi * si, xi + pr * si + pi * sr
        s *= 2
    nv = xr.shape[0] // SUB
    out_r, out_i = [None] * nv, [None] * nv
    cr, ci = carry
    for v in (reversed(range(nv)) if reverse else range(nv)):
        rows = slice(v * SUB, (v + 1) * SUB)
        out_r[v] = xr[rows, :] + p8r * cr - p8i * ci
        out_i[v] = xi[rows, :] + p8r * ci + p8i * cr
        edge = slice(0, 1) if reverse else slice(SUB - 1, SUB)
        cr, ci = out_r[v][edge, :], out_i[v][edge, :]
    return jnp.concatenate(out_r, axis=0), jnp.concatenate(out_i, axis=0)


def _dep_args(dep):
    return ([], []) if dep is None else ([pl.BlockSpec(memory_space=pl.ANY)], [dep])


def _mm_nt(a, w, tm, tn, name, add=None, add_scale=1.0, dep=None):
    m, k = a.shape
    n = w.shape[0]

    def body(a_ref, w_ref, *rest):
        o_ref = rest[-1]
        if add is None:
            o_ref[...] = _dot_nt(a_ref[...], w_ref[...])
        else:
            o_ref[...] = _dot_nt(a_ref[...], w_ref[...]) + add_scale * rest[0][...]

    in_specs = [pl.BlockSpec((tm, k), lambda j, i: (i, 0)), pl.BlockSpec((tn, k), lambda j, i: (j, 0))]
    args = [a, w]
    if add is not None:
        in_specs.append(pl.BlockSpec((tm, tn), lambda j, i: (i, j)))
        args.append(add)
    dep_specs, dep_ops = _dep_args(dep)
    return pl.pallas_call(
        body, out_shape=jax.ShapeDtypeStruct((m, n), F32), grid=(n // tn, m // tm),
        in_specs=in_specs + dep_specs, out_specs=pl.BlockSpec((tm, tn), lambda j, i: (i, j)), name=name,
        compiler_params=_cparams(("parallel", "parallel")))(*args, *dep_ops)


def _mm_dw(at, b, tm, tn, ts, name):
    m, s = at.shape
    n = b.shape[1]

    def body(a_ref, b_ref, o_ref):
        @pl.when(pl.program_id(2) == 0)
        def _():
            o_ref[...] = jnp.zeros_like(o_ref)
        o_ref[...] += _dot(a_ref[...], b_ref[...])

    return pl.pallas_call(
        body, out_shape=jax.ShapeDtypeStruct((m, n), F32), grid=(m // tm, n // tn, s // ts),
        in_specs=[pl.BlockSpec((tm, ts), lambda i, j, k: (i, k)), pl.BlockSpec((ts, tn), lambda i, j, k: (k, j))],
        out_specs=pl.BlockSpec((tm, tn), lambda i, j, k: (i, j)), name=name,
        compiler_params=_cparams(("parallel", "parallel", "arbitrary")))(at, b)


def _transpose_bf16(x, name):
    s, d = x.shape
    tm = 512

    def body(x_ref, o_ref):
        o_ref[...] = x_ref[...].T.astype(BF16)

    return pl.pallas_call(
        body, out_shape=jax.ShapeDtypeStruct((d, s), BF16), grid=(s // tm,),
        in_specs=[pl.BlockSpec((tm, d), lambda i: (i, 0))], out_specs=pl.BlockSpec((d, tm), lambda i: (0, i)),
        name=name, compiler_params=_cparams(("parallel",)))(x)


def _mm_up_dw(ht, dup, name):
    d, s = ht.shape

    def body(a_ref, b_ref, o_ref):
        o_ref[...] = _dot(a_ref[...], b_ref[...])

    return pl.pallas_call(
        body, out_shape=jax.ShapeDtypeStruct((N_DEV, d, FFN_CB), F32), grid=(N_DEV,),
        in_specs=[pl.BlockSpec((d, s), lambda j: (0, 0)), pl.BlockSpec((s, FFN_CB), lambda j: (0, j))],
        out_specs=pl.BlockSpec((None, d, FFN_CB), lambda j: (_ffn_dev(j), 0, 0)), name=name,
        compiler_params=_cparams(("parallel",)))(ht, dup)


def _layer_norm(r, g, b):
    mu = jnp.mean(r, axis=-1, keepdims=True)
    xc = r - mu
    var = jnp.mean(xc * xc, axis=-1, keepdims=True)
    return xc * lax.rsqrt(var + LN_EPS) * g + b


def _proj_ln(a, w, resid, g, bias, name, transposed=True, target=None):
    s, k = a.shape
    d = w.shape[1]
    tm = 512

    def body(a_ref, w_ref, x_ref, g_ref, bias_ref, *rest):
        r = ALPHA * x_ref[...] + _dot(a_ref[...], w_ref[...])
        h = _layer_norm(r, g_ref[...], bias_ref[...])
        if target is None:
            r_ref, h_ref = rest[0], rest[1]
            h_ref[...] = h
            if transposed:
                rest[2][...] = h.T.astype(BF16)
        else:
            t_ref, r_ref, dy_ref, l_ref = rest

            @pl.when(pl.program_id(0) == 0)
            def _():
                l_ref[...] = jnp.zeros_like(l_ref)
            e = h - t_ref[...]
            dy_ref[...] = e * (1.0 / d)
            part = 0.5 * jnp.sum(jnp.mean(e * e, axis=-1, keepdims=True), axis=0, keepdims=True)
            l_ref[...] += jnp.broadcast_to(part, l_ref.shape)
        r_ref[...] = r

    row = pl.BlockSpec((tm, d), lambda i: (i, 0))
    vec = pl.BlockSpec((1, d), lambda i: (0, 0))
    in_specs = [pl.BlockSpec((tm, k), lambda i: (i, 0)), pl.BlockSpec((k, d), lambda i: (0, 0)), row, vec, vec]
    args = [a, w, resid, g, bias]
    shapes = [jax.ShapeDtypeStruct((s, d), F32), jax.ShapeDtypeStruct((s, d), F32)]
    specs = [row, row]
    if target is not None:
        in_specs.append(row)
        args.append(target)
        shapes.append(jax.ShapeDtypeStruct((1, LANE), F32))
        specs.append(pl.BlockSpec((1, LANE), lambda i: (0, 0)))
    elif transposed:
        shapes.append(jax.ShapeDtypeStruct((d, s), BF16))
        specs.append(pl.BlockSpec((d, tm), lambda i: (0, i)))
    return pl.pallas_call(
        body, out_shape=tuple(shapes), grid=(s // tm,), in_specs=in_specs, out_specs=tuple(specs), name=name,
        compiler_params=_cparams(("arbitrary",) if target is not None else ("parallel",)))(*args)


def _layer_norm_bwd(r, dh, g):
    mu = jnp.mean(r, axis=-1, keepdims=True)
    xc = r - mu
    var = jnp.mean(xc * xc, axis=-1, keepdims=True)
    rstd = lax.rsqrt(var + LN_EPS)
    xh = xc * rstd
    dxh = dh * g
    m1 = jnp.mean(dxh, axis=-1, keepdims=True)
    m2 = jnp.mean(dxh * xh, axis=-1, keepdims=True)
    return (rstd * (dxh - m1 - xh * m2), jnp.sum(dh * xh, axis=0, keepdims=True),
            jnp.sum(dh, axis=0, keepdims=True))


def _ln_bwd(r, dh, g, name, dep=None):
    s, d = r.shape
    tm = 512

    def body(r_ref, dh_ref, g_ref, *rest):
        dr_ref, dg_ref, db_ref = rest[-3:]

        @pl.when(pl.program_id(0) == 0)
        def _():
            dg_ref[...] = jnp.zeros_like(dg_ref)
            db_ref[...] = jnp.zeros_like(db_ref)
        dr_ref[...], dg_rows, db_rows = _layer_norm_bwd(r_ref[...], dh_ref[...], g_ref[...])
        dg_ref[...] += dg_rows
        db_ref[...] += db_rows

    row = pl.BlockSpec((tm, d), lambda i: (i, 0))
    vec = pl.BlockSpec((1, d), lambda i: (0, 0))
    dep_specs, dep_ops = _dep_args(dep)
    return pl.pallas_call(
        body, out_shape=(jax.ShapeDtypeStruct((s, d), F32), jax.ShapeDtypeStruct((1, d), F32),
                         jax.ShapeDtypeStruct((1, d), F32)),
        grid=(s // tm,), in_specs=[row, row, vec] + dep_specs, out_specs=(row, vec, vec), name=name,
        compiler_params=_cparams(("arbitrary",)))(r, dh, g, *dep_ops)


def _rope_tables(s):
    half = HEAD // 2
    pos = jnp.arange(s, dtype=F32)
    inv = ROPE_THETA ** (-jnp.arange(half, dtype=F32) * 2.0 / HEAD)
    ang = pos[:, None] * inv[None, :]
    cos, sin = jnp.cos(ang), jnp.sin(ang)
    cos = jnp.concatenate([cos, cos, cos, cos], axis=1)
    sin = jnp.concatenate([-sin, sin, -sin, sin], axis=1)
    return cos, sin


def _rotate(x, cos, sin):
    lane = lax.broadcasted_iota(jnp.int32, x.shape, 1)
    partner = jnp.where((lane % HEAD) < HEAD // 2, pltpu.roll(x, LANE - HEAD // 2, axis=1),
                        pltpu.roll(x, HEAD // 2, axis=1))
    return x * cos + partner * sin


def _class_rows(c, d, tm):
    return pl.ds(c, tm // d, stride=d) if d > 1 else pl.ds(0, tm)


def _dilated_spec(tm, d, w):
    return pl.BlockSpec((tm // d, d * w), lambda i: (i, 0))


def _token_scratch(tm, w):
    return pltpu.VMEM((w // LANE, tm, LANE), F32)


def _to_tokens(src_ref, dst3, d, tm):
    nj = dst3.shape[0]
    for cls in range(d):
        for j in range(nj):
            col = (cls * nj + j) * LANE
            dst3.at[j][_class_rows(cls, d, tm), :] = src_ref[:, col:col + LANE]


def _to_dilated(src3, dst_ref, d, tm):
    nj = src3.shape[0]
    for cls in range(d):
        for j in range(nj):
            col = (cls * nj + j) * LANE
            dst_ref[:, col:col + LANE] = src3.at[j][_class_rows(cls, d, tm), :].astype(dst_ref.dtype)


def _token_value(src3):
    return jnp.concatenate([src3[j] for j in range(src3.shape[0])], axis=1)


def _proj_rope(h, w_in, cos, sin, name, dep=None):
    s, d_model = h.shape
    tm = 512
    w = 3 * ATTN_W
    nj = w // LANE

    def body(h_ref, w_ref, c_ref, s_ref, *rest):
        p_ref, o_refs, rot = rest[-5], rest[-4:-1], rest[-1]
        y = _dot(h_ref[...], w_ref[...])
        p_ref[...] = y
        c, sn = c_ref[...], s_ref[...]
        for j in range(nj):
            x = y[:, j * LANE:(j + 1) * LANE]
            rot[j] = _rotate(x, c, sn) if j < 2 * ATTN_W // LANE else x
        for d, o_ref in zip(DILATIONS, o_refs):
            _to_dilated(rot, o_ref, d, tm)

    tab = pl.BlockSpec((tm, LANE), lambda i: (i, 0))
    dep_specs, dep_ops = _dep_args(dep)
    res = pl.pallas_call(
        body, out_shape=(jax.ShapeDtypeStruct((s, D_IN), F32),
                         *[jax.ShapeDtypeStruct((s // d, d * w), BF16) for d in DILATIONS]),
        grid=(s // tm,),
        in_specs=[pl.BlockSpec((tm, d_model), lambda i: (i, 0)), pl.BlockSpec((d_model, D_IN), lambda i: (0, 0)),
                  tab, tab] + dep_specs,
        out_specs=(pl.BlockSpec((tm, D_IN), lambda i: (i, 0)), *[_dilated_spec(tm, d, w) for d in DILATIONS]),
        scratch_shapes=[_token_scratch(tm, w)], name=name,
        compiler_params=_cparams(("parallel",)))(h, w_in, cos, sin, *dep_ops)
    return res[0], res[1:]


def _dproj_assemble(dqkv_list, dxr, dgate, du, cos, sin, name):
    s = dxr.shape[0]
    tm = 512
    nq = 3 * ATTN_W // LANE

    def body(*refs):
        br = refs[:9]
        dxr_ref, dg_ref, du_ref, c_ref, s_ref, o_ref = refs[9:15]
        tok = refs[15:]
        c, sn = c_ref[...], -s_ref[...]
        for part in range(3):
            for b, d in enumerate(DILATIONS[1:], start=1):
                _to_tokens(br[3 * b + part], tok[2 * part + b - 1], d, tm)
        for j in range(nq):
            part, jj = divmod(j, ATTN_W // LANE)
            x = br[part][:, jj * LANE:(jj + 1) * LANE] + tok[2 * part][jj] + tok[2 * part + 1][jj]
            if part < 2:
                x = _rotate(x, c, sn)
            o_ref[:, j * LANE:(j + 1) * LANE] = x.astype(BF16)
        o_ref[:, 3 * ATTN_W:3 * ATTN_W + LRU_W] = dxr_ref[...].astype(BF16)
        o_ref[:, 3 * ATTN_W + LRU_W:3 * ATTN_W + 2 * LRU_W] = dg_ref[...].astype(BF16)
        o_ref[:, 3 * ATTN_W + 2 * LRU_W:] = du_ref[...].astype(BF16)

    a_spec = pl.BlockSpec((tm, ATTN_W), lambda i: (i, 0))
    tab = pl.BlockSpec((tm, LANE), lambda i: (i, 0))
    ordered = [dqkv_list[b][p] for b in range(3) for p in range(3)]
    d_specs = [_dilated_spec(tm, d, ATTN_W) for d in DILATIONS for _ in range(3)]
    return pl.pallas_call(
        body, out_shape=jax.ShapeDtypeStruct((s, D_IN), BF16), grid=(s // tm,),
        in_specs=d_specs + [a_spec, a_spec, pl.BlockSpec((tm, S5_W), lambda i: (i, 0)), tab, tab],
        out_specs=pl.BlockSpec((tm, D_IN), lambda i: (i, 0)),
        scratch_shapes=[_token_scratch(tm, ATTN_W)] * 6, name=name,
        compiler_params=_cparams(("parallel",)))(*ordered, dxr, dgate, du, cos, sin)


def _attn_tiles(s, d):
    m = s // d
    tq = min(m, ATTN_TILE)
    return m, tq, tq // ATTN_BLK


def _band_mask(qb):
    qi = lax.broadcasted_iota(jnp.int32, (ATTN_BLK, 2 * ATTN_BLK), 0)
    ki = lax.broadcasted_iota(jnp.int32, (ATTN_BLK, 2 * ATTN_BLK), 1)
    dist = qi + ATTN_BLK - ki
    return (dist >= 0) & (dist <= ATTN_BLK) & ((ki >= ATTN_BLK) | (qb > 0))


def _head_cols(h):
    return (slice(h * HEAD, (h + 1) * HEAD), slice(ATTN_W + h * HEAD, ATTN_W + (h + 1) * HEAD),
            slice(2 * ATTN_W + h * HEAD, 2 * ATTN_W + (h + 1) * HEAD))


def _attn_fwd(qv, d, name):
    m = qv.shape[0]
    w3 = 3 * ATTN_W
    _, tq, n = _attn_tiles(m * d, d)
    scale = HEAD ** -0.5

    def body(x_ref, p_ref, o_ref, l_ref):
        b = pl.program_id(1)

        def block(i, first):
            r0 = 0 if first else pl.multiple_of(i * ATTN_BLK, ATTN_BLK)
            rows = pl.ds(r0, ATTN_BLK)
            valid = _band_mask(b * n + i)
            if not first:
                krows = pl.ds(pl.multiple_of(i * ATTN_BLK - ATTN_BLK, ATTN_BLK), 2 * ATTN_BLK)
            for h in range(ATTN_W // HEAD):
                qs, ks, vs = _head_cols(h)
                q = x_ref[rows, qs]
                if first:
                    k = jnp.concatenate([p_ref[:, ks], x_ref[0:ATTN_BLK, ks]], axis=0)
                    v = jnp.concatenate([p_ref[:, vs], x_ref[0:ATTN_BLK, vs]], axis=0)
                else:
                    k = x_ref[krows, ks]
                    v = x_ref[krows, vs]
                sc = jnp.where(valid, _dot_nt(q, k) * scale, -1e30)
                mx = jnp.max(sc, axis=-1, keepdims=True)
                p = jnp.exp(sc - mx)
                l = jnp.sum(p, axis=-1, keepdims=True)
                o_ref[rows, qs] = _dot(p, v) / l
                l_ref[rows, qs] = jnp.broadcast_to(mx + jnp.log(l), (ATTN_BLK, HEAD))

        block(0, True)
        if n > 1:
            def loop(i, carry):
                block(i, False)
                return carry
            lax.fori_loop(1, n, loop, 0)

    shp = jax.ShapeDtypeStruct((m, d * ATTN_W), F32)
    ospec = pl.BlockSpec((tq, ATTN_W), lambda c, b: (b, c))
    out, lse = pl.pallas_call(
        body, out_shape=(shp, shp), grid=(d, m // tq),
        in_specs=[pl.BlockSpec((tq, w3), lambda c, b: (b, c)),
                  pl.BlockSpec((ATTN_BLK, w3), lambda c, b: (jnp.maximum(b * n - 1, 0), c))],
        out_specs=(ospec, ospec), name=name,
        compiler_params=_cparams(("parallel", "parallel")))(qv, qv)
    return out, lse


def _attn_bwd(qv, ov, dov, lv, d, name, dep=None):
    m = qv.shape[0]
    w3 = 3 * ATTN_W
    _, tq, n = _attn_tiles(m * d, d)
    nb = m // ATTN_BLK
    scale = HEAD ** -0.5

    def body(x_ref, p_ref, nx_ref, o_ref, do_ref, l_ref, on_ref, don_ref, ln_ref, *rest):
        dq_ref, dk_ref, dv_ref = rest[-3:]
        b = pl.program_id(1)
        dk_ref[...] = jnp.zeros_like(dk_ref)
        dv_ref[...] = jnp.zeros_like(dv_ref)

        def grads(q, k, v, o, do, lse, valid):
            sc = jnp.where(valid, _dot_nt(q, k) * scale, -1e30)
            p = jnp.exp(sc - lse)
            delta = jnp.sum(do * o, axis=-1, keepdims=True)
            return p, p * (_dot_nt(do, v) - delta) * scale

        def block(i, first):
            r0 = 0 if first else pl.multiple_of(i * ATTN_BLK, ATTN_BLK)
            rows = pl.ds(r0, ATTN_BLK)
            valid = _band_mask(b * n + i)
            if not first:
                krows = pl.ds(pl.multiple_of(i * ATTN_BLK - ATTN_BLK, ATTN_BLK), 2 * ATTN_BLK)
            for h in range(ATTN_W // HEAD):
                qs, ks, vs = _head_cols(h)
                q = x_ref[rows, qs]
                do = do_ref[rows, qs]
                if first:
                    k = jnp.concatenate([p_ref[:, ks], x_ref[0:ATTN_BLK, ks]], axis=0)
                    v = jnp.concatenate([p_ref[:, vs], x_ref[0:ATTN_BLK, vs]], axis=0)
                else:
                    k = x_ref[krows, ks]
                    v = x_ref[krows, vs]
                p, ds = grads(q, k, v, o_ref[rows, qs], do, l_ref[rows, qs][:, 0:1], valid)
                dq_ref[rows, qs] = _dot(ds, k)
                if first:
                    dk_ref[0:ATTN_BLK, qs] += _dot_tn(ds[:, ATTN_BLK:], q)
                    dv_ref[0:ATTN_BLK, qs] += _dot_tn(p[:, ATTN_BLK:], do)
                else:
                    dk_ref[krows, qs] += _dot_tn(ds, q)
                    dv_ref[krows, qs] += _dot_tn(p, do)

        block(0, True)
        if n > 1:
            def loop(i, carry):
                block(i, False)
                return carry
            lax.fori_loop(1, n, loop, 0)

        last = slice((n - 1) * ATTN_BLK, n * ATTN_BLK)
        qi = lax.broadcasted_iota(jnp.int32, (ATTN_BLK, ATTN_BLK), 0)
        ki = lax.broadcasted_iota(jnp.int32, (ATTN_BLK, ATTN_BLK), 1)
        valid_next = (qi <= ki) & ((b + 1) * n < nb)
        for h in range(ATTN_W // HEAD):
            qs, ks, vs = _head_cols(h)
            q = nx_ref[:, qs]
            do = don_ref[:, qs]
            p, ds = grads(q, x_ref[last, ks], x_ref[last, vs], on_ref[:, qs], do, ln_ref[:, qs][:, 0:1],
                          valid_next)
            dk_ref[last, qs] += _dot_tn(ds, q)
            dv_ref[last, qs] += _dot_tn(p, do)

    nxt = lambda b: jnp.minimum((b + 1) * n, nb - 1)
    xs = pl.BlockSpec((tq, w3), lambda c, b: (b, c))
    xp = pl.BlockSpec((ATTN_BLK, w3), lambda c, b: (jnp.maximum(b * n - 1, 0), c))
    xn = pl.BlockSpec((ATTN_BLK, w3), lambda c, b: (nxt(b), c))
    a = pl.BlockSpec((tq, ATTN_W), lambda c, b: (b, c))
    an = pl.BlockSpec((ATTN_BLK, ATTN_W), lambda c, b: (nxt(b), c))
    shp = jax.ShapeDtypeStruct((m, d * ATTN_W), F32)
    dep_specs, dep_ops = _dep_args(dep)
    return pl.pallas_call(
        body, out_shape=(shp, shp, shp), grid=(d, m // tq),
        in_specs=[xs, xp, xn, a, a, a, an, an, an] + dep_specs, out_specs=(a, a, a), name=name,
        compiler_params=_cparams(("parallel", "parallel")))(qv, qv, qv, ov, dov, lv, ov, dov, lv, *dep_ops)


def _rms(x, g):
    ms = jnp.mean(x * x, axis=-1, keepdims=True)
    return x * lax.rsqrt(ms + RMS_EPS) * g


def _rms_bwd(x, g, dy):
    ms = jnp.mean(x * x, axis=-1, keepdims=True)
    r = lax.rsqrt(ms + RMS_EPS)
    dyg = dy * g
    dx = r * dyg - x * (r * r * r) * jnp.mean(x * dyg, axis=-1, keepdims=True)
    return dx, dy * x * r


def _mix_fwd(outs, lses, lru, s5, g, h_in, w_out, ln_g, ln_b, name):
    s = lru.shape[0]
    tm = 256

    def body(o1, o2, o3, l1, l2, l3, lru_ref, s5_ref, g_ref, x_ref, w_ref, lg_ref, lb_ref,
             mixed_t_ref, r_ref, h_ref, ht_ref, ov1, ov2, ov3, lv1, lv2, lv3, so2, so3, sl2, sl3):
        for d, src, dst in ((DILATIONS[1], o2, so2), (DILATIONS[2], o3, so3),
                            (DILATIONS[1], l2, sl2), (DILATIONS[2], l3, sl3)):
            _to_tokens(src, dst, d, tm)
        a1, a2, a3 = l1[...], _token_value(sl2), _token_value(sl3)
        mx = jnp.maximum(jnp.maximum(a1, a2), a3)
        e1, e2, e3 = jnp.exp(a1 - mx), jnp.exp(a2 - mx), jnp.exp(a3 - mx)
        den = e1 + e2 + e3
        o = (e1 * o1[...] + e2 * _token_value(so2) + e3 * _token_value(so3)) / den
        lse = mx + jnp.log(den)
        ov1[...] = o
        lv1[...] = lse
        for j in range(ATTN_W // LANE):
            so2[j] = o[:, j * LANE:(j + 1) * LANE]
            sl2[j] = lse[:, j * LANE:(j + 1) * LANE]
        for d, o_dst, l_dst in ((DILATIONS[1], ov2, lv2), (DILATIONS[2], ov3, lv3)):
            _to_dilated(so2, o_dst, d, tm)
            _to_dilated(sl2, l_dst, d, tm)
        gg = g_ref[...]
        mixed = jnp.concatenate([_rms(o, gg[:, :ATTN_W]),
                                 _rms(lru_ref[...], gg[:, ATTN_W:ATTN_W + LRU_W]),
                                 _rms(s5_ref[...], gg[:, ATTN_W + LRU_W:])], axis=1)
        mixed_t_ref[...] = mixed.T.astype(BF16)
        r = ALPHA * x_ref[...] + _dot(mixed, w_ref[...])
        h = _layer_norm(r, lg_ref[...], lb_ref[...])
        r_ref[...] = r
        h_ref[...] = h
        ht_ref[...] = h.T.astype(BF16)

    a = pl.BlockSpec((tm, ATTN_W), lambda i: (i, 0))
    s5s = pl.BlockSpec((tm, S5_W), lambda i: (i, 0))
    full = pl.BlockSpec((tm, D_MODEL), lambda i: (i, 0))
    vec = pl.BlockSpec((1, D_MODEL), lambda i: (0, 0))
    dil = [_dilated_spec(tm, d, ATTN_W) for d in DILATIONS]
    dshape = [jax.ShapeDtypeStruct((s // d, d * ATTN_W), F32) for d in DILATIONS]
    tshape = jax.ShapeDtypeStruct((D_MODEL, s), BF16)
    fshape = jax.ShapeDtypeStruct((s, D_MODEL), F32)
    tspec = pl.BlockSpec((D_MODEL, tm), lambda i: (0, i))
    res = pl.pallas_call(
        body, out_shape=(tshape, fshape, fshape, tshape, *dshape, *dshape),
        grid=(s // tm,),
        in_specs=dil + dil + [a, s5s, vec, full, pl.BlockSpec((D_MODEL, D_MODEL), lambda i: (0, 0)), vec, vec],
        out_specs=(tspec, full, full, tspec, *dil, *dil),
        scratch_shapes=[_token_scratch(tm, ATTN_W)] * 4, name=name,
        compiler_params=_cparams(("parallel",)))(*outs, *lses, lru, s5, g, h_in, w_out, ln_g, ln_b)
    return res[0], res[1], res[2], res[3], res[4:7], res[7:10]


def _mix_bwd(r, dh, ln_g, w_out, o, lru, s5, g, name, dep=None):
    s = lru.shape[0]
    tm = 256

    def body(r_ref, dh_ref, lg_ref, w_ref, o_ref, lru_ref, s5_ref, g_ref, *rest):
        dr_ref, dlg_ref, dlb_ref, do_ref, do2_ref, do3_ref, dlru_ref, ds5_ref, dg_ref, stage = rest[-10:]

        @pl.when(pl.program_id(0) == 0)
        def _():
            dg_ref[...] = jnp.zeros_like(dg_ref)
            dlg_ref[...] = jnp.zeros_like(dlg_ref)
            dlb_ref[...] = jnp.zeros_like(dlb_ref)
        gg = g_ref[...]
        dr, dlg_rows, dlb_rows = _layer_norm_bwd(r_ref[...], dh_ref[...], lg_ref[...])
        dr_ref[...] = dr
        dlg_ref[...] += dlg_rows
        dlb_ref[...] += dlb_rows
        dm = _dot_nt(dr, w_ref[...])
        dx, dgr = _rms_bwd(o_ref[...], gg[:, :ATTN_W], dm[:, :ATTN_W])
        do_ref[...] = dx
        for j in range(ATTN_W // LANE):
            stage[j] = dx[:, j * LANE:(j + 1) * LANE]
        _to_dilated(stage, do2_ref, DILATIONS[1], tm)
        _to_dilated(stage, do3_ref, DILATIONS[2], tm)
        dg_ref[:, :ATTN_W] += jnp.sum(dgr, axis=0, keepdims=True)
        dx, dgr = _rms_bwd(lru_ref[...], gg[:, ATTN_W:ATTN_W + LRU_W], dm[:, ATTN_W:ATTN_W + LRU_W])
        dlru_ref[...] = dx
        dg_ref[:, ATTN_W:ATTN_W + LRU_W] += jnp.sum(dgr, axis=0, keepdims=True)
        dx, dgr = _rms_bwd(s5_ref[...], gg[:, ATTN_W + LRU_W:], dm[:, ATTN_W + LRU_W:])
        ds5_ref[...] = dx
        dg_ref[:, ATTN_W + LRU_W:] += jnp.sum(dgr, axis=0, keepdims=True)

    a = pl.BlockSpec((tm, ATTN_W), lambda i: (i, 0))
    s5s = pl.BlockSpec((tm, S5_W), lambda i: (i, 0))
    full = pl.BlockSpec((tm, D_MODEL), lambda i: (i, 0))
    vec = pl.BlockSpec((1, D_MODEL), lambda i: (0, 0))
    dil = [_dilated_spec(tm, d, ATTN_W) for d in DILATIONS]
    dshape = [jax.ShapeDtypeStruct((s // d, d * ATTN_W), F32) for d in DILATIONS]
    dep_specs, dep_ops = _dep_args(dep)
    vshape = jax.ShapeDtypeStruct((1, D_MODEL), F32)
    res = pl.pallas_call(
        body, out_shape=(jax.ShapeDtypeStruct((s, D_MODEL), F32), vshape, vshape, *dshape,
                         jax.ShapeDtypeStruct((s, LRU_W), F32), jax.ShapeDtypeStruct((s, S5_W), F32), vshape),
        grid=(s // tm,),
        in_specs=[full, full, vec, pl.BlockSpec((D_MODEL, D_MODEL), lambda i: (0, 0)), a, a, s5s, vec] + dep_specs,
        out_specs=(full, vec, vec, *dil, a, s5s, vec), scratch_shapes=[_token_scratch(tm, ATTN_W)], name=name,
        compiler_params=_cparams(("arbitrary",)))(r, dh, ln_g, w_out, o, lru, s5, g, *dep_ops)
    return res[0], res[1], res[2], res[3:6], res[6], res[7], res[8]


def _lru_gate_math(xc, pre_r, pre_i, lam):
    r = _sigmoid(pre_r)
    i = _sigmoid(pre_i)
    log_a = -LRU_C * r * _softplus(-lam)
    a = jnp.exp(log_a)
    u = jnp.sqrt(-_expm1(2.0 * log_a)) * (i * xc)
    return a, u


def _lru_conv(x, prev8, cw, cb):
    y = cb + cw[LRU_CONV - 1:LRU_CONV, :] * x
    for k in range(LRU_CONV - 1):
        y = y + cw[k:k + 1, :] * _shift_down_prev(x, LRU_CONV - 1 - k, prev8)
    return y


def _lru_specs(s):
    xo = 3 * ATTN_W // LANE
    go = xo + LRU_W // LANE
    xr = pl.BlockSpec((s, LANE), lambda j: (0, xo + j))
    gt = pl.BlockSpec((s, LANE), lambda j: (0, go + j))
    cw = pl.BlockSpec((LRU_CONV, LANE), lambda j: (0, j))
    vec = pl.BlockSpec((1, LANE), lambda j: (0, j))
    wbd = pl.BlockSpec((LANE, LANE), lambda j: (j, j))
    col = pl.BlockSpec((s, LANE), lambda j: (0, j))
    return xr, gt, cw, vec, wbd, col


def _lru_fwd(proj, cw, cb, wr, br, wi, bi, lam, name):
    s = proj.shape[0]
    t = SCAN_T

    def body(xr_ref, gt_ref, cw_ref, cb_ref, wr_ref, br_ref, wi_ref, bi_ref, lam_ref, o_ref, xc_ref, a_ref, h_ref):
        cwv, cbv, lamv = cw_ref[...], cb_ref[...], lam_ref[...]
        wrv, wiv, brv, biv = wr_ref[...], wi_ref[...], br_ref[...], bi_ref[...]

        def chunk(c, carry):
            h_c, prev8 = carry
            rows = pl.ds(pl.multiple_of(c * t, t), t)
            x = xr_ref[rows, :]
            xc = _lru_conv(x, prev8, cwv, cbv)
            a, u = _lru_gate_math(xc, _dot(xc, wrv) + brv, _dot(xc, wiv) + biv, lamv)
            h = _scan_chunk(a, u, h_c)
            xc_ref[rows, :] = xc
            a_ref[rows, :] = a
            h_ref[rows, :] = h
            o_ref[rows, :] = h * _gelu(gt_ref[rows, :])
            return h[t - 1:t, :], x[t - 8:t, :]

        lax.fori_loop(0, s // t, chunk, (jnp.zeros((1, LANE), F32), jnp.zeros((8, LANE), F32)))

    xr, gt, cws, vec, wbd, col = _lru_specs(s)
    shp = jax.ShapeDtypeStruct((s, LRU_W), F32)
    return pl.pallas_call(
        body, out_shape=(shp,) * 4, grid=(LRU_W // LANE,),
        in_specs=[xr, gt, cws, vec, wbd, vec, wbd, vec, vec], out_specs=(col,) * 4, name=name,
        compiler_params=_cparams(("parallel",)))(proj, proj, cw, cb, wr, br, wi, bi, lam)


def _lru_bwd(proj, dout, xc_all, a_all, h_all, cw, cb, wr, br, wi, bi, lam, name):
    s = proj.shape[0]
    t = SCAN_T
    nc = s // t

    def body(xr_ref, gt_ref, do_ref, xc_s, a_s, h_s, cw_ref, cb_ref, wr_ref, br_ref, wi_ref, bi_ref, lam_ref,
             dxr_ref, dgt_ref, dcw_ref, dcb_ref, dwr_ref, dbr_ref, dwi_ref, dbi_ref, dlam_ref):
        cwv, cbv, lamv = cw_ref[...], cb_ref[...], lam_ref[...]
        wrv, wiv, brv, biv = wr_ref[...], wi_ref[...], br_ref[...], bi_ref[...]
        z1 = jnp.zeros((1, LANE), F32)
        zw = jnp.zeros((LANE, LANE), F32)

        def bchunk(ci, carry):
            g_next, a_next, dxc_next8, dcw, dcb, dwr, dbr, dwi, dbi, dlam = carry
            c = nc - 1 - ci
            t0 = pl.multiple_of(c * t, t)
            rows = pl.ds(t0, t)
            before = pl.ds(pl.multiple_of(jnp.maximum(t0 - 8, 0), 8), 8)
            has_prev = (c > 0).astype(F32)
            x, gt, do = xr_ref[rows, :], gt_ref[rows, :], do_ref[rows, :]
            xc, a, h = xc_s[rows, :], a_s[rows, :], h_s[rows, :]
            prev8_x = xr_ref[before, :] * has_prev
            prev8_h = h_s[before, :] * has_prev
            dgt_ref[rows, :] = do * h * _gelu_grad(gt)
            dh = do * _gelu(gt)
            a_plus = _shift_up_next(a, 1, jnp.broadcast_to(a_next, (8, LANE)))
            g = _scan_chunk(a_plus, dh, g_next, reverse=True)
            da = g * _shift_down_prev(h, 1, prev8_h)
            pre_r = _dot(xc, wrv) + brv
            pre_i = _dot(xc, wiv) + biv
            _, vjp = jax.vjp(_lru_gate_math, xc, pre_r, pre_i, lamv)
            dxc, dpre_r, dpre_i, dlam_c = vjp((da, g))
            dxc = dxc + _dot_nt(dpre_r, wrv) + _dot_nt(dpre_i, wiv)
            dx = cwv[LRU_CONV - 1:LRU_CONV, :] * dxc
            dcw_rows = [None] * LRU_CONV
            dcw_rows[LRU_CONV - 1] = jnp.sum(dxc * x, axis=0, keepdims=True)
            for k in range(LRU_CONV - 1):
                sh = LRU_CONV - 1 - k
                dx = dx + cwv[k:k + 1, :] * _shift_up_next(dxc, sh, dxc_next8)
                dcw_rows[k] = jnp.sum(dxc * _shift_down_prev(x, sh, prev8_x), axis=0, keepdims=True)
            dxr_ref[rows, :] = dx
            return (g[0:1, :], a[0:1, :], dxc[0:8, :],
                    dcw + jnp.concatenate(dcw_rows, axis=0),
                    dcb + jnp.sum(dxc, axis=0, keepdims=True),
                    dwr + _dot_tn(xc, dpre_r), dbr + jnp.sum(dpre_r, axis=0, keepdims=True),
                    dwi + _dot_tn(xc, dpre_i), dbi + jnp.sum(dpre_i, axis=0, keepdims=True),
                    dlam + dlam_c)

        init = (z1, z1, jnp.zeros((8, LANE), F32), jnp.zeros((LRU_CONV, LANE), F32), z1, zw, z1, zw, z1, z1)
        res = lax.fori_loop(0, nc, bchunk, init)
        dcw_ref[...] = res[3]
        dcb_ref[...] = res[4]
        dwr_ref[...] = res[5]
        dbr_ref[...] = res[6]
        dwi_ref[...] = res[7]
        dbi_ref[...] = res[8]
        dlam_ref[...] = res[9]

    xr, gt, cws, vec, wbd, col = _lru_specs(s)
    vshape = jax.ShapeDtypeStruct((1, LRU_W), F32)
    wshape = jax.ShapeDtypeStruct((LRU_W, LRU_W), F32)
    return pl.pallas_call(
        body,
        out_shape=(jax.ShapeDtypeStruct((s, LRU_W), F32), jax.ShapeDtypeStruct((s, LRU_W), F32),
                   jax.ShapeDtypeStruct((LRU_CONV, LRU_W), F32), vshape, wshape, vshape, wshape, vshape, vshape),
        grid=(LRU_W // LANE,),
        in_specs=[xr, gt, col, col, col, col, cws, vec, wbd, vec, wbd, vec, vec],
        out_specs=(col, col, cws, vec, wbd, vec, wbd, vec, vec), name=name,
        compiler_params=_cparams(("parallel",)))(proj, proj, dout, xc_all, a_all, h_all, cw, cb, wr, br, wi, bi,
                                                 lam)


def _s5_disc_math(a_re, a_im, log_step, bt_re, bt_im):
    step = jnp.exp(log_step)
    dt_re, dt_im = step * a_re, step * a_im
    mag = jnp.exp(dt_re)
    ab_re, ab_im = mag * jnp.cos(dt_im), mag * jnp.sin(dt_im)
    z_re, z_im = ab_re - 1.0, ab_im
    den = a_re * a_re + a_im * a_im
    f_re = (z_re * a_re + z_im * a_im) / den
    f_im = (z_im * a_re - z_re * a_im) / den
    bb_re = f_re * bt_re - f_im * bt_im
    bb_im = f_re * bt_im + f_im * bt_re
    return ab_re, ab_im, bb_re, bb_im


def _s5_disc_fwd(a_re, a_im, log_step, bt_re, bt_im, name):
    def body(ar, ai, ls, br, bi, o1, o2, o3, o4):
        r = _s5_disc_math(ar[...], ai[...], ls[...], br[...], bi[...])
        o1[...], o2[...], o3[...], o4[...] = r

    shp = jax.ShapeDtypeStruct(a_re.shape, F32)
    return pl.pallas_call(body, out_shape=(shp,) * 4, name=name)(a_re, a_im, log_step, bt_re, bt_im)


def _s5_disc_bwd(a_re, a_im, log_step, bt_re, bt_im, cts, name):
    def body(ar, ai, ls, br, bi, c1, c2, c3, c4, o1, o2, o3, o4, o5):
        _, vjp = jax.vjp(_s5_disc_math, ar[...], ai[...], ls[...], br[...], bi[...])
        r = vjp((c1[...], c2[...], c3[...], c4[...]))
        o1[...], o2[...], o3[...], o4[...], o5[...] = r

    shp = jax.ShapeDtypeStruct(a_re.shape, F32)
    return pl.pallas_call(body, out_shape=(shp,) * 5, name=name)(a_re, a_im, log_step, bt_re, bt_im, *cts)


def _s5_u_specs(s):
    uo = (3 * ATTN_W + 2 * LRU_W) // LANE
    return (pl.BlockSpec((s, LANE), lambda j: (0, uo)), pl.BlockSpec((s, LANE), lambda j: (0, uo + 1)))


def _s5_scan_fwd(proj, b_re, b_im, lam_re, lam_im, c_re, c_im, name):
    s = proj.shape[0]
    t = SCAN_T

    def body(u0_ref, u1_ref, bre_ref, bim_ref, lre_ref, lim_ref, cre_ref, cim_ref, xre_ref, xim_ref, y_ref):
        @pl.when(pl.program_id(0) == 0)
        def _():
            y_ref[...] = jnp.zeros_like(y_ref)
        lr, li = lre_ref[...], lim_ref[...]
        consts = _cscan_consts(lr, li, False)
        bre, bim, cre, cim = bre_ref[...], bim_ref[...], cre_ref[...], cim_ref[...]

        def chunk(c, carry):
            cr, ci = carry
            rows = pl.ds(pl.multiple_of(c * t, t), t)
            u = jnp.concatenate([u0_ref[rows, :], u1_ref[rows, :]], axis=1).astype(BF16)
            xr, xi = _cscan_chunk(_dot(u, bre), _dot(u, bim), consts, (cr, ci))
            xre_ref[rows, :] = xr
            xim_ref[rows, :] = xi
            y_ref[rows, :] += _dot(xr, cre) - _dot(xi, cim)
            return xr[t - 1:t, :], xi[t - 1:t, :]

        z = jnp.zeros((1, S5_BLK), F32)
        lax.fori_loop(0, s // t, chunk, (z, z))

    u0, u1 = _s5_u_specs(s)
    bsp = pl.BlockSpec((S5_W, S5_BLK), lambda j: (0, j))
    csp = pl.BlockSpec((S5_BLK, S5_W), lambda j: (j, 0))
    vec = pl.BlockSpec((1, S5_BLK), lambda j: (0, j))
    xsp = pl.BlockSpec((s, S5_BLK), lambda j: (0, j))
    ysp = pl.BlockSpec((s, S5_W), lambda j: (0, 0))
    xshape = jax.ShapeDtypeStruct((s, S5_STATES), F32)
    return pl.pallas_call(
        body, out_shape=(xshape, xshape, jax.ShapeDtypeStruct((s, S5_W), F32)),
        grid=(S5_STATES // S5_BLK,), in_specs=[u0, u1, bsp, bsp, vec, vec, csp, csp],
        out_specs=(xsp, xsp, ysp), name=name,
        compiler_params=_cparams(("arbitrary",)))(proj, proj, b_re, b_im, lam_re, lam_im, c_re, c_im)


def _s5_scan_bwd(proj, dy, du_init, x_re, x_im, b_re, b_im, lam_re, lam_im, c_re, c_im, name):
    s = proj.shape[0]
    t = SCAN_T
    nc = s // t

    def body(u0_ref, u1_ref, dy_ref, dui_ref, xre_ref, xim_ref, bre_ref, bim_ref, lre_ref, lim_ref,
             cre_ref, cim_ref, du_ref, dlr_ref, dli_ref, dbr_ref, dbi_ref, dcr_ref, dci_ref):
        @pl.when(pl.program_id(0) == 0)
        def _():
            du_ref[...] = dui_ref[...]
        mr, mi = lre_ref[...], -lim_ref[...]
        consts = _cscan_consts(mr, mi, True)
        bre, bim, cre, cim = bre_ref[...], bim_ref[...], cre_ref[...], cim_ref[...]
        dbr_ref[...] = jnp.zeros_like(dbr_ref)
        dbi_ref[...] = jnp.zeros_like(dbi_ref)
        dcr_ref[...] = jnp.zeros_like(dcr_ref)
        dci_ref[...] = jnp.zeros_like(dci_ref)

        def chunk(ci_, carry):
            gnr, gni, dlr, dli = carry
            c = nc - 1 - ci_
            t0 = pl.multiple_of(c * t, t)
            rows = pl.ds(t0, t)
            before = pl.ds(pl.multiple_of(jnp.maximum(t0 - 8, 0), 8), 8)
            has_prev = (c > 0).astype(F32)
            dyc = dy_ref[rows, :].astype(BF16)
            u = jnp.concatenate([u0_ref[rows, :], u1_ref[rows, :]], axis=1).astype(BF16)
            gr, gi = _cscan_chunk(_dot_nt(dyc, cre), -_dot_nt(dyc, cim), consts, (gnr, gni), reverse=True)
            xr, xi = xre_ref[rows, :], xim_ref[rows, :]
            xpr = _shift_down_prev(xr, 1, xre_ref[before, :] * has_prev)
            xpi = _shift_down_prev(xi, 1, xim_ref[before, :] * has_prev)
            dlr = dlr + jnp.sum(gr * xpr + gi * xpi, axis=0, keepdims=True)
            dli = dli + jnp.sum(gi * xpr - gr * xpi, axis=0, keepdims=True)
            du_ref[rows, :] += _dot_nt(gr, bre) + _dot_nt(gi, bim)
            dbr_ref[...] += _dot_tn(u, gr)
            dbi_ref[...] += _dot_tn(u, gi)
            dcr_ref[...] += _dot_tn(xr, dyc)
            dci_ref[...] -= _dot_tn(xi, dyc)
            return gr[0:1, :], gi[0:1, :], dlr, dli

        z = jnp.zeros((1, S5_BLK), F32)
        res = lax.fori_loop(0, nc, chunk, (z, z, z, z))
        dlr_ref[...] = res[2]
        dli_ref[...] = res[3]

    u0, u1 = _s5_u_specs(s)
    bsp = pl.BlockSpec((S5_W, S5_BLK), lambda j: (0, j))
    csp = pl.BlockSpec((S5_BLK, S5_W), lambda j: (j, 0))
    vec = pl.BlockSpec((1, S5_BLK), lambda j: (0, j))
    xsp = pl.BlockSpec((s, S5_BLK), lambda j: (0, j))
    ysp = pl.BlockSpec((s, S5_W), lambda j: (0, 0))
    return pl.pallas_call(
        body,
        out_shape=(jax.ShapeDtypeStruct((s, S5_W), F32),
                   jax.ShapeDtypeStruct((1, S5_STATES), F32), jax.ShapeDtypeStruct((1, S5_STATES), F32),
                   jax.ShapeDtypeStruct((S5_W, S5_STATES), F32), jax.ShapeDtypeStruct((S5_W, S5_STATES), F32),
                   jax.ShapeDtypeStruct((S5_STATES, S5_W), F32), jax.ShapeDtypeStruct((S5_STATES, S5_W), F32)),
        grid=(S5_STATES // S5_BLK,),
        in_specs=[u0, u1, ysp, ysp, xsp, xsp, bsp, bsp, vec, vec, csp, csp],
        out_specs=(ysp, vec, vec, bsp, bsp, csp, csp), name=name,
        compiler_params=_cparams(("arbitrary",)))(
            proj, proj, dy, du_init, x_re, x_im, b_re, b_im, lam_re, lam_im, c_re, c_im)


def _s5_out_fwd(proj, y_acc, dvec, w_glu, b_glu, name):
    s = proj.shape[0]
    tm = 512
    uo = (3 * ATTN_W + 2 * LRU_W) // LANE

    def body(u0_ref, u1_ref, y_ref, d_ref, w_ref, b_ref, o_ref, yp_ref):
        u = jnp.concatenate([u0_ref[...], u1_ref[...]], axis=1)
        y = y_ref[...] + d_ref[...] * u
        yp_ref[...] = y
        yg = _gelu(y)
        o_ref[...] = yg * _sigmoid(_dot(yg, w_ref[...]) + b_ref[...])

    u0 = pl.BlockSpec((tm, LANE), lambda i: (i, uo))
    u1 = pl.BlockSpec((tm, LANE), lambda i: (i, uo + 1))
    row = pl.BlockSpec((tm, S5_W), lambda i: (i, 0))
    vec = pl.BlockSpec((1, S5_W), lambda i: (0, 0))
    wsp = pl.BlockSpec((S5_W, S5_W), lambda i: (0, 0))
    shp = jax.ShapeDtypeStruct((s, S5_W), F32)
    return pl.pallas_call(
        body, out_shape=(shp, shp), grid=(s // tm,), in_specs=[u0, u1, row, vec, wsp, vec],
        out_specs=(row, row), name=name,
        compiler_params=_cparams(("parallel",)))(proj, proj, y_acc, dvec, w_glu, b_glu)


def _s5_out_bwd(proj, y_pre, dout, dvec, w_glu, b_glu, name, dep=None):
    s = proj.shape[0]
    tm = 512
    uo = (3 * ATTN_W + 2 * LRU_W) // LANE

    def body(u0_ref, u1_ref, y_ref, do_ref, d_ref, w_ref, b_ref, *rest):
        dy_ref, dud_ref, dd_ref, dw_ref, db_ref = rest[-5:]

        @pl.when(pl.program_id(0) == 0)
        def _():
            dd_ref[...] = jnp.zeros_like(dd_ref)
            dw_ref[...] = jnp.zeros_like(dw_ref)
            db_ref[...] = jnp.zeros_like(db_ref)
        u = jnp.concatenate([u0_ref[...], u1_ref[...]], axis=1)
        y = y_ref[...]
        do = do_ref[...]
        yg = _gelu(y)
        sg = _sigmoid(_dot(yg, w_ref[...]) + b_ref[...])
        dz = do * yg * sg * (1.0 - sg)
        dyg = do * sg + _dot_nt(dz, w_ref[...])
        dy = dyg * _gelu_grad(y)
        dy_ref[...] = dy
        dud_ref[...] = d_ref[...] * dy
        dd_ref[...] += jnp.sum(dy * u, axis=0, keepdims=True)
        dw_ref[...] += _dot_tn(yg, dz)
        db_ref[...] += jnp.sum(dz, axis=0, keepdims=True)

    u0 = pl.BlockSpec((tm, LANE), lambda i: (i, uo))
    u1 = pl.BlockSpec((tm, LANE), lambda i: (i, uo + 1))
    row = pl.BlockSpec((tm, S5_W), lambda i: (i, 0))
    vec = pl.BlockSpec((1, S5_W), lambda i: (0, 0))
    wsp = pl.BlockSpec((S5_W, S5_W), lambda i: (0, 0))
    shp = jax.ShapeDtypeStruct((s, S5_W), F32)
    vshape = jax.ShapeDtypeStruct((1, S5_W), F32)
    dep_specs, dep_ops = _dep_args(dep)
    return pl.pallas_call(
        body, out_shape=(shp, shp, vshape, jax.ShapeDtypeStruct((S5_W, S5_W), F32), vshape),
        grid=(s // tm,), in_specs=[u0, u1, row, row, vec, wsp, vec] + dep_specs,
        out_specs=(row, row, vec, wsp, vec), name=name,
        compiler_params=_cparams(("arbitrary",)))(proj, proj, y_pre, dout, dvec, w_glu, b_glu, *dep_ops)


def _ffn_conv(x, prev8, cw, cb):
    y = cb + cw[FFN_CONV - 1:FFN_CONV, :] * x
    for k in range(FFN_CONV - 1):
        y = y + cw[k:k + 1, :] * _shift_down_prev(x, FFN_CONV - 1 - k, prev8)
    return y


def _ffn_up_act(h, wg, cw, cb, name, dep=None):
    s, d = h.shape
    tm = 512
    tb = 2 * FFN_CB
    nt = D_FF // FFN_CB

    def body(h_ref, wgate_ref, wval_ref, cw_ref, cb_ref, *rest):
        up_ref, y_ref, o_ref, ot_ref, carry = rest[-5:]

        @pl.when(pl.program_id(1) == 0)
        def _():
            carry[...] = jnp.zeros_like(carry)
        hb = h_ref[...].astype(BF16)
        x = jnp.concatenate([_dot(hb, wgate_ref[...]), _dot(hb, wval_ref[...])], axis=1)
        up_ref[...] = x
        y = _ffn_conv(x, carry[...], cw_ref[...], cb_ref[...])
        y_ref[...] = y
        carry[...] = x[tm - 8:tm, :]
        act = _gelu(y[:, :FFN_CB]) * y[:, FFN_CB:]
        o_ref[...] = act.astype(BF16)
        ot_ref[...] = act.T.astype(BF16)

    dep_specs, dep_ops = _dep_args(dep)
    return pl.pallas_call(
        body, out_shape=(jax.ShapeDtypeStruct((s, 2 * D_FF), F32), jax.ShapeDtypeStruct((s, 2 * D_FF), F32),
                         jax.ShapeDtypeStruct((s, D_FF), BF16), jax.ShapeDtypeStruct((D_FF, s), BF16)),
        grid=(nt, s // tm),
        in_specs=[pl.BlockSpec((tm, d), lambda t, i: (i, 0)),
                  pl.BlockSpec((None, d, FFN_CB), lambda t, i: (t, 0, 0)),
                  pl.BlockSpec((None, d, FFN_CB), lambda t, i: (t + nt, 0, 0)),
                  pl.BlockSpec((FFN_CONV, tb), lambda t, i: (0, t)),
                  pl.BlockSpec((1, tb), lambda t, i: (0, t))] + dep_specs,
        out_specs=(pl.BlockSpec((tm, tb), lambda t, i: (i, t)), pl.BlockSpec((tm, tb), lambda t, i: (i, t)),
                   pl.BlockSpec((tm, FFN_CB), lambda t, i: (i, t)), pl.BlockSpec((FFN_CB, tm), lambda t, i: (t, i))),
        scratch_shapes=[pltpu.VMEM((8, tb), F32)], name=name,
        compiler_params=_cparams(("parallel", "arbitrary")))(h, wg, wg, cw, cb, *dep_ops)


def _ffn_bwd(up, y_conv, dr, w_down, wg, cw, name):
    s = up.shape[0]
    d = dr.shape[1]
    tm = 256
    tb = 2 * FFN_CB
    nr = s // tm
    nt = D_FF // FFN_CB

    def body(x_ref, p_ref, y_ref, dr_ref, wd_ref, wgate_ref, wval_ref, cw_ref,
             dup_ref, dh_ref, dcw_ref, dcb_ref, carry):
        i, t = pl.program_id(0), pl.program_id(1)

        @pl.when(i == 0)
        def _():
            carry[t] = jnp.zeros((8, tb), F32)

        @pl.when(t == 0)
        def _():
            dh_ref[...] = ALPHA * dr_ref[...]
        prev8 = p_ref[...] * (i < nr - 1).astype(F32)
        cwv = cw_ref[...]
        x = x_ref[...]
        dact = _dot_nt(dr_ref[...], wd_ref[...])
        shifted = [_shift_down_prev(x, FFN_CONV - 1 - k, prev8) for k in range(FFN_CONV - 1)]
        gate, val = y_ref[:, :FFN_CB], y_ref[:, FFN_CB:]
        dy = jnp.concatenate([dact * val * _gelu_grad(gate), dact * _gelu(gate)], axis=1)
        next8 = carry[t]
        carry[t] = dy[0:8, :]
        dx = cwv[FFN_CONV - 1:FFN_CONV, :] * dy
        dcw_rows = [None] * FFN_CONV
        dcw_rows[FFN_CONV - 1] = jnp.sum(dy * x, axis=0, keepdims=True)
        for k in range(FFN_CONV - 1):
            dx = dx + cwv[k:k + 1, :] * _shift_up_next(dy, FFN_CONV - 1 - k, next8)
            dcw_rows[k] = jnp.sum(dy * shifted[k], axis=0, keepdims=True)
        dup = dx.astype(BF16)
        dup_ref[...] = dup
        dh_ref[...] += _dot_nt(dup[:, :FFN_CB], wgate_ref[...]) + _dot_nt(dup[:, FFN_CB:], wval_ref[...])
        dcw_ref[...] = jnp.concatenate(dcw_rows, axis=0)
        dcb_ref[...] = jnp.sum(dy, axis=0, keepdims=True)

    row = lambda i: nr - 1 - i
    return pl.pallas_call(
        body, out_shape=(jax.ShapeDtypeStruct((s, 2 * D_FF), BF16), jax.ShapeDtypeStruct((s, d), F32),
                         jax.ShapeDtypeStruct((nr, FFN_CONV, 2 * D_FF), F32),
                         jax.ShapeDtypeStruct((nr, 1, 2 * D_FF), F32)),
        grid=(nr, nt),
        in_specs=[pl.BlockSpec((tm, tb), lambda i, t: (row(i), t)),
                  pl.BlockSpec((8, tb), lambda i, t: (jnp.maximum(row(i) * (tm // 8) - 1, 0), t)),
                  pl.BlockSpec((tm, tb), lambda i, t: (row(i), t)),
                  pl.BlockSpec((tm, d), lambda i, t: (row(i), 0)),
                  pl.BlockSpec((FFN_CB, d), lambda i, t: (t, 0)),
                  pl.BlockSpec((None, d, FFN_CB), lambda i, t: (t, 0, 0)),
                  pl.BlockSpec((None, d, FFN_CB), lambda i, t: (t + nt, 0, 0)),
                  pl.BlockSpec((FFN_CONV, tb), lambda i, t: (0, t))],
        out_specs=(pl.BlockSpec((tm, tb), lambda i, t: (row(i), t)),
                   pl.BlockSpec((tm, d), lambda i, t: (row(i), 0)),
                   pl.BlockSpec((None, FFN_CONV, tb), lambda i, t: (row(i), 0, t)),
                   pl.BlockSpec((None, 1, tb), lambda i, t: (row(i), 0, t))),
        scratch_shapes=[pltpu.VMEM((nt, 8, tb), F32)], name=name,
        compiler_params=_cparams(("arbitrary", "arbitrary")))(up, up, y_conv, dr, w_down, wg, wg, cw)


def _sum_partials(ld_ref):
    gg = ld_ref[0].astype(F32)
    for k in range(1, N_DEV):
        gg = gg + ld_ref[k].astype(F32)
    return gg


def _adam_update(w, g, m, v):
    mn = ADAM_B1 * m + (1.0 - ADAM_B1) * g
    vn = ADAM_B2 * v + (1.0 - ADAM_B2) * (g * g)
    m_hat = mn / (1.0 - ADAM_B1 ** ADAM_STEP)
    v_hat = vn / (1.0 - ADAM_B2 ** ADAM_STEP)
    return -ADAM_LR * (m_hat / (jnp.sqrt(v_hat) + ADAM_EPS) + ADAM_WD * w), mn, vn


def _adamw_many(landed, ws, ms, vs, name):
    n, nl = len(ws), len(landed)

    def body(*refs):
        ld = refs[:nl * n]
        w_refs, m_refs, v_refs = (refs[(nl + k) * n:(nl + k + 1) * n] for k in range(3))
        outs = refs[(nl + 3) * n:]
        for i in range(n):
            for l in range(nl):
                one = slice(l, l + 1)
                gg = _sum_partials(ld[l * n + i])
                outs[i][one] = gg
                outs[n + i][one], outs[2 * n + i][one], outs[3 * n + i][one] = _adam_update(
                    w_refs[i][one], gg, m_refs[i][one], v_refs[i][one])

    vm = pl.BlockSpec(memory_space=pltpu.VMEM)
    shapes = [jax.ShapeDtypeStruct(w.shape, F32) for w in ws] * 4
    res = pl.pallas_call(
        body, out_shape=tuple(shapes), in_specs=[vm] * ((nl + 3) * n), out_specs=tuple([vm] * (4 * n)),
        name=name, compiler_params=_cparams())(*[a for layer in landed for a in layer], *ws, *ms, *vs)
    return res[:n], res[n:2 * n], res[2 * n:3 * n], res[3 * n:]


def _adamw_sum(landed, w, m, v, layer, prev, name):
    _, r, c = landed.shape
    nl = w.shape[0]
    tm = 8
    for cand in (512, 256, 128, 64, 32, 16):
        if r % cand == 0 and N_DEV * cand * c * 4 <= 4 * 1024 * 1024:
            tm = cand
            break

    def body(*refs):
        ld_ref, w_ref, m_ref, v_ref = refs[:4]
        g_ref, d_ref, mo_ref, vo_ref = refs[-4:]
        gg = _sum_partials(ld_ref)
        g_ref[...] = gg
        d_ref[...], mo_ref[...], vo_ref[...] = _adam_update(w_ref[...], gg, m_ref[...], v_ref[...])

    blk = pl.BlockSpec((None, tm, c), lambda i: (layer, i, 0))
    in_specs = [pl.BlockSpec((N_DEV, tm, c), lambda i: (0, i, 0)), blk, blk, blk]
    args = [landed, w, m, v]
    aliases = {}
    if prev is not None:
        in_specs += [pl.BlockSpec(memory_space=pl.ANY)] * 4
        args += list(prev)
        aliases = {4 + k: k for k in range(4)}
    shp = jax.ShapeDtypeStruct((nl, r, c), F32)
    return pl.pallas_call(
        body, out_shape=(shp,) * 4, grid=(r // tm,), in_specs=in_specs, out_specs=(blk,) * 4,
        input_output_aliases=aliases, name=name, compiler_params=_cparams(("parallel",)))(*args)


def _all_gather(shards, name):
    na = len(shards)

    def body(*refs):
        x_refs, out_refs = refs[:na], refs[na:2 * na]
        send_sems, recv_sems, local_sems = refs[2 * na:]
        x, y, c = lax.axis_index("x"), lax.axis_index("y"), lax.axis_index("c")
        me, sibling = (x, y, c), (x, y, 1 - c)
        chips = [(1 - x, y), (x, 1 - y), (1 - x, 1 - y)]

        def copy(a, k, block, to, src=None):
            dst = out_refs[a].at[4 * block[0] + 2 * block[1] + block[2]]
            return pltpu.make_async_remote_copy(
                src_ref=dst if src is None else src, dst_ref=dst,
                send_sem=send_sems.at[7 * a + k], recv_sem=recv_sems.at[7 * a + k],
                device_id=to, device_id_type=pl.DeviceIdType.MESH)

        mine, first, passed = [], [], []
        for a in range(na):
            cp = pltpu.make_async_copy(x_refs[a], out_refs[a].at[4 * x + 2 * y + c], local_sems.at[a])
            cp.start()
            mine.append(cp)
            cps = [copy(a, 0, me, sibling, src=x_refs[a])]
            cps += [copy(a, 1 + j, me, (*chip, c), src=x_refs[a]) for j, chip in enumerate(chips)]
            for cp in cps:
                cp.start()
            first += cps
        for j, chip in enumerate(chips):
            for a in range(na):
                copy(a, 1 + j, (*chip, c), me).wait_recv()
                cp = copy(a, 4 + j, (*chip, c), sibling)
                cp.start()
                passed.append(cp)
        for a in range(na):
            copy(a, 0, sibling, me).wait_recv()
            for j, chip in enumerate(chips):
                copy(a, 4 + j, (*chip, 1 - c), me).wait_recv()
        for cp in first + passed:
            cp.wait_send()
        for cp in mine:
            cp.wait()

    anyspec = pl.BlockSpec(memory_space=pl.ANY)
    return pl.pallas_call(
        body, out_shape=tuple(jax.ShapeDtypeStruct((N_DEV,) + t.shape, t.dtype) for t in shards),
        in_specs=[anyspec] * na, out_specs=tuple([anyspec] * na),
        scratch_shapes=[pltpu.SemaphoreType.DMA((7 * na,)), pltpu.SemaphoreType.DMA((7 * na,)),
                        pltpu.SemaphoreType.DMA((na,))],
        name=name)(*shards)


_HBM = pl.BlockSpec(memory_space=pltpu.HBM)
_SEM = pl.BlockSpec(memory_space=pltpu.SEMAPHORE)
_EFFECT = pltpu.SideEffectType.DATAFLOW_SIDE_EFFECTING


def _exchange_copies(src_refs, land_refs, send_sems, recv_sems, local_sems, gather):
    x, y, c = lax.axis_index("x"), lax.axis_index("y"), lax.axis_index("c")
    me = 4 * x + 2 * y + c
    per_array = send_sems.shape[0] > N_DEV - 1
    local, remote = [], []
    for a, (src, land) in enumerate(zip(src_refs, land_refs)):
        local.append(pltpu.make_async_copy(src if gather else src.at[me], land.at[me],
                                           local_sems.at[a if per_array else 0]))
    for k in range(1, N_DEV):
        px = x ^ ((k >> 2) & 1)
        py = y ^ ((k >> 1) & 1)
        pc = c ^ (k & 1)
        for a, (src, land) in enumerate(zip(src_refs, land_refs)):
            remote.append(pltpu.make_async_remote_copy(
                src_ref=src if gather else src.at[4 * px + 2 * py + pc], dst_ref=land.at[me],
                send_sem=send_sems.at[(7 * a if per_array else 0) + k - 1],
                recv_sem=recv_sems.at[(7 * a if per_array else 0) + k - 1],
                device_id=(px, py, pc), device_id_type=pl.DeviceIdType.MESH))
    return local, remote


def _exchange_start(srcs, gather, name, dep=None):
    na = len(srcs)
    ns = na if na <= 4 else 1
    lands = [lax.empty(((N_DEV,) + t.shape) if gather else t.shape, t.dtype) for t in srcs]

    def body(*refs):
        src_refs, land_refs = refs[:na], refs[na:2 * na]
        nin = 2 * na + (0 if dep is None else 1)
        send_sems, recv_sems, local_sems = refs[nin:nin + 3]
        token = refs[-1]
        local, remote = _exchange_copies(src_refs, land_refs, send_sems, recv_sems, local_sems, gather)
        for cp in local + remote:
            cp.start()
        token[...] = jnp.zeros_like(token)

    dep_specs, dep_ops = _dep_args(dep)
    hbm = lambda t: pltpu.HBM(t.shape, t.dtype)
    out = pl.pallas_call(
        body, name=name,
        out_shape=(pltpu.SemaphoreType.DMA((7 * ns,)), pltpu.SemaphoreType.DMA((7 * ns,)),
                   pltpu.SemaphoreType.DMA((ns,)), *[hbm(t) for t in srcs], *[hbm(t) for t in lands],
                   jax.ShapeDtypeStruct((8, LANE), F32)),
        in_specs=[_HBM] * (2 * na) + dep_specs,
        out_specs=(_SEM, _SEM, _SEM, *[_HBM] * (2 * na), pl.BlockSpec(memory_space=pltpu.VMEM)),
        input_output_aliases={i: 3 + i for i in range(2 * na)},
        compiler_params=pltpu.CompilerParams(has_side_effects=_EFFECT),
    )(*[pltpu.with_memory_space_constraint(t, pltpu.HBM) for t in srcs + lands], *dep_ops)
    return (out[:3], out[3:3 + na], out[3 + na:3 + 2 * na]), out[-1]


def _exchange_wait(handle, gather, after, name):
    sems, srcs, lands = handle
    na = len(srcs)

    def body(*refs):
        src_refs, land_refs = refs[:na], refs[na:2 * na]
        send_sems, recv_sems, local_sems = refs[2 * na:2 * na + 3]
        local, remote = _exchange_copies(src_refs, land_refs, send_sems, recv_sems, local_sems, gather)
        for cp in remote:
            cp.wait_send()
            cp.wait_recv()
        for cp in local:
            cp.wait()

    hbm = lambda t: pltpu.HBM(t.shape, t.dtype)
    out = pl.pallas_call(
        body, name=name, out_shape=(*[hbm(t) for t in srcs], *[hbm(t) for t in lands]),
        in_specs=[_HBM] * (2 * na) + [_SEM] * 3 + [pl.BlockSpec(memory_space=pl.ANY)],
        out_specs=tuple([_HBM] * (2 * na)), input_output_aliases={i: i for i in range(2 * na)},
        compiler_params=pltpu.CompilerParams(has_side_effects=_EFFECT),
    )(*srcs, *lands, *sems, after)
    return out[na:]


def _block_diag(w):
    h, a, b = w.shape
    eye = jnp.eye(h, dtype=w.dtype)
    return (w[:, :, None, :] * eye[:, None, :, None]).reshape(h * a, h * b)


def _block_diag_extract(m, h):
    a, b = m.shape[0] // h, m.shape[1] // h
    return jnp.stack([m[i * a:(i + 1) * a, i * b:(i + 1) * b] for i in range(h)], axis=0)


def _block_diag_take(m, h):
    a, b = m.shape[0] // h, m.shape[1] // h
    eye = jnp.eye(h, dtype=m.dtype)
    return (m.reshape(h, a, h, b) * eye[:, None, :, None]).sum(axis=2)


def _ffn_interleave(w):
    lead = w.shape[:-1]
    nb = D_FF // FFN_CB
    return jnp.swapaxes(w.reshape(*lead, 2, nb, FFN_CB), -3, -2).reshape(*lead, 2 * D_FF)


def _ffn_deinterleave(w):
    lead = w.shape[:-1]
    nb = D_FF // FFN_CB
    return jnp.swapaxes(w.reshape(*lead, nb, 2, FFN_CB), -3, -2).reshape(*lead, 2 * D_FF)


def _gather_full(gathered, axis):
    shape = list(gathered.shape[1:])
    shape[axis] *= N_DEV
    return jnp.moveaxis(gathered, 0, axis).reshape(shape)


def _scatter_blocks(full, axis):
    shape = list(full.shape)
    shape[axis:axis + 1] = [N_DEV, shape[axis] // N_DEV]
    return jnp.moveaxis(full.reshape(shape), axis, 0)


def _pad_to(flat, mult):
    pad = (-flat.shape[-1]) % mult
    if pad:
        flat = jnp.concatenate([flat, jnp.zeros(flat.shape[:-1] + (pad,), flat.dtype)], axis=-1)
    return flat


def _layer_fwd(h_in, h_in_t, w, cos, sin, l, dep, get_ffn, target=None):
    tag = "l%d_" % l
    proj, qkv = _proj_rope(h_in, w['w_in'], cos, sin, tag + "proj_rope", dep=dep)
    outs, lses = [], []
    for d, qv in zip(DILATIONS, qkv):
        o, ls = _attn_fwd(qv, d, tag + "attn_d%d" % d)
        outs.append(o)
        lses.append(ls)
    lru, *lru_saved = _lru_fwd(proj, w['lru_conv_w'], w['lru_conv_b'], w['lru_wr'], w['lru_br'], w['lru_wi'],
                               w['lru_bi'], w['lru_lambda'], tag + "lru")
    x_re, x_im, y_acc = _s5_scan_fwd(proj, w['s5_bb_re'], w['s5_bb_im'], w['s5_lam_re'], w['s5_lam_im'],
                                     w['s5_cc_re'], w['s5_cc_im'], tag + "s5_scan")
    s5, y_pre = _s5_out_fwd(proj, y_acc, w['s5_d'], w['s5_w_glu'], w['s5_b_glu'], tag + "s5_out")
    w['w_up_g'], w['w_down'], w_out, ffn_dep = get_ffn(l, s5)
    if w_out is not None:
        w['w_out'] = w_out
    mixed_t, r1, h1, h1_t, attn_o, attn_lse = _mix_fwd(outs, lses, lru, s5, w['mix_norm_g'], h_in, w['w_out'],
                                                       w['ln1_g'], w['ln1_b'], tag + "mix_out_ln1")
    up, y_conv, act, act_t = _ffn_up_act(h1, w['w_up_g'], w['ffn_conv_w'], w['ffn_conv_b'], tag + "up_act",
                                         dep=ffn_dep)
    r2, out_a, out_b = _proj_ln(act, w['w_down'], h1, w['ln2_g'], w['ln2_b'], tag + "down_ln2", target=target)
    saved = dict(h_in_t=h_in_t, proj=proj, qkv=qkv, lru=lru, lru_saved=lru_saved, x_re=x_re, x_im=x_im,
                 y_pre=y_pre, s5=s5, mixed_t=mixed_t, attn_o=attn_o, attn_lse=attn_lse, r1=r1, h1_t=h1_t, up=up,
                 act_t=act_t, r2=r2, y_conv=y_conv)
    return out_a, out_b, saved


def _layer_bwd_ffn(dh2, sv, w, l, dep=None):
    tag = "l%d_" % l
    g = {}
    dr2, g['ln2_g'], g['ln2_b'] = _ln_bwd(sv['r2'], dh2, w['ln2_g'], tag + "ln2_bwd", dep=dep)
    g['w_down'] = _mm_dw(sv['act_t'], dr2, 1024, D_MODEL, 1024, tag + "down_dw")
    dup, dh1, dcw_parts, dcb_parts = _ffn_bwd(sv['up'], sv['y_conv'], dr2, w['w_down'], w['w_up_g'],
                                              w['ffn_conv_w'], tag + "ffn_bwd")
    g['ffn_conv_w'] = dcw_parts.sum(axis=0)
    g['ffn_conv_b'] = dcb_parts.sum(axis=0)
    g['w_up_g'] = _mm_up_dw(sv['h1_t'], dup, tag + "up_dw")
    return dh1, g


def _layer_bwd_mix(dh1, sv, w, cos, sin, l, dep, g_ffn, after_out_grad, after_small_grads, after_in_grad):
    tag = "l%d_" % l
    g = {}
    dr1, g['ln1_g'], g['ln1_b'], d_o, dlru, ds5, g['mix_norm_g'] = _mix_bwd(
        sv['r1'], dh1, w['ln1_g'], w['w_out'], sv['attn_o'][0], sv['lru'], sv['s5'], w['mix_norm_g'],
        tag + "ln1_mix_bwd", dep=dep)
    g['w_out'] = _mm_dw(sv['mixed_t'], dr1, 1024, D_MODEL, 1024, tag + "out_dw")
    dy, dud, g['s5_d'], g['s5_w_glu'], g['s5_b_glu'] = _s5_out_bwd(
        sv['proj'], sv['y_pre'], ds5, w['s5_d'], w['s5_w_glu'], w['s5_b_glu'], tag + "s5_out_bwd",
        dep=after_out_grad(l, g['w_out']))
    du, g['s5_lam_re'], g['s5_lam_im'], g['s5_bb_re'], g['s5_bb_im'], g['s5_cc_re'], g['s5_cc_im'] = \
        _s5_scan_bwd(sv['proj'], dy, dud, sv['x_re'], sv['x_im'], w['s5_bb_re'], w['s5_bb_im'],
                     w['s5_lam_re'], w['s5_lam_im'], w['s5_cc_re'], w['s5_cc_im'], tag + "s5_scan_bwd")
    (dxr, dgate, g['lru_conv_w'], g['lru_conv_b'], g['lru_wr'], g['lru_br'], g['lru_wi'], g['lru_bi'],
     g['lru_lambda']) = _lru_bwd(sv['proj'], dlru, *sv['lru_saved'], w['lru_conv_w'], w['lru_conv_b'], w['lru_wr'],
                                 w['lru_br'], w['lru_wi'], w['lru_bi'], w['lru_lambda'], tag + "lru_bwd")
    token = after_small_grads(l, _finish_layer_grads({**g_ffn, **g}, w, l))
    dqkv = [_attn_bwd(sv['qkv'][b], sv['attn_o'][b], d_o[b], sv['attn_lse'][b], d, tag + "attn_bwd_d%d" % d,
                      dep=token if b == 0 else None)
            for b, d in enumerate(DILATIONS)]
    dproj = _dproj_assemble(dqkv, dxr, dgate, du, cos, sin, tag + "dproj")
    g_in = _mm_dw(sv['h_in_t'], dproj, 1024, D_IN, 1024, tag + "in_dw")
    return _mm_nt(dproj, w['w_in'], 512, D_MODEL, tag + "in_dx", add=dr1, add_scale=ALPHA,
                  dep=after_in_grad(l, g_in))


def _s5_rep(a):
    return jnp.repeat(a, S5_C, axis=0)


def _prepare_layer(p, l):
    w = {}
    for n in ('w_in', 'w_out', 's5_w_glu'):
        if n in p:
            w[n] = p[n].astype(BF16)
    w['ffn_conv_w'] = _ffn_interleave(p['ffn_conv_w'])
    w['ffn_conv_b'] = _ffn_interleave(p['ffn_conv_b'])[None, :]
    w['lru_conv_w'] = p['lru_conv_w']
    for n in ('lru_conv_b', 'lru_br', 'lru_bi', 'lru_lambda', 's5_b_glu', 'mix_norm_g',
              'ln1_g', 'ln1_b', 'ln2_g', 'ln2_b'):
        w[n] = p[n][None, :]
    w['lru_wr'] = _block_diag(p['lru_wr']).astype(BF16)
    w['lru_wi'] = _block_diag(p['lru_wi']).astype(BF16)
    w['s5_d'] = p['s5_d'].reshape(1, S5_W)
    disc_in = (_s5_rep(p['s5_a_re']), _s5_rep(p['s5_a_im']),
               _s5_rep(jnp.broadcast_to(p['s5_log_step'][:, None], (S5_G, S5_P))),
               jnp.swapaxes(p['s5_b_re'], 1, 2).reshape(S5_W, S5_P),
               jnp.swapaxes(p['s5_b_im'], 1, 2).reshape(S5_W, S5_P))
    ab_re, ab_im, bb_re, bb_im = _s5_disc_fwd(*disc_in, "l%d_s5_disc" % l)
    w['s5_disc_in'] = disc_in
    w['s5_lam_re'] = ab_re.reshape(S5_G, S5_C, S5_P)[:, 0, :].reshape(1, S5_STATES)
    w['s5_lam_im'] = ab_im.reshape(S5_G, S5_C, S5_P)[:, 0, :].reshape(1, S5_STATES)
    w['s5_bb_re'] = _block_diag(bb_re.reshape(S5_G, S5_C, S5_P)).astype(BF16)
    w['s5_bb_im'] = _block_diag(bb_im.reshape(S5_G, S5_C, S5_P)).astype(BF16)
    w['s5_cc_re'] = _block_diag(jnp.swapaxes(p['s5_c_re'], 1, 2)).astype(BF16)
    w['s5_cc_im'] = _block_diag(jnp.swapaxes(p['s5_c_im'], 1, 2)).astype(BF16)
    return w


def _finish_layer_grads(g, w, l):
    out = {}
    for n in ('s5_w_glu', 'lru_conv_w'):
        out[n] = g[n]
    out['ffn_conv_w'] = _ffn_deinterleave(g['ffn_conv_w'])
    out['ffn_conv_b'] = _ffn_deinterleave(g['ffn_conv_b'])[0]
    for n in ('lru_conv_b', 'lru_br', 'lru_bi', 'lru_lambda', 's5_b_glu', 'mix_norm_g',
              'ln1_g', 'ln1_b', 'ln2_g', 'ln2_b'):
        out[n] = g[n][0]
    out['lru_wr'] = _block_diag_extract(g['lru_wr'], LRU_W // HEAD)
    out['lru_wi'] = _block_diag_extract(g['lru_wi'], LRU_W // HEAD)
    out['s5_d'] = g['s5_d'].reshape(S5_G, S5_C)
    out['s5_c_re'] = jnp.swapaxes(_block_diag_take(g['s5_cc_re'], S5_G), 1, 2)
    out['s5_c_im'] = jnp.swapaxes(_block_diag_take(g['s5_cc_im'], S5_G), 1, 2)
    rep = lambda v: _s5_rep(v.reshape(S5_G, S5_P)) * (1.0 / S5_C)
    cts = (rep(g['s5_lam_re']), rep(g['s5_lam_im']),
           _block_diag_take(g['s5_bb_re'], S5_G).reshape(S5_W, S5_P),
           _block_diag_take(g['s5_bb_im'], S5_G).reshape(S5_W, S5_P))
    da_re, da_im, dls, dbt_re, dbt_im = _s5_disc_bwd(*w['s5_disc_in'], cts, "l%d_s5_disc_bwd" % l)
    out['s5_a_re'] = da_re.reshape(S5_G, S5_C, S5_P).sum(axis=1)
    out['s5_a_im'] = da_im.reshape(S5_G, S5_C, S5_P).sum(axis=1)
    out['s5_log_step'] = dls.reshape(S5_G, S5_C * S5_P).sum(axis=1)
    out['s5_b_re'] = jnp.swapaxes(dbt_re.reshape(S5_G, S5_C, S5_P), 1, 2)
    out['s5_b_im'] = jnp.swapaxes(dbt_im.reshape(S5_G, S5_C, S5_P), 1, 2)
    return out


def _run_step(x, target, get_layer, get_ffn, after_ffn_grads, after_out_grad, after_small_grads, after_in_grad):
    cos, sin = _rope_tables(x.shape[0])
    h, h_t = x, _transpose_bf16(x, "x_transpose")
    ws, saved = [], []
    for l in range(DEPTH):
        p, dep = get_layer(l, h)
        ws.append(_prepare_layer(p, l))
        h, h_t, sv = _layer_fwd(h, h_t, ws[l], cos, sin, l, dep, get_ffn, target if l == DEPTH - 1 else None)
        saved.append(sv)
    dh, loss_vec = h, h_t
    dep = None
    for l in reversed(range(DEPTH)):
        dh1, g = _layer_bwd_ffn(dh, saved[l], ws[l], l, dep)
        dep = after_ffn_grads(l, g)
        dh = _layer_bwd_mix(dh1, saved[l], ws[l], cos, sin, l, dep, g, after_out_grad, after_small_grads,
                            after_in_grad)
        dep = None
    return loss_vec[0, 0], dh


def _local_step(x, target, layers):
    grads = [{} for _ in range(DEPTH)]

    def ffn(l, h1):
        return layers[l]['w_up_g'].astype(BF16), layers[l]['w_down'].astype(BF16), None, None

    def keep_ffn(l, g):
        grads[l].update(w_up_g=g['w_up_g'], w_down=g['w_down'])

    def keep_small(l, g):
        grads[l].update(g)

    loss, dx = _run_step(x, target, lambda l, h: (layers[l], None), ffn, keep_ffn,
                         lambda l, g: grads[l].update(w_out=g), keep_small, lambda l, g: grads[l].update(w_in=g))
    return loss, dx, grads


def kernel(x, w_in, lru_conv_w, lru_conv_b, lru_wr, lru_br, lru_wi, lru_bi, lru_lambda, s5_a_re, s5_a_im, s5_b_re, s5_b_im, s5_c_re, s5_c_im, s5_d, s5_log_step, s5_w_glu, s5_b_glu, mix_norm_g, w_out, ln1_g, ln1_b, w_up, ffn_conv_w, ffn_conv_b, w_down, ln2_g, ln2_b, loss_target, m_w_in, m_lru_conv_w, m_lru_conv_b, m_lru_wr, m_lru_br, m_lru_wi, m_lru_bi, m_lru_lambda, m_s5_a_re, m_s5_a_im, m_s5_b_re, m_s5_b_im, m_s5_c_re, m_s5_c_im, m_s5_d, m_s5_log_step, m_s5_w_glu, m_s5_b_glu, m_mix_norm_g, m_w_out, m_ln1_g, m_ln1_b, m_w_up, m_ffn_conv_w, m_ffn_conv_b, m_w_down, m_ln2_g, m_ln2_b, v_w_in, v_lru_conv_w, v_lru_conv_b, v_lru_wr, v_lru_br, v_lru_wi, v_lru_bi, v_lru_lambda, v_s5_a_re, v_s5_a_im, v_s5_b_re, v_s5_b_im, v_s5_c_re, v_s5_c_im, v_s5_d, v_s5_log_step, v_s5_w_glu, v_s5_b_glu, v_mix_norm_g, v_w_out, v_ln1_g, v_ln1_b, v_w_up, v_ffn_conv_w, v_ffn_conv_b, v_w_down, v_ln2_g, v_ln2_b):
    args = locals()
    wl = {n: args[n] for n in WEIGHTS}
    ml = {n: args['m_' + n] for n in WEIGHTS}
    vl = {n: args['v_' + n] for n in WEIGHTS}

    small_sizes = [int(wl[n].size) for n in SMALL_SHARDED]
    small_flat = _pad_to(jnp.concatenate([wl[n].reshape(-1) for n in SMALL_SHARDED]), 8 * 1024)
    small_all, w_in0 = _all_gather([small_flat.reshape(-1, 1024), wl['w_in'][0].astype(BF16)], "gather_first")
    small_all = small_all.reshape(N_DEV, -1)
    small_full, off = {}, 0
    for n, sz in zip(SMALL_SHARDED, small_sizes):
        small_full[n] = _gather_full(small_all[:, off:off + sz].reshape((N_DEV,) + wl[n].shape), SHARD_AXIS[n])
        off += sz
    def mixer_params(l, g_in, g_out):
        p = {n: wl[n][l] for n in REPLICATED}
        p.update({n: small_full[n][l] for n in SMALL_SHARDED})
        p['w_in'] = _gather_full(g_in, 1)
        if g_out is not None:
            p['w_out'] = g_out.reshape(D_MODEL, D_MODEL)
        return p

    mix_names, ffn_names = ('w_in', 'w_out'), ('w_up', 'w_down')
    shards = lambda names, l: [wl[n][l].astype(BF16) for n in names]
    gathers = {}
    gathers[0, 'ffn'], rest0_token = _exchange_start(shards(('w_out',) + ffn_names, 0), True,
                                                     "gather_rest_l0_start", dep=w_in0)

    def get_layer(l, h):
        if l == 0:
            return mixer_params(0, w_in0, None), rest0_token
        return mixer_params(1, *_exchange_wait(gathers[1, 'mix'], True, h, "gather_mix_l1_wait")), None

    def get_ffn(l, after):
        landed = _exchange_wait(gathers[l, 'ffn'], True, after, "gather_ffn_l%d_wait" % l)
        if l > 0:
            return landed[0], landed[1].reshape(D_FF, D_MODEL), None, None
        g_out, g_up, g_down = landed
        gathers[1, 'mix'], token = _exchange_start(shards(mix_names, 1), True, "gather_mix_l1_start", dep=g_up)
        gathers[1, 'ffn'], token = _exchange_start(shards(ffn_names, 1), True, "gather_ffn_l1_start", dep=token)
        return g_up, g_down.reshape(D_FF, D_MODEL), g_out.reshape(D_MODEL, D_MODEL), token

    scatters = {}

    def after_ffn_grads(l, g):
        send = [g['w_up_g'], g['w_down'].reshape(N_DEV, D_FF // N_DEV, D_MODEL)]
        scatters[l, 'ffn'], token = _exchange_start(send, False, "scatter_ffn_l%d_start" % l)
        return token

    def after_out_grad(l, g_out):
        send = [g_out.reshape(N_DEV, D_MODEL // N_DEV, D_MODEL)]
        scatters[l, 'out'], token = _exchange_start(send, False, "scatter_out_l%d_start" % l)
        return token

    def after_in_grad(l, g_in):
        send = _scatter_blocks(g_in, 1)
        if l == 0:
            send = send.astype(BF16)
        scatters[l, 'in'], token = _exchange_start([send], False, "scatter_in_l%d_start" % l)
        return token

    def after_small_grads(l, g):
        rep = [g[n][None] for n in REPLICATED]
        shd = [_scatter_blocks(g[n], SHARD_AXIS[n] - 1)[:, None] for n in SMALL_SHARDED]
        scatters[l, 'rep'], token = _exchange_start(rep, True, "gather_rep_grads_l%d_start" % l)
        scatters[l, 'small'], token = _exchange_start(shd, False, "scatter_small_l%d_start" % l, dep=token)
        return token

    loss_local, grad_x = _run_step(x[0], loss_target[0], get_layer, get_ffn, after_ffn_grads, after_out_grad,
                                   after_small_grads, after_in_grad)
    loss = lax.psum(loss_local, AXES)

    results = {}
    big_prev = {n: None for n in BIG}

    def finish_big(l, part, names, after):
        landed = _exchange_wait(scatters[l, part], False, after, "scatter_%s_l%d_wait" % (part, l))
        for n, ld in zip(names, landed):
            big_prev[n] = _adamw_sum(ld, wl[n], ml[n], vl[n], l, big_prev[n], "adamw_%s_l%d" % (n, l))

    for l, part, names in ((1, 'ffn', ffn_names), (1, 'out', ('w_out',)), (1, 'in', ('w_in',)),
                           (0, 'ffn', ffn_names), (0, 'out', ('w_out',))):
        finish_big(l, part, names, grad_x)

    kinds = ('grad', 'delta', 'm', 'v')
    landed = [dict(zip(REPLICATED + SMALL_SHARDED,
                       list(_exchange_wait(scatters[l, 'rep'], True, grad_x, "gather_rep_grads_l%d_wait" % l)) +
                       list(_exchange_wait(scatters[l, 'small'], False, grad_x, "scatter_small_l%d_wait" % l))))
              for l in range(DEPTH)]
    matrices = ['lru_wr', 'lru_wi', 's5_a_re', 's5_a_im', 's5_c_re', 's5_c_im', 's5_d']
    widest = ['s5_b_re', 's5_b_im']
    vectors = [n for n in REPLICATED + SMALL_SHARDED if n not in matrices + widest]
    last = None
    for tag, names in (("vectors", vectors), ("matrices", matrices), ("s5_b", widest)):
        res = _adamw_many([[landed[l][n] for n in names] for l in range(DEPTH)], [wl[n] for n in names],
                          [ml[n] for n in names], [vl[n] for n in names], "adamw_" + tag)
        for kind, arrs in zip(kinds, res):
            for n, a in zip(names, arrs):
                results[kind, n] = a
        last = res[0][0]
    finish_big(0, 'in', ('w_in',), last)
    for n in BIG:
        results['grad', n], results['delta', n], results['m', n], results['v', n] = big_prev[n]

    out = [loss, grad_x[None]]
    for kind in kinds:
        out.extend(results[kind, n] for n in WEIGHTS)
    return tuple(out)
```

```python
import math

import jax
import jax.numpy as jnp
from jax import lax
from jax.experimental import pallas as pl
from jax.experimental.pallas import tpu as pltpu

F32 = jnp.float32
BF16 = jnp.bfloat16

N_DEV = 8
DEPTH = 2
D_MODEL = 1024
ATTN_W = 384
LRU_W = 384
S5_W = 256
D_IN = 2176
D_FF = 3072
HEAD = 64
ATTN_BLK = 128
ATTN_TILE = 1024
DILATIONS = (1, 4, 16)
S5_G = 16
S5_P = 64
S5_C = 16
S5_STATES = S5_G * S5_P
LRU_C = 8.0
LRU_CONV = 4
FFN_CONV = 3
ROPE_THETA = 10000.0
ALPHA = (2 * DEPTH) ** 0.25
LN_EPS = 1e-5
RMS_EPS = 1e-6
ADAM_LR, ADAM_B1, ADAM_B2, ADAM_EPS, ADAM_WD, ADAM_STEP = 0.001, 0.9, 0.999, 1e-8, 0.01, 10

LANE = 128
SCAN_T = 256
S5_BLK = 256
FFN_CB = 2 * D_FF // N_DEV
VMEM_LIMIT = 56 * 1024 * 1024

AXES = ("x", "y", "c")

WEIGHTS = ['w_in', 'lru_conv_w', 'lru_conv_b', 'lru_wr', 'lru_br', 'lru_wi', 'lru_bi', 'lru_lambda',
           's5_a_re', 's5_a_im', 's5_b_re', 's5_b_im', 's5_c_re', 's5_c_im', 's5_d', 's5_log_step',
           's5_w_glu', 's5_b_glu', 'mix_norm_g', 'w_out', 'ln1_g', 'ln1_b', 'w_up', 'ffn_conv_w',
           'ffn_conv_b', 'w_down', 'ln2_g', 'ln2_b']
SHARD_AXIS = {'w_in': 2, 'lru_conv_w': 2, 's5_w_glu': 1, 'w_out': 1, 'w_up': 2, 'ffn_conv_w': 2, 'w_down': 1}
BIG = ['w_in', 'w_out', 'w_up', 'w_down']
SMALL_SHARDED = ['lru_conv_w', 'ffn_conv_w', 's5_w_glu']
REPLICATED = [n for n in WEIGHTS if n not in SHARD_AXIS]


def _cparams(sem=None):
    return pltpu.CompilerParams(dimension_semantics=sem, vmem_limit_bytes=VMEM_LIMIT)


def _ffn_dev(jb):
    return jb // 2 + (N_DEV // 2) * (jb % 2)


def _gelu(x):
    c = math.sqrt(2.0 / math.pi)
    t = jnp.tanh(c * (x + 0.044715 * (x * x * x)))
    return 0.5 * x * (1.0 + t)


def _gelu_grad(x):
    c = math.sqrt(2.0 / math.pi)
    x2 = x * x
    t = jnp.tanh(c * (x + 0.044715 * (x2 * x)))
    return 0.5 * (1.0 + t) + 0.5 * x * (1.0 - t * t) * (c * (1.0 + 3.0 * 0.044715 * x2))


def _sigmoid(x):
    return 1.0 / (1.0 + jnp.exp(-x))


def _log1p(x):
    u = 1.0 + x
    d = u - 1.0
    return jnp.where(d == 0.0, x, jnp.log(u) * (x / jnp.where(d == 0.0, 1.0, d)))


def _softplus(x):
    return jnp.maximum(x, 0.0) + _log1p(jnp.exp(-jnp.abs(x)))


def _expm1(x):
    return jnp.tanh(0.5 * x) * (jnp.exp(x) + 1.0)


def _dot(a, b):
    return jnp.dot(a.astype(BF16), b.astype(BF16), preferred_element_type=F32)


def _dot_nt(a, b):
    return lax.dot_general(a.astype(BF16), b.astype(BF16), (((1,), (1,)), ((), ())),
                           preferred_element_type=F32)


def _dot_tn(a, b):
    return lax.dot_general(a.astype(BF16), b.astype(BF16), (((0,), (0,)), ((), ())),
                           preferred_element_type=F32)


def _rows(shape):
    return lax.broadcasted_iota(jnp.int32, shape, 0)


def _shift_down_prev(x, s, prev8):
    if s == 0:
        return x
    t, l = x.shape
    r = pltpu.roll(x, s, axis=0)
    pr = pltpu.roll(prev8, s, axis=0)
    pad = jnp.concatenate([pr, jnp.zeros((t - 8, l), x.dtype)], axis=0)
    return jnp.where(_rows(x.shape) < s, pad, r)


def _shift_up_next(x, s, next8):
    if s == 0:
        return x
    t, l = x.shape
    r = pltpu.roll(x, t - s, axis=0)
    nx = pltpu.roll(next8, 8 - s, axis=0)
    pad = jnp.concatenate([jnp.zeros((t - 8, l), x.dtype), nx], axis=0)
    return jnp.where(_rows(x.shape) >= t - s, pad, r)


SUB = 8


def _tile_shift(x, s, fill, reverse):
    t = x.shape[0]
    pos = _rows(x.shape) & (SUB - 1)
    if reverse:
        return jnp.where(pos < SUB - s, pltpu.roll(x, t - s, axis=0), fill)
    return jnp.where(pos >= s, pltpu.roll(x, s, axis=0), fill)


def _scan_chunk(a, x, carry, reverse=False):
    s = 1
    while s < SUB:
        x = x + a * _tile_shift(x, s, 0.0, reverse)
        a = a * _tile_shift(a, s, 1.0, reverse)
        s *= 2
    nv = x.shape[0] // SUB
    out = [None] * nv
    for v in (reversed(range(nv)) if reverse else range(nv)):
        rows = slice(v * SUB, (v + 1) * SUB)
        out[v] = x[rows, :] + a[rows, :] * carry
        carry = out[v][0:1, :] if reverse else out[v][SUB - 1:SUB, :]
    return jnp.concatenate(out, axis=0)


def _cmul(ar, ai, br, bi):
    return ar * br - ai * bi, ar * bi + ai * br


def _cscan_consts(lr, li, reverse):
    pows = [(lr, li)]
    for _ in range(2):
        pows.append(_cmul(*pows[-1], *pows[-1]))
    rows = [(lr, li)]
    for _ in range(SUB - 1):
        rows.append(_cmul(*rows[-1], lr, li))
    if reverse:
        rows = rows[::-1]
    return pows, (jnp.concatenate([r for r, _ in rows], axis=0), jnp.concatenate([i for _, i in rows], axis=0))


def _cscan_chunk(xr, xi, consts, carry, reverse=False):
    pows, (p8r, p8i) = consts
    s = 1
    for pr, pi in pows:
        sr = _tile_shift(xr, s, 0.0, reverse)
        si = _tile_shift(xi, s, 0.0, reverse)
        xr, xi = xr + pr * sr - pi * si, xi + pr * si + pi * sr
        s *= 2
    nv = xr.shape[0] // SUB
    out_r, out_i = [None] * nv, [None] * nv
    cr, ci = carry
    for v in (reversed(range(nv)) if reverse else range(nv)):
        rows = slice(v * SUB, (v + 1) * SUB)
        out_r[v] = xr[rows, :] + p8r * cr - p8i * ci
        out_i[v] = xi[rows, :] + p8r * ci + p8i * cr
        edge = slice(0, 1) if reverse else slice(SUB - 1, SUB)
        cr, ci = out_r[v][edge, :], out_i[v][edge, :]
    return jnp.concatenate(out_r, axis=0), jnp.concatenate(out_i, axis=0)


def _dep_args(dep):
    return ([], []) if dep is None else ([pl.BlockSpec(memory_space=pl.ANY)], [dep])


def _mm_nt(a, w, tm, tn, name, add=None, add_scale=1.0, dep=None):
    m, k = a.shape
    n = w.shape[0]

    def body(a_ref, w_ref, *rest):
        o_ref = rest[-1]
        if add is None:
            o_ref[...] = _dot_nt(a_ref[...], w_ref[...])
        else:
            o_ref[...] = _dot_nt(a_ref[...], w_ref[...]) + add_scale * rest[0][...]

    in_specs = [pl.BlockSpec((tm, k), lambda j, i: (i, 0)), pl.BlockSpec((tn, k), lambda j, i: (j, 0))]
    args = [a, w]
    if add is not None:
        in_specs.append(pl.BlockSpec((tm, tn), lambda j, i: (i, j)))
        args.append(add)
    dep_specs, dep_ops = _dep_args(dep)
    return pl.pallas_call(
        body, out_shape=jax.ShapeDtypeStruct((m, n), F32), grid=(n // tn, m // tm),
        in_specs=in_specs + dep_specs, out_specs=pl.BlockSpec((tm, tn), lambda j, i: (i, j)), name=name,
        compiler_params=_cparams(("parallel", "parallel")))(*args, *dep_ops)


def _mm_dw(at, b, tm, tn, ts, name):
    m, s = at.shape
    n = b.shape[1]

    def body(a_ref, b_ref, o_ref):
        @pl.when(pl.program_id(2) == 0)
        def _():
            o_ref[...] = jnp.zeros_like(o_ref)
        o_ref[...] += _dot(a_ref[...], b_ref[...])

    return pl.pallas_call(
        body, out_shape=jax.ShapeDtypeStruct((m, n), F32), grid=(m // tm, n // tn, s // ts),
        in_specs=[pl.BlockSpec((tm, ts), lambda i, j, k: (i, k)), pl.BlockSpec((ts, tn), lambda i, j, k: (k, j))],
        out_specs=pl.BlockSpec((tm, tn), lambda i, j, k: (i, j)), name=name,
        compiler_params=_cparams(("parallel", "parallel", "arbitrary")))(at, b)


def _mm_up_dw(ht, dup, name):
    d, s = ht.shape

    def body(a_ref, b_ref, o_ref):
        o_ref[...] = _dot(a_ref[...], b_ref[...])

    return pl.pallas_call(
        body, out_shape=jax.ShapeDtypeStruct((N_DEV, d, FFN_CB), F32), grid=(N_DEV,),
        in_specs=[pl.BlockSpec((d, s), lambda j: (0, 0)), pl.BlockSpec((s, FFN_CB), lambda j: (0, j))],
        out_specs=pl.BlockSpec((None, d, FFN_CB), lambda j: (_ffn_dev(j), 0, 0)), name=name,
        compiler_params=_cparams(("parallel",)))(ht, dup)


def _layer_norm(r, g, b):
    mu = jnp.mean(r, axis=-1, keepdims=True)
    xc = r - mu
    var = jnp.mean(xc * xc, axis=-1, keepdims=True)
    return xc * lax.rsqrt(var + LN_EPS) * g + b


def _proj_ln(a, w, resid, g, bias, name, transposed=True, target=None):
    s, k = a.shape
    d = w.shape[1]
    tm = 512

    def body(a_ref, w_ref, x_ref, g_ref, bias_ref, *rest):
        r = ALPHA * x_ref[...] + _dot(a_ref[...], w_ref[...])
        h = _layer_norm(r, g_ref[...], bias_ref[...])
        if target is None:
            r_ref, h_ref = rest[0], rest[1]
            h_ref[...] = h
            if transposed:
                rest[2][...] = h.T.astype(BF16)
        else:
            t_ref, r_ref, dy_ref, l_ref = rest

            @pl.when(pl.program_id(0) == 0)
            def _():
                l_ref[...] = jnp.zeros_like(l_ref)
            e = h - t_ref[...]
            dy_ref[...] = e * (1.0 / d)
            part = 0.5 * jnp.sum(jnp.mean(e * e, axis=-1, keepdims=True), axis=0, keepdims=True)
            l_ref[...] += jnp.broadcast_to(part, l_ref.shape)
        r_ref[...] = r

    row = pl.BlockSpec((tm, d), lambda i: (i, 0))
    vec = pl.BlockSpec((1, d), lambda i: (0, 0))
    in_specs = [pl.BlockSpec((tm, k), lambda i: (i, 0)), pl.BlockSpec((k, d), lambda i: (0, 0)), row, vec, vec]
    args = [a, w, resid, g, bias]
    shapes = [jax.ShapeDtypeStruct((s, d), F32), jax.ShapeDtypeStruct((s, d), F32)]
    specs = [row, row]
    if target is not None:
        in_specs.append(row)
        args.append(target)
        shapes.append(jax.ShapeDtypeStruct((1, LANE), F32))
        specs.append(pl.BlockSpec((1, LANE), lambda i: (0, 0)))
    elif transposed:
        shapes.append(jax.ShapeDtypeStruct((d, s), BF16))
        specs.append(pl.BlockSpec((d, tm), lambda i: (0, i)))
    return pl.pallas_call(
        body, out_shape=tuple(shapes), grid=(s // tm,), in_specs=in_specs, out_specs=tuple(specs), name=name,
        compiler_params=_cparams(("arbitrary",) if target is not None else ("parallel",)))(*args)


def _layer_norm_bwd(r, dh, g):
    mu = jnp.mean(r, axis=-1, keepdims=True)
    xc = r - mu
    var = jnp.mean(xc * xc, axis=-1, keepdims=True)
    rstd = lax.rsqrt(var + LN_EPS)
    xh = xc * rstd
    dxh = dh * g
    m1 = jnp.mean(dxh, axis=-1, keepdims=True)
    m2 = jnp.mean(dxh * xh, axis=-1, keepdims=True)
    return (rstd * (dxh - m1 - xh * m2), jnp.sum(dh * xh, axis=0, keepdims=True),
            jnp.sum(dh, axis=0, keepdims=True))


def _ln_bwd(r, dh, g, name, dep=None):
    s, d = r.shape
    tm = 512

    def body(r_ref, dh_ref, g_ref, *rest):
        dr_ref, dg_ref, db_ref = rest[-3:]

        @pl.when(pl.program_id(0) == 0)
        def _():
            dg_ref[...] = jnp.zeros_like(dg_ref)
            db_ref[...] = jnp.zeros_like(db_ref)
        dr_ref[...], dg_rows, db_rows = _layer_norm_bwd(r_ref[...], dh_ref[...], g_ref[...])
        dg_ref[...] += dg_rows
        db_ref[...] += db_rows

    row = pl.BlockSpec((tm, d), lambda i: (i, 0))
    vec = pl.BlockSpec((1, d), lambda i: (0, 0))
    dep_specs, dep_ops = _dep_args(dep)
    return pl.pallas_call(
        body, out_shape=(jax.ShapeDtypeStruct((s, d), F32), jax.ShapeDtypeStruct((1, d), F32),
                         jax.ShapeDtypeStruct((1, d), F32)),
        grid=(s // tm,), in_specs=[row, row, vec] + dep_specs, out_specs=(row, vec, vec), name=name,
        compiler_params=_cparams(("arbitrary",)))(r, dh, g, *dep_ops)


def _rope_tables(s):
    half = HEAD // 2
    pos = jnp.arange(s, dtype=F32)
    inv = ROPE_THETA ** (-jnp.arange(half, dtype=F32) * 2.0 / HEAD)
    ang = pos[:, None] * inv[None, :]
    cos, sin = jnp.cos(ang), jnp.sin(ang)
    cos = jnp.concatenate([cos, cos, cos, cos], axis=1)
    sin = jnp.concatenate([-sin, sin, -sin, sin], axis=1)
    return cos, sin


def _rotate(x, cos, sin):
    lane = lax.broadcasted_iota(jnp.int32, x.shape, 1)
    partner = jnp.where((lane % HEAD) < HEAD // 2, pltpu.roll(x, LANE - HEAD // 2, axis=1),
                        pltpu.roll(x, HEAD // 2, axis=1))
    return x * cos + partner * sin


def _class_rows(c, d, tm):
    return pl.ds(c, tm // d, stride=d) if d > 1 else pl.ds(0, tm)


def _dilated_spec(tm, d, w):
    return pl.BlockSpec((tm // d, d * w), lambda i: (i, 0))


def _token_scratch(tm, w):
    return pltpu.VMEM((w // LANE, tm, LANE), F32)


def _to_tokens(src_ref, dst3, d, tm):
    nj = dst3.shape[0]
    for cls in range(d):
        for j in range(nj):
            col = (cls * nj + j) * LANE
            dst3.at[j][_class_rows(cls, d, tm), :] = src_ref[:, col:col + LANE]


def _to_dilated(src3, dst_ref, d, tm):
    nj = src3.shape[0]
    for cls in range(d):
        for j in range(nj):
            col = (cls * nj + j) * LANE
            dst_ref[:, col:col + LANE] = src3.at[j][_class_rows(cls, d, tm), :].astype(dst_ref.dtype)


def _token_value(src3):
    return jnp.concatenate([src3[j] for j in range(src3.shape[0])], axis=1)


def _proj_rope(h, w_in, cos, sin, name, dep=None, transposed=False):
    s, d_model = h.shape
    tm = 512
    w = 3 * ATTN_W
    nj = w // LANE

    def body(h_ref, w_ref, c_ref, s_ref, *rest):
        rot = rest[-1]
        if transposed:
            p_ref, o_refs, ht_ref = rest[-6], rest[-5:-2], rest[-2]
            ht_ref[...] = h_ref[...].T.astype(BF16)
        else:
            p_ref, o_refs = rest[-5], rest[-4:-1]
        y = _dot(h_ref[...], w_ref[...])
        p_ref[...] = y
        c, sn = c_ref[...], s_ref[...]
        for j in range(nj):
            x = y[:, j * LANE:(j + 1) * LANE]
            rot[j] = _rotate(x, c, sn) if j < 2 * ATTN_W // LANE else x
        for d, o_ref in zip(DILATIONS, o_refs):
            _to_dilated(rot, o_ref, d, tm)

    tab = pl.BlockSpec((tm, LANE), lambda i: (i, 0))
    dep_specs, dep_ops = _dep_args(dep)
    shapes = [jax.ShapeDtypeStruct((s, D_IN), F32), *[jax.ShapeDtypeStruct((s // d, d * w), BF16) for d in DILATIONS]]
    specs = [pl.BlockSpec((tm, D_IN), lambda i: (i, 0)), *[_dilated_spec(tm, d, w) for d in DILATIONS]]
    if transposed:
        shapes.append(jax.ShapeDtypeStruct((d_model, s), BF16))
        specs.append(pl.BlockSpec((d_model, tm), lambda i: (0, i)))
    res = pl.pallas_call(
        body, out_shape=tuple(shapes), grid=(s // tm,),
        in_specs=[pl.BlockSpec((tm, d_model), lambda i: (i, 0)), pl.BlockSpec((d_model, D_IN), lambda i: (0, 0)),
                  tab, tab] + dep_specs,
        out_specs=tuple(specs), scratch_shapes=[_token_scratch(tm, w)], name=name,
        compiler_params=_cparams(("parallel",)))(h, w_in, cos, sin, *dep_ops)
    return res[0], res[1:4], (res[4] if transposed else None)


def _dproj_assemble(dqkv_list, dxr, dgate, du, cos, sin, name):
    s = dxr.shape[0]
    tm = 512
    nq = 3 * ATTN_W // LANE

    def body(*refs):
        br = refs[:9]
        dxr_ref, dg_ref, du_ref, c_ref, s_ref, o_ref = refs[9:15]
        tok = refs[15:]
        c, sn = c_ref[...], -s_ref[...]
        for part in range(3):
            for b, d in enumerate(DILATIONS[1:], start=1):
                _to_tokens(br[3 * b + part], tok[2 * part + b - 1], d, tm)
        for j in range(nq):
            part, jj = divmod(j, ATTN_W // LANE)
            x = br[part][:, jj * LANE:(jj + 1) * LANE] + tok[2 * part][jj] + tok[2 * part + 1][jj]
            if part < 2:
                x = _rotate(x, c, sn)
            o_ref[:, j * LANE:(j + 1) * LANE] = x.astype(BF16)
        o_ref[:, 3 * ATTN_W:3 * ATTN_W + LRU_W] = dxr_ref[...].astype(BF16)
        o_ref[:, 3 * ATTN_W + LRU_W:3 * ATTN_W + 2 * LRU_W] = dg_ref[...].astype(BF16)
        o_ref[:, 3 * ATTN_W + 2 * LRU_W:] = du_ref[...].astype(BF16)

    a_spec = pl.BlockSpec((tm, ATTN_W), lambda i: (i, 0))
    tab = pl.BlockSpec((tm, LANE), lambda i: (i, 0))
    ordered = [dqkv_list[b][p] for b in range(3) for p in range(3)]
    d_specs = [_dilated_spec(tm, d, ATTN_W) for d in DILATIONS for _ in range(3)]
    return pl.pallas_call(
        body, out_shape=jax.ShapeDtypeStruct((s, D_IN), BF16), grid=(s // tm,),
        in_specs=d_specs + [a_spec, a_spec, pl.BlockSpec((tm, S5_W), lambda i: (i, 0)), tab, tab],
        out_specs=pl.BlockSpec((tm, D_IN), lambda i: (i, 0)),
        scratch_shapes=[_token_scratch(tm, ATTN_W)] * 6, name=name,
        compiler_params=_cparams(("parallel",)))(*ordered, dxr, dgate, du, cos, sin)


def _attn_tiles(s, d):
    m = s // d
    tq = min(m, ATTN_TILE)
    return m, tq, tq // ATTN_BLK


def _band_mask(qb):
    qi = lax.broadcasted_iota(jnp.int32, (ATTN_BLK, 2 * ATTN_BLK), 0)
    ki = lax.broadcasted_iota(jnp.int32, (ATTN_BLK, 2 * ATTN_BLK), 1)
    dist = qi + ATTN_BLK - ki
    return (dist >= 0) & (dist <= ATTN_BLK) & ((ki >= ATTN_BLK) | (qb > 0))


def _head_cols(h):
    return (slice(h * HEAD, (h + 1) * HEAD), slice(ATTN_W + h * HEAD, ATTN_W + (h + 1) * HEAD),
            slice(2 * ATTN_W + h * HEAD, 2 * ATTN_W + (h + 1) * HEAD))


def _attn_fwd(qv, d, name):
    m = qv.shape[0]
    w3 = 3 * ATTN_W
    _, tq, n = _attn_tiles(m * d, d)
    scale = HEAD ** -0.5

    def body(x_ref, p_ref, o_ref, l_ref):
        b = pl.program_id(1)

        def block(i, first):
            r0 = 0 if first else pl.multiple_of(i * ATTN_BLK, ATTN_BLK)
            rows = pl.ds(r0, ATTN_BLK)
            valid = _band_mask(b * n + i)
            if not first:
                krows = pl.ds(pl.multiple_of(i * ATTN_BLK - ATTN_BLK, ATTN_BLK), 2 * ATTN_BLK)
            for h in range(ATTN_W // HEAD):
                qs, ks, vs = _head_cols(h)
                q = x_ref[rows, qs]
                if first:
                    k = jnp.concatenate([p_ref[:, ks], x_ref[0:ATTN_BLK, ks]], axis=0)
                    v = jnp.concatenate([p_ref[:, vs], x_ref[0:ATTN_BLK, vs]], axis=0)
                else:
                    k = x_ref[krows, ks]
                    v = x_ref[krows, vs]
                sc = jnp.where(valid, _dot_nt(q, k) * scale, -1e30)
                mx = jnp.max(sc, axis=-1, keepdims=True)
                p = jnp.exp(sc - mx)
                l = jnp.sum(p, axis=-1, keepdims=True)
                o_ref[rows, qs] = _dot(p, v) / l
                l_ref[rows, qs] = jnp.broadcast_to(mx + jnp.log(l), (ATTN_BLK, HEAD))

        block(0, True)
        if n > 1:
            def loop(i, carry):
                block(i, False)
                return carry
            lax.fori_loop(1, n, loop, 0)

    shp = jax.ShapeDtypeStruct((m, d * ATTN_W), F32)
    ospec = pl.BlockSpec((tq, ATTN_W), lambda c, b: (b, c))
    out, lse = pl.pallas_call(
        body, out_shape=(shp, shp), grid=(d, m // tq),
        in_specs=[pl.BlockSpec((tq, w3), lambda c, b: (b, c)),
                  pl.BlockSpec((ATTN_BLK, w3), lambda c, b: (jnp.maximum(b * n - 1, 0), c))],
        out_specs=(ospec, ospec), name=name,
        compiler_params=_cparams(("parallel", "parallel")))(qv, qv)
    return out, lse


def _attn_bwd(qv, ov, dov, lv, d, name, dep=None):
    m = qv.shape[0]
    w3 = 3 * ATTN_W
    _, tq, n = _attn_tiles(m * d, d)
    nb = m // ATTN_BLK
    scale = HEAD ** -0.5

    def body(x_ref, p_ref, nx_ref, o_ref, do_ref, l_ref, on_ref, don_ref, ln_ref, *rest):
        dq_ref, dk_ref, dv_ref = rest[-3:]
        b = pl.program_id(1)
        dk_ref[...] = jnp.zeros_like(dk_ref)
        dv_ref[...] = jnp.zeros_like(dv_ref)

        def grads(q, k, v, o, do, lse, valid):
            sc = jnp.where(valid, _dot_nt(q, k) * scale, -1e30)
            p = jnp.exp(sc - lse)
            delta = jnp.sum(do * o, axis=-1, keepdims=True)
            return p, p * (_dot_nt(do, v) - delta) * scale

        def block(i, first):
            r0 = 0 if first else pl.multiple_of(i * ATTN_BLK, ATTN_BLK)
            rows = pl.ds(r0, ATTN_BLK)
            valid = _band_mask(b * n + i)
            if not first:
                krows = pl.ds(pl.multiple_of(i * ATTN_BLK - ATTN_BLK, ATTN_BLK), 2 * ATTN_BLK)
            for h in range(ATTN_W // HEAD):
                qs, ks, vs = _head_cols(h)
                q = x_ref[rows, qs]
                do = do_ref[rows, qs]
                if first:
                    k = jnp.concatenate([p_ref[:, ks], x_ref[0:ATTN_BLK, ks]], axis=0)
                    v = jnp.concatenate([p_ref[:, vs], x_ref[0:ATTN_BLK, vs]], axis=0)
                else:
                    k = x_ref[krows, ks]
                    v = x_ref[krows, vs]
                p, ds = grads(q, k, v, o_ref[rows, qs], do, l_ref[rows, qs][:, 0:1], valid)
                dq_ref[rows, qs] = _dot(ds, k)
                if first:
                    dk_ref[0:ATTN_BLK, qs] += _dot_tn(ds[:, ATTN_BLK:], q)
                    dv_ref[0:ATTN_BLK, qs] += _dot_tn(p[:, ATTN_BLK:], do)
                else:
                    dk_ref[krows, qs] += _dot_tn(ds, q)
                    dv_ref[krows, qs] += _dot_tn(p, do)

        block(0, True)
        if n > 1:
            def loop(i, carry):
                block(i, False)
                return carry
            lax.fori_loop(1, n, loop, 0)

        last = slice((n - 1) * ATTN_BLK, n * ATTN_BLK)
        qi = lax.broadcasted_iota(jnp.int32, (ATTN_BLK, ATTN_BLK), 0)
        ki = lax.broadcasted_iota(jnp.int32, (ATTN_BLK, ATTN_BLK), 1)
        valid_next = (qi <= ki) & ((b + 1) * n < nb)
        for h in range(ATTN_W // HEAD):
            qs, ks, vs = _head_cols(h)
            q = nx_ref[:, qs]
            do = don_ref[:, qs]
            p, ds = grads(q, x_ref[last, ks], x_ref[last, vs], on_ref[:, qs], do, ln_ref[:, qs][:, 0:1],
                          valid_next)
            dk_ref[last, qs] += _dot_tn(ds, q)
            dv_ref[last, qs] += _dot_tn(p, do)

    nxt = lambda b: jnp.minimum((b + 1) * n, nb - 1)
    xs = pl.BlockSpec((tq, w3), lambda c, b: (b, c))
    xp = pl.BlockSpec((ATTN_BLK, w3), lambda c, b: (jnp.maximum(b * n - 1, 0), c))
    xn = pl.BlockSpec((ATTN_BLK, w3), lambda c, b: (nxt(b), c))
    a = pl.BlockSpec((tq, ATTN_W), lambda c, b: (b, c))
    an = pl.BlockSpec((ATTN_BLK, ATTN_W), lambda c, b: (nxt(b), c))
    shp = jax.ShapeDtypeStruct((m, d * ATTN_W), F32)
    dep_specs, dep_ops = _dep_args(dep)
    return pl.pallas_call(
        body, out_shape=(shp, shp, shp), grid=(d, m // tq),
        in_specs=[xs, xp, xn, a, a, a, an, an, an] + dep_specs, out_specs=(a, a, a), name=name,
        compiler_params=_cparams(("parallel", "parallel")))(qv, qv, qv, ov, dov, lv, ov, dov, lv, *dep_ops)


def _rms(x, g):
    ms = jnp.mean(x * x, axis=-1, keepdims=True)
    return x * lax.rsqrt(ms + RMS_EPS) * g


def _rms_bwd(x, g, dy):
    ms = jnp.mean(x * x, axis=-1, keepdims=True)
    r = lax.rsqrt(ms + RMS_EPS)
    dyg = dy * g
    dx = r * dyg - x * (r * r * r) * jnp.mean(x * dyg, axis=-1, keepdims=True)
    return dx, dy * x * r


def _mix_fwd(outs, lses, lru, s5, g, h_in, w_out, ln_g, ln_b, name):
    s = lru.shape[0]
    tm = 256

    def body(o1, o2, o3, l1, l2, l3, lru_ref, s5_ref, g_ref, x_ref, w_ref, lg_ref, lb_ref,
             mixed_t_ref, r_ref, h_ref, ht_ref, ov1, ov2, ov3, lv1, lv2, lv3, so2, so3, sl2, sl3):
        for d, src, dst in ((DILATIONS[1], o2, so2), (DILATIONS[2], o3, so3),
                            (DILATIONS[1], l2, sl2), (DILATIONS[2], l3, sl3)):
            _to_tokens(src, dst, d, tm)
        a1, a2, a3 = l1[...], _token_value(sl2), _token_value(sl3)
        mx = jnp.maximum(jnp.maximum(a1, a2), a3)
        e1, e2, e3 = jnp.exp(a1 - mx), jnp.exp(a2 - mx), jnp.exp(a3 - mx)
        den = e1 + e2 + e3
        o = (e1 * o1[...] + e2 * _token_value(so2) + e3 * _token_value(so3)) / den
        lse = mx + jnp.log(den)
        ov1[...] = o
        lv1[...] = lse
        for j in range(ATTN_W // LANE):
            so2[j] = o[:, j * LANE:(j + 1) * LANE]
            sl2[j] = lse[:, j * LANE:(j + 1) * LANE]
        for d, o_dst, l_dst in ((DILATIONS[1], ov2, lv2), (DILATIONS[2], ov3, lv3)):
            _to_dilated(so2, o_dst, d, tm)
            _to_dilated(sl2, l_dst, d, tm)
        gg = g_ref[...]
        mixed = jnp.concatenate([_rms(o, gg[:, :ATTN_W]),
                                 _rms(lru_ref[...], gg[:, ATTN_W:ATTN_W + LRU_W]),
                                 _rms(s5_ref[...], gg[:, ATTN_W + LRU_W:])], axis=1)
        mixed_t_ref[...] = mixed.T.astype(BF16)
        r = ALPHA * x_ref[...] + _dot(mixed, w_ref[...])
        h = _layer_norm(r, lg_ref[...], lb_ref[...])
        r_ref[...] = r
        h_ref[...] = h
        ht_ref[...] = h.T.astype(BF16)

    a = pl.BlockSpec((tm, ATTN_W), lambda i: (i, 0))
    s5s = pl.BlockSpec((tm, S5_W), lambda i: (i, 0))
    full = pl.BlockSpec((tm, D_MODEL), lambda i: (i, 0))
    vec = pl.BlockSpec((1, D_MODEL), lambda i: (0, 0))
    dil = [_dilated_spec(tm, d, ATTN_W) for d in DILATIONS]
    dshape = [jax.ShapeDtypeStruct((s // d, d * ATTN_W), F32) for d in DILATIONS]
    tshape = jax.ShapeDtypeStruct((D_MODEL, s), BF16)
    fshape = jax.ShapeDtypeStruct((s, D_MODEL), F32)
    tspec = pl.BlockSpec((D_MODEL, tm), lambda i: (0, i))
    res = pl.pallas_call(
        body, out_shape=(tshape, fshape, fshape, tshape, *dshape, *dshape),
        grid=(s // tm,),
        in_specs=dil + dil + [a, s5s, vec, full, pl.BlockSpec((D_MODEL, D_MODEL), lambda i: (0, 0)), vec, vec],
        out_specs=(tspec, full, full, tspec, *dil, *dil),
        scratch_shapes=[_token_scratch(tm, ATTN_W)] * 4, name=name,
        compiler_params=_cparams(("parallel",)))(*outs, *lses, lru, s5, g, h_in, w_out, ln_g, ln_b)
    return res[0], res[1], res[2], res[3], res[4:7], res[7:10]


def _mix_bwd(r, dh, ln_g, w_out, o, lru, s5, g, name, dep=None):
    s = lru.shape[0]
    tm = 256

    def body(r_ref, dh_ref, lg_ref, w_ref, o_ref, lru_ref, s5_ref, g_ref, *rest):
        dr_ref, dlg_ref, dlb_ref, do_ref, do2_ref, do3_ref, dlru_ref, ds5_ref, dg_ref, stage = rest[-10:]

        @pl.when(pl.program_id(0) == 0)
        def _():
            dg_ref[...] = jnp.zeros_like(dg_ref)
            dlg_ref[...] = jnp.zeros_like(dlg_ref)
            dlb_ref[...] = jnp.zeros_like(dlb_ref)
        gg = g_ref[...]
        dr, dlg_rows, dlb_rows = _layer_norm_bwd(r_ref[...], dh_ref[...], lg_ref[...])
        dr_ref[...] = dr
        dlg_ref[...] += dlg_rows
        dlb_ref[...] += dlb_rows
        dm = _dot_nt(dr, w_ref[...])
        dx, dgr = _rms_bwd(o_ref[...], gg[:, :ATTN_W], dm[:, :ATTN_W])
        do_ref[...] = dx
        for j in range(ATTN_W // LANE):
            stage[j] = dx[:, j * LANE:(j + 1) * LANE]
        _to_dilated(stage, do2_ref, DILATIONS[1], tm)
        _to_dilated(stage, do3_ref, DILATIONS[2], tm)
        dg_ref[:, :ATTN_W] += jnp.sum(dgr, axis=0, keepdims=True)
        dx, dgr = _rms_bwd(lru_ref[...], gg[:, ATTN_W:ATTN_W + LRU_W], dm[:, ATTN_W:ATTN_W + LRU_W])
        dlru_ref[...] = dx
        dg_ref[:, ATTN_W:ATTN_W + LRU_W] += jnp.sum(dgr, axis=0, keepdims=True)
        dx, dgr = _rms_bwd(s5_ref[...], gg[:, ATTN_W + LRU_W:], dm[:, ATTN_W + LRU_W:])
        ds5_ref[...] = dx
        dg_ref[:, ATTN_W + LRU_W:] += jnp.sum(dgr, axis=0, keepdims=True)

    a = pl.BlockSpec((tm, ATTN_W), lambda i: (i, 0))
    s5s = pl.BlockSpec((tm, S5_W), lambda i: (i, 0))
    full = pl.BlockSpec((tm, D_MODEL), lambda i: (i, 0))
    vec = pl.BlockSpec((1, D_MODEL), lambda i: (0, 0))
    dil = [_dilated_spec(tm, d, ATTN_W) for d in DILATIONS]
    dshape = [jax.ShapeDtypeStruct((s // d, d * ATTN_W), F32) for d in DILATIONS]
    dep_specs, dep_ops = _dep_args(dep)
    vshape = jax.ShapeDtypeStruct((1, D_MODEL), F32)
    res = pl.pallas_call(
        body, out_shape=(jax.ShapeDtypeStruct((s, D_MODEL), F32), vshape, vshape, *dshape,
                         jax.ShapeDtypeStruct((s, LRU_W), F32), jax.ShapeDtypeStruct((s, S5_W), F32), vshape),
        grid=(s // tm,),
        in_specs=[full, full, vec, pl.BlockSpec((D_MODEL, D_MODEL), lambda i: (0, 0)), a, a, s5s, vec] + dep_specs,
        out_specs=(full, vec, vec, *dil, a, s5s, vec), scratch_shapes=[_token_scratch(tm, ATTN_W)], name=name,
        compiler_params=_cparams(("arbitrary",)))(r, dh, ln_g, w_out, o, lru, s5, g, *dep_ops)
    return res[0], res[1], res[2], res[3:6], res[6], res[7], res[8]


def _lru_gate_math(xc, pre_r, pre_i, lam):
    r = _sigmoid(pre_r)
    i = _sigmoid(pre_i)
    log_a = -LRU_C * r * _softplus(-lam)
    a = jnp.exp(log_a)
    u = jnp.sqrt(-_expm1(2.0 * log_a)) * (i * xc)
    return a, u


def _lru_conv(x, prev8, cw, cb):
    y = cb + cw[LRU_CONV - 1:LRU_CONV, :] * x
    for k in range(LRU_CONV - 1):
        y = y + cw[k:k + 1, :] * _shift_down_prev(x, LRU_CONV - 1 - k, prev8)
    return y


def _lru_specs(s):
    xo = 3 * ATTN_W // LANE
    go = xo + LRU_W // LANE
    xr = pl.BlockSpec((s, LANE), lambda j: (0, xo + j))
    gt = pl.BlockSpec((s, LANE), lambda j: (0, go + j))
    cw = pl.BlockSpec((LRU_CONV, LANE), lambda j: (0, j))
    vec = pl.BlockSpec((1, LANE), lambda j: (0, j))
    wbd = pl.BlockSpec((LANE, LANE), lambda j: (j, j))
    col = pl.BlockSpec((s, LANE), lambda j: (0, j))
    return xr, gt, cw, vec, wbd, col


def _lru_fwd(proj, cw, cb, wr, br, wi, bi, lam, name):
    s = proj.shape[0]
    t = SCAN_T

    def body(xr_ref, gt_ref, cw_ref, cb_ref, wr_ref, br_ref, wi_ref, bi_ref, lam_ref, o_ref, xc_ref, a_ref, h_ref):
        cwv, cbv, lamv = cw_ref[...], cb_ref[...], lam_ref[...]
        wrv, wiv, brv, biv = wr_ref[...], wi_ref[...], br_ref[...], bi_ref[...]

        def chunk(c, carry):
            h_c, prev8 = carry
            rows = pl.ds(pl.multiple_of(c * t, t), t)
            x = xr_ref[rows, :]
            xc = _lru_conv(x, prev8, cwv, cbv)
            a, u = _lru_gate_math(xc, _dot(xc, wrv) + brv, _dot(xc, wiv) + biv, lamv)
            h = _scan_chunk(a, u, h_c)
            xc_ref[rows, :] = xc
            a_ref[rows, :] = a
            h_ref[rows, :] = h
            o_ref[rows, :] = h * _gelu(gt_ref[rows, :])
            return h[t - 1:t, :], x[t - 8:t, :]

        lax.fori_loop(0, s // t, chunk, (jnp.zeros((1, LANE), F32), jnp.zeros((8, LANE), F32)))

    xr, gt, cws, vec, wbd, col = _lru_specs(s)
    shp = jax.ShapeDtypeStruct((s, LRU_W), F32)
    return pl.pallas_call(
        body, out_shape=(shp,) * 4, grid=(LRU_W // LANE,),
        in_specs=[xr, gt, cws, vec, wbd, vec, wbd, vec, vec], out_specs=(col,) * 4, name=name,
        compiler_params=_cparams(("parallel",)))(proj, proj, cw, cb, wr, br, wi, bi, lam)


def _lru_bwd(proj, dout, xc_all, a_all, h_all, cw, cb, wr, br, wi, bi, lam, name):
    s = proj.shape[0]
    t = SCAN_T
    nc = s // t

    def body(xr_ref, gt_ref, do_ref, xc_s, a_s, h_s, cw_ref, cb_ref, wr_ref, br_ref, wi_ref, bi_ref, lam_ref,
             dxr_ref, dgt_ref, dcw_ref, dcb_ref, dwr_ref, dbr_ref, dwi_ref, dbi_ref, dlam_ref):
        cwv, cbv, lamv = cw_ref[...], cb_ref[...], lam_ref[...]
        wrv, wiv, brv, biv = wr_ref[...], wi_ref[...], br_ref[...], bi_ref[...]
        z1 = jnp.zeros((1, LANE), F32)
        zw = jnp.zeros((LANE, LANE), F32)

        def bchunk(ci, carry):
            g_next, a_next, dxc_next8, dcw, dcb, dwr, dbr, dwi, dbi, dlam = carry
            c = nc - 1 - ci
            t0 = pl.multiple_of(c * t, t)
            rows = pl.ds(t0, t)
            before = pl.ds(pl.multiple_of(jnp.maximum(t0 - 8, 0), 8), 8)
            has_prev = (c > 0).astype(F32)
            x, gt, do = xr_ref[rows, :], gt_ref[rows, :], do_ref[rows, :]
            xc, a, h = xc_s[rows, :], a_s[rows, :], h_s[rows, :]
            prev8_h = h_s[before, :] * has_prev
            dgt_ref[rows, :] = do * h * _gelu_grad(gt)
            dh = do * _gelu(gt)
            a_plus = _shift_up_next(a, 1, jnp.broadcast_to(a_next, (8, LANE)))
            g = _scan_chunk(a_plus, dh, g_next, reverse=True)
            da = g * _shift_down_prev(h, 1, prev8_h)
            pre_r = _dot(xc, wrv) + brv
            pre_i = _dot(xc, wiv) + biv
            _, vjp = jax.vjp(_lru_gate_math, xc, pre_r, pre_i, lamv)
            dxc, dpre_r, dpre_i, dlam_c = vjp((da, g))
            dxc = dxc + _dot_nt(dpre_r, wrv) + _dot_nt(dpre_i, wiv)
            dx = cwv[LRU_CONV - 1:LRU_CONV, :] * dxc
            dcw_rows = [None] * LRU_CONV
            dcw_rows[LRU_CONV - 1] = jnp.sum(dxc * x, axis=0, keepdims=True)
            for k in range(LRU_CONV - 1):
                dxc_ahead = _shift_up_next(dxc, LRU_CONV - 1 - k, dxc_next8)
                dx = dx + cwv[k:k + 1, :] * dxc_ahead
                dcw_rows[k] = jnp.sum(dxc_ahead * x, axis=0, keepdims=True)
            dxr_ref[rows, :] = dx
            return (g[0:1, :], a[0:1, :], dxc[0:8, :],
                    dcw + jnp.concatenate(dcw_rows, axis=0),
                    dcb + jnp.sum(dxc, axis=0, keepdims=True),
                    dwr + _dot_tn(xc, dpre_r), dbr + jnp.sum(dpre_r, axis=0, keepdims=True),
                    dwi + _dot_tn(xc, dpre_i), dbi + jnp.sum(dpre_i, axis=0, keepdims=True),
                    dlam + dlam_c)

        init = (z1, z1, jnp.zeros((8, LANE), F32), jnp.zeros((LRU_CONV, LANE), F32), z1, zw, z1, zw, z1, z1)
        res = lax.fori_loop(0, nc, bchunk, init)
        dcw_ref[...] = res[3]
        dcb_ref[...] = res[4]
        dwr_ref[...] = res[5]
        dbr_ref[...] = res[6]
        dwi_ref[...] = res[7]
        dbi_ref[...] = res[8]
        dlam_ref[...] = res[9]

    xr, gt, cws, vec, wbd, col = _lru_specs(s)
    vshape = jax.ShapeDtypeStruct((1, LRU_W), F32)
    wshape = jax.ShapeDtypeStruct((LRU_W, LRU_W), F32)
    return pl.pallas_call(
        body,
        out_shape=(jax.ShapeDtypeStruct((s, LRU_W), F32), jax.ShapeDtypeStruct((s, LRU_W), F32),
                   jax.ShapeDtypeStruct((LRU_CONV, LRU_W), F32), vshape, wshape, vshape, wshape, vshape, vshape),
        grid=(LRU_W // LANE,),
        in_specs=[xr, gt, col, col, col, col, cws, vec, wbd, vec, wbd, vec, vec],
        out_specs=(col, col, cws, vec, wbd, vec, wbd, vec, vec), name=name,
        compiler_params=_cparams(("parallel",)))(proj, proj, dout, xc_all, a_all, h_all, cw, cb, wr, br, wi, bi,
                                                 lam)


def _s5_disc_math(a_re, a_im, log_step, bt_re, bt_im):
    step = jnp.exp(log_step)
    dt_re, dt_im = step * a_re, step * a_im
    mag = jnp.exp(dt_re)
    ab_re, ab_im = mag * jnp.cos(dt_im), mag * jnp.sin(dt_im)
    z_re, z_im = ab_re - 1.0, ab_im
    den = a_re * a_re + a_im * a_im
    f_re = (z_re * a_re + z_im * a_im) / den
    f_im = (z_im * a_re - z_re * a_im) / den
    bb_re = f_re * bt_re - f_im * bt_im
    bb_im = f_re * bt_im + f_im * bt_re
    return ab_re, ab_im, bb_re, bb_im


def _s5_disc_fwd(a_re, a_im, log_step, bt_re, bt_im, name):
    def body(ar, ai, ls, br, bi, o1, o2, o3, o4):
        r = _s5_disc_math(ar[...], ai[...], ls[...], br[...], bi[...])
        o1[...], o2[...], o3[...], o4[...] = r

    shp = jax.ShapeDtypeStruct(a_re.shape, F32)
    return pl.pallas_call(body, out_shape=(shp,) * 4, name=name)(a_re, a_im, log_step, bt_re, bt_im)


def _s5_disc_bwd(a_re, a_im, log_step, bt_re, bt_im, cts, name):
    def body(ar, ai, ls, br, bi, c1, c2, c3, c4, o1, o2, o3, o4, o5):
        _, vjp = jax.vjp(_s5_disc_math, ar[...], ai[...], ls[...], br[...], bi[...])
        r = vjp((c1[...], c2[...], c3[...], c4[...]))
        o1[...], o2[...], o3[...], o4[...], o5[...] = r

    shp = jax.ShapeDtypeStruct(a_re.shape, F32)
    return pl.pallas_call(body, out_shape=(shp,) * 5, name=name)(a_re, a_im, log_step, bt_re, bt_im, *cts)


def _s5_u_specs(s):
    uo = (3 * ATTN_W + 2 * LRU_W) // LANE
    return (pl.BlockSpec((s, LANE), lambda j: (0, uo)), pl.BlockSpec((s, LANE), lambda j: (0, uo + 1)))


def _s5_scan_fwd(proj, b_re, b_im, lam_re, lam_im, c_re, c_im, name):
    s = proj.shape[0]
    t = SCAN_T

    def body(u0_ref, u1_ref, bre_ref, bim_ref, lre_ref, lim_ref, cre_ref, cim_ref, xre_ref, xim_ref, y_ref):
        @pl.when(pl.program_id(0) == 0)
        def _():
            y_ref[...] = jnp.zeros_like(y_ref)
        lr, li = lre_ref[...], lim_ref[...]
        consts = _cscan_consts(lr, li, False)
        bre, bim, cre, cim = bre_ref[...], bim_ref[...], cre_ref[...], cim_ref[...]

        def chunk(c, carry):
            cr, ci = carry
            rows = pl.ds(pl.multiple_of(c * t, t), t)
            u = jnp.concatenate([u0_ref[rows, :], u1_ref[rows, :]], axis=1).astype(BF16)
            xr, xi = _cscan_chunk(_dot(u, bre), _dot(u, bim), consts, (cr, ci))
            xre_ref[rows, :] = xr
            xim_ref[rows, :] = xi
            y_ref[rows, :] += _dot(xr, cre) - _dot(xi, cim)
            return xr[t - 1:t, :], xi[t - 1:t, :]

        z = jnp.zeros((1, S5_BLK), F32)
        lax.fori_loop(0, s // t, chunk, (z, z))

    u0, u1 = _s5_u_specs(s)
    bsp = pl.BlockSpec((S5_W, S5_BLK), lambda j: (0, j))
    csp = pl.BlockSpec((S5_BLK, S5_W), lambda j: (j, 0))
    vec = pl.BlockSpec((1, S5_BLK), lambda j: (0, j))
    xsp = pl.BlockSpec((s, S5_BLK), lambda j: (0, j))
    ysp = pl.BlockSpec((s, S5_W), lambda j: (0, 0))
    xshape = jax.ShapeDtypeStruct((s, S5_STATES), F32)
    return pl.pallas_call(
        body, out_shape=(xshape, xshape, jax.ShapeDtypeStruct((s, S5_W), F32)),
        grid=(S5_STATES // S5_BLK,), in_specs=[u0, u1, bsp, bsp, vec, vec, csp, csp],
        out_specs=(xsp, xsp, ysp), name=name,
        compiler_params=_cparams(("arbitrary",)))(proj, proj, b_re, b_im, lam_re, lam_im, c_re, c_im)


def _s5_scan_bwd(proj, dy, du_init, x_re, x_im, b_re, b_im, lam_re, lam_im, c_re, c_im, name):
    s = proj.shape[0]
    t = SCAN_T
    nc = s // t

    def body(u0_ref, u1_ref, dy_ref, dui_ref, xre_ref, xim_ref, bre_ref, bim_ref, lre_ref, lim_ref,
             cre_ref, cim_ref, du_ref, dlr_ref, dli_ref, dbr_ref, dbi_ref, dcr_ref, dci_ref):
        @pl.when(pl.program_id(0) == 0)
        def _():
            du_ref[...] = dui_ref[...]
        mr, mi = lre_ref[...], -lim_ref[...]
        consts = _cscan_consts(mr, mi, True)
        bre, bim, cre, cim = bre_ref[...], bim_ref[...], cre_ref[...], cim_ref[...]
        dbr_ref[...] = jnp.zeros_like(dbr_ref)
        dbi_ref[...] = jnp.zeros_like(dbi_ref)
        dcr_ref[...] = jnp.zeros_like(dcr_ref)
        dci_ref[...] = jnp.zeros_like(dci_ref)

        def chunk(ci_, carry):
            gnr, gni, dlr, dli = carry
            c = nc - 1 - ci_
            t0 = pl.multiple_of(c * t, t)
            rows = pl.ds(t0, t)
            before = pl.ds(pl.multiple_of(jnp.maximum(t0 - 8, 0), 8), 8)
            has_prev = (c > 0).astype(F32)
            dyc = dy_ref[rows, :].astype(BF16)
            u = jnp.concatenate([u0_ref[rows, :], u1_ref[rows, :]], axis=1).astype(BF16)
            gr, gi = _cscan_chunk(_dot_nt(dyc, cre), -_dot_nt(dyc, cim), consts, (gnr, gni), reverse=True)
            xr, xi = xre_ref[rows, :], xim_ref[rows, :]
            xpr = _shift_down_prev(xr, 1, xre_ref[before, :] * has_prev)
            xpi = _shift_down_prev(xi, 1, xim_ref[before, :] * has_prev)
            dlr = dlr + jnp.sum(gr * xpr + gi * xpi, axis=0, keepdims=True)
            dli = dli + jnp.sum(gi * xpr - gr * xpi, axis=0, keepdims=True)
            du_ref[rows, :] += _dot_nt(gr, bre) + _dot_nt(gi, bim)
            dbr_ref[...] += _dot_tn(u, gr)
            dbi_ref[...] += _dot_tn(u, gi)
            dcr_ref[...] += _dot_tn(xr, dyc)
            dci_ref[...] -= _dot_tn(xi, dyc)
            return gr[0:1, :], gi[0:1, :], dlr, dli

        z = jnp.zeros((1, S5_BLK), F32)
        res = lax.fori_loop(0, nc, chunk, (z, z, z, z))
        dlr_ref[...] = res[2]
        dli_ref[...] = res[3]

    u0, u1 = _s5_u_specs(s)
    bsp = pl.BlockSpec((S5_W, S5_BLK), lambda j: (0, j))
    csp = pl.BlockSpec((S5_BLK, S5_W), lambda j: (j, 0))
    vec = pl.BlockSpec((1, S5_BLK), lambda j: (0, j))
    xsp = pl.BlockSpec((s, S5_BLK), lambda j: (0, j))
    ysp = pl.BlockSpec((s, S5_W), lambda j: (0, 0))
    return pl.pallas_call(
        body,
        out_shape=(jax.ShapeDtypeStruct((s, S5_W), F32),
                   jax.ShapeDtypeStruct((1, S5_STATES), F32), jax.ShapeDtypeStruct((1, S5_STATES), F32),
                   jax.ShapeDtypeStruct((S5_W, S5_STATES), F32), jax.ShapeDtypeStruct((S5_W, S5_STATES), F32),
                   jax.ShapeDtypeStruct((S5_STATES, S5_W), F32), jax.ShapeDtypeStruct((S5_STATES, S5_W), F32)),
        grid=(S5_STATES // S5_BLK,),
        in_specs=[u0, u1, ysp, ysp, xsp, xsp, bsp, bsp, vec, vec, csp, csp],
        out_specs=(ysp, vec, vec, bsp, bsp, csp, csp), name=name,
        compiler_params=_cparams(("arbitrary",)))(
            proj, proj, dy, du_init, x_re, x_im, b_re, b_im, lam_re, lam_im, c_re, c_im)


def _s5_out_fwd(proj, y_acc, dvec, w_glu, b_glu, name):
    s = proj.shape[0]
    tm = 512
    uo = (3 * ATTN_W + 2 * LRU_W) // LANE

    def body(u0_ref, u1_ref, y_ref, d_ref, w_ref, b_ref, o_ref, yp_ref):
        u = jnp.concatenate([u0_ref[...], u1_ref[...]], axis=1)
        y = y_ref[...] + d_ref[...] * u
        yp_ref[...] = y
        yg = _gelu(y)
        o_ref[...] = yg * _sigmoid(_dot(yg, w_ref[...]) + b_ref[...])

    u0 = pl.BlockSpec((tm, LANE), lambda i: (i, uo))
    u1 = pl.BlockSpec((tm, LANE), lambda i: (i, uo + 1))
    row = pl.BlockSpec((tm, S5_W), lambda i: (i, 0))
    vec = pl.BlockSpec((1, S5_W), lambda i: (0, 0))
    wsp = pl.BlockSpec((S5_W, S5_W), lambda i: (0, 0))
    shp = jax.ShapeDtypeStruct((s, S5_W), F32)
    return pl.pallas_call(
        body, out_shape=(shp, shp), grid=(s // tm,), in_specs=[u0, u1, row, vec, wsp, vec],
        out_specs=(row, row), name=name,
        compiler_params=_cparams(("parallel",)))(proj, proj, y_acc, dvec, w_glu, b_glu)


def _s5_out_bwd(proj, y_pre, dout, dvec, w_glu, b_glu, name, dep=None):
    s = proj.shape[0]
    tm = 512
    uo = (3 * ATTN_W + 2 * LRU_W) // LANE

    def body(u0_ref, u1_ref, y_ref, do_ref, d_ref, w_ref, b_ref, *rest):
        dy_ref, dud_ref, dd_ref, dw_ref, db_ref = rest[-5:]

        @pl.when(pl.program_id(0) == 0)
        def _():
            dd_ref[...] = jnp.zeros_like(dd_ref)
            dw_ref[...] = jnp.zeros_like(dw_ref)
            db_ref[...] = jnp.zeros_like(db_ref)
        u = jnp.concatenate([u0_ref[...], u1_ref[...]], axis=1)
        y = y_ref[...]
        do = do_ref[...]
        yg = _gelu(y)
        sg = _sigmoid(_dot(yg, w_ref[...]) + b_ref[...])
        dz = do * yg * sg * (1.0 - sg)
        dyg = do * sg + _dot_nt(dz, w_ref[...])
        dy = dyg * _gelu_grad(y)
        dy_ref[...] = dy
        dud_ref[...] = d_ref[...] * dy
        dd_ref[...] += jnp.sum(dy * u, axis=0, keepdims=True)
        dw_ref[...] += _dot_tn(yg, dz)
        db_ref[...] += jnp.sum(dz, axis=0, keepdims=True)

    u0 = pl.BlockSpec((tm, LANE), lambda i: (i, uo))
    u1 = pl.BlockSpec((tm, LANE), lambda i: (i, uo + 1))
    row = pl.BlockSpec((tm, S5_W), lambda i: (i, 0))
    vec = pl.BlockSpec((1, S5_W), lambda i: (0, 0))
    wsp = pl.BlockSpec((S5_W, S5_W), lambda i: (0, 0))
    shp = jax.ShapeDtypeStruct((s, S5_W), F32)
    vshape = jax.ShapeDtypeStruct((1, S5_W), F32)
    dep_specs, dep_ops = _dep_args(dep)
    return pl.pallas_call(
        body, out_shape=(shp, shp, vshape, jax.ShapeDtypeStruct((S5_W, S5_W), F32), vshape),
        grid=(s // tm,), in_specs=[u0, u1, row, row, vec, wsp, vec] + dep_specs,
        out_specs=(row, row, vec, wsp, vec), name=name,
        compiler_params=_cparams(("arbitrary",)))(proj, proj, y_pre, dout, dvec, w_glu, b_glu, *dep_ops)


def _ffn_conv(x, prev8, cw, cb):
    y = cb + cw[FFN_CONV - 1:FFN_CONV, :] * x
    for k in range(FFN_CONV - 1):
        y = y + cw[k:k + 1, :] * _shift_down_prev(x, FFN_CONV - 1 - k, prev8)
    return y


def _ffn_up_act(h, wg, cw, cb, name, dep=None):
    s, d = h.shape
    tm = 512
    tb = 2 * FFN_CB
    nt = D_FF // FFN_CB

    def body(h_ref, wgate_ref, wval_ref, cw_ref, cb_ref, *rest):
        up_ref, y_ref, o_ref, ot_ref, carry = rest[-5:]

        @pl.when(pl.program_id(1) == 0)
        def _():
            carry[...] = jnp.zeros_like(carry)
        hb = h_ref[...].astype(BF16)
        x = jnp.concatenate([_dot(hb, wgate_ref[...]), _dot(hb, wval_ref[...])], axis=1)
        up_ref[...] = x
        y = _ffn_conv(x, carry[...], cw_ref[...], cb_ref[...])
        y_ref[...] = y
        carry[...] = x[tm - 8:tm, :]
        act = _gelu(y[:, :FFN_CB]) * y[:, FFN_CB:]
        o_ref[...] = act.astype(BF16)
        ot_ref[...] = act.T.astype(BF16)

    dep_specs, dep_ops = _dep_args(dep)
    return pl.pallas_call(
        body, out_shape=(jax.ShapeDtypeStruct((s, 2 * D_FF), F32), jax.ShapeDtypeStruct((s, 2 * D_FF), F32),
                         jax.ShapeDtypeStruct((s, D_FF), BF16), jax.ShapeDtypeStruct((D_FF, s), BF16)),
        grid=(nt, s // tm),
        in_specs=[pl.BlockSpec((tm, d), lambda t, i: (i, 0)),
                  pl.BlockSpec((None, d, FFN_CB), lambda t, i: (t, 0, 0)),
                  pl.BlockSpec((None, d, FFN_CB), lambda t, i: (t + nt, 0, 0)),
                  pl.BlockSpec((FFN_CONV, tb), lambda t, i: (0, t)),
                  pl.BlockSpec((1, tb), lambda t, i: (0, t))] + dep_specs,
        out_specs=(pl.BlockSpec((tm, tb), lambda t, i: (i, t)), pl.BlockSpec((tm, tb), lambda t, i: (i, t)),
                   pl.BlockSpec((tm, FFN_CB), lambda t, i: (i, t)), pl.BlockSpec((FFN_CB, tm), lambda t, i: (t, i))),
        scratch_shapes=[pltpu.VMEM((8, tb), F32)], name=name,
        compiler_params=_cparams(("parallel", "arbitrary")))(h, wg, wg, cw, cb, *dep_ops)


def _ffn_bwd(up, y_conv, dr, w_down, wg, cw, name):
    s = up.shape[0]
    d = dr.shape[1]
    tm = 256
    tb = 2 * FFN_CB
    nr = s // tm
    nt = D_FF // FFN_CB

    def body(x_ref, y_ref, dr_ref, wd_ref, wgate_ref, wval_ref, cw_ref,
             dup_ref, dh_ref, dcw_ref, dcb_ref, carry):
        i, t = pl.program_id(0), pl.program_id(1)

        @pl.when(i == 0)
        def _():
            carry[t] = jnp.zeros((8, tb), F32)

        @pl.when(t == 0)
        def _():
            dh_ref[...] = ALPHA * dr_ref[...]
        cwv = cw_ref[...]
        x = x_ref[...]
        dact = _dot_nt(dr_ref[...], wd_ref[...])
        gate, val = y_ref[:, :FFN_CB], y_ref[:, FFN_CB:]
        dy = jnp.concatenate([dact * val * _gelu_grad(gate), dact * _gelu(gate)], axis=1)
        next8 = carry[t]
        carry[t] = dy[0:8, :]
        dx = cwv[FFN_CONV - 1:FFN_CONV, :] * dy
        dcw_rows = [None] * FFN_CONV
        dcw_rows[FFN_CONV - 1] = jnp.sum(dy * x, axis=0, keepdims=True)
        for k in range(FFN_CONV - 1):
            dy_ahead = _shift_up_next(dy, FFN_CONV - 1 - k, next8)
            dx = dx + cwv[k:k + 1, :] * dy_ahead
            dcw_rows[k] = jnp.sum(dy_ahead * x, axis=0, keepdims=True)
        dup = dx.astype(BF16)
        dup_ref[...] = dup
        dh_ref[...] += _dot_nt(dup[:, :FFN_CB], wgate_ref[...]) + _dot_nt(dup[:, FFN_CB:], wval_ref[...])
        dcw_ref[...] = jnp.concatenate(dcw_rows, axis=0)
        dcb_ref[...] = jnp.sum(dy, axis=0, keepdims=True)

    row = lambda i: nr - 1 - i
    return pl.pallas_call(
        body, out_shape=(jax.ShapeDtypeStruct((s, 2 * D_FF), BF16), jax.ShapeDtypeStruct((s, d), F32),
                         jax.ShapeDtypeStruct((nr, FFN_CONV, 2 * D_FF), F32),
                         jax.ShapeDtypeStruct((nr, 1, 2 * D_FF), F32)),
        grid=(nr, nt),
        in_specs=[pl.BlockSpec((tm, tb), lambda i, t: (row(i), t)),
                  pl.BlockSpec((tm, tb), lambda i, t: (row(i), t)),
                  pl.BlockSpec((tm, d), lambda i, t: (row(i), 0)),
                  pl.BlockSpec((FFN_CB, d), lambda i, t: (t, 0)),
                  pl.BlockSpec((None, d, FFN_CB), lambda i, t: (t, 0, 0)),
                  pl.BlockSpec((None, d, FFN_CB), lambda i, t: (t + nt, 0, 0)),
                  pl.BlockSpec((FFN_CONV, tb), lambda i, t: (0, t))],
        out_specs=(pl.BlockSpec((tm, tb), lambda i, t: (row(i), t)),
                   pl.BlockSpec((tm, d), lambda i, t: (row(i), 0)),
                   pl.BlockSpec((None, FFN_CONV, tb), lambda i, t: (row(i), 0, t)),
                   pl.BlockSpec((None, 1, tb), lambda i, t: (row(i), 0, t))),
        scratch_shapes=[pltpu.VMEM((nt, 8, tb), F32)], name=name,
        compiler_params=_cparams(("arbitrary", "arbitrary")))(up, y_conv, dr, w_down, wg, wg, cw)


def _sum_partials(ld_ref):
    gg = ld_ref[0].astype(F32)
    for k in range(1, N_DEV):
        gg = gg + ld_ref[k].astype(F32)
    return gg


def _adam_update(w, g, m, v):
    mn = ADAM_B1 * m + (1.0 - ADAM_B1) * g
    vn = ADAM_B2 * v + (1.0 - ADAM_B2) * (g * g)
    m_hat = mn / (1.0 - ADAM_B1 ** ADAM_STEP)
    v_hat = vn / (1.0 - ADAM_B2 ** ADAM_STEP)
    return -ADAM_LR * (m_hat / (jnp.sqrt(v_hat) + ADAM_EPS) + ADAM_WD * w), mn, vn


def _adamw_many(landed, ws, ms, vs, name):
    n, nl = len(ws), len(landed)

    def body(*refs):
        ld = refs[:nl * n]
        w_refs, m_refs, v_refs = (refs[(nl + k) * n:(nl + k + 1) * n] for k in range(3))
        outs = refs[(nl + 3) * n:]
        for i in range(n):
            for l in range(nl):
                one = slice(l, l + 1)
                gg = _sum_partials(ld[l * n + i])
                outs[i][one] = gg
                outs[n + i][one], outs[2 * n + i][one], outs[3 * n + i][one] = _adam_update(
                    w_refs[i][one], gg, m_refs[i][one], v_refs[i][one])

    vm = pl.BlockSpec(memory_space=pltpu.VMEM)
    shapes = [jax.ShapeDtypeStruct(w.shape, F32) for w in ws] * 4
    res = pl.pallas_call(
        body, out_shape=tuple(shapes), in_specs=[vm] * ((nl + 3) * n), out_specs=tuple([vm] * (4 * n)),
        name=name, compiler_params=_cparams())(*[a for layer in landed for a in layer], *ws, *ms, *vs)
    return res[:n], res[n:2 * n], res[2 * n:3 * n], res[3 * n:]


def _adamw_sum(landed, w, m, v, layer, prev, name):
    _, r, c = landed.shape
    nl = w.shape[0]
    tm = 8
    for cand in (512, 256, 128, 64, 32, 16):
        if r % cand == 0 and N_DEV * cand * c * 4 <= 4 * 1024 * 1024:
            tm = cand
            break

    def body(*refs):
        ld_ref, w_ref, m_ref, v_ref = refs[:4]
        g_ref, d_ref, mo_ref, vo_ref = refs[-4:]
        gg = _sum_partials(ld_ref)
        g_ref[...] = gg
        d_ref[...], mo_ref[...], vo_ref[...] = _adam_update(w_ref[...], gg, m_ref[...], v_ref[...])

    blk = pl.BlockSpec((None, tm, c), lambda i: (layer, i, 0))
    in_specs = [pl.BlockSpec((N_DEV, tm, c), lambda i: (0, i, 0)), blk, blk, blk]
    args = [landed, w, m, v]
    aliases = {}
    if prev is not None:
        in_specs += [pl.BlockSpec(memory_space=pl.ANY)] * 4
        args += list(prev)
        aliases = {4 + k: k for k in range(4)}
    shp = jax.ShapeDtypeStruct((nl, r, c), F32)
    return pl.pallas_call(
        body, out_shape=(shp,) * 4, grid=(r // tm,), in_specs=in_specs, out_specs=(blk,) * 4,
        input_output_aliases=aliases, name=name, compiler_params=_cparams(("parallel",)))(*args)


def _all_gather(shards, name):
    na = len(shards)

    def body(*refs):
        x_refs, out_refs = refs[:na], refs[na:2 * na]
        send_sems, recv_sems, local_sems = refs[2 * na:]
        x, y, c = lax.axis_index("x"), lax.axis_index("y"), lax.axis_index("c")
        me, sibling = (x, y, c), (x, y, 1 - c)
        chips = [(1 - x, y), (x, 1 - y), (1 - x, 1 - y)]

        def copy(a, k, block, to, src=None):
            dst = out_refs[a].at[4 * block[0] + 2 * block[1] + block[2]]
            return pltpu.make_async_remote_copy(
                src_ref=dst if src is None else src, dst_ref=dst,
                send_sem=send_sems.at[7 * a + k], recv_sem=recv_sems.at[7 * a + k],
                device_id=to, device_id_type=pl.DeviceIdType.MESH)

        mine, first, passed = [], [], []
        for a in range(na):
            cp = pltpu.make_async_copy(x_refs[a], out_refs[a].at[4 * x + 2 * y + c], local_sems.at[a])
            cp.start()
            mine.append(cp)
            cps = [copy(a, 0, me, sibling, src=x_refs[a])]
            cps += [copy(a, 1 + j, me, (*chip, c), src=x_refs[a]) for j, chip in enumerate(chips)]
            for cp in cps:
                cp.start()
            first += cps
        for j, chip in enumerate(chips):
            for a in range(na):
                copy(a, 1 + j, (*chip, c), me).wait_recv()
                cp = copy(a, 4 + j, (*chip, c), sibling)
                cp.start()
                passed.append(cp)
        for a in range(na):
            copy(a, 0, sibling, me).wait_recv()
            for j, chip in enumerate(chips):
                copy(a, 4 + j, (*chip, 1 - c), me).wait_recv()
        for cp in first + passed:
            cp.wait_send()
        for cp in mine:
            cp.wait()

    anyspec = pl.BlockSpec(memory_space=pl.ANY)
    return pl.pallas_call(
        body, out_shape=tuple(jax.ShapeDtypeStruct((N_DEV,) + t.shape, t.dtype) for t in shards),
        in_specs=[anyspec] * na, out_specs=tuple([anyspec] * na),
        scratch_shapes=[pltpu.SemaphoreType.DMA((7 * na,)), pltpu.SemaphoreType.DMA((7 * na,)),
                        pltpu.SemaphoreType.DMA((na,))],
        name=name)(*shards)


_HBM = pl.BlockSpec(memory_space=pltpu.HBM)
_SEM = pl.BlockSpec(memory_space=pltpu.SEMAPHORE)
_EFFECT = pltpu.SideEffectType.DATAFLOW_SIDE_EFFECTING


def _exchange_copies(src_refs, land_refs, send_sems, recv_sems, local_sems, gather):
    x, y, c = lax.axis_index("x"), lax.axis_index("y"), lax.axis_index("c")
    me = 4 * x + 2 * y + c
    per_array = send_sems.shape[0] > N_DEV - 1
    local, remote = [], []
    for a, (src, land) in enumerate(zip(src_refs, land_refs)):
        local.append(pltpu.make_async_copy(src if gather else src.at[me], land.at[me],
                                           local_sems.at[a if per_array else 0]))
    for k in range(1, N_DEV):
        px = x ^ ((k >> 2) & 1)
        py = y ^ ((k >> 1) & 1)
        pc = c ^ (k & 1)
        for a, (src, land) in enumerate(zip(src_refs, land_refs)):
            remote.append(pltpu.make_async_remote_copy(
                src_ref=src if gather else src.at[4 * px + 2 * py + pc], dst_ref=land.at[me],
                send_sem=send_sems.at[(7 * a if per_array else 0) + k - 1],
                recv_sem=recv_sems.at[(7 * a if per_array else 0) + k - 1],
                device_id=(px, py, pc), device_id_type=pl.DeviceIdType.MESH))
    return local, remote


def _exchange_start(srcs, gather, name, dep=None):
    na = len(srcs)
    ns = na if na <= 4 else 1
    lands = [lax.empty(((N_DEV,) + t.shape) if gather else t.shape, t.dtype) for t in srcs]

    def body(*refs):
        src_refs, land_refs = refs[:na], refs[na:2 * na]
        nin = 2 * na + (0 if dep is None else 1)
        send_sems, recv_sems, local_sems = refs[nin:nin + 3]
        token = refs[-1]
        local, remote = _exchange_copies(src_refs, land_refs, send_sems, recv_sems, local_sems, gather)
        for cp in local + remote:
            cp.start()
        token[...] = jnp.zeros_like(token)

    dep_specs, dep_ops = _dep_args(dep)
    hbm = lambda t: pltpu.HBM(t.shape, t.dtype)
    out = pl.pallas_call(
        body, name=name,
        out_shape=(pltpu.SemaphoreType.DMA((7 * ns,)), pltpu.SemaphoreType.DMA((7 * ns,)),
                   pltpu.SemaphoreType.DMA((ns,)), *[hbm(t) for t in srcs], *[hbm(t) for t in lands],
                   jax.ShapeDtypeStruct((8, LANE), F32)),
        in_specs=[_HBM] * (2 * na) + dep_specs,
        out_specs=(_SEM, _SEM, _SEM, *[_HBM] * (2 * na), pl.BlockSpec(memory_space=pltpu.VMEM)),
        input_output_aliases={i: 3 + i for i in range(2 * na)},
        compiler_params=pltpu.CompilerParams(has_side_effects=_EFFECT),
    )(*[pltpu.with_memory_space_constraint(t, pltpu.HBM) for t in srcs + lands], *dep_ops)
    return (out[:3], out[3:3 + na], out[3 + na:3 + 2 * na]), out[-1]


def _exchange_wait(handle, gather, after, name):
    sems, srcs, lands = handle
    na = len(srcs)

    def body(*refs):
        src_refs, land_refs = refs[:na], refs[na:2 * na]
        send_sems, recv_sems, local_sems = refs[2 * na:2 * na + 3]
        local, remote = _exchange_copies(src_refs, land_refs, send_sems, recv_sems, local_sems, gather)
        for cp in remote:
            cp.wait_send()
            cp.wait_recv()
        for cp in local:
            cp.wait()

    hbm = lambda t: pltpu.HBM(t.shape, t.dtype)
    out = pl.pallas_call(
        body, name=name, out_shape=(*[hbm(t) for t in srcs], *[hbm(t) for t in lands]),
        in_specs=[_HBM] * (2 * na) + [_SEM] * 3 + [pl.BlockSpec(memory_space=pl.ANY)],
        out_specs=tuple([_HBM] * (2 * na)), input_output_aliases={i: i for i in range(2 * na)},
        compiler_params=pltpu.CompilerParams(has_side_effects=_EFFECT),
    )(*srcs, *lands, *sems, after)
    return out[na:]


def _block_diag(w):
    h, a, b = w.shape
    eye = jnp.eye(h, dtype=w.dtype)
    return (w[:, :, None, :] * eye[:, None, :, None]).reshape(h * a, h * b)


def _block_diag_extract(m, h):
    a, b = m.shape[0] // h, m.shape[1] // h
    return jnp.stack([m[i * a:(i + 1) * a, i * b:(i + 1) * b] for i in range(h)], axis=0)


def _block_diag_take(m, h):
    a, b = m.shape[0] // h, m.shape[1] // h
    eye = jnp.eye(h, dtype=m.dtype)
    return (m.reshape(h, a, h, b) * eye[:, None, :, None]).sum(axis=2)


def _ffn_interleave(w):
    lead = w.shape[:-1]
    nb = D_FF // FFN_CB
    return jnp.swapaxes(w.reshape(*lead, 2, nb, FFN_CB), -3, -2).reshape(*lead, 2 * D_FF)


def _ffn_deinterleave(w):
    lead = w.shape[:-1]
    nb = D_FF // FFN_CB
    return jnp.swapaxes(w.reshape(*lead, nb, 2, FFN_CB), -3, -2).reshape(*lead, 2 * D_FF)


def _gather_full(gathered, axis):
    shape = list(gathered.shape[1:])
    shape[axis] *= N_DEV
    return jnp.moveaxis(gathered, 0, axis).reshape(shape)


def _scatter_blocks(full, axis):
    shape = list(full.shape)
    shape[axis:axis + 1] = [N_DEV, shape[axis] // N_DEV]
    return jnp.moveaxis(full.reshape(shape), axis, 0)


def _pad_to(flat, mult):
    pad = (-flat.shape[-1]) % mult
    if pad:
        flat = jnp.concatenate([flat, jnp.zeros(flat.shape[:-1] + (pad,), flat.dtype)], axis=-1)
    return flat


def _layer_fwd(h_in, h_in_t, w, cos, sin, l, dep, get_ffn, target=None):
    tag = "l%d_" % l
    proj, qkv, h_t = _proj_rope(h_in, w['w_in'], cos, sin, tag + "proj_rope", dep=dep,
                                transposed=h_in_t is None)
    h_in_t = h_t if h_in_t is None else h_in_t
    outs, lses = [], []
    for d, qv in zip(DILATIONS, qkv):
        o, ls = _attn_fwd(qv, d, tag + "attn_d%d" % d)
        outs.append(o)
        lses.append(ls)
    lru, *lru_saved = _lru_fwd(proj, w['lru_conv_w'], w['lru_conv_b'], w['lru_wr'], w['lru_br'], w['lru_wi'],
                               w['lru_bi'], w['lru_lambda'], tag + "lru")
    x_re, x_im, y_acc = _s5_scan_fwd(proj, w['s5_bb_re'], w['s5_bb_im'], w['s5_lam_re'], w['s5_lam_im'],
                                     w['s5_cc_re'], w['s5_cc_im'], tag + "s5_scan")
    s5, y_pre = _s5_out_fwd(proj, y_acc, w['s5_d'], w['s5_w_glu'], w['s5_b_glu'], tag + "s5_out")
    w['w_up_g'], w['w_down'], w_out, ffn_dep = get_ffn(l, s5)
    if w_out is not None:
        w['w_out'] = w_out
    mixed_t, r1, h1, h1_t, attn_o, attn_lse = _mix_fwd(outs, lses, lru, s5, w['mix_norm_g'], h_in, w['w_out'],
                                                       w['ln1_g'], w['ln1_b'], tag + "mix_out_ln1")
    up, y_conv, act, act_t = _ffn_up_act(h1, w['w_up_g'], w['ffn_conv_w'], w['ffn_conv_b'], tag + "up_act",
                                         dep=ffn_dep)
    r2, out_a, out_b = _proj_ln(act, w['w_down'], h1, w['ln2_g'], w['ln2_b'], tag + "down_ln2", target=target)
    saved = dict(h_in_t=h_in_t, proj=proj, qkv=qkv, lru=lru, lru_saved=lru_saved, x_re=x_re, x_im=x_im,
                 y_pre=y_pre, s5=s5, mixed_t=mixed_t, attn_o=attn_o, attn_lse=attn_lse, r1=r1, h1_t=h1_t, up=up,
                 act_t=act_t, r2=r2, y_conv=y_conv)
    return out_a, out_b, saved


def _layer_bwd_ffn(dh2, sv, w, l, dep=None):
    tag = "l%d_" % l
    g = {}
    dr2, g['ln2_g'], g['ln2_b'] = _ln_bwd(sv['r2'], dh2, w['ln2_g'], tag + "ln2_bwd", dep=dep)
    g['w_down'] = _mm_dw(sv['act_t'], dr2, 1024, D_MODEL, 1024, tag + "down_dw")
    dup, dh1, dcw_parts, dcb_parts = _ffn_bwd(sv['up'], sv['y_conv'], dr2, w['w_down'], w['w_up_g'],
                                              w['ffn_conv_w'], tag + "ffn_bwd")
    g['ffn_conv_w'] = dcw_parts.sum(axis=0)
    g['ffn_conv_b'] = dcb_parts.sum(axis=0)
    g['w_up_g'] = _mm_up_dw(sv['h1_t'], dup, tag + "up_dw")
    return dh1, g


def _layer_bwd_mix(dh1, sv, w, cos, sin, l, dep, g_ffn, after_out_grad, after_small_grads, after_in_grad):
    tag = "l%d_" % l
    g = {}
    dr1, g['ln1_g'], g['ln1_b'], d_o, dlru, ds5, g['mix_norm_g'] = _mix_bwd(
        sv['r1'], dh1, w['ln1_g'], w['w_out'], sv['attn_o'][0], sv['lru'], sv['s5'], w['mix_norm_g'],
        tag + "ln1_mix_bwd", dep=dep)
    g['w_out'] = _mm_dw(sv['mixed_t'], dr1, 1024, D_MODEL, 1024, tag + "out_dw")
    dy, dud, g['s5_d'], g['s5_w_glu'], g['s5_b_glu'] = _s5_out_bwd(
        sv['proj'], sv['y_pre'], ds5, w['s5_d'], w['s5_w_glu'], w['s5_b_glu'], tag + "s5_out_bwd",
        dep=after_out_grad(l, g['w_out']))
    du, g['s5_lam_re'], g['s5_lam_im'], g['s5_bb_re'], g['s5_bb_im'], g['s5_cc_re'], g['s5_cc_im'] = \
        _s5_scan_bwd(sv['proj'], dy, dud, sv['x_re'], sv['x_im'], w['s5_bb_re'], w['s5_bb_im'],
                     w['s5_lam_re'], w['s5_lam_im'], w['s5_cc_re'], w['s5_cc_im'], tag + "s5_scan_bwd")
    (dxr, dgate, g['lru_conv_w'], g['lru_conv_b'], g['lru_wr'], g['lru_br'], g['lru_wi'], g['lru_bi'],
     g['lru_lambda']) = _lru_bwd(sv['proj'], dlru, *sv['lru_saved'], w['lru_conv_w'], w['lru_conv_b'], w['lru_wr'],
                                 w['lru_br'], w['lru_wi'], w['lru_bi'], w['lru_lambda'], tag + "lru_bwd")
    token = after_small_grads(l, _finish_layer_grads({**g_ffn, **g}, w, l))
    dqkv = [_attn_bwd(sv['qkv'][b], sv['attn_o'][b], d_o[b], sv['attn_lse'][b], d, tag + "attn_bwd_d%d" % d,
                      dep=token if b == 0 else None)
            for b, d in enumerate(DILATIONS)]
    dproj = _dproj_assemble(dqkv, dxr, dgate, du, cos, sin, tag + "dproj")
    g_in = _mm_dw(sv['h_in_t'], dproj, 1024, D_IN, 1024, tag + "in_dw")
    return _mm_nt(dproj, w['w_in'], 512, D_MODEL, tag + "in_dx", add=dr1, add_scale=ALPHA,
                  dep=after_in_grad(l, g_in))


def _s5_rep(a):
    return jnp.repeat(a, S5_C, axis=0)


def _prepare_layer(p, l):
    w = {}
    for n in ('w_in', 'w_out', 's5_w_glu'):
        if n in p:
            w[n] = p[n].astype(BF16)
    w['ffn_conv_w'] = _ffn_interleave(p['ffn_conv_w'])
    w['ffn_conv_b'] = _ffn_interleave(p['ffn_conv_b'])[None, :]
    w['lru_conv_w'] = p['lru_conv_w']
    for n in ('lru_conv_b', 'lru_br', 'lru_bi', 'lru_lambda', 's5_b_glu', 'mix_norm_g',
              'ln1_g', 'ln1_b', 'ln2_g', 'ln2_b'):
        w[n] = p[n][None, :]
    w['lru_wr'] = _block_diag(p['lru_wr']).astype(BF16)
    w['lru_wi'] = _block_diag(p['lru_wi']).astype(BF16)
    w['s5_d'] = p['s5_d'].reshape(1, S5_W)
    disc_in = (_s5_rep(p['s5_a_re']), _s5_rep(p['s5_a_im']),
               _s5_rep(jnp.broadcast_to(p['s5_log_step'][:, None], (S5_G, S5_P))),
               jnp.swapaxes(p['s5_b_re'], 1, 2).reshape(S5_W, S5_P),
               jnp.swapaxes(p['s5_b_im'], 1, 2).reshape(S5_W, S5_P))
    ab_re, ab_im, bb_re, bb_im = _s5_disc_fwd(*disc_in, "l%d_s5_disc" % l)
    w['s5_disc_in'] = disc_in
    w['s5_lam_re'] = ab_re.reshape(S5_G, S5_C, S5_P)[:, 0, :].reshape(1, S5_STATES)
    w['s5_lam_im'] = ab_im.reshape(S5_G, S5_C, S5_P)[:, 0, :].reshape(1, S5_STATES)
    w['s5_bb_re'] = _block_diag(bb_re.reshape(S5_G, S5_C, S5_P)).astype(BF16)
    w['s5_bb_im'] = _block_diag(bb_im.reshape(S5_G, S5_C, S5_P)).astype(BF16)
    w['s5_cc_re'] = _block_diag(jnp.swapaxes(p['s5_c_re'], 1, 2)).astype(BF16)
    w['s5_cc_im'] = _block_diag(jnp.swapaxes(p['s5_c_im'], 1, 2)).astype(BF16)
    return w


def _finish_layer_grads(g, w, l):
    out = {}
    for n in ('s5_w_glu', 'lru_conv_w'):
        out[n] = g[n]
    out['ffn_conv_w'] = _ffn_deinterleave(g['ffn_conv_w'])
    out['ffn_conv_b'] = _ffn_deinterleave(g['ffn_conv_b'])[0]
    for n in ('lru_conv_b', 'lru_br', 'lru_bi', 'lru_lambda', 's5_b_glu', 'mix_norm_g',
              'ln1_g', 'ln1_b', 'ln2_g', 'ln2_b'):
        out[n] = g[n][0]
    out['lru_wr'] = _block_diag_extract(g['lru_wr'], LRU_W // HEAD)
    out['lru_wi'] = _block_diag_extract(g['lru_wi'], LRU_W // HEAD)
    out['s5_d'] = g['s5_d'].reshape(S5_G, S5_C)
    out['s5_c_re'] = jnp.swapaxes(_block_diag_take(g['s5_cc_re'], S5_G), 1, 2)
    out['s5_c_im'] = jnp.swapaxes(_block_diag_take(g['s5_cc_im'], S5_G), 1, 2)
    rep = lambda v: _s5_rep(v.reshape(S5_G, S5_P)) * (1.0 / S5_C)
    cts = (rep(g['s5_lam_re']), rep(g['s5_lam_im']),
           _block_diag_take(g['s5_bb_re'], S5_G).reshape(S5_W, S5_P),
           _block_diag_take(g['s5_bb_im'], S5_G).reshape(S5_W, S5_P))
    da_re, da_im, dls, dbt_re, dbt_im = _s5_disc_bwd(*w['s5_disc_in'], cts, "l%d_s5_disc_bwd" % l)
    out['s5_a_re'] = da_re.reshape(S5_G, S5_C, S5_P).sum(axis=1)
    out['s5_a_im'] = da_im.reshape(S5_G, S5_C, S5_P).sum(axis=1)
    out['s5_log_step'] = dls.reshape(S5_G, S5_C * S5_P).sum(axis=1)
    out['s5_b_re'] = jnp.swapaxes(dbt_re.reshape(S5_G, S5_C, S5_P), 1, 2)
    out['s5_b_im'] = jnp.swapaxes(dbt_im.reshape(S5_G, S5_C, S5_P), 1, 2)
    return out


def _run_step(x, target, get_layer, get_ffn, after_ffn_grads, after_out_grad, after_small_grads, after_in_grad):
    cos, sin = _rope_tables(x.shape[0])
    h, h_t = x, None
    ws, saved = [], []
    for l in range(DEPTH):
        p, dep = get_layer(l, h)
        ws.append(_prepare_layer(p, l))
        h, h_t, sv = _layer_fwd(h, h_t, ws[l], cos, sin, l, dep, get_ffn, target if l == DEPTH - 1 else None)
        saved.append(sv)
    dh, loss_vec = h, h_t
    dep = None
    for l in reversed(range(DEPTH)):
        dh1, g = _layer_bwd_ffn(dh, saved[l], ws[l], l, dep)
        dep = after_ffn_grads(l, g)
        dh = _layer_bwd_mix(dh1, saved[l], ws[l], cos, sin, l, dep, g, after_out_grad, after_small_grads,
                            after_in_grad)
        dep = None
    return loss_vec[0, 0], dh


def _local_step(x, target, layers):
    grads = [{} for _ in range(DEPTH)]

    def ffn(l, h1):
        return layers[l]['w_up_g'].astype(BF16), layers[l]['w_down'].astype(BF16), None, None

    def keep_ffn(l, g):
        grads[l].update(w_up_g=g['w_up_g'], w_down=g['w_down'])

    def keep_small(l, g):
        grads[l].update(g)

    loss, dx = _run_step(x, target, lambda l, h: (layers[l], None), ffn, keep_ffn,
                         lambda l, g: grads[l].update(w_out=g), keep_small, lambda l, g: grads[l].update(w_in=g))
    return loss, dx, grads


def kernel(x, w_in, lru_conv_w, lru_conv_b, lru_wr, lru_br, lru_wi, lru_bi, lru_lambda, s5_a_re, s5_a_im, s5_b_re, s5_b_im, s5_c_re, s5_c_im, s5_d, s5_log_step, s5_w_glu, s5_b_glu, mix_norm_g, w_out, ln1_g, ln1_b, w_up, ffn_conv_w, ffn_conv_b, w_down, ln2_g, ln2_b, loss_target, m_w_in, m_lru_conv_w, m_lru_conv_b, m_lru_wr, m_lru_br, m_lru_wi, m_lru_bi, m_lru_lambda, m_s5_a_re, m_s5_a_im, m_s5_b_re, m_s5_b_im, m_s5_c_re, m_s5_c_im, m_s5_d, m_s5_log_step, m_s5_w_glu, m_s5_b_glu, m_mix_norm_g, m_w_out, m_ln1_g, m_ln1_b, m_w_up, m_ffn_conv_w, m_ffn_conv_b, m_w_down, m_ln2_g, m_ln2_b, v_w_in, v_lru_conv_w, v_lru_conv_b, v_lru_wr, v_lru_br, v_lru_wi, v_lru_bi, v_lru_lambda, v_s5_a_re, v_s5_a_im, v_s5_b_re, v_s5_b_im, v_s5_c_re, v_s5_c_im, v_s5_d, v_s5_log_step, v_s5_w_glu, v_s5_b_glu, v_mix_norm_g, v_w_out, v_ln1_g, v_ln1_b, v_w_up, v_ffn_conv_w, v_ffn_conv_b, v_w_down, v_ln2_g, v_ln2_b):
    args = locals()
    wl = {n: args[n] for n in WEIGHTS}
    ml = {n: args['m_' + n] for n in WEIGHTS}
    vl = {n: args['v_' + n] for n in WEIGHTS}

    small_sizes = [int(wl[n].size) for n in SMALL_SHARDED]
    small_flat = _pad_to(jnp.concatenate([wl[n].reshape(-1) for n in SMALL_SHARDED]), 8 * 1024)
    small_all, w_in0 = _all_gather([small_flat.reshape(-1, 1024), wl['w_in'][0].astype(BF16)], "gather_first")
    small_all = small_all.reshape(N_DEV, -1)
    small_full, off = {}, 0
    for n, sz in zip(SMALL_SHARDED, small_sizes):
        small_full[n] = _gather_full(small_all[:, off:off + sz].reshape((N_DEV,) + wl[n].shape), SHARD_AXIS[n])
        off += sz
    def mixer_params(l, g_in, g_out):
        p = {n: wl[n][l] for n in REPLICATED}
        p.update({n: small_full[n][l] for n in SMALL_SHARDED})
        p['w_in'] = _gather_full(g_in, 1)
        if g_out is not None:
            p['w_out'] = g_out.reshape(D_MODEL, D_MODEL)
        return p

    mix_names, ffn_names = ('w_in', 'w_out'), ('w_up', 'w_down')
    shards = lambda names, l: [wl[n][l].astype(BF16) for n in names]
    gathers = {}
    gathers[0, 'ffn'], rest0_token = _exchange_start(shards(('w_out',) + ffn_names, 0), True,
                                                     "gather_rest_l0_start", dep=w_in0)

    def get_layer(l, h):
        if l == 0:
            return mixer_params(0, w_in0, None), rest0_token
        return mixer_params(1, *_exchange_wait(gathers[1, 'mix'], True, h, "gather_mix_l1_wait")), None

    def get_ffn(l, after):
        landed = _exchange_wait(gathers[l, 'ffn'], True, after, "gather_ffn_l%d_wait" % l)
        if l > 0:
            return landed[0], landed[1].reshape(D_FF, D_MODEL), None, None
        g_out, g_up, g_down = landed
        gathers[1, 'mix'], token = _exchange_start(shards(mix_names, 1), True, "gather_mix_l1_start", dep=g_up)
        gathers[1, 'ffn'], token = _exchange_start(shards(ffn_names, 1), True, "gather_ffn_l1_start", dep=token)
        return g_up, g_down.reshape(D_FF, D_MODEL), g_out.reshape(D_MODEL, D_MODEL), token

    scatters = {}

    def after_ffn_grads(l, g):
        send = [g['w_up_g'], g['w_down'].reshape(N_DEV, D_FF // N_DEV, D_MODEL)]
        scatters[l, 'ffn'], token = _exchange_start(send, False, "scatter_ffn_l%d_start" % l)
        return token

    def after_out_grad(l, g_out):
        send = [g_out.reshape(N_DEV, D_MODEL // N_DEV, D_MODEL)]
        scatters[l, 'out'], token = _exchange_start(send, False, "scatter_out_l%d_start" % l)
        return token

    def after_in_grad(l, g_in):
        send = _scatter_blocks(g_in, 1)
        if l == 0:
            send = send.astype(BF16)
        scatters[l, 'in'], token = _exchange_start([send], False, "scatter_in_l%d_start" % l)
        return token

    def after_small_grads(l, g):
        rep = [g[n][None] for n in REPLICATED]
        shd = [_scatter_blocks(g[n], SHARD_AXIS[n] - 1)[:, None] for n in SMALL_SHARDED]
        scatters[l, 'rep'], token = _exchange_start(rep, True, "gather_rep_grads_l%d_start" % l)
        scatters[l, 'small'], token = _exchange_start(shd, False, "scatter_small_l%d_start" % l, dep=token)
        return token

    loss_local, grad_x = _run_step(x[0], loss_target[0], get_layer, get_ffn, after_ffn_grads, after_out_grad,
                                   after_small_grads, after_in_grad)
    loss = lax.psum(loss_local, AXES)

    results = {}
    big_prev = {n: None for n in BIG}

    def finish_big(l, part, names, after):
        landed = _exchange_wait(scatters[l, part], False, after, "scatter_%s_l%d_wait" % (part, l))
        for n, ld in zip(names, landed):
            big_prev[n] = _adamw_sum(ld, wl[n], ml[n], vl[n], l, big_prev[n], "adamw_%s_l%d" % (n, l))

    for l, part, names in ((1, 'ffn', ffn_names), (1, 'out', ('w_out',)), (1, 'in', ('w_in',)),
                           (0, 'ffn', ffn_names), (0, 'out', ('w_out',))):
        finish_big(l, part, names, grad_x)

    kinds = ('grad', 'delta', 'm', 'v')
    landed = [dict(zip(REPLICATED + SMALL_SHARDED,
                       list(_exchange_wait(scatters[l, 'rep'], True, grad_x, "gather_rep_grads_l%d_wait" % l)) +
                       list(_exchange_wait(scatters[l, 'small'], False, grad_x, "scatter_small_l%d_wait" % l))))
              for l in range(DEPTH)]
    matrices = ['lru_wr', 'lru_wi', 's5_a_re', 's5_a_im', 's5_c_re', 's5_c_im', 's5_d']
    widest = ['s5_b_re', 's5_b_im']
    vectors = [n for n in REPLICATED + SMALL_SHARDED if n not in matrices + widest]
    last = None
    for tag, names in (("vectors", vectors), ("matrices", matrices), ("s5_b", widest)):
        res = _adamw_many([[landed[l][n] for n in names] for l in range(DEPTH)], [wl[n] for n in names],
                          [ml[n] for n in names], [vl[n] for n in names], "adamw_" + tag)
        for kind, arrs in zip(kinds, res):
            for n, a in zip(names, arrs):
                results[kind, n] = a
        last = res[0][0]
    finish_big(0, 'in', ('w_in',), last)
    for n in BIG:
        results['grad', n], results['delta', n], results['m', n], results['v', n] = big_prev[n]

    out = [loss, grad_x[None]]
    for kind in kinds:
        out.extend(results[kind, n] for n in WEIGHTS)
    return tuple(out)
```

```python
import math

import jax
import jax.numpy as jnp
from jax import lax
from jax.experimental import pallas as pl
from jax.experimental.pallas import tpu as pltpu

F32 = jnp.float32
BF16 = jnp.bfloat16

N_DEV = 8
DEPTH = 2
D_MODEL = 1024
ATTN_W = 384
LRU_W = 384
S5_W = 256
D_IN = 2176
D_FF = 3072
HEAD = 64
ATTN_BLK = 128
ATTN_TILE = 1024
DILATIONS = (1, 4, 16)
S5_G = 16
S5_P = 64
S5_C = 16
S5_STATES = S5_G * S5_P
LRU_C = 8.0
LRU_CONV = 4
FFN_CONV = 3
ROPE_THETA = 10000.0
ALPHA = (2 * DEPTH) ** 0.25
LN_EPS = 1e-5
RMS_EPS = 1e-6
ADAM_LR, ADAM_B1, ADAM_B2, ADAM_EPS, ADAM_WD, ADAM_STEP = 0.001, 0.9, 0.999, 1e-8, 0.01, 10

LANE = 128
SCAN_T = 256
S5_BLK = 256
FFN_CB = 2 * D_FF // N_DEV
VMEM_LIMIT = 56 * 1024 * 1024

AXES = ("x", "y", "c")

WEIGHTS = ['w_in', 'lru_conv_w', 'lru_conv_b', 'lru_wr', 'lru_br', 'lru_wi', 'lru_bi', 'lru_lambda',
           's5_a_re', 's5_a_im', 's5_b_re', 's5_b_im', 's5_c_re', 's5_c_im', 's5_d', 's5_log_step',
           's5_w_glu', 's5_b_glu', 'mix_norm_g', 'w_out', 'ln1_g', 'ln1_b', 'w_up', 'ffn_conv_w',
           'ffn_conv_b', 'w_down', 'ln2_g', 'ln2_b']
SHARD_AXIS = {'w_in': 2, 'lru_conv_w': 2, 's5_w_glu': 1, 'w_out': 1, 'w_up': 2, 'ffn_conv_w': 2, 'w_down': 1}
BIG = ['w_in', 'w_out', 'w_up', 'w_down']
SMALL_SHARDED = ['lru_conv_w', 'ffn_conv_w', 's5_w_glu']
REPLICATED = [n for n in WEIGHTS if n not in SHARD_AXIS]


def _cparams(sem=None):
    return pltpu.CompilerParams(dimension_semantics=sem, vmem_limit_bytes=VMEM_LIMIT)


def _ffn_dev(jb):
    return jb // 2 + (N_DEV // 2) * (jb % 2)


def _gelu(x):
    c = math.sqrt(2.0 / math.pi)
    t = jnp.tanh(c * (x + 0.044715 * (x * x * x)))
    return 0.5 * x * (1.0 + t)


def _gelu_grad(x):
    c = math.sqrt(2.0 / math.pi)
    x2 = x * x
    t = jnp.tanh(c * (x + 0.044715 * (x2 * x)))
    return 0.5 * (1.0 + t) + 0.5 * x * (1.0 - t * t) * (c * (1.0 + 3.0 * 0.044715 * x2))


def _sigmoid(x):
    return 1.0 / (1.0 + jnp.exp(-x))


def _log1p(x):
    u = 1.0 + x
    d = u - 1.0
    return jnp.where(d == 0.0, x, jnp.log(u) * (x / jnp.where(d == 0.0, 1.0, d)))


def _softplus(x):
    return jnp.maximum(x, 0.0) + _log1p(jnp.exp(-jnp.abs(x)))


def _expm1(x):
    return jnp.tanh(0.5 * x) * (jnp.exp(x) + 1.0)


def _dot(a, b):
    return jnp.dot(a.astype(BF16), b.astype(BF16), preferred_element_type=F32)


def _dot_nt(a, b):
    return lax.dot_general(a.astype(BF16), b.astype(BF16), (((1,), (1,)), ((), ())),
                           preferred_element_type=F32)


def _dot_tn(a, b):
    return lax.dot_general(a.astype(BF16), b.astype(BF16), (((0,), (0,)), ((), ())),
                           preferred_element_type=F32)


def _rows(shape):
    return lax.broadcasted_iota(jnp.int32, shape, 0)


def _shift_down_prev(x, s, prev8):
    if s == 0:
        return x
    t, l = x.shape
    r = pltpu.roll(x, s, axis=0)
    pr = pltpu.roll(prev8, s, axis=0)
    pad = jnp.concatenate([pr, jnp.zeros((t - 8, l), x.dtype)], axis=0)
    return jnp.where(_rows(x.shape) < s, pad, r)


def _shift_up_next(x, s, next8):
    if s == 0:
        return x
    t, l = x.shape
    r = pltpu.roll(x, t - s, axis=0)
    nx = pltpu.roll(next8, 8 - s, axis=0)
    pad = jnp.concatenate([jnp.zeros((t - 8, l), x.dtype), nx], axis=0)
    return jnp.where(_rows(x.shape) >= t - s, pad, r)


SUB = 8


def _tile_shift(x, s, fill, reverse):
    t = x.shape[0]
    pos = _rows(x.shape) & (SUB - 1)
    if reverse:
        return jnp.where(pos < SUB - s, pltpu.roll(x, t - s, axis=0), fill)
    return jnp.where(pos >= s, pltpu.roll(x, s, axis=0), fill)


def _scan_chunk(a, x, carry, reverse=False):
    s = 1
    while s < SUB:
        x = x + a * _tile_shift(x, s, 0.0, reverse)
        a = a * _tile_shift(a, s, 1.0, reverse)
        s *= 2
    nv = x.shape[0] // SUB
    out = [None] * nv
    for v in (reversed(range(nv)) if reverse else range(nv)):
        rows = slice(v * SUB, (v + 1) * SUB)
        out[v] = x[rows, :] + a[rows, :] * carry
        carry = out[v][0:1, :] if reverse else out[v][SUB - 1:SUB, :]
    return jnp.concatenate(out, axis=0)


def _cmul(ar, ai, br, bi):
    return ar * br - ai * bi, ar * bi + ai * br


def _cscan_consts(lr, li, reverse):
    pows = [(lr, li)]
    for _ in range(2):
        pows.append(_cmul(*pows[-1], *pows[-1]))
    rows = [(lr, li)]
    for _ in range(SUB - 1):
        rows.append(_cmul(*rows[-1], lr, li))
    if reverse:
        rows = rows[::-1]
    return pows, (jnp.concatenate([r for r, _ in rows], axis=0), jnp.concatenate([i for _, i in rows], axis=0))


def _cscan_chunk(xr, xi, consts, carry, reverse=False):
    pows, (p8r, p8i) = consts
    s = 1
    for pr, pi in pows:
        sr = _tile_shift(xr, s, 0.0, reverse)
        si = _tile_shift(xi, s, 0.0, reverse)
        xr, xi = xr + pr * sr - pi * si, xi + pr * si + pi * sr
        s *= 2
    nv = xr.shape[0] // SUB
    out_r, out_i = [None] * nv, [None] * nv
    cr, ci = carry
    for v in (reversed(range(nv)) if reverse else range(nv)):
        rows = slice(v * SUB, (v + 1) * SUB)
        out_r[v] = xr[rows, :] + p8r * cr - p8i * ci
        out_i[v] = xi[rows, :] + p8r * ci + p8i * cr
        edge = slice(0, 1) if reverse else slice(SUB - 1, SUB)
        cr, ci = out_r[v][edge, :], out_i[v][edge, :]
    return jnp.concatenate(out_r, axis=0), jnp.concatenate(out_i, axis=0)


def _dep_args(dep):
    return ([], []) if dep is None else ([pl.BlockSpec(memory_space=pl.ANY)], [dep])


def _mm_nt(a, w, tm, tn, name, add=None, add_scale=1.0, dep=None):
    m, k = a.shape
    n = w.shape[0]

    def body(a_ref, w_ref, *rest):
        o_ref = rest[-1]
        if add is None:
            o_ref[...] = _dot_nt(a_ref[...], w_ref[...])
        else:
            o_ref[...] = _dot_nt(a_ref[...], w_ref[...]) + add_scale * rest[0][...]

    in_specs = [pl.BlockSpec((tm, k), lambda j, i: (i, 0)), pl.BlockSpec((tn, k), lambda j, i: (j, 0))]
    args = [a, w]
    if add is not None:
        in_specs.append(pl.BlockSpec((tm, tn), lambda j, i: (i, j)))
        args.append(add)
    dep_specs, dep_ops = _dep_args(dep)
    return pl.pallas_call(
        body, out_shape=jax.ShapeDtypeStruct((m, n), F32), grid=(n // tn, m // tm),
        in_specs=in_specs + dep_specs, out_specs=pl.BlockSpec((tm, tn), lambda j, i: (i, j)), name=name,
        compiler_params=_cparams(("parallel", "parallel")))(*args, *dep_ops)


def _mm_dw(at, b, tm, tn, ts, name):
    m, s = at.shape
    n = b.shape[1]

    def body(a_ref, b_ref, o_ref):
        @pl.when(pl.program_id(2) == 0)
        def _():
            o_ref[...] = jnp.zeros_like(o_ref)
        o_ref[...] += _dot(a_ref[...], b_ref[...])

    return pl.pallas_call(
        body, out_shape=jax.ShapeDtypeStruct((m, n), F32), grid=(m // tm, n // tn, s // ts),
        in_specs=[pl.BlockSpec((tm, ts), lambda i, j, k: (i, k)), pl.BlockSpec((ts, tn), lambda i, j, k: (k, j))],
        out_specs=pl.BlockSpec((tm, tn), lambda i, j, k: (i, j)), name=name,
        compiler_params=_cparams(("parallel", "parallel", "arbitrary")))(at, b)


def _mm_up_dw(ht, dup, name):
    d, s = ht.shape

    def body(a_ref, b_ref, o_ref):
        o_ref[...] = _dot(a_ref[...], b_ref[...])

    return pl.pallas_call(
        body, out_shape=jax.ShapeDtypeStruct((N_DEV, d, FFN_CB), F32), grid=(N_DEV,),
        in_specs=[pl.BlockSpec((d, s), lambda j: (0, 0)), pl.BlockSpec((s, FFN_CB), lambda j: (0, j))],
        out_specs=pl.BlockSpec((None, d, FFN_CB), lambda j: (_ffn_dev(j), 0, 0)), name=name,
        compiler_params=_cparams(("parallel",)))(ht, dup)


def _layer_norm(r, g, b):
    mu = jnp.mean(r, axis=-1, keepdims=True)
    xc = r - mu
    var = jnp.mean(xc * xc, axis=-1, keepdims=True)
    return xc * lax.rsqrt(var + LN_EPS) * g + b


def _proj_ln(a, w, resid, g, bias, name, transposed=True, target=None):
    s, k = a.shape
    d = w.shape[1]
    tm = 512

    def body(a_ref, w_ref, x_ref, g_ref, bias_ref, *rest):
        r = ALPHA * x_ref[...] + _dot(a_ref[...], w_ref[...])
        h = _layer_norm(r, g_ref[...], bias_ref[...])
        if target is None:
            r_ref, h_ref = rest[0], rest[1]
            h_ref[...] = h
            if transposed:
                rest[2][...] = h.T.astype(BF16)
        else:
            t_ref, r_ref, dy_ref, l_ref = rest

            @pl.when(pl.program_id(0) == 0)
            def _():
                l_ref[...] = jnp.zeros_like(l_ref)
            e = h - t_ref[...]
            dy_ref[...] = e * (1.0 / d)
            part = 0.5 * jnp.sum(jnp.mean(e * e, axis=-1, keepdims=True), axis=0, keepdims=True)
            l_ref[...] += jnp.broadcast_to(part, l_ref.shape)
        r_ref[...] = r

    row = pl.BlockSpec((tm, d), lambda i: (i, 0))
    vec = pl.BlockSpec((1, d), lambda i: (0, 0))
    in_specs = [pl.BlockSpec((tm, k), lambda i: (i, 0)), pl.BlockSpec((k, d), lambda i: (0, 0)), row, vec, vec]
    args = [a, w, resid, g, bias]
    shapes = [jax.ShapeDtypeStruct((s, d), F32), jax.ShapeDtypeStruct((s, d), F32)]
    specs = [row, row]
    if target is not None:
        in_specs.append(row)
        args.append(target)
        shapes.append(jax.ShapeDtypeStruct((1, LANE), F32))
        specs.append(pl.BlockSpec((1, LANE), lambda i: (0, 0)))
    elif transposed:
        shapes.append(jax.ShapeDtypeStruct((d, s), BF16))
        specs.append(pl.BlockSpec((d, tm), lambda i: (0, i)))
    return pl.pallas_call(
        body, out_shape=tuple(shapes), grid=(s // tm,), in_specs=in_specs, out_specs=tuple(specs), name=name,
        compiler_params=_cparams(("arbitrary",) if target is not None else ("parallel",)))(*args)


def _layer_norm_bwd(r, dh, g):
    mu = jnp.mean(r, axis=-1, keepdims=True)
    xc = r - mu
    var = jnp.mean(xc * xc, axis=-1, keepdims=True)
    rstd = lax.rsqrt(var + LN_EPS)
    xh = xc * rstd
    dxh = dh * g
    m1 = jnp.mean(dxh, axis=-1, keepdims=True)
    m2 = jnp.mean(dxh * xh, axis=-1, keepdims=True)
    return (rstd * (dxh - m1 - xh * m2), jnp.sum(dh * xh, axis=0, keepdims=True),
            jnp.sum(dh, axis=0, keepdims=True))


def _ln_bwd(r, dh, g, name, dep=None):
    s, d = r.shape
    tm = 512

    def body(r_ref, dh_ref, g_ref, *rest):
        dr_ref, dg_ref, db_ref = rest[-3:]

        @pl.when(pl.program_id(0) == 0)
        def _():
            dg_ref[...] = jnp.zeros_like(dg_ref)
            db_ref[...] = jnp.zeros_like(db_ref)
        dr_ref[...], dg_rows, db_rows = _layer_norm_bwd(r_ref[...], dh_ref[...], g_ref[...])
        dg_ref[...] += dg_rows
        db_ref[...] += db_rows

    row = pl.BlockSpec((tm, d), lambda i: (i, 0))
    vec = pl.BlockSpec((1, d), lambda i: (0, 0))
    dep_specs, dep_ops = _dep_args(dep)
    return pl.pallas_call(
        body, out_shape=(jax.ShapeDtypeStruct((s, d), F32), jax.ShapeDtypeStruct((1, d), F32),
                         jax.ShapeDtypeStruct((1, d), F32)),
        grid=(s // tm,), in_specs=[row, row, vec] + dep_specs, out_specs=(row, vec, vec), name=name,
        compiler_params=_cparams(("arbitrary",)))(r, dh, g, *dep_ops)


def _rope_tables(s):
    half = HEAD // 2
    pos = jnp.arange(s, dtype=F32)
    inv = ROPE_THETA ** (-jnp.arange(half, dtype=F32) * 2.0 / HEAD)
    ang = pos[:, None] * inv[None, :]
    cos, sin = jnp.cos(ang), jnp.sin(ang)
    cos = jnp.concatenate([cos, cos, cos, cos], axis=1)
    sin = jnp.concatenate([-sin, sin, -sin, sin], axis=1)
    return cos, sin


def _rotate(x, cos, sin):
    lane = lax.broadcasted_iota(jnp.int32, x.shape, 1)
    partner = jnp.where((lane % HEAD) < HEAD // 2, pltpu.roll(x, LANE - HEAD // 2, axis=1),
                        pltpu.roll(x, HEAD // 2, axis=1))
    return x * cos + partner * sin


def _class_rows(c, d, tm):
    return pl.ds(c, tm // d, stride=d) if d > 1 else pl.ds(0, tm)


def _dilated_spec(tm, d, w):
    return pl.BlockSpec((tm // d, d * w), lambda i: (i, 0))


def _token_scratch(tm, w):
    return pltpu.VMEM((w // LANE, tm, LANE), F32)


def _to_tokens(src_ref, dst3, d, tm):
    nj = dst3.shape[0]
    for cls in range(d):
        for j in range(nj):
            col = (cls * nj + j) * LANE
            dst3.at[j][_class_rows(cls, d, tm), :] = src_ref[:, col:col + LANE]


def _to_dilated(src3, dst_ref, d, tm):
    nj = src3.shape[0]
    for cls in range(d):
        for j in range(nj):
            col = (cls * nj + j) * LANE
            dst_ref[:, col:col + LANE] = src3.at[j][_class_rows(cls, d, tm), :].astype(dst_ref.dtype)


def _token_value(src3):
    return jnp.concatenate([src3[j] for j in range(src3.shape[0])], axis=1)


def _proj_rope(h, w_in, cos, sin, name, dep=None, transposed=False):
    s, d_model = h.shape
    tm = 512
    w = 3 * ATTN_W
    nj = w // LANE

    def body(h_ref, w_ref, c_ref, s_ref, *rest):
        rot = rest[-1]
        if transposed:
            p_ref, o_refs, ht_ref = rest[-6], rest[-5:-2], rest[-2]
            ht_ref[...] = h_ref[...].T.astype(BF16)
        else:
            p_ref, o_refs = rest[-5], rest[-4:-1]
        y = _dot(h_ref[...], w_ref[...])
        p_ref[...] = y
        c, sn = c_ref[...], s_ref[...]
        for j in range(nj):
            x = y[:, j * LANE:(j + 1) * LANE]
            rot[j] = _rotate(x, c, sn) if j < 2 * ATTN_W // LANE else x
        for d, o_ref in zip(DILATIONS, o_refs):
            _to_dilated(rot, o_ref, d, tm)

    tab = pl.BlockSpec((tm, LANE), lambda i: (i, 0))
    dep_specs, dep_ops = _dep_args(dep)
    shapes = [jax.ShapeDtypeStruct((s, D_IN), F32), *[jax.ShapeDtypeStruct((s // d, d * w), BF16) for d in DILATIONS]]
    specs = [pl.BlockSpec((tm, D_IN), lambda i: (i, 0)), *[_dilated_spec(tm, d, w) for d in DILATIONS]]
    if transposed:
        shapes.append(jax.ShapeDtypeStruct((d_model, s), BF16))
        specs.append(pl.BlockSpec((d_model, tm), lambda i: (0, i)))
    res = pl.pallas_call(
        body, out_shape=tuple(shapes), grid=(s // tm,),
        in_specs=[pl.BlockSpec((tm, d_model), lambda i: (i, 0)), pl.BlockSpec((d_model, D_IN), lambda i: (0, 0)),
                  tab, tab] + dep_specs,
        out_specs=tuple(specs), scratch_shapes=[_token_scratch(tm, w)], name=name,
        compiler_params=_cparams(("parallel",)))(h, w_in, cos, sin, *dep_ops)
    return res[0], res[1:4], (res[4] if transposed else None)


def _dproj_assemble(dqkv_list, dxr, dgate, du, cos, sin, name):
    s = dxr.shape[0]
    tm = 512
    nq = 3 * ATTN_W // LANE

    def body(*refs):
        br = refs[:9]
        dxr_ref, dg_ref, du_ref, c_ref, s_ref, o_ref = refs[9:15]
        tok = refs[15:]
        c, sn = c_ref[...], -s_ref[...]
        for part in range(3):
            for b, d in enumerate(DILATIONS[1:], start=1):
                _to_tokens(br[3 * b + part], tok[2 * part + b - 1], d, tm)
        for j in range(nq):
            part, jj = divmod(j, ATTN_W // LANE)
            x = br[part][:, jj * LANE:(jj + 1) * LANE] + tok[2 * part][jj] + tok[2 * part + 1][jj]
            if part < 2:
                x = _rotate(x, c, sn)
            o_ref[:, j * LANE:(j + 1) * LANE] = x.astype(BF16)
        o_ref[:, 3 * ATTN_W:3 * ATTN_W + LRU_W] = dxr_ref[...].astype(BF16)
        o_ref[:, 3 * ATTN_W + LRU_W:3 * ATTN_W + 2 * LRU_W] = dg_ref[...].astype(BF16)
        o_ref[:, 3 * ATTN_W + 2 * LRU_W:] = du_ref[...].astype(BF16)

    a_spec = pl.BlockSpec((tm, ATTN_W), lambda i: (i, 0))
    tab = pl.BlockSpec((tm, LANE), lambda i: (i, 0))
    ordered = [dqkv_list[b][p] for b in range(3) for p in range(3)]
    d_specs = [_dilated_spec(tm, d, ATTN_W) for d in DILATIONS for _ in range(3)]
    return pl.pallas_call(
        body, out_shape=jax.ShapeDtypeStruct((s, D_IN), BF16), grid=(s // tm,),
        in_specs=d_specs + [a_spec, a_spec, pl.BlockSpec((tm, S5_W), lambda i: (i, 0)), tab, tab],
        out_specs=pl.BlockSpec((tm, D_IN), lambda i: (i, 0)),
        scratch_shapes=[_token_scratch(tm, ATTN_W)] * 6, name=name,
        compiler_params=_cparams(("parallel",)))(*ordered, dxr, dgate, du, cos, sin)


def _attn_tiles(s, d):
    m = s // d
    tq = min(m, ATTN_TILE)
    return m, tq, tq // ATTN_BLK


def _band_mask(qb):
    qi = lax.broadcasted_iota(jnp.int32, (ATTN_BLK, 2 * ATTN_BLK), 0)
    ki = lax.broadcasted_iota(jnp.int32, (ATTN_BLK, 2 * ATTN_BLK), 1)
    dist = qi + ATTN_BLK - ki
    return (dist >= 0) & (dist <= ATTN_BLK) & ((ki >= ATTN_BLK) | (qb > 0))


def _attn_fwd(qv, d, name):
    m = qv.shape[0]
    w3 = 3 * ATTN_W
    _, tq, n = _attn_tiles(m * d, d)
    scale = HEAD ** -0.5

    def body(x_ref, p_ref, o_ref, l_ref):
        b = pl.program_id(1)

        def block(i, first):
            r0 = 0 if first else pl.multiple_of(i * ATTN_BLK, ATTN_BLK)
            rows = pl.ds(r0, ATTN_BLK)
            valid = _band_mask(b * n + i)
            if not first:
                krows = pl.ds(pl.multiple_of(i * ATTN_BLK - ATTN_BLK, ATTN_BLK), 2 * ATTN_BLK)
            low = lax.broadcasted_iota(jnp.int32, (1, LANE), 1) < HEAD
            for hp in range(ATTN_W // LANE):
                qs, ks, vs = (slice(part * ATTN_W + hp * LANE, part * ATTN_W + (hp + 1) * LANE) for part in range(3))
                q2 = x_ref[rows, qs]
                if first:
                    k2 = jnp.concatenate([p_ref[:, ks], x_ref[0:ATTN_BLK, ks]], axis=0)
                    v2 = jnp.concatenate([p_ref[:, vs], x_ref[0:ATTN_BLK, vs]], axis=0)
                else:
                    k2 = x_ref[krows, ks]
                    v2 = x_ref[krows, vs]
                outs, lses = [], []
                for mask in (low, ~low):
                    q = jnp.where(mask, q2, jnp.zeros_like(q2))
                    sc = jnp.where(valid, _dot_nt(q, k2) * scale, -1e30)
                    mx = jnp.max(sc, axis=-1, keepdims=True)
                    p = jnp.exp(sc - mx)
                    l = jnp.sum(p, axis=-1, keepdims=True)
                    outs.append(_dot(p, v2) / l)
                    lses.append(mx + jnp.log(l))
                o_ref[rows, hp * LANE:(hp + 1) * LANE] = jnp.where(low, outs[0], outs[1])
                l_ref[rows, hp * LANE:(hp + 1) * LANE] = jnp.where(low, lses[0], lses[1])

        block(0, True)
        if n > 1:
            def loop(i, carry):
                block(i, False)
                return carry
            lax.fori_loop(1, n, loop, 0)

    shp = jax.ShapeDtypeStruct((m, d * ATTN_W), F32)
    ospec = pl.BlockSpec((tq, ATTN_W), lambda c, b: (b, c))
    out, lse = pl.pallas_call(
        body, out_shape=(shp, shp), grid=(d, m // tq),
        in_specs=[pl.BlockSpec((tq, w3), lambda c, b: (b, c)),
                  pl.BlockSpec((ATTN_BLK, w3), lambda c, b: (jnp.maximum(b * n - 1, 0), c))],
        out_specs=(ospec, ospec), name=name,
        compiler_params=_cparams(("parallel", "parallel")))(qv, qv)
    return out, lse


def _attn_bwd(qv, ov, dov, lv, d, name, dep=None):
    m = qv.shape[0]
    w3 = 3 * ATTN_W
    _, tq, n = _attn_tiles(m * d, d)
    nb = m // ATTN_BLK
    scale = HEAD ** -0.5

    def body(x_ref, p_ref, nx_ref, o_ref, do_ref, l_ref, on_ref, don_ref, ln_ref, *rest):
        dq_ref, dk_ref, dv_ref = rest[-3:]
        b = pl.program_id(1)
        dk_ref[...] = jnp.zeros_like(dk_ref)
        dv_ref[...] = jnp.zeros_like(dv_ref)

        low = lax.broadcasted_iota(jnp.int32, (1, LANE), 1) < HEAD

        def pair_grads(q2, k2, v2, o2, do2, l2, valid):
            dq, dk, dv = [], 0.0, 0.0
            for mask, lse in ((low, l2[:, 0:1]), (~low, l2[:, HEAD:HEAD + 1])):
                q = jnp.where(mask, q2, jnp.zeros_like(q2))
                do = jnp.where(mask, do2, 0.0)
                sc = jnp.where(valid, _dot_nt(q, k2) * scale, -1e30)
                p = jnp.exp(sc - lse)
                delta = jnp.sum(do * o2, axis=-1, keepdims=True)
                ds = p * (_dot_nt(do, v2) - delta) * scale
                dq.append(_dot(ds, k2))
                dk = dk + _dot_tn(ds, q)
                dv = dv + _dot_tn(p, do)
            return jnp.where(low, dq[0], dq[1]), dk, dv

        def cols(hp):
            return [slice(part * ATTN_W + hp * LANE, part * ATTN_W + (hp + 1) * LANE) for part in range(3)]

        def block(i, first):
            r0 = 0 if first else pl.multiple_of(i * ATTN_BLK, ATTN_BLK)
            rows = pl.ds(r0, ATTN_BLK)
            valid = _band_mask(b * n + i)
            if not first:
                krows = pl.ds(pl.multiple_of(i * ATTN_BLK - ATTN_BLK, ATTN_BLK), 2 * ATTN_BLK)
            for hp in range(ATTN_W // LANE):
                qs, ks, vs = cols(hp)
                if first:
                    k2 = jnp.concatenate([p_ref[:, ks], x_ref[0:ATTN_BLK, ks]], axis=0)
                    v2 = jnp.concatenate([p_ref[:, vs], x_ref[0:ATTN_BLK, vs]], axis=0)
                else:
                    k2 = x_ref[krows, ks]
                    v2 = x_ref[krows, vs]
                dq, dk, dv = pair_grads(x_ref[rows, qs], k2, v2, o_ref[rows, qs], do_ref[rows, qs],
                                        l_ref[rows, qs], valid)
                dq_ref[rows, qs] = dq
                if first:
                    dk_ref[0:ATTN_BLK, qs] += dk[ATTN_BLK:, :]
                    dv_ref[0:ATTN_BLK, qs] += dv[ATTN_BLK:, :]
                else:
                    dk_ref[krows, qs] += dk
                    dv_ref[krows, qs] += dv

        block(0, True)
        if n > 1:
            def loop(i, carry):
                block(i, False)
                return carry
            lax.fori_loop(1, n, loop, 0)

        last = slice((n - 1) * ATTN_BLK, n * ATTN_BLK)
        qi = lax.broadcasted_iota(jnp.int32, (ATTN_BLK, ATTN_BLK), 0)
        ki = lax.broadcasted_iota(jnp.int32, (ATTN_BLK, ATTN_BLK), 1)
        valid_next = (qi <= ki) & ((b + 1) * n < nb)
        for hp in range(ATTN_W // LANE):
            qs, ks, vs = cols(hp)
            _, dk, dv = pair_grads(nx_ref[:, qs], x_ref[last, ks], x_ref[last, vs], on_ref[:, qs], don_ref[:, qs],
                                   ln_ref[:, qs], valid_next)
            dk_ref[last, qs] += dk
            dv_ref[last, qs] += dv

    nxt = lambda b: jnp.minimum((b + 1) * n, nb - 1)
    xs = pl.BlockSpec((tq, w3), lambda c, b: (b, c))
    xp = pl.BlockSpec((ATTN_BLK, w3), lambda c, b: (jnp.maximum(b * n - 1, 0), c))
    xn = pl.BlockSpec((ATTN_BLK, w3), lambda c, b: (nxt(b), c))
    a = pl.BlockSpec((tq, ATTN_W), lambda c, b: (b, c))
    an = pl.BlockSpec((ATTN_BLK, ATTN_W), lambda c, b: (nxt(b), c))
    shp = jax.ShapeDtypeStruct((m, d * ATTN_W), F32)
    dep_specs, dep_ops = _dep_args(dep)
    return pl.pallas_call(
        body, out_shape=(shp, shp, shp), grid=(d, m // tq),
        in_specs=[xs, xp, xn, a, a, a, an, an, an] + dep_specs, out_specs=(a, a, a), name=name,
        compiler_params=_cparams(("parallel", "parallel")))(qv, qv, qv, ov, dov, lv, ov, dov, lv, *dep_ops)


def _rms(x, g):
    ms = jnp.mean(x * x, axis=-1, keepdims=True)
    return x * lax.rsqrt(ms + RMS_EPS) * g


def _rms_bwd(x, g, dy):
    ms = jnp.mean(x * x, axis=-1, keepdims=True)
    r = lax.rsqrt(ms + RMS_EPS)
    dyg = dy * g
    dx = r * dyg - x * (r * r * r) * jnp.mean(x * dyg, axis=-1, keepdims=True)
    return dx, dy * x * r


def _mix_fwd(outs, lses, lru, s5, g, h_in, w_out, ln_g, ln_b, name):
    s = lru.shape[0]
    tm = 256

    def body(o1, o2, o3, l1, l2, l3, lru_ref, s5_ref, g_ref, x_ref, w_ref, lg_ref, lb_ref,
             mixed_t_ref, r_ref, h_ref, ht_ref, ov1, ov2, ov3, lv1, lv2, lv3, so2, so3, sl2, sl3):
        for d, src, dst in ((DILATIONS[1], o2, so2), (DILATIONS[2], o3, so3),
                            (DILATIONS[1], l2, sl2), (DILATIONS[2], l3, sl3)):
            _to_tokens(src, dst, d, tm)
        a1, a2, a3 = l1[...], _token_value(sl2), _token_value(sl3)
        mx = jnp.maximum(jnp.maximum(a1, a2), a3)
        e1, e2, e3 = jnp.exp(a1 - mx), jnp.exp(a2 - mx), jnp.exp(a3 - mx)
        den = e1 + e2 + e3
        o = (e1 * o1[...] + e2 * _token_value(so2) + e3 * _token_value(so3)) / den
        lse = mx + jnp.log(den)
        ov1[...] = o
        lv1[...] = lse
        for j in range(ATTN_W // LANE):
            so2[j] = o[:, j * LANE:(j + 1) * LANE]
            sl2[j] = lse[:, j * LANE:(j + 1) * LANE]
        for d, o_dst, l_dst in ((DILATIONS[1], ov2, lv2), (DILATIONS[2], ov3, lv3)):
            _to_dilated(so2, o_dst, d, tm)
            _to_dilated(sl2, l_dst, d, tm)
        gg = g_ref[...]
        mixed = jnp.concatenate([_rms(o, gg[:, :ATTN_W]),
                                 _rms(lru_ref[...], gg[:, ATTN_W:ATTN_W + LRU_W]),
                                 _rms(s5_ref[...], gg[:, ATTN_W + LRU_W:])], axis=1)
        mixed_t_ref[...] = mixed.T.astype(BF16)
        r = ALPHA * x_ref[...] + _dot(mixed, w_ref[...])
        h = _layer_norm(r, lg_ref[...], lb_ref[...])
        r_ref[...] = r
        h_ref[...] = h
        ht_ref[...] = h.T.astype(BF16)

    a = pl.BlockSpec((tm, ATTN_W), lambda i: (i, 0))
    s5s = pl.BlockSpec((tm, S5_W), lambda i: (i, 0))
    full = pl.BlockSpec((tm, D_MODEL), lambda i: (i, 0))
    vec = pl.BlockSpec((1, D_MODEL), lambda i: (0, 0))
    dil = [_dilated_spec(tm, d, ATTN_W) for d in DILATIONS]
    dshape = [jax.ShapeDtypeStruct((s // d, d * ATTN_W), F32) for d in DILATIONS]
    tshape = jax.ShapeDtypeStruct((D_MODEL, s), BF16)
    fshape = jax.ShapeDtypeStruct((s, D_MODEL), F32)
    tspec = pl.BlockSpec((D_MODEL, tm), lambda i: (0, i))
    res = pl.pallas_call(
        body, out_shape=(tshape, fshape, fshape, tshape, *dshape, *dshape),
        grid=(s // tm,),
        in_specs=dil + dil + [a, s5s, vec, full, pl.BlockSpec((D_MODEL, D_MODEL), lambda i: (0, 0)), vec, vec],
        out_specs=(tspec, full, full, tspec, *dil, *dil),
        scratch_shapes=[_token_scratch(tm, ATTN_W)] * 4, name=name,
        compiler_params=_cparams(("parallel",)))(*outs, *lses, lru, s5, g, h_in, w_out, ln_g, ln_b)
    return res[0], res[1], res[2], res[3], res[4:7], res[7:10]


def _mix_bwd(r, dh, ln_g, w_out, o, lru, s5, g, name, dep=None):
    s = lru.shape[0]
    tm = 256

    def body(r_ref, dh_ref, lg_ref, w_ref, o_ref, lru_ref, s5_ref, g_ref, *rest):
        dr_ref, dlg_ref, dlb_ref, do_ref, do2_ref, do3_ref, dlru_ref, ds5_ref, dg_ref, stage = rest[-10:]

        @pl.when(pl.program_id(0) == 0)
        def _():
            dg_ref[...] = jnp.zeros_like(dg_ref)
            dlg_ref[...] = jnp.zeros_like(dlg_ref)
            dlb_ref[...] = jnp.zeros_like(dlb_ref)
        gg = g_ref[...]
        dr, dlg_rows, dlb_rows = _layer_norm_bwd(r_ref[...], dh_ref[...], lg_ref[...])
        dr_ref[...] = dr
        dlg_ref[...] += dlg_rows
        dlb_ref[...] += dlb_rows
        dm = _dot_nt(dr, w_ref[...])
        dx, dgr = _rms_bwd(o_ref[...], gg[:, :ATTN_W], dm[:, :ATTN_W])
        do_ref[...] = dx
        for j in range(ATTN_W // LANE):
            stage[j] = dx[:, j * LANE:(j + 1) * LANE]
        _to_dilated(stage, do2_ref, DILATIONS[1], tm)
        _to_dilated(stage, do3_ref, DILATIONS[2], tm)
        dg_ref[:, :ATTN_W] += jnp.sum(dgr, axis=0, keepdims=True)
        dx, dgr = _rms_bwd(lru_ref[...], gg[:, ATTN_W:ATTN_W + LRU_W], dm[:, ATTN_W:ATTN_W + LRU_W])
        dlru_ref[...] = dx
        dg_ref[:, ATTN_W:ATTN_W + LRU_W] += jnp.sum(dgr, axis=0, keepdims=True)
        dx, dgr = _rms_bwd(s5_ref[...], gg[:, ATTN_W + LRU_W:], dm[:, ATTN_W + LRU_W:])
        ds5_ref[...] = dx
        dg_ref[:, ATTN_W + LRU_W:] += jnp.sum(dgr, axis=0, keepdims=True)

    a = pl.BlockSpec((tm, ATTN_W), lambda i: (i, 0))
    s5s = pl.BlockSpec((tm, S5_W), lambda i: (i, 0))
    full = pl.BlockSpec((tm, D_MODEL), lambda i: (i, 0))
    vec = pl.BlockSpec((1, D_MODEL), lambda i: (0, 0))
    dil = [_dilated_spec(tm, d, ATTN_W) for d in DILATIONS]
    dshape = [jax.ShapeDtypeStruct((s // d, d * ATTN_W), F32) for d in DILATIONS]
    dep_specs, dep_ops = _dep_args(dep)
    vshape = jax.ShapeDtypeStruct((1, D_MODEL), F32)
    res = pl.pallas_call(
        body, out_shape=(jax.ShapeDtypeStruct((s, D_MODEL), F32), vshape, vshape, *dshape,
                         jax.ShapeDtypeStruct((s, LRU_W), F32), jax.ShapeDtypeStruct((s, S5_W), F32), vshape),
        grid=(s // tm,),
        in_specs=[full, full, vec, pl.BlockSpec((D_MODEL, D_MODEL), lambda i: (0, 0)), a, a, s5s, vec] + dep_specs,
        out_specs=(full, vec, vec, *dil, a, s5s, vec), scratch_shapes=[_token_scratch(tm, ATTN_W)], name=name,
        compiler_params=_cparams(("arbitrary",)))(r, dh, ln_g, w_out, o, lru, s5, g, *dep_ops)
    return res[0], res[1], res[2], res[3:6], res[6], res[7], res[8]


def _lru_gate_math(xc, pre_r, pre_i, lam):
    r = _sigmoid(pre_r)
    i = _sigmoid(pre_i)
    log_a = -LRU_C * r * _softplus(-lam)
    a = jnp.exp(log_a)
    u = jnp.sqrt(-_expm1(2.0 * log_a)) * (i * xc)
    return a, u


def _lru_conv(x, prev8, cw, cb):
    y = cb + cw[LRU_CONV - 1:LRU_CONV, :] * x
    for k in range(LRU_CONV - 1):
        y = y + cw[k:k + 1, :] * _shift_down_prev(x, LRU_CONV - 1 - k, prev8)
    return y


def _lru_specs(s):
    xo = 3 * ATTN_W // LANE
    go = xo + LRU_W // LANE
    xr = pl.BlockSpec((s, LANE), lambda j: (0, xo + j))
    gt = pl.BlockSpec((s, LANE), lambda j: (0, go + j))
    cw = pl.BlockSpec((LRU_CONV, LANE), lambda j: (0, j))
    vec = pl.BlockSpec((1, LANE), lambda j: (0, j))
    wbd = pl.BlockSpec((LANE, LANE), lambda j: (j, j))
    col = pl.BlockSpec((s, LANE), lambda j: (0, j))
    return xr, gt, cw, vec, wbd, col


def _lru_fwd(proj, cw, cb, wr, br, wi, bi, lam, name):
    s = proj.shape[0]
    t = SCAN_T

    def body(xr_ref, gt_ref, cw_ref, cb_ref, wr_ref, br_ref, wi_ref, bi_ref, lam_ref, o_ref, xc_ref, a_ref, h_ref):
        cwv, cbv, lamv = cw_ref[...], cb_ref[...], lam_ref[...]
        wrv, wiv, brv, biv = wr_ref[...], wi_ref[...], br_ref[...], bi_ref[...]

        def chunk(c, carry):
            h_c, prev8 = carry
            rows = pl.ds(pl.multiple_of(c * t, t), t)
            x = xr_ref[rows, :]
            xc = _lru_conv(x, prev8, cwv, cbv)
            a, u = _lru_gate_math(xc, _dot(xc, wrv) + brv, _dot(xc, wiv) + biv, lamv)
            h = _scan_chunk(a, u, h_c)
            xc_ref[rows, :] = xc
            a_ref[rows, :] = a
            h_ref[rows, :] = h
            o_ref[rows, :] = h * _gelu(gt_ref[rows, :])
            return h[t - 1:t, :], x[t - 8:t, :]

        lax.fori_loop(0, s // t, chunk, (jnp.zeros((1, LANE), F32), jnp.zeros((8, LANE), F32)))

    xr, gt, cws, vec, wbd, col = _lru_specs(s)
    shp = jax.ShapeDtypeStruct((s, LRU_W), F32)
    return pl.pallas_call(
        body, out_shape=(shp,) * 4, grid=(LRU_W // LANE,),
        in_specs=[xr, gt, cws, vec, wbd, vec, wbd, vec, vec], out_specs=(col,) * 4, name=name,
        compiler_params=_cparams(("parallel",)))(proj, proj, cw, cb, wr, br, wi, bi, lam)


def _lru_bwd(proj, dout, xc_all, a_all, h_all, cw, cb, wr, br, wi, bi, lam, name):
    s = proj.shape[0]
    t = SCAN_T
    nc = s // t

    def body(xr_ref, gt_ref, do_ref, xc_s, a_s, h_s, cw_ref, cb_ref, wr_ref, br_ref, wi_ref, bi_ref, lam_ref,
             dxr_ref, dgt_ref, dcw_ref, dcb_ref, dwr_ref, dbr_ref, dwi_ref, dbi_ref, dlam_ref):
        cwv, cbv, lamv = cw_ref[...], cb_ref[...], lam_ref[...]
        wrv, wiv, brv, biv = wr_ref[...], wi_ref[...], br_ref[...], bi_ref[...]
        z1 = jnp.zeros((1, LANE), F32)
        zw = jnp.zeros((LANE, LANE), F32)

        def bchunk(ci, carry):
            g_next, a_next, dxc_next8, dcw, dcb, dwr, dbr, dwi, dbi, dlam = carry
            c = nc - 1 - ci
            t0 = pl.multiple_of(c * t, t)
            rows = pl.ds(t0, t)
            before = pl.ds(pl.multiple_of(jnp.maximum(t0 - 8, 0), 8), 8)
            has_prev = (c > 0).astype(F32)
            x, gt, do = xr_ref[rows, :], gt_ref[rows, :], do_ref[rows, :]
            xc, a, h = xc_s[rows, :], a_s[rows, :], h_s[rows, :]
            prev8_h = h_s[before, :] * has_prev
            dgt_ref[rows, :] = do * h * _gelu_grad(gt)
            dh = do * _gelu(gt)
            a_plus = _shift_up_next(a, 1, jnp.broadcast_to(a_next, (8, LANE)))
            g = _scan_chunk(a_plus, dh, g_next, reverse=True)
            da = g * _shift_down_prev(h, 1, prev8_h)
            pre_r = _dot(xc, wrv) + brv
            pre_i = _dot(xc, wiv) + biv
            _, vjp = jax.vjp(_lru_gate_math, xc, pre_r, pre_i, lamv)
            dxc, dpre_r, dpre_i, dlam_c = vjp((da, g))
            dxc = dxc + _dot_nt(dpre_r, wrv) + _dot_nt(dpre_i, wiv)
            dx = cwv[LRU_CONV - 1:LRU_CONV, :] * dxc
            dcw_rows = [None] * LRU_CONV
            dcw_rows[LRU_CONV - 1] = jnp.sum(dxc * x, axis=0, keepdims=True)
            for k in range(LRU_CONV - 1):
                dxc_ahead = _shift_up_next(dxc, LRU_CONV - 1 - k, dxc_next8)
                dx = dx + cwv[k:k + 1, :] * dxc_ahead
                dcw_rows[k] = jnp.sum(dxc_ahead * x, axis=0, keepdims=True)
            dxr_ref[rows, :] = dx
            return (g[0:1, :], a[0:1, :], dxc[0:8, :],
                    dcw + jnp.concatenate(dcw_rows, axis=0),
                    dcb + jnp.sum(dxc, axis=0, keepdims=True),
                    dwr + _dot_tn(xc, dpre_r), dbr + jnp.sum(dpre_r, axis=0, keepdims=True),
                    dwi + _dot_tn(xc, dpre_i), dbi + jnp.sum(dpre_i, axis=0, keepdims=True),
                    dlam + dlam_c)

        init = (z1, z1, jnp.zeros((8, LANE), F32), jnp.zeros((LRU_CONV, LANE), F32), z1, zw, z1, zw, z1, z1)
        res = lax.fori_loop(0, nc, bchunk, init)
        dcw_ref[...] = res[3]
        dcb_ref[...] = res[4]
        dwr_ref[...] = res[5]
        dbr_ref[...] = res[6]
        dwi_ref[...] = res[7]
        dbi_ref[...] = res[8]
        dlam_ref[...] = res[9]

    xr, gt, cws, vec, wbd, col = _lru_specs(s)
    vshape = jax.ShapeDtypeStruct((1, LRU_W), F32)
    wshape = jax.ShapeDtypeStruct((LRU_W, LRU_W), F32)
    return pl.pallas_call(
        body,
        out_shape=(jax.ShapeDtypeStruct((s, LRU_W), F32), jax.ShapeDtypeStruct((s, LRU_W), F32),
                   jax.ShapeDtypeStruct((LRU_CONV, LRU_W), F32), vshape, wshape, vshape, wshape, vshape, vshape),
        grid=(LRU_W // LANE,),
        in_specs=[xr, gt, col, col, col, col, cws, vec, wbd, vec, wbd, vec, vec],
        out_specs=(col, col, cws, vec, wbd, vec, wbd, vec, vec), name=name,
        compiler_params=_cparams(("parallel",)))(proj, proj, dout, xc_all, a_all, h_all, cw, cb, wr, br, wi, bi,
                                                 lam)


def _s5_disc_math(a_re, a_im, log_step, bt_re, bt_im):
    step = jnp.exp(log_step)
    dt_re, dt_im = step * a_re, step * a_im
    mag = jnp.exp(dt_re)
    ab_re, ab_im = mag * jnp.cos(dt_im), mag * jnp.sin(dt_im)
    z_re, z_im = ab_re - 1.0, ab_im
    den = a_re * a_re + a_im * a_im
    f_re = (z_re * a_re + z_im * a_im) / den
    f_im = (z_im * a_re - z_re * a_im) / den
    bb_re = f_re * bt_re - f_im * bt_im
    bb_im = f_re * bt_im + f_im * bt_re
    return ab_re, ab_im, bb_re, bb_im


def _s5_disc_fwd(a_re, a_im, log_step, bt_re, bt_im, name):
    def body(ar, ai, ls, br, bi, o1, o2, o3, o4):
        r = _s5_disc_math(ar[...], ai[...], ls[...], br[...], bi[...])
        o1[...], o2[...], o3[...], o4[...] = r

    shp = jax.ShapeDtypeStruct(a_re.shape, F32)
    return pl.pallas_call(body, out_shape=(shp,) * 4, name=name)(a_re, a_im, log_step, bt_re, bt_im)


def _s5_disc_bwd(a_re, a_im, log_step, bt_re, bt_im, cts, name):
    def body(ar, ai, ls, br, bi, c1, c2, c3, c4, o1, o2, o3, o4, o5):
        _, vjp = jax.vjp(_s5_disc_math, ar[...], ai[...], ls[...], br[...], bi[...])
        r = vjp((c1[...], c2[...], c3[...], c4[...]))
        o1[...], o2[...], o3[...], o4[...], o5[...] = r

    shp = jax.ShapeDtypeStruct(a_re.shape, F32)
    return pl.pallas_call(body, out_shape=(shp,) * 5, name=name)(a_re, a_im, log_step, bt_re, bt_im, *cts)


def _s5_u_specs(s):
    uo = (3 * ATTN_W + 2 * LRU_W) // LANE
    return (pl.BlockSpec((s, LANE), lambda j: (0, uo)), pl.BlockSpec((s, LANE), lambda j: (0, uo + 1)))


def _s5_scan_fwd(proj, b_re, b_im, lam_re, lam_im, c_re, c_im, name):
    s = proj.shape[0]
    t = SCAN_T

    def body(u0_ref, u1_ref, bre_ref, bim_ref, lre_ref, lim_ref, cre_ref, cim_ref, xre_ref, xim_ref, y_ref):
        @pl.when(pl.program_id(0) == 0)
        def _():
            y_ref[...] = jnp.zeros_like(y_ref)
        lr, li = lre_ref[...], lim_ref[...]
        consts = _cscan_consts(lr, li, False)
        bre, bim, cre, cim = bre_ref[...], bim_ref[...], cre_ref[...], cim_ref[...]

        def chunk(c, carry):
            cr, ci = carry
            rows = pl.ds(pl.multiple_of(c * t, t), t)
            u = jnp.concatenate([u0_ref[rows, :], u1_ref[rows, :]], axis=1).astype(BF16)
            xr, xi = _cscan_chunk(_dot(u, bre), _dot(u, bim), consts, (cr, ci))
            xre_ref[rows, :] = xr
            xim_ref[rows, :] = xi
            y_ref[rows, :] += _dot(xr, cre) - _dot(xi, cim)
            return xr[t - 1:t, :], xi[t - 1:t, :]

        z = jnp.zeros((1, S5_BLK), F32)
        lax.fori_loop(0, s // t, chunk, (z, z))

    u0, u1 = _s5_u_specs(s)
    bsp = pl.BlockSpec((S5_W, S5_BLK), lambda j: (0, j))
    csp = pl.BlockSpec((S5_BLK, S5_W), lambda j: (j, 0))
    vec = pl.BlockSpec((1, S5_BLK), lambda j: (0, j))
    xsp = pl.BlockSpec((s, S5_BLK), lambda j: (0, j))
    ysp = pl.BlockSpec((s, S5_W), lambda j: (0, 0))
    xshape = jax.ShapeDtypeStruct((s, S5_STATES), F32)
    return pl.pallas_call(
        body, out_shape=(xshape, xshape, jax.ShapeDtypeStruct((s, S5_W), F32)),
        grid=(S5_STATES // S5_BLK,), in_specs=[u0, u1, bsp, bsp, vec, vec, csp, csp],
        out_specs=(xsp, xsp, ysp), name=name,
        compiler_params=_cparams(("arbitrary",)))(proj, proj, b_re, b_im, lam_re, lam_im, c_re, c_im)


def _s5_scan_bwd(proj, dy, du_init, x_re, x_im, b_re, b_im, lam_re, lam_im, c_re, c_im, name):
    s = proj.shape[0]
    t = SCAN_T
    nc = s // t

    def body(u0_ref, u1_ref, dy_ref, dui_ref, xre_ref, xim_ref, bre_ref, bim_ref, lre_ref, lim_ref,
             cre_ref, cim_ref, du_ref, dlr_ref, dli_ref, dbr_ref, dbi_ref, dcr_ref, dci_ref):
        @pl.when(pl.program_id(0) == 0)
        def _():
            du_ref[...] = dui_ref[...]
        mr, mi = lre_ref[...], -lim_ref[...]
        consts = _cscan_consts(mr, mi, True)
        bre, bim, cre, cim = bre_ref[...], bim_ref[...], cre_ref[...], cim_ref[...]
        dbr_ref[...] = jnp.zeros_like(dbr_ref)
        dbi_ref[...] = jnp.zeros_like(dbi_ref)
        dcr_ref[...] = jnp.zeros_like(dcr_ref)
        dci_ref[...] = jnp.zeros_like(dci_ref)

        def chunk(ci_, carry):
            gnr, gni, dlr, dli = carry
            c = nc - 1 - ci_
            t0 = pl.multiple_of(c * t, t)
            rows = pl.ds(t0, t)
            before = pl.ds(pl.multiple_of(jnp.maximum(t0 - 8, 0), 8), 8)
            has_prev = (c > 0).astype(F32)
            dyc = dy_ref[rows, :].astype(BF16)
            u = jnp.concatenate([u0_ref[rows, :], u1_ref[rows, :]], axis=1).astype(BF16)
            gr, gi = _cscan_chunk(_dot_nt(dyc, cre), -_dot_nt(dyc, cim), consts, (gnr, gni), reverse=True)
            xr, xi = xre_ref[rows, :], xim_ref[rows, :]
            xpr = _shift_down_prev(xr, 1, xre_ref[before, :] * has_prev)
            xpi = _shift_down_prev(xi, 1, xim_ref[before, :] * has_prev)
            dlr = dlr + jnp.sum(gr * xpr + gi * xpi, axis=0, keepdims=True)
            dli = dli + jnp.sum(gi * xpr - gr * xpi, axis=0, keepdims=True)
            du_ref[rows, :] += _dot_nt(gr, bre) + _dot_nt(gi, bim)
            dbr_ref[...] += _dot_tn(u, gr)
            dbi_ref[...] += _dot_tn(u, gi)
            dcr_ref[...] += _dot_tn(xr, dyc)
            dci_ref[...] -= _dot_tn(xi, dyc)
            return gr[0:1, :], gi[0:1, :], dlr, dli

        z = jnp.zeros((1, S5_BLK), F32)
        res = lax.fori_loop(0, nc, chunk, (z, z, z, z))
        dlr_ref[...] = res[2]
        dli_ref[...] = res[3]

    u0, u1 = _s5_u_specs(s)
    bsp = pl.BlockSpec((S5_W, S5_BLK), lambda j: (0, j))
    csp = pl.BlockSpec((S5_BLK, S5_W), lambda j: (j, 0))
    vec = pl.BlockSpec((1, S5_BLK), lambda j: (0, j))
    xsp = pl.BlockSpec((s, S5_BLK), lambda j: (0, j))
    ysp = pl.BlockSpec((s, S5_W), lambda j: (0, 0))
    return pl.pallas_call(
        body,
        out_shape=(jax.ShapeDtypeStruct((s, S5_W), F32),
                   jax.ShapeDtypeStruct((1, S5_STATES), F32), jax.ShapeDtypeStruct((1, S5_STATES), F32),
                   jax.ShapeDtypeStruct((S5_W, S5_STATES), F32), jax.ShapeDtypeStruct((S5_W, S5_STATES), F32),
                   jax.ShapeDtypeStruct((S5_STATES, S5_W), F32), jax.ShapeDtypeStruct((S5_STATES, S5_W), F32)),
        grid=(S5_STATES // S5_BLK,),
        in_specs=[u0, u1, ysp, ysp, xsp, xsp, bsp, bsp, vec, vec, csp, csp],
        out_specs=(ysp, vec, vec, bsp, bsp, csp, csp), name=name,
        compiler_params=_cparams(("arbitrary",)))(
            proj, proj, dy, du_init, x_re, x_im, b_re, b_im, lam_re, lam_im, c_re, c_im)


def _s5_out_fwd(proj, y_acc, dvec, w_glu, b_glu, name):
    s = proj.shape[0]
    tm = 512
    uo = (3 * ATTN_W + 2 * LRU_W) // LANE

    def body(u0_ref, u1_ref, y_ref, d_ref, w_ref, b_ref, o_ref, yp_ref):
        u = jnp.concatenate([u0_ref[...], u1_ref[...]], axis=1)
        y = y_ref[...] + d_ref[...] * u
        yp_ref[...] = y
        yg = _gelu(y)
        o_ref[...] = yg * _sigmoid(_dot(yg, w_ref[...]) + b_ref[...])

    u0 = pl.BlockSpec((tm, LANE), lambda i: (i, uo))
    u1 = pl.BlockSpec((tm, LANE), lambda i: (i, uo + 1))
    row = pl.BlockSpec((tm, S5_W), lambda i: (i, 0))
    vec = pl.BlockSpec((1, S5_W), lambda i: (0, 0))
    wsp = pl.BlockSpec((S5_W, S5_W), lambda i: (0, 0))
    shp = jax.ShapeDtypeStruct((s, S5_W), F32)
    return pl.pallas_call(
        body, out_shape=(shp, shp), grid=(s // tm,), in_specs=[u0, u1, row, vec, wsp, vec],
        out_specs=(row, row), name=name,
        compiler_params=_cparams(("parallel",)))(proj, proj, y_acc, dvec, w_glu, b_glu)


def _s5_out_bwd(proj, y_pre, dout, dvec, w_glu, b_glu, name, dep=None):
    s = proj.shape[0]
    tm = 512
    uo = (3 * ATTN_W + 2 * LRU_W) // LANE

    def body(u0_ref, u1_ref, y_ref, do_ref, d_ref, w_ref, b_ref, *rest):
        dy_ref, dud_ref, dd_ref, dw_ref, db_ref = rest[-5:]

        @pl.when(pl.program_id(0) == 0)
        def _():
            dd_ref[...] = jnp.zeros_like(dd_ref)
            dw_ref[...] = jnp.zeros_like(dw_ref)
            db_ref[...] = jnp.zeros_like(db_ref)
        u = jnp.concatenate([u0_ref[...], u1_ref[...]], axis=1)
        y = y_ref[...]
        do = do_ref[...]
        yg = _gelu(y)
        sg = _sigmoid(_dot(yg, w_ref[...]) + b_ref[...])
        dz = do * yg * sg * (1.0 - sg)
        dyg = do * sg + _dot_nt(dz, w_ref[...])
        dy = dyg * _gelu_grad(y)
        dy_ref[...] = dy
        dud_ref[...] = d_ref[...] * dy
        dd_ref[...] += jnp.sum(dy * u, axis=0, keepdims=True)
        dw_ref[...] += _dot_tn(yg, dz)
        db_ref[...] += jnp.sum(dz, axis=0, keepdims=True)

    u0 = pl.BlockSpec((tm, LANE), lambda i: (i, uo))
    u1 = pl.BlockSpec((tm, LANE), lambda i: (i, uo + 1))
    row = pl.BlockSpec((tm, S5_W), lambda i: (i, 0))
    vec = pl.BlockSpec((1, S5_W), lambda i: (0, 0))
    wsp = pl.BlockSpec((S5_W, S5_W), lambda i: (0, 0))
    shp = jax.ShapeDtypeStruct((s, S5_W), F32)
    vshape = jax.ShapeDtypeStruct((1, S5_W), F32)
    dep_specs, dep_ops = _dep_args(dep)
    return pl.pallas_call(
        body, out_shape=(shp, shp, vshape, jax.ShapeDtypeStruct((S5_W, S5_W), F32), vshape),
        grid=(s // tm,), in_specs=[u0, u1, row, row, vec, wsp, vec] + dep_specs,
        out_specs=(row, row, vec, wsp, vec), name=name,
        compiler_params=_cparams(("arbitrary",)))(proj, proj, y_pre, dout, dvec, w_glu, b_glu, *dep_ops)


def _ffn_conv(x, prev8, cw, cb):
    y = cb + cw[FFN_CONV - 1:FFN_CONV, :] * x
    for k in range(FFN_CONV - 1):
        y = y + cw[k:k + 1, :] * _shift_down_prev(x, FFN_CONV - 1 - k, prev8)
    return y


def _ffn_up_act(h, wg, cw, cb, name, dep=None):
    s, d = h.shape
    tm = 512
    tb = 2 * FFN_CB
    nt = D_FF // FFN_CB

    def body(h_ref, wgate_ref, wval_ref, cw_ref, cb_ref, *rest):
        up_ref, y_ref, o_ref, ot_ref, carry = rest[-5:]

        @pl.when(pl.program_id(1) == 0)
        def _():
            carry[...] = jnp.zeros_like(carry)
        hb = h_ref[...].astype(BF16)
        x = jnp.concatenate([_dot(hb, wgate_ref[...]), _dot(hb, wval_ref[...])], axis=1)
        up_ref[...] = x
        y = _ffn_conv(x, carry[...], cw_ref[...], cb_ref[...])
        y_ref[...] = y
        carry[...] = x[tm - 8:tm, :]
        act = _gelu(y[:, :FFN_CB]) * y[:, FFN_CB:]
        o_ref[...] = act.astype(BF16)
        ot_ref[...] = act.T.astype(BF16)

    dep_specs, dep_ops = _dep_args(dep)
    return pl.pallas_call(
        body, out_shape=(jax.ShapeDtypeStruct((s, 2 * D_FF), F32), jax.ShapeDtypeStruct((s, 2 * D_FF), F32),
                         jax.ShapeDtypeStruct((s, D_FF), BF16), jax.ShapeDtypeStruct((D_FF, s), BF16)),
        grid=(nt, s // tm),
        in_specs=[pl.BlockSpec((tm, d), lambda t, i: (i, 0)),
                  pl.BlockSpec((None, d, FFN_CB), lambda t, i: (t, 0, 0)),
                  pl.BlockSpec((None, d, FFN_CB), lambda t, i: (t + nt, 0, 0)),
                  pl.BlockSpec((FFN_CONV, tb), lambda t, i: (0, t)),
                  pl.BlockSpec((1, tb), lambda t, i: (0, t))] + dep_specs,
        out_specs=(pl.BlockSpec((tm, tb), lambda t, i: (i, t)), pl.BlockSpec((tm, tb), lambda t, i: (i, t)),
                   pl.BlockSpec((tm, FFN_CB), lambda t, i: (i, t)), pl.BlockSpec((FFN_CB, tm), lambda t, i: (t, i))),
        scratch_shapes=[pltpu.VMEM((8, tb), F32)], name=name,
        compiler_params=_cparams(("parallel", "arbitrary")))(h, wg, wg, cw, cb, *dep_ops)


def _ffn_bwd(up, y_conv, dr, w_down, wg, cw, name):
    s = up.shape[0]
    d = dr.shape[1]
    tm = 256
    tb = 2 * FFN_CB
    nr = s // tm
    nt = D_FF // FFN_CB

    def body(x_ref, y_ref, dr_ref, wd_ref, wgate_ref, wval_ref, cw_ref,
             dup_ref, dh_ref, dcw_ref, dcb_ref, carry):
        i, t = pl.program_id(0), pl.program_id(1)

        @pl.when(i == 0)
        def _():
            carry[t] = jnp.zeros((8, tb), F32)

        @pl.when(t == 0)
        def _():
            dh_ref[...] = ALPHA * dr_ref[...]
        cwv = cw_ref[...]
        x = x_ref[...]
        dact = _dot_nt(dr_ref[...], wd_ref[...])
        gate, val = y_ref[:, :FFN_CB], y_ref[:, FFN_CB:]
        dy = jnp.concatenate([dact * val * _gelu_grad(gate), dact * _gelu(gate)], axis=1)
        next8 = carry[t]
        carry[t] = dy[0:8, :]
        dx = cwv[FFN_CONV - 1:FFN_CONV, :] * dy
        dcw_rows = [None] * FFN_CONV
        dcw_rows[FFN_CONV - 1] = jnp.sum(dy * x, axis=0, keepdims=True)
        for k in range(FFN_CONV - 1):
            dy_ahead = _shift_up_next(dy, FFN_CONV - 1 - k, next8)
            dx = dx + cwv[k:k + 1, :] * dy_ahead
            dcw_rows[k] = jnp.sum(dy_ahead * x, axis=0, keepdims=True)
        dup = dx.astype(BF16)
        dup_ref[...] = dup
        dh_ref[...] += _dot_nt(dup[:, :FFN_CB], wgate_ref[...]) + _dot_nt(dup[:, FFN_CB:], wval_ref[...])
        dcw_ref[...] = jnp.concatenate(dcw_rows, axis=0)
        dcb_ref[...] = jnp.sum(dy, axis=0, keepdims=True)

    row = lambda i: nr - 1 - i
    return pl.pallas_call(
        body, out_shape=(jax.ShapeDtypeStruct((s, 2 * D_FF), BF16), jax.ShapeDtypeStruct((s, d), F32),
                         jax.ShapeDtypeStruct((nr, FFN_CONV, 2 * D_FF), F32),
                         jax.ShapeDtypeStruct((nr, 1, 2 * D_FF), F32)),
        grid=(nr, nt),
        in_specs=[pl.BlockSpec((tm, tb), lambda i, t: (row(i), t)),
                  pl.BlockSpec((tm, tb), lambda i, t: (row(i), t)),
                  pl.BlockSpec((tm, d), lambda i, t: (row(i), 0)),
                  pl.BlockSpec((FFN_CB, d), lambda i, t: (t, 0)),
                  pl.BlockSpec((None, d, FFN_CB), lambda i, t: (t, 0, 0)),
                  pl.BlockSpec((None, d, FFN_CB), lambda i, t: (t + nt, 0, 0)),
                  pl.BlockSpec((FFN_CONV, tb), lambda i, t: (0, t))],
        out_specs=(pl.BlockSpec((tm, tb), lambda i, t: (row(i), t)),
                   pl.BlockSpec((tm, d), lambda i, t: (row(i), 0)),
                   pl.BlockSpec((None, FFN_CONV, tb), lambda i, t: (row(i), 0, t)),
                   pl.BlockSpec((None, 1, tb), lambda i, t: (row(i), 0, t))),
        scratch_shapes=[pltpu.VMEM((nt, 8, tb), F32)], name=name,
        compiler_params=_cparams(("arbitrary", "arbitrary")))(up, y_conv, dr, w_down, wg, wg, cw)


def _sum_partials(ld_ref):
    gg = ld_ref[0].astype(F32)
    for k in range(1, N_DEV):
        gg = gg + ld_ref[k].astype(F32)
    return gg


def _adam_update(w, g, m, v):
    mn = ADAM_B1 * m + (1.0 - ADAM_B1) * g
    vn = ADAM_B2 * v + (1.0 - ADAM_B2) * (g * g)
    m_hat = mn / (1.0 - ADAM_B1 ** ADAM_STEP)
    v_hat = vn / (1.0 - ADAM_B2 ** ADAM_STEP)
    return -ADAM_LR * (m_hat / (jnp.sqrt(v_hat) + ADAM_EPS) + ADAM_WD * w), mn, vn


def _adamw_many(landed, ws, ms, vs, name):
    n, nl = len(ws), len(landed)

    def body(*refs):
        ld = refs[:nl * n]
        w_refs, m_refs, v_refs = (refs[(nl + k) * n:(nl + k + 1) * n] for k in range(3))
        outs = refs[(nl + 3) * n:]
        for i in range(n):
            for l in range(nl):
                one = slice(l, l + 1)
                gg = _sum_partials(ld[l * n + i])
                outs[i][one] = gg
                outs[n + i][one], outs[2 * n + i][one], outs[3 * n + i][one] = _adam_update(
                    w_refs[i][one], gg, m_refs[i][one], v_refs[i][one])

    vm = pl.BlockSpec(memory_space=pltpu.VMEM)
    shapes = [jax.ShapeDtypeStruct(w.shape, F32) for w in ws] * 4
    res = pl.pallas_call(
        body, out_shape=tuple(shapes), in_specs=[vm] * ((nl + 3) * n), out_specs=tuple([vm] * (4 * n)),
        name=name, compiler_params=_cparams())(*[a for layer in landed for a in layer], *ws, *ms, *vs)
    return res[:n], res[n:2 * n], res[2 * n:3 * n], res[3 * n:]


def _adamw_sum(landed, w, m, v, layer, prev, name):
    _, r, c = landed.shape
    nl = w.shape[0]
    tm = 8
    for cand in (512, 256, 128, 64, 32, 16):
        if r % cand == 0 and N_DEV * cand * c * 4 <= 4 * 1024 * 1024:
            tm = cand
            break

    def body(*refs):
        ld_ref, w_ref, m_ref, v_ref = refs[:4]
        g_ref, d_ref, mo_ref, vo_ref = refs[-4:]
        gg = _sum_partials(ld_ref)
        g_ref[...] = gg
        d_ref[...], mo_ref[...], vo_ref[...] = _adam_update(w_ref[...], gg, m_ref[...], v_ref[...])

    blk = pl.BlockSpec((None, tm, c), lambda i: (layer, i, 0))
    in_specs = [pl.BlockSpec((N_DEV, tm, c), lambda i: (0, i, 0)), blk, blk, blk]
    args = [landed, w, m, v]
    aliases = {}
    if prev is not None:
        in_specs += [pl.BlockSpec(memory_space=pl.ANY)] * 4
        args += list(prev)
        aliases = {4 + k: k for k in range(4)}
    shp = jax.ShapeDtypeStruct((nl, r, c), F32)
    return pl.pallas_call(
        body, out_shape=(shp,) * 4, grid=(r // tm,), in_specs=in_specs, out_specs=(blk,) * 4,
        input_output_aliases=aliases, name=name, compiler_params=_cparams(("parallel",)))(*args)


def _all_gather(shards, name):
    na = len(shards)

    def body(*refs):
        x_refs, out_refs = refs[:na], refs[na:2 * na]
        send_sems, recv_sems, local_sems = refs[2 * na:]
        x, y, c = lax.axis_index("x"), lax.axis_index("y"), lax.axis_index("c")
        me, sibling = (x, y, c), (x, y, 1 - c)
        chips = [(1 - x, y), (x, 1 - y), (1 - x, 1 - y)]

        def copy(a, k, block, to, src=None):
            dst = out_refs[a].at[4 * block[0] + 2 * block[1] + block[2]]
            return pltpu.make_async_remote_copy(
                src_ref=dst if src is None else src, dst_ref=dst,
                send_sem=send_sems.at[7 * a + k], recv_sem=recv_sems.at[7 * a + k],
                device_id=to, device_id_type=pl.DeviceIdType.MESH)

        mine, first, passed = [], [], []
        for a in range(na):
            cp = pltpu.make_async_copy(x_refs[a], out_refs[a].at[4 * x + 2 * y + c], local_sems.at[a])
            cp.start()
            mine.append(cp)
            cps = [copy(a, 0, me, sibling, src=x_refs[a])]
            cps += [copy(a, 1 + j, me, (*chip, c), src=x_refs[a]) for j, chip in enumerate(chips)]
            for cp in cps:
                cp.start()
            first += cps
        for j, chip in enumerate(chips):
            for a in range(na):
                copy(a, 1 + j, (*chip, c), me).wait_recv()
                cp = copy(a, 4 + j, (*chip, c), sibling)
                cp.start()
                passed.append(cp)
        for a in range(na):
            copy(a, 0, sibling, me).wait_recv()
            for j, chip in enumerate(chips):
                copy(a, 4 + j, (*chip, 1 - c), me).wait_recv()
        for cp in first + passed:
            cp.wait_send()
        for cp in mine:
            cp.wait()

    anyspec = pl.BlockSpec(memory_space=pl.ANY)
    return pl.pallas_call(
        body, out_shape=tuple(jax.ShapeDtypeStruct((N_DEV,) + t.shape, t.dtype) for t in shards),
        in_specs=[anyspec] * na, out_specs=tuple([anyspec] * na),
        scratch_shapes=[pltpu.SemaphoreType.DMA((7 * na,)), pltpu.SemaphoreType.DMA((7 * na,)),
                        pltpu.SemaphoreType.DMA((na,))],
        name=name)(*shards)


_HBM = pl.BlockSpec(memory_space=pltpu.HBM)
_SEM = pl.BlockSpec(memory_space=pltpu.SEMAPHORE)
_EFFECT = pltpu.SideEffectType.DATAFLOW_SIDE_EFFECTING


def _exchange_copies(src_refs, land_refs, send_sems, recv_sems, local_sems, gather):
    x, y, c = lax.axis_index("x"), lax.axis_index("y"), lax.axis_index("c")
    me = 4 * x + 2 * y + c
    per_array = send_sems.shape[0] > N_DEV - 1
    local, remote = [], []
    for a, (src, land) in enumerate(zip(src_refs, land_refs)):
        local.append(pltpu.make_async_copy(src if gather else src.at[me], land.at[me],
                                           local_sems.at[a if per_array else 0]))
    for k in range(1, N_DEV):
        px = x ^ ((k >> 2) & 1)
        py = y ^ ((k >> 1) & 1)
        pc = c ^ (k & 1)
        for a, (src, land) in enumerate(zip(src_refs, land_refs)):
            remote.append(pltpu.make_async_remote_copy(
                src_ref=src if gather else src.at[4 * px + 2 * py + pc], dst_ref=land.at[me],
                send_sem=send_sems.at[(7 * a if per_array else 0) + k - 1],
                recv_sem=recv_sems.at[(7 * a if per_array else 0) + k - 1],
                device_id=(px, py, pc), device_id_type=pl.DeviceIdType.MESH))
    return local, remote


def _exchange_start(srcs, gather, name, dep=None):
    na = len(srcs)
    ns = na if na <= 4 else 1
    lands = [lax.empty(((N_DEV,) + t.shape) if gather else t.shape, t.dtype) for t in srcs]

    def body(*refs):
        src_refs, land_refs = refs[:na], refs[na:2 * na]
        nin = 2 * na + (0 if dep is None else 1)
        send_sems, recv_sems, local_sems = refs[nin:nin + 3]
        token = refs[-1]
        local, remote = _exchange_copies(src_refs, land_refs, send_sems, recv_sems, local_sems, gather)
        for cp in local + remote:
            cp.start()
        token[...] = jnp.zeros_like(token)

    dep_specs, dep_ops = _dep_args(dep)
    hbm = lambda t: pltpu.HBM(t.shape, t.dtype)
    out = pl.pallas_call(
        body, name=name,
        out_shape=(pltpu.SemaphoreType.DMA((7 * ns,)), pltpu.SemaphoreType.DMA((7 * ns,)),
                   pltpu.SemaphoreType.DMA((ns,)), *[hbm(t) for t in srcs], *[hbm(t) for t in lands],
                   jax.ShapeDtypeStruct((8, LANE), F32)),
        in_specs=[_HBM] * (2 * na) + dep_specs,
        out_specs=(_SEM, _SEM, _SEM, *[_HBM] * (2 * na), pl.BlockSpec(memory_space=pltpu.VMEM)),
        input_output_aliases={i: 3 + i for i in range(2 * na)},
        compiler_params=pltpu.CompilerParams(has_side_effects=_EFFECT),
    )(*[pltpu.with_memory_space_constraint(t, pltpu.HBM) for t in srcs + lands], *dep_ops)
    return (out[:3], out[3:3 + na], out[3 + na:3 + 2 * na]), out[-1]


def _exchange_wait(handle, gather, after, name):
    sems, srcs, lands = handle
    na = len(srcs)

    def body(*refs):
        src_refs, land_refs = refs[:na], refs[na:2 * na]
        send_sems, recv_sems, local_sems = refs[2 * na:2 * na + 3]
        local, remote = _exchange_copies(src_refs, land_refs, send_sems, recv_sems, local_sems, gather)
        for cp in remote:
            cp.wait_send()
            cp.wait_recv()
        for cp in local:
            cp.wait()

    hbm = lambda t: pltpu.HBM(t.shape, t.dtype)
    out = pl.pallas_call(
        body, name=name, out_shape=(*[hbm(t) for t in srcs], *[hbm(t) for t in lands]),
        in_specs=[_HBM] * (2 * na) + [_SEM] * 3 + [pl.BlockSpec(memory_space=pl.ANY)],
        out_specs=tuple([_HBM] * (2 * na)), input_output_aliases={i: i for i in range(2 * na)},
        compiler_params=pltpu.CompilerParams(has_side_effects=_EFFECT),
    )(*srcs, *lands, *sems, after)
    return out[na:]


def _block_diag(w):
    h, a, b = w.shape
    eye = jnp.eye(h, dtype=w.dtype)
    return (w[:, :, None, :] * eye[:, None, :, None]).reshape(h * a, h * b)


def _block_diag_extract(m, h):
    a, b = m.shape[0] // h, m.shape[1] // h
    return jnp.stack([m[i * a:(i + 1) * a, i * b:(i + 1) * b] for i in range(h)], axis=0)


def _block_diag_take(m, h):
    a, b = m.shape[0] // h, m.shape[1] // h
    eye = jnp.eye(h, dtype=m.dtype)
    return (m.reshape(h, a, h, b) * eye[:, None, :, None]).sum(axis=2)


def _ffn_interleave(w):
    lead = w.shape[:-1]
    nb = D_FF // FFN_CB
    return jnp.swapaxes(w.reshape(*lead, 2, nb, FFN_CB), -3, -2).reshape(*lead, 2 * D_FF)


def _ffn_deinterleave(w):
    lead = w.shape[:-1]
    nb = D_FF // FFN_CB
    return jnp.swapaxes(w.reshape(*lead, nb, 2, FFN_CB), -3, -2).reshape(*lead, 2 * D_FF)


def _gather_full(gathered, axis):
    shape = list(gathered.shape[1:])
    shape[axis] *= N_DEV
    return jnp.moveaxis(gathered, 0, axis).reshape(shape)


def _scatter_blocks(full, axis):
    shape = list(full.shape)
    shape[axis:axis + 1] = [N_DEV, shape[axis] // N_DEV]
    return jnp.moveaxis(full.reshape(shape), axis, 0)


def _pad_to(flat, mult):
    pad = (-flat.shape[-1]) % mult
    if pad:
        flat = jnp.concatenate([flat, jnp.zeros(flat.shape[:-1] + (pad,), flat.dtype)], axis=-1)
    return flat


def _layer_fwd(h_in, h_in_t, w, cos, sin, l, dep, get_ffn, target=None):
    tag = "l%d_" % l
    proj, qkv, h_t = _proj_rope(h_in, w['w_in'], cos, sin, tag + "proj_rope", dep=dep,
                                transposed=h_in_t is None)
    h_in_t = h_t if h_in_t is None else h_in_t
    outs, lses = [], []
    for d, qv in zip(DILATIONS, qkv):
        o, ls = _attn_fwd(qv, d, tag + "attn_d%d" % d)
        outs.append(o)
        lses.append(ls)
    lru, *lru_saved = _lru_fwd(proj, w['lru_conv_w'], w['lru_conv_b'], w['lru_wr'], w['lru_br'], w['lru_wi'],
                               w['lru_bi'], w['lru_lambda'], tag + "lru")
    x_re, x_im, y_acc = _s5_scan_fwd(proj, w['s5_bb_re'], w['s5_bb_im'], w['s5_lam_re'], w['s5_lam_im'],
                                     w['s5_cc_re'], w['s5_cc_im'], tag + "s5_scan")
    s5, y_pre = _s5_out_fwd(proj, y_acc, w['s5_d'], w['s5_w_glu'], w['s5_b_glu'], tag + "s5_out")
    w['w_up_g'], w['w_down'], w_out, ffn_dep = get_ffn(l, s5)
    if w_out is not None:
        w['w_out'] = w_out
    mixed_t, r1, h1, h1_t, attn_o, attn_lse = _mix_fwd(outs, lses, lru, s5, w['mix_norm_g'], h_in, w['w_out'],
                                                       w['ln1_g'], w['ln1_b'], tag + "mix_out_ln1")
    up, y_conv, act, act_t = _ffn_up_act(h1, w['w_up_g'], w['ffn_conv_w'], w['ffn_conv_b'], tag + "up_act",
                                         dep=ffn_dep)
    r2, out_a, out_b = _proj_ln(act, w['w_down'], h1, w['ln2_g'], w['ln2_b'], tag + "down_ln2", target=target)
    saved = dict(h_in_t=h_in_t, proj=proj, qkv=qkv, lru=lru, lru_saved=lru_saved, x_re=x_re, x_im=x_im,
                 y_pre=y_pre, s5=s5, mixed_t=mixed_t, attn_o=attn_o, attn_lse=attn_lse, r1=r1, h1_t=h1_t, up=up,
                 act_t=act_t, r2=r2, y_conv=y_conv)
    return out_a, out_b, saved


def _layer_bwd_ffn(dh2, sv, w, l, dep=None):
    tag = "l%d_" % l
    g = {}
    dr2, g['ln2_g'], g['ln2_b'] = _ln_bwd(sv['r2'], dh2, w['ln2_g'], tag + "ln2_bwd", dep=dep)
    g['w_down'] = _mm_dw(sv['act_t'], dr2, 1024, D_MODEL, 1024, tag + "down_dw")
    dup, dh1, dcw_parts, dcb_parts = _ffn_bwd(sv['up'], sv['y_conv'], dr2, w['w_down'], w['w_up_g'],
                                              w['ffn_conv_w'], tag + "ffn_bwd")
    g['ffn_conv_w'] = dcw_parts.sum(axis=0)
    g['ffn_conv_b'] = dcb_parts.sum(axis=0)
    g['w_up_g'] = _mm_up_dw(sv['h1_t'], dup, tag + "up_dw")
    return dh1, g


def _layer_bwd_mix(dh1, sv, w, cos, sin, l, dep, g_ffn, after_out_grad, after_small_grads, after_in_grad):
    tag = "l%d_" % l
    g = {}
    dr1, g['ln1_g'], g['ln1_b'], d_o, dlru, ds5, g['mix_norm_g'] = _mix_bwd(
        sv['r1'], dh1, w['ln1_g'], w['w_out'], sv['attn_o'][0], sv['lru'], sv['s5'], w['mix_norm_g'],
        tag + "ln1_mix_bwd", dep=dep)
    g['w_out'] = _mm_dw(sv['mixed_t'], dr1, 1024, D_MODEL, 1024, tag + "out_dw")
    dy, dud, g['s5_d'], g['s5_w_glu'], g['s5_b_glu'] = _s5_out_bwd(
        sv['proj'], sv['y_pre'], ds5, w['s5_d'], w['s5_w_glu'], w['s5_b_glu'], tag + "s5_out_bwd",
        dep=after_out_grad(l, g['w_out']))
    du, g['s5_lam_re'], g['s5_lam_im'], g['s5_bb_re'], g['s5_bb_im'], g['s5_cc_re'], g['s5_cc_im'] = \
        _s5_scan_bwd(sv['proj'], dy, dud, sv['x_re'], sv['x_im'], w['s5_bb_re'], w['s5_bb_im'],
                     w['s5_lam_re'], w['s5_lam_im'], w['s5_cc_re'], w['s5_cc_im'], tag + "s5_scan_bwd")
    (dxr, dgate, g['lru_conv_w'], g['lru_conv_b'], g['lru_wr'], g['lru_br'], g['lru_wi'], g['lru_bi'],
     g['lru_lambda']) = _lru_bwd(sv['proj'], dlru, *sv['lru_saved'], w['lru_conv_w'], w['lru_conv_b'], w['lru_wr'],
                                 w['lru_br'], w['lru_wi'], w['lru_bi'], w['lru_lambda'], tag + "lru_bwd")
    token = after_small_grads(l, _finish_layer_grads({**g_ffn, **g}, w, l))
    dqkv = [_attn_bwd(sv['qkv'][b], sv['attn_o'][b], d_o[b], sv['attn_lse'][b], d, tag + "attn_bwd_d%d" % d,
                      dep=token if b == 0 else None)
            for b, d in enumerate(DILATIONS)]
    dproj = _dproj_assemble(dqkv, dxr, dgate, du, cos, sin, tag + "dproj")
    g_in = _mm_dw(sv['h_in_t'], dproj, 1024, D_IN, 1024, tag + "in_dw")
    return _mm_nt(dproj, w['w_in'], 512, D_MODEL, tag + "in_dx", add=dr1, add_scale=ALPHA,
                  dep=after_in_grad(l, g_in))


def _s5_rep(a):
    return jnp.repeat(a, S5_C, axis=0)


def _prepare_layer(p, l):
    w = {}
    for n in ('w_in', 'w_out', 's5_w_glu'):
        if n in p:
            w[n] = p[n].astype(BF16)
    w['ffn_conv_w'] = _ffn_interleave(p['ffn_conv_w'])
    w['ffn_conv_b'] = _ffn_interleave(p['ffn_conv_b'])[None, :]
    w['lru_conv_w'] = p['lru_conv_w']
    for n in ('lru_conv_b', 'lru_br', 'lru_bi', 'lru_lambda', 's5_b_glu', 'mix_norm_g',
              'ln1_g', 'ln1_b', 'ln2_g', 'ln2_b'):
        w[n] = p[n][None, :]
    w['lru_wr'] = _block_diag(p['lru_wr']).astype(BF16)
    w['lru_wi'] = _block_diag(p['lru_wi']).astype(BF16)
    w['s5_d'] = p['s5_d'].reshape(1, S5_W)
    disc_in = (_s5_rep(p['s5_a_re']), _s5_rep(p['s5_a_im']),
               _s5_rep(jnp.broadcast_to(p['s5_log_step'][:, None], (S5_G, S5_P))),
               jnp.swapaxes(p['s5_b_re'], 1, 2).reshape(S5_W, S5_P),
               jnp.swapaxes(p['s5_b_im'], 1, 2).reshape(S5_W, S5_P))
    ab_re, ab_im, bb_re, bb_im = _s5_disc_fwd(*disc_in, "l%d_s5_disc" % l)
    w['s5_disc_in'] = disc_in
    w['s5_lam_re'] = ab_re.reshape(S5_G, S5_C, S5_P)[:, 0, :].reshape(1, S5_STATES)
    w['s5_lam_im'] = ab_im.reshape(S5_G, S5_C, S5_P)[:, 0, :].reshape(1, S5_STATES)
    w['s5_bb_re'] = _block_diag(bb_re.reshape(S5_G, S5_C, S5_P)).astype(BF16)
    w['s5_bb_im'] = _block_diag(bb_im.reshape(S5_G, S5_C, S5_P)).astype(BF16)
    w['s5_cc_re'] = _block_diag(jnp.swapaxes(p['s5_c_re'], 1, 2)).astype(BF16)
    w['s5_cc_im'] = _block_diag(jnp.swapaxes(p['s5_c_im'], 1, 2)).astype(BF16)
    return w


def _finish_layer_grads(g, w, l):
    out = {}
    for n in ('s5_w_glu', 'lru_conv_w'):
        out[n] = g[n]
    out['ffn_conv_w'] = _ffn_deinterleave(g['ffn_conv_w'])
    out['ffn_conv_b'] = _ffn_deinterleave(g['ffn_conv_b'])[0]
    for n in ('lru_conv_b', 'lru_br', 'lru_bi', 'lru_lambda', 's5_b_glu', 'mix_norm_g',
              'ln1_g', 'ln1_b', 'ln2_g', 'ln2_b'):
        out[n] = g[n][0]
    out['lru_wr'] = _block_diag_extract(g['lru_wr'], LRU_W // HEAD)
    out['lru_wi'] = _block_diag_extract(g['lru_wi'], LRU_W // HEAD)
    out['s5_d'] = g['s5_d'].reshape(S5_G, S5_C)
    out['s5_c_re'] = jnp.swapaxes(_block_diag_take(g['s5_cc_re'], S5_G), 1, 2)
    out['s5_c_im'] = jnp.swapaxes(_block_diag_take(g['s5_cc_im'], S5_G), 1, 2)
    rep = lambda v: _s5_rep(v.reshape(S5_G, S5_P)) * (1.0 / S5_C)
    cts = (rep(g['s5_lam_re']), rep(g['s5_lam_im']),
           _block_diag_take(g['s5_bb_re'], S5_G).reshape(S5_W, S5_P),
           _block_diag_take(g['s5_bb_im'], S5_G).reshape(S5_W, S5_P))
    da_re, da_im, dls, dbt_re, dbt_im = _s5_disc_bwd(*w['s5_disc_in'], cts, "l%d_s5_disc_bwd" % l)
    out['s5_a_re'] = da_re.reshape(S5_G, S5_C, S5_P).sum(axis=1)
    out['s5_a_im'] = da_im.reshape(S5_G, S5_C, S5_P).sum(axis=1)
    out['s5_log_step'] = dls.reshape(S5_G, S5_C * S5_P).sum(axis=1)
    out['s5_b_re'] = jnp.swapaxes(dbt_re.reshape(S5_G, S5_C, S5_P), 1, 2)
    out['s5_b_im'] = jnp.swapaxes(dbt_im.reshape(S5_G, S5_C, S5_P), 1, 2)
    return out


def _run_step(x, target, get_layer, get_ffn, after_ffn_grads, after_out_grad, after_small_grads, after_in_grad):
    cos, sin = _rope_tables(x.shape[0])
    h, h_t = x, None
    ws, saved = [], []
    for l in range(DEPTH):
        p, dep = get_layer(l, h)
        ws.append(_prepare_layer(p, l))
        h, h_t, sv = _layer_fwd(h, h_t, ws[l], cos, sin, l, dep, get_ffn, target if l == DEPTH - 1 else None)
        saved.append(sv)
    dh, loss_vec = h, h_t
    dep = None
    for l in reversed(range(DEPTH)):
        dh1, g = _layer_bwd_ffn(dh, saved[l], ws[l], l, dep)
        dep = after_ffn_grads(l, g)
        dh = _layer_bwd_mix(dh1, saved[l], ws[l], cos, sin, l, dep, g, after_out_grad, after_small_grads,
                            after_in_grad)
        dep = None
    return loss_vec[0, 0], dh


def _local_step(x, target, layers):
    grads = [{} for _ in range(DEPTH)]

    def ffn(l, h1):
        return layers[l]['w_up_g'].astype(BF16), layers[l]['w_down'].astype(BF16), None, None

    def keep_ffn(l, g):
        grads[l].update(w_up_g=g['w_up_g'], w_down=g['w_down'])

    def keep_small(l, g):
        grads[l].update(g)

    loss, dx = _run_step(x, target, lambda l, h: (layers[l], None), ffn, keep_ffn,
                         lambda l, g: grads[l].update(w_out=g), keep_small, lambda l, g: grads[l].update(w_in=g))
    return loss, dx, grads


def kernel(x, w_in, lru_conv_w, lru_conv_b, lru_wr, lru_br, lru_wi, lru_bi, lru_lambda, s5_a_re, s5_a_im, s5_b_re, s5_b_im, s5_c_re, s5_c_im, s5_d, s5_log_step, s5_w_glu, s5_b_glu, mix_norm_g, w_out, ln1_g, ln1_b, w_up, ffn_conv_w, ffn_conv_b, w_down, ln2_g, ln2_b, loss_target, m_w_in, m_lru_conv_w, m_lru_conv_b, m_lru_wr, m_lru_br, m_lru_wi, m_lru_bi, m_lru_lambda, m_s5_a_re, m_s5_a_im, m_s5_b_re, m_s5_b_im, m_s5_c_re, m_s5_c_im, m_s5_d, m_s5_log_step, m_s5_w_glu, m_s5_b_glu, m_mix_norm_g, m_w_out, m_ln1_g, m_ln1_b, m_w_up, m_ffn_conv_w, m_ffn_conv_b, m_w_down, m_ln2_g, m_ln2_b, v_w_in, v_lru_conv_w, v_lru_conv_b, v_lru_wr, v_lru_br, v_lru_wi, v_lru_bi, v_lru_lambda, v_s5_a_re, v_s5_a_im, v_s5_b_re, v_s5_b_im, v_s5_c_re, v_s5_c_im, v_s5_d, v_s5_log_step, v_s5_w_glu, v_s5_b_glu, v_mix_norm_g, v_w_out, v_ln1_g, v_ln1_b, v_w_up, v_ffn_conv_w, v_ffn_conv_b, v_w_down, v_ln2_g, v_ln2_b):
    args = locals()
    wl = {n: args[n] for n in WEIGHTS}
    ml = {n: args['m_' + n] for n in WEIGHTS}
    vl = {n: args['v_' + n] for n in WEIGHTS}

    small_sizes = [int(wl[n].size) for n in SMALL_SHARDED]
    small_flat = _pad_to(jnp.concatenate([wl[n].reshape(-1) for n in SMALL_SHARDED]), 8 * 1024)
    small_all, w_in0 = _all_gather([small_flat.reshape(-1, 1024), wl['w_in'][0].astype(BF16)], "gather_first")
    small_all = small_all.reshape(N_DEV, -1)
    small_full, off = {}, 0
    for n, sz in zip(SMALL_SHARDED, small_sizes):
        small_full[n] = _gather_full(small_all[:, off:off + sz].reshape((N_DEV,) + wl[n].shape), SHARD_AXIS[n])
        off += sz
    def mixer_params(l, g_in, g_out):
        p = {n: wl[n][l] for n in REPLICATED}
        p.update({n: small_full[n][l] for n in SMALL_SHARDED})
        p['w_in'] = _gather_full(g_in, 1)
        if g_out is not None:
            p['w_out'] = g_out.reshape(D_MODEL, D_MODEL)
        return p

    mix_names, ffn_names = ('w_in', 'w_out'), ('w_up', 'w_down')
    shards = lambda names, l: [wl[n][l].astype(BF16) for n in names]
    gathers = {}
    gathers[0, 'ffn'], rest0_token = _exchange_start(shards(('w_out',) + ffn_names, 0), True,
                                                     "gather_rest_l0_start", dep=w_in0)

    def get_layer(l, h):
        if l == 0:
            return mixer_params(0, w_in0, None), rest0_token
        return mixer_params(1, *_exchange_wait(gathers[1, 'mix'], True, h, "gather_mix_l1_wait")), None

    def get_ffn(l, after):
        landed = _exchange_wait(gathers[l, 'ffn'], True, after, "gather_ffn_l%d_wait" % l)
        if l > 0:
            return landed[0], landed[1].reshape(D_FF, D_MODEL), None, None
        g_out, g_up, g_down = landed
        gathers[1, 'mix'], token = _exchange_start(shards(mix_names, 1), True, "gather_mix_l1_start", dep=g_up)
        gathers[1, 'ffn'], token = _exchange_start(shards(ffn_names, 1), True, "gather_ffn_l1_start", dep=token)
        return g_up, g_down.reshape(D_FF, D_MODEL), g_out.reshape(D_MODEL, D_MODEL), token

    scatters = {}

    def after_ffn_grads(l, g):
        send = [g['w_up_g'], g['w_down'].reshape(N_DEV, D_FF // N_DEV, D_MODEL)]
        scatters[l, 'ffn'], token = _exchange_start(send, False, "scatter_ffn_l%d_start" % l)
        return token

    def after_out_grad(l, g_out):
        send = [g_out.reshape(N_DEV, D_MODEL // N_DEV, D_MODEL)]
        scatters[l, 'out'], token = _exchange_start(send, False, "scatter_out_l%d_start" % l)
        return token

    def after_in_grad(l, g_in):
        send = _scatter_blocks(g_in, 1)
        if l == 0:
            send = send.astype(BF16)
        scatters[l, 'in'], token = _exchange_start([send], False, "scatter_in_l%d_start" % l)
        return token

    def after_small_grads(l, g):
        rep = [g[n][None] for n in REPLICATED]
        shd = [_scatter_blocks(g[n], SHARD_AXIS[n] - 1)[:, None] for n in SMALL_SHARDED]
        scatters[l, 'rep'], token = _exchange_start(rep, True, "gather_rep_grads_l%d_start" % l)
        scatters[l, 'small'], token = _exchange_start(shd, False, "scatter_small_l%d_start" % l, dep=token)
        return token

    loss_local, grad_x = _run_step(x[0], loss_target[0], get_layer, get_ffn, after_ffn_grads, after_out_grad,
                                   after_small_grads, after_in_grad)
    loss = lax.psum(loss_local, AXES)

    results = {}
    big_prev = {n: None for n in BIG}

    def finish_big(l, part, names, after):
        landed = _exchange_wait(scatters[l, part], False, after, "scatter_%s_l%d_wait" % (part, l))
        for n, ld in zip(names, landed):
            big_prev[n] = _adamw_sum(ld, wl[n], ml[n], vl[n], l, big_prev[n], "adamw_%s_l%d" % (n, l))

    for l, part, names in ((1, 'ffn', ffn_names), (1, 'out', ('w_out',)), (1, 'in', ('w_in',)),
                           (0, 'ffn', ffn_names), (0, 'out', ('w_out',))):
        finish_big(l, part, names, grad_x)

    kinds = ('grad', 'delta', 'm', 'v')
    landed = [dict(zip(REPLICATED + SMALL_SHARDED,
                       list(_exchange_wait(scatters[l, 'rep'], True, grad_x, "gather_rep_grads_l%d_wait" % l)) +
                       list(_exchange_wait(scatters[l, 'small'], False, grad_x, "scatter_small_l%d_wait" % l))))
              for l in range(DEPTH)]
    matrices = ['lru_wr', 'lru_wi', 's5_a_re', 's5_a_im', 's5_c_re', 's5_c_im', 's5_d']
    widest = ['s5_b_re', 's5_b_im']
    vectors = [n for n in REPLICATED + SMALL_SHARDED if n not in matrices + widest]
    last = None
    for tag, names in (("vectors", vectors), ("matrices", matrices), ("s5_b", widest)):
        res = _adamw_many([[landed[l][n] for n in names] for l in range(DEPTH)], [wl[n] for n in names],
                          [ml[n] for n in names], [vl[n] for n in names], "adamw_" + tag)
        for kind, arrs in zip(kinds, res):
            for n, a in zip(names, arrs):
                results[kind, n] = a
        last = res[0][0]
    finish_big(0, 'in', ('w_in',), last)
    for n in BIG:
        results['grad', n], results['delta', n], results['m', n], results['v', n] = big_prev[n]

    out = [loss, grad_x[None]]
    for kind in kinds:
        out.extend(results[kind, n] for n in WEIGHTS)
    return tuple(out)
```

```python
import math

import jax
import jax.numpy as jnp
from jax import lax
from jax.experimental import pallas as pl
from jax.experimental.pallas import tpu as pltpu

F32 = jnp.float32
BF16 = jnp.bfloat16

N_DEV = 8
DEPTH = 2
D_MODEL = 1024
ATTN_W = 384
LRU_W = 384
S5_W = 256
D_IN = 2176
D_FF = 3072
HEAD = 64
ATTN_BLK = 128
ATTN_TILE = 1024
DILATIONS = (1, 4, 16)
S5_G = 16
S5_P = 64
S5_C = 16
S5_STATES = S5_G * S5_P
LRU_C = 8.0
LRU_CONV = 4
FFN_CONV = 3
ROPE_THETA = 10000.0
ALPHA = (2 * DEPTH) ** 0.25
LN_EPS = 1e-5
RMS_EPS = 1e-6
ADAM_LR, ADAM_B1, ADAM_B2, ADAM_EPS, ADAM_WD, ADAM_STEP = 0.001, 0.9, 0.999, 1e-8, 0.01, 10

LANE = 128
SCAN_T = 256
S5_BLK = 256
FFN_CB = 2 * D_FF // N_DEV
VMEM_LIMIT = 56 * 1024 * 1024

WEIGHTS = ['w_in', 'lru_conv_w', 'lru_conv_b', 'lru_wr', 'lru_br', 'lru_wi', 'lru_bi', 'lru_lambda',
           's5_a_re', 's5_a_im', 's5_b_re', 's5_b_im', 's5_c_re', 's5_c_im', 's5_d', 's5_log_step',
           's5_w_glu', 's5_b_glu', 'mix_norm_g', 'w_out', 'ln1_g', 'ln1_b', 'w_up', 'ffn_conv_w',
           'ffn_conv_b', 'w_down', 'ln2_g', 'ln2_b']
SHARD_AXIS = {'w_in': 2, 'lru_conv_w': 2, 's5_w_glu': 1, 'w_out': 1, 'w_up': 2, 'ffn_conv_w': 2, 'w_down': 1}
BIG = ['w_in', 'w_out', 'w_up', 'w_down']
SMALL_SHARDED = ['lru_conv_w', 'ffn_conv_w', 's5_w_glu']
REPLICATED = [n for n in WEIGHTS if n not in SHARD_AXIS]


def _cparams(sem=None):
    return pltpu.CompilerParams(dimension_semantics=sem, vmem_limit_bytes=VMEM_LIMIT)


def _grad_dtype(l):
    return BF16 if l == 0 else F32


def _ffn_dev(jb):
    return jb // 2 + (N_DEV // 2) * (jb % 2)


def _gelu(x):
    c = math.sqrt(2.0 / math.pi)
    t = jnp.tanh(c * (x + 0.044715 * (x * x * x)))
    return 0.5 * x * (1.0 + t)


def _gelu_grad(x):
    c = math.sqrt(2.0 / math.pi)
    x2 = x * x
    t = jnp.tanh(c * (x + 0.044715 * (x2 * x)))
    return 0.5 * (1.0 + t) + 0.5 * x * (1.0 - t * t) * (c * (1.0 + 3.0 * 0.044715 * x2))


def _sigmoid(x):
    return 1.0 / (1.0 + jnp.exp(-x))


def _log1p(x):
    u = 1.0 + x
    d = u - 1.0
    return jnp.where(d == 0.0, x, jnp.log(u) * (x / jnp.where(d == 0.0, 1.0, d)))


def _softplus(x):
    return jnp.maximum(x, 0.0) + _log1p(jnp.exp(-jnp.abs(x)))


def _expm1(x):
    return jnp.tanh(0.5 * x) * (jnp.exp(x) + 1.0)


def _dot(a, b):
    return jnp.dot(a.astype(BF16), b.astype(BF16), preferred_element_type=F32)


def _dot_nt(a, b):
    return lax.dot_general(a.astype(BF16), b.astype(BF16), (((1,), (1,)), ((), ())),
                           preferred_element_type=F32)


def _dot_tn(a, b):
    return lax.dot_general(a.astype(BF16), b.astype(BF16), (((0,), (0,)), ((), ())),
                           preferred_element_type=F32)


def _rows(shape):
    return lax.broadcasted_iota(jnp.int32, shape, 0)


def _shift_down_prev(x, s, prev8):
    if s == 0:
        return x
    t, l = x.shape
    r = pltpu.roll(x, s, axis=0)
    pr = pltpu.roll(prev8, s, axis=0)
    pad = jnp.concatenate([pr, jnp.zeros((t - 8, l), x.dtype)], axis=0)
    return jnp.where(_rows(x.shape) < s, pad, r)


def _shift_up_next(x, s, next8):
    if s == 0:
        return x
    t, l = x.shape
    r = pltpu.roll(x, t - s, axis=0)
    nx = pltpu.roll(next8, 8 - s, axis=0)
    pad = jnp.concatenate([jnp.zeros((t - 8, l), x.dtype), nx], axis=0)
    return jnp.where(_rows(x.shape) >= t - s, pad, r)


SUB = 8


def _tile_shift(x, s, fill, reverse):
    t = x.shape[0]
    pos = _rows(x.shape) & (SUB - 1)
    if reverse:
        return jnp.where(pos < SUB - s, pltpu.roll(x, t - s, axis=0), fill)
    return jnp.where(pos >= s, pltpu.roll(x, s, axis=0), fill)


def _scan_chunk(a, x, carry, reverse=False):
    s = 1
    while s < SUB:
        x = x + a * _tile_shift(x, s, 0.0, reverse)
        a = a * _tile_shift(a, s, 1.0, reverse)
        s *= 2
    nv = x.shape[0] // SUB
    out = [None] * nv
    for v in (reversed(range(nv)) if reverse else range(nv)):
        rows = slice(v * SUB, (v + 1) * SUB)
        out[v] = x[rows, :] + a[rows, :] * carry
        carry = out[v][0:1, :] if reverse else out[v][SUB - 1:SUB, :]
    return jnp.concatenate(out, axis=0)


def _cmul(ar, ai, br, bi):
    return ar * br - ai * bi, ar * bi + ai * br


def _cscan_consts(lr, li, reverse):
    pows = [(lr, li)]
    for _ in range(2):
        pows.append(_cmul(*pows[-1], *pows[-1]))
    rows = [(lr, li)]
    for _ in range(SUB - 1):
        rows.append(_cmul(*rows[-1], lr, li))
    if reverse:
        rows = rows[::-1]
    return pows, (jnp.concatenate([r for r, _ in rows], axis=0), jnp.concatenate([i for _, i in rows], axis=0))


def _cscan_chunk(xr, xi, consts, carry, reverse=False):
    pows, (p8r, p8i) = consts
    s = 1
    for pr, pi in pows:
        sr = _tile_shift(xr, s, 0.0, reverse)
        si = _tile_shift(xi, s, 0.0, reverse)
        xr, xi = xr + pr * sr - pi * si, xi + pr * si + pi * sr
        s *= 2
    nv = xr.shape[0] // SUB
    out_r, out_i = [None] * nv, [None] * nv
    cr, ci = carry
    for v in (reversed(range(nv)) if reverse else range(nv)):
        rows = slice(v * SUB, (v + 1) * SUB)
        out_r[v] = xr[rows, :] + p8r * cr - p8i * ci
        out_i[v] = xi[rows, :] + p8r * ci + p8i * cr
        edge = slice(0, 1) if reverse else slice(SUB - 1, SUB)
        cr, ci = out_r[v][edge, :], out_i[v][edge, :]
    return jnp.concatenate(out_r, axis=0), jnp.concatenate(out_i, axis=0)


def _dep_args(dep):
    return ([], []) if dep is None else ([pl.BlockSpec(memory_space=pl.ANY)], [dep])


def _mm_nt(a, w, tm, tn, name, add=None, add_scale=1.0, dep=None):
    m, k = a.shape
    n = w.shape[0]

    def body(a_ref, w_ref, *rest):
        o_ref = rest[-1]
        if add is None:
            o_ref[...] = _dot_nt(a_ref[...], w_ref[...])
        else:
            o_ref[...] = _dot_nt(a_ref[...], w_ref[...]) + add_scale * rest[0][...]

    in_specs = [pl.BlockSpec((tm, k), lambda j, i: (i, 0)), pl.BlockSpec((tn, k), lambda j, i: (j, 0))]
    args = [a, w]
    if add is not None:
        in_specs.append(pl.BlockSpec((tm, tn), lambda j, i: (i, j)))
        args.append(add)
    dep_specs, dep_ops = _dep_args(dep)
    return pl.pallas_call(
        body, out_shape=jax.ShapeDtypeStruct((m, n), F32), grid=(n // tn, m // tm),
        in_specs=in_specs + dep_specs, out_specs=pl.BlockSpec((tm, tn), lambda j, i: (i, j)), name=name,
        compiler_params=_cparams(("parallel", "parallel")))(*args, *dep_ops)


def _mm_dw(at, b, tm, tn, ts, name, out_dtype=F32):
    m, s = at.shape
    n = b.shape[1]
    nk = s // ts

    def body(a_ref, b_ref, o_ref, acc):
        @pl.when(pl.program_id(2) == 0)
        def _():
            acc[...] = jnp.zeros_like(acc)
        acc[...] += _dot(a_ref[...], b_ref[...])

        @pl.when(pl.program_id(2) == nk - 1)
        def _():
            o_ref[...] = acc[...].astype(out_dtype)

    return pl.pallas_call(
        body, out_shape=jax.ShapeDtypeStruct((m, n), out_dtype), grid=(m // tm, n // tn, nk),
        in_specs=[pl.BlockSpec((tm, ts), lambda i, j, k: (i, k)), pl.BlockSpec((ts, tn), lambda i, j, k: (k, j))],
        out_specs=pl.BlockSpec((tm, tn), lambda i, j, k: (i, j)),
        scratch_shapes=[pltpu.VMEM((tm, tn), F32)], name=name,
        compiler_params=_cparams(("parallel", "parallel", "arbitrary")))(at, b)


def _mm_up_dw(ht, dup, name, out_dtype=F32):
    d, s = ht.shape

    def body(a_ref, b_ref, o_ref):
        o_ref[...] = _dot(a_ref[...], b_ref[...]).astype(out_dtype)

    return pl.pallas_call(
        body, out_shape=jax.ShapeDtypeStruct((N_DEV, d, FFN_CB), out_dtype), grid=(N_DEV,),
        in_specs=[pl.BlockSpec((d, s), lambda j: (0, 0)), pl.BlockSpec((s, FFN_CB), lambda j: (0, j))],
        out_specs=pl.BlockSpec((None, d, FFN_CB), lambda j: (_ffn_dev(j), 0, 0)), name=name,
        compiler_params=_cparams(("parallel",)))(ht, dup)


def _layer_norm(r, g, b):
    mu = jnp.mean(r, axis=-1, keepdims=True)
    xc = r - mu
    var = jnp.mean(xc * xc, axis=-1, keepdims=True)
    return xc * lax.rsqrt(var + LN_EPS) * g + b


def _proj_ln(a, w, resid, g, bias, name, transposed=True, target=None):
    s, k = a.shape
    d = w.shape[1]
    tm = 512

    def body(a_ref, w_ref, x_ref, g_ref, bias_ref, *rest):
        r = ALPHA * x_ref[...] + _dot(a_ref[...], w_ref[...])
        h = _layer_norm(r, g_ref[...], bias_ref[...])
        if target is None:
            r_ref, h_ref = rest[0], rest[1]
            h_ref[...] = h
            if transposed:
                rest[2][...] = h.T.astype(BF16)
        else:
            t_ref, r_ref, dy_ref, l_ref = rest

            @pl.when(pl.program_id(0) == 0)
            def _():
                l_ref[...] = jnp.zeros_like(l_ref)
            e = h - t_ref[...]
            dy_ref[...] = e * (1.0 / d)
            part = 0.5 * jnp.sum(jnp.mean(e * e, axis=-1, keepdims=True), axis=0, keepdims=True)
            l_ref[...] += jnp.broadcast_to(part, l_ref.shape)
        r_ref[...] = r

    row = pl.BlockSpec((tm, d), lambda i: (i, 0))
    vec = pl.BlockSpec((1, d), lambda i: (0, 0))
    in_specs = [pl.BlockSpec((tm, k), lambda i: (i, 0)), pl.BlockSpec((k, d), lambda i: (0, 0)), row, vec, vec]
    args = [a, w, resid, g, bias]
    shapes = [jax.ShapeDtypeStruct((s, d), F32), jax.ShapeDtypeStruct((s, d), F32)]
    specs = [row, row]
    if target is not None:
        in_specs.append(row)
        args.append(target)
        shapes.append(jax.ShapeDtypeStruct((1, LANE), F32))
        specs.append(pl.BlockSpec((1, LANE), lambda i: (0, 0)))
    elif transposed:
        shapes.append(jax.ShapeDtypeStruct((d, s), BF16))
        specs.append(pl.BlockSpec((d, tm), lambda i: (0, i)))
    return pl.pallas_call(
        body, out_shape=tuple(shapes), grid=(s // tm,), in_specs=in_specs, out_specs=tuple(specs), name=name,
        compiler_params=_cparams(("arbitrary",) if target is not None else ("parallel",)))(*args)


def _layer_norm_bwd(r, dh, g):
    mu = jnp.mean(r, axis=-1, keepdims=True)
    xc = r - mu
    var = jnp.mean(xc * xc, axis=-1, keepdims=True)
    rstd = lax.rsqrt(var + LN_EPS)
    xh = xc * rstd
    dxh = dh * g
    m1 = jnp.mean(dxh, axis=-1, keepdims=True)
    m2 = jnp.mean(dxh * xh, axis=-1, keepdims=True)
    return (rstd * (dxh - m1 - xh * m2), jnp.sum(dh * xh, axis=0, keepdims=True),
            jnp.sum(dh, axis=0, keepdims=True))


def _ln_bwd(r, dh, g, name, dep=None):
    s, d = r.shape
    tm = 512

    def body(r_ref, dh_ref, g_ref, *rest):
        dr_ref, dg_ref, db_ref = rest[-3:]

        @pl.when(pl.program_id(0) == 0)
        def _():
            dg_ref[...] = jnp.zeros_like(dg_ref)
            db_ref[...] = jnp.zeros_like(db_ref)
        dr_ref[...], dg_rows, db_rows = _layer_norm_bwd(r_ref[...], dh_ref[...], g_ref[...])
        dg_ref[...] += dg_rows
        db_ref[...] += db_rows

    row = pl.BlockSpec((tm, d), lambda i: (i, 0))
    vec = pl.BlockSpec((1, d), lambda i: (0, 0))
    dep_specs, dep_ops = _dep_args(dep)
    return pl.pallas_call(
        body, out_shape=(jax.ShapeDtypeStruct((s, d), F32), jax.ShapeDtypeStruct((1, d), F32),
                         jax.ShapeDtypeStruct((1, d), F32)),
        grid=(s // tm,), in_specs=[row, row, vec] + dep_specs, out_specs=(row, vec, vec), name=name,
        compiler_params=_cparams(("arbitrary",)))(r, dh, g, *dep_ops)


def _rope_tables(s):
    half = HEAD // 2
    pos = jnp.arange(s, dtype=F32)
    inv = ROPE_THETA ** (-jnp.arange(half, dtype=F32) * 2.0 / HEAD)
    ang = pos[:, None] * inv[None, :]
    cos, sin = jnp.cos(ang), jnp.sin(ang)
    cos = jnp.concatenate([cos, cos, cos, cos], axis=1)
    sin = jnp.concatenate([-sin, sin, -sin, sin], axis=1)
    return cos, sin


def _rotate(x, cos, sin):
    lane = lax.broadcasted_iota(jnp.int32, x.shape, 1)
    partner = jnp.where((lane % HEAD) < HEAD // 2, pltpu.roll(x, LANE - HEAD // 2, axis=1),
                        pltpu.roll(x, HEAD // 2, axis=1))
    return x * cos + partner * sin


def _class_rows(c, d, tm):
    return pl.ds(c, tm // d, stride=d) if d > 1 else pl.ds(0, tm)


def _dilated_spec(tm, d, w):
    return pl.BlockSpec((tm // d, d * w), lambda i: (i, 0))


def _token_scratch(tm, w):
    return pltpu.VMEM((w // LANE, tm, LANE), F32)


def _to_tokens(src_ref, dst3, d, tm):
    nj = dst3.shape[0]
    for cls in range(d):
        for j in range(nj):
            col = (cls * nj + j) * LANE
            dst3.at[j][_class_rows(cls, d, tm), :] = src_ref[:, col:col + LANE]


def _to_dilated(src3, dst_ref, d, tm):
    nj = src3.shape[0]
    for cls in range(d):
        for j in range(nj):
            col = (cls * nj + j) * LANE
            dst_ref[:, col:col + LANE] = src3.at[j][_class_rows(cls, d, tm), :].astype(dst_ref.dtype)


def _token_value(src3):
    return jnp.concatenate([src3[j] for j in range(src3.shape[0])], axis=1)


def _proj_rope(h, w_in, cos, sin, name, dep=None, transposed=False):
    s, d_model = h.shape
    tm = 512
    w = 3 * ATTN_W
    nj = w // LANE

    def body(h_ref, w_ref, c_ref, s_ref, *rest):
        rot = rest[-1]
        if transposed:
            p_ref, o_refs, ht_ref = rest[-6], rest[-5:-2], rest[-2]
            ht_ref[...] = h_ref[...].T.astype(BF16)
        else:
            p_ref, o_refs = rest[-5], rest[-4:-1]
        y = _dot(h_ref[...], w_ref[...])
        p_ref[...] = y
        c, sn = c_ref[...], s_ref[...]
        for j in range(nj):
            x = y[:, j * LANE:(j + 1) * LANE]
            rot[j] = _rotate(x, c, sn) if j < 2 * ATTN_W // LANE else x
        for d, o_ref in zip(DILATIONS, o_refs):
            _to_dilated(rot, o_ref, d, tm)

    tab = pl.BlockSpec((tm, LANE), lambda i: (i, 0))
    dep_specs, dep_ops = _dep_args(dep)
    shapes = [jax.ShapeDtypeStruct((s, D_IN), F32), *[jax.ShapeDtypeStruct((s // d, d * w), BF16) for d in DILATIONS]]
    specs = [pl.BlockSpec((tm, D_IN), lambda i: (i, 0)), *[_dilated_spec(tm, d, w) for d in DILATIONS]]
    if transposed:
        shapes.append(jax.ShapeDtypeStruct((d_model, s), BF16))
        specs.append(pl.BlockSpec((d_model, tm), lambda i: (0, i)))
    res = pl.pallas_call(
        body, out_shape=tuple(shapes), grid=(s // tm,),
        in_specs=[pl.BlockSpec((tm, d_model), lambda i: (i, 0)), pl.BlockSpec((d_model, D_IN), lambda i: (0, 0)),
                  tab, tab] + dep_specs,
        out_specs=tuple(specs), scratch_shapes=[_token_scratch(tm, w)], name=name,
        compiler_params=_cparams(("parallel",)))(h, w_in, cos, sin, *dep_ops)
    return res[0], res[1:4], (res[4] if transposed else None)


def _dproj_assemble(dqkv_list, dxr, dgate, du, cos, sin, name):
    s = dxr.shape[0]
    tm = 512
    nq = 3 * ATTN_W // LANE

    def body(*refs):
        br = refs[:9]
        dxr_ref, dg_ref, du_ref, c_ref, s_ref, o_ref = refs[9:15]
        tok = refs[15:]
        c, sn = c_ref[...], -s_ref[...]
        for part in range(3):
            for b, d in enumerate(DILATIONS[1:], start=1):
                _to_tokens(br[3 * b + part], tok[2 * part + b - 1], d, tm)
        for j in range(nq):
            part, jj = divmod(j, ATTN_W // LANE)
            x = br[part][:, jj * LANE:(jj + 1) * LANE] + tok[2 * part][jj] + tok[2 * part + 1][jj]
            if part < 2:
                x = _rotate(x, c, sn)
            o_ref[:, j * LANE:(j + 1) * LANE] = x.astype(BF16)
        o_ref[:, 3 * ATTN_W:3 * ATTN_W + LRU_W] = dxr_ref[...].astype(BF16)
        o_ref[:, 3 * ATTN_W + LRU_W:3 * ATTN_W + 2 * LRU_W] = dg_ref[...].astype(BF16)
        o_ref[:, 3 * ATTN_W + 2 * LRU_W:] = du_ref[...].astype(BF16)

    a_spec = pl.BlockSpec((tm, ATTN_W), lambda i: (i, 0))
    tab = pl.BlockSpec((tm, LANE), lambda i: (i, 0))
    ordered = [dqkv_list[b][p] for b in range(3) for p in range(3)]
    d_specs = [_dilated_spec(tm, d, ATTN_W) for d in DILATIONS for _ in range(3)]
    return pl.pallas_call(
        body, out_shape=jax.ShapeDtypeStruct((s, D_IN), BF16), grid=(s // tm,),
        in_specs=d_specs + [a_spec, a_spec, pl.BlockSpec((tm, S5_W), lambda i: (i, 0)), tab, tab],
        out_specs=pl.BlockSpec((tm, D_IN), lambda i: (i, 0)),
        scratch_shapes=[_token_scratch(tm, ATTN_W)] * 6, name=name,
        compiler_params=_cparams(("parallel",)))(*ordered, dxr, dgate, du, cos, sin)


def _attn_tiles(s, d):
    m = s // d
    tq = min(m, ATTN_TILE)
    return m, tq, tq // ATTN_BLK


def _band_mask(qb):
    qi = lax.broadcasted_iota(jnp.int32, (ATTN_BLK, 2 * ATTN_BLK), 0)
    ki = lax.broadcasted_iota(jnp.int32, (ATTN_BLK, 2 * ATTN_BLK), 1)
    dist = qi + ATTN_BLK - ki
    return (dist >= 0) & (dist <= ATTN_BLK) & ((ki >= ATTN_BLK) | (qb > 0))


def _attn_fwd(qv, d, name):
    m = qv.shape[0]
    w3 = 3 * ATTN_W
    _, tq, n = _attn_tiles(m * d, d)
    scale = HEAD ** -0.5

    def body(x_ref, p_ref, o_ref, l_ref):
        b = pl.program_id(1)

        def block(i, first):
            r0 = 0 if first else pl.multiple_of(i * ATTN_BLK, ATTN_BLK)
            rows = pl.ds(r0, ATTN_BLK)
            valid = _band_mask(b * n + i)
            if not first:
                krows = pl.ds(pl.multiple_of(i * ATTN_BLK - ATTN_BLK, ATTN_BLK), 2 * ATTN_BLK)
            low = lax.broadcasted_iota(jnp.int32, (1, LANE), 1) < HEAD
            for hp in range(ATTN_W // LANE):
                qs, ks, vs = (slice(part * ATTN_W + hp * LANE, part * ATTN_W + (hp + 1) * LANE) for part in range(3))
                q2 = x_ref[rows, qs]
                if first:
                    k2 = jnp.concatenate([p_ref[:, ks], x_ref[0:ATTN_BLK, ks]], axis=0)
                    v2 = jnp.concatenate([p_ref[:, vs], x_ref[0:ATTN_BLK, vs]], axis=0)
                else:
                    k2 = x_ref[krows, ks]
                    v2 = x_ref[krows, vs]
                outs, lses = [], []
                for mask in (low, ~low):
                    q = jnp.where(mask, q2, jnp.zeros_like(q2))
                    sc = jnp.where(valid, _dot_nt(q, k2) * scale, -1e30)
                    mx = jnp.max(sc, axis=-1, keepdims=True)
                    p = jnp.exp(sc - mx)
                    l = jnp.sum(p, axis=-1, keepdims=True)
                    outs.append(_dot(p, v2) / l)
                    lses.append(mx + jnp.log(l))
                o_ref[rows, hp * LANE:(hp + 1) * LANE] = jnp.where(low, outs[0], outs[1])
                l_ref[rows, hp * LANE:(hp + 1) * LANE] = jnp.where(low, lses[0], lses[1])

        block(0, True)
        if n > 1:
            def loop(i, carry):
                block(i, False)
                return carry
            lax.fori_loop(1, n, loop, 0)

    shp = jax.ShapeDtypeStruct((m, d * ATTN_W), F32)
    ospec = pl.BlockSpec((tq, ATTN_W), lambda c, b: (b, c))
    out, lse = pl.pallas_call(
        body, out_shape=(shp, shp), grid=(d, m // tq),
        in_specs=[pl.BlockSpec((tq, w3), lambda c, b: (b, c)),
                  pl.BlockSpec((ATTN_BLK, w3), lambda c, b: (jnp.maximum(b * n - 1, 0), c))],
        out_specs=(ospec, ospec), name=name,
        compiler_params=_cparams(("parallel", "parallel")))(qv, qv)
    return out, lse


def _attn_bwd(qv, ov, dov, lv, d, name, dep=None):
    m = qv.shape[0]
    w3 = 3 * ATTN_W
    _, tq, n = _attn_tiles(m * d, d)
    nb = m // ATTN_BLK
    scale = HEAD ** -0.5

    def body(x_ref, p_ref, nx_ref, o_ref, do_ref, l_ref, on_ref, don_ref, ln_ref, *rest):
        dq_ref, dk_ref, dv_ref = rest[-3:]
        b = pl.program_id(1)
        dk_ref[...] = jnp.zeros_like(dk_ref)
        dv_ref[...] = jnp.zeros_like(dv_ref)

        low = lax.broadcasted_iota(jnp.int32, (1, LANE), 1) < HEAD

        def pair_grads(q2, k2, v2, o2, do2, l2, valid):
            dq, dk, dv = [], 0.0, 0.0
            for mask, lse in ((low, l2[:, 0:1]), (~low, l2[:, HEAD:HEAD + 1])):
                q = jnp.where(mask, q2, jnp.zeros_like(q2))
                do = jnp.where(mask, do2, 0.0)
                sc = jnp.where(valid, _dot_nt(q, k2) * scale, -1e30)
                p = jnp.exp(sc - lse)
                delta = jnp.sum(do * o2, axis=-1, keepdims=True)
                ds = p * (_dot_nt(do, v2) - delta) * scale
                dq.append(_dot(ds, k2))
                dk = dk + _dot_tn(ds, q)
                dv = dv + _dot_tn(p, do)
            return jnp.where(low, dq[0], dq[1]), dk, dv

        def cols(hp):
            return [slice(part * ATTN_W + hp * LANE, part * ATTN_W + (hp + 1) * LANE) for part in range(3)]

        def block(i, first):
            r0 = 0 if first else pl.multiple_of(i * ATTN_BLK, ATTN_BLK)
            rows = pl.ds(r0, ATTN_BLK)
            valid = _band_mask(b * n + i)
            if not first:
                krows = pl.ds(pl.multiple_of(i * ATTN_BLK - ATTN_BLK, ATTN_BLK), 2 * ATTN_BLK)
            for hp in range(ATTN_W // LANE):
                qs, ks, vs = cols(hp)
                if first:
                    k2 = jnp.concatenate([p_ref[:, ks], x_ref[0:ATTN_BLK, ks]], axis=0)
                    v2 = jnp.concatenate([p_ref[:, vs], x_ref[0:ATTN_BLK, vs]], axis=0)
                else:
                    k2 = x_ref[krows, ks]
                    v2 = x_ref[krows, vs]
                dq, dk, dv = pair_grads(x_ref[rows, qs], k2, v2, o_ref[rows, qs], do_ref[rows, qs],
                                        l_ref[rows, qs], valid)
                dq_ref[rows, qs] = dq
                if first:
                    dk_ref[0:ATTN_BLK, qs] += dk[ATTN_BLK:, :]
                    dv_ref[0:ATTN_BLK, qs] += dv[ATTN_BLK:, :]
                else:
                    dk_ref[krows, qs] += dk
                    dv_ref[krows, qs] += dv

        block(0, True)
        if n > 1:
            def loop(i, carry):
                block(i, False)
                return carry
            lax.fori_loop(1, n, loop, 0)

        last = slice((n - 1) * ATTN_BLK, n * ATTN_BLK)
        qi = lax.broadcasted_iota(jnp.int32, (ATTN_BLK, ATTN_BLK), 0)
        ki = lax.broadcasted_iota(jnp.int32, (ATTN_BLK, ATTN_BLK), 1)
        valid_next = (qi <= ki) & ((b + 1) * n < nb)
        for hp in range(ATTN_W // LANE):
            qs, ks, vs = cols(hp)
            _, dk, dv = pair_grads(nx_ref[:, qs], x_ref[last, ks], x_ref[last, vs], on_ref[:, qs], don_ref[:, qs],
                                   ln_ref[:, qs], valid_next)
            dk_ref[last, qs] += dk
            dv_ref[last, qs] += dv

    nxt = lambda b: jnp.minimum((b + 1) * n, nb - 1)
    xs = pl.BlockSpec((tq, w3), lambda c, b: (b, c))
    xp = pl.BlockSpec((ATTN_BLK, w3), lambda c, b: (jnp.maximum(b * n - 1, 0), c))
    xn = pl.BlockSpec((ATTN_BLK, w3), lambda c, b: (nxt(b), c))
    a = pl.BlockSpec((tq, ATTN_W), lambda c, b: (b, c))
    an = pl.BlockSpec((ATTN_BLK, ATTN_W), lambda c, b: (nxt(b), c))
    shp = jax.ShapeDtypeStruct((m, d * ATTN_W), F32)
    dep_specs, dep_ops = _dep_args(dep)
    return pl.pallas_call(
        body, out_shape=(shp, shp, shp), grid=(d, m // tq),
        in_specs=[xs, xp, xn, a, a, a, an, an, an] + dep_specs, out_specs=(a, a, a), name=name,
        compiler_params=_cparams(("parallel", "parallel")))(qv, qv, qv, ov, dov, lv, ov, dov, lv, *dep_ops)


def _rms(x, g):
    ms = jnp.mean(x * x, axis=-1, keepdims=True)
    return x * lax.rsqrt(ms + RMS_EPS) * g


def _rms_bwd(x, g, dy):
    ms = jnp.mean(x * x, axis=-1, keepdims=True)
    r = lax.rsqrt(ms + RMS_EPS)
    dyg = dy * g
    dx = r * dyg - x * (r * r * r) * jnp.mean(x * dyg, axis=-1, keepdims=True)
    return dx, dy * x * r


def _mix_fwd(outs, lses, lru, s5, g, h_in, w_out, ln_g, ln_b, name):
    s = lru.shape[0]
    tm = 256

    def body(o1, o2, o3, l1, l2, l3, lru_ref, s5_ref, g_ref, x_ref, w_ref, lg_ref, lb_ref,
             mixed_t_ref, r_ref, h_ref, ht_ref, ov1, ov2, ov3, lv1, lv2, lv3, so2, so3, sl2, sl3):
        for d, src, dst in ((DILATIONS[1], o2, so2), (DILATIONS[2], o3, so3),
                            (DILATIONS[1], l2, sl2), (DILATIONS[2], l3, sl3)):
            _to_tokens(src, dst, d, tm)
        a1, a2, a3 = l1[...], _token_value(sl2), _token_value(sl3)
        mx = jnp.maximum(jnp.maximum(a1, a2), a3)
        e1, e2, e3 = jnp.exp(a1 - mx), jnp.exp(a2 - mx), jnp.exp(a3 - mx)
        den = e1 + e2 + e3
        o = (e1 * o1[...] + e2 * _token_value(so2) + e3 * _token_value(so3)) / den
        lse = mx + jnp.log(den)
        ov1[...] = o
        lv1[...] = lse
        for j in range(ATTN_W // LANE):
            so2[j] = o[:, j * LANE:(j + 1) * LANE]
            sl2[j] = lse[:, j * LANE:(j + 1) * LANE]
        for d, o_dst, l_dst in ((DILATIONS[1], ov2, lv2), (DILATIONS[2], ov3, lv3)):
            _to_dilated(so2, o_dst, d, tm)
            _to_dilated(sl2, l_dst, d, tm)
        gg = g_ref[...]
        mixed = jnp.concatenate([_rms(o, gg[:, :ATTN_W]),
                                 _rms(lru_ref[...], gg[:, ATTN_W:ATTN_W + LRU_W]),
                                 _rms(s5_ref[...], gg[:, ATTN_W + LRU_W:])], axis=1)
        mixed_t_ref[...] = mixed.T.astype(BF16)
        r = ALPHA * x_ref[...] + _dot(mixed, w_ref[...])
        h = _layer_norm(r, lg_ref[...], lb_ref[...])
        r_ref[...] = r
        h_ref[...] = h
        ht_ref[...] = h.T.astype(BF16)

    a = pl.BlockSpec((tm, ATTN_W), lambda i: (i, 0))
    s5s = pl.BlockSpec((tm, S5_W), lambda i: (i, 0))
    full = pl.BlockSpec((tm, D_MODEL), lambda i: (i, 0))
    vec = pl.BlockSpec((1, D_MODEL), lambda i: (0, 0))
    dil = [_dilated_spec(tm, d, ATTN_W) for d in DILATIONS]
    dshape = [jax.ShapeDtypeStruct((s // d, d * ATTN_W), F32) for d in DILATIONS]
    tshape = jax.ShapeDtypeStruct((D_MODEL, s), BF16)
    fshape = jax.ShapeDtypeStruct((s, D_MODEL), F32)
    tspec = pl.BlockSpec((D_MODEL, tm), lambda i: (0, i))
    res = pl.pallas_call(
        body, out_shape=(tshape, fshape, fshape, tshape, *dshape, *dshape),
        grid=(s // tm,),
        in_specs=dil + dil + [a, s5s, vec, full, pl.BlockSpec((D_MODEL, D_MODEL), lambda i: (0, 0)), vec, vec],
        out_specs=(tspec, full, full, tspec, *dil, *dil),
        scratch_shapes=[_token_scratch(tm, ATTN_W)] * 4, name=name,
        compiler_params=_cparams(("parallel",)))(*outs, *lses, lru, s5, g, h_in, w_out, ln_g, ln_b)
    return res[0], res[1], res[2], res[3], res[4:7], res[7:10]


def _mix_bwd(r, dh, ln_g, w_out, o, lru, s5, g, name, dep=None):
    s = lru.shape[0]
    tm = 256

    def body(r_ref, dh_ref, lg_ref, w_ref, o_ref, lru_ref, s5_ref, g_ref, *rest):
        dr_ref, dlg_ref, dlb_ref, do_ref, do2_ref, do3_ref, dlru_ref, ds5_ref, dg_ref, stage = rest[-10:]

        @pl.when(pl.program_id(0) == 0)
        def _():
            dg_ref[...] = jnp.zeros_like(dg_ref)
            dlg_ref[...] = jnp.zeros_like(dlg_ref)
            dlb_ref[...] = jnp.zeros_like(dlb_ref)
        gg = g_ref[...]
        dr, dlg_rows, dlb_rows = _layer_norm_bwd(r_ref[...], dh_ref[...], lg_ref[...])
        dr_ref[...] = dr
        dlg_ref[...] += dlg_rows
        dlb_ref[...] += dlb_rows
        dm = _dot_nt(dr, w_ref[...])
        dx, dgr = _rms_bwd(o_ref[...], gg[:, :ATTN_W], dm[:, :ATTN_W])
        do_ref[...] = dx
        for j in range(ATTN_W // LANE):
            stage[j] = dx[:, j * LANE:(j + 1) * LANE]
        _to_dilated(stage, do2_ref, DILATIONS[1], tm)
        _to_dilated(stage, do3_ref, DILATIONS[2], tm)
        dg_ref[:, :ATTN_W] += jnp.sum(dgr, axis=0, keepdims=True)
        dx, dgr = _rms_bwd(lru_ref[...], gg[:, ATTN_W:ATTN_W + LRU_W], dm[:, ATTN_W:ATTN_W + LRU_W])
        dlru_ref[...] = dx
        dg_ref[:, ATTN_W:ATTN_W + LRU_W] += jnp.sum(dgr, axis=0, keepdims=True)
        dx, dgr = _rms_bwd(s5_ref[...], gg[:, ATTN_W + LRU_W:], dm[:, ATTN_W + LRU_W:])
        ds5_ref[...] = dx
        dg_ref[:, ATTN_W + LRU_W:] += jnp.sum(dgr, axis=0, keepdims=True)

    a = pl.BlockSpec((tm, ATTN_W), lambda i: (i, 0))
    s5s = pl.BlockSpec((tm, S5_W), lambda i: (i, 0))
    full = pl.BlockSpec((tm, D_MODEL), lambda i: (i, 0))
    vec = pl.BlockSpec((1, D_MODEL), lambda i: (0, 0))
    dil = [_dilated_spec(tm, d, ATTN_W) for d in DILATIONS]
    dshape = [jax.ShapeDtypeStruct((s // d, d * ATTN_W), F32) for d in DILATIONS]
    dep_specs, dep_ops = _dep_args(dep)
    vshape = jax.ShapeDtypeStruct((1, D_MODEL), F32)
    res = pl.pallas_call(
        body, out_shape=(jax.ShapeDtypeStruct((s, D_MODEL), F32), vshape, vshape, *dshape,
                         jax.ShapeDtypeStruct((s, LRU_W), F32), jax.ShapeDtypeStruct((s, S5_W), F32), vshape),
        grid=(s // tm,),
        in_specs=[full, full, vec, pl.BlockSpec((D_MODEL, D_MODEL), lambda i: (0, 0)), a, a, s5s, vec] + dep_specs,
        out_specs=(full, vec, vec, *dil, a, s5s, vec), scratch_shapes=[_token_scratch(tm, ATTN_W)], name=name,
        compiler_params=_cparams(("arbitrary",)))(r, dh, ln_g, w_out, o, lru, s5, g, *dep_ops)
    return res[0], res[1], res[2], res[3:6], res[6], res[7], res[8]


def _lru_gate_math(xc, pre_r, pre_i, lam):
    r = _sigmoid(pre_r)
    i = _sigmoid(pre_i)
    log_a = -LRU_C * r * _softplus(-lam)
    a = jnp.exp(log_a)
    u = jnp.sqrt(-_expm1(2.0 * log_a)) * (i * xc)
    return a, u


def _lru_conv(x, prev8, cw, cb):
    y = cb + cw[LRU_CONV - 1:LRU_CONV, :] * x
    for k in range(LRU_CONV - 1):
        y = y + cw[k:k + 1, :] * _shift_down_prev(x, LRU_CONV - 1 - k, prev8)
    return y


def _lru_specs(s):
    xo = 3 * ATTN_W // LANE
    go = xo + LRU_W // LANE
    xr = pl.BlockSpec((s, LANE), lambda j: (0, xo + j))
    gt = pl.BlockSpec((s, LANE), lambda j: (0, go + j))
    cw = pl.BlockSpec((LRU_CONV, LANE), lambda j: (0, j))
    vec = pl.BlockSpec((1, LANE), lambda j: (0, j))
    wbd = pl.BlockSpec((LANE, LANE), lambda j: (j, j))
    col = pl.BlockSpec((s, LANE), lambda j: (0, j))
    return xr, gt, cw, vec, wbd, col


def _lru_fwd(proj, cw, cb, wr, br, wi, bi, lam, name):
    s = proj.shape[0]
    t = SCAN_T

    def body(xr_ref, gt_ref, cw_ref, cb_ref, wr_ref, br_ref, wi_ref, bi_ref, lam_ref, o_ref, xc_ref, a_ref, h_ref):
        cwv, cbv, lamv = cw_ref[...], cb_ref[...], lam_ref[...]
        wrv, wiv, brv, biv = wr_ref[...], wi_ref[...], br_ref[...], bi_ref[...]

        def chunk(c, carry):
            h_c, prev8 = carry
            rows = pl.ds(pl.multiple_of(c * t, t), t)
            x = xr_ref[rows, :]
            xc = _lru_conv(x, prev8, cwv, cbv)
            a, u = _lru_gate_math(xc, _dot(xc, wrv) + brv, _dot(xc, wiv) + biv, lamv)
            h = _scan_chunk(a, u, h_c)
            xc_ref[rows, :] = xc
            a_ref[rows, :] = a
            h_ref[rows, :] = h
            o_ref[rows, :] = h * _gelu(gt_ref[rows, :])
            return h[t - 1:t, :], x[t - 8:t, :]

        lax.fori_loop(0, s // t, chunk, (jnp.zeros((1, LANE), F32), jnp.zeros((8, LANE), F32)))

    xr, gt, cws, vec, wbd, col = _lru_specs(s)
    shp = jax.ShapeDtypeStruct((s, LRU_W), F32)
    return pl.pallas_call(
        body, out_shape=(shp,) * 4, grid=(LRU_W // LANE,),
        in_specs=[xr, gt, cws, vec, wbd, vec, wbd, vec, vec], out_specs=(col,) * 4, name=name,
        compiler_params=_cparams(("parallel",)))(proj, proj, cw, cb, wr, br, wi, bi, lam)


def _lru_bwd(proj, dout, xc_all, a_all, h_all, cw, cb, wr, br, wi, bi, lam, name):
    s = proj.shape[0]
    t = SCAN_T
    nc = s // t

    def body(xr_ref, gt_ref, do_ref, xc_s, a_s, h_s, cw_ref, cb_ref, wr_ref, br_ref, wi_ref, bi_ref, lam_ref,
             dxr_ref, dgt_ref, dcw_ref, dcb_ref, dwr_ref, dbr_ref, dwi_ref, dbi_ref, dlam_ref):
        cwv, cbv, lamv = cw_ref[...], cb_ref[...], lam_ref[...]
        wrv, wiv, brv, biv = wr_ref[...], wi_ref[...], br_ref[...], bi_ref[...]
        z1 = jnp.zeros((1, LANE), F32)
        zw = jnp.zeros((LANE, LANE), F32)

        def bchunk(ci, carry):
            g_next, a_next, dxc_next8, dcw, dcb, dwr, dbr, dwi, dbi, dlam = carry
            c = nc - 1 - ci
            t0 = pl.multiple_of(c * t, t)
            rows = pl.ds(t0, t)
            before = pl.ds(pl.multiple_of(jnp.maximum(t0 - 8, 0), 8), 8)
            has_prev = (c > 0).astype(F32)
            x, gt, do = xr_ref[rows, :], gt_ref[rows, :], do_ref[rows, :]
            xc, a, h = xc_s[rows, :], a_s[rows, :], h_s[rows, :]
            prev8_h = h_s[before, :] * has_prev
            dgt_ref[rows, :] = do * h * _gelu_grad(gt)
            dh = do * _gelu(gt)
            a_plus = _shift_up_next(a, 1, jnp.broadcast_to(a_next, (8, LANE)))
            g = _scan_chunk(a_plus, dh, g_next, reverse=True)
            da = g * _shift_down_prev(h, 1, prev8_h)
            pre_r = _dot(xc, wrv) + brv
            pre_i = _dot(xc, wiv) + biv
            _, vjp = jax.vjp(_lru_gate_math, xc, pre_r, pre_i, lamv)
            dxc, dpre_r, dpre_i, dlam_c = vjp((da, g))
            dxc = dxc + _dot_nt(dpre_r, wrv) + _dot_nt(dpre_i, wiv)
            dx = cwv[LRU_CONV - 1:LRU_CONV, :] * dxc
            dcw_rows = [None] * LRU_CONV
            dcw_rows[LRU_CONV - 1] = jnp.sum(dxc * x, axis=0, keepdims=True)
            for k in range(LRU_CONV - 1):
                dxc_ahead = _shift_up_next(dxc, LRU_CONV - 1 - k, dxc_next8)
                dx = dx + cwv[k:k + 1, :] * dxc_ahead
                dcw_rows[k] = jnp.sum(dxc_ahead * x, axis=0, keepdims=True)
            dxr_ref[rows, :] = dx
            return (g[0:1, :], a[0:1, :], dxc[0:8, :],
                    dcw + jnp.concatenate(dcw_rows, axis=0),
                    dcb + jnp.sum(dxc, axis=0, keepdims=True),
                    dwr + _dot_tn(xc, dpre_r), dbr + jnp.sum(dpre_r, axis=0, keepdims=True),
                    dwi + _dot_tn(xc, dpre_i), dbi + jnp.sum(dpre_i, axis=0, keepdims=True),
                    dlam + dlam_c)

        init = (z1, z1, jnp.zeros((8, LANE), F32), jnp.zeros((LRU_CONV, LANE), F32), z1, zw, z1, zw, z1, z1)
        res = lax.fori_loop(0, nc, bchunk, init)
        dcw_ref[...] = res[3]
        dcb_ref[...] = res[4]
        dwr_ref[...] = res[5]
        dbr_ref[...] = res[6]
        dwi_ref[...] = res[7]
        dbi_ref[...] = res[8]
        dlam_ref[...] = res[9]

    xr, gt, cws, vec, wbd, col = _lru_specs(s)
    vshape = jax.ShapeDtypeStruct((1, LRU_W), F32)
    wshape = jax.ShapeDtypeStruct((LRU_W, LRU_W), F32)
    return pl.pallas_call(
        body,
        out_shape=(jax.ShapeDtypeStruct((s, LRU_W), F32), jax.ShapeDtypeStruct((s, LRU_W), F32),
                   jax.ShapeDtypeStruct((LRU_CONV, LRU_W), F32), vshape, wshape, vshape, wshape, vshape, vshape),
        grid=(LRU_W // LANE,),
        in_specs=[xr, gt, col, col, col, col, cws, vec, wbd, vec, wbd, vec, vec],
        out_specs=(col, col, cws, vec, wbd, vec, wbd, vec, vec), name=name,
        compiler_params=_cparams(("parallel",)))(proj, proj, dout, xc_all, a_all, h_all, cw, cb, wr, br, wi, bi,
                                                 lam)


def _s5_disc_math(a_re, a_im, log_step, bt_re, bt_im):
    step = jnp.exp(log_step)
    dt_re, dt_im = step * a_re, step * a_im
    mag = jnp.exp(dt_re)
    ab_re, ab_im = mag * jnp.cos(dt_im), mag * jnp.sin(dt_im)
    z_re, z_im = ab_re - 1.0, ab_im
    den = a_re * a_re + a_im * a_im
    f_re = (z_re * a_re + z_im * a_im) / den
    f_im = (z_im * a_re - z_re * a_im) / den
    bb_re = f_re * bt_re - f_im * bt_im
    bb_im = f_re * bt_im + f_im * bt_re
    return ab_re, ab_im, bb_re, bb_im


def _s5_disc_fwd(a_re, a_im, log_step, bt_re, bt_im, name):
    def body(ar, ai, ls, br, bi, o1, o2, o3, o4):
        r = _s5_disc_math(ar[...], ai[...], ls[...], br[...], bi[...])
        o1[...], o2[...], o3[...], o4[...] = r

    shp = jax.ShapeDtypeStruct(a_re.shape, F32)
    return pl.pallas_call(body, out_shape=(shp,) * 4, name=name)(a_re, a_im, log_step, bt_re, bt_im)


def _s5_disc_bwd(a_re, a_im, log_step, bt_re, bt_im, cts, name):
    def body(ar, ai, ls, br, bi, c1, c2, c3, c4, o1, o2, o3, o4, o5):
        _, vjp = jax.vjp(_s5_disc_math, ar[...], ai[...], ls[...], br[...], bi[...])
        r = vjp((c1[...], c2[...], c3[...], c4[...]))
        o1[...], o2[...], o3[...], o4[...], o5[...] = r

    shp = jax.ShapeDtypeStruct(a_re.shape, F32)
    return pl.pallas_call(body, out_shape=(shp,) * 5, name=name)(a_re, a_im, log_step, bt_re, bt_im, *cts)


def _s5_u_specs(s):
    uo = (3 * ATTN_W + 2 * LRU_W) // LANE
    return (pl.BlockSpec((s, LANE), lambda j: (0, uo)), pl.BlockSpec((s, LANE), lambda j: (0, uo + 1)))


def _s5_scan_fwd(proj, b_re, b_im, lam_re, lam_im, c_re, c_im, name):
    s = proj.shape[0]
    t = SCAN_T

    def body(u0_ref, u1_ref, bre_ref, bim_ref, lre_ref, lim_ref, cre_ref, cim_ref, xre_ref, xim_ref, y_ref):
        @pl.when(pl.program_id(0) == 0)
        def _():
            y_ref[...] = jnp.zeros_like(y_ref)
        lr, li = lre_ref[...], lim_ref[...]
        consts = _cscan_consts(lr, li, False)
        bre, bim, cre, cim = bre_ref[...], bim_ref[...], cre_ref[...], cim_ref[...]

        def chunk(c, carry):
            cr, ci = carry
            rows = pl.ds(pl.multiple_of(c * t, t), t)
            u = jnp.concatenate([u0_ref[rows, :], u1_ref[rows, :]], axis=1).astype(BF16)
            xr, xi = _cscan_chunk(_dot(u, bre), _dot(u, bim), consts, (cr, ci))
            xre_ref[rows, :] = xr
            xim_ref[rows, :] = xi
            y_ref[rows, :] += _dot(xr, cre) - _dot(xi, cim)
            return xr[t - 1:t, :], xi[t - 1:t, :]

        z = jnp.zeros((1, S5_BLK), F32)
        lax.fori_loop(0, s // t, chunk, (z, z))

    u0, u1 = _s5_u_specs(s)
    bsp = pl.BlockSpec((S5_W, S5_BLK), lambda j: (0, j))
    csp = pl.BlockSpec((S5_BLK, S5_W), lambda j: (j, 0))
    vec = pl.BlockSpec((1, S5_BLK), lambda j: (0, j))
    xsp = pl.BlockSpec((s, S5_BLK), lambda j: (0, j))
    ysp = pl.BlockSpec((s, S5_W), lambda j: (0, 0))
    xshape = jax.ShapeDtypeStruct((s, S5_STATES), F32)
    return pl.pallas_call(
        body, out_shape=(xshape, xshape, jax.ShapeDtypeStruct((s, S5_W), F32)),
        grid=(S5_STATES // S5_BLK,), in_specs=[u0, u1, bsp, bsp, vec, vec, csp, csp],
        out_specs=(xsp, xsp, ysp), name=name,
        compiler_params=_cparams(("arbitrary",)))(proj, proj, b_re, b_im, lam_re, lam_im, c_re, c_im)


def _s5_scan_bwd(proj, dy, du_init, x_re, x_im, b_re, b_im, lam_re, lam_im, c_re, c_im, name):
    s = proj.shape[0]
    t = SCAN_T
    nc = s // t

    def body(u0_ref, u1_ref, dy_ref, dui_ref, xre_ref, xim_ref, bre_ref, bim_ref, lre_ref, lim_ref,
             cre_ref, cim_ref, du_ref, dlr_ref, dli_ref, dbr_ref, dbi_ref, dcr_ref, dci_ref):
        @pl.when(pl.program_id(0) == 0)
        def _():
            du_ref[...] = dui_ref[...]
        mr, mi = lre_ref[...], -lim_ref[...]
        consts = _cscan_consts(mr, mi, True)
        bre, bim, cre, cim = bre_ref[...], bim_ref[...], cre_ref[...], cim_ref[...]
        dbr_ref[...] = jnp.zeros_like(dbr_ref)
        dbi_ref[...] = jnp.zeros_like(dbi_ref)
        dcr_ref[...] = jnp.zeros_like(dcr_ref)
        dci_ref[...] = jnp.zeros_like(dci_ref)

        def chunk(ci_, carry):
            gnr, gni, dlr, dli = carry
            c = nc - 1 - ci_
            t0 = pl.multiple_of(c * t, t)
            rows = pl.ds(t0, t)
            before = pl.ds(pl.multiple_of(jnp.maximum(t0 - 8, 0), 8), 8)
            has_prev = (c > 0).astype(F32)
            dyc = dy_ref[rows, :].astype(BF16)
            u = jnp.concatenate([u0_ref[rows, :], u1_ref[rows, :]], axis=1).astype(BF16)
            gr, gi = _cscan_chunk(_dot_nt(dyc, cre), -_dot_nt(dyc, cim), consts, (gnr, gni), reverse=True)
            xr, xi = xre_ref[rows, :], xim_ref[rows, :]
            xpr = _shift_down_prev(xr, 1, xre_ref[before, :] * has_prev)
            xpi = _shift_down_prev(xi, 1, xim_ref[before, :] * has_prev)
            dlr = dlr + jnp.sum(gr * xpr + gi * xpi, axis=0, keepdims=True)
            dli = dli + jnp.sum(gi * xpr - gr * xpi, axis=0, keepdims=True)
            du_ref[rows, :] += _dot_nt(gr, bre) + _dot_nt(gi, bim)
            dbr_ref[...] += _dot_tn(u, gr)
            dbi_ref[...] += _dot_tn(u, gi)
            dcr_ref[...] += _dot_tn(xr, dyc)
            dci_ref[...] -= _dot_tn(xi, dyc)
            return gr[0:1, :], gi[0:1, :], dlr, dli

        z = jnp.zeros((1, S5_BLK), F32)
        res = lax.fori_loop(0, nc, chunk, (z, z, z, z))
        dlr_ref[...] = res[2]
        dli_ref[...] = res[3]

    u0, u1 = _s5_u_specs(s)
    bsp = pl.BlockSpec((S5_W, S5_BLK), lambda j: (0, j))
    csp = pl.BlockSpec((S5_BLK, S5_W), lambda j: (j, 0))
    vec = pl.BlockSpec((1, S5_BLK), lambda j: (0, j))
    xsp = pl.BlockSpec((s, S5_BLK), lambda j: (0, j))
    ysp = pl.BlockSpec((s, S5_W), lambda j: (0, 0))
    return pl.pallas_call(
        body,
        out_shape=(jax.ShapeDtypeStruct((s, S5_W), F32),
                   jax.ShapeDtypeStruct((1, S5_STATES), F32), jax.ShapeDtypeStruct((1, S5_STATES), F32),
                   jax.ShapeDtypeStruct((S5_W, S5_STATES), F32), jax.ShapeDtypeStruct((S5_W, S5_STATES), F32),
                   jax.ShapeDtypeStruct((S5_STATES, S5_W), F32), jax.ShapeDtypeStruct((S5_STATES, S5_W), F32)),
        grid=(S5_STATES // S5_BLK,),
        in_specs=[u0, u1, ysp, ysp, xsp, xsp, bsp, bsp, vec, vec, csp, csp],
        out_specs=(ysp, vec, vec, bsp, bsp, csp, csp), name=name,
        compiler_params=_cparams(("arbitrary",)))(
            proj, proj, dy, du_init, x_re, x_im, b_re, b_im, lam_re, lam_im, c_re, c_im)


def _s5_out_fwd(proj, y_acc, dvec, w_glu, b_glu, name):
    s = proj.shape[0]
    tm = 512
    uo = (3 * ATTN_W + 2 * LRU_W) // LANE

    def body(u0_ref, u1_ref, y_ref, d_ref, w_ref, b_ref, o_ref, yp_ref):
        u = jnp.concatenate([u0_ref[...], u1_ref[...]], axis=1)
        y = y_ref[...] + d_ref[...] * u
        yp_ref[...] = y
        yg = _gelu(y)
        o_ref[...] = yg * _sigmoid(_dot(yg, w_ref[...]) + b_ref[...])

    u0 = pl.BlockSpec((tm, LANE), lambda i: (i, uo))
    u1 = pl.BlockSpec((tm, LANE), lambda i: (i, uo + 1))
    row = pl.BlockSpec((tm, S5_W), lambda i: (i, 0))
    vec = pl.BlockSpec((1, S5_W), lambda i: (0, 0))
    wsp = pl.BlockSpec((S5_W, S5_W), lambda i: (0, 0))
    shp = jax.ShapeDtypeStruct((s, S5_W), F32)
    return pl.pallas_call(
        body, out_shape=(shp, shp), grid=(s // tm,), in_specs=[u0, u1, row, vec, wsp, vec],
        out_specs=(row, row), name=name,
        compiler_params=_cparams(("parallel",)))(proj, proj, y_acc, dvec, w_glu, b_glu)


def _s5_out_bwd(proj, y_pre, dout, dvec, w_glu, b_glu, name, dep=None):
    s = proj.shape[0]
    tm = 512
    uo = (3 * ATTN_W + 2 * LRU_W) // LANE

    def body(u0_ref, u1_ref, y_ref, do_ref, d_ref, w_ref, b_ref, *rest):
        dy_ref, dud_ref, dd_ref, dw_ref, db_ref = rest[-5:]

        @pl.when(pl.program_id(0) == 0)
        def _():
            dd_ref[...] = jnp.zeros_like(dd_ref)
            dw_ref[...] = jnp.zeros_like(dw_ref)
            db_ref[...] = jnp.zeros_like(db_ref)
        u = jnp.concatenate([u0_ref[...], u1_ref[...]], axis=1)
        y = y_ref[...]
        do = do_ref[...]
        yg = _gelu(y)
        sg = _sigmoid(_dot(yg, w_ref[...]) + b_ref[...])
        dz = do * yg * sg * (1.0 - sg)
        dyg = do * sg + _dot_nt(dz, w_ref[...])
        dy = dyg * _gelu_grad(y)
        dy_ref[...] = dy
        dud_ref[...] = d_ref[...] * dy
        dd_ref[...] += jnp.sum(dy * u, axis=0, keepdims=True)
        dw_ref[...] += _dot_tn(yg, dz)
        db_ref[...] += jnp.sum(dz, axis=0, keepdims=True)

    u0 = pl.BlockSpec((tm, LANE), lambda i: (i, uo))
    u1 = pl.BlockSpec((tm, LANE), lambda i: (i, uo + 1))
    row = pl.BlockSpec((tm, S5_W), lambda i: (i, 0))
    vec = pl.BlockSpec((1, S5_W), lambda i: (0, 0))
    wsp = pl.BlockSpec((S5_W, S5_W), lambda i: (0, 0))
    shp = jax.ShapeDtypeStruct((s, S5_W), F32)
    vshape = jax.ShapeDtypeStruct((1, S5_W), F32)
    dep_specs, dep_ops = _dep_args(dep)
    return pl.pallas_call(
        body, out_shape=(shp, shp, vshape, jax.ShapeDtypeStruct((S5_W, S5_W), F32), vshape),
        grid=(s // tm,), in_specs=[u0, u1, row, row, vec, wsp, vec] + dep_specs,
        out_specs=(row, row, vec, wsp, vec), name=name,
        compiler_params=_cparams(("arbitrary",)))(proj, proj, y_pre, dout, dvec, w_glu, b_glu, *dep_ops)


def _ffn_conv(x, prev8, cw, cb):
    y = cb + cw[FFN_CONV - 1:FFN_CONV, :] * x
    for k in range(FFN_CONV - 1):
        y = y + cw[k:k + 1, :] * _shift_down_prev(x, FFN_CONV - 1 - k, prev8)
    return y


def _ffn_up_act(h, wg, cw, cb, name, dep=None):
    s, d = h.shape
    tm = 512
    tb = 2 * FFN_CB
    nt = D_FF // FFN_CB

    def body(h_ref, wgate_ref, wval_ref, cw_ref, cb_ref, *rest):
        up_ref, y_ref, o_ref, ot_ref, carry = rest[-5:]

        @pl.when(pl.program_id(1) == 0)
        def _():
            carry[...] = jnp.zeros_like(carry)
        hb = h_ref[...].astype(BF16)
        x = jnp.concatenate([_dot(hb, wgate_ref[...]), _dot(hb, wval_ref[...])], axis=1)
        up_ref[...] = x
        y = _ffn_conv(x, carry[...], cw_ref[...], cb_ref[...])
        y_ref[...] = y
        carry[...] = x[tm - 8:tm, :]
        act = _gelu(y[:, :FFN_CB]) * y[:, FFN_CB:]
        o_ref[...] = act.astype(BF16)
        ot_ref[...] = act.T.astype(BF16)

    dep_specs, dep_ops = _dep_args(dep)
    return pl.pallas_call(
        body, out_shape=(jax.ShapeDtypeStruct((s, 2 * D_FF), F32), jax.ShapeDtypeStruct((s, 2 * D_FF), F32),
                         jax.ShapeDtypeStruct((s, D_FF), BF16), jax.ShapeDtypeStruct((D_FF, s), BF16)),
        grid=(nt, s // tm),
        in_specs=[pl.BlockSpec((tm, d), lambda t, i: (i, 0)),
                  pl.BlockSpec((None, d, FFN_CB), lambda t, i: (t, 0, 0)),
                  pl.BlockSpec((None, d, FFN_CB), lambda t, i: (t + nt, 0, 0)),
                  pl.BlockSpec((FFN_CONV, tb), lambda t, i: (0, t)),
                  pl.BlockSpec((1, tb), lambda t, i: (0, t))] + dep_specs,
        out_specs=(pl.BlockSpec((tm, tb), lambda t, i: (i, t)), pl.BlockSpec((tm, tb), lambda t, i: (i, t)),
                   pl.BlockSpec((tm, FFN_CB), lambda t, i: (i, t)), pl.BlockSpec((FFN_CB, tm), lambda t, i: (t, i))),
        scratch_shapes=[pltpu.VMEM((8, tb), F32)], name=name,
        compiler_params=_cparams(("parallel", "arbitrary")))(h, wg, wg, cw, cb, *dep_ops)


def _ffn_bwd(up, y_conv, dr, w_down, wg, cw, name):
    s = up.shape[0]
    d = dr.shape[1]
    tm = 256
    tb = 2 * FFN_CB
    nr = s // tm
    nt = D_FF // FFN_CB

    def body(x_ref, y_ref, dr_ref, wd_ref, wgate_ref, wval_ref, cw_ref,
             dup_ref, dh_ref, dcw_ref, dcb_ref, carry):
        i, t = pl.program_id(0), pl.program_id(1)

        @pl.when(i == 0)
        def _():
            carry[t] = jnp.zeros((8, tb), F32)

        @pl.when(t == 0)
        def _():
            dh_ref[...] = ALPHA * dr_ref[...]
        cwv = cw_ref[...]
        x = x_ref[...]
        dact = _dot_nt(dr_ref[...], wd_ref[...])
        gate, val = y_ref[:, :FFN_CB], y_ref[:, FFN_CB:]
        dy = jnp.concatenate([dact * val * _gelu_grad(gate), dact * _gelu(gate)], axis=1)
        next8 = carry[t]
        carry[t] = dy[0:8, :]
        dx = cwv[FFN_CONV - 1:FFN_CONV, :] * dy
        dcw_rows = [None] * FFN_CONV
        dcw_rows[FFN_CONV - 1] = jnp.sum(dy * x, axis=0, keepdims=True)
        for k in range(FFN_CONV - 1):
            dy_ahead = _shift_up_next(dy, FFN_CONV - 1 - k, next8)
            dx = dx + cwv[k:k + 1, :] * dy_ahead
            dcw_rows[k] = jnp.sum(dy_ahead * x, axis=0, keepdims=True)
        dup = dx.astype(BF16)
        dup_ref[...] = dup
        dh_ref[...] += _dot_nt(dup[:, :FFN_CB], wgate_ref[...]) + _dot_nt(dup[:, FFN_CB:], wval_ref[...])
        dcw_ref[...] = jnp.concatenate(dcw_rows, axis=0)
        dcb_ref[...] = jnp.sum(dy, axis=0, keepdims=True)

    row = lambda i: nr - 1 - i
    return pl.pallas_call(
        body, out_shape=(jax.ShapeDtypeStruct((s, 2 * D_FF), BF16), jax.ShapeDtypeStruct((s, d), F32),
                         jax.ShapeDtypeStruct((nr, FFN_CONV, 2 * D_FF), F32),
                         jax.ShapeDtypeStruct((nr, 1, 2 * D_FF), F32)),
        grid=(nr, nt),
        in_specs=[pl.BlockSpec((tm, tb), lambda i, t: (row(i), t)),
                  pl.BlockSpec((tm, tb), lambda i, t: (row(i), t)),
                  pl.BlockSpec((tm, d), lambda i, t: (row(i), 0)),
                  pl.BlockSpec((FFN_CB, d), lambda i, t: (t, 0)),
                  pl.BlockSpec((None, d, FFN_CB), lambda i, t: (t, 0, 0)),
                  pl.BlockSpec((None, d, FFN_CB), lambda i, t: (t + nt, 0, 0)),
                  pl.BlockSpec((FFN_CONV, tb), lambda i, t: (0, t))],
        out_specs=(pl.BlockSpec((tm, tb), lambda i, t: (row(i), t)),
                   pl.BlockSpec((tm, d), lambda i, t: (row(i), 0)),
                   pl.BlockSpec((None, FFN_CONV, tb), lambda i, t: (row(i), 0, t)),
                   pl.BlockSpec((None, 1, tb), lambda i, t: (row(i), 0, t))),
        scratch_shapes=[pltpu.VMEM((nt, 8, tb), F32)], name=name,
        compiler_params=_cparams(("arbitrary", "arbitrary")))(up, y_conv, dr, w_down, wg, wg, cw)


def _sum_partials(ld_ref):
    gg = ld_ref[0].astype(F32)
    for k in range(1, N_DEV):
        gg = gg + ld_ref[k].astype(F32)
    return gg


def _adam_update(w, g, m, v):
    mn = ADAM_B1 * m + (1.0 - ADAM_B1) * g
    vn = ADAM_B2 * v + (1.0 - ADAM_B2) * (g * g)
    m_hat = mn / (1.0 - ADAM_B1 ** ADAM_STEP)
    v_hat = vn / (1.0 - ADAM_B2 ** ADAM_STEP)
    return -ADAM_LR * (m_hat / (jnp.sqrt(v_hat) + ADAM_EPS) + ADAM_WD * w), mn, vn


def _adamw_many(landed, ws, ms, vs, name):
    n, nl = len(ws), len(landed)

    def body(*refs):
        ld = refs[:nl * n]
        w_refs, m_refs, v_refs = (refs[(nl + k) * n:(nl + k + 1) * n] for k in range(3))
        outs = refs[(nl + 3) * n:]
        for i in range(n):
            for l in range(nl):
                one = slice(l, l + 1)
                gg = _sum_partials(ld[l * n + i])
                outs[i][one] = gg
                outs[n + i][one], outs[2 * n + i][one], outs[3 * n + i][one] = _adam_update(
                    w_refs[i][one], gg, m_refs[i][one], v_refs[i][one])

    vm = pl.BlockSpec(memory_space=pltpu.VMEM)
    shapes = [jax.ShapeDtypeStruct(w.shape, F32) for w in ws] * 4
    res = pl.pallas_call(
        body, out_shape=tuple(shapes), in_specs=[vm] * ((nl + 3) * n), out_specs=tuple([vm] * (4 * n)),
        name=name, compiler_params=_cparams())(*[a for layer in landed for a in layer], *ws, *ms, *vs)
    return res[:n], res[n:2 * n], res[2 * n:3 * n], res[3 * n:]


def _adamw_sum(landed, w, m, v, layer, prev, name):
    _, r, c = landed.shape
    nl = w.shape[0]
    tm = 8
    for cand in (512, 256, 128, 64, 32, 16):
        if r % cand == 0 and N_DEV * cand * c * 4 <= 4 * 1024 * 1024:
            tm = cand
            break

    def body(*refs):
        ld_ref, w_ref, m_ref, v_ref = refs[:4]
        g_ref, d_ref, mo_ref, vo_ref = refs[-4:]
        gg = _sum_partials(ld_ref)
        g_ref[...] = gg
        d_ref[...], mo_ref[...], vo_ref[...] = _adam_update(w_ref[...], gg, m_ref[...], v_ref[...])

    blk = pl.BlockSpec((None, tm, c), lambda i: (layer, i, 0))
    in_specs = [pl.BlockSpec((N_DEV, tm, c), lambda i: (0, i, 0)), blk, blk, blk]
    args = [landed, w, m, v]
    aliases = {}
    if prev is not None:
        in_specs += [pl.BlockSpec(memory_space=pl.ANY)] * 4
        args += list(prev)
        aliases = {4 + k: k for k in range(4)}
    shp = jax.ShapeDtypeStruct((nl, r, c), F32)
    return pl.pallas_call(
        body, out_shape=(shp,) * 4, grid=(r // tm,), in_specs=in_specs, out_specs=(blk,) * 4,
        input_output_aliases=aliases, name=name, compiler_params=_cparams(("parallel",)))(*args)


def _all_gather(shards, name):
    na = len(shards)

    def body(*refs):
        x_refs, out_refs = refs[:na], refs[na:2 * na]
        send_sems, recv_sems, local_sems = refs[2 * na:]
        x, y, c = lax.axis_index("x"), lax.axis_index("y"), lax.axis_index("c")
        me, sibling = (x, y, c), (x, y, 1 - c)
        chips = [(1 - x, y), (x, 1 - y), (1 - x, 1 - y)]

        def copy(a, k, block, to, src=None):
            dst = out_refs[a].at[4 * block[0] + 2 * block[1] + block[2]]
            return pltpu.make_async_remote_copy(
                src_ref=dst if src is None else src, dst_ref=dst,
                send_sem=send_sems.at[7 * a + k], recv_sem=recv_sems.at[7 * a + k],
                device_id=to, device_id_type=pl.DeviceIdType.MESH)

        mine, first, passed = [], [], []
        for a in range(na):
            cp = pltpu.make_async_copy(x_refs[a], out_refs[a].at[4 * x + 2 * y + c], local_sems.at[a])
            cp.start()
            mine.append(cp)
            cps = [copy(a, 0, me, sibling, src=x_refs[a])]
            cps += [copy(a, 1 + j, me, (*chip, c), src=x_refs[a]) for j, chip in enumerate(chips)]
            for cp in cps:
                cp.start()
            first += cps
        for j, chip in enumerate(chips):
            for a in range(na):
                copy(a, 1 + j, (*chip, c), me).wait_recv()
                cp = copy(a, 4 + j, (*chip, c), sibling)
                cp.start()
                passed.append(cp)
        for a in range(na):
            copy(a, 0, sibling, me).wait_recv()
            for j, chip in enumerate(chips):
                copy(a, 4 + j, (*chip, 1 - c), me).wait_recv()
        for cp in first + passed:
            cp.wait_send()
        for cp in mine:
            cp.wait()

    anyspec = pl.BlockSpec(memory_space=pl.ANY)
    return pl.pallas_call(
        body, out_shape=tuple(jax.ShapeDtypeStruct((N_DEV,) + t.shape, t.dtype) for t in shards),
        in_specs=[anyspec] * na, out_specs=tuple([anyspec] * na),
        scratch_shapes=[pltpu.SemaphoreType.DMA((7 * na,)), pltpu.SemaphoreType.DMA((7 * na,)),
                        pltpu.SemaphoreType.DMA((na,))],
        name=name)(*shards)


_HBM = pl.BlockSpec(memory_space=pltpu.HBM)
_SEM = pl.BlockSpec(memory_space=pltpu.SEMAPHORE)
_EFFECT = pltpu.SideEffectType.DATAFLOW_SIDE_EFFECTING


def _exchange_copies(src_refs, land_refs, send_sems, recv_sems, local_sems, gather):
    x, y, c = lax.axis_index("x"), lax.axis_index("y"), lax.axis_index("c")
    me = 4 * x + 2 * y + c
    per_array = send_sems.shape[0] > N_DEV - 1
    local, remote = [], []
    for a, (src, land) in enumerate(zip(src_refs, land_refs)):
        local.append(pltpu.make_async_copy(src if gather else src.at[me], land.at[me],
                                           local_sems.at[a if per_array else 0]))
    for k in range(1, N_DEV):
        px = x ^ ((k >> 2) & 1)
        py = y ^ ((k >> 1) & 1)
        pc = c ^ (k & 1)
        for a, (src, land) in enumerate(zip(src_refs, land_refs)):
            remote.append(pltpu.make_async_remote_copy(
                src_ref=src if gather else src.at[4 * px + 2 * py + pc], dst_ref=land.at[me],
                send_sem=send_sems.at[(7 * a if per_array else 0) + k - 1],
                recv_sem=recv_sems.at[(7 * a if per_array else 0) + k - 1],
                device_id=(px, py, pc), device_id_type=pl.DeviceIdType.MESH))
    return local, remote


def _exchange_start(srcs, gather, name, dep=None):
    na = len(srcs)
    ns = na if na <= 4 else 1
    lands = [lax.empty(((N_DEV,) + t.shape) if gather else t.shape, t.dtype) for t in srcs]

    def body(*refs):
        src_refs, land_refs = refs[:na], refs[na:2 * na]
        nin = 2 * na + (0 if dep is None else 1)
        send_sems, recv_sems, local_sems = refs[nin:nin + 3]
        token = refs[-1]
        local, remote = _exchange_copies(src_refs, land_refs, send_sems, recv_sems, local_sems, gather)
        for cp in local + remote:
            cp.start()
        token[...] = jnp.zeros_like(token)

    dep_specs, dep_ops = _dep_args(dep)
    hbm = lambda t: pltpu.HBM(t.shape, t.dtype)
    out = pl.pallas_call(
        body, name=name,
        out_shape=(pltpu.SemaphoreType.DMA((7 * ns,)), pltpu.SemaphoreType.DMA((7 * ns,)),
                   pltpu.SemaphoreType.DMA((ns,)), *[hbm(t) for t in srcs], *[hbm(t) for t in lands],
                   jax.ShapeDtypeStruct((8, LANE), F32)),
        in_specs=[_HBM] * (2 * na) + dep_specs,
        out_specs=(_SEM, _SEM, _SEM, *[_HBM] * (2 * na), pl.BlockSpec(memory_space=pltpu.VMEM)),
        input_output_aliases={i: 3 + i for i in range(2 * na)},
        compiler_params=pltpu.CompilerParams(has_side_effects=_EFFECT),
    )(*[pltpu.with_memory_space_constraint(t, pltpu.HBM) for t in srcs + lands], *dep_ops)
    return (out[:3], out[3:3 + na], out[3 + na:3 + 2 * na]), out[-1]


def _exchange_wait(handle, gather, after, name):
    sems, srcs, lands = handle
    na = len(srcs)

    def body(*refs):
        src_refs, land_refs = refs[:na], refs[na:2 * na]
        send_sems, recv_sems, local_sems = refs[2 * na:2 * na + 3]
        local, remote = _exchange_copies(src_refs, land_refs, send_sems, recv_sems, local_sems, gather)
        for cp in remote:
            cp.wait_send()
            cp.wait_recv()
        for cp in local:
            cp.wait()

    hbm = lambda t: pltpu.HBM(t.shape, t.dtype)
    out = pl.pallas_call(
        body, name=name, out_shape=(*[hbm(t) for t in srcs], *[hbm(t) for t in lands]),
        in_specs=[_HBM] * (2 * na) + [_SEM] * 3 + [pl.BlockSpec(memory_space=pl.ANY)],
        out_specs=tuple([_HBM] * (2 * na)), input_output_aliases={i: i for i in range(2 * na)},
        compiler_params=pltpu.CompilerParams(has_side_effects=_EFFECT),
    )(*srcs, *lands, *sems, after)
    return out[na:]


def _block_diag(w):
    h, a, b = w.shape
    eye = jnp.eye(h, dtype=w.dtype)
    return (w[:, :, None, :] * eye[:, None, :, None]).reshape(h * a, h * b)


def _block_diag_extract(m, h):
    a, b = m.shape[0] // h, m.shape[1] // h
    return jnp.stack([m[i * a:(i + 1) * a, i * b:(i + 1) * b] for i in range(h)], axis=0)


def _block_diag_take(m, h):
    a, b = m.shape[0] // h, m.shape[1] // h
    eye = jnp.eye(h, dtype=m.dtype)
    return (m.reshape(h, a, h, b) * eye[:, None, :, None]).sum(axis=2)


def _ffn_interleave(w):
    lead = w.shape[:-1]
    nb = D_FF // FFN_CB
    return jnp.swapaxes(w.reshape(*lead, 2, nb, FFN_CB), -3, -2).reshape(*lead, 2 * D_FF)


def _ffn_deinterleave(w):
    lead = w.shape[:-1]
    nb = D_FF // FFN_CB
    return jnp.swapaxes(w.reshape(*lead, nb, 2, FFN_CB), -3, -2).reshape(*lead, 2 * D_FF)


def _gather_full(gathered, axis):
    shape = list(gathered.shape[1:])
    shape[axis] *= N_DEV
    return jnp.moveaxis(gathered, 0, axis).reshape(shape)


def _scatter_blocks(full, axis):
    shape = list(full.shape)
    shape[axis:axis + 1] = [N_DEV, shape[axis] // N_DEV]
    return jnp.moveaxis(full.reshape(shape), axis, 0)


def _pad_to(flat, mult):
    pad = (-flat.shape[-1]) % mult
    if pad:
        flat = jnp.concatenate([flat, jnp.zeros(flat.shape[:-1] + (pad,), flat.dtype)], axis=-1)
    return flat


def _layer_fwd(h_in, h_in_t, w, cos, sin, l, dep, get_ffn, target=None):
    tag = "l%d_" % l
    proj, qkv, h_t = _proj_rope(h_in, w['w_in'], cos, sin, tag + "proj_rope", dep=dep,
                                transposed=h_in_t is None)
    h_in_t = h_t if h_in_t is None else h_in_t
    outs, lses = [], []
    for d, qv in zip(DILATIONS, qkv):
        o, ls = _attn_fwd(qv, d, tag + "attn_d%d" % d)
        outs.append(o)
        lses.append(ls)
    lru, *lru_saved = _lru_fwd(proj, w['lru_conv_w'], w['lru_conv_b'], w['lru_wr'], w['lru_br'], w['lru_wi'],
                               w['lru_bi'], w['lru_lambda'], tag + "lru")
    x_re, x_im, y_acc = _s5_scan_fwd(proj, w['s5_bb_re'], w['s5_bb_im'], w['s5_lam_re'], w['s5_lam_im'],
                                     w['s5_cc_re'], w['s5_cc_im'], tag + "s5_scan")
    s5, y_pre = _s5_out_fwd(proj, y_acc, w['s5_d'], w['s5_w_glu'], w['s5_b_glu'], tag + "s5_out")
    w_out = get_ffn(l, s5, 'out')
    if w_out is not None:
        w['w_out'] = w_out
    mixed_t, r1, h1, h1_t, attn_o, attn_lse = _mix_fwd(outs, lses, lru, s5, w['mix_norm_g'], h_in, w['w_out'],
                                                       w['ln1_g'], w['ln1_b'], tag + "mix_out_ln1")
    w['w_up_g'], w['w_down'], ffn_dep = get_ffn(l, h1, 'ffn')
    up, y_conv, act, act_t = _ffn_up_act(h1, w['w_up_g'], w['ffn_conv_w'], w['ffn_conv_b'], tag + "up_act",
                                         dep=ffn_dep)
    r2, out_a, out_b = _proj_ln(act, w['w_down'], h1, w['ln2_g'], w['ln2_b'], tag + "down_ln2", target=target)
    saved = dict(h_in_t=h_in_t, proj=proj, qkv=qkv, lru=lru, lru_saved=lru_saved, x_re=x_re, x_im=x_im,
                 y_pre=y_pre, s5=s5, mixed_t=mixed_t, attn_o=attn_o, attn_lse=attn_lse, r1=r1, h1_t=h1_t, up=up,
                 act_t=act_t, r2=r2, y_conv=y_conv)
    return out_a, out_b, saved


def _layer_bwd_ffn(dh2, sv, w, l, dep=None):
    tag = "l%d_" % l
    g = {}
    dr2, g['ln2_g'], g['ln2_b'] = _ln_bwd(sv['r2'], dh2, w['ln2_g'], tag + "ln2_bwd", dep=dep)
    g['w_down'] = _mm_dw(sv['act_t'], dr2, 1024, D_MODEL, 1024, tag + "down_dw", _grad_dtype(l))
    dup, dh1, dcw_parts, dcb_parts = _ffn_bwd(sv['up'], sv['y_conv'], dr2, w['w_down'], w['w_up_g'],
                                              w['ffn_conv_w'], tag + "ffn_bwd")
    g['ffn_conv_w'] = dcw_parts.sum(axis=0)
    g['ffn_conv_b'] = dcb_parts.sum(axis=0)
    g['w_up_g'] = _mm_up_dw(sv['h1_t'], dup, tag + "up_dw", _grad_dtype(l))
    return dh1, g


def _layer_bwd_mix(dh1, sv, w, cos, sin, l, dep, g_ffn, after_out_grad, after_small_grads, after_in_grad):
    tag = "l%d_" % l
    g = {}
    dr1, g['ln1_g'], g['ln1_b'], d_o, dlru, ds5, g['mix_norm_g'] = _mix_bwd(
        sv['r1'], dh1, w['ln1_g'], w['w_out'], sv['attn_o'][0], sv['lru'], sv['s5'], w['mix_norm_g'],
        tag + "ln1_mix_bwd", dep=dep)
    g['w_out'] = _mm_dw(sv['mixed_t'], dr1, 1024, D_MODEL, 1024, tag + "out_dw", _grad_dtype(l))
    dy, dud, g['s5_d'], g['s5_w_glu'], g['s5_b_glu'] = _s5_out_bwd(
        sv['proj'], sv['y_pre'], ds5, w['s5_d'], w['s5_w_glu'], w['s5_b_glu'], tag + "s5_out_bwd",
        dep=after_out_grad(l, g['w_out']))
    du, g['s5_lam_re'], g['s5_lam_im'], g['s5_bb_re'], g['s5_bb_im'], g['s5_cc_re'], g['s5_cc_im'] = \
        _s5_scan_bwd(sv['proj'], dy, dud, sv['x_re'], sv['x_im'], w['s5_bb_re'], w['s5_bb_im'],
                     w['s5_lam_re'], w['s5_lam_im'], w['s5_cc_re'], w['s5_cc_im'], tag + "s5_scan_bwd")
    (dxr, dgate, g['lru_conv_w'], g['lru_conv_b'], g['lru_wr'], g['lru_br'], g['lru_wi'], g['lru_bi'],
     g['lru_lambda']) = _lru_bwd(sv['proj'], dlru, *sv['lru_saved'], w['lru_conv_w'], w['lru_conv_b'], w['lru_wr'],
                                 w['lru_br'], w['lru_wi'], w['lru_bi'], w['lru_lambda'], tag + "lru_bwd")
    token = after_small_grads(l, _finish_layer_grads({**g_ffn, **g}, w, l))
    dqkv = [_attn_bwd(sv['qkv'][b], sv['attn_o'][b], d_o[b], sv['attn_lse'][b], d, tag + "attn_bwd_d%d" % d,
                      dep=token if b == 0 else None)
            for b, d in enumerate(DILATIONS)]
    dproj = _dproj_assemble(dqkv, dxr, dgate, du, cos, sin, tag + "dproj")
    g_in = _mm_dw(sv['h_in_t'], dproj, 1024, D_IN, 1024, tag + "in_dw", _grad_dtype(l))
    return _mm_nt(dproj, w['w_in'], 512, D_MODEL, tag + "in_dx", add=dr1, add_scale=ALPHA,
                  dep=after_in_grad(l, g_in))


def _s5_rep(a):
    return jnp.repeat(a, S5_C, axis=0)


def _prepare_layer(p, l):
    w = {}
    for n in ('w_in', 'w_out', 's5_w_glu'):
        if n in p:
            w[n] = p[n].astype(BF16)
    w['ffn_conv_w'] = _ffn_interleave(p['ffn_conv_w'])
    w['ffn_conv_b'] = _ffn_interleave(p['ffn_conv_b'])[None, :]
    w['lru_conv_w'] = p['lru_conv_w']
    for n in ('lru_conv_b', 'lru_br', 'lru_bi', 'lru_lambda', 's5_b_glu', 'mix_norm_g',
              'ln1_g', 'ln1_b', 'ln2_g', 'ln2_b'):
        w[n] = p[n][None, :]
    w['lru_wr'] = _block_diag(p['lru_wr']).astype(BF16)
    w['lru_wi'] = _block_diag(p['lru_wi']).astype(BF16)
    w['s5_d'] = p['s5_d'].reshape(1, S5_W)
    disc_in = (_s5_rep(p['s5_a_re']), _s5_rep(p['s5_a_im']),
               _s5_rep(jnp.broadcast_to(p['s5_log_step'][:, None], (S5_G, S5_P))),
               jnp.swapaxes(p['s5_b_re'], 1, 2).reshape(S5_W, S5_P),
               jnp.swapaxes(p['s5_b_im'], 1, 2).reshape(S5_W, S5_P))
    ab_re, ab_im, bb_re, bb_im = _s5_disc_fwd(*disc_in, "l%d_s5_disc" % l)
    w['s5_disc_in'] = disc_in
    w['s5_lam_re'] = ab_re.reshape(S5_G, S5_C, S5_P)[:, 0, :].reshape(1, S5_STATES)
    w['s5_lam_im'] = ab_im.reshape(S5_G, S5_C, S5_P)[:, 0, :].reshape(1, S5_STATES)
    w['s5_bb_re'] = _block_diag(bb_re.reshape(S5_G, S5_C, S5_P)).astype(BF16)
    w['s5_bb_im'] = _block_diag(bb_im.reshape(S5_G, S5_C, S5_P)).astype(BF16)
    w['s5_cc_re'] = _block_diag(jnp.swapaxes(p['s5_c_re'], 1, 2)).astype(BF16)
    w['s5_cc_im'] = _block_diag(jnp.swapaxes(p['s5_c_im'], 1, 2)).astype(BF16)
    return w


def _finish_layer_grads(g, w, l):
    out = {}
    for n in ('s5_w_glu', 'lru_conv_w'):
        out[n] = g[n]
    out['ffn_conv_w'] = _ffn_deinterleave(g['ffn_conv_w'])
    out['ffn_conv_b'] = _ffn_deinterleave(g['ffn_conv_b'])[0]
    for n in ('lru_conv_b', 'lru_br', 'lru_bi', 'lru_lambda', 's5_b_glu', 'mix_norm_g',
              'ln1_g', 'ln1_b', 'ln2_g', 'ln2_b'):
        out[n] = g[n][0]
    out['lru_wr'] = _block_diag_extract(g['lru_wr'], LRU_W // HEAD)
    out['lru_wi'] = _block_diag_extract(g['lru_wi'], LRU_W // HEAD)
    out['s5_d'] = g['s5_d'].reshape(S5_G, S5_C)
    out['s5_c_re'] = jnp.swapaxes(_block_diag_take(g['s5_cc_re'], S5_G), 1, 2)
    out['s5_c_im'] = jnp.swapaxes(_block_diag_take(g['s5_cc_im'], S5_G), 1, 2)
    rep = lambda v: _s5_rep(v.reshape(S5_G, S5_P)) * (1.0 / S5_C)
    cts = (rep(g['s5_lam_re']), rep(g['s5_lam_im']),
           _block_diag_take(g['s5_bb_re'], S5_G).reshape(S5_W, S5_P),
           _block_diag_take(g['s5_bb_im'], S5_G).reshape(S5_W, S5_P))
    da_re, da_im, dls, dbt_re, dbt_im = _s5_disc_bwd(*w['s5_disc_in'], cts, "l%d_s5_disc_bwd" % l)
    out['s5_a_re'] = da_re.reshape(S5_G, S5_C, S5_P).sum(axis=1)
    out['s5_a_im'] = da_im.reshape(S5_G, S5_C, S5_P).sum(axis=1)
    out['s5_log_step'] = dls.reshape(S5_G, S5_C * S5_P).sum(axis=1)
    out['s5_b_re'] = jnp.swapaxes(dbt_re.reshape(S5_G, S5_C, S5_P), 1, 2)
    out['s5_b_im'] = jnp.swapaxes(dbt_im.reshape(S5_G, S5_C, S5_P), 1, 2)
    return out


def _run_step(x, target, get_layer, get_ffn, on_loss, after_ffn_grads, after_out_grad, after_small_grads,
              after_in_grad):
    cos, sin = _rope_tables(x.shape[0])
    h, h_t = x, None
    ws, saved = [], []
    for l in range(DEPTH):
        p, dep = get_layer(l, h)
        ws.append(_prepare_layer(p, l))
        h, h_t, sv = _layer_fwd(h, h_t, ws[l], cos, sin, l, dep, get_ffn, target if l == DEPTH - 1 else None)
        saved.append(sv)
    dh, loss_vec = h, h_t
    on_loss(loss_vec)
    dep = None
    for l in reversed(range(DEPTH)):
        dh1, g = _layer_bwd_ffn(dh, saved[l], ws[l], l, dep)
        dep = after_ffn_grads(l, g)
        dh = _layer_bwd_mix(dh1, saved[l], ws[l], cos, sin, l, dep, g, after_out_grad, after_small_grads,
                            after_in_grad)
        dep = None
    return loss_vec, dh


def _local_step(x, target, layers):
    grads = [{} for _ in range(DEPTH)]

    def ffn(l, after, part):
        if part == 'out':
            return None
        return layers[l]['w_up_g'].astype(BF16), layers[l]['w_down'].astype(BF16), None

    def keep_ffn(l, g):
        grads[l].update(w_up_g=g['w_up_g'], w_down=g['w_down'])

    def keep_small(l, g):
        grads[l].update(g)

    loss, dx = _run_step(x, target, lambda l, h: (layers[l], None), ffn, lambda row: None, keep_ffn,
                         lambda l, g: grads[l].update(w_out=g), keep_small, lambda l, g: grads[l].update(w_in=g))
    return loss[0, 0], dx, grads


def kernel(x, w_in, lru_conv_w, lru_conv_b, lru_wr, lru_br, lru_wi, lru_bi, lru_lambda, s5_a_re, s5_a_im, s5_b_re, s5_b_im, s5_c_re, s5_c_im, s5_d, s5_log_step, s5_w_glu, s5_b_glu, mix_norm_g, w_out, ln1_g, ln1_b, w_up, ffn_conv_w, ffn_conv_b, w_down, ln2_g, ln2_b, loss_target, m_w_in, m_lru_conv_w, m_lru_conv_b, m_lru_wr, m_lru_br, m_lru_wi, m_lru_bi, m_lru_lambda, m_s5_a_re, m_s5_a_im, m_s5_b_re, m_s5_b_im, m_s5_c_re, m_s5_c_im, m_s5_d, m_s5_log_step, m_s5_w_glu, m_s5_b_glu, m_mix_norm_g, m_w_out, m_ln1_g, m_ln1_b, m_w_up, m_ffn_conv_w, m_ffn_conv_b, m_w_down, m_ln2_g, m_ln2_b, v_w_in, v_lru_conv_w, v_lru_conv_b, v_lru_wr, v_lru_br, v_lru_wi, v_lru_bi, v_lru_lambda, v_s5_a_re, v_s5_a_im, v_s5_b_re, v_s5_b_im, v_s5_c_re, v_s5_c_im, v_s5_d, v_s5_log_step, v_s5_w_glu, v_s5_b_glu, v_mix_norm_g, v_w_out, v_ln1_g, v_ln1_b, v_w_up, v_ffn_conv_w, v_ffn_conv_b, v_w_down, v_ln2_g, v_ln2_b):
    args = locals()
    wl = {n: args[n] for n in WEIGHTS}
    ml = {n: args['m_' + n] for n in WEIGHTS}
    vl = {n: args['v_' + n] for n in WEIGHTS}

    small_sizes = [int(wl[n].size) for n in SMALL_SHARDED]
    small_flat = _pad_to(jnp.concatenate([wl[n].reshape(-1) for n in SMALL_SHARDED]), 8 * 1024)
    small_all, w_in0 = _all_gather([small_flat.reshape(-1, 1024), wl['w_in'][0].astype(BF16)], "gather_first")
    small_all = small_all.reshape(N_DEV, -1)
    small_full, off = {}, 0
    for n, sz in zip(SMALL_SHARDED, small_sizes):
        small_full[n] = _gather_full(small_all[:, off:off + sz].reshape((N_DEV,) + wl[n].shape), SHARD_AXIS[n])
        off += sz
    def mixer_params(l, g_in, g_out):
        p = {n: wl[n][l] for n in REPLICATED}
        p.update({n: small_full[n][l] for n in SMALL_SHARDED})
        p['w_in'] = _gather_full(g_in, 1)
        if g_out is not None:
            p['w_out'] = g_out.reshape(D_MODEL, D_MODEL)
        return p

    mix_names, ffn_names = ('w_in', 'w_out'), ('w_up', 'w_down')
    shards = lambda names, l: [wl[n][l].astype(BF16) for n in names]
    gathers = {}
    gathers[0, 'out'], token = _exchange_start(shards(('w_out',), 0), True, "gather_out_l0_start", dep=w_in0)
    gathers[0, 'ffn'], rest0_token = _exchange_start(shards(ffn_names, 0), True, "gather_ffn_l0_start", dep=token)

    def get_layer(l, h):
        if l == 0:
            return mixer_params(0, w_in0, None), rest0_token
        return mixer_params(1, *_exchange_wait(gathers[1, 'mix'], True, h, "gather_mix_l1_wait")), None

    def get_ffn(l, after, part):
        if part == 'out':
            if l > 0:
                return None
            g_out, = _exchange_wait(gathers[0, 'out'], True, after, "gather_out_l0_wait")
            return g_out.reshape(D_MODEL, D_MODEL)
        g_up, g_down = _exchange_wait(gathers[l, 'ffn'], True, after, "gather_ffn_l%d_wait" % l)
        token = None
        if l == 0:
            gathers[1, 'mix'], token = _exchange_start(shards(mix_names, 1), True, "gather_mix_l1_start", dep=g_up)
            gathers[1, 'ffn'], token = _exchange_start(shards(ffn_names, 1), True, "gather_ffn_l1_start", dep=token)
        return g_up, g_down.reshape(D_FF, D_MODEL), token

    scatters = {}

    def after_ffn_grads(l, g):
        send = [g['w_up_g'], g['w_down'].reshape(N_DEV, D_FF // N_DEV, D_MODEL)]
        scatters[l, 'ffn'], token = _exchange_start(send, False, "scatter_ffn_l%d_start" % l)
        return token

    def after_out_grad(l, g_out):
        send = [g_out.reshape(N_DEV, D_MODEL // N_DEV, D_MODEL)]
        scatters[l, 'out'], token = _exchange_start(send, False, "scatter_out_l%d_start" % l)
        return token

    def after_in_grad(l, g_in):
        scatters[l, 'in'], token = _exchange_start([_scatter_blocks(g_in, 1)], False, "scatter_in_l%d_start" % l)
        return token

    def after_small_grads(l, g):
        rep = [g[n][None] for n in REPLICATED]
        if l == DEPTH - 1:
            rep.append(loss_rows[0][None])
        shd = [_scatter_blocks(g[n], SHARD_AXIS[n] - 1)[:, None] for n in SMALL_SHARDED]
        scatters[l, 'rep'], token = _exchange_start(rep, True, "gather_rep_grads_l%d_start" % l)
        scatters[l, 'small'], token = _exchange_start(shd, False, "scatter_small_l%d_start" % l, dep=token)
        return token

    loss_rows = []
    _, grad_x = _run_step(x[0], loss_target[0], get_layer, get_ffn, loss_rows.append, after_ffn_grads,
                          after_out_grad, after_small_grads, after_in_grad)

    results = {}
    big_prev = {n: None for n in BIG}

    def finish_big(l, part, names, after):
        landed = _exchange_wait(scatters[l, part], False, after, "scatter_%s_l%d_wait" % (part, l))
        for n, ld in zip(names, landed):
            big_prev[n] = _adamw_sum(ld, wl[n], ml[n], vl[n], l, big_prev[n], "adamw_%s_l%d" % (n, l))

    for l, part, names in ((1, 'ffn', ffn_names), (1, 'out', ('w_out',)), (1, 'in', ('w_in',)),
                           (0, 'ffn', ffn_names), (0, 'out', ('w_out',))):
        finish_big(l, part, names, grad_x)

    kinds = ('grad', 'delta', 'm', 'v')
    landed = []
    for l in range(DEPTH):
        rep = list(_exchange_wait(scatters[l, 'rep'], True, grad_x, "gather_rep_grads_l%d_wait" % l))
        if l == DEPTH - 1:
            loss = jnp.sum(rep.pop()[:, 0, 0, 0])
        shd = list(_exchange_wait(scatters[l, 'small'], False, grad_x, "scatter_small_l%d_wait" % l))
        landed.append(dict(zip(REPLICATED + SMALL_SHARDED, rep + shd)))
    matrices = ['lru_wr', 'lru_wi', 's5_a_re', 's5_a_im', 's5_c_re', 's5_c_im', 's5_d']
    widest = ['s5_b_re', 's5_b_im']
    vectors = [n for n in REPLICATED + SMALL_SHARDED if n not in matrices + widest]
    last = None
    for tag, names in (("vectors", vectors), ("matrices", matrices), ("s5_b", widest)):
        res = _adamw_many([[landed[l][n] for n in names] for l in range(DEPTH)], [wl[n] for n in names],
                          [ml[n] for n in names], [vl[n] for n in names], "adamw_" + tag)
        for kind, arrs in zip(kinds, res):
            for n, a in zip(names, arrs):
                results[kind, n] = a
        last = res[0][0]
    finish_big(0, 'in', ('w_in',), last)
    for n in BIG:
        results['grad', n], results['delta', n], results['m', n], results['v', n] = big_prev[n]

    out = [loss, grad_x[None]]
    for kind in kinds:
        out.extend(results[kind, n] for n in WEIGHTS)
    return tuple(out)
```

```python
import math

import jax
import jax.numpy as jnp
from jax import lax
from jax.experimental import pallas as pl
from jax.experimental.pallas import tpu as pltpu

F32 = jnp.float32
BF16 = jnp.bfloat16

N_DEV = 8
DEPTH = 2
D_MODEL = 1024
ATTN_W = 384
LRU_W = 384
S5_W = 256
D_IN = 2176
D_FF = 3072
HEAD = 64
ATTN_BLK = 128
ATTN_TILE = 1024
DILATIONS = (1, 4, 16)
S5_G = 16
S5_P = 64
S5_C = 16
S5_STATES = S5_G * S5_P
LRU_C = 8.0
LRU_CONV = 4
FFN_CONV = 3
ROPE_THETA = 10000.0
ALPHA = (2 * DEPTH) ** 0.25
LN_EPS = 1e-5
RMS_EPS = 1e-6
ADAM_LR, ADAM_B1, ADAM_B2, ADAM_EPS, ADAM_WD, ADAM_STEP = 0.001, 0.9, 0.999, 1e-8, 0.01, 10

LANE = 128
SCAN_T = 256
S5_BLK = 256
FFN_CB = 2 * D_FF // N_DEV
VMEM_LIMIT = 56 * 1024 * 1024

WEIGHTS = ['w_in', 'lru_conv_w', 'lru_conv_b', 'lru_wr', 'lru_br', 'lru_wi', 'lru_bi', 'lru_lambda',
           's5_a_re', 's5_a_im', 's5_b_re', 's5_b_im', 's5_c_re', 's5_c_im', 's5_d', 's5_log_step',
           's5_w_glu', 's5_b_glu', 'mix_norm_g', 'w_out', 'ln1_g', 'ln1_b', 'w_up', 'ffn_conv_w',
           'ffn_conv_b', 'w_down', 'ln2_g', 'ln2_b']
SHARD_AXIS = {'w_in': 2, 'lru_conv_w': 2, 's5_w_glu': 1, 'w_out': 1, 'w_up': 2, 'ffn_conv_w': 2, 'w_down': 1}
BIG = ['w_in', 'w_out', 'w_up', 'w_down']
SMALL_SHARDED = ['lru_conv_w', 'ffn_conv_w', 's5_w_glu']
REPLICATED = [n for n in WEIGHTS if n not in SHARD_AXIS]


def _cparams(sem=None):
    return pltpu.CompilerParams(dimension_semantics=sem, vmem_limit_bytes=VMEM_LIMIT)


def _grad_dtype(l):
    return BF16 if l == 0 else F32


def _ffn_dev(jb):
    return jb // 2 + (N_DEV // 2) * (jb % 2)


def _gelu(x):
    c = math.sqrt(2.0 / math.pi)
    t = jnp.tanh(c * (x + 0.044715 * (x * x * x)))
    return 0.5 * x * (1.0 + t)


def _gelu_grad(x):
    c = math.sqrt(2.0 / math.pi)
    x2 = x * x
    t = jnp.tanh(c * (x + 0.044715 * (x2 * x)))
    return 0.5 * (1.0 + t) + 0.5 * x * (1.0 - t * t) * (c * (1.0 + 3.0 * 0.044715 * x2))


def _sigmoid(x):
    return 1.0 / (1.0 + jnp.exp(-x))


def _log1p(x):
    u = 1.0 + x
    d = u - 1.0
    return jnp.where(d == 0.0, x, jnp.log(u) * (x / jnp.where(d == 0.0, 1.0, d)))


def _softplus(x):
    return jnp.maximum(x, 0.0) + _log1p(jnp.exp(-jnp.abs(x)))


def _expm1(x):
    return jnp.tanh(0.5 * x) * (jnp.exp(x) + 1.0)


def _dot(a, b):
    return jnp.dot(a.astype(BF16), b.astype(BF16), preferred_element_type=F32)


def _dot_nt(a, b):
    return lax.dot_general(a.astype(BF16), b.astype(BF16), (((1,), (1,)), ((), ())),
                           preferred_element_type=F32)


def _dot_tn(a, b):
    return lax.dot_general(a.astype(BF16), b.astype(BF16), (((0,), (0,)), ((), ())),
                           preferred_element_type=F32)


def _rows(shape):
    return lax.broadcasted_iota(jnp.int32, shape, 0)


def _shift_down_prev(x, s, prev8):
    if s == 0:
        return x
    t, l = x.shape
    r = pltpu.roll(x, s, axis=0)
    pr = pltpu.roll(prev8, s, axis=0)
    pad = jnp.concatenate([pr, jnp.zeros((t - 8, l), x.dtype)], axis=0)
    return jnp.where(_rows(x.shape) < s, pad, r)


def _shift_up_next(x, s, next8):
    if s == 0:
        return x
    t, l = x.shape
    r = pltpu.roll(x, t - s, axis=0)
    nx = pltpu.roll(next8, 8 - s, axis=0)
    pad = jnp.concatenate([jnp.zeros((t - 8, l), x.dtype), nx], axis=0)
    return jnp.where(_rows(x.shape) >= t - s, pad, r)


SUB = 8


def _tile_shift(x, s, fill, reverse):
    t = x.shape[0]
    pos = _rows(x.shape) & (SUB - 1)
    if reverse:
        return jnp.where(pos < SUB - s, pltpu.roll(x, t - s, axis=0), fill)
    return jnp.where(pos >= s, pltpu.roll(x, s, axis=0), fill)


def _scan_chunk(a, x, carry, reverse=False):
    s = 1
    while s < SUB:
        x = x + a * _tile_shift(x, s, 0.0, reverse)
        a = a * _tile_shift(a, s, 1.0, reverse)
        s *= 2
    nv = x.shape[0] // SUB
    out = [None] * nv
    for v in (reversed(range(nv)) if reverse else range(nv)):
        rows = slice(v * SUB, (v + 1) * SUB)
        out[v] = x[rows, :] + a[rows, :] * carry
        carry = out[v][0:1, :] if reverse else out[v][SUB - 1:SUB, :]
    return jnp.concatenate(out, axis=0)


def _cmul(ar, ai, br, bi):
    return ar * br - ai * bi, ar * bi + ai * br


def _cscan_consts(lr, li, reverse):
    pows = [(lr, li)]
    for _ in range(2):
        pows.append(_cmul(*pows[-1], *pows[-1]))
    rows = [(lr, li)]
    for _ in range(SUB - 1):
        rows.append(_cmul(*rows[-1], lr, li))
    if reverse:
        rows = rows[::-1]
    return pows, (jnp.concatenate([r for r, _ in rows], axis=0), jnp.concatenate([i for _, i in rows], axis=0))


def _cscan_chunk(xr, xi, consts, carry, reverse=False):
    pows, (p8r, p8i) = consts
    s = 1
    for pr, pi in pows:
        sr = _tile_shift(xr, s, 0.0, reverse)
        si = _tile_shift(xi, s, 0.0, reverse)
        xr, xi = xr + pr * sr - pi * si, xi + pr * si + pi * sr
        s *= 2
    nv = xr.shape[0] // SUB
    out_r, out_i = [None] * nv, [None] * nv
    cr, ci = carry
    for v in (reversed(range(nv)) if reverse else range(nv)):
        rows = slice(v * SUB, (v + 1) * SUB)
        out_r[v] = xr[rows, :] + p8r * cr - p8i * ci
        out_i[v] = xi[rows, :] + p8r * ci + p8i * cr
        edge = slice(0, 1) if reverse else slice(SUB - 1, SUB)
        cr, ci = out_r[v][edge, :], out_i[v][edge, :]
    return jnp.concatenate(out_r, axis=0), jnp.concatenate(out_i, axis=0)


def _dep_args(dep):
    return ([], []) if dep is None else ([pl.BlockSpec(memory_space=pl.ANY)], [dep])


def _mm_nt(a, w, tm, tn, name, add=None, add_scale=1.0, dep=None):
    m, k = a.shape
    n = w.shape[0]

    def body(a_ref, w_ref, *rest):
        o_ref = rest[-1]
        if add is None:
            o_ref[...] = _dot_nt(a_ref[...], w_ref[...])
        else:
            o_ref[...] = _dot_nt(a_ref[...], w_ref[...]) + add_scale * rest[0][...]

    in_specs = [pl.BlockSpec((tm, k), lambda j, i: (i, 0)), pl.BlockSpec((tn, k), lambda j, i: (j, 0))]
    args = [a, w]
    if add is not None:
        in_specs.append(pl.BlockSpec((tm, tn), lambda j, i: (i, j)))
        args.append(add)
    dep_specs, dep_ops = _dep_args(dep)
    return pl.pallas_call(
        body, out_shape=jax.ShapeDtypeStruct((m, n), F32), grid=(n // tn, m // tm),
        in_specs=in_specs + dep_specs, out_specs=pl.BlockSpec((tm, tn), lambda j, i: (i, j)), name=name,
        compiler_params=_cparams(("parallel", "parallel")))(*args, *dep_ops)


def _mm_dw(at, b, tm, tn, ts, name, out_dtype=F32):
    m, s = at.shape
    n = b.shape[1]
    nk = s // ts

    def body(a_ref, b_ref, o_ref, acc):
        @pl.when(pl.program_id(2) == 0)
        def _():
            acc[...] = jnp.zeros_like(acc)
        acc[...] += _dot(a_ref[...], b_ref[...])

        @pl.when(pl.program_id(2) == nk - 1)
        def _():
            o_ref[...] = acc[...].astype(out_dtype)

    return pl.pallas_call(
        body, out_shape=jax.ShapeDtypeStruct((m, n), out_dtype), grid=(m // tm, n // tn, nk),
        in_specs=[pl.BlockSpec((tm, ts), lambda i, j, k: (i, k)), pl.BlockSpec((ts, tn), lambda i, j, k: (k, j))],
        out_specs=pl.BlockSpec((tm, tn), lambda i, j, k: (i, j)),
        scratch_shapes=[pltpu.VMEM((tm, tn), F32)], name=name,
        compiler_params=_cparams(("parallel", "parallel", "arbitrary")))(at, b)


def _mm_up_dw(ht, dup, name, out_dtype=F32):
    d, s = ht.shape

    def body(a_ref, b_ref, o_ref):
        o_ref[...] = _dot(a_ref[...], b_ref[...]).astype(out_dtype)

    return pl.pallas_call(
        body, out_shape=jax.ShapeDtypeStruct((N_DEV, d, FFN_CB), out_dtype), grid=(N_DEV,),
        in_specs=[pl.BlockSpec((d, s), lambda j: (0, 0)), pl.BlockSpec((s, FFN_CB), lambda j: (0, j))],
        out_specs=pl.BlockSpec((None, d, FFN_CB), lambda j: (_ffn_dev(j), 0, 0)), name=name,
        compiler_params=_cparams(("parallel",)))(ht, dup)


def _layer_norm(r, g, b):
    mu = jnp.mean(r, axis=-1, keepdims=True)
    xc = r - mu
    var = jnp.mean(xc * xc, axis=-1, keepdims=True)
    return xc * lax.rsqrt(var + LN_EPS) * g + b


def _proj_ln(a, w, resid, g, bias, name, transposed=True, target=None):
    s, k = a.shape
    d = w.shape[1]
    tm = 512

    def body(a_ref, w_ref, x_ref, g_ref, bias_ref, *rest):
        r = ALPHA * x_ref[...] + _dot(a_ref[...], w_ref[...])
        h = _layer_norm(r, g_ref[...], bias_ref[...])
        if target is None:
            r_ref, h_ref = rest[0], rest[1]
            h_ref[...] = h
            if transposed:
                rest[2][...] = h.T.astype(BF16)
        else:
            t_ref, r_ref, dy_ref, l_ref = rest

            @pl.when(pl.program_id(0) == 0)
            def _():
                l_ref[...] = jnp.zeros_like(l_ref)
            e = h - t_ref[...]
            dy_ref[...] = e * (1.0 / d)
            part = 0.5 * jnp.sum(jnp.mean(e * e, axis=-1, keepdims=True), axis=0, keepdims=True)
            l_ref[...] += jnp.broadcast_to(part, l_ref.shape)
        r_ref[...] = r

    row = pl.BlockSpec((tm, d), lambda i: (i, 0))
    vec = pl.BlockSpec((1, d), lambda i: (0, 0))
    in_specs = [pl.BlockSpec((tm, k), lambda i: (i, 0)), pl.BlockSpec((k, d), lambda i: (0, 0)), row, vec, vec]
    args = [a, w, resid, g, bias]
    shapes = [jax.ShapeDtypeStruct((s, d), F32), jax.ShapeDtypeStruct((s, d), F32)]
    specs = [row, row]
    if target is not None:
        in_specs.append(row)
        args.append(target)
        shapes.append(jax.ShapeDtypeStruct((1, LANE), F32))
        specs.append(pl.BlockSpec((1, LANE), lambda i: (0, 0)))
    elif transposed:
        shapes.append(jax.ShapeDtypeStruct((d, s), BF16))
        specs.append(pl.BlockSpec((d, tm), lambda i: (0, i)))
    return pl.pallas_call(
        body, out_shape=tuple(shapes), grid=(s // tm,), in_specs=in_specs, out_specs=tuple(specs), name=name,
        compiler_params=_cparams(("arbitrary",) if target is not None else ("parallel",)))(*args)


def _layer_norm_bwd(r, dh, g):
    mu = jnp.mean(r, axis=-1, keepdims=True)
    xc = r - mu
    var = jnp.mean(xc * xc, axis=-1, keepdims=True)
    rstd = lax.rsqrt(var + LN_EPS)
    xh = xc * rstd
    dxh = dh * g
    m1 = jnp.mean(dxh, axis=-1, keepdims=True)
    m2 = jnp.mean(dxh * xh, axis=-1, keepdims=True)
    return (rstd * (dxh - m1 - xh * m2), jnp.sum(dh * xh, axis=0, keepdims=True),
            jnp.sum(dh, axis=0, keepdims=True))


def _ln_bwd(r, dh, g, name, dep=None):
    s, d = r.shape
    tm = 512

    def body(r_ref, dh_ref, g_ref, *rest):
        dr_ref, dg_ref, db_ref = rest[-3:]

        @pl.when(pl.program_id(0) == 0)
        def _():
            dg_ref[...] = jnp.zeros_like(dg_ref)
            db_ref[...] = jnp.zeros_like(db_ref)
        dr_ref[...], dg_rows, db_rows = _layer_norm_bwd(r_ref[...], dh_ref[...], g_ref[...])
        dg_ref[...] += dg_rows
        db_ref[...] += db_rows

    row = pl.BlockSpec((tm, d), lambda i: (i, 0))
    vec = pl.BlockSpec((1, d), lambda i: (0, 0))
    dep_specs, dep_ops = _dep_args(dep)
    return pl.pallas_call(
        body, out_shape=(jax.ShapeDtypeStruct((s, d), F32), jax.ShapeDtypeStruct((1, d), F32),
                         jax.ShapeDtypeStruct((1, d), F32)),
        grid=(s // tm,), in_specs=[row, row, vec] + dep_specs, out_specs=(row, vec, vec), name=name,
        compiler_params=_cparams(("arbitrary",)))(r, dh, g, *dep_ops)


def _rope_tables(s):
    half = HEAD // 2
    pos = jnp.arange(s, dtype=F32)
    inv = ROPE_THETA ** (-jnp.arange(half, dtype=F32) * 2.0 / HEAD)
    ang = pos[:, None] * inv[None, :]
    cos, sin = jnp.cos(ang), jnp.sin(ang)
    cos = jnp.concatenate([cos, cos, cos, cos], axis=1)
    sin = jnp.concatenate([-sin, sin, -sin, sin], axis=1)
    return cos, sin


def _rotate(x, cos, sin):
    lane = lax.broadcasted_iota(jnp.int32, x.shape, 1)
    partner = jnp.where((lane % HEAD) < HEAD // 2, pltpu.roll(x, LANE - HEAD // 2, axis=1),
                        pltpu.roll(x, HEAD // 2, axis=1))
    return x * cos + partner * sin


def _class_rows(c, d, tm):
    return pl.ds(c, tm // d, stride=d) if d > 1 else pl.ds(0, tm)


def _dilated_spec(tm, d, w):
    return pl.BlockSpec((tm // d, d * w), lambda i: (i, 0))


def _token_scratch(tm, w):
    return pltpu.VMEM((w // LANE, tm, LANE), F32)


def _to_tokens(src_ref, dst3, d, tm):
    nj = dst3.shape[0]
    for cls in range(d):
        for j in range(nj):
            col = (cls * nj + j) * LANE
            dst3.at[j][_class_rows(cls, d, tm), :] = src_ref[:, col:col + LANE]


def _to_dilated(src3, dst_ref, d, tm):
    nj = src3.shape[0]
    for cls in range(d):
        for j in range(nj):
            col = (cls * nj + j) * LANE
            dst_ref[:, col:col + LANE] = src3.at[j][_class_rows(cls, d, tm), :].astype(dst_ref.dtype)


def _token_value(src3):
    return jnp.concatenate([src3[j] for j in range(src3.shape[0])], axis=1)


def _proj_rope(h, w_in, cos, sin, name, dep=None, transposed=False):
    s, d_model = h.shape
    tm = 512
    w = 3 * ATTN_W
    nj = w // LANE

    def body(h_ref, w_ref, c_ref, s_ref, *rest):
        rot = rest[-1]
        if transposed:
            p_ref, o_refs, ht_ref = rest[-6], rest[-5:-2], rest[-2]
            ht_ref[...] = h_ref[...].T.astype(BF16)
        else:
            p_ref, o_refs = rest[-5], rest[-4:-1]
        y = _dot(h_ref[...], w_ref[...])
        p_ref[...] = y
        c, sn = c_ref[...], s_ref[...]
        for j in range(nj):
            x = y[:, j * LANE:(j + 1) * LANE]
            rot[j] = _rotate(x, c, sn) if j < 2 * ATTN_W // LANE else x
        for d, o_ref in zip(DILATIONS, o_refs):
            _to_dilated(rot, o_ref, d, tm)

    tab = pl.BlockSpec((tm, LANE), lambda i: (i, 0))
    dep_specs, dep_ops = _dep_args(dep)
    shapes = [jax.ShapeDtypeStruct((s, D_IN), F32), *[jax.ShapeDtypeStruct((s // d, d * w), BF16) for d in DILATIONS]]
    specs = [pl.BlockSpec((tm, D_IN), lambda i: (i, 0)), *[_dilated_spec(tm, d, w) for d in DILATIONS]]
    if transposed:
        shapes.append(jax.ShapeDtypeStruct((d_model, s), BF16))
        specs.append(pl.BlockSpec((d_model, tm), lambda i: (0, i)))
    res = pl.pallas_call(
        body, out_shape=tuple(shapes), grid=(s // tm,),
        in_specs=[pl.BlockSpec((tm, d_model), lambda i: (i, 0)), pl.BlockSpec((d_model, D_IN), lambda i: (0, 0)),
                  tab, tab] + dep_specs,
        out_specs=tuple(specs), scratch_shapes=[_token_scratch(tm, w)], name=name,
        compiler_params=_cparams(("parallel",)))(h, w_in, cos, sin, *dep_ops)
    return res[0], res[1:4], (res[4] if transposed else None)


def _dproj_assemble(dqkv_list, dxr, dgate, du, cos, sin, name):
    s = dxr.shape[0]
    tm = 512
    nq = 3 * ATTN_W // LANE

    def body(*refs):
        br = refs[:9]
        dxr_ref, dg_ref, du_ref, c_ref, s_ref, o_ref = refs[9:15]
        tok = refs[15:]
        c, sn = c_ref[...], -s_ref[...]
        for part in range(3):
            for b, d in enumerate(DILATIONS[1:], start=1):
                _to_tokens(br[3 * b + part], tok[2 * part + b - 1], d, tm)
        for j in range(nq):
            part, jj = divmod(j, ATTN_W // LANE)
            x = br[part][:, jj * LANE:(jj + 1) * LANE] + tok[2 * part][jj] + tok[2 * part + 1][jj]
            if part < 2:
                x = _rotate(x, c, sn)
            o_ref[:, j * LANE:(j + 1) * LANE] = x.astype(BF16)
        o_ref[:, 3 * ATTN_W:3 * ATTN_W + LRU_W] = dxr_ref[...].astype(BF16)
        o_ref[:, 3 * ATTN_W + LRU_W:3 * ATTN_W + 2 * LRU_W] = dg_ref[...].astype(BF16)
        o_ref[:, 3 * ATTN_W + 2 * LRU_W:] = du_ref[...].astype(BF16)

    a_spec = pl.BlockSpec((tm, ATTN_W), lambda i: (i, 0))
    tab = pl.BlockSpec((tm, LANE), lambda i: (i, 0))
    ordered = [dqkv_list[b][p] for b in range(3) for p in range(3)]
    d_specs = [_dilated_spec(tm, d, ATTN_W) for d in DILATIONS for _ in range(3)]
    return pl.pallas_call(
        body, out_shape=jax.ShapeDtypeStruct((s, D_IN), BF16), grid=(s // tm,),
        in_specs=d_specs + [a_spec, a_spec, pl.BlockSpec((tm, S5_W), lambda i: (i, 0)), tab, tab],
        out_specs=pl.BlockSpec((tm, D_IN), lambda i: (i, 0)),
        scratch_shapes=[_token_scratch(tm, ATTN_W)] * 6, name=name,
        compiler_params=_cparams(("parallel",)))(*ordered, dxr, dgate, du, cos, sin)


def _attn_tiles(s, d):
    m = s // d
    tq = min(m, ATTN_TILE)
    return m, tq, tq // ATTN_BLK


def _band_mask(qb):
    qi = lax.broadcasted_iota(jnp.int32, (ATTN_BLK, 2 * ATTN_BLK), 0)
    ki = lax.broadcasted_iota(jnp.int32, (ATTN_BLK, 2 * ATTN_BLK), 1)
    dist = qi + ATTN_BLK - ki
    return (dist >= 0) & (dist <= ATTN_BLK) & ((ki >= ATTN_BLK) | (qb > 0))


def _attn_fwd(qv, d, name):
    m = qv.shape[0]
    w3 = 3 * ATTN_W
    _, tq, n = _attn_tiles(m * d, d)
    scale = HEAD ** -0.5

    def body(x_ref, p_ref, o_ref, l_ref):
        b = pl.program_id(1)

        def block(i, first):
            r0 = 0 if first else pl.multiple_of(i * ATTN_BLK, ATTN_BLK)
            rows = pl.ds(r0, ATTN_BLK)
            valid = _band_mask(b * n + i)
            if not first:
                krows = pl.ds(pl.multiple_of(i * ATTN_BLK - ATTN_BLK, ATTN_BLK), 2 * ATTN_BLK)
            low = lax.broadcasted_iota(jnp.int32, (1, LANE), 1) < HEAD
            for hp in range(ATTN_W // LANE):
                qs, ks, vs = (slice(part * ATTN_W + hp * LANE, part * ATTN_W + (hp + 1) * LANE) for part in range(3))
                q2 = x_ref[rows, qs]
                if first:
                    k2 = jnp.concatenate([p_ref[:, ks], x_ref[0:ATTN_BLK, ks]], axis=0)
                    v2 = jnp.concatenate([p_ref[:, vs], x_ref[0:ATTN_BLK, vs]], axis=0)
                else:
                    k2 = x_ref[krows, ks]
                    v2 = x_ref[krows, vs]
                outs, lses = [], []
                for mask in (low, ~low):
                    q = jnp.where(mask, q2, jnp.zeros_like(q2))
                    sc = jnp.where(valid, _dot_nt(q, k2) * scale, -1e30)
                    mx = jnp.max(sc, axis=-1, keepdims=True)
                    p = jnp.exp(sc - mx)
                    l = jnp.sum(p, axis=-1, keepdims=True)
                    outs.append(_dot(p, v2) / l)
                    lses.append(mx + jnp.log(l))
                o_ref[rows, hp * LANE:(hp + 1) * LANE] = jnp.where(low, outs[0], outs[1])
                l_ref[rows, hp * LANE:(hp + 1) * LANE] = jnp.where(low, lses[0], lses[1])

        block(0, True)
        if n > 1:
            def loop(i, carry):
                block(i, False)
                return carry
            lax.fori_loop(1, n, loop, 0)

    shp = jax.ShapeDtypeStruct((m, d * ATTN_W), F32)
    ospec = pl.BlockSpec((tq, ATTN_W), lambda c, b: (b, c))
    out, lse = pl.pallas_call(
        body, out_shape=(shp, shp), grid=(d, m // tq),
        in_specs=[pl.BlockSpec((tq, w3), lambda c, b: (b, c)),
                  pl.BlockSpec((ATTN_BLK, w3), lambda c, b: (jnp.maximum(b * n - 1, 0), c))],
        out_specs=(ospec, ospec), name=name,
        compiler_params=_cparams(("parallel", "parallel")))(qv, qv)
    return out, lse


def _attn_bwd(qv, ov, dov, lv, d, name, dep=None):
    m = qv.shape[0]
    w3 = 3 * ATTN_W
    _, tq, n = _attn_tiles(m * d, d)
    nb = m // ATTN_BLK
    scale = HEAD ** -0.5

    def body(x_ref, p_ref, nx_ref, o_ref, do_ref, l_ref, on_ref, don_ref, ln_ref, *rest):
        dq_ref, dk_ref, dv_ref = rest[-3:]
        b = pl.program_id(1)
        dk_ref[...] = jnp.zeros_like(dk_ref)
        dv_ref[...] = jnp.zeros_like(dv_ref)

        low = lax.broadcasted_iota(jnp.int32, (1, LANE), 1) < HEAD

        def pair_grads(q2, k2, v2, o2, do2, l2, valid):
            dq, dk, dv = [], 0.0, 0.0
            for mask, lse in ((low, l2[:, 0:1]), (~low, l2[:, HEAD:HEAD + 1])):
                q = jnp.where(mask, q2, jnp.zeros_like(q2))
                do = jnp.where(mask, do2, 0.0)
                sc = jnp.where(valid, _dot_nt(q, k2) * scale, -1e30)
                p = jnp.exp(sc - lse)
                delta = jnp.sum(do * o2, axis=-1, keepdims=True)
                ds = p * (_dot_nt(do, v2) - delta) * scale
                dq.append(_dot(ds, k2))
                dk = dk + _dot_tn(ds, q)
                dv = dv + _dot_tn(p, do)
            return jnp.where(low, dq[0], dq[1]), dk, dv

        def cols(hp):
            return [slice(part * ATTN_W + hp * LANE, part * ATTN_W + (hp + 1) * LANE) for part in range(3)]

        def block(i, first):
            r0 = 0 if first else pl.multiple_of(i * ATTN_BLK, ATTN_BLK)
            rows = pl.ds(r0, ATTN_BLK)
            valid = _band_mask(b * n + i)
            if not first:
                krows = pl.ds(pl.multiple_of(i * ATTN_BLK - ATTN_BLK, ATTN_BLK), 2 * ATTN_BLK)
            for hp in range(ATTN_W // LANE):
                qs, ks, vs = cols(hp)
                if first:
                    k2 = jnp.concatenate([p_ref[:, ks], x_ref[0:ATTN_BLK, ks]], axis=0)
                    v2 = jnp.concatenate([p_ref[:, vs], x_ref[0:ATTN_BLK, vs]], axis=0)
                else:
                    k2 = x_ref[krows, ks]
                    v2 = x_ref[krows, vs]
                dq, dk, dv = pair_grads(x_ref[rows, qs], k2, v2, o_ref[rows, qs], do_ref[rows, qs],
                                        l_ref[rows, qs], valid)
                dq_ref[rows, qs] = dq
                if first:
                    dk_ref[0:ATTN_BLK, qs] += dk[ATTN_BLK:, :]
                    dv_ref[0:ATTN_BLK, qs] += dv[ATTN_BLK:, :]
                else:
                    dk_ref[krows, qs] += dk
                    dv_ref[krows, qs] += dv

        block(0, True)
        if n > 1:
            def loop(i, carry):
                block(i, False)
                return carry
            lax.fori_loop(1, n, loop, 0)

        last = slice((n - 1) * ATTN_BLK, n * ATTN_BLK)
        qi = lax.broadcasted_iota(jnp.int32, (ATTN_BLK, ATTN_BLK), 0)
        ki = lax.broadcasted_iota(jnp.int32, (ATTN_BLK, ATTN_BLK), 1)
        valid_next = (qi <= ki) & ((b + 1) * n < nb)
        for hp in range(ATTN_W // LANE):
            qs, ks, vs = cols(hp)
            _, dk, dv = pair_grads(nx_ref[:, qs], x_ref[last, ks], x_ref[last, vs], on_ref[:, qs], don_ref[:, qs],
                                   ln_ref[:, qs], valid_next)
            dk_ref[last, qs] += dk
            dv_ref[last, qs] += dv

    nxt = lambda b: jnp.minimum((b + 1) * n, nb - 1)
    xs = pl.BlockSpec((tq, w3), lambda c, b: (b, c))
    xp = pl.BlockSpec((ATTN_BLK, w3), lambda c, b: (jnp.maximum(b * n - 1, 0), c))
    xn = pl.BlockSpec((ATTN_BLK, w3), lambda c, b: (nxt(b), c))
    a = pl.BlockSpec((tq, ATTN_W), lambda c, b: (b, c))
    an = pl.BlockSpec((ATTN_BLK, ATTN_W), lambda c, b: (nxt(b), c))
    shp = jax.ShapeDtypeStruct((m, d * ATTN_W), F32)
    dep_specs, dep_ops = _dep_args(dep)
    return pl.pallas_call(
        body, out_shape=(shp, shp, shp), grid=(d, m // tq),
        in_specs=[xs, xp, xn, a, a, a, an, an, an] + dep_specs, out_specs=(a, a, a), name=name,
        compiler_params=_cparams(("parallel", "parallel")))(qv, qv, qv, ov, dov, lv, ov, dov, lv, *dep_ops)


def _rms(x, g):
    ms = jnp.mean(x * x, axis=-1, keepdims=True)
    return x * lax.rsqrt(ms + RMS_EPS) * g


def _rms_bwd(x, g, dy):
    ms = jnp.mean(x * x, axis=-1, keepdims=True)
    r = lax.rsqrt(ms + RMS_EPS)
    dyg = dy * g
    dx = r * dyg - x * (r * r * r) * jnp.mean(x * dyg, axis=-1, keepdims=True)
    return dx, dy * x * r


def _mix_fwd(outs, lses, lru, s5, g, h_in, w_out, ln_g, ln_b, name):
    s = lru.shape[0]
    tm = 256

    def body(o1, o2, o3, l1, l2, l3, lru_ref, s5_ref, g_ref, x_ref, w_ref, lg_ref, lb_ref,
             mixed_t_ref, r_ref, h_ref, ht_ref, ov1, ov2, ov3, lv1, lv2, lv3, so2, so3, sl2, sl3):
        for d, src, dst in ((DILATIONS[1], o2, so2), (DILATIONS[2], o3, so3),
                            (DILATIONS[1], l2, sl2), (DILATIONS[2], l3, sl3)):
            _to_tokens(src, dst, d, tm)
        a1, a2, a3 = l1[...], _token_value(sl2), _token_value(sl3)
        mx = jnp.maximum(jnp.maximum(a1, a2), a3)
        e1, e2, e3 = jnp.exp(a1 - mx), jnp.exp(a2 - mx), jnp.exp(a3 - mx)
        den = e1 + e2 + e3
        o = (e1 * o1[...] + e2 * _token_value(so2) + e3 * _token_value(so3)) / den
        lse = mx + jnp.log(den)
        ov1[...] = o
        lv1[...] = lse
        for j in range(ATTN_W // LANE):
            so2[j] = o[:, j * LANE:(j + 1) * LANE]
            sl2[j] = lse[:, j * LANE:(j + 1) * LANE]
        for d, o_dst, l_dst in ((DILATIONS[1], ov2, lv2), (DILATIONS[2], ov3, lv3)):
            _to_dilated(so2, o_dst, d, tm)
            _to_dilated(sl2, l_dst, d, tm)
        gg = g_ref[...]
        mixed = jnp.concatenate([_rms(o, gg[:, :ATTN_W]),
                                 _rms(lru_ref[...], gg[:, ATTN_W:ATTN_W + LRU_W]),
                                 _rms(s5_ref[...], gg[:, ATTN_W + LRU_W:])], axis=1)
        mixed_t_ref[...] = mixed.T.astype(BF16)
        r = ALPHA * x_ref[...] + _dot(mixed, w_ref[...])
        h = _layer_norm(r, lg_ref[...], lb_ref[...])
        r_ref[...] = r
        h_ref[...] = h
        ht_ref[...] = h.T.astype(BF16)

    a = pl.BlockSpec((tm, ATTN_W), lambda i: (i, 0))
    s5s = pl.BlockSpec((tm, S5_W), lambda i: (i, 0))
    full = pl.BlockSpec((tm, D_MODEL), lambda i: (i, 0))
    vec = pl.BlockSpec((1, D_MODEL), lambda i: (0, 0))
    dil = [_dilated_spec(tm, d, ATTN_W) for d in DILATIONS]
    dshape = [jax.ShapeDtypeStruct((s // d, d * ATTN_W), F32) for d in DILATIONS]
    tshape = jax.ShapeDtypeStruct((D_MODEL, s), BF16)
    fshape = jax.ShapeDtypeStruct((s, D_MODEL), F32)
    tspec = pl.BlockSpec((D_MODEL, tm), lambda i: (0, i))
    res = pl.pallas_call(
        body, out_shape=(tshape, fshape, fshape, tshape, *dshape, *dshape),
        grid=(s // tm,),
        in_specs=dil + dil + [a, s5s, vec, full, pl.BlockSpec((D_MODEL, D_MODEL), lambda i: (0, 0)), vec, vec],
        out_specs=(tspec, full, full, tspec, *dil, *dil),
        scratch_shapes=[_token_scratch(tm, ATTN_W)] * 4, name=name,
        compiler_params=_cparams(("parallel",)))(*outs, *lses, lru, s5, g, h_in, w_out, ln_g, ln_b)
    return res[0], res[1], res[2], res[3], res[4:7], res[7:10]


def _mix_bwd(r, dh, ln_g, w_out, o, lru, s5, g, name, dep=None):
    s = lru.shape[0]
    tm = 256

    def body(r_ref, dh_ref, lg_ref, w_ref, o_ref, lru_ref, s5_ref, g_ref, *rest):
        dr_ref, dlg_ref, dlb_ref, do_ref, do2_ref, do3_ref, dlru_ref, ds5_ref, dg_ref, stage = rest[-10:]

        @pl.when(pl.program_id(0) == 0)
        def _():
            dg_ref[...] = jnp.zeros_like(dg_ref)
            dlg_ref[...] = jnp.zeros_like(dlg_ref)
            dlb_ref[...] = jnp.zeros_like(dlb_ref)
        gg = g_ref[...]
        dr, dlg_rows, dlb_rows = _layer_norm_bwd(r_ref[...], dh_ref[...], lg_ref[...])
        dr_ref[...] = dr
        dlg_ref[...] += dlg_rows
        dlb_ref[...] += dlb_rows
        dm = _dot_nt(dr, w_ref[...])
        dx, dgr = _rms_bwd(o_ref[...], gg[:, :ATTN_W], dm[:, :ATTN_W])
        do_ref[...] = dx
        for j in range(ATTN_W // LANE):
            stage[j] = dx[:, j * LANE:(j + 1) * LANE]
        _to_dilated(stage, do2_ref, DILATIONS[1], tm)
        _to_dilated(stage, do3_ref, DILATIONS[2], tm)
        dg_ref[:, :ATTN_W] += jnp.sum(dgr, axis=0, keepdims=True)
        dx, dgr = _rms_bwd(lru_ref[...], gg[:, ATTN_W:ATTN_W + LRU_W], dm[:, ATTN_W:ATTN_W + LRU_W])
        dlru_ref[...] = dx
        dg_ref[:, ATTN_W:ATTN_W + LRU_W] += jnp.sum(dgr, axis=0, keepdims=True)
        dx, dgr = _rms_bwd(s5_ref[...], gg[:, ATTN_W + LRU_W:], dm[:, ATTN_W + LRU_W:])
        ds5_ref[...] = dx
        dg_ref[:, ATTN_W + LRU_W:] += jnp.sum(dgr, axis=0, keepdims=True)

    a = pl.BlockSpec((tm, ATTN_W), lambda i: (i, 0))
    s5s = pl.BlockSpec((tm, S5_W), lambda i: (i, 0))
    full = pl.BlockSpec((tm, D_MODEL), lambda i: (i, 0))
    vec = pl.BlockSpec((1, D_MODEL), lambda i: (0, 0))
    dil = [_dilated_spec(tm, d, ATTN_W) for d in DILATIONS]
    dshape = [jax.ShapeDtypeStruct((s // d, d * ATTN_W), F32) for d in DILATIONS]
    dep_specs, dep_ops = _dep_args(dep)
    vshape = jax.ShapeDtypeStruct((1, D_MODEL), F32)
    res = pl.pallas_call(
        body, out_shape=(jax.ShapeDtypeStruct((s, D_MODEL), F32), vshape, vshape, *dshape,
                         jax.ShapeDtypeStruct((s, LRU_W), F32), jax.ShapeDtypeStruct((s, S5_W), F32), vshape),
        grid=(s // tm,),
        in_specs=[full, full, vec, pl.BlockSpec((D_MODEL, D_MODEL), lambda i: (0, 0)), a, a, s5s, vec] + dep_specs,
        out_specs=(full, vec, vec, *dil, a, s5s, vec), scratch_shapes=[_token_scratch(tm, ATTN_W)], name=name,
        compiler_params=_cparams(("arbitrary",)))(r, dh, ln_g, w_out, o, lru, s5, g, *dep_ops)
    return res[0], res[1], res[2], res[3:6], res[6], res[7], res[8]


def _lru_gate_math(xc, pre_r, pre_i, lam):
    r = _sigmoid(pre_r)
    i = _sigmoid(pre_i)
    log_a = -LRU_C * r * _softplus(-lam)
    a = jnp.exp(log_a)
    u = jnp.sqrt(-_expm1(2.0 * log_a)) * (i * xc)
    return a, u


def _lru_conv(x, prev8, cw, cb):
    y = cb + cw[LRU_CONV - 1:LRU_CONV, :] * x
    for k in range(LRU_CONV - 1):
        y = y + cw[k:k + 1, :] * _shift_down_prev(x, LRU_CONV - 1 - k, prev8)
    return y


def _lru_specs(s):
    xo = 3 * ATTN_W // LANE
    go = xo + LRU_W // LANE
    xr = pl.BlockSpec((s, LANE), lambda j: (0, xo + j))
    gt = pl.BlockSpec((s, LANE), lambda j: (0, go + j))
    cw = pl.BlockSpec((LRU_CONV, LANE), lambda j: (0, j))
    vec = pl.BlockSpec((1, LANE), lambda j: (0, j))
    wbd = pl.BlockSpec((LANE, LANE), lambda j: (j, j))
    col = pl.BlockSpec((s, LANE), lambda j: (0, j))
    return xr, gt, cw, vec, wbd, col


def _lru_fwd(proj, cw, cb, wr, br, wi, bi, lam, name):
    s = proj.shape[0]
    t = SCAN_T

    def body(xr_ref, gt_ref, cw_ref, cb_ref, wr_ref, br_ref, wi_ref, bi_ref, lam_ref, o_ref, xc_ref, a_ref, h_ref):
        cwv, cbv, lamv = cw_ref[...], cb_ref[...], lam_ref[...]
        wrv, wiv, brv, biv = wr_ref[...], wi_ref[...], br_ref[...], bi_ref[...]

        def chunk(c, carry):
            h_c, prev8 = carry
            rows = pl.ds(pl.multiple_of(c * t, t), t)
            x = xr_ref[rows, :]
            xc = _lru_conv(x, prev8, cwv, cbv)
            a, u = _lru_gate_math(xc, _dot(xc, wrv) + brv, _dot(xc, wiv) + biv, lamv)
            h = _scan_chunk(a, u, h_c)
            xc_ref[rows, :] = xc
            a_ref[rows, :] = a
            h_ref[rows, :] = h
            o_ref[rows, :] = h * _gelu(gt_ref[rows, :])
            return h[t - 1:t, :], x[t - 8:t, :]

        lax.fori_loop(0, s // t, chunk, (jnp.zeros((1, LANE), F32), jnp.zeros((8, LANE), F32)))

    xr, gt, cws, vec, wbd, col = _lru_specs(s)
    shp = jax.ShapeDtypeStruct((s, LRU_W), F32)
    return pl.pallas_call(
        body, out_shape=(shp,) * 4, grid=(LRU_W // LANE,),
        in_specs=[xr, gt, cws, vec, wbd, vec, wbd, vec, vec], out_specs=(col,) * 4, name=name,
        compiler_params=_cparams(("parallel",)))(proj, proj, cw, cb, wr, br, wi, bi, lam)


def _lru_bwd(proj, dout, xc_all, a_all, h_all, cw, cb, wr, br, wi, bi, lam, name):
    s = proj.shape[0]
    t = SCAN_T
    nc = s // t

    def body(xr_ref, gt_ref, do_ref, xc_s, a_s, h_s, cw_ref, cb_ref, wr_ref, br_ref, wi_ref, bi_ref, lam_ref,
             dxr_ref, dgt_ref, dcw_ref, dcb_ref, dwr_ref, dbr_ref, dwi_ref, dbi_ref, dlam_ref):
        cwv, cbv, lamv = cw_ref[...], cb_ref[...], lam_ref[...]
        wrv, wiv, brv, biv = wr_ref[...], wi_ref[...], br_ref[...], bi_ref[...]
        z1 = jnp.zeros((1, LANE), F32)
        zw = jnp.zeros((LANE, LANE), F32)

        def bchunk(ci, carry):
            g_next, a_next, dxc_next8, dcw, dcb, dwr, dbr, dwi, dbi, dlam = carry
            c = nc - 1 - ci
            t0 = pl.multiple_of(c * t, t)
            rows = pl.ds(t0, t)
            before = pl.ds(pl.multiple_of(jnp.maximum(t0 - 8, 0), 8), 8)
            has_prev = (c > 0).astype(F32)
            x, gt, do = xr_ref[rows, :], gt_ref[rows, :], do_ref[rows, :]
            xc, a, h = xc_s[rows, :], a_s[rows, :], h_s[rows, :]
            prev8_h = h_s[before, :] * has_prev
            dgt_ref[rows, :] = do * h * _gelu_grad(gt)
            dh = do * _gelu(gt)
            a_plus = _shift_up_next(a, 1, jnp.broadcast_to(a_next, (8, LANE)))
            g = _scan_chunk(a_plus, dh, g_next, reverse=True)
            da = g * _shift_down_prev(h, 1, prev8_h)
            pre_r = _dot(xc, wrv) + brv
            pre_i = _dot(xc, wiv) + biv
            _, vjp = jax.vjp(_lru_gate_math, xc, pre_r, pre_i, lamv)
            dxc, dpre_r, dpre_i, dlam_c = vjp((da, g))
            dxc = dxc + _dot_nt(dpre_r, wrv) + _dot_nt(dpre_i, wiv)
            dx = cwv[LRU_CONV - 1:LRU_CONV, :] * dxc
            dcw_rows = [None] * LRU_CONV
            dcw_rows[LRU_CONV - 1] = jnp.sum(dxc * x, axis=0, keepdims=True)
            for k in range(LRU_CONV - 1):
                dxc_ahead = _shift_up_next(dxc, LRU_CONV - 1 - k, dxc_next8)
                dx = dx + cwv[k:k + 1, :] * dxc_ahead
                dcw_rows[k] = jnp.sum(dxc_ahead * x, axis=0, keepdims=True)
            dxr_ref[rows, :] = dx
            return (g[0:1, :], a[0:1, :], dxc[0:8, :],
                    dcw + jnp.concatenate(dcw_rows, axis=0),
                    dcb + jnp.sum(dxc, axis=0, keepdims=True),
                    dwr + _dot_tn(xc, dpre_r), dbr + jnp.sum(dpre_r, axis=0, keepdims=True),
                    dwi + _dot_tn(xc, dpre_i), dbi + jnp.sum(dpre_i, axis=0, keepdims=True),
                    dlam + dlam_c)

        init = (z1, z1, jnp.zeros((8, LANE), F32), jnp.zeros((LRU_CONV, LANE), F32), z1, zw, z1, zw, z1, z1)
        res = lax.fori_loop(0, nc, bchunk, init)
        dcw_ref[...] = res[3]
        dcb_ref[...] = res[4]
        dwr_ref[...] = res[5]
        dbr_ref[...] = res[6]
        dwi_ref[...] = res[7]
        dbi_ref[...] = res[8]
        dlam_ref[...] = res[9]

    xr, gt, cws, vec, wbd, col = _lru_specs(s)
    vshape = jax.ShapeDtypeStruct((1, LRU_W), F32)
    wshape = jax.ShapeDtypeStruct((LRU_W, LRU_W), F32)
    return pl.pallas_call(
        body,
        out_shape=(jax.ShapeDtypeStruct((s, LRU_W), F32), jax.ShapeDtypeStruct((s, LRU_W), F32),
                   jax.ShapeDtypeStruct((LRU_CONV, LRU_W), F32), vshape, wshape, vshape, wshape, vshape, vshape),
        grid=(LRU_W // LANE,),
        in_specs=[xr, gt, col, col, col, col, cws, vec, wbd, vec, wbd, vec, vec],
        out_specs=(col, col, cws, vec, wbd, vec, wbd, vec, vec), name=name,
        compiler_params=_cparams(("parallel",)))(proj, proj, dout, xc_all, a_all, h_all, cw, cb, wr, br, wi, bi,
                                                 lam)


def _s5_disc_math(a_re, a_im, log_step, bt_re, bt_im):
    step = jnp.exp(log_step)
    dt_re, dt_im = step * a_re, step * a_im
    mag = jnp.exp(dt_re)
    ab_re, ab_im = mag * jnp.cos(dt_im), mag * jnp.sin(dt_im)
    z_re, z_im = ab_re - 1.0, ab_im
    den = a_re * a_re + a_im * a_im
    f_re = (z_re * a_re + z_im * a_im) / den
    f_im = (z_im * a_re - z_re * a_im) / den
    bb_re = f_re * bt_re - f_im * bt_im
    bb_im = f_re * bt_im + f_im * bt_re
    return ab_re, ab_im, bb_re, bb_im


def _s5_disc_fwd(a_re, a_im, log_step, bt_re, bt_im, name):
    def body(ar, ai, ls, br, bi, o1, o2, o3, o4):
        r = _s5_disc_math(ar[...], ai[...], ls[...], br[...], bi[...])
        o1[...], o2[...], o3[...], o4[...] = r

    shp = jax.ShapeDtypeStruct(a_re.shape, F32)
    return pl.pallas_call(body, out_shape=(shp,) * 4, name=name)(a_re, a_im, log_step, bt_re, bt_im)


def _s5_disc_bwd(a_re, a_im, log_step, bt_re, bt_im, cts, name):
    def body(ar, ai, ls, br, bi, c1, c2, c3, c4, o1, o2, o3, o4, o5):
        _, vjp = jax.vjp(_s5_disc_math, ar[...], ai[...], ls[...], br[...], bi[...])
        r = vjp((c1[...], c2[...], c3[...], c4[...]))
        o1[...], o2[...], o3[...], o4[...], o5[...] = r

    shp = jax.ShapeDtypeStruct(a_re.shape, F32)
    return pl.pallas_call(body, out_shape=(shp,) * 5, name=name)(a_re, a_im, log_step, bt_re, bt_im, *cts)


def _s5_u_specs(s):
    uo = (3 * ATTN_W + 2 * LRU_W) // LANE
    return (pl.BlockSpec((s, LANE), lambda j: (0, uo)), pl.BlockSpec((s, LANE), lambda j: (0, uo + 1)))


def _s5_scan_fwd(proj, b_re, b_im, lam_re, lam_im, c_re, c_im, name):
    s = proj.shape[0]
    t = SCAN_T

    def body(u0_ref, u1_ref, bre_ref, bim_ref, lre_ref, lim_ref, cre_ref, cim_ref, xre_ref, xim_ref, y_ref):
        @pl.when(pl.program_id(0) == 0)
        def _():
            y_ref[...] = jnp.zeros_like(y_ref)
        lr, li = lre_ref[...], lim_ref[...]
        consts = _cscan_consts(lr, li, False)
        bre, bim, cre, cim = bre_ref[...], bim_ref[...], cre_ref[...], cim_ref[...]

        def chunk(c, carry):
            cr, ci = carry
            rows = pl.ds(pl.multiple_of(c * t, t), t)
            u = jnp.concatenate([u0_ref[rows, :], u1_ref[rows, :]], axis=1).astype(BF16)
            xr, xi = _cscan_chunk(_dot(u, bre), _dot(u, bim), consts, (cr, ci))
            xre_ref[rows, :] = xr
            xim_ref[rows, :] = xi
            y_ref[rows, :] += _dot(xr, cre) - _dot(xi, cim)
            return xr[t - 1:t, :], xi[t - 1:t, :]

        z = jnp.zeros((1, S5_BLK), F32)
        lax.fori_loop(0, s // t, chunk, (z, z))

    u0, u1 = _s5_u_specs(s)
    bsp = pl.BlockSpec((S5_W, S5_BLK), lambda j: (0, j))
    csp = pl.BlockSpec((S5_BLK, S5_W), lambda j: (j, 0))
    vec = pl.BlockSpec((1, S5_BLK), lambda j: (0, j))
    xsp = pl.BlockSpec((s, S5_BLK), lambda j: (0, j))
    ysp = pl.BlockSpec((s, S5_W), lambda j: (0, 0))
    xshape = jax.ShapeDtypeStruct((s, S5_STATES), F32)
    return pl.pallas_call(
        body, out_shape=(xshape, xshape, jax.ShapeDtypeStruct((s, S5_W), F32)),
        grid=(S5_STATES // S5_BLK,), in_specs=[u0, u1, bsp, bsp, vec, vec, csp, csp],
        out_specs=(xsp, xsp, ysp), name=name,
        compiler_params=_cparams(("arbitrary",)))(proj, proj, b_re, b_im, lam_re, lam_im, c_re, c_im)


def _s5_scan_bwd(proj, dy, du_init, x_re, x_im, b_re, b_im, lam_re, lam_im, c_re, c_im, name):
    s = proj.shape[0]
    t = SCAN_T
    nc = s // t

    def body(u0_ref, u1_ref, dy_ref, dui_ref, xre_ref, xim_ref, bre_ref, bim_ref, lre_ref, lim_ref,
             cre_ref, cim_ref, du_ref, dlr_ref, dli_ref, dbr_ref, dbi_ref, dcr_ref, dci_ref):
        @pl.when(pl.program_id(0) == 0)
        def _():
            du_ref[...] = dui_ref[...]
        mr, mi = lre_ref[...], -lim_ref[...]
        consts = _cscan_consts(mr, mi, True)
        bre, bim, cre, cim = bre_ref[...], bim_ref[...], cre_ref[...], cim_ref[...]
        dbr_ref[...] = jnp.zeros_like(dbr_ref)
        dbi_ref[...] = jnp.zeros_like(dbi_ref)
        dcr_ref[...] = jnp.zeros_like(dcr_ref)
        dci_ref[...] = jnp.zeros_like(dci_ref)

        def chunk(ci_, carry):
            gnr, gni, dlr, dli = carry
            c = nc - 1 - ci_
            t0 = pl.multiple_of(c * t, t)
            rows = pl.ds(t0, t)
            before = pl.ds(pl.multiple_of(jnp.maximum(t0 - 8, 0), 8), 8)
            has_prev = (c > 0).astype(F32)
            dyc = dy_ref[rows, :].astype(BF16)
            u = jnp.concatenate([u0_ref[rows, :], u1_ref[rows, :]], axis=1).astype(BF16)
            gr, gi = _cscan_chunk(_dot_nt(dyc, cre), -_dot_nt(dyc, cim), consts, (gnr, gni), reverse=True)
            xr, xi = xre_ref[rows, :], xim_ref[rows, :]
            xpr = _shift_down_prev(xr, 1, xre_ref[before, :] * has_prev)
            xpi = _shift_down_prev(xi, 1, xim_ref[before, :] * has_prev)
            dlr = dlr + jnp.sum(gr * xpr + gi * xpi, axis=0, keepdims=True)
            dli = dli + jnp.sum(gi * xpr - gr * xpi, axis=0, keepdims=True)
            du_ref[rows, :] += _dot_nt(gr, bre) + _dot_nt(gi, bim)
            dbr_ref[...] += _dot_tn(u, gr)
            dbi_ref[...] += _dot_tn(u, gi)
            dcr_ref[...] += _dot_tn(xr, dyc)
            dci_ref[...] -= _dot_tn(xi, dyc)
            return gr[0:1, :], gi[0:1, :], dlr, dli

        z = jnp.zeros((1, S5_BLK), F32)
        res = lax.fori_loop(0, nc, chunk, (z, z, z, z))
        dlr_ref[...] = res[2]
        dli_ref[...] = res[3]

    u0, u1 = _s5_u_specs(s)
    bsp = pl.BlockSpec((S5_W, S5_BLK), lambda j: (0, j))
    csp = pl.BlockSpec((S5_BLK, S5_W), lambda j: (j, 0))
    vec = pl.BlockSpec((1, S5_BLK), lambda j: (0, j))
    xsp = pl.BlockSpec((s, S5_BLK), lambda j: (0, j))
    ysp = pl.BlockSpec((s, S5_W), lambda j: (0, 0))
    return pl.pallas_call(
        body,
        out_shape=(jax.ShapeDtypeStruct((s, S5_W), F32),
                   jax.ShapeDtypeStruct((1, S5_STATES), F32), jax.ShapeDtypeStruct((1, S5_STATES), F32),
                   jax.ShapeDtypeStruct((S5_W, S5_STATES), F32), jax.ShapeDtypeStruct((S5_W, S5_STATES), F32),
                   jax.ShapeDtypeStruct((S5_STATES, S5_W), F32), jax.ShapeDtypeStruct((S5_STATES, S5_W), F32)),
        grid=(S5_STATES // S5_BLK,),
        in_specs=[u0, u1, ysp, ysp, xsp, xsp, bsp, bsp, vec, vec, csp, csp],
        out_specs=(ysp, vec, vec, bsp, bsp, csp, csp), name=name,
        compiler_params=_cparams(("arbitrary",)))(
            proj, proj, dy, du_init, x_re, x_im, b_re, b_im, lam_re, lam_im, c_re, c_im)


def _s5_out_fwd(proj, y_acc, dvec, w_glu, b_glu, name):
    s = proj.shape[0]
    tm = 512
    uo = (3 * ATTN_W + 2 * LRU_W) // LANE

    def body(u0_ref, u1_ref, y_ref, d_ref, w_ref, b_ref, o_ref, yp_ref):
        u = jnp.concatenate([u0_ref[...], u1_ref[...]], axis=1)
        y = y_ref[...] + d_ref[...] * u
        yp_ref[...] = y
        yg = _gelu(y)
        o_ref[...] = yg * _sigmoid(_dot(yg, w_ref[...]) + b_ref[...])

    u0 = pl.BlockSpec((tm, LANE), lambda i: (i, uo))
    u1 = pl.BlockSpec((tm, LANE), lambda i: (i, uo + 1))
    row = pl.BlockSpec((tm, S5_W), lambda i: (i, 0))
    vec = pl.BlockSpec((1, S5_W), lambda i: (0, 0))
    wsp = pl.BlockSpec((S5_W, S5_W), lambda i: (0, 0))
    shp = jax.ShapeDtypeStruct((s, S5_W), F32)
    return pl.pallas_call(
        body, out_shape=(shp, shp), grid=(s // tm,), in_specs=[u0, u1, row, vec, wsp, vec],
        out_specs=(row, row), name=name,
        compiler_params=_cparams(("parallel",)))(proj, proj, y_acc, dvec, w_glu, b_glu)


def _s5_out_bwd(proj, y_pre, dout, dvec, w_glu, b_glu, name, dep=None):
    s = proj.shape[0]
    tm = 512
    uo = (3 * ATTN_W + 2 * LRU_W) // LANE

    def body(u0_ref, u1_ref, y_ref, do_ref, d_ref, w_ref, b_ref, *rest):
        dy_ref, dud_ref, dd_ref, dw_ref, db_ref = rest[-5:]

        @pl.when(pl.program_id(0) == 0)
        def _():
            dd_ref[...] = jnp.zeros_like(dd_ref)
            dw_ref[...] = jnp.zeros_like(dw_ref)
            db_ref[...] = jnp.zeros_like(db_ref)
        u = jnp.concatenate([u0_ref[...], u1_ref[...]], axis=1)
        y = y_ref[...]
        do = do_ref[...]
        yg = _gelu(y)
        sg = _sigmoid(_dot(yg, w_ref[...]) + b_ref[...])
        dz = do * yg * sg * (1.0 - sg)
        dyg = do * sg + _dot_nt(dz, w_ref[...])
        dy = dyg * _gelu_grad(y)
        dy_ref[...] = dy
        dud_ref[...] = d_ref[...] * dy
        dd_ref[...] += jnp.sum(dy * u, axis=0, keepdims=True)
        dw_ref[...] += _dot_tn(yg, dz)
        db_ref[...] += jnp.sum(dz, axis=0, keepdims=True)

    u0 = pl.BlockSpec((tm, LANE), lambda i: (i, uo))
    u1 = pl.BlockSpec((tm, LANE), lambda i: (i, uo + 1))
    row = pl.BlockSpec((tm, S5_W), lambda i: (i, 0))
    vec = pl.BlockSpec((1, S5_W), lambda i: (0, 0))
    wsp = pl.BlockSpec((S5_W, S5_W), lambda i: (0, 0))
    shp = jax.ShapeDtypeStruct((s, S5_W), F32)
    vshape = jax.ShapeDtypeStruct((1, S5_W), F32)
    dep_specs, dep_ops = _dep_args(dep)
    return pl.pallas_call(
        body, out_shape=(shp, shp, vshape, jax.ShapeDtypeStruct((S5_W, S5_W), F32), vshape),
        grid=(s // tm,), in_specs=[u0, u1, row, row, vec, wsp, vec] + dep_specs,
        out_specs=(row, row, vec, wsp, vec), name=name,
        compiler_params=_cparams(("arbitrary",)))(proj, proj, y_pre, dout, dvec, w_glu, b_glu, *dep_ops)


def _ffn_conv(x, prev8, cw, cb):
    y = cb + cw[FFN_CONV - 1:FFN_CONV, :] * x
    for k in range(FFN_CONV - 1):
        y = y + cw[k:k + 1, :] * _shift_down_prev(x, FFN_CONV - 1 - k, prev8)
    return y


def _ffn_up_act(h, wg, cw, cb, name, dep=None):
    s, d = h.shape
    tm = 512
    tb = 2 * FFN_CB
    nt = D_FF // FFN_CB

    def body(h_ref, wgate_ref, wval_ref, cw_ref, cb_ref, *rest):
        up_ref, y_ref, o_ref, ot_ref, carry = rest[-5:]

        @pl.when(pl.program_id(1) == 0)
        def _():
            carry[...] = jnp.zeros_like(carry)
        hb = h_ref[...].astype(BF16)
        x = jnp.concatenate([_dot(hb, wgate_ref[...]), _dot(hb, wval_ref[...])], axis=1)
        up_ref[...] = x
        y = _ffn_conv(x, carry[...], cw_ref[...], cb_ref[...])
        y_ref[...] = y
        carry[...] = x[tm - 8:tm, :]
        act = _gelu(y[:, :FFN_CB]) * y[:, FFN_CB:]
        o_ref[...] = act.astype(BF16)
        ot_ref[...] = act.T.astype(BF16)

    dep_specs, dep_ops = _dep_args(dep)
    return pl.pallas_call(
        body, out_shape=(jax.ShapeDtypeStruct((s, 2 * D_FF), F32), jax.ShapeDtypeStruct((s, 2 * D_FF), F32),
                         jax.ShapeDtypeStruct((s, D_FF), BF16), jax.ShapeDtypeStruct((D_FF, s), BF16)),
        grid=(nt, s // tm),
        in_specs=[pl.BlockSpec((tm, d), lambda t, i: (i, 0)),
                  pl.BlockSpec((None, d, FFN_CB), lambda t, i: (t, 0, 0)),
                  pl.BlockSpec((None, d, FFN_CB), lambda t, i: (t + nt, 0, 0)),
                  pl.BlockSpec((FFN_CONV, tb), lambda t, i: (0, t)),
                  pl.BlockSpec((1, tb), lambda t, i: (0, t))] + dep_specs,
        out_specs=(pl.BlockSpec((tm, tb), lambda t, i: (i, t)), pl.BlockSpec((tm, tb), lambda t, i: (i, t)),
                   pl.BlockSpec((tm, FFN_CB), lambda t, i: (i, t)), pl.BlockSpec((FFN_CB, tm), lambda t, i: (t, i))),
        scratch_shapes=[pltpu.VMEM((8, tb), F32)], name=name,
        compiler_params=_cparams(("parallel", "arbitrary")))(h, wg, wg, cw, cb, *dep_ops)


def _ffn_bwd(up, y_conv, dr, w_down, wg, cw, name):
    s = up.shape[0]
    d = dr.shape[1]
    tm = 512
    tb = 2 * FFN_CB
    nr = s // tm
    nt = D_FF // FFN_CB

    def body(x_ref, y_ref, dr_ref, wd_ref, wgate_ref, wval_ref, cw_ref,
             dup_ref, dh_ref, dcw_ref, dcb_ref, carry):
        i, t = pl.program_id(0), pl.program_id(1)

        @pl.when(i == 0)
        def _():
            carry[t] = jnp.zeros((8, tb), F32)

        @pl.when(t == 0)
        def _():
            dh_ref[...] = ALPHA * dr_ref[...]
        cwv = cw_ref[...]
        x = x_ref[...]
        dact = _dot_nt(dr_ref[...], wd_ref[...])
        gate, val = y_ref[:, :FFN_CB], y_ref[:, FFN_CB:]
        dy = jnp.concatenate([dact * val * _gelu_grad(gate), dact * _gelu(gate)], axis=1)
        next8 = carry[t]
        carry[t] = dy[0:8, :]
        dx = cwv[FFN_CONV - 1:FFN_CONV, :] * dy
        dcw_rows = [None] * FFN_CONV
        dcw_rows[FFN_CONV - 1] = jnp.sum(dy * x, axis=0, keepdims=True)
        for k in range(FFN_CONV - 1):
            dy_ahead = _shift_up_next(dy, FFN_CONV - 1 - k, next8)
            dx = dx + cwv[k:k + 1, :] * dy_ahead
            dcw_rows[k] = jnp.sum(dy_ahead * x, axis=0, keepdims=True)
        dup = dx.astype(BF16)
        dup_ref[...] = dup
        dh_ref[...] += _dot_nt(dup[:, :FFN_CB], wgate_ref[...]) + _dot_nt(dup[:, FFN_CB:], wval_ref[...])
        dcw_ref[...] = jnp.concatenate(dcw_rows, axis=0)
        dcb_ref[...] = jnp.sum(dy, axis=0, keepdims=True)

    row = lambda i: nr - 1 - i
    return pl.pallas_call(
        body, out_shape=(jax.ShapeDtypeStruct((s, 2 * D_FF), BF16), jax.ShapeDtypeStruct((s, d), F32),
                         jax.ShapeDtypeStruct((nr, FFN_CONV, 2 * D_FF), F32),
                         jax.ShapeDtypeStruct((nr, 1, 2 * D_FF), F32)),
        grid=(nr, nt),
        in_specs=[pl.BlockSpec((tm, tb), lambda i, t: (row(i), t)),
                  pl.BlockSpec((tm, tb), lambda i, t: (row(i), t)),
                  pl.BlockSpec((tm, d), lambda i, t: (row(i), 0)),
                  pl.BlockSpec((FFN_CB, d), lambda i, t: (t, 0)),
                  pl.BlockSpec((None, d, FFN_CB), lambda i, t: (t, 0, 0)),
                  pl.BlockSpec((None, d, FFN_CB), lambda i, t: (t + nt, 0, 0)),
                  pl.BlockSpec((FFN_CONV, tb), lambda i, t: (0, t))],
        out_specs=(pl.BlockSpec((tm, tb), lambda i, t: (row(i), t)),
                   pl.BlockSpec((tm, d), lambda i, t: (row(i), 0)),
                   pl.BlockSpec((None, FFN_CONV, tb), lambda i, t: (row(i), 0, t)),
                   pl.BlockSpec((None, 1, tb), lambda i, t: (row(i), 0, t))),
        scratch_shapes=[pltpu.VMEM((nt, 8, tb), F32)], name=name,
        compiler_params=_cparams(("arbitrary", "arbitrary")))(up, y_conv, dr, w_down, wg, wg, cw)


def _sum_partials(ld_ref):
    gg = ld_ref[0].astype(F32)
    for k in range(1, N_DEV):
        gg = gg + ld_ref[k].astype(F32)
    return gg


def _adam_update(w, g, m, v):
    mn = ADAM_B1 * m + (1.0 - ADAM_B1) * g
    vn = ADAM_B2 * v + (1.0 - ADAM_B2) * (g * g)
    m_hat = mn / (1.0 - ADAM_B1 ** ADAM_STEP)
    v_hat = vn / (1.0 - ADAM_B2 ** ADAM_STEP)
    return -ADAM_LR * (m_hat / (jnp.sqrt(v_hat) + ADAM_EPS) + ADAM_WD * w), mn, vn


def _adamw_many(landed, ws, ms, vs, name):
    n, nl = len(ws), len(landed)

    def body(*refs):
        ld = refs[:nl * n]
        w_refs, m_refs, v_refs = (refs[(nl + k) * n:(nl + k + 1) * n] for k in range(3))
        outs = refs[(nl + 3) * n:]
        for i in range(n):
            for l in range(nl):
                one = slice(l, l + 1)
                gg = _sum_partials(ld[l * n + i])
                outs[i][one] = gg
                outs[n + i][one], outs[2 * n + i][one], outs[3 * n + i][one] = _adam_update(
                    w_refs[i][one], gg, m_refs[i][one], v_refs[i][one])

    vm = pl.BlockSpec(memory_space=pltpu.VMEM)
    shapes = [jax.ShapeDtypeStruct(w.shape, F32) for w in ws] * 4
    res = pl.pallas_call(
        body, out_shape=tuple(shapes), in_specs=[vm] * ((nl + 3) * n), out_specs=tuple([vm] * (4 * n)),
        name=name, compiler_params=_cparams())(*[a for layer in landed for a in layer], *ws, *ms, *vs)
    return res[:n], res[n:2 * n], res[2 * n:3 * n], res[3 * n:]


def _adamw_sum(landed, w, m, v, layer, prev, name):
    _, r, c = landed.shape
    nl = w.shape[0]
    tm = 8
    for cand in (512, 256, 128, 64, 32, 16):
        if r % cand == 0 and N_DEV * cand * c * 4 <= 4 * 1024 * 1024:
            tm = cand
            break

    def body(*refs):
        ld_ref, w_ref, m_ref, v_ref = refs[:4]
        g_ref, d_ref, mo_ref, vo_ref = refs[-4:]
        gg = _sum_partials(ld_ref)
        g_ref[...] = gg
        d_ref[...], mo_ref[...], vo_ref[...] = _adam_update(w_ref[...], gg, m_ref[...], v_ref[...])

    blk = pl.BlockSpec((None, tm, c), lambda i: (layer, i, 0))
    in_specs = [pl.BlockSpec((N_DEV, tm, c), lambda i: (0, i, 0)), blk, blk, blk]
    args = [landed, w, m, v]
    aliases = {}
    if prev is not None:
        in_specs += [pl.BlockSpec(memory_space=pl.ANY)] * 4
        args += list(prev)
        aliases = {4 + k: k for k in range(4)}
    shp = jax.ShapeDtypeStruct((nl, r, c), F32)
    return pl.pallas_call(
        body, out_shape=(shp,) * 4, grid=(r // tm,), in_specs=in_specs, out_specs=(blk,) * 4,
        input_output_aliases=aliases, name=name, compiler_params=_cparams(("parallel",)))(*args)


def _all_gather(shards, name):
    na = len(shards)

    def body(*refs):
        x_refs, out_refs = refs[:na], refs[na:2 * na]
        send_sems, recv_sems, local_sems = refs[2 * na:]
        x, y, c = lax.axis_index("x"), lax.axis_index("y"), lax.axis_index("c")
        me, sibling = (x, y, c), (x, y, 1 - c)
        chips = [(1 - x, y), (x, 1 - y), (1 - x, 1 - y)]

        def copy(a, k, block, to, src=None):
            dst = out_refs[a].at[4 * block[0] + 2 * block[1] + block[2]]
            return pltpu.make_async_remote_copy(
                src_ref=dst if src is None else src, dst_ref=dst,
                send_sem=send_sems.at[7 * a + k], recv_sem=recv_sems.at[7 * a + k],
                device_id=to, device_id_type=pl.DeviceIdType.MESH)

        mine, first, passed = [], [], []
        for a in range(na):
            cp = pltpu.make_async_copy(x_refs[a], out_refs[a].at[4 * x + 2 * y + c], local_sems.at[a])
            cp.start()
            mine.append(cp)
            cps = [copy(a, 0, me, sibling, src=x_refs[a])]
            cps += [copy(a, 1 + j, me, (*chip, c), src=x_refs[a]) for j, chip in enumerate(chips)]
            for cp in cps:
                cp.start()
            first += cps
        for j, chip in enumerate(chips):
            for a in range(na):
                copy(a, 1 + j, (*chip, c), me).wait_recv()
                cp = copy(a, 4 + j, (*chip, c), sibling)
                cp.start()
                passed.append(cp)
        for a in range(na):
            copy(a, 0, sibling, me).wait_recv()
            for j, chip in enumerate(chips):
                copy(a, 4 + j, (*chip, 1 - c), me).wait_recv()
        for cp in first + passed:
            cp.wait_send()
        for cp in mine:
            cp.wait()

    anyspec = pl.BlockSpec(memory_space=pl.ANY)
    return pl.pallas_call(
        body, out_shape=tuple(jax.ShapeDtypeStruct((N_DEV,) + t.shape, t.dtype) for t in shards),
        in_specs=[anyspec] * na, out_specs=tuple([anyspec] * na),
        scratch_shapes=[pltpu.SemaphoreType.DMA((7 * na,)), pltpu.SemaphoreType.DMA((7 * na,)),
                        pltpu.SemaphoreType.DMA((na,))],
        name=name)(*shards)


_HBM = pl.BlockSpec(memory_space=pltpu.HBM)
_SEM = pl.BlockSpec(memory_space=pltpu.SEMAPHORE)
_EFFECT = pltpu.SideEffectType.DATAFLOW_SIDE_EFFECTING


def _exchange_copies(src_refs, land_refs, send_sems, recv_sems, local_sems, gather):
    x, y, c = lax.axis_index("x"), lax.axis_index("y"), lax.axis_index("c")
    me = 4 * x + 2 * y + c
    per_array = send_sems.shape[0] > N_DEV - 1
    local, remote = [], []
    for a, (src, land) in enumerate(zip(src_refs, land_refs)):
        local.append(pltpu.make_async_copy(src if gather else src.at[me], land.at[me],
                                           local_sems.at[a if per_array else 0]))
    for k in range(1, N_DEV):
        px = x ^ ((k >> 2) & 1)
        py = y ^ ((k >> 1) & 1)
        pc = c ^ (k & 1)
        for a, (src, land) in enumerate(zip(src_refs, land_refs)):
            remote.append(pltpu.make_async_remote_copy(
                src_ref=src if gather else src.at[4 * px + 2 * py + pc], dst_ref=land.at[me],
                send_sem=send_sems.at[(7 * a if per_array else 0) + k - 1],
                recv_sem=recv_sems.at[(7 * a if per_array else 0) + k - 1],
                device_id=(px, py, pc), device_id_type=pl.DeviceIdType.MESH))
    return local, remote


def _exchange_start(srcs, gather, name, dep=None):
    na = len(srcs)
    ns = na if na <= 4 else 1
    lands = [lax.empty(((N_DEV,) + t.shape) if gather else t.shape, t.dtype) for t in srcs]

    def body(*refs):
        src_refs, land_refs = refs[:na], refs[na:2 * na]
        nin = 2 * na + (0 if dep is None else 1)
        send_sems, recv_sems, local_sems = refs[nin:nin + 3]
        token = refs[-1]
        local, remote = _exchange_copies(src_refs, land_refs, send_sems, recv_sems, local_sems, gather)
        for cp in local + remote:
            cp.start()
        token[...] = jnp.zeros_like(token)

    dep_specs, dep_ops = _dep_args(dep)
    hbm = lambda t: pltpu.HBM(t.shape, t.dtype)
    out = pl.pallas_call(
        body, name=name,
        out_shape=(pltpu.SemaphoreType.DMA((7 * ns,)), pltpu.SemaphoreType.DMA((7 * ns,)),
                   pltpu.SemaphoreType.DMA((ns,)), *[hbm(t) for t in srcs], *[hbm(t) for t in lands],
                   jax.ShapeDtypeStruct((8, LANE), F32)),
        in_specs=[_HBM] * (2 * na) + dep_specs,
        out_specs=(_SEM, _SEM, _SEM, *[_HBM] * (2 * na), pl.BlockSpec(memory_space=pltpu.VMEM)),
        input_output_aliases={i: 3 + i for i in range(2 * na)},
        compiler_params=pltpu.CompilerParams(has_side_effects=_EFFECT),
    )(*[pltpu.with_memory_space_constraint(t, pltpu.HBM) for t in srcs + lands], *dep_ops)
    return (out[:3], out[3:3 + na], out[3 + na:3 + 2 * na]), out[-1]


def _exchange_wait(handle, gather, after, name):
    sems, srcs, lands = handle
    na = len(srcs)

    def body(*refs):
        src_refs, land_refs = refs[:na], refs[na:2 * na]
        send_sems, recv_sems, local_sems = refs[2 * na:2 * na + 3]
        local, remote = _exchange_copies(src_refs, land_refs, send_sems, recv_sems, local_sems, gather)
        for cp in remote:
            cp.wait_send()
            cp.wait_recv()
        for cp in local:
            cp.wait()

    hbm = lambda t: pltpu.HBM(t.shape, t.dtype)
    out = pl.pallas_call(
        body, name=name, out_shape=(*[hbm(t) for t in srcs], *[hbm(t) for t in lands]),
        in_specs=[_HBM] * (2 * na) + [_SEM] * 3 + [pl.BlockSpec(memory_space=pl.ANY)],
        out_specs=tuple([_HBM] * (2 * na)), input_output_aliases={i: i for i in range(2 * na)},
        compiler_params=pltpu.CompilerParams(has_side_effects=_EFFECT),
    )(*srcs, *lands, *sems, after)
    return out[na:]


def _block_diag(w):
    h, a, b = w.shape
    eye = jnp.eye(h, dtype=w.dtype)
    return (w[:, :, None, :] * eye[:, None, :, None]).reshape(h * a, h * b)


def _block_diag_extract(m, h):
    a, b = m.shape[0] // h, m.shape[1] // h
    return jnp.stack([m[i * a:(i + 1) * a, i * b:(i + 1) * b] for i in range(h)], axis=0)


def _block_diag_take(m, h):
    a, b = m.shape[0] // h, m.shape[1] // h
    eye = jnp.eye(h, dtype=m.dtype)
    return (m.reshape(h, a, h, b) * eye[:, None, :, None]).sum(axis=2)


def _ffn_interleave(w):
    lead = w.shape[:-1]
    nb = D_FF // FFN_CB
    return jnp.swapaxes(w.reshape(*lead, 2, nb, FFN_CB), -3, -2).reshape(*lead, 2 * D_FF)


def _ffn_deinterleave(w):
    lead = w.shape[:-1]
    nb = D_FF // FFN_CB
    return jnp.swapaxes(w.reshape(*lead, nb, 2, FFN_CB), -3, -2).reshape(*lead, 2 * D_FF)


def _gather_full(gathered, axis):
    shape = list(gathered.shape[1:])
    shape[axis] *= N_DEV
    return jnp.moveaxis(gathered, 0, axis).reshape(shape)


def _scatter_blocks(full, axis):
    shape = list(full.shape)
    shape[axis:axis + 1] = [N_DEV, shape[axis] // N_DEV]
    return jnp.moveaxis(full.reshape(shape), axis, 0)


def _pad_to(flat, mult):
    pad = (-flat.shape[-1]) % mult
    if pad:
        flat = jnp.concatenate([flat, jnp.zeros(flat.shape[:-1] + (pad,), flat.dtype)], axis=-1)
    return flat


def _layer_fwd(h_in, h_in_t, w, cos, sin, l, dep, get_ffn, target=None):
    tag = "l%d_" % l
    proj, qkv, h_t = _proj_rope(h_in, w['w_in'], cos, sin, tag + "proj_rope", dep=dep,
                                transposed=h_in_t is None)
    h_in_t = h_t if h_in_t is None else h_in_t
    outs, lses = [], []
    for d, qv in zip(DILATIONS, qkv):
        o, ls = _attn_fwd(qv, d, tag + "attn_d%d" % d)
        outs.append(o)
        lses.append(ls)
    lru, *lru_saved = _lru_fwd(proj, w['lru_conv_w'], w['lru_conv_b'], w['lru_wr'], w['lru_br'], w['lru_wi'],
                               w['lru_bi'], w['lru_lambda'], tag + "lru")
    x_re, x_im, y_acc = _s5_scan_fwd(proj, w['s5_bb_re'], w['s5_bb_im'], w['s5_lam_re'], w['s5_lam_im'],
                                     w['s5_cc_re'], w['s5_cc_im'], tag + "s5_scan")
    s5, y_pre = _s5_out_fwd(proj, y_acc, w['s5_d'], w['s5_w_glu'], w['s5_b_glu'], tag + "s5_out")
    w_out = get_ffn(l, s5, 'out')
    if w_out is not None:
        w['w_out'] = w_out
    mixed_t, r1, h1, h1_t, attn_o, attn_lse = _mix_fwd(outs, lses, lru, s5, w['mix_norm_g'], h_in, w['w_out'],
                                                       w['ln1_g'], w['ln1_b'], tag + "mix_out_ln1")
    w['w_up_g'], w['w_down'], ffn_dep = get_ffn(l, h1, 'ffn')
    up, y_conv, act, act_t = _ffn_up_act(h1, w['w_up_g'], w['ffn_conv_w'], w['ffn_conv_b'], tag + "up_act",
                                         dep=ffn_dep)
    r2, out_a, out_b = _proj_ln(act, w['w_down'], h1, w['ln2_g'], w['ln2_b'], tag + "down_ln2", target=target)
    saved = dict(h_in_t=h_in_t, proj=proj, qkv=qkv, lru=lru, lru_saved=lru_saved, x_re=x_re, x_im=x_im,
                 y_pre=y_pre, s5=s5, mixed_t=mixed_t, attn_o=attn_o, attn_lse=attn_lse, r1=r1, h1_t=h1_t, up=up,
                 act_t=act_t, r2=r2, y_conv=y_conv)
    return out_a, out_b, saved


def _layer_bwd_ffn(dh2, sv, w, l, dep=None):
    tag = "l%d_" % l
    g = {}
    dr2, g['ln2_g'], g['ln2_b'] = _ln_bwd(sv['r2'], dh2, w['ln2_g'], tag + "ln2_bwd", dep=dep)
    g['w_down'] = _mm_dw(sv['act_t'], dr2, 1024, D_MODEL, 1024, tag + "down_dw", _grad_dtype(l))
    dup, dh1, dcw_parts, dcb_parts = _ffn_bwd(sv['up'], sv['y_conv'], dr2, w['w_down'], w['w_up_g'],
                                              w['ffn_conv_w'], tag + "ffn_bwd")
    g['ffn_conv_w'] = dcw_parts.sum(axis=0)
    g['ffn_conv_b'] = dcb_parts.sum(axis=0)
    g['w_up_g'] = _mm_up_dw(sv['h1_t'], dup, tag + "up_dw", _grad_dtype(l))
    return dh1, g


def _layer_bwd_mix(dh1, sv, w, cos, sin, l, dep, g_ffn, after_out_grad, after_small_grads, after_in_grad):
    tag = "l%d_" % l
    g = {}
    dr1, g['ln1_g'], g['ln1_b'], d_o, dlru, ds5, g['mix_norm_g'] = _mix_bwd(
        sv['r1'], dh1, w['ln1_g'], w['w_out'], sv['attn_o'][0], sv['lru'], sv['s5'], w['mix_norm_g'],
        tag + "ln1_mix_bwd", dep=dep)
    g['w_out'] = _mm_dw(sv['mixed_t'], dr1, 1024, D_MODEL, 1024, tag + "out_dw", _grad_dtype(l))
    dy, dud, g['s5_d'], g['s5_w_glu'], g['s5_b_glu'] = _s5_out_bwd(
        sv['proj'], sv['y_pre'], ds5, w['s5_d'], w['s5_w_glu'], w['s5_b_glu'], tag + "s5_out_bwd",
        dep=after_out_grad(l, g['w_out']))
    du, g['s5_lam_re'], g['s5_lam_im'], g['s5_bb_re'], g['s5_bb_im'], g['s5_cc_re'], g['s5_cc_im'] = \
        _s5_scan_bwd(sv['proj'], dy, dud, sv['x_re'], sv['x_im'], w['s5_bb_re'], w['s5_bb_im'],
                     w['s5_lam_re'], w['s5_lam_im'], w['s5_cc_re'], w['s5_cc_im'], tag + "s5_scan_bwd")
    (dxr, dgate, g['lru_conv_w'], g['lru_conv_b'], g['lru_wr'], g['lru_br'], g['lru_wi'], g['lru_bi'],
     g['lru_lambda']) = _lru_bwd(sv['proj'], dlru, *sv['lru_saved'], w['lru_conv_w'], w['lru_conv_b'], w['lru_wr'],
                                 w['lru_br'], w['lru_wi'], w['lru_bi'], w['lru_lambda'], tag + "lru_bwd")
    token = after_small_grads(l, _finish_layer_grads({**g_ffn, **g}, w, l))
    dqkv = [_attn_bwd(sv['qkv'][b], sv['attn_o'][b], d_o[b], sv['attn_lse'][b], d, tag + "attn_bwd_d%d" % d,
                      dep=token if b == 0 else None)
            for b, d in enumerate(DILATIONS)]
    dproj = _dproj_assemble(dqkv, dxr, dgate, du, cos, sin, tag + "dproj")
    g_in = _mm_dw(sv['h_in_t'], dproj, 1024, D_IN, 1024, tag + "in_dw", _grad_dtype(l))
    return _mm_nt(dproj, w['w_in'], 512, D_MODEL, tag + "in_dx", add=dr1, add_scale=ALPHA,
                  dep=after_in_grad(l, g_in))


def _s5_rep(a):
    return jnp.repeat(a, S5_C, axis=0)


def _prepare_layer(p, l):
    w = {}
    for n in ('w_in', 'w_out', 's5_w_glu'):
        if n in p:
            w[n] = p[n].astype(BF16)
    w['ffn_conv_w'] = _ffn_interleave(p['ffn_conv_w'])
    w['ffn_conv_b'] = _ffn_interleave(p['ffn_conv_b'])[None, :]
    w['lru_conv_w'] = p['lru_conv_w']
    for n in ('lru_conv_b', 'lru_br', 'lru_bi', 'lru_lambda', 's5_b_glu', 'mix_norm_g',
              'ln1_g', 'ln1_b', 'ln2_g', 'ln2_b'):
        w[n] = p[n][None, :]
    w['lru_wr'] = _block_diag(p['lru_wr']).astype(BF16)
    w['lru_wi'] = _block_diag(p['lru_wi']).astype(BF16)
    w['s5_d'] = p['s5_d'].reshape(1, S5_W)
    disc_in = (_s5_rep(p['s5_a_re']), _s5_rep(p['s5_a_im']),
               _s5_rep(jnp.broadcast_to(p['s5_log_step'][:, None], (S5_G, S5_P))),
               jnp.swapaxes(p['s5_b_re'], 1, 2).reshape(S5_W, S5_P),
               jnp.swapaxes(p['s5_b_im'], 1, 2).reshape(S5_W, S5_P))
    ab_re, ab_im, bb_re, bb_im = _s5_disc_fwd(*disc_in, "l%d_s5_disc" % l)
    w['s5_disc_in'] = disc_in
    w['s5_lam_re'] = ab_re.reshape(S5_G, S5_C, S5_P)[:, 0, :].reshape(1, S5_STATES)
    w['s5_lam_im'] = ab_im.reshape(S5_G, S5_C, S5_P)[:, 0, :].reshape(1, S5_STATES)
    w['s5_bb_re'] = _block_diag(bb_re.reshape(S5_G, S5_C, S5_P)).astype(BF16)
    w['s5_bb_im'] = _block_diag(bb_im.reshape(S5_G, S5_C, S5_P)).astype(BF16)
    w['s5_cc_re'] = _block_diag(jnp.swapaxes(p['s5_c_re'], 1, 2)).astype(BF16)
    w['s5_cc_im'] = _block_diag(jnp.swapaxes(p['s5_c_im'], 1, 2)).astype(BF16)
    return w


def _finish_layer_grads(g, w, l):
    out = {}
    for n in ('s5_w_glu', 'lru_conv_w'):
        out[n] = g[n]
    out['ffn_conv_w'] = _ffn_deinterleave(g['ffn_conv_w'])
    out['ffn_conv_b'] = _ffn_deinterleave(g['ffn_conv_b'])[0]
    for n in ('lru_conv_b', 'lru_br', 'lru_bi', 'lru_lambda', 's5_b_glu', 'mix_norm_g',
              'ln1_g', 'ln1_b', 'ln2_g', 'ln2_b'):
        out[n] = g[n][0]
    out['lru_wr'] = _block_diag_extract(g['lru_wr'], LRU_W // HEAD)
    out['lru_wi'] = _block_diag_extract(g['lru_wi'], LRU_W // HEAD)
    out['s5_d'] = g['s5_d'].reshape(S5_G, S5_C)
    out['s5_c_re'] = jnp.swapaxes(_block_diag_take(g['s5_cc_re'], S5_G), 1, 2)
    out['s5_c_im'] = jnp.swapaxes(_block_diag_take(g['s5_cc_im'], S5_G), 1, 2)
    rep = lambda v: _s5_rep(v.reshape(S5_G, S5_P)) * (1.0 / S5_C)
    cts = (rep(g['s5_lam_re']), rep(g['s5_lam_im']),
           _block_diag_take(g['s5_bb_re'], S5_G).reshape(S5_W, S5_P),
           _block_diag_take(g['s5_bb_im'], S5_G).reshape(S5_W, S5_P))
    da_re, da_im, dls, dbt_re, dbt_im = _s5_disc_bwd(*w['s5_disc_in'], cts, "l%d_s5_disc_bwd" % l)
    out['s5_a_re'] = da_re.reshape(S5_G, S5_C, S5_P).sum(axis=1)
    out['s5_a_im'] = da_im.reshape(S5_G, S5_C, S5_P).sum(axis=1)
    out['s5_log_step'] = dls.reshape(S5_G, S5_C * S5_P).sum(axis=1)
    out['s5_b_re'] = jnp.swapaxes(dbt_re.reshape(S5_G, S5_C, S5_P), 1, 2)
    out['s5_b_im'] = jnp.swapaxes(dbt_im.reshape(S5_G, S5_C, S5_P), 1, 2)
    return out


def _run_step(x, target, get_layer, get_ffn, on_loss, after_ffn_grads, after_out_grad, after_small_grads,
              after_in_grad):
    cos, sin = _rope_tables(x.shape[0])
    h, h_t = x, None
    ws, saved = [], []
    for l in range(DEPTH):
        p, dep = get_layer(l, h)
        ws.append(_prepare_layer(p, l))
        h, h_t, sv = _layer_fwd(h, h_t, ws[l], cos, sin, l, dep, get_ffn, target if l == DEPTH - 1 else None)
        saved.append(sv)
    dh, loss_vec = h, h_t
    on_loss(loss_vec)
    dep = None
    for l in reversed(range(DEPTH)):
        dh1, g = _layer_bwd_ffn(dh, saved[l], ws[l], l, dep)
        dep = after_ffn_grads(l, g)
        dh = _layer_bwd_mix(dh1, saved[l], ws[l], cos, sin, l, dep, g, after_out_grad, after_small_grads,
                            after_in_grad)
        dep = None
    return loss_vec, dh


def _local_step(x, target, layers):
    grads = [{} for _ in range(DEPTH)]

    def ffn(l, after, part):
        if part == 'out':
            return None
        return layers[l]['w_up_g'].astype(BF16), layers[l]['w_down'].astype(BF16), None

    def keep_ffn(l, g):
        grads[l].update(w_up_g=g['w_up_g'], w_down=g['w_down'])

    def keep_small(l, g):
        grads[l].update(g)

    loss, dx = _run_step(x, target, lambda l, h: (layers[l], None), ffn, lambda row: None, keep_ffn,
                         lambda l, g: grads[l].update(w_out=g), keep_small, lambda l, g: grads[l].update(w_in=g))
    return loss[0, 0], dx, grads


def kernel(x, w_in, lru_conv_w, lru_conv_b, lru_wr, lru_br, lru_wi, lru_bi, lru_lambda, s5_a_re, s5_a_im, s5_b_re, s5_b_im, s5_c_re, s5_c_im, s5_d, s5_log_step, s5_w_glu, s5_b_glu, mix_norm_g, w_out, ln1_g, ln1_b, w_up, ffn_conv_w, ffn_conv_b, w_down, ln2_g, ln2_b, loss_target, m_w_in, m_lru_conv_w, m_lru_conv_b, m_lru_wr, m_lru_br, m_lru_wi, m_lru_bi, m_lru_lambda, m_s5_a_re, m_s5_a_im, m_s5_b_re, m_s5_b_im, m_s5_c_re, m_s5_c_im, m_s5_d, m_s5_log_step, m_s5_w_glu, m_s5_b_glu, m_mix_norm_g, m_w_out, m_ln1_g, m_ln1_b, m_w_up, m_ffn_conv_w, m_ffn_conv_b, m_w_down, m_ln2_g, m_ln2_b, v_w_in, v_lru_conv_w, v_lru_conv_b, v_lru_wr, v_lru_br, v_lru_wi, v_lru_bi, v_lru_lambda, v_s5_a_re, v_s5_a_im, v_s5_b_re, v_s5_b_im, v_s5_c_re, v_s5_c_im, v_s5_d, v_s5_log_step, v_s5_w_glu, v_s5_b_glu, v_mix_norm_g, v_w_out, v_ln1_g, v_ln1_b, v_w_up, v_ffn_conv_w, v_ffn_conv_b, v_w_down, v_ln2_g, v_ln2_b):
    args = locals()
    wl = {n: args[n] for n in WEIGHTS}
    ml = {n: args['m_' + n] for n in WEIGHTS}
    vl = {n: args['v_' + n] for n in WEIGHTS}

    small_sizes = [int(wl[n].size) for n in SMALL_SHARDED]
    small_flat = _pad_to(jnp.concatenate([wl[n].reshape(-1) for n in SMALL_SHARDED]), 8 * 1024)
    small_all, w_in0 = _all_gather([small_flat.reshape(-1, 1024), wl['w_in'][0].astype(BF16)], "gather_first")
    small_all = small_all.reshape(N_DEV, -1)
    small_full, off = {}, 0
    for n, sz in zip(SMALL_SHARDED, small_sizes):
        small_full[n] = _gather_full(small_all[:, off:off + sz].reshape((N_DEV,) + wl[n].shape), SHARD_AXIS[n])
        off += sz
    def mixer_params(l, g_in, g_out):
        p = {n: wl[n][l] for n in REPLICATED}
        p.update({n: small_full[n][l] for n in SMALL_SHARDED})
        p['w_in'] = _gather_full(g_in, 1)
        if g_out is not None:
            p['w_out'] = g_out.reshape(D_MODEL, D_MODEL)
        return p

    mix_names, ffn_names = ('w_in', 'w_out'), ('w_up', 'w_down')
    shards = lambda names, l: [wl[n][l].astype(BF16) for n in names]
    gathers = {}
    gathers[0, 'out'], token = _exchange_start(shards(('w_out',), 0), True, "gather_out_l0_start", dep=w_in0)
    gathers[0, 'ffn'], rest0_token = _exchange_start(shards(ffn_names, 0), True, "gather_ffn_l0_start", dep=token)

    def get_layer(l, h):
        if l == 0:
            return mixer_params(0, w_in0, None), rest0_token
        return mixer_params(1, *_exchange_wait(gathers[1, 'mix'], True, h, "gather_mix_l1_wait")), None

    def get_ffn(l, after, part):
        if part == 'out':
            if l > 0:
                return None
            g_out, = _exchange_wait(gathers[0, 'out'], True, after, "gather_out_l0_wait")
            return g_out.reshape(D_MODEL, D_MODEL)
        g_up, g_down = _exchange_wait(gathers[l, 'ffn'], True, after, "gather_ffn_l%d_wait" % l)
        token = None
        if l == 0:
            gathers[1, 'mix'], token = _exchange_start(shards(mix_names, 1), True, "gather_mix_l1_start", dep=g_up)
            gathers[1, 'ffn'], token = _exchange_start(shards(ffn_names, 1), True, "gather_ffn_l1_start", dep=token)
        return g_up, g_down.reshape(D_FF, D_MODEL), token

    scatters = {}

    def after_ffn_grads(l, g):
        send = [g['w_up_g'], g['w_down'].reshape(N_DEV, D_FF // N_DEV, D_MODEL)]
        scatters[l, 'ffn'], token = _exchange_start(send, False, "scatter_ffn_l%d_start" % l)
        return token

    def after_out_grad(l, g_out):
        send = [g_out.reshape(N_DEV, D_MODEL // N_DEV, D_MODEL)]
        scatters[l, 'out'], token = _exchange_start(send, False, "scatter_out_l%d_start" % l)
        return token

    def after_in_grad(l, g_in):
        scatters[l, 'in'], token = _exchange_start([_scatter_blocks(g_in, 1)], False, "scatter_in_l%d_start" % l)
        return token

    def after_small_grads(l, g):
        rep = [g[n][None] for n in REPLICATED]
        if l == DEPTH - 1:
            rep.append(loss_rows[0][None])
        shd = [_scatter_blocks(g[n], SHARD_AXIS[n] - 1)[:, None] for n in SMALL_SHARDED]
        scatters[l, 'rep'], token = _exchange_start(rep, True, "gather_rep_grads_l%d_start" % l)
        scatters[l, 'small'], token = _exchange_start(shd, False, "scatter_small_l%d_start" % l, dep=token)
        return token

    loss_rows = []
    _, grad_x = _run_step(x[0], loss_target[0], get_layer, get_ffn, loss_rows.append, after_ffn_grads,
                          after_out_grad, after_small_grads, after_in_grad)

    results = {}
    big_prev = {n: None for n in BIG}

    def finish_big(l, part, names, after):
        landed = _exchange_wait(scatters[l, part], False, after, "scatter_%s_l%d_wait" % (part, l))
        for n, ld in zip(names, landed):
            big_prev[n] = _adamw_sum(ld, wl[n], ml[n], vl[n], l, big_prev[n], "adamw_%s_l%d" % (n, l))

    for l, part, names in ((1, 'ffn', ffn_names), (1, 'out', ('w_out',)), (1, 'in', ('w_in',)),
                           (0, 'ffn', ffn_names), (0, 'out', ('w_out',))):
        finish_big(l, part, names, grad_x)

    kinds = ('grad', 'delta', 'm', 'v')
    landed = []
    for l in range(DEPTH):
        rep = list(_exchange_wait(scatters[l, 'rep'], True, grad_x, "gather_rep_grads_l%d_wait" % l))
        if l == DEPTH - 1:
            loss = jnp.sum(rep.pop()[:, 0, 0, 0])
        shd = list(_exchange_wait(scatters[l, 'small'], False, grad_x, "scatter_small_l%d_wait" % l))
        landed.append(dict(zip(REPLICATED + SMALL_SHARDED, rep + shd)))
    matrices = ['lru_wr', 'lru_wi', 's5_a_re', 's5_a_im', 's5_c_re', 's5_c_im', 's5_d']
    widest = ['s5_b_re', 's5_b_im']
    vectors = [n for n in REPLICATED + SMALL_SHARDED if n not in matrices + widest]
    last = None
    for tag, names in (("vectors", vectors), ("matrices", matrices), ("s5_b", widest)):
        res = _adamw_many([[landed[l][n] for n in names] for l in range(DEPTH)], [wl[n] for n in names],
                          [ml[n] for n in names], [vl[n] for n in names], "adamw_" + tag)
        for kind, arrs in zip(kinds, res):
            for n, a in zip(names, arrs):
                results[kind, n] = a
        last = res[0][0]
    finish_big(0, 'in', ('w_in',), last)
    for n in BIG:
        results['grad', n], results['delta', n], results['m', n], results['v', n] = big_prev[n]

    out = [loss, grad_x[None]]
    for kind in kinds:
        out.extend(results[kind, n] for n in WEIGHTS)
    return tuple(out)
```

```python
import math

import jax
import jax.numpy as jnp
from jax import lax
from jax.experimental import pallas as pl
from jax.experimental.pallas import tpu as pltpu

F32 = jnp.float32
BF16 = jnp.bfloat16

N_DEV = 8
DEPTH = 2
D_MODEL = 1024
ATTN_W = 384
LRU_W = 384
S5_W = 256
D_IN = 2176
D_FF = 3072
HEAD = 64
ATTN_BLK = 128
ATTN_TILE = 1024
DILATIONS = (1, 4, 16)
S5_G = 16
S5_P = 64
S5_C = 16
S5_STATES = S5_G * S5_P
LRU_C = 8.0
LRU_CONV = 4
FFN_CONV = 3
ROPE_THETA = 10000.0
ALPHA = (2 * DEPTH) ** 0.25
LN_EPS = 1e-5
RMS_EPS = 1e-6
ADAM_LR, ADAM_B1, ADAM_B2, ADAM_EPS, ADAM_WD, ADAM_STEP = 0.001, 0.9, 0.999, 1e-8, 0.01, 10

LANE = 128
SCAN_T = 512
S5_BLK = 256
FFN_CB = 2 * D_FF // N_DEV
VMEM_LIMIT = 56 * 1024 * 1024

WEIGHTS = ['w_in', 'lru_conv_w', 'lru_conv_b', 'lru_wr', 'lru_br', 'lru_wi', 'lru_bi', 'lru_lambda',
           's5_a_re', 's5_a_im', 's5_b_re', 's5_b_im', 's5_c_re', 's5_c_im', 's5_d', 's5_log_step',
           's5_w_glu', 's5_b_glu', 'mix_norm_g', 'w_out', 'ln1_g', 'ln1_b', 'w_up', 'ffn_conv_w',
           'ffn_conv_b', 'w_down', 'ln2_g', 'ln2_b']
SHARD_AXIS = {'w_in': 2, 'lru_conv_w': 2, 's5_w_glu': 1, 'w_out': 1, 'w_up': 2, 'ffn_conv_w': 2, 'w_down': 1}
BIG = ['w_in', 'w_out', 'w_up', 'w_down']
SMALL_SHARDED = ['lru_conv_w', 'ffn_conv_w', 's5_w_glu']
REPLICATED = [n for n in WEIGHTS if n not in SHARD_AXIS]


def _cparams(sem=None):
    return pltpu.CompilerParams(dimension_semantics=sem, vmem_limit_bytes=VMEM_LIMIT)


def _grad_dtype(l):
    return BF16 if l == 0 else F32


def _ffn_dev(jb):
    return jb // 2 + (N_DEV // 2) * (jb % 2)


def _gelu(x):
    c = math.sqrt(2.0 / math.pi)
    t = jnp.tanh(c * (x + 0.044715 * (x * x * x)))
    return 0.5 * x * (1.0 + t)


def _gelu_grad(x):
    c = math.sqrt(2.0 / math.pi)
    x2 = x * x
    t = jnp.tanh(c * (x + 0.044715 * (x2 * x)))
    return 0.5 * (1.0 + t) + 0.5 * x * (1.0 - t * t) * (c * (1.0 + 3.0 * 0.044715 * x2))


def _sigmoid(x):
    return 1.0 / (1.0 + jnp.exp(-x))


def _log1p(x):
    u = 1.0 + x
    d = u - 1.0
    return jnp.where(d == 0.0, x, jnp.log(u) * (x / jnp.where(d == 0.0, 1.0, d)))


def _softplus(x):
    return jnp.maximum(x, 0.0) + _log1p(jnp.exp(-jnp.abs(x)))


def _expm1(x):
    return jnp.tanh(0.5 * x) * (jnp.exp(x) + 1.0)


def _dot(a, b):
    return jnp.dot(a.astype(BF16), b.astype(BF16), preferred_element_type=F32)


def _dot_nt(a, b):
    return lax.dot_general(a.astype(BF16), b.astype(BF16), (((1,), (1,)), ((), ())),
                           preferred_element_type=F32)


def _dot_tn(a, b):
    return lax.dot_general(a.astype(BF16), b.astype(BF16), (((0,), (0,)), ((), ())),
                           preferred_element_type=F32)


def _rows(shape):
    return lax.broadcasted_iota(jnp.int32, shape, 0)


def _shift_down_prev(x, s, prev8):
    if s == 0:
        return x
    t, l = x.shape
    r = pltpu.roll(x, s, axis=0)
    pr = pltpu.roll(prev8, s, axis=0)
    pad = jnp.concatenate([pr, jnp.zeros((t - 8, l), x.dtype)], axis=0)
    return jnp.where(_rows(x.shape) < s, pad, r)


def _shift_up_next(x, s, next8):
    if s == 0:
        return x
    t, l = x.shape
    r = pltpu.roll(x, t - s, axis=0)
    nx = pltpu.roll(next8, 8 - s, axis=0)
    pad = jnp.concatenate([jnp.zeros((t - 8, l), x.dtype), nx], axis=0)
    return jnp.where(_rows(x.shape) >= t - s, pad, r)


SUB = 8


def _tile_shift(x, s, fill, reverse):
    t = x.shape[0]
    pos = _rows(x.shape) & (SUB - 1)
    if reverse:
        return jnp.where(pos < SUB - s, pltpu.roll(x, t - s, axis=0), fill)
    return jnp.where(pos >= s, pltpu.roll(x, s, axis=0), fill)


def _scan_chunk(a, x, carry, reverse=False):
    s = 1
    while s < SUB:
        x = x + a * _tile_shift(x, s, 0.0, reverse)
        a = a * _tile_shift(a, s, 1.0, reverse)
        s *= 2
    nv = x.shape[0] // SUB
    out = [None] * nv
    for v in (reversed(range(nv)) if reverse else range(nv)):
        rows = slice(v * SUB, (v + 1) * SUB)
        out[v] = x[rows, :] + a[rows, :] * carry
        carry = out[v][0:1, :] if reverse else out[v][SUB - 1:SUB, :]
    return jnp.concatenate(out, axis=0)


def _cmul(ar, ai, br, bi):
    return ar * br - ai * bi, ar * bi + ai * br


def _cscan_consts(lr, li, reverse):
    pows = [(lr, li)]
    for _ in range(2):
        pows.append(_cmul(*pows[-1], *pows[-1]))
    rows = [(lr, li)]
    for _ in range(SUB - 1):
        rows.append(_cmul(*rows[-1], lr, li))
    if reverse:
        rows = rows[::-1]
    return pows, (jnp.concatenate([r for r, _ in rows], axis=0), jnp.concatenate([i for _, i in rows], axis=0))


def _cscan_chunk(xr, xi, consts, carry, reverse=False):
    pows, (p8r, p8i) = consts
    s = 1
    for pr, pi in pows:
        sr = _tile_shift(xr, s, 0.0, reverse)
        si = _tile_shift(xi, s, 0.0, reverse)
        xr, xi = xr + pr * sr - pi * si, xi + pr * si + pi * sr
        s *= 2
    nv = xr.shape[0] // SUB
    out_r, out_i = [None] * nv, [None] * nv
    cr, ci = carry
    for v in (reversed(range(nv)) if reverse else range(nv)):
        rows = slice(v * SUB, (v + 1) * SUB)
        out_r[v] = xr[rows, :] + p8r * cr - p8i * ci
        out_i[v] = xi[rows, :] + p8r * ci + p8i * cr
        edge = slice(0, 1) if reverse else slice(SUB - 1, SUB)
        cr, ci = out_r[v][edge, :], out_i[v][edge, :]
    return jnp.concatenate(out_r, axis=0), jnp.concatenate(out_i, axis=0)


def _dep_args(dep):
    return ([], []) if dep is None else ([pl.BlockSpec(memory_space=pl.ANY)], [dep])


def _mm_nt(a, w, tm, tn, name, add=None, add_scale=1.0, dep=None):
    m, k = a.shape
    n = w.shape[0]

    def body(a_ref, w_ref, *rest):
        o_ref = rest[-1]
        if add is None:
            o_ref[...] = _dot_nt(a_ref[...], w_ref[...])
        else:
            o_ref[...] = _dot_nt(a_ref[...], w_ref[...]) + add_scale * rest[0][...]

    in_specs = [pl.BlockSpec((tm, k), lambda j, i: (i, 0)), pl.BlockSpec((tn, k), lambda j, i: (j, 0))]
    args = [a, w]
    if add is not None:
        in_specs.append(pl.BlockSpec((tm, tn), lambda j, i: (i, j)))
        args.append(add)
    dep_specs, dep_ops = _dep_args(dep)
    return pl.pallas_call(
        body, out_shape=jax.ShapeDtypeStruct((m, n), F32), grid=(n // tn, m // tm),
        in_specs=in_specs + dep_specs, out_specs=pl.BlockSpec((tm, tn), lambda j, i: (i, j)), name=name,
        compiler_params=_cparams(("parallel", "parallel")))(*args, *dep_ops)


def _mm_dw(at, b, tm, tn, ts, name, out_dtype=F32):
    m, s = at.shape
    n = b.shape[1]
    nk = s // ts

    def body(a_ref, b_ref, o_ref, acc):
        @pl.when(pl.program_id(2) == 0)
        def _():
            acc[...] = jnp.zeros_like(acc)
        acc[...] += _dot(a_ref[...], b_ref[...])

        @pl.when(pl.program_id(2) == nk - 1)
        def _():
            o_ref[...] = acc[...].astype(out_dtype)

    return pl.pallas_call(
        body, out_shape=jax.ShapeDtypeStruct((m, n), out_dtype), grid=(m // tm, n // tn, nk),
        in_specs=[pl.BlockSpec((tm, ts), lambda i, j, k: (i, k)), pl.BlockSpec((ts, tn), lambda i, j, k: (k, j))],
        out_specs=pl.BlockSpec((tm, tn), lambda i, j, k: (i, j)),
        scratch_shapes=[pltpu.VMEM((tm, tn), F32)], name=name,
        compiler_params=_cparams(("parallel", "parallel", "arbitrary")))(at, b)


def _mm_up_dw(ht, dup, name, out_dtype=F32):
    d, s = ht.shape

    def body(a_ref, b_ref, o_ref):
        o_ref[...] = _dot(a_ref[...], b_ref[...]).astype(out_dtype)

    return pl.pallas_call(
        body, out_shape=jax.ShapeDtypeStruct((N_DEV, d, FFN_CB), out_dtype), grid=(N_DEV,),
        in_specs=[pl.BlockSpec((d, s), lambda j: (0, 0)), pl.BlockSpec((s, FFN_CB), lambda j: (0, j))],
        out_specs=pl.BlockSpec((None, d, FFN_CB), lambda j: (_ffn_dev(j), 0, 0)), name=name,
        compiler_params=_cparams(("parallel",)))(ht, dup)


def _layer_norm(r, g, b):
    mu = jnp.mean(r, axis=-1, keepdims=True)
    xc = r - mu
    var = jnp.mean(xc * xc, axis=-1, keepdims=True)
    return xc * lax.rsqrt(var + LN_EPS) * g + b


def _proj_ln(a, w, resid, g, bias, name, transposed=True, target=None):
    s, k = a.shape
    d = w.shape[1]
    tm = 512

    def body(a_ref, w_ref, x_ref, g_ref, bias_ref, *rest):
        r = ALPHA * x_ref[...] + _dot(a_ref[...], w_ref[...])
        h = _layer_norm(r, g_ref[...], bias_ref[...])
        if target is None:
            r_ref, h_ref = rest[0], rest[1]
            h_ref[...] = h
            if transposed:
                rest[2][...] = h.T.astype(BF16)
        else:
            t_ref, r_ref, dy_ref, l_ref = rest

            @pl.when(pl.program_id(0) == 0)
            def _():
                l_ref[...] = jnp.zeros_like(l_ref)
            e = h - t_ref[...]
            dy_ref[...] = e * (1.0 / d)
            part = 0.5 * jnp.sum(jnp.mean(e * e, axis=-1, keepdims=True), axis=0, keepdims=True)
            l_ref[...] += jnp.broadcast_to(part, l_ref.shape)
        r_ref[...] = r

    row = pl.BlockSpec((tm, d), lambda i: (i, 0))
    vec = pl.BlockSpec((1, d), lambda i: (0, 0))
    in_specs = [pl.BlockSpec((tm, k), lambda i: (i, 0)), pl.BlockSpec((k, d), lambda i: (0, 0)), row, vec, vec]
    args = [a, w, resid, g, bias]
    shapes = [jax.ShapeDtypeStruct((s, d), F32), jax.ShapeDtypeStruct((s, d), F32)]
    specs = [row, row]
    if target is not None:
        in_specs.append(row)
        args.append(target)
        shapes.append(jax.ShapeDtypeStruct((1, LANE), F32))
        specs.append(pl.BlockSpec((1, LANE), lambda i: (0, 0)))
    elif transposed:
        shapes.append(jax.ShapeDtypeStruct((d, s), BF16))
        specs.append(pl.BlockSpec((d, tm), lambda i: (0, i)))
    return pl.pallas_call(
        body, out_shape=tuple(shapes), grid=(s // tm,), in_specs=in_specs, out_specs=tuple(specs), name=name,
        compiler_params=_cparams(("arbitrary",) if target is not None else ("parallel",)))(*args)


def _layer_norm_bwd(r, dh, g):
    mu = jnp.mean(r, axis=-1, keepdims=True)
    xc = r - mu
    var = jnp.mean(xc * xc, axis=-1, keepdims=True)
    rstd = lax.rsqrt(var + LN_EPS)
    xh = xc * rstd
    dxh = dh * g
    m1 = jnp.mean(dxh, axis=-1, keepdims=True)
    m2 = jnp.mean(dxh * xh, axis=-1, keepdims=True)
    return (rstd * (dxh - m1 - xh * m2), jnp.sum(dh * xh, axis=0, keepdims=True),
            jnp.sum(dh, axis=0, keepdims=True))


def _ln_bwd(r, dh, g, name, dep=None):
    s, d = r.shape
    tm = 512

    def body(r_ref, dh_ref, g_ref, *rest):
        dr_ref, dg_ref, db_ref = rest[-3:]

        @pl.when(pl.program_id(0) == 0)
        def _():
            dg_ref[...] = jnp.zeros_like(dg_ref)
            db_ref[...] = jnp.zeros_like(db_ref)
        dr_ref[...], dg_rows, db_rows = _layer_norm_bwd(r_ref[...], dh_ref[...], g_ref[...])
        dg_ref[...] += dg_rows
        db_ref[...] += db_rows

    row = pl.BlockSpec((tm, d), lambda i: (i, 0))
    vec = pl.BlockSpec((1, d), lambda i: (0, 0))
    dep_specs, dep_ops = _dep_args(dep)
    return pl.pallas_call(
        body, out_shape=(jax.ShapeDtypeStruct((s, d), F32), jax.ShapeDtypeStruct((1, d), F32),
                         jax.ShapeDtypeStruct((1, d), F32)),
        grid=(s // tm,), in_specs=[row, row, vec] + dep_specs, out_specs=(row, vec, vec), name=name,
        compiler_params=_cparams(("arbitrary",)))(r, dh, g, *dep_ops)


def _rope_tables(s):
    half = HEAD // 2
    pos = jnp.arange(s, dtype=F32)
    inv = ROPE_THETA ** (-jnp.arange(half, dtype=F32) * 2.0 / HEAD)
    ang = pos[:, None] * inv[None, :]
    cos, sin = jnp.cos(ang), jnp.sin(ang)
    cos = jnp.concatenate([cos, cos, cos, cos], axis=1)
    sin = jnp.concatenate([-sin, sin, -sin, sin], axis=1)
    return cos, sin


def _rotate(x, cos, sin):
    lane = lax.broadcasted_iota(jnp.int32, x.shape, 1)
    partner = jnp.where((lane % HEAD) < HEAD // 2, pltpu.roll(x, LANE - HEAD // 2, axis=1),
                        pltpu.roll(x, HEAD // 2, axis=1))
    return x * cos + partner * sin


def _class_rows(c, d, tm):
    return pl.ds(c, tm // d, stride=d) if d > 1 else pl.ds(0, tm)


def _dilated_spec(tm, d, w):
    return pl.BlockSpec((tm // d, d * w), lambda i: (i, 0))


def _token_scratch(tm, w):
    return pltpu.VMEM((w // LANE, tm, LANE), F32)


def _to_tokens(src_ref, dst3, d, tm):
    nj = dst3.shape[0]
    for cls in range(d):
        for j in range(nj):
            col = (cls * nj + j) * LANE
            dst3.at[j][_class_rows(cls, d, tm), :] = src_ref[:, col:col + LANE]


def _to_dilated(src3, dst_ref, d, tm):
    nj = src3.shape[0]
    for cls in range(d):
        for j in range(nj):
            col = (cls * nj + j) * LANE
            dst_ref[:, col:col + LANE] = src3.at[j][_class_rows(cls, d, tm), :].astype(dst_ref.dtype)


def _token_value(src3):
    return jnp.concatenate([src3[j] for j in range(src3.shape[0])], axis=1)


def _proj_rope(h, w_in, cos, sin, name, dep=None, transposed=False):
    s, d_model = h.shape
    tm = 512
    w = 3 * ATTN_W
    nj = w // LANE

    def body(h_ref, w_ref, c_ref, s_ref, *rest):
        rot = rest[-1]
        if transposed:
            p_ref, o_refs, ht_ref = rest[-6], rest[-5:-2], rest[-2]
            ht_ref[...] = h_ref[...].T.astype(BF16)
        else:
            p_ref, o_refs = rest[-5], rest[-4:-1]
        y = _dot(h_ref[...], w_ref[...])
        p_ref[...] = y
        c, sn = c_ref[...], s_ref[...]
        for j in range(nj):
            x = y[:, j * LANE:(j + 1) * LANE]
            rot[j] = _rotate(x, c, sn) if j < 2 * ATTN_W // LANE else x
        for d, o_ref in zip(DILATIONS, o_refs):
            _to_dilated(rot, o_ref, d, tm)

    tab = pl.BlockSpec((tm, LANE), lambda i: (i, 0))
    dep_specs, dep_ops = _dep_args(dep)
    shapes = [jax.ShapeDtypeStruct((s, D_IN), F32), *[jax.ShapeDtypeStruct((s // d, d * w), BF16) for d in DILATIONS]]
    specs = [pl.BlockSpec((tm, D_IN), lambda i: (i, 0)), *[_dilated_spec(tm, d, w) for d in DILATIONS]]
    if transposed:
        shapes.append(jax.ShapeDtypeStruct((d_model, s), BF16))
        specs.append(pl.BlockSpec((d_model, tm), lambda i: (0, i)))
    res = pl.pallas_call(
        body, out_shape=tuple(shapes), grid=(s // tm,),
        in_specs=[pl.BlockSpec((tm, d_model), lambda i: (i, 0)), pl.BlockSpec((d_model, D_IN), lambda i: (0, 0)),
                  tab, tab] + dep_specs,
        out_specs=tuple(specs), scratch_shapes=[_token_scratch(tm, w)], name=name,
        compiler_params=_cparams(("parallel",)))(h, w_in, cos, sin, *dep_ops)
    return res[0], res[1:4], (res[4] if transposed else None)


def _dproj_assemble(dqkv_list, dxr, dgate, du, cos, sin, name):
    s = dxr.shape[0]
    tm = 512
    nq = 3 * ATTN_W // LANE

    def body(*refs):
        br = refs[:9]
        dxr_ref, dg_ref, du_ref, c_ref, s_ref, o_ref = refs[9:15]
        tok = refs[15:]
        c, sn = c_ref[...], -s_ref[...]
        for part in range(3):
            for b, d in enumerate(DILATIONS[1:], start=1):
                _to_tokens(br[3 * b + part], tok[2 * part + b - 1], d, tm)
        for j in range(nq):
            part, jj = divmod(j, ATTN_W // LANE)
            x = br[part][:, jj * LANE:(jj + 1) * LANE] + tok[2 * part][jj] + tok[2 * part + 1][jj]
            if part < 2:
                x = _rotate(x, c, sn)
            o_ref[:, j * LANE:(j + 1) * LANE] = x.astype(BF16)
        o_ref[:, 3 * ATTN_W:3 * ATTN_W + LRU_W] = dxr_ref[...].astype(BF16)
        o_ref[:, 3 * ATTN_W + LRU_W:3 * ATTN_W + 2 * LRU_W] = dg_ref[...].astype(BF16)
        o_ref[:, 3 * ATTN_W + 2 * LRU_W:] = du_ref[...].astype(BF16)

    a_spec = pl.BlockSpec((tm, ATTN_W), lambda i: (i, 0))
    tab = pl.BlockSpec((tm, LANE), lambda i: (i, 0))
    ordered = [dqkv_list[b][p] for b in range(3) for p in range(3)]
    d_specs = [_dilated_spec(tm, d, ATTN_W) for d in DILATIONS for _ in range(3)]
    return pl.pallas_call(
        body, out_shape=jax.ShapeDtypeStruct((s, D_IN), BF16), grid=(s // tm,),
        in_specs=d_specs + [a_spec, a_spec, pl.BlockSpec((tm, S5_W), lambda i: (i, 0)), tab, tab],
        out_specs=pl.BlockSpec((tm, D_IN), lambda i: (i, 0)),
        scratch_shapes=[_token_scratch(tm, ATTN_W)] * 6, name=name,
        compiler_params=_cparams(("parallel",)))(*ordered, dxr, dgate, du, cos, sin)


def _attn_tiles(s, d):
    m = s // d
    tq = min(m, ATTN_TILE)
    return m, tq, tq // ATTN_BLK


def _band_mask(qb):
    qi = lax.broadcasted_iota(jnp.int32, (ATTN_BLK, 2 * ATTN_BLK), 0)
    ki = lax.broadcasted_iota(jnp.int32, (ATTN_BLK, 2 * ATTN_BLK), 1)
    dist = qi + ATTN_BLK - ki
    return (dist >= 0) & (dist <= ATTN_BLK) & ((ki >= ATTN_BLK) | (qb > 0))


def _attn_fwd(qv, d, name):
    m = qv.shape[0]
    w3 = 3 * ATTN_W
    _, tq, n = _attn_tiles(m * d, d)
    scale = HEAD ** -0.5

    def body(x_ref, p_ref, o_ref, l_ref):
        b = pl.program_id(1)

        def block(i, first):
            r0 = 0 if first else pl.multiple_of(i * ATTN_BLK, ATTN_BLK)
            rows = pl.ds(r0, ATTN_BLK)
            valid = _band_mask(b * n + i)
            if not first:
                krows = pl.ds(pl.multiple_of(i * ATTN_BLK - ATTN_BLK, ATTN_BLK), 2 * ATTN_BLK)
            low = lax.broadcasted_iota(jnp.int32, (1, LANE), 1) < HEAD
            for hp in range(ATTN_W // LANE):
                qs, ks, vs = (slice(part * ATTN_W + hp * LANE, part * ATTN_W + (hp + 1) * LANE) for part in range(3))
                q2 = x_ref[rows, qs]
                if first:
                    k2 = jnp.concatenate([p_ref[:, ks], x_ref[0:ATTN_BLK, ks]], axis=0)
                    v2 = jnp.concatenate([p_ref[:, vs], x_ref[0:ATTN_BLK, vs]], axis=0)
                else:
                    k2 = x_ref[krows, ks]
                    v2 = x_ref[krows, vs]
                outs, lses = [], []
                for mask in (low, ~low):
                    q = jnp.where(mask, q2, jnp.zeros_like(q2))
                    sc = jnp.where(valid, _dot_nt(q, k2) * scale, -1e30)
                    mx = jnp.max(sc, axis=-1, keepdims=True)
                    p = jnp.exp(sc - mx)
                    l = jnp.sum(p, axis=-1, keepdims=True)
                    outs.append(_dot(p, v2) / l)
                    lses.append(mx + jnp.log(l))
                o_ref[rows, hp * LANE:(hp + 1) * LANE] = jnp.where(low, outs[0], outs[1])
                l_ref[rows, hp * LANE:(hp + 1) * LANE] = jnp.where(low, lses[0], lses[1])

        block(0, True)
        if n > 1:
            def loop(i, carry):
                block(i, False)
                return carry
            lax.fori_loop(1, n, loop, 0)

    shp = jax.ShapeDtypeStruct((m, d * ATTN_W), F32)
    ospec = pl.BlockSpec((tq, ATTN_W), lambda c, b: (b, c))
    out, lse = pl.pallas_call(
        body, out_shape=(shp, shp), grid=(d, m // tq),
        in_specs=[pl.BlockSpec((tq, w3), lambda c, b: (b, c)),
                  pl.BlockSpec((ATTN_BLK, w3), lambda c, b: (jnp.maximum(b * n - 1, 0), c))],
        out_specs=(ospec, ospec), name=name,
        compiler_params=_cparams(("parallel", "parallel")))(qv, qv)
    return out, lse


def _attn_bwd(qv, ov, dov, lv, d, name, dep=None):
    m = qv.shape[0]
    w3 = 3 * ATTN_W
    _, tq, n = _attn_tiles(m * d, d)
    nb = m // ATTN_BLK
    scale = HEAD ** -0.5

    def body(x_ref, p_ref, nx_ref, o_ref, do_ref, l_ref, on_ref, don_ref, ln_ref, *rest):
        dq_ref, dk_ref, dv_ref = rest[-3:]
        b = pl.program_id(1)
        dk_ref[...] = jnp.zeros_like(dk_ref)
        dv_ref[...] = jnp.zeros_like(dv_ref)

        low = lax.broadcasted_iota(jnp.int32, (1, LANE), 1) < HEAD

        def pair_grads(q2, k2, v2, o2, do2, l2, valid):
            dq, dk, dv = [], 0.0, 0.0
            for mask, lse in ((low, l2[:, 0:1]), (~low, l2[:, HEAD:HEAD + 1])):
                q = jnp.where(mask, q2, jnp.zeros_like(q2))
                do = jnp.where(mask, do2, 0.0)
                sc = jnp.where(valid, _dot_nt(q, k2) * scale, -1e30)
                p = jnp.exp(sc - lse)
                delta = jnp.sum(do * o2, axis=-1, keepdims=True)
                ds = p * (_dot_nt(do, v2) - delta) * scale
                dq.append(_dot(ds, k2))
                dk = dk + _dot_tn(ds, q)
                dv = dv + _dot_tn(p, do)
            return jnp.where(low, dq[0], dq[1]), dk, dv

        def cols(hp):
            return [slice(part * ATTN_W + hp * LANE, part * ATTN_W + (hp + 1) * LANE) for part in range(3)]

        def block(i, first):
            r0 = 0 if first else pl.multiple_of(i * ATTN_BLK, ATTN_BLK)
            rows = pl.ds(r0, ATTN_BLK)
            valid = _band_mask(b * n + i)
            if not first:
                krows = pl.ds(pl.multiple_of(i * ATTN_BLK - ATTN_BLK, ATTN_BLK), 2 * ATTN_BLK)
            for hp in range(ATTN_W // LANE):
                qs, ks, vs = cols(hp)
                if first:
                    k2 = jnp.concatenate([p_ref[:, ks], x_ref[0:ATTN_BLK, ks]], axis=0)
                    v2 = jnp.concatenate([p_ref[:, vs], x_ref[0:ATTN_BLK, vs]], axis=0)
                else:
                    k2 = x_ref[krows, ks]
                    v2 = x_ref[krows, vs]
                dq, dk, dv = pair_grads(x_ref[rows, qs], k2, v2, o_ref[rows, qs], do_ref[rows, qs],
                                        l_ref[rows, qs], valid)
                dq_ref[rows, qs] = dq
                if first:
                    dk_ref[0:ATTN_BLK, qs] += dk[ATTN_BLK:, :]
                    dv_ref[0:ATTN_BLK, qs] += dv[ATTN_BLK:, :]
                else:
                    dk_ref[krows, qs] += dk
                    dv_ref[krows, qs] += dv

        block(0, True)
        if n > 1:
            def loop(i, carry):
                block(i, False)
                return carry
            lax.fori_loop(1, n, loop, 0)

        last = slice((n - 1) * ATTN_BLK, n * ATTN_BLK)
        qi = lax.broadcasted_iota(jnp.int32, (ATTN_BLK, ATTN_BLK), 0)
        ki = lax.broadcasted_iota(jnp.int32, (ATTN_BLK, ATTN_BLK), 1)
        valid_next = (qi <= ki) & ((b + 1) * n < nb)
        for hp in range(ATTN_W // LANE):
            qs, ks, vs = cols(hp)
            _, dk, dv = pair_grads(nx_ref[:, qs], x_ref[last, ks], x_ref[last, vs], on_ref[:, qs], don_ref[:, qs],
                                   ln_ref[:, qs], valid_next)
            dk_ref[last, qs] += dk
            dv_ref[last, qs] += dv

    nxt = lambda b: jnp.minimum((b + 1) * n, nb - 1)
    xs = pl.BlockSpec((tq, w3), lambda c, b: (b, c))
    xp = pl.BlockSpec((ATTN_BLK, w3), lambda c, b: (jnp.maximum(b * n - 1, 0), c))
    xn = pl.BlockSpec((ATTN_BLK, w3), lambda c, b: (nxt(b), c))
    a = pl.BlockSpec((tq, ATTN_W), lambda c, b: (b, c))
    an = pl.BlockSpec((ATTN_BLK, ATTN_W), lambda c, b: (nxt(b), c))
    shp = jax.ShapeDtypeStruct((m, d * ATTN_W), F32)
    dep_specs, dep_ops = _dep_args(dep)
    return pl.pallas_call(
        body, out_shape=(shp, shp, shp), grid=(d, m // tq),
        in_specs=[xs, xp, xn, a, a, a, an, an, an] + dep_specs, out_specs=(a, a, a), name=name,
        compiler_params=_cparams(("parallel", "parallel")))(qv, qv, qv, ov, dov, lv, ov, dov, lv, *dep_ops)


def _rms(x, g):
    ms = jnp.mean(x * x, axis=-1, keepdims=True)
    return x * lax.rsqrt(ms + RMS_EPS) * g


def _rms_bwd(x, g, dy):
    ms = jnp.mean(x * x, axis=-1, keepdims=True)
    r = lax.rsqrt(ms + RMS_EPS)
    dyg = dy * g
    dx = r * dyg - x * (r * r * r) * jnp.mean(x * dyg, axis=-1, keepdims=True)
    return dx, dy * x * r


def _mix_fwd(outs, lses, lru, s5, g, h_in, w_out, ln_g, ln_b, name):
    s = lru.shape[0]
    tm = 256

    def body(o1, o2, o3, l1, l2, l3, lru_ref, s5_ref, g_ref, x_ref, w_ref, lg_ref, lb_ref,
             mixed_t_ref, r_ref, h_ref, ht_ref, ov1, ov2, ov3, lv1, lv2, lv3, so2, so3, sl2, sl3):
        for d, src, dst in ((DILATIONS[1], o2, so2), (DILATIONS[2], o3, so3),
                            (DILATIONS[1], l2, sl2), (DILATIONS[2], l3, sl3)):
            _to_tokens(src, dst, d, tm)
        a1, a2, a3 = l1[...], _token_value(sl2), _token_value(sl3)
        mx = jnp.maximum(jnp.maximum(a1, a2), a3)
        e1, e2, e3 = jnp.exp(a1 - mx), jnp.exp(a2 - mx), jnp.exp(a3 - mx)
        den = e1 + e2 + e3
        o = (e1 * o1[...] + e2 * _token_value(so2) + e3 * _token_value(so3)) / den
        lse = mx + jnp.log(den)
        ov1[...] = o
        lv1[...] = lse
        for j in range(ATTN_W // LANE):
            so2[j] = o[:, j * LANE:(j + 1) * LANE]
            sl2[j] = lse[:, j * LANE:(j + 1) * LANE]
        for d, o_dst, l_dst in ((DILATIONS[1], ov2, lv2), (DILATIONS[2], ov3, lv3)):
            _to_dilated(so2, o_dst, d, tm)
            _to_dilated(sl2, l_dst, d, tm)
        gg = g_ref[...]
        mixed = jnp.concatenate([_rms(o, gg[:, :ATTN_W]),
                                 _rms(lru_ref[...], gg[:, ATTN_W:ATTN_W + LRU_W]),
                                 _rms(s5_ref[...], gg[:, ATTN_W + LRU_W:])], axis=1)
        mixed_t_ref[...] = mixed.T.astype(BF16)
        r = ALPHA * x_ref[...] + _dot(mixed, w_ref[...])
        h = _layer_norm(r, lg_ref[...], lb_ref[...])
        r_ref[...] = r
        h_ref[...] = h
        ht_ref[...] = h.T.astype(BF16)

    a = pl.BlockSpec((tm, ATTN_W), lambda i: (i, 0))
    s5s = pl.BlockSpec((tm, S5_W), lambda i: (i, 0))
    full = pl.BlockSpec((tm, D_MODEL), lambda i: (i, 0))
    vec = pl.BlockSpec((1, D_MODEL), lambda i: (0, 0))
    dil = [_dilated_spec(tm, d, ATTN_W) for d in DILATIONS]
    dshape = [jax.ShapeDtypeStruct((s // d, d * ATTN_W), F32) for d in DILATIONS]
    tshape = jax.ShapeDtypeStruct((D_MODEL, s), BF16)
    fshape = jax.ShapeDtypeStruct((s, D_MODEL), F32)
    tspec = pl.BlockSpec((D_MODEL, tm), lambda i: (0, i))
    res = pl.pallas_call(
        body, out_shape=(tshape, fshape, fshape, tshape, *dshape, *dshape),
        grid=(s // tm,),
        in_specs=dil + dil + [a, s5s, vec, full, pl.BlockSpec((D_MODEL, D_MODEL), lambda i: (0, 0)), vec, vec],
        out_specs=(tspec, full, full, tspec, *dil, *dil),
        scratch_shapes=[_token_scratch(tm, ATTN_W)] * 4, name=name,
        compiler_params=_cparams(("parallel",)))(*outs, *lses, lru, s5, g, h_in, w_out, ln_g, ln_b)
    return res[0], res[1], res[2], res[3], res[4:7], res[7:10]


def _mix_bwd(r, dh, ln_g, w_out, o, lru, s5, g, name, dep=None):
    s = lru.shape[0]
    tm = 256

    def body(r_ref, dh_ref, lg_ref, w_ref, o_ref, lru_ref, s5_ref, g_ref, *rest):
        dr_ref, dlg_ref, dlb_ref, do_ref, do2_ref, do3_ref, dlru_ref, ds5_ref, dg_ref, stage = rest[-10:]

        @pl.when(pl.program_id(0) == 0)
        def _():
            dg_ref[...] = jnp.zeros_like(dg_ref)
            dlg_ref[...] = jnp.zeros_like(dlg_ref)
            dlb_ref[...] = jnp.zeros_like(dlb_ref)
        gg = g_ref[...]
        dr, dlg_rows, dlb_rows = _layer_norm_bwd(r_ref[...], dh_ref[...], lg_ref[...])
        dr_ref[...] = dr
        dlg_ref[...] += dlg_rows
        dlb_ref[...] += dlb_rows
        dm = _dot_nt(dr, w_ref[...])
        dx, dgr = _rms_bwd(o_ref[...], gg[:, :ATTN_W], dm[:, :ATTN_W])
        do_ref[...] = dx
        for j in range(ATTN_W // LANE):
            stage[j] = dx[:, j * LANE:(j + 1) * LANE]
        _to_dilated(stage, do2_ref, DILATIONS[1], tm)
        _to_dilated(stage, do3_ref, DILATIONS[2], tm)
        dg_ref[:, :ATTN_W] += jnp.sum(dgr, axis=0, keepdims=True)
        dx, dgr = _rms_bwd(lru_ref[...], gg[:, ATTN_W:ATTN_W + LRU_W], dm[:, ATTN_W:ATTN_W + LRU_W])
        dlru_ref[...] = dx
        dg_ref[:, ATTN_W:ATTN_W + LRU_W] += jnp.sum(dgr, axis=0, keepdims=True)
        dx, dgr = _rms_bwd(s5_ref[...], gg[:, ATTN_W + LRU_W:], dm[:, ATTN_W + LRU_W:])
        ds5_ref[...] = dx
        dg_ref[:, ATTN_W + LRU_W:] += jnp.sum(dgr, axis=0, keepdims=True)

    a = pl.BlockSpec((tm, ATTN_W), lambda i: (i, 0))
    s5s = pl.BlockSpec((tm, S5_W), lambda i: (i, 0))
    full = pl.BlockSpec((tm, D_MODEL), lambda i: (i, 0))
    vec = pl.BlockSpec((1, D_MODEL), lambda i: (0, 0))
    dil = [_dilated_spec(tm, d, ATTN_W) for d in DILATIONS]
    dshape = [jax.ShapeDtypeStruct((s // d, d * ATTN_W), F32) for d in DILATIONS]
    dep_specs, dep_ops = _dep_args(dep)
    vshape = jax.ShapeDtypeStruct((1, D_MODEL), F32)
    res = pl.pallas_call(
        body, out_shape=(jax.ShapeDtypeStruct((s, D_MODEL), F32), vshape, vshape, *dshape,
                         jax.ShapeDtypeStruct((s, LRU_W), F32), jax.ShapeDtypeStruct((s, S5_W), F32), vshape),
        grid=(s // tm,),
        in_specs=[full, full, vec, pl.BlockSpec((D_MODEL, D_MODEL), lambda i: (0, 0)), a, a, s5s, vec] + dep_specs,
        out_specs=(full, vec, vec, *dil, a, s5s, vec), scratch_shapes=[_token_scratch(tm, ATTN_W)], name=name,
        compiler_params=_cparams(("arbitrary",)))(r, dh, ln_g, w_out, o, lru, s5, g, *dep_ops)
    return res[0], res[1], res[2], res[3:6], res[6], res[7], res[8]


def _lru_gate_math(xc, pre_r, pre_i, lam):
    r = _sigmoid(pre_r)
    i = _sigmoid(pre_i)
    log_a = -LRU_C * r * _softplus(-lam)
    a = jnp.exp(log_a)
    u = jnp.sqrt(-_expm1(2.0 * log_a)) * (i * xc)
    return a, u


def _lru_conv(x, prev8, cw, cb):
    y = cb + cw[LRU_CONV - 1:LRU_CONV, :] * x
    for k in range(LRU_CONV - 1):
        y = y + cw[k:k + 1, :] * _shift_down_prev(x, LRU_CONV - 1 - k, prev8)
    return y


def _lru_specs(s):
    xo = 3 * ATTN_W // LANE
    go = xo + LRU_W // LANE
    xr = pl.BlockSpec((s, LANE), lambda j: (0, xo + j))
    gt = pl.BlockSpec((s, LANE), lambda j: (0, go + j))
    cw = pl.BlockSpec((LRU_CONV, LANE), lambda j: (0, j))
    vec = pl.BlockSpec((1, LANE), lambda j: (0, j))
    wbd = pl.BlockSpec((LANE, LANE), lambda j: (j, j))
    col = pl.BlockSpec((s, LANE), lambda j: (0, j))
    return xr, gt, cw, vec, wbd, col


def _lru_fwd(proj, cw, cb, wr, br, wi, bi, lam, name):
    s = proj.shape[0]
    t = SCAN_T

    def body(xr_ref, gt_ref, cw_ref, cb_ref, wr_ref, br_ref, wi_ref, bi_ref, lam_ref, o_ref, xc_ref, a_ref, h_ref):
        cwv, cbv, lamv = cw_ref[...], cb_ref[...], lam_ref[...]
        wrv, wiv, brv, biv = wr_ref[...], wi_ref[...], br_ref[...], bi_ref[...]

        def chunk(c, carry):
            h_c, prev8 = carry
            rows = pl.ds(pl.multiple_of(c * t, t), t)
            x = xr_ref[rows, :]
            xc = _lru_conv(x, prev8, cwv, cbv)
            a, u = _lru_gate_math(xc, _dot(xc, wrv) + brv, _dot(xc, wiv) + biv, lamv)
            h = _scan_chunk(a, u, h_c)
            xc_ref[rows, :] = xc
            a_ref[rows, :] = a
            h_ref[rows, :] = h
            o_ref[rows, :] = h * _gelu(gt_ref[rows, :])
            return h[t - 1:t, :], x[t - 8:t, :]

        lax.fori_loop(0, s // t, chunk, (jnp.zeros((1, LANE), F32), jnp.zeros((8, LANE), F32)))

    xr, gt, cws, vec, wbd, col = _lru_specs(s)
    shp = jax.ShapeDtypeStruct((s, LRU_W), F32)
    return pl.pallas_call(
        body, out_shape=(shp,) * 4, grid=(LRU_W // LANE,),
        in_specs=[xr, gt, cws, vec, wbd, vec, wbd, vec, vec], out_specs=(col,) * 4, name=name,
        compiler_params=_cparams(("parallel",)))(proj, proj, cw, cb, wr, br, wi, bi, lam)


def _lru_bwd(proj, dout, xc_all, a_all, h_all, cw, cb, wr, br, wi, bi, lam, name):
    s = proj.shape[0]
    t = SCAN_T
    nc = s // t

    def body(xr_ref, gt_ref, do_ref, xc_s, a_s, h_s, cw_ref, cb_ref, wr_ref, br_ref, wi_ref, bi_ref, lam_ref,
             dxr_ref, dgt_ref, dcw_ref, dcb_ref, dwr_ref, dbr_ref, dwi_ref, dbi_ref, dlam_ref):
        cwv, cbv, lamv = cw_ref[...], cb_ref[...], lam_ref[...]
        wrv, wiv, brv, biv = wr_ref[...], wi_ref[...], br_ref[...], bi_ref[...]
        z1 = jnp.zeros((1, LANE), F32)
        zw = jnp.zeros((LANE, LANE), F32)

        def bchunk(ci, carry):
            g_next, a_next, dxc_next8, dcw, dcb, dwr, dbr, dwi, dbi, dlam = carry
            c = nc - 1 - ci
            t0 = pl.multiple_of(c * t, t)
            rows = pl.ds(t0, t)
            before = pl.ds(pl.multiple_of(jnp.maximum(t0 - 8, 0), 8), 8)
            has_prev = (c > 0).astype(F32)
            x, gt, do = xr_ref[rows, :], gt_ref[rows, :], do_ref[rows, :]
            xc, a, h = xc_s[rows, :], a_s[rows, :], h_s[rows, :]
            prev8_h = h_s[before, :] * has_prev
            dgt_ref[rows, :] = do * h * _gelu_grad(gt)
            dh = do * _gelu(gt)
            a_plus = _shift_up_next(a, 1, jnp.broadcast_to(a_next, (8, LANE)))
            g = _scan_chunk(a_plus, dh, g_next, reverse=True)
            da = g * _shift_down_prev(h, 1, prev8_h)
            pre_r = _dot(xc, wrv) + brv
            pre_i = _dot(xc, wiv) + biv
            _, vjp = jax.vjp(_lru_gate_math, xc, pre_r, pre_i, lamv)
            dxc, dpre_r, dpre_i, dlam_c = vjp((da, g))
            dxc = dxc + _dot_nt(dpre_r, wrv) + _dot_nt(dpre_i, wiv)
            dx = cwv[LRU_CONV - 1:LRU_CONV, :] * dxc
            dcw_rows = [None] * LRU_CONV
            dcw_rows[LRU_CONV - 1] = jnp.sum(dxc * x, axis=0, keepdims=True)
            for k in range(LRU_CONV - 1):
                dxc_ahead = _shift_up_next(dxc, LRU_CONV - 1 - k, dxc_next8)
                dx = dx + cwv[k:k + 1, :] * dxc_ahead
                dcw_rows[k] = jnp.sum(dxc_ahead * x, axis=0, keepdims=True)
            dxr_ref[rows, :] = dx
            return (g[0:1, :], a[0:1, :], dxc[0:8, :],
                    dcw + jnp.concatenate(dcw_rows, axis=0),
                    dcb + jnp.sum(dxc, axis=0, keepdims=True),
                    dwr + _dot_tn(xc, dpre_r), dbr + jnp.sum(dpre_r, axis=0, keepdims=True),
                    dwi + _dot_tn(xc, dpre_i), dbi + jnp.sum(dpre_i, axis=0, keepdims=True),
                    dlam + dlam_c)

        init = (z1, z1, jnp.zeros((8, LANE), F32), jnp.zeros((LRU_CONV, LANE), F32), z1, zw, z1, zw, z1, z1)
        res = lax.fori_loop(0, nc, bchunk, init)
        dcw_ref[...] = res[3]
        dcb_ref[...] = res[4]
        dwr_ref[...] = res[5]
        dbr_ref[...] = res[6]
        dwi_ref[...] = res[7]
        dbi_ref[...] = res[8]
        dlam_ref[...] = res[9]

    xr, gt, cws, vec, wbd, col = _lru_specs(s)
    vshape = jax.ShapeDtypeStruct((1, LRU_W), F32)
    wshape = jax.ShapeDtypeStruct((LRU_W, LRU_W), F32)
    return pl.pallas_call(
        body,
        out_shape=(jax.ShapeDtypeStruct((s, LRU_W), F32), jax.ShapeDtypeStruct((s, LRU_W), F32),
                   jax.ShapeDtypeStruct((LRU_CONV, LRU_W), F32), vshape, wshape, vshape, wshape, vshape, vshape),
        grid=(LRU_W // LANE,),
        in_specs=[xr, gt, col, col, col, col, cws, vec, wbd, vec, wbd, vec, vec],
        out_specs=(col, col, cws, vec, wbd, vec, wbd, vec, vec), name=name,
        compiler_params=_cparams(("parallel",)))(proj, proj, dout, xc_all, a_all, h_all, cw, cb, wr, br, wi, bi,
                                                 lam)


def _s5_disc_math(a_re, a_im, log_step, bt_re, bt_im):
    step = jnp.exp(log_step)
    dt_re, dt_im = step * a_re, step * a_im
    mag = jnp.exp(dt_re)
    ab_re, ab_im = mag * jnp.cos(dt_im), mag * jnp.sin(dt_im)
    z_re, z_im = ab_re - 1.0, ab_im
    den = a_re * a_re + a_im * a_im
    f_re = (z_re * a_re + z_im * a_im) / den
    f_im = (z_im * a_re - z_re * a_im) / den
    bb_re = f_re * bt_re - f_im * bt_im
    bb_im = f_re * bt_im + f_im * bt_re
    return ab_re, ab_im, bb_re, bb_im


def _s5_disc_fwd(a_re, a_im, log_step, bt_re, bt_im, name):
    def body(ar, ai, ls, br, bi, o1, o2, o3, o4):
        r = _s5_disc_math(ar[...], ai[...], ls[...], br[...], bi[...])
        o1[...], o2[...], o3[...], o4[...] = r

    shp = jax.ShapeDtypeStruct(a_re.shape, F32)
    return pl.pallas_call(body, out_shape=(shp,) * 4, name=name)(a_re, a_im, log_step, bt_re, bt_im)


def _s5_disc_bwd(a_re, a_im, log_step, bt_re, bt_im, cts, name):
    def body(ar, ai, ls, br, bi, c1, c2, c3, c4, o1, o2, o3, o4, o5):
        _, vjp = jax.vjp(_s5_disc_math, ar[...], ai[...], ls[...], br[...], bi[...])
        r = vjp((c1[...], c2[...], c3[...], c4[...]))
        o1[...], o2[...], o3[...], o4[...], o5[...] = r

    shp = jax.ShapeDtypeStruct(a_re.shape, F32)
    return pl.pallas_call(body, out_shape=(shp,) * 5, name=name)(a_re, a_im, log_step, bt_re, bt_im, *cts)


def _s5_u_specs(s):
    uo = (3 * ATTN_W + 2 * LRU_W) // LANE
    return (pl.BlockSpec((s, LANE), lambda j: (0, uo)), pl.BlockSpec((s, LANE), lambda j: (0, uo + 1)))


def _s5_scan_fwd(proj, b_re, b_im, lam_re, lam_im, c_re, c_im, name):
    s = proj.shape[0]
    t = SCAN_T

    def body(u0_ref, u1_ref, bre_ref, bim_ref, lre_ref, lim_ref, cre_ref, cim_ref, xre_ref, xim_ref, y_ref):
        @pl.when(pl.program_id(0) == 0)
        def _():
            y_ref[...] = jnp.zeros_like(y_ref)
        lr, li = lre_ref[...], lim_ref[...]
        consts = _cscan_consts(lr, li, False)
        bre, bim, cre, cim = bre_ref[...], bim_ref[...], cre_ref[...], cim_ref[...]

        def chunk(c, carry):
            cr, ci = carry
            rows = pl.ds(pl.multiple_of(c * t, t), t)
            u = jnp.concatenate([u0_ref[rows, :], u1_ref[rows, :]], axis=1).astype(BF16)
            xr, xi = _cscan_chunk(_dot(u, bre), _dot(u, bim), consts, (cr, ci))
            xre_ref[rows, :] = xr
            xim_ref[rows, :] = xi
            y_ref[rows, :] += _dot(xr, cre) - _dot(xi, cim)
            return xr[t - 1:t, :], xi[t - 1:t, :]

        z = jnp.zeros((1, S5_BLK), F32)
        lax.fori_loop(0, s // t, chunk, (z, z))

    u0, u1 = _s5_u_specs(s)
    bsp = pl.BlockSpec((S5_W, S5_BLK), lambda j: (0, j))
    csp = pl.BlockSpec((S5_BLK, S5_W), lambda j: (j, 0))
    vec = pl.BlockSpec((1, S5_BLK), lambda j: (0, j))
    xsp = pl.BlockSpec((s, S5_BLK), lambda j: (0, j))
    ysp = pl.BlockSpec((s, S5_W), lambda j: (0, 0))
    xshape = jax.ShapeDtypeStruct((s, S5_STATES), F32)
    return pl.pallas_call(
        body, out_shape=(xshape, xshape, jax.ShapeDtypeStruct((s, S5_W), F32)),
        grid=(S5_STATES // S5_BLK,), in_specs=[u0, u1, bsp, bsp, vec, vec, csp, csp],
        out_specs=(xsp, xsp, ysp), name=name,
        compiler_params=_cparams(("arbitrary",)))(proj, proj, b_re, b_im, lam_re, lam_im, c_re, c_im)


def _s5_scan_bwd(proj, dy, du_init, x_re, x_im, b_re, b_im, lam_re, lam_im, c_re, c_im, name):
    s = proj.shape[0]
    t = SCAN_T
    nc = s // t

    def body(u0_ref, u1_ref, dy_ref, dui_ref, xre_ref, xim_ref, bre_ref, bim_ref, lre_ref, lim_ref,
             cre_ref, cim_ref, du_ref, dlr_ref, dli_ref, dbr_ref, dbi_ref, dcr_ref, dci_ref):
        @pl.when(pl.program_id(0) == 0)
        def _():
            du_ref[...] = dui_ref[...]
        mr, mi = lre_ref[...], -lim_ref[...]
        consts = _cscan_consts(mr, mi, True)
        bre, bim, cre, cim = bre_ref[...], bim_ref[...], cre_ref[...], cim_ref[...]
        dbr_ref[...] = jnp.zeros_like(dbr_ref)
        dbi_ref[...] = jnp.zeros_like(dbi_ref)
        dcr_ref[...] = jnp.zeros_like(dcr_ref)
        dci_ref[...] = jnp.zeros_like(dci_ref)

        def chunk(ci_, carry):
            gnr, gni, dlr, dli = carry
            c = nc - 1 - ci_
            t0 = pl.multiple_of(c * t, t)
            rows = pl.ds(t0, t)
            before = pl.ds(pl.multiple_of(jnp.maximum(t0 - 8, 0), 8), 8)
            has_prev = (c > 0).astype(F32)
            dyc = dy_ref[rows, :].astype(BF16)
            u = jnp.concatenate([u0_ref[rows, :], u1_ref[rows, :]], axis=1).astype(BF16)
            gr, gi = _cscan_chunk(_dot_nt(dyc, cre), -_dot_nt(dyc, cim), consts, (gnr, gni), reverse=True)
            xr, xi = xre_ref[rows, :], xim_ref[rows, :]
            xpr = _shift_down_prev(xr, 1, xre_ref[before, :] * has_prev)
            xpi = _shift_down_prev(xi, 1, xim_ref[before, :] * has_prev)
            dlr = dlr + jnp.sum(gr * xpr + gi * xpi, axis=0, keepdims=True)
            dli = dli + jnp.sum(gi * xpr - gr * xpi, axis=0, keepdims=True)
            du_ref[rows, :] += _dot_nt(gr, bre) + _dot_nt(gi, bim)
            dbr_ref[...] += _dot_tn(u, gr)
            dbi_ref[...] += _dot_tn(u, gi)
            dcr_ref[...] += _dot_tn(xr, dyc)
            dci_ref[...] -= _dot_tn(xi, dyc)
            return gr[0:1, :], gi[0:1, :], dlr, dli

        z = jnp.zeros((1, S5_BLK), F32)
        res = lax.fori_loop(0, nc, chunk, (z, z, z, z))
        dlr_ref[...] = res[2]
        dli_ref[...] = res[3]

    u0, u1 = _s5_u_specs(s)
    bsp = pl.BlockSpec((S5_W, S5_BLK), lambda j: (0, j))
    csp = pl.BlockSpec((S5_BLK, S5_W), lambda j: (j, 0))
    vec = pl.BlockSpec((1, S5_BLK), lambda j: (0, j))
    xsp = pl.BlockSpec((s, S5_BLK), lambda j: (0, j))
    ysp = pl.BlockSpec((s, S5_W), lambda j: (0, 0))
    return pl.pallas_call(
        body,
        out_shape=(jax.ShapeDtypeStruct((s, S5_W), F32),
                   jax.ShapeDtypeStruct((1, S5_STATES), F32), jax.ShapeDtypeStruct((1, S5_STATES), F32),
                   jax.ShapeDtypeStruct((S5_W, S5_STATES), F32), jax.ShapeDtypeStruct((S5_W, S5_STATES), F32),
                   jax.ShapeDtypeStruct((S5_STATES, S5_W), F32), jax.ShapeDtypeStruct((S5_STATES, S5_W), F32)),
        grid=(S5_STATES // S5_BLK,),
        in_specs=[u0, u1, ysp, ysp, xsp, xsp, bsp, bsp, vec, vec, csp, csp],
        out_specs=(ysp, vec, vec, bsp, bsp, csp, csp), name=name,
        compiler_params=_cparams(("arbitrary",)))(
            proj, proj, dy, du_init, x_re, x_im, b_re, b_im, lam_re, lam_im, c_re, c_im)


def _s5_out_fwd(proj, y_acc, dvec, w_glu, b_glu, name):
    s = proj.shape[0]
    tm = 512
    uo = (3 * ATTN_W + 2 * LRU_W) // LANE

    def body(u0_ref, u1_ref, y_ref, d_ref, w_ref, b_ref, o_ref, yp_ref):
        u = jnp.concatenate([u0_ref[...], u1_ref[...]], axis=1)
        y = y_ref[...] + d_ref[...] * u
        yp_ref[...] = y
        yg = _gelu(y)
        o_ref[...] = yg * _sigmoid(_dot(yg, w_ref[...]) + b_ref[...])

    u0 = pl.BlockSpec((tm, LANE), lambda i: (i, uo))
    u1 = pl.BlockSpec((tm, LANE), lambda i: (i, uo + 1))
    row = pl.BlockSpec((tm, S5_W), lambda i: (i, 0))
    vec = pl.BlockSpec((1, S5_W), lambda i: (0, 0))
    wsp = pl.BlockSpec((S5_W, S5_W), lambda i: (0, 0))
    shp = jax.ShapeDtypeStruct((s, S5_W), F32)
    return pl.pallas_call(
        body, out_shape=(shp, shp), grid=(s // tm,), in_specs=[u0, u1, row, vec, wsp, vec],
        out_specs=(row, row), name=name,
        compiler_params=_cparams(("parallel",)))(proj, proj, y_acc, dvec, w_glu, b_glu)


def _s5_out_bwd(proj, y_pre, dout, dvec, w_glu, b_glu, name, dep=None):
    s = proj.shape[0]
    tm = 512
    uo = (3 * ATTN_W + 2 * LRU_W) // LANE

    def body(u0_ref, u1_ref, y_ref, do_ref, d_ref, w_ref, b_ref, *rest):
        dy_ref, dud_ref, dd_ref, dw_ref, db_ref = rest[-5:]

        @pl.when(pl.program_id(0) == 0)
        def _():
            dd_ref[...] = jnp.zeros_like(dd_ref)
            dw_ref[...] = jnp.zeros_like(dw_ref)
            db_ref[...] = jnp.zeros_like(db_ref)
        u = jnp.concatenate([u0_ref[...], u1_ref[...]], axis=1)
        y = y_ref[...]
        do = do_ref[...]
        yg = _gelu(y)
        sg = _sigmoid(_dot(yg, w_ref[...]) + b_ref[...])
        dz = do * yg * sg * (1.0 - sg)
        dyg = do * sg + _dot_nt(dz, w_ref[...])
        dy = dyg * _gelu_grad(y)
        dy_ref[...] = dy
        dud_ref[...] = d_ref[...] * dy
        dd_ref[...] += jnp.sum(dy * u, axis=0, keepdims=True)
        dw_ref[...] += _dot_tn(yg, dz)
        db_ref[...] += jnp.sum(dz, axis=0, keepdims=True)

    u0 = pl.BlockSpec((tm, LANE), lambda i: (i, uo))
    u1 = pl.BlockSpec((tm, LANE), lambda i: (i, uo + 1))
    row = pl.BlockSpec((tm, S5_W), lambda i: (i, 0))
    vec = pl.BlockSpec((1, S5_W), lambda i: (0, 0))
    wsp = pl.BlockSpec((S5_W, S5_W), lambda i: (0, 0))
    shp = jax.ShapeDtypeStruct((s, S5_W), F32)
    vshape = jax.ShapeDtypeStruct((1, S5_W), F32)
    dep_specs, dep_ops = _dep_args(dep)
    return pl.pallas_call(
        body, out_shape=(shp, shp, vshape, jax.ShapeDtypeStruct((S5_W, S5_W), F32), vshape),
        grid=(s // tm,), in_specs=[u0, u1, row, row, vec, wsp, vec] + dep_specs,
        out_specs=(row, row, vec, wsp, vec), name=name,
        compiler_params=_cparams(("arbitrary",)))(proj, proj, y_pre, dout, dvec, w_glu, b_glu, *dep_ops)


def _ffn_conv(x, prev8, cw, cb):
    y = cb + cw[FFN_CONV - 1:FFN_CONV, :] * x
    for k in range(FFN_CONV - 1):
        y = y + cw[k:k + 1, :] * _shift_down_prev(x, FFN_CONV - 1 - k, prev8)
    return y


def _ffn_up_act(h, wg, cw, cb, name, dep=None):
    s, d = h.shape
    tm = 512
    tb = 2 * FFN_CB
    nt = D_FF // FFN_CB

    def body(h_ref, wgate_ref, wval_ref, cw_ref, cb_ref, *rest):
        up_ref, y_ref, o_ref, ot_ref, carry = rest[-5:]

        @pl.when(pl.program_id(1) == 0)
        def _():
            carry[...] = jnp.zeros_like(carry)
        hb = h_ref[...].astype(BF16)
        x = jnp.concatenate([_dot(hb, wgate_ref[...]), _dot(hb, wval_ref[...])], axis=1)
        up_ref[...] = x
        y = _ffn_conv(x, carry[...], cw_ref[...], cb_ref[...])
        y_ref[...] = y
        carry[...] = x[tm - 8:tm, :]
        act = _gelu(y[:, :FFN_CB]) * y[:, FFN_CB:]
        o_ref[...] = act.astype(BF16)
        ot_ref[...] = act.T.astype(BF16)

    dep_specs, dep_ops = _dep_args(dep)
    return pl.pallas_call(
        body, out_shape=(jax.ShapeDtypeStruct((s, 2 * D_FF), F32), jax.ShapeDtypeStruct((s, 2 * D_FF), F32),
                         jax.ShapeDtypeStruct((s, D_FF), BF16), jax.ShapeDtypeStruct((D_FF, s), BF16)),
        grid=(nt, s // tm),
        in_specs=[pl.BlockSpec((tm, d), lambda t, i: (i, 0)),
                  pl.BlockSpec((None, d, FFN_CB), lambda t, i: (t, 0, 0)),
                  pl.BlockSpec((None, d, FFN_CB), lambda t, i: (t + nt, 0, 0)),
                  pl.BlockSpec((FFN_CONV, tb), lambda t, i: (0, t)),
                  pl.BlockSpec((1, tb), lambda t, i: (0, t))] + dep_specs,
        out_specs=(pl.BlockSpec((tm, tb), lambda t, i: (i, t)), pl.BlockSpec((tm, tb), lambda t, i: (i, t)),
                   pl.BlockSpec((tm, FFN_CB), lambda t, i: (i, t)), pl.BlockSpec((FFN_CB, tm), lambda t, i: (t, i))),
        scratch_shapes=[pltpu.VMEM((8, tb), F32)], name=name,
        compiler_params=_cparams(("parallel", "arbitrary")))(h, wg, wg, cw, cb, *dep_ops)


def _ffn_bwd(up, y_conv, dr, w_down, wg, cw, name):
    s = up.shape[0]
    d = dr.shape[1]
    tm = 512
    tb = 2 * FFN_CB
    nr = s // tm
    nt = D_FF // FFN_CB

    def body(x_ref, y_ref, dr_ref, wd_ref, wgate_ref, wval_ref, cw_ref,
             dup_ref, dh_ref, dcw_ref, dcb_ref, carry):
        i, t = pl.program_id(0), pl.program_id(1)

        @pl.when(i == 0)
        def _():
            carry[t] = jnp.zeros((8, tb), F32)

        @pl.when(t == 0)
        def _():
            dh_ref[...] = ALPHA * dr_ref[...]
        cwv = cw_ref[...]
        x = x_ref[...]
        dact = _dot_nt(dr_ref[...], wd_ref[...])
        gate, val = y_ref[:, :FFN_CB], y_ref[:, FFN_CB:]
        dy = jnp.concatenate([dact * val * _gelu_grad(gate), dact * _gelu(gate)], axis=1)
        next8 = carry[t]
        carry[t] = dy[0:8, :]
        dx = cwv[FFN_CONV - 1:FFN_CONV, :] * dy
        dcw_rows = [None] * FFN_CONV
        dcw_rows[FFN_CONV - 1] = jnp.sum(dy * x, axis=0, keepdims=True)
        for k in range(FFN_CONV - 1):
            dy_ahead = _shift_up_next(dy, FFN_CONV - 1 - k, next8)
            dx = dx + cwv[k:k + 1, :] * dy_ahead
            dcw_rows[k] = jnp.sum(dy_ahead * x, axis=0, keepdims=True)
        dup = dx.astype(BF16)
        dup_ref[...] = dup
        dh_ref[...] += _dot_nt(dup[:, :FFN_CB], wgate_ref[...]) + _dot_nt(dup[:, FFN_CB:], wval_ref[...])
        dcw_ref[...] = jnp.concatenate(dcw_rows, axis=0)
        dcb_ref[...] = jnp.sum(dy, axis=0, keepdims=True)

    row = lambda i: nr - 1 - i
    return pl.pallas_call(
        body, out_shape=(jax.ShapeDtypeStruct((s, 2 * D_FF), BF16), jax.ShapeDtypeStruct((s, d), F32),
                         jax.ShapeDtypeStruct((nr, FFN_CONV, 2 * D_FF), F32),
                         jax.ShapeDtypeStruct((nr, 1, 2 * D_FF), F32)),
        grid=(nr, nt),
        in_specs=[pl.BlockSpec((tm, tb), lambda i, t: (row(i), t)),
                  pl.BlockSpec((tm, tb), lambda i, t: (row(i), t)),
                  pl.BlockSpec((tm, d), lambda i, t: (row(i), 0)),
                  pl.BlockSpec((FFN_CB, d), lambda i, t: (t, 0)),
                  pl.BlockSpec((None, d, FFN_CB), lambda i, t: (t, 0, 0)),
                  pl.BlockSpec((None, d, FFN_CB), lambda i, t: (t + nt, 0, 0)),
                  pl.BlockSpec((FFN_CONV, tb), lambda i, t: (0, t))],
        out_specs=(pl.BlockSpec((tm, tb), lambda i, t: (row(i), t)),
                   pl.BlockSpec((tm, d), lambda i, t: (row(i), 0)),
                   pl.BlockSpec((None, FFN_CONV, tb), lambda i, t: (row(i), 0, t)),
                   pl.BlockSpec((None, 1, tb), lambda i, t: (row(i), 0, t))),
        scratch_shapes=[pltpu.VMEM((nt, 8, tb), F32)], name=name,
        compiler_params=_cparams(("arbitrary", "arbitrary")))(up, y_conv, dr, w_down, wg, wg, cw)


def _sum_partials(ld_ref):
    gg = ld_ref[0].astype(F32)
    for k in range(1, N_DEV):
        gg = gg + ld_ref[k].astype(F32)
    return gg


def _adam_update(w, g, m, v):
    mn = ADAM_B1 * m + (1.0 - ADAM_B1) * g
    vn = ADAM_B2 * v + (1.0 - ADAM_B2) * (g * g)
    m_hat = mn / (1.0 - ADAM_B1 ** ADAM_STEP)
    v_hat = vn / (1.0 - ADAM_B2 ** ADAM_STEP)
    return -ADAM_LR * (m_hat / (jnp.sqrt(v_hat) + ADAM_EPS) + ADAM_WD * w), mn, vn


def _adamw_many(landed, ws, ms, vs, name):
    n, nl = len(ws), len(landed)

    def body(*refs):
        ld = refs[:nl * n]
        w_refs, m_refs, v_refs = (refs[(nl + k) * n:(nl + k + 1) * n] for k in range(3))
        outs = refs[(nl + 3) * n:]
        for i in range(n):
            for l in range(nl):
                one = slice(l, l + 1)
                gg = _sum_partials(ld[l * n + i])
                outs[i][one] = gg
                outs[n + i][one], outs[2 * n + i][one], outs[3 * n + i][one] = _adam_update(
                    w_refs[i][one], gg, m_refs[i][one], v_refs[i][one])

    vm = pl.BlockSpec(memory_space=pltpu.VMEM)
    shapes = [jax.ShapeDtypeStruct(w.shape, F32) for w in ws] * 4
    res = pl.pallas_call(
        body, out_shape=tuple(shapes), in_specs=[vm] * ((nl + 3) * n), out_specs=tuple([vm] * (4 * n)),
        name=name, compiler_params=_cparams())(*[a for layer in landed for a in layer], *ws, *ms, *vs)
    return res[:n], res[n:2 * n], res[2 * n:3 * n], res[3 * n:]


def _adamw_sum(landed, w, m, v, layer, prev, name):
    _, r, c = landed.shape
    nl = w.shape[0]
    tm = 8
    for cand in (512, 256, 128, 64, 32, 16):
        if r % cand == 0 and N_DEV * cand * c * 4 <= 4 * 1024 * 1024:
            tm = cand
            break

    def body(*refs):
        ld_ref, w_ref, m_ref, v_ref = refs[:4]
        g_ref, d_ref, mo_ref, vo_ref = refs[-4:]
        gg = _sum_partials(ld_ref)
        g_ref[...] = gg
        d_ref[...], mo_ref[...], vo_ref[...] = _adam_update(w_ref[...], gg, m_ref[...], v_ref[...])

    blk = pl.BlockSpec((None, tm, c), lambda i: (layer, i, 0))
    in_specs = [pl.BlockSpec((N_DEV, tm, c), lambda i: (0, i, 0)), blk, blk, blk]
    args = [landed, w, m, v]
    aliases = {}
    if prev is not None:
        in_specs += [pl.BlockSpec(memory_space=pl.ANY)] * 4
        args += list(prev)
        aliases = {4 + k: k for k in range(4)}
    shp = jax.ShapeDtypeStruct((nl, r, c), F32)
    return pl.pallas_call(
        body, out_shape=(shp,) * 4, grid=(r // tm,), in_specs=in_specs, out_specs=(blk,) * 4,
        input_output_aliases=aliases, name=name, compiler_params=_cparams(("parallel",)))(*args)


def _all_gather(shards, name):
    na = len(shards)

    def body(*refs):
        x_refs, out_refs = refs[:na], refs[na:2 * na]
        send_sems, recv_sems, local_sems = refs[2 * na:]
        x, y, c = lax.axis_index("x"), lax.axis_index("y"), lax.axis_index("c")
        me, sibling = (x, y, c), (x, y, 1 - c)
        chips = [(1 - x, y), (x, 1 - y), (1 - x, 1 - y)]

        def copy(a, k, block, to, src=None):
            dst = out_refs[a].at[4 * block[0] + 2 * block[1] + block[2]]
            return pltpu.make_async_remote_copy(
                src_ref=dst if src is None else src, dst_ref=dst,
                send_sem=send_sems.at[7 * a + k], recv_sem=recv_sems.at[7 * a + k],
                device_id=to, device_id_type=pl.DeviceIdType.MESH)

        mine, first, passed = [], [], []
        for a in range(na):
            cp = pltpu.make_async_copy(x_refs[a], out_refs[a].at[4 * x + 2 * y + c], local_sems.at[a])
            cp.start()
            mine.append(cp)
            cps = [copy(a, 0, me, sibling, src=x_refs[a])]
            cps += [copy(a, 1 + j, me, (*chip, c), src=x_refs[a]) for j, chip in enumerate(chips)]
            for cp in cps:
                cp.start()
            first += cps
        for j, chip in enumerate(chips):
            for a in range(na):
                copy(a, 1 + j, (*chip, c), me).wait_recv()
                cp = copy(a, 4 + j, (*chip, c), sibling)
                cp.start()
                passed.append(cp)
        for a in range(na):
            copy(a, 0, sibling, me).wait_recv()
            for j, chip in enumerate(chips):
                copy(a, 4 + j, (*chip, 1 - c), me).wait_recv()
        for cp in first + passed:
            cp.wait_send()
        for cp in mine:
            cp.wait()

    anyspec = pl.BlockSpec(memory_space=pl.ANY)
    return pl.pallas_call(
        body, out_shape=tuple(jax.ShapeDtypeStruct((N_DEV,) + t.shape, t.dtype) for t in shards),
        in_specs=[anyspec] * na, out_specs=tuple([anyspec] * na),
        scratch_shapes=[pltpu.SemaphoreType.DMA((7 * na,)), pltpu.SemaphoreType.DMA((7 * na,)),
                        pltpu.SemaphoreType.DMA((na,))],
        name=name)(*shards)


_HBM = pl.BlockSpec(memory_space=pltpu.HBM)
_SEM = pl.BlockSpec(memory_space=pltpu.SEMAPHORE)
_EFFECT = pltpu.SideEffectType.DATAFLOW_SIDE_EFFECTING


def _exchange_copies(src_refs, land_refs, send_sems, recv_sems, local_sems, gather):
    x, y, c = lax.axis_index("x"), lax.axis_index("y"), lax.axis_index("c")
    me = 4 * x + 2 * y + c
    per_array = send_sems.shape[0] > N_DEV - 1
    local, remote = [], []
    for a, (src, land) in enumerate(zip(src_refs, land_refs)):
        local.append(pltpu.make_async_copy(src if gather else src.at[me], land.at[me],
                                           local_sems.at[a if per_array else 0]))
    for k in range(1, N_DEV):
        px = x ^ ((k >> 2) & 1)
        py = y ^ ((k >> 1) & 1)
        pc = c ^ (k & 1)
        for a, (src, land) in enumerate(zip(src_refs, land_refs)):
            remote.append(pltpu.make_async_remote_copy(
                src_ref=src if gather else src.at[4 * px + 2 * py + pc], dst_ref=land.at[me],
                send_sem=send_sems.at[(7 * a if per_array else 0) + k - 1],
                recv_sem=recv_sems.at[(7 * a if per_array else 0) + k - 1],
                device_id=(px, py, pc), device_id_type=pl.DeviceIdType.MESH))
    return local, remote


def _exchange_start(srcs, gather, name, dep=None):
    na = len(srcs)
    ns = na if na <= 4 else 1
    lands = [lax.empty(((N_DEV,) + t.shape) if gather else t.shape, t.dtype) for t in srcs]

    def body(*refs):
        src_refs, land_refs = refs[:na], refs[na:2 * na]
        nin = 2 * na + (0 if dep is None else 1)
        send_sems, recv_sems, local_sems = refs[nin:nin + 3]
        token = refs[-1]
        local, remote = _exchange_copies(src_refs, land_refs, send_sems, recv_sems, local_sems, gather)
        for cp in local + remote:
            cp.start()
        token[...] = jnp.zeros_like(token)

    dep_specs, dep_ops = _dep_args(dep)
    hbm = lambda t: pltpu.HBM(t.shape, t.dtype)
    out = pl.pallas_call(
        body, name=name,
        out_shape=(pltpu.SemaphoreType.DMA((7 * ns,)), pltpu.SemaphoreType.DMA((7 * ns,)),
                   pltpu.SemaphoreType.DMA((ns,)), *[hbm(t) for t in srcs], *[hbm(t) for t in lands],
                   jax.ShapeDtypeStruct((8, LANE), F32)),
        in_specs=[_HBM] * (2 * na) + dep_specs,
        out_specs=(_SEM, _SEM, _SEM, *[_HBM] * (2 * na), pl.BlockSpec(memory_space=pltpu.VMEM)),
        input_output_aliases={i: 3 + i for i in range(2 * na)},
        compiler_params=pltpu.CompilerParams(has_side_effects=_EFFECT),
    )(*[pltpu.with_memory_space_constraint(t, pltpu.HBM) for t in srcs + lands], *dep_ops)
    return (out[:3], out[3:3 + na], out[3 + na:3 + 2 * na]), out[-1]


def _exchange_wait(handle, gather, after, name):
    sems, srcs, lands = handle
    na = len(srcs)

    def body(*refs):
        src_refs, land_refs = refs[:na], refs[na:2 * na]
        send_sems, recv_sems, local_sems = refs[2 * na:2 * na + 3]
        local, remote = _exchange_copies(src_refs, land_refs, send_sems, recv_sems, local_sems, gather)
        for cp in remote:
            cp.wait_send()
            cp.wait_recv()
        for cp in local:
            cp.wait()

    hbm = lambda t: pltpu.HBM(t.shape, t.dtype)
    out = pl.pallas_call(
        body, name=name, out_shape=(*[hbm(t) for t in srcs], *[hbm(t) for t in lands]),
        in_specs=[_HBM] * (2 * na) + [_SEM] * 3 + [pl.BlockSpec(memory_space=pl.ANY)],
        out_specs=tuple([_HBM] * (2 * na)), input_output_aliases={i: i for i in range(2 * na)},
        compiler_params=pltpu.CompilerParams(has_side_effects=_EFFECT),
    )(*srcs, *lands, *sems, after)
    return out[na:]


def _block_diag(w):
    h, a, b = w.shape
    eye = jnp.eye(h, dtype=w.dtype)
    return (w[:, :, None, :] * eye[:, None, :, None]).reshape(h * a, h * b)


def _block_diag_extract(m, h):
    a, b = m.shape[0] // h, m.shape[1] // h
    return jnp.stack([m[i * a:(i + 1) * a, i * b:(i + 1) * b] for i in range(h)], axis=0)


def _block_diag_take(m, h):
    a, b = m.shape[0] // h, m.shape[1] // h
    eye = jnp.eye(h, dtype=m.dtype)
    return (m.reshape(h, a, h, b) * eye[:, None, :, None]).sum(axis=2)


def _ffn_interleave(w):
    lead = w.shape[:-1]
    nb = D_FF // FFN_CB
    return jnp.swapaxes(w.reshape(*lead, 2, nb, FFN_CB), -3, -2).reshape(*lead, 2 * D_FF)


def _ffn_deinterleave(w):
    lead = w.shape[:-1]
    nb = D_FF // FFN_CB
    return jnp.swapaxes(w.reshape(*lead, nb, 2, FFN_CB), -3, -2).reshape(*lead, 2 * D_FF)


def _gather_full(gathered, axis):
    shape = list(gathered.shape[1:])
    shape[axis] *= N_DEV
    return jnp.moveaxis(gathered, 0, axis).reshape(shape)


def _scatter_blocks(full, axis):
    shape = list(full.shape)
    shape[axis:axis + 1] = [N_DEV, shape[axis] // N_DEV]
    return jnp.moveaxis(full.reshape(shape), axis, 0)


def _pad_to(flat, mult):
    pad = (-flat.shape[-1]) % mult
    if pad:
        flat = jnp.concatenate([flat, jnp.zeros(flat.shape[:-1] + (pad,), flat.dtype)], axis=-1)
    return flat


def _layer_fwd(h_in, h_in_t, w, cos, sin, l, dep, get_ffn, target=None):
    tag = "l%d_" % l
    proj, qkv, h_t = _proj_rope(h_in, w['w_in'], cos, sin, tag + "proj_rope", dep=dep,
                                transposed=h_in_t is None)
    h_in_t = h_t if h_in_t is None else h_in_t
    outs, lses = [], []
    for d, qv in zip(DILATIONS, qkv):
        o, ls = _attn_fwd(qv, d, tag + "attn_d%d" % d)
        outs.append(o)
        lses.append(ls)
    lru, *lru_saved = _lru_fwd(proj, w['lru_conv_w'], w['lru_conv_b'], w['lru_wr'], w['lru_br'], w['lru_wi'],
                               w['lru_bi'], w['lru_lambda'], tag + "lru")
    x_re, x_im, y_acc = _s5_scan_fwd(proj, w['s5_bb_re'], w['s5_bb_im'], w['s5_lam_re'], w['s5_lam_im'],
                                     w['s5_cc_re'], w['s5_cc_im'], tag + "s5_scan")
    s5, y_pre = _s5_out_fwd(proj, y_acc, w['s5_d'], w['s5_w_glu'], w['s5_b_glu'], tag + "s5_out")
    w_out = get_ffn(l, s5, 'out')
    if w_out is not None:
        w['w_out'] = w_out
    mixed_t, r1, h1, h1_t, attn_o, attn_lse = _mix_fwd(outs, lses, lru, s5, w['mix_norm_g'], h_in, w['w_out'],
                                                       w['ln1_g'], w['ln1_b'], tag + "mix_out_ln1")
    w['w_up_g'], w['w_down'], ffn_dep = get_ffn(l, h1, 'ffn')
    up, y_conv, act, act_t = _ffn_up_act(h1, w['w_up_g'], w['ffn_conv_w'], w['ffn_conv_b'], tag + "up_act",
                                         dep=ffn_dep)
    r2, out_a, out_b = _proj_ln(act, w['w_down'], h1, w['ln2_g'], w['ln2_b'], tag + "down_ln2", target=target)
    saved = dict(h_in_t=h_in_t, proj=proj, qkv=qkv, lru=lru, lru_saved=lru_saved, x_re=x_re, x_im=x_im,
                 y_pre=y_pre, s5=s5, mixed_t=mixed_t, attn_o=attn_o, attn_lse=attn_lse, r1=r1, h1_t=h1_t, up=up,
                 act_t=act_t, r2=r2, y_conv=y_conv)
    return out_a, out_b, saved


def _layer_bwd_ffn(dh2, sv, w, l, dep=None):
    tag = "l%d_" % l
    g = {}
    dr2, g['ln2_g'], g['ln2_b'] = _ln_bwd(sv['r2'], dh2, w['ln2_g'], tag + "ln2_bwd", dep=dep)
    g['w_down'] = _mm_dw(sv['act_t'], dr2, 1024, D_MODEL, 1024, tag + "down_dw", _grad_dtype(l))
    dup, dh1, dcw_parts, dcb_parts = _ffn_bwd(sv['up'], sv['y_conv'], dr2, w['w_down'], w['w_up_g'],
                                              w['ffn_conv_w'], tag + "ffn_bwd")
    g['ffn_conv_w'] = dcw_parts.sum(axis=0)
    g['ffn_conv_b'] = dcb_parts.sum(axis=0)
    g['w_up_g'] = _mm_up_dw(sv['h1_t'], dup, tag + "up_dw", _grad_dtype(l))
    return dh1, g


def _layer_bwd_mix(dh1, sv, w, cos, sin, l, dep, g_ffn, after_out_grad, after_small_grads, after_in_grad):
    tag = "l%d_" % l
    g = {}
    dr1, g['ln1_g'], g['ln1_b'], d_o, dlru, ds5, g['mix_norm_g'] = _mix_bwd(
        sv['r1'], dh1, w['ln1_g'], w['w_out'], sv['attn_o'][0], sv['lru'], sv['s5'], w['mix_norm_g'],
        tag + "ln1_mix_bwd", dep=dep)
    g['w_out'] = _mm_dw(sv['mixed_t'], dr1, 1024, D_MODEL, 1024, tag + "out_dw", _grad_dtype(l))
    dy, dud, g['s5_d'], g['s5_w_glu'], g['s5_b_glu'] = _s5_out_bwd(
        sv['proj'], sv['y_pre'], ds5, w['s5_d'], w['s5_w_glu'], w['s5_b_glu'], tag + "s5_out_bwd",
        dep=after_out_grad(l, g['w_out']))
    du, g['s5_lam_re'], g['s5_lam_im'], g['s5_bb_re'], g['s5_bb_im'], g['s5_cc_re'], g['s5_cc_im'] = \
        _s5_scan_bwd(sv['proj'], dy, dud, sv['x_re'], sv['x_im'], w['s5_bb_re'], w['s5_bb_im'],
                     w['s5_lam_re'], w['s5_lam_im'], w['s5_cc_re'], w['s5_cc_im'], tag + "s5_scan_bwd")
    (dxr, dgate, g['lru_conv_w'], g['lru_conv_b'], g['lru_wr'], g['lru_br'], g['lru_wi'], g['lru_bi'],
     g['lru_lambda']) = _lru_bwd(sv['proj'], dlru, *sv['lru_saved'], w['lru_conv_w'], w['lru_conv_b'], w['lru_wr'],
                                 w['lru_br'], w['lru_wi'], w['lru_bi'], w['lru_lambda'], tag + "lru_bwd")
    token = after_small_grads(l, _finish_layer_grads({**g_ffn, **g}, w, l))
    dqkv = [_attn_bwd(sv['qkv'][b], sv['attn_o'][b], d_o[b], sv['attn_lse'][b], d, tag + "attn_bwd_d%d" % d,
                      dep=token if b == 0 else None)
            for b, d in enumerate(DILATIONS)]
    dproj = _dproj_assemble(dqkv, dxr, dgate, du, cos, sin, tag + "dproj")
    g_in = _mm_dw(sv['h_in_t'], dproj, 1024, D_IN, 1024, tag + "in_dw", _grad_dtype(l))
    return _mm_nt(dproj, w['w_in'], 512, D_MODEL, tag + "in_dx", add=dr1, add_scale=ALPHA,
                  dep=after_in_grad(l, g_in))


def _s5_rep(a):
    return jnp.repeat(a, S5_C, axis=0)


def _prepare_layer(p, l):
    w = {}
    for n in ('w_in', 'w_out', 's5_w_glu'):
        if n in p:
            w[n] = p[n].astype(BF16)
    w['ffn_conv_w'] = _ffn_interleave(p['ffn_conv_w'])
    w['ffn_conv_b'] = _ffn_interleave(p['ffn_conv_b'])[None, :]
    w['lru_conv_w'] = p['lru_conv_w']
    for n in ('lru_conv_b', 'lru_br', 'lru_bi', 'lru_lambda', 's5_b_glu', 'mix_norm_g',
              'ln1_g', 'ln1_b', 'ln2_g', 'ln2_b'):
        w[n] = p[n][None, :]
    w['lru_wr'] = _block_diag(p['lru_wr']).astype(BF16)
    w['lru_wi'] = _block_diag(p['lru_wi']).astype(BF16)
    w['s5_d'] = p['s5_d'].reshape(1, S5_W)
    disc_in = (_s5_rep(p['s5_a_re']), _s5_rep(p['s5_a_im']),
               _s5_rep(jnp.broadcast_to(p['s5_log_step'][:, None], (S5_G, S5_P))),
               jnp.swapaxes(p['s5_b_re'], 1, 2).reshape(S5_W, S5_P),
               jnp.swapaxes(p['s5_b_im'], 1, 2).reshape(S5_W, S5_P))
    ab_re, ab_im, bb_re, bb_im = _s5_disc_fwd(*disc_in, "l%d_s5_disc" % l)
    w['s5_disc_in'] = disc_in
    w['s5_lam_re'] = ab_re.reshape(S5_G, S5_C, S5_P)[:, 0, :].reshape(1, S5_STATES)
    w['s5_lam_im'] = ab_im.reshape(S5_G, S5_C, S5_P)[:, 0, :].reshape(1, S5_STATES)
    w['s5_bb_re'] = _block_diag(bb_re.reshape(S5_G, S5_C, S5_P)).astype(BF16)
    w['s5_bb_im'] = _block_diag(bb_im.reshape(S5_G, S5_C, S5_P)).astype(BF16)
    w['s5_cc_re'] = _block_diag(jnp.swapaxes(p['s5_c_re'], 1, 2)).astype(BF16)
    w['s5_cc_im'] = _block_diag(jnp.swapaxes(p['s5_c_im'], 1, 2)).astype(BF16)
    return w


def _finish_layer_grads(g, w, l):
    out = {}
    for n in ('s5_w_glu', 'lru_conv_w'):
        out[n] = g[n]
    out['ffn_conv_w'] = _ffn_deinterleave(g['ffn_conv_w'])
    out['ffn_conv_b'] = _ffn_deinterleave(g['ffn_conv_b'])[0]
    for n in ('lru_conv_b', 'lru_br', 'lru_bi', 'lru_lambda', 's5_b_glu', 'mix_norm_g',
              'ln1_g', 'ln1_b', 'ln2_g', 'ln2_b'):
        out[n] = g[n][0]
    out['lru_wr'] = _block_diag_extract(g['lru_wr'], LRU_W // HEAD)
    out['lru_wi'] = _block_diag_extract(g['lru_wi'], LRU_W // HEAD)
    out['s5_d'] = g['s5_d'].reshape(S5_G, S5_C)
    out['s5_c_re'] = jnp.swapaxes(_block_diag_take(g['s5_cc_re'], S5_G), 1, 2)
    out['s5_c_im'] = jnp.swapaxes(_block_diag_take(g['s5_cc_im'], S5_G), 1, 2)
    rep = lambda v: _s5_rep(v.reshape(S5_G, S5_P)) * (1.0 / S5_C)
    cts = (rep(g['s5_lam_re']), rep(g['s5_lam_im']),
           _block_diag_take(g['s5_bb_re'], S5_G).reshape(S5_W, S5_P),
           _block_diag_take(g['s5_bb_im'], S5_G).reshape(S5_W, S5_P))
    da_re, da_im, dls, dbt_re, dbt_im = _s5_disc_bwd(*w['s5_disc_in'], cts, "l%d_s5_disc_bwd" % l)
    out['s5_a_re'] = da_re.reshape(S5_G, S5_C, S5_P).sum(axis=1)
    out['s5_a_im'] = da_im.reshape(S5_G, S5_C, S5_P).sum(axis=1)
    out['s5_log_step'] = dls.reshape(S5_G, S5_C * S5_P).sum(axis=1)
    out['s5_b_re'] = jnp.swapaxes(dbt_re.reshape(S5_G, S5_C, S5_P), 1, 2)
    out['s5_b_im'] = jnp.swapaxes(dbt_im.reshape(S5_G, S5_C, S5_P), 1, 2)
    return out


def _run_step(x, target, get_layer, get_ffn, on_loss, after_ffn_grads, after_out_grad, after_small_grads,
              after_in_grad):
    cos, sin = _rope_tables(x.shape[0])
    h, h_t = x, None
    ws, saved = [], []
    for l in range(DEPTH):
        p, dep = get_layer(l, h)
        ws.append(_prepare_layer(p, l))
        h, h_t, sv = _layer_fwd(h, h_t, ws[l], cos, sin, l, dep, get_ffn, target if l == DEPTH - 1 else None)
        saved.append(sv)
    dh, loss_vec = h, h_t
    on_loss(loss_vec)
    dep = None
    for l in reversed(range(DEPTH)):
        dh1, g = _layer_bwd_ffn(dh, saved[l], ws[l], l, dep)
        dep = after_ffn_grads(l, g)
        dh = _layer_bwd_mix(dh1, saved[l], ws[l], cos, sin, l, dep, g, after_out_grad, after_small_grads,
                            after_in_grad)
        dep = None
    return loss_vec, dh


def _local_step(x, target, layers):
    grads = [{} for _ in range(DEPTH)]

    def ffn(l, after, part):
        if part == 'out':
            return None
        return layers[l]['w_up_g'].astype(BF16), layers[l]['w_down'].astype(BF16), None

    def keep_ffn(l, g):
        grads[l].update(w_up_g=g['w_up_g'], w_down=g['w_down'])

    def keep_small(l, g):
        grads[l].update(g)

    loss, dx = _run_step(x, target, lambda l, h: (layers[l], None), ffn, lambda row: None, keep_ffn,
                         lambda l, g: grads[l].update(w_out=g), keep_small, lambda l, g: grads[l].update(w_in=g))
    return loss[0, 0], dx, grads


def kernel(x, w_in, lru_conv_w, lru_conv_b, lru_wr, lru_br, lru_wi, lru_bi, lru_lambda, s5_a_re, s5_a_im, s5_b_re, s5_b_im, s5_c_re, s5_c_im, s5_d, s5_log_step, s5_w_glu, s5_b_glu, mix_norm_g, w_out, ln1_g, ln1_b, w_up, ffn_conv_w, ffn_conv_b, w_down, ln2_g, ln2_b, loss_target, m_w_in, m_lru_conv_w, m_lru_conv_b, m_lru_wr, m_lru_br, m_lru_wi, m_lru_bi, m_lru_lambda, m_s5_a_re, m_s5_a_im, m_s5_b_re, m_s5_b_im, m_s5_c_re, m_s5_c_im, m_s5_d, m_s5_log_step, m_s5_w_glu, m_s5_b_glu, m_mix_norm_g, m_w_out, m_ln1_g, m_ln1_b, m_w_up, m_ffn_conv_w, m_ffn_conv_b, m_w_down, m_ln2_g, m_ln2_b, v_w_in, v_lru_conv_w, v_lru_conv_b, v_lru_wr, v_lru_br, v_lru_wi, v_lru_bi, v_lru_lambda, v_s5_a_re, v_s5_a_im, v_s5_b_re, v_s5_b_im, v_s5_c_re, v_s5_c_im, v_s5_d, v_s5_log_step, v_s5_w_glu, v_s5_b_glu, v_mix_norm_g, v_w_out, v_ln1_g, v_ln1_b, v_w_up, v_ffn_conv_w, v_ffn_conv_b, v_w_down, v_ln2_g, v_ln2_b):
    args = locals()
    wl = {n: args[n] for n in WEIGHTS}
    ml = {n: args['m_' + n] for n in WEIGHTS}
    vl = {n: args['v_' + n] for n in WEIGHTS}

    small_sizes = [int(wl[n].size) for n in SMALL_SHARDED]
    small_flat = _pad_to(jnp.concatenate([wl[n].reshape(-1) for n in SMALL_SHARDED]), 8 * 1024)
    small_all, w_in0 = _all_gather([small_flat.reshape(-1, 1024), wl['w_in'][0].astype(BF16)], "gather_first")
    small_all = small_all.reshape(N_DEV, -1)
    small_full, off = {}, 0
    for n, sz in zip(SMALL_SHARDED, small_sizes):
        small_full[n] = _gather_full(small_all[:, off:off + sz].reshape((N_DEV,) + wl[n].shape), SHARD_AXIS[n])
        off += sz
    def mixer_params(l, g_in, g_out):
        p = {n: wl[n][l] for n in REPLICATED}
        p.update({n: small_full[n][l] for n in SMALL_SHARDED})
        p['w_in'] = _gather_full(g_in, 1)
        if g_out is not None:
            p['w_out'] = g_out.reshape(D_MODEL, D_MODEL)
        return p

    mix_names, ffn_names = ('w_in', 'w_out'), ('w_up', 'w_down')
    shards = lambda names, l: [wl[n][l].astype(BF16) for n in names]
    gathers = {}
    gathers[0, 'out'], token = _exchange_start(shards(('w_out',), 0), True, "gather_out_l0_start", dep=w_in0)
    gathers[0, 'ffn'], rest0_token = _exchange_start(shards(ffn_names, 0), True, "gather_ffn_l0_start", dep=token)

    def get_layer(l, h):
        if l == 0:
            return mixer_params(0, w_in0, None), rest0_token
        return mixer_params(1, *_exchange_wait(gathers[1, 'mix'], True, h, "gather_mix_l1_wait")), None

    def get_ffn(l, after, part):
        if part == 'out':
            if l > 0:
                return None
            g_out, = _exchange_wait(gathers[0, 'out'], True, after, "gather_out_l0_wait")
            return g_out.reshape(D_MODEL, D_MODEL)
        g_up, g_down = _exchange_wait(gathers[l, 'ffn'], True, after, "gather_ffn_l%d_wait" % l)
        token = None
        if l == 0:
            gathers[1, 'mix'], token = _exchange_start(shards(mix_names, 1), True, "gather_mix_l1_start", dep=g_up)
            gathers[1, 'ffn'], token = _exchange_start(shards(ffn_names, 1), True, "gather_ffn_l1_start", dep=token)
        return g_up, g_down.reshape(D_FF, D_MODEL), token

    scatters = {}

    def after_ffn_grads(l, g):
        send = [g['w_up_g'], g['w_down'].reshape(N_DEV, D_FF // N_DEV, D_MODEL)]
        scatters[l, 'ffn'], token = _exchange_start(send, False, "scatter_ffn_l%d_start" % l)
        return token

    def after_out_grad(l, g_out):
        send = [g_out.reshape(N_DEV, D_MODEL // N_DEV, D_MODEL)]
        scatters[l, 'out'], token = _exchange_start(send, False, "scatter_out_l%d_start" % l)
        return token

    def after_in_grad(l, g_in):
        scatters[l, 'in'], token = _exchange_start([_scatter_blocks(g_in, 1)], False, "scatter_in_l%d_start" % l)
        return token

    def after_small_grads(l, g):
        rep = [g[n][None] for n in REPLICATED]
        if l == DEPTH - 1:
            rep.append(loss_rows[0][None])
        shd = [_scatter_blocks(g[n], SHARD_AXIS[n] - 1)[:, None] for n in SMALL_SHARDED]
        scatters[l, 'rep'], token = _exchange_start(rep, True, "gather_rep_grads_l%d_start" % l)
        scatters[l, 'small'], token = _exchange_start(shd, False, "scatter_small_l%d_start" % l, dep=token)
        return token

    loss_rows = []
    _, grad_x = _run_step(x[0], loss_target[0], get_layer, get_ffn, loss_rows.append, after_ffn_grads,
                          after_out_grad, after_small_grads, after_in_grad)

    results = {}
    big_prev = {n: None for n in BIG}

    def finish_big(l, part, names, after):
        landed = _exchange_wait(scatters[l, part], False, after, "scatter_%s_l%d_wait" % (part, l))
        for n, ld in zip(names, landed):
            big_prev[n] = _adamw_sum(ld, wl[n], ml[n], vl[n], l, big_prev[n], "adamw_%s_l%d" % (n, l))

    for l, part, names in ((1, 'ffn', ffn_names), (1, 'out', ('w_out',)), (1, 'in', ('w_in',)),
                           (0, 'ffn', ffn_names), (0, 'out', ('w_out',))):
        finish_big(l, part, names, grad_x)

    kinds = ('grad', 'delta', 'm', 'v')
    landed = []
    for l in range(DEPTH):
        rep = list(_exchange_wait(scatters[l, 'rep'], True, grad_x, "gather_rep_grads_l%d_wait" % l))
        if l == DEPTH - 1:
            loss = jnp.sum(rep.pop()[:, 0, 0, 0])
        shd = list(_exchange_wait(scatters[l, 'small'], False, grad_x, "scatter_small_l%d_wait" % l))
        landed.append(dict(zip(REPLICATED + SMALL_SHARDED, rep + shd)))
    matrices = ['lru_wr', 'lru_wi', 's5_a_re', 's5_a_im', 's5_c_re', 's5_c_im', 's5_d']
    widest = ['s5_b_re', 's5_b_im']
    vectors = [n for n in REPLICATED + SMALL_SHARDED if n not in matrices + widest]
    last = None
    for tag, names in (("vectors", vectors), ("matrices", matrices), ("s5_b", widest)):
        res = _adamw_many([[landed[l][n] for n in names] for l in range(DEPTH)], [wl[n] for n in names],
                          [ml[n] for n in names], [vl[n] for n in names], "adamw_" + tag)
        for kind, arrs in zip(kinds, res):
            for n, a in zip(names, arrs):
                results[kind, n] = a
        last = res[0][0]
    finish_big(0, 'in', ('w_in',), last)
    for n in BIG:
        results['grad', n], results['delta', n], results['m', n], results['v', n] = big_prev[n]

    out = [loss, grad_x[None]]
    for kind in kinds:
        out.extend(results[kind, n] for n in WEIGHTS)
    return tuple(out)
```

```python
import math

import jax
import jax.numpy as jnp
from jax import lax
from jax.experimental import pallas as pl
from jax.experimental.pallas import tpu as pltpu

F32 = jnp.float32
BF16 = jnp.bfloat16

N_DEV = 8
DEPTH = 2
D_MODEL = 1024
ATTN_W = 384
LRU_W = 384
S5_W = 256
D_IN = 2176
D_FF = 3072
HEAD = 64
ATTN_BLK = 128
ATTN_TILE = 1024
DILATIONS = (1, 4, 16)
S5_G = 16
S5_P = 64
S5_C = 16
S5_STATES = S5_G * S5_P
LRU_C = 8.0
LRU_CONV = 4
FFN_CONV = 3
ROPE_THETA = 10000.0
ALPHA = (2 * DEPTH) ** 0.25
LN_EPS = 1e-5
RMS_EPS = 1e-6
ADAM_LR, ADAM_B1, ADAM_B2, ADAM_EPS, ADAM_WD, ADAM_STEP = 0.001, 0.9, 0.999, 1e-8, 0.01, 10

LANE = 128
SCAN_T = 512
S5_BLK = 256
FFN_CB = 2 * D_FF // N_DEV
VMEM_LIMIT = 56 * 1024 * 1024

WEIGHTS = ['w_in', 'lru_conv_w', 'lru_conv_b', 'lru_wr', 'lru_br', 'lru_wi', 'lru_bi', 'lru_lambda',
           's5_a_re', 's5_a_im', 's5_b_re', 's5_b_im', 's5_c_re', 's5_c_im', 's5_d', 's5_log_step',
           's5_w_glu', 's5_b_glu', 'mix_norm_g', 'w_out', 'ln1_g', 'ln1_b', 'w_up', 'ffn_conv_w',
           'ffn_conv_b', 'w_down', 'ln2_g', 'ln2_b']
SHARD_AXIS = {'w_in': 2, 'lru_conv_w': 2, 's5_w_glu': 1, 'w_out': 1, 'w_up': 2, 'ffn_conv_w': 2, 'w_down': 1}
BIG = ['w_in', 'w_out', 'w_up', 'w_down']
SMALL_SHARDED = ['lru_conv_w', 'ffn_conv_w', 's5_w_glu']
REPLICATED = [n for n in WEIGHTS if n not in SHARD_AXIS]


def _cparams(sem=None):
    return pltpu.CompilerParams(dimension_semantics=sem, vmem_limit_bytes=VMEM_LIMIT)


def _grad_dtype(l):
    return BF16 if l == 0 else F32


def _ffn_dev(jb):
    return jb // 2 + (N_DEV // 2) * (jb % 2)


def _gelu(x):
    c = math.sqrt(2.0 / math.pi)
    t = jnp.tanh(c * (x + 0.044715 * (x * x * x)))
    return 0.5 * x * (1.0 + t)


def _gelu_grad(x):
    c = math.sqrt(2.0 / math.pi)
    x2 = x * x
    t = jnp.tanh(c * (x + 0.044715 * (x2 * x)))
    return 0.5 * (1.0 + t) + 0.5 * x * (1.0 - t * t) * (c * (1.0 + 3.0 * 0.044715 * x2))


def _sigmoid(x):
    return 1.0 / (1.0 + jnp.exp(-x))


def _log1p(x):
    u = 1.0 + x
    d = u - 1.0
    return jnp.where(d == 0.0, x, jnp.log(u) * (x / jnp.where(d == 0.0, 1.0, d)))


def _softplus(x):
    return jnp.maximum(x, 0.0) + _log1p(jnp.exp(-jnp.abs(x)))


def _expm1(x):
    return jnp.tanh(0.5 * x) * (jnp.exp(x) + 1.0)


def _dot(a, b):
    return jnp.dot(a.astype(BF16), b.astype(BF16), preferred_element_type=F32)


def _dot_nt(a, b):
    return lax.dot_general(a.astype(BF16), b.astype(BF16), (((1,), (1,)), ((), ())),
                           preferred_element_type=F32)


def _dot_tn(a, b):
    return lax.dot_general(a.astype(BF16), b.astype(BF16), (((0,), (0,)), ((), ())),
                           preferred_element_type=F32)


def _rows(shape):
    return lax.broadcasted_iota(jnp.int32, shape, 0)


def _shift_down_prev(x, s, prev8):
    if s == 0:
        return x
    t, l = x.shape
    r = pltpu.roll(x, s, axis=0)
    pr = pltpu.roll(prev8, s, axis=0)
    pad = jnp.concatenate([pr, jnp.zeros((t - 8, l), x.dtype)], axis=0)
    return jnp.where(_rows(x.shape) < s, pad, r)


def _shift_up_next(x, s, next8):
    if s == 0:
        return x
    t, l = x.shape
    r = pltpu.roll(x, t - s, axis=0)
    nx = pltpu.roll(next8, 8 - s, axis=0)
    pad = jnp.concatenate([jnp.zeros((t - 8, l), x.dtype), nx], axis=0)
    return jnp.where(_rows(x.shape) >= t - s, pad, r)


SUB = 8


def _tile_shift(x, s, fill, reverse):
    t = x.shape[0]
    pos = _rows(x.shape) & (SUB - 1)
    if reverse:
        return jnp.where(pos < SUB - s, pltpu.roll(x, t - s, axis=0), fill)
    return jnp.where(pos >= s, pltpu.roll(x, s, axis=0), fill)


def _scan_chunk(a, x, carry, reverse=False):
    s = 1
    while s < SUB:
        x = x + a * _tile_shift(x, s, 0.0, reverse)
        a = a * _tile_shift(a, s, 1.0, reverse)
        s *= 2
    nv = x.shape[0] // SUB
    out = [None] * nv
    for v in (reversed(range(nv)) if reverse else range(nv)):
        rows = slice(v * SUB, (v + 1) * SUB)
        out[v] = x[rows, :] + a[rows, :] * carry
        carry = out[v][0:1, :] if reverse else out[v][SUB - 1:SUB, :]
    return jnp.concatenate(out, axis=0)


def _cmul(ar, ai, br, bi):
    return ar * br - ai * bi, ar * bi + ai * br


def _cscan_consts(lr, li, reverse):
    pows = [(lr, li)]
    for _ in range(2):
        pows.append(_cmul(*pows[-1], *pows[-1]))
    rows = [(lr, li)]
    for _ in range(SUB - 1):
        rows.append(_cmul(*rows[-1], lr, li))
    if reverse:
        rows = rows[::-1]
    return pows, (jnp.concatenate([r for r, _ in rows], axis=0), jnp.concatenate([i for _, i in rows], axis=0))


def _cscan_chunk(xr, xi, consts, carry, reverse=False):
    pows, (p8r, p8i) = consts
    s = 1
    for pr, pi in pows:
        sr = _tile_shift(xr, s, 0.0, reverse)
        si = _tile_shift(xi, s, 0.0, reverse)
        xr, xi = xr + pr * sr - pi * si, xi + pr * si + pi * sr
        s *= 2
    nv = xr.shape[0] // SUB
    out_r, out_i = [None] * nv, [None] * nv
    cr, ci = carry
    for v in (reversed(range(nv)) if reverse else range(nv)):
        rows = slice(v * SUB, (v + 1) * SUB)
        out_r[v] = xr[rows, :] + p8r * cr - p8i * ci
        out_i[v] = xi[rows, :] + p8r * ci + p8i * cr
        edge = slice(0, 1) if reverse else slice(SUB - 1, SUB)
        cr, ci = out_r[v][edge, :], out_i[v][edge, :]
    return jnp.concatenate(out_r, axis=0), jnp.concatenate(out_i, axis=0)


def _dep_args(dep):
    return ([], []) if dep is None else ([pl.BlockSpec(memory_space=pl.ANY)], [dep])


def _mm_nt(a, w, tm, tn, name, add=None, add_scale=1.0, dep=None):
    m, k = a.shape
    n = w.shape[0]

    def body(a_ref, w_ref, *rest):
        o_ref = rest[-1]
        if add is None:
            o_ref[...] = _dot_nt(a_ref[...], w_ref[...])
        else:
            o_ref[...] = _dot_nt(a_ref[...], w_ref[...]) + add_scale * rest[0][...]

    in_specs = [pl.BlockSpec((tm, k), lambda j, i: (i, 0)), pl.BlockSpec((tn, k), lambda j, i: (j, 0))]
    args = [a, w]
    if add is not None:
        in_specs.append(pl.BlockSpec((tm, tn), lambda j, i: (i, j)))
        args.append(add)
    dep_specs, dep_ops = _dep_args(dep)
    return pl.pallas_call(
        body, out_shape=jax.ShapeDtypeStruct((m, n), F32), grid=(n // tn, m // tm),
        in_specs=in_specs + dep_specs, out_specs=pl.BlockSpec((tm, tn), lambda j, i: (i, j)), name=name,
        compiler_params=_cparams(("parallel", "parallel")))(*args, *dep_ops)


def _mm_dw(at, b, tm, tn, ts, name, out_dtype=F32):
    m, s = at.shape
    n = b.shape[1]
    nk = s // ts

    def body(a_ref, b_ref, o_ref, acc):
        @pl.when(pl.program_id(2) == 0)
        def _():
            acc[...] = jnp.zeros_like(acc)
        acc[...] += _dot(a_ref[...], b_ref[...])

        @pl.when(pl.program_id(2) == nk - 1)
        def _():
            o_ref[...] = acc[...].astype(out_dtype)

    return pl.pallas_call(
        body, out_shape=jax.ShapeDtypeStruct((m, n), out_dtype), grid=(m // tm, n // tn, nk),
        in_specs=[pl.BlockSpec((tm, ts), lambda i, j, k: (i, k)), pl.BlockSpec((ts, tn), lambda i, j, k: (k, j))],
        out_specs=pl.BlockSpec((tm, tn), lambda i, j, k: (i, j)),
        scratch_shapes=[pltpu.VMEM((tm, tn), F32)], name=name,
        compiler_params=_cparams(("parallel", "parallel", "arbitrary")))(at, b)


def _mm_up_dw(ht, dup, name, out_dtype=F32):
    d, s = ht.shape

    def body(a_ref, b_ref, o_ref):
        o_ref[...] = _dot(a_ref[...], b_ref[...]).astype(out_dtype)

    return pl.pallas_call(
        body, out_shape=jax.ShapeDtypeStruct((N_DEV, d, FFN_CB), out_dtype), grid=(N_DEV,),
        in_specs=[pl.BlockSpec((d, s), lambda j: (0, 0)), pl.BlockSpec((s, FFN_CB), lambda j: (0, j))],
        out_specs=pl.BlockSpec((None, d, FFN_CB), lambda j: (_ffn_dev(j), 0, 0)), name=name,
        compiler_params=_cparams(("parallel",)))(ht, dup)


def _layer_norm(r, g, b):
    mu = jnp.mean(r, axis=-1, keepdims=True)
    xc = r - mu
    var = jnp.mean(xc * xc, axis=-1, keepdims=True)
    return xc * lax.rsqrt(var + LN_EPS) * g + b


def _proj_ln(a, w, resid, g, bias, name, transposed=True, target=None):
    s, k = a.shape
    d = w.shape[1]
    tm = 512

    def body(a_ref, w_ref, x_ref, g_ref, bias_ref, *rest):
        r = ALPHA * x_ref[...] + _dot(a_ref[...], w_ref[...])
        h = _layer_norm(r, g_ref[...], bias_ref[...])
        if target is None:
            r_ref, h_ref = rest[0], rest[1]
            h_ref[...] = h
            if transposed:
                rest[2][...] = h.T.astype(BF16)
        else:
            t_ref, r_ref, dy_ref, l_ref = rest

            @pl.when(pl.program_id(0) == 0)
            def _():
                l_ref[...] = jnp.zeros_like(l_ref)
            e = h - t_ref[...]
            dy_ref[...] = e * (1.0 / d)
            part = 0.5 * jnp.sum(jnp.mean(e * e, axis=-1, keepdims=True), axis=0, keepdims=True)
            l_ref[...] += jnp.broadcast_to(part, l_ref.shape)
        r_ref[...] = r

    row = pl.BlockSpec((tm, d), lambda i: (i, 0))
    vec = pl.BlockSpec((1, d), lambda i: (0, 0))
    in_specs = [pl.BlockSpec((tm, k), lambda i: (i, 0)), pl.BlockSpec((k, d), lambda i: (0, 0)), row, vec, vec]
    args = [a, w, resid, g, bias]
    shapes = [jax.ShapeDtypeStruct((s, d), F32), jax.ShapeDtypeStruct((s, d), F32)]
    specs = [row, row]
    if target is not None:
        in_specs.append(row)
        args.append(target)
        shapes.append(jax.ShapeDtypeStruct((1, LANE), F32))
        specs.append(pl.BlockSpec((1, LANE), lambda i: (0, 0)))
    elif transposed:
        shapes.append(jax.ShapeDtypeStruct((d, s), BF16))
        specs.append(pl.BlockSpec((d, tm), lambda i: (0, i)))
    return pl.pallas_call(
        body, out_shape=tuple(shapes), grid=(s // tm,), in_specs=in_specs, out_specs=tuple(specs), name=name,
        compiler_params=_cparams(("arbitrary",) if target is not None else ("parallel",)))(*args)


def _layer_norm_bwd(r, dh, g):
    mu = jnp.mean(r, axis=-1, keepdims=True)
    xc = r - mu
    var = jnp.mean(xc * xc, axis=-1, keepdims=True)
    rstd = lax.rsqrt(var + LN_EPS)
    xh = xc * rstd
    dxh = dh * g
    m1 = jnp.mean(dxh, axis=-1, keepdims=True)
    m2 = jnp.mean(dxh * xh, axis=-1, keepdims=True)
    return (rstd * (dxh - m1 - xh * m2), jnp.sum(dh * xh, axis=0, keepdims=True),
            jnp.sum(dh, axis=0, keepdims=True))


def _ln_bwd(r, dh, g, name, dep=None):
    s, d = r.shape
    tm = 512

    def body(r_ref, dh_ref, g_ref, *rest):
        dr_ref, dg_ref, db_ref = rest[-3:]

        @pl.when(pl.program_id(0) == 0)
        def _():
            dg_ref[...] = jnp.zeros_like(dg_ref)
            db_ref[...] = jnp.zeros_like(db_ref)
        dr_ref[...], dg_rows, db_rows = _layer_norm_bwd(r_ref[...], dh_ref[...], g_ref[...])
        dg_ref[...] += dg_rows
        db_ref[...] += db_rows

    row = pl.BlockSpec((tm, d), lambda i: (i, 0))
    vec = pl.BlockSpec((1, d), lambda i: (0, 0))
    dep_specs, dep_ops = _dep_args(dep)
    return pl.pallas_call(
        body, out_shape=(jax.ShapeDtypeStruct((s, d), F32), jax.ShapeDtypeStruct((1, d), F32),
                         jax.ShapeDtypeStruct((1, d), F32)),
        grid=(s // tm,), in_specs=[row, row, vec] + dep_specs, out_specs=(row, vec, vec), name=name,
        compiler_params=_cparams(("arbitrary",)))(r, dh, g, *dep_ops)


def _rope_tables(s):
    half = HEAD // 2
    pos = jnp.arange(s, dtype=F32)
    inv = ROPE_THETA ** (-jnp.arange(half, dtype=F32) * 2.0 / HEAD)
    ang = pos[:, None] * inv[None, :]
    cos, sin = jnp.cos(ang), jnp.sin(ang)
    cos = jnp.concatenate([cos, cos, cos, cos], axis=1)
    sin = jnp.concatenate([-sin, sin, -sin, sin], axis=1)
    return cos, sin


def _rotate(x, cos, sin):
    lane = lax.broadcasted_iota(jnp.int32, x.shape, 1)
    partner = jnp.where((lane % HEAD) < HEAD // 2, pltpu.roll(x, LANE - HEAD // 2, axis=1),
                        pltpu.roll(x, HEAD // 2, axis=1))
    return x * cos + partner * sin


def _class_rows(c, d, tm):
    return pl.ds(c, tm // d, stride=d) if d > 1 else pl.ds(0, tm)


def _dilated_spec(tm, d, w):
    return pl.BlockSpec((tm // d, d * w), lambda i: (i, 0))


def _token_scratch(tm, w):
    return pltpu.VMEM((w // LANE, tm, LANE), F32)


def _to_tokens(src_ref, dst3, d, tm):
    nj = dst3.shape[0]
    for cls in range(d):
        for j in range(nj):
            col = (cls * nj + j) * LANE
            dst3.at[j][_class_rows(cls, d, tm), :] = src_ref[:, col:col + LANE]


def _to_dilated(src3, dst_ref, d, tm):
    nj = src3.shape[0]
    for cls in range(d):
        for j in range(nj):
            col = (cls * nj + j) * LANE
            dst_ref[:, col:col + LANE] = src3.at[j][_class_rows(cls, d, tm), :].astype(dst_ref.dtype)


def _token_value(src3):
    return jnp.concatenate([src3[j] for j in range(src3.shape[0])], axis=1)


def _proj_rope(h, w_in, cos, sin, name, dep=None, transposed=False):
    s, d_model = h.shape
    tm = 512
    w = 3 * ATTN_W
    nj = w // LANE

    def body(h_ref, w_ref, c_ref, s_ref, *rest):
        rot = rest[-1]
        if transposed:
            p_ref, o_refs, ht_ref = rest[-6], rest[-5:-2], rest[-2]
            ht_ref[...] = h_ref[...].T.astype(BF16)
        else:
            p_ref, o_refs = rest[-5], rest[-4:-1]
        y = _dot(h_ref[...], w_ref[...])
        p_ref[...] = y
        c, sn = c_ref[...], s_ref[...]
        for j in range(nj):
            x = y[:, j * LANE:(j + 1) * LANE]
            rot[j] = _rotate(x, c, sn) if j < 2 * ATTN_W // LANE else x
        for d, o_ref in zip(DILATIONS, o_refs):
            _to_dilated(rot, o_ref, d, tm)

    tab = pl.BlockSpec((tm, LANE), lambda i: (i, 0))
    dep_specs, dep_ops = _dep_args(dep)
    shapes = [jax.ShapeDtypeStruct((s, D_IN), F32), *[jax.ShapeDtypeStruct((s // d, d * w), BF16) for d in DILATIONS]]
    specs = [pl.BlockSpec((tm, D_IN), lambda i: (i, 0)), *[_dilated_spec(tm, d, w) for d in DILATIONS]]
    if transposed:
        shapes.append(jax.ShapeDtypeStruct((d_model, s), BF16))
        specs.append(pl.BlockSpec((d_model, tm), lambda i: (0, i)))
    res = pl.pallas_call(
        body, out_shape=tuple(shapes), grid=(s // tm,),
        in_specs=[pl.BlockSpec((tm, d_model), lambda i: (i, 0)), pl.BlockSpec((d_model, D_IN), lambda i: (0, 0)),
                  tab, tab] + dep_specs,
        out_specs=tuple(specs), scratch_shapes=[_token_scratch(tm, w)], name=name,
        compiler_params=_cparams(("parallel",)))(h, w_in, cos, sin, *dep_ops)
    return res[0], res[1:4], (res[4] if transposed else None)


def _dproj_assemble(dqkv_list, dxr, dgate, du, cos, sin, name):
    s = dxr.shape[0]
    tm = 512
    nq = 3 * ATTN_W // LANE

    def body(*refs):
        br = refs[:9]
        dxr_ref, dg_ref, du_ref, c_ref, s_ref, o_ref = refs[9:15]
        tok = refs[15:]
        c, sn = c_ref[...], -s_ref[...]
        for part in range(3):
            for b, d in enumerate(DILATIONS[1:], start=1):
                _to_tokens(br[3 * b + part], tok[2 * part + b - 1], d, tm)
        for j in range(nq):
            part, jj = divmod(j, ATTN_W // LANE)
            x = br[part][:, jj * LANE:(jj + 1) * LANE] + tok[2 * part][jj] + tok[2 * part + 1][jj]
            if part < 2:
                x = _rotate(x, c, sn)
            o_ref[:, j * LANE:(j + 1) * LANE] = x.astype(BF16)
        o_ref[:, 3 * ATTN_W:3 * ATTN_W + LRU_W] = dxr_ref[...].astype(BF16)
        o_ref[:, 3 * ATTN_W + LRU_W:3 * ATTN_W + 2 * LRU_W] = dg_ref[...].astype(BF16)
        o_ref[:, 3 * ATTN_W + 2 * LRU_W:] = du_ref[...].astype(BF16)

    a_spec = pl.BlockSpec((tm, ATTN_W), lambda i: (i, 0))
    tab = pl.BlockSpec((tm, LANE), lambda i: (i, 0))
    ordered = [dqkv_list[b][p] for b in range(3) for p in range(3)]
    d_specs = [_dilated_spec(tm, d, ATTN_W) for d in DILATIONS for _ in range(3)]
    return pl.pallas_call(
        body, out_shape=jax.ShapeDtypeStruct((s, D_IN), BF16), grid=(s // tm,),
        in_specs=d_specs + [a_spec, a_spec, pl.BlockSpec((tm, S5_W), lambda i: (i, 0)), tab, tab],
        out_specs=pl.BlockSpec((tm, D_IN), lambda i: (i, 0)),
        scratch_shapes=[_token_scratch(tm, ATTN_W)] * 6, name=name,
        compiler_params=_cparams(("parallel",)))(*ordered, dxr, dgate, du, cos, sin)


def _attn_tiles(s, d):
    m = s // d
    tq = min(m, ATTN_TILE)
    return m, tq, tq // ATTN_BLK


def _band_mask(qb):
    qi = lax.broadcasted_iota(jnp.int32, (ATTN_BLK, 2 * ATTN_BLK), 0)
    ki = lax.broadcasted_iota(jnp.int32, (ATTN_BLK, 2 * ATTN_BLK), 1)
    dist = qi + ATTN_BLK - ki
    return (dist >= 0) & (dist <= ATTN_BLK) & ((ki >= ATTN_BLK) | (qb > 0))


def _attn_fwd(qv, d, name):
    m = qv.shape[0]
    w3 = 3 * ATTN_W
    _, tq, n = _attn_tiles(m * d, d)
    scale = HEAD ** -0.5

    def body(x_ref, p_ref, o_ref, l_ref):
        b = pl.program_id(1)

        def block(i, first):
            r0 = 0 if first else pl.multiple_of(i * ATTN_BLK, ATTN_BLK)
            rows = pl.ds(r0, ATTN_BLK)
            valid = _band_mask(b * n + i)
            if not first:
                krows = pl.ds(pl.multiple_of(i * ATTN_BLK - ATTN_BLK, ATTN_BLK), 2 * ATTN_BLK)
            low = lax.broadcasted_iota(jnp.int32, (1, LANE), 1) < HEAD
            for hp in range(ATTN_W // LANE):
                qs, ks, vs = (slice(part * ATTN_W + hp * LANE, part * ATTN_W + (hp + 1) * LANE) for part in range(3))
                q2 = x_ref[rows, qs]
                if first:
                    k2 = jnp.concatenate([p_ref[:, ks], x_ref[0:ATTN_BLK, ks]], axis=0)
                    v2 = jnp.concatenate([p_ref[:, vs], x_ref[0:ATTN_BLK, vs]], axis=0)
                else:
                    k2 = x_ref[krows, ks]
                    v2 = x_ref[krows, vs]
                outs, lses = [], []
                for mask in (low, ~low):
                    q = jnp.where(mask, q2, jnp.zeros_like(q2))
                    sc = jnp.where(valid, _dot_nt(q, k2) * scale, -1e30)
                    mx = jnp.max(sc, axis=-1, keepdims=True)
                    p = jnp.exp(sc - mx)
                    l = jnp.sum(p, axis=-1, keepdims=True)
                    outs.append(_dot(p, v2) / l)
                    lses.append(mx + jnp.log(l))
                o_ref[rows, hp * LANE:(hp + 1) * LANE] = jnp.where(low, outs[0], outs[1])
                l_ref[rows, hp * LANE:(hp + 1) * LANE] = jnp.where(low, lses[0], lses[1])

        block(0, True)
        if n > 1:
            def loop(i, carry):
                block(i, False)
                return carry
            lax.fori_loop(1, n, loop, 0)

    shp = jax.ShapeDtypeStruct((m, d * ATTN_W), F32)
    ospec = pl.BlockSpec((tq, ATTN_W), lambda c, b: (b, c))
    out, lse = pl.pallas_call(
        body, out_shape=(shp, shp), grid=(d, m // tq),
        in_specs=[pl.BlockSpec((tq, w3), lambda c, b: (b, c)),
                  pl.BlockSpec((ATTN_BLK, w3), lambda c, b: (jnp.maximum(b * n - 1, 0), c))],
        out_specs=(ospec, ospec), name=name,
        compiler_params=_cparams(("parallel", "parallel")))(qv, qv)
    return out, lse


def _attn_bwd(qv, ov, dov, lv, d, name, dep=None):
    m = qv.shape[0]
    w3 = 3 * ATTN_W
    _, tq, n = _attn_tiles(m * d, d)
    nb = m // ATTN_BLK
    scale = HEAD ** -0.5

    def body(x_ref, p_ref, nx_ref, o_ref, do_ref, l_ref, on_ref, don_ref, ln_ref, *rest):
        dq_ref, dk_ref, dv_ref = rest[-3:]
        b = pl.program_id(1)
        dk_ref[...] = jnp.zeros_like(dk_ref)
        dv_ref[...] = jnp.zeros_like(dv_ref)

        low = lax.broadcasted_iota(jnp.int32, (1, LANE), 1) < HEAD

        def pair_grads(q2, k2, v2, o2, do2, l2, valid):
            dq, dk, dv = [], 0.0, 0.0
            for mask, lse in ((low, l2[:, 0:1]), (~low, l2[:, HEAD:HEAD + 1])):
                q = jnp.where(mask, q2, jnp.zeros_like(q2))
                do = jnp.where(mask, do2, 0.0)
                sc = jnp.where(valid, _dot_nt(q, k2) * scale, -1e30)
                p = jnp.exp(sc - lse)
                delta = jnp.sum(do * o2, axis=-1, keepdims=True)
                ds = p * (_dot_nt(do, v2) - delta) * scale
                dq.append(_dot(ds, k2))
                dk = dk + _dot_tn(ds, q)
                dv = dv + _dot_tn(p, do)
            return jnp.where(low, dq[0], dq[1]), dk, dv

        def cols(hp):
            return [slice(part * ATTN_W + hp * LANE, part * ATTN_W + (hp + 1) * LANE) for part in range(3)]

        def block(i, first):
            r0 = 0 if first else pl.multiple_of(i * ATTN_BLK, ATTN_BLK)
            rows = pl.ds(r0, ATTN_BLK)
            valid = _band_mask(b * n + i)
            if not first:
                krows = pl.ds(pl.multiple_of(i * ATTN_BLK - ATTN_BLK, ATTN_BLK), 2 * ATTN_BLK)
            for hp in range(ATTN_W // LANE):
                qs, ks, vs = cols(hp)
                if first:
                    k2 = jnp.concatenate([p_ref[:, ks], x_ref[0:ATTN_BLK, ks]], axis=0)
                    v2 = jnp.concatenate([p_ref[:, vs], x_ref[0:ATTN_BLK, vs]], axis=0)
                else:
                    k2 = x_ref[krows, ks]
                    v2 = x_ref[krows, vs]
                dq, dk, dv = pair_grads(x_ref[rows, qs], k2, v2, o_ref[rows, qs], do_ref[rows, qs],
                                        l_ref[rows, qs], valid)
                dq_ref[rows, qs] = dq
                if first:
                    dk_ref[0:ATTN_BLK, qs] += dk[ATTN_BLK:, :]
                    dv_ref[0:ATTN_BLK, qs] += dv[ATTN_BLK:, :]
                else:
                    dk_ref[krows, qs] += dk
                    dv_ref[krows, qs] += dv

        block(0, True)
        if n > 1:
            def loop(i, carry):
                block(i, False)
                return carry
            lax.fori_loop(1, n, loop, 0)

        last = slice((n - 1) * ATTN_BLK, n * ATTN_BLK)
        qi = lax.broadcasted_iota(jnp.int32, (ATTN_BLK, ATTN_BLK), 0)
        ki = lax.broadcasted_iota(jnp.int32, (ATTN_BLK, ATTN_BLK), 1)
        valid_next = (qi <= ki) & ((b + 1) * n < nb)
        for hp in range(ATTN_W // LANE):
            qs, ks, vs = cols(hp)
            _, dk, dv = pair_grads(nx_ref[:, qs], x_ref[last, ks], x_ref[last, vs], on_ref[:, qs], don_ref[:, qs],
                                   ln_ref[:, qs], valid_next)
            dk_ref[last, qs] += dk
            dv_ref[last, qs] += dv

    nxt = lambda b: jnp.minimum((b + 1) * n, nb - 1)
    xs = pl.BlockSpec((tq, w3), lambda c, b: (b, c))
    xp = pl.BlockSpec((ATTN_BLK, w3), lambda c, b: (jnp.maximum(b * n - 1, 0), c))
    xn = pl.BlockSpec((ATTN_BLK, w3), lambda c, b: (nxt(b), c))
    a = pl.BlockSpec((tq, ATTN_W), lambda c, b: (b, c))
    an = pl.BlockSpec((ATTN_BLK, ATTN_W), lambda c, b: (nxt(b), c))
    shp = jax.ShapeDtypeStruct((m, d * ATTN_W), F32)
    dep_specs, dep_ops = _dep_args(dep)
    return pl.pallas_call(
        body, out_shape=(shp, shp, shp), grid=(d, m // tq),
        in_specs=[xs, xp, xn, a, a, a, an, an, an] + dep_specs, out_specs=(a, a, a), name=name,
        compiler_params=_cparams(("parallel", "parallel")))(qv, qv, qv, ov, dov, lv, ov, dov, lv, *dep_ops)


def _rms(x, g):
    ms = jnp.mean(x * x, axis=-1, keepdims=True)
    return x * lax.rsqrt(ms + RMS_EPS) * g


def _rms_bwd(x, g, dy):
    ms = jnp.mean(x * x, axis=-1, keepdims=True)
    r = lax.rsqrt(ms + RMS_EPS)
    dyg = dy * g
    dx = r * dyg - x * (r * r * r) * jnp.mean(x * dyg, axis=-1, keepdims=True)
    return dx, dy * x * r


def _mix_fwd(outs, lses, lru, s5, g, h_in, w_out, ln_g, ln_b, name):
    s = lru.shape[0]
    tm = 512

    def body(o1, o2, o3, l1, l2, l3, lru_ref, s5_ref, g_ref, x_ref, w_ref, lg_ref, lb_ref,
             mixed_t_ref, r_ref, h_ref, ht_ref, ov1, ov2, ov3, lv1, lv2, lv3, so2, so3, sl2, sl3):
        for d, src, dst in ((DILATIONS[1], o2, so2), (DILATIONS[2], o3, so3),
                            (DILATIONS[1], l2, sl2), (DILATIONS[2], l3, sl3)):
            _to_tokens(src, dst, d, tm)
        a1, a2, a3 = l1[...], _token_value(sl2), _token_value(sl3)
        mx = jnp.maximum(jnp.maximum(a1, a2), a3)
        e1, e2, e3 = jnp.exp(a1 - mx), jnp.exp(a2 - mx), jnp.exp(a3 - mx)
        den = e1 + e2 + e3
        o = (e1 * o1[...] + e2 * _token_value(so2) + e3 * _token_value(so3)) / den
        lse = mx + jnp.log(den)
        ov1[...] = o
        lv1[...] = lse
        for j in range(ATTN_W // LANE):
            so2[j] = o[:, j * LANE:(j + 1) * LANE]
            sl2[j] = lse[:, j * LANE:(j + 1) * LANE]
        for d, o_dst, l_dst in ((DILATIONS[1], ov2, lv2), (DILATIONS[2], ov3, lv3)):
            _to_dilated(so2, o_dst, d, tm)
            _to_dilated(sl2, l_dst, d, tm)
        gg = g_ref[...]
        mixed = jnp.concatenate([_rms(o, gg[:, :ATTN_W]),
                                 _rms(lru_ref[...], gg[:, ATTN_W:ATTN_W + LRU_W]),
                                 _rms(s5_ref[...], gg[:, ATTN_W + LRU_W:])], axis=1)
        mixed_t_ref[...] = mixed.T.astype(BF16)
        r = ALPHA * x_ref[...] + _dot(mixed, w_ref[...])
        h = _layer_norm(r, lg_ref[...], lb_ref[...])
        r_ref[...] = r
        h_ref[...] = h
        ht_ref[...] = h.T.astype(BF16)

    a = pl.BlockSpec((tm, ATTN_W), lambda i: (i, 0))
    s5s = pl.BlockSpec((tm, S5_W), lambda i: (i, 0))
    full = pl.BlockSpec((tm, D_MODEL), lambda i: (i, 0))
    vec = pl.BlockSpec((1, D_MODEL), lambda i: (0, 0))
    dil = [_dilated_spec(tm, d, ATTN_W) for d in DILATIONS]
    dshape = [jax.ShapeDtypeStruct((s // d, d * ATTN_W), F32) for d in DILATIONS]
    tshape = jax.ShapeDtypeStruct((D_MODEL, s), BF16)
    fshape = jax.ShapeDtypeStruct((s, D_MODEL), F32)
    tspec = pl.BlockSpec((D_MODEL, tm), lambda i: (0, i))
    res = pl.pallas_call(
        body, out_shape=(tshape, fshape, fshape, tshape, *dshape, *dshape),
        grid=(s // tm,),
        in_specs=dil + dil + [a, s5s, vec, full, pl.BlockSpec((D_MODEL, D_MODEL), lambda i: (0, 0)), vec, vec],
        out_specs=(tspec, full, full, tspec, *dil, *dil),
        scratch_shapes=[_token_scratch(tm, ATTN_W)] * 4, name=name,
        compiler_params=_cparams(("parallel",)))(*outs, *lses, lru, s5, g, h_in, w_out, ln_g, ln_b)
    return res[0], res[1], res[2], res[3], res[4:7], res[7:10]


def _mix_bwd(r, dh, ln_g, w_out, o, lru, s5, g, name, dep=None):
    s = lru.shape[0]
    tm = 512

    def body(r_ref, dh_ref, lg_ref, w_ref, o_ref, lru_ref, s5_ref, g_ref, *rest):
        dr_ref, dlg_ref, dlb_ref, do_ref, do2_ref, do3_ref, dlru_ref, ds5_ref, dg_ref, stage = rest[-10:]

        @pl.when(pl.program_id(0) == 0)
        def _():
            dg_ref[...] = jnp.zeros_like(dg_ref)
            dlg_ref[...] = jnp.zeros_like(dlg_ref)
            dlb_ref[...] = jnp.zeros_like(dlb_ref)
        gg = g_ref[...]
        dr, dlg_rows, dlb_rows = _layer_norm_bwd(r_ref[...], dh_ref[...], lg_ref[...])
        dr_ref[...] = dr
        dlg_ref[...] += dlg_rows
        dlb_ref[...] += dlb_rows
        dm = _dot_nt(dr, w_ref[...])
        dx, dgr = _rms_bwd(o_ref[...], gg[:, :ATTN_W], dm[:, :ATTN_W])
        do_ref[...] = dx
        for j in range(ATTN_W // LANE):
            stage[j] = dx[:, j * LANE:(j + 1) * LANE]
        _to_dilated(stage, do2_ref, DILATIONS[1], tm)
        _to_dilated(stage, do3_ref, DILATIONS[2], tm)
        dg_ref[:, :ATTN_W] += jnp.sum(dgr, axis=0, keepdims=True)
        dx, dgr = _rms_bwd(lru_ref[...], gg[:, ATTN_W:ATTN_W + LRU_W], dm[:, ATTN_W:ATTN_W + LRU_W])
        dlru_ref[...] = dx
        dg_ref[:, ATTN_W:ATTN_W + LRU_W] += jnp.sum(dgr, axis=0, keepdims=True)
        dx, dgr = _rms_bwd(s5_ref[...], gg[:, ATTN_W + LRU_W:], dm[:, ATTN_W + LRU_W:])
        ds5_ref[...] = dx
        dg_ref[:, ATTN_W + LRU_W:] += jnp.sum(dgr, axis=0, keepdims=True)

    a = pl.BlockSpec((tm, ATTN_W), lambda i: (i, 0))
    s5s = pl.BlockSpec((tm, S5_W), lambda i: (i, 0))
    full = pl.BlockSpec((tm, D_MODEL), lambda i: (i, 0))
    vec = pl.BlockSpec((1, D_MODEL), lambda i: (0, 0))
    dil = [_dilated_spec(tm, d, ATTN_W) for d in DILATIONS]
    dshape = [jax.ShapeDtypeStruct((s // d, d * ATTN_W), F32) for d in DILATIONS]
    dep_specs, dep_ops = _dep_args(dep)
    vshape = jax.ShapeDtypeStruct((1, D_MODEL), F32)
    res = pl.pallas_call(
        body, out_shape=(jax.ShapeDtypeStruct((s, D_MODEL), F32), vshape, vshape, *dshape,
                         jax.ShapeDtypeStruct((s, LRU_W), F32), jax.ShapeDtypeStruct((s, S5_W), F32), vshape),
        grid=(s // tm,),
        in_specs=[full, full, vec, pl.BlockSpec((D_MODEL, D_MODEL), lambda i: (0, 0)), a, a, s5s, vec] + dep_specs,
        out_specs=(full, vec, vec, *dil, a, s5s, vec), scratch_shapes=[_token_scratch(tm, ATTN_W)], name=name,
        compiler_params=_cparams(("arbitrary",)))(r, dh, ln_g, w_out, o, lru, s5, g, *dep_ops)
    return res[0], res[1], res[2], res[3:6], res[6], res[7], res[8]


def _lru_gate_math(xc, pre_r, pre_i, lam):
    r = _sigmoid(pre_r)
    i = _sigmoid(pre_i)
    log_a = -LRU_C * r * _softplus(-lam)
    a = jnp.exp(log_a)
    u = jnp.sqrt(-_expm1(2.0 * log_a)) * (i * xc)
    return a, u


def _lru_conv(x, prev8, cw, cb):
    y = cb + cw[LRU_CONV - 1:LRU_CONV, :] * x
    for k in range(LRU_CONV - 1):
        y = y + cw[k:k + 1, :] * _shift_down_prev(x, LRU_CONV - 1 - k, prev8)
    return y


def _lru_specs(s):
    xo = 3 * ATTN_W // LANE
    go = xo + LRU_W // LANE
    xr = pl.BlockSpec((s, LANE), lambda j: (0, xo + j))
    gt = pl.BlockSpec((s, LANE), lambda j: (0, go + j))
    cw = pl.BlockSpec((LRU_CONV, LANE), lambda j: (0, j))
    vec = pl.BlockSpec((1, LANE), lambda j: (0, j))
    wbd = pl.BlockSpec((LANE, LANE), lambda j: (j, j))
    col = pl.BlockSpec((s, LANE), lambda j: (0, j))
    return xr, gt, cw, vec, wbd, col


def _lru_fwd(proj, cw, cb, wr, br, wi, bi, lam, name):
    s = proj.shape[0]
    t = SCAN_T

    def body(xr_ref, gt_ref, cw_ref, cb_ref, wr_ref, br_ref, wi_ref, bi_ref, lam_ref, o_ref, xc_ref, a_ref, h_ref):
        cwv, cbv, lamv = cw_ref[...], cb_ref[...], lam_ref[...]
        wrv, wiv, brv, biv = wr_ref[...], wi_ref[...], br_ref[...], bi_ref[...]

        def chunk(c, carry):
            h_c, prev8 = carry
            rows = pl.ds(pl.multiple_of(c * t, t), t)
            x = xr_ref[rows, :]
            xc = _lru_conv(x, prev8, cwv, cbv)
            a, u = _lru_gate_math(xc, _dot(xc, wrv) + brv, _dot(xc, wiv) + biv, lamv)
            h = _scan_chunk(a, u, h_c)
            xc_ref[rows, :] = xc
            a_ref[rows, :] = a
            h_ref[rows, :] = h
            o_ref[rows, :] = h * _gelu(gt_ref[rows, :])
            return h[t - 1:t, :], x[t - 8:t, :]

        lax.fori_loop(0, s // t, chunk, (jnp.zeros((1, LANE), F32), jnp.zeros((8, LANE), F32)))

    xr, gt, cws, vec, wbd, col = _lru_specs(s)
    shp = jax.ShapeDtypeStruct((s, LRU_W), F32)
    return pl.pallas_call(
        body, out_shape=(shp,) * 4, grid=(LRU_W // LANE,),
        in_specs=[xr, gt, cws, vec, wbd, vec, wbd, vec, vec], out_specs=(col,) * 4, name=name,
        compiler_params=_cparams(("parallel",)))(proj, proj, cw, cb, wr, br, wi, bi, lam)


def _lru_bwd(proj, dout, xc_all, a_all, h_all, cw, cb, wr, br, wi, bi, lam, name):
    s = proj.shape[0]
    t = SCAN_T
    nc = s // t

    def body(xr_ref, gt_ref, do_ref, xc_s, a_s, h_s, cw_ref, cb_ref, wr_ref, br_ref, wi_ref, bi_ref, lam_ref,
             dxr_ref, dgt_ref, dcw_ref, dcb_ref, dwr_ref, dbr_ref, dwi_ref, dbi_ref, dlam_ref):
        cwv, cbv, lamv = cw_ref[...], cb_ref[...], lam_ref[...]
        wrv, wiv, brv, biv = wr_ref[...], wi_ref[...], br_ref[...], bi_ref[...]
        z1 = jnp.zeros((1, LANE), F32)
        zw = jnp.zeros((LANE, LANE), F32)

        def bchunk(ci, carry):
            g_next, a_next, dxc_next8, dcw, dcb, dwr, dbr, dwi, dbi, dlam = carry
            c = nc - 1 - ci
            t0 = pl.multiple_of(c * t, t)
            rows = pl.ds(t0, t)
            before = pl.ds(pl.multiple_of(jnp.maximum(t0 - 8, 0), 8), 8)
            has_prev = (c > 0).astype(F32)
            x, gt, do = xr_ref[rows, :], gt_ref[rows, :], do_ref[rows, :]
            xc, a, h = xc_s[rows, :], a_s[rows, :], h_s[rows, :]
            prev8_h = h_s[before, :] * has_prev
            dgt_ref[rows, :] = do * h * _gelu_grad(gt)
            dh = do * _gelu(gt)
            a_plus = _shift_up_next(a, 1, jnp.broadcast_to(a_next, (8, LANE)))
            g = _scan_chunk(a_plus, dh, g_next, reverse=True)
            da = g * _shift_down_prev(h, 1, prev8_h)
            pre_r = _dot(xc, wrv) + brv
            pre_i = _dot(xc, wiv) + biv
            _, vjp = jax.vjp(_lru_gate_math, xc, pre_r, pre_i, lamv)
            dxc, dpre_r, dpre_i, dlam_c = vjp((da, g))
            dxc = dxc + _dot_nt(dpre_r, wrv) + _dot_nt(dpre_i, wiv)
            dx = cwv[LRU_CONV - 1:LRU_CONV, :] * dxc
            dcw_rows = [None] * LRU_CONV
            dcw_rows[LRU_CONV - 1] = jnp.sum(dxc * x, axis=0, keepdims=True)
            for k in range(LRU_CONV - 1):
                dxc_ahead = _shift_up_next(dxc, LRU_CONV - 1 - k, dxc_next8)
                dx = dx + cwv[k:k + 1, :] * dxc_ahead
                dcw_rows[k] = jnp.sum(dxc_ahead * x, axis=0, keepdims=True)
            dxr_ref[rows, :] = dx
            return (g[0:1, :], a[0:1, :], dxc[0:8, :],
                    dcw + jnp.concatenate(dcw_rows, axis=0),
                    dcb + jnp.sum(dxc, axis=0, keepdims=True),
                    dwr + _dot_tn(xc, dpre_r), dbr + jnp.sum(dpre_r, axis=0, keepdims=True),
                    dwi + _dot_tn(xc, dpre_i), dbi + jnp.sum(dpre_i, axis=0, keepdims=True),
                    dlam + dlam_c)

        init = (z1, z1, jnp.zeros((8, LANE), F32), jnp.zeros((LRU_CONV, LANE), F32), z1, zw, z1, zw, z1, z1)
        res = lax.fori_loop(0, nc, bchunk, init)
        dcw_ref[...] = res[3]
        dcb_ref[...] = res[4]
        dwr_ref[...] = res[5]
        dbr_ref[...] = res[6]
        dwi_ref[...] = res[7]
        dbi_ref[...] = res[8]
        dlam_ref[...] = res[9]

    xr, gt, cws, vec, wbd, col = _lru_specs(s)
    vshape = jax.ShapeDtypeStruct((1, LRU_W), F32)
    wshape = jax.ShapeDtypeStruct((LRU_W, LRU_W), F32)
    return pl.pallas_call(
        body,
        out_shape=(jax.ShapeDtypeStruct((s, LRU_W), F32), jax.ShapeDtypeStruct((s, LRU_W), F32),
                   jax.ShapeDtypeStruct((LRU_CONV, LRU_W), F32), vshape, wshape, vshape, wshape, vshape, vshape),
        grid=(LRU_W // LANE,),
        in_specs=[xr, gt, col, col, col, col, cws, vec, wbd, vec, wbd, vec, vec],
        out_specs=(col, col, cws, vec, wbd, vec, wbd, vec, vec), name=name,
        compiler_params=_cparams(("parallel",)))(proj, proj, dout, xc_all, a_all, h_all, cw, cb, wr, br, wi, bi,
                                                 lam)


def _s5_disc_math(a_re, a_im, log_step, bt_re, bt_im):
    step = jnp.exp(log_step)
    dt_re, dt_im = step * a_re, step * a_im
    mag = jnp.exp(dt_re)
    ab_re, ab_im = mag * jnp.cos(dt_im), mag * jnp.sin(dt_im)
    z_re, z_im = ab_re - 1.0, ab_im
    den = a_re * a_re + a_im * a_im
    f_re = (z_re * a_re + z_im * a_im) / den
    f_im = (z_im * a_re - z_re * a_im) / den
    bb_re = f_re * bt_re - f_im * bt_im
    bb_im = f_re * bt_im + f_im * bt_re
    return ab_re, ab_im, bb_re, bb_im


def _s5_disc_fwd(a_re, a_im, log_step, bt_re, bt_im, name):
    def body(ar, ai, ls, br, bi, o1, o2, o3, o4):
        r = _s5_disc_math(ar[...], ai[...], ls[...], br[...], bi[...])
        o1[...], o2[...], o3[...], o4[...] = r

    shp = jax.ShapeDtypeStruct(a_re.shape, F32)
    return pl.pallas_call(body, out_shape=(shp,) * 4, name=name)(a_re, a_im, log_step, bt_re, bt_im)


def _s5_disc_bwd(a_re, a_im, log_step, bt_re, bt_im, cts, name):
    def body(ar, ai, ls, br, bi, c1, c2, c3, c4, o1, o2, o3, o4, o5):
        _, vjp = jax.vjp(_s5_disc_math, ar[...], ai[...], ls[...], br[...], bi[...])
        r = vjp((c1[...], c2[...], c3[...], c4[...]))
        o1[...], o2[...], o3[...], o4[...], o5[...] = r

    shp = jax.ShapeDtypeStruct(a_re.shape, F32)
    return pl.pallas_call(body, out_shape=(shp,) * 5, name=name)(a_re, a_im, log_step, bt_re, bt_im, *cts)


def _s5_u_specs(s):
    uo = (3 * ATTN_W + 2 * LRU_W) // LANE
    return (pl.BlockSpec((s, LANE), lambda j: (0, uo)), pl.BlockSpec((s, LANE), lambda j: (0, uo + 1)))


def _s5_scan_fwd(proj, b_re, b_im, lam_re, lam_im, c_re, c_im, name):
    s = proj.shape[0]
    t = SCAN_T

    def body(u0_ref, u1_ref, bre_ref, bim_ref, lre_ref, lim_ref, cre_ref, cim_ref, xre_ref, xim_ref, y_ref):
        @pl.when(pl.program_id(0) == 0)
        def _():
            y_ref[...] = jnp.zeros_like(y_ref)
        lr, li = lre_ref[...], lim_ref[...]
        consts = _cscan_consts(lr, li, False)
        bre, bim, cre, cim = bre_ref[...], bim_ref[...], cre_ref[...], cim_ref[...]

        def chunk(c, carry):
            cr, ci = carry
            rows = pl.ds(pl.multiple_of(c * t, t), t)
            u = jnp.concatenate([u0_ref[rows, :], u1_ref[rows, :]], axis=1).astype(BF16)
            xr, xi = _cscan_chunk(_dot(u, bre), _dot(u, bim), consts, (cr, ci))
            xre_ref[rows, :] = xr
            xim_ref[rows, :] = xi
            y_ref[rows, :] += _dot(xr, cre) - _dot(xi, cim)
            return xr[t - 1:t, :], xi[t - 1:t, :]

        z = jnp.zeros((1, S5_BLK), F32)
        lax.fori_loop(0, s // t, chunk, (z, z))

    u0, u1 = _s5_u_specs(s)
    bsp = pl.BlockSpec((S5_W, S5_BLK), lambda j: (0, j))
    csp = pl.BlockSpec((S5_BLK, S5_W), lambda j: (j, 0))
    vec = pl.BlockSpec((1, S5_BLK), lambda j: (0, j))
    xsp = pl.BlockSpec((s, S5_BLK), lambda j: (0, j))
    ysp = pl.BlockSpec((s, S5_W), lambda j: (0, 0))
    xshape = jax.ShapeDtypeStruct((s, S5_STATES), F32)
    return pl.pallas_call(
        body, out_shape=(xshape, xshape, jax.ShapeDtypeStruct((s, S5_W), F32)),
        grid=(S5_STATES // S5_BLK,), in_specs=[u0, u1, bsp, bsp, vec, vec, csp, csp],
        out_specs=(xsp, xsp, ysp), name=name,
        compiler_params=_cparams(("arbitrary",)))(proj, proj, b_re, b_im, lam_re, lam_im, c_re, c_im)


def _s5_scan_bwd(proj, dy, du_init, x_re, x_im, b_re, b_im, lam_re, lam_im, c_re, c_im, name):
    s = proj.shape[0]
    t = SCAN_T
    nc = s // t

    def body(u0_ref, u1_ref, dy_ref, dui_ref, xre_ref, xim_ref, bre_ref, bim_ref, lre_ref, lim_ref,
             cre_ref, cim_ref, du_ref, dlr_ref, dli_ref, dbr_ref, dbi_ref, dcr_ref, dci_ref):
        @pl.when(pl.program_id(0) == 0)
        def _():
            du_ref[...] = dui_ref[...]
        mr, mi = lre_ref[...], -lim_ref[...]
        consts = _cscan_consts(mr, mi, True)
        bre, bim, cre, cim = bre_ref[...], bim_ref[...], cre_ref[...], cim_ref[...]
        dbr_ref[...] = jnp.zeros_like(dbr_ref)
        dbi_ref[...] = jnp.zeros_like(dbi_ref)
        dcr_ref[...] = jnp.zeros_like(dcr_ref)
        dci_ref[...] = jnp.zeros_like(dci_ref)

        def chunk(ci_, carry):
            gnr, gni, dlr, dli = carry
            c = nc - 1 - ci_
            t0 = pl.multiple_of(c * t, t)
            rows = pl.ds(t0, t)
            before = pl.ds(pl.multiple_of(jnp.maximum(t0 - 8, 0), 8), 8)
            has_prev = (c > 0).astype(F32)
            dyc = dy_ref[rows, :].astype(BF16)
            u = jnp.concatenate([u0_ref[rows, :], u1_ref[rows, :]], axis=1).astype(BF16)
            gr, gi = _cscan_chunk(_dot_nt(dyc, cre), -_dot_nt(dyc, cim), consts, (gnr, gni), reverse=True)
            xr, xi = xre_ref[rows, :], xim_ref[rows, :]
            xpr = _shift_down_prev(xr, 1, xre_ref[before, :] * has_prev)
            xpi = _shift_down_prev(xi, 1, xim_ref[before, :] * has_prev)
            dlr = dlr + jnp.sum(gr * xpr + gi * xpi, axis=0, keepdims=True)
            dli = dli + jnp.sum(gi * xpr - gr * xpi, axis=0, keepdims=True)
            du_ref[rows, :] += _dot_nt(gr, bre) + _dot_nt(gi, bim)
            dbr_ref[...] += _dot_tn(u, gr)
            dbi_ref[...] += _dot_tn(u, gi)
            dcr_ref[...] += _dot_tn(xr, dyc)
            dci_ref[...] -= _dot_tn(xi, dyc)
            return gr[0:1, :], gi[0:1, :], dlr, dli

        z = jnp.zeros((1, S5_BLK), F32)
        res = lax.fori_loop(0, nc, chunk, (z, z, z, z))
        dlr_ref[...] = res[2]
        dli_ref[...] = res[3]

    u0, u1 = _s5_u_specs(s)
    bsp = pl.BlockSpec((S5_W, S5_BLK), lambda j: (0, j))
    csp = pl.BlockSpec((S5_BLK, S5_W), lambda j: (j, 0))
    vec = pl.BlockSpec((1, S5_BLK), lambda j: (0, j))
    xsp = pl.BlockSpec((s, S5_BLK), lambda j: (0, j))
    ysp = pl.BlockSpec((s, S5_W), lambda j: (0, 0))
    return pl.pallas_call(
        body,
        out_shape=(jax.ShapeDtypeStruct((s, S5_W), F32),
                   jax.ShapeDtypeStruct((1, S5_STATES), F32), jax.ShapeDtypeStruct((1, S5_STATES), F32),
                   jax.ShapeDtypeStruct((S5_W, S5_STATES), F32), jax.ShapeDtypeStruct((S5_W, S5_STATES), F32),
                   jax.ShapeDtypeStruct((S5_STATES, S5_W), F32), jax.ShapeDtypeStruct((S5_STATES, S5_W), F32)),
        grid=(S5_STATES // S5_BLK,),
        in_specs=[u0, u1, ysp, ysp, xsp, xsp, bsp, bsp, vec, vec, csp, csp],
        out_specs=(ysp, vec, vec, bsp, bsp, csp, csp), name=name,
        compiler_params=_cparams(("arbitrary",)))(
            proj, proj, dy, du_init, x_re, x_im, b_re, b_im, lam_re, lam_im, c_re, c_im)


def _s5_out_fwd(proj, y_acc, dvec, w_glu, b_glu, name):
    s = proj.shape[0]
    tm = 512
    uo = (3 * ATTN_W + 2 * LRU_W) // LANE

    def body(u0_ref, u1_ref, y_ref, d_ref, w_ref, b_ref, o_ref, yp_ref):
        u = jnp.concatenate([u0_ref[...], u1_ref[...]], axis=1)
        y = y_ref[...] + d_ref[...] * u
        yp_ref[...] = y
        yg = _gelu(y)
        o_ref[...] = yg * _sigmoid(_dot(yg, w_ref[...]) + b_ref[...])

    u0 = pl.BlockSpec((tm, LANE), lambda i: (i, uo))
    u1 = pl.BlockSpec((tm, LANE), lambda i: (i, uo + 1))
    row = pl.BlockSpec((tm, S5_W), lambda i: (i, 0))
    vec = pl.BlockSpec((1, S5_W), lambda i: (0, 0))
    wsp = pl.BlockSpec((S5_W, S5_W), lambda i: (0, 0))
    shp = jax.ShapeDtypeStruct((s, S5_W), F32)
    return pl.pallas_call(
        body, out_shape=(shp, shp), grid=(s // tm,), in_specs=[u0, u1, row, vec, wsp, vec],
        out_specs=(row, row), name=name,
        compiler_params=_cparams(("parallel",)))(proj, proj, y_acc, dvec, w_glu, b_glu)


def _s5_out_bwd(proj, y_pre, dout, dvec, w_glu, b_glu, name, dep=None):
    s = proj.shape[0]
    tm = 512
    uo = (3 * ATTN_W + 2 * LRU_W) // LANE

    def body(u0_ref, u1_ref, y_ref, do_ref, d_ref, w_ref, b_ref, *rest):
        dy_ref, dud_ref, dd_ref, dw_ref, db_ref = rest[-5:]

        @pl.when(pl.program_id(0) == 0)
        def _():
            dd_ref[...] = jnp.zeros_like(dd_ref)
            dw_ref[...] = jnp.zeros_like(dw_ref)
            db_ref[...] = jnp.zeros_like(db_ref)
        u = jnp.concatenate([u0_ref[...], u1_ref[...]], axis=1)
        y = y_ref[...]
        do = do_ref[...]
        yg = _gelu(y)
        sg = _sigmoid(_dot(yg, w_ref[...]) + b_ref[...])
        dz = do * yg * sg * (1.0 - sg)
        dyg = do * sg + _dot_nt(dz, w_ref[...])
        dy = dyg * _gelu_grad(y)
        dy_ref[...] = dy
        dud_ref[...] = d_ref[...] * dy
        dd_ref[...] += jnp.sum(dy * u, axis=0, keepdims=True)
        dw_ref[...] += _dot_tn(yg, dz)
        db_ref[...] += jnp.sum(dz, axis=0, keepdims=True)

    u0 = pl.BlockSpec((tm, LANE), lambda i: (i, uo))
    u1 = pl.BlockSpec((tm, LANE), lambda i: (i, uo + 1))
    row = pl.BlockSpec((tm, S5_W), lambda i: (i, 0))
    vec = pl.BlockSpec((1, S5_W), lambda i: (0, 0))
    wsp = pl.BlockSpec((S5_W, S5_W), lambda i: (0, 0))
    shp = jax.ShapeDtypeStruct((s, S5_W), F32)
    vshape = jax.ShapeDtypeStruct((1, S5_W), F32)
    dep_specs, dep_ops = _dep_args(dep)
    return pl.pallas_call(
        body, out_shape=(shp, shp, vshape, jax.ShapeDtypeStruct((S5_W, S5_W), F32), vshape),
        grid=(s // tm,), in_specs=[u0, u1, row, row, vec, wsp, vec] + dep_specs,
        out_specs=(row, row, vec, wsp, vec), name=name,
        compiler_params=_cparams(("arbitrary",)))(proj, proj, y_pre, dout, dvec, w_glu, b_glu, *dep_ops)


def _ffn_conv(x, prev8, cw, cb):
    y = cb + cw[FFN_CONV - 1:FFN_CONV, :] * x
    for k in range(FFN_CONV - 1):
        y = y + cw[k:k + 1, :] * _shift_down_prev(x, FFN_CONV - 1 - k, prev8)
    return y


def _ffn_up_act(h, wg, cw, cb, name, dep=None):
    s, d = h.shape
    tm = 512
    tb = 2 * FFN_CB
    nt = D_FF // FFN_CB

    def body(h_ref, wgate_ref, wval_ref, cw_ref, cb_ref, *rest):
        up_ref, y_ref, o_ref, ot_ref, carry = rest[-5:]

        @pl.when(pl.program_id(1) == 0)
        def _():
            carry[...] = jnp.zeros_like(carry)
        hb = h_ref[...].astype(BF16)
        x = jnp.concatenate([_dot(hb, wgate_ref[...]), _dot(hb, wval_ref[...])], axis=1)
        up_ref[...] = x.astype(BF16)
        y = _ffn_conv(x, carry[...], cw_ref[...], cb_ref[...])
        y_ref[...] = y
        carry[...] = x[tm - 8:tm, :]
        act = _gelu(y[:, :FFN_CB]) * y[:, FFN_CB:]
        o_ref[...] = act.astype(BF16)
        ot_ref[...] = act.T.astype(BF16)

    dep_specs, dep_ops = _dep_args(dep)
    return pl.pallas_call(
        body, out_shape=(jax.ShapeDtypeStruct((s, 2 * D_FF), BF16), jax.ShapeDtypeStruct((s, 2 * D_FF), F32),
                         jax.ShapeDtypeStruct((s, D_FF), BF16), jax.ShapeDtypeStruct((D_FF, s), BF16)),
        grid=(nt, s // tm),
        in_specs=[pl.BlockSpec((tm, d), lambda t, i: (i, 0)),
                  pl.BlockSpec((None, d, FFN_CB), lambda t, i: (t, 0, 0)),
                  pl.BlockSpec((None, d, FFN_CB), lambda t, i: (t + nt, 0, 0)),
                  pl.BlockSpec((FFN_CONV, tb), lambda t, i: (0, t)),
                  pl.BlockSpec((1, tb), lambda t, i: (0, t))] + dep_specs,
        out_specs=(pl.BlockSpec((tm, tb), lambda t, i: (i, t)), pl.BlockSpec((tm, tb), lambda t, i: (i, t)),
                   pl.BlockSpec((tm, FFN_CB), lambda t, i: (i, t)), pl.BlockSpec((FFN_CB, tm), lambda t, i: (t, i))),
        scratch_shapes=[pltpu.VMEM((8, tb), F32)], name=name,
        compiler_params=_cparams(("parallel", "arbitrary")))(h, wg, wg, cw, cb, *dep_ops)


def _ffn_bwd(up, y_conv, dr, w_down, wg, cw, name):
    s = up.shape[0]
    d = dr.shape[1]
    tm = 512
    tb = 2 * FFN_CB
    nr = s // tm
    nt = D_FF // FFN_CB

    def body(x_ref, y_ref, dr_ref, wd_ref, wgate_ref, wval_ref, cw_ref,
             dup_ref, dh_ref, dcw_ref, dcb_ref, carry):
        i, t = pl.program_id(0), pl.program_id(1)

        @pl.when(i == 0)
        def _():
            carry[t] = jnp.zeros((8, tb), F32)

        @pl.when(t == 0)
        def _():
            dh_ref[...] = ALPHA * dr_ref[...]
        cwv = cw_ref[...]
        x = x_ref[...]
        dact = _dot_nt(dr_ref[...], wd_ref[...])
        gate, val = y_ref[:, :FFN_CB], y_ref[:, FFN_CB:]
        dy = jnp.concatenate([dact * val * _gelu_grad(gate), dact * _gelu(gate)], axis=1)
        next8 = carry[t]
        carry[t] = dy[0:8, :]
        dx = cwv[FFN_CONV - 1:FFN_CONV, :] * dy
        dcw_rows = [None] * FFN_CONV
        dcw_rows[FFN_CONV - 1] = jnp.sum(dy * x, axis=0, keepdims=True)
        for k in range(FFN_CONV - 1):
            dy_ahead = _shift_up_next(dy, FFN_CONV - 1 - k, next8)
            dx = dx + cwv[k:k + 1, :] * dy_ahead
            dcw_rows[k] = jnp.sum(dy_ahead * x, axis=0, keepdims=True)
        dup = dx.astype(BF16)
        dup_ref[...] = dup
        dh_ref[...] += _dot_nt(dup[:, :FFN_CB], wgate_ref[...]) + _dot_nt(dup[:, FFN_CB:], wval_ref[...])
        dcw_ref[...] = jnp.concatenate(dcw_rows, axis=0)
        dcb_ref[...] = jnp.sum(dy, axis=0, keepdims=True)

    row = lambda i: nr - 1 - i
    return pl.pallas_call(
        body, out_shape=(jax.ShapeDtypeStruct((s, 2 * D_FF), BF16), jax.ShapeDtypeStruct((s, d), F32),
                         jax.ShapeDtypeStruct((nr, FFN_CONV, 2 * D_FF), F32),
                         jax.ShapeDtypeStruct((nr, 1, 2 * D_FF), F32)),
        grid=(nr, nt),
        in_specs=[pl.BlockSpec((tm, tb), lambda i, t: (row(i), t)),
                  pl.BlockSpec((tm, tb), lambda i, t: (row(i), t)),
                  pl.BlockSpec((tm, d), lambda i, t: (row(i), 0)),
                  pl.BlockSpec((FFN_CB, d), lambda i, t: (t, 0)),
                  pl.BlockSpec((None, d, FFN_CB), lambda i, t: (t, 0, 0)),
                  pl.BlockSpec((None, d, FFN_CB), lambda i, t: (t + nt, 0, 0)),
                  pl.BlockSpec((FFN_CONV, tb), lambda i, t: (0, t))],
        out_specs=(pl.BlockSpec((tm, tb), lambda i, t: (row(i), t)),
                   pl.BlockSpec((tm, d), lambda i, t: (row(i), 0)),
                   pl.BlockSpec((None, FFN_CONV, tb), lambda i, t: (row(i), 0, t)),
                   pl.BlockSpec((None, 1, tb), lambda i, t: (row(i), 0, t))),
        scratch_shapes=[pltpu.VMEM((nt, 8, tb), F32)], name=name,
        compiler_params=_cparams(("arbitrary", "arbitrary")))(up, y_conv, dr, w_down, wg, wg, cw)


def _sum_partials(ld_ref):
    gg = ld_ref[0].astype(F32)
    for k in range(1, N_DEV):
        gg = gg + ld_ref[k].astype(F32)
    return gg


def _adam_update(w, g, m, v):
    mn = ADAM_B1 * m + (1.0 - ADAM_B1) * g
    vn = ADAM_B2 * v + (1.0 - ADAM_B2) * (g * g)
    m_hat = mn / (1.0 - ADAM_B1 ** ADAM_STEP)
    v_hat = vn / (1.0 - ADAM_B2 ** ADAM_STEP)
    return -ADAM_LR * (m_hat / (jnp.sqrt(v_hat) + ADAM_EPS) + ADAM_WD * w), mn, vn


def _adamw_many(landed, ws, ms, vs, name):
    n, nl = len(ws), len(landed)

    def body(*refs):
        ld = refs[:nl * n]
        w_refs, m_refs, v_refs = (refs[(nl + k) * n:(nl + k + 1) * n] for k in range(3))
        outs = refs[(nl + 3) * n:]
        for i in range(n):
            for l in range(nl):
                one = slice(l, l + 1)
                gg = _sum_partials(ld[l * n + i])
                outs[i][one] = gg
                outs[n + i][one], outs[2 * n + i][one], outs[3 * n + i][one] = _adam_update(
                    w_refs[i][one], gg, m_refs[i][one], v_refs[i][one])

    vm = pl.BlockSpec(memory_space=pltpu.VMEM)
    shapes = [jax.ShapeDtypeStruct(w.shape, F32) for w in ws] * 4
    res = pl.pallas_call(
        body, out_shape=tuple(shapes), in_specs=[vm] * ((nl + 3) * n), out_specs=tuple([vm] * (4 * n)),
        name=name, compiler_params=_cparams())(*[a for layer in landed for a in layer], *ws, *ms, *vs)
    return res[:n], res[n:2 * n], res[2 * n:3 * n], res[3 * n:]


def _adamw_sum(landed, w, m, v, layer, prev, name):
    _, r, c = landed.shape
    nl = w.shape[0]
    tm = 8
    for cand in (512, 256, 128, 64, 32, 16):
        if r % cand == 0 and N_DEV * cand * c * 4 <= 4 * 1024 * 1024:
            tm = cand
            break

    def body(*refs):
        ld_ref, w_ref, m_ref, v_ref = refs[:4]
        g_ref, d_ref, mo_ref, vo_ref = refs[-4:]
        gg = _sum_partials(ld_ref)
        g_ref[...] = gg
        d_ref[...], mo_ref[...], vo_ref[...] = _adam_update(w_ref[...], gg, m_ref[...], v_ref[...])

    blk = pl.BlockSpec((None, tm, c), lambda i: (layer, i, 0))
    in_specs = [pl.BlockSpec((N_DEV, tm, c), lambda i: (0, i, 0)), blk, blk, blk]
    args = [landed, w, m, v]
    aliases = {}
    if prev is not None:
        in_specs += [pl.BlockSpec(memory_space=pl.ANY)] * 4
        args += list(prev)
        aliases = {4 + k: k for k in range(4)}
    shp = jax.ShapeDtypeStruct((nl, r, c), F32)
    return pl.pallas_call(
        body, out_shape=(shp,) * 4, grid=(r // tm,), in_specs=in_specs, out_specs=(blk,) * 4,
        input_output_aliases=aliases, name=name, compiler_params=_cparams(("parallel",)))(*args)


def _all_gather(shards, name):
    na = len(shards)

    def body(*refs):
        x_refs, out_refs = refs[:na], refs[na:2 * na]
        send_sems, recv_sems, local_sems = refs[2 * na:]
        x, y, c = lax.axis_index("x"), lax.axis_index("y"), lax.axis_index("c")
        me, sibling = (x, y, c), (x, y, 1 - c)
        chips = [(1 - x, y), (x, 1 - y), (1 - x, 1 - y)]

        def copy(a, k, block, to, src=None):
            dst = out_refs[a].at[4 * block[0] + 2 * block[1] + block[2]]
            return pltpu.make_async_remote_copy(
                src_ref=dst if src is None else src, dst_ref=dst,
                send_sem=send_sems.at[7 * a + k], recv_sem=recv_sems.at[7 * a + k],
                device_id=to, device_id_type=pl.DeviceIdType.MESH)

        mine, first, passed = [], [], []
        for a in range(na):
            cp = pltpu.make_async_copy(x_refs[a], out_refs[a].at[4 * x + 2 * y + c], local_sems.at[a])
            cp.start()
            mine.append(cp)
            cps = [copy(a, 0, me, sibling, src=x_refs[a])]
            cps += [copy(a, 1 + j, me, (*chip, c), src=x_refs[a]) for j, chip in enumerate(chips)]
            for cp in cps:
                cp.start()
            first += cps
        for j, chip in enumerate(chips):
            for a in range(na):
                copy(a, 1 + j, (*chip, c), me).wait_recv()
                cp = copy(a, 4 + j, (*chip, c), sibling)
                cp.start()
                passed.append(cp)
        for a in range(na):
            copy(a, 0, sibling, me).wait_recv()
            for j, chip in enumerate(chips):
                copy(a, 4 + j, (*chip, 1 - c), me).wait_recv()
        for cp in first + passed:
            cp.wait_send()
        for cp in mine:
            cp.wait()

    anyspec = pl.BlockSpec(memory_space=pl.ANY)
    return pl.pallas_call(
        body, out_shape=tuple(jax.ShapeDtypeStruct((N_DEV,) + t.shape, t.dtype) for t in shards),
        in_specs=[anyspec] * na, out_specs=tuple([anyspec] * na),
        scratch_shapes=[pltpu.SemaphoreType.DMA((7 * na,)), pltpu.SemaphoreType.DMA((7 * na,)),
                        pltpu.SemaphoreType.DMA((na,))],
        name=name)(*shards)


_HBM = pl.BlockSpec(memory_space=pltpu.HBM)
_SEM = pl.BlockSpec(memory_space=pltpu.SEMAPHORE)
_EFFECT = pltpu.SideEffectType.DATAFLOW_SIDE_EFFECTING


def _exchange_copies(src_refs, land_refs, send_sems, recv_sems, local_sems, gather):
    x, y, c = lax.axis_index("x"), lax.axis_index("y"), lax.axis_index("c")
    me = 4 * x + 2 * y + c
    per_array = send_sems.shape[0] > N_DEV - 1
    local, remote = [], []
    for a, (src, land) in enumerate(zip(src_refs, land_refs)):
        local.append(pltpu.make_async_copy(src if gather else src.at[me], land.at[me],
                                           local_sems.at[a if per_array else 0]))
    for k in range(1, N_DEV):
        px = x ^ ((k >> 2) & 1)
        py = y ^ ((k >> 1) & 1)
        pc = c ^ (k & 1)
        for a, (src, land) in enumerate(zip(src_refs, land_refs)):
            remote.append(pltpu.make_async_remote_copy(
                src_ref=src if gather else src.at[4 * px + 2 * py + pc], dst_ref=land.at[me],
                send_sem=send_sems.at[(7 * a if per_array else 0) + k - 1],
                recv_sem=recv_sems.at[(7 * a if per_array else 0) + k - 1],
                device_id=(px, py, pc), device_id_type=pl.DeviceIdType.MESH))
    return local, remote


def _exchange_start(srcs, gather, name, dep=None):
    na = len(srcs)
    ns = na if na <= 4 else 1
    lands = [lax.empty(((N_DEV,) + t.shape) if gather else t.shape, t.dtype) for t in srcs]

    def body(*refs):
        src_refs, land_refs = refs[:na], refs[na:2 * na]
        nin = 2 * na + (0 if dep is None else 1)
        send_sems, recv_sems, local_sems = refs[nin:nin + 3]
        token = refs[-1]
        local, remote = _exchange_copies(src_refs, land_refs, send_sems, recv_sems, local_sems, gather)
        for cp in local + remote:
            cp.start()
        token[...] = jnp.zeros_like(token)

    dep_specs, dep_ops = _dep_args(dep)
    hbm = lambda t: pltpu.HBM(t.shape, t.dtype)
    out = pl.pallas_call(
        body, name=name,
        out_shape=(pltpu.SemaphoreType.DMA((7 * ns,)), pltpu.SemaphoreType.DMA((7 * ns,)),
                   pltpu.SemaphoreType.DMA((ns,)), *[hbm(t) for t in srcs], *[hbm(t) for t in lands],
                   jax.ShapeDtypeStruct((8, LANE), F32)),
        in_specs=[_HBM] * (2 * na) + dep_specs,
        out_specs=(_SEM, _SEM, _SEM, *[_HBM] * (2 * na), pl.BlockSpec(memory_space=pltpu.VMEM)),
        input_output_aliases={i: 3 + i for i in range(2 * na)},
        compiler_params=pltpu.CompilerParams(has_side_effects=_EFFECT),
    )(*[pltpu.with_memory_space_constraint(t, pltpu.HBM) for t in srcs + lands], *dep_ops)
    return (out[:3], out[3:3 + na], out[3 + na:3 + 2 * na]), out[-1]


def _exchange_wait(handle, gather, after, name):
    sems, srcs, lands = handle
    na = len(srcs)

    def body(*refs):
        src_refs, land_refs = refs[:na], refs[na:2 * na]
        send_sems, recv_sems, local_sems = refs[2 * na:2 * na + 3]
        local, remote = _exchange_copies(src_refs, land_refs, send_sems, recv_sems, local_sems, gather)
        for cp in remote:
            cp.wait_send()
            cp.wait_recv()
        for cp in local:
            cp.wait()

    hbm = lambda t: pltpu.HBM(t.shape, t.dtype)
    out = pl.pallas_call(
        body, name=name, out_shape=(*[hbm(t) for t in srcs], *[hbm(t) for t in lands]),
        in_specs=[_HBM] * (2 * na) + [_SEM] * 3 + [pl.BlockSpec(memory_space=pl.ANY)],
        out_specs=tuple([_HBM] * (2 * na)), input_output_aliases={i: i for i in range(2 * na)},
        compiler_params=pltpu.CompilerParams(has_side_effects=_EFFECT),
    )(*srcs, *lands, *sems, after)
    return out[na:]


def _block_diag(w):
    h, a, b = w.shape
    eye = jnp.eye(h, dtype=w.dtype)
    return (w[:, :, None, :] * eye[:, None, :, None]).reshape(h * a, h * b)


def _block_diag_extract(m, h):
    a, b = m.shape[0] // h, m.shape[1] // h
    return jnp.stack([m[i * a:(i + 1) * a, i * b:(i + 1) * b] for i in range(h)], axis=0)


def _block_diag_take(m, h):
    a, b = m.shape[0] // h, m.shape[1] // h
    eye = jnp.eye(h, dtype=m.dtype)
    return (m.reshape(h, a, h, b) * eye[:, None, :, None]).sum(axis=2)


def _ffn_interleave(w):
    lead = w.shape[:-1]
    nb = D_FF // FFN_CB
    return jnp.swapaxes(w.reshape(*lead, 2, nb, FFN_CB), -3, -2).reshape(*lead, 2 * D_FF)


def _ffn_deinterleave(w):
    lead = w.shape[:-1]
    nb = D_FF // FFN_CB
    return jnp.swapaxes(w.reshape(*lead, nb, 2, FFN_CB), -3, -2).reshape(*lead, 2 * D_FF)


def _gather_full(gathered, axis):
    shape = list(gathered.shape[1:])
    shape[axis] *= N_DEV
    return jnp.moveaxis(gathered, 0, axis).reshape(shape)


def _scatter_blocks(full, axis):
    shape = list(full.shape)
    shape[axis:axis + 1] = [N_DEV, shape[axis] // N_DEV]
    return jnp.moveaxis(full.reshape(shape), axis, 0)


def _pad_to(flat, mult):
    pad = (-flat.shape[-1]) % mult
    if pad:
        flat = jnp.concatenate([flat, jnp.zeros(flat.shape[:-1] + (pad,), flat.dtype)], axis=-1)
    return flat


def _layer_fwd(h_in, h_in_t, w, cos, sin, l, dep, get_ffn, target=None):
    tag = "l%d_" % l
    proj, qkv, h_t = _proj_rope(h_in, w['w_in'], cos, sin, tag + "proj_rope", dep=dep,
                                transposed=h_in_t is None)
    h_in_t = h_t if h_in_t is None else h_in_t
    outs, lses = [], []
    for d, qv in zip(DILATIONS, qkv):
        o, ls = _attn_fwd(qv, d, tag + "attn_d%d" % d)
        outs.append(o)
        lses.append(ls)
    lru, *lru_saved = _lru_fwd(proj, w['lru_conv_w'], w['lru_conv_b'], w['lru_wr'], w['lru_br'], w['lru_wi'],
                               w['lru_bi'], w['lru_lambda'], tag + "lru")
    x_re, x_im, y_acc = _s5_scan_fwd(proj, w['s5_bb_re'], w['s5_bb_im'], w['s5_lam_re'], w['s5_lam_im'],
                                     w['s5_cc_re'], w['s5_cc_im'], tag + "s5_scan")
    s5, y_pre = _s5_out_fwd(proj, y_acc, w['s5_d'], w['s5_w_glu'], w['s5_b_glu'], tag + "s5_out")
    w_out = get_ffn(l, s5, 'out')
    if w_out is not None:
        w['w_out'] = w_out
    mixed_t, r1, h1, h1_t, attn_o, attn_lse = _mix_fwd(outs, lses, lru, s5, w['mix_norm_g'], h_in, w['w_out'],
                                                       w['ln1_g'], w['ln1_b'], tag + "mix_out_ln1")
    w['w_up_g'], w['w_down'], ffn_dep = get_ffn(l, h1, 'ffn')
    up, y_conv, act, act_t = _ffn_up_act(h1, w['w_up_g'], w['ffn_conv_w'], w['ffn_conv_b'], tag + "up_act",
                                         dep=ffn_dep)
    r2, out_a, out_b = _proj_ln(act, w['w_down'], h1, w['ln2_g'], w['ln2_b'], tag + "down_ln2", target=target)
    saved = dict(h_in_t=h_in_t, proj=proj, qkv=qkv, lru=lru, lru_saved=lru_saved, x_re=x_re, x_im=x_im,
                 y_pre=y_pre, s5=s5, mixed_t=mixed_t, attn_o=attn_o, attn_lse=attn_lse, r1=r1, h1_t=h1_t, up=up,
                 act_t=act_t, r2=r2, y_conv=y_conv)
    return out_a, out_b, saved


def _layer_bwd_ffn(dh2, sv, w, l, dep=None):
    tag = "l%d_" % l
    g = {}
    dr2, g['ln2_g'], g['ln2_b'] = _ln_bwd(sv['r2'], dh2, w['ln2_g'], tag + "ln2_bwd", dep=dep)
    g['w_down'] = _mm_dw(sv['act_t'], dr2, 1024, D_MODEL, 1024, tag + "down_dw", _grad_dtype(l))
    dup, dh1, dcw_parts, dcb_parts = _ffn_bwd(sv['up'], sv['y_conv'], dr2, w['w_down'], w['w_up_g'],
                                              w['ffn_conv_w'], tag + "ffn_bwd")
    g['ffn_conv_w'] = dcw_parts.sum(axis=0)
    g['ffn_conv_b'] = dcb_parts.sum(axis=0)
    g['w_up_g'] = _mm_up_dw(sv['h1_t'], dup, tag + "up_dw", _grad_dtype(l))
    return dh1, g


def _layer_bwd_mix(dh1, sv, w, cos, sin, l, dep, g_ffn, after_out_grad, after_small_grads, after_in_grad):
    tag = "l%d_" % l
    g = {}
    dr1, g['ln1_g'], g['ln1_b'], d_o, dlru, ds5, g['mix_norm_g'] = _mix_bwd(
        sv['r1'], dh1, w['ln1_g'], w['w_out'], sv['attn_o'][0], sv['lru'], sv['s5'], w['mix_norm_g'],
        tag + "ln1_mix_bwd", dep=dep)
    g['w_out'] = _mm_dw(sv['mixed_t'], dr1, 1024, D_MODEL, 1024, tag + "out_dw", _grad_dtype(l))
    dy, dud, g['s5_d'], g['s5_w_glu'], g['s5_b_glu'] = _s5_out_bwd(
        sv['proj'], sv['y_pre'], ds5, w['s5_d'], w['s5_w_glu'], w['s5_b_glu'], tag + "s5_out_bwd",
        dep=after_out_grad(l, g['w_out']))
    du, g['s5_lam_re'], g['s5_lam_im'], g['s5_bb_re'], g['s5_bb_im'], g['s5_cc_re'], g['s5_cc_im'] = \
        _s5_scan_bwd(sv['proj'], dy, dud, sv['x_re'], sv['x_im'], w['s5_bb_re'], w['s5_bb_im'],
                     w['s5_lam_re'], w['s5_lam_im'], w['s5_cc_re'], w['s5_cc_im'], tag + "s5_scan_bwd")
    (dxr, dgate, g['lru_conv_w'], g['lru_conv_b'], g['lru_wr'], g['lru_br'], g['lru_wi'], g['lru_bi'],
     g['lru_lambda']) = _lru_bwd(sv['proj'], dlru, *sv['lru_saved'], w['lru_conv_w'], w['lru_conv_b'], w['lru_wr'],
                                 w['lru_br'], w['lru_wi'], w['lru_bi'], w['lru_lambda'], tag + "lru_bwd")
    token = after_small_grads(l, _finish_layer_grads({**g_ffn, **g}, w, l))
    dqkv = [_attn_bwd(sv['qkv'][b], sv['attn_o'][b], d_o[b], sv['attn_lse'][b], d, tag + "attn_bwd_d%d" % d,
                      dep=token if b == 0 else None)
            for b, d in enumerate(DILATIONS)]
    dproj = _dproj_assemble(dqkv, dxr, dgate, du, cos, sin, tag + "dproj")
    g_in = _mm_dw(sv['h_in_t'], dproj, 1024, D_IN, 1024, tag + "in_dw", _grad_dtype(l))
    return _mm_nt(dproj, w['w_in'], 512, D_MODEL, tag + "in_dx", add=dr1, add_scale=ALPHA,
                  dep=after_in_grad(l, g_in))


def _s5_rep(a):
    return jnp.repeat(a, S5_C, axis=0)


def _prepare_layer(p, l):
    w = {}
    for n in ('w_in', 'w_out', 's5_w_glu'):
        if n in p:
            w[n] = p[n].astype(BF16)
    w['ffn_conv_w'] = _ffn_interleave(p['ffn_conv_w'])
    w['ffn_conv_b'] = _ffn_interleave(p['ffn_conv_b'])[None, :]
    w['lru_conv_w'] = p['lru_conv_w']
    for n in ('lru_conv_b', 'lru_br', 'lru_bi', 'lru_lambda', 's5_b_glu', 'mix_norm_g',
              'ln1_g', 'ln1_b', 'ln2_g', 'ln2_b'):
        w[n] = p[n][None, :]
    w['lru_wr'] = _block_diag(p['lru_wr']).astype(BF16)
    w['lru_wi'] = _block_diag(p['lru_wi']).astype(BF16)
    w['s5_d'] = p['s5_d'].reshape(1, S5_W)
    disc_in = (_s5_rep(p['s5_a_re']), _s5_rep(p['s5_a_im']),
               _s5_rep(jnp.broadcast_to(p['s5_log_step'][:, None], (S5_G, S5_P))),
               jnp.swapaxes(p['s5_b_re'], 1, 2).reshape(S5_W, S5_P),
               jnp.swapaxes(p['s5_b_im'], 1, 2).reshape(S5_W, S5_P))
    ab_re, ab_im, bb_re, bb_im = _s5_disc_fwd(*disc_in, "l%d_s5_disc" % l)
    w['s5_disc_in'] = disc_in
    w['s5_lam_re'] = ab_re.reshape(S5_G, S5_C, S5_P)[:, 0, :].reshape(1, S5_STATES)
    w['s5_lam_im'] = ab_im.reshape(S5_G, S5_C, S5_P)[:, 0, :].reshape(1, S5_STATES)
    w['s5_bb_re'] = _block_diag(bb_re.reshape(S5_G, S5_C, S5_P)).astype(BF16)
    w['s5_bb_im'] = _block_diag(bb_im.reshape(S5_G, S5_C, S5_P)).astype(BF16)
    w['s5_cc_re'] = _block_diag(jnp.swapaxes(p['s5_c_re'], 1, 2)).astype(BF16)
    w['s5_cc_im'] = _block_diag(jnp.swapaxes(p['s5_c_im'], 1, 2)).astype(BF16)
    return w


def _finish_layer_grads(g, w, l):
    out = {}
    for n in ('s5_w_glu', 'lru_conv_w'):
        out[n] = g[n]
    out['ffn_conv_w'] = _ffn_deinterleave(g['ffn_conv_w'])
    out['ffn_conv_b'] = _ffn_deinterleave(g['ffn_conv_b'])[0]
    for n in ('lru_conv_b', 'lru_br', 'lru_bi', 'lru_lambda', 's5_b_glu', 'mix_norm_g',
              'ln1_g', 'ln1_b', 'ln2_g', 'ln2_b'):
        out[n] = g[n][0]
    out['lru_wr'] = _block_diag_extract(g['lru_wr'], LRU_W // HEAD)
    out['lru_wi'] = _block_diag_extract(g['lru_wi'], LRU_W // HEAD)
    out['s5_d'] = g['s5_d'].reshape(S5_G, S5_C)
    out['s5_c_re'] = jnp.swapaxes(_block_diag_take(g['s5_cc_re'], S5_G), 1, 2)
    out['s5_c_im'] = jnp.swapaxes(_block_diag_take(g['s5_cc_im'], S5_G), 1, 2)
    rep = lambda v: _s5_rep(v.reshape(S5_G, S5_P)) * (1.0 / S5_C)
    cts = (rep(g['s5_lam_re']), rep(g['s5_lam_im']),
           _block_diag_take(g['s5_bb_re'], S5_G).reshape(S5_W, S5_P),
           _block_diag_take(g['s5_bb_im'], S5_G).reshape(S5_W, S5_P))
    da_re, da_im, dls, dbt_re, dbt_im = _s5_disc_bwd(*w['s5_disc_in'], cts, "l%d_s5_disc_bwd" % l)
    out['s5_a_re'] = da_re.reshape(S5_G, S5_C, S5_P).sum(axis=1)
    out['s5_a_im'] = da_im.reshape(S5_G, S5_C, S5_P).sum(axis=1)
    out['s5_log_step'] = dls.reshape(S5_G, S5_C * S5_P).sum(axis=1)
    out['s5_b_re'] = jnp.swapaxes(dbt_re.reshape(S5_G, S5_C, S5_P), 1, 2)
    out['s5_b_im'] = jnp.swapaxes(dbt_im.reshape(S5_G, S5_C, S5_P), 1, 2)
    return out


def _run_step(x, target, get_layer, get_ffn, on_loss, after_ffn_grads, after_out_grad, after_small_grads,
              after_in_grad):
    cos, sin = _rope_tables(x.shape[0])
    h, h_t = x, None
    ws, saved = [], []
    for l in range(DEPTH):
        p, dep = get_layer(l, h)
        ws.append(_prepare_layer(p, l))
        h, h_t, sv = _layer_fwd(h, h_t, ws[l], cos, sin, l, dep, get_ffn, target if l == DEPTH - 1 else None)
        saved.append(sv)
    dh, loss_vec = h, h_t
    on_loss(loss_vec)
    dep = None
    for l in reversed(range(DEPTH)):
        dh1, g = _layer_bwd_ffn(dh, saved[l], ws[l], l, dep)
        dep = after_ffn_grads(l, g)
        dh = _layer_bwd_mix(dh1, saved[l], ws[l], cos, sin, l, dep, g, after_out_grad, after_small_grads,
                            after_in_grad)
        dep = None
    return loss_vec, dh


def _local_step(x, target, layers):
    grads = [{} for _ in range(DEPTH)]

    def ffn(l, after, part):
        if part == 'out':
            return None
        return layers[l]['w_up_g'].astype(BF16), layers[l]['w_down'].astype(BF16), None

    def keep_ffn(l, g):
        grads[l].update(w_up_g=g['w_up_g'], w_down=g['w_down'])

    def keep_small(l, g):
        grads[l].update(g)

    loss, dx = _run_step(x, target, lambda l, h: (layers[l], None), ffn, lambda row: None, keep_ffn,
                         lambda l, g: grads[l].update(w_out=g), keep_small, lambda l, g: grads[l].update(w_in=g))
    return loss[0, 0], dx, grads


def kernel(x, w_in, lru_conv_w, lru_conv_b, lru_wr, lru_br, lru_wi, lru_bi, lru_lambda, s5_a_re, s5_a_im, s5_b_re, s5_b_im, s5_c_re, s5_c_im, s5_d, s5_log_step, s5_w_glu, s5_b_glu, mix_norm_g, w_out, ln1_g, ln1_b, w_up, ffn_conv_w, ffn_conv_b, w_down, ln2_g, ln2_b, loss_target, m_w_in, m_lru_conv_w, m_lru_conv_b, m_lru_wr, m_lru_br, m_lru_wi, m_lru_bi, m_lru_lambda, m_s5_a_re, m_s5_a_im, m_s5_b_re, m_s5_b_im, m_s5_c_re, m_s5_c_im, m_s5_d, m_s5_log_step, m_s5_w_glu, m_s5_b_glu, m_mix_norm_g, m_w_out, m_ln1_g, m_ln1_b, m_w_up, m_ffn_conv_w, m_ffn_conv_b, m_w_down, m_ln2_g, m_ln2_b, v_w_in, v_lru_conv_w, v_lru_conv_b, v_lru_wr, v_lru_br, v_lru_wi, v_lru_bi, v_lru_lambda, v_s5_a_re, v_s5_a_im, v_s5_b_re, v_s5_b_im, v_s5_c_re, v_s5_c_im, v_s5_d, v_s5_log_step, v_s5_w_glu, v_s5_b_glu, v_mix_norm_g, v_w_out, v_ln1_g, v_ln1_b, v_w_up, v_ffn_conv_w, v_ffn_conv_b, v_w_down, v_ln2_g, v_ln2_b):
    args = locals()
    wl = {n: args[n] for n in WEIGHTS}
    ml = {n: args['m_' + n] for n in WEIGHTS}
    vl = {n: args['v_' + n] for n in WEIGHTS}

    small_sizes = [int(wl[n].size) for n in SMALL_SHARDED]
    small_flat = _pad_to(jnp.concatenate([wl[n].reshape(-1) for n in SMALL_SHARDED]), 8 * 1024)
    small_all, w_in0 = _all_gather([small_flat.reshape(-1, 1024), wl['w_in'][0].astype(BF16)], "gather_first")
    small_all = small_all.reshape(N_DEV, -1)
    small_full, off = {}, 0
    for n, sz in zip(SMALL_SHARDED, small_sizes):
        small_full[n] = _gather_full(small_all[:, off:off + sz].reshape((N_DEV,) + wl[n].shape), SHARD_AXIS[n])
        off += sz
    def mixer_params(l, g_in, g_out):
        p = {n: wl[n][l] for n in REPLICATED}
        p.update({n: small_full[n][l] for n in SMALL_SHARDED})
        p['w_in'] = _gather_full(g_in, 1)
        if g_out is not None:
            p['w_out'] = g_out.reshape(D_MODEL, D_MODEL)
        return p

    mix_names, ffn_names = ('w_in', 'w_out'), ('w_up', 'w_down')
    shards = lambda names, l: [wl[n][l].astype(BF16) for n in names]
    gathers = {}
    gathers[0, 'out'], token = _exchange_start(shards(('w_out',), 0), True, "gather_out_l0_start", dep=w_in0)
    gathers[0, 'ffn'], rest0_token = _exchange_start(shards(ffn_names, 0), True, "gather_ffn_l0_start", dep=token)

    def get_layer(l, h):
        if l == 0:
            return mixer_params(0, w_in0, None), rest0_token
        return mixer_params(1, *_exchange_wait(gathers[1, 'mix'], True, h, "gather_mix_l1_wait")), None

    def get_ffn(l, after, part):
        if part == 'out':
            if l > 0:
                return None
            g_out, = _exchange_wait(gathers[0, 'out'], True, after, "gather_out_l0_wait")
            return g_out.reshape(D_MODEL, D_MODEL)
        g_up, g_down = _exchange_wait(gathers[l, 'ffn'], True, after, "gather_ffn_l%d_wait" % l)
        token = None
        if l == 0:
            gathers[1, 'mix'], token = _exchange_start(shards(mix_names, 1), True, "gather_mix_l1_start", dep=g_up)
            gathers[1, 'ffn'], token = _exchange_start(shards(ffn_names, 1), True, "gather_ffn_l1_start", dep=token)
        return g_up, g_down.reshape(D_FF, D_MODEL), token

    scatters = {}

    def after_ffn_grads(l, g):
        send = [g['w_up_g'], g['w_down'].reshape(N_DEV, D_FF // N_DEV, D_MODEL)]
        scatters[l, 'ffn'], token = _exchange_start(send, False, "scatter_ffn_l%d_start" % l)
        return token

    def after_out_grad(l, g_out):
        send = [g_out.reshape(N_DEV, D_MODEL // N_DEV, D_MODEL)]
        scatters[l, 'out'], token = _exchange_start(send, False, "scatter_out_l%d_start" % l)
        return token

    def after_in_grad(l, g_in):
        scatters[l, 'in'], token = _exchange_start([_scatter_blocks(g_in, 1)], False, "scatter_in_l%d_start" % l)
        return token

    def after_small_grads(l, g):
        rep = [g[n][None] for n in REPLICATED]
        if l == DEPTH - 1:
            rep.append(loss_rows[0][None])
        shd = [_scatter_blocks(g[n], SHARD_AXIS[n] - 1)[:, None] for n in SMALL_SHARDED]
        scatters[l, 'rep'], token = _exchange_start(rep, True, "gather_rep_grads_l%d_start" % l)
        scatters[l, 'small'], token = _exchange_start(shd, False, "scatter_small_l%d_start" % l, dep=token)
        return token

    loss_rows = []
    _, grad_x = _run_step(x[0], loss_target[0], get_layer, get_ffn, loss_rows.append, after_ffn_grads,
                          after_out_grad, after_small_grads, after_in_grad)

    results = {}
    big_prev = {n: None for n in BIG}

    def finish_big(l, part, names, after):
        landed = _exchange_wait(scatters[l, part], False, after, "scatter_%s_l%d_wait" % (part, l))
        for n, ld in zip(names, landed):
            big_prev[n] = _adamw_sum(ld, wl[n], ml[n], vl[n], l, big_prev[n], "adamw_%s_l%d" % (n, l))

    for l, part, names in ((1, 'ffn', ffn_names), (1, 'out', ('w_out',)), (1, 'in', ('w_in',)),
                           (0, 'ffn', ffn_names), (0, 'out', ('w_out',))):
        finish_big(l, part, names, grad_x)

    kinds = ('grad', 'delta', 'm', 'v')
    landed = []
    for l in range(DEPTH):
        rep = list(_exchange_wait(scatters[l, 'rep'], True, grad_x, "gather_rep_grads_l%d_wait" % l))
        if l == DEPTH - 1:
            loss = jnp.sum(rep.pop()[:, 0, 0, 0])
        shd = list(_exchange_wait(scatters[l, 'small'], False, grad_x, "scatter_small_l%d_wait" % l))
        landed.append(dict(zip(REPLICATED + SMALL_SHARDED, rep + shd)))
    matrices = ['lru_wr', 'lru_wi', 's5_a_re', 's5_a_im', 's5_c_re', 's5_c_im', 's5_d']
    widest = ['s5_b_re', 's5_b_im']
    vectors = [n for n in REPLICATED + SMALL_SHARDED if n not in matrices + widest]
    last = None
    for tag, names in (("vectors", vectors), ("matrices", matrices), ("s5_b", widest)):
        res = _adamw_many([[landed[l][n] for n in names] for l in range(DEPTH)], [wl[n] for n in names],
                          [ml[n] for n in names], [vl[n] for n in names], "adamw_" + tag)
        for kind, arrs in zip(kinds, res):
            for n, a in zip(names, arrs):
                results[kind, n] = a
        last = res[0][0]
    finish_big(0, 'in', ('w_in',), last)
    for n in BIG:
        results['grad', n], results['delta', n], results['m', n], results['v', n] = big_prev[n]

    out = [loss, grad_x[None]]
    for kind in kinds:
        out.extend(results[kind, n] for n in WEIGHTS)
    return tuple(out)
```

```python
import math

import jax
import jax.numpy as jnp
from jax import lax
from jax.experimental import pallas as pl
from jax.experimental.pallas import tpu as pltpu

F32 = jnp.float32
BF16 = jnp.bfloat16

N_DEV = 8
DEPTH = 2
D_MODEL = 1024
ATTN_W = 384
LRU_W = 384
S5_W = 256
D_IN = 2176
D_FF = 3072
HEAD = 64
ATTN_BLK = 128
ATTN_TILE = 1024
DILATIONS = (1, 4, 16)
S5_G = 16
S5_P = 64
S5_C = 16
S5_STATES = S5_G * S5_P
LRU_C = 8.0
LRU_CONV = 4
FFN_CONV = 3
ROPE_THETA = 10000.0
ALPHA = (2 * DEPTH) ** 0.25
LN_EPS = 1e-5
RMS_EPS = 1e-6
ADAM_LR, ADAM_B1, ADAM_B2, ADAM_EPS, ADAM_WD, ADAM_STEP = 0.001, 0.9, 0.999, 1e-8, 0.01, 10

LANE = 128
SCAN_T = 1024
S5_BLK = 256
FFN_CB = 2 * D_FF // N_DEV
VMEM_LIMIT = 56 * 1024 * 1024

WEIGHTS = ['w_in', 'lru_conv_w', 'lru_conv_b', 'lru_wr', 'lru_br', 'lru_wi', 'lru_bi', 'lru_lambda',
           's5_a_re', 's5_a_im', 's5_b_re', 's5_b_im', 's5_c_re', 's5_c_im', 's5_d', 's5_log_step',
           's5_w_glu', 's5_b_glu', 'mix_norm_g', 'w_out', 'ln1_g', 'ln1_b', 'w_up', 'ffn_conv_w',
           'ffn_conv_b', 'w_down', 'ln2_g', 'ln2_b']
SHARD_AXIS = {'w_in': 2, 'lru_conv_w': 2, 's5_w_glu': 1, 'w_out': 1, 'w_up': 2, 'ffn_conv_w': 2, 'w_down': 1}
BIG = ['w_in', 'w_out', 'w_up', 'w_down']
SMALL_SHARDED = ['lru_conv_w', 'ffn_conv_w', 's5_w_glu']
REPLICATED = [n for n in WEIGHTS if n not in SHARD_AXIS]


def _cparams(sem=None):
    return pltpu.CompilerParams(dimension_semantics=sem, vmem_limit_bytes=VMEM_LIMIT)


def _grad_dtype(l):
    return BF16 if l == 0 else F32


def _ffn_dev(jb):
    return jb // 2 + (N_DEV // 2) * (jb % 2)


def _gelu(x):
    c = math.sqrt(2.0 / math.pi)
    t = jnp.tanh(c * (x + 0.044715 * (x * x * x)))
    return 0.5 * x * (1.0 + t)


def _gelu_grad(x):
    c = math.sqrt(2.0 / math.pi)
    x2 = x * x
    t = jnp.tanh(c * (x + 0.044715 * (x2 * x)))
    return 0.5 * (1.0 + t) + 0.5 * x * (1.0 - t * t) * (c * (1.0 + 3.0 * 0.044715 * x2))


def _sigmoid(x):
    return 1.0 / (1.0 + jnp.exp(-x))


def _log1p(x):
    u = 1.0 + x
    d = u - 1.0
    return jnp.where(d == 0.0, x, jnp.log(u) * (x / jnp.where(d == 0.0, 1.0, d)))


def _softplus(x):
    return jnp.maximum(x, 0.0) + _log1p(jnp.exp(-jnp.abs(x)))


def _expm1(x):
    return jnp.tanh(0.5 * x) * (jnp.exp(x) + 1.0)


def _dot(a, b):
    return jnp.dot(a.astype(BF16), b.astype(BF16), preferred_element_type=F32)


def _dot_nt(a, b):
    return lax.dot_general(a.astype(BF16), b.astype(BF16), (((1,), (1,)), ((), ())),
                           preferred_element_type=F32)


def _dot_tn(a, b):
    return lax.dot_general(a.astype(BF16), b.astype(BF16), (((0,), (0,)), ((), ())),
                           preferred_element_type=F32)


def _rows(shape):
    return lax.broadcasted_iota(jnp.int32, shape, 0)


def _shift_down_prev(x, s, prev8):
    if s == 0:
        return x
    t, l = x.shape
    r = pltpu.roll(x, s, axis=0)
    pr = pltpu.roll(prev8, s, axis=0)
    pad = jnp.concatenate([pr, jnp.zeros((t - 8, l), x.dtype)], axis=0)
    return jnp.where(_rows(x.shape) < s, pad, r)


def _shift_up_next(x, s, next8):
    if s == 0:
        return x
    t, l = x.shape
    r = pltpu.roll(x, t - s, axis=0)
    nx = pltpu.roll(next8, 8 - s, axis=0)
    pad = jnp.concatenate([jnp.zeros((t - 8, l), x.dtype), nx], axis=0)
    return jnp.where(_rows(x.shape) >= t - s, pad, r)


SUB = 8


def _tile_shift(x, s, fill, reverse):
    t = x.shape[0]
    pos = _rows(x.shape) & (SUB - 1)
    if reverse:
        return jnp.where(pos < SUB - s, pltpu.roll(x, t - s, axis=0), fill)
    return jnp.where(pos >= s, pltpu.roll(x, s, axis=0), fill)


def _scan_chunk(a, x, carry, reverse=False):
    s = 1
    while s < SUB:
        x = x + a * _tile_shift(x, s, 0.0, reverse)
        a = a * _tile_shift(a, s, 1.0, reverse)
        s *= 2
    nv = x.shape[0] // SUB
    out = [None] * nv
    for v in (reversed(range(nv)) if reverse else range(nv)):
        rows = slice(v * SUB, (v + 1) * SUB)
        out[v] = x[rows, :] + a[rows, :] * carry
        carry = out[v][0:1, :] if reverse else out[v][SUB - 1:SUB, :]
    return jnp.concatenate(out, axis=0)


def _cmul(ar, ai, br, bi):
    return ar * br - ai * bi, ar * bi + ai * br


def _cscan_consts(lr, li, reverse):
    pows = [(lr, li)]
    for _ in range(2):
        pows.append(_cmul(*pows[-1], *pows[-1]))
    rows = [(lr, li)]
    for _ in range(SUB - 1):
        rows.append(_cmul(*rows[-1], lr, li))
    if reverse:
        rows = rows[::-1]
    return pows, (jnp.concatenate([r for r, _ in rows], axis=0), jnp.concatenate([i for _, i in rows], axis=0))


def _cscan_chunk(xr, xi, consts, carry, reverse=False):
    pows, (p8r, p8i) = consts
    s = 1
    for pr, pi in pows:
        sr = _tile_shift(xr, s, 0.0, reverse)
        si = _tile_shift(xi, s, 0.0, reverse)
        xr, xi = xr + pr * sr - pi * si, xi + pr * si + pi * sr
        s *= 2
    nv = xr.shape[0] // SUB
    out_r, out_i = [None] * nv, [None] * nv
    cr, ci = carry
    for v in (reversed(range(nv)) if reverse else range(nv)):
        rows = slice(v * SUB, (v + 1) * SUB)
        out_r[v] = xr[rows, :] + p8r * cr - p8i * ci
        out_i[v] = xi[rows, :] + p8r * ci + p8i * cr
        edge = slice(0, 1) if reverse else slice(SUB - 1, SUB)
        cr, ci = out_r[v][edge, :], out_i[v][edge, :]
    return jnp.concatenate(out_r, axis=0), jnp.concatenate(out_i, axis=0)


def _dep_args(dep):
    return ([], []) if dep is None else ([pl.BlockSpec(memory_space=pl.ANY)], [dep])


def _mm_nt(a, w, tm, tn, name, add=None, add_scale=1.0, dep=None):
    m, k = a.shape
    n = w.shape[0]

    def body(a_ref, w_ref, *rest):
        o_ref = rest[-1]
        if add is None:
            o_ref[...] = _dot_nt(a_ref[...], w_ref[...])
        else:
            o_ref[...] = _dot_nt(a_ref[...], w_ref[...]) + add_scale * rest[0][...]

    in_specs = [pl.BlockSpec((tm, k), lambda j, i: (i, 0)), pl.BlockSpec((tn, k), lambda j, i: (j, 0))]
    args = [a, w]
    if add is not None:
        in_specs.append(pl.BlockSpec((tm, tn), lambda j, i: (i, j)))
        args.append(add)
    dep_specs, dep_ops = _dep_args(dep)
    return pl.pallas_call(
        body, out_shape=jax.ShapeDtypeStruct((m, n), F32), grid=(n // tn, m // tm),
        in_specs=in_specs + dep_specs, out_specs=pl.BlockSpec((tm, tn), lambda j, i: (i, j)), name=name,
        compiler_params=_cparams(("parallel", "parallel")))(*args, *dep_ops)


def _mm_dw(at, b, tm, tn, ts, name, out_dtype=F32):
    m, s = at.shape
    n = b.shape[1]
    nk = s // ts

    def body(a_ref, b_ref, o_ref, acc):
        @pl.when(pl.program_id(2) == 0)
        def _():
            acc[...] = jnp.zeros_like(acc)
        acc[...] += _dot(a_ref[...], b_ref[...])

        @pl.when(pl.program_id(2) == nk - 1)
        def _():
            o_ref[...] = acc[...].astype(out_dtype)

    return pl.pallas_call(
        body, out_shape=jax.ShapeDtypeStruct((m, n), out_dtype), grid=(m // tm, n // tn, nk),
        in_specs=[pl.BlockSpec((tm, ts), lambda i, j, k: (i, k)), pl.BlockSpec((ts, tn), lambda i, j, k: (k, j))],
        out_specs=pl.BlockSpec((tm, tn), lambda i, j, k: (i, j)),
        scratch_shapes=[pltpu.VMEM((tm, tn), F32)], name=name,
        compiler_params=_cparams(("parallel", "parallel", "arbitrary")))(at, b)


def _mm_up_dw(ht, dup, name, out_dtype=F32):
    d, s = ht.shape

    def body(a_ref, b_ref, o_ref):
        o_ref[...] = _dot(a_ref[...], b_ref[...]).astype(out_dtype)

    return pl.pallas_call(
        body, out_shape=jax.ShapeDtypeStruct((N_DEV, d, FFN_CB), out_dtype), grid=(N_DEV,),
        in_specs=[pl.BlockSpec((d, s), lambda j: (0, 0)), pl.BlockSpec((s, FFN_CB), lambda j: (0, j))],
        out_specs=pl.BlockSpec((None, d, FFN_CB), lambda j: (_ffn_dev(j), 0, 0)), name=name,
        compiler_params=_cparams(("parallel",)))(ht, dup)


def _layer_norm(r, g, b):
    mu = jnp.mean(r, axis=-1, keepdims=True)
    xc = r - mu
    var = jnp.mean(xc * xc, axis=-1, keepdims=True)
    return xc * lax.rsqrt(var + LN_EPS) * g + b


def _proj_ln(a, w, resid, g, bias, name, transposed=True, target=None):
    s, k = a.shape
    d = w.shape[1]
    tm = 512

    def body(a_ref, w_ref, x_ref, g_ref, bias_ref, *rest):
        r = ALPHA * x_ref[...] + _dot(a_ref[...], w_ref[...])
        h = _layer_norm(r, g_ref[...], bias_ref[...])
        if target is None:
            r_ref, h_ref = rest[0], rest[1]
            h_ref[...] = h
            if transposed:
                rest[2][...] = h.T.astype(BF16)
        else:
            t_ref, r_ref, dy_ref, l_ref = rest

            @pl.when(pl.program_id(0) == 0)
            def _():
                l_ref[...] = jnp.zeros_like(l_ref)
            e = h - t_ref[...]
            dy_ref[...] = e * (1.0 / d)
            part = 0.5 * jnp.sum(jnp.mean(e * e, axis=-1, keepdims=True), axis=0, keepdims=True)
            l_ref[...] += jnp.broadcast_to(part, l_ref.shape)
        r_ref[...] = r

    row = pl.BlockSpec((tm, d), lambda i: (i, 0))
    vec = pl.BlockSpec((1, d), lambda i: (0, 0))
    in_specs = [pl.BlockSpec((tm, k), lambda i: (i, 0)), pl.BlockSpec((k, d), lambda i: (0, 0)), row, vec, vec]
    args = [a, w, resid, g, bias]
    shapes = [jax.ShapeDtypeStruct((s, d), F32), jax.ShapeDtypeStruct((s, d), F32)]
    specs = [row, row]
    if target is not None:
        in_specs.append(row)
        args.append(target)
        shapes.append(jax.ShapeDtypeStruct((1, LANE), F32))
        specs.append(pl.BlockSpec((1, LANE), lambda i: (0, 0)))
    elif transposed:
        shapes.append(jax.ShapeDtypeStruct((d, s), BF16))
        specs.append(pl.BlockSpec((d, tm), lambda i: (0, i)))
    return pl.pallas_call(
        body, out_shape=tuple(shapes), grid=(s // tm,), in_specs=in_specs, out_specs=tuple(specs), name=name,
        compiler_params=_cparams(("arbitrary",) if target is not None else ("parallel",)))(*args)


def _layer_norm_bwd(r, dh, g):
    mu = jnp.mean(r, axis=-1, keepdims=True)
    xc = r - mu
    var = jnp.mean(xc * xc, axis=-1, keepdims=True)
    rstd = lax.rsqrt(var + LN_EPS)
    xh = xc * rstd
    dxh = dh * g
    m1 = jnp.mean(dxh, axis=-1, keepdims=True)
    m2 = jnp.mean(dxh * xh, axis=-1, keepdims=True)
    return (rstd * (dxh - m1 - xh * m2), jnp.sum(dh * xh, axis=0, keepdims=True),
            jnp.sum(dh, axis=0, keepdims=True))


def _ln_bwd(r, dh, g, name, dep=None):
    s, d = r.shape
    tm = 512

    def body(r_ref, dh_ref, g_ref, *rest):
        dr_ref, dg_ref, db_ref = rest[-3:]

        @pl.when(pl.program_id(0) == 0)
        def _():
            dg_ref[...] = jnp.zeros_like(dg_ref)
            db_ref[...] = jnp.zeros_like(db_ref)
        dr_ref[...], dg_rows, db_rows = _layer_norm_bwd(r_ref[...], dh_ref[...], g_ref[...])
        dg_ref[...] += dg_rows
        db_ref[...] += db_rows

    row = pl.BlockSpec((tm, d), lambda i: (i, 0))
    vec = pl.BlockSpec((1, d), lambda i: (0, 0))
    dep_specs, dep_ops = _dep_args(dep)
    return pl.pallas_call(
        body, out_shape=(jax.ShapeDtypeStruct((s, d), F32), jax.ShapeDtypeStruct((1, d), F32),
                         jax.ShapeDtypeStruct((1, d), F32)),
        grid=(s // tm,), in_specs=[row, row, vec] + dep_specs, out_specs=(row, vec, vec), name=name,
        compiler_params=_cparams(("arbitrary",)))(r, dh, g, *dep_ops)


def _rope_tables(s):
    half = HEAD // 2
    pos = jnp.arange(s, dtype=F32)
    inv = ROPE_THETA ** (-jnp.arange(half, dtype=F32) * 2.0 / HEAD)
    ang = pos[:, None] * inv[None, :]
    cos, sin = jnp.cos(ang), jnp.sin(ang)
    cos = jnp.concatenate([cos, cos, cos, cos], axis=1)
    sin = jnp.concatenate([-sin, sin, -sin, sin], axis=1)
    return cos, sin


def _rotate(x, cos, sin):
    lane = lax.broadcasted_iota(jnp.int32, x.shape, 1)
    partner = jnp.where((lane % HEAD) < HEAD // 2, pltpu.roll(x, LANE - HEAD // 2, axis=1),
                        pltpu.roll(x, HEAD // 2, axis=1))
    return x * cos + partner * sin


def _class_rows(c, d, tm):
    return pl.ds(c, tm // d, stride=d) if d > 1 else pl.ds(0, tm)


def _dilated_spec(tm, d, w):
    return pl.BlockSpec((tm // d, d * w), lambda i: (i, 0))


def _token_scratch(tm, w):
    return pltpu.VMEM((w // LANE, tm, LANE), F32)


def _to_tokens(src_ref, dst3, d, tm):
    nj = dst3.shape[0]
    for cls in range(d):
        for j in range(nj):
            col = (cls * nj + j) * LANE
            dst3.at[j][_class_rows(cls, d, tm), :] = src_ref[:, col:col + LANE]


def _to_dilated(src3, dst_ref, d, tm):
    nj = src3.shape[0]
    for cls in range(d):
        for j in range(nj):
            col = (cls * nj + j) * LANE
            dst_ref[:, col:col + LANE] = src3.at[j][_class_rows(cls, d, tm), :].astype(dst_ref.dtype)


def _token_value(src3):
    return jnp.concatenate([src3[j] for j in range(src3.shape[0])], axis=1)


def _proj_rope(h, w_in, cos, sin, name, dep=None, transposed=False):
    s, d_model = h.shape
    tm = 512
    w = 3 * ATTN_W
    nj = w // LANE

    def body(h_ref, w_ref, c_ref, s_ref, *rest):
        rot = rest[-1]
        if transposed:
            p_ref, o_refs, ht_ref = rest[-6], rest[-5:-2], rest[-2]
            ht_ref[...] = h_ref[...].T.astype(BF16)
        else:
            p_ref, o_refs = rest[-5], rest[-4:-1]
        y = _dot(h_ref[...], w_ref[...])
        p_ref[...] = y
        c, sn = c_ref[...], s_ref[...]
        for j in range(nj):
            x = y[:, j * LANE:(j + 1) * LANE]
            rot[j] = _rotate(x, c, sn) if j < 2 * ATTN_W // LANE else x
        for d, o_ref in zip(DILATIONS, o_refs):
            _to_dilated(rot, o_ref, d, tm)

    tab = pl.BlockSpec((tm, LANE), lambda i: (i, 0))
    dep_specs, dep_ops = _dep_args(dep)
    shapes = [jax.ShapeDtypeStruct((s, D_IN), F32), *[jax.ShapeDtypeStruct((s // d, d * w), BF16) for d in DILATIONS]]
    specs = [pl.BlockSpec((tm, D_IN), lambda i: (i, 0)), *[_dilated_spec(tm, d, w) for d in DILATIONS]]
    if transposed:
        shapes.append(jax.ShapeDtypeStruct((d_model, s), BF16))
        specs.append(pl.BlockSpec((d_model, tm), lambda i: (0, i)))
    res = pl.pallas_call(
        body, out_shape=tuple(shapes), grid=(s // tm,),
        in_specs=[pl.BlockSpec((tm, d_model), lambda i: (i, 0)), pl.BlockSpec((d_model, D_IN), lambda i: (0, 0)),
                  tab, tab] + dep_specs,
        out_specs=tuple(specs), scratch_shapes=[_token_scratch(tm, w)], name=name,
        compiler_params=_cparams(("parallel",)))(h, w_in, cos, sin, *dep_ops)
    return res[0], res[1:4], (res[4] if transposed else None)


def _dproj_assemble(dqkv_list, dxr, dgate, du, cos, sin, name):
    s = dxr.shape[0]
    tm = 512
    nq = 3 * ATTN_W // LANE

    def body(*refs):
        br = refs[:9]
        dxr_ref, dg_ref, du_ref, c_ref, s_ref, o_ref = refs[9:15]
        tok = refs[15:]
        c, sn = c_ref[...], -s_ref[...]
        for part in range(3):
            for b, d in enumerate(DILATIONS[1:], start=1):
                _to_tokens(br[3 * b + part], tok[2 * part + b - 1], d, tm)
        for j in range(nq):
            part, jj = divmod(j, ATTN_W // LANE)
            x = br[part][:, jj * LANE:(jj + 1) * LANE] + tok[2 * part][jj] + tok[2 * part + 1][jj]
            if part < 2:
                x = _rotate(x, c, sn)
            o_ref[:, j * LANE:(j + 1) * LANE] = x.astype(BF16)
        o_ref[:, 3 * ATTN_W:3 * ATTN_W + LRU_W] = dxr_ref[...].astype(BF16)
        o_ref[:, 3 * ATTN_W + LRU_W:3 * ATTN_W + 2 * LRU_W] = dg_ref[...].astype(BF16)
        o_ref[:, 3 * ATTN_W + 2 * LRU_W:] = du_ref[...].astype(BF16)

    a_spec = pl.BlockSpec((tm, ATTN_W), lambda i: (i, 0))
    tab = pl.BlockSpec((tm, LANE), lambda i: (i, 0))
    ordered = [dqkv_list[b][p] for b in range(3) for p in range(3)]
    d_specs = [_dilated_spec(tm, d, ATTN_W) for d in DILATIONS for _ in range(3)]
    return pl.pallas_call(
        body, out_shape=jax.ShapeDtypeStruct((s, D_IN), BF16), grid=(s // tm,),
        in_specs=d_specs + [a_spec, a_spec, pl.BlockSpec((tm, S5_W), lambda i: (i, 0)), tab, tab],
        out_specs=pl.BlockSpec((tm, D_IN), lambda i: (i, 0)),
        scratch_shapes=[_token_scratch(tm, ATTN_W)] * 6, name=name,
        compiler_params=_cparams(("parallel",)))(*ordered, dxr, dgate, du, cos, sin)


def _attn_tiles(s, d):
    m = s // d
    tq = min(m, ATTN_TILE)
    return m, tq, tq // ATTN_BLK


def _band_mask(qb):
    qi = lax.broadcasted_iota(jnp.int32, (ATTN_BLK, 2 * ATTN_BLK), 0)
    ki = lax.broadcasted_iota(jnp.int32, (ATTN_BLK, 2 * ATTN_BLK), 1)
    dist = qi + ATTN_BLK - ki
    return (dist >= 0) & (dist <= ATTN_BLK) & ((ki >= ATTN_BLK) | (qb > 0))


def _attn_fwd(qv, d, name):
    m = qv.shape[0]
    w3 = 3 * ATTN_W
    _, tq, n = _attn_tiles(m * d, d)
    scale = HEAD ** -0.5

    def body(x_ref, p_ref, o_ref, l_ref):
        b = pl.program_id(1)

        def block(i, first):
            r0 = 0 if first else pl.multiple_of(i * ATTN_BLK, ATTN_BLK)
            rows = pl.ds(r0, ATTN_BLK)
            valid = _band_mask(b * n + i)
            if not first:
                krows = pl.ds(pl.multiple_of(i * ATTN_BLK - ATTN_BLK, ATTN_BLK), 2 * ATTN_BLK)
            low = lax.broadcasted_iota(jnp.int32, (1, LANE), 1) < HEAD
            for hp in range(ATTN_W // LANE):
                qs, ks, vs = (slice(part * ATTN_W + hp * LANE, part * ATTN_W + (hp + 1) * LANE) for part in range(3))
                q2 = x_ref[rows, qs]
                if first:
                    k2 = jnp.concatenate([p_ref[:, ks], x_ref[0:ATTN_BLK, ks]], axis=0)
                    v2 = jnp.concatenate([p_ref[:, vs], x_ref[0:ATTN_BLK, vs]], axis=0)
                else:
                    k2 = x_ref[krows, ks]
                    v2 = x_ref[krows, vs]
                outs, lses = [], []
                for mask in (low, ~low):
                    q = jnp.where(mask, q2, jnp.zeros_like(q2))
                    sc = jnp.where(valid, _dot_nt(q, k2) * scale, -1e30)
                    mx = jnp.max(sc, axis=-1, keepdims=True)
                    p = jnp.exp(sc - mx)
                    l = jnp.sum(p, axis=-1, keepdims=True)
                    outs.append(_dot(p, v2) / l)
                    lses.append(mx + jnp.log(l))
                o_ref[rows, hp * LANE:(hp + 1) * LANE] = jnp.where(low, outs[0], outs[1])
                l_ref[rows, hp * LANE:(hp + 1) * LANE] = jnp.where(low, lses[0], lses[1])

        block(0, True)
        if n > 1:
            def loop(i, carry):
                block(i, False)
                return carry
            lax.fori_loop(1, n, loop, 0)

    shp = jax.ShapeDtypeStruct((m, d * ATTN_W), F32)
    ospec = pl.BlockSpec((tq, ATTN_W), lambda c, b: (b, c))
    out, lse = pl.pallas_call(
        body, out_shape=(shp, shp), grid=(d, m // tq),
        in_specs=[pl.BlockSpec((tq, w3), lambda c, b: (b, c)),
                  pl.BlockSpec((ATTN_BLK, w3), lambda c, b: (jnp.maximum(b * n - 1, 0), c))],
        out_specs=(ospec, ospec), name=name,
        compiler_params=_cparams(("parallel", "parallel")))(qv, qv)
    return out, lse


def _attn_bwd(qv, ov, dov, lv, d, name, dep=None):
    m = qv.shape[0]
    w3 = 3 * ATTN_W
    _, tq, n = _attn_tiles(m * d, d)
    nb = m // ATTN_BLK
    scale = HEAD ** -0.5

    def body(x_ref, p_ref, nx_ref, o_ref, do_ref, l_ref, on_ref, don_ref, ln_ref, *rest):
        dq_ref, dk_ref, dv_ref = rest[-3:]
        b = pl.program_id(1)
        dk_ref[...] = jnp.zeros_like(dk_ref)
        dv_ref[...] = jnp.zeros_like(dv_ref)

        low = lax.broadcasted_iota(jnp.int32, (1, LANE), 1) < HEAD

        def pair_grads(q2, k2, v2, o2, do2, l2, valid):
            dq, dk, dv = [], 0.0, 0.0
            for mask, lse in ((low, l2[:, 0:1]), (~low, l2[:, HEAD:HEAD + 1])):
                q = jnp.where(mask, q2, jnp.zeros_like(q2))
                do = jnp.where(mask, do2, 0.0)
                sc = jnp.where(valid, _dot_nt(q, k2) * scale, -1e30)
                p = jnp.exp(sc - lse)
                delta = jnp.sum(do * o2, axis=-1, keepdims=True)
                ds = p * (_dot_nt(do, v2) - delta) * scale
                dq.append(_dot(ds, k2))
                dk = dk + _dot_tn(ds, q)
                dv = dv + _dot_tn(p, do)
            return jnp.where(low, dq[0], dq[1]), dk, dv

        def cols(hp):
            return [slice(part * ATTN_W + hp * LANE, part * ATTN_W + (hp + 1) * LANE) for part in range(3)]

        def block(i, first):
            r0 = 0 if first else pl.multiple_of(i * ATTN_BLK, ATTN_BLK)
            rows = pl.ds(r0, ATTN_BLK)
            valid = _band_mask(b * n + i)
            if not first:
                krows = pl.ds(pl.multiple_of(i * ATTN_BLK - ATTN_BLK, ATTN_BLK), 2 * ATTN_BLK)
            for hp in range(ATTN_W // LANE):
                qs, ks, vs = cols(hp)
                if first:
                    k2 = jnp.concatenate([p_ref[:, ks], x_ref[0:ATTN_BLK, ks]], axis=0)
                    v2 = jnp.concatenate([p_ref[:, vs], x_ref[0:ATTN_BLK, vs]], axis=0)
                else:
                    k2 = x_ref[krows, ks]
                    v2 = x_ref[krows, vs]
                dq, dk, dv = pair_grads(x_ref[rows, qs], k2, v2, o_ref[rows, qs], do_ref[rows, qs],
                                        l_ref[rows, qs], valid)
                dq_ref[rows, qs] = dq
                if first:
                    dk_ref[0:ATTN_BLK, qs] += dk[ATTN_BLK:, :]
                    dv_ref[0:ATTN_BLK, qs] += dv[ATTN_BLK:, :]
                else:
                    dk_ref[krows, qs] += dk
                    dv_ref[krows, qs] += dv

        block(0, True)
        if n > 1:
            def loop(i, carry):
                block(i, False)
                return carry
            lax.fori_loop(1, n, loop, 0)

        last = slice((n - 1) * ATTN_BLK, n * ATTN_BLK)
        qi = lax.broadcasted_iota(jnp.int32, (ATTN_BLK, ATTN_BLK), 0)
        ki = lax.broadcasted_iota(jnp.int32, (ATTN_BLK, ATTN_BLK), 1)
        valid_next = (qi <= ki) & ((b + 1) * n < nb)
        for hp in range(ATTN_W // LANE):
            qs, ks, vs = cols(hp)
            _, dk, dv = pair_grads(nx_ref[:, qs], x_ref[last, ks], x_ref[last, vs], on_ref[:, qs], don_ref[:, qs],
                                   ln_ref[:, qs], valid_next)
            dk_ref[last, qs] += dk
            dv_ref[last, qs] += dv

    nxt = lambda b: jnp.minimum((b + 1) * n, nb - 1)
    xs = pl.BlockSpec((tq, w3), lambda c, b: (b, c))
    xp = pl.BlockSpec((ATTN_BLK, w3), lambda c, b: (jnp.maximum(b * n - 1, 0), c))
    xn = pl.BlockSpec((ATTN_BLK, w3), lambda c, b: (nxt(b), c))
    a = pl.BlockSpec((tq, ATTN_W), lambda c, b: (b, c))
    an = pl.BlockSpec((ATTN_BLK, ATTN_W), lambda c, b: (nxt(b), c))
    shp = jax.ShapeDtypeStruct((m, d * ATTN_W), F32)
    dep_specs, dep_ops = _dep_args(dep)
    return pl.pallas_call(
        body, out_shape=(shp, shp, shp), grid=(d, m // tq),
        in_specs=[xs, xp, xn, a, a, a, an, an, an] + dep_specs, out_specs=(a, a, a), name=name,
        compiler_params=_cparams(("parallel", "parallel")))(qv, qv, qv, ov, dov, lv, ov, dov, lv, *dep_ops)


def _rms(x, g):
    ms = jnp.mean(x * x, axis=-1, keepdims=True)
    return x * lax.rsqrt(ms + RMS_EPS) * g


def _rms_bwd(x, g, dy):
    ms = jnp.mean(x * x, axis=-1, keepdims=True)
    r = lax.rsqrt(ms + RMS_EPS)
    dyg = dy * g
    dx = r * dyg - x * (r * r * r) * jnp.mean(x * dyg, axis=-1, keepdims=True)
    return dx, dy * x * r


def _mix_fwd(outs, lses, lru, s5, g, h_in, w_out, ln_g, ln_b, name):
    s = lru.shape[0]
    tm = 512

    def body(o1, o2, o3, l1, l2, l3, lru_ref, s5_ref, g_ref, x_ref, w_ref, lg_ref, lb_ref,
             mixed_t_ref, r_ref, h_ref, ht_ref, ov1, ov2, ov3, lv1, lv2, lv3, so2, so3, sl2, sl3):
        for d, src, dst in ((DILATIONS[1], o2, so2), (DILATIONS[2], o3, so3),
                            (DILATIONS[1], l2, sl2), (DILATIONS[2], l3, sl3)):
            _to_tokens(src, dst, d, tm)
        a1, a2, a3 = l1[...], _token_value(sl2), _token_value(sl3)
        mx = jnp.maximum(jnp.maximum(a1, a2), a3)
        e1, e2, e3 = jnp.exp(a1 - mx), jnp.exp(a2 - mx), jnp.exp(a3 - mx)
        den = e1 + e2 + e3
        o = (e1 * o1[...] + e2 * _token_value(so2) + e3 * _token_value(so3)) / den
        lse = mx + jnp.log(den)
        ov1[...] = o
        lv1[...] = lse
        for j in range(ATTN_W // LANE):
            so2[j] = o[:, j * LANE:(j + 1) * LANE]
            sl2[j] = lse[:, j * LANE:(j + 1) * LANE]
        for d, o_dst, l_dst in ((DILATIONS[1], ov2, lv2), (DILATIONS[2], ov3, lv3)):
            _to_dilated(so2, o_dst, d, tm)
            _to_dilated(sl2, l_dst, d, tm)
        gg = g_ref[...]
        mixed = jnp.concatenate([_rms(o, gg[:, :ATTN_W]),
                                 _rms(lru_ref[...], gg[:, ATTN_W:ATTN_W + LRU_W]),
                                 _rms(s5_ref[...], gg[:, ATTN_W + LRU_W:])], axis=1)
        mixed_t_ref[...] = mixed.T.astype(BF16)
        r = ALPHA * x_ref[...] + _dot(mixed, w_ref[...])
        h = _layer_norm(r, lg_ref[...], lb_ref[...])
        r_ref[...] = r
        h_ref[...] = h
        ht_ref[...] = h.T.astype(BF16)

    a = pl.BlockSpec((tm, ATTN_W), lambda i: (i, 0))
    s5s = pl.BlockSpec((tm, S5_W), lambda i: (i, 0))
    full = pl.BlockSpec((tm, D_MODEL), lambda i: (i, 0))
    vec = pl.BlockSpec((1, D_MODEL), lambda i: (0, 0))
    dil = [_dilated_spec(tm, d, ATTN_W) for d in DILATIONS]
    dshape = [jax.ShapeDtypeStruct((s // d, d * ATTN_W), F32) for d in DILATIONS]
    tshape = jax.ShapeDtypeStruct((D_MODEL, s), BF16)
    fshape = jax.ShapeDtypeStruct((s, D_MODEL), F32)
    tspec = pl.BlockSpec((D_MODEL, tm), lambda i: (0, i))
    res = pl.pallas_call(
        body, out_shape=(tshape, fshape, fshape, tshape, *dshape, *dshape),
        grid=(s // tm,),
        in_specs=dil + dil + [a, s5s, vec, full, pl.BlockSpec((D_MODEL, D_MODEL), lambda i: (0, 0)), vec, vec],
        out_specs=(tspec, full, full, tspec, *dil, *dil),
        scratch_shapes=[_token_scratch(tm, ATTN_W)] * 4, name=name,
        compiler_params=_cparams(("parallel",)))(*outs, *lses, lru, s5, g, h_in, w_out, ln_g, ln_b)
    return res[0], res[1], res[2], res[3], res[4:7], res[7:10]


def _mix_bwd(r, dh, ln_g, w_out, o, lru, s5, g, name, dep=None):
    s = lru.shape[0]
    tm = 512

    def body(r_ref, dh_ref, lg_ref, w_ref, o_ref, lru_ref, s5_ref, g_ref, *rest):
        dr_ref, dlg_ref, dlb_ref, do_ref, do2_ref, do3_ref, dlru_ref, ds5_ref, dg_ref, stage = rest[-10:]

        @pl.when(pl.program_id(0) == 0)
        def _():
            dg_ref[...] = jnp.zeros_like(dg_ref)
            dlg_ref[...] = jnp.zeros_like(dlg_ref)
            dlb_ref[...] = jnp.zeros_like(dlb_ref)
        gg = g_ref[...]
        dr, dlg_rows, dlb_rows = _layer_norm_bwd(r_ref[...], dh_ref[...], lg_ref[...])
        dr_ref[...] = dr
        dlg_ref[...] += dlg_rows
        dlb_ref[...] += dlb_rows
        dm = _dot_nt(dr, w_ref[...])
        dx, dgr = _rms_bwd(o_ref[...], gg[:, :ATTN_W], dm[:, :ATTN_W])
        do_ref[...] = dx
        for j in range(ATTN_W // LANE):
            stage[j] = dx[:, j * LANE:(j + 1) * LANE]
        _to_dilated(stage, do2_ref, DILATIONS[1], tm)
        _to_dilated(stage, do3_ref, DILATIONS[2], tm)
        dg_ref[:, :ATTN_W] += jnp.sum(dgr, axis=0, keepdims=True)
        dx, dgr = _rms_bwd(lru_ref[...], gg[:, ATTN_W:ATTN_W + LRU_W], dm[:, ATTN_W:ATTN_W + LRU_W])
        dlru_ref[...] = dx
        dg_ref[:, ATTN_W:ATTN_W + LRU_W] += jnp.sum(dgr, axis=0, keepdims=True)
        dx, dgr = _rms_bwd(s5_ref[...], gg[:, ATTN_W + LRU_W:], dm[:, ATTN_W + LRU_W:])
        ds5_ref[...] = dx
        dg_ref[:, ATTN_W + LRU_W:] += jnp.sum(dgr, axis=0, keepdims=True)

    a = pl.BlockSpec((tm, ATTN_W), lambda i: (i, 0))
    s5s = pl.BlockSpec((tm, S5_W), lambda i: (i, 0))
    full = pl.BlockSpec((tm, D_MODEL), lambda i: (i, 0))
    vec = pl.BlockSpec((1, D_MODEL), lambda i: (0, 0))
    dil = [_dilated_spec(tm, d, ATTN_W) for d in DILATIONS]
    dshape = [jax.ShapeDtypeStruct((s // d, d * ATTN_W), F32) for d in DILATIONS]
    dep_specs, dep_ops = _dep_args(dep)
    vshape = jax.ShapeDtypeStruct((1, D_MODEL), F32)
    res = pl.pallas_call(
        body, out_shape=(jax.ShapeDtypeStruct((s, D_MODEL), F32), vshape, vshape, *dshape,
                         jax.ShapeDtypeStruct((s, LRU_W), F32), jax.ShapeDtypeStruct((s, S5_W), F32), vshape),
        grid=(s // tm,),
        in_specs=[full, full, vec, pl.BlockSpec((D_MODEL, D_MODEL), lambda i: (0, 0)), a, a, s5s, vec] + dep_specs,
        out_specs=(full, vec, vec, *dil, a, s5s, vec), scratch_shapes=[_token_scratch(tm, ATTN_W)], name=name,
        compiler_params=_cparams(("arbitrary",)))(r, dh, ln_g, w_out, o, lru, s5, g, *dep_ops)
    return res[0], res[1], res[2], res[3:6], res[6], res[7], res[8]


def _lru_gate_math(xc, pre_r, pre_i, lam):
    r = _sigmoid(pre_r)
    i = _sigmoid(pre_i)
    log_a = -LRU_C * r * _softplus(-lam)
    a = jnp.exp(log_a)
    u = jnp.sqrt(-_expm1(2.0 * log_a)) * (i * xc)
    return a, u


def _lru_conv(x, prev8, cw, cb):
    y = cb + cw[LRU_CONV - 1:LRU_CONV, :] * x
    for k in range(LRU_CONV - 1):
        y = y + cw[k:k + 1, :] * _shift_down_prev(x, LRU_CONV - 1 - k, prev8)
    return y


def _lru_specs(s):
    xo = 3 * ATTN_W // LANE
    go = xo + LRU_W // LANE
    xr = pl.BlockSpec((s, LANE), lambda j: (0, xo + j))
    gt = pl.BlockSpec((s, LANE), lambda j: (0, go + j))
    cw = pl.BlockSpec((LRU_CONV, LANE), lambda j: (0, j))
    vec = pl.BlockSpec((1, LANE), lambda j: (0, j))
    wbd = pl.BlockSpec((LANE, LANE), lambda j: (j, j))
    col = pl.BlockSpec((s, LANE), lambda j: (0, j))
    return xr, gt, cw, vec, wbd, col


def _lru_fwd(proj, cw, cb, wr, br, wi, bi, lam, name):
    s = proj.shape[0]
    t = SCAN_T

    def body(xr_ref, gt_ref, cw_ref, cb_ref, wr_ref, br_ref, wi_ref, bi_ref, lam_ref, o_ref, xc_ref, a_ref, h_ref):
        cwv, cbv, lamv = cw_ref[...], cb_ref[...], lam_ref[...]
        wrv, wiv, brv, biv = wr_ref[...], wi_ref[...], br_ref[...], bi_ref[...]

        def chunk(c, carry):
            h_c, prev8 = carry
            rows = pl.ds(pl.multiple_of(c * t, t), t)
            x = xr_ref[rows, :]
            xc = _lru_conv(x, prev8, cwv, cbv)
            a, u = _lru_gate_math(xc, _dot(xc, wrv) + brv, _dot(xc, wiv) + biv, lamv)
            h = _scan_chunk(a, u, h_c)
            xc_ref[rows, :] = xc
            a_ref[rows, :] = a
            h_ref[rows, :] = h
            o_ref[rows, :] = h * _gelu(gt_ref[rows, :])
            return h[t - 1:t, :], x[t - 8:t, :]

        lax.fori_loop(0, s // t, chunk, (jnp.zeros((1, LANE), F32), jnp.zeros((8, LANE), F32)))

    xr, gt, cws, vec, wbd, col = _lru_specs(s)
    shp = jax.ShapeDtypeStruct((s, LRU_W), F32)
    return pl.pallas_call(
        body, out_shape=(shp,) * 4, grid=(LRU_W // LANE,),
        in_specs=[xr, gt, cws, vec, wbd, vec, wbd, vec, vec], out_specs=(col,) * 4, name=name,
        compiler_params=_cparams(("parallel",)))(proj, proj, cw, cb, wr, br, wi, bi, lam)


def _lru_bwd(proj, dout, xc_all, a_all, h_all, cw, cb, wr, br, wi, bi, lam, name):
    s = proj.shape[0]
    t = SCAN_T
    nc = s // t

    def body(xr_ref, gt_ref, do_ref, xc_s, a_s, h_s, cw_ref, cb_ref, wr_ref, br_ref, wi_ref, bi_ref, lam_ref,
             dxr_ref, dgt_ref, dcw_ref, dcb_ref, dwr_ref, dbr_ref, dwi_ref, dbi_ref, dlam_ref):
        cwv, cbv, lamv = cw_ref[...], cb_ref[...], lam_ref[...]
        wrv, wiv, brv, biv = wr_ref[...], wi_ref[...], br_ref[...], bi_ref[...]
        z1 = jnp.zeros((1, LANE), F32)
        zw = jnp.zeros((LANE, LANE), F32)

        def bchunk(ci, carry):
            g_next, a_next, dxc_next8, dcw, dcb, dwr, dbr, dwi, dbi, dlam = carry
            c = nc - 1 - ci
            t0 = pl.multiple_of(c * t, t)
            rows = pl.ds(t0, t)
            before = pl.ds(pl.multiple_of(jnp.maximum(t0 - 8, 0), 8), 8)
            has_prev = (c > 0).astype(F32)
            x, gt, do = xr_ref[rows, :], gt_ref[rows, :], do_ref[rows, :]
            xc, a, h = xc_s[rows, :], a_s[rows, :], h_s[rows, :]
            prev8_h = h_s[before, :] * has_prev
            dgt_ref[rows, :] = do * h * _gelu_grad(gt)
            dh = do * _gelu(gt)
            a_plus = _shift_up_next(a, 1, jnp.broadcast_to(a_next, (8, LANE)))
            g = _scan_chunk(a_plus, dh, g_next, reverse=True)
            da = g * _shift_down_prev(h, 1, prev8_h)
            pre_r = _dot(xc, wrv) + brv
            pre_i = _dot(xc, wiv) + biv
            _, vjp = jax.vjp(_lru_gate_math, xc, pre_r, pre_i, lamv)
            dxc, dpre_r, dpre_i, dlam_c = vjp((da, g))
            dxc = dxc + _dot_nt(dpre_r, wrv) + _dot_nt(dpre_i, wiv)
            dx = cwv[LRU_CONV - 1:LRU_CONV, :] * dxc
            dcw_rows = [None] * LRU_CONV
            dcw_rows[LRU_CONV - 1] = jnp.sum(dxc * x, axis=0, keepdims=True)
            for k in range(LRU_CONV - 1):
                dxc_ahead = _shift_up_next(dxc, LRU_CONV - 1 - k, dxc_next8)
                dx = dx + cwv[k:k + 1, :] * dxc_ahead
                dcw_rows[k] = jnp.sum(dxc_ahead * x, axis=0, keepdims=True)
            dxr_ref[rows, :] = dx
            return (g[0:1, :], a[0:1, :], dxc[0:8, :],
                    dcw + jnp.concatenate(dcw_rows, axis=0),
                    dcb + jnp.sum(dxc, axis=0, keepdims=True),
                    dwr + _dot_tn(xc, dpre_r), dbr + jnp.sum(dpre_r, axis=0, keepdims=True),
                    dwi + _dot_tn(xc, dpre_i), dbi + jnp.sum(dpre_i, axis=0, keepdims=True),
                    dlam + dlam_c)

        init = (z1, z1, jnp.zeros((8, LANE), F32), jnp.zeros((LRU_CONV, LANE), F32), z1, zw, z1, zw, z1, z1)
        res = lax.fori_loop(0, nc, bchunk, init)
        dcw_ref[...] = res[3]
        dcb_ref[...] = res[4]
        dwr_ref[...] = res[5]
        dbr_ref[...] = res[6]
        dwi_ref[...] = res[7]
        dbi_ref[...] = res[8]
        dlam_ref[...] = res[9]

    xr, gt, cws, vec, wbd, col = _lru_specs(s)
    vshape = jax.ShapeDtypeStruct((1, LRU_W), F32)
    wshape = jax.ShapeDtypeStruct((LRU_W, LRU_W), F32)
    return pl.pallas_call(
        body,
        out_shape=(jax.ShapeDtypeStruct((s, LRU_W), F32), jax.ShapeDtypeStruct((s, LRU_W), F32),
                   jax.ShapeDtypeStruct((LRU_CONV, LRU_W), F32), vshape, wshape, vshape, wshape, vshape, vshape),
        grid=(LRU_W // LANE,),
        in_specs=[xr, gt, col, col, col, col, cws, vec, wbd, vec, wbd, vec, vec],
        out_specs=(col, col, cws, vec, wbd, vec, wbd, vec, vec), name=name,
        compiler_params=_cparams(("parallel",)))(proj, proj, dout, xc_all, a_all, h_all, cw, cb, wr, br, wi, bi,
                                                 lam)


def _s5_disc_math(a_re, a_im, log_step, bt_re, bt_im):
    step = jnp.exp(log_step)
    dt_re, dt_im = step * a_re, step * a_im
    mag = jnp.exp(dt_re)
    ab_re, ab_im = mag * jnp.cos(dt_im), mag * jnp.sin(dt_im)
    z_re, z_im = ab_re - 1.0, ab_im
    den = a_re * a_re + a_im * a_im
    f_re = (z_re * a_re + z_im * a_im) / den
    f_im = (z_im * a_re - z_re * a_im) / den
    bb_re = f_re * bt_re - f_im * bt_im
    bb_im = f_re * bt_im + f_im * bt_re
    return ab_re, ab_im, bb_re, bb_im


def _s5_disc_fwd(a_re, a_im, log_step, bt_re, bt_im, name):
    def body(ar, ai, ls, br, bi, o1, o2, o3, o4):
        r = _s5_disc_math(ar[...], ai[...], ls[...], br[...], bi[...])
        o1[...], o2[...], o3[...], o4[...] = r

    shp = jax.ShapeDtypeStruct(a_re.shape, F32)
    return pl.pallas_call(body, out_shape=(shp,) * 4, name=name)(a_re, a_im, log_step, bt_re, bt_im)


def _s5_disc_bwd(a_re, a_im, log_step, bt_re, bt_im, cts, name):
    def body(ar, ai, ls, br, bi, c1, c2, c3, c4, o1, o2, o3, o4, o5):
        _, vjp = jax.vjp(_s5_disc_math, ar[...], ai[...], ls[...], br[...], bi[...])
        r = vjp((c1[...], c2[...], c3[...], c4[...]))
        o1[...], o2[...], o3[...], o4[...], o5[...] = r

    shp = jax.ShapeDtypeStruct(a_re.shape, F32)
    return pl.pallas_call(body, out_shape=(shp,) * 5, name=name)(a_re, a_im, log_step, bt_re, bt_im, *cts)


def _s5_u_specs(s):
    uo = (3 * ATTN_W + 2 * LRU_W) // LANE
    return (pl.BlockSpec((s, LANE), lambda j: (0, uo)), pl.BlockSpec((s, LANE), lambda j: (0, uo + 1)))


def _s5_scan_fwd(proj, b_re, b_im, lam_re, lam_im, c_re, c_im, name):
    s = proj.shape[0]
    t = SCAN_T

    def body(u0_ref, u1_ref, bre_ref, bim_ref, lre_ref, lim_ref, cre_ref, cim_ref, xre_ref, xim_ref, y_ref):
        @pl.when(pl.program_id(0) == 0)
        def _():
            y_ref[...] = jnp.zeros_like(y_ref)
        lr, li = lre_ref[...], lim_ref[...]
        consts = _cscan_consts(lr, li, False)
        bre, bim, cre, cim = bre_ref[...], bim_ref[...], cre_ref[...], cim_ref[...]

        def chunk(c, carry):
            cr, ci = carry
            rows = pl.ds(pl.multiple_of(c * t, t), t)
            u = jnp.concatenate([u0_ref[rows, :], u1_ref[rows, :]], axis=1).astype(BF16)
            xr, xi = _cscan_chunk(_dot(u, bre), _dot(u, bim), consts, (cr, ci))
            xre_ref[rows, :] = xr
            xim_ref[rows, :] = xi
            y_ref[rows, :] += _dot(xr, cre) - _dot(xi, cim)
            return xr[t - 1:t, :], xi[t - 1:t, :]

        z = jnp.zeros((1, S5_BLK), F32)
        lax.fori_loop(0, s // t, chunk, (z, z))

    u0, u1 = _s5_u_specs(s)
    bsp = pl.BlockSpec((S5_W, S5_BLK), lambda j: (0, j))
    csp = pl.BlockSpec((S5_BLK, S5_W), lambda j: (j, 0))
    vec = pl.BlockSpec((1, S5_BLK), lambda j: (0, j))
    xsp = pl.BlockSpec((s, S5_BLK), lambda j: (0, j))
    ysp = pl.BlockSpec((s, S5_W), lambda j: (0, 0))
    xshape = jax.ShapeDtypeStruct((s, S5_STATES), F32)
    return pl.pallas_call(
        body, out_shape=(xshape, xshape, jax.ShapeDtypeStruct((s, S5_W), F32)),
        grid=(S5_STATES // S5_BLK,), in_specs=[u0, u1, bsp, bsp, vec, vec, csp, csp],
        out_specs=(xsp, xsp, ysp), name=name,
        compiler_params=_cparams(("arbitrary",)))(proj, proj, b_re, b_im, lam_re, lam_im, c_re, c_im)


def _s5_scan_bwd(proj, dy, du_init, x_re, x_im, b_re, b_im, lam_re, lam_im, c_re, c_im, name):
    s = proj.shape[0]
    t = SCAN_T
    nc = s // t

    def body(u0_ref, u1_ref, dy_ref, dui_ref, xre_ref, xim_ref, bre_ref, bim_ref, lre_ref, lim_ref,
             cre_ref, cim_ref, du_ref, dlr_ref, dli_ref, dbr_ref, dbi_ref, dcr_ref, dci_ref):
        @pl.when(pl.program_id(0) == 0)
        def _():
            du_ref[...] = dui_ref[...]
        mr, mi = lre_ref[...], -lim_ref[...]
        consts = _cscan_consts(mr, mi, True)
        bre, bim, cre, cim = bre_ref[...], bim_ref[...], cre_ref[...], cim_ref[...]
        dbr_ref[...] = jnp.zeros_like(dbr_ref)
        dbi_ref[...] = jnp.zeros_like(dbi_ref)
        dcr_ref[...] = jnp.zeros_like(dcr_ref)
        dci_ref[...] = jnp.zeros_like(dci_ref)

        def chunk(ci_, carry):
            gnr, gni, dlr, dli = carry
            c = nc - 1 - ci_
            t0 = pl.multiple_of(c * t, t)
            rows = pl.ds(t0, t)
            before = pl.ds(pl.multiple_of(jnp.maximum(t0 - 8, 0), 8), 8)
            has_prev = (c > 0).astype(F32)
            dyc = dy_ref[rows, :].astype(BF16)
            u = jnp.concatenate([u0_ref[rows, :], u1_ref[rows, :]], axis=1).astype(BF16)
            gr, gi = _cscan_chunk(_dot_nt(dyc, cre), -_dot_nt(dyc, cim), consts, (gnr, gni), reverse=True)
            xr, xi = xre_ref[rows, :], xim_ref[rows, :]
            xpr = _shift_down_prev(xr, 1, xre_ref[before, :] * has_prev)
            xpi = _shift_down_prev(xi, 1, xim_ref[before, :] * has_prev)
            dlr = dlr + jnp.sum(gr * xpr + gi * xpi, axis=0, keepdims=True)
            dli = dli + jnp.sum(gi * xpr - gr * xpi, axis=0, keepdims=True)
            du_ref[rows, :] += _dot_nt(gr, bre) + _dot_nt(gi, bim)
            dbr_ref[...] += _dot_tn(u, gr)
            dbi_ref[...] += _dot_tn(u, gi)
            dcr_ref[...] += _dot_tn(xr, dyc)
            dci_ref[...] -= _dot_tn(xi, dyc)
            return gr[0:1, :], gi[0:1, :], dlr, dli

        z = jnp.zeros((1, S5_BLK), F32)
        res = lax.fori_loop(0, nc, chunk, (z, z, z, z))
        dlr_ref[...] = res[2]
        dli_ref[...] = res[3]

    u0, u1 = _s5_u_specs(s)
    bsp = pl.BlockSpec((S5_W, S5_BLK), lambda j: (0, j))
    csp = pl.BlockSpec((S5_BLK, S5_W), lambda j: (j, 0))
    vec = pl.BlockSpec((1, S5_BLK), lambda j: (0, j))
    xsp = pl.BlockSpec((s, S5_BLK), lambda j: (0, j))
    ysp = pl.BlockSpec((s, S5_W), lambda j: (0, 0))
    return pl.pallas_call(
        body,
        out_shape=(jax.ShapeDtypeStruct((s, S5_W), F32),
                   jax.ShapeDtypeStruct((1, S5_STATES), F32), jax.ShapeDtypeStruct((1, S5_STATES), F32),
                   jax.ShapeDtypeStruct((S5_W, S5_STATES), F32), jax.ShapeDtypeStruct((S5_W, S5_STATES), F32),
                   jax.ShapeDtypeStruct((S5_STATES, S5_W), F32), jax.ShapeDtypeStruct((S5_STATES, S5_W), F32)),
        grid=(S5_STATES // S5_BLK,),
        in_specs=[u0, u1, ysp, ysp, xsp, xsp, bsp, bsp, vec, vec, csp, csp],
        out_specs=(ysp, vec, vec, bsp, bsp, csp, csp), name=name,
        compiler_params=_cparams(("arbitrary",)))(
            proj, proj, dy, du_init, x_re, x_im, b_re, b_im, lam_re, lam_im, c_re, c_im)


def _s5_out_fwd(proj, y_acc, dvec, w_glu, b_glu, name):
    s = proj.shape[0]
    tm = 512
    uo = (3 * ATTN_W + 2 * LRU_W) // LANE

    def body(u0_ref, u1_ref, y_ref, d_ref, w_ref, b_ref, o_ref, yp_ref):
        u = jnp.concatenate([u0_ref[...], u1_ref[...]], axis=1)
        y = y_ref[...] + d_ref[...] * u
        yp_ref[...] = y
        yg = _gelu(y)
        o_ref[...] = yg * _sigmoid(_dot(yg, w_ref[...]) + b_ref[...])

    u0 = pl.BlockSpec((tm, LANE), lambda i: (i, uo))
    u1 = pl.BlockSpec((tm, LANE), lambda i: (i, uo + 1))
    row = pl.BlockSpec((tm, S5_W), lambda i: (i, 0))
    vec = pl.BlockSpec((1, S5_W), lambda i: (0, 0))
    wsp = pl.BlockSpec((S5_W, S5_W), lambda i: (0, 0))
    shp = jax.ShapeDtypeStruct((s, S5_W), F32)
    return pl.pallas_call(
        body, out_shape=(shp, shp), grid=(s // tm,), in_specs=[u0, u1, row, vec, wsp, vec],
        out_specs=(row, row), name=name,
        compiler_params=_cparams(("parallel",)))(proj, proj, y_acc, dvec, w_glu, b_glu)


def _s5_out_bwd(proj, y_pre, dout, dvec, w_glu, b_glu, name, dep=None):
    s = proj.shape[0]
    tm = 512
    uo = (3 * ATTN_W + 2 * LRU_W) // LANE

    def body(u0_ref, u1_ref, y_ref, do_ref, d_ref, w_ref, b_ref, *rest):
        dy_ref, dud_ref, dd_ref, dw_ref, db_ref = rest[-5:]

        @pl.when(pl.program_id(0) == 0)
        def _():
            dd_ref[...] = jnp.zeros_like(dd_ref)
            dw_ref[...] = jnp.zeros_like(dw_ref)
            db_ref[...] = jnp.zeros_like(db_ref)
        u = jnp.concatenate([u0_ref[...], u1_ref[...]], axis=1)
        y = y_ref[...]
        do = do_ref[...]
        yg = _gelu(y)
        sg = _sigmoid(_dot(yg, w_ref[...]) + b_ref[...])
        dz = do * yg * sg * (1.0 - sg)
        dyg = do * sg + _dot_nt(dz, w_ref[...])
        dy = dyg * _gelu_grad(y)
        dy_ref[...] = dy
        dud_ref[...] = d_ref[...] * dy
        dd_ref[...] += jnp.sum(dy * u, axis=0, keepdims=True)
        dw_ref[...] += _dot_tn(yg, dz)
        db_ref[...] += jnp.sum(dz, axis=0, keepdims=True)

    u0 = pl.BlockSpec((tm, LANE), lambda i: (i, uo))
    u1 = pl.BlockSpec((tm, LANE), lambda i: (i, uo + 1))
    row = pl.BlockSpec((tm, S5_W), lambda i: (i, 0))
    vec = pl.BlockSpec((1, S5_W), lambda i: (0, 0))
    wsp = pl.BlockSpec((S5_W, S5_W), lambda i: (0, 0))
    shp = jax.ShapeDtypeStruct((s, S5_W), F32)
    vshape = jax.ShapeDtypeStruct((1, S5_W), F32)
    dep_specs, dep_ops = _dep_args(dep)
    return pl.pallas_call(
        body, out_shape=(shp, shp, vshape, jax.ShapeDtypeStruct((S5_W, S5_W), F32), vshape),
        grid=(s // tm,), in_specs=[u0, u1, row, row, vec, wsp, vec] + dep_specs,
        out_specs=(row, row, vec, wsp, vec), name=name,
        compiler_params=_cparams(("arbitrary",)))(proj, proj, y_pre, dout, dvec, w_glu, b_glu, *dep_ops)


def _ffn_conv(x, prev8, cw, cb):
    y = cb + cw[FFN_CONV - 1:FFN_CONV, :] * x
    for k in range(FFN_CONV - 1):
        y = y + cw[k:k + 1, :] * _shift_down_prev(x, FFN_CONV - 1 - k, prev8)
    return y


def _ffn_up_act(h, wg, cw, cb, name, dep=None):
    s, d = h.shape
    tm = 512
    tb = 2 * FFN_CB
    nt = D_FF // FFN_CB

    def body(h_ref, wgate_ref, wval_ref, cw_ref, cb_ref, *rest):
        up_ref, y_ref, o_ref, ot_ref, carry = rest[-5:]

        @pl.when(pl.program_id(1) == 0)
        def _():
            carry[...] = jnp.zeros_like(carry)
        hb = h_ref[...].astype(BF16)
        x = jnp.concatenate([_dot(hb, wgate_ref[...]), _dot(hb, wval_ref[...])], axis=1)
        up_ref[...] = x.astype(BF16)
        y = _ffn_conv(x, carry[...], cw_ref[...], cb_ref[...])
        y_ref[...] = y
        carry[...] = x[tm - 8:tm, :]
        act = _gelu(y[:, :FFN_CB]) * y[:, FFN_CB:]
        o_ref[...] = act.astype(BF16)
        ot_ref[...] = act.T.astype(BF16)

    dep_specs, dep_ops = _dep_args(dep)
    return pl.pallas_call(
        body, out_shape=(jax.ShapeDtypeStruct((s, 2 * D_FF), BF16), jax.ShapeDtypeStruct((s, 2 * D_FF), F32),
                         jax.ShapeDtypeStruct((s, D_FF), BF16), jax.ShapeDtypeStruct((D_FF, s), BF16)),
        grid=(nt, s // tm),
        in_specs=[pl.BlockSpec((tm, d), lambda t, i: (i, 0)),
                  pl.BlockSpec((None, d, FFN_CB), lambda t, i: (t, 0, 0)),
                  pl.BlockSpec((None, d, FFN_CB), lambda t, i: (t + nt, 0, 0)),
                  pl.BlockSpec((FFN_CONV, tb), lambda t, i: (0, t)),
                  pl.BlockSpec((1, tb), lambda t, i: (0, t))] + dep_specs,
        out_specs=(pl.BlockSpec((tm, tb), lambda t, i: (i, t)), pl.BlockSpec((tm, tb), lambda t, i: (i, t)),
                   pl.BlockSpec((tm, FFN_CB), lambda t, i: (i, t)), pl.BlockSpec((FFN_CB, tm), lambda t, i: (t, i))),
        scratch_shapes=[pltpu.VMEM((8, tb), F32)], name=name,
        compiler_params=_cparams(("parallel", "arbitrary")))(h, wg, wg, cw, cb, *dep_ops)


def _ffn_bwd(up, y_conv, dr, w_down, wg, cw, name):
    s = up.shape[0]
    d = dr.shape[1]
    tm = 512
    tb = 2 * FFN_CB
    nr = s // tm
    nt = D_FF // FFN_CB

    def body(x_ref, y_ref, dr_ref, wd_ref, wgate_ref, wval_ref, cw_ref,
             dup_ref, dh_ref, dcw_ref, dcb_ref, carry):
        i, t = pl.program_id(0), pl.program_id(1)

        @pl.when(i == 0)
        def _():
            carry[t] = jnp.zeros((8, tb), F32)

        @pl.when(t == 0)
        def _():
            dh_ref[...] = ALPHA * dr_ref[...]
        cwv = cw_ref[...]
        x = x_ref[...]
        dact = _dot_nt(dr_ref[...], wd_ref[...])
        gate, val = y_ref[:, :FFN_CB], y_ref[:, FFN_CB:]
        dy = jnp.concatenate([dact * val * _gelu_grad(gate), dact * _gelu(gate)], axis=1)
        next8 = carry[t]
        carry[t] = dy[0:8, :]
        dx = cwv[FFN_CONV - 1:FFN_CONV, :] * dy
        dcw_rows = [None] * FFN_CONV
        dcw_rows[FFN_CONV - 1] = jnp.sum(dy * x, axis=0, keepdims=True)
        for k in range(FFN_CONV - 1):
            dy_ahead = _shift_up_next(dy, FFN_CONV - 1 - k, next8)
            dx = dx + cwv[k:k + 1, :] * dy_ahead
            dcw_rows[k] = jnp.sum(dy_ahead * x, axis=0, keepdims=True)
        dup = dx.astype(BF16)
        dup_ref[...] = dup
        dh_ref[...] += _dot_nt(dup[:, :FFN_CB], wgate_ref[...]) + _dot_nt(dup[:, FFN_CB:], wval_ref[...])
        dcw_ref[...] = jnp.concatenate(dcw_rows, axis=0)
        dcb_ref[...] = jnp.sum(dy, axis=0, keepdims=True)

    row = lambda i: nr - 1 - i
    return pl.pallas_call(
        body, out_shape=(jax.ShapeDtypeStruct((s, 2 * D_FF), BF16), jax.ShapeDtypeStruct((s, d), F32),
                         jax.ShapeDtypeStruct((nr, FFN_CONV, 2 * D_FF), F32),
                         jax.ShapeDtypeStruct((nr, 1, 2 * D_FF), F32)),
        grid=(nr, nt),
        in_specs=[pl.BlockSpec((tm, tb), lambda i, t: (row(i), t)),
                  pl.BlockSpec((tm, tb), lambda i, t: (row(i), t)),
                  pl.BlockSpec((tm, d), lambda i, t: (row(i), 0)),
                  pl.BlockSpec((FFN_CB, d), lambda i, t: (t, 0)),
                  pl.BlockSpec((None, d, FFN_CB), lambda i, t: (t, 0, 0)),
                  pl.BlockSpec((None, d, FFN_CB), lambda i, t: (t + nt, 0, 0)),
                  pl.BlockSpec((FFN_CONV, tb), lambda i, t: (0, t))],
        out_specs=(pl.BlockSpec((tm, tb), lambda i, t: (row(i), t)),
                   pl.BlockSpec((tm, d), lambda i, t: (row(i), 0)),
                   pl.BlockSpec((None, FFN_CONV, tb), lambda i, t: (row(i), 0, t)),
                   pl.BlockSpec((None, 1, tb), lambda i, t: (row(i), 0, t))),
        scratch_shapes=[pltpu.VMEM((nt, 8, tb), F32)], name=name,
        compiler_params=_cparams(("arbitrary", "arbitrary")))(up, y_conv, dr, w_down, wg, wg, cw)


def _sum_partials(ld_ref):
    gg = ld_ref[0].astype(F32)
    for k in range(1, N_DEV):
        gg = gg + ld_ref[k].astype(F32)
    return gg


def _adam_update(w, g, m, v):
    mn = ADAM_B1 * m + (1.0 - ADAM_B1) * g
    vn = ADAM_B2 * v + (1.0 - ADAM_B2) * (g * g)
    m_hat = mn / (1.0 - ADAM_B1 ** ADAM_STEP)
    v_hat = vn / (1.0 - ADAM_B2 ** ADAM_STEP)
    return -ADAM_LR * (m_hat / (jnp.sqrt(v_hat) + ADAM_EPS) + ADAM_WD * w), mn, vn


def _adamw_many(landed, ws, ms, vs, name):
    n, nl = len(ws), len(landed)

    def body(*refs):
        ld = refs[:nl * n]
        w_refs, m_refs, v_refs = (refs[(nl + k) * n:(nl + k + 1) * n] for k in range(3))
        outs = refs[(nl + 3) * n:]
        for i in range(n):
            for l in range(nl):
                one = slice(l, l + 1)
                gg = _sum_partials(ld[l * n + i])
                outs[i][one] = gg
                outs[n + i][one], outs[2 * n + i][one], outs[3 * n + i][one] = _adam_update(
                    w_refs[i][one], gg, m_refs[i][one], v_refs[i][one])

    vm = pl.BlockSpec(memory_space=pltpu.VMEM)
    shapes = [jax.ShapeDtypeStruct(w.shape, F32) for w in ws] * 4
    res = pl.pallas_call(
        body, out_shape=tuple(shapes), in_specs=[vm] * ((nl + 3) * n), out_specs=tuple([vm] * (4 * n)),
        name=name, compiler_params=_cparams())(*[a for layer in landed for a in layer], *ws, *ms, *vs)
    return res[:n], res[n:2 * n], res[2 * n:3 * n], res[3 * n:]


def _adamw_sum(landed, w, m, v, layer, prev, name):
    _, r, c = landed.shape
    nl = w.shape[0]
    tm = 8
    for cand in (512, 256, 128, 64, 32, 16):
        if r % cand == 0 and N_DEV * cand * c * 4 <= 4 * 1024 * 1024:
            tm = cand
            break

    def body(*refs):
        ld_ref, w_ref, m_ref, v_ref = refs[:4]
        g_ref, d_ref, mo_ref, vo_ref = refs[-4:]
        gg = _sum_partials(ld_ref)
        g_ref[...] = gg
        d_ref[...], mo_ref[...], vo_ref[...] = _adam_update(w_ref[...], gg, m_ref[...], v_ref[...])

    blk = pl.BlockSpec((None, tm, c), lambda i: (layer, i, 0))
    in_specs = [pl.BlockSpec((N_DEV, tm, c), lambda i: (0, i, 0)), blk, blk, blk]
    args = [landed, w, m, v]
    aliases = {}
    if prev is not None:
        in_specs += [pl.BlockSpec(memory_space=pl.ANY)] * 4
        args += list(prev)
        aliases = {4 + k: k for k in range(4)}
    shp = jax.ShapeDtypeStruct((nl, r, c), F32)
    return pl.pallas_call(
        body, out_shape=(shp,) * 4, grid=(r // tm,), in_specs=in_specs, out_specs=(blk,) * 4,
        input_output_aliases=aliases, name=name, compiler_params=_cparams(("parallel",)))(*args)


def _all_gather(shards, name):
    na = len(shards)

    def body(*refs):
        x_refs, out_refs = refs[:na], refs[na:2 * na]
        send_sems, recv_sems, local_sems = refs[2 * na:]
        x, y, c = lax.axis_index("x"), lax.axis_index("y"), lax.axis_index("c")
        me, sibling = (x, y, c), (x, y, 1 - c)
        chips = [(1 - x, y), (x, 1 - y), (1 - x, 1 - y)]

        def copy(a, k, block, to, src=None):
            dst = out_refs[a].at[4 * block[0] + 2 * block[1] + block[2]]
            return pltpu.make_async_remote_copy(
                src_ref=dst if src is None else src, dst_ref=dst,
                send_sem=send_sems.at[7 * a + k], recv_sem=recv_sems.at[7 * a + k],
                device_id=to, device_id_type=pl.DeviceIdType.MESH)

        mine, first, passed = [], [], []
        for a in range(na):
            cp = pltpu.make_async_copy(x_refs[a], out_refs[a].at[4 * x + 2 * y + c], local_sems.at[a])
            cp.start()
            mine.append(cp)
            cps = [copy(a, 0, me, sibling, src=x_refs[a])]
            cps += [copy(a, 1 + j, me, (*chip, c), src=x_refs[a]) for j, chip in enumerate(chips)]
            for cp in cps:
                cp.start()
            first += cps
        for j, chip in enumerate(chips):
            for a in range(na):
                copy(a, 1 + j, (*chip, c), me).wait_recv()
                cp = copy(a, 4 + j, (*chip, c), sibling)
                cp.start()
                passed.append(cp)
        for a in range(na):
            copy(a, 0, sibling, me).wait_recv()
            for j, chip in enumerate(chips):
                copy(a, 4 + j, (*chip, 1 - c), me).wait_recv()
        for cp in first + passed:
            cp.wait_send()
        for cp in mine:
            cp.wait()

    anyspec = pl.BlockSpec(memory_space=pl.ANY)
    return pl.pallas_call(
        body, out_shape=tuple(jax.ShapeDtypeStruct((N_DEV,) + t.shape, t.dtype) for t in shards),
        in_specs=[anyspec] * na, out_specs=tuple([anyspec] * na),
        scratch_shapes=[pltpu.SemaphoreType.DMA((7 * na,)), pltpu.SemaphoreType.DMA((7 * na,)),
                        pltpu.SemaphoreType.DMA((na,))],
        name=name)(*shards)


_HBM = pl.BlockSpec(memory_space=pltpu.HBM)
_SEM = pl.BlockSpec(memory_space=pltpu.SEMAPHORE)
_EFFECT = pltpu.SideEffectType.DATAFLOW_SIDE_EFFECTING


def _exchange_copies(src_refs, land_refs, send_sems, recv_sems, local_sems, gather):
    x, y, c = lax.axis_index("x"), lax.axis_index("y"), lax.axis_index("c")
    me = 4 * x + 2 * y + c
    per_array = send_sems.shape[0] > N_DEV - 1
    local, remote = [], []
    for a, (src, land) in enumerate(zip(src_refs, land_refs)):
        local.append(pltpu.make_async_copy(src if gather else src.at[me], land.at[me],
                                           local_sems.at[a if per_array else 0]))
    for k in range(1, N_DEV):
        px = x ^ ((k >> 2) & 1)
        py = y ^ ((k >> 1) & 1)
        pc = c ^ (k & 1)
        for a, (src, land) in enumerate(zip(src_refs, land_refs)):
            remote.append(pltpu.make_async_remote_copy(
                src_ref=src if gather else src.at[4 * px + 2 * py + pc], dst_ref=land.at[me],
                send_sem=send_sems.at[(7 * a if per_array else 0) + k - 1],
                recv_sem=recv_sems.at[(7 * a if per_array else 0) + k - 1],
                device_id=(px, py, pc), device_id_type=pl.DeviceIdType.MESH))
    return local, remote


def _exchange_start(srcs, gather, name, dep=None):
    na = len(srcs)
    ns = na if na <= 4 else 1
    lands = [lax.empty(((N_DEV,) + t.shape) if gather else t.shape, t.dtype) for t in srcs]

    def body(*refs):
        src_refs, land_refs = refs[:na], refs[na:2 * na]
        nin = 2 * na + (0 if dep is None else 1)
        send_sems, recv_sems, local_sems = refs[nin:nin + 3]
        token = refs[-1]
        local, remote = _exchange_copies(src_refs, land_refs, send_sems, recv_sems, local_sems, gather)
        for cp in local + remote:
            cp.start()
        token[...] = jnp.zeros_like(token)

    dep_specs, dep_ops = _dep_args(dep)
    hbm = lambda t: pltpu.HBM(t.shape, t.dtype)
    out = pl.pallas_call(
        body, name=name,
        out_shape=(pltpu.SemaphoreType.DMA((7 * ns,)), pltpu.SemaphoreType.DMA((7 * ns,)),
                   pltpu.SemaphoreType.DMA((ns,)), *[hbm(t) for t in srcs], *[hbm(t) for t in lands],
                   jax.ShapeDtypeStruct((8, LANE), F32)),
        in_specs=[_HBM] * (2 * na) + dep_specs,
        out_specs=(_SEM, _SEM, _SEM, *[_HBM] * (2 * na), pl.BlockSpec(memory_space=pltpu.VMEM)),
        input_output_aliases={i: 3 + i for i in range(2 * na)},
        compiler_params=pltpu.CompilerParams(has_side_effects=_EFFECT),
    )(*[pltpu.with_memory_space_constraint(t, pltpu.HBM) for t in srcs + lands], *dep_ops)
    return (out[:3], out[3:3 + na], out[3 + na:3 + 2 * na]), out[-1]


def _exchange_wait(handle, gather, after, name):
    sems, srcs, lands = handle
    na = len(srcs)

    def body(*refs):
        src_refs, land_refs = refs[:na], refs[na:2 * na]
        send_sems, recv_sems, local_sems = refs[2 * na:2 * na + 3]
        local, remote = _exchange_copies(src_refs, land_refs, send_sems, recv_sems, local_sems, gather)
        for cp in remote:
            cp.wait_send()
            cp.wait_recv()
        for cp in local:
            cp.wait()

    hbm = lambda t: pltpu.HBM(t.shape, t.dtype)
    out = pl.pallas_call(
        body, name=name, out_shape=(*[hbm(t) for t in srcs], *[hbm(t) for t in lands]),
        in_specs=[_HBM] * (2 * na) + [_SEM] * 3 + [pl.BlockSpec(memory_space=pl.ANY)],
        out_specs=tuple([_HBM] * (2 * na)), input_output_aliases={i: i for i in range(2 * na)},
        compiler_params=pltpu.CompilerParams(has_side_effects=_EFFECT),
    )(*srcs, *lands, *sems, after)
    return out[na:]


def _block_diag(w):
    h, a, b = w.shape
    eye = jnp.eye(h, dtype=w.dtype)
    return (w[:, :, None, :] * eye[:, None, :, None]).reshape(h * a, h * b)


def _block_diag_extract(m, h):
    a, b = m.shape[0] // h, m.shape[1] // h
    return jnp.stack([m[i * a:(i + 1) * a, i * b:(i + 1) * b] for i in range(h)], axis=0)


def _block_diag_take(m, h):
    a, b = m.shape[0] // h, m.shape[1] // h
    eye = jnp.eye(h, dtype=m.dtype)
    return (m.reshape(h, a, h, b) * eye[:, None, :, None]).sum(axis=2)


def _ffn_interleave(w):
    lead = w.shape[:-1]
    nb = D_FF // FFN_CB
    return jnp.swapaxes(w.reshape(*lead, 2, nb, FFN_CB), -3, -2).reshape(*lead, 2 * D_FF)


def _ffn_deinterleave(w):
    lead = w.shape[:-1]
    nb = D_FF // FFN_CB
    return jnp.swapaxes(w.reshape(*lead, nb, 2, FFN_CB), -3, -2).reshape(*lead, 2 * D_FF)


def _gather_full(gathered, axis):
    shape = list(gathered.shape[1:])
    shape[axis] *= N_DEV
    return jnp.moveaxis(gathered, 0, axis).reshape(shape)


def _scatter_blocks(full, axis):
    shape = list(full.shape)
    shape[axis:axis + 1] = [N_DEV, shape[axis] // N_DEV]
    return jnp.moveaxis(full.reshape(shape), axis, 0)


def _pad_to(flat, mult):
    pad = (-flat.shape[-1]) % mult
    if pad:
        flat = jnp.concatenate([flat, jnp.zeros(flat.shape[:-1] + (pad,), flat.dtype)], axis=-1)
    return flat


def _layer_fwd(h_in, h_in_t, w, cos, sin, l, dep, get_ffn, target=None):
    tag = "l%d_" % l
    proj, qkv, h_t = _proj_rope(h_in, w['w_in'], cos, sin, tag + "proj_rope", dep=dep,
                                transposed=h_in_t is None)
    h_in_t = h_t if h_in_t is None else h_in_t
    outs, lses = [], []
    for d, qv in zip(DILATIONS, qkv):
        o, ls = _attn_fwd(qv, d, tag + "attn_d%d" % d)
        outs.append(o)
        lses.append(ls)
    lru, *lru_saved = _lru_fwd(proj, w['lru_conv_w'], w['lru_conv_b'], w['lru_wr'], w['lru_br'], w['lru_wi'],
                               w['lru_bi'], w['lru_lambda'], tag + "lru")
    x_re, x_im, y_acc = _s5_scan_fwd(proj, w['s5_bb_re'], w['s5_bb_im'], w['s5_lam_re'], w['s5_lam_im'],
                                     w['s5_cc_re'], w['s5_cc_im'], tag + "s5_scan")
    s5, y_pre = _s5_out_fwd(proj, y_acc, w['s5_d'], w['s5_w_glu'], w['s5_b_glu'], tag + "s5_out")
    w_out = get_ffn(l, s5, 'out')
    if w_out is not None:
        w['w_out'] = w_out
    mixed_t, r1, h1, h1_t, attn_o, attn_lse = _mix_fwd(outs, lses, lru, s5, w['mix_norm_g'], h_in, w['w_out'],
                                                       w['ln1_g'], w['ln1_b'], tag + "mix_out_ln1")
    w['w_up_g'], w['w_down'], ffn_dep = get_ffn(l, h1, 'ffn')
    up, y_conv, act, act_t = _ffn_up_act(h1, w['w_up_g'], w['ffn_conv_w'], w['ffn_conv_b'], tag + "up_act",
                                         dep=ffn_dep)
    r2, out_a, out_b = _proj_ln(act, w['w_down'], h1, w['ln2_g'], w['ln2_b'], tag + "down_ln2", target=target)
    saved = dict(h_in_t=h_in_t, proj=proj, qkv=qkv, lru=lru, lru_saved=lru_saved, x_re=x_re, x_im=x_im,
                 y_pre=y_pre, s5=s5, mixed_t=mixed_t, attn_o=attn_o, attn_lse=attn_lse, r1=r1, h1_t=h1_t, up=up,
                 act_t=act_t, r2=r2, y_conv=y_conv)
    return out_a, out_b, saved


def _layer_bwd_ffn(dh2, sv, w, l, dep=None):
    tag = "l%d_" % l
    g = {}
    dr2, g['ln2_g'], g['ln2_b'] = _ln_bwd(sv['r2'], dh2, w['ln2_g'], tag + "ln2_bwd", dep=dep)
    g['w_down'] = _mm_dw(sv['act_t'], dr2, 1024, D_MODEL, 1024, tag + "down_dw", _grad_dtype(l))
    dup, dh1, dcw_parts, dcb_parts = _ffn_bwd(sv['up'], sv['y_conv'], dr2, w['w_down'], w['w_up_g'],
                                              w['ffn_conv_w'], tag + "ffn_bwd")
    g['ffn_conv_w'] = dcw_parts.sum(axis=0)
    g['ffn_conv_b'] = dcb_parts.sum(axis=0)
    g['w_up_g'] = _mm_up_dw(sv['h1_t'], dup, tag + "up_dw", _grad_dtype(l))
    return dh1, g


def _layer_bwd_mix(dh1, sv, w, cos, sin, l, dep, g_ffn, after_out_grad, after_small_grads, after_in_grad):
    tag = "l%d_" % l
    g = {}
    dr1, g['ln1_g'], g['ln1_b'], d_o, dlru, ds5, g['mix_norm_g'] = _mix_bwd(
        sv['r1'], dh1, w['ln1_g'], w['w_out'], sv['attn_o'][0], sv['lru'], sv['s5'], w['mix_norm_g'],
        tag + "ln1_mix_bwd", dep=dep)
    g['w_out'] = _mm_dw(sv['mixed_t'], dr1, 1024, D_MODEL, 1024, tag + "out_dw", _grad_dtype(l))
    dy, dud, g['s5_d'], g['s5_w_glu'], g['s5_b_glu'] = _s5_out_bwd(
        sv['proj'], sv['y_pre'], ds5, w['s5_d'], w['s5_w_glu'], w['s5_b_glu'], tag + "s5_out_bwd",
        dep=after_out_grad(l, g['w_out']))
    du, g['s5_lam_re'], g['s5_lam_im'], g['s5_bb_re'], g['s5_bb_im'], g['s5_cc_re'], g['s5_cc_im'] = \
        _s5_scan_bwd(sv['proj'], dy, dud, sv['x_re'], sv['x_im'], w['s5_bb_re'], w['s5_bb_im'],
                     w['s5_lam_re'], w['s5_lam_im'], w['s5_cc_re'], w['s5_cc_im'], tag + "s5_scan_bwd")
    (dxr, dgate, g['lru_conv_w'], g['lru_conv_b'], g['lru_wr'], g['lru_br'], g['lru_wi'], g['lru_bi'],
     g['lru_lambda']) = _lru_bwd(sv['proj'], dlru, *sv['lru_saved'], w['lru_conv_w'], w['lru_conv_b'], w['lru_wr'],
                                 w['lru_br'], w['lru_wi'], w['lru_bi'], w['lru_lambda'], tag + "lru_bwd")
    token = after_small_grads(l, _finish_layer_grads({**g_ffn, **g}, w, l))
    dqkv = [_attn_bwd(sv['qkv'][b], sv['attn_o'][b], d_o[b], sv['attn_lse'][b], d, tag + "attn_bwd_d%d" % d,
                      dep=token if b == 0 else None)
            for b, d in enumerate(DILATIONS)]
    dproj = _dproj_assemble(dqkv, dxr, dgate, du, cos, sin, tag + "dproj")
    g_in = _mm_dw(sv['h_in_t'], dproj, 1024, D_IN, 1024, tag + "in_dw", _grad_dtype(l))
    return _mm_nt(dproj, w['w_in'], 512, D_MODEL, tag + "in_dx", add=dr1, add_scale=ALPHA,
                  dep=after_in_grad(l, g_in))


def _s5_rep(a):
    return jnp.repeat(a, S5_C, axis=0)


def _prepare_layer(p, l):
    w = {}
    for n in ('w_in', 'w_out', 's5_w_glu'):
        if n in p:
            w[n] = p[n].astype(BF16)
    w['ffn_conv_w'] = _ffn_interleave(p['ffn_conv_w'])
    w['ffn_conv_b'] = _ffn_interleave(p['ffn_conv_b'])[None, :]
    w['lru_conv_w'] = p['lru_conv_w']
    for n in ('lru_conv_b', 'lru_br', 'lru_bi', 'lru_lambda', 's5_b_glu', 'mix_norm_g',
              'ln1_g', 'ln1_b', 'ln2_g', 'ln2_b'):
        w[n] = p[n][None, :]
    w['lru_wr'] = _block_diag(p['lru_wr']).astype(BF16)
    w['lru_wi'] = _block_diag(p['lru_wi']).astype(BF16)
    w['s5_d'] = p['s5_d'].reshape(1, S5_W)
    disc_in = (_s5_rep(p['s5_a_re']), _s5_rep(p['s5_a_im']),
               _s5_rep(jnp.broadcast_to(p['s5_log_step'][:, None], (S5_G, S5_P))),
               jnp.swapaxes(p['s5_b_re'], 1, 2).reshape(S5_W, S5_P),
               jnp.swapaxes(p['s5_b_im'], 1, 2).reshape(S5_W, S5_P))
    ab_re, ab_im, bb_re, bb_im = _s5_disc_fwd(*disc_in, "l%d_s5_disc" % l)
    w['s5_disc_in'] = disc_in
    w['s5_lam_re'] = ab_re.reshape(S5_G, S5_C, S5_P)[:, 0, :].reshape(1, S5_STATES)
    w['s5_lam_im'] = ab_im.reshape(S5_G, S5_C, S5_P)[:, 0, :].reshape(1, S5_STATES)
    w['s5_bb_re'] = _block_diag(bb_re.reshape(S5_G, S5_C, S5_P)).astype(BF16)
    w['s5_bb_im'] = _block_diag(bb_im.reshape(S5_G, S5_C, S5_P)).astype(BF16)
    w['s5_cc_re'] = _block_diag(jnp.swapaxes(p['s5_c_re'], 1, 2)).astype(BF16)
    w['s5_cc_im'] = _block_diag(jnp.swapaxes(p['s5_c_im'], 1, 2)).astype(BF16)
    return w


def _finish_layer_grads(g, w, l):
    out = {}
    for n in ('s5_w_glu', 'lru_conv_w'):
        out[n] = g[n]
    out['ffn_conv_w'] = _ffn_deinterleave(g['ffn_conv_w'])
    out['ffn_conv_b'] = _ffn_deinterleave(g['ffn_conv_b'])[0]
    for n in ('lru_conv_b', 'lru_br', 'lru_bi', 'lru_lambda', 's5_b_glu', 'mix_norm_g',
              'ln1_g', 'ln1_b', 'ln2_g', 'ln2_b'):
        out[n] = g[n][0]
    out['lru_wr'] = _block_diag_extract(g['lru_wr'], LRU_W // HEAD)
    out['lru_wi'] = _block_diag_extract(g['lru_wi'], LRU_W // HEAD)
    out['s5_d'] = g['s5_d'].reshape(S5_G, S5_C)
    out['s5_c_re'] = jnp.swapaxes(_block_diag_take(g['s5_cc_re'], S5_G), 1, 2)
    out['s5_c_im'] = jnp.swapaxes(_block_diag_take(g['s5_cc_im'], S5_G), 1, 2)
    rep = lambda v: _s5_rep(v.reshape(S5_G, S5_P)) * (1.0 / S5_C)
    cts = (rep(g['s5_lam_re']), rep(g['s5_lam_im']),
           _block_diag_take(g['s5_bb_re'], S5_G).reshape(S5_W, S5_P),
           _block_diag_take(g['s5_bb_im'], S5_G).reshape(S5_W, S5_P))
    da_re, da_im, dls, dbt_re, dbt_im = _s5_disc_bwd(*w['s5_disc_in'], cts, "l%d_s5_disc_bwd" % l)
    out['s5_a_re'] = da_re.reshape(S5_G, S5_C, S5_P).sum(axis=1)
    out['s5_a_im'] = da_im.reshape(S5_G, S5_C, S5_P).sum(axis=1)
    out['s5_log_step'] = dls.reshape(S5_G, S5_C * S5_P).sum(axis=1)
    out['s5_b_re'] = jnp.swapaxes(dbt_re.reshape(S5_G, S5_C, S5_P), 1, 2)
    out['s5_b_im'] = jnp.swapaxes(dbt_im.reshape(S5_G, S5_C, S5_P), 1, 2)
    return out


def _run_step(x, target, get_layer, get_ffn, on_loss, after_ffn_grads, after_out_grad, after_small_grads,
              after_in_grad):
    cos, sin = _rope_tables(x.shape[0])
    h, h_t = x, None
    ws, saved = [], []
    for l in range(DEPTH):
        p, dep = get_layer(l, h)
        ws.append(_prepare_layer(p, l))
        h, h_t, sv = _layer_fwd(h, h_t, ws[l], cos, sin, l, dep, get_ffn, target if l == DEPTH - 1 else None)
        saved.append(sv)
    dh, loss_vec = h, h_t
    on_loss(loss_vec)
    dep = None
    for l in reversed(range(DEPTH)):
        dh1, g = _layer_bwd_ffn(dh, saved[l], ws[l], l, dep)
        dep = after_ffn_grads(l, g)
        dh = _layer_bwd_mix(dh1, saved[l], ws[l], cos, sin, l, dep, g, after_out_grad, after_small_grads,
                            after_in_grad)
        dep = None
    return loss_vec, dh


def _local_step(x, target, layers):
    grads = [{} for _ in range(DEPTH)]

    def ffn(l, after, part):
        if part == 'out':
            return None
        return layers[l]['w_up_g'].astype(BF16), layers[l]['w_down'].astype(BF16), None

    def keep_ffn(l, g):
        grads[l].update(w_up_g=g['w_up_g'], w_down=g['w_down'])

    def keep_small(l, g):
        grads[l].update(g)

    loss, dx = _run_step(x, target, lambda l, h: (layers[l], None), ffn, lambda row: None, keep_ffn,
                         lambda l, g: grads[l].update(w_out=g), keep_small, lambda l, g: grads[l].update(w_in=g))
    return loss[0, 0], dx, grads


def kernel(x, w_in, lru_conv_w, lru_conv_b, lru_wr, lru_br, lru_wi, lru_bi, lru_lambda, s5_a_re, s5_a_im, s5_b_re, s5_b_im, s5_c_re, s5_c_im, s5_d, s5_log_step, s5_w_glu, s5_b_glu, mix_norm_g, w_out, ln1_g, ln1_b, w_up, ffn_conv_w, ffn_conv_b, w_down, ln2_g, ln2_b, loss_target, m_w_in, m_lru_conv_w, m_lru_conv_b, m_lru_wr, m_lru_br, m_lru_wi, m_lru_bi, m_lru_lambda, m_s5_a_re, m_s5_a_im, m_s5_b_re, m_s5_b_im, m_s5_c_re, m_s5_c_im, m_s5_d, m_s5_log_step, m_s5_w_glu, m_s5_b_glu, m_mix_norm_g, m_w_out, m_ln1_g, m_ln1_b, m_w_up, m_ffn_conv_w, m_ffn_conv_b, m_w_down, m_ln2_g, m_ln2_b, v_w_in, v_lru_conv_w, v_lru_conv_b, v_lru_wr, v_lru_br, v_lru_wi, v_lru_bi, v_lru_lambda, v_s5_a_re, v_s5_a_im, v_s5_b_re, v_s5_b_im, v_s5_c_re, v_s5_c_im, v_s5_d, v_s5_log_step, v_s5_w_glu, v_s5_b_glu, v_mix_norm_g, v_w_out, v_ln1_g, v_ln1_b, v_w_up, v_ffn_conv_w, v_ffn_conv_b, v_w_down, v_ln2_g, v_ln2_b):
    args = locals()
    wl = {n: args[n] for n in WEIGHTS}
    ml = {n: args['m_' + n] for n in WEIGHTS}
    vl = {n: args['v_' + n] for n in WEIGHTS}

    small_sizes = [int(wl[n].size) for n in SMALL_SHARDED]
    small_flat = _pad_to(jnp.concatenate([wl[n].reshape(-1) for n in SMALL_SHARDED]), 8 * 1024)
    small_all, w_in0 = _all_gather([small_flat.reshape(-1, 1024), wl['w_in'][0].astype(BF16)], "gather_first")
    small_all = small_all.reshape(N_DEV, -1)
    small_full, off = {}, 0
    for n, sz in zip(SMALL_SHARDED, small_sizes):
        small_full[n] = _gather_full(small_all[:, off:off + sz].reshape((N_DEV,) + wl[n].shape), SHARD_AXIS[n])
        off += sz
    def mixer_params(l, g_in, g_out):
        p = {n: wl[n][l] for n in REPLICATED}
        p.update({n: small_full[n][l] for n in SMALL_SHARDED})
        p['w_in'] = _gather_full(g_in, 1)
        if g_out is not None:
            p['w_out'] = g_out.reshape(D_MODEL, D_MODEL)
        return p

    mix_names, ffn_names = ('w_in', 'w_out'), ('w_up', 'w_down')
    shards = lambda names, l: [wl[n][l].astype(BF16) for n in names]
    gathers = {}
    gathers[0, 'out'], token = _exchange_start(shards(('w_out',), 0), True, "gather_out_l0_start", dep=w_in0)
    gathers[0, 'ffn'], rest0_token = _exchange_start(shards(ffn_names, 0), True, "gather_ffn_l0_start", dep=token)

    def get_layer(l, h):
        if l == 0:
            return mixer_params(0, w_in0, None), rest0_token
        return mixer_params(1, *_exchange_wait(gathers[1, 'mix'], True, h, "gather_mix_l1_wait")), None

    def get_ffn(l, after, part):
        if part == 'out':
            if l > 0:
                return None
            g_out, = _exchange_wait(gathers[0, 'out'], True, after, "gather_out_l0_wait")
            return g_out.reshape(D_MODEL, D_MODEL)
        g_up, g_down = _exchange_wait(gathers[l, 'ffn'], True, after, "gather_ffn_l%d_wait" % l)
        token = None
        if l == 0:
            gathers[1, 'mix'], token = _exchange_start(shards(mix_names, 1), True, "gather_mix_l1_start", dep=g_up)
            gathers[1, 'ffn'], token = _exchange_start(shards(ffn_names, 1), True, "gather_ffn_l1_start", dep=token)
        return g_up, g_down.reshape(D_FF, D_MODEL), token

    scatters = {}

    def after_ffn_grads(l, g):
        send = [g['w_up_g'], g['w_down'].reshape(N_DEV, D_FF // N_DEV, D_MODEL)]
        scatters[l, 'ffn'], token = _exchange_start(send, False, "scatter_ffn_l%d_start" % l)
        return token

    def after_out_grad(l, g_out):
        send = [g_out.reshape(N_DEV, D_MODEL // N_DEV, D_MODEL)]
        scatters[l, 'out'], token = _exchange_start(send, False, "scatter_out_l%d_start" % l)
        return token

    def after_in_grad(l, g_in):
        scatters[l, 'in'], token = _exchange_start([_scatter_blocks(g_in, 1)], False, "scatter_in_l%d_start" % l)
        return token

    def after_small_grads(l, g):
        rep = [g[n][None] for n in REPLICATED]
        if l == DEPTH - 1:
            rep.append(loss_rows[0][None])
        shd = [_scatter_blocks(g[n], SHARD_AXIS[n] - 1)[:, None] for n in SMALL_SHARDED]
        scatters[l, 'rep'], token = _exchange_start(rep, True, "gather_rep_grads_l%d_start" % l)
        scatters[l, 'small'], token = _exchange_start(shd, False, "scatter_small_l%d_start" % l, dep=token)
        return token

    loss_rows = []
    _, grad_x = _run_step(x[0], loss_target[0], get_layer, get_ffn, loss_rows.append, after_ffn_grads,
                          after_out_grad, after_small_grads, after_in_grad)

    results = {}
    big_prev = {n: None for n in BIG}

    def finish_big(l, part, names, after):
        landed = _exchange_wait(scatters[l, part], False, after, "scatter_%s_l%d_wait" % (part, l))
        for n, ld in zip(names, landed):
            big_prev[n] = _adamw_sum(ld, wl[n], ml[n], vl[n], l, big_prev[n], "adamw_%s_l%d" % (n, l))

    for l, part, names in ((1, 'ffn', ffn_names), (1, 'out', ('w_out',)), (1, 'in', ('w_in',)),
                           (0, 'ffn', ffn_names), (0, 'out', ('w_out',))):
        finish_big(l, part, names, grad_x)

    kinds = ('grad', 'delta', 'm', 'v')
    landed = []
    for l in range(DEPTH):
        rep = list(_exchange_wait(scatters[l, 'rep'], True, grad_x, "gather_rep_grads_l%d_wait" % l))
        if l == DEPTH - 1:
            loss = jnp.sum(rep.pop()[:, 0, 0, 0])
        shd = list(_exchange_wait(scatters[l, 'small'], False, grad_x, "scatter_small_l%d_wait" % l))
        landed.append(dict(zip(REPLICATED + SMALL_SHARDED, rep + shd)))
    matrices = ['lru_wr', 'lru_wi', 's5_a_re', 's5_a_im', 's5_c_re', 's5_c_im', 's5_d']
    widest = ['s5_b_re', 's5_b_im']
    vectors = [n for n in REPLICATED + SMALL_SHARDED if n not in matrices + widest]
    last = None
    for tag, names in (("vectors", vectors), ("matrices", matrices), ("s5_b", widest)):
        res = _adamw_many([[landed[l][n] for n in names] for l in range(DEPTH)], [wl[n] for n in names],
                          [ml[n] for n in names], [vl[n] for n in names], "adamw_" + tag)
        for kind, arrs in zip(kinds, res):
            for n, a in zip(names, arrs):
                results[kind, n] = a
        last = res[0][0]
    finish_big(0, 'in', ('w_in',), last)
    for n in BIG:
        results['grad', n], results['delta', n], results['m', n], results['v', n] = big_prev[n]

    out = [loss, grad_x[None]]
    for kind in kinds:
        out.extend(results[kind, n] for n in WEIGHTS)
    return tuple(out)
```

```python
import math

import jax
import jax.numpy as jnp
from jax import lax
from jax.experimental import pallas as pl
from jax.experimental.pallas import tpu as pltpu

F32 = jnp.float32
BF16 = jnp.bfloat16

N_DEV = 8
DEPTH = 2
D_MODEL = 1024
ATTN_W = 384
LRU_W = 384
S5_W = 256
D_IN = 2176
D_FF = 3072
HEAD = 64
ATTN_BLK = 128
ATTN_TILE = 1024
DILATIONS = (1, 4, 16)
S5_G = 16
S5_P = 64
S5_C = 16
S5_STATES = S5_G * S5_P
LRU_C = 8.0
LRU_CONV = 4
FFN_CONV = 3
ROPE_THETA = 10000.0
ALPHA = (2 * DEPTH) ** 0.25
LN_EPS = 1e-5
RMS_EPS = 1e-6
ADAM_LR, ADAM_B1, ADAM_B2, ADAM_EPS, ADAM_WD, ADAM_STEP = 0.001, 0.9, 0.999, 1e-8, 0.01, 10

LANE = 128
SCAN_T = 1024
S5_BLK = 256
FFN_CB = 2 * D_FF // N_DEV
VMEM_LIMIT = 56 * 1024 * 1024

WEIGHTS = ['w_in', 'lru_conv_w', 'lru_conv_b', 'lru_wr', 'lru_br', 'lru_wi', 'lru_bi', 'lru_lambda',
           's5_a_re', 's5_a_im', 's5_b_re', 's5_b_im', 's5_c_re', 's5_c_im', 's5_d', 's5_log_step',
           's5_w_glu', 's5_b_glu', 'mix_norm_g', 'w_out', 'ln1_g', 'ln1_b', 'w_up', 'ffn_conv_w',
           'ffn_conv_b', 'w_down', 'ln2_g', 'ln2_b']
SHARD_AXIS = {'w_in': 2, 'lru_conv_w': 2, 's5_w_glu': 1, 'w_out': 1, 'w_up': 2, 'ffn_conv_w': 2, 'w_down': 1}
BIG = ['w_in', 'w_out', 'w_up', 'w_down']
SMALL_SHARDED = ['lru_conv_w', 'ffn_conv_w', 's5_w_glu']
REPLICATED = [n for n in WEIGHTS if n not in SHARD_AXIS]


def _cparams(sem=None):
    return pltpu.CompilerParams(dimension_semantics=sem, vmem_limit_bytes=VMEM_LIMIT)


def _grad_dtype(l):
    return BF16 if l == 0 else F32


def _ffn_dev(jb):
    return jb // 2 + (N_DEV // 2) * (jb % 2)


def _gelu(x):
    c = math.sqrt(2.0 / math.pi)
    t = jnp.tanh(c * (x + 0.044715 * (x * x * x)))
    return 0.5 * x * (1.0 + t)


def _gelu_grad(x):
    c = math.sqrt(2.0 / math.pi)
    x2 = x * x
    t = jnp.tanh(c * (x + 0.044715 * (x2 * x)))
    return 0.5 * (1.0 + t) + 0.5 * x * (1.0 - t * t) * (c * (1.0 + 3.0 * 0.044715 * x2))


def _sigmoid(x):
    return 1.0 / (1.0 + jnp.exp(-x))


def _log1p(x):
    u = 1.0 + x
    d = u - 1.0
    return jnp.where(d == 0.0, x, jnp.log(u) * (x / jnp.where(d == 0.0, 1.0, d)))


def _softplus(x):
    return jnp.maximum(x, 0.0) + _log1p(jnp.exp(-jnp.abs(x)))


def _expm1(x):
    return jnp.tanh(0.5 * x) * (jnp.exp(x) + 1.0)


def _dot(a, b):
    return jnp.dot(a.astype(BF16), b.astype(BF16), preferred_element_type=F32)


def _dot_nt(a, b):
    return lax.dot_general(a.astype(BF16), b.astype(BF16), (((1,), (1,)), ((), ())),
                           preferred_element_type=F32)


def _dot_tn(a, b):
    return lax.dot_general(a.astype(BF16), b.astype(BF16), (((0,), (0,)), ((), ())),
                           preferred_element_type=F32)


def _rows(shape):
    return lax.broadcasted_iota(jnp.int32, shape, 0)


def _shift_down_prev(x, s, prev8):
    if s == 0:
        return x
    t, l = x.shape
    r = pltpu.roll(x, s, axis=0)
    pr = pltpu.roll(prev8, s, axis=0)
    pad = jnp.concatenate([pr, jnp.zeros((t - 8, l), x.dtype)], axis=0)
    return jnp.where(_rows(x.shape) < s, pad, r)


def _shift_up_next(x, s, next8):
    if s == 0:
        return x
    t, l = x.shape
    r = pltpu.roll(x, t - s, axis=0)
    nx = pltpu.roll(next8, 8 - s, axis=0)
    pad = jnp.concatenate([jnp.zeros((t - 8, l), x.dtype), nx], axis=0)
    return jnp.where(_rows(x.shape) >= t - s, pad, r)


SUB = 8


def _tile_shift(x, s, fill, reverse):
    t = x.shape[0]
    pos = _rows(x.shape) & (SUB - 1)
    if reverse:
        return jnp.where(pos < SUB - s, pltpu.roll(x, t - s, axis=0), fill)
    return jnp.where(pos >= s, pltpu.roll(x, s, axis=0), fill)


def _scan_chunk(a, x, carry, reverse=False):
    s = 1
    while s < SUB:
        x = x + a * _tile_shift(x, s, 0.0, reverse)
        a = a * _tile_shift(a, s, 1.0, reverse)
        s *= 2
    nv = x.shape[0] // SUB
    out = [None] * nv
    for v in (reversed(range(nv)) if reverse else range(nv)):
        rows = slice(v * SUB, (v + 1) * SUB)
        out[v] = x[rows, :] + a[rows, :] * carry
        carry = out[v][0:1, :] if reverse else out[v][SUB - 1:SUB, :]
    return jnp.concatenate(out, axis=0)


def _cmul(ar, ai, br, bi):
    return ar * br - ai * bi, ar * bi + ai * br


def _cscan_consts(lr, li, reverse):
    pows = [(lr, li)]
    for _ in range(2):
        pows.append(_cmul(*pows[-1], *pows[-1]))
    rows = [(lr, li)]
    for _ in range(SUB - 1):
        rows.append(_cmul(*rows[-1], lr, li))
    if reverse:
        rows = rows[::-1]
    return pows, (jnp.concatenate([r for r, _ in rows], axis=0), jnp.concatenate([i for _, i in rows], axis=0))


def _cscan_chunk(xr, xi, consts, carry, reverse=False):
    pows, (p8r, p8i) = consts
    s = 1
    for pr, pi in pows:
        sr = _tile_shift(xr, s, 0.0, reverse)
        si = _tile_shift(xi, s, 0.0, reverse)
        xr, xi = xr + pr * sr - pi * si, xi + pr * si + pi * sr
        s *= 2
    nv = xr.shape[0] // SUB
    out_r, out_i = [None] * nv, [None] * nv
    cr, ci = carry
    for v in (reversed(range(nv)) if reverse else range(nv)):
        rows = slice(v * SUB, (v + 1) * SUB)
        out_r[v] = xr[rows, :] + p8r * cr - p8i * ci
        out_i[v] = xi[rows, :] + p8r * ci + p8i * cr
        edge = slice(0, 1) if reverse else slice(SUB - 1, SUB)
        cr, ci = out_r[v][edge, :], out_i[v][edge, :]
    return jnp.concatenate(out_r, axis=0), jnp.concatenate(out_i, axis=0)


def _dep_args(dep):
    return ([], []) if dep is None else ([pl.BlockSpec(memory_space=pl.ANY)], [dep])


def _mm_nt(a, w, tm, tn, name, add=None, add_scale=1.0, dep=None):
    m, k = a.shape
    n = w.shape[0]

    def body(a_ref, w_ref, *rest):
        o_ref = rest[-1]
        if add is None:
            o_ref[...] = _dot_nt(a_ref[...], w_ref[...])
        else:
            o_ref[...] = _dot_nt(a_ref[...], w_ref[...]) + add_scale * rest[0][...]

    in_specs = [pl.BlockSpec((tm, k), lambda j, i: (i, 0)), pl.BlockSpec((tn, k), lambda j, i: (j, 0))]
    args = [a, w]
    if add is not None:
        in_specs.append(pl.BlockSpec((tm, tn), lambda j, i: (i, j)))
        args.append(add)
    dep_specs, dep_ops = _dep_args(dep)
    return pl.pallas_call(
        body, out_shape=jax.ShapeDtypeStruct((m, n), F32), grid=(n // tn, m // tm),
        in_specs=in_specs + dep_specs, out_specs=pl.BlockSpec((tm, tn), lambda j, i: (i, j)), name=name,
        compiler_params=_cparams(("parallel", "parallel")))(*args, *dep_ops)


def _mm_dw(at, b, tm, tn, ts, name, out_dtype=F32):
    m, s = at.shape
    n = b.shape[1]
    nk = s // ts

    def body(a_ref, b_ref, o_ref, acc):
        @pl.when(pl.program_id(2) == 0)
        def _():
            acc[...] = jnp.zeros_like(acc)
        acc[...] += _dot(a_ref[...], b_ref[...])

        @pl.when(pl.program_id(2) == nk - 1)
        def _():
            o_ref[...] = acc[...].astype(out_dtype)

    return pl.pallas_call(
        body, out_shape=jax.ShapeDtypeStruct((m, n), out_dtype), grid=(m // tm, n // tn, nk),
        in_specs=[pl.BlockSpec((tm, ts), lambda i, j, k: (i, k)), pl.BlockSpec((ts, tn), lambda i, j, k: (k, j))],
        out_specs=pl.BlockSpec((tm, tn), lambda i, j, k: (i, j)),
        scratch_shapes=[pltpu.VMEM((tm, tn), F32)], name=name,
        compiler_params=_cparams(("parallel", "parallel", "arbitrary")))(at, b)


def _mm_up_dw(ht, dup, name, out_dtype=F32):
    d, s = ht.shape

    def body(a_ref, b_ref, o_ref):
        o_ref[...] = _dot(a_ref[...], b_ref[...]).astype(out_dtype)

    return pl.pallas_call(
        body, out_shape=jax.ShapeDtypeStruct((N_DEV, d, FFN_CB), out_dtype), grid=(N_DEV,),
        in_specs=[pl.BlockSpec((d, s), lambda j: (0, 0)), pl.BlockSpec((s, FFN_CB), lambda j: (0, j))],
        out_specs=pl.BlockSpec((None, d, FFN_CB), lambda j: (_ffn_dev(j), 0, 0)), name=name,
        compiler_params=_cparams(("parallel",)))(ht, dup)


def _layer_norm(r, g, b):
    mu = jnp.mean(r, axis=-1, keepdims=True)
    xc = r - mu
    var = jnp.mean(xc * xc, axis=-1, keepdims=True)
    return xc * lax.rsqrt(var + LN_EPS) * g + b


def _proj_ln(a, w, resid, g, bias, name, transposed=True, target=None):
    s, k = a.shape
    d = w.shape[1]
    tm = 512

    def body(a_ref, w_ref, x_ref, g_ref, bias_ref, *rest):
        r = ALPHA * x_ref[...] + _dot(a_ref[...], w_ref[...])
        h = _layer_norm(r, g_ref[...], bias_ref[...])
        if target is None:
            r_ref, h_ref = rest[0], rest[1]
            h_ref[...] = h
            if transposed:
                rest[2][...] = h.T.astype(BF16)
        else:
            t_ref, r_ref, dy_ref, l_ref = rest

            @pl.when(pl.program_id(0) == 0)
            def _():
                l_ref[...] = jnp.zeros_like(l_ref)
            e = h - t_ref[...]
            dy_ref[...] = e * (1.0 / d)
            part = 0.5 * jnp.sum(jnp.mean(e * e, axis=-1, keepdims=True), axis=0, keepdims=True)
            l_ref[...] += jnp.broadcast_to(part, l_ref.shape)
        r_ref[...] = r

    row = pl.BlockSpec((tm, d), lambda i: (i, 0))
    vec = pl.BlockSpec((1, d), lambda i: (0, 0))
    in_specs = [pl.BlockSpec((tm, k), lambda i: (i, 0)), pl.BlockSpec((k, d), lambda i: (0, 0)), row, vec, vec]
    args = [a, w, resid, g, bias]
    shapes = [jax.ShapeDtypeStruct((s, d), F32), jax.ShapeDtypeStruct((s, d), F32)]
    specs = [row, row]
    if target is not None:
        in_specs.append(row)
        args.append(target)
        shapes.append(jax.ShapeDtypeStruct((1, LANE), F32))
        specs.append(pl.BlockSpec((1, LANE), lambda i: (0, 0)))
    elif transposed:
        shapes.append(jax.ShapeDtypeStruct((d, s), BF16))
        specs.append(pl.BlockSpec((d, tm), lambda i: (0, i)))
    return pl.pallas_call(
        body, out_shape=tuple(shapes), grid=(s // tm,), in_specs=in_specs, out_specs=tuple(specs), name=name,
        compiler_params=_cparams(("arbitrary",) if target is not None else ("parallel",)))(*args)


def _layer_norm_bwd(r, dh, g):
    mu = jnp.mean(r, axis=-1, keepdims=True)
    xc = r - mu
    var = jnp.mean(xc * xc, axis=-1, keepdims=True)
    rstd = lax.rsqrt(var + LN_EPS)
    xh = xc * rstd
    dxh = dh * g
    m1 = jnp.mean(dxh, axis=-1, keepdims=True)
    m2 = jnp.mean(dxh * xh, axis=-1, keepdims=True)
    return (rstd * (dxh - m1 - xh * m2), jnp.sum(dh * xh, axis=0, keepdims=True),
            jnp.sum(dh, axis=0, keepdims=True))


def _ln_bwd(r, dh, g, name, dep=None):
    s, d = r.shape
    tm = 512

    def body(r_ref, dh_ref, g_ref, *rest):
        dr_ref, dg_ref, db_ref = rest[-3:]

        @pl.when(pl.program_id(0) == 0)
        def _():
            dg_ref[...] = jnp.zeros_like(dg_ref)
            db_ref[...] = jnp.zeros_like(db_ref)
        dr_ref[...], dg_rows, db_rows = _layer_norm_bwd(r_ref[...], dh_ref[...], g_ref[...])
        dg_ref[...] += dg_rows
        db_ref[...] += db_rows

    row = pl.BlockSpec((tm, d), lambda i: (i, 0))
    vec = pl.BlockSpec((1, d), lambda i: (0, 0))
    dep_specs, dep_ops = _dep_args(dep)
    return pl.pallas_call(
        body, out_shape=(jax.ShapeDtypeStruct((s, d), F32), jax.ShapeDtypeStruct((1, d), F32),
                         jax.ShapeDtypeStruct((1, d), F32)),
        grid=(s // tm,), in_specs=[row, row, vec] + dep_specs, out_specs=(row, vec, vec), name=name,
        compiler_params=_cparams(("arbitrary",)))(r, dh, g, *dep_ops)


def _rope_tables(s):
    half = HEAD // 2
    pos = jnp.arange(s, dtype=F32)
    inv = ROPE_THETA ** (-jnp.arange(half, dtype=F32) * 2.0 / HEAD)
    ang = pos[:, None] * inv[None, :]
    cos, sin = jnp.cos(ang), jnp.sin(ang)
    cos = jnp.concatenate([cos, cos, cos, cos], axis=1)
    sin = jnp.concatenate([-sin, sin, -sin, sin], axis=1)
    return cos, sin


def _rotate(x, cos, sin):
    lane = lax.broadcasted_iota(jnp.int32, x.shape, 1)
    partner = jnp.where((lane % HEAD) < HEAD // 2, pltpu.roll(x, LANE - HEAD // 2, axis=1),
                        pltpu.roll(x, HEAD // 2, axis=1))
    return x * cos + partner * sin


def _class_rows(c, d, tm):
    return pl.ds(c, tm // d, stride=d) if d > 1 else pl.ds(0, tm)


def _dilated_spec(tm, d, w):
    return pl.BlockSpec((tm // d, d * w), lambda i: (i, 0))


def _token_scratch(tm, w):
    return pltpu.VMEM((w // LANE, tm, LANE), F32)


def _to_tokens(src_ref, dst3, d, tm):
    nj = dst3.shape[0]
    for cls in range(d):
        for j in range(nj):
            col = (cls * nj + j) * LANE
            dst3.at[j][_class_rows(cls, d, tm), :] = src_ref[:, col:col + LANE]


def _to_dilated(src3, dst_ref, d, tm):
    nj = src3.shape[0]
    for cls in range(d):
        for j in range(nj):
            col = (cls * nj + j) * LANE
            dst_ref[:, col:col + LANE] = src3.at[j][_class_rows(cls, d, tm), :].astype(dst_ref.dtype)


def _token_value(src3):
    return jnp.concatenate([src3[j] for j in range(src3.shape[0])], axis=1)


def _proj_rope(h, w_in, cos, sin, name, dep=None, transposed=False):
    s, d_model = h.shape
    tm = 512
    w = 3 * ATTN_W
    nj = w // LANE

    def body(h_ref, w_ref, c_ref, s_ref, *rest):
        rot = rest[-1]
        if transposed:
            p_ref, o_refs, ht_ref = rest[-6], rest[-5:-2], rest[-2]
            ht_ref[...] = h_ref[...].T.astype(BF16)
        else:
            p_ref, o_refs = rest[-5], rest[-4:-1]
        y = _dot(h_ref[...], w_ref[...])
        p_ref[...] = y
        c, sn = c_ref[...], s_ref[...]
        for j in range(nj):
            x = y[:, j * LANE:(j + 1) * LANE]
            rot[j] = _rotate(x, c, sn) if j < 2 * ATTN_W // LANE else x
        for d, o_ref in zip(DILATIONS, o_refs):
            _to_dilated(rot, o_ref, d, tm)

    tab = pl.BlockSpec((tm, LANE), lambda i: (i, 0))
    dep_specs, dep_ops = _dep_args(dep)
    shapes = [jax.ShapeDtypeStruct((s, D_IN), F32), *[jax.ShapeDtypeStruct((s // d, d * w), BF16) for d in DILATIONS]]
    specs = [pl.BlockSpec((tm, D_IN), lambda i: (i, 0)), *[_dilated_spec(tm, d, w) for d in DILATIONS]]
    if transposed:
        shapes.append(jax.ShapeDtypeStruct((d_model, s), BF16))
        specs.append(pl.BlockSpec((d_model, tm), lambda i: (0, i)))
    res = pl.pallas_call(
        body, out_shape=tuple(shapes), grid=(s // tm,),
        in_specs=[pl.BlockSpec((tm, d_model), lambda i: (i, 0)), pl.BlockSpec((d_model, D_IN), lambda i: (0, 0)),
                  tab, tab] + dep_specs,
        out_specs=tuple(specs), scratch_shapes=[_token_scratch(tm, w)], name=name,
        compiler_params=_cparams(("parallel",)))(h, w_in, cos, sin, *dep_ops)
    return res[0], res[1:4], (res[4] if transposed else None)


def _dproj_assemble(dqkv_list, dxr, dgate, du, cos, sin, name):
    s = dxr.shape[0]
    tm = 512
    nq = 3 * ATTN_W // LANE

    def body(*refs):
        br = refs[:9]
        dxr_ref, dg_ref, du_ref, c_ref, s_ref, o_ref = refs[9:15]
        tok = refs[15:]
        c, sn = c_ref[...], -s_ref[...]
        for part in range(3):
            for b, d in enumerate(DILATIONS[1:], start=1):
                _to_tokens(br[3 * b + part], tok[2 * part + b - 1], d, tm)
        for j in range(nq):
            part, jj = divmod(j, ATTN_W // LANE)
            x = br[part][:, jj * LANE:(jj + 1) * LANE] + tok[2 * part][jj] + tok[2 * part + 1][jj]
            if part < 2:
                x = _rotate(x, c, sn)
            o_ref[:, j * LANE:(j + 1) * LANE] = x.astype(BF16)
        o_ref[:, 3 * ATTN_W:3 * ATTN_W + LRU_W] = dxr_ref[...].astype(BF16)
        o_ref[:, 3 * ATTN_W + LRU_W:3 * ATTN_W + 2 * LRU_W] = dg_ref[...].astype(BF16)
        o_ref[:, 3 * ATTN_W + 2 * LRU_W:] = du_ref[...].astype(BF16)

    a_spec = pl.BlockSpec((tm, ATTN_W), lambda i: (i, 0))
    tab = pl.BlockSpec((tm, LANE), lambda i: (i, 0))
    ordered = [dqkv_list[b][p] for b in range(3) for p in range(3)]
    d_specs = [_dilated_spec(tm, d, ATTN_W) for d in DILATIONS for _ in range(3)]
    return pl.pallas_call(
        body, out_shape=jax.ShapeDtypeStruct((s, D_IN), BF16), grid=(s // tm,),
        in_specs=d_specs + [a_spec, a_spec, pl.BlockSpec((tm, S5_W), lambda i: (i, 0)), tab, tab],
        out_specs=pl.BlockSpec((tm, D_IN), lambda i: (i, 0)),
        scratch_shapes=[_token_scratch(tm, ATTN_W)] * 6, name=name,
        compiler_params=_cparams(("parallel",)))(*ordered, dxr, dgate, du, cos, sin)


def _attn_tiles(s, d):
    m = s // d
    tq = min(m, ATTN_TILE)
    return m, tq, tq // ATTN_BLK


def _band_mask(qb):
    qi = lax.broadcasted_iota(jnp.int32, (ATTN_BLK, 2 * ATTN_BLK), 0)
    ki = lax.broadcasted_iota(jnp.int32, (ATTN_BLK, 2 * ATTN_BLK), 1)
    dist = qi + ATTN_BLK - ki
    return (dist >= 0) & (dist <= ATTN_BLK) & ((ki >= ATTN_BLK) | (qb > 0))


def _attn_fwd(qv, d, name):
    m = qv.shape[0]
    w3 = 3 * ATTN_W
    _, tq, n = _attn_tiles(m * d, d)
    scale = HEAD ** -0.5

    def body(x_ref, p_ref, o_ref, l_ref):
        b = pl.program_id(1)

        def block(i, first):
            r0 = 0 if first else pl.multiple_of(i * ATTN_BLK, ATTN_BLK)
            rows = pl.ds(r0, ATTN_BLK)
            valid = _band_mask(b * n + i)
            if not first:
                krows = pl.ds(pl.multiple_of(i * ATTN_BLK - ATTN_BLK, ATTN_BLK), 2 * ATTN_BLK)
            low = lax.broadcasted_iota(jnp.int32, (1, LANE), 1) < HEAD
            for hp in range(ATTN_W // LANE):
                qs, ks, vs = (slice(part * ATTN_W + hp * LANE, part * ATTN_W + (hp + 1) * LANE) for part in range(3))
                q2 = x_ref[rows, qs]
                if first:
                    k2 = jnp.concatenate([p_ref[:, ks], x_ref[0:ATTN_BLK, ks]], axis=0)
                    v2 = jnp.concatenate([p_ref[:, vs], x_ref[0:ATTN_BLK, vs]], axis=0)
                else:
                    k2 = x_ref[krows, ks]
                    v2 = x_ref[krows, vs]
                outs, lses = [], []
                for mask in (low, ~low):
                    q = jnp.where(mask, q2, jnp.zeros_like(q2))
                    sc = jnp.where(valid, _dot_nt(q, k2) * scale, -1e30)
                    mx = jnp.max(sc, axis=-1, keepdims=True)
                    p = jnp.exp(sc - mx)
                    l = jnp.sum(p, axis=-1, keepdims=True)
                    outs.append(_dot(p, v2) / l)
                    lses.append(mx + jnp.log(l))
                o_ref[rows, hp * LANE:(hp + 1) * LANE] = jnp.where(low, outs[0], outs[1])
                l_ref[rows, hp * LANE:(hp + 1) * LANE] = jnp.where(low, lses[0], lses[1])

        block(0, True)
        if n > 1:
            def loop(i, carry):
                block(i, False)
                return carry
            lax.fori_loop(1, n, loop, 0, unroll=2)

    shp = jax.ShapeDtypeStruct((m, d * ATTN_W), F32)
    ospec =pl.BlockSpec((tq, ATTN_W), lambda c, b: (b, c))
    out, lse = pl.pallas_call(
        body, out_shape=(shp, shp), grid=(d, m // tq),
        in_specs=[pl.BlockSpec((tq, w3), lambda c, b: (b, c)),
                  pl.BlockSpec((ATTN_BLK, w3), lambda c, b: (jnp.maximum(b * n - 1, 0), c))],
        out_specs=(ospec, ospec), name=name,
        compiler_params=_cparams(("parallel", "parallel")))(qv, qv)
    return out, lse


def _attn_bwd(qv, ov, dov, lv, d, name, dep=None):
    m = qv.shape[0]
    w3 = 3 * ATTN_W
    _, tq, n = _attn_tiles(m * d, d)
    nb = m // ATTN_BLK
    scale = HEAD ** -0.5

    def body(x_ref, p_ref, nx_ref, o_ref, do_ref, l_ref, on_ref, don_ref, ln_ref, *rest):
        dq_ref, dk_ref, dv_ref = rest[-3:]
        b = pl.program_id(1)
        dk_ref[...] = jnp.zeros_like(dk_ref)
        dv_ref[...] = jnp.zeros_like(dv_ref)

        low = lax.broadcasted_iota(jnp.int32, (1, LANE), 1) < HEAD

        def pair_grads(q2, k2, v2, o2, do2, l2, valid):
            dq, dk, dv = [], 0.0, 0.0
            for mask, lse in ((low, l2[:, 0:1]), (~low, l2[:, HEAD:HEAD + 1])):
                q = jnp.where(mask, q2, jnp.zeros_like(q2))
                do = jnp.where(mask, do2, 0.0)
                sc = jnp.where(valid, _dot_nt(q, k2) * scale, -1e30)
                p = jnp.exp(sc - lse)
                delta = jnp.sum(do * o2, axis=-1, keepdims=True)
                ds = p * (_dot_nt(do, v2) - delta) * scale
                dq.append(_dot(ds, k2))
                dk = dk + _dot_tn(ds, q)
                dv = dv + _dot_tn(p, do)
            return jnp.where(low, dq[0], dq[1]), dk, dv

        def cols(hp):
            return [slice(part * ATTN_W + hp * LANE, part * ATTN_W + (hp + 1) * LANE) for part in range(3)]

        def block(i, first):
            r0 = 0 if first else pl.multiple_of(i * ATTN_BLK, ATTN_BLK)
            rows = pl.ds(r0, ATTN_BLK)
            valid = _band_mask(b * n + i)
            if not first:
                krows = pl.ds(pl.multiple_of(i * ATTN_BLK - ATTN_BLK, ATTN_BLK), 2 * ATTN_BLK)
            for hp in range(ATTN_W // LANE):
                qs, ks, vs = cols(hp)
                if first:
                    k2 = jnp.concatenate([p_ref[:, ks], x_ref[0:ATTN_BLK, ks]], axis=0)
                    v2 = jnp.concatenate([p_ref[:, vs], x_ref[0:ATTN_BLK, vs]], axis=0)
                else:
                    k2 = x_ref[krows, ks]
                    v2 = x_ref[krows, vs]
                dq, dk, dv = pair_grads(x_ref[rows, qs], k2, v2, o_ref[rows, qs], do_ref[rows, qs],
                                        l_ref[rows, qs], valid)
                dq_ref[rows, qs] = dq
                if first:
                    dk_ref[0:ATTN_BLK, qs] += dk[ATTN_BLK:, :]
                    dv_ref[0:ATTN_BLK, qs] += dv[ATTN_BLK:, :]
                else:
                    dk_ref[krows, qs] += dk
                    dv_ref[krows, qs] += dv

        block(0, True)
        if n > 1:
            def loop(i, carry):
                block(i, False)
                return carry
            lax.fori_loop(1, n, loop, 0, unroll=2)

        last = slice((n - 1) * ATTN_BLK, n * ATTN_BLK)
        qi = lax.broadcasted_iota(jnp.int32, (ATTN_BLK, ATTN_BLK), 0)
        ki = lax.broadcasted_iota(jnp.int32, (ATTN_BLK, ATTN_BLK), 1)
        valid_next = (qi <= ki) & ((b + 1) * n < nb)
        for hp in range(ATTN_W // LANE):
            qs, ks, vs = cols(hp)
            _, dk, dv = pair_grads(nx_ref[:, qs], x_ref[last, ks], x_ref[last, vs], on_ref[:, qs], don_ref[:, qs],
                                   ln_ref[:, qs], valid_next)
            dk_ref[last, qs] += dk
            dv_ref[last, qs] += dv

    nxt = lambda b: jnp.minimum((b + 1) * n, nb - 1)
    xs = pl.BlockSpec((tq, w3), lambda c, b: (b, c))
    xp = pl.BlockSpec((ATTN_BLK, w3), lambda c, b: (jnp.maximum(b * n - 1, 0), c))
    xn = pl.BlockSpec((ATTN_BLK, w3), lambda c, b: (nxt(b), c))
    a = pl.BlockSpec((tq, ATTN_W), lambda c, b: (b, c))
    an = pl.BlockSpec((ATTN_BLK, ATTN_W), lambda c, b: (nxt(b), c))
    shp = jax.ShapeDtypeStruct((m, d * ATTN_W), F32)
    dep_specs, dep_ops = _dep_args(dep)
    return pl.pallas_call(
        body, out_shape=(shp, shp, shp), grid=(d, m // tq),
        in_specs=[xs, xp, xn, a, a, a, an, an, an] + dep_specs, out_specs=(a, a, a), name=name,
        compiler_params=_cparams(("parallel", "parallel")))(qv, qv, qv, ov, dov, lv, ov, dov, lv, *dep_ops)


def _rms(x, g):
    ms = jnp.mean(x * x, axis=-1, keepdims=True)
    return x * lax.rsqrt(ms + RMS_EPS) * g


def _rms_bwd(x, g, dy):
    ms = jnp.mean(x * x, axis=-1, keepdims=True)
    r = lax.rsqrt(ms + RMS_EPS)
    dyg = dy * g
    dx = r * dyg - x * (r * r * r) * jnp.mean(x * dyg, axis=-1, keepdims=True)
    return dx, dy * x * r


def _mix_fwd(outs, lses, lru, s5, g, h_in, w_out, ln_g, ln_b, name):
    s = lru.shape[0]
    tm = 512

    def body(o1, o2, o3, l1, l2, l3, lru_ref, s5_ref, g_ref, x_ref, w_ref, lg_ref, lb_ref,
             mixed_t_ref, r_ref, h_ref, ht_ref, ov1, ov2, ov3, lv1, lv2, lv3, so2, so3, sl2, sl3):
        for d, src, dst in ((DILATIONS[1], o2, so2), (DILATIONS[2], o3, so3),
                            (DILATIONS[1], l2, sl2), (DILATIONS[2], l3, sl3)):
            _to_tokens(src, dst, d, tm)
        a1, a2, a3 = l1[...], _token_value(sl2), _token_value(sl3)
        mx = jnp.maximum(jnp.maximum(a1, a2), a3)
        e1, e2, e3 = jnp.exp(a1 - mx), jnp.exp(a2 - mx), jnp.exp(a3 - mx)
        den = e1 + e2 + e3
        o = (e1 * o1[...] + e2 * _token_value(so2) + e3 * _token_value(so3)) / den
        lse = mx + jnp.log(den)
        ov1[...] = o
        lv1[...] = lse
        for j in range(ATTN_W // LANE):
            so2[j] = o[:, j * LANE:(j + 1) * LANE]
            sl2[j] = lse[:, j * LANE:(j + 1) * LANE]
        for d, o_dst, l_dst in ((DILATIONS[1], ov2, lv2), (DILATIONS[2], ov3, lv3)):
            _to_dilated(so2, o_dst, d, tm)
            _to_dilated(sl2, l_dst, d, tm)
        gg = g_ref[...]
        mixed = jnp.concatenate([_rms(o, gg[:, :ATTN_W]),
                                 _rms(lru_ref[...], gg[:, ATTN_W:ATTN_W + LRU_W]),
                                 _rms(s5_ref[...], gg[:, ATTN_W + LRU_W:])], axis=1)
        mixed_t_ref[...] = mixed.T.astype(BF16)
        r = ALPHA * x_ref[...] + _dot(mixed, w_ref[...])
        h = _layer_norm(r, lg_ref[...], lb_ref[...])
        r_ref[...] = r
        h_ref[...] = h
        ht_ref[...] = h.T.astype(BF16)

    a = pl.BlockSpec((tm, ATTN_W), lambda i: (i, 0))
    s5s = pl.BlockSpec((tm, S5_W), lambda i: (i, 0))
    full = pl.BlockSpec((tm, D_MODEL), lambda i: (i, 0))
    vec = pl.BlockSpec((1, D_MODEL), lambda i: (0, 0))
    dil = [_dilated_spec(tm, d, ATTN_W) for d in DILATIONS]
    dshape = [jax.ShapeDtypeStruct((s // d, d * ATTN_W), F32) for d in DILATIONS]
    tshape = jax.ShapeDtypeStruct((D_MODEL, s), BF16)
    fshape = jax.ShapeDtypeStruct((s, D_MODEL), F32)
    tspec = pl.BlockSpec((D_MODEL, tm), lambda i: (0, i))
    res = pl.pallas_call(
        body, out_shape=(tshape, fshape, fshape, tshape, *dshape, *dshape),
        grid=(s // tm,),
        in_specs=dil + dil + [a, s5s, vec, full, pl.BlockSpec((D_MODEL, D_MODEL), lambda i: (0, 0)), vec, vec],
        out_specs=(tspec, full, full, tspec, *dil, *dil),
        scratch_shapes=[_token_scratch(tm, ATTN_W)] * 4, name=name,
        compiler_params=_cparams(("parallel",)))(*outs, *lses, lru, s5, g, h_in, w_out, ln_g, ln_b)
    return res[0], res[1], res[2], res[3], res[4:7], res[7:10]


def _mix_bwd(r, dh, ln_g, w_out, o, lru, s5, g, name, dep=None):
    s = lru.shape[0]
    tm = 512

    def body(r_ref, dh_ref, lg_ref, w_ref, o_ref, lru_ref, s5_ref, g_ref, *rest):
        dr_ref, dlg_ref, dlb_ref, do_ref, do2_ref, do3_ref, dlru_ref, ds5_ref, dg_ref, stage = rest[-10:]

        @pl.when(pl.program_id(0) == 0)
        def _():
            dg_ref[...] = jnp.zeros_like(dg_ref)
            dlg_ref[...] = jnp.zeros_like(dlg_ref)
            dlb_ref[...] = jnp.zeros_like(dlb_ref)
        gg = g_ref[...]
        dr, dlg_rows, dlb_rows = _layer_norm_bwd(r_ref[...], dh_ref[...], lg_ref[...])
        dr_ref[...] = dr
        dlg_ref[...] += dlg_rows
        dlb_ref[...] += dlb_rows
        dm = _dot_nt(dr, w_ref[...])
        dx, dgr = _rms_bwd(o_ref[...], gg[:, :ATTN_W], dm[:, :ATTN_W])
        do_ref[...] = dx
        for j in range(ATTN_W // LANE):
            stage[j] = dx[:, j * LANE:(j + 1) * LANE]
        _to_dilated(stage, do2_ref, DILATIONS[1], tm)
        _to_dilated(stage, do3_ref, DILATIONS[2], tm)
        dg_ref[:, :ATTN_W] += jnp.sum(dgr, axis=0, keepdims=True)
        dx, dgr = _rms_bwd(lru_ref[...], gg[:, ATTN_W:ATTN_W + LRU_W], dm[:, ATTN_W:ATTN_W + LRU_W])
        dlru_ref[...] = dx
        dg_ref[:, ATTN_W:ATTN_W + LRU_W] += jnp.sum(dgr, axis=0, keepdims=True)
        dx, dgr = _rms_bwd(s5_ref[...], gg[:, ATTN_W + LRU_W:], dm[:, ATTN_W + LRU_W:])
        ds5_ref[...] = dx
        dg_ref[:, ATTN_W + LRU_W:] += jnp.sum(dgr, axis=0, keepdims=True)

    a = pl.BlockSpec((tm, ATTN_W), lambda i: (i, 0))
    s5s = pl.BlockSpec((tm, S5_W), lambda i: (i, 0))
    full = pl.BlockSpec((tm, D_MODEL), lambda i: (i, 0))
    vec = pl.BlockSpec((1, D_MODEL), lambda i: (0, 0))
    dil = [_dilated_spec(tm, d, ATTN_W) for d in DILATIONS]
    dshape = [jax.ShapeDtypeStruct((s // d, d * ATTN_W), F32) for d in DILATIONS]
    dep_specs, dep_ops = _dep_args(dep)
    vshape = jax.ShapeDtypeStruct((1, D_MODEL), F32)
    res = pl.pallas_call(
        body, out_shape=(jax.ShapeDtypeStruct((s, D_MODEL), F32), vshape, vshape, *dshape,
                         jax.ShapeDtypeStruct((s, LRU_W), F32), jax.ShapeDtypeStruct((s, S5_W), F32), vshape),
        grid=(s // tm,),
        in_specs=[full, full, vec, pl.BlockSpec((D_MODEL, D_MODEL), lambda i: (0, 0)), a, a, s5s, vec] + dep_specs,
        out_specs=(full, vec, vec, *dil, a, s5s, vec), scratch_shapes=[_token_scratch(tm, ATTN_W)], name=name,
        compiler_params=_cparams(("arbitrary",)))(r, dh, ln_g, w_out, o, lru, s5, g, *dep_ops)
    return res[0], res[1], res[2], res[3:6], res[6], res[7], res[8]


def _lru_gate_math(xc, pre_r, pre_i, lam):
    r = _sigmoid(pre_r)
    i = _sigmoid(pre_i)
    log_a = -LRU_C * r * _softplus(-lam)
    a = jnp.exp(log_a)
    u = jnp.sqrt(-_expm1(2.0 * log_a)) * (i * xc)
    return a, u


def _lru_conv(x, prev8, cw, cb):
    y = cb + cw[LRU_CONV - 1:LRU_CONV, :] * x
    for k in range(LRU_CONV - 1):
        y = y + cw[k:k + 1, :] * _shift_down_prev(x, LRU_CONV - 1 - k, prev8)
    return y


def _lru_specs(s):
    xo = 3 * ATTN_W // LANE
    go = xo + LRU_W // LANE
    xr = pl.BlockSpec((s, LANE), lambda j: (0, xo + j))
    gt = pl.BlockSpec((s, LANE), lambda j: (0, go + j))
    cw = pl.BlockSpec((LRU_CONV, LANE), lambda j: (0, j))
    vec = pl.BlockSpec((1, LANE), lambda j: (0, j))
    wbd = pl.BlockSpec((LANE, LANE), lambda j: (j, j))
    col = pl.BlockSpec((s, LANE), lambda j: (0, j))
    return xr, gt, cw, vec, wbd, col


def _lru_fwd(proj, cw, cb, wr, br, wi, bi, lam, name):
    s = proj.shape[0]
    t = SCAN_T

    def body(xr_ref, gt_ref, cw_ref, cb_ref, wr_ref, br_ref, wi_ref, bi_ref, lam_ref, o_ref, xc_ref, a_ref, h_ref):
        cwv, cbv, lamv = cw_ref[...], cb_ref[...], lam_ref[...]
        wrv, wiv, brv, biv = wr_ref[...], wi_ref[...], br_ref[...], bi_ref[...]

        def chunk(c, carry):
            h_c, prev8 = carry
            rows = pl.ds(pl.multiple_of(c * t, t), t)
            x = xr_ref[rows, :]
            xc = _lru_conv(x, prev8, cwv, cbv)
            a, u = _lru_gate_math(xc, _dot(xc, wrv) + brv, _dot(xc, wiv) + biv, lamv)
            h = _scan_chunk(a, u, h_c)
            xc_ref[rows, :] = xc
            a_ref[rows, :] = a
            h_ref[rows, :] = h
            o_ref[rows, :] = h * _gelu(gt_ref[rows, :])
            return h[t - 1:t, :], x[t - 8:t, :]

        lax.fori_loop(0, s // t, chunk, (jnp.zeros((1, LANE), F32), jnp.zeros((8, LANE), F32)))

    xr, gt, cws, vec, wbd, col = _lru_specs(s)
    shp = jax.ShapeDtypeStruct((s, LRU_W), F32)
    return pl.pallas_call(
        body, out_shape=(shp,) * 4, grid=(LRU_W // LANE,),
        in_specs=[xr, gt, cws, vec, wbd, vec, wbd, vec, vec], out_specs=(col,) * 4, name=name,
        compiler_params=_cparams(("parallel",)))(proj, proj, cw, cb, wr, br, wi, bi, lam)


def _lru_bwd(proj, dout, xc_all, a_all, h_all, cw, cb, wr, br, wi, bi, lam, name):
    s = proj.shape[0]
    t = SCAN_T
    nc = s // t

    def body(xr_ref, gt_ref, do_ref, xc_s, a_s, h_s, cw_ref, cb_ref, wr_ref, br_ref, wi_ref, bi_ref, lam_ref,
             dxr_ref, dgt_ref, dcw_ref, dcb_ref, dwr_ref, dbr_ref, dwi_ref, dbi_ref, dlam_ref):
        cwv, cbv, lamv = cw_ref[...], cb_ref[...], lam_ref[...]
        wrv, wiv, brv, biv = wr_ref[...], wi_ref[...], br_ref[...], bi_ref[...]
        z1 = jnp.zeros((1, LANE), F32)
        zw = jnp.zeros((LANE, LANE), F32)

        def bchunk(ci, carry):
            g_next, a_next, dxc_next8, dcw, dcb, dwr, dbr, dwi, dbi, dlam = carry
            c = nc - 1 - ci
            t0 = pl.multiple_of(c * t, t)
            rows = pl.ds(t0, t)
            before = pl.ds(pl.multiple_of(jnp.maximum(t0 - 8, 0), 8), 8)
            has_prev = (c > 0).astype(F32)
            x, gt, do = xr_ref[rows, :], gt_ref[rows, :], do_ref[rows, :]
            xc, a, h = xc_s[rows, :], a_s[rows, :], h_s[rows, :]
            prev8_h = h_s[before, :] * has_prev
            dgt_ref[rows, :] = do * h * _gelu_grad(gt)
            dh = do * _gelu(gt)
            a_plus = _shift_up_next(a, 1, jnp.broadcast_to(a_next, (8, LANE)))
            g = _scan_chunk(a_plus, dh, g_next, reverse=True)
            da = g * _shift_down_prev(h, 1, prev8_h)
            pre_r = _dot(xc, wrv) + brv
            pre_i = _dot(xc, wiv) + biv
            _, vjp = jax.vjp(_lru_gate_math, xc, pre_r, pre_i, lamv)
            dxc, dpre_r, dpre_i, dlam_c = vjp((da, g))
            dxc = dxc + _dot_nt(dpre_r, wrv) + _dot_nt(dpre_i, wiv)
            dx = cwv[LRU_CONV - 1:LRU_CONV, :] * dxc
            dcw_rows = [None] * LRU_CONV
            dcw_rows[LRU_CONV - 1] = jnp.sum(dxc * x, axis=0, keepdims=True)
            for k in range(LRU_CONV - 1):
                dxc_ahead = _shift_up_next(dxc, LRU_CONV - 1 - k, dxc_next8)
                dx = dx + cwv[k:k + 1, :] * dxc_ahead
                dcw_rows[k] = jnp.sum(dxc_ahead * x, axis=0, keepdims=True)
            dxr_ref[rows, :] = dx
            return (g[0:1, :], a[0:1, :], dxc[0:8, :],
                    dcw + jnp.concatenate(dcw_rows, axis=0),
                    dcb + jnp.sum(dxc, axis=0, keepdims=True),
                    dwr + _dot_tn(xc, dpre_r), dbr + jnp.sum(dpre_r, axis=0, keepdims=True),
                    dwi + _dot_tn(xc, dpre_i), dbi + jnp.sum(dpre_i, axis=0, keepdims=True),
                    dlam + dlam_c)

        init = (z1, z1, jnp.zeros((8, LANE), F32), jnp.zeros((LRU_CONV, LANE), F32), z1, zw, z1, zw, z1, z1)
        res = lax.fori_loop(0, nc, bchunk, init)
        dcw_ref[...] = res[3]
        dcb_ref[...] = res[4]
        dwr_ref[...] = res[5]
        dbr_ref[...] = res[6]
        dwi_ref[...] = res[7]
        dbi_ref[...] = res[8]
        dlam_ref[...] = res[9]

    xr, gt, cws, vec, wbd, col = _lru_specs(s)
    vshape = jax.ShapeDtypeStruct((1, LRU_W), F32)
    wshape = jax.ShapeDtypeStruct((LRU_W, LRU_W), F32)
    return pl.pallas_call(
        body,
        out_shape=(jax.ShapeDtypeStruct((s, LRU_W), F32), jax.ShapeDtypeStruct((s, LRU_W), F32),
                   jax.ShapeDtypeStruct((LRU_CONV, LRU_W), F32), vshape, wshape, vshape, wshape, vshape, vshape),
        grid=(LRU_W // LANE,),
        in_specs=[xr, gt, col, col, col, col, cws, vec, wbd, vec, wbd, vec, vec],
        out_specs=(col, col, cws, vec, wbd, vec, wbd, vec, vec), name=name,
        compiler_params=_cparams(("parallel",)))(proj, proj, dout, xc_all, a_all, h_all, cw, cb, wr, br, wi, bi,
                                                 lam)


def _s5_disc_math(a_re, a_im, log_step, bt_re, bt_im):
    step = jnp.exp(log_step)
    dt_re, dt_im = step * a_re, step * a_im
    mag = jnp.exp(dt_re)
    ab_re, ab_im = mag * jnp.cos(dt_im), mag * jnp.sin(dt_im)
    z_re, z_im = ab_re - 1.0, ab_im
    den = a_re * a_re + a_im * a_im
    f_re = (z_re * a_re + z_im * a_im) / den
    f_im = (z_im * a_re - z_re * a_im) / den
    bb_re = f_re * bt_re - f_im * bt_im
    bb_im = f_re * bt_im + f_im * bt_re
    return ab_re, ab_im, bb_re, bb_im


def _s5_disc_fwd(a_re, a_im, log_step, bt_re, bt_im, name):
    def body(ar, ai, ls, br, bi, o1, o2, o3, o4):
        r = _s5_disc_math(ar[...], ai[...], ls[...], br[...], bi[...])
        o1[...], o2[...], o3[...], o4[...] = r

    shp = jax.ShapeDtypeStruct(a_re.shape, F32)
    return pl.pallas_call(body, out_shape=(shp,) * 4, name=name)(a_re, a_im, log_step, bt_re, bt_im)


def _s5_disc_bwd(a_re, a_im, log_step, bt_re, bt_im, cts, name):
    def body(ar, ai, ls, br, bi, c1, c2, c3, c4, o1, o2, o3, o4, o5):
        _, vjp = jax.vjp(_s5_disc_math, ar[...], ai[...], ls[...], br[...], bi[...])
        r = vjp((c1[...], c2[...], c3[...], c4[...]))
        o1[...], o2[...], o3[...], o4[...], o5[...] = r

    shp = jax.ShapeDtypeStruct(a_re.shape, F32)
    return pl.pallas_call(body, out_shape=(shp,) * 5, name=name)(a_re, a_im, log_step, bt_re, bt_im, *cts)


def _s5_u_specs(s):
    uo = (3 * ATTN_W + 2 * LRU_W) // LANE
    return (pl.BlockSpec((s, LANE), lambda j: (0, uo)), pl.BlockSpec((s, LANE), lambda j: (0, uo + 1)))


def _s5_scan_fwd(proj, b_re, b_im, lam_re, lam_im, c_re, c_im, name):
    s = proj.shape[0]
    t = SCAN_T

    def body(u0_ref, u1_ref, bre_ref, bim_ref, lre_ref, lim_ref, cre_ref, cim_ref, xre_ref, xim_ref, y_ref):
        @pl.when(pl.program_id(0) == 0)
        def _():
            y_ref[...] = jnp.zeros_like(y_ref)
        lr, li = lre_ref[...], lim_ref[...]
        consts = _cscan_consts(lr, li, False)
        bre, bim, cre, cim = bre_ref[...], bim_ref[...], cre_ref[...], cim_ref[...]

        def chunk(c, carry):
            cr, ci = carry
            rows = pl.ds(pl.multiple_of(c * t, t), t)
            u = jnp.concatenate([u0_ref[rows, :], u1_ref[rows, :]], axis=1).astype(BF16)
            xr, xi = _cscan_chunk(_dot(u, bre), _dot(u, bim), consts, (cr, ci))
            xre_ref[rows, :] = xr
            xim_ref[rows, :] = xi
            y_ref[rows, :] += _dot(xr, cre) - _dot(xi, cim)
            return xr[t - 1:t, :], xi[t - 1:t, :]

        z = jnp.zeros((1, S5_BLK), F32)
        lax.fori_loop(0, s // t, chunk, (z, z))

    u0, u1 = _s5_u_specs(s)
    bsp = pl.BlockSpec((S5_W, S5_BLK), lambda j: (0, j))
    csp = pl.BlockSpec((S5_BLK, S5_W), lambda j: (j, 0))
    vec = pl.BlockSpec((1, S5_BLK), lambda j: (0, j))
    xsp = pl.BlockSpec((s, S5_BLK), lambda j: (0, j))
    ysp = pl.BlockSpec((s, S5_W), lambda j: (0, 0))
    xshape = jax.ShapeDtypeStruct((s, S5_STATES), F32)
    return pl.pallas_call(
        body, out_shape=(xshape, xshape, jax.ShapeDtypeStruct((s, S5_W), F32)),
        grid=(S5_STATES // S5_BLK,), in_specs=[u0, u1, bsp, bsp, vec, vec, csp, csp],
        out_specs=(xsp, xsp, ysp), name=name,
        compiler_params=_cparams(("arbitrary",)))(proj, proj, b_re, b_im, lam_re, lam_im, c_re, c_im)


def _s5_scan_bwd(proj, dy, du_init, x_re, x_im, b_re, b_im, lam_re, lam_im, c_re, c_im, name):
    s = proj.shape[0]
    t = SCAN_T
    nc = s // t

    def body(u0_ref, u1_ref, dy_ref, dui_ref, xre_ref, xim_ref, bre_ref, bim_ref, lre_ref, lim_ref,
             cre_ref, cim_ref, du_ref, dlr_ref, dli_ref, dbr_ref, dbi_ref, dcr_ref, dci_ref):
        @pl.when(pl.program_id(0) == 0)
        def _():
            du_ref[...] = dui_ref[...]
        mr, mi = lre_ref[...], -lim_ref[...]
        consts = _cscan_consts(mr, mi, True)
        bre, bim, cre, cim = bre_ref[...], bim_ref[...], cre_ref[...], cim_ref[...]
        dbr_ref[...] = jnp.zeros_like(dbr_ref)
        dbi_ref[...] = jnp.zeros_like(dbi_ref)
        dcr_ref[...] = jnp.zeros_like(dcr_ref)
        dci_ref[...] = jnp.zeros_like(dci_ref)

        def chunk(ci_, carry):
            gnr, gni, dlr, dli = carry
            c = nc - 1 - ci_
            t0 = pl.multiple_of(c * t, t)
            rows = pl.ds(t0, t)
            before = pl.ds(pl.multiple_of(jnp.maximum(t0 - 8, 0), 8), 8)
            has_prev = (c > 0).astype(F32)
            dyc = dy_ref[rows, :].astype(BF16)
            u = jnp.concatenate([u0_ref[rows, :], u1_ref[rows, :]], axis=1).astype(BF16)
            gr, gi = _cscan_chunk(_dot_nt(dyc, cre), -_dot_nt(dyc, cim), consts, (gnr, gni), reverse=True)
            xr, xi = xre_ref[rows, :], xim_ref[rows, :]
            xpr = _shift_down_prev(xr, 1, xre_ref[before, :] * has_prev)
            xpi = _shift_down_prev(xi, 1, xim_ref[before, :] * has_prev)
            dlr = dlr + jnp.sum(gr * xpr + gi * xpi, axis=0, keepdims=True)
            dli = dli + jnp.sum(gi * xpr - gr * xpi, axis=0, keepdims=True)
            du_ref[rows, :] += _dot_nt(gr, bre) + _dot_nt(gi, bim)
            dbr_ref[...] += _dot_tn(u, gr)
            dbi_ref[...] += _dot_tn(u, gi)
            dcr_ref[...] += _dot_tn(xr, dyc)
            dci_ref[...] -= _dot_tn(xi, dyc)
            return gr[0:1, :], gi[0:1, :], dlr, dli

        z = jnp.zeros((1, S5_BLK), F32)
        res = lax.fori_loop(0, nc, chunk, (z, z, z, z))
        dlr_ref[...] = res[2]
        dli_ref[...] = res[3]

    u0, u1 = _s5_u_specs(s)
    bsp = pl.BlockSpec((S5_W, S5_BLK), lambda j: (0, j))
    csp = pl.BlockSpec((S5_BLK, S5_W), lambda j: (j, 0))
    vec = pl.BlockSpec((1, S5_BLK), lambda j: (0, j))
    xsp = pl.BlockSpec((s, S5_BLK), lambda j: (0, j))
    ysp = pl.BlockSpec((s, S5_W), lambda j: (0, 0))
    return pl.pallas_call(
        body,
        out_shape=(jax.ShapeDtypeStruct((s, S5_W), F32),
                   jax.ShapeDtypeStruct((1, S5_STATES), F32), jax.ShapeDtypeStruct((1, S5_STATES), F32),
                   jax.ShapeDtypeStruct((S5_W, S5_STATES), F32), jax.ShapeDtypeStruct((S5_W, S5_STATES), F32),
                   jax.ShapeDtypeStruct((S5_STATES, S5_W), F32), jax.ShapeDtypeStruct((S5_STATES, S5_W), F32)),
        grid=(S5_STATES // S5_BLK,),
        in_specs=[u0, u1, ysp, ysp, xsp, xsp, bsp, bsp, vec, vec, csp, csp],
        out_specs=(ysp, vec, vec, bsp, bsp, csp, csp), name=name,
        compiler_params=_cparams(("arbitrary",)))(
            proj, proj, dy, du_init, x_re, x_im, b_re, b_im, lam_re, lam_im, c_re, c_im)


def _s5_out_fwd(proj, y_acc, dvec, w_glu, b_glu, name):
    s = proj.shape[0]
    tm = 512
    uo = (3 * ATTN_W + 2 * LRU_W) // LANE

    def body(u0_ref, u1_ref, y_ref, d_ref, w_ref, b_ref, o_ref, yp_ref):
        u = jnp.concatenate([u0_ref[...], u1_ref[...]], axis=1)
        y = y_ref[...] + d_ref[...] * u
        yp_ref[...] = y
        yg = _gelu(y)
        o_ref[...] = yg * _sigmoid(_dot(yg, w_ref[...]) + b_ref[...])

    u0 = pl.BlockSpec((tm, LANE), lambda i: (i, uo))
    u1 = pl.BlockSpec((tm, LANE), lambda i: (i, uo + 1))
    row = pl.BlockSpec((tm, S5_W), lambda i: (i, 0))
    vec = pl.BlockSpec((1, S5_W), lambda i: (0, 0))
    wsp = pl.BlockSpec((S5_W, S5_W), lambda i: (0, 0))
    shp = jax.ShapeDtypeStruct((s, S5_W), F32)
    return pl.pallas_call(
        body, out_shape=(shp, shp), grid=(s // tm,), in_specs=[u0, u1, row, vec, wsp, vec],
        out_specs=(row, row), name=name,
        compiler_params=_cparams(("parallel",)))(proj, proj, y_acc, dvec, w_glu, b_glu)


def _s5_out_bwd(proj, y_pre, dout, dvec, w_glu, b_glu, name, dep=None):
    s = proj.shape[0]
    tm = 512
    uo = (3 * ATTN_W + 2 * LRU_W) // LANE

    def body(u0_ref, u1_ref, y_ref, do_ref, d_ref, w_ref, b_ref, *rest):
        dy_ref, dud_ref, dd_ref, dw_ref, db_ref = rest[-5:]

        @pl.when(pl.program_id(0) == 0)
        def _():
            dd_ref[...] = jnp.zeros_like(dd_ref)
            dw_ref[...] = jnp.zeros_like(dw_ref)
            db_ref[...] = jnp.zeros_like(db_ref)
        u = jnp.concatenate([u0_ref[...], u1_ref[...]], axis=1)
        y = y_ref[...]
        do = do_ref[...]
        yg = _gelu(y)
        sg = _sigmoid(_dot(yg, w_ref[...]) + b_ref[...])
        dz = do * yg * sg * (1.0 - sg)
        dyg = do * sg + _dot_nt(dz, w_ref[...])
        dy = dyg * _gelu_grad(y)
        dy_ref[...] = dy
        dud_ref[...] = d_ref[...] * dy
        dd_ref[...] += jnp.sum(dy * u, axis=0, keepdims=True)
        dw_ref[...] += _dot_tn(yg, dz)
        db_ref[...] += jnp.sum(dz, axis=0, keepdims=True)

    u0 = pl.BlockSpec((tm, LANE), lambda i: (i, uo))
    u1 = pl.BlockSpec((tm, LANE), lambda i: (i, uo + 1))
    row = pl.BlockSpec((tm, S5_W), lambda i: (i, 0))
    vec = pl.BlockSpec((1, S5_W), lambda i: (0, 0))
    wsp = pl.BlockSpec((S5_W, S5_W), lambda i: (0, 0))
    shp = jax.ShapeDtypeStruct((s, S5_W), F32)
    vshape = jax.ShapeDtypeStruct((1, S5_W), F32)
    dep_specs, dep_ops = _dep_args(dep)
    return pl.pallas_call(
        body, out_shape=(shp, shp, vshape, jax.ShapeDtypeStruct((S5_W, S5_W), F32), vshape),
        grid=(s // tm,), in_specs=[u0, u1, row, row, vec, wsp, vec] + dep_specs,
        out_specs=(row, row, vec, wsp, vec), name=name,
        compiler_params=_cparams(("arbitrary",)))(proj, proj, y_pre, dout, dvec, w_glu, b_glu, *dep_ops)


def _ffn_conv(x, prev8, cw, cb):
    y = cb + cw[FFN_CONV - 1:FFN_CONV, :] * x
    for k in range(FFN_CONV - 1):
        y = y + cw[k:k + 1, :] * _shift_down_prev(x, FFN_CONV - 1 - k, prev8)
    return y


def _ffn_up_act(h, wg, cw, cb, name, dep=None):
    s, d = h.shape
    tm = 512
    tb = 2 * FFN_CB
    nt = D_FF // FFN_CB

    def body(h_ref, wgate_ref, wval_ref, cw_ref, cb_ref, *rest):
        up_ref, y_ref, o_ref, ot_ref, carry = rest[-5:]

        @pl.when(pl.program_id(1) == 0)
        def _():
            carry[...] = jnp.zeros_like(carry)
        hb = h_ref[...].astype(BF16)
        x = jnp.concatenate([_dot(hb, wgate_ref[...]), _dot(hb, wval_ref[...])], axis=1)
        up_ref[...] = x.astype(BF16)
        y = _ffn_conv(x, carry[...], cw_ref[...], cb_ref[...])
        y_ref[...] = y
        carry[...] = x[tm - 8:tm, :]
        act = _gelu(y[:, :FFN_CB]) * y[:, FFN_CB:]
        o_ref[...] = act.astype(BF16)
        ot_ref[...] = act.T.astype(BF16)

    dep_specs, dep_ops = _dep_args(dep)
    return pl.pallas_call(
        body, out_shape=(jax.ShapeDtypeStruct((s, 2 * D_FF), BF16), jax.ShapeDtypeStruct((s, 2 * D_FF), F32),
                         jax.ShapeDtypeStruct((s, D_FF), BF16), jax.ShapeDtypeStruct((D_FF, s), BF16)),
        grid=(nt, s // tm),
        in_specs=[pl.BlockSpec((tm, d), lambda t, i: (i, 0)),
                  pl.BlockSpec((None, d, FFN_CB), lambda t, i: (t, 0, 0)),
                  pl.BlockSpec((None, d, FFN_CB), lambda t, i: (t + nt, 0, 0)),
                  pl.BlockSpec((FFN_CONV, tb), lambda t, i: (0, t)),
                  pl.BlockSpec((1, tb), lambda t, i: (0, t))] + dep_specs,
        out_specs=(pl.BlockSpec((tm, tb), lambda t, i: (i, t)), pl.BlockSpec((tm, tb), lambda t, i: (i, t)),
                   pl.BlockSpec((tm, FFN_CB), lambda t, i: (i, t)), pl.BlockSpec((FFN_CB, tm), lambda t, i: (t, i))),
        scratch_shapes=[pltpu.VMEM((8, tb), F32)], name=name,
        compiler_params=_cparams(("parallel", "arbitrary")))(h, wg, wg, cw, cb, *dep_ops)


def _ffn_bwd(up, y_conv, dr, w_down, wg, cw, name):
    s = up.shape[0]
    d = dr.shape[1]
    tm = 512
    tb = 2 * FFN_CB
    nr = s // tm
    nt = D_FF // FFN_CB

    def body(x_ref, y_ref, dr_ref, wd_ref, wgate_ref, wval_ref, cw_ref,
             dup_ref, dh_ref, dcw_ref, dcb_ref, carry):
        i, t = pl.program_id(0), pl.program_id(1)

        @pl.when(i == 0)
        def _():
            carry[t] = jnp.zeros((8, tb), F32)

        @pl.when(t == 0)
        def _():
            dh_ref[...] = ALPHA * dr_ref[...]
        cwv = cw_ref[...]
        x = x_ref[...]
        dact = _dot_nt(dr_ref[...], wd_ref[...])
        gate, val = y_ref[:, :FFN_CB], y_ref[:, FFN_CB:]
        dy = jnp.concatenate([dact * val * _gelu_grad(gate), dact * _gelu(gate)], axis=1)
        next8 = carry[t]
        carry[t] = dy[0:8, :]
        dx = cwv[FFN_CONV - 1:FFN_CONV, :] * dy
        dcw_rows = [None] * FFN_CONV
        dcw_rows[FFN_CONV - 1] = jnp.sum(dy * x, axis=0, keepdims=True)
        for k in range(FFN_CONV - 1):
            dy_ahead = _shift_up_next(dy, FFN_CONV - 1 - k, next8)
            dx = dx + cwv[k:k + 1, :] * dy_ahead
            dcw_rows[k] = jnp.sum(dy_ahead * x, axis=0, keepdims=True)
        dup = dx.astype(BF16)
        dup_ref[...] = dup
        dh_ref[...] += _dot_nt(dup[:, :FFN_CB], wgate_ref[...]) + _dot_nt(dup[:, FFN_CB:], wval_ref[...])
        dcw_ref[...] = jnp.concatenate(dcw_rows, axis=0)
        dcb_ref[...] = jnp.sum(dy, axis=0, keepdims=True)

    row = lambda i: nr - 1 - i
    return pl.pallas_call(
        body, out_shape=(jax.ShapeDtypeStruct((s, 2 * D_FF), BF16), jax.ShapeDtypeStruct((s, d), F32),
                         jax.ShapeDtypeStruct((nr, FFN_CONV, 2 * D_FF), F32),
                         jax.ShapeDtypeStruct((nr, 1, 2 * D_FF), F32)),
        grid=(nr, nt),
        in_specs=[pl.BlockSpec((tm, tb), lambda i, t: (row(i), t)),
                  pl.BlockSpec((tm, tb), lambda i, t: (row(i), t)),
                  pl.BlockSpec((tm, d), lambda i, t: (row(i), 0)),
                  pl.BlockSpec((FFN_CB, d), lambda i, t: (t, 0)),
                  pl.BlockSpec((None, d, FFN_CB), lambda i, t: (t, 0, 0)),
                  pl.BlockSpec((None, d, FFN_CB), lambda i, t: (t + nt, 0, 0)),
                  pl.BlockSpec((FFN_CONV, tb), lambda i, t: (0, t))],
        out_specs=(pl.BlockSpec((tm, tb), lambda i, t: (row(i), t)),
                   pl.BlockSpec((tm, d), lambda i, t: (row(i), 0)),
                   pl.BlockSpec((None, FFN_CONV, tb), lambda i, t: (row(i), 0, t)),
                   pl.BlockSpec((None, 1, tb), lambda i, t: (row(i), 0, t))),
        scratch_shapes=[pltpu.VMEM((nt, 8, tb), F32)], name=name,
        compiler_params=_cparams(("arbitrary", "arbitrary")))(up, y_conv, dr, w_down, wg, wg, cw)


def _sum_partials(ld_ref):
    gg = ld_ref[0].astype(F32)
    for k in range(1, N_DEV):
        gg = gg + ld_ref[k].astype(F32)
    return gg


def _adam_update(w, g, m, v):
    mn = ADAM_B1 * m + (1.0 - ADAM_B1) * g
    vn = ADAM_B2 * v + (1.0 - ADAM_B2) * (g * g)
    m_hat = mn / (1.0 - ADAM_B1 ** ADAM_STEP)
    v_hat = vn / (1.0 - ADAM_B2 ** ADAM_STEP)
    return -ADAM_LR * (m_hat / (jnp.sqrt(v_hat) + ADAM_EPS) + ADAM_WD * w), mn, vn


def _adamw_many(landed, ws, ms, vs, name):
    n, nl = len(ws), len(landed)

    def body(*refs):
        ld = refs[:nl * n]
        w_refs, m_refs, v_refs = (refs[(nl + k) * n:(nl + k + 1) * n] for k in range(3))
        outs = refs[(nl + 3) * n:]
        for i in range(n):
            for l in range(nl):
                one = slice(l, l + 1)
                gg = _sum_partials(ld[l * n + i])
                outs[i][one] = gg
                outs[n + i][one], outs[2 * n + i][one], outs[3 * n + i][one] = _adam_update(
                    w_refs[i][one], gg, m_refs[i][one], v_refs[i][one])

    vm = pl.BlockSpec(memory_space=pltpu.VMEM)
    shapes = [jax.ShapeDtypeStruct(w.shape, F32) for w in ws] * 4
    res = pl.pallas_call(
        body, out_shape=tuple(shapes), in_specs=[vm] * ((nl + 3) * n), out_specs=tuple([vm] * (4 * n)),
        name=name, compiler_params=_cparams())(*[a for layer in landed for a in layer], *ws, *ms, *vs)
    return res[:n], res[n:2 * n], res[2 * n:3 * n], res[3 * n:]


def _adamw_sum(landed, w, m, v, layer, prev, name):
    _, r, c = landed.shape
    nl = w.shape[0]
    tm = 8
    for cand in (512, 256, 128, 64, 32, 16):
        if r % cand == 0 and N_DEV * cand * c * 4 <= 4 * 1024 * 1024:
            tm = cand
            break

    def body(*refs):
        ld_ref, w_ref, m_ref, v_ref = refs[:4]
        g_ref, d_ref, mo_ref, vo_ref = refs[-4:]
        gg = _sum_partials(ld_ref)
        g_ref[...] = gg
        d_ref[...], mo_ref[...], vo_ref[...] = _adam_update(w_ref[...], gg, m_ref[...], v_ref[...])

    blk = pl.BlockSpec((None, tm, c), lambda i: (layer, i, 0))
    in_specs = [pl.BlockSpec((N_DEV, tm, c), lambda i: (0, i, 0)), blk, blk, blk]
    args = [landed, w, m, v]
    aliases = {}
    if prev is not None:
        in_specs += [pl.BlockSpec(memory_space=pl.ANY)] * 4
        args += list(prev)
        aliases = {4 + k: k for k in range(4)}
    shp = jax.ShapeDtypeStruct((nl, r, c), F32)
    return pl.pallas_call(
        body, out_shape=(shp,) * 4, grid=(r // tm,), in_specs=in_specs, out_specs=(blk,) * 4,
        input_output_aliases=aliases, name=name, compiler_params=_cparams(("parallel",)))(*args)


def _all_gather(shards, name):
    na = len(shards)

    def body(*refs):
        x_refs, out_refs = refs[:na], refs[na:2 * na]
        send_sems, recv_sems, local_sems = refs[2 * na:]
        x, y, c = lax.axis_index("x"), lax.axis_index("y"), lax.axis_index("c")
        me, sibling = (x, y, c), (x, y, 1 - c)
        chips = [(1 - x, y), (x, 1 - y), (1 - x, 1 - y)]

        def copy(a, k, block, to, src=None):
            dst = out_refs[a].at[4 * block[0] + 2 * block[1] + block[2]]
            return pltpu.make_async_remote_copy(
                src_ref=dst if src is None else src, dst_ref=dst,
                send_sem=send_sems.at[7 * a + k], recv_sem=recv_sems.at[7 * a + k],
                device_id=to, device_id_type=pl.DeviceIdType.MESH)

        mine, first, passed = [], [], []
        for a in range(na):
            cp = pltpu.make_async_copy(x_refs[a], out_refs[a].at[4 * x + 2 * y + c], local_sems.at[a])
            cp.start()
            mine.append(cp)
            cps = [copy(a, 0, me, sibling, src=x_refs[a])]
            cps += [copy(a, 1 + j, me, (*chip, c), src=x_refs[a]) for j, chip in enumerate(chips)]
            for cp in cps:
                cp.start()
            first += cps
        for j, chip in enumerate(chips):
            for a in range(na):
                copy(a, 1 + j, (*chip, c), me).wait_recv()
                cp = copy(a, 4 + j, (*chip, c), sibling)
                cp.start()
                passed.append(cp)
        for a in range(na):
            copy(a, 0, sibling, me).wait_recv()
            for j, chip in enumerate(chips):
                copy(a, 4 + j, (*chip, 1 - c), me).wait_recv()
        for cp in first + passed:
            cp.wait_send()
        for cp in mine:
            cp.wait()

    anyspec = pl.BlockSpec(memory_space=pl.ANY)
    return pl.pallas_call(
        body, out_shape=tuple(jax.ShapeDtypeStruct((N_DEV,) + t.shape, t.dtype) for t in shards),
        in_specs=[anyspec] * na, out_specs=tuple([anyspec] * na),
        scratch_shapes=[pltpu.SemaphoreType.DMA((7 * na,)), pltpu.SemaphoreType.DMA((7 * na,)),
                        pltpu.SemaphoreType.DMA((na,))],
        name=name)(*shards)


_HBM = pl.BlockSpec(memory_space=pltpu.HBM)
_SEM = pl.BlockSpec(memory_space=pltpu.SEMAPHORE)
_EFFECT = pltpu.SideEffectType.DATAFLOW_SIDE_EFFECTING


def _exchange_copies(src_refs, land_refs, send_sems, recv_sems, local_sems, gather):
    x, y, c = lax.axis_index("x"), lax.axis_index("y"), lax.axis_index("c")
    me = 4 * x + 2 * y + c
    per_array = send_sems.shape[0] > N_DEV - 1
    local, remote = [], []
    for a, (src, land) in enumerate(zip(src_refs, land_refs)):
        local.append(pltpu.make_async_copy(src if gather else src.at[me], land.at[me],
                                           local_sems.at[a if per_array else 0]))
    for k in range(1, N_DEV):
        px = x ^ ((k >> 2) & 1)
        py = y ^ ((k >> 1) & 1)
        pc = c ^ (k & 1)
        for a, (src, land) in enumerate(zip(src_refs, land_refs)):
            remote.append(pltpu.make_async_remote_copy(
                src_ref=src if gather else src.at[4 * px + 2 * py + pc], dst_ref=land.at[me],
                send_sem=send_sems.at[(7 * a if per_array else 0) + k - 1],
                recv_sem=recv_sems.at[(7 * a if per_array else 0) + k - 1],
                device_id=(px, py, pc), device_id_type=pl.DeviceIdType.MESH))
    return local, remote


def _exchange_start(srcs, gather, name, dep=None):
    na = len(srcs)
    ns = na if na <= 4 else 1
    lands = [lax.empty(((N_DEV,) + t.shape) if gather else t.shape, t.dtype) for t in srcs]

    def body(*refs):
        src_refs, land_refs = refs[:na], refs[na:2 * na]
        nin = 2 * na + (0 if dep is None else 1)
        send_sems, recv_sems, local_sems = refs[nin:nin + 3]
        token = refs[-1]
        local, remote = _exchange_copies(src_refs, land_refs, send_sems, recv_sems, local_sems, gather)
        for cp in local + remote:
            cp.start()
        token[...] = jnp.zeros_like(token)

    dep_specs, dep_ops = _dep_args(dep)
    hbm = lambda t: pltpu.HBM(t.shape, t.dtype)
    out = pl.pallas_call(
        body, name=name,
        out_shape=(pltpu.SemaphoreType.DMA((7 * ns,)), pltpu.SemaphoreType.DMA((7 * ns,)),
                   pltpu.SemaphoreType.DMA((ns,)), *[hbm(t) for t in srcs], *[hbm(t) for t in lands],
                   jax.ShapeDtypeStruct((8, LANE), F32)),
        in_specs=[_HBM] * (2 * na) + dep_specs,
        out_specs=(_SEM, _SEM, _SEM, *[_HBM] * (2 * na), pl.BlockSpec(memory_space=pltpu.VMEM)),
        input_output_aliases={i: 3 + i for i in range(2 * na)},
        compiler_params=pltpu.CompilerParams(has_side_effects=_EFFECT),
    )(*[pltpu.with_memory_space_constraint(t, pltpu.HBM) for t in srcs + lands], *dep_ops)
    return (out[:3], out[3:3 + na], out[3 + na:3 + 2 * na]), out[-1]


def _exchange_wait(handle, gather, after, name):
    sems, srcs, lands = handle
    na = len(srcs)

    def body(*refs):
        src_refs, land_refs = refs[:na], refs[na:2 * na]
        send_sems, recv_sems, local_sems = refs[2 * na:2 * na + 3]
        local, remote = _exchange_copies(src_refs, land_refs, send_sems, recv_sems, local_sems, gather)
        for cp in remote:
            cp.wait_send()
            cp.wait_recv()
        for cp in local:
            cp.wait()

    hbm = lambda t: pltpu.HBM(t.shape, t.dtype)
    out = pl.pallas_call(
        body, name=name, out_shape=(*[hbm(t) for t in srcs], *[hbm(t) for t in lands]),
        in_specs=[_HBM] * (2 * na) + [_SEM] * 3 + [pl.BlockSpec(memory_space=pl.ANY)],
        out_specs=tuple([_HBM] * (2 * na)), input_output_aliases={i: i for i in range(2 * na)},
        compiler_params=pltpu.CompilerParams(has_side_effects=_EFFECT),
    )(*srcs, *lands, *sems, after)
    return out[na:]


def _block_diag(w):
    h, a, b = w.shape
    eye = jnp.eye(h, dtype=w.dtype)
    return (w[:, :, None, :] * eye[:, None, :, None]).reshape(h * a, h * b)


def _block_diag_extract(m, h):
    a, b = m.shape[0] // h, m.shape[1] // h
    return jnp.stack([m[i * a:(i + 1) * a, i * b:(i + 1) * b] for i in range(h)], axis=0)


def _block_diag_take(m, h):
    a, b = m.shape[0] // h, m.shape[1] // h
    eye = jnp.eye(h, dtype=m.dtype)
    return (m.reshape(h, a, h, b) * eye[:, None, :, None]).sum(axis=2)


def _ffn_interleave(w):
    lead = w.shape[:-1]
    nb = D_FF // FFN_CB
    return jnp.swapaxes(w.reshape(*lead, 2, nb, FFN_CB), -3, -2).reshape(*lead, 2 * D_FF)


def _ffn_deinterleave(w):
    lead = w.shape[:-1]
    nb = D_FF // FFN_CB
    return jnp.swapaxes(w.reshape(*lead, nb, 2, FFN_CB), -3, -2).reshape(*lead, 2 * D_FF)


def _gather_full(gathered, axis):
    shape = list(gathered.shape[1:])
    shape[axis] *= N_DEV
    return jnp.moveaxis(gathered, 0, axis).reshape(shape)


def _scatter_blocks(full, axis):
    shape = list(full.shape)
    shape[axis:axis + 1] = [N_DEV, shape[axis] // N_DEV]
    return jnp.moveaxis(full.reshape(shape), axis, 0)


def _pad_to(flat, mult):
    pad = (-flat.shape[-1]) % mult
    if pad:
        flat = jnp.concatenate([flat, jnp.zeros(flat.shape[:-1] + (pad,), flat.dtype)], axis=-1)
    return flat


def _layer_fwd(h_in, h_in_t, w, cos, sin, l, dep, get_ffn, target=None):
    tag = "l%d_" % l
    proj, qkv, h_t = _proj_rope(h_in, w['w_in'], cos, sin, tag + "proj_rope", dep=dep,
                                transposed=h_in_t is None)
    h_in_t = h_t if h_in_t is None else h_in_t
    outs, lses = [], []
    for d, qv in zip(DILATIONS, qkv):
        o, ls = _attn_fwd(qv, d, tag + "attn_d%d" % d)
        outs.append(o)
        lses.append(ls)
    lru, *lru_saved = _lru_fwd(proj, w['lru_conv_w'], w['lru_conv_b'], w['lru_wr'], w['lru_br'], w['lru_wi'],
                               w['lru_bi'], w['lru_lambda'], tag + "lru")
    x_re, x_im, y_acc = _s5_scan_fwd(proj, w['s5_bb_re'], w['s5_bb_im'], w['s5_lam_re'], w['s5_lam_im'],
                                     w['s5_cc_re'], w['s5_cc_im'], tag + "s5_scan")
    s5, y_pre = _s5_out_fwd(proj, y_acc, w['s5_d'], w['s5_w_glu'], w['s5_b_glu'], tag + "s5_out")
    w_out = get_ffn(l, s5, 'out')
    if w_out is not None:
        w['w_out'] = w_out
    mixed_t, r1, h1, h1_t, attn_o, attn_lse = _mix_fwd(outs, lses, lru, s5, w['mix_norm_g'], h_in, w['w_out'],
                                                       w['ln1_g'], w['ln1_b'], tag + "mix_out_ln1")
    w['w_up_g'], w['w_down'], ffn_dep = get_ffn(l, h1, 'ffn')
    up, y_conv, act, act_t = _ffn_up_act(h1, w['w_up_g'], w['ffn_conv_w'], w['ffn_conv_b'], tag + "up_act",
                                         dep=ffn_dep)
    r2, out_a, out_b = _proj_ln(act, w['w_down'], h1, w['ln2_g'], w['ln2_b'], tag + "down_ln2", target=target)
    saved = dict(h_in_t=h_in_t, proj=proj, qkv=qkv, lru=lru, lru_saved=lru_saved, x_re=x_re, x_im=x_im,
                 y_pre=y_pre, s5=s5, mixed_t=mixed_t, attn_o=attn_o, attn_lse=attn_lse, r1=r1, h1_t=h1_t, up=up,
                 act_t=act_t, r2=r2, y_conv=y_conv)
    return out_a, out_b, saved


def _layer_bwd_ffn(dh2, sv, w, l, dep=None):
    tag = "l%d_" % l
    g = {}
    dr2, g['ln2_g'], g['ln2_b'] = _ln_bwd(sv['r2'], dh2, w['ln2_g'], tag + "ln2_bwd", dep=dep)
    g['w_down'] = _mm_dw(sv['act_t'], dr2, 1024, D_MODEL, 1024, tag + "down_dw", _grad_dtype(l))
    dup, dh1, dcw_parts, dcb_parts = _ffn_bwd(sv['up'], sv['y_conv'], dr2, w['w_down'], w['w_up_g'],
                                              w['ffn_conv_w'], tag + "ffn_bwd")
    g['ffn_conv_w'] = dcw_parts.sum(axis=0)
    g['ffn_conv_b'] = dcb_parts.sum(axis=0)
    g['w_up_g'] = _mm_up_dw(sv['h1_t'], dup, tag + "up_dw", _grad_dtype(l))
    return dh1, g


def _layer_bwd_mix(dh1, sv, w, cos, sin, l, dep, g_ffn, after_out_grad, after_small_grads, after_in_grad):
    tag = "l%d_" % l
    g = {}
    dr1, g['ln1_g'], g['ln1_b'], d_o, dlru, ds5, g['mix_norm_g'] = _mix_bwd(
        sv['r1'], dh1, w['ln1_g'], w['w_out'], sv['attn_o'][0], sv['lru'], sv['s5'], w['mix_norm_g'],
        tag + "ln1_mix_bwd", dep=dep)
    g['w_out'] = _mm_dw(sv['mixed_t'], dr1, 1024, D_MODEL, 1024, tag + "out_dw", _grad_dtype(l))
    dy, dud, g['s5_d'], g['s5_w_glu'], g['s5_b_glu'] = _s5_out_bwd(
        sv['proj'], sv['y_pre'], ds5, w['s5_d'], w['s5_w_glu'], w['s5_b_glu'], tag + "s5_out_bwd",
        dep=after_out_grad(l, g['w_out']))
    du, g['s5_lam_re'], g['s5_lam_im'], g['s5_bb_re'], g['s5_bb_im'], g['s5_cc_re'], g['s5_cc_im'] = \
        _s5_scan_bwd(sv['proj'], dy, dud, sv['x_re'], sv['x_im'], w['s5_bb_re'], w['s5_bb_im'],
                     w['s5_lam_re'], w['s5_lam_im'], w['s5_cc_re'], w['s5_cc_im'], tag + "s5_scan_bwd")
    (dxr, dgate, g['lru_conv_w'], g['lru_conv_b'], g['lru_wr'], g['lru_br'], g['lru_wi'], g['lru_bi'],
     g['lru_lambda']) = _lru_bwd(sv['proj'], dlru, *sv['lru_saved'], w['lru_conv_w'], w['lru_conv_b'], w['lru_wr'],
                                 w['lru_br'], w['lru_wi'], w['lru_bi'], w['lru_lambda'], tag + "lru_bwd")
    token = after_small_grads(l, _finish_layer_grads({**g_ffn, **g}, w, l))
    dqkv = [_attn_bwd(sv['qkv'][b], sv['attn_o'][b], d_o[b], sv['attn_lse'][b], d, tag + "attn_bwd_d%d" % d,
                      dep=token if b == 0 else None)
            for b, d in enumerate(DILATIONS)]
    dproj = _dproj_assemble(dqkv, dxr, dgate, du, cos, sin, tag + "dproj")
    g_in = _mm_dw(sv['h_in_t'], dproj, 1024, D_IN, 1024, tag + "in_dw", _grad_dtype(l))
    return _mm_nt(dproj, w['w_in'], 512, D_MODEL, tag + "in_dx", add=dr1, add_scale=ALPHA,
                  dep=after_in_grad(l, g_in))


def _s5_rep(a):
    return jnp.repeat(a, S5_C, axis=0)


def _prepare_layer(p, l):
    w = {}
    for n in ('w_in', 'w_out', 's5_w_glu'):
        if n in p:
            w[n] = p[n].astype(BF16)
    w['ffn_conv_w'] = _ffn_interleave(p['ffn_conv_w'])
    w['ffn_conv_b'] = _ffn_interleave(p['ffn_conv_b'])[None, :]
    w['lru_conv_w'] = p['lru_conv_w']
    for n in ('lru_conv_b', 'lru_br', 'lru_bi', 'lru_lambda', 's5_b_glu', 'mix_norm_g',
              'ln1_g', 'ln1_b', 'ln2_g', 'ln2_b'):
        w[n] = p[n][None, :]
    w['lru_wr'] = _block_diag(p['lru_wr']).astype(BF16)
    w['lru_wi'] = _block_diag(p['lru_wi']).astype(BF16)
    w['s5_d'] = p['s5_d'].reshape(1, S5_W)
    disc_in = (_s5_rep(p['s5_a_re']), _s5_rep(p['s5_a_im']),
               _s5_rep(jnp.broadcast_to(p['s5_log_step'][:, None], (S5_G, S5_P))),
               jnp.swapaxes(p['s5_b_re'], 1, 2).reshape(S5_W, S5_P),
               jnp.swapaxes(p['s5_b_im'], 1, 2).reshape(S5_W, S5_P))
    ab_re, ab_im, bb_re, bb_im = _s5_disc_fwd(*disc_in, "l%d_s5_disc" % l)
    w['s5_disc_in'] = disc_in
    w['s5_lam_re'] = ab_re.reshape(S5_G, S5_C, S5_P)[:, 0, :].reshape(1, S5_STATES)
    w['s5_lam_im'] = ab_im.reshape(S5_G, S5_C, S5_P)[:, 0, :].reshape(1, S5_STATES)
    w['s5_bb_re'] = _block_diag(bb_re.reshape(S5_G, S5_C, S5_P)).astype(BF16)
    w['s5_bb_im'] = _block_diag(bb_im.reshape(S5_G, S5_C, S5_P)).astype(BF16)
    w['s5_cc_re'] = _block_diag(jnp.swapaxes(p['s5_c_re'], 1, 2)).astype(BF16)
    w['s5_cc_im'] = _block_diag(jnp.swapaxes(p['s5_c_im'], 1, 2)).astype(BF16)
    return w


def _finish_layer_grads(g, w, l):
    out = {}
    for n in ('s5_w_glu', 'lru_conv_w'):
        out[n] = g[n]
    out['ffn_conv_w'] = _ffn_deinterleave(g['ffn_conv_w'])
    out['ffn_conv_b'] = _ffn_deinterleave(g['ffn_conv_b'])[0]
    for n in ('lru_conv_b', 'lru_br', 'lru_bi', 'lru_lambda', 's5_b_glu', 'mix_norm_g',
              'ln1_g', 'ln1_b', 'ln2_g', 'ln2_b'):
        out[n] = g[n][0]
    out['lru_wr'] = _block_diag_extract(g['lru_wr'], LRU_W // HEAD)
    out['lru_wi'] = _block_diag_extract(g['lru_wi'], LRU_W // HEAD)
    out['s5_d'] = g['s5_d'].reshape(S5_G, S5_C)
    out['s5_c_re'] = jnp.swapaxes(_block_diag_take(g['s5_cc_re'], S5_G), 1, 2)
    out['s5_c_im'] = jnp.swapaxes(_block_diag_take(g['s5_cc_im'], S5_G), 1, 2)
    rep = lambda v: _s5_rep(v.reshape(S5_G, S5_P)) * (1.0 / S5_C)
    cts = (rep(g['s5_lam_re']), rep(g['s5_lam_im']),
           _block_diag_take(g['s5_bb_re'], S5_G).reshape(S5_W, S5_P),
           _block_diag_take(g['s5_bb_im'], S5_G).reshape(S5_W, S5_P))
    da_re, da_im, dls, dbt_re, dbt_im = _s5_disc_bwd(*w['s5_disc_in'], cts, "l%d_s5_disc_bwd" % l)
    out['s5_a_re'] = da_re.reshape(S5_G, S5_C, S5_P).sum(axis=1)
    out['s5_a_im'] = da_im.reshape(S5_G, S5_C, S5_P).sum(axis=1)
    out['s5_log_step'] = dls.reshape(S5_G, S5_C * S5_P).sum(axis=1)
    out['s5_b_re'] = jnp.swapaxes(dbt_re.reshape(S5_G, S5_C, S5_P), 1, 2)
    out['s5_b_im'] = jnp.swapaxes(dbt_im.reshape(S5_G, S5_C, S5_P), 1, 2)
    return out


def _run_step(x, target, get_layer, get_ffn, on_loss, after_ffn_grads, after_out_grad, after_small_grads,
              after_in_grad):
    cos, sin = _rope_tables(x.shape[0])
    h, h_t = x, None
    ws, saved = [], []
    for l in range(DEPTH):
        p, dep = get_layer(l, h)
        ws.append(_prepare_layer(p, l))
        h, h_t, sv = _layer_fwd(h, h_t, ws[l], cos, sin, l, dep, get_ffn, target if l == DEPTH - 1 else None)
        saved.append(sv)
    dh, loss_vec = h, h_t
    on_loss(loss_vec)
    dep = None
    for l in reversed(range(DEPTH)):
        dh1, g = _layer_bwd_ffn(dh, saved[l], ws[l], l, dep)
        dep = after_ffn_grads(l, g)
        dh = _layer_bwd_mix(dh1, saved[l], ws[l], cos, sin, l, dep, g, after_out_grad, after_small_grads,
                            after_in_grad)
        dep = None
    return loss_vec, dh


def _local_step(x, target, layers):
    grads = [{} for _ in range(DEPTH)]

    def ffn(l, after, part):
        if part == 'out':
            return None
        return layers[l]['w_up_g'].astype(BF16), layers[l]['w_down'].astype(BF16), None

    def keep_ffn(l, g):
        grads[l].update(w_up_g=g['w_up_g'], w_down=g['w_down'])

    def keep_small(l, g):
        grads[l].update(g)

    loss, dx = _run_step(x, target, lambda l, h: (layers[l], None), ffn, lambda row: None, keep_ffn,
                         lambda l, g: grads[l].update(w_out=g), keep_small, lambda l, g: grads[l].update(w_in=g))
    return loss[0, 0], dx, grads


def kernel(x, w_in, lru_conv_w, lru_conv_b, lru_wr, lru_br, lru_wi, lru_bi, lru_lambda, s5_a_re, s5_a_im, s5_b_re, s5_b_im, s5_c_re, s5_c_im, s5_d, s5_log_step, s5_w_glu, s5_b_glu, mix_norm_g, w_out, ln1_g, ln1_b, w_up, ffn_conv_w, ffn_conv_b, w_down, ln2_g, ln2_b, loss_target, m_w_in, m_lru_conv_w, m_lru_conv_b, m_lru_wr, m_lru_br, m_lru_wi, m_lru_bi, m_lru_lambda, m_s5_a_re, m_s5_a_im, m_s5_b_re, m_s5_b_im, m_s5_c_re, m_s5_c_im, m_s5_d, m_s5_log_step, m_s5_w_glu, m_s5_b_glu, m_mix_norm_g, m_w_out, m_ln1_g, m_ln1_b, m_w_up, m_ffn_conv_w, m_ffn_conv_b, m_w_down, m_ln2_g, m_ln2_b, v_w_in, v_lru_conv_w, v_lru_conv_b, v_lru_wr, v_lru_br, v_lru_wi, v_lru_bi, v_lru_lambda, v_s5_a_re, v_s5_a_im, v_s5_b_re, v_s5_b_im, v_s5_c_re, v_s5_c_im, v_s5_d, v_s5_log_step, v_s5_w_glu, v_s5_b_glu, v_mix_norm_g, v_w_out, v_ln1_g, v_ln1_b, v_w_up, v_ffn_conv_w, v_ffn_conv_b, v_w_down, v_ln2_g, v_ln2_b):
    args = locals()
    wl = {n: args[n] for n in WEIGHTS}
    ml = {n: args['m_' + n] for n in WEIGHTS}
    vl = {n: args['v_' + n] for n in WEIGHTS}

    small_sizes = [int(wl[n].size) for n in SMALL_SHARDED]
    small_flat = _pad_to(jnp.concatenate([wl[n].reshape(-1) for n in SMALL_SHARDED]), 8 * 1024)
    small_all, w_in0 = _all_gather([small_flat.reshape(-1, 1024), wl['w_in'][0].astype(BF16)], "gather_first")
    small_all = small_all.reshape(N_DEV, -1)
    small_full, off = {}, 0
    for n, sz in zip(SMALL_SHARDED, small_sizes):
        small_full[n] = _gather_full(small_all[:, off:off + sz].reshape((N_DEV,) + wl[n].shape), SHARD_AXIS[n])
        off += sz
    def mixer_params(l, g_in, g_out):
        p = {n: wl[n][l] for n in REPLICATED}
        p.update({n: small_full[n][l] for n in SMALL_SHARDED})
        p['w_in'] = _gather_full(g_in, 1)
        if g_out is not None:
            p['w_out'] = g_out.reshape(D_MODEL, D_MODEL)
        return p

    mix_names, ffn_names = ('w_in', 'w_out'), ('w_up', 'w_down')
    shards = lambda names, l: [wl[n][l].astype(BF16) for n in names]
    gathers = {}
    gathers[0, 'out'], token = _exchange_start(shards(('w_out',), 0), True, "gather_out_l0_start", dep=w_in0)
    gathers[0, 'ffn'], rest0_token = _exchange_start(shards(ffn_names, 0), True, "gather_ffn_l0_start", dep=token)

    def get_layer(l, h):
        if l == 0:
            return mixer_params(0, w_in0, None), rest0_token
        return mixer_params(1, *_exchange_wait(gathers[1, 'mix'], True, h, "gather_mix_l1_wait")), None

    def get_ffn(l, after, part):
        if part == 'out':
            if l > 0:
                return None
            g_out, = _exchange_wait(gathers[0, 'out'], True, after, "gather_out_l0_wait")
            return g_out.reshape(D_MODEL, D_MODEL)
        g_up, g_down = _exchange_wait(gathers[l, 'ffn'], True, after, "gather_ffn_l%d_wait" % l)
        token = None
        if l == 0:
            gathers[1, 'mix'], token = _exchange_start(shards(mix_names, 1), True, "gather_mix_l1_start", dep=g_up)
            gathers[1, 'ffn'], token = _exchange_start(shards(ffn_names, 1), True, "gather_ffn_l1_start", dep=token)
        return g_up, g_down.reshape(D_FF, D_MODEL), token

    scatters = {}

    def after_ffn_grads(l, g):
        send = [g['w_up_g'], g['w_down'].reshape(N_DEV, D_FF // N_DEV, D_MODEL)]
        scatters[l, 'ffn'], token = _exchange_start(send, False, "scatter_ffn_l%d_start" % l)
        return token

    def after_out_grad(l, g_out):
        send = [g_out.reshape(N_DEV, D_MODEL // N_DEV, D_MODEL)]
        scatters[l, 'out'], token = _exchange_start(send, False, "scatter_out_l%d_start" % l)
        return token

    def after_in_grad(l, g_in):
        scatters[l, 'in'], token = _exchange_start([_scatter_blocks(g_in, 1)], False, "scatter_in_l%d_start" % l)
        return token

    def after_small_grads(l, g):
        rep = [g[n][None] for n in REPLICATED]
        if l == DEPTH - 1:
            rep.append(loss_rows[0][None])
        shd = [_scatter_blocks(g[n], SHARD_AXIS[n] - 1)[:, None] for n in SMALL_SHARDED]
        scatters[l, 'rep'], token = _exchange_start(rep, True, "gather_rep_grads_l%d_start" % l)
        scatters[l, 'small'], token = _exchange_start(shd, False, "scatter_small_l%d_start" % l, dep=token)
        return token

    loss_rows = []
    _, grad_x = _run_step(x[0], loss_target[0], get_layer, get_ffn, loss_rows.append, after_ffn_grads,
                          after_out_grad, after_small_grads, after_in_grad)

    results = {}
    big_prev = {n: None for n in BIG}

    def finish_big(l, part, names, after):
        landed = _exchange_wait(scatters[l, part], False, after, "scatter_%s_l%d_wait" % (part, l))
        for n, ld in zip(names, landed):
            big_prev[n] = _adamw_sum(ld, wl[n], ml[n], vl[n], l, big_prev[n], "adamw_%s_l%d" % (n, l))

    for l, part, names in ((1, 'ffn', ffn_names), (1, 'out', ('w_out',)), (1, 'in', ('w_in',)),
                           (0, 'ffn', ffn_names), (0, 'out', ('w_out',))):
        finish_big(l, part, names, grad_x)

    kinds = ('grad', 'delta', 'm', 'v')
    landed = []
    for l in range(DEPTH):
        rep = list(_exchange_wait(scatters[l, 'rep'], True, grad_x, "gather_rep_grads_l%d_wait" % l))
        if l == DEPTH - 1:
            loss = jnp.sum(rep.pop()[:, 0, 0, 0])
        shd = list(_exchange_wait(scatters[l, 'small'], False, grad_x, "scatter_small_l%d_wait" % l))
        landed.append(dict(zip(REPLICATED + SMALL_SHARDED, rep + shd)))
    matrices = ['lru_wr', 'lru_wi', 's5_a_re', 's5_a_im', 's5_c_re', 's5_c_im', 's5_d']
    widest = ['s5_b_re', 's5_b_im']
    vectors = [n for n in REPLICATED + SMALL_SHARDED if n not in matrices + widest]
    last = None
    for tag, names in (("vectors", vectors), ("matrices", matrices), ("s5_b", widest)):
        res = _adamw_many([[landed[l][n] for n in names] for l in range(DEPTH)], [wl[n] for n in names],
                          [ml[n] for n in names], [vl[n] for n in names], "adamw_" + tag)
        for kind, arrs in zip(kinds, res):
            for n, a in zip(names, arrs):
                results[kind, n] = a
        last = res[0][0]
    finish_big(0, 'in', ('w_in',), last)
    for n in BIG:
        results['grad', n], results['delta', n], results['m', n], results['v', n] = big_prev[n]

    out = [loss, grad_x[None]]
    for kind in kinds:
        out.extend(results[kind, n] for n in WEIGHTS)
    return tuple(out)
```

```python
import math

import jax
import jax.numpy as jnp
from jax import lax
from jax.experimental import pallas as pl
from jax.experimental.pallas import tpu as pltpu

F32 = jnp.float32
BF16 = jnp.bfloat16

N_DEV = 8
DEPTH = 2
D_MODEL = 1024
ATTN_W = 384
LRU_W = 384
S5_W = 256
D_IN = 2176
D_FF = 3072
HEAD = 64
ATTN_BLK = 128
ATTN_TILE = 1024
DILATIONS = (1, 4, 16)
S5_G = 16
S5_P = 64
S5_C = 16
S5_STATES = S5_G * S5_P
LRU_C = 8.0
LRU_CONV = 4
FFN_CONV = 3
ROPE_THETA = 10000.0
ALPHA = (2 * DEPTH) ** 0.25
LN_EPS = 1e-5
RMS_EPS = 1e-6
ADAM_LR, ADAM_B1, ADAM_B2, ADAM_EPS, ADAM_WD, ADAM_STEP = 0.001, 0.9, 0.999, 1e-8, 0.01, 10

LANE = 128
SCAN_T = 1024
S5_BLK = 256
FFN_CB = 2 * D_FF // N_DEV
VMEM_LIMIT = 56 * 1024 * 1024

WEIGHTS = ['w_in', 'lru_conv_w', 'lru_conv_b', 'lru_wr', 'lru_br', 'lru_wi', 'lru_bi', 'lru_lambda',
           's5_a_re', 's5_a_im', 's5_b_re', 's5_b_im', 's5_c_re', 's5_c_im', 's5_d', 's5_log_step',
           's5_w_glu', 's5_b_glu', 'mix_norm_g', 'w_out', 'ln1_g', 'ln1_b', 'w_up', 'ffn_conv_w',
           'ffn_conv_b', 'w_down', 'ln2_g', 'ln2_b']
SHARD_AXIS = {'w_in': 2, 'lru_conv_w': 2, 's5_w_glu': 1, 'w_out': 1, 'w_up': 2, 'ffn_conv_w': 2, 'w_down': 1}
BIG = ['w_in', 'w_out', 'w_up', 'w_down']
SMALL_SHARDED = ['lru_conv_w', 'ffn_conv_w', 's5_w_glu']
REPLICATED = [n for n in WEIGHTS if n not in SHARD_AXIS]


def _cparams(sem=None):
    return pltpu.CompilerParams(dimension_semantics=sem, vmem_limit_bytes=VMEM_LIMIT)


def _grad_dtype(l):
    return BF16 if l == 0 else F32


def _ffn_dev(jb):
    return jb // 2 + (N_DEV // 2) * (jb % 2)


def _gelu(x):
    c = math.sqrt(2.0 / math.pi)
    t = jnp.tanh(c * (x + 0.044715 * (x * x * x)))
    return 0.5 * x * (1.0 + t)


def _gelu_grad(x):
    c = math.sqrt(2.0 / math.pi)
    x2 = x * x
    t = jnp.tanh(c * (x + 0.044715 * (x2 * x)))
    return 0.5 * (1.0 + t) + 0.5 * x * (1.0 - t * t) * (c * (1.0 + 3.0 * 0.044715 * x2))


def _sigmoid(x):
    return 1.0 / (1.0 + jnp.exp(-x))


def _log1p(x):
    u = 1.0 + x
    d = u - 1.0
    return jnp.where(d == 0.0, x, jnp.log(u) * (x / jnp.where(d == 0.0, 1.0, d)))


def _softplus(x):
    return jnp.maximum(x, 0.0) + _log1p(jnp.exp(-jnp.abs(x)))


def _expm1(x):
    return jnp.tanh(0.5 * x) * (jnp.exp(x) + 1.0)


def _dot(a, b):
    return jnp.dot(a.astype(BF16), b.astype(BF16), preferred_element_type=F32)


def _dot_nt(a, b):
    return lax.dot_general(a.astype(BF16), b.astype(BF16), (((1,), (1,)), ((), ())),
                           preferred_element_type=F32)


def _dot_tn(a, b):
    return lax.dot_general(a.astype(BF16), b.astype(BF16), (((0,), (0,)), ((), ())),
                           preferred_element_type=F32)


def _rows(shape):
    return lax.broadcasted_iota(jnp.int32, shape, 0)


def _shift_down_prev(x, s, prev8):
    if s == 0:
        return x
    t, l = x.shape
    r = pltpu.roll(x, s, axis=0)
    pr = pltpu.roll(prev8, s, axis=0)
    pad = jnp.concatenate([pr, jnp.zeros((t - 8, l), x.dtype)], axis=0)
    return jnp.where(_rows(x.shape) < s, pad, r)


def _shift_up_next(x, s, next8):
    if s == 0:
        return x
    t, l = x.shape
    r = pltpu.roll(x, t - s, axis=0)
    nx = pltpu.roll(next8, 8 - s, axis=0)
    pad = jnp.concatenate([jnp.zeros((t - 8, l), x.dtype), nx], axis=0)
    return jnp.where(_rows(x.shape) >= t - s, pad, r)


SUB = 8


def _tile_shift(x, s, fill, reverse):
    t = x.shape[0]
    pos = _rows(x.shape) & (SUB - 1)
    if reverse:
        return jnp.where(pos < SUB - s, pltpu.roll(x, t - s, axis=0), fill)
    return jnp.where(pos >= s, pltpu.roll(x, s, axis=0), fill)


def _scan_chunk(a, x, carry, reverse=False):
    s = 1
    while s < SUB:
        x = x + a * _tile_shift(x, s, 0.0, reverse)
        a = a * _tile_shift(a, s, 1.0, reverse)
        s *= 2
    nv = x.shape[0] // SUB
    out = [None] * nv
    for v in (reversed(range(nv)) if reverse else range(nv)):
        rows = slice(v * SUB, (v + 1) * SUB)
        out[v] = x[rows, :] + a[rows, :] * carry
        carry = out[v][0:1, :] if reverse else out[v][SUB - 1:SUB, :]
    return jnp.concatenate(out, axis=0)


def _cmul(ar, ai, br, bi):
    return ar * br - ai * bi, ar * bi + ai * br


def _cscan_consts(lr, li, reverse):
    pows = [(lr, li)]
    for _ in range(2):
        pows.append(_cmul(*pows[-1], *pows[-1]))
    rows = [(lr, li)]
    for _ in range(SUB - 1):
        rows.append(_cmul(*rows[-1], lr, li))
    if reverse:
        rows = rows[::-1]
    return pows, (jnp.concatenate([r for r, _ in rows], axis=0), jnp.concatenate([i for _, i in rows], axis=0))


def _cscan_chunk(xr, xi, consts, carry, reverse=False):
    pows, (p8r, p8i) = consts
    s = 1
    for pr, pi in pows:
        sr = _tile_shift(xr, s, 0.0, reverse)
        si = _tile_shift(xi, s, 0.0, reverse)
        xr, xi = xr + pr * sr - pi * si, xi + pr * si + pi * sr
        s *= 2
    nv = xr.shape[0] // SUB
    out_r, out_i = [None] * nv, [None] * nv
    cr, ci = carry
    for v in (reversed(range(nv)) if reverse else range(nv)):
        rows = slice(v * SUB, (v + 1) * SUB)
        out_r[v] = xr[rows, :] + p8r * cr - p8i * ci
        out_i[v] = xi[rows, :] + p8r * ci + p8i * cr
        edge = slice(0, 1) if reverse else slice(SUB - 1, SUB)
        cr, ci = out_r[v][edge, :], out_i[v][edge, :]
    return jnp.concatenate(out_r, axis=0), jnp.concatenate(out_i, axis=0)


def _dep_args(dep):
    return ([], []) if dep is None else ([pl.BlockSpec(memory_space=pl.ANY)], [dep])


def _mm_nt(a, w, tm, tn, name, add=None, add_scale=1.0, dep=None):
    m, k = a.shape
    n = w.shape[0]

    def body(a_ref, w_ref, *rest):
        o_ref = rest[-1]
        if add is None:
            o_ref[...] = _dot_nt(a_ref[...], w_ref[...])
        else:
            o_ref[...] = _dot_nt(a_ref[...], w_ref[...]) + add_scale * rest[0][...]

    in_specs = [pl.BlockSpec((tm, k), lambda j, i: (i, 0)), pl.BlockSpec((tn, k), lambda j, i: (j, 0))]
    args = [a, w]
    if add is not None:
        in_specs.append(pl.BlockSpec((tm, tn), lambda j, i: (i, j)))
        args.append(add)
    dep_specs, dep_ops = _dep_args(dep)
    return pl.pallas_call(
        body, out_shape=jax.ShapeDtypeStruct((m, n), F32), grid=(n // tn, m // tm),
        in_specs=in_specs + dep_specs, out_specs=pl.BlockSpec((tm, tn), lambda j, i: (i, j)), name=name,
        compiler_params=_cparams(("parallel", "parallel")))(*args, *dep_ops)


def _mm_dw(at, b, tm, tn, ts, name, out_dtype=F32):
    m, s = at.shape
    n = b.shape[1]
    nk = s // ts

    def body(a_ref, b_ref, o_ref, acc):
        @pl.when(pl.program_id(2) == 0)
        def _():
            acc[...] = jnp.zeros_like(acc)
        acc[...] += _dot(a_ref[...], b_ref[...])

        @pl.when(pl.program_id(2) == nk - 1)
        def _():
            o_ref[...] = acc[...].astype(out_dtype)

    return pl.pallas_call(
        body, out_shape=jax.ShapeDtypeStruct((m, n), out_dtype), grid=(m // tm, n // tn, nk),
        in_specs=[pl.BlockSpec((tm, ts), lambda i, j, k: (i, k)), pl.BlockSpec((ts, tn), lambda i, j, k: (k, j))],
        out_specs=pl.BlockSpec((tm, tn), lambda i, j, k: (i, j)),
        scratch_shapes=[pltpu.VMEM((tm, tn), F32)], name=name,
        compiler_params=_cparams(("parallel", "parallel", "arbitrary")))(at, b)


def _mm_up_dw(ht, dup, name, out_dtype=F32):
    d, s = ht.shape

    def body(a_ref, b_ref, o_ref):
        o_ref[...] = _dot(a_ref[...], b_ref[...]).astype(out_dtype)

    return pl.pallas_call(
        body, out_shape=jax.ShapeDtypeStruct((N_DEV, d, FFN_CB), out_dtype), grid=(N_DEV,),
        in_specs=[pl.BlockSpec((d, s), lambda j: (0, 0)), pl.BlockSpec((s, FFN_CB), lambda j: (0, j))],
        out_specs=pl.BlockSpec((None, d, FFN_CB), lambda j: (_ffn_dev(j), 0, 0)), name=name,
        compiler_params=_cparams(("parallel",)))(ht, dup)


def _layer_norm(r, g, b):
    mu = jnp.mean(r, axis=-1, keepdims=True)
    xc = r - mu
    var = jnp.mean(xc * xc, axis=-1, keepdims=True)
    return xc * lax.rsqrt(var + LN_EPS) * g + b


def _proj_ln(a, w, resid, g, bias, name, transposed=True, target=None):
    s, k = a.shape
    d = w.shape[1]
    tm = 512

    def body(a_ref, w_ref, x_ref, g_ref, bias_ref, *rest):
        r = ALPHA * x_ref[...] + _dot(a_ref[...], w_ref[...])
        h = _layer_norm(r, g_ref[...], bias_ref[...])
        if target is None:
            r_ref, h_ref = rest[0], rest[1]
            h_ref[...] = h
            if transposed:
                rest[2][...] = h.T.astype(BF16)
        else:
            t_ref, r_ref, dy_ref, l_ref = rest

            @pl.when(pl.program_id(0) == 0)
            def _():
                l_ref[...] = jnp.zeros_like(l_ref)
            e = h - t_ref[...]
            dy_ref[...] = e * (1.0 / d)
            part = 0.5 * jnp.sum(jnp.mean(e * e, axis=-1, keepdims=True), axis=0, keepdims=True)
            l_ref[...] += jnp.broadcast_to(part, l_ref.shape)
        r_ref[...] = r

    row = pl.BlockSpec((tm, d), lambda i: (i, 0))
    vec = pl.BlockSpec((1, d), lambda i: (0, 0))
    in_specs = [pl.BlockSpec((tm, k), lambda i: (i, 0)), pl.BlockSpec((k, d), lambda i: (0, 0)), row, vec, vec]
    args = [a, w, resid, g, bias]
    shapes = [jax.ShapeDtypeStruct((s, d), F32), jax.ShapeDtypeStruct((s, d), F32)]
    specs = [row, row]
    if target is not None:
        in_specs.append(row)
        args.append(target)
        shapes.append(jax.ShapeDtypeStruct((1, LANE), F32))
        specs.append(pl.BlockSpec((1, LANE), lambda i: (0, 0)))
    elif transposed:
        shapes.append(jax.ShapeDtypeStruct((d, s), BF16))
        specs.append(pl.BlockSpec((d, tm), lambda i: (0, i)))
    return pl.pallas_call(
        body, out_shape=tuple(shapes), grid=(s // tm,), in_specs=in_specs, out_specs=tuple(specs), name=name,
        compiler_params=_cparams(("arbitrary",) if target is not None else ("parallel",)))(*args)


def _layer_norm_bwd(r, dh, g):
    mu = jnp.mean(r, axis=-1, keepdims=True)
    xc = r - mu
    var = jnp.mean(xc * xc, axis=-1, keepdims=True)
    rstd = lax.rsqrt(var + LN_EPS)
    xh = xc * rstd
    dxh = dh * g
    m1 = jnp.mean(dxh, axis=-1, keepdims=True)
    m2 = jnp.mean(dxh * xh, axis=-1, keepdims=True)
    return (rstd * (dxh - m1 - xh * m2), jnp.sum(dh * xh, axis=0, keepdims=True),
            jnp.sum(dh, axis=0, keepdims=True))


def _ln_bwd(r, dh, g, name, dep=None):
    s, d = r.shape
    tm = 512

    def body(r_ref, dh_ref, g_ref, *rest):
        dr_ref, dg_ref, db_ref = rest[-3:]

        @pl.when(pl.program_id(0) == 0)
        def _():
            dg_ref[...] = jnp.zeros_like(dg_ref)
            db_ref[...] = jnp.zeros_like(db_ref)
        dr_ref[...], dg_rows, db_rows = _layer_norm_bwd(r_ref[...], dh_ref[...], g_ref[...])
        dg_ref[...] += dg_rows
        db_ref[...] += db_rows

    row = pl.BlockSpec((tm, d), lambda i: (i, 0))
    vec = pl.BlockSpec((1, d), lambda i: (0, 0))
    dep_specs, dep_ops = _dep_args(dep)
    return pl.pallas_call(
        body, out_shape=(jax.ShapeDtypeStruct((s, d), F32), jax.ShapeDtypeStruct((1, d), F32),
                         jax.ShapeDtypeStruct((1, d), F32)),
        grid=(s // tm,), in_specs=[row, row, vec] + dep_specs, out_specs=(row, vec, vec), name=name,
        compiler_params=_cparams(("arbitrary",)))(r, dh, g, *dep_ops)


def _rope_tables(s):
    half = HEAD // 2
    pos = jnp.arange(s, dtype=F32)
    inv = ROPE_THETA ** (-jnp.arange(half, dtype=F32) * 2.0 / HEAD)
    ang = pos[:, None] * inv[None, :]
    cos, sin = jnp.cos(ang), jnp.sin(ang)
    cos = jnp.concatenate([cos, cos, cos, cos], axis=1)
    sin = jnp.concatenate([-sin, sin, -sin, sin], axis=1)
    return cos, sin


def _rotate(x, cos, sin):
    lane = lax.broadcasted_iota(jnp.int32, x.shape, 1)
    partner = jnp.where((lane % HEAD) < HEAD // 2, pltpu.roll(x, LANE - HEAD // 2, axis=1),
                        pltpu.roll(x, HEAD // 2, axis=1))
    return x * cos + partner * sin


def _class_rows(c, d, tm):
    return pl.ds(c, tm // d, stride=d) if d > 1 else pl.ds(0, tm)


def _dilated_spec(tm, d, w):
    return pl.BlockSpec((tm // d, d * w), lambda i: (i, 0))


def _token_scratch(tm, w):
    return pltpu.VMEM((w // LANE, tm, LANE), F32)


def _to_tokens(src_ref, dst3, d, tm):
    nj = dst3.shape[0]
    for cls in range(d):
        for j in range(nj):
            col = (cls * nj + j) * LANE
            dst3.at[j][_class_rows(cls, d, tm), :] = src_ref[:, col:col + LANE]


def _to_dilated(src3, dst_ref, d, tm):
    nj = src3.shape[0]
    for cls in range(d):
        for j in range(nj):
            col = (cls * nj + j) * LANE
            dst_ref[:, col:col + LANE] = src3.at[j][_class_rows(cls, d, tm), :].astype(dst_ref.dtype)


def _token_value(src3):
    return jnp.concatenate([src3[j] for j in range(src3.shape[0])], axis=1)


def _proj_rope(h, w_in, cos, sin, name, dep=None, transposed=False):
    s, d_model = h.shape
    tm = 512
    w = 3 * ATTN_W
    nj = w // LANE

    def body(h_ref, w_ref, c_ref, s_ref, *rest):
        rot = rest[-1]
        if transposed:
            p_ref, o_refs, ht_ref = rest[-6], rest[-5:-2], rest[-2]
            ht_ref[...] = h_ref[...].T.astype(BF16)
        else:
            p_ref, o_refs = rest[-5], rest[-4:-1]
        y = _dot(h_ref[...], w_ref[...])
        p_ref[...] = y
        c, sn = c_ref[...], s_ref[...]
        for j in range(nj):
            x = y[:, j * LANE:(j + 1) * LANE]
            rot[j] = _rotate(x, c, sn) if j < 2 * ATTN_W // LANE else x
        for d, o_ref in zip(DILATIONS, o_refs):
            _to_dilated(rot, o_ref, d, tm)

    tab = pl.BlockSpec((tm, LANE), lambda i: (i, 0))
    dep_specs, dep_ops = _dep_args(dep)
    shapes = [jax.ShapeDtypeStruct((s, D_IN), F32), *[jax.ShapeDtypeStruct((s // d, d * w), BF16) for d in DILATIONS]]
    specs = [pl.BlockSpec((tm, D_IN), lambda i: (i, 0)), *[_dilated_spec(tm, d, w) for d in DILATIONS]]
    if transposed:
        shapes.append(jax.ShapeDtypeStruct((d_model, s), BF16))
        specs.append(pl.BlockSpec((d_model, tm), lambda i: (0, i)))
    res = pl.pallas_call(
        body, out_shape=tuple(shapes), grid=(s // tm,),
        in_specs=[pl.BlockSpec((tm, d_model), lambda i: (i, 0)), pl.BlockSpec((d_model, D_IN), lambda i: (0, 0)),
                  tab, tab] + dep_specs,
        out_specs=tuple(specs), scratch_shapes=[_token_scratch(tm, w)], name=name,
        compiler_params=_cparams(("parallel",)))(h, w_in, cos, sin, *dep_ops)
    return res[0], res[1:4], (res[4] if transposed else None)


def _dproj_assemble(dqkv_list, dxr, dgate, du, cos, sin, name):
    s = dxr.shape[0]
    tm = 512
    nq = 3 * ATTN_W // LANE

    def body(*refs):
        br = refs[:9]
        dxr_ref, dg_ref, du_ref, c_ref, s_ref, o_ref = refs[9:15]
        tok = refs[15:]
        c, sn = c_ref[...], -s_ref[...]
        for part in range(3):
            for b, d in enumerate(DILATIONS[1:], start=1):
                _to_tokens(br[3 * b + part], tok[2 * part + b - 1], d, tm)
        for j in range(nq):
            part, jj = divmod(j, ATTN_W // LANE)
            x = br[part][:, jj * LANE:(jj + 1) * LANE] + tok[2 * part][jj] + tok[2 * part + 1][jj]
            if part < 2:
                x = _rotate(x, c, sn)
            o_ref[:, j * LANE:(j + 1) * LANE] = x.astype(BF16)
        o_ref[:, 3 * ATTN_W:3 * ATTN_W + LRU_W] = dxr_ref[...].astype(BF16)
        o_ref[:, 3 * ATTN_W + LRU_W:3 * ATTN_W + 2 * LRU_W] = dg_ref[...].astype(BF16)
        o_ref[:, 3 * ATTN_W + 2 * LRU_W:] = du_ref[...].astype(BF16)

    a_spec = pl.BlockSpec((tm, ATTN_W), lambda i: (i, 0))
    tab = pl.BlockSpec((tm, LANE), lambda i: (i, 0))
    ordered = [dqkv_list[b][p] for b in range(3) for p in range(3)]
    d_specs = [_dilated_spec(tm, d, ATTN_W) for d in DILATIONS for _ in range(3)]
    return pl.pallas_call(
        body, out_shape=jax.ShapeDtypeStruct((s, D_IN), BF16), grid=(s // tm,),
        in_specs=d_specs + [a_spec, a_spec, pl.BlockSpec((tm, S5_W), lambda i: (i, 0)), tab, tab],
        out_specs=pl.BlockSpec((tm, D_IN), lambda i: (i, 0)),
        scratch_shapes=[_token_scratch(tm, ATTN_W)] * 6, name=name,
        compiler_params=_cparams(("parallel",)))(*ordered, dxr, dgate, du, cos, sin)


def _attn_tiles(s, d):
    m = s // d
    tq = min(m, ATTN_TILE)
    return m, tq, tq // ATTN_BLK


def _band_mask(qb):
    qi = lax.broadcasted_iota(jnp.int32, (ATTN_BLK, 2 * ATTN_BLK), 0)
    ki = lax.broadcasted_iota(jnp.int32, (ATTN_BLK, 2 * ATTN_BLK), 1)
    dist = qi + ATTN_BLK - ki
    return (dist >= 0) & (dist <= ATTN_BLK) & ((ki >= ATTN_BLK) | (qb > 0))


def _attn_fwd(qv, d, name):
    m = qv.shape[0]
    w3 = 3 * ATTN_W
    _, tq, n = _attn_tiles(m * d, d)
    scale = HEAD ** -0.5
    assert math.frexp(scale)[0] == 0.5, "the kernel scales bf16 q: exact only for a power of two"

    def body(x_ref, p_ref, o_ref, l_ref):
        b = pl.program_id(1)

        def block(i, first):
            r0 = 0 if first else pl.multiple_of(i * ATTN_BLK, ATTN_BLK)
            rows = pl.ds(r0, ATTN_BLK)
            valid = _band_mask(b * n + i)
            if not first:
                krows = pl.ds(pl.multiple_of(i * ATTN_BLK - ATTN_BLK, ATTN_BLK), 2 * ATTN_BLK)
            low = lax.broadcasted_iota(jnp.int32, (1, LANE), 1) < HEAD
            for hp in range(ATTN_W // LANE):
                qs, ks, vs = (slice(part * ATTN_W + hp * LANE, part * ATTN_W + (hp + 1) * LANE) for part in range(3))
                q2 = x_ref[rows, qs] * scale
                if first:
                    k2 = jnp.concatenate([p_ref[:, ks], x_ref[0:ATTN_BLK, ks]], axis=0)
                    v2 = jnp.concatenate([p_ref[:, vs], x_ref[0:ATTN_BLK, vs]], axis=0)
                else:
                    k2 = x_ref[krows, ks]
                    v2 = x_ref[krows, vs]
                outs, lses = [], []
                for mask in (low, ~low):
                    q = jnp.where(mask, q2, jnp.zeros_like(q2))
                    sc = jnp.where(valid, _dot_nt(q, k2), -1e30)
                    mx = jnp.max(sc, axis=-1, keepdims=True)
                    p = jnp.exp(sc - mx)
                    l = jnp.sum(p, axis=-1, keepdims=True)
                    outs.append(_dot(p, v2) / l)
                    lses.append(mx + jnp.log(l))
                o_ref[rows, hp * LANE:(hp + 1) * LANE] = jnp.where(low, outs[0], outs[1])
                l_ref[rows, hp * LANE:(hp + 1) * LANE] = jnp.where(low, lses[0], lses[1])

        block(0, True)
        if n > 1:
            def loop(i, carry):
                block(i, False)
                return carry
            lax.fori_loop(1, n, loop, 0, unroll=2)

    shp = jax.ShapeDtypeStruct((m, d * ATTN_W), F32)
    ospec =pl.BlockSpec((tq, ATTN_W), lambda c, b: (b, c))
    out, lse = pl.pallas_call(
        body, out_shape=(shp, shp), grid=(d, m // tq),
        in_specs=[pl.BlockSpec((tq, w3), lambda c, b: (b, c)),
                  pl.BlockSpec((ATTN_BLK, w3), lambda c, b: (jnp.maximum(b * n - 1, 0), c))],
        out_specs=(ospec, ospec), name=name,
        compiler_params=_cparams(("parallel", "parallel")))(qv, qv)
    return out, lse


def _attn_bwd(qv, ov, dov, lv, d, name, dep=None):
    m = qv.shape[0]
    w3 = 3 * ATTN_W
    _, tq, n = _attn_tiles(m * d, d)
    nb = m // ATTN_BLK
    scale = HEAD ** -0.5
    assert math.frexp(scale)[0] == 0.5, "the kernel scales bf16 q: exact only for a power of two"

    def body(x_ref, p_ref, nx_ref, o_ref, do_ref, l_ref, on_ref, don_ref, ln_ref, *rest):
        dq_ref, dk_ref, dv_ref = rest[-3:]
        b = pl.program_id(1)
        dk_ref[...] = jnp.zeros_like(dk_ref)
        dv_ref[...] = jnp.zeros_like(dv_ref)

        low = lax.broadcasted_iota(jnp.int32, (1, LANE), 1) < HEAD

        def pair_grads(q2, k2, v2, o2, do2, l2, valid):
            dq, dk, dv = [], 0.0, 0.0
            q2 = q2 * scale
            for mask, lse in ((low, l2[:, 0:1]), (~low, l2[:, HEAD:HEAD + 1])):
                q = jnp.where(mask, q2, jnp.zeros_like(q2))
                do = jnp.where(mask, do2, 0.0)
                sc = jnp.where(valid, _dot_nt(q, k2), -1e30)
                p = jnp.exp(sc - lse)
                delta = jnp.sum(do * o2, axis=-1, keepdims=True)
                ds = p * (_dot_nt(do, v2) - delta)
                dq.append(_dot(ds, k2))
                dk = dk + _dot_tn(ds, q)
                dv = dv + _dot_tn(p, do)
            return jnp.where(low, dq[0], dq[1]) * scale, dk, dv

        def cols(hp):
            return [slice(part * ATTN_W + hp * LANE, part * ATTN_W + (hp + 1) * LANE) for part in range(3)]

        def block(i, first):
            r0 = 0 if first else pl.multiple_of(i * ATTN_BLK, ATTN_BLK)
            rows = pl.ds(r0, ATTN_BLK)
            valid = _band_mask(b * n + i)
            if not first:
                krows = pl.ds(pl.multiple_of(i * ATTN_BLK - ATTN_BLK, ATTN_BLK), 2 * ATTN_BLK)
            for hp in range(ATTN_W // LANE):
                qs, ks, vs = cols(hp)
                if first:
                    k2 = jnp.concatenate([p_ref[:, ks], x_ref[0:ATTN_BLK, ks]], axis=0)
                    v2 = jnp.concatenate([p_ref[:, vs], x_ref[0:ATTN_BLK, vs]], axis=0)
                else:
                    k2 = x_ref[krows, ks]
                    v2 = x_ref[krows, vs]
                dq, dk, dv = pair_grads(x_ref[rows, qs], k2, v2, o_ref[rows, qs], do_ref[rows, qs],
                                        l_ref[rows, qs], valid)
                dq_ref[rows, qs] = dq
                if first:
                    dk_ref[0:ATTN_BLK, qs] += dk[ATTN_BLK:, :]
                    dv_ref[0:ATTN_BLK, qs] += dv[ATTN_BLK:, :]
                else:
                    dk_ref[krows, qs] += dk
                    dv_ref[krows, qs] += dv

        block(0, True)
        if n > 1:
            def loop(i, carry):
                block(i, False)
                return carry
            lax.fori_loop(1, n, loop, 0, unroll=2)

        last = slice((n - 1) * ATTN_BLK, n * ATTN_BLK)
        qi = lax.broadcasted_iota(jnp.int32, (ATTN_BLK, ATTN_BLK), 0)
        ki = lax.broadcasted_iota(jnp.int32, (ATTN_BLK, ATTN_BLK), 1)
        valid_next = (qi <= ki) & ((b + 1) * n < nb)
        for hp in range(ATTN_W // LANE):
            qs, ks, vs = cols(hp)
            _, dk, dv = pair_grads(nx_ref[:, qs], x_ref[last, ks], x_ref[last, vs], on_ref[:, qs], don_ref[:, qs],
                                   ln_ref[:, qs], valid_next)
            dk_ref[last, qs] += dk
            dv_ref[last, qs] += dv

    nxt = lambda b: jnp.minimum((b + 1) * n, nb - 1)
    xs = pl.BlockSpec((tq, w3), lambda c, b: (b, c))
    xp = pl.BlockSpec((ATTN_BLK, w3), lambda c, b: (jnp.maximum(b * n - 1, 0), c))
    xn = pl.BlockSpec((ATTN_BLK, w3), lambda c, b: (nxt(b), c))
    a = pl.BlockSpec((tq, ATTN_W), lambda c, b: (b, c))
    an = pl.BlockSpec((ATTN_BLK, ATTN_W), lambda c, b: (nxt(b), c))
    shp = jax.ShapeDtypeStruct((m, d * ATTN_W), F32)
    dep_specs, dep_ops = _dep_args(dep)
    return pl.pallas_call(
        body, out_shape=(shp, shp, shp), grid=(d, m // tq),
        in_specs=[xs, xp, xn, a, a, a, an, an, an] + dep_specs, out_specs=(a, a, a), name=name,
        compiler_params=_cparams(("parallel", "parallel")))(qv, qv, qv, ov, dov, lv, ov, dov, lv, *dep_ops)


def _rms(x, g):
    ms = jnp.mean(x * x, axis=-1, keepdims=True)
    return x * lax.rsqrt(ms + RMS_EPS) * g


def _rms_bwd(x, g, dy):
    ms = jnp.mean(x * x, axis=-1, keepdims=True)
    r = lax.rsqrt(ms + RMS_EPS)
    dyg = dy * g
    dx = r * dyg - x * (r * r * r) * jnp.mean(x * dyg, axis=-1, keepdims=True)
    return dx, dy * x * r


def _mix_fwd(outs, lses, lru, s5, g, h_in, w_out, ln_g, ln_b, name):
    s = lru.shape[0]
    tm = 512

    def body(o1, o2, o3, l1, l2, l3, lru_ref, s5_ref, g_ref, x_ref, w_ref, lg_ref, lb_ref,
             mixed_t_ref, r_ref, h_ref, ht_ref, ov1, ov2, ov3, lv1, lv2, lv3, so2, so3, sl2, sl3):
        for d, src, dst in ((DILATIONS[1], o2, so2), (DILATIONS[2], o3, so3),
                            (DILATIONS[1], l2, sl2), (DILATIONS[2], l3, sl3)):
            _to_tokens(src, dst, d, tm)
        a1, a2, a3 = l1[...], _token_value(sl2), _token_value(sl3)
        mx = jnp.maximum(jnp.maximum(a1, a2), a3)
        e1, e2, e3 = jnp.exp(a1 - mx), jnp.exp(a2 - mx), jnp.exp(a3 - mx)
        den = e1 + e2 + e3
        o = (e1 * o1[...] + e2 * _token_value(so2) + e3 * _token_value(so3)) / den
        lse = mx + jnp.log(den)
        ov1[...] = o
        lv1[...] = lse
        for j in range(ATTN_W // LANE):
            so2[j] = o[:, j * LANE:(j + 1) * LANE]
            sl2[j] = lse[:, j * LANE:(j + 1) * LANE]
        for d, o_dst, l_dst in ((DILATIONS[1], ov2, lv2), (DILATIONS[2], ov3, lv3)):
            _to_dilated(so2, o_dst, d, tm)
            _to_dilated(sl2, l_dst, d, tm)
        gg = g_ref[...]
        mixed = jnp.concatenate([_rms(o, gg[:, :ATTN_W]),
                                 _rms(lru_ref[...], gg[:, ATTN_W:ATTN_W + LRU_W]),
                                 _rms(s5_ref[...], gg[:, ATTN_W + LRU_W:])], axis=1)
        mixed_t_ref[...] = mixed.T.astype(BF16)
        r = ALPHA * x_ref[...] + _dot(mixed, w_ref[...])
        h = _layer_norm(r, lg_ref[...], lb_ref[...])
        r_ref[...] = r
        h_ref[...] = h
        ht_ref[...] = h.T.astype(BF16)

    a = pl.BlockSpec((tm, ATTN_W), lambda i: (i, 0))
    s5s = pl.BlockSpec((tm, S5_W), lambda i: (i, 0))
    full = pl.BlockSpec((tm, D_MODEL), lambda i: (i, 0))
    vec = pl.BlockSpec((1, D_MODEL), lambda i: (0, 0))
    dil = [_dilated_spec(tm, d, ATTN_W) for d in DILATIONS]
    dshape = [jax.ShapeDtypeStruct((s // d, d * ATTN_W), F32) for d in DILATIONS]
    tshape = jax.ShapeDtypeStruct((D_MODEL, s), BF16)
    fshape = jax.ShapeDtypeStruct((s, D_MODEL), F32)
    tspec = pl.BlockSpec((D_MODEL, tm), lambda i: (0, i))
    res = pl.pallas_call(
        body, out_shape=(tshape, fshape, fshape, tshape, *dshape, *dshape),
        grid=(s // tm,),
        in_specs=dil + dil + [a, s5s, vec, full, pl.BlockSpec((D_MODEL, D_MODEL), lambda i: (0, 0)), vec, vec],
        out_specs=(tspec, full, full, tspec, *dil, *dil),
        scratch_shapes=[_token_scratch(tm, ATTN_W)] * 4, name=name,
        compiler_params=_cparams(("parallel",)))(*outs, *lses, lru, s5, g, h_in, w_out, ln_g, ln_b)
    return res[0], res[1], res[2], res[3], res[4:7], res[7:10]


def _mix_bwd(r, dh, ln_g, w_out, o, lru, s5, g, name, dep=None):
    s = lru.shape[0]
    tm = 512

    def body(r_ref, dh_ref, lg_ref, w_ref, o_ref, lru_ref, s5_ref, g_ref, *rest):
        dr_ref, dlg_ref, dlb_ref, do_ref, do2_ref, do3_ref, dlru_ref, ds5_ref, dg_ref, stage = rest[-10:]

        @pl.when(pl.program_id(0) == 0)
        def _():
            dg_ref[...] = jnp.zeros_like(dg_ref)
            dlg_ref[...] = jnp.zeros_like(dlg_ref)
            dlb_ref[...] = jnp.zeros_like(dlb_ref)
        gg = g_ref[...]
        dr, dlg_rows, dlb_rows = _layer_norm_bwd(r_ref[...], dh_ref[...], lg_ref[...])
        dr_ref[...] = dr
        dlg_ref[...] += dlg_rows
        dlb_ref[...] += dlb_rows
        dm = _dot_nt(dr, w_ref[...])
        dx, dgr = _rms_bwd(o_ref[...], gg[:, :ATTN_W], dm[:, :ATTN_W])
        do_ref[...] = dx
        for j in range(ATTN_W // LANE):
            stage[j] = dx[:, j * LANE:(j + 1) * LANE]
        _to_dilated(stage, do2_ref, DILATIONS[1], tm)
        _to_dilated(stage, do3_ref, DILATIONS[2], tm)
        dg_ref[:, :ATTN_W] += jnp.sum(dgr, axis=0, keepdims=True)
        dx, dgr = _rms_bwd(lru_ref[...], gg[:, ATTN_W:ATTN_W + LRU_W], dm[:, ATTN_W:ATTN_W + LRU_W])
        dlru_ref[...] = dx
        dg_ref[:, ATTN_W:ATTN_W + LRU_W] += jnp.sum(dgr, axis=0, keepdims=True)
        dx, dgr = _rms_bwd(s5_ref[...], gg[:, ATTN_W + LRU_W:], dm[:, ATTN_W + LRU_W:])
        ds5_ref[...] = dx
        dg_ref[:, ATTN_W + LRU_W:] += jnp.sum(dgr, axis=0, keepdims=True)

    a = pl.BlockSpec((tm, ATTN_W), lambda i: (i, 0))
    s5s = pl.BlockSpec((tm, S5_W), lambda i: (i, 0))
    full = pl.BlockSpec((tm, D_MODEL), lambda i: (i, 0))
    vec = pl.BlockSpec((1, D_MODEL), lambda i: (0, 0))
    dil = [_dilated_spec(tm, d, ATTN_W) for d in DILATIONS]
    dshape = [jax.ShapeDtypeStruct((s // d, d * ATTN_W), F32) for d in DILATIONS]
    dep_specs, dep_ops = _dep_args(dep)
    vshape = jax.ShapeDtypeStruct((1, D_MODEL), F32)
    res = pl.pallas_call(
        body, out_shape=(jax.ShapeDtypeStruct((s, D_MODEL), F32), vshape, vshape, *dshape,
                         jax.ShapeDtypeStruct((s, LRU_W), F32), jax.ShapeDtypeStruct((s, S5_W), F32), vshape),
        grid=(s // tm,),
        in_specs=[full, full, vec, pl.BlockSpec((D_MODEL, D_MODEL), lambda i: (0, 0)), a, a, s5s, vec] + dep_specs,
        out_specs=(full, vec, vec, *dil, a, s5s, vec), scratch_shapes=[_token_scratch(tm, ATTN_W)], name=name,
        compiler_params=_cparams(("arbitrary",)))(r, dh, ln_g, w_out, o, lru, s5, g, *dep_ops)
    return res[0], res[1], res[2], res[3:6], res[6], res[7], res[8]


def _lru_gate_math(xc, pre_r, pre_i, lam):
    r = _sigmoid(pre_r)
    i = _sigmoid(pre_i)
    log_a = -LRU_C * r * _softplus(-lam)
    a = jnp.exp(log_a)
    u = jnp.sqrt(-_expm1(2.0 * log_a)) * (i * xc)
    return a, u


def _lru_conv(x, prev8, cw, cb):
    y = cb + cw[LRU_CONV - 1:LRU_CONV, :] * x
    for k in range(LRU_CONV - 1):
        y = y + cw[k:k + 1, :] * _shift_down_prev(x, LRU_CONV - 1 - k, prev8)
    return y


def _lru_specs(s):
    xo = 3 * ATTN_W // LANE
    go = xo + LRU_W // LANE
    xr = pl.BlockSpec((s, LANE), lambda j: (0, xo + j))
    gt = pl.BlockSpec((s, LANE), lambda j: (0, go + j))
    cw = pl.BlockSpec((LRU_CONV, LANE), lambda j: (0, j))
    vec = pl.BlockSpec((1, LANE), lambda j: (0, j))
    wbd = pl.BlockSpec((LANE, LANE), lambda j: (j, j))
    col = pl.BlockSpec((s, LANE), lambda j: (0, j))
    return xr, gt, cw, vec, wbd, col


def _lru_fwd(proj, cw, cb, wr, br, wi, bi, lam, name):
    s = proj.shape[0]
    t = SCAN_T

    def body(xr_ref, gt_ref, cw_ref, cb_ref, wr_ref, br_ref, wi_ref, bi_ref, lam_ref, o_ref, xc_ref, a_ref, h_ref):
        cwv, cbv, lamv = cw_ref[...], cb_ref[...], lam_ref[...]
        wrv, wiv, brv, biv = wr_ref[...], wi_ref[...], br_ref[...], bi_ref[...]

        def chunk(c, carry):
            h_c, prev8 = carry
            rows = pl.ds(pl.multiple_of(c * t, t), t)
            x = xr_ref[rows, :]
            xc = _lru_conv(x, prev8, cwv, cbv)
            a, u = _lru_gate_math(xc, _dot(xc, wrv) + brv, _dot(xc, wiv) + biv, lamv)
            h = _scan_chunk(a, u, h_c)
            xc_ref[rows, :] = xc
            a_ref[rows, :] = a
            h_ref[rows, :] = h
            o_ref[rows, :] = h * _gelu(gt_ref[rows, :])
            return h[t - 1:t, :], x[t - 8:t, :]

        lax.fori_loop(0, s // t, chunk, (jnp.zeros((1, LANE), F32), jnp.zeros((8, LANE), F32)))

    xr, gt, cws, vec, wbd, col = _lru_specs(s)
    shp = jax.ShapeDtypeStruct((s, LRU_W), F32)
    return pl.pallas_call(
        body, out_shape=(shp,) * 4, grid=(LRU_W // LANE,),
        in_specs=[xr, gt, cws, vec, wbd, vec, wbd, vec, vec], out_specs=(col,) * 4, name=name,
        compiler_params=_cparams(("parallel",)))(proj, proj, cw, cb, wr, br, wi, bi, lam)


def _lru_bwd(proj, dout, xc_all, a_all, h_all, cw, cb, wr, br, wi, bi, lam, name):
    s = proj.shape[0]
    t = SCAN_T
    nc = s // t

    def body(xr_ref, gt_ref, do_ref, xc_s, a_s, h_s, cw_ref, cb_ref, wr_ref, br_ref, wi_ref, bi_ref, lam_ref,
             dxr_ref, dgt_ref, dcw_ref, dcb_ref, dwr_ref, dbr_ref, dwi_ref, dbi_ref, dlam_ref):
        cwv, cbv, lamv = cw_ref[...], cb_ref[...], lam_ref[...]
        wrv, wiv, brv, biv = wr_ref[...], wi_ref[...], br_ref[...], bi_ref[...]
        z1 = jnp.zeros((1, LANE), F32)
        zw = jnp.zeros((LANE, LANE), F32)

        def bchunk(ci, carry):
            g_next, a_next, dxc_next8, dcw, dcb, dwr, dbr, dwi, dbi, dlam = carry
            c = nc - 1 - ci
            t0 = pl.multiple_of(c * t, t)
            rows = pl.ds(t0, t)
            before = pl.ds(pl.multiple_of(jnp.maximum(t0 - 8, 0), 8), 8)
            has_prev = (c > 0).astype(F32)
            x, gt, do = xr_ref[rows, :], gt_ref[rows, :], do_ref[rows, :]
            xc, a, h = xc_s[rows, :], a_s[rows, :], h_s[rows, :]
            prev8_h = h_s[before, :] * has_prev
            dgt_ref[rows, :] = do * h * _gelu_grad(gt)
            dh = do * _gelu(gt)
            a_plus = _shift_up_next(a, 1, jnp.broadcast_to(a_next, (8, LANE)))
            g = _scan_chunk(a_plus, dh, g_next, reverse=True)
            da = g * _shift_down_prev(h, 1, prev8_h)
            pre_r = _dot(xc, wrv) + brv
            pre_i = _dot(xc, wiv) + biv
            _, vjp = jax.vjp(_lru_gate_math, xc, pre_r, pre_i, lamv)
            dxc, dpre_r, dpre_i, dlam_c = vjp((da, g))
            dxc = dxc + _dot_nt(dpre_r, wrv) + _dot_nt(dpre_i, wiv)
            dx = cwv[LRU_CONV - 1:LRU_CONV, :] * dxc
            dcw_rows = [None] * LRU_CONV
            dcw_rows[LRU_CONV - 1] = jnp.sum(dxc * x, axis=0, keepdims=True)
            for k in range(LRU_CONV - 1):
                dxc_ahead = _shift_up_next(dxc, LRU_CONV - 1 - k, dxc_next8)
                dx = dx + cwv[k:k + 1, :] * dxc_ahead
                dcw_rows[k] = jnp.sum(dxc_ahead * x, axis=0, keepdims=True)
            dxr_ref[rows, :] = dx
            return (g[0:1, :], a[0:1, :], dxc[0:8, :],
                    dcw + jnp.concatenate(dcw_rows, axis=0),
                    dcb + jnp.sum(dxc, axis=0, keepdims=True),
                    dwr + _dot_tn(xc, dpre_r), dbr + jnp.sum(dpre_r, axis=0, keepdims=True),
                    dwi + _dot_tn(xc, dpre_i), dbi + jnp.sum(dpre_i, axis=0, keepdims=True),
                    dlam + dlam_c)

        init = (z1, z1, jnp.zeros((8, LANE), F32), jnp.zeros((LRU_CONV, LANE), F32), z1, zw, z1, zw, z1, z1)
        res = lax.fori_loop(0, nc, bchunk, init)
        dcw_ref[...] = res[3]
        dcb_ref[...] = res[4]
        dwr_ref[...] = res[5]
        dbr_ref[...] = res[6]
        dwi_ref[...] = res[7]
        dbi_ref[...] = res[8]
        dlam_ref[...] = res[9]

    xr, gt, cws, vec, wbd, col = _lru_specs(s)
    vshape = jax.ShapeDtypeStruct((1, LRU_W), F32)
    wshape = jax.ShapeDtypeStruct((LRU_W, LRU_W), F32)
    return pl.pallas_call(
        body,
        out_shape=(jax.ShapeDtypeStruct((s, LRU_W), F32), jax.ShapeDtypeStruct((s, LRU_W), F32),
                   jax.ShapeDtypeStruct((LRU_CONV, LRU_W), F32), vshape, wshape, vshape, wshape, vshape, vshape),
        grid=(LRU_W // LANE,),
        in_specs=[xr, gt, col, col, col, col, cws, vec, wbd, vec, wbd, vec, vec],
        out_specs=(col, col, cws, vec, wbd, vec, wbd, vec, vec), name=name,
        compiler_params=_cparams(("parallel",)))(proj, proj, dout, xc_all, a_all, h_all, cw, cb, wr, br, wi, bi,
                                                 lam)


def _s5_disc_math(a_re, a_im, log_step, bt_re, bt_im):
    step = jnp.exp(log_step)
    dt_re, dt_im = step * a_re, step * a_im
    mag = jnp.exp(dt_re)
    ab_re, ab_im = mag * jnp.cos(dt_im), mag * jnp.sin(dt_im)
    z_re, z_im = ab_re - 1.0, ab_im
    den = a_re * a_re + a_im * a_im
    f_re = (z_re * a_re + z_im * a_im) / den
    f_im = (z_im * a_re - z_re * a_im) / den
    bb_re = f_re * bt_re - f_im * bt_im
    bb_im = f_re * bt_im + f_im * bt_re
    return ab_re, ab_im, bb_re, bb_im


def _s5_disc_fwd(a_re, a_im, log_step, bt_re, bt_im, name):
    def body(ar, ai, ls, br, bi, o1, o2, o3, o4):
        r = _s5_disc_math(ar[...], ai[...], ls[...], br[...], bi[...])
        o1[...], o2[...], o3[...], o4[...] = r

    shp = jax.ShapeDtypeStruct(a_re.shape, F32)
    return pl.pallas_call(body, out_shape=(shp,) * 4, name=name)(a_re, a_im, log_step, bt_re, bt_im)


def _s5_disc_bwd(a_re, a_im, log_step, bt_re, bt_im, cts, name):
    def body(ar, ai, ls, br, bi, c1, c2, c3, c4, o1, o2, o3, o4, o5):
        _, vjp = jax.vjp(_s5_disc_math, ar[...], ai[...], ls[...], br[...], bi[...])
        r = vjp((c1[...], c2[...], c3[...], c4[...]))
        o1[...], o2[...], o3[...], o4[...], o5[...] = r

    shp = jax.ShapeDtypeStruct(a_re.shape, F32)
    return pl.pallas_call(body, out_shape=(shp,) * 5, name=name)(a_re, a_im, log_step, bt_re, bt_im, *cts)


def _s5_u_specs(s):
    uo = (3 * ATTN_W + 2 * LRU_W) // LANE
    return (pl.BlockSpec((s, LANE), lambda j: (0, uo)), pl.BlockSpec((s, LANE), lambda j: (0, uo + 1)))


def _s5_scan_fwd(proj, b_re, b_im, lam_re, lam_im, c_re, c_im, name):
    s = proj.shape[0]
    t = SCAN_T

    def body(u0_ref, u1_ref, bre_ref, bim_ref, lre_ref, lim_ref, cre_ref, cim_ref, xre_ref, xim_ref, y_ref):
        @pl.when(pl.program_id(0) == 0)
        def _():
            y_ref[...] = jnp.zeros_like(y_ref)
        lr, li = lre_ref[...], lim_ref[...]
        consts = _cscan_consts(lr, li, False)
        bre, bim, cre, cim = bre_ref[...], bim_ref[...], cre_ref[...], cim_ref[...]

        def chunk(c, carry):
            cr, ci = carry
            rows = pl.ds(pl.multiple_of(c * t, t), t)
            u = jnp.concatenate([u0_ref[rows, :], u1_ref[rows, :]], axis=1).astype(BF16)
            xr, xi = _cscan_chunk(_dot(u, bre), _dot(u, bim), consts, (cr, ci))
            xre_ref[rows, :] = xr
            xim_ref[rows, :] = xi
            y_ref[rows, :] += _dot(xr, cre) - _dot(xi, cim)
            return xr[t - 1:t, :], xi[t - 1:t, :]

        z = jnp.zeros((1, S5_BLK), F32)
        lax.fori_loop(0, s // t, chunk, (z, z))

    u0, u1 = _s5_u_specs(s)
    bsp = pl.BlockSpec((S5_W, S5_BLK), lambda j: (0, j))
    csp = pl.BlockSpec((S5_BLK, S5_W), lambda j: (j, 0))
    vec = pl.BlockSpec((1, S5_BLK), lambda j: (0, j))
    xsp = pl.BlockSpec((s, S5_BLK), lambda j: (0, j))
    ysp = pl.BlockSpec((s, S5_W), lambda j: (0, 0))
    xshape = jax.ShapeDtypeStruct((s, S5_STATES), F32)
    return pl.pallas_call(
        body, out_shape=(xshape, xshape, jax.ShapeDtypeStruct((s, S5_W), F32)),
        grid=(S5_STATES // S5_BLK,), in_specs=[u0, u1, bsp, bsp, vec, vec, csp, csp],
        out_specs=(xsp, xsp, ysp), name=name,
        compiler_params=_cparams(("arbitrary",)))(proj, proj, b_re, b_im, lam_re, lam_im, c_re, c_im)


def _s5_scan_bwd(proj, dy, du_init, x_re, x_im, b_re, b_im, lam_re, lam_im, c_re, c_im, name):
    s = proj.shape[0]
    t = SCAN_T
    nc = s // t

    def body(u0_ref, u1_ref, dy_ref, dui_ref, xre_ref, xim_ref, bre_ref, bim_ref, lre_ref, lim_ref,
             cre_ref, cim_ref, du_ref, dlr_ref, dli_ref, dbr_ref, dbi_ref, dcr_ref, dci_ref):
        @pl.when(pl.program_id(0) == 0)
        def _():
            du_ref[...] = dui_ref[...]
        mr, mi = lre_ref[...], -lim_ref[...]
        consts = _cscan_consts(mr, mi, True)
        bre, bim, cre, cim = bre_ref[...], bim_ref[...], cre_ref[...], cim_ref[...]
        dbr_ref[...] = jnp.zeros_like(dbr_ref)
        dbi_ref[...] = jnp.zeros_like(dbi_ref)
        dcr_ref[...] = jnp.zeros_like(dcr_ref)
        dci_ref[...] = jnp.zeros_like(dci_ref)

        def chunk(ci_, carry):
            gnr, gni, dlr, dli = carry
            c = nc - 1 - ci_
            t0 = pl.multiple_of(c * t, t)
            rows = pl.ds(t0, t)
            before = pl.ds(pl.multiple_of(jnp.maximum(t0 - 8, 0), 8), 8)
            has_prev = (c > 0).astype(F32)
            dyc = dy_ref[rows, :].astype(BF16)
            u = jnp.concatenate([u0_ref[rows, :], u1_ref[rows, :]], axis=1).astype(BF16)
            gr, gi = _cscan_chunk(_dot_nt(dyc, cre), -_dot_nt(dyc, cim), consts, (gnr, gni), reverse=True)
            xr, xi = xre_ref[rows, :], xim_ref[rows, :]
            xpr = _shift_down_prev(xr, 1, xre_ref[before, :] * has_prev)
            xpi = _shift_down_prev(xi, 1, xim_ref[before, :] * has_prev)
            dlr = dlr + jnp.sum(gr * xpr + gi * xpi, axis=0, keepdims=True)
            dli = dli + jnp.sum(gi * xpr - gr * xpi, axis=0, keepdims=True)
            du_ref[rows, :] += _dot_nt(gr, bre) + _dot_nt(gi, bim)
            dbr_ref[...] += _dot_tn(u, gr)
            dbi_ref[...] += _dot_tn(u, gi)
            dcr_ref[...] += _dot_tn(xr, dyc)
            dci_ref[...] -= _dot_tn(xi, dyc)
            return gr[0:1, :], gi[0:1, :], dlr, dli

        z = jnp.zeros((1, S5_BLK), F32)
        res = lax.fori_loop(0, nc, chunk, (z, z, z, z))
        dlr_ref[...] = res[2]
        dli_ref[...] = res[3]

    u0, u1 = _s5_u_specs(s)
    bsp = pl.BlockSpec((S5_W, S5_BLK), lambda j: (0, j))
    csp = pl.BlockSpec((S5_BLK, S5_W), lambda j: (j, 0))
    vec = pl.BlockSpec((1, S5_BLK), lambda j: (0, j))
    xsp = pl.BlockSpec((s, S5_BLK), lambda j: (0, j))
    ysp = pl.BlockSpec((s, S5_W), lambda j: (0, 0))
    return pl.pallas_call(
        body,
        out_shape=(jax.ShapeDtypeStruct((s, S5_W), F32),
                   jax.ShapeDtypeStruct((1, S5_STATES), F32), jax.ShapeDtypeStruct((1, S5_STATES), F32),
                   jax.ShapeDtypeStruct((S5_W, S5_STATES), F32), jax.ShapeDtypeStruct((S5_W, S5_STATES), F32),
                   jax.ShapeDtypeStruct((S5_STATES, S5_W), F32), jax.ShapeDtypeStruct((S5_STATES, S5_W), F32)),
        grid=(S5_STATES // S5_BLK,),
        in_specs=[u0, u1, ysp, ysp, xsp, xsp, bsp, bsp, vec, vec, csp, csp],
        out_specs=(ysp, vec, vec, bsp, bsp, csp, csp), name=name,
        compiler_params=_cparams(("arbitrary",)))(
            proj, proj, dy, du_init, x_re, x_im, b_re, b_im, lam_re, lam_im, c_re, c_im)


def _s5_out_fwd(proj, y_acc, dvec, w_glu, b_glu, name):
    s = proj.shape[0]
    tm = 512
    uo = (3 * ATTN_W + 2 * LRU_W) // LANE

    def body(u0_ref, u1_ref, y_ref, d_ref, w_ref, b_ref, o_ref, yp_ref):
        u = jnp.concatenate([u0_ref[...], u1_ref[...]], axis=1)
        y = y_ref[...] + d_ref[...] * u
        yp_ref[...] = y
        yg = _gelu(y)
        o_ref[...] = yg * _sigmoid(_dot(yg, w_ref[...]) + b_ref[...])

    u0 = pl.BlockSpec((tm, LANE), lambda i: (i, uo))
    u1 = pl.BlockSpec((tm, LANE), lambda i: (i, uo + 1))
    row = pl.BlockSpec((tm, S5_W), lambda i: (i, 0))
    vec = pl.BlockSpec((1, S5_W), lambda i: (0, 0))
    wsp = pl.BlockSpec((S5_W, S5_W), lambda i: (0, 0))
    shp = jax.ShapeDtypeStruct((s, S5_W), F32)
    return pl.pallas_call(
        body, out_shape=(shp, shp), grid=(s // tm,), in_specs=[u0, u1, row, vec, wsp, vec],
        out_specs=(row, row), name=name,
        compiler_params=_cparams(("parallel",)))(proj, proj, y_acc, dvec, w_glu, b_glu)


def _s5_out_bwd(proj, y_pre, dout, dvec, w_glu, b_glu, name, dep=None):
    s = proj.shape[0]
    tm = 512
    uo = (3 * ATTN_W + 2 * LRU_W) // LANE

    def body(u0_ref, u1_ref, y_ref, do_ref, d_ref, w_ref, b_ref, *rest):
        dy_ref, dud_ref, dd_ref, dw_ref, db_ref = rest[-5:]

        @pl.when(pl.program_id(0) == 0)
        def _():
            dd_ref[...] = jnp.zeros_like(dd_ref)
            dw_ref[...] = jnp.zeros_like(dw_ref)
            db_ref[...] = jnp.zeros_like(db_ref)
        u = jnp.concatenate([u0_ref[...], u1_ref[...]], axis=1)
        y = y_ref[...]
        do = do_ref[...]
        yg = _gelu(y)
        sg = _sigmoid(_dot(yg, w_ref[...]) + b_ref[...])
        dz = do * yg * sg * (1.0 - sg)
        dyg = do * sg + _dot_nt(dz, w_ref[...])
        dy = dyg * _gelu_grad(y)
        dy_ref[...] = dy
        dud_ref[...] = d_ref[...] * dy
        dd_ref[...] += jnp.sum(dy * u, axis=0, keepdims=True)
        dw_ref[...] += _dot_tn(yg, dz)
        db_ref[...] += jnp.sum(dz, axis=0, keepdims=True)

    u0 = pl.BlockSpec((tm, LANE), lambda i: (i, uo))
    u1 = pl.BlockSpec((tm, LANE), lambda i: (i, uo + 1))
    row = pl.BlockSpec((tm, S5_W), lambda i: (i, 0))
    vec = pl.BlockSpec((1, S5_W), lambda i: (0, 0))
    wsp = pl.BlockSpec((S5_W, S5_W), lambda i: (0, 0))
    shp = jax.ShapeDtypeStruct((s, S5_W), F32)
    vshape = jax.ShapeDtypeStruct((1, S5_W), F32)
    dep_specs, dep_ops = _dep_args(dep)
    return pl.pallas_call(
        body, out_shape=(shp, shp, vshape, jax.ShapeDtypeStruct((S5_W, S5_W), F32), vshape),
        grid=(s // tm,), in_specs=[u0, u1, row, row, vec, wsp, vec] + dep_specs,
        out_specs=(row, row, vec, wsp, vec), name=name,
        compiler_params=_cparams(("arbitrary",)))(proj, proj, y_pre, dout, dvec, w_glu, b_glu, *dep_ops)


def _ffn_conv(x, prev8, cw, cb):
    y = cb + cw[FFN_CONV - 1:FFN_CONV, :] * x
    for k in range(FFN_CONV - 1):
        y = y + cw[k:k + 1, :] * _shift_down_prev(x, FFN_CONV - 1 - k, prev8)
    return y


def _ffn_up_act(h, wg, cw, cb, name, dep=None):
    s, d = h.shape
    tm = 512
    tb = 2 * FFN_CB
    nt = D_FF // FFN_CB

    def body(h_ref, wgate_ref, wval_ref, cw_ref, cb_ref, *rest):
        up_ref, y_ref, o_ref, ot_ref, carry = rest[-5:]

        @pl.when(pl.program_id(1) == 0)
        def _():
            carry[...] = jnp.zeros_like(carry)
        hb = h_ref[...].astype(BF16)
        x = jnp.concatenate([_dot(hb, wgate_ref[...]), _dot(hb, wval_ref[...])], axis=1)
        up_ref[...] = x.astype(BF16)
        y = _ffn_conv(x, carry[...], cw_ref[...], cb_ref[...])
        y_ref[...] = y
        carry[...] = x[tm - 8:tm, :]
        act = _gelu(y[:, :FFN_CB]) * y[:, FFN_CB:]
        o_ref[...] = act.astype(BF16)
        ot_ref[...] = act.T.astype(BF16)

    dep_specs, dep_ops = _dep_args(dep)
    return pl.pallas_call(
        body, out_shape=(jax.ShapeDtypeStruct((s, 2 * D_FF), BF16), jax.ShapeDtypeStruct((s, 2 * D_FF), F32),
                         jax.ShapeDtypeStruct((s, D_FF), BF16), jax.ShapeDtypeStruct((D_FF, s), BF16)),
        grid=(nt, s // tm),
        in_specs=[pl.BlockSpec((tm, d), lambda t, i: (i, 0)),
                  pl.BlockSpec((None, d, FFN_CB), lambda t, i: (t, 0, 0)),
                  pl.BlockSpec((None, d, FFN_CB), lambda t, i: (t + nt, 0, 0)),
                  pl.BlockSpec((FFN_CONV, tb), lambda t, i: (0, t)),
                  pl.BlockSpec((1, tb), lambda t, i: (0, t))] + dep_specs,
        out_specs=(pl.BlockSpec((tm, tb), lambda t, i: (i, t)), pl.BlockSpec((tm, tb), lambda t, i: (i, t)),
                   pl.BlockSpec((tm, FFN_CB), lambda t, i: (i, t)), pl.BlockSpec((FFN_CB, tm), lambda t, i: (t, i))),
        scratch_shapes=[pltpu.VMEM((8, tb), F32)], name=name,
        compiler_params=_cparams(("parallel", "arbitrary")))(h, wg, wg, cw, cb, *dep_ops)


def _ffn_bwd(up, y_conv, dr, w_down, wg, cw, name):
    s = up.shape[0]
    d = dr.shape[1]
    tm = 512
    tb = 2 * FFN_CB
    nr = s // tm
    nt = D_FF // FFN_CB

    def body(x_ref, y_ref, dr_ref, wd_ref, wgate_ref, wval_ref, cw_ref,
             dup_ref, dh_ref, dcw_ref, dcb_ref, carry):
        i, t = pl.program_id(0), pl.program_id(1)

        @pl.when(i == 0)
        def _():
            carry[t] = jnp.zeros((8, tb), F32)

        @pl.when(t == 0)
        def _():
            dh_ref[...] = ALPHA * dr_ref[...]
        cwv = cw_ref[...]
        x = x_ref[...]
        dact = _dot_nt(dr_ref[...], wd_ref[...])
        gate, val = y_ref[:, :FFN_CB], y_ref[:, FFN_CB:]
        dy = jnp.concatenate([dact * val * _gelu_grad(gate), dact * _gelu(gate)], axis=1)
        next8 = carry[t]
        carry[t] = dy[0:8, :]
        dx = cwv[FFN_CONV - 1:FFN_CONV, :] * dy
        dcw_rows = [None] * FFN_CONV
        dcw_rows[FFN_CONV - 1] = jnp.sum(dy * x, axis=0, keepdims=True)
        for k in range(FFN_CONV - 1):
            dy_ahead = _shift_up_next(dy, FFN_CONV - 1 - k, next8)
            dx = dx + cwv[k:k + 1, :] * dy_ahead
            dcw_rows[k] = jnp.sum(dy_ahead * x, axis=0, keepdims=True)
        dup = dx.astype(BF16)
        dup_ref[...] = dup
        dh_ref[...] += _dot_nt(dup[:, :FFN_CB], wgate_ref[...]) + _dot_nt(dup[:, FFN_CB:], wval_ref[...])
        dcw_ref[...] = jnp.concatenate(dcw_rows, axis=0)
        dcb_ref[...] = jnp.sum(dy, axis=0, keepdims=True)

    row = lambda i: nr - 1 - i
    return pl.pallas_call(
        body, out_shape=(jax.ShapeDtypeStruct((s, 2 * D_FF), BF16), jax.ShapeDtypeStruct((s, d), F32),
                         jax.ShapeDtypeStruct((nr, FFN_CONV, 2 * D_FF), F32),
                         jax.ShapeDtypeStruct((nr, 1, 2 * D_FF), F32)),
        grid=(nr, nt),
        in_specs=[pl.BlockSpec((tm, tb), lambda i, t: (row(i), t)),
                  pl.BlockSpec((tm, tb), lambda i, t: (row(i), t)),
                  pl.BlockSpec((tm, d), lambda i, t: (row(i), 0)),
                  pl.BlockSpec((FFN_CB, d), lambda i, t: (t, 0)),
                  pl.BlockSpec((None, d, FFN_CB), lambda i, t: (t, 0, 0)),
                  pl.BlockSpec((None, d, FFN_CB), lambda i, t: (t + nt, 0, 0)),
                  pl.BlockSpec((FFN_CONV, tb), lambda i, t: (0, t))],
        out_specs=(pl.BlockSpec((tm, tb), lambda i, t: (row(i), t)),
                   pl.BlockSpec((tm, d), lambda i, t: (row(i), 0)),
                   pl.BlockSpec((None, FFN_CONV, tb), lambda i, t: (row(i), 0, t)),
                   pl.BlockSpec((None, 1, tb), lambda i, t: (row(i), 0, t))),
        scratch_shapes=[pltpu.VMEM((nt, 8, tb), F32)], name=name,
        compiler_params=_cparams(("arbitrary", "arbitrary")))(up, y_conv, dr, w_down, wg, wg, cw)


def _sum_partials(ld_ref):
    gg = ld_ref[0].astype(F32)
    for k in range(1, N_DEV):
        gg = gg + ld_ref[k].astype(F32)
    return gg


def _adam_update(w, g, m, v):
    mn = ADAM_B1 * m + (1.0 - ADAM_B1) * g
    vn = ADAM_B2 * v + (1.0 - ADAM_B2) * (g * g)
    m_hat = mn / (1.0 - ADAM_B1 ** ADAM_STEP)
    v_hat = vn / (1.0 - ADAM_B2 ** ADAM_STEP)
    return -ADAM_LR * (m_hat / (jnp.sqrt(v_hat) + ADAM_EPS) + ADAM_WD * w), mn, vn


def _adamw_many(landed, ws, ms, vs, name):
    n, nl = len(ws), len(landed)

    def body(*refs):
        ld = refs[:nl * n]
        w_refs, m_refs, v_refs = (refs[(nl + k) * n:(nl + k + 1) * n] for k in range(3))
        outs = refs[(nl + 3) * n:]
        for i in range(n):
            for l in range(nl):
                one = slice(l, l + 1)
                gg = _sum_partials(ld[l * n + i])
                outs[i][one] = gg
                outs[n + i][one], outs[2 * n + i][one], outs[3 * n + i][one] = _adam_update(
                    w_refs[i][one], gg, m_refs[i][one], v_refs[i][one])

    vm = pl.BlockSpec(memory_space=pltpu.VMEM)
    shapes = [jax.ShapeDtypeStruct(w.shape, F32) for w in ws] * 4
    res = pl.pallas_call(
        body, out_shape=tuple(shapes), in_specs=[vm] * ((nl + 3) * n), out_specs=tuple([vm] * (4 * n)),
        name=name, compiler_params=_cparams())(*[a for layer in landed for a in layer], *ws, *ms, *vs)
    return res[:n], res[n:2 * n], res[2 * n:3 * n], res[3 * n:]


def _adamw_sum(landed, w, m, v, layer, prev, name):
    _, r, c = landed.shape
    nl = w.shape[0]
    tm = 8
    for cand in (512, 256, 128, 64, 32, 16):
        if r % cand == 0 and N_DEV * cand * c * 4 <= 4 * 1024 * 1024:
            tm = cand
            break

    def body(*refs):
        ld_ref, w_ref, m_ref, v_ref = refs[:4]
        g_ref, d_ref, mo_ref, vo_ref = refs[-4:]
        gg = _sum_partials(ld_ref)
        g_ref[...] = gg
        d_ref[...], mo_ref[...], vo_ref[...] = _adam_update(w_ref[...], gg, m_ref[...], v_ref[...])

    blk = pl.BlockSpec((None, tm, c), lambda i: (layer, i, 0))
    in_specs = [pl.BlockSpec((N_DEV, tm, c), lambda i: (0, i, 0)), blk, blk, blk]
    args = [landed, w, m, v]
    aliases = {}
    if prev is not None:
        in_specs += [pl.BlockSpec(memory_space=pl.ANY)] * 4
        args += list(prev)
        aliases = {4 + k: k for k in range(4)}
    shp = jax.ShapeDtypeStruct((nl, r, c), F32)
    return pl.pallas_call(
        body, out_shape=(shp,) * 4, grid=(r // tm,), in_specs=in_specs, out_specs=(blk,) * 4,
        input_output_aliases=aliases, name=name, compiler_params=_cparams(("parallel",)))(*args)


def _all_gather(shards, name):
    na = len(shards)

    def body(*refs):
        x_refs, out_refs = refs[:na], refs[na:2 * na]
        send_sems, recv_sems, local_sems = refs[2 * na:]
        x, y, c = lax.axis_index("x"), lax.axis_index("y"), lax.axis_index("c")
        me, sibling = (x, y, c), (x, y, 1 - c)
        chips = [(1 - x, y), (x, 1 - y), (1 - x, 1 - y)]

        def copy(a, k, block, to, src=None):
            dst = out_refs[a].at[4 * block[0] + 2 * block[1] + block[2]]
            return pltpu.make_async_remote_copy(
                src_ref=dst if src is None else src, dst_ref=dst,
                send_sem=send_sems.at[7 * a + k], recv_sem=recv_sems.at[7 * a + k],
                device_id=to, device_id_type=pl.DeviceIdType.MESH)

        mine, first, passed = [], [], []
        for a in range(na):
            cp = pltpu.make_async_copy(x_refs[a], out_refs[a].at[4 * x + 2 * y + c], local_sems.at[a])
            cp.start()
            mine.append(cp)
            cps = [copy(a, 0, me, sibling, src=x_refs[a])]
            cps += [copy(a, 1 + j, me, (*chip, c), src=x_refs[a]) for j, chip in enumerate(chips)]
            for cp in cps:
                cp.start()
            first += cps
        for j, chip in enumerate(chips):
            for a in range(na):
                copy(a, 1 + j, (*chip, c), me).wait_recv()
                cp = copy(a, 4 + j, (*chip, c), sibling)
                cp.start()
                passed.append(cp)
        for a in range(na):
            copy(a, 0, sibling, me).wait_recv()
            for j, chip in enumerate(chips):
                copy(a, 4 + j, (*chip, 1 - c), me).wait_recv()
        for cp in first + passed:
            cp.wait_send()
        for cp in mine:
            cp.wait()

    anyspec = pl.BlockSpec(memory_space=pl.ANY)
    return pl.pallas_call(
        body, out_shape=tuple(jax.ShapeDtypeStruct((N_DEV,) + t.shape, t.dtype) for t in shards),
        in_specs=[anyspec] * na, out_specs=tuple([anyspec] * na),
        scratch_shapes=[pltpu.SemaphoreType.DMA((7 * na,)), pltpu.SemaphoreType.DMA((7 * na,)),
                        pltpu.SemaphoreType.DMA((na,))],
        name=name)(*shards)


_HBM = pl.BlockSpec(memory_space=pltpu.HBM)
_SEM = pl.BlockSpec(memory_space=pltpu.SEMAPHORE)
_EFFECT = pltpu.SideEffectType.DATAFLOW_SIDE_EFFECTING


def _exchange_copies(src_refs, land_refs, send_sems, recv_sems, local_sems, gather):
    x, y, c = lax.axis_index("x"), lax.axis_index("y"), lax.axis_index("c")
    me = 4 * x + 2 * y + c
    per_array = send_sems.shape[0] > N_DEV - 1
    local, remote = [], []
    for a, (src, land) in enumerate(zip(src_refs, land_refs)):
        local.append(pltpu.make_async_copy(src if gather else src.at[me], land.at[me],
                                           local_sems.at[a if per_array else 0]))
    for k in range(1, N_DEV):
        px = x ^ ((k >> 2) & 1)
        py = y ^ ((k >> 1) & 1)
        pc = c ^ (k & 1)
        for a, (src, land) in enumerate(zip(src_refs, land_refs)):
            remote.append(pltpu.make_async_remote_copy(
                src_ref=src if gather else src.at[4 * px + 2 * py + pc], dst_ref=land.at[me],
                send_sem=send_sems.at[(7 * a if per_array else 0) + k - 1],
                recv_sem=recv_sems.at[(7 * a if per_array else 0) + k - 1],
                device_id=(px, py, pc), device_id_type=pl.DeviceIdType.MESH))
    return local, remote


def _exchange_start(srcs, gather, name, dep=None):
    na = len(srcs)
    ns = na if na <= 4 else 1
    lands = [lax.empty(((N_DEV,) + t.shape) if gather else t.shape, t.dtype) for t in srcs]

    def body(*refs):
        src_refs, land_refs = refs[:na], refs[na:2 * na]
        nin = 2 * na + (0 if dep is None else 1)
        send_sems, recv_sems, local_sems = refs[nin:nin + 3]
        token = refs[-1]
        local, remote = _exchange_copies(src_refs, land_refs, send_sems, recv_sems, local_sems, gather)
        for cp in local + remote:
            cp.start()
        token[...] = jnp.zeros_like(token)

    dep_specs, dep_ops = _dep_args(dep)
    hbm = lambda t: pltpu.HBM(t.shape, t.dtype)
    out = pl.pallas_call(
        body, name=name,
        out_shape=(pltpu.SemaphoreType.DMA((7 * ns,)), pltpu.SemaphoreType.DMA((7 * ns,)),
                   pltpu.SemaphoreType.DMA((ns,)), *[hbm(t) for t in srcs], *[hbm(t) for t in lands],
                   jax.ShapeDtypeStruct((8, LANE), F32)),
        in_specs=[_HBM] * (2 * na) + dep_specs,
        out_specs=(_SEM, _SEM, _SEM, *[_HBM] * (2 * na), pl.BlockSpec(memory_space=pltpu.VMEM)),
        input_output_aliases={i: 3 + i for i in range(2 * na)},
        compiler_params=pltpu.CompilerParams(has_side_effects=_EFFECT),
    )(*[pltpu.with_memory_space_constraint(t, pltpu.HBM) for t in srcs + lands], *dep_ops)
    return (out[:3], out[3:3 + na], out[3 + na:3 + 2 * na]), out[-1]


def _exchange_wait(handle, gather, after, name):
    sems, srcs, lands = handle
    na = len(srcs)

    def body(*refs):
        src_refs, land_refs = refs[:na], refs[na:2 * na]
        send_sems, recv_sems, local_sems = refs[2 * na:2 * na + 3]
        local, remote = _exchange_copies(src_refs, land_refs, send_sems, recv_sems, local_sems, gather)
        for cp in remote:
            cp.wait_send()
            cp.wait_recv()
        for cp in local:
            cp.wait()

    hbm = lambda t: pltpu.HBM(t.shape, t.dtype)
    out = pl.pallas_call(
        body, name=name, out_shape=(*[hbm(t) for t in srcs], *[hbm(t) for t in lands]),
        in_specs=[_HBM] * (2 * na) + [_SEM] * 3 + [pl.BlockSpec(memory_space=pl.ANY)],
        out_specs=tuple([_HBM] * (2 * na)), input_output_aliases={i: i for i in range(2 * na)},
        compiler_params=pltpu.CompilerParams(has_side_effects=_EFFECT),
    )(*srcs, *lands, *sems, after)
    return out[na:]


def _block_diag(w):
    h, a, b = w.shape
    eye = jnp.eye(h, dtype=w.dtype)
    return (w[:, :, None, :] * eye[:, None, :, None]).reshape(h * a, h * b)


def _block_diag_extract(m, h):
    a, b = m.shape[0] // h, m.shape[1] // h
    return jnp.stack([m[i * a:(i + 1) * a, i * b:(i + 1) * b] for i in range(h)], axis=0)


def _block_diag_take(m, h):
    a, b = m.shape[0] // h, m.shape[1] // h
    eye = jnp.eye(h, dtype=m.dtype)
    return (m.reshape(h, a, h, b) * eye[:, None, :, None]).sum(axis=2)


def _ffn_interleave(w):
    lead = w.shape[:-1]
    nb = D_FF // FFN_CB
    return jnp.swapaxes(w.reshape(*lead, 2, nb, FFN_CB), -3, -2).reshape(*lead, 2 * D_FF)


def _ffn_deinterleave(w):
    lead = w.shape[:-1]
    nb = D_FF // FFN_CB
    return jnp.swapaxes(w.reshape(*lead, nb, 2, FFN_CB), -3, -2).reshape(*lead, 2 * D_FF)


def _gather_full(gathered, axis):
    shape = list(gathered.shape[1:])
    shape[axis] *= N_DEV
    return jnp.moveaxis(gathered, 0, axis).reshape(shape)


def _scatter_blocks(full, axis):
    shape = list(full.shape)
    shape[axis:axis + 1] = [N_DEV, shape[axis] // N_DEV]
    return jnp.moveaxis(full.reshape(shape), axis, 0)


def _pad_to(flat, mult):
    pad = (-flat.shape[-1]) % mult
    if pad:
        flat = jnp.concatenate([flat, jnp.zeros(flat.shape[:-1] + (pad,), flat.dtype)], axis=-1)
    return flat


def _layer_fwd(h_in, h_in_t, w, cos, sin, l, dep, get_ffn, target=None):
    tag = "l%d_" % l
    proj, qkv, h_t = _proj_rope(h_in, w['w_in'], cos, sin, tag + "proj_rope", dep=dep,
                                transposed=h_in_t is None)
    h_in_t = h_t if h_in_t is None else h_in_t
    outs, lses = [], []
    for d, qv in zip(DILATIONS, qkv):
        o, ls = _attn_fwd(qv, d, tag + "attn_d%d" % d)
        outs.append(o)
        lses.append(ls)
    lru, *lru_saved = _lru_fwd(proj, w['lru_conv_w'], w['lru_conv_b'], w['lru_wr'], w['lru_br'], w['lru_wi'],
                               w['lru_bi'], w['lru_lambda'], tag + "lru")
    x_re, x_im, y_acc = _s5_scan_fwd(proj, w['s5_bb_re'], w['s5_bb_im'], w['s5_lam_re'], w['s5_lam_im'],
                                     w['s5_cc_re'], w['s5_cc_im'], tag + "s5_scan")
    s5, y_pre = _s5_out_fwd(proj, y_acc, w['s5_d'], w['s5_w_glu'], w['s5_b_glu'], tag + "s5_out")
    w_out = get_ffn(l, s5, 'out')
    if w_out is not None:
        w['w_out'] = w_out
    mixed_t, r1, h1, h1_t, attn_o, attn_lse = _mix_fwd(outs, lses, lru, s5, w['mix_norm_g'], h_in, w['w_out'],
                                                       w['ln1_g'], w['ln1_b'], tag + "mix_out_ln1")
    w['w_up_g'], w['w_down'], ffn_dep = get_ffn(l, h1, 'ffn')
    up, y_conv, act, act_t = _ffn_up_act(h1, w['w_up_g'], w['ffn_conv_w'], w['ffn_conv_b'], tag + "up_act",
                                         dep=ffn_dep)
    r2, out_a, out_b = _proj_ln(act, w['w_down'], h1, w['ln2_g'], w['ln2_b'], tag + "down_ln2", target=target)
    saved = dict(h_in_t=h_in_t, proj=proj, qkv=qkv, lru=lru, lru_saved=lru_saved, x_re=x_re, x_im=x_im,
                 y_pre=y_pre, s5=s5, mixed_t=mixed_t, attn_o=attn_o, attn_lse=attn_lse, r1=r1, h1_t=h1_t, up=up,
                 act_t=act_t, r2=r2, y_conv=y_conv)
    return out_a, out_b, saved


def _layer_bwd_ffn(dh2, sv, w, l, dep=None):
    tag = "l%d_" % l
    g = {}
    dr2, g['ln2_g'], g['ln2_b'] = _ln_bwd(sv['r2'], dh2, w['ln2_g'], tag + "ln2_bwd", dep=dep)
    g['w_down'] = _mm_dw(sv['act_t'], dr2, 1024, D_MODEL, 1024, tag + "down_dw", _grad_dtype(l))
    dup, dh1, dcw_parts, dcb_parts = _ffn_bwd(sv['up'], sv['y_conv'], dr2, w['w_down'], w['w_up_g'],
                                              w['ffn_conv_w'], tag + "ffn_bwd")
    g['ffn_conv_w'] = dcw_parts.sum(axis=0)
    g['ffn_conv_b'] = dcb_parts.sum(axis=0)
    g['w_up_g'] = _mm_up_dw(sv['h1_t'], dup, tag + "up_dw", _grad_dtype(l))
    return dh1, g


def _layer_bwd_mix(dh1, sv, w, cos, sin, l, dep, g_ffn, after_out_grad, after_small_grads, after_in_grad):
    tag = "l%d_" % l
    g = {}
    dr1, g['ln1_g'], g['ln1_b'], d_o, dlru, ds5, g['mix_norm_g'] = _mix_bwd(
        sv['r1'], dh1, w['ln1_g'], w['w_out'], sv['attn_o'][0], sv['lru'], sv['s5'], w['mix_norm_g'],
        tag + "ln1_mix_bwd", dep=dep)
    g['w_out'] = _mm_dw(sv['mixed_t'], dr1, 1024, D_MODEL, 1024, tag + "out_dw", _grad_dtype(l))
    dy, dud, g['s5_d'], g['s5_w_glu'], g['s5_b_glu'] = _s5_out_bwd(
        sv['proj'], sv['y_pre'], ds5, w['s5_d'], w['s5_w_glu'], w['s5_b_glu'], tag + "s5_out_bwd",
        dep=after_out_grad(l, g['w_out']))
    du, g['s5_lam_re'], g['s5_lam_im'], g['s5_bb_re'], g['s5_bb_im'], g['s5_cc_re'], g['s5_cc_im'] = \
        _s5_scan_bwd(sv['proj'], dy, dud, sv['x_re'], sv['x_im'], w['s5_bb_re'], w['s5_bb_im'],
                     w['s5_lam_re'], w['s5_lam_im'], w['s5_cc_re'], w['s5_cc_im'], tag + "s5_scan_bwd")
    (dxr, dgate, g['lru_conv_w'], g['lru_conv_b'], g['lru_wr'], g['lru_br'], g['lru_wi'], g['lru_bi'],
     g['lru_lambda']) = _lru_bwd(sv['proj'], dlru, *sv['lru_saved'], w['lru_conv_w'], w['lru_conv_b'], w['lru_wr'],
                                 w['lru_br'], w['lru_wi'], w['lru_bi'], w['lru_lambda'], tag + "lru_bwd")
    token = after_small_grads(l, _finish_layer_grads({**g_ffn, **g}, w, l))
    dqkv = [_attn_bwd(sv['qkv'][b], sv['attn_o'][b], d_o[b], sv['attn_lse'][b], d, tag + "attn_bwd_d%d" % d,
                      dep=token if b == 0 else None)
            for b, d in enumerate(DILATIONS)]
    dproj = _dproj_assemble(dqkv, dxr, dgate, du, cos, sin, tag + "dproj")
    g_in = _mm_dw(sv['h_in_t'], dproj, 1024, D_IN, 1024, tag + "in_dw", _grad_dtype(l))
    return _mm_nt(dproj, w['w_in'], 512, D_MODEL, tag + "in_dx", add=dr1, add_scale=ALPHA,
                  dep=after_in_grad(l, g_in))


def _s5_rep(a):
    return jnp.repeat(a, S5_C, axis=0)


def _prepare_layer(p, l):
    w = {}
    for n in ('w_in', 'w_out', 's5_w_glu'):
        if n in p:
            w[n] = p[n].astype(BF16)
    w['ffn_conv_w'] = _ffn_interleave(p['ffn_conv_w'])
    w['ffn_conv_b'] = _ffn_interleave(p['ffn_conv_b'])[None, :]
    w['lru_conv_w'] = p['lru_conv_w']
    for n in ('lru_conv_b', 'lru_br', 'lru_bi', 'lru_lambda', 's5_b_glu', 'mix_norm_g',
              'ln1_g', 'ln1_b', 'ln2_g', 'ln2_b'):
        w[n] = p[n][None, :]
    w['lru_wr'] = _block_diag(p['lru_wr']).astype(BF16)
    w['lru_wi'] = _block_diag(p['lru_wi']).astype(BF16)
    w['s5_d'] = p['s5_d'].reshape(1, S5_W)
    disc_in = (_s5_rep(p['s5_a_re']), _s5_rep(p['s5_a_im']),
               _s5_rep(jnp.broadcast_to(p['s5_log_step'][:, None], (S5_G, S5_P))),
               jnp.swapaxes(p['s5_b_re'], 1, 2).reshape(S5_W, S5_P),
               jnp.swapaxes(p['s5_b_im'], 1, 2).reshape(S5_W, S5_P))
    ab_re, ab_im, bb_re, bb_im = _s5_disc_fwd(*disc_in, "l%d_s5_disc" % l)
    w['s5_disc_in'] = disc_in
    w['s5_lam_re'] = ab_re.reshape(S5_G, S5_C, S5_P)[:, 0, :].reshape(1, S5_STATES)
    w['s5_lam_im'] = ab_im.reshape(S5_G, S5_C, S5_P)[:, 0, :].reshape(1, S5_STATES)
    w['s5_bb_re'] = _block_diag(bb_re.reshape(S5_G, S5_C, S5_P)).astype(BF16)
    w['s5_bb_im'] = _block_diag(bb_im.reshape(S5_G, S5_C, S5_P)).astype(BF16)
    w['s5_cc_re'] = _block_diag(jnp.swapaxes(p['s5_c_re'], 1, 2)).astype(BF16)
    w['s5_cc_im'] = _block_diag(jnp.swapaxes(p['s5_c_im'], 1, 2)).astype(BF16)
    return w


def _finish_layer_grads(g, w, l):
    out = {}
    for n in ('s5_w_glu', 'lru_conv_w'):
        out[n] = g[n]
    out['ffn_conv_w'] = _ffn_deinterleave(g['ffn_conv_w'])
    out['ffn_conv_b'] = _ffn_deinterleave(g['ffn_conv_b'])[0]
    for n in ('lru_conv_b', 'lru_br', 'lru_bi', 'lru_lambda', 's5_b_glu', 'mix_norm_g',
              'ln1_g', 'ln1_b', 'ln2_g', 'ln2_b'):
        out[n] = g[n][0]
    out['lru_wr'] = _block_diag_extract(g['lru_wr'], LRU_W // HEAD)
    out['lru_wi'] = _block_diag_extract(g['lru_wi'], LRU_W // HEAD)
    out['s5_d'] = g['s5_d'].reshape(S5_G, S5_C)
    out['s5_c_re'] = jnp.swapaxes(_block_diag_take(g['s5_cc_re'], S5_G), 1, 2)
    out['s5_c_im'] = jnp.swapaxes(_block_diag_take(g['s5_cc_im'], S5_G), 1, 2)
    rep = lambda v: _s5_rep(v.reshape(S5_G, S5_P)) * (1.0 / S5_C)
    cts = (rep(g['s5_lam_re']), rep(g['s5_lam_im']),
           _block_diag_take(g['s5_bb_re'], S5_G).reshape(S5_W, S5_P),
           _block_diag_take(g['s5_bb_im'], S5_G).reshape(S5_W, S5_P))
    da_re, da_im, dls, dbt_re, dbt_im = _s5_disc_bwd(*w['s5_disc_in'], cts, "l%d_s5_disc_bwd" % l)
    out['s5_a_re'] = da_re.reshape(S5_G, S5_C, S5_P).sum(axis=1)
    out['s5_a_im'] = da_im.reshape(S5_G, S5_C, S5_P).sum(axis=1)
    out['s5_log_step'] = dls.reshape(S5_G, S5_C * S5_P).sum(axis=1)
    out['s5_b_re'] = jnp.swapaxes(dbt_re.reshape(S5_G, S5_C, S5_P), 1, 2)
    out['s5_b_im'] = jnp.swapaxes(dbt_im.reshape(S5_G, S5_C, S5_P), 1, 2)
    return out


def _run_step(x, target, get_layer, get_ffn, on_loss, after_ffn_grads, after_out_grad, after_small_grads,
              after_in_grad):
    cos, sin = _rope_tables(x.shape[0])
    h, h_t = x, None
    ws, saved = [], []
    for l in range(DEPTH):
        p, dep = get_layer(l, h)
        ws.append(_prepare_layer(p, l))
        h, h_t, sv = _layer_fwd(h, h_t, ws[l], cos, sin, l, dep, get_ffn, target if l == DEPTH - 1 else None)
        saved.append(sv)
    dh, loss_vec = h, h_t
    on_loss(loss_vec)
    dep = None
    for l in reversed(range(DEPTH)):
        dh1, g = _layer_bwd_ffn(dh, saved[l], ws[l], l, dep)
        dep = after_ffn_grads(l, g)
        dh = _layer_bwd_mix(dh1, saved[l], ws[l], cos, sin, l, dep, g, after_out_grad, after_small_grads,
                            after_in_grad)
        dep = None
    return loss_vec, dh


def _local_step(x, target, layers):
    grads = [{} for _ in range(DEPTH)]

    def ffn(l, after, part):
        if part == 'out':
            return None
        return layers[l]['w_up_g'].astype(BF16), layers[l]['w_down'].astype(BF16), None

    def keep_ffn(l, g):
        grads[l].update(w_up_g=g['w_up_g'], w_down=g['w_down'])

    def keep_small(l, g):
        grads[l].update(g)

    loss, dx = _run_step(x, target, lambda l, h: (layers[l], None), ffn, lambda row: None, keep_ffn,
                         lambda l, g: grads[l].update(w_out=g), keep_small, lambda l, g: grads[l].update(w_in=g))
    return loss[0, 0], dx, grads


def kernel(x, w_in, lru_conv_w, lru_conv_b, lru_wr, lru_br, lru_wi, lru_bi, lru_lambda, s5_a_re, s5_a_im, s5_b_re, s5_b_im, s5_c_re, s5_c_im, s5_d, s5_log_step, s5_w_glu, s5_b_glu, mix_norm_g, w_out, ln1_g, ln1_b, w_up, ffn_conv_w, ffn_conv_b, w_down, ln2_g, ln2_b, loss_target, m_w_in, m_lru_conv_w, m_lru_conv_b, m_lru_wr, m_lru_br, m_lru_wi, m_lru_bi, m_lru_lambda, m_s5_a_re, m_s5_a_im, m_s5_b_re, m_s5_b_im, m_s5_c_re, m_s5_c_im, m_s5_d, m_s5_log_step, m_s5_w_glu, m_s5_b_glu, m_mix_norm_g, m_w_out, m_ln1_g, m_ln1_b, m_w_up, m_ffn_conv_w, m_ffn_conv_b, m_w_down, m_ln2_g, m_ln2_b, v_w_in, v_lru_conv_w, v_lru_conv_b, v_lru_wr, v_lru_br, v_lru_wi, v_lru_bi, v_lru_lambda, v_s5_a_re, v_s5_a_im, v_s5_b_re, v_s5_b_im, v_s5_c_re, v_s5_c_im, v_s5_d, v_s5_log_step, v_s5_w_glu, v_s5_b_glu, v_mix_norm_g, v_w_out, v_ln1_g, v_ln1_b, v_w_up, v_ffn_conv_w, v_ffn_conv_b, v_w_down, v_ln2_g, v_ln2_b):
    args = locals()
    wl = {n: args[n] for n in WEIGHTS}
    ml = {n: args['m_' + n] for n in WEIGHTS}
    vl = {n: args['v_' + n] for n in WEIGHTS}

    small_sizes = [int(wl[n].size) for n in SMALL_SHARDED]
    small_flat = _pad_to(jnp.concatenate([wl[n].reshape(-1) for n in SMALL_SHARDED]), 8 * 1024)
    small_all, w_in0 = _all_gather([small_flat.reshape(-1, 1024), wl['w_in'][0].astype(BF16)], "gather_first")
    small_all = small_all.reshape(N_DEV, -1)
    small_full, off = {}, 0
    for n, sz in zip(SMALL_SHARDED, small_sizes):
        small_full[n] = _gather_full(small_all[:, off:off + sz].reshape((N_DEV,) + wl[n].shape), SHARD_AXIS[n])
        off += sz
    def mixer_params(l, g_in, g_out):
        p = {n: wl[n][l] for n in REPLICATED}
        p.update({n: small_full[n][l] for n in SMALL_SHARDED})
        p['w_in'] = _gather_full(g_in, 1)
        if g_out is not None:
            p['w_out'] = g_out.reshape(D_MODEL, D_MODEL)
        return p

    mix_names, ffn_names = ('w_in', 'w_out'), ('w_up', 'w_down')
    shards = lambda names, l: [wl[n][l].astype(BF16) for n in names]
    gathers = {}
    gathers[0, 'out'], token = _exchange_start(shards(('w_out',), 0), True, "gather_out_l0_start", dep=w_in0)
    gathers[0, 'ffn'], rest0_token = _exchange_start(shards(ffn_names, 0), True, "gather_ffn_l0_start", dep=token)

    def get_layer(l, h):
        if l == 0:
            return mixer_params(0, w_in0, None), rest0_token
        return mixer_params(1, *_exchange_wait(gathers[1, 'mix'], True, h, "gather_mix_l1_wait")), None

    def get_ffn(l, after, part):
        if part == 'out':
            if l > 0:
                return None
            g_out, = _exchange_wait(gathers[0, 'out'], True, after, "gather_out_l0_wait")
            return g_out.reshape(D_MODEL, D_MODEL)
        g_up, g_down = _exchange_wait(gathers[l, 'ffn'], True, after, "gather_ffn_l%d_wait" % l)
        token = None
        if l == 0:
            gathers[1, 'mix'], token = _exchange_start(shards(mix_names, 1), True, "gather_mix_l1_start", dep=g_up)
            gathers[1, 'ffn'], token = _exchange_start(shards(ffn_names, 1), True, "gather_ffn_l1_start", dep=token)
        return g_up, g_down.reshape(D_FF, D_MODEL), token

    scatters = {}

    def after_ffn_grads(l, g):
        send = [g['w_up_g'], g['w_down'].reshape(N_DEV, D_FF // N_DEV, D_MODEL)]
        scatters[l, 'ffn'], token = _exchange_start(send, False, "scatter_ffn_l%d_start" % l)
        return token

    def after_out_grad(l, g_out):
        send = [g_out.reshape(N_DEV, D_MODEL // N_DEV, D_MODEL)]
        scatters[l, 'out'], token = _exchange_start(send, False, "scatter_out_l%d_start" % l)
        return token

    def after_in_grad(l, g_in):
        scatters[l, 'in'], token = _exchange_start([_scatter_blocks(g_in, 1)], False, "scatter_in_l%d_start" % l)
        return token

    def after_small_grads(l, g):
        rep = [g[n][None] for n in REPLICATED]
        if l == DEPTH - 1:
            rep.append(loss_rows[0][None])
        shd = [_scatter_blocks(g[n], SHARD_AXIS[n] - 1)[:, None] for n in SMALL_SHARDED]
        scatters[l, 'rep'], token = _exchange_start(rep, True, "gather_rep_grads_l%d_start" % l)
        scatters[l, 'small'], token = _exchange_start(shd, False, "scatter_small_l%d_start" % l, dep=token)
        return token

    loss_rows = []
    _, grad_x = _run_step(x[0], loss_target[0], get_layer, get_ffn, loss_rows.append, after_ffn_grads,
                          after_out_grad, after_small_grads, after_in_grad)

    results = {}
    big_prev = {n: None for n in BIG}

    def finish_big(l, part, names, after):
        landed = _exchange_wait(scatters[l, part], False, after, "scatter_%s_l%d_wait" % (part, l))
        for n, ld in zip(names, landed):
            big_prev[n] = _adamw_sum(ld, wl[n], ml[n], vl[n], l, big_prev[n], "adamw_%s_l%d" % (n, l))

    for l, part, names in ((1, 'ffn', ffn_names), (1, 'out', ('w_out',)), (1, 'in', ('w_in',)),
                           (0, 'ffn', ffn_names), (0, 'out', ('w_out',))):
        finish_big(l, part, names, grad_x)

    kinds = ('grad', 'delta', 'm', 'v')
    landed = []
    for l in range(DEPTH):
        rep = list(_exchange_wait(scatters[l, 'rep'], True, grad_x, "gather_rep_grads_l%d_wait" % l))
        if l == DEPTH - 1:
            loss = jnp.sum(rep.pop()[:, 0, 0, 0])
        shd = list(_exchange_wait(scatters[l, 'small'], False, grad_x, "scatter_small_l%d_wait" % l))
        landed.append(dict(zip(REPLICATED + SMALL_SHARDED, rep + shd)))
    matrices = ['lru_wr', 'lru_wi', 's5_a_re', 's5_a_im', 's5_c_re', 's5_c_im', 's5_d']
    widest = ['s5_b_re', 's5_b_im']
    vectors = [n for n in REPLICATED + SMALL_SHARDED if n not in matrices + widest]
    last = None
    for tag, names in (("vectors", vectors), ("matrices", matrices), ("s5_b", widest)):
        res = _adamw_many([[landed[l][n] for n in names] for l in range(DEPTH)], [wl[n] for n in names],
                          [ml[n] for n in names], [vl[n] for n in names], "adamw_" + tag)
        for kind, arrs in zip(kinds, res):
            for n, a in zip(names, arrs):
                results[kind, n] = a
        last = res[0][0]
    finish_big(0, 'in', ('w_in',), last)
    for n in BIG:
        results['grad', n], results['delta', n], results['m', n], results['v', n] = big_prev[n]

    out = [loss, grad_x[None]]
    for kind in kinds:
        out.extend(results[kind, n] for n in WEIGHTS)
    return tuple(out)
```

```python
import math

import jax
import jax.numpy as jnp
from jax import lax
from jax.experimental import pallas as pl
from jax.experimental.pallas import tpu as pltpu

F32 = jnp.float32
BF16 = jnp.bfloat16

N_DEV = 8
DEPTH = 2
D_MODEL = 1024
ATTN_W = 384
LRU_W = 384
S5_W = 256
D_IN = 2176
D_FF = 3072
HEAD = 64
ATTN_BLK = 128
ATTN_TILE = 1024
DILATIONS = (1, 4, 16)
S5_G = 16
S5_P = 64
S5_C = 16
S5_STATES = S5_G * S5_P
LRU_C = 8.0
LRU_CONV = 4
FFN_CONV = 3
ROPE_THETA = 10000.0
ALPHA = (2 * DEPTH) ** 0.25
LN_EPS = 1e-5
RMS_EPS = 1e-6
ADAM_LR, ADAM_B1, ADAM_B2, ADAM_EPS, ADAM_WD, ADAM_STEP = 0.001, 0.9, 0.999, 1e-8, 0.01, 10

LANE = 128
SCAN_T = 1024
S5_BLK = 256
FFN_CB = 2 * D_FF // N_DEV
VMEM_LIMIT = 56 * 1024 * 1024

WEIGHTS = ['w_in', 'lru_conv_w', 'lru_conv_b', 'lru_wr', 'lru_br', 'lru_wi', 'lru_bi', 'lru_lambda',
           's5_a_re', 's5_a_im', 's5_b_re', 's5_b_im', 's5_c_re', 's5_c_im', 's5_d', 's5_log_step',
           's5_w_glu', 's5_b_glu', 'mix_norm_g', 'w_out', 'ln1_g', 'ln1_b', 'w_up', 'ffn_conv_w',
           'ffn_conv_b', 'w_down', 'ln2_g', 'ln2_b']
SHARD_AXIS = {'w_in': 2, 'lru_conv_w': 2, 's5_w_glu': 1, 'w_out': 1, 'w_up': 2, 'ffn_conv_w': 2, 'w_down': 1}
BIG = ['w_in', 'w_out', 'w_up', 'w_down']
SMALL_SHARDED = ['lru_conv_w', 'ffn_conv_w', 's5_w_glu']
REPLICATED = [n for n in WEIGHTS if n not in SHARD_AXIS]


def _cparams(sem=None):
    return pltpu.CompilerParams(dimension_semantics=sem, vmem_limit_bytes=VMEM_LIMIT)


def _grad_dtype(l):
    return BF16 if l == 0 else F32


def _ffn_dev(jb):
    return jb // 2 + (N_DEV // 2) * (jb % 2)


def _gelu(x):
    c = math.sqrt(2.0 / math.pi)
    t = jnp.tanh(c * (x + 0.044715 * (x * x * x)))
    return 0.5 * x * (1.0 + t)


def _gelu_grad(x):
    c = math.sqrt(2.0 / math.pi)
    x2 = x * x
    t = jnp.tanh(c * (x + 0.044715 * (x2 * x)))
    return 0.5 * (1.0 + t) + 0.5 * x * (1.0 - t * t) * (c * (1.0 + 3.0 * 0.044715 * x2))


def _sigmoid(x):
    return 1.0 / (1.0 + jnp.exp(-x))


def _log1p(x):
    u = 1.0 + x
    d = u - 1.0
    return jnp.where(d == 0.0, x, jnp.log(u) * (x / jnp.where(d == 0.0, 1.0, d)))


def _softplus(x):
    return jnp.maximum(x, 0.0) + _log1p(jnp.exp(-jnp.abs(x)))


def _expm1(x):
    return jnp.tanh(0.5 * x) * (jnp.exp(x) + 1.0)


def _dot(a, b):
    return jnp.dot(a.astype(BF16), b.astype(BF16), preferred_element_type=F32)


def _dot_nt(a, b):
    return lax.dot_general(a.astype(BF16), b.astype(BF16), (((1,), (1,)), ((), ())),
                           preferred_element_type=F32)


def _dot_tn(a, b):
    return lax.dot_general(a.astype(BF16), b.astype(BF16), (((0,), (0,)), ((), ())),
                           preferred_element_type=F32)


def _rows(shape):
    return lax.broadcasted_iota(jnp.int32, shape, 0)


def _shift_down_prev(x, s, prev8):
    if s == 0:
        return x
    t, l = x.shape
    r = pltpu.roll(x, s, axis=0)
    pr = pltpu.roll(prev8, s, axis=0)
    pad = jnp.concatenate([pr, jnp.zeros((t - 8, l), x.dtype)], axis=0)
    return jnp.where(_rows(x.shape) < s, pad, r)


def _shift_up_next(x, s, next8):
    if s == 0:
        return x
    t, l = x.shape
    r = pltpu.roll(x, t - s, axis=0)
    nx = pltpu.roll(next8, 8 - s, axis=0)
    pad = jnp.concatenate([jnp.zeros((t - 8, l), x.dtype), nx], axis=0)
    return jnp.where(_rows(x.shape) >= t - s, pad, r)


SUB = 8


def _tile_shift(x, s, fill, reverse):
    t = x.shape[0]
    pos = _rows(x.shape) & (SUB - 1)
    if reverse:
        return jnp.where(pos < SUB - s, pltpu.roll(x, t - s, axis=0), fill)
    return jnp.where(pos >= s, pltpu.roll(x, s, axis=0), fill)


def _scan_chunk(a, x, carry, reverse=False):
    s = 1
    while s < SUB:
        x = x + a * _tile_shift(x, s, 0.0, reverse)
        a = a * _tile_shift(a, s, 1.0, reverse)
        s *= 2
    nv = x.shape[0] // SUB
    out = [None] * nv
    for v in (reversed(range(nv)) if reverse else range(nv)):
        rows = slice(v * SUB, (v + 1) * SUB)
        out[v] = x[rows, :] + a[rows, :] * carry
        carry = out[v][0:1, :] if reverse else out[v][SUB - 1:SUB, :]
    return jnp.concatenate(out, axis=0)


def _cmul(ar, ai, br, bi):
    return ar * br - ai * bi, ar * bi + ai * br


def _cscan_consts(lr, li, reverse):
    pows = [(lr, li)]
    for _ in range(2):
        pows.append(_cmul(*pows[-1], *pows[-1]))
    rows = [(lr, li)]
    for _ in range(SUB - 1):
        rows.append(_cmul(*rows[-1], lr, li))
    if reverse:
        rows = rows[::-1]
    return pows, (jnp.concatenate([r for r, _ in rows], axis=0), jnp.concatenate([i for _, i in rows], axis=0))


def _cscan_chunk(xr, xi, consts, carry, reverse=False):
    pows, (p8r, p8i) = consts
    s = 1
    for pr, pi in pows:
        sr = _tile_shift(xr, s, 0.0, reverse)
        si = _tile_shift(xi, s, 0.0, reverse)
        xr, xi = xr + pr * sr - pi * si, xi + pr * si + pi * sr
        s *= 2
    nv = xr.shape[0] // SUB
    out_r, out_i = [None] * nv, [None] * nv
    cr, ci = carry
    for v in (reversed(range(nv)) if reverse else range(nv)):
        rows = slice(v * SUB, (v + 1) * SUB)
        out_r[v] = xr[rows, :] + p8r * cr - p8i * ci
        out_i[v] = xi[rows, :] + p8r * ci + p8i * cr
        edge = slice(0, 1) if reverse else slice(SUB - 1, SUB)
        cr, ci = out_r[v][edge, :], out_i[v][edge, :]
    return jnp.concatenate(out_r, axis=0), jnp.concatenate(out_i, axis=0)


def _dep_args(dep):
    return ([], []) if dep is None else ([pl.BlockSpec(memory_space=pl.ANY)], [dep])


def _mm_nt(a, w, tm, tn, name, add=None, add_scale=1.0, dep=None):
    m, k = a.shape
    n = w.shape[0]

    def body(a_ref, w_ref, *rest):
        o_ref = rest[-1]
        if add is None:
            o_ref[...] = _dot_nt(a_ref[...], w_ref[...])
        else:
            o_ref[...] = _dot_nt(a_ref[...], w_ref[...]) + add_scale * rest[0][...]

    in_specs = [pl.BlockSpec((tm, k), lambda j, i: (i, 0)), pl.BlockSpec((tn, k), lambda j, i: (j, 0))]
    args = [a, w]
    if add is not None:
        in_specs.append(pl.BlockSpec((tm, tn), lambda j, i: (i, j)))
        args.append(add)
    dep_specs, dep_ops = _dep_args(dep)
    return pl.pallas_call(
        body, out_shape=jax.ShapeDtypeStruct((m, n), F32), grid=(n // tn, m // tm),
        in_specs=in_specs + dep_specs, out_specs=pl.BlockSpec((tm, tn), lambda j, i: (i, j)), name=name,
        compiler_params=_cparams(("parallel", "parallel")))(*args, *dep_ops)


def _mm_dw(at, b, tm, tn, ts, name, out_dtype=F32):
    m, s = at.shape
    n = b.shape[1]
    nk = s // ts

    def body(a_ref, b_ref, o_ref, acc):
        @pl.when(pl.program_id(2) == 0)
        def _():
            acc[...] = jnp.zeros_like(acc)
        acc[...] += _dot(a_ref[...], b_ref[...])

        @pl.when(pl.program_id(2) == nk - 1)
        def _():
            o_ref[...] = acc[...].astype(out_dtype)

    return pl.pallas_call(
        body, out_shape=jax.ShapeDtypeStruct((m, n), out_dtype), grid=(m // tm, n // tn, nk),
        in_specs=[pl.BlockSpec((tm, ts), lambda i, j, k: (i, k)), pl.BlockSpec((ts, tn), lambda i, j, k: (k, j))],
        out_specs=pl.BlockSpec((tm, tn), lambda i, j, k: (i, j)),
        scratch_shapes=[pltpu.VMEM((tm, tn), F32)], name=name,
        compiler_params=_cparams(("parallel", "parallel", "arbitrary")))(at, b)


def _mm_up_dw(ht, dup, name, out_dtype=F32):
    d, s = ht.shape

    def body(a_ref, b_ref, o_ref):
        o_ref[...] = _dot(a_ref[...], b_ref[...]).astype(out_dtype)

    return pl.pallas_call(
        body, out_shape=jax.ShapeDtypeStruct((N_DEV, d, FFN_CB), out_dtype), grid=(N_DEV,),
        in_specs=[pl.BlockSpec((d, s), lambda j: (0, 0)), pl.BlockSpec((s, FFN_CB), lambda j: (0, j))],
        out_specs=pl.BlockSpec((None, d, FFN_CB), lambda j: (_ffn_dev(j), 0, 0)), name=name,
        compiler_params=_cparams(("parallel",)))(ht, dup)


def _layer_norm(r, g, b):
    mu = jnp.mean(r, axis=-1, keepdims=True)
    xc = r - mu
    var = jnp.mean(xc * xc, axis=-1, keepdims=True)
    return xc * lax.rsqrt(var + LN_EPS) * g + b


def _proj_ln(a, w, resid, g, bias, name, transposed=True, target=None):
    s, k = a.shape
    d = w.shape[1]
    tm = 512

    def body(a_ref, w_ref, x_ref, g_ref, bias_ref, *rest):
        r = ALPHA * x_ref[...] + _dot(a_ref[...], w_ref[...])
        h = _layer_norm(r, g_ref[...], bias_ref[...])
        if target is None:
            r_ref, h_ref = rest[0], rest[1]
            h_ref[...] = h
            if transposed:
                rest[2][...] = h.T.astype(BF16)
        else:
            t_ref, r_ref, dy_ref, l_ref = rest

            @pl.when(pl.program_id(0) == 0)
            def _():
                l_ref[...] = jnp.zeros_like(l_ref)
            e = h - t_ref[...]
            dy_ref[...] = e * (1.0 / d)
            part = 0.5 * jnp.sum(jnp.mean(e * e, axis=-1, keepdims=True), axis=0, keepdims=True)
            l_ref[...] += jnp.broadcast_to(part, l_ref.shape)
        r_ref[...] = r

    row = pl.BlockSpec((tm, d), lambda i: (i, 0))
    vec = pl.BlockSpec((1, d), lambda i: (0, 0))
    in_specs = [pl.BlockSpec((tm, k), lambda i: (i, 0)), pl.BlockSpec((k, d), lambda i: (0, 0)), row, vec, vec]
    args = [a, w, resid, g, bias]
    shapes = [jax.ShapeDtypeStruct((s, d), F32), jax.ShapeDtypeStruct((s, d), F32)]
    specs = [row, row]
    if target is not None:
        in_specs.append(row)
        args.append(target)
        shapes.append(jax.ShapeDtypeStruct((1, LANE), F32))
        specs.append(pl.BlockSpec((1, LANE), lambda i: (0, 0)))
    elif transposed:
        shapes.append(jax.ShapeDtypeStruct((d, s), BF16))
        specs.append(pl.BlockSpec((d, tm), lambda i: (0, i)))
    return pl.pallas_call(
        body, out_shape=tuple(shapes), grid=(s // tm,), in_specs=in_specs, out_specs=tuple(specs), name=name,
        compiler_params=_cparams(("arbitrary",) if target is not None else ("parallel",)))(*args)


def _layer_norm_bwd(r, dh, g):
    mu = jnp.mean(r, axis=-1, keepdims=True)
    xc = r - mu
    var = jnp.mean(xc * xc, axis=-1, keepdims=True)
    rstd = lax.rsqrt(var + LN_EPS)
    xh = xc * rstd
    dxh = dh * g
    m1 = jnp.mean(dxh, axis=-1, keepdims=True)
    m2 = jnp.mean(dxh * xh, axis=-1, keepdims=True)
    return (rstd * (dxh - m1 - xh * m2), jnp.sum(dh * xh, axis=0, keepdims=True),
            jnp.sum(dh, axis=0, keepdims=True))


def _ln_bwd(r, dh, g, name, dep=None):
    s, d = r.shape
    tm = 512

    def body(r_ref, dh_ref, g_ref, *rest):
        dr_ref, dg_ref, db_ref = rest[-3:]

        @pl.when(pl.program_id(0) == 0)
        def _():
            dg_ref[...] = jnp.zeros_like(dg_ref)
            db_ref[...] = jnp.zeros_like(db_ref)
        dr_ref[...], dg_rows, db_rows = _layer_norm_bwd(r_ref[...], dh_ref[...], g_ref[...])
        dg_ref[...] += dg_rows
        db_ref[...] += db_rows

    row = pl.BlockSpec((tm, d), lambda i: (i, 0))
    vec = pl.BlockSpec((1, d), lambda i: (0, 0))
    dep_specs, dep_ops = _dep_args(dep)
    return pl.pallas_call(
        body, out_shape=(jax.ShapeDtypeStruct((s, d), F32), jax.ShapeDtypeStruct((1, d), F32),
                         jax.ShapeDtypeStruct((1, d), F32)),
        grid=(s // tm,), in_specs=[row, row, vec] + dep_specs, out_specs=(row, vec, vec), name=name,
        compiler_params=_cparams(("arbitrary",)))(r, dh, g, *dep_ops)


def _rope_tables(s):
    half = HEAD // 2
    pos = jnp.arange(s, dtype=F32)
    inv = ROPE_THETA ** (-jnp.arange(half, dtype=F32) * 2.0 / HEAD)
    ang = pos[:, None] * inv[None, :]
    cos, sin = jnp.cos(ang), jnp.sin(ang)
    cos = jnp.concatenate([cos, cos, cos, cos], axis=1)
    sin = jnp.concatenate([-sin, sin, -sin, sin], axis=1)
    return cos, sin


def _rotate(x, cos, sin):
    lane = lax.broadcasted_iota(jnp.int32, x.shape, 1)
    partner = jnp.where((lane % HEAD) < HEAD // 2, pltpu.roll(x, LANE - HEAD // 2, axis=1),
                        pltpu.roll(x, HEAD // 2, axis=1))
    return x * cos + partner * sin


def _class_rows(c, d, tm):
    return pl.ds(c, tm // d, stride=d) if d > 1 else pl.ds(0, tm)


def _dilated_spec(tm, d, w):
    return pl.BlockSpec((tm // d, d * w), lambda i: (i, 0))


def _token_scratch(tm, w):
    return pltpu.VMEM((w // LANE, tm, LANE), F32)


def _to_tokens(src_ref, dst3, d, tm):
    nj = dst3.shape[0]
    for cls in range(d):
        for j in range(nj):
            col = (cls * nj + j) * LANE
            dst3.at[j][_class_rows(cls, d, tm), :] = src_ref[:, col:col + LANE]


def _to_dilated(src3, dst_ref, d, tm):
    nj = src3.shape[0]
    for cls in range(d):
        for j in range(nj):
            col = (cls * nj + j) * LANE
            dst_ref[:, col:col + LANE] = src3.at[j][_class_rows(cls, d, tm), :].astype(dst_ref.dtype)


def _token_value(src3):
    return jnp.concatenate([src3[j] for j in range(src3.shape[0])], axis=1)


def _proj_rope(h, w_in, cos, sin, name, dep=None, transposed=False):
    s, d_model = h.shape
    tm = 512
    w = 3 * ATTN_W
    nj = w // LANE

    def body(h_ref, w_ref, c_ref, s_ref, *rest):
        rot = rest[-1]
        if transposed:
            p_ref, o_refs, ht_ref = rest[-6], rest[-5:-2], rest[-2]
            ht_ref[...] = h_ref[...].T.astype(BF16)
        else:
            p_ref, o_refs = rest[-5], rest[-4:-1]
        y = _dot(h_ref[...], w_ref[...])
        p_ref[...] = y
        c, sn = c_ref[...], s_ref[...]
        for j in range(nj):
            x = y[:, j * LANE:(j + 1) * LANE]
            rot[j] = _rotate(x, c, sn) if j < 2 * ATTN_W // LANE else x
        for d, o_ref in zip(DILATIONS, o_refs):
            _to_dilated(rot, o_ref, d, tm)

    tab = pl.BlockSpec((tm, LANE), lambda i: (i, 0))
    dep_specs, dep_ops = _dep_args(dep)
    shapes = [jax.ShapeDtypeStruct((s, D_IN), F32), *[jax.ShapeDtypeStruct((s // d, d * w), BF16) for d in DILATIONS]]
    specs = [pl.BlockSpec((tm, D_IN), lambda i: (i, 0)), *[_dilated_spec(tm, d, w) for d in DILATIONS]]
    if transposed:
        shapes.append(jax.ShapeDtypeStruct((d_model, s), BF16))
        specs.append(pl.BlockSpec((d_model, tm), lambda i: (0, i)))
    res = pl.pallas_call(
        body, out_shape=tuple(shapes), grid=(s // tm,),
        in_specs=[pl.BlockSpec((tm, d_model), lambda i: (i, 0)), pl.BlockSpec((d_model, D_IN), lambda i: (0, 0)),
                  tab, tab] + dep_specs,
        out_specs=tuple(specs), scratch_shapes=[_token_scratch(tm, w)], name=name,
        compiler_params=_cparams(("parallel",)))(h, w_in, cos, sin, *dep_ops)
    return res[0], res[1:4], (res[4] if transposed else None)


def _dproj_assemble(dqkv_list, dxr, dgate, du, cos, sin, name):
    s = dxr.shape[0]
    tm = 512
    nq = 3 * ATTN_W // LANE

    def body(*refs):
        br = refs[:9]
        dxr_ref, dg_ref, du_ref, c_ref, s_ref, o_ref = refs[9:15]
        tok = refs[15:]
        c, sn = c_ref[...], -s_ref[...]
        for part in range(3):
            for b, d in enumerate(DILATIONS[1:], start=1):
                _to_tokens(br[3 * b + part], tok[2 * part + b - 1], d, tm)
        for j in range(nq):
            part, jj = divmod(j, ATTN_W // LANE)
            x = br[part][:, jj * LANE:(jj + 1) * LANE] + tok[2 * part][jj] + tok[2 * part + 1][jj]
            if part < 2:
                x = _rotate(x, c, sn)
            o_ref[:, j * LANE:(j + 1) * LANE] = x.astype(BF16)
        o_ref[:, 3 * ATTN_W:3 * ATTN_W + LRU_W] = dxr_ref[...].astype(BF16)
        o_ref[:, 3 * ATTN_W + LRU_W:3 * ATTN_W + 2 * LRU_W] = dg_ref[...].astype(BF16)
        o_ref[:, 3 * ATTN_W + 2 * LRU_W:] = du_ref[...].astype(BF16)

    a_spec = pl.BlockSpec((tm, ATTN_W), lambda i: (i, 0))
    tab = pl.BlockSpec((tm, LANE), lambda i: (i, 0))
    ordered = [dqkv_list[b][p] for b in range(3) for p in range(3)]
    d_specs = [_dilated_spec(tm, d, ATTN_W) for d in DILATIONS for _ in range(3)]
    return pl.pallas_call(
        body, out_shape=jax.ShapeDtypeStruct((s, D_IN), BF16), grid=(s // tm,),
        in_specs=d_specs + [a_spec, a_spec, pl.BlockSpec((tm, S5_W), lambda i: (i, 0)), tab, tab],
        out_specs=pl.BlockSpec((tm, D_IN), lambda i: (i, 0)),
        scratch_shapes=[_token_scratch(tm, ATTN_W)] * 6, name=name,
        compiler_params=_cparams(("parallel",)))(*ordered, dxr, dgate, du, cos, sin)


def _attn_tiles(s, d):
    m = s // d
    tq = min(m, ATTN_TILE)
    return m, tq, tq // ATTN_BLK


def _band_mask(qb):
    qi = lax.broadcasted_iota(jnp.int32, (ATTN_BLK, 2 * ATTN_BLK), 0)
    ki = lax.broadcasted_iota(jnp.int32, (ATTN_BLK, 2 * ATTN_BLK), 1)
    dist = qi + ATTN_BLK - ki
    return (dist >= 0) & (dist <= ATTN_BLK) & ((ki >= ATTN_BLK) | (qb > 0))


def _attn_fwd(qv, d, name):
    m = qv.shape[0]
    w3 = 3 * ATTN_W
    _, tq, n = _attn_tiles(m * d, d)
    scale = HEAD ** -0.5
    assert math.frexp(scale)[0] == 0.5, "the kernel scales bf16 q: exact only for a power of two"

    def body(x_ref, p_ref, o_ref, l_ref):
        b = pl.program_id(1)

        def block(i, first):
            r0 = 0 if first else pl.multiple_of(i * ATTN_BLK, ATTN_BLK)
            rows = pl.ds(r0, ATTN_BLK)
            valid = _band_mask(b * n + i)
            if not first:
                krows = pl.ds(pl.multiple_of(i * ATTN_BLK - ATTN_BLK, ATTN_BLK), 2 * ATTN_BLK)
            low = lax.broadcasted_iota(jnp.int32, (1, LANE), 1) < HEAD
            for hp in range(ATTN_W // LANE):
                qs, ks, vs = (slice(part * ATTN_W + hp * LANE, part * ATTN_W + (hp + 1) * LANE) for part in range(3))
                q2 = x_ref[rows, qs] * scale
                if first:
                    k2 = jnp.concatenate([p_ref[:, ks], x_ref[0:ATTN_BLK, ks]], axis=0)
                    v2 = jnp.concatenate([p_ref[:, vs], x_ref[0:ATTN_BLK, vs]], axis=0)
                else:
                    k2 = x_ref[krows, ks]
                    v2 = x_ref[krows, vs]
                outs, lses = [], []
                for mask in (low, ~low):
                    q = jnp.where(mask, q2, jnp.zeros_like(q2))
                    sc = jnp.where(valid, _dot_nt(q, k2), -1e30)
                    mx = jnp.max(sc, axis=-1, keepdims=True)
                    p = jnp.exp(sc - mx)
                    l = jnp.sum(p, axis=-1, keepdims=True)
                    outs.append(_dot(p, v2) / l)
                    lses.append(mx + jnp.log(l))
                o_ref[rows, hp * LANE:(hp + 1) * LANE] = jnp.where(low, outs[0], outs[1])
                l_ref[rows, hp * LANE:(hp + 1) * LANE] = jnp.where(low, lses[0], lses[1])

        block(0, True)
        if n > 1:
            def loop(i, carry):
                block(i, False)
                return carry
            lax.fori_loop(1, n, loop, 0, unroll=2)

    shp = jax.ShapeDtypeStruct((m, d * ATTN_W), F32)
    ospec =pl.BlockSpec((tq, ATTN_W), lambda c, b: (b, c))
    out, lse = pl.pallas_call(
        body, out_shape=(shp, shp), grid=(d, m // tq),
        in_specs=[pl.BlockSpec((tq, w3), lambda c, b: (b, c)),
                  pl.BlockSpec((ATTN_BLK, w3), lambda c, b: (jnp.maximum(b * n - 1, 0), c))],
        out_specs=(ospec, ospec), name=name,
        compiler_params=_cparams(("parallel", "parallel")))(qv, qv)
    return out, lse


def _attn_bwd(qv, ov, dov, lv, d, name, dep=None):
    m = qv.shape[0]
    w3 = 3 * ATTN_W
    _, tq, n = _attn_tiles(m * d, d)
    nb = m // ATTN_BLK
    scale = HEAD ** -0.5
    assert math.frexp(scale)[0] == 0.5, "the kernel scales bf16 q: exact only for a power of two"

    def body(x_ref, p_ref, nx_ref, o_ref, do_ref, l_ref, on_ref, don_ref, ln_ref, *rest):
        dq_ref, dk_ref, dv_ref = rest[-3:]
        b = pl.program_id(1)
        dk_ref[...] = jnp.zeros_like(dk_ref)
        dv_ref[...] = jnp.zeros_like(dv_ref)

        low = lax.broadcasted_iota(jnp.int32, (1, LANE), 1) < HEAD

        def pair_grads(q2, k2, v2, o2, do2, l2, valid):
            dq, dk, dv = [], 0.0, 0.0
            q2 = q2 * scale
            for mask, lse in ((low, l2[:, 0:1]), (~low, l2[:, HEAD:HEAD + 1])):
                q = jnp.where(mask, q2, jnp.zeros_like(q2))
                do = jnp.where(mask, do2, 0.0)
                sc = jnp.where(valid, _dot_nt(q, k2), -1e30)
                p = jnp.exp(sc - lse)
                delta = jnp.sum(do * o2, axis=-1, keepdims=True)
                ds = p * (_dot_nt(do, v2) - delta)
                dq.append(_dot(ds, k2))
                dk = dk + _dot_tn(ds, q)
                dv = dv + _dot_tn(p, do)
            return jnp.where(low, dq[0], dq[1]) * scale, dk, dv

        def cols(hp):
            return [slice(part * ATTN_W + hp * LANE, part * ATTN_W + (hp + 1) * LANE) for part in range(3)]

        def block(i, first):
            r0 = 0 if first else pl.multiple_of(i * ATTN_BLK, ATTN_BLK)
            rows = pl.ds(r0, ATTN_BLK)
            valid = _band_mask(b * n + i)
            if not first:
                krows = pl.ds(pl.multiple_of(i * ATTN_BLK - ATTN_BLK, ATTN_BLK), 2 * ATTN_BLK)
            for hp in range(ATTN_W // LANE):
                qs, ks, vs = cols(hp)
                if first:
                    k2 = jnp.concatenate([p_ref[:, ks], x_ref[0:ATTN_BLK, ks]], axis=0)
                    v2 = jnp.concatenate([p_ref[:, vs], x_ref[0:ATTN_BLK, vs]], axis=0)
                else:
                    k2 = x_ref[krows, ks]
                    v2 = x_ref[krows, vs]
                dq, dk, dv = pair_grads(x_ref[rows, qs], k2, v2, o_ref[rows, qs], do_ref[rows, qs],
                                        l_ref[rows, qs], valid)
                dq_ref[rows, qs] = dq
                if first:
                    dk_ref[0:ATTN_BLK, qs] += dk[ATTN_BLK:, :]
                    dv_ref[0:ATTN_BLK, qs] += dv[ATTN_BLK:, :]
                else:
                    dk_ref[krows, qs] += dk
                    dv_ref[krows, qs] += dv

        block(0, True)
        if n > 1:
            def loop(i, carry):
                block(i, False)
                return carry
            lax.fori_loop(1, n, loop, 0, unroll=2)

        def next_tile():
            last = slice((n - 1) * ATTN_BLK, n * ATTN_BLK)
            qi = lax.broadcasted_iota(jnp.int32, (ATTN_BLK, ATTN_BLK), 0)
            ki = lax.broadcasted_iota(jnp.int32, (ATTN_BLK, ATTN_BLK), 1)
            for hp in range(ATTN_W // LANE):
                qs, ks, vs = cols(hp)
                _, dk, dv = pair_grads(nx_ref[:, qs], x_ref[last, ks], x_ref[last, vs], on_ref[:, qs],
                                       don_ref[:, qs], ln_ref[:, qs], qi <= ki)
                dk_ref[last, qs] += dk
                dv_ref[last, qs] += dv

        if nb > n:
            pl.when((b + 1) * n < nb)(next_tile)

    nxt = lambda b: jnp.minimum((b + 1) * n, nb - 1)
    xs = pl.BlockSpec((tq, w3), lambda c, b: (b, c))
    xp = pl.BlockSpec((ATTN_BLK, w3), lambda c, b: (jnp.maximum(b * n - 1, 0), c))
    xn = pl.BlockSpec((ATTN_BLK, w3), lambda c, b: (nxt(b), c))
    a = pl.BlockSpec((tq, ATTN_W), lambda c, b: (b, c))
    an = pl.BlockSpec((ATTN_BLK, ATTN_W), lambda c, b: (nxt(b), c))
    shp = jax.ShapeDtypeStruct((m, d * ATTN_W), F32)
    dep_specs, dep_ops = _dep_args(dep)
    return pl.pallas_call(
        body, out_shape=(shp, shp, shp), grid=(d, m // tq),
        in_specs=[xs, xp, xn, a, a, a, an, an, an] + dep_specs, out_specs=(a, a, a), name=name,
        compiler_params=_cparams(("parallel", "parallel")))(qv, qv, qv, ov, dov, lv, ov, dov, lv, *dep_ops)


def _rms(x, g):
    ms = jnp.mean(x * x, axis=-1, keepdims=True)
    return x * lax.rsqrt(ms + RMS_EPS) * g


def _rms_bwd(x, g, dy):
    ms = jnp.mean(x * x, axis=-1, keepdims=True)
    r = lax.rsqrt(ms + RMS_EPS)
    dyg = dy * g
    dx = r * dyg - x * (r * r * r) * jnp.mean(x * dyg, axis=-1, keepdims=True)
    return dx, dy * x * r


def _mix_fwd(outs, lses, lru, s5, g, h_in, w_out, ln_g, ln_b, name):
    s = lru.shape[0]
    tm = 512

    def body(o1, o2, o3, l1, l2, l3, lru_ref, s5_ref, g_ref, x_ref, w_ref, lg_ref, lb_ref,
             mixed_t_ref, r_ref, h_ref, ht_ref, ov1, ov2, ov3, lv1, lv2, lv3, so2, so3, sl2, sl3):
        for d, src, dst in ((DILATIONS[1], o2, so2), (DILATIONS[2], o3, so3),
                            (DILATIONS[1], l2, sl2), (DILATIONS[2], l3, sl3)):
            _to_tokens(src, dst, d, tm)
        a1, a2, a3 = l1[...], _token_value(sl2), _token_value(sl3)
        mx = jnp.maximum(jnp.maximum(a1, a2), a3)
        e1, e2, e3 = jnp.exp(a1 - mx), jnp.exp(a2 - mx), jnp.exp(a3 - mx)
        den = e1 + e2 + e3
        o = (e1 * o1[...] + e2 * _token_value(so2) + e3 * _token_value(so3)) / den
        lse = mx + jnp.log(den)
        ov1[...] = o
        lv1[...] = lse
        for j in range(ATTN_W // LANE):
            so2[j] = o[:, j * LANE:(j + 1) * LANE]
            sl2[j] = lse[:, j * LANE:(j + 1) * LANE]
        for d, o_dst, l_dst in ((DILATIONS[1], ov2, lv2), (DILATIONS[2], ov3, lv3)):
            _to_dilated(so2, o_dst, d, tm)
            _to_dilated(sl2, l_dst, d, tm)
        gg = g_ref[...]
        mixed = jnp.concatenate([_rms(o, gg[:, :ATTN_W]),
                                 _rms(lru_ref[...], gg[:, ATTN_W:ATTN_W + LRU_W]),
                                 _rms(s5_ref[...], gg[:, ATTN_W + LRU_W:])], axis=1)
        mixed_t_ref[...] = mixed.T.astype(BF16)
        r = ALPHA * x_ref[...] + _dot(mixed, w_ref[...])
        h = _layer_norm(r, lg_ref[...], lb_ref[...])
        r_ref[...] = r
        h_ref[...] = h
        ht_ref[...] = h.T.astype(BF16)

    a = pl.BlockSpec((tm, ATTN_W), lambda i: (i, 0))
    s5s = pl.BlockSpec((tm, S5_W), lambda i: (i, 0))
    full = pl.BlockSpec((tm, D_MODEL), lambda i: (i, 0))
    vec = pl.BlockSpec((1, D_MODEL), lambda i: (0, 0))
    dil = [_dilated_spec(tm, d, ATTN_W) for d in DILATIONS]
    dshape = [jax.ShapeDtypeStruct((s // d, d * ATTN_W), F32) for d in DILATIONS]
    tshape = jax.ShapeDtypeStruct((D_MODEL, s), BF16)
    fshape = jax.ShapeDtypeStruct((s, D_MODEL), F32)
    tspec = pl.BlockSpec((D_MODEL, tm), lambda i: (0, i))
    res = pl.pallas_call(
        body, out_shape=(tshape, fshape, fshape, tshape, *dshape, *dshape),
        grid=(s // tm,),
        in_specs=dil + dil + [a, s5s, vec, full, pl.BlockSpec((D_MODEL, D_MODEL), lambda i: (0, 0)), vec, vec],
        out_specs=(tspec, full, full, tspec, *dil, *dil),
        scratch_shapes=[_token_scratch(tm, ATTN_W)] * 4, name=name,
        compiler_params=_cparams(("parallel",)))(*outs, *lses, lru, s5, g, h_in, w_out, ln_g, ln_b)
    return res[0], res[1], res[2], res[3], res[4:7], res[7:10]


def _mix_bwd(r, dh, ln_g, w_out, o, lru, s5, g, name, dep=None):
    s = lru.shape[0]
    tm = 512

    def body(r_ref, dh_ref, lg_ref, w_ref, o_ref, lru_ref, s5_ref, g_ref, *rest):
        dr_ref, dlg_ref, dlb_ref, do_ref, do2_ref, do3_ref, dlru_ref, ds5_ref, dg_ref, stage = rest[-10:]

        @pl.when(pl.program_id(0) == 0)
        def _():
            dg_ref[...] = jnp.zeros_like(dg_ref)
            dlg_ref[...] = jnp.zeros_like(dlg_ref)
            dlb_ref[...] = jnp.zeros_like(dlb_ref)
        gg = g_ref[...]
        dr, dlg_rows, dlb_rows = _layer_norm_bwd(r_ref[...], dh_ref[...], lg_ref[...])
        dr_ref[...] = dr
        dlg_ref[...] += dlg_rows
        dlb_ref[...] += dlb_rows
        dm = _dot_nt(dr, w_ref[...])
        dx, dgr = _rms_bwd(o_ref[...], gg[:, :ATTN_W], dm[:, :ATTN_W])
        do_ref[...] = dx
        for j in range(ATTN_W // LANE):
            stage[j] = dx[:, j * LANE:(j + 1) * LANE]
        _to_dilated(stage, do2_ref, DILATIONS[1], tm)
        _to_dilated(stage, do3_ref, DILATIONS[2], tm)
        dg_ref[:, :ATTN_W] += jnp.sum(dgr, axis=0, keepdims=True)
        dx, dgr = _rms_bwd(lru_ref[...], gg[:, ATTN_W:ATTN_W + LRU_W], dm[:, ATTN_W:ATTN_W + LRU_W])
        dlru_ref[...] = dx
        dg_ref[:, ATTN_W:ATTN_W + LRU_W] += jnp.sum(dgr, axis=0, keepdims=True)
        dx, dgr = _rms_bwd(s5_ref[...], gg[:, ATTN_W + LRU_W:], dm[:, ATTN_W + LRU_W:])
        ds5_ref[...] = dx
        dg_ref[:, ATTN_W + LRU_W:] += jnp.sum(dgr, axis=0, keepdims=True)

    a = pl.BlockSpec((tm, ATTN_W), lambda i: (i, 0))
    s5s = pl.BlockSpec((tm, S5_W), lambda i: (i, 0))
    full = pl.BlockSpec((tm, D_MODEL), lambda i: (i, 0))
    vec = pl.BlockSpec((1, D_MODEL), lambda i: (0, 0))
    dil = [_dilated_spec(tm, d, ATTN_W) for d in DILATIONS]
    dshape = [jax.ShapeDtypeStruct((s // d, d * ATTN_W), F32) for d in DILATIONS]
    dep_specs, dep_ops = _dep_args(dep)
    vshape = jax.ShapeDtypeStruct((1, D_MODEL), F32)
    res = pl.pallas_call(
        body, out_shape=(jax.ShapeDtypeStruct((s, D_MODEL), F32), vshape, vshape, *dshape,
                         jax.ShapeDtypeStruct((s, LRU_W), F32), jax.ShapeDtypeStruct((s, S5_W), F32), vshape),
        grid=(s // tm,),
        in_specs=[full, full, vec, pl.BlockSpec((D_MODEL, D_MODEL), lambda i: (0, 0)), a, a, s5s, vec] + dep_specs,
        out_specs=(full, vec, vec, *dil, a, s5s, vec), scratch_shapes=[_token_scratch(tm, ATTN_W)], name=name,
        compiler_params=_cparams(("arbitrary",)))(r, dh, ln_g, w_out, o, lru, s5, g, *dep_ops)
    return res[0], res[1], res[2], res[3:6], res[6], res[7], res[8]


def _lru_gate_math(xc, pre_r, pre_i, lam):
    r = _sigmoid(pre_r)
    i = _sigmoid(pre_i)
    log_a = -LRU_C * r * _softplus(-lam)
    a = jnp.exp(log_a)
    u = jnp.sqrt(-_expm1(2.0 * log_a)) * (i * xc)
    return a, u


def _lru_conv(x, prev8, cw, cb):
    y = cb + cw[LRU_CONV - 1:LRU_CONV, :] * x
    for k in range(LRU_CONV - 1):
        y = y + cw[k:k + 1, :] * _shift_down_prev(x, LRU_CONV - 1 - k, prev8)
    return y


def _lru_specs(s):
    xo = 3 * ATTN_W // LANE
    go = xo + LRU_W // LANE
    xr = pl.BlockSpec((s, LANE), lambda j: (0, xo + j))
    gt = pl.BlockSpec((s, LANE), lambda j: (0, go + j))
    cw = pl.BlockSpec((LRU_CONV, LANE), lambda j: (0, j))
    vec = pl.BlockSpec((1, LANE), lambda j: (0, j))
    wbd = pl.BlockSpec((LANE, LANE), lambda j: (j, j))
    col = pl.BlockSpec((s, LANE), lambda j: (0, j))
    return xr, gt, cw, vec, wbd, col


def _lru_fwd(proj, cw, cb, wr, br, wi, bi, lam, name):
    s = proj.shape[0]
    t = SCAN_T

    def body(xr_ref, gt_ref, cw_ref, cb_ref, wr_ref, br_ref, wi_ref, bi_ref, lam_ref, o_ref, xc_ref, a_ref, h_ref):
        cwv, cbv, lamv = cw_ref[...], cb_ref[...], lam_ref[...]
        wrv, wiv, brv, biv = wr_ref[...], wi_ref[...], br_ref[...], bi_ref[...]

        def chunk(c, carry):
            h_c, prev8 = carry
            rows = pl.ds(pl.multiple_of(c * t, t), t)
            x = xr_ref[rows, :]
            xc = _lru_conv(x, prev8, cwv, cbv)
            a, u = _lru_gate_math(xc, _dot(xc, wrv) + brv, _dot(xc, wiv) + biv, lamv)
            h = _scan_chunk(a, u, h_c)
            xc_ref[rows, :] = xc
            a_ref[rows, :] = a
            h_ref[rows, :] = h
            o_ref[rows, :] = h * _gelu(gt_ref[rows, :])
            return h[t - 1:t, :], x[t - 8:t, :]

        lax.fori_loop(0, s // t, chunk, (jnp.zeros((1, LANE), F32), jnp.zeros((8, LANE), F32)))

    xr, gt, cws, vec, wbd, col = _lru_specs(s)
    shp = jax.ShapeDtypeStruct((s, LRU_W), F32)
    return pl.pallas_call(
        body, out_shape=(shp,) * 4, grid=(LRU_W // LANE,),
        in_specs=[xr, gt, cws, vec, wbd, vec, wbd, vec, vec], out_specs=(col,) * 4, name=name,
        compiler_params=_cparams(("parallel",)))(proj, proj, cw, cb, wr, br, wi, bi, lam)


def _lru_bwd(proj, dout, xc_all, a_all, h_all, cw, cb, wr, br, wi, bi, lam, name):
    s = proj.shape[0]
    t = SCAN_T
    nc = s // t

    def body(xr_ref, gt_ref, do_ref, xc_s, a_s, h_s, cw_ref, cb_ref, wr_ref, br_ref, wi_ref, bi_ref, lam_ref,
             dxr_ref, dgt_ref, dcw_ref, dcb_ref, dwr_ref, dbr_ref, dwi_ref, dbi_ref, dlam_ref):
        cwv, cbv, lamv = cw_ref[...], cb_ref[...], lam_ref[...]
        wrv, wiv, brv, biv = wr_ref[...], wi_ref[...], br_ref[...], bi_ref[...]
        z1 = jnp.zeros((1, LANE), F32)
        zw = jnp.zeros((LANE, LANE), F32)

        def bchunk(ci, carry):
            g_next, a_next, dxc_next8, dcw, dcb, dwr, dbr, dwi, dbi, dlam = carry
            c = nc - 1 - ci
            t0 = pl.multiple_of(c * t, t)
            rows = pl.ds(t0, t)
            before = pl.ds(pl.multiple_of(jnp.maximum(t0 - 8, 0), 8), 8)
            has_prev = (c > 0).astype(F32)
            x, gt, do = xr_ref[rows, :], gt_ref[rows, :], do_ref[rows, :]
            xc, a, h = xc_s[rows, :], a_s[rows, :], h_s[rows, :]
            prev8_h = h_s[before, :] * has_prev
            dgt_ref[rows, :] = do * h * _gelu_grad(gt)
            dh = do * _gelu(gt)
            a_plus = _shift_up_next(a, 1, jnp.broadcast_to(a_next, (8, LANE)))
            g = _scan_chunk(a_plus, dh, g_next, reverse=True)
            da = g * _shift_down_prev(h, 1, prev8_h)
            pre_r = _dot(xc, wrv) + brv
            pre_i = _dot(xc, wiv) + biv
            _, vjp = jax.vjp(_lru_gate_math, xc, pre_r, pre_i, lamv)
            dxc, dpre_r, dpre_i, dlam_c = vjp((da, g))
            dxc = dxc + _dot_nt(dpre_r, wrv) + _dot_nt(dpre_i, wiv)
            dx = cwv[LRU_CONV - 1:LRU_CONV, :] * dxc
            dcw_rows = [None] * LRU_CONV
            dcw_rows[LRU_CONV - 1] = jnp.sum(dxc * x, axis=0, keepdims=True)
            for k in range(LRU_CONV - 1):
                dxc_ahead = _shift_up_next(dxc, LRU_CONV - 1 - k, dxc_next8)
                dx = dx + cwv[k:k + 1, :] * dxc_ahead
                dcw_rows[k] = jnp.sum(dxc_ahead * x, axis=0, keepdims=True)
            dxr_ref[rows, :] = dx
            return (g[0:1, :], a[0:1, :], dxc[0:8, :],
                    dcw + jnp.concatenate(dcw_rows, axis=0),
                    dcb + jnp.sum(dxc, axis=0, keepdims=True),
                    dwr + _dot_tn(xc, dpre_r), dbr + jnp.sum(dpre_r, axis=0, keepdims=True),
                    dwi + _dot_tn(xc, dpre_i), dbi + jnp.sum(dpre_i, axis=0, keepdims=True),
                    dlam + dlam_c)

        init = (z1, z1, jnp.zeros((8, LANE), F32), jnp.zeros((LRU_CONV, LANE), F32), z1, zw, z1, zw, z1, z1)
        res = lax.fori_loop(0, nc, bchunk, init)
        dcw_ref[...] = res[3]
        dcb_ref[...] = res[4]
        dwr_ref[...] = res[5]
        dbr_ref[...] = res[6]
        dwi_ref[...] = res[7]
        dbi_ref[...] = res[8]
        dlam_ref[...] = res[9]

    xr, gt, cws, vec, wbd, col = _lru_specs(s)
    vshape = jax.ShapeDtypeStruct((1, LRU_W), F32)
    wshape = jax.ShapeDtypeStruct((LRU_W, LRU_W), F32)
    return pl.pallas_call(
        body,
        out_shape=(jax.ShapeDtypeStruct((s, LRU_W), F32), jax.ShapeDtypeStruct((s, LRU_W), F32),
                   jax.ShapeDtypeStruct((LRU_CONV, LRU_W), F32), vshape, wshape, vshape, wshape, vshape, vshape),
        grid=(LRU_W // LANE,),
        in_specs=[xr, gt, col, col, col, col, cws, vec, wbd, vec, wbd, vec, vec],
        out_specs=(col, col, cws, vec, wbd, vec, wbd, vec, vec), name=name,
        compiler_params=_cparams(("parallel",)))(proj, proj, dout, xc_all, a_all, h_all, cw, cb, wr, br, wi, bi,
                                                 lam)


def _s5_disc_math(a_re, a_im, log_step, bt_re, bt_im):
    step = jnp.exp(log_step)
    dt_re, dt_im = step * a_re, step * a_im
    mag = jnp.exp(dt_re)
    ab_re, ab_im = mag * jnp.cos(dt_im), mag * jnp.sin(dt_im)
    z_re, z_im = ab_re - 1.0, ab_im
    den = a_re * a_re + a_im * a_im
    f_re = (z_re * a_re + z_im * a_im) / den
    f_im = (z_im * a_re - z_re * a_im) / den
    bb_re = f_re * bt_re - f_im * bt_im
    bb_im = f_re * bt_im + f_im * bt_re
    return ab_re, ab_im, bb_re, bb_im


def _s5_disc_fwd(a_re, a_im, log_step, bt_re, bt_im, name):
    def body(ar, ai, ls, br, bi, o1, o2, o3, o4):
        r = _s5_disc_math(ar[...], ai[...], ls[...], br[...], bi[...])
        o1[...], o2[...], o3[...], o4[...] = r

    shp = jax.ShapeDtypeStruct(a_re.shape, F32)
    return pl.pallas_call(body, out_shape=(shp,) * 4, name=name)(a_re, a_im, log_step, bt_re, bt_im)


def _s5_disc_bwd(a_re, a_im, log_step, bt_re, bt_im, cts, name):
    def body(ar, ai, ls, br, bi, c1, c2, c3, c4, o1, o2, o3, o4, o5):
        _, vjp = jax.vjp(_s5_disc_math, ar[...], ai[...], ls[...], br[...], bi[...])
        r = vjp((c1[...], c2[...], c3[...], c4[...]))
        o1[...], o2[...], o3[...], o4[...], o5[...] = r

    shp = jax.ShapeDtypeStruct(a_re.shape, F32)
    return pl.pallas_call(body, out_shape=(shp,) * 5, name=name)(a_re, a_im, log_step, bt_re, bt_im, *cts)


def _s5_u_specs(s):
    uo = (3 * ATTN_W + 2 * LRU_W) // LANE
    return (pl.BlockSpec((s, LANE), lambda j: (0, uo)), pl.BlockSpec((s, LANE), lambda j: (0, uo + 1)))


def _s5_scan_fwd(proj, b_re, b_im, lam_re, lam_im, c_re, c_im, name):
    s = proj.shape[0]
    t = SCAN_T

    def body(u0_ref, u1_ref, bre_ref, bim_ref, lre_ref, lim_ref, cre_ref, cim_ref, xre_ref, xim_ref, y_ref):
        @pl.when(pl.program_id(0) == 0)
        def _():
            y_ref[...] = jnp.zeros_like(y_ref)
        lr, li = lre_ref[...], lim_ref[...]
        consts = _cscan_consts(lr, li, False)
        bre, bim, cre, cim = bre_ref[...], bim_ref[...], cre_ref[...], cim_ref[...]

        def chunk(c, carry):
            cr, ci = carry
            rows = pl.ds(pl.multiple_of(c * t, t), t)
            u = jnp.concatenate([u0_ref[rows, :], u1_ref[rows, :]], axis=1).astype(BF16)
            xr, xi = _cscan_chunk(_dot(u, bre), _dot(u, bim), consts, (cr, ci))
            xre_ref[rows, :] = xr
            xim_ref[rows, :] = xi
            y_ref[rows, :] += _dot(xr, cre) - _dot(xi, cim)
            return xr[t - 1:t, :], xi[t - 1:t, :]

        z = jnp.zeros((1, S5_BLK), F32)
        lax.fori_loop(0, s // t, chunk, (z, z))

    u0, u1 = _s5_u_specs(s)
    bsp = pl.BlockSpec((S5_W, S5_BLK), lambda j: (0, j))
    csp = pl.BlockSpec((S5_BLK, S5_W), lambda j: (j, 0))
    vec = pl.BlockSpec((1, S5_BLK), lambda j: (0, j))
    xsp = pl.BlockSpec((s, S5_BLK), lambda j: (0, j))
    ysp = pl.BlockSpec((s, S5_W), lambda j: (0, 0))
    xshape = jax.ShapeDtypeStruct((s, S5_STATES), F32)
    return pl.pallas_call(
        body, out_shape=(xshape, xshape, jax.ShapeDtypeStruct((s, S5_W), F32)),
        grid=(S5_STATES // S5_BLK,), in_specs=[u0, u1, bsp, bsp, vec, vec, csp, csp],
        out_specs=(xsp, xsp, ysp), name=name,
        compiler_params=_cparams(("arbitrary",)))(proj, proj, b_re, b_im, lam_re, lam_im, c_re, c_im)


def _s5_scan_bwd(proj, dy, du_init, x_re, x_im, b_re, b_im, lam_re, lam_im, c_re, c_im, name):
    s = proj.shape[0]
    t = SCAN_T
    nc = s // t

    def body(u0_ref, u1_ref, dy_ref, dui_ref, xre_ref, xim_ref, bre_ref, bim_ref, lre_ref, lim_ref,
             cre_ref, cim_ref, du_ref, dlr_ref, dli_ref, dbr_ref, dbi_ref, dcr_ref, dci_ref):
        @pl.when(pl.program_id(0) == 0)
        def _():
            du_ref[...] = dui_ref[...]
        mr, mi = lre_ref[...], -lim_ref[...]
        consts = _cscan_consts(mr, mi, True)
        bre, bim, cre, cim = bre_ref[...], bim_ref[...], cre_ref[...], cim_ref[...]
        dbr_ref[...] = jnp.zeros_like(dbr_ref)
        dbi_ref[...] = jnp.zeros_like(dbi_ref)
        dcr_ref[...] = jnp.zeros_like(dcr_ref)
        dci_ref[...] = jnp.zeros_like(dci_ref)

        def chunk(ci_, carry):
            gnr, gni, dlr, dli = carry
            c = nc - 1 - ci_
            t0 = pl.multiple_of(c * t, t)
            rows = pl.ds(t0, t)
            before = pl.ds(pl.multiple_of(jnp.maximum(t0 - 8, 0), 8), 8)
            has_prev = (c > 0).astype(F32)
            dyc = dy_ref[rows, :].astype(BF16)
            u = jnp.concatenate([u0_ref[rows, :], u1_ref[rows, :]], axis=1).astype(BF16)
            gr, gi = _cscan_chunk(_dot_nt(dyc, cre), -_dot_nt(dyc, cim), consts, (gnr, gni), reverse=True)
            xr, xi = xre_ref[rows, :], xim_ref[rows, :]
            xpr = _shift_down_prev(xr, 1, xre_ref[before, :] * has_prev)
            xpi = _shift_down_prev(xi, 1, xim_ref[before, :] * has_prev)
            dlr = dlr + jnp.sum(gr * xpr + gi * xpi, axis=0, keepdims=True)
            dli = dli + jnp.sum(gi * xpr - gr * xpi, axis=0, keepdims=True)
            du_ref[rows, :] += _dot_nt(gr, bre) + _dot_nt(gi, bim)
            dbr_ref[...] += _dot_tn(u, gr)
            dbi_ref[...] += _dot_tn(u, gi)
            dcr_ref[...] += _dot_tn(xr, dyc)
            dci_ref[...] -= _dot_tn(xi, dyc)
            return gr[0:1, :], gi[0:1, :], dlr, dli

        z = jnp.zeros((1, S5_BLK), F32)
        res = lax.fori_loop(0, nc, chunk, (z, z, z, z))
        dlr_ref[...] = res[2]
        dli_ref[...] = res[3]

    u0, u1 = _s5_u_specs(s)
    bsp = pl.BlockSpec((S5_W, S5_BLK), lambda j: (0, j))
    csp = pl.BlockSpec((S5_BLK, S5_W), lambda j: (j, 0))
    vec = pl.BlockSpec((1, S5_BLK), lambda j: (0, j))
    xsp = pl.BlockSpec((s, S5_BLK), lambda j: (0, j))
    ysp = pl.BlockSpec((s, S5_W), lambda j: (0, 0))
    return pl.pallas_call(
        body,
        out_shape=(jax.ShapeDtypeStruct((s, S5_W), F32),
                   jax.ShapeDtypeStruct((1, S5_STATES), F32), jax.ShapeDtypeStruct((1, S5_STATES), F32),
                   jax.ShapeDtypeStruct((S5_W, S5_STATES), F32), jax.ShapeDtypeStruct((S5_W, S5_STATES), F32),
                   jax.ShapeDtypeStruct((S5_STATES, S5_W), F32), jax.ShapeDtypeStruct((S5_STATES, S5_W), F32)),
        grid=(S5_STATES // S5_BLK,),
        in_specs=[u0, u1, ysp, ysp, xsp, xsp, bsp, bsp, vec, vec, csp, csp],
        out_specs=(ysp, vec, vec, bsp, bsp, csp, csp), name=name,
        compiler_params=_cparams(("arbitrary",)))(
            proj, proj, dy, du_init, x_re, x_im, b_re, b_im, lam_re, lam_im, c_re, c_im)


def _s5_out_fwd(proj, y_acc, dvec, w_glu, b_glu, name):
    s = proj.shape[0]
    tm = 512
    uo = (3 * ATTN_W + 2 * LRU_W) // LANE

    def body(u0_ref, u1_ref, y_ref, d_ref, w_ref, b_ref, o_ref, yp_ref):
        u = jnp.concatenate([u0_ref[...], u1_ref[...]], axis=1)
        y = y_ref[...] + d_ref[...] * u
        yp_ref[...] = y
        yg = _gelu(y)
        o_ref[...] = yg * _sigmoid(_dot(yg, w_ref[...]) + b_ref[...])

    u0 = pl.BlockSpec((tm, LANE), lambda i: (i, uo))
    u1 = pl.BlockSpec((tm, LANE), lambda i: (i, uo + 1))
    row = pl.BlockSpec((tm, S5_W), lambda i: (i, 0))
    vec = pl.BlockSpec((1, S5_W), lambda i: (0, 0))
    wsp = pl.BlockSpec((S5_W, S5_W), lambda i: (0, 0))
    shp = jax.ShapeDtypeStruct((s, S5_W), F32)
    return pl.pallas_call(
        body, out_shape=(shp, shp), grid=(s // tm,), in_specs=[u0, u1, row, vec, wsp, vec],
        out_specs=(row, row), name=name,
        compiler_params=_cparams(("parallel",)))(proj, proj, y_acc, dvec, w_glu, b_glu)


def _s5_out_bwd(proj, y_pre, dout, dvec, w_glu, b_glu, name, dep=None):
    s = proj.shape[0]
    tm = 512
    uo = (3 * ATTN_W + 2 * LRU_W) // LANE

    def body(u0_ref, u1_ref, y_ref, do_ref, d_ref, w_ref, b_ref, *rest):
        dy_ref, dud_ref, dd_ref, dw_ref, db_ref = rest[-5:]

        @pl.when(pl.program_id(0) == 0)
        def _():
            dd_ref[...] = jnp.zeros_like(dd_ref)
            dw_ref[...] = jnp.zeros_like(dw_ref)
            db_ref[...] = jnp.zeros_like(db_ref)
        u = jnp.concatenate([u0_ref[...], u1_ref[...]], axis=1)
        y = y_ref[...]
        do = do_ref[...]
        yg = _gelu(y)
        sg = _sigmoid(_dot(yg, w_ref[...]) + b_ref[...])
        dz = do * yg * sg * (1.0 - sg)
        dyg = do * sg + _dot_nt(dz, w_ref[...])
        dy = dyg * _gelu_grad(y)
        dy_ref[...] = dy
        dud_ref[...] = d_ref[...] * dy
        dd_ref[...] += jnp.sum(dy * u, axis=0, keepdims=True)
        dw_ref[...] += _dot_tn(yg, dz)
        db_ref[...] += jnp.sum(dz, axis=0, keepdims=True)

    u0 = pl.BlockSpec((tm, LANE), lambda i: (i, uo))
    u1 = pl.BlockSpec((tm, LANE), lambda i: (i, uo + 1))
    row = pl.BlockSpec((tm, S5_W), lambda i: (i, 0))
    vec = pl.BlockSpec((1, S5_W), lambda i: (0, 0))
    wsp = pl.BlockSpec((S5_W, S5_W), lambda i: (0, 0))
    shp = jax.ShapeDtypeStruct((s, S5_W), F32)
    vshape = jax.ShapeDtypeStruct((1, S5_W), F32)
    dep_specs, dep_ops = _dep_args(dep)
    return pl.pallas_call(
        body, out_shape=(shp, shp, vshape, jax.ShapeDtypeStruct((S5_W, S5_W), F32), vshape),
        grid=(s // tm,), in_specs=[u0, u1, row, row, vec, wsp, vec] + dep_specs,
        out_specs=(row, row, vec, wsp, vec), name=name,
        compiler_params=_cparams(("arbitrary",)))(proj, proj, y_pre, dout, dvec, w_glu, b_glu, *dep_ops)


def _ffn_conv(x, prev8, cw, cb):
    y = cb + cw[FFN_CONV - 1:FFN_CONV, :] * x
    for k in range(FFN_CONV - 1):
        y = y + cw[k:k + 1, :] * _shift_down_prev(x, FFN_CONV - 1 - k, prev8)
    return y


def _ffn_up_act(h, wg, cw, cb, name, dep=None):
    s, d = h.shape
    tm = 512
    tb = 2 * FFN_CB
    nt = D_FF // FFN_CB

    def body(h_ref, wgate_ref, wval_ref, cw_ref, cb_ref, *rest):
        up_ref, y_ref, o_ref, ot_ref, carry = rest[-5:]

        @pl.when(pl.program_id(1) == 0)
        def _():
            carry[...] = jnp.zeros_like(carry)
        hb = h_ref[...].astype(BF16)
        x = jnp.concatenate([_dot(hb, wgate_ref[...]), _dot(hb, wval_ref[...])], axis=1)
        up_ref[...] = x.astype(BF16)
        y = _ffn_conv(x, carry[...], cw_ref[...], cb_ref[...])
        y_ref[...] = y
        carry[...] = x[tm - 8:tm, :]
        act = _gelu(y[:, :FFN_CB]) * y[:, FFN_CB:]
        o_ref[...] = act.astype(BF16)
        ot_ref[...] = act.T.astype(BF16)

    dep_specs, dep_ops = _dep_args(dep)
    return pl.pallas_call(
        body, out_shape=(jax.ShapeDtypeStruct((s, 2 * D_FF), BF16), jax.ShapeDtypeStruct((s, 2 * D_FF), F32),
                         jax.ShapeDtypeStruct((s, D_FF), BF16), jax.ShapeDtypeStruct((D_FF, s), BF16)),
        grid=(nt, s // tm),
        in_specs=[pl.BlockSpec((tm, d), lambda t, i: (i, 0)),
                  pl.BlockSpec((None, d, FFN_CB), lambda t, i: (t, 0, 0)),
                  pl.BlockSpec((None, d, FFN_CB), lambda t, i: (t + nt, 0, 0)),
                  pl.BlockSpec((FFN_CONV, tb), lambda t, i: (0, t)),
                  pl.BlockSpec((1, tb), lambda t, i: (0, t))] + dep_specs,
        out_specs=(pl.BlockSpec((tm, tb), lambda t, i: (i, t)), pl.BlockSpec((tm, tb), lambda t, i: (i, t)),
                   pl.BlockSpec((tm, FFN_CB), lambda t, i: (i, t)), pl.BlockSpec((FFN_CB, tm), lambda t, i: (t, i))),
        scratch_shapes=[pltpu.VMEM((8, tb), F32)], name=name,
        compiler_params=_cparams(("parallel", "arbitrary")))(h, wg, wg, cw, cb, *dep_ops)


def _ffn_bwd(up, y_conv, dr, w_down, wg, cw, name):
    s = up.shape[0]
    d = dr.shape[1]
    tm = 512
    tb = 2 * FFN_CB
    nr = s // tm
    nt = D_FF // FFN_CB

    def body(x_ref, y_ref, dr_ref, wd_ref, wgate_ref, wval_ref, cw_ref,
             dup_ref, dh_ref, dcw_ref, dcb_ref, carry):
        i, t = pl.program_id(0), pl.program_id(1)

        @pl.when(i == 0)
        def _():
            carry[t] = jnp.zeros((8, tb), F32)

        @pl.when(t == 0)
        def _():
            dh_ref[...] = ALPHA * dr_ref[...]
        cwv = cw_ref[...]
        x = x_ref[...]
        dact = _dot_nt(dr_ref[...], wd_ref[...])
        gate, val = y_ref[:, :FFN_CB], y_ref[:, FFN_CB:]
        dy = jnp.concatenate([dact * val * _gelu_grad(gate), dact * _gelu(gate)], axis=1)
        next8 = carry[t]
        carry[t] = dy[0:8, :]
        dx = cwv[FFN_CONV - 1:FFN_CONV, :] * dy
        dcw_rows = [None] * FFN_CONV
        dcw_rows[FFN_CONV - 1] = jnp.sum(dy * x, axis=0, keepdims=True)
        for k in range(FFN_CONV - 1):
            dy_ahead = _shift_up_next(dy, FFN_CONV - 1 - k, next8)
            dx = dx + cwv[k:k + 1, :] * dy_ahead
            dcw_rows[k] = jnp.sum(dy_ahead * x, axis=0, keepdims=True)
        dup = dx.astype(BF16)
        dup_ref[...] = dup
        dh_ref[...] += _dot_nt(dup[:, :FFN_CB], wgate_ref[...]) + _dot_nt(dup[:, FFN_CB:], wval_ref[...])
        dcw_ref[...] = jnp.concatenate(dcw_rows, axis=0)
        dcb_ref[...] = jnp.sum(dy, axis=0, keepdims=True)

    row = lambda i: nr - 1 - i
    return pl.pallas_call(
        body, out_shape=(jax.ShapeDtypeStruct((s, 2 * D_FF), BF16), jax.ShapeDtypeStruct((s, d), F32),
                         jax.ShapeDtypeStruct((nr, FFN_CONV, 2 * D_FF), F32),
                         jax.ShapeDtypeStruct((nr, 1, 2 * D_FF), F32)),
        grid=(nr, nt),
        in_specs=[pl.BlockSpec((tm, tb), lambda i, t: (row(i), t)),
                  pl.BlockSpec((tm, tb), lambda i, t: (row(i), t)),
                  pl.BlockSpec((tm, d), lambda i, t: (row(i), 0)),
                  pl.BlockSpec((FFN_CB, d), lambda i, t: (t, 0)),
                  pl.BlockSpec((None, d, FFN_CB), lambda i, t: (t, 0, 0)),
                  pl.BlockSpec((None, d, FFN_CB), lambda i, t: (t + nt, 0, 0)),
                  pl.BlockSpec((FFN_CONV, tb), lambda i, t: (0, t))],
        out_specs=(pl.BlockSpec((tm, tb), lambda i, t: (row(i), t)),
                   pl.BlockSpec((tm, d), lambda i, t: (row(i), 0)),
                   pl.BlockSpec((None, FFN_CONV, tb), lambda i, t: (row(i), 0, t)),
                   pl.BlockSpec((None, 1, tb), lambda i, t: (row(i), 0, t))),
        scratch_shapes=[pltpu.VMEM((nt, 8, tb), F32)], name=name,
        compiler_params=_cparams(("arbitrary", "arbitrary")))(up, y_conv, dr, w_down, wg, wg, cw)


def _sum_partials(ld_ref):
    gg = ld_ref[0].astype(F32)
    for k in range(1, N_DEV):
        gg = gg + ld_ref[k].astype(F32)
    return gg


def _adam_update(w, g, m, v):
    mn = ADAM_B1 * m + (1.0 - ADAM_B1) * g
    vn = ADAM_B2 * v + (1.0 - ADAM_B2) * (g * g)
    m_hat = mn / (1.0 - ADAM_B1 ** ADAM_STEP)
    v_hat = vn / (1.0 - ADAM_B2 ** ADAM_STEP)
    return -ADAM_LR * (m_hat / (jnp.sqrt(v_hat) + ADAM_EPS) + ADAM_WD * w), mn, vn


def _adamw_many(landed, ws, ms, vs, name):
    n, nl = len(ws), len(landed)

    def body(*refs):
        ld = refs[:nl * n]
        w_refs, m_refs, v_refs = (refs[(nl + k) * n:(nl + k + 1) * n] for k in range(3))
        outs = refs[(nl + 3) * n:]
        for i in range(n):
            for l in range(nl):
                one = slice(l, l + 1)
                gg = _sum_partials(ld[l * n + i])
                outs[i][one] = gg
                outs[n + i][one], outs[2 * n + i][one], outs[3 * n + i][one] = _adam_update(
                    w_refs[i][one], gg, m_refs[i][one], v_refs[i][one])

    vm = pl.BlockSpec(memory_space=pltpu.VMEM)
    shapes = [jax.ShapeDtypeStruct(w.shape, F32) for w in ws] * 4
    res = pl.pallas_call(
        body, out_shape=tuple(shapes), in_specs=[vm] * ((nl + 3) * n), out_specs=tuple([vm] * (4 * n)),
        name=name, compiler_params=_cparams())(*[a for layer in landed for a in layer], *ws, *ms, *vs)
    return res[:n], res[n:2 * n], res[2 * n:3 * n], res[3 * n:]


def _adamw_sum(landed, w, m, v, layer, prev, name):
    _, r, c = landed.shape
    nl = w.shape[0]
    tm = 8
    for cand in (512, 256, 128, 64, 32, 16):
        if r % cand == 0 and N_DEV * cand * c * 4 <= 4 * 1024 * 1024:
            tm = cand
            break

    def body(*refs):
        ld_ref, w_ref, m_ref, v_ref = refs[:4]
        g_ref, d_ref, mo_ref, vo_ref = refs[-4:]
        gg = _sum_partials(ld_ref)
        g_ref[...] = gg
        d_ref[...], mo_ref[...], vo_ref[...] = _adam_update(w_ref[...], gg, m_ref[...], v_ref[...])

    blk = pl.BlockSpec((None, tm, c), lambda i: (layer, i, 0))
    in_specs = [pl.BlockSpec((N_DEV, tm, c), lambda i: (0, i, 0)), blk, blk, blk]
    args = [landed, w, m, v]
    aliases = {}
    if prev is not None:
        in_specs += [pl.BlockSpec(memory_space=pl.ANY)] * 4
        args += list(prev)
        aliases = {4 + k: k for k in range(4)}
    shp = jax.ShapeDtypeStruct((nl, r, c), F32)
    return pl.pallas_call(
        body, out_shape=(shp,) * 4, grid=(r // tm,), in_specs=in_specs, out_specs=(blk,) * 4,
        input_output_aliases=aliases, name=name, compiler_params=_cparams(("parallel",)))(*args)


def _all_gather(shards, name):
    na = len(shards)

    def body(*refs):
        x_refs, out_refs = refs[:na], refs[na:2 * na]
        send_sems, recv_sems, local_sems = refs[2 * na:]
        x, y, c = lax.axis_index("x"), lax.axis_index("y"), lax.axis_index("c")
        me, sibling = (x, y, c), (x, y, 1 - c)
        chips = [(1 - x, y), (x, 1 - y), (1 - x, 1 - y)]

        def copy(a, k, block, to, src=None):
            dst = out_refs[a].at[4 * block[0] + 2 * block[1] + block[2]]
            return pltpu.make_async_remote_copy(
                src_ref=dst if src is None else src, dst_ref=dst,
                send_sem=send_sems.at[7 * a + k], recv_sem=recv_sems.at[7 * a + k],
                device_id=to, device_id_type=pl.DeviceIdType.MESH)

        mine, first, passed = [], [], []
        for a in range(na):
            cp = pltpu.make_async_copy(x_refs[a], out_refs[a].at[4 * x + 2 * y + c], local_sems.at[a])
            cp.start()
            mine.append(cp)
            cps = [copy(a, 0, me, sibling, src=x_refs[a])]
            cps += [copy(a, 1 + j, me, (*chip, c), src=x_refs[a]) for j, chip in enumerate(chips)]
            for cp in cps:
                cp.start()
            first += cps
        for j, chip in enumerate(chips):
            for a in range(na):
                copy(a, 1 + j, (*chip, c), me).wait_recv()
                cp = copy(a, 4 + j, (*chip, c), sibling)
                cp.start()
                passed.append(cp)
        for a in range(na):
            copy(a, 0, sibling, me).wait_recv()
            for j, chip in enumerate(chips):
                copy(a, 4 + j, (*chip, 1 - c), me).wait_recv()
        for cp in first + passed:
            cp.wait_send()
        for cp in mine:
            cp.wait()

    anyspec = pl.BlockSpec(memory_space=pl.ANY)
    return pl.pallas_call(
        body, out_shape=tuple(jax.ShapeDtypeStruct((N_DEV,) + t.shape, t.dtype) for t in shards),
        in_specs=[anyspec] * na, out_specs=tuple([anyspec] * na),
        scratch_shapes=[pltpu.SemaphoreType.DMA((7 * na,)), pltpu.SemaphoreType.DMA((7 * na,)),
                        pltpu.SemaphoreType.DMA((na,))],
        name=name)(*shards)


_HBM = pl.BlockSpec(memory_space=pltpu.HBM)
_SEM = pl.BlockSpec(memory_space=pltpu.SEMAPHORE)
_EFFECT = pltpu.SideEffectType.DATAFLOW_SIDE_EFFECTING


def _exchange_copies(src_refs, land_refs, send_sems, recv_sems, local_sems, gather):
    x, y, c = lax.axis_index("x"), lax.axis_index("y"), lax.axis_index("c")
    me = 4 * x + 2 * y + c
    per_array = send_sems.shape[0] > N_DEV - 1
    local, remote = [], []
    for a, (src, land) in enumerate(zip(src_refs, land_refs)):
        local.append(pltpu.make_async_copy(src if gather else src.at[me], land.at[me],
                                           local_sems.at[a if per_array else 0]))
    for k in range(1, N_DEV):
        px = x ^ ((k >> 2) & 1)
        py = y ^ ((k >> 1) & 1)
        pc = c ^ (k & 1)
        for a, (src, land) in enumerate(zip(src_refs, land_refs)):
            remote.append(pltpu.make_async_remote_copy(
                src_ref=src if gather else src.at[4 * px + 2 * py + pc], dst_ref=land.at[me],
                send_sem=send_sems.at[(7 * a if per_array else 0) + k - 1],
                recv_sem=recv_sems.at[(7 * a if per_array else 0) + k - 1],
                device_id=(px, py, pc), device_id_type=pl.DeviceIdType.MESH))
    return local, remote


def _exchange_start(srcs, gather, name, dep=None):
    na = len(srcs)
    ns = na if na <= 4 else 1
    lands = [lax.empty(((N_DEV,) + t.shape) if gather else t.shape, t.dtype) for t in srcs]

    def body(*refs):
        src_refs, land_refs = refs[:na], refs[na:2 * na]
        nin = 2 * na + (0 if dep is None else 1)
        send_sems, recv_sems, local_sems = refs[nin:nin + 3]
        token = refs[-1]
        local, remote = _exchange_copies(src_refs, land_refs, send_sems, recv_sems, local_sems, gather)
        for cp in local + remote:
            cp.start()
        token[...] = jnp.zeros_like(token)

    dep_specs, dep_ops = _dep_args(dep)
    hbm = lambda t: pltpu.HBM(t.shape, t.dtype)
    out = pl.pallas_call(
        body, name=name,
        out_shape=(pltpu.SemaphoreType.DMA((7 * ns,)), pltpu.SemaphoreType.DMA((7 * ns,)),
                   pltpu.SemaphoreType.DMA((ns,)), *[hbm(t) for t in srcs], *[hbm(t) for t in lands],
                   jax.ShapeDtypeStruct((8, LANE), F32)),
        in_specs=[_HBM] * (2 * na) + dep_specs,
        out_specs=(_SEM, _SEM, _SEM, *[_HBM] * (2 * na), pl.BlockSpec(memory_space=pltpu.VMEM)),
        input_output_aliases={i: 3 + i for i in range(2 * na)},
        compiler_params=pltpu.CompilerParams(has_side_effects=_EFFECT),
    )(*[pltpu.with_memory_space_constraint(t, pltpu.HBM) for t in srcs + lands], *dep_ops)
    return (out[:3], out[3:3 + na], out[3 + na:3 + 2 * na]), out[-1]


def _exchange_wait(handle, gather, after, name):
    sems, srcs, lands = handle
    na = len(srcs)

    def body(*refs):
        src_refs, land_refs = refs[:na], refs[na:2 * na]
        send_sems, recv_sems, local_sems = refs[2 * na:2 * na + 3]
        local, remote = _exchange_copies(src_refs, land_refs, send_sems, recv_sems, local_sems, gather)
        for cp in remote:
            cp.wait_send()
            cp.wait_recv()
        for cp in local:
            cp.wait()

    hbm = lambda t: pltpu.HBM(t.shape, t.dtype)
    out = pl.pallas_call(
        body, name=name, out_shape=(*[hbm(t) for t in srcs], *[hbm(t) for t in lands]),
        in_specs=[_HBM] * (2 * na) + [_SEM] * 3 + [pl.BlockSpec(memory_space=pl.ANY)],
        out_specs=tuple([_HBM] * (2 * na)), input_output_aliases={i: i for i in range(2 * na)},
        compiler_params=pltpu.CompilerParams(has_side_effects=_EFFECT),
    )(*srcs, *lands, *sems, after)
    return out[na:]


def _block_diag(w):
    h, a, b = w.shape
    eye = jnp.eye(h, dtype=w.dtype)
    return (w[:, :, None, :] * eye[:, None, :, None]).reshape(h * a, h * b)


def _block_diag_extract(m, h):
    a, b = m.shape[0] // h, m.shape[1] // h
    return jnp.stack([m[i * a:(i + 1) * a, i * b:(i + 1) * b] for i in range(h)], axis=0)


def _block_diag_take(m, h):
    a, b = m.shape[0] // h, m.shape[1] // h
    eye = jnp.eye(h, dtype=m.dtype)
    return (m.reshape(h, a, h, b) * eye[:, None, :, None]).sum(axis=2)


def _ffn_interleave(w):
    lead = w.shape[:-1]
    nb = D_FF // FFN_CB
    return jnp.swapaxes(w.reshape(*lead, 2, nb, FFN_CB), -3, -2).reshape(*lead, 2 * D_FF)


def _ffn_deinterleave(w):
    lead = w.shape[:-1]
    nb = D_FF // FFN_CB
    return jnp.swapaxes(w.reshape(*lead, nb, 2, FFN_CB), -3, -2).reshape(*lead, 2 * D_FF)


def _gather_full(gathered, axis):
    shape = list(gathered.shape[1:])
    shape[axis] *= N_DEV
    return jnp.moveaxis(gathered, 0, axis).reshape(shape)


def _scatter_blocks(full, axis):
    shape = list(full.shape)
    shape[axis:axis + 1] = [N_DEV, shape[axis] // N_DEV]
    return jnp.moveaxis(full.reshape(shape), axis, 0)


def _pad_to(flat, mult):
    pad = (-flat.shape[-1]) % mult
    if pad:
        flat = jnp.concatenate([flat, jnp.zeros(flat.shape[:-1] + (pad,), flat.dtype)], axis=-1)
    return flat


def _layer_fwd(h_in, h_in_t, w, cos, sin, l, dep, get_ffn, target=None):
    tag = "l%d_" % l
    proj, qkv, h_t = _proj_rope(h_in, w['w_in'], cos, sin, tag + "proj_rope", dep=dep,
                                transposed=h_in_t is None)
    h_in_t = h_t if h_in_t is None else h_in_t
    outs, lses = [], []
    for d, qv in zip(DILATIONS, qkv):
        o, ls = _attn_fwd(qv, d, tag + "attn_d%d" % d)
        outs.append(o)
        lses.append(ls)
    lru, *lru_saved = _lru_fwd(proj, w['lru_conv_w'], w['lru_conv_b'], w['lru_wr'], w['lru_br'], w['lru_wi'],
                               w['lru_bi'], w['lru_lambda'], tag + "lru")
    x_re, x_im, y_acc = _s5_scan_fwd(proj, w['s5_bb_re'], w['s5_bb_im'], w['s5_lam_re'], w['s5_lam_im'],
                                     w['s5_cc_re'], w['s5_cc_im'], tag + "s5_scan")
    s5, y_pre = _s5_out_fwd(proj, y_acc, w['s5_d'], w['s5_w_glu'], w['s5_b_glu'], tag + "s5_out")
    w_out = get_ffn(l, s5, 'out')
    if w_out is not None:
        w['w_out'] = w_out
    mixed_t, r1, h1, h1_t, attn_o, attn_lse = _mix_fwd(outs, lses, lru, s5, w['mix_norm_g'], h_in, w['w_out'],
                                                       w['ln1_g'], w['ln1_b'], tag + "mix_out_ln1")
    w['w_up_g'], w['w_down'], ffn_dep = get_ffn(l, h1, 'ffn')
    up, y_conv, act, act_t = _ffn_up_act(h1, w['w_up_g'], w['ffn_conv_w'], w['ffn_conv_b'], tag + "up_act",
                                         dep=ffn_dep)
    r2, out_a, out_b = _proj_ln(act, w['w_down'], h1, w['ln2_g'], w['ln2_b'], tag + "down_ln2", target=target)
    saved = dict(h_in_t=h_in_t, proj=proj, qkv=qkv, lru=lru, lru_saved=lru_saved, x_re=x_re, x_im=x_im,
                 y_pre=y_pre, s5=s5, mixed_t=mixed_t, attn_o=attn_o, attn_lse=attn_lse, r1=r1, h1_t=h1_t, up=up,
                 act_t=act_t, r2=r2, y_conv=y_conv)
    return out_a, out_b, saved


def _layer_bwd_ffn(dh2, sv, w, l, dep=None):
    tag = "l%d_" % l
    g = {}
    dr2, g['ln2_g'], g['ln2_b'] = _ln_bwd(sv['r2'], dh2, w['ln2_g'], tag + "ln2_bwd", dep=dep)
    g['w_down'] = _mm_dw(sv['act_t'], dr2, 1024, D_MODEL, 1024, tag + "down_dw", _grad_dtype(l))
    dup, dh1, dcw_parts, dcb_parts = _ffn_bwd(sv['up'], sv['y_conv'], dr2, w['w_down'], w['w_up_g'],
                                              w['ffn_conv_w'], tag + "ffn_bwd")
    g['ffn_conv_w'] = dcw_parts.sum(axis=0)
    g['ffn_conv_b'] = dcb_parts.sum(axis=0)
    g['w_up_g'] = _mm_up_dw(sv['h1_t'], dup, tag + "up_dw", _grad_dtype(l))
    return dh1, g


def _layer_bwd_mix(dh1, sv, w, cos, sin, l, dep, g_ffn, after_out_grad, after_small_grads, after_in_grad):
    tag = "l%d_" % l
    g = {}
    dr1, g['ln1_g'], g['ln1_b'], d_o, dlru, ds5, g['mix_norm_g'] = _mix_bwd(
        sv['r1'], dh1, w['ln1_g'], w['w_out'], sv['attn_o'][0], sv['lru'], sv['s5'], w['mix_norm_g'],
        tag + "ln1_mix_bwd", dep=dep)
    g['w_out'] = _mm_dw(sv['mixed_t'], dr1, 1024, D_MODEL, 1024, tag + "out_dw", _grad_dtype(l))
    dy, dud, g['s5_d'], g['s5_w_glu'], g['s5_b_glu'] = _s5_out_bwd(
        sv['proj'], sv['y_pre'], ds5, w['s5_d'], w['s5_w_glu'], w['s5_b_glu'], tag + "s5_out_bwd",
        dep=after_out_grad(l, g['w_out']))
    du, g['s5_lam_re'], g['s5_lam_im'], g['s5_bb_re'], g['s5_bb_im'], g['s5_cc_re'], g['s5_cc_im'] = \
        _s5_scan_bwd(sv['proj'], dy, dud, sv['x_re'], sv['x_im'], w['s5_bb_re'], w['s5_bb_im'],
                     w['s5_lam_re'], w['s5_lam_im'], w['s5_cc_re'], w['s5_cc_im'], tag + "s5_scan_bwd")
    (dxr, dgate, g['lru_conv_w'], g['lru_conv_b'], g['lru_wr'], g['lru_br'], g['lru_wi'], g['lru_bi'],
     g['lru_lambda']) = _lru_bwd(sv['proj'], dlru, *sv['lru_saved'], w['lru_conv_w'], w['lru_conv_b'], w['lru_wr'],
                                 w['lru_br'], w['lru_wi'], w['lru_bi'], w['lru_lambda'], tag + "lru_bwd")
    token = after_small_grads(l, _finish_layer_grads({**g_ffn, **g}, w, l))
    dqkv = [_attn_bwd(sv['qkv'][b], sv['attn_o'][b], d_o[b], sv['attn_lse'][b], d, tag + "attn_bwd_d%d" % d,
                      dep=token if b == 0 else None)
            for b, d in enumerate(DILATIONS)]
    dproj = _dproj_assemble(dqkv, dxr, dgate, du, cos, sin, tag + "dproj")
    g_in = _mm_dw(sv['h_in_t'], dproj, 1024, D_IN, 1024, tag + "in_dw", _grad_dtype(l))
    return _mm_nt(dproj, w['w_in'], 512, D_MODEL, tag + "in_dx", add=dr1, add_scale=ALPHA,
                  dep=after_in_grad(l, g_in))


def _s5_rep(a):
    return jnp.repeat(a, S5_C, axis=0)


def _prepare_layer(p, l):
    w = {}
    for n in ('w_in', 'w_out', 's5_w_glu'):
        if n in p:
            w[n] = p[n].astype(BF16)
    w['ffn_conv_w'] = _ffn_interleave(p['ffn_conv_w'])
    w['ffn_conv_b'] = _ffn_interleave(p['ffn_conv_b'])[None, :]
    w['lru_conv_w'] = p['lru_conv_w']
    for n in ('lru_conv_b', 'lru_br', 'lru_bi', 'lru_lambda', 's5_b_glu', 'mix_norm_g',
              'ln1_g', 'ln1_b', 'ln2_g', 'ln2_b'):
        w[n] = p[n][None, :]
    w['lru_wr'] = _block_diag(p['lru_wr']).astype(BF16)
    w['lru_wi'] = _block_diag(p['lru_wi']).astype(BF16)
    w['s5_d'] = p['s5_d'].reshape(1, S5_W)
    disc_in = (_s5_rep(p['s5_a_re']), _s5_rep(p['s5_a_im']),
               _s5_rep(jnp.broadcast_to(p['s5_log_step'][:, None], (S5_G, S5_P))),
               jnp.swapaxes(p['s5_b_re'], 1, 2).reshape(S5_W, S5_P),
               jnp.swapaxes(p['s5_b_im'], 1, 2).reshape(S5_W, S5_P))
    ab_re, ab_im, bb_re, bb_im = _s5_disc_fwd(*disc_in, "l%d_s5_disc" % l)
    w['s5_disc_in'] = disc_in
    w['s5_lam_re'] = ab_re.reshape(S5_G, S5_C, S5_P)[:, 0, :].reshape(1, S5_STATES)
    w['s5_lam_im'] = ab_im.reshape(S5_G, S5_C, S5_P)[:, 0, :].reshape(1, S5_STATES)
    w['s5_bb_re'] = _block_diag(bb_re.reshape(S5_G, S5_C, S5_P)).astype(BF16)
    w['s5_bb_im'] = _block_diag(bb_im.reshape(S5_G, S5_C, S5_P)).astype(BF16)
    w['s5_cc_re'] = _block_diag(jnp.swapaxes(p['s5_c_re'], 1, 2)).astype(BF16)
    w['s5_cc_im'] = _block_diag(jnp.swapaxes(p['s5_c_im'], 1, 2)).astype(BF16)
    return w


def _finish_layer_grads(g, w, l):
    out = {}
    for n in ('s5_w_glu', 'lru_conv_w'):
        out[n] = g[n]
    out['ffn_conv_w'] = _ffn_deinterleave(g['ffn_conv_w'])
    out['ffn_conv_b'] = _ffn_deinterleave(g['ffn_conv_b'])[0]
    for n in ('lru_conv_b', 'lru_br', 'lru_bi', 'lru_lambda', 's5_b_glu', 'mix_norm_g',
              'ln1_g', 'ln1_b', 'ln2_g', 'ln2_b'):
        out[n] = g[n][0]
    out['lru_wr'] = _block_diag_extract(g['lru_wr'], LRU_W // HEAD)
    out['lru_wi'] = _block_diag_extract(g['lru_wi'], LRU_W // HEAD)
    out['s5_d'] = g['s5_d'].reshape(S5_G, S5_C)
    out['s5_c_re'] = jnp.swapaxes(_block_diag_take(g['s5_cc_re'], S5_G), 1, 2)
    out['s5_c_im'] = jnp.swapaxes(_block_diag_take(g['s5_cc_im'], S5_G), 1, 2)
    rep = lambda v: _s5_rep(v.reshape(S5_G, S5_P)) * (1.0 / S5_C)
    cts = (rep(g['s5_lam_re']), rep(g['s5_lam_im']),
           _block_diag_take(g['s5_bb_re'], S5_G).reshape(S5_W, S5_P),
           _block_diag_take(g['s5_bb_im'], S5_G).reshape(S5_W, S5_P))
    da_re, da_im, dls, dbt_re, dbt_im = _s5_disc_bwd(*w['s5_disc_in'], cts, "l%d_s5_disc_bwd" % l)
    out['s5_a_re'] = da_re.reshape(S5_G, S5_C, S5_P).sum(axis=1)
    out['s5_a_im'] = da_im.reshape(S5_G, S5_C, S5_P).sum(axis=1)
    out['s5_log_step'] = dls.reshape(S5_G, S5_C * S5_P).sum(axis=1)
    out['s5_b_re'] = jnp.swapaxes(dbt_re.reshape(S5_G, S5_C, S5_P), 1, 2)
    out['s5_b_im'] = jnp.swapaxes(dbt_im.reshape(S5_G, S5_C, S5_P), 1, 2)
    return out


def _run_step(x, target, get_layer, get_ffn, on_loss, after_ffn_grads, after_out_grad, after_small_grads,
              after_in_grad):
    cos, sin = _rope_tables(x.shape[0])
    h, h_t = x, None
    ws, saved = [], []
    for l in range(DEPTH):
        p, dep = get_layer(l, h)
        ws.append(_prepare_layer(p, l))
        h, h_t, sv = _layer_fwd(h, h_t, ws[l], cos, sin, l, dep, get_ffn, target if l == DEPTH - 1 else None)
        saved.append(sv)
    dh, loss_vec = h, h_t
    on_loss(loss_vec)
    dep = None
    for l in reversed(range(DEPTH)):
        dh1, g = _layer_bwd_ffn(dh, saved[l], ws[l], l, dep)
        dep = after_ffn_grads(l, g)
        dh = _layer_bwd_mix(dh1, saved[l], ws[l], cos, sin, l, dep, g, after_out_grad, after_small_grads,
                            after_in_grad)
        dep = None
    return loss_vec, dh


def _local_step(x, target, layers):
    grads = [{} for _ in range(DEPTH)]

    def ffn(l, after, part):
        if part == 'out':
            return None
        return layers[l]['w_up_g'].astype(BF16), layers[l]['w_down'].astype(BF16), None

    def keep_ffn(l, g):
        grads[l].update(w_up_g=g['w_up_g'], w_down=g['w_down'])

    def keep_small(l, g):
        grads[l].update(g)

    loss, dx = _run_step(x, target, lambda l, h: (layers[l], None), ffn, lambda row: None, keep_ffn,
                         lambda l, g: grads[l].update(w_out=g), keep_small, lambda l, g: grads[l].update(w_in=g))
    return loss[0, 0], dx, grads


def kernel(x, w_in, lru_conv_w, lru_conv_b, lru_wr, lru_br, lru_wi, lru_bi, lru_lambda, s5_a_re, s5_a_im, s5_b_re, s5_b_im, s5_c_re, s5_c_im, s5_d, s5_log_step, s5_w_glu, s5_b_glu, mix_norm_g, w_out, ln1_g, ln1_b, w_up, ffn_conv_w, ffn_conv_b, w_down, ln2_g, ln2_b, loss_target, m_w_in, m_lru_conv_w, m_lru_conv_b, m_lru_wr, m_lru_br, m_lru_wi, m_lru_bi, m_lru_lambda, m_s5_a_re, m_s5_a_im, m_s5_b_re, m_s5_b_im, m_s5_c_re, m_s5_c_im, m_s5_d, m_s5_log_step, m_s5_w_glu, m_s5_b_glu, m_mix_norm_g, m_w_out, m_ln1_g, m_ln1_b, m_w_up, m_ffn_conv_w, m_ffn_conv_b, m_w_down, m_ln2_g, m_ln2_b, v_w_in, v_lru_conv_w, v_lru_conv_b, v_lru_wr, v_lru_br, v_lru_wi, v_lru_bi, v_lru_lambda, v_s5_a_re, v_s5_a_im, v_s5_b_re, v_s5_b_im, v_s5_c_re, v_s5_c_im, v_s5_d, v_s5_log_step, v_s5_w_glu, v_s5_b_glu, v_mix_norm_g, v_w_out, v_ln1_g, v_ln1_b, v_w_up, v_ffn_conv_w, v_ffn_conv_b, v_w_down, v_ln2_g, v_ln2_b):
    args = locals()
    wl = {n: args[n] for n in WEIGHTS}
    ml = {n: args['m_' + n] for n in WEIGHTS}
    vl = {n: args['v_' + n] for n in WEIGHTS}

    small_sizes = [int(wl[n].size) for n in SMALL_SHARDED]
    small_flat = _pad_to(jnp.concatenate([wl[n].reshape(-1) for n in SMALL_SHARDED]), 8 * 1024)
    small_all, w_in0 = _all_gather([small_flat.reshape(-1, 1024), wl['w_in'][0].astype(BF16)], "gather_first")
    small_all = small_all.reshape(N_DEV, -1)
    small_full, off = {}, 0
    for n, sz in zip(SMALL_SHARDED, small_sizes):
        small_full[n] = _gather_full(small_all[:, off:off + sz].reshape((N_DEV,) + wl[n].shape), SHARD_AXIS[n])
        off += sz
    def mixer_params(l, g_in, g_out):
        p = {n: wl[n][l] for n in REPLICATED}
        p.update({n: small_full[n][l] for n in SMALL_SHARDED})
        p['w_in'] = _gather_full(g_in, 1)
        if g_out is not None:
            p['w_out'] = g_out.reshape(D_MODEL, D_MODEL)
        return p

    mix_names, ffn_names = ('w_in', 'w_out'), ('w_up', 'w_down')
    shards = lambda names, l: [wl[n][l].astype(BF16) for n in names]
    gathers = {}
    gathers[0, 'out'], token = _exchange_start(shards(('w_out',), 0), True, "gather_out_l0_start", dep=w_in0)
    gathers[0, 'ffn'], rest0_token = _exchange_start(shards(ffn_names, 0), True, "gather_ffn_l0_start", dep=token)

    def get_layer(l, h):
        if l == 0:
            return mixer_params(0, w_in0, None), rest0_token
        return mixer_params(1, *_exchange_wait(gathers[1, 'mix'], True, h, "gather_mix_l1_wait")), None

    def get_ffn(l, after, part):
        if part == 'out':
            if l > 0:
                return None
            g_out, = _exchange_wait(gathers[0, 'out'], True, after, "gather_out_l0_wait")
            return g_out.reshape(D_MODEL, D_MODEL)
        g_up, g_down = _exchange_wait(gathers[l, 'ffn'], True, after, "gather_ffn_l%d_wait" % l)
        token = None
        if l == 0:
            gathers[1, 'mix'], token = _exchange_start(shards(mix_names, 1), True, "gather_mix_l1_start", dep=g_up)
            gathers[1, 'ffn'], token = _exchange_start(shards(ffn_names, 1), True, "gather_ffn_l1_start", dep=token)
        return g_up, g_down.reshape(D_FF, D_MODEL), token

    scatters = {}

    def after_ffn_grads(l, g):
        send = [g['w_up_g'], g['w_down'].reshape(N_DEV, D_FF // N_DEV, D_MODEL)]
        scatters[l, 'ffn'], token = _exchange_start(send, False, "scatter_ffn_l%d_start" % l)
        return token

    def after_out_grad(l, g_out):
        send = [g_out.reshape(N_DEV, D_MODEL // N_DEV, D_MODEL)]
        scatters[l, 'out'], token = _exchange_start(send, False, "scatter_out_l%d_start" % l)
        return token

    def after_in_grad(l, g_in):
        scatters[l, 'in'], token = _exchange_start([_scatter_blocks(g_in, 1)], False, "scatter_in_l%d_start" % l)
        return token

    def after_small_grads(l, g):
        rep = [g[n][None] for n in REPLICATED]
        if l == DEPTH - 1:
            rep.append(loss_rows[0][None])
        shd = [_scatter_blocks(g[n], SHARD_AXIS[n] - 1)[:, None] for n in SMALL_SHARDED]
        scatters[l, 'rep'], token = _exchange_start(rep, True, "gather_rep_grads_l%d_start" % l)
        scatters[l, 'small'], token = _exchange_start(shd, False, "scatter_small_l%d_start" % l, dep=token)
        return token

    loss_rows = []
    _, grad_x = _run_step(x[0], loss_target[0], get_layer, get_ffn, loss_rows.append, after_ffn_grads,
                          after_out_grad, after_small_grads, after_in_grad)

    results = {}
    big_prev = {n: None for n in BIG}

    def finish_big(l, part, names, after):
        landed = _exchange_wait(scatters[l, part], False, after, "scatter_%s_l%d_wait" % (part, l))
        for n, ld in zip(names, landed):
            big_prev[n] = _adamw_sum(ld, wl[n], ml[n], vl[n], l, big_prev[n], "adamw_%s_l%d" % (n, l))

    for l, part, names in ((1, 'ffn', ffn_names), (1, 'out', ('w_out',)), (1, 'in', ('w_in',)),
                           (0, 'ffn', ffn_names), (0, 'out', ('w_out',))):
        finish_big(l, part, names, grad_x)

    kinds = ('grad', 'delta', 'm', 'v')
    landed = []
    for l in range(DEPTH):
        rep = list(_exchange_wait(scatters[l, 'rep'], True, grad_x, "gather_rep_grads_l%d_wait" % l))
        if l == DEPTH - 1:
            loss = jnp.sum(rep.pop()[:, 0, 0, 0])
        shd = list(_exchange_wait(scatters[l, 'small'], False, grad_x, "scatter_small_l%d_wait" % l))
        landed.append(dict(zip(REPLICATED + SMALL_SHARDED, rep + shd)))
    matrices = ['lru_wr', 'lru_wi', 's5_a_re', 's5_a_im', 's5_c_re', 's5_c_im', 's5_d']
    widest = ['s5_b_re', 's5_b_im']
    vectors = [n for n in REPLICATED + SMALL_SHARDED if n not in matrices + widest]
    last = None
    for tag, names in (("vectors", vectors), ("matrices", matrices), ("s5_b", widest)):
        res = _adamw_many([[landed[l][n] for n in names] for l in range(DEPTH)], [wl[n] for n in names],
                          [ml[n] for n in names], [vl[n] for n in names], "adamw_" + tag)
        for kind, arrs in zip(kinds, res):
            for n, a in zip(names, arrs):
                results[kind, n] = a
        last = res[0][0]
    finish_big(0, 'in', ('w_in',), last)
    for n in BIG:
        results['grad', n], results['delta', n], results['m', n], results['v', n] = big_prev[n]

    out = [loss, grad_x[None]]
    for kind in kinds:
        out.extend(results[kind, n] for n in WEIGHTS)
    return tuple(out)
```

```python
import math

import jax
import jax.numpy as jnp
from jax import lax
from jax.experimental import pallas as pl
from jax.experimental.pallas import tpu as pltpu

F32 = jnp.float32
BF16 = jnp.bfloat16

N_DEV = 8
DEPTH = 2
D_MODEL = 1024
ATTN_W = 384
LRU_W = 384
S5_W = 256
D_IN = 2176
D_FF = 3072
HEAD = 64
ATTN_BLK = 128
ATTN_TILE = 1024
DILATIONS = (1, 4, 16)
S5_G = 16
S5_P = 64
S5_C = 16
S5_STATES = S5_G * S5_P
LRU_C = 8.0
LRU_CONV = 4
FFN_CONV = 3
ROPE_THETA = 10000.0
ALPHA = (2 * DEPTH) ** 0.25
LN_EPS = 1e-5
RMS_EPS = 1e-6
ADAM_LR, ADAM_B1, ADAM_B2, ADAM_EPS, ADAM_WD, ADAM_STEP = 0.001, 0.9, 0.999, 1e-8, 0.01, 10

LANE = 128
SCAN_T = 1024
S5_BLK = 256
FFN_CB = 2 * D_FF // N_DEV
VMEM_LIMIT = 56 * 1024 * 1024

WEIGHTS = ['w_in', 'lru_conv_w', 'lru_conv_b', 'lru_wr', 'lru_br', 'lru_wi', 'lru_bi', 'lru_lambda',
           's5_a_re', 's5_a_im', 's5_b_re', 's5_b_im', 's5_c_re', 's5_c_im', 's5_d', 's5_log_step',
           's5_w_glu', 's5_b_glu', 'mix_norm_g', 'w_out', 'ln1_g', 'ln1_b', 'w_up', 'ffn_conv_w',
           'ffn_conv_b', 'w_down', 'ln2_g', 'ln2_b']
SHARD_AXIS = {'w_in': 2, 'lru_conv_w': 2, 's5_w_glu': 1, 'w_out': 1, 'w_up': 2, 'ffn_conv_w': 2, 'w_down': 1}
BIG = ['w_in', 'w_out', 'w_up', 'w_down']
SMALL_SHARDED = ['lru_conv_w', 'ffn_conv_w', 's5_w_glu']
REPLICATED = [n for n in WEIGHTS if n not in SHARD_AXIS]


def _cparams(sem=None):
    return pltpu.CompilerParams(dimension_semantics=sem, vmem_limit_bytes=VMEM_LIMIT)


def _grad_dtype(l):
    return BF16 if l == 0 else F32


def _ffn_dev(jb):
    return jb // 2 + (N_DEV // 2) * (jb % 2)


def _gelu(x):
    c = math.sqrt(2.0 / math.pi)
    t = jnp.tanh(c * (x + 0.044715 * (x * x * x)))
    return 0.5 * x * (1.0 + t)


def _gelu_grad(x):
    c = math.sqrt(2.0 / math.pi)
    x2 = x * x
    t = jnp.tanh(c * (x + 0.044715 * (x2 * x)))
    return 0.5 * (1.0 + t) + 0.5 * x * (1.0 - t * t) * (c * (1.0 + 3.0 * 0.044715 * x2))


def _sigmoid(x):
    return 1.0 / (1.0 + jnp.exp(-x))


def _log1p(x):
    u = 1.0 + x
    d = u - 1.0
    return jnp.where(d == 0.0, x, jnp.log(u) * (x / jnp.where(d == 0.0, 1.0, d)))


def _softplus(x):
    return jnp.maximum(x, 0.0) + _log1p(jnp.exp(-jnp.abs(x)))


def _expm1(x):
    return jnp.tanh(0.5 * x) * (jnp.exp(x) + 1.0)


def _dot(a, b):
    return jnp.dot(a.astype(BF16), b.astype(BF16), preferred_element_type=F32)


def _dot_nt(a, b):
    return lax.dot_general(a.astype(BF16), b.astype(BF16), (((1,), (1,)), ((), ())),
                           preferred_element_type=F32)


def _dot_tn(a, b):
    return lax.dot_general(a.astype(BF16), b.astype(BF16), (((0,), (0,)), ((), ())),
                           preferred_element_type=F32)


def _rows(shape):
    return lax.broadcasted_iota(jnp.int32, shape, 0)


def _shift_down_prev(x, s, prev8):
    if s == 0:
        return x
    t, l = x.shape
    r = pltpu.roll(x, s, axis=0)
    pr = pltpu.roll(prev8, s, axis=0)
    pad = jnp.concatenate([pr, jnp.zeros((t - 8, l), x.dtype)], axis=0)
    return jnp.where(_rows(x.shape) < s, pad, r)


def _shift_up_next(x, s, next8):
    if s == 0:
        return x
    t, l = x.shape
    r = pltpu.roll(x, t - s, axis=0)
    nx = pltpu.roll(next8, 8 - s, axis=0)
    pad = jnp.concatenate([jnp.zeros((t - 8, l), x.dtype), nx], axis=0)
    return jnp.where(_rows(x.shape) >= t - s, pad, r)


SUB = 8


def _tile_shift(x, s, fill, reverse):
    t = x.shape[0]
    pos = _rows(x.shape) & (SUB - 1)
    if reverse:
        return jnp.where(pos < SUB - s, pltpu.roll(x, t - s, axis=0), fill)
    return jnp.where(pos >= s, pltpu.roll(x, s, axis=0), fill)


def _scan_chunk(a, x, carry, reverse=False):
    s = 1
    while s < SUB:
        x = x + a * _tile_shift(x, s, 0.0, reverse)
        a = a * _tile_shift(a, s, 1.0, reverse)
        s *= 2
    nv = x.shape[0] // SUB
    out = [None] * nv
    for v in (reversed(range(nv)) if reverse else range(nv)):
        rows = slice(v * SUB, (v + 1) * SUB)
        out[v] = x[rows, :] + a[rows, :] * carry
        carry = out[v][0:1, :] if reverse else out[v][SUB - 1:SUB, :]
    return jnp.concatenate(out, axis=0)


def _cmul(ar, ai, br, bi):
    return ar * br - ai * bi, ar * bi + ai * br


def _cscan_consts(lr, li, reverse):
    pows = [(lr, li)]
    for _ in range(2):
        pows.append(_cmul(*pows[-1], *pows[-1]))
    rows = [(lr, li)]
    for _ in range(SUB - 1):
        rows.append(_cmul(*rows[-1], lr, li))
    if reverse:
        rows = rows[::-1]
    return pows, (jnp.concatenate([r for r, _ in rows], axis=0), jnp.concatenate([i for _, i in rows], axis=0))


def _cscan_chunk(xr, xi, consts, carry, reverse=False):
    pows, (p8r, p8i) = consts
    s = 1
    for pr, pi in pows:
        sr = _tile_shift(xr, s, 0.0, reverse)
        si = _tile_shift(xi, s, 0.0, reverse)
        xr, xi = xr + pr * sr - pi * si, xi + pr * si + pi * sr
        s *= 2
    nv = xr.shape[0] // SUB
    out_r, out_i = [None] * nv, [None] * nv
    cr, ci = carry
    for v in (reversed(range(nv)) if reverse else range(nv)):
        rows = slice(v * SUB, (v + 1) * SUB)
        out_r[v] = xr[rows, :] + p8r * cr - p8i * ci
        out_i[v] = xi[rows, :] + p8r * ci + p8i * cr
        edge = slice(0, 1) if reverse else slice(SUB - 1, SUB)
        cr, ci = out_r[v][edge, :], out_i[v][edge, :]
    return jnp.concatenate(out_r, axis=0), jnp.concatenate(out_i, axis=0)


def _dep_args(dep):
    return ([], []) if dep is None else ([pl.BlockSpec(memory_space=pl.ANY)], [dep])


def _mm_nt(a, w, tm, tn, name, add=None, add_scale=1.0, dep=None):
    m, k = a.shape
    n = w.shape[0]

    def body(a_ref, w_ref, *rest):
        o_ref = rest[-1]
        if add is None:
            o_ref[...] = _dot_nt(a_ref[...], w_ref[...])
        else:
            o_ref[...] = _dot_nt(a_ref[...], w_ref[...]) + add_scale * rest[0][...]

    in_specs = [pl.BlockSpec((tm, k), lambda j, i: (i, 0)), pl.BlockSpec((tn, k), lambda j, i: (j, 0))]
    args = [a, w]
    if add is not None:
        in_specs.append(pl.BlockSpec((tm, tn), lambda j, i: (i, j)))
        args.append(add)
    dep_specs, dep_ops = _dep_args(dep)
    return pl.pallas_call(
        body, out_shape=jax.ShapeDtypeStruct((m, n), F32), grid=(n // tn, m // tm),
        in_specs=in_specs + dep_specs, out_specs=pl.BlockSpec((tm, tn), lambda j, i: (i, j)), name=name,
        compiler_params=_cparams(("parallel", "parallel")))(*args, *dep_ops)


def _mm_dw(at, b, tm, tn, ts, name, out_dtype=F32):
    m, s = at.shape
    n = b.shape[1]
    nk = s // ts

    def body(a_ref, b_ref, o_ref, acc):
        @pl.when(pl.program_id(2) == 0)
        def _():
            acc[...] = jnp.zeros_like(acc)
        acc[...] += _dot(a_ref[...], b_ref[...])

        @pl.when(pl.program_id(2) == nk - 1)
        def _():
            o_ref[...] = acc[...].astype(out_dtype)

    return pl.pallas_call(
        body, out_shape=jax.ShapeDtypeStruct((m, n), out_dtype), grid=(m // tm, n // tn, nk),
        in_specs=[pl.BlockSpec((tm, ts), lambda i, j, k: (i, k)), pl.BlockSpec((ts, tn), lambda i, j, k: (k, j))],
        out_specs=pl.BlockSpec((tm, tn), lambda i, j, k: (i, j)),
        scratch_shapes=[pltpu.VMEM((tm, tn), F32)], name=name,
        compiler_params=_cparams(("parallel", "parallel", "arbitrary")))(at, b)


def _mm_up_dw(ht, dup, name, out_dtype=F32):
    d, s = ht.shape

    def body(a_ref, b_ref, o_ref):
        o_ref[...] = _dot(a_ref[...], b_ref[...]).astype(out_dtype)

    return pl.pallas_call(
        body, out_shape=jax.ShapeDtypeStruct((N_DEV, d, FFN_CB), out_dtype), grid=(N_DEV,),
        in_specs=[pl.BlockSpec((d, s), lambda j: (0, 0)), pl.BlockSpec((s, FFN_CB), lambda j: (0, j))],
        out_specs=pl.BlockSpec((None, d, FFN_CB), lambda j: (_ffn_dev(j), 0, 0)), name=name,
        compiler_params=_cparams(("parallel",)))(ht, dup)


def _layer_norm(r, g, b):
    mu = jnp.mean(r, axis=-1, keepdims=True)
    xc = r - mu
    var = jnp.mean(xc * xc, axis=-1, keepdims=True)
    return xc * lax.rsqrt(var + LN_EPS) * g + b


def _proj_ln(a, w, resid, g, bias, name, transposed=True, target=None):
    s, k = a.shape
    d = w.shape[1]
    tm = 512

    def body(a_ref, w_ref, x_ref, g_ref, bias_ref, *rest):
        r = ALPHA * x_ref[...] + _dot(a_ref[...], w_ref[...])
        h = _layer_norm(r, g_ref[...], bias_ref[...])
        if target is None:
            r_ref, h_ref = rest[0], rest[1]
            h_ref[...] = h
            if transposed:
                rest[2][...] = h.T.astype(BF16)
        else:
            t_ref, r_ref, dy_ref, l_ref = rest

            @pl.when(pl.program_id(0) == 0)
            def _():
                l_ref[...] = jnp.zeros_like(l_ref)
            e = h - t_ref[...]
            dy_ref[...] = e * (1.0 / d)
            part = 0.5 * jnp.sum(jnp.mean(e * e, axis=-1, keepdims=True), axis=0, keepdims=True)
            l_ref[...] += jnp.broadcast_to(part, l_ref.shape)
        r_ref[...] = r

    row = pl.BlockSpec((tm, d), lambda i: (i, 0))
    vec = pl.BlockSpec((1, d), lambda i: (0, 0))
    in_specs = [pl.BlockSpec((tm, k), lambda i: (i, 0)), pl.BlockSpec((k, d), lambda i: (0, 0)), row, vec, vec]
    args = [a, w, resid, g, bias]
    shapes = [jax.ShapeDtypeStruct((s, d), F32), jax.ShapeDtypeStruct((s, d), F32)]
    specs = [row, row]
    if target is not None:
        in_specs.append(row)
        args.append(target)
        shapes.append(jax.ShapeDtypeStruct((1, LANE), F32))
        specs.append(pl.BlockSpec((1, LANE), lambda i: (0, 0)))
    elif transposed:
        shapes.append(jax.ShapeDtypeStruct((d, s), BF16))
        specs.append(pl.BlockSpec((d, tm), lambda i: (0, i)))
    return pl.pallas_call(
        body, out_shape=tuple(shapes), grid=(s // tm,), in_specs=in_specs, out_specs=tuple(specs), name=name,
        compiler_params=_cparams(("arbitrary",) if target is not None else ("parallel",)))(*args)


def _layer_norm_bwd(r, dh, g):
    mu = jnp.mean(r, axis=-1, keepdims=True)
    xc = r - mu
    var = jnp.mean(xc * xc, axis=-1, keepdims=True)
    rstd = lax.rsqrt(var + LN_EPS)
    xh = xc * rstd
    dxh = dh * g
    m1 = jnp.mean(dxh, axis=-1, keepdims=True)
    m2 = jnp.mean(dxh * xh, axis=-1, keepdims=True)
    return (rstd * (dxh - m1 - xh * m2), jnp.sum(dh * xh, axis=0, keepdims=True),
            jnp.sum(dh, axis=0, keepdims=True))


def _ln_bwd(r, dh, g, name, dep=None):
    s, d = r.shape
    tm = 512

    def body(r_ref, dh_ref, g_ref, *rest):
        dr_ref, dg_ref, db_ref = rest[-3:]

        @pl.when(pl.program_id(0) == 0)
        def _():
            dg_ref[...] = jnp.zeros_like(dg_ref)
            db_ref[...] = jnp.zeros_like(db_ref)
        dr_ref[...], dg_rows, db_rows = _layer_norm_bwd(r_ref[...], dh_ref[...], g_ref[...])
        dg_ref[...] += dg_rows
        db_ref[...] += db_rows

    row = pl.BlockSpec((tm, d), lambda i: (i, 0))
    vec = pl.BlockSpec((1, d), lambda i: (0, 0))
    dep_specs, dep_ops = _dep_args(dep)
    return pl.pallas_call(
        body, out_shape=(jax.ShapeDtypeStruct((s, d), F32), jax.ShapeDtypeStruct((1, d), F32),
                         jax.ShapeDtypeStruct((1, d), F32)),
        grid=(s // tm,), in_specs=[row, row, vec] + dep_specs, out_specs=(row, vec, vec), name=name,
        compiler_params=_cparams(("arbitrary",)))(r, dh, g, *dep_ops)


def _rope_tables(s):
    half = HEAD // 2
    pos = jnp.arange(s, dtype=F32)
    inv = ROPE_THETA ** (-jnp.arange(half, dtype=F32) * 2.0 / HEAD)
    ang = pos[:, None] * inv[None, :]
    cos, sin = jnp.cos(ang), jnp.sin(ang)
    cos = jnp.concatenate([cos, cos, cos, cos], axis=1)
    sin = jnp.concatenate([-sin, sin, -sin, sin], axis=1)
    return cos, sin


def _rotate(x, cos, sin):
    lane = lax.broadcasted_iota(jnp.int32, x.shape, 1)
    partner = jnp.where((lane % HEAD) < HEAD // 2, pltpu.roll(x, LANE - HEAD // 2, axis=1),
                        pltpu.roll(x, HEAD // 2, axis=1))
    return x * cos + partner * sin


def _class_rows(c, d, tm):
    return pl.ds(c, tm // d, stride=d) if d > 1 else pl.ds(0, tm)


def _dilated_spec(tm, d, w):
    return pl.BlockSpec((tm // d, d * w), lambda i: (i, 0))


def _token_scratch(tm, w):
    return pltpu.VMEM((w // LANE, tm, LANE), F32)


def _to_tokens(src_ref, dst3, d, tm):
    nj = dst3.shape[0]
    for cls in range(d):
        for j in range(nj):
            col = (cls * nj + j) * LANE
            dst3.at[j][_class_rows(cls, d, tm), :] = src_ref[:, col:col + LANE]


def _to_dilated(src3, dst_ref, d, tm):
    nj = src3.shape[0]
    for cls in range(d):
        for j in range(nj):
            col = (cls * nj + j) * LANE
            dst_ref[:, col:col + LANE] = src3.at[j][_class_rows(cls, d, tm), :].astype(dst_ref.dtype)


def _token_value(src3):
    return jnp.concatenate([src3[j] for j in range(src3.shape[0])], axis=1)


def _proj_rope(h, w_in, cos, sin, name, dep=None, transposed=False):
    s, d_model = h.shape
    tm = 512
    w = 3 * ATTN_W
    nj = w // LANE

    def body(h_ref, w_ref, c_ref, s_ref, *rest):
        rot = rest[-1]
        if transposed:
            p_ref, o_refs, ht_ref = rest[-6], rest[-5:-2], rest[-2]
            ht_ref[...] = h_ref[...].T.astype(BF16)
        else:
            p_ref, o_refs = rest[-5], rest[-4:-1]
        y = _dot(h_ref[...], w_ref[...])
        p_ref[...] = y
        c, sn = c_ref[...], s_ref[...]
        for j in range(nj):
            x = y[:, j * LANE:(j + 1) * LANE]
            rot[j] = _rotate(x, c, sn) if j < 2 * ATTN_W // LANE else x
        for d, o_ref in zip(DILATIONS, o_refs):
            _to_dilated(rot, o_ref, d, tm)

    tab = pl.BlockSpec((tm, LANE), lambda i: (i, 0))
    dep_specs, dep_ops = _dep_args(dep)
    shapes = [jax.ShapeDtypeStruct((s, D_IN), F32), *[jax.ShapeDtypeStruct((s // d, d * w), BF16) for d in DILATIONS]]
    specs = [pl.BlockSpec((tm, D_IN), lambda i: (i, 0)), *[_dilated_spec(tm, d, w) for d in DILATIONS]]
    if transposed:
        shapes.append(jax.ShapeDtypeStruct((d_model, s), BF16))
        specs.append(pl.BlockSpec((d_model, tm), lambda i: (0, i)))
    res = pl.pallas_call(
        body, out_shape=tuple(shapes), grid=(s // tm,),
        in_specs=[pl.BlockSpec((tm, d_model), lambda i: (i, 0)), pl.BlockSpec((d_model, D_IN), lambda i: (0, 0)),
                  tab, tab] + dep_specs,
        out_specs=tuple(specs), scratch_shapes=[_token_scratch(tm, w)], name=name,
        compiler_params=_cparams(("parallel",)))(h, w_in, cos, sin, *dep_ops)
    return res[0], res[1:4], (res[4] if transposed else None)


def _dproj_assemble(dqkv_list, dxr, dgate, du, cos, sin, name):
    s = dxr.shape[0]
    tm = 512
    nq = 3 * ATTN_W // LANE

    def body(*refs):
        br = refs[:9]
        dxr_ref, dg_ref, du_ref, c_ref, s_ref, o_ref = refs[9:15]
        tok = refs[15:]
        c, sn = c_ref[...], -s_ref[...]
        for part in range(3):
            for b, d in enumerate(DILATIONS[1:], start=1):
                _to_tokens(br[3 * b + part], tok[2 * part + b - 1], d, tm)
        for j in range(nq):
            part, jj = divmod(j, ATTN_W // LANE)
            x = br[part][:, jj * LANE:(jj + 1) * LANE] + tok[2 * part][jj] + tok[2 * part + 1][jj]
            if part < 2:
                x = _rotate(x, c, sn)
            o_ref[:, j * LANE:(j + 1) * LANE] = x.astype(BF16)
        o_ref[:, 3 * ATTN_W:3 * ATTN_W + LRU_W] = dxr_ref[...].astype(BF16)
        o_ref[:, 3 * ATTN_W + LRU_W:3 * ATTN_W + 2 * LRU_W] = dg_ref[...].astype(BF16)
        o_ref[:, 3 * ATTN_W + 2 * LRU_W:] = du_ref[...].astype(BF16)

    a_spec = pl.BlockSpec((tm, ATTN_W), lambda i: (i, 0))
    tab = pl.BlockSpec((tm, LANE), lambda i: (i, 0))
    ordered = [dqkv_list[b][p] for b in range(3) for p in range(3)]
    d_specs = [_dilated_spec(tm, d, ATTN_W) for d in DILATIONS for _ in range(3)]
    return pl.pallas_call(
        body, out_shape=jax.ShapeDtypeStruct((s, D_IN), BF16), grid=(s // tm,),
        in_specs=d_specs + [a_spec, a_spec, pl.BlockSpec((tm, S5_W), lambda i: (i, 0)), tab, tab],
        out_specs=pl.BlockSpec((tm, D_IN), lambda i: (i, 0)),
        scratch_shapes=[_token_scratch(tm, ATTN_W)] * 6, name=name,
        compiler_params=_cparams(("parallel",)))(*ordered, dxr, dgate, du, cos, sin)


def _attn_tiles(s, d):
    m = s // d
    tq = min(m, ATTN_TILE)
    return m, tq, tq // ATTN_BLK


def _band_mask(qb):
    qi = lax.broadcasted_iota(jnp.int32, (ATTN_BLK, 2 * ATTN_BLK), 0)
    ki = lax.broadcasted_iota(jnp.int32, (ATTN_BLK, 2 * ATTN_BLK), 1)
    dist = qi + ATTN_BLK - ki
    return (dist >= 0) & (dist <= ATTN_BLK) & ((ki >= ATTN_BLK) | (qb > 0))


def _causal_mask():
    qi = lax.broadcasted_iota(jnp.int32, (ATTN_BLK, ATTN_BLK), 0)
    ki = lax.broadcasted_iota(jnp.int32, (ATTN_BLK, ATTN_BLK), 1)
    return qi >= ki


def _attn_fwd(qv, d, name):
    m = qv.shape[0]
    w3 = 3 * ATTN_W
    _, tq, n = _attn_tiles(m * d, d)
    single = m == tq
    scale = HEAD ** -0.5
    assert math.frexp(scale)[0] == 0.5, "the kernel scales bf16 q: exact only for a power of two"

    def body(x_ref, p_ref, o_ref, l_ref):
        b = pl.program_id(1)

        def block(i, first):
            r0 = 0 if first else pl.multiple_of(i * ATTN_BLK, ATTN_BLK)
            rows = pl.ds(r0, ATTN_BLK)
            own_only = first and single
            valid = _causal_mask() if own_only else _band_mask(b * n + i)
            if not first:
                krows = pl.ds(pl.multiple_of(i * ATTN_BLK - ATTN_BLK, ATTN_BLK), 2 * ATTN_BLK)
            low = lax.broadcasted_iota(jnp.int32, (1, LANE), 1) < HEAD
            for hp in range(ATTN_W // LANE):
                qs, ks, vs = (slice(part * ATTN_W + hp * LANE, part * ATTN_W + (hp + 1) * LANE) for part in range(3))
                q2 = x_ref[rows, qs] * scale
                if own_only:
                    k2, v2 = x_ref[0:ATTN_BLK, ks], x_ref[0:ATTN_BLK, vs]
                elif first:
                    k2 = jnp.concatenate([p_ref[:, ks], x_ref[0:ATTN_BLK, ks]], axis=0)
                    v2 = jnp.concatenate([p_ref[:, vs], x_ref[0:ATTN_BLK, vs]], axis=0)
                else:
                    k2 = x_ref[krows, ks]
                    v2 = x_ref[krows, vs]
                outs, lses = [], []
                for mask in (low, ~low):
                    q = jnp.where(mask, q2, jnp.zeros_like(q2))
                    sc = jnp.where(valid, _dot_nt(q, k2), -1e30)
                    mx = jnp.max(sc, axis=-1, keepdims=True)
                    p = jnp.exp(sc - mx)
                    l = jnp.sum(p, axis=-1, keepdims=True)
                    outs.append(_dot(p, v2) / l)
                    lses.append(mx + jnp.log(l))
                o_ref[rows, hp * LANE:(hp + 1) * LANE] = jnp.where(low, outs[0], outs[1])
                l_ref[rows, hp * LANE:(hp + 1) * LANE] = jnp.where(low, lses[0], lses[1])

        block(0, True)
        if n > 1:
            def loop(i, carry):
                block(i, False)
                return carry
            lax.fori_loop(1, n, loop, 0, unroll=2)

    shp = jax.ShapeDtypeStruct((m, d * ATTN_W), F32)
    ospec =pl.BlockSpec((tq, ATTN_W), lambda c, b: (b, c))
    out, lse = pl.pallas_call(
        body, out_shape=(shp, shp), grid=(d, m // tq),
        in_specs=[pl.BlockSpec((tq, w3), lambda c, b: (b, c)),
                  pl.BlockSpec((ATTN_BLK, w3), lambda c, b: (jnp.maximum(b * n - 1, 0), c))],
        out_specs=(ospec, ospec), name=name,
        compiler_params=_cparams(("parallel", "parallel")))(qv, qv)
    return out, lse


def _attn_bwd(qv, ov, dov, lv, d, name, dep=None):
    m = qv.shape[0]
    w3 = 3 * ATTN_W
    _, tq, n = _attn_tiles(m * d, d)
    nb = m // ATTN_BLK
    scale = HEAD ** -0.5
    assert math.frexp(scale)[0] == 0.5, "the kernel scales bf16 q: exact only for a power of two"

    def body(x_ref, p_ref, nx_ref, o_ref, do_ref, l_ref, on_ref, don_ref, ln_ref, *rest):
        dq_ref, dk_ref, dv_ref = rest[-3:]
        b = pl.program_id(1)
        dk_ref[...] = jnp.zeros_like(dk_ref)
        dv_ref[...] = jnp.zeros_like(dv_ref)

        low = lax.broadcasted_iota(jnp.int32, (1, LANE), 1) < HEAD

        def pair_grads(q2, k2, v2, o2, do2, l2, valid):
            dq, dk, dv = [], 0.0, 0.0
            q2 = q2 * scale
            for mask, lse in ((low, l2[:, 0:1]), (~low, l2[:, HEAD:HEAD + 1])):
                q = jnp.where(mask, q2, jnp.zeros_like(q2))
                do = jnp.where(mask, do2, 0.0)
                sc = jnp.where(valid, _dot_nt(q, k2), -1e30)
                p = jnp.exp(sc - lse)
                delta = jnp.sum(do * o2, axis=-1, keepdims=True)
                ds = p * (_dot_nt(do, v2) - delta)
                dq.append(_dot(ds, k2))
                dk = dk + _dot_tn(ds, q)
                dv = dv + _dot_tn(p, do)
            return jnp.where(low, dq[0], dq[1]) * scale, dk, dv

        def cols(hp):
            return [slice(part * ATTN_W + hp * LANE, part * ATTN_W + (hp + 1) * LANE) for part in range(3)]

        def block(i, first):
            r0 = 0 if first else pl.multiple_of(i * ATTN_BLK, ATTN_BLK)
            rows = pl.ds(r0, ATTN_BLK)
            own_only = first and nb == n
            valid = _causal_mask() if own_only else _band_mask(b * n + i)
            if not first:
                krows = pl.ds(pl.multiple_of(i * ATTN_BLK - ATTN_BLK, ATTN_BLK), 2 * ATTN_BLK)
            for hp in range(ATTN_W // LANE):
                qs, ks, vs = cols(hp)
                if own_only:
                    k2, v2 = x_ref[0:ATTN_BLK, ks], x_ref[0:ATTN_BLK, vs]
                elif first:
                    k2 = jnp.concatenate([p_ref[:, ks], x_ref[0:ATTN_BLK, ks]], axis=0)
                    v2 = jnp.concatenate([p_ref[:, vs], x_ref[0:ATTN_BLK, vs]], axis=0)
                else:
                    k2 = x_ref[krows, ks]
                    v2 = x_ref[krows, vs]
                dq, dk, dv = pair_grads(x_ref[rows, qs], k2, v2, o_ref[rows, qs], do_ref[rows, qs],
                                        l_ref[rows, qs], valid)
                dq_ref[rows, qs] = dq
                if own_only:
                    dk_ref[0:ATTN_BLK, qs] += dk
                    dv_ref[0:ATTN_BLK, qs] += dv
                elif first:
                    dk_ref[0:ATTN_BLK, qs] += dk[ATTN_BLK:, :]
                    dv_ref[0:ATTN_BLK, qs] += dv[ATTN_BLK:, :]
                else:
                    dk_ref[krows, qs] += dk
                    dv_ref[krows, qs] += dv

        block(0, True)
        if n > 1:
            def loop(i, carry):
                block(i, False)
                return carry
            lax.fori_loop(1, n, loop, 0, unroll=2)

        def next_tile():
            last = slice((n - 1) * ATTN_BLK, n * ATTN_BLK)
            qi = lax.broadcasted_iota(jnp.int32, (ATTN_BLK, ATTN_BLK), 0)
            ki = lax.broadcasted_iota(jnp.int32, (ATTN_BLK, ATTN_BLK), 1)
            for hp in range(ATTN_W // LANE):
                qs, ks, vs = cols(hp)
                _, dk, dv = pair_grads(nx_ref[:, qs], x_ref[last, ks], x_ref[last, vs], on_ref[:, qs],
                                       don_ref[:, qs], ln_ref[:, qs], qi <= ki)
                dk_ref[last, qs] += dk
                dv_ref[last, qs] += dv

        if nb > n:
            pl.when((b + 1) * n < nb)(next_tile)

    nxt = lambda b: jnp.minimum((b + 1) * n, nb - 1)
    xs = pl.BlockSpec((tq, w3), lambda c, b: (b, c))
    xp = pl.BlockSpec((ATTN_BLK, w3), lambda c, b: (jnp.maximum(b * n - 1, 0), c))
    xn = pl.BlockSpec((ATTN_BLK, w3), lambda c, b: (nxt(b), c))
    a = pl.BlockSpec((tq, ATTN_W), lambda c, b: (b, c))
    an = pl.BlockSpec((ATTN_BLK, ATTN_W), lambda c, b: (nxt(b), c))
    shp = jax.ShapeDtypeStruct((m, d * ATTN_W), F32)
    dep_specs, dep_ops = _dep_args(dep)
    return pl.pallas_call(
        body, out_shape=(shp, shp, shp), grid=(d, m // tq),
        in_specs=[xs, xp, xn, a, a, a, an, an, an] + dep_specs, out_specs=(a, a, a), name=name,
        compiler_params=_cparams(("parallel", "parallel")))(qv, qv, qv, ov, dov, lv, ov, dov, lv, *dep_ops)


def _rms(x, g):
    ms = jnp.mean(x * x, axis=-1, keepdims=True)
    return x * lax.rsqrt(ms + RMS_EPS) * g


def _rms_bwd(x, g, dy):
    ms = jnp.mean(x * x, axis=-1, keepdims=True)
    r = lax.rsqrt(ms + RMS_EPS)
    dyg = dy * g
    dx = r * dyg - x * (r * r * r) * jnp.mean(x * dyg, axis=-1, keepdims=True)
    return dx, dy * x * r


def _mix_fwd(outs, lses, lru, s5, g, h_in, w_out, ln_g, ln_b, name):
    s = lru.shape[0]
    tm = 512

    def body(o1, o2, o3, l1, l2, l3, lru_ref, s5_ref, g_ref, x_ref, w_ref, lg_ref, lb_ref,
             mixed_t_ref, r_ref, h_ref, ht_ref, ov1, ov2, ov3, lv1, lv2, lv3, so2, so3, sl2, sl3):
        for d, src, dst in ((DILATIONS[1], o2, so2), (DILATIONS[2], o3, so3),
                            (DILATIONS[1], l2, sl2), (DILATIONS[2], l3, sl3)):
            _to_tokens(src, dst, d, tm)
        a1, a2, a3 = l1[...], _token_value(sl2), _token_value(sl3)
        mx = jnp.maximum(jnp.maximum(a1, a2), a3)
        e1, e2, e3 = jnp.exp(a1 - mx), jnp.exp(a2 - mx), jnp.exp(a3 - mx)
        den = e1 + e2 + e3
        o = (e1 * o1[...] + e2 * _token_value(so2) + e3 * _token_value(so3)) / den
        lse = mx + jnp.log(den)
        ov1[...] = o
        lv1[...] = lse
        for j in range(ATTN_W // LANE):
            so2[j] = o[:, j * LANE:(j + 1) * LANE]
            sl2[j] = lse[:, j * LANE:(j + 1) * LANE]
        for d, o_dst, l_dst in ((DILATIONS[1], ov2, lv2), (DILATIONS[2], ov3, lv3)):
            _to_dilated(so2, o_dst, d, tm)
            _to_dilated(sl2, l_dst, d, tm)
        gg = g_ref[...]
        mixed = jnp.concatenate([_rms(o, gg[:, :ATTN_W]),
                                 _rms(lru_ref[...], gg[:, ATTN_W:ATTN_W + LRU_W]),
                                 _rms(s5_ref[...], gg[:, ATTN_W + LRU_W:])], axis=1)
        mixed_t_ref[...] = mixed.T.astype(BF16)
        r = ALPHA * x_ref[...] + _dot(mixed, w_ref[...])
        h = _layer_norm(r, lg_ref[...], lb_ref[...])
        r_ref[...] = r
        h_ref[...] = h
        ht_ref[...] = h.T.astype(BF16)

    a = pl.BlockSpec((tm, ATTN_W), lambda i: (i, 0))
    s5s = pl.BlockSpec((tm, S5_W), lambda i: (i, 0))
    full = pl.BlockSpec((tm, D_MODEL), lambda i: (i, 0))
    vec = pl.BlockSpec((1, D_MODEL), lambda i: (0, 0))
    dil = [_dilated_spec(tm, d, ATTN_W) for d in DILATIONS]
    dshape = [jax.ShapeDtypeStruct((s // d, d * ATTN_W), F32) for d in DILATIONS]
    tshape = jax.ShapeDtypeStruct((D_MODEL, s), BF16)
    fshape = jax.ShapeDtypeStruct((s, D_MODEL), F32)
    tspec = pl.BlockSpec((D_MODEL, tm), lambda i: (0, i))
    res = pl.pallas_call(
        body, out_shape=(tshape, fshape, fshape, tshape, *dshape, *dshape),
        grid=(s // tm,),
        in_specs=dil + dil + [a, s5s, vec, full, pl.BlockSpec((D_MODEL, D_MODEL), lambda i: (0, 0)), vec, vec],
        out_specs=(tspec, full, full, tspec, *dil, *dil),
        scratch_shapes=[_token_scratch(tm, ATTN_W)] * 4, name=name,
        compiler_params=_cparams(("parallel",)))(*outs, *lses, lru, s5, g, h_in, w_out, ln_g, ln_b)
    return res[0], res[1], res[2], res[3], res[4:7], res[7:10]


def _mix_bwd(r, dh, ln_g, w_out, o, lru, s5, g, name, dep=None):
    s = lru.shape[0]
    tm = 512

    def body(r_ref, dh_ref, lg_ref, w_ref, o_ref, lru_ref, s5_ref, g_ref, *rest):
        dr_ref, dlg_ref, dlb_ref, do_ref, do2_ref, do3_ref, dlru_ref, ds5_ref, dg_ref, stage = rest[-10:]

        @pl.when(pl.program_id(0) == 0)
        def _():
            dg_ref[...] = jnp.zeros_like(dg_ref)
            dlg_ref[...] = jnp.zeros_like(dlg_ref)
            dlb_ref[...] = jnp.zeros_like(dlb_ref)
        gg = g_ref[...]
        dr, dlg_rows, dlb_rows = _layer_norm_bwd(r_ref[...], dh_ref[...], lg_ref[...])
        dr_ref[...] = dr
        dlg_ref[...] += dlg_rows
        dlb_ref[...] += dlb_rows
        dm = _dot_nt(dr, w_ref[...])
        dx, dgr = _rms_bwd(o_ref[...], gg[:, :ATTN_W], dm[:, :ATTN_W])
        do_ref[...] = dx
        for j in range(ATTN_W // LANE):
            stage[j] = dx[:, j * LANE:(j + 1) * LANE]
        _to_dilated(stage, do2_ref, DILATIONS[1], tm)
        _to_dilated(stage, do3_ref, DILATIONS[2], tm)
        dg_ref[:, :ATTN_W] += jnp.sum(dgr, axis=0, keepdims=True)
        dx, dgr = _rms_bwd(lru_ref[...], gg[:, ATTN_W:ATTN_W + LRU_W], dm[:, ATTN_W:ATTN_W + LRU_W])
        dlru_ref[...] = dx
        dg_ref[:, ATTN_W:ATTN_W + LRU_W] += jnp.sum(dgr, axis=0, keepdims=True)
        dx, dgr = _rms_bwd(s5_ref[...], gg[:, ATTN_W + LRU_W:], dm[:, ATTN_W + LRU_W:])
        ds5_ref[...] = dx
        dg_ref[:, ATTN_W + LRU_W:] += jnp.sum(dgr, axis=0, keepdims=True)

    a = pl.BlockSpec((tm, ATTN_W), lambda i: (i, 0))
    s5s = pl.BlockSpec((tm, S5_W), lambda i: (i, 0))
    full = pl.BlockSpec((tm, D_MODEL), lambda i: (i, 0))
    vec = pl.BlockSpec((1, D_MODEL), lambda i: (0, 0))
    dil = [_dilated_spec(tm, d, ATTN_W) for d in DILATIONS]
    dshape = [jax.ShapeDtypeStruct((s // d, d * ATTN_W), F32) for d in DILATIONS]
    dep_specs, dep_ops = _dep_args(dep)
    vshape = jax.ShapeDtypeStruct((1, D_MODEL), F32)
    res = pl.pallas_call(
        body, out_shape=(jax.ShapeDtypeStruct((s, D_MODEL), F32), vshape, vshape, *dshape,
                         jax.ShapeDtypeStruct((s, LRU_W), F32), jax.ShapeDtypeStruct((s, S5_W), F32), vshape),
        grid=(s // tm,),
        in_specs=[full, full, vec, pl.BlockSpec((D_MODEL, D_MODEL), lambda i: (0, 0)), a, a, s5s, vec] + dep_specs,
        out_specs=(full, vec, vec, *dil, a, s5s, vec), scratch_shapes=[_token_scratch(tm, ATTN_W)], name=name,
        compiler_params=_cparams(("arbitrary",)))(r, dh, ln_g, w_out, o, lru, s5, g, *dep_ops)
    return res[0], res[1], res[2], res[3:6], res[6], res[7], res[8]


def _lru_gate_math(xc, pre_r, pre_i, lam):
    r = _sigmoid(pre_r)
    i = _sigmoid(pre_i)
    log_a = -LRU_C * r * _softplus(-lam)
    a = jnp.exp(log_a)
    u = jnp.sqrt(-_expm1(2.0 * log_a)) * (i * xc)
    return a, u


def _lru_conv(x, prev8, cw, cb):
    y = cb + cw[LRU_CONV - 1:LRU_CONV, :] * x
    for k in range(LRU_CONV - 1):
        y = y + cw[k:k + 1, :] * _shift_down_prev(x, LRU_CONV - 1 - k, prev8)
    return y


def _lru_specs(s):
    xo = 3 * ATTN_W // LANE
    go = xo + LRU_W // LANE
    xr = pl.BlockSpec((s, LANE), lambda j: (0, xo + j))
    gt = pl.BlockSpec((s, LANE), lambda j: (0, go + j))
    cw = pl.BlockSpec((LRU_CONV, LANE), lambda j: (0, j))
    vec = pl.BlockSpec((1, LANE), lambda j: (0, j))
    wbd = pl.BlockSpec((LANE, LANE), lambda j: (j, j))
    col = pl.BlockSpec((s, LANE), lambda j: (0, j))
    return xr, gt, cw, vec, wbd, col


def _lru_fwd(proj, cw, cb, wr, br, wi, bi, lam, name):
    s = proj.shape[0]
    t = SCAN_T

    def body(xr_ref, gt_ref, cw_ref, cb_ref, wr_ref, br_ref, wi_ref, bi_ref, lam_ref, o_ref, xc_ref, a_ref, h_ref):
        cwv, cbv, lamv = cw_ref[...], cb_ref[...], lam_ref[...]
        wrv, wiv, brv, biv = wr_ref[...], wi_ref[...], br_ref[...], bi_ref[...]

        def chunk(c, carry):
            h_c, prev8 = carry
            rows = pl.ds(pl.multiple_of(c * t, t), t)
            x = xr_ref[rows, :]
            xc = _lru_conv(x, prev8, cwv, cbv)
            a, u = _lru_gate_math(xc, _dot(xc, wrv) + brv, _dot(xc, wiv) + biv, lamv)
            h = _scan_chunk(a, u, h_c)
            xc_ref[rows, :] = xc
            a_ref[rows, :] = a
            h_ref[rows, :] = h
            o_ref[rows, :] = h * _gelu(gt_ref[rows, :])
            return h[t - 1:t, :], x[t - 8:t, :]

        lax.fori_loop(0, s // t, chunk, (jnp.zeros((1, LANE), F32), jnp.zeros((8, LANE), F32)))

    xr, gt, cws, vec, wbd, col = _lru_specs(s)
    shp = jax.ShapeDtypeStruct((s, LRU_W), F32)
    return pl.pallas_call(
        body, out_shape=(shp,) * 4, grid=(LRU_W // LANE,),
        in_specs=[xr, gt, cws, vec, wbd, vec, wbd, vec, vec], out_specs=(col,) * 4, name=name,
        compiler_params=_cparams(("parallel",)))(proj, proj, cw, cb, wr, br, wi, bi, lam)


def _lru_bwd(proj, dout, xc_all, a_all, h_all, cw, cb, wr, br, wi, bi, lam, name):
    s = proj.shape[0]
    t = SCAN_T
    nc = s // t

    def body(xr_ref, gt_ref, do_ref, xc_s, a_s, h_s, cw_ref, cb_ref, wr_ref, br_ref, wi_ref, bi_ref, lam_ref,
             dxr_ref, dgt_ref, dcw_ref, dcb_ref, dwr_ref, dbr_ref, dwi_ref, dbi_ref, dlam_ref):
        cwv, cbv, lamv = cw_ref[...], cb_ref[...], lam_ref[...]
        wrv, wiv, brv, biv = wr_ref[...], wi_ref[...], br_ref[...], bi_ref[...]
        z1 = jnp.zeros((1, LANE), F32)
        zw = jnp.zeros((LANE, LANE), F32)

        def bchunk(ci, carry):
            g_next, a_next, dxc_next8, dcw, dcb, dwr, dbr, dwi, dbi, dlam = carry
            c = nc - 1 - ci
            t0 = pl.multiple_of(c * t, t)
            rows = pl.ds(t0, t)
            before = pl.ds(pl.multiple_of(jnp.maximum(t0 - 8, 0), 8), 8)
            has_prev = (c > 0).astype(F32)
            x, gt, do = xr_ref[rows, :], gt_ref[rows, :], do_ref[rows, :]
            xc, a, h = xc_s[rows, :], a_s[rows, :], h_s[rows, :]
            prev8_h = h_s[before, :] * has_prev
            dgt_ref[rows, :] = do * h * _gelu_grad(gt)
            dh = do * _gelu(gt)
            a_plus = _shift_up_next(a, 1, jnp.broadcast_to(a_next, (8, LANE)))
            g = _scan_chunk(a_plus, dh, g_next, reverse=True)
            da = g * _shift_down_prev(h, 1, prev8_h)
            pre_r = _dot(xc, wrv) + brv
            pre_i = _dot(xc, wiv) + biv
            _, vjp = jax.vjp(_lru_gate_math, xc, pre_r, pre_i, lamv)
            dxc, dpre_r, dpre_i, dlam_c = vjp((da, g))
            dxc = dxc + _dot_nt(dpre_r, wrv) + _dot_nt(dpre_i, wiv)
            dx = cwv[LRU_CONV - 1:LRU_CONV, :] * dxc
            dcw_rows = [None] * LRU_CONV
            dcw_rows[LRU_CONV - 1] = jnp.sum(dxc * x, axis=0, keepdims=True)
            for k in range(LRU_CONV - 1):
                dxc_ahead = _shift_up_next(dxc, LRU_CONV - 1 - k, dxc_next8)
                dx = dx + cwv[k:k + 1, :] * dxc_ahead
                dcw_rows[k] = jnp.sum(dxc_ahead * x, axis=0, keepdims=True)
            dxr_ref[rows, :] = dx
            return (g[0:1, :], a[0:1, :], dxc[0:8, :],
                    dcw + jnp.concatenate(dcw_rows, axis=0),
                    dcb + jnp.sum(dxc, axis=0, keepdims=True),
                    dwr + _dot_tn(xc, dpre_r), dbr + jnp.sum(dpre_r, axis=0, keepdims=True),
                    dwi + _dot_tn(xc, dpre_i), dbi + jnp.sum(dpre_i, axis=0, keepdims=True),
                    dlam + dlam_c)

        init = (z1, z1, jnp.zeros((8, LANE), F32), jnp.zeros((LRU_CONV, LANE), F32), z1, zw, z1, zw, z1, z1)
        res = lax.fori_loop(0, nc, bchunk, init)
        dcw_ref[...] = res[3]
        dcb_ref[...] = res[4]
        dwr_ref[...] = res[5]
        dbr_ref[...] = res[6]
        dwi_ref[...] = res[7]
        dbi_ref[...] = res[8]
        dlam_ref[...] = res[9]

    xr, gt, cws, vec, wbd, col = _lru_specs(s)
    vshape = jax.ShapeDtypeStruct((1, LRU_W), F32)
    wshape = jax.ShapeDtypeStruct((LRU_W, LRU_W), F32)
    return pl.pallas_call(
        body,
        out_shape=(jax.ShapeDtypeStruct((s, LRU_W), F32), jax.ShapeDtypeStruct((s, LRU_W), F32),
                   jax.ShapeDtypeStruct((LRU_CONV, LRU_W), F32), vshape, wshape, vshape, wshape, vshape, vshape),
        grid=(LRU_W // LANE,),
        in_specs=[xr, gt, col, col, col, col, cws, vec, wbd, vec, wbd, vec, vec],
        out_specs=(col, col, cws, vec, wbd, vec, wbd, vec, vec), name=name,
        compiler_params=_cparams(("parallel",)))(proj, proj, dout, xc_all, a_all, h_all, cw, cb, wr, br, wi, bi,
                                                 lam)


def _s5_disc_math(a_re, a_im, log_step, bt_re, bt_im):
    step = jnp.exp(log_step)
    dt_re, dt_im = step * a_re, step * a_im
    mag = jnp.exp(dt_re)
    ab_re, ab_im = mag * jnp.cos(dt_im), mag * jnp.sin(dt_im)
    z_re, z_im = ab_re - 1.0, ab_im
    den = a_re * a_re + a_im * a_im
    f_re = (z_re * a_re + z_im * a_im) / den
    f_im = (z_im * a_re - z_re * a_im) / den
    bb_re = f_re * bt_re - f_im * bt_im
    bb_im = f_re * bt_im + f_im * bt_re
    return ab_re, ab_im, bb_re, bb_im


def _s5_disc_fwd(a_re, a_im, log_step, bt_re, bt_im, name):
    def body(ar, ai, ls, br, bi, o1, o2, o3, o4):
        r = _s5_disc_math(ar[...], ai[...], ls[...], br[...], bi[...])
        o1[...], o2[...], o3[...], o4[...] = r

    shp = jax.ShapeDtypeStruct(a_re.shape, F32)
    return pl.pallas_call(body, out_shape=(shp,) * 4, name=name)(a_re, a_im, log_step, bt_re, bt_im)


def _s5_disc_bwd(a_re, a_im, log_step, bt_re, bt_im, cts, name):
    def body(ar, ai, ls, br, bi, c1, c2, c3, c4, o1, o2, o3, o4, o5):
        _, vjp = jax.vjp(_s5_disc_math, ar[...], ai[...], ls[...], br[...], bi[...])
        r = vjp((c1[...], c2[...], c3[...], c4[...]))
        o1[...], o2[...], o3[...], o4[...], o5[...] = r

    shp = jax.ShapeDtypeStruct(a_re.shape, F32)
    return pl.pallas_call(body, out_shape=(shp,) * 5, name=name)(a_re, a_im, log_step, bt_re, bt_im, *cts)


def _s5_u_specs(s):
    uo = (3 * ATTN_W + 2 * LRU_W) // LANE
    return (pl.BlockSpec((s, LANE), lambda j: (0, uo)), pl.BlockSpec((s, LANE), lambda j: (0, uo + 1)))


def _s5_scan_fwd(proj, b_re, b_im, lam_re, lam_im, c_re, c_im, name):
    s = proj.shape[0]
    t = SCAN_T

    def body(u0_ref, u1_ref, bre_ref, bim_ref, lre_ref, lim_ref, cre_ref, cim_ref, xre_ref, xim_ref, y_ref):
        @pl.when(pl.program_id(0) == 0)
        def _():
            y_ref[...] = jnp.zeros_like(y_ref)
        lr, li = lre_ref[...], lim_ref[...]
        consts = _cscan_consts(lr, li, False)
        bre, bim, cre, cim = bre_ref[...], bim_ref[...], cre_ref[...], cim_ref[...]

        def chunk(c, carry):
            cr, ci = carry
            rows = pl.ds(pl.multiple_of(c * t, t), t)
            u = jnp.concatenate([u0_ref[rows, :], u1_ref[rows, :]], axis=1).astype(BF16)
            xr, xi = _cscan_chunk(_dot(u, bre), _dot(u, bim), consts, (cr, ci))
            xre_ref[rows, :] = xr
            xim_ref[rows, :] = xi
            y_ref[rows, :] += _dot(xr, cre) - _dot(xi, cim)
            return xr[t - 1:t, :], xi[t - 1:t, :]

        z = jnp.zeros((1, S5_BLK), F32)
        lax.fori_loop(0, s // t, chunk, (z, z))

    u0, u1 = _s5_u_specs(s)
    bsp = pl.BlockSpec((S5_W, S5_BLK), lambda j: (0, j))
    csp = pl.BlockSpec((S5_BLK, S5_W), lambda j: (j, 0))
    vec = pl.BlockSpec((1, S5_BLK), lambda j: (0, j))
    xsp = pl.BlockSpec((s, S5_BLK), lambda j: (0, j))
    ysp = pl.BlockSpec((s, S5_W), lambda j: (0, 0))
    xshape = jax.ShapeDtypeStruct((s, S5_STATES), F32)
    return pl.pallas_call(
        body, out_shape=(xshape, xshape, jax.ShapeDtypeStruct((s, S5_W), F32)),
        grid=(S5_STATES // S5_BLK,), in_specs=[u0, u1, bsp, bsp, vec, vec, csp, csp],
        out_specs=(xsp, xsp, ysp), name=name,
        compiler_params=_cparams(("arbitrary",)))(proj, proj, b_re, b_im, lam_re, lam_im, c_re, c_im)


def _s5_scan_bwd(proj, dy, du_init, x_re, x_im, b_re, b_im, lam_re, lam_im, c_re, c_im, name):
    s = proj.shape[0]
    t = SCAN_T
    nc = s // t

    def body(u0_ref, u1_ref, dy_ref, dui_ref, xre_ref, xim_ref, bre_ref, bim_ref, lre_ref, lim_ref,
             cre_ref, cim_ref, du_ref, dlr_ref, dli_ref, dbr_ref, dbi_ref, dcr_ref, dci_ref):
        @pl.when(pl.program_id(0) == 0)
        def _():
            du_ref[...] = dui_ref[...]
        mr, mi = lre_ref[...], -lim_ref[...]
        consts = _cscan_consts(mr, mi, True)
        bre, bim, cre, cim = bre_ref[...], bim_ref[...], cre_ref[...], cim_ref[...]
        dbr_ref[...] = jnp.zeros_like(dbr_ref)
        dbi_ref[...] = jnp.zeros_like(dbi_ref)
        dcr_ref[...] = jnp.zeros_like(dcr_ref)
        dci_ref[...] = jnp.zeros_like(dci_ref)

        def chunk(ci_, carry):
            gnr, gni, dlr, dli = carry
            c = nc - 1 - ci_
            t0 = pl.multiple_of(c * t, t)
            rows = pl.ds(t0, t)
            before = pl.ds(pl.multiple_of(jnp.maximum(t0 - 8, 0), 8), 8)
            has_prev = (c > 0).astype(F32)
            dyc = dy_ref[rows, :].astype(BF16)
            u = jnp.concatenate([u0_ref[rows, :], u1_ref[rows, :]], axis=1).astype(BF16)
            gr, gi = _cscan_chunk(_dot_nt(dyc, cre), -_dot_nt(dyc, cim), consts, (gnr, gni), reverse=True)
            xr, xi = xre_ref[rows, :], xim_ref[rows, :]
            xpr = _shift_down_prev(xr, 1, xre_ref[before, :] * has_prev)
            xpi = _shift_down_prev(xi, 1, xim_ref[before, :] * has_prev)
            dlr = dlr + jnp.sum(gr * xpr + gi * xpi, axis=0, keepdims=True)
            dli = dli + jnp.sum(gi * xpr - gr * xpi, axis=0, keepdims=True)
            du_ref[rows, :] += _dot_nt(gr, bre) + _dot_nt(gi, bim)
            dbr_ref[...] += _dot_tn(u, gr)
            dbi_ref[...] += _dot_tn(u, gi)
            dcr_ref[...] += _dot_tn(xr, dyc)
            dci_ref[...] -= _dot_tn(xi, dyc)
            return gr[0:1, :], gi[0:1, :], dlr, dli

        z = jnp.zeros((1, S5_BLK), F32)
        res = lax.fori_loop(0, nc, chunk, (z, z, z, z))
        dlr_ref[...] = res[2]
        dli_ref[...] = res[3]

    u0, u1 = _s5_u_specs(s)
    bsp = pl.BlockSpec((S5_W, S5_BLK), lambda j: (0, j))
    csp = pl.BlockSpec((S5_BLK, S5_W), lambda j: (j, 0))
    vec = pl.BlockSpec((1, S5_BLK), lambda j: (0, j))
    xsp = pl.BlockSpec((s, S5_BLK), lambda j: (0, j))
    ysp = pl.BlockSpec((s, S5_W), lambda j: (0, 0))
    return pl.pallas_call(
        body,
        out_shape=(jax.ShapeDtypeStruct((s, S5_W), F32),
                   jax.ShapeDtypeStruct((1, S5_STATES), F32), jax.ShapeDtypeStruct((1, S5_STATES), F32),
                   jax.ShapeDtypeStruct((S5_W, S5_STATES), F32), jax.ShapeDtypeStruct((S5_W, S5_STATES), F32),
                   jax.ShapeDtypeStruct((S5_STATES, S5_W), F32), jax.ShapeDtypeStruct((S5_STATES, S5_W), F32)),
        grid=(S5_STATES // S5_BLK,),
        in_specs=[u0, u1, ysp, ysp, xsp, xsp, bsp, bsp, vec, vec, csp, csp],
        out_specs=(ysp, vec, vec, bsp, bsp, csp, csp), name=name,
        compiler_params=_cparams(("arbitrary",)))(
            proj, proj, dy, du_init, x_re, x_im, b_re, b_im, lam_re, lam_im, c_re, c_im)


def _s5_out_fwd(proj, y_acc, dvec, w_glu, b_glu, name):
    s = proj.shape[0]
    tm = 512
    uo = (3 * ATTN_W + 2 * LRU_W) // LANE

    def body(u0_ref, u1_ref, y_ref, d_ref, w_ref, b_ref, o_ref, yp_ref):
        u = jnp.concatenate([u0_ref[...], u1_ref[...]], axis=1)
        y = y_ref[...] + d_ref[...] * u
        yp_ref[...] = y
        yg = _gelu(y)
        o_ref[...] = yg * _sigmoid(_dot(yg, w_ref[...]) + b_ref[...])

    u0 = pl.BlockSpec((tm, LANE), lambda i: (i, uo))
    u1 = pl.BlockSpec((tm, LANE), lambda i: (i, uo + 1))
    row = pl.BlockSpec((tm, S5_W), lambda i: (i, 0))
    vec = pl.BlockSpec((1, S5_W), lambda i: (0, 0))
    wsp = pl.BlockSpec((S5_W, S5_W), lambda i: (0, 0))
    shp = jax.ShapeDtypeStruct((s, S5_W), F32)
    return pl.pallas_call(
        body, out_shape=(shp, shp), grid=(s // tm,), in_specs=[u0, u1, row, vec, wsp, vec],
        out_specs=(row, row), name=name,
        compiler_params=_cparams(("parallel",)))(proj, proj, y_acc, dvec, w_glu, b_glu)


def _s5_out_bwd(proj, y_pre, dout, dvec, w_glu, b_glu, name, dep=None):
    s = proj.shape[0]
    tm = 512
    uo = (3 * ATTN_W + 2 * LRU_W) // LANE

    def body(u0_ref, u1_ref, y_ref, do_ref, d_ref, w_ref, b_ref, *rest):
        dy_ref, dud_ref, dd_ref, dw_ref, db_ref = rest[-5:]

        @pl.when(pl.program_id(0) == 0)
        def _():
            dd_ref[...] = jnp.zeros_like(dd_ref)
            dw_ref[...] = jnp.zeros_like(dw_ref)
            db_ref[...] = jnp.zeros_like(db_ref)
        u = jnp.concatenate([u0_ref[...], u1_ref[...]], axis=1)
        y = y_ref[...]
        do = do_ref[...]
        yg = _gelu(y)
        sg = _sigmoid(_dot(yg, w_ref[...]) + b_ref[...])
        dz = do * yg * sg * (1.0 - sg)
        dyg = do * sg + _dot_nt(dz, w_ref[...])
        dy = dyg * _gelu_grad(y)
        dy_ref[...] = dy
        dud_ref[...] = d_ref[...] * dy
        dd_ref[...] += jnp.sum(dy * u, axis=0, keepdims=True)
        dw_ref[...] += _dot_tn(yg, dz)
        db_ref[...] += jnp.sum(dz, axis=0, keepdims=True)

    u0 = pl.BlockSpec((tm, LANE), lambda i: (i, uo))
    u1 = pl.BlockSpec((tm, LANE), lambda i: (i, uo + 1))
    row = pl.BlockSpec((tm, S5_W), lambda i: (i, 0))
    vec = pl.BlockSpec((1, S5_W), lambda i: (0, 0))
    wsp = pl.BlockSpec((S5_W, S5_W), lambda i: (0, 0))
    shp = jax.ShapeDtypeStruct((s, S5_W), F32)
    vshape = jax.ShapeDtypeStruct((1, S5_W), F32)
    dep_specs, dep_ops = _dep_args(dep)
    return pl.pallas_call(
        body, out_shape=(shp, shp, vshape, jax.ShapeDtypeStruct((S5_W, S5_W), F32), vshape),
        grid=(s // tm,), in_specs=[u0, u1, row, row, vec, wsp, vec] + dep_specs,
        out_specs=(row, row, vec, wsp, vec), name=name,
        compiler_params=_cparams(("arbitrary",)))(proj, proj, y_pre, dout, dvec, w_glu, b_glu, *dep_ops)


def _ffn_conv(x, prev8, cw, cb):
    y = cb + cw[FFN_CONV - 1:FFN_CONV, :] * x
    for k in range(FFN_CONV - 1):
        y = y + cw[k:k + 1, :] * _shift_down_prev(x, FFN_CONV - 1 - k, prev8)
    return y


def _ffn_up_act(h, wg, cw, cb, name, dep=None):
    s, d = h.shape
    tm = 512
    tb = 2 * FFN_CB
    nt = D_FF // FFN_CB

    def body(h_ref, wgate_ref, wval_ref, cw_ref, cb_ref, *rest):
        up_ref, y_ref, o_ref, ot_ref, carry = rest[-5:]

        @pl.when(pl.program_id(1) == 0)
        def _():
            carry[...] = jnp.zeros_like(carry)
        hb = h_ref[...].astype(BF16)
        x = jnp.concatenate([_dot(hb, wgate_ref[...]), _dot(hb, wval_ref[...])], axis=1)
        up_ref[...] = x.astype(BF16)
        y = _ffn_conv(x, carry[...], cw_ref[...], cb_ref[...])
        y_ref[...] = y
        carry[...] = x[tm - 8:tm, :]
        act = _gelu(y[:, :FFN_CB]) * y[:, FFN_CB:]
        o_ref[...] = act.astype(BF16)
        ot_ref[...] = act.T.astype(BF16)

    dep_specs, dep_ops = _dep_args(dep)
    return pl.pallas_call(
        body, out_shape=(jax.ShapeDtypeStruct((s, 2 * D_FF), BF16), jax.ShapeDtypeStruct((s, 2 * D_FF), F32),
                         jax.ShapeDtypeStruct((s, D_FF), BF16), jax.ShapeDtypeStruct((D_FF, s), BF16)),
        grid=(nt, s // tm),
        in_specs=[pl.BlockSpec((tm, d), lambda t, i: (i, 0)),
                  pl.BlockSpec((None, d, FFN_CB), lambda t, i: (t, 0, 0)),
                  pl.BlockSpec((None, d, FFN_CB), lambda t, i: (t + nt, 0, 0)),
                  pl.BlockSpec((FFN_CONV, tb), lambda t, i: (0, t)),
                  pl.BlockSpec((1, tb), lambda t, i: (0, t))] + dep_specs,
        out_specs=(pl.BlockSpec((tm, tb), lambda t, i: (i, t)), pl.BlockSpec((tm, tb), lambda t, i: (i, t)),
                   pl.BlockSpec((tm, FFN_CB), lambda t, i: (i, t)), pl.BlockSpec((FFN_CB, tm), lambda t, i: (t, i))),
        scratch_shapes=[pltpu.VMEM((8, tb), F32)], name=name,
        compiler_params=_cparams(("parallel", "arbitrary")))(h, wg, wg, cw, cb, *dep_ops)


def _ffn_bwd(up, y_conv, dr, w_down, wg, cw, name):
    s = up.shape[0]
    d = dr.shape[1]
    tm = 512
    tb = 2 * FFN_CB
    nr = s // tm
    nt = D_FF // FFN_CB

    def body(x_ref, y_ref, dr_ref, wd_ref, wgate_ref, wval_ref, cw_ref,
             dup_ref, dh_ref, dcw_ref, dcb_ref, carry):
        i, t = pl.program_id(0), pl.program_id(1)

        @pl.when(i == 0)
        def _():
            carry[t] = jnp.zeros((8, tb), F32)

        @pl.when(t == 0)
        def _():
            dh_ref[...] = ALPHA * dr_ref[...]
        cwv = cw_ref[...]
        x = x_ref[...]
        dact = _dot_nt(dr_ref[...], wd_ref[...])
        gate, val = y_ref[:, :FFN_CB], y_ref[:, FFN_CB:]
        dy = jnp.concatenate([dact * val * _gelu_grad(gate), dact * _gelu(gate)], axis=1)
        next8 = carry[t]
        carry[t] = dy[0:8, :]
        dx = cwv[FFN_CONV - 1:FFN_CONV, :] * dy
        dcw_rows = [None] * FFN_CONV
        dcw_rows[FFN_CONV - 1] = jnp.sum(dy * x, axis=0, keepdims=True)
        for k in range(FFN_CONV - 1):
            dy_ahead = _shift_up_next(dy, FFN_CONV - 1 - k, next8)
            dx = dx + cwv[k:k + 1, :] * dy_ahead
            dcw_rows[k] = jnp.sum(dy_ahead * x, axis=0, keepdims=True)
        dup = dx.astype(BF16)
        dup_ref[...] = dup
        dh_ref[...] += _dot_nt(dup[:, :FFN_CB], wgate_ref[...]) + _dot_nt(dup[:, FFN_CB:], wval_ref[...])
        dcw_ref[...] = jnp.concatenate(dcw_rows, axis=0)
        dcb_ref[...] = jnp.sum(dy, axis=0, keepdims=True)

    row = lambda i: nr - 1 - i
    return pl.pallas_call(
        body, out_shape=(jax.ShapeDtypeStruct((s, 2 * D_FF), BF16), jax.ShapeDtypeStruct((s, d), F32),
                         jax.ShapeDtypeStruct((nr, FFN_CONV, 2 * D_FF), F32),
                         jax.ShapeDtypeStruct((nr, 1, 2 * D_FF), F32)),
        grid=(nr, nt),
        in_specs=[pl.BlockSpec((tm, tb), lambda i, t: (row(i), t)),
                  pl.BlockSpec((tm, tb), lambda i, t: (row(i), t)),
                  pl.BlockSpec((tm, d), lambda i, t: (row(i), 0)),
                  pl.BlockSpec((FFN_CB, d), lambda i, t: (t, 0)),
                  pl.BlockSpec((None, d, FFN_CB), lambda i, t: (t, 0, 0)),
                  pl.BlockSpec((None, d, FFN_CB), lambda i, t: (t + nt, 0, 0)),
                  pl.BlockSpec((FFN_CONV, tb), lambda i, t: (0, t))],
        out_specs=(pl.BlockSpec((tm, tb), lambda i, t: (row(i), t)),
                   pl.BlockSpec((tm, d), lambda i, t: (row(i), 0)),
                   pl.BlockSpec((None, FFN_CONV, tb), lambda i, t: (row(i), 0, t)),
                   pl.BlockSpec((None, 1, tb), lambda i, t: (row(i), 0, t))),
        scratch_shapes=[pltpu.VMEM((nt, 8, tb), F32)], name=name,
        compiler_params=_cparams(("arbitrary", "arbitrary")))(up, y_conv, dr, w_down, wg, wg, cw)


def _sum_partials(ld_ref):
    gg = ld_ref[0].astype(F32)
    for k in range(1, N_DEV):
        gg = gg + ld_ref[k].astype(F32)
    return gg


def _adam_update(w, g, m, v):
    mn = ADAM_B1 * m + (1.0 - ADAM_B1) * g
    vn = ADAM_B2 * v + (1.0 - ADAM_B2) * (g * g)
    m_hat = mn / (1.0 - ADAM_B1 ** ADAM_STEP)
    v_hat = vn / (1.0 - ADAM_B2 ** ADAM_STEP)
    return -ADAM_LR * (m_hat / (jnp.sqrt(v_hat) + ADAM_EPS) + ADAM_WD * w), mn, vn


def _adamw_many(landed, ws, ms, vs, name):
    n, nl = len(ws), len(landed)

    def body(*refs):
        ld = refs[:nl * n]
        w_refs, m_refs, v_refs = (refs[(nl + k) * n:(nl + k + 1) * n] for k in range(3))
        outs = refs[(nl + 3) * n:]
        for i in range(n):
            for l in range(nl):
                one = slice(l, l + 1)
                gg = _sum_partials(ld[l * n + i])
                outs[i][one] = gg
                outs[n + i][one], outs[2 * n + i][one], outs[3 * n + i][one] = _adam_update(
                    w_refs[i][one], gg, m_refs[i][one], v_refs[i][one])

    vm = pl.BlockSpec(memory_space=pltpu.VMEM)
    shapes = [jax.ShapeDtypeStruct(w.shape, F32) for w in ws] * 4
    res = pl.pallas_call(
        body, out_shape=tuple(shapes), in_specs=[vm] * ((nl + 3) * n), out_specs=tuple([vm] * (4 * n)),
        name=name, compiler_params=_cparams())(*[a for layer in landed for a in layer], *ws, *ms, *vs)
    return res[:n], res[n:2 * n], res[2 * n:3 * n], res[3 * n:]


def _adamw_sum(landed, w, m, v, layer, prev, name):
    _, r, c = landed.shape
    nl = w.shape[0]
    tm = 8
    for cand in (512, 256, 128, 64, 32, 16):
        if r % cand == 0 and N_DEV * cand * c * 4 <= 4 * 1024 * 1024:
            tm = cand
            break

    def body(*refs):
        ld_ref, w_ref, m_ref, v_ref = refs[:4]
        g_ref, d_ref, mo_ref, vo_ref = refs[-4:]
        gg = _sum_partials(ld_ref)
        g_ref[...] = gg
        d_ref[...], mo_ref[...], vo_ref[...] = _adam_update(w_ref[...], gg, m_ref[...], v_ref[...])

    blk = pl.BlockSpec((None, tm, c), lambda i: (layer, i, 0))
    in_specs = [pl.BlockSpec((N_DEV, tm, c), lambda i: (0, i, 0)), blk, blk, blk]
    args = [landed, w, m, v]
    aliases = {}
    if prev is not None:
        in_specs += [pl.BlockSpec(memory_space=pl.ANY)] * 4
        args += list(prev)
        aliases = {4 + k: k for k in range(4)}
    shp = jax.ShapeDtypeStruct((nl, r, c), F32)
    return pl.pallas_call(
        body, out_shape=(shp,) * 4, grid=(r // tm,), in_specs=in_specs, out_specs=(blk,) * 4,
        input_output_aliases=aliases, name=name, compiler_params=_cparams(("parallel",)))(*args)


def _all_gather(shards, name):
    na = len(shards)

    def body(*refs):
        x_refs, out_refs = refs[:na], refs[na:2 * na]
        send_sems, recv_sems, local_sems = refs[2 * na:]
        x, y, c = lax.axis_index("x"), lax.axis_index("y"), lax.axis_index("c")
        me, sibling = (x, y, c), (x, y, 1 - c)
        chips = [(1 - x, y), (x, 1 - y), (1 - x, 1 - y)]

        def copy(a, k, block, to, src=None):
            dst = out_refs[a].at[4 * block[0] + 2 * block[1] + block[2]]
            return pltpu.make_async_remote_copy(
                src_ref=dst if src is None else src, dst_ref=dst,
                send_sem=send_sems.at[7 * a + k], recv_sem=recv_sems.at[7 * a + k],
                device_id=to, device_id_type=pl.DeviceIdType.MESH)

        mine, first, passed = [], [], []
        for a in range(na):
            cp = pltpu.make_async_copy(x_refs[a], out_refs[a].at[4 * x + 2 * y + c], local_sems.at[a])
            cp.start()
            mine.append(cp)
            cps = [copy(a, 0, me, sibling, src=x_refs[a])]
            cps += [copy(a, 1 + j, me, (*chip, c), src=x_refs[a]) for j, chip in enumerate(chips)]
            for cp in cps:
                cp.start()
            first += cps
        for j, chip in enumerate(chips):
            for a in range(na):
                copy(a, 1 + j, (*chip, c), me).wait_recv()
                cp = copy(a, 4 + j, (*chip, c), sibling)
                cp.start()
                passed.append(cp)
        for a in range(na):
            copy(a, 0, sibling, me).wait_recv()
            for j, chip in enumerate(chips):
                copy(a, 4 + j, (*chip, 1 - c), me).wait_recv()
        for cp in first + passed:
            cp.wait_send()
        for cp in mine:
            cp.wait()

    anyspec = pl.BlockSpec(memory_space=pl.ANY)
    return pl.pallas_call(
        body, out_shape=tuple(jax.ShapeDtypeStruct((N_DEV,) + t.shape, t.dtype) for t in shards),
        in_specs=[anyspec] * na, out_specs=tuple([anyspec] * na),
        scratch_shapes=[pltpu.SemaphoreType.DMA((7 * na,)), pltpu.SemaphoreType.DMA((7 * na,)),
                        pltpu.SemaphoreType.DMA((na,))],
        name=name)(*shards)


_HBM = pl.BlockSpec(memory_space=pltpu.HBM)
_SEM = pl.BlockSpec(memory_space=pltpu.SEMAPHORE)
_EFFECT = pltpu.SideEffectType.DATAFLOW_SIDE_EFFECTING


def _exchange_copies(src_refs, land_refs, send_sems, recv_sems, local_sems, gather):
    x, y, c = lax.axis_index("x"), lax.axis_index("y"), lax.axis_index("c")
    me = 4 * x + 2 * y + c
    per_array = send_sems.shape[0] > N_DEV - 1
    local, remote = [], []
    for a, (src, land) in enumerate(zip(src_refs, land_refs)):
        local.append(pltpu.make_async_copy(src if gather else src.at[me], land.at[me],
                                           local_sems.at[a if per_array else 0]))
    for k in range(1, N_DEV):
        px = x ^ ((k >> 2) & 1)
        py = y ^ ((k >> 1) & 1)
        pc = c ^ (k & 1)
        for a, (src, land) in enumerate(zip(src_refs, land_refs)):
            remote.append(pltpu.make_async_remote_copy(
                src_ref=src if gather else src.at[4 * px + 2 * py + pc], dst_ref=land.at[me],
                send_sem=send_sems.at[(7 * a if per_array else 0) + k - 1],
                recv_sem=recv_sems.at[(7 * a if per_array else 0) + k - 1],
                device_id=(px, py, pc), device_id_type=pl.DeviceIdType.MESH))
    return local, remote


def _exchange_start(srcs, gather, name, dep=None):
    na = len(srcs)
    ns = na if na <= 4 else 1
    lands = [lax.empty(((N_DEV,) + t.shape) if gather else t.shape, t.dtype) for t in srcs]

    def body(*refs):
        src_refs, land_refs = refs[:na], refs[na:2 * na]
        nin = 2 * na + (0 if dep is None else 1)
        send_sems, recv_sems, local_sems = refs[nin:nin + 3]
        token = refs[-1]
        local, remote = _exchange_copies(src_refs, land_refs, send_sems, recv_sems, local_sems, gather)
        for cp in local + remote:
            cp.start()
        token[...] = jnp.zeros_like(token)

    dep_specs, dep_ops = _dep_args(dep)
    hbm = lambda t: pltpu.HBM(t.shape, t.dtype)
    out = pl.pallas_call(
        body, name=name,
        out_shape=(pltpu.SemaphoreType.DMA((7 * ns,)), pltpu.SemaphoreType.DMA((7 * ns,)),
                   pltpu.SemaphoreType.DMA((ns,)), *[hbm(t) for t in srcs], *[hbm(t) for t in lands],
                   jax.ShapeDtypeStruct((8, LANE), F32)),
        in_specs=[_HBM] * (2 * na) + dep_specs,
        out_specs=(_SEM, _SEM, _SEM, *[_HBM] * (2 * na), pl.BlockSpec(memory_space=pltpu.VMEM)),
        input_output_aliases={i: 3 + i for i in range(2 * na)},
        compiler_params=pltpu.CompilerParams(has_side_effects=_EFFECT),
    )(*[pltpu.with_memory_space_constraint(t, pltpu.HBM) for t in srcs + lands], *dep_ops)
    return (out[:3], out[3:3 + na], out[3 + na:3 + 2 * na]), out[-1]


def _exchange_wait(handle, gather, after, name):
    sems, srcs, lands = handle
    na = len(srcs)

    def body(*refs):
        src_refs, land_refs = refs[:na], refs[na:2 * na]
        send_sems, recv_sems, local_sems = refs[2 * na:2 * na + 3]
        local, remote = _exchange_copies(src_refs, land_refs, send_sems, recv_sems, local_sems, gather)
        for cp in remote:
            cp.wait_send()
            cp.wait_recv()
        for cp in local:
            cp.wait()

    hbm = lambda t: pltpu.HBM(t.shape, t.dtype)
    out = pl.pallas_call(
        body, name=name, out_shape=(*[hbm(t) for t in srcs], *[hbm(t) for t in lands]),
        in_specs=[_HBM] * (2 * na) + [_SEM] * 3 + [pl.BlockSpec(memory_space=pl.ANY)],
        out_specs=tuple([_HBM] * (2 * na)), input_output_aliases={i: i for i in range(2 * na)},
        compiler_params=pltpu.CompilerParams(has_side_effects=_EFFECT),
    )(*srcs, *lands, *sems, after)
    return out[na:]


def _block_diag(w):
    h, a, b = w.shape
    eye = jnp.eye(h, dtype=w.dtype)
    return (w[:, :, None, :] * eye[:, None, :, None]).reshape(h * a, h * b)


def _block_diag_extract(m, h):
    a, b = m.shape[0] // h, m.shape[1] // h
    return jnp.stack([m[i * a:(i + 1) * a, i * b:(i + 1) * b] for i in range(h)], axis=0)


def _block_diag_take(m, h):
    a, b = m.shape[0] // h, m.shape[1] // h
    eye = jnp.eye(h, dtype=m.dtype)
    return (m.reshape(h, a, h, b) * eye[:, None, :, None]).sum(axis=2)


def _ffn_interleave(w):
    lead = w.shape[:-1]
    nb = D_FF // FFN_CB
    return jnp.swapaxes(w.reshape(*lead, 2, nb, FFN_CB), -3, -2).reshape(*lead, 2 * D_FF)


def _ffn_deinterleave(w):
    lead = w.shape[:-1]
    nb = D_FF // FFN_CB
    return jnp.swapaxes(w.reshape(*lead, nb, 2, FFN_CB), -3, -2).reshape(*lead, 2 * D_FF)


def _gather_full(gathered, axis):
    shape = list(gathered.shape[1:])
    shape[axis] *= N_DEV
    return jnp.moveaxis(gathered, 0, axis).reshape(shape)


def _scatter_blocks(full, axis):
    shape = list(full.shape)
    shape[axis:axis + 1] = [N_DEV, shape[axis] // N_DEV]
    return jnp.moveaxis(full.reshape(shape), axis, 0)


def _pad_to(flat, mult):
    pad = (-flat.shape[-1]) % mult
    if pad:
        flat = jnp.concatenate([flat, jnp.zeros(flat.shape[:-1] + (pad,), flat.dtype)], axis=-1)
    return flat


def _layer_fwd(h_in, h_in_t, w, cos, sin, l, dep, get_ffn, target=None):
    tag = "l%d_" % l
    proj, qkv, h_t = _proj_rope(h_in, w['w_in'], cos, sin, tag + "proj_rope", dep=dep,
                                transposed=h_in_t is None)
    h_in_t = h_t if h_in_t is None else h_in_t
    outs, lses = [], []
    for d, qv in zip(DILATIONS, qkv):
        o, ls = _attn_fwd(qv, d, tag + "attn_d%d" % d)
        outs.append(o)
        lses.append(ls)
    lru, *lru_saved = _lru_fwd(proj, w['lru_conv_w'], w['lru_conv_b'], w['lru_wr'], w['lru_br'], w['lru_wi'],
                               w['lru_bi'], w['lru_lambda'], tag + "lru")
    x_re, x_im, y_acc = _s5_scan_fwd(proj, w['s5_bb_re'], w['s5_bb_im'], w['s5_lam_re'], w['s5_lam_im'],
                                     w['s5_cc_re'], w['s5_cc_im'], tag + "s5_scan")
    s5, y_pre = _s5_out_fwd(proj, y_acc, w['s5_d'], w['s5_w_glu'], w['s5_b_glu'], tag + "s5_out")
    w_out = get_ffn(l, s5, 'out')
    if w_out is not None:
        w['w_out'] = w_out
    mixed_t, r1, h1, h1_t, attn_o, attn_lse = _mix_fwd(outs, lses, lru, s5, w['mix_norm_g'], h_in, w['w_out'],
                                                       w['ln1_g'], w['ln1_b'], tag + "mix_out_ln1")
    w['w_up_g'], w['w_down'], ffn_dep = get_ffn(l, h1, 'ffn')
    up, y_conv, act, act_t = _ffn_up_act(h1, w['w_up_g'], w['ffn_conv_w'], w['ffn_conv_b'], tag + "up_act",
                                         dep=ffn_dep)
    r2, out_a, out_b = _proj_ln(act, w['w_down'], h1, w['ln2_g'], w['ln2_b'], tag + "down_ln2", target=target)
    saved = dict(h_in_t=h_in_t, proj=proj, qkv=qkv, lru=lru, lru_saved=lru_saved, x_re=x_re, x_im=x_im,
                 y_pre=y_pre, s5=s5, mixed_t=mixed_t, attn_o=attn_o, attn_lse=attn_lse, r1=r1, h1_t=h1_t, up=up,
                 act_t=act_t, r2=r2, y_conv=y_conv)
    return out_a, out_b, saved


def _layer_bwd_ffn(dh2, sv, w, l, dep=None):
    tag = "l%d_" % l
    g = {}
    dr2, g['ln2_g'], g['ln2_b'] = _ln_bwd(sv['r2'], dh2, w['ln2_g'], tag + "ln2_bwd", dep=dep)
    g['w_down'] = _mm_dw(sv['act_t'], dr2, 1024, D_MODEL, 1024, tag + "down_dw", _grad_dtype(l))
    dup, dh1, dcw_parts, dcb_parts = _ffn_bwd(sv['up'], sv['y_conv'], dr2, w['w_down'], w['w_up_g'],
                                              w['ffn_conv_w'], tag + "ffn_bwd")
    g['ffn_conv_w'] = dcw_parts.sum(axis=0)
    g['ffn_conv_b'] = dcb_parts.sum(axis=0)
    g['w_up_g'] = _mm_up_dw(sv['h1_t'], dup, tag + "up_dw", _grad_dtype(l))
    return dh1, g


def _layer_bwd_mix(dh1, sv, w, cos, sin, l, dep, g_ffn, after_out_grad, after_small_grads, after_in_grad):
    tag = "l%d_" % l
    g = {}
    dr1, g['ln1_g'], g['ln1_b'], d_o, dlru, ds5, g['mix_norm_g'] = _mix_bwd(
        sv['r1'], dh1, w['ln1_g'], w['w_out'], sv['attn_o'][0], sv['lru'], sv['s5'], w['mix_norm_g'],
        tag + "ln1_mix_bwd", dep=dep)
    g['w_out'] = _mm_dw(sv['mixed_t'], dr1, 1024, D_MODEL, 1024, tag + "out_dw", _grad_dtype(l))
    dy, dud, g['s5_d'], g['s5_w_glu'], g['s5_b_glu'] = _s5_out_bwd(
        sv['proj'], sv['y_pre'], ds5, w['s5_d'], w['s5_w_glu'], w['s5_b_glu'], tag + "s5_out_bwd",
        dep=after_out_grad(l, g['w_out']))
    du, g['s5_lam_re'], g['s5_lam_im'], g['s5_bb_re'], g['s5_bb_im'], g['s5_cc_re'], g['s5_cc_im'] = \
        _s5_scan_bwd(sv['proj'], dy, dud, sv['x_re'], sv['x_im'], w['s5_bb_re'], w['s5_bb_im'],
                     w['s5_lam_re'], w['s5_lam_im'], w['s5_cc_re'], w['s5_cc_im'], tag + "s5_scan_bwd")
    (dxr, dgate, g['lru_conv_w'], g['lru_conv_b'], g['lru_wr'], g['lru_br'], g['lru_wi'], g['lru_bi'],
     g['lru_lambda']) = _lru_bwd(sv['proj'], dlru, *sv['lru_saved'], w['lru_conv_w'], w['lru_conv_b'], w['lru_wr'],
                                 w['lru_br'], w['lru_wi'], w['lru_bi'], w['lru_lambda'], tag + "lru_bwd")
    token = after_small_grads(l, _finish_layer_grads({**g_ffn, **g}, w, l))
    dqkv = [_attn_bwd(sv['qkv'][b], sv['attn_o'][b], d_o[b], sv['attn_lse'][b], d, tag + "attn_bwd_d%d" % d,
                      dep=token if b == 0 else None)
            for b, d in enumerate(DILATIONS)]
    dproj = _dproj_assemble(dqkv, dxr, dgate, du, cos, sin, tag + "dproj")
    g_in = _mm_dw(sv['h_in_t'], dproj, 1024, D_IN, 1024, tag + "in_dw", _grad_dtype(l))
    return _mm_nt(dproj, w['w_in'], 512, D_MODEL, tag + "in_dx", add=dr1, add_scale=ALPHA,
                  dep=after_in_grad(l, g_in))


def _s5_rep(a):
    return jnp.repeat(a, S5_C, axis=0)


def _prepare_layer(p, l):
    w = {}
    for n in ('w_in', 'w_out', 's5_w_glu'):
        if n in p:
            w[n] = p[n].astype(BF16)
    w['ffn_conv_w'] = _ffn_interleave(p['ffn_conv_w'])
    w['ffn_conv_b'] = _ffn_interleave(p['ffn_conv_b'])[None, :]
    w['lru_conv_w'] = p['lru_conv_w']
    for n in ('lru_conv_b', 'lru_br', 'lru_bi', 'lru_lambda', 's5_b_glu', 'mix_norm_g',
              'ln1_g', 'ln1_b', 'ln2_g', 'ln2_b'):
        w[n] = p[n][None, :]
    w['lru_wr'] = _block_diag(p['lru_wr']).astype(BF16)
    w['lru_wi'] = _block_diag(p['lru_wi']).astype(BF16)
    w['s5_d'] = p['s5_d'].reshape(1, S5_W)
    disc_in = (_s5_rep(p['s5_a_re']), _s5_rep(p['s5_a_im']),
               _s5_rep(jnp.broadcast_to(p['s5_log_step'][:, None], (S5_G, S5_P))),
               jnp.swapaxes(p['s5_b_re'], 1, 2).reshape(S5_W, S5_P),
               jnp.swapaxes(p['s5_b_im'], 1, 2).reshape(S5_W, S5_P))
    ab_re, ab_im, bb_re, bb_im = _s5_disc_fwd(*disc_in, "l%d_s5_disc" % l)
    w['s5_disc_in'] = disc_in
    w['s5_lam_re'] = ab_re.reshape(S5_G, S5_C, S5_P)[:, 0, :].reshape(1, S5_STATES)
    w['s5_lam_im'] = ab_im.reshape(S5_G, S5_C, S5_P)[:, 0, :].reshape(1, S5_STATES)
    w['s5_bb_re'] = _block_diag(bb_re.reshape(S5_G, S5_C, S5_P)).astype(BF16)
    w['s5_bb_im'] = _block_diag(bb_im.reshape(S5_G, S5_C, S5_P)).astype(BF16)
    w['s5_cc_re'] = _block_diag(jnp.swapaxes(p['s5_c_re'], 1, 2)).astype(BF16)
    w['s5_cc_im'] = _block_diag(jnp.swapaxes(p['s5_c_im'], 1, 2)).astype(BF16)
    return w


def _finish_layer_grads(g, w, l):
    out = {}
    for n in ('s5_w_glu', 'lru_conv_w'):
        out[n] = g[n]
    out['ffn_conv_w'] = _ffn_deinterleave(g['ffn_conv_w'])
    out['ffn_conv_b'] = _ffn_deinterleave(g['ffn_conv_b'])[0]
    for n in ('lru_conv_b', 'lru_br', 'lru_bi', 'lru_lambda', 's5_b_glu', 'mix_norm_g',
              'ln1_g', 'ln1_b', 'ln2_g', 'ln2_b'):
        out[n] = g[n][0]
    out['lru_wr'] = _block_diag_extract(g['lru_wr'], LRU_W // HEAD)
    out['lru_wi'] = _block_diag_extract(g['lru_wi'], LRU_W // HEAD)
    out['s5_d'] = g['s5_d'].reshape(S5_G, S5_C)
    out['s5_c_re'] = jnp.swapaxes(_block_diag_take(g['s5_cc_re'], S5_G), 1, 2)
    out['s5_c_im'] = jnp.swapaxes(_block_diag_take(g['s5_cc_im'], S5_G), 1, 2)
    rep = lambda v: _s5_rep(v.reshape(S5_G, S5_P)) * (1.0 / S5_C)
    cts = (rep(g['s5_lam_re']), rep(g['s5_lam_im']),
           _block_diag_take(g['s5_bb_re'], S5_G).reshape(S5_W, S5_P),
           _block_diag_take(g['s5_bb_im'], S5_G).reshape(S5_W, S5_P))
    da_re, da_im, dls, dbt_re, dbt_im = _s5_disc_bwd(*w['s5_disc_in'], cts, "l%d_s5_disc_bwd" % l)
    out['s5_a_re'] = da_re.reshape(S5_G, S5_C, S5_P).sum(axis=1)
    out['s5_a_im'] = da_im.reshape(S5_G, S5_C, S5_P).sum(axis=1)
    out['s5_log_step'] = dls.reshape(S5_G, S5_C * S5_P).sum(axis=1)
    out['s5_b_re'] = jnp.swapaxes(dbt_re.reshape(S5_G, S5_C, S5_P), 1, 2)
    out['s5_b_im'] = jnp.swapaxes(dbt_im.reshape(S5_G, S5_C, S5_P), 1, 2)
    return out


def _run_step(x, target, get_layer, get_ffn, on_loss, after_ffn_grads, after_out_grad, after_small_grads,
              after_in_grad):
    cos, sin = _rope_tables(x.shape[0])
    h, h_t = x, None
    ws, saved = [], []
    for l in range(DEPTH):
        p, dep = get_layer(l, h)
        ws.append(_prepare_layer(p, l))
        h, h_t, sv = _layer_fwd(h, h_t, ws[l], cos, sin, l, dep, get_ffn, target if l == DEPTH - 1 else None)
        saved.append(sv)
    dh, loss_vec = h, h_t
    on_loss(loss_vec)
    dep = None
    for l in reversed(range(DEPTH)):
        dh1, g = _layer_bwd_ffn(dh, saved[l], ws[l], l, dep)
        dep = after_ffn_grads(l, g)
        dh = _layer_bwd_mix(dh1, saved[l], ws[l], cos, sin, l, dep, g, after_out_grad, after_small_grads,
                            after_in_grad)
        dep = None
    return loss_vec, dh


def _local_step(x, target, layers):
    grads = [{} for _ in range(DEPTH)]

    def ffn(l, after, part):
        if part == 'out':
            return None
        return layers[l]['w_up_g'].astype(BF16), layers[l]['w_down'].astype(BF16), None

    def keep_ffn(l, g):
        grads[l].update(w_up_g=g['w_up_g'], w_down=g['w_down'])

    def keep_small(l, g):
        grads[l].update(g)

    loss, dx = _run_step(x, target, lambda l, h: (layers[l], None), ffn, lambda row: None, keep_ffn,
                         lambda l, g: grads[l].update(w_out=g), keep_small, lambda l, g: grads[l].update(w_in=g))
    return loss[0, 0], dx, grads


def kernel(x, w_in, lru_conv_w, lru_conv_b, lru_wr, lru_br, lru_wi, lru_bi, lru_lambda, s5_a_re, s5_a_im, s5_b_re, s5_b_im, s5_c_re, s5_c_im, s5_d, s5_log_step, s5_w_glu, s5_b_glu, mix_norm_g, w_out, ln1_g, ln1_b, w_up, ffn_conv_w, ffn_conv_b, w_down, ln2_g, ln2_b, loss_target, m_w_in, m_lru_conv_w, m_lru_conv_b, m_lru_wr, m_lru_br, m_lru_wi, m_lru_bi, m_lru_lambda, m_s5_a_re, m_s5_a_im, m_s5_b_re, m_s5_b_im, m_s5_c_re, m_s5_c_im, m_s5_d, m_s5_log_step, m_s5_w_glu, m_s5_b_glu, m_mix_norm_g, m_w_out, m_ln1_g, m_ln1_b, m_w_up, m_ffn_conv_w, m_ffn_conv_b, m_w_down, m_ln2_g, m_ln2_b, v_w_in, v_lru_conv_w, v_lru_conv_b, v_lru_wr, v_lru_br, v_lru_wi, v_lru_bi, v_lru_lambda, v_s5_a_re, v_s5_a_im, v_s5_b_re, v_s5_b_im, v_s5_c_re, v_s5_c_im, v_s5_d, v_s5_log_step, v_s5_w_glu, v_s5_b_glu, v_mix_norm_g, v_w_out, v_ln1_g, v_ln1_b, v_w_up, v_ffn_conv_w, v_ffn_conv_b, v_w_down, v_ln2_g, v_ln2_b):
    args = locals()
    wl = {n: args[n] for n in WEIGHTS}
    ml = {n: args['m_' + n] for n in WEIGHTS}
    vl = {n: args['v_' + n] for n in WEIGHTS}

    small_sizes = [int(wl[n].size) for n in SMALL_SHARDED]
    small_flat = _pad_to(jnp.concatenate([wl[n].reshape(-1) for n in SMALL_SHARDED]), 8 * 1024)
    small_all, w_in0 = _all_gather([small_flat.reshape(-1, 1024), wl['w_in'][0].astype(BF16)], "gather_first")
    small_all = small_all.reshape(N_DEV, -1)
    small_full, off = {}, 0
    for n, sz in zip(SMALL_SHARDED, small_sizes):
        small_full[n] = _gather_full(small_all[:, off:off + sz].reshape((N_DEV,) + wl[n].shape), SHARD_AXIS[n])
        off += sz
    def mixer_params(l, g_in, g_out):
        p = {n: wl[n][l] for n in REPLICATED}
        p.update({n: small_full[n][l] for n in SMALL_SHARDED})
        p['w_in'] = _gather_full(g_in, 1)
        if g_out is not None:
            p['w_out'] = g_out.reshape(D_MODEL, D_MODEL)
        return p

    mix_names, ffn_names = ('w_in', 'w_out'), ('w_up', 'w_down')
    shards = lambda names, l: [wl[n][l].astype(BF16) for n in names]
    gathers = {}
    gathers[0, 'out'], token = _exchange_start(shards(('w_out',), 0), True, "gather_out_l0_start", dep=w_in0)
    gathers[0, 'ffn'], rest0_token = _exchange_start(shards(ffn_names, 0), True, "gather_ffn_l0_start", dep=token)

    def get_layer(l, h):
        if l == 0:
            return mixer_params(0, w_in0, None), rest0_token
        return mixer_params(1, *_exchange_wait(gathers[1, 'mix'], True, h, "gather_mix_l1_wait")), None

    def get_ffn(l, after, part):
        if part == 'out':
            if l > 0:
                return None
            g_out, = _exchange_wait(gathers[0, 'out'], True, after, "gather_out_l0_wait")
            return g_out.reshape(D_MODEL, D_MODEL)
        g_up, g_down = _exchange_wait(gathers[l, 'ffn'], True, after, "gather_ffn_l%d_wait" % l)
        token = None
        if l == 0:
            gathers[1, 'mix'], token = _exchange_start(shards(mix_names, 1), True, "gather_mix_l1_start", dep=g_up)
            gathers[1, 'ffn'], token = _exchange_start(shards(ffn_names, 1), True, "gather_ffn_l1_start", dep=token)
        return g_up, g_down.reshape(D_FF, D_MODEL), token

    scatters = {}

    def after_ffn_grads(l, g):
        send = [g['w_up_g'], g['w_down'].reshape(N_DEV, D_FF // N_DEV, D_MODEL)]
        scatters[l, 'ffn'], token = _exchange_start(send, False, "scatter_ffn_l%d_start" % l)
        return token

    def after_out_grad(l, g_out):
        send = [g_out.reshape(N_DEV, D_MODEL // N_DEV, D_MODEL)]
        scatters[l, 'out'], token = _exchange_start(send, False, "scatter_out_l%d_start" % l)
        return token

    def after_in_grad(l, g_in):
        scatters[l, 'in'], token = _exchange_start([_scatter_blocks(g_in, 1)], False, "scatter_in_l%d_start" % l)
        return token

    def after_small_grads(l, g):
        rep = [g[n][None] for n in REPLICATED]
        if l == DEPTH - 1:
            rep.append(loss_rows[0][None])
        shd = [_scatter_blocks(g[n], SHARD_AXIS[n] - 1)[:, None] for n in SMALL_SHARDED]
        scatters[l, 'rep'], token = _exchange_start(rep, True, "gather_rep_grads_l%d_start" % l)
        scatters[l, 'small'], token = _exchange_start(shd, False, "scatter_small_l%d_start" % l, dep=token)
        return token

    loss_rows = []
    _, grad_x = _run_step(x[0], loss_target[0], get_layer, get_ffn, loss_rows.append, after_ffn_grads,
                          after_out_grad, after_small_grads, after_in_grad)

    results = {}
    big_prev = {n: None for n in BIG}

    def finish_big(l, part, names, after):
        landed = _exchange_wait(scatters[l, part], False, after, "scatter_%s_l%d_wait" % (part, l))
        for n, ld in zip(names, landed):
            big_prev[n] = _adamw_sum(ld, wl[n], ml[n], vl[n], l, big_prev[n], "adamw_%s_l%d" % (n, l))

    for l, part, names in ((1, 'ffn', ffn_names), (1, 'out', ('w_out',)), (1, 'in', ('w_in',)),
                           (0, 'ffn', ffn_names), (0, 'out', ('w_out',))):
        finish_big(l, part, names, grad_x)

    kinds = ('grad', 'delta', 'm', 'v')
    landed = []
    for l in range(DEPTH):
        rep = list(_exchange_wait(scatters[l, 'rep'], True, grad_x, "gather_rep_grads_l%d_wait" % l))
        if l == DEPTH - 1:
            loss = jnp.sum(rep.pop()[:, 0, 0, 0])
        shd = list(_exchange_wait(scatters[l, 'small'], False, grad_x, "scatter_small_l%d_wait" % l))
        landed.append(dict(zip(REPLICATED + SMALL_SHARDED, rep + shd)))
    matrices = ['lru_wr', 'lru_wi', 's5_a_re', 's5_a_im', 's5_c_re', 's5_c_im', 's5_d']
    widest = ['s5_b_re', 's5_b_im']
    vectors = [n for n in REPLICATED + SMALL_SHARDED if n not in matrices + widest]
    last = None
    for tag, names in (("vectors", vectors), ("matrices", matrices), ("s5_b", widest)):
        res = _adamw_many([[landed[l][n] for n in names] for l in range(DEPTH)], [wl[n] for n in names],
                          [ml[n] for n in names], [vl[n] for n in names], "adamw_" + tag)
        for kind, arrs in zip(kinds, res):
            for n, a in zip(names, arrs):
                results[kind, n] = a
        last = res[0][0]
    finish_big(0, 'in', ('w_in',), last)
    for n in BIG:
        results['grad', n], results['delta', n], results['m', n], results['v', n] = big_prev[n]

    out = [loss, grad_x[None]]
    for kind in kinds:
        out.extend(results[kind, n] for n in WEIGHTS)
    return tuple(out)
```
